```python
import jax, jax.numpy as jnp
from jax import lax
import numpy as np

D_MODEL = 1024
BATCH = 8
SEQ = 4096
DEPTH = 1

N_META = 16
EPS = 1e-6
D_FF = 2816
CHUNK = 64
A_DK = 128
A_DV = 128
A_HEADS = D_MODEL // A_DV
A_CONV = 4
A_WK = A_HEADS * A_DK
A_WV = A_HEADS * A_DV
B_N = 64
B_HEADS = D_MODEL // B_N
B_W = B_HEADS * B_N
W_LORA = 64
AA_LORA = 64
G_LORA = 160
B_GN_EPS = B_N * 1e-5
B_COLS = 3 * B_W + W_LORA + AA_LORA + G_LORA
IN_SIZES = (A_WK, A_WK, A_WV, A_WV, A_HEADS, A_HEADS, B_COLS, D_MODEL, D_MODEL)
IN_TOTAL = sum(IN_SIZES)

kernel_name = "meta_macaron_deltanet_rwkv7_hybrid"


def _offsets(sizes):
    out, acc = [], 0
    for s in sizes[:-1]:
        acc += s
        out.append(acc)
    return out


def _rmsnorm(x, gain):
    xf = x.astype(jnp.float32)
    y = xf * lax.rsqrt(jnp.mean(xf * xf, axis=-1, keepdims=True) + EPS)
    return (y * gain.astype(jnp.float32)).astype(x.dtype)


def _l2norm(x):
    xf = x.astype(jnp.float32)
    return xf * lax.rsqrt(jnp.sum(xf * xf, axis=-1, keepdims=True) + 1e-6)


def _swiglu(x, w_gu, w_down):
    gate, up = jnp.split(x @ w_gu, 2, axis=-1)
    return (jax.nn.silu(gate) * up) @ w_down


def _causal_dwconv(x, w):
    k = w.shape[0]
    return lax.conv_general_dilated(x, w[:, None, :].astype(x.dtype), window_strides=(1,),
                                    padding=[(k - 1, 0)], dimension_numbers=('NWC', 'WIO', 'NWC'),
                                    feature_group_count=x.shape[-1])


def _gated_delta_chunked(q, k, v, beta, g):
    b, h, t, dk = q.shape
    dv = v.shape[-1]
    n = t // CHUNK

    def ch(z):
        return z.reshape((b, h, n, CHUNK) + z.shape[3:])

    q, k, v, beta, g = ch(q), ch(k), ch(v), ch(beta), ch(g)
    g = jnp.cumsum(g, axis=-1)
    kb = k * beta[..., None]
    vb = v * beta[..., None]
    idx = jnp.arange(CHUNK)
    incl = idx[:, None] >= idx[None, :]
    strict = idx[:, None] > idx[None, :]
    diff = g[..., :, None] - g[..., None, :]
    decay = jnp.where(incl, jnp.exp(jnp.where(incl, diff, 0.0)), 0.0)
    m = jnp.where(strict, jnp.einsum('bhncd,bhnsd->bhncs', kb, k) * decay, 0.0)
    eye = jnp.eye(CHUNK, dtype=m.dtype)
    tinv = lax.linalg.triangular_solve(m + eye, jnp.broadcast_to(eye, m.shape), left_side=True,
                                       lower=True, unit_diagonal=True)
    u = jnp.einsum('bhncs,bhnsd->bhncd', tinv, vb)
    wk = jnp.einsum('bhncs,bhnsd->bhncd', tinv, kb * jnp.exp(g)[..., None])
    attn = jnp.einsum('bhncd,bhnsd->bhncs', q, k) * decay
    qg = q * jnp.exp(g)[..., None]
    g_last = g[..., -1]
    k_tail = k * jnp.exp(g_last[..., None] - g)[..., None]

    def step(state, inp):
        u_i, w_i, attn_i, qg_i, kt_i, gl_i = inp
        v_new = u_i - jnp.einsum('bhcd,bhde->bhce', w_i, state)
        o = jnp.einsum('bhcd,bhde->bhce', qg_i, state) + jnp.einsum('bhcs,bhse->bhce', attn_i, v_new)
        state = state * jnp.exp(gl_i)[..., None, None] + jnp.einsum('bhcd,bhce->bhde', kt_i, v_new)
        return state, o

    xs = (jnp.moveaxis(u, 2, 0), jnp.moveaxis(wk, 2, 0), jnp.moveaxis(attn, 2, 0),
          jnp.moveaxis(qg, 2, 0), jnp.moveaxis(k_tail, 2, 0), jnp.moveaxis(g_last, 2, 0))
    s0 = jnp.zeros((b, h, dk, dv), jnp.float32)
    _, o = lax.scan(step, s0, xs)
    return jnp.moveaxis(o, 0, 2).reshape(b, h, t, dv)


def _deltanet_branch(q, k, v, z, beta_pre, alpha_pre, conv_w, log_rate, dt_bias, out_gain):
    bsz, t, _ = q.shape
    qkv = jax.nn.silu(_causal_dwconv(jnp.concatenate([q, k, v], axis=-1), conv_w))
    q, k, v = jnp.split(qkv, [A_WK, 2 * A_WK], axis=-1)
    q = _l2norm(q.reshape(bsz, t, A_HEADS, A_DK)) * (A_DK ** -0.5)
    k = _l2norm(k.reshape(bsz, t, A_HEADS, A_DK))
    v = v.reshape(bsz, t, A_HEADS, A_DV).astype(jnp.float32)
    beta = jax.nn.sigmoid(beta_pre.astype(jnp.float32))
    g = -jnp.exp(log_rate.astype(jnp.float32)) * jax.nn.softplus(
        alpha_pre.astype(jnp.float32) + dt_bias.astype(jnp.float32))
    pad = CHUNK - N_META

    def prep(a):
        a = jnp.pad(a, ((0, 0), (pad, 0)) + ((0, 0),) * (a.ndim - 2))
        return jnp.moveaxis(a, 2, 1)

    o = _gated_delta_chunked(prep(q), prep(k), prep(v), prep(beta), prep(g))
    o = jnp.moveaxis(o, 1, 2)[:, pad:]
    o = o * lax.rsqrt(jnp.mean(o * o, axis=-1, keepdims=True) + EPS) * out_gain.astype(jnp.float32)
    o = o.reshape(bsz, t, A_WV) * jax.nn.silu(z.astype(jnp.float32))
    return o.astype(z.dtype)


def _rwkv7_scan(r, w, k, v, a_vec, b_vec):
    def step(state, inp):
        r_t, w_t, k_t, v_t, a_t, b_t = inp
        sa = jnp.einsum('bhvk,bhk->bhv', state, a_t)
        state = state * w_t[:, :, None, :] + sa[..., None] * b_t[:, :, None, :] \
            + v_t[..., None] * k_t[:, :, None, :]
        return state, jnp.einsum('bhvk,bhk->bhv', state, r_t)

    bsz, t, h, n = r.shape
    xs = tuple(jnp.moveaxis(a, 1, 0) for a in (r, w, k, v, a_vec, b_vec))
    s0 = jnp.zeros((bsz, h, n, n), jnp.float32)
    _, y = lax.scan(step, s0, xs)
    return jnp.moveaxis(y, 0, 1)


def _rwkv7_branch(zb, mu, w0, w_up, a0, a_up, g_up, k_k, k_a, r_k, ln_gain, ln_bias):
    bsz, t, _ = zb.shape
    f32 = jnp.float32
    zf = zb.astype(f32)
    prev = jnp.pad(zf, ((0, 0), (1, 0), (0, 0)))[:, :-1]
    zf = zf + (prev - zf) * mu.astype(f32)
    r, k, v, wd, ad, gd = jnp.split(
        zf, [B_W, 2 * B_W, 3 * B_W, 3 * B_W + W_LORA, 3 * B_W + W_LORA + AA_LORA], axis=-1)
    w_log = -jax.nn.softplus(-(w0.astype(f32) + jnp.tanh(wd) @ w_up.astype(f32))) - 0.5
    decay = jnp.exp(-jnp.exp(w_log))
    a = jax.nn.sigmoid(a0.astype(f32) + ad @ a_up.astype(f32))
    gate = jax.nn.sigmoid(gd) @ g_up.astype(f32)
    hs = (bsz, t, B_HEADS, B_N)
    kk = _l2norm((k * k_k.astype(f32)).reshape(hs))
    k = k * (1.0 + (a - 1.0) * k_a.astype(f32))
    r, k, v, decay, a = r.reshape(hs), k.reshape(hs), v.reshape(hs), decay.reshape(hs), a.reshape(hs)
    y = _rwkv7_scan(r, decay, k, v, -kk, kk * a)
    mean = jnp.mean(y, axis=-1, keepdims=True)
    var = jnp.mean(jnp.square(y - mean), axis=-1, keepdims=True)
    y = (y - mean) * lax.rsqrt(var + B_GN_EPS) * ln_gain.astype(f32).reshape(B_HEADS, B_N) \
        + ln_bias.astype(f32).reshape(B_HEADS, B_N)
    y = y + jnp.sum(r * k * r_k.astype(f32), axis=-1, keepdims=True) * v
    return (y.reshape(bsz, t, B_W) * gate).astype(zb.dtype)


def _fwd_setup_inputs(seed: int = 0) -> dict:
    key = jax.random.key(seed)
    ks = jax.random.split(key, 32)
    f32 = jnp.float32

    def nrm(k, shape, scale):
        return jax.random.normal(k, shape, f32) * scale

    def gain(k, shape):
        return 1.0 + 0.02 * jax.random.normal(k, shape, f32)

    dt = jnp.exp(jax.random.uniform(ks[8], (DEPTH, A_HEADS), f32, np.log(1e-3), np.log(1e-1)))
    return {
        "x": nrm(ks[0], (BATCH, SEQ, D_MODEL), 1.0),
        "meta_tokens": nrm(ks[1], (N_META, D_MODEL), 1.0),
        "ffn1_norm": gain(ks[2], (DEPTH, D_MODEL)),
        "ffn1_w_gu": nrm(ks[3], (DEPTH, D_MODEL, 2 * D_FF), D_MODEL ** -0.5),
        "ffn1_w_down": nrm(ks[4], (DEPTH, D_FF, D_MODEL), D_FF ** -0.5),
        "mix_norm": gain(ks[5], (DEPTH, D_MODEL)),
        "w_in": nrm(ks[6], (DEPTH, D_MODEL, IN_TOTAL), D_MODEL ** -0.5),
        "a_conv_w": nrm(ks[7], (DEPTH, A_CONV, 2 * A_WK + A_WV), A_CONV ** -0.5),
        "a_log_rate": jnp.log(jax.random.uniform(ks[9], (DEPTH, A_HEADS), f32, 1.0, 16.0)),
        "a_dt_bias": dt + jnp.log(-jnp.expm1(-dt)),
        "a_out_norm": gain(ks[10], (DEPTH, A_DV)),
        "b_shift_mu": jax.random.uniform(ks[11], (DEPTH, B_COLS), f32, 0.0, 1.0),
        "b_w0": jax.random.uniform(ks[12], (DEPTH, B_W), f32, -6.5, -1.5),
        "b_w_up": nrm(ks[13], (DEPTH, W_LORA, B_W), 0.5 * W_LORA ** -0.5),
        "b_a0": nrm(ks[14], (DEPTH, B_W), 0.1),
        "b_a_up": nrm(ks[15], (DEPTH, AA_LORA, B_W), AA_LORA ** -0.5),
        "b_g_up": nrm(ks[16], (DEPTH, G_LORA, B_W), G_LORA ** -0.5),
        "b_k_k": 0.85 + 0.05 * jax.random.normal(ks[17], (DEPTH, B_W), f32),
        "b_k_a": 1.0 + 0.05 * jax.random.normal(ks[18], (DEPTH, B_W), f32),
        "b_r_k": nrm(ks[19], (DEPTH, B_HEADS, B_N), 0.1),
        "b_ln_gain": gain(ks[20], (DEPTH, B_W)),
        "b_ln_bias": nrm(ks[21], (DEPTH, B_W), 0.02),
        "w_out": nrm(ks[22], (DEPTH, D_MODEL, D_MODEL), D_MODEL ** -0.5),
        "ffn2_norm": gain(ks[23], (DEPTH, D_MODEL)),
        "ffn2_w_gu": nrm(ks[24], (DEPTH, D_MODEL, 2 * D_FF), D_MODEL ** -0.5),
        "ffn2_w_down": nrm(ks[25], (DEPTH, D_FF, D_MODEL), D_FF ** -0.5),
        "final_norm": gain(ks[26], (D_MODEL,)),
    }


def _fwd_reference(x, meta_tokens, ffn1_norm, ffn1_w_gu, ffn1_w_down, mix_norm, w_in, a_conv_w,
              a_log_rate, a_dt_bias, a_out_norm, b_shift_mu, b_w0, b_w_up, b_a0, b_a_up, b_g_up,
              b_k_k, b_k_a, b_r_k, b_ln_gain, b_ln_bias, w_out, ffn2_norm, ffn2_w_gu, ffn2_w_down,
              final_norm):
    bsz = x.shape[0]
    meta = jnp.broadcast_to(meta_tokens[None].astype(x.dtype), (bsz, N_META, D_MODEL))
    h = jnp.concatenate([meta, x], axis=1)
    for l in range(DEPTH):
        h = h + 0.5 * _swiglu(_rmsnorm(h, ffn1_norm[l]), ffn1_w_gu[l], ffn1_w_down[l])
        u = _rmsnorm(h, mix_norm[l])
        aq, ak, av, az, abeta, aalpha, bcols, ga, gb = jnp.split(u @ w_in[l], _offsets(IN_SIZES), axis=-1)
        o_a = _deltanet_branch(aq, ak, av, az, abeta, aalpha, a_conv_w[l], a_log_rate[l],
                               a_dt_bias[l], a_out_norm[l])
        o_b = _rwkv7_branch(bcols, b_shift_mu[l], b_w0[l], b_w_up[l], b_a0[l], b_a_up[l], b_g_up[l],
                            b_k_k[l], b_k_a[l], b_r_k[l], b_ln_gain[l], b_ln_bias[l])
        merged = jax.nn.sigmoid(ga) * o_a + jax.nn.sigmoid(gb) * o_b
        h = h + merged @ w_out[l]
        h = h + 0.5 * _swiglu(_rmsnorm(h, ffn2_norm[l]), ffn2_w_gu[l], ffn2_w_down[l])
    return _rmsnorm(h, final_norm)[:, N_META:]


import jax as _jax
import jax.numpy as _jnp

TWIN_FORMAT = 'train_step'
FWD_PARAMS = ['x', 'meta_tokens', 'ffn1_norm', 'ffn1_w_gu', 'ffn1_w_down', 'mix_norm', 'w_in', 'a_conv_w', 'a_log_rate', 'a_dt_bias', 'a_out_norm', 'b_shift_mu', 'b_w0', 'b_w_up', 'b_a0', 'b_a_up', 'b_g_up', 'b_k_k', 'b_k_a', 'b_r_k', 'b_ln_gain', 'b_ln_bias', 'w_out', 'ffn2_norm', 'ffn2_w_gu', 'ffn2_w_down', 'final_norm']
TWIN_WEIGHTS = ['meta_tokens', 'ffn1_norm', 'ffn1_w_gu', 'ffn1_w_down', 'mix_norm', 'w_in', 'a_conv_w', 'a_log_rate', 'a_dt_bias', 'a_out_norm', 'b_shift_mu', 'b_w0', 'b_w_up', 'b_a0', 'b_a_up', 'b_g_up', 'b_k_k', 'b_k_a', 'b_r_k', 'b_ln_gain', 'b_ln_bias', 'w_out', 'ffn2_norm', 'ffn2_w_gu', 'ffn2_w_down', 'final_norm']
TWIN_DIFF_INPUT = 'x'
TWIN_INPUTS = ['x', 'meta_tokens', 'ffn1_norm', 'ffn1_w_gu', 'ffn1_w_down', 'mix_norm', 'w_in', 'a_conv_w', 'a_log_rate', 'a_dt_bias', 'a_out_norm', 'b_shift_mu', 'b_w0', 'b_w_up', 'b_a0', 'b_a_up', 'b_g_up', 'b_k_k', 'b_k_a', 'b_r_k', 'b_ln_gain', 'b_ln_bias', 'w_out', 'ffn2_norm', 'ffn2_w_gu', 'ffn2_w_down', 'final_norm', 'loss_target', 'm_meta_tokens', 'm_ffn1_norm', 'm_ffn1_w_gu', 'm_ffn1_w_down', 'm_mix_norm', 'm_w_in', 'm_a_conv_w', 'm_a_log_rate', 'm_a_dt_bias', 'm_a_out_norm', 'm_b_shift_mu', 'm_b_w0', 'm_b_w_up', 'm_b_a0', 'm_b_a_up', 'm_b_g_up', 'm_b_k_k', 'm_b_k_a', 'm_b_r_k', 'm_b_ln_gain', 'm_b_ln_bias', 'm_w_out', 'm_ffn2_norm', 'm_ffn2_w_gu', 'm_ffn2_w_down', 'm_final_norm', 'v_meta_tokens', 'v_ffn1_norm', 'v_ffn1_w_gu', 'v_ffn1_w_down', 'v_mix_norm', 'v_w_in', 'v_a_conv_w', 'v_a_log_rate', 'v_a_dt_bias', 'v_a_out_norm', 'v_b_shift_mu', 'v_b_w0', 'v_b_w_up', 'v_b_a0', 'v_b_a_up', 'v_b_g_up', 'v_b_k_k', 'v_b_k_a', 'v_b_r_k', 'v_b_ln_gain', 'v_b_ln_bias', 'v_w_out', 'v_ffn2_norm', 'v_ffn2_w_gu', 'v_ffn2_w_down', 'v_final_norm']
TWIN_OUTPUTS = ['loss', 'grad_x', 'grad_meta_tokens', 'grad_ffn1_norm', 'grad_ffn1_w_gu', 'grad_ffn1_w_down', 'grad_mix_norm', 'grad_w_in', 'grad_a_conv_w', 'grad_a_log_rate', 'grad_a_dt_bias', 'grad_a_out_norm', 'grad_b_shift_mu', 'grad_b_w0', 'grad_b_w_up', 'grad_b_a0', 'grad_b_a_up', 'grad_b_g_up', 'grad_b_k_k', 'grad_b_k_a', 'grad_b_r_k', 'grad_b_ln_gain', 'grad_b_ln_bias', 'grad_w_out', 'grad_ffn2_norm', 'grad_ffn2_w_gu', 'grad_ffn2_w_down', 'grad_final_norm', 'delta_meta_tokens', 'delta_ffn1_norm', 'delta_ffn1_w_gu', 'delta_ffn1_w_down', 'delta_mix_norm', 'delta_w_in', 'delta_a_conv_w', 'delta_a_log_rate', 'delta_a_dt_bias', 'delta_a_out_norm', 'delta_b_shift_mu', 'delta_b_w0', 'delta_b_w_up', 'delta_b_a0', 'delta_b_a_up', 'delta_b_g_up', 'delta_b_k_k', 'delta_b_k_a', 'delta_b_r_k', 'delta_b_ln_gain', 'delta_b_ln_bias', 'delta_w_out', 'delta_ffn2_norm', 'delta_ffn2_w_gu', 'delta_ffn2_w_down', 'delta_final_norm', 'new_m_meta_tokens', 'new_m_ffn1_norm', 'new_m_ffn1_w_gu', 'new_m_ffn1_w_down', 'new_m_mix_norm', 'new_m_w_in', 'new_m_a_conv_w', 'new_m_a_log_rate', 'new_m_a_dt_bias', 'new_m_a_out_norm', 'new_m_b_shift_mu', 'new_m_b_w0', 'new_m_b_w_up', 'new_m_b_a0', 'new_m_b_a_up', 'new_m_b_g_up', 'new_m_b_k_k', 'new_m_b_k_a', 'new_m_b_r_k', 'new_m_b_ln_gain', 'new_m_b_ln_bias', 'new_m_w_out', 'new_m_ffn2_norm', 'new_m_ffn2_w_gu', 'new_m_ffn2_w_down', 'new_m_final_norm', 'new_v_meta_tokens', 'new_v_ffn1_norm', 'new_v_ffn1_w_gu', 'new_v_ffn1_w_down', 'new_v_mix_norm', 'new_v_w_in', 'new_v_a_conv_w', 'new_v_a_log_rate', 'new_v_a_dt_bias', 'new_v_a_out_norm', 'new_v_b_shift_mu', 'new_v_b_w0', 'new_v_b_w_up', 'new_v_b_a0', 'new_v_b_a_up', 'new_v_b_g_up', 'new_v_b_k_k', 'new_v_b_k_a', 'new_v_b_r_k', 'new_v_b_ln_gain', 'new_v_b_ln_bias', 'new_v_w_out', 'new_v_ffn2_norm', 'new_v_ffn2_w_gu', 'new_v_ffn2_w_down', 'new_v_final_norm']
TWIN_LEAF_KINDS = {'loss': 'loss', 'grad_x': 'grad_x', 'grad_meta_tokens': 'grad_w', 'grad_ffn1_norm': 'grad_w', 'grad_ffn1_w_gu': 'grad_w', 'grad_ffn1_w_down': 'grad_w', 'grad_mix_norm': 'grad_w', 'grad_w_in': 'grad_w', 'grad_a_conv_w': 'grad_w', 'grad_a_log_rate': 'grad_w', 'grad_a_dt_bias': 'grad_w', 'grad_a_out_norm': 'grad_w', 'grad_b_shift_mu': 'grad_w', 'grad_b_w0': 'grad_w', 'grad_b_w_up': 'grad_w', 'grad_b_a0': 'grad_w', 'grad_b_a_up': 'grad_w', 'grad_b_g_up': 'grad_w', 'grad_b_k_k': 'grad_w', 'grad_b_k_a': 'grad_w', 'grad_b_r_k': 'grad_w', 'grad_b_ln_gain': 'grad_w', 'grad_b_ln_bias': 'grad_w', 'grad_w_out': 'grad_w', 'grad_ffn2_norm': 'grad_w', 'grad_ffn2_w_gu': 'grad_w', 'grad_ffn2_w_down': 'grad_w', 'grad_final_norm': 'grad_w', 'delta_meta_tokens': 'delta_w', 'delta_ffn1_norm': 'delta_w', 'delta_ffn1_w_gu': 'delta_w', 'delta_ffn1_w_down': 'delta_w', 'delta_mix_norm': 'delta_w', 'delta_w_in': 'delta_w', 'delta_a_conv_w': 'delta_w', 'delta_a_log_rate': 'delta_w', 'delta_a_dt_bias': 'delta_w', 'delta_a_out_norm': 'delta_w', 'delta_b_shift_mu': 'delta_w', 'delta_b_w0': 'delta_w', 'delta_b_w_up': 'delta_w', 'delta_b_a0': 'delta_w', 'delta_b_a_up': 'delta_w', 'delta_b_g_up': 'delta_w', 'delta_b_k_k': 'delta_w', 'delta_b_k_a': 'delta_w', 'delta_b_r_k': 'delta_w', 'delta_b_ln_gain': 'delta_w', 'delta_b_ln_bias': 'delta_w', 'delta_w_out': 'delta_w', 'delta_ffn2_norm': 'delta_w', 'delta_ffn2_w_gu': 'delta_w', 'delta_ffn2_w_down': 'delta_w', 'delta_final_norm': 'delta_w', 'new_m_meta_tokens': 'new_m', 'new_m_ffn1_norm': 'new_m', 'new_m_ffn1_w_gu': 'new_m', 'new_m_ffn1_w_down': 'new_m', 'new_m_mix_norm': 'new_m', 'new_m_w_in': 'new_m', 'new_m_a_conv_w': 'new_m', 'new_m_a_log_rate': 'new_m', 'new_m_a_dt_bias': 'new_m', 'new_m_a_out_norm': 'new_m', 'new_m_b_shift_mu': 'new_m', 'new_m_b_w0': 'new_m', 'new_m_b_w_up': 'new_m', 'new_m_b_a0': 'new_m', 'new_m_b_a_up': 'new_m', 'new_m_b_g_up': 'new_m', 'new_m_b_k_k': 'new_m', 'new_m_b_k_a': 'new_m', 'new_m_b_r_k': 'new_m', 'new_m_b_ln_gain': 'new_m', 'new_m_b_ln_bias': 'new_m', 'new_m_w_out': 'new_m', 'new_m_ffn2_norm': 'new_m', 'new_m_ffn2_w_gu': 'new_m', 'new_m_ffn2_w_down': 'new_m', 'new_m_final_norm': 'new_m', 'new_v_meta_tokens': 'new_v', 'new_v_ffn1_norm': 'new_v', 'new_v_ffn1_w_gu': 'new_v', 'new_v_ffn1_w_down': 'new_v', 'new_v_mix_norm': 'new_v', 'new_v_w_in': 'new_v', 'new_v_a_conv_w': 'new_v', 'new_v_a_log_rate': 'new_v', 'new_v_a_dt_bias': 'new_v', 'new_v_a_out_norm': 'new_v', 'new_v_b_shift_mu': 'new_v', 'new_v_b_w0': 'new_v', 'new_v_b_w_up': 'new_v', 'new_v_b_a0': 'new_v', 'new_v_b_a_up': 'new_v', 'new_v_b_g_up': 'new_v', 'new_v_b_k_k': 'new_v', 'new_v_b_k_a': 'new_v', 'new_v_b_r_k': 'new_v', 'new_v_b_ln_gain': 'new_v', 'new_v_b_ln_bias': 'new_v', 'new_v_w_out': 'new_v', 'new_v_ffn2_norm': 'new_v', 'new_v_ffn2_w_gu': 'new_v', 'new_v_ffn2_w_down': 'new_v', 'new_v_final_norm': 'new_v'}


def _forward(args):
    return _fwd_reference(*[args[k] for k in FWD_PARAMS])


def _output_shape():
    def fwd():
        inp = _fwd_setup_inputs(0)
        return _fwd_reference(*[inp[k] for k in FWD_PARAMS])
    out = _jax.eval_shape(fwd)
    return out.shape, out.dtype

N_MICROBATCH = 1
ADAM_LR = 0.001
ADAM_B1 = 0.9
ADAM_B2 = 0.999
ADAM_EPS = 1e-08
ADAM_WD = 0.01
ADAM_STEP = 10
PER_EXAMPLE_BATCH_AXIS = {'x': 0, 'loss_target': 0}
SHARED_INPUTS = []
_WEIGHT_DTYPES = {'meta_tokens': _jnp.float32, 'ffn1_norm': _jnp.float32, 'ffn1_w_gu': _jnp.float32, 'ffn1_w_down': _jnp.float32, 'mix_norm': _jnp.float32, 'w_in': _jnp.float32, 'a_conv_w': _jnp.float32, 'a_log_rate': _jnp.float32, 'a_dt_bias': _jnp.float32, 'a_out_norm': _jnp.float32, 'b_shift_mu': _jnp.float32, 'b_w0': _jnp.float32, 'b_w_up': _jnp.float32, 'b_a0': _jnp.float32, 'b_a_up': _jnp.float32, 'b_g_up': _jnp.float32, 'b_k_k': _jnp.float32, 'b_k_a': _jnp.float32, 'b_r_k': _jnp.float32, 'b_ln_gain': _jnp.float32, 'b_ln_bias': _jnp.float32, 'w_out': _jnp.float32, 'ffn2_norm': _jnp.float32, 'ffn2_w_gu': _jnp.float32, 'ffn2_w_down': _jnp.float32, 'final_norm': _jnp.float32}
MOMENT_SCALE = {'meta_tokens': 5.923607e-03, 'ffn1_norm': 8.992477e-02, 'ffn1_w_gu': 3.695607e-02, 'ffn1_w_down': 6.041355e-02, 'mix_norm': 1.304664e-01, 'w_in': 4.158593e-02, 'a_conv_w': 4.161515e-02, 'a_log_rate': 3.091384e-01, 'a_dt_bias': 3.043839e-01, 'a_out_norm': 1.600121e-01, 'b_shift_mu': 7.852156e-02, 'b_w0': 1.670260e-02, 'b_w_up': 1.913956e-03, 'b_a0': 1.716846e-02, 'b_a_up': 1.627843e-02, 'b_g_up': 4.792364e-02, 'b_k_k': 5.934765e-02, 'b_k_a': 4.929459e-02, 'b_r_k': 1.030358e-01, 'b_ln_gain': 4.777268e-02, 'b_ln_bias': 4.789160e-02, 'w_out': 6.957530e-02, 'ffn2_norm': 6.640641e-02, 'ffn2_w_gu': 2.779067e-02, 'ffn2_w_down': 4.538335e-02, 'final_norm': 3.199392e+01}


def _to_microbatches(a, axis):
    t = _jnp.moveaxis(a, axis, 0)
    t = t.reshape((N_MICROBATCH, t.shape[0] // N_MICROBATCH) + t.shape[1:])
    return _jnp.moveaxis(t, 1, axis + 1)


def setup_inputs(seed: int = 0) -> dict:
    inp = _fwd_setup_inputs(seed)
    key = _jax.random.fold_in(_jax.random.key(seed), 7919)
    shape, _ = _output_shape()
    out = dict(inp)
    out["loss_target"] = _jax.random.normal(_jax.random.fold_in(key, 0), shape, _jnp.float32)
    for i, name in enumerate(TWIN_WEIGHTS):
        w = inp[name].astype(_jnp.float32)
        if MOMENT_SCALE is None:
            s = _jnp.sqrt(_jnp.mean(_jnp.square(w)) + 1e-30)
        else:
            s = MOMENT_SCALE[name]
        km, kv = _jax.random.split(_jax.random.fold_in(key, i + 1))
        out[name] = w
        out["m_" + name] = s * _jax.random.normal(km, w.shape, _jnp.float32)
        out["v_" + name] = (s * s) * _jax.random.uniform(kv, w.shape, _jnp.float32, 0.5, 1.5)
    if N_MICROBATCH > 1:
        for name, axis in PER_EXAMPLE_BATCH_AXIS.items():
            out[name] = _to_microbatches(out[name], axis)
    return {'x': out['x'], 'meta_tokens': out['meta_tokens'], 'ffn1_norm': out['ffn1_norm'], 'ffn1_w_gu': out['ffn1_w_gu'], 'ffn1_w_down': out['ffn1_w_down'], 'mix_norm': out['mix_norm'], 'w_in': out['w_in'], 'a_conv_w': out['a_conv_w'], 'a_log_rate': out['a_log_rate'], 'a_dt_bias': out['a_dt_bias'], 'a_out_norm': out['a_out_norm'], 'b_shift_mu': out['b_shift_mu'], 'b_w0': out['b_w0'], 'b_w_up': out['b_w_up'], 'b_a0': out['b_a0'], 'b_a_up': out['b_a_up'], 'b_g_up': out['b_g_up'], 'b_k_k': out['b_k_k'], 'b_k_a': out['b_k_a'], 'b_r_k': out['b_r_k'], 'b_ln_gain': out['b_ln_gain'], 'b_ln_bias': out['b_ln_bias'], 'w_out': out['w_out'], 'ffn2_norm': out['ffn2_norm'], 'ffn2_w_gu': out['ffn2_w_gu'], 'ffn2_w_down': out['ffn2_w_down'], 'final_norm': out['final_norm'], 'loss_target': out['loss_target'], 'm_meta_tokens': out['m_meta_tokens'], 'm_ffn1_norm': out['m_ffn1_norm'], 'm_ffn1_w_gu': out['m_ffn1_w_gu'], 'm_ffn1_w_down': out['m_ffn1_w_down'], 'm_mix_norm': out['m_mix_norm'], 'm_w_in': out['m_w_in'], 'm_a_conv_w': out['m_a_conv_w'], 'm_a_log_rate': out['m_a_log_rate'], 'm_a_dt_bias': out['m_a_dt_bias'], 'm_a_out_norm': out['m_a_out_norm'], 'm_b_shift_mu': out['m_b_shift_mu'], 'm_b_w0': out['m_b_w0'], 'm_b_w_up': out['m_b_w_up'], 'm_b_a0': out['m_b_a0'], 'm_b_a_up': out['m_b_a_up'], 'm_b_g_up': out['m_b_g_up'], 'm_b_k_k': out['m_b_k_k'], 'm_b_k_a': out['m_b_k_a'], 'm_b_r_k': out['m_b_r_k'], 'm_b_ln_gain': out['m_b_ln_gain'], 'm_b_ln_bias': out['m_b_ln_bias'], 'm_w_out': out['m_w_out'], 'm_ffn2_norm': out['m_ffn2_norm'], 'm_ffn2_w_gu': out['m_ffn2_w_gu'], 'm_ffn2_w_down': out['m_ffn2_w_down'], 'm_final_norm': out['m_final_norm'], 'v_meta_tokens': out['v_meta_tokens'], 'v_ffn1_norm': out['v_ffn1_norm'], 'v_ffn1_w_gu': out['v_ffn1_w_gu'], 'v_ffn1_w_down': out['v_ffn1_w_down'], 'v_mix_norm': out['v_mix_norm'], 'v_w_in': out['v_w_in'], 'v_a_conv_w': out['v_a_conv_w'], 'v_a_log_rate': out['v_a_log_rate'], 'v_a_dt_bias': out['v_a_dt_bias'], 'v_a_out_norm': out['v_a_out_norm'], 'v_b_shift_mu': out['v_b_shift_mu'], 'v_b_w0': out['v_b_w0'], 'v_b_w_up': out['v_b_w_up'], 'v_b_a0': out['v_b_a0'], 'v_b_a_up': out['v_b_a_up'], 'v_b_g_up': out['v_b_g_up'], 'v_b_k_k': out['v_b_k_k'], 'v_b_k_a': out['v_b_k_a'], 'v_b_r_k': out['v_b_r_k'], 'v_b_ln_gain': out['v_b_ln_gain'], 'v_b_ln_bias': out['v_b_ln_bias'], 'v_w_out': out['v_w_out'], 'v_ffn2_norm': out['v_ffn2_norm'], 'v_ffn2_w_gu': out['v_ffn2_w_gu'], 'v_ffn2_w_down': out['v_ffn2_w_down'], 'v_final_norm': out['v_final_norm']}


def _loss(weights, diff, rest, loss_target):
    with _jax.named_scope("forward"):
        args = {**rest, TWIN_DIFF_INPUT: diff, **{k: w.astype(_WEIGHT_DTYPES[k]) for k, w in weights.items()}}
        y = _forward(args)
    with _jax.named_scope("loss_head"):
        err = _jnp.square(y.astype(_jnp.float32) - loss_target)
        return 0.5 * _jnp.sum(_jnp.mean(err, axis=-1)) if err.ndim else 0.5 * err


def _adamw(w, g, m, v):
    m = ADAM_B1 * m + (1.0 - ADAM_B1) * g
    v = ADAM_B2 * v + (1.0 - ADAM_B2) * _jnp.square(g)
    m_hat = m / (1.0 - ADAM_B1 ** ADAM_STEP)
    v_hat = v / (1.0 - ADAM_B2 ** ADAM_STEP)
    delta = -ADAM_LR * (m_hat / (_jnp.sqrt(v_hat) + ADAM_EPS) + ADAM_WD * w)
    return delta, m, v


def reference(x, meta_tokens, ffn1_norm, ffn1_w_gu, ffn1_w_down, mix_norm, w_in, a_conv_w, a_log_rate, a_dt_bias, a_out_norm, b_shift_mu, b_w0, b_w_up, b_a0, b_a_up, b_g_up, b_k_k, b_k_a, b_r_k, b_ln_gain, b_ln_bias, w_out, ffn2_norm, ffn2_w_gu, ffn2_w_down, final_norm, loss_target, m_meta_tokens, m_ffn1_norm, m_ffn1_w_gu, m_ffn1_w_down, m_mix_norm, m_w_in, m_a_conv_w, m_a_log_rate, m_a_dt_bias, m_a_out_norm, m_b_shift_mu, m_b_w0, m_b_w_up, m_b_a0, m_b_a_up, m_b_g_up, m_b_k_k, m_b_k_a, m_b_r_k, m_b_ln_gain, m_b_ln_bias, m_w_out, m_ffn2_norm, m_ffn2_w_gu, m_ffn2_w_down, m_final_norm, v_meta_tokens, v_ffn1_norm, v_ffn1_w_gu, v_ffn1_w_down, v_mix_norm, v_w_in, v_a_conv_w, v_a_log_rate, v_a_dt_bias, v_a_out_norm, v_b_shift_mu, v_b_w0, v_b_w_up, v_b_a0, v_b_a_up, v_b_g_up, v_b_k_k, v_b_k_a, v_b_r_k, v_b_ln_gain, v_b_ln_bias, v_w_out, v_ffn2_norm, v_ffn2_w_gu, v_ffn2_w_down, v_final_norm):
    given = dict(x=x, meta_tokens=meta_tokens, ffn1_norm=ffn1_norm, ffn1_w_gu=ffn1_w_gu, ffn1_w_down=ffn1_w_down, mix_norm=mix_norm, w_in=w_in, a_conv_w=a_conv_w, a_log_rate=a_log_rate, a_dt_bias=a_dt_bias, a_out_norm=a_out_norm, b_shift_mu=b_shift_mu, b_w0=b_w0, b_w_up=b_w_up, b_a0=b_a0, b_a_up=b_a_up, b_g_up=b_g_up, b_k_k=b_k_k, b_k_a=b_k_a, b_r_k=b_r_k, b_ln_gain=b_ln_gain, b_ln_bias=b_ln_bias, w_out=w_out, ffn2_norm=ffn2_norm, ffn2_w_gu=ffn2_w_gu, ffn2_w_down=ffn2_w_down, final_norm=final_norm, loss_target=loss_target, m_meta_tokens=m_meta_tokens, m_ffn1_norm=m_ffn1_norm, m_ffn1_w_gu=m_ffn1_w_gu, m_ffn1_w_down=m_ffn1_w_down, m_mix_norm=m_mix_norm, m_w_in=m_w_in, m_a_conv_w=m_a_conv_w, m_a_log_rate=m_a_log_rate, m_a_dt_bias=m_a_dt_bias, m_a_out_norm=m_a_out_norm, m_b_shift_mu=m_b_shift_mu, m_b_w0=m_b_w0, m_b_w_up=m_b_w_up, m_b_a0=m_b_a0, m_b_a_up=m_b_a_up, m_b_g_up=m_b_g_up, m_b_k_k=m_b_k_k, m_b_k_a=m_b_k_a, m_b_r_k=m_b_r_k, m_b_ln_gain=m_b_ln_gain, m_b_ln_bias=m_b_ln_bias, m_w_out=m_w_out, m_ffn2_norm=m_ffn2_norm, m_ffn2_w_gu=m_ffn2_w_gu, m_ffn2_w_down=m_ffn2_w_down, m_final_norm=m_final_norm, v_meta_tokens=v_meta_tokens, v_ffn1_norm=v_ffn1_norm, v_ffn1_w_gu=v_ffn1_w_gu, v_ffn1_w_down=v_ffn1_w_down, v_mix_norm=v_mix_norm, v_w_in=v_w_in, v_a_conv_w=v_a_conv_w, v_a_log_rate=v_a_log_rate, v_a_dt_bias=v_a_dt_bias, v_a_out_norm=v_a_out_norm, v_b_shift_mu=v_b_shift_mu, v_b_w0=v_b_w0, v_b_w_up=v_b_w_up, v_b_a0=v_b_a0, v_b_a_up=v_b_a_up, v_b_g_up=v_b_g_up, v_b_k_k=v_b_k_k, v_b_k_a=v_b_k_a, v_b_r_k=v_b_r_k, v_b_ln_gain=v_b_ln_gain, v_b_ln_bias=v_b_ln_bias, v_w_out=v_w_out, v_ffn2_norm=v_ffn2_norm, v_ffn2_w_gu=v_ffn2_w_gu, v_ffn2_w_down=v_ffn2_w_down, v_final_norm=v_final_norm)
    weights = {n: given[n] for n in TWIN_WEIGHTS}
    shared = {n: given[n] for n in SHARED_INPUTS}
    per_example = {n: given[n] for n in ['x']}
    grad_fn = _jax.value_and_grad(_loss, argnums=(0, 1))

    def one_microbatch(ex, loss_target):
        ex = dict(ex)
        diff = ex.pop(TWIN_DIFF_INPUT)
        return grad_fn(weights, diff, {**shared, **ex}, loss_target)

    if N_MICROBATCH == 1:
        loss, (grad_w, grad_x) = one_microbatch(per_example, given["loss_target"])
    else:
        def body(carry, xs):
            loss_sum, grad_sum = carry
            l_k, (gw_k, gx_k) = one_microbatch(xs[0], xs[1])
            with _jax.named_scope("update"):
                return (loss_sum + l_k, _jax.tree.map(_jnp.add, grad_sum, gw_k)), gx_k

        init = (_jnp.zeros((), _jnp.float32), _jax.tree.map(_jnp.zeros_like, weights))
        (loss, grad_w), grad_x = _jax.lax.scan(body, init, (per_example, given["loss_target"]))
    with _jax.named_scope("update"):
        delta_w, new_m, new_v = {}, {}, {}
        for n in TWIN_WEIGHTS:
            delta_w[n], new_m[n], new_v[n] = _adamw(weights[n], grad_w[n], given["m_" + n], given["v_" + n])
    return (loss, grad_x, *[grad_w[n] for n in TWIN_WEIGHTS], *[delta_w[n] for n in TWIN_WEIGHTS],
            *[new_m[n] for n in TWIN_WEIGHTS], *[new_v[n] for n in TWIN_WEIGHTS])
```

```python
import functools

import jax
import jax.numpy as jnp
from jax import lax
from jax.experimental import pallas as pl
from jax.experimental.pallas import tpu as pltpu

f32 = jnp.float32
bf16 = jnp.bfloat16
HI = lax.Precision.HIGHEST
MESH = pl.DeviceIdType.MESH
ANY = pl.BlockSpec(memory_space=pl.ANY)

D = 1024
N_META = 16
CHUNK = 64
PAD = CHUNK - N_META
SKIP = PAD + N_META
EPS = 1e-6
D_FF = 2816
A_HEADS = 8
A_DK = 128
B_HEADS = 16
B_N = 64
B_GN_EPS = B_N * 1e-5
W_LORA, AA_LORA, G_LORA = 64, 64, 160
IN_TOTAL = 9520
ZP = 9600
LANES = 128
N_CHIPS = 4
N_DEV = 8

ADAM_LR, ADAM_B1, ADAM_B2, ADAM_EPS, ADAM_WD, ADAM_STEP = 0.001, 0.9, 0.999, 1e-08, 0.01, 10

MXU_DTYPE = bf16


def _tile(n, cap, mult):
    if n <= cap:
        return n
    best = None
    for t in range(mult, cap + 1, mult):
        if n % t == 0:
            best = t
    assert best is not None, (n, cap, mult)
    return best


def _sigmoid(x):
    return jax.nn.sigmoid(x)


def _silu(x):
    return x * jax.nn.sigmoid(x)


def _softplus(x):
    return jnp.maximum(x, 0.0) + jnp.log(1.0 + jnp.exp(-jnp.abs(x)))


def _mmf(a, b):
    return jnp.dot(a, b, precision=HI, preferred_element_type=f32)


def _mm_nt(a, b):
    return lax.dot_general(a, b, (((1,), (1,)), ((), ())), precision=HI, preferred_element_type=f32)


def _mm_tn(a, b):
    return lax.dot_general(a, b, (((0,), (0,)), ((), ())), precision=HI, preferred_element_type=f32)


def _head_matrix(c, nh):
    hd = c // nh
    r = lax.broadcasted_iota(jnp.int32, (c, nh), 0)
    h = lax.broadcasted_iota(jnp.int32, (c, nh), 1)
    return ((r >= h * hd) & (r < (h + 1) * hd)).astype(f32)


def _head_sum(x, nh):
    e = _head_matrix(x.shape[-1], nh)
    return _mm_nt(_mmf(x, e), e)


@functools.partial(jax.custom_vjp, nondiff_argnums=(1,))
def _shift_rows(x, s):
    n = x.shape[0]
    row = lax.broadcasted_iota(jnp.int32, x.shape, 0)
    if s > 0:
        return jnp.where(row >= s, pltpu.roll(x, s, 0), 0.0)
    return jnp.where(row < n + s, pltpu.roll(x, n + s, 0), 0.0)


def _shift_rows_fwd(x, s):
    return _shift_rows(x, s), None


def _shift_rows_bwd(s, _, g):
    return (_shift_rows(g, -s),)


_shift_rows.defvjp(_shift_rows_fwd, _shift_rows_bwd)


def _matmul(a, b, *, ta=False, tb=False, res=None, scale=1.0, name):
    assert not (ta and tb)
    (ar, ac), (br, bc) = a.shape, b.shape
    m, k = (ac, ar) if ta else (ar, ac)
    n, kb = (br, bc) if tb else (bc, br)
    assert k == kb, (a.shape, b.shape, ta, tb)
    tm = _tile(m, 1408, LANES) if ta else _tile(m, 832, 8)
    tn = _tile(n, 1408, LANES)
    tk = _tile(k, 1040, 8) if ta else _tile(k, 1408, LANES)
    nk = k // tk
    dn = (((0 if ta else 1,), (1 if tb else 0,)), ((), ()))

    def body(*refs):
        if res is not None:
            a_ref, b_ref, r_ref, o_ref, acc = refs
        else:
            a_ref, b_ref, o_ref, acc = refs
        kk = pl.program_id(2)

        @pl.when(kk == 0)
        def _():
            acc[...] = jnp.zeros_like(acc)

        acc[...] += lax.dot_general(a_ref[...].astype(MXU_DTYPE), b_ref[...].astype(MXU_DTYPE), dn,
                                    preferred_element_type=f32,
                                    precision=None if MXU_DTYPE == bf16 else HI)

        @pl.when(kk == nk - 1)
        def _():
            out = acc[...]
            if scale != 1.0:
                out = out * scale
            if res is not None:
                out = r_ref[...] + out
            o_ref[...] = out

    if ta:
        a_spec = pl.BlockSpec((tk, tm), lambda i, j, kk: (kk, i))
    else:
        a_spec = pl.BlockSpec((tm, tk), lambda i, j, kk: (i, kk))
    if tb:
        b_spec = pl.BlockSpec((tn, tk), lambda i, j, kk: (j, kk))
    else:
        b_spec = pl.BlockSpec((tk, tn), lambda i, j, kk: (kk, j))
    in_specs = [a_spec, b_spec]
    args = [a, b]
    if res is not None:
        in_specs.append(pl.BlockSpec((tm, tn), lambda i, j, kk: (i, j)))
        args.append(res)
    return pl.pallas_call(
        body, name=name, grid=(m // tm, n // tn, nk), in_specs=in_specs,
        out_specs=pl.BlockSpec((tm, tn), lambda i, j, kk: (i, j)),
        out_shape=jax.ShapeDtypeStruct((m, n), f32),
        scratch_shapes=[pltpu.VMEM((tm, tn), f32)],
        compiler_params=pltpu.CompilerParams(dimension_semantics=("parallel", "parallel", "arbitrary")),
    )(*args)


def _tw_fwd(fn, ins, in_specs, out_shapes, out_specs, grid, name, with_pid=False):
    n_in = len(ins)

    def body(*refs):
        vals = [r[...] for r in refs[:n_in]]
        outs = fn(pl.program_id(0), *vals) if with_pid else fn(*vals)
        for r, o in zip(refs[n_in:], outs):
            r[...] = o

    return pl.pallas_call(body, name=name, grid=grid, in_specs=in_specs, out_specs=out_specs,
                          out_shape=out_shapes)(*ins)


def _tw_bwd(fn, ins, in_specs, cts, ct_specs, kinds, grid, name, with_pid=False):
    n_in, n_ct = len(ins), len(cts)
    diff = [i for i, kd in enumerate(kinds) if kd is not None]

    def body(*refs):
        vals = [r[...] for r in refs[:n_in]]
        ctv = tuple(r[...] for r in refs[n_in:n_in + n_ct])
        g_refs = refs[n_in + n_ct:]
        pid = pl.program_id(0)

        def f(*dv):
            full = list(vals)
            for i, v in zip(diff, dv):
                full[i] = v
            out = fn(pid, *full) if with_pid else fn(*full)
            return tuple(out)

        _, vjp = jax.vjp(f, *[vals[i] for i in diff])
        gs = vjp(ctv)
        first = pid == 0
        for i2 in range(1, len(grid)):
            first = first & (pl.program_id(i2) == 0)
        for i, g, g_ref in zip(diff, gs, g_refs):
            if kinds[i] != 'acc':
                g_ref[...] = g
            else:
                @pl.when(first)
                def _(g=g, g_ref=g_ref):
                    g_ref[...] = g

                @pl.when(jnp.logical_not(first))
                def _(g=g, g_ref=g_ref):
                    g_ref[...] += g

    zero_map = {1: lambda *a: (0,), 2: lambda *a: (0, 0), 3: lambda *a: (0, 0, 0)}
    out_specs, out_shapes = [], []
    for i in diff:
        if kinds[i] == 'tile':
            out_shapes.append(jax.ShapeDtypeStruct(ins[i].shape, f32))
            out_specs.append(in_specs[i])
        elif kinds[i] == 'acc':
            out_shapes.append(jax.ShapeDtypeStruct(ins[i].shape, f32))
            out_specs.append(pl.BlockSpec(ins[i].shape, zero_map[ins[i].ndim]))
        else:
            out_shapes.append(jax.ShapeDtypeStruct(kinds[i][1], f32))
            out_specs.append(kinds[i][2])
    return pl.pallas_call(body, name=name, grid=grid, in_specs=list(in_specs) + list(ct_specs),
                          out_specs=out_specs, out_shape=out_shapes)(*ins, *cts)


def _row_spec(tm, c, col_block=0):
    return pl.BlockSpec((tm, c), lambda i, cb=col_block: (i, cb))


def _full_spec(shape):
    nd = len(shape)
    return pl.BlockSpec(shape, lambda *a, nd=nd: (0,) * nd)


def _f_rms(x, g):
    return (x * lax.rsqrt(jnp.mean(x * x, axis=-1, keepdims=True) + EPS) * g,)


def _f_swiglu(gate, up):
    return (_silu(gate) * up,)


def _f_loss(pid, h, g, tgt, *, tm):
    y = h * lax.rsqrt(jnp.mean(h * h, axis=-1, keepdims=True) + EPS) * g
    row = pid * tm + lax.broadcasted_iota(jnp.int32, (tm, 1), 0)
    err = jnp.where(row >= SKIP, y - tgt, 0.0)
    per_row = jnp.mean(err * err, axis=-1, keepdims=True)
    return (0.5 * jnp.sum(per_row, axis=0, keepdims=True),)


def _f_conv(x, w, *, norm, scale):
    y = x * w[3:4, :]
    for s in (1, 2, 3):
        y = y + _shift_rows(x, s) * w[3 - s:4 - s, :]
    y = _silu(y)
    if norm:
        y = y * lax.rsqrt(jnp.sum(y * y, axis=-1, keepdims=True) + 1e-6) * scale
    return (y,)


def _f_dgates(pid, abeta, aalpha, log_rate, dt_bias, *, tm):
    row = pid * tm + lax.broadcasted_iota(jnp.int32, (tm, 1), 0)
    live = row >= PAD
    beta = jnp.where(live, _sigmoid(abeta), 0.0)
    g = jnp.where(live, -jnp.exp(log_rate) * _softplus(aalpha + dt_bias), 0.0)
    return beta, g


def _f_tshift(z, mu):
    return (z + (_shift_rows(z, 1) - z) * mu,)


def _f_rwkv_pre(k, wd, ad, gd, w0, w_up, a0, a_up, g_up, k_k, k_a):
    w_log = -_softplus(-(w0 + _mmf(jnp.tanh(wd), w_up))) - 0.5
    lw = -jnp.exp(w_log)
    a_lr = _sigmoid(a0 + _mmf(ad, a_up))
    gate = _mmf(_sigmoid(gd), g_up)
    kkp = k * k_k
    kk = kkp * lax.rsqrt(_head_sum(kkp * kkp, B_HEADS) + 1e-6)
    kmod = k * (1.0 + (a_lr - 1.0) * k_a)
    return lw, kmod, -kk, kk * a_lr, gate


def _f_mix_post(o, az, y, r, kmod, v, gate, ga, gb, out_gain, ln_g, ln_b, r_k):
    ms = _head_sum(o * o, A_HEADS) * (1.0 / A_DK)
    oa = o * lax.rsqrt(ms + EPS) * out_gain * _silu(az)
    mean = _head_sum(y, B_HEADS) * (1.0 / B_N)
    yc = y - mean
    var = _head_sum(yc * yc, B_HEADS) * (1.0 / B_N)
    yn = yc * lax.rsqrt(var + B_GN_EPS) * ln_g + ln_b
    bonus = _head_sum(r * kmod * r_k, B_HEADS) * v
    ob = (yn + bonus) * gate
    return (_sigmoid(ga) * oa + _sigmoid(gb) * ob,)


def _tri_masks(n):
    i = lax.broadcasted_iota(jnp.int32, (n, n), 0)
    j = lax.broadcasted_iota(jnp.int32, (n, n), 1)
    return i >= j, i > j, i == j, i <= j


def _unit_lower_inv(low):
    n = low.shape[0]
    assert n == CHUNK
    _, _, eye, _ = _tri_masks(n)
    acc = eye.astype(f32) + low
    p = low
    for _ in range(5):
        p = _mmf(p, p)
        acc = acc + _mmf(acc, p)
    return acc


def _delta_chunk(s, q, k, v, beta, g):
    incl, strict, eye, upper = _tri_masks(CHUNK)
    g_row = jnp.sum(jnp.where(eye, g, 0.0), axis=0, keepdims=True)
    gc = jnp.sum(jnp.where(incl, g_row, 0.0), axis=1, keepdims=True)
    gc_row = jnp.sum(jnp.where(upper, g, 0.0), axis=0, keepdims=True)
    decay = jnp.where(incl, jnp.exp(jnp.where(incl, gc - gc_row, 0.0)), 0.0)
    kb = k * beta
    vb = v * beta
    m = jnp.where(strict, _mm_nt(kb, k) * decay, 0.0)
    tinv = _unit_lower_inv(-m)
    u = _mmf(tinv, vb)
    wk = _mmf(tinv, kb * jnp.exp(gc))
    attn = _mm_nt(q, k) * decay
    qg = q * jnp.exp(gc)
    g_last = jnp.sum(g, axis=0, keepdims=True)
    k_tail = k * jnp.exp(g_last - gc)
    v_new = u - _mmf(wk, s)
    o = _mmf(qg, s) + _mmf(attn, v_new)
    s_new = s * jnp.exp(g_last) + _mm_tn(k_tail, v_new)
    return o, s_new


def _rwkv_chunk(st, r, k, v, a, b, lw):
    c = CHUNK
    incl, strict, _, _ = _tri_masks(c)
    lane = lax.broadcasted_iota(jnp.int32, (c, 2 * B_N), 1)
    first = lane < B_N
    bi = lax.broadcasted_iota(jnp.int32, (2 * B_N, 2 * B_N), 0) < B_N
    bj = lax.broadcasted_iota(jnp.int32, (2 * B_N, 2 * B_N), 1) < B_N
    blockdiag = bi == bj
    cum = _mmf(incl.astype(f32), lw)
    e_pos = jnp.exp(cum)
    e_neg = jnp.exp(-cum)
    rt = r * e_pos
    at = a * jnp.exp(cum - lw)
    kt = k * e_neg
    bt = b * e_neg
    a_s0 = _mm_nt(at, st)
    r_s0 = _mm_nt(rt, st)
    u = jnp.zeros((c, 2 * B_N), f32)
    for sel in (first, jnp.logical_not(first)):
        at_h = jnp.where(sel, at, 0.0)
        ab = jnp.where(strict, _mm_nt(at_h, bt), 0.0)
        ak = jnp.where(strict, _mm_nt(at_h, kt), 0.0)
        t_h = _unit_lower_inv(ab)
        u_h = _mmf(t_h, jnp.where(sel, a_s0, 0.0) + _mmf(ak, jnp.where(sel, v, 0.0)))
        u = u + u_h
    y = r_s0
    for sel in (first, jnp.logical_not(first)):
        rt_h = jnp.where(sel, rt, 0.0)
        rb = jnp.where(incl, _mm_nt(rt_h, bt), 0.0)
        rk = jnp.where(incl, _mm_nt(rt_h, kt), 0.0)
        y = y + _mmf(rb, jnp.where(sel, u, 0.0)) + _mmf(rk, jnp.where(sel, v, 0.0))
    cl = jnp.sum(lw, axis=0, keepdims=True)
    dec = jnp.exp(cl - cum)
    st_new = st * jnp.exp(cl) + jnp.where(blockdiag, _mm_tn(u, b * dec) + _mm_tn(v, k * dec), 0.0)
    return y, st_new


def _scan_fwd(chunk_fn, ins, specs, n_groups, n_chunks, state_shape, out_shape, out_spec, name):
    n_in = len(ins)

    def body(*refs):
        in_refs = refs[:n_in]
        o_ref, s0_ref, st = refs[n_in:]

        @pl.when(pl.program_id(1) == 0)
        def _():
            st[...] = jnp.zeros_like(st)

        s = st[...]
        s0_ref[...] = s
        o, s_new = chunk_fn(s, *[r[...] for r in in_refs])
        o_ref[...] = o
        st[...] = s_new

    s0_shape = (n_groups, n_chunks) + state_shape
    return pl.pallas_call(
        body, name=name, grid=(n_groups, n_chunks), in_specs=specs,
        out_specs=[out_spec, pl.BlockSpec((None, None) + state_shape, lambda h, c: (h, c, 0, 0))],
        out_shape=[out_shape, jax.ShapeDtypeStruct(s0_shape, f32)],
        scratch_shapes=[pltpu.VMEM(state_shape, f32)],
        compiler_params=pltpu.CompilerParams(dimension_semantics=("parallel", "arbitrary")),
    )(*ins)


def _scan_bwd(chunk_fn, s0s, ins, rev_specs, g_shapes, g_specs, d_out, d_out_spec, n_groups, n_chunks, state_shape,
              name):
    n_in = len(ins)

    def body(*refs):
        s0_ref = refs[0]
        in_refs = refs[1:1 + n_in]
        do_ref = refs[1 + n_in]
        g_refs = refs[2 + n_in:2 + 2 * n_in]
        dst = refs[2 + 2 * n_in]

        @pl.when(pl.program_id(1) == 0)
        def _():
            dst[...] = jnp.zeros_like(dst)

        _, vjp = jax.vjp(chunk_fn, s0_ref[...], *[r[...] for r in in_refs])
        gs = vjp((do_ref[...], dst[...]))
        dst[...] = gs[0]
        for g_ref, g in zip(g_refs, gs[1:]):
            g_ref[...] = g

    s0_spec = pl.BlockSpec((None, None) + state_shape, lambda h, c: (h, n_chunks - 1 - c, 0, 0))
    return pl.pallas_call(
        body, name=name, grid=(n_groups, n_chunks), in_specs=[s0_spec] + list(rev_specs) + [d_out_spec],
        out_specs=list(g_specs),
        out_shape=[jax.ShapeDtypeStruct(sh, f32) for sh in g_shapes],
        scratch_shapes=[pltpu.VMEM(state_shape, f32)],
        compiler_params=pltpu.CompilerParams(dimension_semantics=("parallel", "arbitrary")),
    )(s0s, *ins, d_out)


def _rms_fwd(x, g, name):
    t = x.shape[0]
    tm = _tile(t, 416, 8)
    return _tw_fwd(_f_rms, [x, g], [_row_spec(tm, D), _full_spec(g.shape)],
                   [jax.ShapeDtypeStruct(x.shape, f32)], [_row_spec(tm, D)], (t // tm,), name)[0]


def _rms_bwd(x, g, dy, name):
    t = x.shape[0]
    tm = _tile(t, 416, 8)
    return _tw_bwd(_f_rms, [x, g], [_row_spec(tm, D), _full_spec(g.shape)], [dy], [_row_spec(tm, D)],
                   ['tile', 'acc'], (t // tm,), name)


def _ffn_fwd(h, gain, wg, wu, wd, tag):
    xn = _rms_fwd(h, gain, f"{tag}_rms")
    gate = _matmul(xn, wg, name=f"{tag}_gate")
    up = _matmul(xn, wu, name=f"{tag}_up")
    t = h.shape[0]
    tm = _tile(t, 208, 8)
    act = _tw_fwd(_f_swiglu, [gate, up], [_row_spec(tm, D_FF)] * 2, [jax.ShapeDtypeStruct((t, D_FF), f32)],
                  [_row_spec(tm, D_FF)], (t // tm,), f"{tag}_act")[0]
    out = _matmul(act, wd, res=h, scale=0.5, name=f"{tag}_down")
    return out, (xn, gate, up, act)


def _ffn_bwd(h, gain, wg, wu, wd, saved, dout, tag):
    xn, gate, up, act = saved
    t = h.shape[0]
    d_wd = _matmul(act, dout, ta=True, scale=0.5, name=f"{tag}_dwd")
    d_act = _matmul(dout, wd, tb=True, scale=0.5, name=f"{tag}_dact")
    tm = _tile(t, 208, 8)
    d_gate, d_up = _tw_bwd(_f_swiglu, [gate, up], [_row_spec(tm, D_FF)] * 2, [d_act], [_row_spec(tm, D_FF)],
                           ['tile', 'tile'], (t // tm,), f"{tag}_dactf")
    d_wg = _matmul(xn, d_gate, ta=True, name=f"{tag}_dwg")
    d_wu = _matmul(xn, d_up, ta=True, name=f"{tag}_dwu")
    d_xn = _matmul(d_gate, wg, tb=True, name=f"{tag}_dxn_g")
    d_xn = _matmul(d_up, wu, tb=True, res=d_xn, name=f"{tag}_dxn_u")
    d_hn, d_gain = _rms_bwd(h, gain, d_xn, f"{tag}_drms")
    return d_hn, d_gain, d_wg, d_wu, d_wd


def _col_spec(t, first_block):
    return pl.BlockSpec((t, LANES), lambda j, fb=first_block: (0, j + fb))


def _local_step(h0, tgt, w):
    t = h0.shape[0]
    assert t % CHUNK == 0
    nc = t // CHUNK
    grads = {}

    h1, ffn1_saved = _ffn_fwd(h0, w['ffn1_norm'], w['ffn1_wg'], w['ffn1_wu'], w['ffn1_wd'], "ffn1")
    u = _rms_fwd(h1, w['mix_norm'], "mix_rms")
    z = _matmul(u, w['w_in_p'], name="in_proj")
    zs = z[:, 9216:9216 + 304]
    abeta, aalpha = zs[:, 288:296], zs[:, 296:304]

    conv_w = w['a_conv_w']
    conv_fns = [functools.partial(_f_conv, norm=True, scale=A_DK ** -0.5),
                functools.partial(_f_conv, norm=True, scale=1.0),
                functools.partial(_f_conv, norm=False, scale=1.0)]
    qkv = []
    for idx, fn in enumerate(conv_fns):
        qkv.append(_tw_fwd(fn, [z, conv_w], [_col_spec(t, 8 * idx), pl.BlockSpec((4, LANES), lambda j, o=8 * idx: (0, j + o))],
                           [jax.ShapeDtypeStruct((t, D), f32)], [_col_spec(t, 0)], (A_HEADS,), f"a_conv{idx}")[0])
    aq, ak, av = qkv
    tmg = _tile(t, 1040, 8)
    dg_fn = functools.partial(_f_dgates, tm=tmg)
    dg_specs = [_row_spec(tmg, A_HEADS)] * 2 + [_full_spec((1, A_HEADS))] * 2
    beta, gdec = _tw_fwd(dg_fn, [abeta, aalpha, w['a_log_rate'], w['a_dt_bias']], dg_specs,
                         [jax.ShapeDtypeStruct((t, A_HEADS), f32)] * 2, [_row_spec(tmg, A_HEADS)] * 2, (t // tmg,),
                         "a_gates", with_pid=True)
    beta_h = beta.T[:, :, None]
    gdec_h = gdec.T[:, :, None]
    hd_spec = pl.BlockSpec((CHUNK, LANES), lambda h, c: (c, h))
    sc_spec = pl.BlockSpec((None, CHUNK, 1), lambda h, c: (h, c, 0))
    a_ins = [aq, ak, av, beta_h, gdec_h]
    o_scan, a_s0 = _scan_fwd(_delta_chunk, a_ins, [hd_spec] * 3 + [sc_spec] * 2, A_HEADS, nc, (A_DK, A_DK),
                             jax.ShapeDtypeStruct((t, D), f32), hd_spec, "a_scan")

    mu = w['b_shift_mu']
    mu_rkv, mu_s = mu[:, :3072], mu[:, 3072:]
    zf_rkv = _tw_fwd(_f_tshift, [z, mu_rkv], [_col_spec(t, 32), pl.BlockSpec((1, LANES), lambda j: (0, j))],
                     [jax.ShapeDtypeStruct((t, 3072), f32)], [_col_spec(t, 0)], (24,), "b_shift")[0]
    zs_b = zs[:, :288]
    zf_s = _tw_fwd(_f_tshift, [zs_b, mu_s], [_full_spec((t, 288)), _full_spec((1, 288))],
                   [jax.ShapeDtypeStruct((t, 288), f32)], [_full_spec((t, 288))], (1,), "b_shift_s")[0]
    wdf, adf, gdf = zf_s[:, 0:64], zf_s[:, 64:128], zf_s[:, 128:288]
    tmr = _tile(t, 160, 8)
    pre_params = [w['b_w0'], w['b_w_up'], w['b_a0'], w['b_a_up'], w['b_g_up'], w['b_k_k'], w['b_k_a']]
    pre_ins = [zf_rkv, wdf, adf, gdf] + pre_params
    pre_specs = ([_row_spec(tmr, D, 1), _row_spec(tmr, 64), _row_spec(tmr, 64), _row_spec(tmr, 160)]
                 + [_full_spec(p.shape) for p in pre_params])
    lw, kmod, a_s, b_s, bgate = _tw_fwd(_f_rwkv_pre, pre_ins, pre_specs, [jax.ShapeDtypeStruct((t, D), f32)] * 5,
                                        [_row_spec(tmr, D)] * 5, (t // tmr,), "b_pre")
    pr_spec = pl.BlockSpec((CHUNK, LANES), lambda h, c: (c, h))
    r_spec = pl.BlockSpec((CHUNK, LANES), lambda h, c: (c, h))
    v_spec = pl.BlockSpec((CHUNK, LANES), lambda h, c: (c, h + 16))
    b_ins = [zf_rkv, kmod, zf_rkv, a_s, b_s, lw]
    b_specs = [r_spec, pr_spec, v_spec, pr_spec, pr_spec, pr_spec]
    y_scan, b_s0 = _scan_fwd(_rwkv_chunk, b_ins, b_specs, B_HEADS // 2, nc, (2 * B_N, 2 * B_N),
                             jax.ShapeDtypeStruct((t, D), f32), pr_spec, "b_scan")

    out_gain_t = jnp.tile(w['a_out_norm'], (1, A_HEADS))
    r_k = w['b_r_k'].reshape(1, D)
    post_params = [out_gain_t, w['b_ln_gain'], w['b_ln_bias'], r_k]
    post_ins = [o_scan, z, y_scan, zf_rkv, kmod, zf_rkv, bgate, z, z] + post_params
    post_specs = ([_row_spec(tmr, D), _row_spec(tmr, D, 3), _row_spec(tmr, D), _row_spec(tmr, D, 0), _row_spec(tmr, D),
                   _row_spec(tmr, D, 2), _row_spec(tmr, D), _row_spec(tmr, D, 7), _row_spec(tmr, D, 8)]
                  + [_full_spec((1, D))] * 4)
    merged = _tw_fwd(_f_mix_post, post_ins, post_specs, [jax.ShapeDtypeStruct((t, D), f32)], [_row_spec(tmr, D)],
                     (t // tmr,), "mix_post")[0]
    h2 = _matmul(merged, w['w_out'], res=h1, name="out_proj")
    h3, ffn2_saved = _ffn_fwd(h2, w['ffn2_norm'], w['ffn2_wg'], w['ffn2_wu'], w['ffn2_wd'], "ffn2")

    tml = _tile(t, 416, 8)
    fnorm = w['final_norm']
    loss_fn = functools.partial(_f_loss, tm=tml)
    loss_specs = [_row_spec(tml, D), _full_spec((1, D)), _row_spec(tml, D)]
    loss_parts = _tw_fwd(loss_fn, [h3, fnorm, tgt], loss_specs, [jax.ShapeDtypeStruct((t // tml, 1, 1), f32)],
                         [pl.BlockSpec((None, 1, 1), lambda i: (i, 0, 0))], (t // tml,), "loss", with_pid=True)[0]
    loss = jnp.sum(loss_parts)
    ones = jnp.ones((t // tml, 1, 1), f32)
    d_h3, grads['final_norm'] = _tw_bwd(loss_fn, [h3, fnorm, tgt], loss_specs, [ones],
                                        [pl.BlockSpec((None, 1, 1), lambda i: (i, 0, 0))], ['tile', 'acc', None],
                                        (t // tml,), "loss_bwd", with_pid=True)

    d_hn, grads['ffn2_norm'], grads['ffn2_wg'], grads['ffn2_wu'], grads['ffn2_wd'] = _ffn_bwd(
        h2, w['ffn2_norm'], w['ffn2_wg'], w['ffn2_wu'], w['ffn2_wd'], ffn2_saved, d_h3, "ffn2")
    d_h2 = _add(d_h3, d_hn, "add_h2")
    grads['w_out'] = _matmul(merged, d_h2, ta=True, name="d_w_out")
    d_merged = _matmul(d_h2, w['w_out'], tb=True, name="d_merged")

    win = ('tile', (t, D), _row_spec(tmr, D))
    post_kinds = ['tile', win, 'tile', win, 'tile', win, 'tile', win, win] + ['acc'] * 4
    (d_o, d_az, d_y, d_r1, d_kmod1, d_v1, d_bgate, d_ga, d_gb,
     d_out_gain_t, grads['b_ln_gain'], grads['b_ln_bias'], d_r_k) = _tw_bwd(
        _f_mix_post, post_ins, post_specs, [d_merged], [_row_spec(tmr, D)], post_kinds, (t // tmr,), "mix_post_bwd")
    grads['a_out_norm'] = jnp.sum(d_out_gain_t.reshape(A_HEADS, A_DK), axis=0, keepdims=True)
    grads['b_r_k'] = d_r_k.reshape(1, B_HEADS, B_N)

    rev = lambda spec_cols: pl.BlockSpec((CHUNK, LANES), lambda h, c, o=spec_cols: (nc - 1 - c, h + o))
    b_rev_specs = [rev(0), rev(0), rev(16), rev(0), rev(0), rev(0)]
    d_r2, d_kmod2, d_v2, d_as, d_bs, d_lw = _scan_bwd(_rwkv_chunk, b_s0, b_ins, b_rev_specs, [(t, D)] * 6, [rev(0)] * 6,
                                                      d_y, rev(0), B_HEADS // 2, nc, (2 * B_N, 2 * B_N), "b_scan_bwd")
    d_kmod = _add(d_kmod1, d_kmod2, "add_kmod")
    pre_kinds = [win] + ['tile'] * 3 + ['acc'] * 7
    pre_ct_specs = [_row_spec(tmr, D)] * 5
    (d_zf_k, d_wdf, d_adf, d_gdf, grads['b_w0'], grads['b_w_up'], grads['b_a0'], grads['b_a_up'], grads['b_g_up'],
     grads['b_k_k'], grads['b_k_a']) = _tw_bwd(
        _f_rwkv_pre, pre_ins, pre_specs, [d_lw, d_kmod, d_as, d_bs, d_bgate], pre_ct_specs, pre_kinds, (t // tmr,),
        "b_pre_bwd")
    d_zf_rkv = _assemble3(d_r1, d_r2, d_zf_k, d_v1, d_v2, "b_dzf")
    d_zb_rkv, d_mu_rkv = _tw_bwd(_f_tshift, [z, mu_rkv], [_col_spec(t, 32), pl.BlockSpec((1, LANES), lambda j: (0, j))],
                                 [d_zf_rkv], [_col_spec(t, 0)], [('tile', (t, 3072), _col_spec(t, 0)), 'tile'], (24,),
                                 "b_shift_bwd")
    d_zf_s = jnp.concatenate([d_wdf, d_adf, d_gdf], axis=1)
    d_zs_b, d_mu_s = _tw_bwd(_f_tshift, [zs_b, mu_s], [_full_spec((t, 288)), _full_spec((1, 288))], [d_zf_s],
                             [_full_spec((t, 288))], ['tile', 'tile'], (1,), "b_shift_s_bwd")
    grads['b_shift_mu'] = jnp.concatenate([d_mu_rkv, d_mu_s], axis=1)

    hd_rev = pl.BlockSpec((CHUNK, LANES), lambda h, c: (nc - 1 - c, h))
    sc_rev = pl.BlockSpec((None, CHUNK, 1), lambda h, c: (h, nc - 1 - c, 0))
    d_aq, d_ak, d_av, d_beta_h, d_g_h = _scan_bwd(_delta_chunk, a_s0, a_ins, [hd_rev] * 3 + [sc_rev] * 2,
                                                  [a.shape for a in a_ins], [hd_rev] * 3 + [sc_rev] * 2, d_o, hd_rev,
                                                  A_HEADS, nc, (A_DK, A_DK), "a_scan_bwd")
    d_beta = d_beta_h[:, :, 0].T
    d_gdec = d_g_h[:, :, 0].T
    d_abeta, d_aalpha, grads['a_log_rate'], grads['a_dt_bias'] = _tw_bwd(
        dg_fn, [abeta, aalpha, w['a_log_rate'], w['a_dt_bias']], dg_specs, [d_beta, d_gdec],
        [_row_spec(tmg, A_HEADS)] * 2, ['tile', 'tile', 'acc', 'acc'], (t // tmg,), "a_gates_bwd", with_pid=True)
    d_zqkv, d_conv = [], []
    for idx, (fn, ct) in enumerate(zip(conv_fns, (d_aq, d_ak, d_av))):
        dz_i, dw_i = _conv_bwd(fn, z, conv_w, ct, idx, t)
        d_zqkv.append(dz_i)
        d_conv.append(dw_i)
    grads['a_conv_w'] = jnp.concatenate(d_conv, axis=1)

    d_z = jnp.concatenate(
        d_zqkv + [d_az, d_zb_rkv, d_ga, d_gb, d_zs_b, d_abeta, d_aalpha, jnp.zeros((t, ZP - 9216 - 304), f32)], axis=1)
    grads['w_in_p'] = _matmul(u, d_z, ta=True, name="d_w_in")
    d_u = _matmul(d_z, w['w_in_p'], tb=True, name="d_u")
    d_h1n, grads['mix_norm'] = _rms_bwd(h1, w['mix_norm'], d_u, "mix_drms")
    d_h1 = _add(d_h2, d_h1n, "add_h1")
    d_h0n, grads['ffn1_norm'], grads['ffn1_wg'], grads['ffn1_wu'], grads['ffn1_wd'] = _ffn_bwd(
        h0, w['ffn1_norm'], w['ffn1_wg'], w['ffn1_wu'], w['ffn1_wd'], ffn1_saved, d_h1, "ffn1")
    d_h0 = _add(d_h1, d_h0n, "add_h0")
    return loss, d_h0, grads


_WIN_SEGMENTS = ((0, 4096), (4112, 7184), (7472, 9520), (7184, 7472), (4096, 4112))


def _win_to_padded(w_in):
    parts = [w_in[:, a:b] for a, b in _WIN_SEGMENTS]
    parts.append(jnp.zeros((w_in.shape[0], ZP - IN_TOTAL), w_in.dtype))
    return jnp.concatenate(parts, axis=1)


def _win_from_padded(w_p):
    widths = [b - a for a, b in _WIN_SEGMENTS]
    offs = [sum(widths[:i]) for i in range(len(widths))]
    seg = {a: w_p[:, o:o + wd] for (a, _), o, wd in zip(_WIN_SEGMENTS, offs, widths)}
    return jnp.concatenate([seg[a] for a in sorted(seg)], axis=1)


def _add(a, b, name):
    t, c = a.shape
    tm = _tile(t, 416, 8)
    return _tw_fwd(lambda x, y: (x + y,), [a, b], [_row_spec(tm, c)] * 2, [jax.ShapeDtypeStruct(a.shape, f32)],
                   [_row_spec(tm, c)], (t // tm,), name)[0]


def _assemble3(d_r1, d_r2, d_k, d_v1, d_v2, name):
    t = d_r1.shape[0]
    tm = _tile(t, 208, 8)

    def body(r1, r2, kk, v1, v2, o_ref):
        o_ref[:, 0:D] = r1[...] + r2[...]
        o_ref[:, D:2 * D] = kk[...]
        o_ref[:, 2 * D:3 * D] = v1[...] + v2[...]

    return pl.pallas_call(body, name=name, grid=(t // tm,), in_specs=[_row_spec(tm, D)] * 5,
                          out_specs=_row_spec(tm, 3 * D), out_shape=jax.ShapeDtypeStruct((t, 3 * D), f32),
                          )(d_r1, d_r2, d_k, d_v1, d_v2)


def _conv_bwd(fn, z, conv_w, ct, idx, t):
    def body(z_ref, w_ref, ct_ref, dz_ref, dw_ref):
        _, vjp = jax.vjp(lambda a, b: fn(a, b), z_ref[...], w_ref[...])
        dz, dw = vjp((ct_ref[...],))
        dz_ref[...] = dz
        dw_ref[...] = dw

    return pl.pallas_call(
        body, name=f"a_conv{idx}_bwd", grid=(A_HEADS,),
        in_specs=[_col_spec(t, 8 * idx), pl.BlockSpec((4, LANES), lambda j, o=8 * idx: (0, j + o)), _col_spec(t, 0)],
        out_specs=[_col_spec(t, 0), pl.BlockSpec((4, LANES), lambda j: (0, j))],
        out_shape=[jax.ShapeDtypeStruct((t, D), f32), jax.ShapeDtypeStruct((4, D), f32)],
    )(z, conv_w, ct)


def _position():
    return lax.axis_index("x"), lax.axis_index("y"), lax.axis_index("c")


def _flip(v, f):
    return 1 - v if f else v


_CHIP_FLIPS = ((1, 0), (0, 1), (1, 1))
_DEV_FLIPS = tuple((fx, fy, fc) for fx in (0, 1) for fy in (0, 1) for fc in (0, 1) if (fx, fy, fc) != (0, 0, 0))


def _gather_chips(arrs, name):
    n = len(arrs)

    def body(*refs):
        ins, outs = refs[:n], refs[n:2 * n]
        send, recv, loc = refs[2 * n:]
        x, y, c = _position()
        me = 2 * x + y
        started = []
        for a in range(n):
            lc = pltpu.make_async_copy(ins[a], outs[a].at[me], loc.at[a])
            lc.start()
            started.append(lc)
        sends, recvs = [], []
        for a in range(n):
            for j, (fx, fy) in enumerate(_CHIP_FLIPS):
                px, py = _flip(x, fx), _flip(y, fy)
                cp = pltpu.make_async_remote_copy(src_ref=ins[a], dst_ref=outs[a].at[me], send_sem=send.at[a, j],
                                                  recv_sem=recv.at[a, j], device_id=(px, py, c), device_id_type=MESH)
                cp.start()
                sends.append(cp)
                recvs.append(pltpu.make_async_remote_copy(
                    src_ref=ins[a], dst_ref=outs[a].at[2 * px + py], send_sem=send.at[a, j], recv_sem=recv.at[a, j],
                    device_id=(px, py, c), device_id_type=MESH))
        for cp in recvs:
            cp.wait_recv()
        for cp in sends:
            cp.wait_send()
        for lc in started:
            lc.wait()

    return pl.pallas_call(
        body, name=name, in_specs=[ANY] * n, out_specs=[ANY] * n,
        out_shape=[jax.ShapeDtypeStruct((N_CHIPS,) + a.shape, a.dtype) for a in arrs],
        scratch_shapes=[pltpu.SemaphoreType.DMA((n, 3)), pltpu.SemaphoreType.DMA((n, 3)), pltpu.SemaphoreType.DMA((n,))],
    )(*arrs)


def _scatter_chips(g, name):
    def body(g_ref, out_ref, send, recv, loc):
        x, y, c = _position()
        me = 2 * x + y
        lc = pltpu.make_async_copy(g_ref.at[me], out_ref.at[me], loc)
        lc.start()
        sends, recvs = [], []
        for j, (fx, fy) in enumerate(_CHIP_FLIPS):
            px, py = _flip(x, fx), _flip(y, fy)
            p = 2 * px + py
            cp = pltpu.make_async_remote_copy(src_ref=g_ref.at[p], dst_ref=out_ref.at[me], send_sem=send.at[j],
                                              recv_sem=recv.at[j], device_id=(px, py, c), device_id_type=MESH)
            cp.start()
            sends.append(cp)
            recvs.append(pltpu.make_async_remote_copy(src_ref=g_ref.at[me], dst_ref=out_ref.at[p], send_sem=send.at[j],
                                                      recv_sem=recv.at[j], device_id=(px, py, c), device_id_type=MESH))
        for cp in recvs:
            cp.wait_recv()
        for cp in sends:
            cp.wait_send()
        lc.wait()

    return pl.pallas_call(
        body, name=name, in_specs=[ANY], out_specs=ANY, out_shape=jax.ShapeDtypeStruct(g.shape, g.dtype),
        scratch_shapes=[pltpu.SemaphoreType.DMA((3,)), pltpu.SemaphoreType.DMA((3,)), pltpu.SemaphoreType.DMA(())],
    )(g)


def _swap_sibling(p, name):
    def body(p_ref, out_ref, send, recv):
        x, y, c = _position()
        cp = pltpu.make_async_remote_copy(src_ref=p_ref, dst_ref=out_ref, send_sem=send, recv_sem=recv,
                                          device_id=(x, y, 1 - c), device_id_type=MESH)
        cp.start()
        cp.wait()

    return pl.pallas_call(
        body, name=name, in_specs=[ANY], out_specs=ANY, out_shape=jax.ShapeDtypeStruct(p.shape, p.dtype),
        scratch_shapes=[pltpu.SemaphoreType.DMA(()), pltpu.SemaphoreType.DMA(())],
    )(p)


def _gather_devices(s, name):
    def body(s_ref, out_ref, send, recv, loc):
        x, y, c = _position()
        me = 4 * x + 2 * y + c
        lc = pltpu.make_async_copy(s_ref, out_ref.at[me], loc)
        lc.start()
        sends, recvs = [], []
        for j, (fx, fy, fc) in enumerate(_DEV_FLIPS):
            px, py, pc = _flip(x, fx), _flip(y, fy), _flip(c, fc)
            cp = pltpu.make_async_remote_copy(src_ref=s_ref, dst_ref=out_ref.at[me], send_sem=send.at[j],
                                              recv_sem=recv.at[j], device_id=(px, py, pc), device_id_type=MESH)
            cp.start()
            sends.append(cp)
            recvs.append(pltpu.make_async_remote_copy(
                src_ref=s_ref, dst_ref=out_ref.at[4 * px + 2 * py + pc], send_sem=send.at[j], recv_sem=recv.at[j],
                device_id=(px, py, pc), device_id_type=MESH))
        for cp in recvs:
            cp.wait_recv()
        for cp in sends:
            cp.wait_send()
        lc.wait()

    return pl.pallas_call(
        body, name=name, in_specs=[ANY], out_specs=ANY, out_shape=jax.ShapeDtypeStruct((N_DEV,) + s.shape, s.dtype),
        scratch_shapes=[pltpu.SemaphoreType.DMA((7,)), pltpu.SemaphoreType.DMA((7,)), pltpu.SemaphoreType.DMA(())],
    )(s)


def _sum_slots(a, name):
    s, r, c = a.shape
    tr = _tile(r, 2048, 8)

    def body(a_ref, o_ref):
        acc = a_ref[0]
        for i in range(1, s):
            acc = acc + a_ref[i]
        o_ref[...] = acc

    return pl.pallas_call(body, name=name, grid=(r // tr,), in_specs=[pl.BlockSpec((s, tr, c), lambda i: (0, i, 0))],
                          out_specs=pl.BlockSpec((tr, c), lambda i: (i, 0)),
                          out_shape=jax.ShapeDtypeStruct((r, c), f32))(a)


def _adamw(w, g_parts, m, v, name):
    shape = w.shape
    size = w.size
    view = (size // LANES, LANES) if size % LANES == 0 else (1, size)
    rows = view[0]
    tr = _tile(rows, 2048, 8) if rows > 2048 else rows
    n_g = len(g_parts)

    def body(*refs):
        w_ref = refs[0]
        g_refs = refs[1:1 + n_g]
        m_ref, v_ref, g_out, d_out, m_out, v_out = refs[1 + n_g:]
        g = g_refs[0][...]
        for gr in g_refs[1:]:
            g = g + gr[...]
        m_new = ADAM_B1 * m_ref[...] + (1.0 - ADAM_B1) * g
        v_new = ADAM_B2 * v_ref[...] + (1.0 - ADAM_B2) * (g * g)
        m_hat = m_new / (1.0 - ADAM_B1 ** ADAM_STEP)
        v_hat = v_new / (1.0 - ADAM_B2 ** ADAM_STEP)
        g_out[...] = g
        d_out[...] = -ADAM_LR * (m_hat / (jnp.sqrt(v_hat) + ADAM_EPS) + ADAM_WD * w_ref[...])
        m_out[...] = m_new
        v_out[...] = v_new

    spec = pl.BlockSpec((tr, view[1]), lambda i: (i, 0))
    args = [w.reshape(view)] + [g.reshape(view) for g in g_parts] + [m.reshape(view), v.reshape(view)]
    outs = pl.pallas_call(body, name=name, grid=(rows // tr,), in_specs=[spec] * len(args), out_specs=[spec] * 4,
                          out_shape=[jax.ShapeDtypeStruct(view, f32)] * 4)(*args)
    return [o.reshape(shape) for o in outs]


_BIG = ('ffn1_w_gu', 'ffn1_w_down', 'w_in', 'w_out', 'ffn2_w_gu', 'ffn2_w_down')
_SMALL_SHARDED = ('meta_tokens', 'a_conv_w', 'b_w_up', 'b_a_up', 'b_g_up')
_WEIGHTS = ('meta_tokens', 'ffn1_norm', 'ffn1_w_gu', 'ffn1_w_down', 'mix_norm', 'w_in', 'a_conv_w', 'a_log_rate',
            'a_dt_bias', 'a_out_norm', 'b_shift_mu', 'b_w0', 'b_w_up', 'b_a0', 'b_a_up', 'b_g_up', 'b_k_k', 'b_k_a',
            'b_r_k', 'b_ln_gain', 'b_ln_bias', 'w_out', 'ffn2_norm', 'ffn2_w_gu', 'ffn2_w_down', 'final_norm')
_SMALL = tuple(n for n in _WEIGHTS if n not in _BIG)


def _pack(arrs, dtype, row_mult=8):
    flat = jnp.concatenate([a.reshape(-1).astype(dtype) for a in arrs])
    rows = -(-flat.size // LANES)
    rows = -(-rows // row_mult) * row_mult
    flat = jnp.pad(flat, (0, rows * LANES - flat.size))
    return flat.reshape(rows, LANES)


def _unpack(packed, shapes, lead=()):
    flat = packed.reshape(lead + (-1,))
    out, off = [], 0
    for sh in shapes:
        n = 1
        for d in sh:
            n *= d
        out.append(flat[..., off:off + n].reshape(lead + tuple(sh)))
        off += n
    return out


def _cols_from_shards(s):
    return jnp.concatenate([s[i] for i in range(N_CHIPS)], axis=-1)


def _cols_to_shards(a):
    r, c = a.shape
    return a.reshape(r, N_CHIPS, c // N_CHIPS).transpose(1, 0, 2)


def kernel(x, meta_tokens, ffn1_norm, ffn1_w_gu, ffn1_w_down, mix_norm, w_in, a_conv_w, a_log_rate, a_dt_bias, a_out_norm, b_shift_mu, b_w0, b_w_up, b_a0, b_a_up, b_g_up, b_k_k, b_k_a, b_r_k, b_ln_gain, b_ln_bias, w_out, ffn2_norm, ffn2_w_gu, ffn2_w_down, final_norm, loss_target, m_meta_tokens, m_ffn1_norm, m_ffn1_w_gu, m_ffn1_w_down, m_mix_norm, m_w_in, m_a_conv_w, m_a_log_rate, m_a_dt_bias, m_a_out_norm, m_b_shift_mu, m_b_w0, m_b_w_up, m_b_a0, m_b_a_up, m_b_g_up, m_b_k_k, m_b_k_a, m_b_r_k, m_b_ln_gain, m_b_ln_bias, m_w_out, m_ffn2_norm, m_ffn2_w_gu, m_ffn2_w_down, m_final_norm, v_meta_tokens, v_ffn1_norm, v_ffn1_w_gu, v_ffn1_w_down, v_mix_norm, v_w_in, v_a_conv_w, v_a_log_rate, v_a_dt_bias, v_a_out_norm, v_b_shift_mu, v_b_w0, v_b_w_up, v_b_a0, v_b_a_up, v_b_g_up, v_b_k_k, v_b_k_a, v_b_r_k, v_b_ln_gain, v_b_ln_bias, v_w_out, v_ffn2_norm, v_ffn2_w_gu, v_ffn2_w_down, v_final_norm):
    args = locals()
    wts = {n: args[n] for n in _WEIGHTS}
    mom = {n: args["m_" + n] for n in _WEIGHTS}
    var = {n: args["v_" + n] for n in _WEIGHTS}
    chip = 2 * lax.axis_index("x") + lax.axis_index("y")

    big_shapes = [wts[n].shape[1:] for n in _BIG]
    small_shapes = [wts[n].shape[-2:] for n in _SMALL_SHARDED]
    big_packed = _pack([wts[n] for n in _BIG], bf16)
    small_packed = _pack([wts[n] for n in _SMALL_SHARDED], f32)
    big_all, small_all = _gather_chips([big_packed, small_packed], "gather_weights")
    gu1, dn1, w_in_s, w_out_s, gu2, dn2 = _unpack(big_all, big_shapes, (N_CHIPS,))
    meta_s, conv_s, wup_s, aup_s, gup_s = _unpack(small_all, small_shapes, (N_CHIPS,))
    w = {
        'ffn1_norm': ffn1_norm, 'mix_norm': mix_norm, 'ffn2_norm': ffn2_norm, 'final_norm': final_norm[None, :],
        'ffn1_wg': jnp.concatenate([gu1[0], gu1[1]], axis=1), 'ffn1_wu': jnp.concatenate([gu1[2], gu1[3]], axis=1),
        'ffn1_wd': dn1.reshape(D_FF, D),
        'ffn2_wg': jnp.concatenate([gu2[0], gu2[1]], axis=1), 'ffn2_wu': jnp.concatenate([gu2[2], gu2[3]], axis=1),
        'ffn2_wd': dn2.reshape(D_FF, D),
        'w_in_p': _win_to_padded(_cols_from_shards(w_in_s)), 'w_out': w_out_s.reshape(D, D),
        'a_conv_w': _cols_from_shards(conv_s), 'b_w_up': _cols_from_shards(wup_s), 'b_a_up': _cols_from_shards(aup_s),
        'b_g_up': _cols_from_shards(gup_s),
        'a_log_rate': a_log_rate, 'a_dt_bias': a_dt_bias, 'a_out_norm': a_out_norm, 'b_shift_mu': b_shift_mu,
        'b_w0': b_w0, 'b_a0': b_a0, 'b_k_k': b_k_k, 'b_k_a': b_k_a, 'b_r_k': b_r_k, 'b_ln_gain': b_ln_gain,
        'b_ln_bias': b_ln_bias,
    }
    meta_full = _cols_from_shards(meta_s)

    h0 = jnp.concatenate([jnp.zeros((PAD, D), f32), meta_full, x[0]], axis=0)
    tgt = jnp.concatenate([jnp.zeros((SKIP, D), f32), loss_target[0]], axis=0)
    loss_local, d_h0, g = _local_step(h0, tgt, w)
    loss = lax.psum(loss_local, ("x", "y", "c"))
    grad_x = d_h0[SKIP:][None]

    big_grads = [
        _cols_to_shards(jnp.concatenate([g['ffn1_wg'], g['ffn1_wu']], axis=1)),
        g['ffn1_wd'].reshape(N_CHIPS, D_FF // N_CHIPS, D),
        _cols_to_shards(_win_from_padded(g['w_in_p'])),
        g['w_out'].reshape(N_CHIPS, D // N_CHIPS, D),
        _cols_to_shards(jnp.concatenate([g['ffn2_wg'], g['ffn2_wu']], axis=1)),
        g['ffn2_wd'].reshape(N_CHIPS, D_FF // N_CHIPS, D),
    ]
    g_packed = jnp.concatenate([a.reshape(N_CHIPS, -1, LANES) for a in big_grads], axis=1)
    assert g_packed.shape[1] == big_packed.shape[0]
    mine = _sum_slots(_scatter_chips(g_packed, "scatter_grads"), "sum_chips")
    theirs = _swap_sibling(mine, "swap_sibling")
    mine_parts = _unpack(mine, big_shapes)
    their_parts = _unpack(theirs, big_shapes)

    small_full = {
        'meta_tokens': d_h0[PAD:SKIP], 'ffn1_norm': g['ffn1_norm'], 'mix_norm': g['mix_norm'], 'a_conv_w': g['a_conv_w'],
        'a_log_rate': g['a_log_rate'], 'a_dt_bias': g['a_dt_bias'], 'a_out_norm': g['a_out_norm'],
        'b_shift_mu': g['b_shift_mu'], 'b_w0': g['b_w0'], 'b_w_up': g['b_w_up'], 'b_a0': g['b_a0'], 'b_a_up': g['b_a_up'],
        'b_g_up': g['b_g_up'], 'b_k_k': g['b_k_k'], 'b_k_a': g['b_k_a'], 'b_r_k': g['b_r_k'], 'b_ln_gain': g['b_ln_gain'],
        'b_ln_bias': g['b_ln_bias'], 'ffn2_norm': g['ffn2_norm'], 'final_norm': g['final_norm'],
    }
    s_shapes = [small_full[n].shape for n in _SMALL]
    s_sum = _sum_slots(_gather_devices(_pack([small_full[n] for n in _SMALL], f32, row_mult=256), "gather_small"),
                       "sum_small")
    s_parts = dict(zip(_SMALL, _unpack(s_sum, s_shapes)))

    grad, delta, new_m, new_v = {}, {}, {}, {}
    for n, a, b in zip(_BIG, mine_parts, their_parts):
        sh = wts[n].shape
        grad[n], delta[n], new_m[n], new_v[n] = _adamw(wts[n], [a.reshape(sh), b.reshape(sh)], mom[n], var[n],
                                                       f"adamw_{n}")
    for n in _SMALL:
        gs = s_parts[n]
        if n in _SMALL_SHARDED:
            width = wts[n].shape[-1]
            gs = lax.dynamic_slice_in_dim(gs, chip * width, width, axis=gs.ndim - 1)
        gs = gs.reshape(wts[n].shape)
        grad[n], delta[n], new_m[n], new_v[n] = _adamw(wts[n], [gs], mom[n], var[n], f"adamw_{n}")

    return (loss, grad_x, *[grad[n] for n in _WEIGHTS], *[delta[n] for n in _WEIGHTS],
            *[new_m[n] for n in _WEIGHTS], *[new_v[n] for n in _WEIGHTS])
```

```python
import functools

import jax
import jax.numpy as jnp
from jax import lax
from jax.experimental import pallas as pl
from jax.experimental.pallas import tpu as pltpu

f32 = jnp.float32
bf16 = jnp.bfloat16
HI = lax.Precision.HIGHEST
MESH = pl.DeviceIdType.MESH
ANY = pl.BlockSpec(memory_space=pl.ANY)

D = 1024
N_META = 16
CHUNK = 64
PAD = CHUNK - N_META
SKIP = PAD + N_META
EPS = 1e-6
D_FF = 2816
A_HEADS = 8
A_DK = 128
B_HEADS = 16
B_N = 64
B_GN_EPS = B_N * 1e-5
W_LORA, AA_LORA, G_LORA = 64, 64, 160
IN_TOTAL = 9520
ZP = 9600
LANES = 128
N_CHIPS = 4
N_DEV = 8

ADAM_LR, ADAM_B1, ADAM_B2, ADAM_EPS, ADAM_WD, ADAM_STEP = 0.001, 0.9, 0.999, 1e-08, 0.01, 10

MXU_DTYPE = bf16


def _tile(n, cap, mult):
    if n <= cap:
        return n
    best = None
    for t in range(mult, cap + 1, mult):
        if n % t == 0:
            best = t
    assert best is not None, (n, cap, mult)
    return best


def _sigmoid(x):
    return jax.nn.sigmoid(x)


def _silu(x):
    return x * jax.nn.sigmoid(x)


def _softplus(x):
    return jnp.maximum(x, 0.0) + jnp.log(1.0 + jnp.exp(-jnp.abs(x)))


def _mmf(a, b):
    return jnp.dot(a, b, precision=HI, preferred_element_type=f32)


def _mm_nt(a, b):
    return lax.dot_general(a, b, (((1,), (1,)), ((), ())), precision=HI, preferred_element_type=f32)


def _mm_tn(a, b):
    return lax.dot_general(a, b, (((0,), (0,)), ((), ())), precision=HI, preferred_element_type=f32)


def _head_matrix(c, nh):
    hd = c // nh
    r = lax.broadcasted_iota(jnp.int32, (c, nh), 0)
    h = lax.broadcasted_iota(jnp.int32, (c, nh), 1)
    return ((r >= h * hd) & (r < (h + 1) * hd)).astype(f32)


def _head_sum(x, nh):
    e = _head_matrix(x.shape[-1], nh)
    return _mm_nt(_mmf(x, e), e)


@functools.partial(jax.custom_vjp, nondiff_argnums=(1,))
def _shift_rows(x, s):
    n = x.shape[0]
    row = lax.broadcasted_iota(jnp.int32, x.shape, 0)
    if s > 0:
        return jnp.where(row >= s, pltpu.roll(x, s, 0), 0.0)
    return jnp.where(row < n + s, pltpu.roll(x, n + s, 0), 0.0)


def _shift_rows_fwd(x, s):
    return _shift_rows(x, s), None


def _shift_rows_bwd(s, _, g):
    return (_shift_rows(g, -s),)


_shift_rows.defvjp(_shift_rows_fwd, _shift_rows_bwd)


def _matmul(a, b, *, ta=False, tb=False, res=None, scale=1.0, name):
    assert not (ta and tb)
    (ar, ac), (br, bc) = a.shape, b.shape
    m, k = (ac, ar) if ta else (ar, ac)
    n, kb = (br, bc) if tb else (bc, br)
    assert k == kb, (a.shape, b.shape, ta, tb)
    tm = _tile(m, 1408, LANES) if ta else _tile(m, 832, 8)
    tn = _tile(n, 1408, LANES)
    tk = _tile(k, 1040, 8) if ta else _tile(k, 1408, LANES)
    nk = k // tk
    dn = (((0 if ta else 1,), (1 if tb else 0,)), ((), ()))

    def body(*refs):
        if res is not None:
            a_ref, b_ref, r_ref, o_ref, acc = refs
        else:
            a_ref, b_ref, o_ref, acc = refs
        kk = pl.program_id(2)

        @pl.when(kk == 0)
        def _():
            acc[...] = jnp.zeros_like(acc)

        acc[...] += lax.dot_general(a_ref[...].astype(MXU_DTYPE), b_ref[...].astype(MXU_DTYPE), dn,
                                    preferred_element_type=f32,
                                    precision=None if MXU_DTYPE == bf16 else HI)

        @pl.when(kk == nk - 1)
        def _():
            out = acc[...]
            if scale != 1.0:
                out = out * scale
            if res is not None:
                out = r_ref[...] + out
            o_ref[...] = out

    if ta:
        a_spec = pl.BlockSpec((tk, tm), lambda i, j, kk: (kk, i))
    else:
        a_spec = pl.BlockSpec((tm, tk), lambda i, j, kk: (i, kk))
    if tb:
        b_spec = pl.BlockSpec((tn, tk), lambda i, j, kk: (j, kk))
    else:
        b_spec = pl.BlockSpec((tk, tn), lambda i, j, kk: (kk, j))
    in_specs = [a_spec, b_spec]
    args = [a, b]
    if res is not None:
        in_specs.append(pl.BlockSpec((tm, tn), lambda i, j, kk: (i, j)))
        args.append(res)
    return pl.pallas_call(
        body, name=name, grid=(m // tm, n // tn, nk), in_specs=in_specs,
        out_specs=pl.BlockSpec((tm, tn), lambda i, j, kk: (i, j)),
        out_shape=jax.ShapeDtypeStruct((m, n), f32),
        scratch_shapes=[pltpu.VMEM((tm, tn), f32)],
        compiler_params=pltpu.CompilerParams(dimension_semantics=("parallel", "parallel", "arbitrary")),
    )(*args)


def _tw_fwd(fn, ins, in_specs, out_shapes, out_specs, grid, name, with_pid=False):
    n_in = len(ins)

    def body(*refs):
        vals = [r[...] for r in refs[:n_in]]
        outs = fn(pl.program_id(0), *vals) if with_pid else fn(*vals)
        for r, o in zip(refs[n_in:], outs):
            r[...] = o

    return pl.pallas_call(body, name=name, grid=grid, in_specs=in_specs, out_specs=out_specs,
                          out_shape=out_shapes)(*ins)


def _tw_bwd(fn, ins, in_specs, cts, ct_specs, kinds, grid, name, with_pid=False):
    n_in, n_ct = len(ins), len(cts)
    diff = [i for i, kd in enumerate(kinds) if kd is not None]

    def body(*refs):
        vals = [r[...] for r in refs[:n_in]]
        ctv = tuple(r[...] for r in refs[n_in:n_in + n_ct])
        g_refs = refs[n_in + n_ct:]
        pid = pl.program_id(0)

        def f(*dv):
            full = list(vals)
            for i, v in zip(diff, dv):
                full[i] = v
            out = fn(pid, *full) if with_pid else fn(*full)
            return tuple(out)

        _, vjp = jax.vjp(f, *[vals[i] for i in diff])
        gs = vjp(ctv)
        first = pid == 0
        for i2 in range(1, len(grid)):
            first = first & (pl.program_id(i2) == 0)
        for i, g, g_ref in zip(diff, gs, g_refs):
            if kinds[i] != 'acc':
                g_ref[...] = g
            else:
                @pl.when(first)
                def _(g=g, g_ref=g_ref):
                    g_ref[...] = g

                @pl.when(jnp.logical_not(first))
                def _(g=g, g_ref=g_ref):
                    g_ref[...] += g

    zero_map = {1: lambda *a: (0,), 2: lambda *a: (0, 0), 3: lambda *a: (0, 0, 0)}
    out_specs, out_shapes = [], []
    for i in diff:
        if kinds[i] == 'tile':
            out_shapes.append(jax.ShapeDtypeStruct(ins[i].shape, f32))
            out_specs.append(in_specs[i])
        elif kinds[i] == 'acc':
            out_shapes.append(jax.ShapeDtypeStruct(ins[i].shape, f32))
            out_specs.append(pl.BlockSpec(ins[i].shape, zero_map[ins[i].ndim]))
        else:
            out_shapes.append(jax.ShapeDtypeStruct(kinds[i][1], f32))
            out_specs.append(kinds[i][2])
    return pl.pallas_call(body, name=name, grid=grid, in_specs=list(in_specs) + list(ct_specs),
                          out_specs=out_specs, out_shape=out_shapes)(*ins, *cts)


def _row_spec(tm, c, col_block=0):
    return pl.BlockSpec((tm, c), lambda i, cb=col_block: (i, cb))


def _full_spec(shape):
    nd = len(shape)
    return pl.BlockSpec(shape, lambda *a, nd=nd: (0,) * nd)


def _f_rms(x, g):
    return (x * lax.rsqrt(jnp.mean(x * x, axis=-1, keepdims=True) + EPS) * g,)


def _f_swiglu(gate, up):
    return (_silu(gate) * up,)


def _f_loss(pid, h, g, tgt, *, tm):
    y = h * lax.rsqrt(jnp.mean(h * h, axis=-1, keepdims=True) + EPS) * g
    row = pid * tm + lax.broadcasted_iota(jnp.int32, (tm, 1), 0)
    err = jnp.where(row >= SKIP, y - tgt, 0.0)
    per_row = jnp.mean(err * err, axis=-1, keepdims=True)
    return (0.5 * jnp.sum(per_row, axis=0, keepdims=True),)


def _f_conv(x, w, *, norm, scale):
    y = x * w[3:4, :]
    for s in (1, 2, 3):
        y = y + _shift_rows(x, s) * w[3 - s:4 - s, :]
    y = _silu(y)
    if norm:
        y = y * lax.rsqrt(jnp.sum(y * y, axis=-1, keepdims=True) + 1e-6) * scale
    return (y,)


def _f_dgates(pid, abeta, aalpha, log_rate, dt_bias, *, tm):
    row = pid * tm + lax.broadcasted_iota(jnp.int32, (tm, 1), 0)
    live = row >= PAD
    beta = jnp.where(live, _sigmoid(abeta), 0.0)
    g = jnp.where(live, -jnp.exp(log_rate) * _softplus(aalpha + dt_bias), 0.0)
    return beta, g


def _f_tshift(z, mu):
    return (z + (_shift_rows(z, 1) - z) * mu,)


def _f_rwkv_pre(k, wd, ad, gd, w0, w_up, a0, a_up, g_up, k_k, k_a):
    w_log = -_softplus(-(w0 + _mmf(jnp.tanh(wd), w_up))) - 0.5
    lw = -jnp.exp(w_log)
    a_lr = _sigmoid(a0 + _mmf(ad, a_up))
    gate = _mmf(_sigmoid(gd), g_up)
    kkp = k * k_k
    kk = kkp * lax.rsqrt(_head_sum(kkp * kkp, B_HEADS) + 1e-6)
    kmod = k * (1.0 + (a_lr - 1.0) * k_a)
    return lw, kmod, -kk, kk * a_lr, gate


def _f_mix_post(o, az, y, r, kmod, v, gate, ga, gb, out_gain, ln_g, ln_b, r_k):
    ms = _head_sum(o * o, A_HEADS) * (1.0 / A_DK)
    oa = o * lax.rsqrt(ms + EPS) * out_gain * _silu(az)
    mean = _head_sum(y, B_HEADS) * (1.0 / B_N)
    yc = y - mean
    var = _head_sum(yc * yc, B_HEADS) * (1.0 / B_N)
    yn = yc * lax.rsqrt(var + B_GN_EPS) * ln_g + ln_b
    bonus = _head_sum(r * kmod * r_k, B_HEADS) * v
    ob = (yn + bonus) * gate
    return (_sigmoid(ga) * oa + _sigmoid(gb) * ob,)


SCAN_PASSES = 3


def _split2(a):
    hi = a.astype(bf16)
    return hi, (a - hi.astype(f32)).astype(bf16)


def _dot_passes(a, b, ca, cb):
    dn = (((ca,), (cb,)), ((), ()))
    if SCAN_PASSES == 0:
        return lax.dot_general(a, b, dn, precision=HI, preferred_element_type=f32)
    if SCAN_PASSES == 1:
        return lax.dot_general(a.astype(bf16), b.astype(bf16), dn, preferred_element_type=f32)
    ah, al = _split2(a)
    bh, bl = _split2(b)
    return (lax.dot_general(ah, bh, dn, preferred_element_type=f32)
            + (lax.dot_general(ah, bl, dn, preferred_element_type=f32)
               + lax.dot_general(al, bh, dn, preferred_element_type=f32)))


@functools.partial(jax.custom_vjp, nondiff_argnums=(2, 3))
def _sdot(a, b, ca, cb):
    return _dot_passes(a, b, ca, cb)


def _sdot_fwd(a, b, ca, cb):
    return _dot_passes(a, b, ca, cb), (a, b)


def _sdot_bwd(ca, cb, res, g):
    a, b = res
    if (ca, cb) == (1, 0):
        return _dot_passes(g, b, 1, 1), _dot_passes(a, g, 0, 0)
    if (ca, cb) == (1, 1):
        return _dot_passes(g, b, 1, 0), _dot_passes(g, a, 0, 0)
    assert (ca, cb) == (0, 0)
    return _dot_passes(b, g, 1, 1), _dot_passes(a, g, 1, 0)


_sdot.defvjp(_sdot_fwd, _sdot_bwd)


def _smm(a, b):
    return _sdot(a, b, 1, 0)


def _smm_nt(a, b):
    return _sdot(a, b, 1, 1)


def _smm_tn(a, b):
    return _sdot(a, b, 0, 0)


def _tri_dot(x, ca):
    n = x.shape[0]
    incl = _tri_masks(n)[0]
    dn = (((ca,), (0,)), ((), ()))
    if SCAN_PASSES == 0:
        return lax.dot_general(incl.astype(f32), x, dn, precision=HI, preferred_element_type=f32)
    tri = incl.astype(bf16)
    hi, r1 = x.astype(bf16), None
    r1 = x - hi.astype(f32)
    mid = r1.astype(bf16)
    lo = (r1 - mid.astype(f32)).astype(bf16)
    return (lax.dot_general(tri, hi, dn, preferred_element_type=f32)
            + (lax.dot_general(tri, mid, dn, preferred_element_type=f32)
               + lax.dot_general(tri, lo, dn, preferred_element_type=f32)))


@jax.custom_vjp
def _cumsum_rows(x):
    return _tri_dot(x, 1)


def _cumsum_rows_fwd(x):
    return _tri_dot(x, 1), None


def _cumsum_rows_bwd(_, g):
    return (_tri_dot(g, 0),)


_cumsum_rows.defvjp(_cumsum_rows_fwd, _cumsum_rows_bwd)


def _tri_masks(n):
    i = lax.broadcasted_iota(jnp.int32, (n, n), 0)
    j = lax.broadcasted_iota(jnp.int32, (n, n), 1)
    return i >= j, i > j, i == j, i <= j


def _unit_lower_inv(low):
    n = low.shape[0]
    assert n == CHUNK
    _, _, eye, _ = _tri_masks(n)
    acc = eye.astype(f32) + low
    p = low
    for _ in range(5):
        p = _smm(p, p)
        acc = acc + _smm(acc, p)
    return acc


def _delta_chunk(s, q, k, v, beta, g):
    incl, strict, eye, upper = _tri_masks(CHUNK)
    g_row = jnp.sum(jnp.where(eye, g, 0.0), axis=0, keepdims=True)
    gc = jnp.sum(jnp.where(incl, g_row, 0.0), axis=1, keepdims=True)
    gc_row = jnp.sum(jnp.where(upper, g, 0.0), axis=0, keepdims=True)
    decay = jnp.where(incl, jnp.exp(jnp.where(incl, gc - gc_row, 0.0)), 0.0)
    kb = k * beta
    vb = v * beta
    m = jnp.where(strict, _smm_nt(kb, k) * decay, 0.0)
    tinv = _unit_lower_inv(-m)
    u = _smm(tinv, vb)
    wk = _smm(tinv, kb * jnp.exp(gc))
    attn = _smm_nt(q, k) * decay
    qg = q * jnp.exp(gc)
    g_last = jnp.sum(g, axis=0, keepdims=True)
    k_tail = k * jnp.exp(g_last - gc)
    v_new = u - _smm(wk, s)
    o = _smm(qg, s) + _smm(attn, v_new)
    s_new = s * jnp.exp(g_last) + _smm_tn(k_tail, v_new)
    return o, s_new


def _rwkv_chunk(st, r, k, v, a, b, lw):
    c = CHUNK
    incl, strict, _, _ = _tri_masks(c)
    lane = lax.broadcasted_iota(jnp.int32, (c, 2 * B_N), 1)
    first = lane < B_N
    bi = lax.broadcasted_iota(jnp.int32, (2 * B_N, 2 * B_N), 0) < B_N
    bj = lax.broadcasted_iota(jnp.int32, (2 * B_N, 2 * B_N), 1) < B_N
    blockdiag = bi == bj
    cum = _cumsum_rows(lw)
    e_pos = jnp.exp(cum)
    e_neg = jnp.exp(-cum)
    rt = r * e_pos
    at = a * jnp.exp(cum - lw)
    kt = k * e_neg
    bt = b * e_neg
    a_s0 = _smm_nt(at, st)
    r_s0 = _smm_nt(rt, st)
    u = jnp.zeros((c, 2 * B_N), f32)
    for sel in (first, jnp.logical_not(first)):
        at_h = jnp.where(sel, at, 0.0)
        ab = jnp.where(strict, _smm_nt(at_h, bt), 0.0)
        ak = jnp.where(strict, _smm_nt(at_h, kt), 0.0)
        t_h = _unit_lower_inv(ab)
        u_h = _smm(t_h, jnp.where(sel, a_s0, 0.0) + _smm(ak, jnp.where(sel, v, 0.0)))
        u = u + u_h
    y = r_s0
    for sel in (first, jnp.logical_not(first)):
        rt_h = jnp.where(sel, rt, 0.0)
        rb = jnp.where(incl, _smm_nt(rt_h, bt), 0.0)
        rk = jnp.where(incl, _smm_nt(rt_h, kt), 0.0)
        y = y + _smm(rb, jnp.where(sel, u, 0.0)) + _smm(rk, jnp.where(sel, v, 0.0))
    cl = jnp.sum(lw, axis=0, keepdims=True)
    dec = jnp.exp(cl - cum)
    st_new = st * jnp.exp(cl) + jnp.where(blockdiag, _smm_tn(u, b * dec) + _smm_tn(v, k * dec), 0.0)
    return y, st_new


GROUPS_PER_STEP = 2


def _scan_specs(ins, col_offs, n_chunks, reverse):
    gw = GROUPS_PER_STEP * LANES
    cidx = (lambda c: n_chunks - 1 - c) if reverse else (lambda c: c)
    specs = []
    for a, off in zip(ins, col_offs):
        if a.ndim == 2:
            assert off % gw == 0
            specs.append(pl.BlockSpec((CHUNK, gw), lambda h, c, o=off // gw: (cidx(c), h + o)))
        else:
            specs.append(pl.BlockSpec((GROUPS_PER_STEP, CHUNK, 1), lambda h, c: (h, cidx(c), 0)))
    return specs, cidx


def _group_vals(refs, g):
    return [r[:, g * LANES:(g + 1) * LANES] if len(r.shape) == 2 else r[g] for r in refs]


def _scan_fwd(chunk_fn, ins, col_offs, n_groups, n_chunks, state_shape, name):
    n_in = len(ins)
    gps = GROUPS_PER_STEP
    t = ins[0].shape[0]

    def body(*refs):
        in_refs = refs[:n_in]
        o_ref, s0_ref, st = refs[n_in:]

        @pl.when(pl.program_id(1) == 0)
        def _():
            st[...] = jnp.zeros_like(st)

        for g in range(gps):
            s = st[g]
            s0_ref[g] = s
            o, s_new = chunk_fn(s, *_group_vals(in_refs, g))
            o_ref[:, g * LANES:(g + 1) * LANES] = o
            st[g] = s_new

    specs, _ = _scan_specs(ins, col_offs, n_chunks, False)
    return pl.pallas_call(
        body, name=name, grid=(n_groups // gps, n_chunks), in_specs=specs,
        out_specs=[pl.BlockSpec((CHUNK, gps * LANES), lambda h, c: (c, h)),
                   pl.BlockSpec((gps, None) + state_shape, lambda h, c: (h, c, 0, 0))],
        out_shape=[jax.ShapeDtypeStruct((t, n_groups * LANES), f32),
                   jax.ShapeDtypeStruct((n_groups, n_chunks) + state_shape, f32)],
        scratch_shapes=[pltpu.VMEM((gps,) + state_shape, f32)],
        compiler_params=pltpu.CompilerParams(dimension_semantics=("parallel", "arbitrary")),
    )(*ins)


def _scan_bwd(chunk_fn, s0s, ins, col_offs, d_out, n_groups, n_chunks, state_shape, name):
    n_in = len(ins)
    gps = GROUPS_PER_STEP
    t = d_out.shape[0]

    def body(*refs):
        s0_ref = refs[0]
        in_refs = refs[1:1 + n_in]
        do_ref = refs[1 + n_in]
        g_refs = refs[2 + n_in:2 + 2 * n_in]
        dst = refs[2 + 2 * n_in]

        @pl.when(pl.program_id(1) == 0)
        def _():
            dst[...] = jnp.zeros_like(dst)

        for g in range(gps):
            _, vjp = jax.vjp(chunk_fn, s0_ref[g], *_group_vals(in_refs, g))
            gs = vjp((do_ref[:, g * LANES:(g + 1) * LANES], dst[g]))
            dst[g] = gs[0]
            for g_ref, gv in zip(g_refs, gs[1:]):
                if len(g_ref.shape) == 2:
                    g_ref[:, g * LANES:(g + 1) * LANES] = gv
                else:
                    g_ref[g] = gv

    specs, cidx = _scan_specs(ins, col_offs, n_chunks, True)
    out_lane = pl.BlockSpec((CHUNK, gps * LANES), lambda h, c: (cidx(c), h))
    g_specs = [out_lane if a.ndim == 2 else sp for a, sp in zip(ins, specs)]
    g_shapes = [(t, n_groups * LANES) if a.ndim == 2 else a.shape for a in ins]
    s0_spec = pl.BlockSpec((gps, None) + state_shape, lambda h, c: (h, cidx(c), 0, 0))
    return pl.pallas_call(
        body, name=name, grid=(n_groups // gps, n_chunks), in_specs=[s0_spec] + specs + [out_lane],
        out_specs=g_specs, out_shape=[jax.ShapeDtypeStruct(sh, f32) for sh in g_shapes],
        scratch_shapes=[pltpu.VMEM((gps,) + state_shape, f32)],
        compiler_params=pltpu.CompilerParams(dimension_semantics=("parallel", "arbitrary")),
    )(s0s, *ins, d_out)


def _rms_fwd(x, g, name):
    t = x.shape[0]
    tm = _tile(t, 416, 8)
    return _tw_fwd(_f_rms, [x, g], [_row_spec(tm, D), _full_spec(g.shape)],
                   [jax.ShapeDtypeStruct(x.shape, f32)], [_row_spec(tm, D)], (t // tm,), name)[0]


def _rms_bwd(x, g, dy, name):
    t = x.shape[0]
    tm = _tile(t, 416, 8)
    return _tw_bwd(_f_rms, [x, g], [_row_spec(tm, D), _full_spec(g.shape)], [dy], [_row_spec(tm, D)],
                   ['tile', 'acc'], (t // tm,), name)


def _ffn_fwd(h, gain, wg, wu, wd, tag):
    xn = _rms_fwd(h, gain, f"{tag}_rms")
    gate = _matmul(xn, wg, name=f"{tag}_gate")
    up = _matmul(xn, wu, name=f"{tag}_up")
    t = h.shape[0]
    tm = _tile(t, 208, 8)
    act = _tw_fwd(_f_swiglu, [gate, up], [_row_spec(tm, D_FF)] * 2, [jax.ShapeDtypeStruct((t, D_FF), f32)],
                  [_row_spec(tm, D_FF)], (t // tm,), f"{tag}_act")[0]
    out = _matmul(act, wd, res=h, scale=0.5, name=f"{tag}_down")
    return out, (xn, gate, up, act)


def _ffn_bwd(h, gain, wg, wu, wd, saved, dout, tag):
    xn, gate, up, act = saved
    t = h.shape[0]
    d_wd = _matmul(act, dout, ta=True, scale=0.5, name=f"{tag}_dwd")
    d_act = _matmul(dout, wd, tb=True, scale=0.5, name=f"{tag}_dact")
    tm = _tile(t, 208, 8)
    d_gate, d_up = _tw_bwd(_f_swiglu, [gate, up], [_row_spec(tm, D_FF)] * 2, [d_act], [_row_spec(tm, D_FF)],
                           ['tile', 'tile'], (t // tm,), f"{tag}_dactf")
    d_wg = _matmul(xn, d_gate, ta=True, name=f"{tag}_dwg")
    d_wu = _matmul(xn, d_up, ta=True, name=f"{tag}_dwu")
    d_xn = _matmul(d_gate, wg, tb=True, name=f"{tag}_dxn_g")
    d_xn = _matmul(d_up, wu, tb=True, res=d_xn, name=f"{tag}_dxn_u")
    d_hn, d_gain = _rms_bwd(h, gain, d_xn, f"{tag}_drms")
    return d_hn, d_gain, d_wg, d_wu, d_wd


def _col_spec(t, first_block):
    return pl.BlockSpec((t, LANES), lambda j, fb=first_block: (0, j + fb))


def _local_step(h0, tgt, w):
    t = h0.shape[0]
    assert t % CHUNK == 0
    nc = t // CHUNK
    grads = {}

    h1, ffn1_saved = _ffn_fwd(h0, w['ffn1_norm'], w['ffn1_wg'], w['ffn1_wu'], w['ffn1_wd'], "ffn1")
    u = _rms_fwd(h1, w['mix_norm'], "mix_rms")
    z = _matmul(u, w['w_in_p'], name="in_proj")
    zs = z[:, 9216:9216 + 304]
    abeta, aalpha = zs[:, 288:296], zs[:, 296:304]

    conv_w = w['a_conv_w']
    conv_fns = [functools.partial(_f_conv, norm=True, scale=A_DK ** -0.5),
                functools.partial(_f_conv, norm=True, scale=1.0),
                functools.partial(_f_conv, norm=False, scale=1.0)]
    qkv = []
    for idx, fn in enumerate(conv_fns):
        qkv.append(_tw_fwd(fn, [z, conv_w], [_col_spec(t, 8 * idx), pl.BlockSpec((4, LANES), lambda j, o=8 * idx: (0, j + o))],
                           [jax.ShapeDtypeStruct((t, D), f32)], [_col_spec(t, 0)], (A_HEADS,), f"a_conv{idx}")[0])
    aq, ak, av = qkv
    tmg = _tile(t, 1040, 8)
    dg_fn = functools.partial(_f_dgates, tm=tmg)
    dg_specs = [_row_spec(tmg, A_HEADS)] * 2 + [_full_spec((1, A_HEADS))] * 2
    beta, gdec = _tw_fwd(dg_fn, [abeta, aalpha, w['a_log_rate'], w['a_dt_bias']], dg_specs,
                         [jax.ShapeDtypeStruct((t, A_HEADS), f32)] * 2, [_row_spec(tmg, A_HEADS)] * 2, (t // tmg,),
                         "a_gates", with_pid=True)
    beta_h = beta.T[:, :, None]
    gdec_h = gdec.T[:, :, None]
    a_ins = [aq, ak, av, beta_h, gdec_h]
    a_offs = [0] * 5
    o_scan, a_s0 = _scan_fwd(_delta_chunk, a_ins, a_offs, A_HEADS, nc, (A_DK, A_DK), "a_scan")

    mu = w['b_shift_mu']
    mu_rkv, mu_s = mu[:, :3072], mu[:, 3072:]
    zf_rkv = _tw_fwd(_f_tshift, [z, mu_rkv], [_col_spec(t, 32), pl.BlockSpec((1, LANES), lambda j: (0, j))],
                     [jax.ShapeDtypeStruct((t, 3072), f32)], [_col_spec(t, 0)], (24,), "b_shift")[0]
    zs_b = zs[:, :288]
    zf_s = _tw_fwd(_f_tshift, [zs_b, mu_s], [_full_spec((t, 288)), _full_spec((1, 288))],
                   [jax.ShapeDtypeStruct((t, 288), f32)], [_full_spec((t, 288))], (1,), "b_shift_s")[0]
    wdf, adf, gdf = zf_s[:, 0:64], zf_s[:, 64:128], zf_s[:, 128:288]
    tmr = _tile(t, 160, 8)
    pre_params = [w['b_w0'], w['b_w_up'], w['b_a0'], w['b_a_up'], w['b_g_up'], w['b_k_k'], w['b_k_a']]
    pre_ins = [zf_rkv, wdf, adf, gdf] + pre_params
    pre_specs = ([_row_spec(tmr, D, 1), _row_spec(tmr, 64), _row_spec(tmr, 64), _row_spec(tmr, 160)]
                 + [_full_spec(p.shape) for p in pre_params])
    lw, kmod, a_s, b_s, bgate = _tw_fwd(_f_rwkv_pre, pre_ins, pre_specs, [jax.ShapeDtypeStruct((t, D), f32)] * 5,
                                        [_row_spec(tmr, D)] * 5, (t // tmr,), "b_pre")
    b_ins = [zf_rkv, kmod, zf_rkv, a_s, b_s, lw]
    b_offs = [0, 0, 2 * D, 0, 0, 0]
    y_scan, b_s0 = _scan_fwd(_rwkv_chunk, b_ins, b_offs, B_HEADS // 2, nc, (2 * B_N, 2 * B_N), "b_scan")

    out_gain_t = jnp.tile(w['a_out_norm'], (1, A_HEADS))
    r_k = w['b_r_k'].reshape(1, D)
    post_params = [out_gain_t, w['b_ln_gain'], w['b_ln_bias'], r_k]
    post_ins = [o_scan, z, y_scan, zf_rkv, kmod, zf_rkv, bgate, z, z] + post_params
    post_specs = ([_row_spec(tmr, D), _row_spec(tmr, D, 3), _row_spec(tmr, D), _row_spec(tmr, D, 0), _row_spec(tmr, D),
                   _row_spec(tmr, D, 2), _row_spec(tmr, D), _row_spec(tmr, D, 7), _row_spec(tmr, D, 8)]
                  + [_full_spec((1, D))] * 4)
    merged = _tw_fwd(_f_mix_post, post_ins, post_specs, [jax.ShapeDtypeStruct((t, D), f32)], [_row_spec(tmr, D)],
                     (t // tmr,), "mix_post")[0]
    h2 = _matmul(merged, w['w_out'], res=h1, name="out_proj")
    h3, ffn2_saved = _ffn_fwd(h2, w['ffn2_norm'], w['ffn2_wg'], w['ffn2_wu'], w['ffn2_wd'], "ffn2")

    tml = _tile(t, 416, 8)
    fnorm = w['final_norm']
    loss_fn = functools.partial(_f_loss, tm=tml)
    loss_specs = [_row_spec(tml, D), _full_spec((1, D)), _row_spec(tml, D)]
    loss_parts = _tw_fwd(loss_fn, [h3, fnorm, tgt], loss_specs, [jax.ShapeDtypeStruct((t // tml, 1, 1), f32)],
                         [pl.BlockSpec((None, 1, 1), lambda i: (i, 0, 0))], (t // tml,), "loss", with_pid=True)[0]
    loss = jnp.sum(loss_parts)
    ones = jnp.ones((t // tml, 1, 1), f32)
    d_h3, grads['final_norm'] = _tw_bwd(loss_fn, [h3, fnorm, tgt], loss_specs, [ones],
                                        [pl.BlockSpec((None, 1, 1), lambda i: (i, 0, 0))], ['tile', 'acc', None],
                                        (t // tml,), "loss_bwd", with_pid=True)

    d_hn, grads['ffn2_norm'], grads['ffn2_wg'], grads['ffn2_wu'], grads['ffn2_wd'] = _ffn_bwd(
        h2, w['ffn2_norm'], w['ffn2_wg'], w['ffn2_wu'], w['ffn2_wd'], ffn2_saved, d_h3, "ffn2")
    d_h2 = _add(d_h3, d_hn, "add_h2")
    grads['w_out'] = _matmul(merged, d_h2, ta=True, name="d_w_out")
    d_merged = _matmul(d_h2, w['w_out'], tb=True, name="d_merged")

    win = ('tile', (t, D), _row_spec(tmr, D))
    post_kinds = ['tile', win, 'tile', win, 'tile', win, 'tile', win, win] + ['acc'] * 4
    (d_o, d_az, d_y, d_r1, d_kmod1, d_v1, d_bgate, d_ga, d_gb,
     d_out_gain_t, grads['b_ln_gain'], grads['b_ln_bias'], d_r_k) = _tw_bwd(
        _f_mix_post, post_ins, post_specs, [d_merged], [_row_spec(tmr, D)], post_kinds, (t // tmr,), "mix_post_bwd")
    grads['a_out_norm'] = jnp.sum(d_out_gain_t.reshape(A_HEADS, A_DK), axis=0, keepdims=True)
    grads['b_r_k'] = d_r_k.reshape(1, B_HEADS, B_N)

    d_r2, d_kmod2, d_v2, d_as, d_bs, d_lw = _scan_bwd(_rwkv_chunk, b_s0, b_ins, b_offs, d_y, B_HEADS // 2, nc,
                                                      (2 * B_N, 2 * B_N), "b_scan_bwd")
    d_kmod = _add(d_kmod1, d_kmod2, "add_kmod")
    pre_kinds = [win] + ['tile'] * 3 + ['acc'] * 7
    pre_ct_specs = [_row_spec(tmr, D)] * 5
    (d_zf_k, d_wdf, d_adf, d_gdf, grads['b_w0'], grads['b_w_up'], grads['b_a0'], grads['b_a_up'], grads['b_g_up'],
     grads['b_k_k'], grads['b_k_a']) = _tw_bwd(
        _f_rwkv_pre, pre_ins, pre_specs, [d_lw, d_kmod, d_as, d_bs, d_bgate], pre_ct_specs, pre_kinds, (t // tmr,),
        "b_pre_bwd")
    d_zf_rkv = _assemble3(d_r1, d_r2, d_zf_k, d_v1, d_v2, "b_dzf")
    d_zb_rkv, d_mu_rkv = _tw_bwd(_f_tshift, [z, mu_rkv], [_col_spec(t, 32), pl.BlockSpec((1, LANES), lambda j: (0, j))],
                                 [d_zf_rkv], [_col_spec(t, 0)], [('tile', (t, 3072), _col_spec(t, 0)), 'tile'], (24,),
                                 "b_shift_bwd")
    d_zf_s = jnp.concatenate([d_wdf, d_adf, d_gdf], axis=1)
    d_zs_b, d_mu_s = _tw_bwd(_f_tshift, [zs_b, mu_s], [_full_spec((t, 288)), _full_spec((1, 288))], [d_zf_s],
                             [_full_spec((t, 288))], ['tile', 'tile'], (1,), "b_shift_s_bwd")
    grads['b_shift_mu'] = jnp.concatenate([d_mu_rkv, d_mu_s], axis=1)

    d_aq, d_ak, d_av, d_beta_h, d_g_h = _scan_bwd(_delta_chunk, a_s0, a_ins, a_offs, d_o, A_HEADS, nc, (A_DK, A_DK),
                                                  "a_scan_bwd")
    d_beta = d_beta_h[:, :, 0].T
    d_gdec = d_g_h[:, :, 0].T
    d_abeta, d_aalpha, grads['a_log_rate'], grads['a_dt_bias'] = _tw_bwd(
        dg_fn, [abeta, aalpha, w['a_log_rate'], w['a_dt_bias']], dg_specs, [d_beta, d_gdec],
        [_row_spec(tmg, A_HEADS)] * 2, ['tile', 'tile', 'acc', 'acc'], (t // tmg,), "a_gates_bwd", with_pid=True)
    d_zqkv, d_conv = [], []
    for idx, (fn, ct) in enumerate(zip(conv_fns, (d_aq, d_ak, d_av))):
        dz_i, dw_i = _conv_bwd(fn, z, conv_w, ct, idx, t)
        d_zqkv.append(dz_i)
        d_conv.append(dw_i)
    grads['a_conv_w'] = jnp.concatenate(d_conv, axis=1)

    d_z = jnp.concatenate(
        d_zqkv + [d_az, d_zb_rkv, d_ga, d_gb, d_zs_b, d_abeta, d_aalpha, jnp.zeros((t, ZP - 9216 - 304), f32)], axis=1)
    grads['w_in_p'] = _matmul(u, d_z, ta=True, name="d_w_in")
    d_u = _matmul(d_z, w['w_in_p'], tb=True, name="d_u")
    d_h1n, grads['mix_norm'] = _rms_bwd(h1, w['mix_norm'], d_u, "mix_drms")
    d_h1 = _add(d_h2, d_h1n, "add_h1")
    d_h0n, grads['ffn1_norm'], grads['ffn1_wg'], grads['ffn1_wu'], grads['ffn1_wd'] = _ffn_bwd(
        h0, w['ffn1_norm'], w['ffn1_wg'], w['ffn1_wu'], w['ffn1_wd'], ffn1_saved, d_h1, "ffn1")
    d_h0 = _add(d_h1, d_h0n, "add_h0")
    return loss, d_h0, grads


_WIN_SEGMENTS = ((0, 4096), (4112, 7184), (7472, 9520), (7184, 7472), (4096, 4112))


def _win_to_padded(w_in):
    parts = [w_in[:, a:b] for a, b in _WIN_SEGMENTS]
    parts.append(jnp.zeros((w_in.shape[0], ZP - IN_TOTAL), w_in.dtype))
    return jnp.concatenate(parts, axis=1)


def _win_from_padded(w_p):
    widths = [b - a for a, b in _WIN_SEGMENTS]
    offs = [sum(widths[:i]) for i in range(len(widths))]
    seg = {a: w_p[:, o:o + wd] for (a, _), o, wd in zip(_WIN_SEGMENTS, offs, widths)}
    return jnp.concatenate([seg[a] for a in sorted(seg)], axis=1)


def _add(a, b, name):
    t, c = a.shape
    tm = _tile(t, 416, 8)
    return _tw_fwd(lambda x, y: (x + y,), [a, b], [_row_spec(tm, c)] * 2, [jax.ShapeDtypeStruct(a.shape, f32)],
                   [_row_spec(tm, c)], (t // tm,), name)[0]


def _assemble3(d_r1, d_r2, d_k, d_v1, d_v2, name):
    t = d_r1.shape[0]
    tm = _tile(t, 208, 8)

    def body(r1, r2, kk, v1, v2, o_ref):
        o_ref[:, 0:D] = r1[...] + r2[...]
        o_ref[:, D:2 * D] = kk[...]
        o_ref[:, 2 * D:3 * D] = v1[...] + v2[...]

    return pl.pallas_call(body, name=name, grid=(t // tm,), in_specs=[_row_spec(tm, D)] * 5,
                          out_specs=_row_spec(tm, 3 * D), out_shape=jax.ShapeDtypeStruct((t, 3 * D), f32),
                          )(d_r1, d_r2, d_k, d_v1, d_v2)


def _conv_bwd(fn, z, conv_w, ct, idx, t):
    def body(z_ref, w_ref, ct_ref, dz_ref, dw_ref):
        _, vjp = jax.vjp(lambda a, b: fn(a, b), z_ref[...], w_ref[...])
        dz, dw = vjp((ct_ref[...],))
        dz_ref[...] = dz
        dw_ref[...] = dw

    return pl.pallas_call(
        body, name=f"a_conv{idx}_bwd", grid=(A_HEADS,),
        in_specs=[_col_spec(t, 8 * idx), pl.BlockSpec((4, LANES), lambda j, o=8 * idx: (0, j + o)), _col_spec(t, 0)],
        out_specs=[_col_spec(t, 0), pl.BlockSpec((4, LANES), lambda j: (0, j))],
        out_shape=[jax.ShapeDtypeStruct((t, D), f32), jax.ShapeDtypeStruct((4, D), f32)],
    )(z, conv_w, ct)


def _position():
    return lax.axis_index("x"), lax.axis_index("y"), lax.axis_index("c")


def _flip(v, f):
    return 1 - v if f else v


_CHIP_FLIPS = ((1, 0), (0, 1), (1, 1))
_DEV_FLIPS = tuple((fx, fy, fc) for fx in (0, 1) for fy in (0, 1) for fc in (0, 1) if (fx, fy, fc) != (0, 0, 0))


def _gather_chips(arrs, name):
    n = len(arrs)

    def body(*refs):
        ins, outs = refs[:n], refs[n:2 * n]
        send, recv, loc = refs[2 * n:]
        x, y, c = _position()
        me = 2 * x + y
        started = []
        for a in range(n):
            lc = pltpu.make_async_copy(ins[a], outs[a].at[me], loc.at[a])
            lc.start()
            started.append(lc)
        sends, recvs = [], []
        for a in range(n):
            for j, (fx, fy) in enumerate(_CHIP_FLIPS):
                px, py = _flip(x, fx), _flip(y, fy)
                cp = pltpu.make_async_remote_copy(src_ref=ins[a], dst_ref=outs[a].at[me], send_sem=send.at[a, j],
                                                  recv_sem=recv.at[a, j], device_id=(px, py, c), device_id_type=MESH)
                cp.start()
                sends.append(cp)
                recvs.append(pltpu.make_async_remote_copy(
                    src_ref=ins[a], dst_ref=outs[a].at[2 * px + py], send_sem=send.at[a, j], recv_sem=recv.at[a, j],
                    device_id=(px, py, c), device_id_type=MESH))
        for cp in recvs:
            cp.wait_recv()
        for cp in sends:
            cp.wait_send()
        for lc in started:
            lc.wait()

    return pl.pallas_call(
        body, name=name, in_specs=[ANY] * n, out_specs=[ANY] * n,
        out_shape=[jax.ShapeDtypeStruct((N_CHIPS,) + a.shape, a.dtype) for a in arrs],
        scratch_shapes=[pltpu.SemaphoreType.DMA((n, 3)), pltpu.SemaphoreType.DMA((n, 3)), pltpu.SemaphoreType.DMA((n,))],
    )(*arrs)


def _scatter_chips(g, name):
    def body(g_ref, out_ref, send, recv, loc):
        x, y, c = _position()
        me = 2 * x + y
        lc = pltpu.make_async_copy(g_ref.at[me], out_ref.at[me], loc)
        lc.start()
        sends, recvs = [], []
        for j, (fx, fy) in enumerate(_CHIP_FLIPS):
            px, py = _flip(x, fx), _flip(y, fy)
            p = 2 * px + py
            cp = pltpu.make_async_remote_copy(src_ref=g_ref.at[p], dst_ref=out_ref.at[me], send_sem=send.at[j],
                                              recv_sem=recv.at[j], device_id=(px, py, c), device_id_type=MESH)
            cp.start()
            sends.append(cp)
            recvs.append(pltpu.make_async_remote_copy(src_ref=g_ref.at[me], dst_ref=out_ref.at[p], send_sem=send.at[j],
                                                      recv_sem=recv.at[j], device_id=(px, py, c), device_id_type=MESH))
        for cp in recvs:
            cp.wait_recv()
        for cp in sends:
            cp.wait_send()
        lc.wait()

    return pl.pallas_call(
        body, name=name, in_specs=[ANY], out_specs=ANY, out_shape=jax.ShapeDtypeStruct(g.shape, g.dtype),
        scratch_shapes=[pltpu.SemaphoreType.DMA((3,)), pltpu.SemaphoreType.DMA((3,)), pltpu.SemaphoreType.DMA(())],
    )(g)


def _swap_sibling(p, name):
    def body(p_ref, out_ref, send, recv):
        x, y, c = _position()
        cp = pltpu.make_async_remote_copy(src_ref=p_ref, dst_ref=out_ref, send_sem=send, recv_sem=recv,
                                          device_id=(x, y, 1 - c), device_id_type=MESH)
        cp.start()
        cp.wait()

    return pl.pallas_call(
        body, name=name, in_specs=[ANY], out_specs=ANY, out_shape=jax.ShapeDtypeStruct(p.shape, p.dtype),
        scratch_shapes=[pltpu.SemaphoreType.DMA(()), pltpu.SemaphoreType.DMA(())],
    )(p)


def _gather_devices(s, name):
    def body(s_ref, out_ref, send, recv, loc):
        x, y, c = _position()
        me = 4 * x + 2 * y + c
        lc = pltpu.make_async_copy(s_ref, out_ref.at[me], loc)
        lc.start()
        sends, recvs = [], []
        for j, (fx, fy, fc) in enumerate(_DEV_FLIPS):
            px, py, pc = _flip(x, fx), _flip(y, fy), _flip(c, fc)
            cp = pltpu.make_async_remote_copy(src_ref=s_ref, dst_ref=out_ref.at[me], send_sem=send.at[j],
                                              recv_sem=recv.at[j], device_id=(px, py, pc), device_id_type=MESH)
            cp.start()
            sends.append(cp)
            recvs.append(pltpu.make_async_remote_copy(
                src_ref=s_ref, dst_ref=out_ref.at[4 * px + 2 * py + pc], send_sem=send.at[j], recv_sem=recv.at[j],
                device_id=(px, py, pc), device_id_type=MESH))
        for cp in recvs:
            cp.wait_recv()
        for cp in sends:
            cp.wait_send()
        lc.wait()

    return pl.pallas_call(
        body, name=name, in_specs=[ANY], out_specs=ANY, out_shape=jax.ShapeDtypeStruct((N_DEV,) + s.shape, s.dtype),
        scratch_shapes=[pltpu.SemaphoreType.DMA((7,)), pltpu.SemaphoreType.DMA((7,)), pltpu.SemaphoreType.DMA(())],
    )(s)


def _sum_slots(a, name):
    s, r, c = a.shape
    tr = _tile(r, 2048, 8)

    def body(a_ref, o_ref):
        acc = a_ref[0]
        for i in range(1, s):
            acc = acc + a_ref[i]
        o_ref[...] = acc

    return pl.pallas_call(body, name=name, grid=(r // tr,), in_specs=[pl.BlockSpec((s, tr, c), lambda i: (0, i, 0))],
                          out_specs=pl.BlockSpec((tr, c), lambda i: (i, 0)),
                          out_shape=jax.ShapeDtypeStruct((r, c), f32))(a)


def _adamw(w, g_parts, m, v, name):
    shape = w.shape
    size = w.size
    view = (size // LANES, LANES) if size % LANES == 0 else (1, size)
    rows = view[0]
    tr = _tile(rows, 2048, 8) if rows > 2048 else rows
    n_g = len(g_parts)

    def body(*refs):
        w_ref = refs[0]
        g_refs = refs[1:1 + n_g]
        m_ref, v_ref, g_out, d_out, m_out, v_out = refs[1 + n_g:]
        g = g_refs[0][...]
        for gr in g_refs[1:]:
            g = g + gr[...]
        m_new = ADAM_B1 * m_ref[...] + (1.0 - ADAM_B1) * g
        v_new = ADAM_B2 * v_ref[...] + (1.0 - ADAM_B2) * (g * g)
        m_hat = m_new / (1.0 - ADAM_B1 ** ADAM_STEP)
        v_hat = v_new / (1.0 - ADAM_B2 ** ADAM_STEP)
        g_out[...] = g
        d_out[...] = -ADAM_LR * (m_hat / (jnp.sqrt(v_hat) + ADAM_EPS) + ADAM_WD * w_ref[...])
        m_out[...] = m_new
        v_out[...] = v_new

    spec = pl.BlockSpec((tr, view[1]), lambda i: (i, 0))
    args = [w.reshape(view)] + [g.reshape(view) for g in g_parts] + [m.reshape(view), v.reshape(view)]
    outs = pl.pallas_call(body, name=name, grid=(rows // tr,), in_specs=[spec] * len(args), out_specs=[spec] * 4,
                          out_shape=[jax.ShapeDtypeStruct(view, f32)] * 4)(*args)
    return [o.reshape(shape) for o in outs]


_BIG = ('ffn1_w_gu', 'ffn1_w_down', 'w_in', 'w_out', 'ffn2_w_gu', 'ffn2_w_down')
_SMALL_SHARDED = ('meta_tokens', 'a_conv_w', 'b_w_up', 'b_a_up', 'b_g_up')
_WEIGHTS = ('meta_tokens', 'ffn1_norm', 'ffn1_w_gu', 'ffn1_w_down', 'mix_norm', 'w_in', 'a_conv_w', 'a_log_rate',
            'a_dt_bias', 'a_out_norm', 'b_shift_mu', 'b_w0', 'b_w_up', 'b_a0', 'b_a_up', 'b_g_up', 'b_k_k', 'b_k_a',
            'b_r_k', 'b_ln_gain', 'b_ln_bias', 'w_out', 'ffn2_norm', 'ffn2_w_gu', 'ffn2_w_down', 'final_norm')
_SMALL = tuple(n for n in _WEIGHTS if n not in _BIG)


def _pack(arrs, dtype, row_mult=8):
    flat = jnp.concatenate([a.reshape(-1).astype(dtype) for a in arrs])
    rows = -(-flat.size // LANES)
    rows = -(-rows // row_mult) * row_mult
    flat = jnp.pad(flat, (0, rows * LANES - flat.size))
    return flat.reshape(rows, LANES)


def _unpack(packed, shapes, lead=()):
    flat = packed.reshape(lead + (-1,))
    out, off = [], 0
    for sh in shapes:
        n = 1
        for d in sh:
            n *= d
        out.append(flat[..., off:off + n].reshape(lead + tuple(sh)))
        off += n
    return out


def _cols_from_shards(s):
    return jnp.concatenate([s[i] for i in range(N_CHIPS)], axis=-1)


def _cols_to_shards(a):
    r, c = a.shape
    return a.reshape(r, N_CHIPS, c // N_CHIPS).transpose(1, 0, 2)


def kernel(x, meta_tokens, ffn1_norm, ffn1_w_gu, ffn1_w_down, mix_norm, w_in, a_conv_w, a_log_rate, a_dt_bias, a_out_norm, b_shift_mu, b_w0, b_w_up, b_a0, b_a_up, b_g_up, b_k_k, b_k_a, b_r_k, b_ln_gain, b_ln_bias, w_out, ffn2_norm, ffn2_w_gu, ffn2_w_down, final_norm, loss_target, m_meta_tokens, m_ffn1_norm, m_ffn1_w_gu, m_ffn1_w_down, m_mix_norm, m_w_in, m_a_conv_w, m_a_log_rate, m_a_dt_bias, m_a_out_norm, m_b_shift_mu, m_b_w0, m_b_w_up, m_b_a0, m_b_a_up, m_b_g_up, m_b_k_k, m_b_k_a, m_b_r_k, m_b_ln_gain, m_b_ln_bias, m_w_out, m_ffn2_norm, m_ffn2_w_gu, m_ffn2_w_down, m_final_norm, v_meta_tokens, v_ffn1_norm, v_ffn1_w_gu, v_ffn1_w_down, v_mix_norm, v_w_in, v_a_conv_w, v_a_log_rate, v_a_dt_bias, v_a_out_norm, v_b_shift_mu, v_b_w0, v_b_w_up, v_b_a0, v_b_a_up, v_b_g_up, v_b_k_k, v_b_k_a, v_b_r_k, v_b_ln_gain, v_b_ln_bias, v_w_out, v_ffn2_norm, v_ffn2_w_gu, v_ffn2_w_down, v_final_norm):
    args = locals()
    wts = {n: args[n] for n in _WEIGHTS}
    mom = {n: args["m_" + n] for n in _WEIGHTS}
    var = {n: args["v_" + n] for n in _WEIGHTS}
    chip = 2 * lax.axis_index("x") + lax.axis_index("y")

    big_shapes = [wts[n].shape[1:] for n in _BIG]
    small_shapes = [wts[n].shape[-2:] for n in _SMALL_SHARDED]
    big_packed = _pack([wts[n] for n in _BIG], bf16)
    small_packed = _pack([wts[n] for n in _SMALL_SHARDED], f32)
    big_all, small_all = _gather_chips([big_packed, small_packed], "gather_weights")
    gu1, dn1, w_in_s, w_out_s, gu2, dn2 = _unpack(big_all, big_shapes, (N_CHIPS,))
    meta_s, conv_s, wup_s, aup_s, gup_s = _unpack(small_all, small_shapes, (N_CHIPS,))
    w = {
        'ffn1_norm': ffn1_norm, 'mix_norm': mix_norm, 'ffn2_norm': ffn2_norm, 'final_norm': final_norm[None, :],
        'ffn1_wg': jnp.concatenate([gu1[0], gu1[1]], axis=1), 'ffn1_wu': jnp.concatenate([gu1[2], gu1[3]], axis=1),
        'ffn1_wd': dn1.reshape(D_FF, D),
        'ffn2_wg': jnp.concatenate([gu2[0], gu2[1]], axis=1), 'ffn2_wu': jnp.concatenate([gu2[2], gu2[3]], axis=1),
        'ffn2_wd': dn2.reshape(D_FF, D),
        'w_in_p': _win_to_padded(_cols_from_shards(w_in_s)), 'w_out': w_out_s.reshape(D, D),
        'a_conv_w': _cols_from_shards(conv_s), 'b_w_up': _cols_from_shards(wup_s), 'b_a_up': _cols_from_shards(aup_s),
        'b_g_up': _cols_from_shards(gup_s),
        'a_log_rate': a_log_rate, 'a_dt_bias': a_dt_bias, 'a_out_norm': a_out_norm, 'b_shift_mu': b_shift_mu,
        'b_w0': b_w0, 'b_a0': b_a0, 'b_k_k': b_k_k, 'b_k_a': b_k_a, 'b_r_k': b_r_k, 'b_ln_gain': b_ln_gain,
        'b_ln_bias': b_ln_bias,
    }
    meta_full = _cols_from_shards(meta_s)

    h0 = jnp.concatenate([jnp.zeros((PAD, D), f32), meta_full, x[0]], axis=0)
    tgt = jnp.concatenate([jnp.zeros((SKIP, D), f32), loss_target[0]], axis=0)
    loss_local, d_h0, g = _local_step(h0, tgt, w)
    loss = lax.psum(loss_local, ("x", "y", "c"))
    grad_x = d_h0[SKIP:][None]

    big_grads = [
        _cols_to_shards(jnp.concatenate([g['ffn1_wg'], g['ffn1_wu']], axis=1)),
        g['ffn1_wd'].reshape(N_CHIPS, D_FF // N_CHIPS, D),
        _cols_to_shards(_win_from_padded(g['w_in_p'])),
        g['w_out'].reshape(N_CHIPS, D // N_CHIPS, D),
        _cols_to_shards(jnp.concatenate([g['ffn2_wg'], g['ffn2_wu']], axis=1)),
        g['ffn2_wd'].reshape(N_CHIPS, D_FF // N_CHIPS, D),
    ]
    g_packed = jnp.concatenate([a.reshape(N_CHIPS, -1, LANES) for a in big_grads], axis=1)
    assert g_packed.shape[1] == big_packed.shape[0]
    mine = _sum_slots(_scatter_chips(g_packed, "scatter_grads"), "sum_chips")
    theirs = _swap_sibling(mine, "swap_sibling")
    mine_parts = _unpack(mine, big_shapes)
    their_parts = _unpack(theirs, big_shapes)

    small_full = {
        'meta_tokens': d_h0[PAD:SKIP], 'ffn1_norm': g['ffn1_norm'], 'mix_norm': g['mix_norm'], 'a_conv_w': g['a_conv_w'],
        'a_log_rate': g['a_log_rate'], 'a_dt_bias': g['a_dt_bias'], 'a_out_norm': g['a_out_norm'],
        'b_shift_mu': g['b_shift_mu'], 'b_w0': g['b_w0'], 'b_w_up': g['b_w_up'], 'b_a0': g['b_a0'], 'b_a_up': g['b_a_up'],
        'b_g_up': g['b_g_up'], 'b_k_k': g['b_k_k'], 'b_k_a': g['b_k_a'], 'b_r_k': g['b_r_k'], 'b_ln_gain': g['b_ln_gain'],
        'b_ln_bias': g['b_ln_bias'], 'ffn2_norm': g['ffn2_norm'], 'final_norm': g['final_norm'],
    }
    s_shapes = [small_full[n].shape for n in _SMALL]
    s_sum = _sum_slots(_gather_devices(_pack([small_full[n] for n in _SMALL], f32, row_mult=256), "gather_small"),
                       "sum_small")
    s_parts = dict(zip(_SMALL, _unpack(s_sum, s_shapes)))

    grad, delta, new_m, new_v = {}, {}, {}, {}
    for n, a, b in zip(_BIG, mine_parts, their_parts):
        sh = wts[n].shape
        grad[n], delta[n], new_m[n], new_v[n] = _adamw(wts[n], [a.reshape(sh), b.reshape(sh)], mom[n], var[n],
                                                       f"adamw_{n}")
    for n in _SMALL:
        gs = s_parts[n]
        if n in _SMALL_SHARDED:
            width = wts[n].shape[-1]
            gs = lax.dynamic_slice_in_dim(gs, chip * width, width, axis=gs.ndim - 1)
        gs = gs.reshape(wts[n].shape)
        grad[n], delta[n], new_m[n], new_v[n] = _adamw(wts[n], [gs], mom[n], var[n], f"adamw_{n}")

    return (loss, grad_x, *[grad[n] for n in _WEIGHTS], *[delta[n] for n in _WEIGHTS],
            *[new_m[n] for n in _WEIGHTS], *[new_v[n] for n in _WEIGHTS])
```

```python
import functools

import jax
import jax.numpy as jnp
from jax import lax
from jax.experimental import pallas as pl
from jax.experimental.pallas import tpu as pltpu

f32 = jnp.float32
bf16 = jnp.bfloat16
HI = lax.Precision.HIGHEST
MESH = pl.DeviceIdType.MESH
ANY = pl.BlockSpec(memory_space=pl.ANY)

D = 1024
N_META = 16
CHUNK = 64
PAD = CHUNK - N_META
SKIP = PAD + N_META
EPS = 1e-6
D_FF = 2816
A_HEADS = 8
A_DK = 128
B_HEADS = 16
B_N = 64
B_GN_EPS = B_N * 1e-5
W_LORA, AA_LORA, G_LORA = 64, 64, 160
IN_TOTAL = 9520
ZP = 9600
LANES = 128
N_CHIPS = 4
N_DEV = 8

ADAM_LR, ADAM_B1, ADAM_B2, ADAM_EPS, ADAM_WD, ADAM_STEP = 0.001, 0.9, 0.999, 1e-08, 0.01, 10

MXU_DTYPE = bf16


def _tile(n, cap, mult):
    if n <= cap:
        return n
    best = None
    for t in range(mult, cap + 1, mult):
        if n % t == 0:
            best = t
    assert best is not None, (n, cap, mult)
    return best


def _sigmoid(x):
    return jax.nn.sigmoid(x)


def _silu(x):
    return x * jax.nn.sigmoid(x)


def _softplus(x):
    return jnp.maximum(x, 0.0) + jnp.log(1.0 + jnp.exp(-jnp.abs(x)))


def _mmf(a, b):
    return jnp.dot(a, b, precision=HI, preferred_element_type=f32)


def _mm_nt(a, b):
    return lax.dot_general(a, b, (((1,), (1,)), ((), ())), precision=HI, preferred_element_type=f32)


def _mm_tn(a, b):
    return lax.dot_general(a, b, (((0,), (0,)), ((), ())), precision=HI, preferred_element_type=f32)


def _head_matrix(c, nh):
    hd = c // nh
    r = lax.broadcasted_iota(jnp.int32, (c, nh), 0)
    h = lax.broadcasted_iota(jnp.int32, (c, nh), 1)
    return ((r >= h * hd) & (r < (h + 1) * hd)).astype(f32)


def _head_sum(x, nh):
    e = _head_matrix(x.shape[-1], nh)
    return _mm_nt(_mmf(x, e), e)


@functools.partial(jax.custom_vjp, nondiff_argnums=(1,))
def _shift_rows(x, s):
    n = x.shape[0]
    row = lax.broadcasted_iota(jnp.int32, x.shape, 0)
    if s > 0:
        return jnp.where(row >= s, pltpu.roll(x, s, 0), 0.0)
    return jnp.where(row < n + s, pltpu.roll(x, n + s, 0), 0.0)


def _shift_rows_fwd(x, s):
    return _shift_rows(x, s), None


def _shift_rows_bwd(s, _, g):
    return (_shift_rows(g, -s),)


_shift_rows.defvjp(_shift_rows_fwd, _shift_rows_bwd)


def _matmul(a, b, *, ta=False, tb=False, res=None, scale=1.0, name):
    assert not (ta and tb)
    (ar, ac), (br, bc) = a.shape, b.shape
    m, k = (ac, ar) if ta else (ar, ac)
    n, kb = (br, bc) if tb else (bc, br)
    assert k == kb, (a.shape, b.shape, ta, tb)
    tm = _tile(m, 1408, LANES) if ta else _tile(m, 832, 8)
    tn = _tile(n, 1408, LANES)
    tk = _tile(k, 1040, 8) if ta else _tile(k, 1408, LANES)
    nk = k // tk
    dn = (((0 if ta else 1,), (1 if tb else 0,)), ((), ()))

    def body(*refs):
        if res is not None:
            a_ref, b_ref, r_ref, o_ref, acc = refs
        else:
            a_ref, b_ref, o_ref, acc = refs
        kk = pl.program_id(2)

        @pl.when(kk == 0)
        def _():
            acc[...] = jnp.zeros_like(acc)

        acc[...] += lax.dot_general(a_ref[...].astype(MXU_DTYPE), b_ref[...].astype(MXU_DTYPE), dn,
                                    preferred_element_type=f32,
                                    precision=None if MXU_DTYPE == bf16 else HI)

        @pl.when(kk == nk - 1)
        def _():
            out = acc[...]
            if scale != 1.0:
                out = out * scale
            if res is not None:
                out = r_ref[...] + out
            o_ref[...] = out

    if ta:
        a_spec = pl.BlockSpec((tk, tm), lambda i, j, kk: (kk, i))
    else:
        a_spec = pl.BlockSpec((tm, tk), lambda i, j, kk: (i, kk))
    if tb:
        b_spec = pl.BlockSpec((tn, tk), lambda i, j, kk: (j, kk))
    else:
        b_spec = pl.BlockSpec((tk, tn), lambda i, j, kk: (kk, j))
    in_specs = [a_spec, b_spec]
    args = [a, b]
    if res is not None:
        in_specs.append(pl.BlockSpec((tm, tn), lambda i, j, kk: (i, j)))
        args.append(res)
    return pl.pallas_call(
        body, name=name, grid=(m // tm, n // tn, nk), in_specs=in_specs,
        out_specs=pl.BlockSpec((tm, tn), lambda i, j, kk: (i, j)),
        out_shape=jax.ShapeDtypeStruct((m, n), f32),
        scratch_shapes=[pltpu.VMEM((tm, tn), f32)],
        compiler_params=pltpu.CompilerParams(dimension_semantics=("parallel", "parallel", "arbitrary")),
    )(*args)


def _tw_fwd(fn, ins, in_specs, out_shapes, out_specs, grid, name, with_pid=False):
    n_in = len(ins)

    def body(*refs):
        vals = [r[...] for r in refs[:n_in]]
        outs = fn(pl.program_id(0), *vals) if with_pid else fn(*vals)
        for r, o in zip(refs[n_in:], outs):
            r[...] = o

    return pl.pallas_call(body, name=name, grid=grid, in_specs=in_specs, out_specs=out_specs,
                          out_shape=out_shapes)(*ins)


def _tw_bwd(fn, ins, in_specs, cts, ct_specs, kinds, grid, name, with_pid=False):
    n_in, n_ct = len(ins), len(cts)
    diff = [i for i, kd in enumerate(kinds) if kd is not None]

    def body(*refs):
        vals = [r[...] for r in refs[:n_in]]
        ctv = tuple(r[...] for r in refs[n_in:n_in + n_ct])
        g_refs = refs[n_in + n_ct:]
        pid = pl.program_id(0)

        def f(*dv):
            full = list(vals)
            for i, v in zip(diff, dv):
                full[i] = v
            out = fn(pid, *full) if with_pid else fn(*full)
            return tuple(out)

        _, vjp = jax.vjp(f, *[vals[i] for i in diff])
        gs = vjp(ctv)
        first = pid == 0
        for i2 in range(1, len(grid)):
            first = first & (pl.program_id(i2) == 0)
        for i, g, g_ref in zip(diff, gs, g_refs):
            if kinds[i] != 'acc':
                g_ref[...] = g
            else:
                @pl.when(first)
                def _(g=g, g_ref=g_ref):
                    g_ref[...] = g

                @pl.when(jnp.logical_not(first))
                def _(g=g, g_ref=g_ref):
                    g_ref[...] += g

    zero_map = {1: lambda *a: (0,), 2: lambda *a: (0, 0), 3: lambda *a: (0, 0, 0)}
    out_specs, out_shapes = [], []
    for i in diff:
        if kinds[i] == 'tile':
            out_shapes.append(jax.ShapeDtypeStruct(ins[i].shape, f32))
            out_specs.append(in_specs[i])
        elif kinds[i] == 'acc':
            out_shapes.append(jax.ShapeDtypeStruct(ins[i].shape, f32))
            out_specs.append(pl.BlockSpec(ins[i].shape, zero_map[ins[i].ndim]))
        else:
            out_shapes.append(jax.ShapeDtypeStruct(kinds[i][1], f32))
            out_specs.append(kinds[i][2])
    return pl.pallas_call(body, name=name, grid=grid, in_specs=list(in_specs) + list(ct_specs),
                          out_specs=out_specs, out_shape=out_shapes)(*ins, *cts)


def _row_spec(tm, c, col_block=0):
    return pl.BlockSpec((tm, c), lambda i, cb=col_block: (i, cb))


def _full_spec(shape):
    nd = len(shape)
    return pl.BlockSpec(shape, lambda *a, nd=nd: (0,) * nd)


def _f_rms(x, g):
    return (x * lax.rsqrt(jnp.mean(x * x, axis=-1, keepdims=True) + EPS) * g,)


def _f_swiglu(gate, up):
    return (_silu(gate) * up,)


def _f_loss(pid, h, g, tgt, *, tm):
    y = h * lax.rsqrt(jnp.mean(h * h, axis=-1, keepdims=True) + EPS) * g
    row = pid * tm + lax.broadcasted_iota(jnp.int32, (tm, 1), 0)
    err = jnp.where(row >= SKIP, y - tgt, 0.0)
    per_row = jnp.mean(err * err, axis=-1, keepdims=True)
    return (0.5 * jnp.sum(per_row, axis=0, keepdims=True),)


def _f_conv(x, w, *, norm, scale):
    y = x * w[3:4, :]
    for s in (1, 2, 3):
        y = y + _shift_rows(x, s) * w[3 - s:4 - s, :]
    y = _silu(y)
    if norm:
        y = y * lax.rsqrt(jnp.sum(y * y, axis=-1, keepdims=True) + 1e-6) * scale
    return (y,)


def _f_dgates(pid, abeta, aalpha, log_rate, dt_bias, *, tm):
    row = pid * tm + lax.broadcasted_iota(jnp.int32, (tm, 1), 0)
    live = row >= PAD
    beta = jnp.where(live, _sigmoid(abeta), 0.0)
    g = jnp.where(live, -jnp.exp(log_rate) * _softplus(aalpha + dt_bias), 0.0)
    return beta, g


def _f_tshift(z, mu):
    return (z + (_shift_rows(z, 1) - z) * mu,)


def _f_rwkv_pre(k, wd, ad, gd, w0, w_up, a0, a_up, g_up, k_k, k_a):
    w_log = -_softplus(-(w0 + _mmf(jnp.tanh(wd), w_up))) - 0.5
    lw = -jnp.exp(w_log)
    a_lr = _sigmoid(a0 + _mmf(ad, a_up))
    gate = _mmf(_sigmoid(gd), g_up)
    kkp = k * k_k
    kk = kkp * lax.rsqrt(_head_sum(kkp * kkp, B_HEADS) + 1e-6)
    kmod = k * (1.0 + (a_lr - 1.0) * k_a)
    return lw, kmod, -kk, kk * a_lr, gate


def _f_mix_post(o, az, y, r, kmod, v, gate, ga, gb, out_gain, ln_g, ln_b, r_k):
    ms = _head_sum(o * o, A_HEADS) * (1.0 / A_DK)
    oa = o * lax.rsqrt(ms + EPS) * out_gain * _silu(az)
    mean = _head_sum(y, B_HEADS) * (1.0 / B_N)
    yc = y - mean
    var = _head_sum(yc * yc, B_HEADS) * (1.0 / B_N)
    yn = yc * lax.rsqrt(var + B_GN_EPS) * ln_g + ln_b
    bonus = _head_sum(r * kmod * r_k, B_HEADS) * v
    ob = (yn + bonus) * gate
    return (_sigmoid(ga) * oa + _sigmoid(gb) * ob,)


SCAN_PASSES = 3


def _split2(a):
    hi = a.astype(bf16)
    return hi, (a - hi.astype(f32)).astype(bf16)


def _dot_passes(a, b, ca, cb):
    dn = (((ca,), (cb,)), ((), ()))
    if SCAN_PASSES == 0:
        return lax.dot_general(a, b, dn, precision=HI, preferred_element_type=f32)
    if SCAN_PASSES == 1:
        return lax.dot_general(a.astype(bf16), b.astype(bf16), dn, preferred_element_type=f32)
    ah, al = _split2(a)
    bh, bl = _split2(b)
    return (lax.dot_general(ah, bh, dn, preferred_element_type=f32)
            + (lax.dot_general(ah, bl, dn, preferred_element_type=f32)
               + lax.dot_general(al, bh, dn, preferred_element_type=f32)))


@functools.partial(jax.custom_vjp, nondiff_argnums=(2, 3))
def _sdot(a, b, ca, cb):
    return _dot_passes(a, b, ca, cb)


def _sdot_fwd(a, b, ca, cb):
    return _dot_passes(a, b, ca, cb), (a, b)


def _sdot_bwd(ca, cb, res, g):
    a, b = res
    if (ca, cb) == (1, 0):
        return _dot_passes(g, b, 1, 1), _dot_passes(a, g, 0, 0)
    if (ca, cb) == (1, 1):
        return _dot_passes(g, b, 1, 0), _dot_passes(g, a, 0, 0)
    assert (ca, cb) == (0, 0)
    return _dot_passes(b, g, 1, 1), _dot_passes(a, g, 1, 0)


_sdot.defvjp(_sdot_fwd, _sdot_bwd)


def _smm(a, b):
    return _sdot(a, b, 1, 0)


def _smm_nt(a, b):
    return _sdot(a, b, 1, 1)


def _smm_tn(a, b):
    return _sdot(a, b, 0, 0)


def _tri_dot(x, ca):
    n = x.shape[0]
    incl = _tri_masks(n)[0]
    dn = (((ca,), (0,)), ((), ()))
    if SCAN_PASSES == 0:
        return lax.dot_general(incl.astype(f32), x, dn, precision=HI, preferred_element_type=f32)
    tri = incl.astype(bf16)
    hi, r1 = x.astype(bf16), None
    r1 = x - hi.astype(f32)
    mid = r1.astype(bf16)
    lo = (r1 - mid.astype(f32)).astype(bf16)
    return (lax.dot_general(tri, hi, dn, preferred_element_type=f32)
            + (lax.dot_general(tri, mid, dn, preferred_element_type=f32)
               + lax.dot_general(tri, lo, dn, preferred_element_type=f32)))


@jax.custom_vjp
def _cumsum_rows(x):
    return _tri_dot(x, 1)


def _cumsum_rows_fwd(x):
    return _tri_dot(x, 1), None


def _cumsum_rows_bwd(_, g):
    return (_tri_dot(g, 0),)


_cumsum_rows.defvjp(_cumsum_rows_fwd, _cumsum_rows_bwd)


def _tri_masks(n):
    i = lax.broadcasted_iota(jnp.int32, (n, n), 0)
    j = lax.broadcasted_iota(jnp.int32, (n, n), 1)
    return i >= j, i > j, i == j, i <= j


def _unit_lower_inv(low):
    n = low.shape[0]
    assert n == CHUNK
    _, _, eye, _ = _tri_masks(n)
    acc = eye.astype(f32) + low
    p = low
    for _ in range(5):
        p = _smm(p, p)
        acc = acc + _smm(acc, p)
    return acc


def _delta_chunk(s, q, k, v, beta, g):
    incl, strict, eye, upper = _tri_masks(CHUNK)
    g_row = jnp.sum(jnp.where(eye, g, 0.0), axis=0, keepdims=True)
    gc = jnp.sum(jnp.where(incl, g_row, 0.0), axis=1, keepdims=True)
    gc_row = jnp.sum(jnp.where(upper, g, 0.0), axis=0, keepdims=True)
    decay = jnp.where(incl, jnp.exp(jnp.where(incl, gc - gc_row, 0.0)), 0.0)
    kb = k * beta
    vb = v * beta
    m = jnp.where(strict, _smm_nt(kb, k) * decay, 0.0)
    tinv = _unit_lower_inv(-m)
    u = _smm(tinv, vb)
    wk = _smm(tinv, kb * jnp.exp(gc))
    attn = _smm_nt(q, k) * decay
    qg = q * jnp.exp(gc)
    g_last = jnp.sum(g, axis=0, keepdims=True)
    k_tail = k * jnp.exp(g_last - gc)
    v_new = u - _smm(wk, s)
    o = _smm(qg, s) + _smm(attn, v_new)
    s_new = s * jnp.exp(g_last) + _smm_tn(k_tail, v_new)
    return o, s_new


def _rwkv_chunk(st, r, k, v, a, b, lw):
    c = CHUNK
    incl, strict, _, _ = _tri_masks(c)
    lane = lax.broadcasted_iota(jnp.int32, (c, 2 * B_N), 1)
    first = lane < B_N
    bi = lax.broadcasted_iota(jnp.int32, (2 * B_N, 2 * B_N), 0) < B_N
    bj = lax.broadcasted_iota(jnp.int32, (2 * B_N, 2 * B_N), 1) < B_N
    blockdiag = bi == bj
    cum = _cumsum_rows(lw)
    e_pos = jnp.exp(cum)
    e_neg = jnp.exp(-cum)
    rt = r * e_pos
    at = a * jnp.exp(cum - lw)
    kt = k * e_neg
    bt = b * e_neg
    a_s0 = _smm_nt(at, st)
    r_s0 = _smm_nt(rt, st)
    u = jnp.zeros((c, 2 * B_N), f32)
    for sel in (first, jnp.logical_not(first)):
        at_h = jnp.where(sel, at, 0.0)
        ab = jnp.where(strict, _smm_nt(at_h, bt), 0.0)
        ak = jnp.where(strict, _smm_nt(at_h, kt), 0.0)
        t_h = _unit_lower_inv(ab)
        u_h = _smm(t_h, jnp.where(sel, a_s0, 0.0) + _smm(ak, jnp.where(sel, v, 0.0)))
        u = u + u_h
    y = r_s0
    for sel in (first, jnp.logical_not(first)):
        rt_h = jnp.where(sel, rt, 0.0)
        rb = jnp.where(incl, _smm_nt(rt_h, bt), 0.0)
        rk = jnp.where(incl, _smm_nt(rt_h, kt), 0.0)
        y = y + _smm(rb, jnp.where(sel, u, 0.0)) + _smm(rk, jnp.where(sel, v, 0.0))
    cl = jnp.sum(lw, axis=0, keepdims=True)
    dec = jnp.exp(cl - cum)
    st_new = st * jnp.exp(cl) + jnp.where(blockdiag, _smm_tn(u, b * dec) + _smm_tn(v, k * dec), 0.0)
    return y, st_new


GROUPS_PER_STEP = 8


def _scan_specs(ins, col_offs, n_chunks, reverse):
    gw = GROUPS_PER_STEP * LANES
    cidx = (lambda c: n_chunks - 1 - c) if reverse else (lambda c: c)
    specs = []
    for a, off in zip(ins, col_offs):
        if a.ndim == 2:
            assert off % gw == 0
            specs.append(pl.BlockSpec((CHUNK, gw), lambda h, c, o=off // gw: (cidx(c), h + o)))
        else:
            specs.append(pl.BlockSpec((GROUPS_PER_STEP, CHUNK, 1), lambda h, c: (h, cidx(c), 0)))
    return specs, cidx


def _group_vals(refs, g):
    return [r[:, g * LANES:(g + 1) * LANES] if len(r.shape) == 2 else r[g] for r in refs]


def _scan_fwd(chunk_fn, ins, col_offs, n_groups, n_chunks, state_shape, name):
    n_in = len(ins)
    gps = GROUPS_PER_STEP
    t = ins[0].shape[0]

    def body(*refs):
        in_refs = refs[:n_in]
        o_ref, s0_ref, st = refs[n_in:]

        @pl.when(pl.program_id(1) == 0)
        def _():
            st[...] = jnp.zeros_like(st)

        states = st[...]
        vals = [jnp.stack(col) for col in zip(*[_group_vals(in_refs, g) for g in range(gps)])]
        o, s_new = jax.vmap(chunk_fn)(states, *vals)
        s0_ref[...] = states
        st[...] = s_new
        for g in range(gps):
            o_ref[:, g * LANES:(g + 1) * LANES] = o[g]

    specs, _ = _scan_specs(ins, col_offs, n_chunks, False)
    return pl.pallas_call(
        body, name=name, grid=(n_groups // gps, n_chunks), in_specs=specs,
        out_specs=[pl.BlockSpec((CHUNK, gps * LANES), lambda h, c: (c, h)),
                   pl.BlockSpec((gps, None) + state_shape, lambda h, c: (h, c, 0, 0))],
        out_shape=[jax.ShapeDtypeStruct((t, n_groups * LANES), f32),
                   jax.ShapeDtypeStruct((n_groups, n_chunks) + state_shape, f32)],
        scratch_shapes=[pltpu.VMEM((gps,) + state_shape, f32)],
        compiler_params=pltpu.CompilerParams(dimension_semantics=("parallel", "arbitrary")),
    )(*ins)


def _scan_bwd(chunk_fn, s0s, ins, col_offs, d_out, n_groups, n_chunks, state_shape, name):
    n_in = len(ins)
    gps = GROUPS_PER_STEP
    t = d_out.shape[0]

    def body(*refs):
        s0_ref = refs[0]
        in_refs = refs[1:1 + n_in]
        do_ref = refs[1 + n_in]
        g_refs = refs[2 + n_in:2 + 2 * n_in]
        dst = refs[2 + 2 * n_in]

        @pl.when(pl.program_id(1) == 0)
        def _():
            dst[...] = jnp.zeros_like(dst)

        vals = [jnp.stack(col) for col in zip(*[_group_vals(in_refs, g) for g in range(gps)])]
        d_o = jnp.stack([do_ref[:, g * LANES:(g + 1) * LANES] for g in range(gps)])
        _, vjp = jax.vjp(jax.vmap(chunk_fn), s0_ref[...], *vals)
        gs = vjp((d_o, dst[...]))
        dst[...] = gs[0]
        for g_ref, gv in zip(g_refs, gs[1:]):
            if len(g_ref.shape) == 2:
                for g in range(gps):
                    g_ref[:, g * LANES:(g + 1) * LANES] = gv[g]
            else:
                g_ref[...] = gv

    specs, cidx = _scan_specs(ins, col_offs, n_chunks, True)
    out_lane = pl.BlockSpec((CHUNK, gps * LANES), lambda h, c: (cidx(c), h))
    g_specs = [out_lane if a.ndim == 2 else sp for a, sp in zip(ins, specs)]
    g_shapes = [(t, n_groups * LANES) if a.ndim == 2 else a.shape for a in ins]
    s0_spec = pl.BlockSpec((gps, None) + state_shape, lambda h, c: (h, cidx(c), 0, 0))
    return pl.pallas_call(
        body, name=name, grid=(n_groups // gps, n_chunks), in_specs=[s0_spec] + specs + [out_lane],
        out_specs=g_specs, out_shape=[jax.ShapeDtypeStruct(sh, f32) for sh in g_shapes],
        scratch_shapes=[pltpu.VMEM((gps,) + state_shape, f32)],
        compiler_params=pltpu.CompilerParams(dimension_semantics=("parallel", "arbitrary")),
    )(s0s, *ins, d_out)


def _rms_fwd(x, g, name):
    t = x.shape[0]
    tm = _tile(t, 416, 8)
    return _tw_fwd(_f_rms, [x, g], [_row_spec(tm, D), _full_spec(g.shape)],
                   [jax.ShapeDtypeStruct(x.shape, f32)], [_row_spec(tm, D)], (t // tm,), name)[0]


def _rms_bwd(x, g, dy, name):
    t = x.shape[0]
    tm = _tile(t, 416, 8)
    return _tw_bwd(_f_rms, [x, g], [_row_spec(tm, D), _full_spec(g.shape)], [dy], [_row_spec(tm, D)],
                   ['tile', 'acc'], (t // tm,), name)


def _ffn_fwd(h, gain, wg, wu, wd, tag):
    xn = _rms_fwd(h, gain, f"{tag}_rms")
    gate = _matmul(xn, wg, name=f"{tag}_gate")
    up = _matmul(xn, wu, name=f"{tag}_up")
    t = h.shape[0]
    tm = _tile(t, 208, 8)
    act = _tw_fwd(_f_swiglu, [gate, up], [_row_spec(tm, D_FF)] * 2, [jax.ShapeDtypeStruct((t, D_FF), f32)],
                  [_row_spec(tm, D_FF)], (t // tm,), f"{tag}_act")[0]
    out = _matmul(act, wd, res=h, scale=0.5, name=f"{tag}_down")
    return out, (xn, gate, up, act)


def _ffn_bwd(h, gain, wg, wu, wd, saved, dout, tag):
    xn, gate, up, act = saved
    t = h.shape[0]
    d_wd = _matmul(act, dout, ta=True, scale=0.5, name=f"{tag}_dwd")
    d_act = _matmul(dout, wd, tb=True, scale=0.5, name=f"{tag}_dact")
    tm = _tile(t, 208, 8)
    d_gate, d_up = _tw_bwd(_f_swiglu, [gate, up], [_row_spec(tm, D_FF)] * 2, [d_act], [_row_spec(tm, D_FF)],
                           ['tile', 'tile'], (t // tm,), f"{tag}_dactf")
    d_wg = _matmul(xn, d_gate, ta=True, name=f"{tag}_dwg")
    d_wu = _matmul(xn, d_up, ta=True, name=f"{tag}_dwu")
    d_xn = _matmul(d_gate, wg, tb=True, name=f"{tag}_dxn_g")
    d_xn = _matmul(d_up, wu, tb=True, res=d_xn, name=f"{tag}_dxn_u")
    d_hn, d_gain = _rms_bwd(h, gain, d_xn, f"{tag}_drms")
    return d_hn, d_gain, d_wg, d_wu, d_wd


def _col_spec(t, first_block):
    return pl.BlockSpec((t, LANES), lambda j, fb=first_block: (0, j + fb))


def _local_step(h0, tgt, w):
    t = h0.shape[0]
    assert t % CHUNK == 0
    nc = t // CHUNK
    grads = {}

    h1, ffn1_saved = _ffn_fwd(h0, w['ffn1_norm'], w['ffn1_wg'], w['ffn1_wu'], w['ffn1_wd'], "ffn1")
    u = _rms_fwd(h1, w['mix_norm'], "mix_rms")
    z = _matmul(u, w['w_in_p'], name="in_proj")
    zs = z[:, 9216:9216 + 304]
    abeta, aalpha = zs[:, 288:296], zs[:, 296:304]

    conv_w = w['a_conv_w']
    conv_fns = [functools.partial(_f_conv, norm=True, scale=A_DK ** -0.5),
                functools.partial(_f_conv, norm=True, scale=1.0),
                functools.partial(_f_conv, norm=False, scale=1.0)]
    qkv = []
    for idx, fn in enumerate(conv_fns):
        qkv.append(_tw_fwd(fn, [z, conv_w], [_col_spec(t, 8 * idx), pl.BlockSpec((4, LANES), lambda j, o=8 * idx: (0, j + o))],
                           [jax.ShapeDtypeStruct((t, D), f32)], [_col_spec(t, 0)], (A_HEADS,), f"a_conv{idx}")[0])
    aq, ak, av = qkv
    tmg = _tile(t, 1040, 8)
    dg_fn = functools.partial(_f_dgates, tm=tmg)
    dg_specs = [_row_spec(tmg, A_HEADS)] * 2 + [_full_spec((1, A_HEADS))] * 2
    beta, gdec = _tw_fwd(dg_fn, [abeta, aalpha, w['a_log_rate'], w['a_dt_bias']], dg_specs,
                         [jax.ShapeDtypeStruct((t, A_HEADS), f32)] * 2, [_row_spec(tmg, A_HEADS)] * 2, (t // tmg,),
                         "a_gates", with_pid=True)
    beta_h = beta.T[:, :, None]
    gdec_h = gdec.T[:, :, None]
    a_ins = [aq, ak, av, beta_h, gdec_h]
    a_offs = [0] * 5
    o_scan, a_s0 = _scan_fwd(_delta_chunk, a_ins, a_offs, A_HEADS, nc, (A_DK, A_DK), "a_scan")

    mu = w['b_shift_mu']
    mu_rkv, mu_s = mu[:, :3072], mu[:, 3072:]
    zf_rkv = _tw_fwd(_f_tshift, [z, mu_rkv], [_col_spec(t, 32), pl.BlockSpec((1, LANES), lambda j: (0, j))],
                     [jax.ShapeDtypeStruct((t, 3072), f32)], [_col_spec(t, 0)], (24,), "b_shift")[0]
    zs_b = zs[:, :288]
    zf_s = _tw_fwd(_f_tshift, [zs_b, mu_s], [_full_spec((t, 288)), _full_spec((1, 288))],
                   [jax.ShapeDtypeStruct((t, 288), f32)], [_full_spec((t, 288))], (1,), "b_shift_s")[0]
    wdf, adf, gdf = zf_s[:, 0:64], zf_s[:, 64:128], zf_s[:, 128:288]
    tmr = _tile(t, 160, 8)
    pre_params = [w['b_w0'], w['b_w_up'], w['b_a0'], w['b_a_up'], w['b_g_up'], w['b_k_k'], w['b_k_a']]
    pre_ins = [zf_rkv, wdf, adf, gdf] + pre_params
    pre_specs = ([_row_spec(tmr, D, 1), _row_spec(tmr, 64), _row_spec(tmr, 64), _row_spec(tmr, 160)]
                 + [_full_spec(p.shape) for p in pre_params])
    lw, kmod, a_s, b_s, bgate = _tw_fwd(_f_rwkv_pre, pre_ins, pre_specs, [jax.ShapeDtypeStruct((t, D), f32)] * 5,
                                        [_row_spec(tmr, D)] * 5, (t // tmr,), "b_pre")
    b_ins = [zf_rkv, kmod, zf_rkv, a_s, b_s, lw]
    b_offs = [0, 0, 2 * D, 0, 0, 0]
    y_scan, b_s0 = _scan_fwd(_rwkv_chunk, b_ins, b_offs, B_HEADS // 2, nc, (2 * B_N, 2 * B_N), "b_scan")

    out_gain_t = jnp.tile(w['a_out_norm'], (1, A_HEADS))
    r_k = w['b_r_k'].reshape(1, D)
    post_params = [out_gain_t, w['b_ln_gain'], w['b_ln_bias'], r_k]
    post_ins = [o_scan, z, y_scan, zf_rkv, kmod, zf_rkv, bgate, z, z] + post_params
    post_specs = ([_row_spec(tmr, D), _row_spec(tmr, D, 3), _row_spec(tmr, D), _row_spec(tmr, D, 0), _row_spec(tmr, D),
                   _row_spec(tmr, D, 2), _row_spec(tmr, D), _row_spec(tmr, D, 7), _row_spec(tmr, D, 8)]
                  + [_full_spec((1, D))] * 4)
    merged = _tw_fwd(_f_mix_post, post_ins, post_specs, [jax.ShapeDtypeStruct((t, D), f32)], [_row_spec(tmr, D)],
                     (t // tmr,), "mix_post")[0]
    h2 = _matmul(merged, w['w_out'], res=h1, name="out_proj")
    h3, ffn2_saved = _ffn_fwd(h2, w['ffn2_norm'], w['ffn2_wg'], w['ffn2_wu'], w['ffn2_wd'], "ffn2")

    tml = _tile(t, 416, 8)
    fnorm = w['final_norm']
    loss_fn = functools.partial(_f_loss, tm=tml)
    loss_specs = [_row_spec(tml, D), _full_spec((1, D)), _row_spec(tml, D)]
    loss_parts = _tw_fwd(loss_fn, [h3, fnorm, tgt], loss_specs, [jax.ShapeDtypeStruct((t // tml, 1, 1), f32)],
                         [pl.BlockSpec((None, 1, 1), lambda i: (i, 0, 0))], (t // tml,), "loss", with_pid=True)[0]
    loss = jnp.sum(loss_parts)
    ones = jnp.ones((t // tml, 1, 1), f32)
    d_h3, grads['final_norm'] = _tw_bwd(loss_fn, [h3, fnorm, tgt], loss_specs, [ones],
                                        [pl.BlockSpec((None, 1, 1), lambda i: (i, 0, 0))], ['tile', 'acc', None],
                                        (t // tml,), "loss_bwd", with_pid=True)

    d_hn, grads['ffn2_norm'], grads['ffn2_wg'], grads['ffn2_wu'], grads['ffn2_wd'] = _ffn_bwd(
        h2, w['ffn2_norm'], w['ffn2_wg'], w['ffn2_wu'], w['ffn2_wd'], ffn2_saved, d_h3, "ffn2")
    d_h2 = _add(d_h3, d_hn, "add_h2")
    grads['w_out'] = _matmul(merged, d_h2, ta=True, name="d_w_out")
    d_merged = _matmul(d_h2, w['w_out'], tb=True, name="d_merged")

    win = ('tile', (t, D), _row_spec(tmr, D))
    post_kinds = ['tile', win, 'tile', win, 'tile', win, 'tile', win, win] + ['acc'] * 4
    (d_o, d_az, d_y, d_r1, d_kmod1, d_v1, d_bgate, d_ga, d_gb,
     d_out_gain_t, grads['b_ln_gain'], grads['b_ln_bias'], d_r_k) = _tw_bwd(
        _f_mix_post, post_ins, post_specs, [d_merged], [_row_spec(tmr, D)], post_kinds, (t // tmr,), "mix_post_bwd")
    grads['a_out_norm'] = jnp.sum(d_out_gain_t.reshape(A_HEADS, A_DK), axis=0, keepdims=True)
    grads['b_r_k'] = d_r_k.reshape(1, B_HEADS, B_N)

    d_r2, d_kmod2, d_v2, d_as, d_bs, d_lw = _scan_bwd(_rwkv_chunk, b_s0, b_ins, b_offs, d_y, B_HEADS // 2, nc,
                                                      (2 * B_N, 2 * B_N), "b_scan_bwd")
    d_kmod = _add(d_kmod1, d_kmod2, "add_kmod")
    pre_kinds = [win] + ['tile'] * 3 + ['acc'] * 7
    pre_ct_specs = [_row_spec(tmr, D)] * 5
    (d_zf_k, d_wdf, d_adf, d_gdf, grads['b_w0'], grads['b_w_up'], grads['b_a0'], grads['b_a_up'], grads['b_g_up'],
     grads['b_k_k'], grads['b_k_a']) = _tw_bwd(
        _f_rwkv_pre, pre_ins, pre_specs, [d_lw, d_kmod, d_as, d_bs, d_bgate], pre_ct_specs, pre_kinds, (t // tmr,),
        "b_pre_bwd")
    d_zf_rkv = _assemble3(d_r1, d_r2, d_zf_k, d_v1, d_v2, "b_dzf")
    d_zb_rkv, d_mu_rkv = _tw_bwd(_f_tshift, [z, mu_rkv], [_col_spec(t, 32), pl.BlockSpec((1, LANES), lambda j: (0, j))],
                                 [d_zf_rkv], [_col_spec(t, 0)], [('tile', (t, 3072), _col_spec(t, 0)), 'tile'], (24,),
                                 "b_shift_bwd")
    d_zf_s = jnp.concatenate([d_wdf, d_adf, d_gdf], axis=1)
    d_zs_b, d_mu_s = _tw_bwd(_f_tshift, [zs_b, mu_s], [_full_spec((t, 288)), _full_spec((1, 288))], [d_zf_s],
                             [_full_spec((t, 288))], ['tile', 'tile'], (1,), "b_shift_s_bwd")
    grads['b_shift_mu'] = jnp.concatenate([d_mu_rkv, d_mu_s], axis=1)

    d_aq, d_ak, d_av, d_beta_h, d_g_h = _scan_bwd(_delta_chunk, a_s0, a_ins, a_offs, d_o, A_HEADS, nc, (A_DK, A_DK),
                                                  "a_scan_bwd")
    d_beta = d_beta_h[:, :, 0].T
    d_gdec = d_g_h[:, :, 0].T
    d_abeta, d_aalpha, grads['a_log_rate'], grads['a_dt_bias'] = _tw_bwd(
        dg_fn, [abeta, aalpha, w['a_log_rate'], w['a_dt_bias']], dg_specs, [d_beta, d_gdec],
        [_row_spec(tmg, A_HEADS)] * 2, ['tile', 'tile', 'acc', 'acc'], (t // tmg,), "a_gates_bwd", with_pid=True)
    d_zqkv, d_conv = [], []
    for idx, (fn, ct) in enumerate(zip(conv_fns, (d_aq, d_ak, d_av))):
        dz_i, dw_i = _conv_bwd(fn, z, conv_w, ct, idx, t)
        d_zqkv.append(dz_i)
        d_conv.append(dw_i)
    grads['a_conv_w'] = jnp.concatenate(d_conv, axis=1)

    d_z = jnp.concatenate(
        d_zqkv + [d_az, d_zb_rkv, d_ga, d_gb, d_zs_b, d_abeta, d_aalpha, jnp.zeros((t, ZP - 9216 - 304), f32)], axis=1)
    grads['w_in_p'] = _matmul(u, d_z, ta=True, name="d_w_in")
    d_u = _matmul(d_z, w['w_in_p'], tb=True, name="d_u")
    d_h1n, grads['mix_norm'] = _rms_bwd(h1, w['mix_norm'], d_u, "mix_drms")
    d_h1 = _add(d_h2, d_h1n, "add_h1")
    d_h0n, grads['ffn1_norm'], grads['ffn1_wg'], grads['ffn1_wu'], grads['ffn1_wd'] = _ffn_bwd(
        h0, w['ffn1_norm'], w['ffn1_wg'], w['ffn1_wu'], w['ffn1_wd'], ffn1_saved, d_h1, "ffn1")
    d_h0 = _add(d_h1, d_h0n, "add_h0")
    return loss, d_h0, grads


_WIN_SEGMENTS = ((0, 4096), (4112, 7184), (7472, 9520), (7184, 7472), (4096, 4112))


def _win_to_padded(w_in):
    parts = [w_in[:, a:b] for a, b in _WIN_SEGMENTS]
    parts.append(jnp.zeros((w_in.shape[0], ZP - IN_TOTAL), w_in.dtype))
    return jnp.concatenate(parts, axis=1)


def _win_from_padded(w_p):
    widths = [b - a for a, b in _WIN_SEGMENTS]
    offs = [sum(widths[:i]) for i in range(len(widths))]
    seg = {a: w_p[:, o:o + wd] for (a, _), o, wd in zip(_WIN_SEGMENTS, offs, widths)}
    return jnp.concatenate([seg[a] for a in sorted(seg)], axis=1)


def _add(a, b, name):
    t, c = a.shape
    tm = _tile(t, 416, 8)
    return _tw_fwd(lambda x, y: (x + y,), [a, b], [_row_spec(tm, c)] * 2, [jax.ShapeDtypeStruct(a.shape, f32)],
                   [_row_spec(tm, c)], (t // tm,), name)[0]


def _assemble3(d_r1, d_r2, d_k, d_v1, d_v2, name):
    t = d_r1.shape[0]
    tm = _tile(t, 208, 8)

    def body(r1, r2, kk, v1, v2, o_ref):
        o_ref[:, 0:D] = r1[...] + r2[...]
        o_ref[:, D:2 * D] = kk[...]
        o_ref[:, 2 * D:3 * D] = v1[...] + v2[...]

    return pl.pallas_call(body, name=name, grid=(t // tm,), in_specs=[_row_spec(tm, D)] * 5,
                          out_specs=_row_spec(tm, 3 * D), out_shape=jax.ShapeDtypeStruct((t, 3 * D), f32),
                          )(d_r1, d_r2, d_k, d_v1, d_v2)


def _conv_bwd(fn, z, conv_w, ct, idx, t):
    def body(z_ref, w_ref, ct_ref, dz_ref, dw_ref):
        _, vjp = jax.vjp(lambda a, b: fn(a, b), z_ref[...], w_ref[...])
        dz, dw = vjp((ct_ref[...],))
        dz_ref[...] = dz
        dw_ref[...] = dw

    return pl.pallas_call(
        body, name=f"a_conv{idx}_bwd", grid=(A_HEADS,),
        in_specs=[_col_spec(t, 8 * idx), pl.BlockSpec((4, LANES), lambda j, o=8 * idx: (0, j + o)), _col_spec(t, 0)],
        out_specs=[_col_spec(t, 0), pl.BlockSpec((4, LANES), lambda j: (0, j))],
        out_shape=[jax.ShapeDtypeStruct((t, D), f32), jax.ShapeDtypeStruct((4, D), f32)],
    )(z, conv_w, ct)


def _position():
    return lax.axis_index("x"), lax.axis_index("y"), lax.axis_index("c")


def _flip(v, f):
    return 1 - v if f else v


_CHIP_FLIPS = ((1, 0), (0, 1), (1, 1))
_DEV_FLIPS = tuple((fx, fy, fc) for fx in (0, 1) for fy in (0, 1) for fc in (0, 1) if (fx, fy, fc) != (0, 0, 0))


def _gather_chips(arrs, name):
    n = len(arrs)

    def body(*refs):
        ins, outs = refs[:n], refs[n:2 * n]
        send, recv, loc = refs[2 * n:]
        x, y, c = _position()
        me = 2 * x + y
        started = []
        for a in range(n):
            lc = pltpu.make_async_copy(ins[a], outs[a].at[me], loc.at[a])
            lc.start()
            started.append(lc)
        sends, recvs = [], []
        for a in range(n):
            for j, (fx, fy) in enumerate(_CHIP_FLIPS):
                px, py = _flip(x, fx), _flip(y, fy)
                cp = pltpu.make_async_remote_copy(src_ref=ins[a], dst_ref=outs[a].at[me], send_sem=send.at[a, j],
                                                  recv_sem=recv.at[a, j], device_id=(px, py, c), device_id_type=MESH)
                cp.start()
                sends.append(cp)
                recvs.append(pltpu.make_async_remote_copy(
                    src_ref=ins[a], dst_ref=outs[a].at[2 * px + py], send_sem=send.at[a, j], recv_sem=recv.at[a, j],
                    device_id=(px, py, c), device_id_type=MESH))
        for cp in recvs:
            cp.wait_recv()
        for cp in sends:
            cp.wait_send()
        for lc in started:
            lc.wait()

    return pl.pallas_call(
        body, name=name, in_specs=[ANY] * n, out_specs=[ANY] * n,
        out_shape=[jax.ShapeDtypeStruct((N_CHIPS,) + a.shape, a.dtype) for a in arrs],
        scratch_shapes=[pltpu.SemaphoreType.DMA((n, 3)), pltpu.SemaphoreType.DMA((n, 3)), pltpu.SemaphoreType.DMA((n,))],
    )(*arrs)


def _scatter_chips(g, name):
    def body(g_ref, out_ref, send, recv, loc):
        x, y, c = _position()
        me = 2 * x + y
        lc = pltpu.make_async_copy(g_ref.at[me], out_ref.at[me], loc)
        lc.start()
        sends, recvs = [], []
        for j, (fx, fy) in enumerate(_CHIP_FLIPS):
            px, py = _flip(x, fx), _flip(y, fy)
            p = 2 * px + py
            cp = pltpu.make_async_remote_copy(src_ref=g_ref.at[p], dst_ref=out_ref.at[me], send_sem=send.at[j],
                                              recv_sem=recv.at[j], device_id=(px, py, c), device_id_type=MESH)
            cp.start()
            sends.append(cp)
            recvs.append(pltpu.make_async_remote_copy(src_ref=g_ref.at[me], dst_ref=out_ref.at[p], send_sem=send.at[j],
                                                      recv_sem=recv.at[j], device_id=(px, py, c), device_id_type=MESH))
        for cp in recvs:
            cp.wait_recv()
        for cp in sends:
            cp.wait_send()
        lc.wait()

    return pl.pallas_call(
        body, name=name, in_specs=[ANY], out_specs=ANY, out_shape=jax.ShapeDtypeStruct(g.shape, g.dtype),
        scratch_shapes=[pltpu.SemaphoreType.DMA((3,)), pltpu.SemaphoreType.DMA((3,)), pltpu.SemaphoreType.DMA(())],
    )(g)


def _swap_sibling(p, name):
    def body(p_ref, out_ref, send, recv):
        x, y, c = _position()
        cp = pltpu.make_async_remote_copy(src_ref=p_ref, dst_ref=out_ref, send_sem=send, recv_sem=recv,
                                          device_id=(x, y, 1 - c), device_id_type=MESH)
        cp.start()
        cp.wait()

    return pl.pallas_call(
        body, name=name, in_specs=[ANY], out_specs=ANY, out_shape=jax.ShapeDtypeStruct(p.shape, p.dtype),
        scratch_shapes=[pltpu.SemaphoreType.DMA(()), pltpu.SemaphoreType.DMA(())],
    )(p)


def _gather_devices(s, name):
    def body(s_ref, out_ref, send, recv, loc):
        x, y, c = _position()
        me = 4 * x + 2 * y + c
        lc = pltpu.make_async_copy(s_ref, out_ref.at[me], loc)
        lc.start()
        sends, recvs = [], []
        for j, (fx, fy, fc) in enumerate(_DEV_FLIPS):
            px, py, pc = _flip(x, fx), _flip(y, fy), _flip(c, fc)
            cp = pltpu.make_async_remote_copy(src_ref=s_ref, dst_ref=out_ref.at[me], send_sem=send.at[j],
                                              recv_sem=recv.at[j], device_id=(px, py, pc), device_id_type=MESH)
            cp.start()
            sends.append(cp)
            recvs.append(pltpu.make_async_remote_copy(
                src_ref=s_ref, dst_ref=out_ref.at[4 * px + 2 * py + pc], send_sem=send.at[j], recv_sem=recv.at[j],
                device_id=(px, py, pc), device_id_type=MESH))
        for cp in recvs:
            cp.wait_recv()
        for cp in sends:
            cp.wait_send()
        lc.wait()

    return pl.pallas_call(
        body, name=name, in_specs=[ANY], out_specs=ANY, out_shape=jax.ShapeDtypeStruct((N_DEV,) + s.shape, s.dtype),
        scratch_shapes=[pltpu.SemaphoreType.DMA((7,)), pltpu.SemaphoreType.DMA((7,)), pltpu.SemaphoreType.DMA(())],
    )(s)


def _sum_slots(a, name):
    s, r, c = a.shape
    tr = _tile(r, 2048, 8)

    def body(a_ref, o_ref):
        acc = a_ref[0]
        for i in range(1, s):
            acc = acc + a_ref[i]
        o_ref[...] = acc

    return pl.pallas_call(body, name=name, grid=(r // tr,), in_specs=[pl.BlockSpec((s, tr, c), lambda i: (0, i, 0))],
                          out_specs=pl.BlockSpec((tr, c), lambda i: (i, 0)),
                          out_shape=jax.ShapeDtypeStruct((r, c), f32))(a)


def _adamw(w, g_parts, m, v, name):
    shape = w.shape
    size = w.size
    view = (size // LANES, LANES) if size % LANES == 0 else (1, size)
    rows = view[0]
    tr = _tile(rows, 2048, 8) if rows > 2048 else rows
    n_g = len(g_parts)

    def body(*refs):
        w_ref = refs[0]
        g_refs = refs[1:1 + n_g]
        m_ref, v_ref, g_out, d_out, m_out, v_out = refs[1 + n_g:]
        g = g_refs[0][...]
        for gr in g_refs[1:]:
            g = g + gr[...]
        m_new = ADAM_B1 * m_ref[...] + (1.0 - ADAM_B1) * g
        v_new = ADAM_B2 * v_ref[...] + (1.0 - ADAM_B2) * (g * g)
        m_hat = m_new / (1.0 - ADAM_B1 ** ADAM_STEP)
        v_hat = v_new / (1.0 - ADAM_B2 ** ADAM_STEP)
        g_out[...] = g
        d_out[...] = -ADAM_LR * (m_hat / (jnp.sqrt(v_hat) + ADAM_EPS) + ADAM_WD * w_ref[...])
        m_out[...] = m_new
        v_out[...] = v_new

    spec = pl.BlockSpec((tr, view[1]), lambda i: (i, 0))
    args = [w.reshape(view)] + [g.reshape(view) for g in g_parts] + [m.reshape(view), v.reshape(view)]
    outs = pl.pallas_call(body, name=name, grid=(rows // tr,), in_specs=[spec] * len(args), out_specs=[spec] * 4,
                          out_shape=[jax.ShapeDtypeStruct(view, f32)] * 4)(*args)
    return [o.reshape(shape) for o in outs]


_BIG = ('ffn1_w_gu', 'ffn1_w_down', 'w_in', 'w_out', 'ffn2_w_gu', 'ffn2_w_down')
_SMALL_SHARDED = ('meta_tokens', 'a_conv_w', 'b_w_up', 'b_a_up', 'b_g_up')
_WEIGHTS = ('meta_tokens', 'ffn1_norm', 'ffn1_w_gu', 'ffn1_w_down', 'mix_norm', 'w_in', 'a_conv_w', 'a_log_rate',
            'a_dt_bias', 'a_out_norm', 'b_shift_mu', 'b_w0', 'b_w_up', 'b_a0', 'b_a_up', 'b_g_up', 'b_k_k', 'b_k_a',
            'b_r_k', 'b_ln_gain', 'b_ln_bias', 'w_out', 'ffn2_norm', 'ffn2_w_gu', 'ffn2_w_down', 'final_norm')
_SMALL = tuple(n for n in _WEIGHTS if n not in _BIG)


def _pack(arrs, dtype, row_mult=8):
    flat = jnp.concatenate([a.reshape(-1).astype(dtype) for a in arrs])
    rows = -(-flat.size // LANES)
    rows = -(-rows // row_mult) * row_mult
    flat = jnp.pad(flat, (0, rows * LANES - flat.size))
    return flat.reshape(rows, LANES)


def _unpack(packed, shapes, lead=()):
    flat = packed.reshape(lead + (-1,))
    out, off = [], 0
    for sh in shapes:
        n = 1
        for d in sh:
            n *= d
        out.append(flat[..., off:off + n].reshape(lead + tuple(sh)))
        off += n
    return out


def _cols_from_shards(s):
    return jnp.concatenate([s[i] for i in range(N_CHIPS)], axis=-1)


def _cols_to_shards(a):
    r, c = a.shape
    return a.reshape(r, N_CHIPS, c // N_CHIPS).transpose(1, 0, 2)


def kernel(x, meta_tokens, ffn1_norm, ffn1_w_gu, ffn1_w_down, mix_norm, w_in, a_conv_w, a_log_rate, a_dt_bias, a_out_norm, b_shift_mu, b_w0, b_w_up, b_a0, b_a_up, b_g_up, b_k_k, b_k_a, b_r_k, b_ln_gain, b_ln_bias, w_out, ffn2_norm, ffn2_w_gu, ffn2_w_down, final_norm, loss_target, m_meta_tokens, m_ffn1_norm, m_ffn1_w_gu, m_ffn1_w_down, m_mix_norm, m_w_in, m_a_conv_w, m_a_log_rate, m_a_dt_bias, m_a_out_norm, m_b_shift_mu, m_b_w0, m_b_w_up, m_b_a0, m_b_a_up, m_b_g_up, m_b_k_k, m_b_k_a, m_b_r_k, m_b_ln_gain, m_b_ln_bias, m_w_out, m_ffn2_norm, m_ffn2_w_gu, m_ffn2_w_down, m_final_norm, v_meta_tokens, v_ffn1_norm, v_ffn1_w_gu, v_ffn1_w_down, v_mix_norm, v_w_in, v_a_conv_w, v_a_log_rate, v_a_dt_bias, v_a_out_norm, v_b_shift_mu, v_b_w0, v_b_w_up, v_b_a0, v_b_a_up, v_b_g_up, v_b_k_k, v_b_k_a, v_b_r_k, v_b_ln_gain, v_b_ln_bias, v_w_out, v_ffn2_norm, v_ffn2_w_gu, v_ffn2_w_down, v_final_norm):
    args = locals()
    wts = {n: args[n] for n in _WEIGHTS}
    mom = {n: args["m_" + n] for n in _WEIGHTS}
    var = {n: args["v_" + n] for n in _WEIGHTS}
    chip = 2 * lax.axis_index("x") + lax.axis_index("y")

    big_shapes = [wts[n].shape[1:] for n in _BIG]
    small_shapes = [wts[n].shape[-2:] for n in _SMALL_SHARDED]
    big_packed = _pack([wts[n] for n in _BIG], bf16)
    small_packed = _pack([wts[n] for n in _SMALL_SHARDED], f32)
    big_all, small_all = _gather_chips([big_packed, small_packed], "gather_weights")
    gu1, dn1, w_in_s, w_out_s, gu2, dn2 = _unpack(big_all, big_shapes, (N_CHIPS,))
    meta_s, conv_s, wup_s, aup_s, gup_s = _unpack(small_all, small_shapes, (N_CHIPS,))
    w = {
        'ffn1_norm': ffn1_norm, 'mix_norm': mix_norm, 'ffn2_norm': ffn2_norm, 'final_norm': final_norm[None, :],
        'ffn1_wg': jnp.concatenate([gu1[0], gu1[1]], axis=1), 'ffn1_wu': jnp.concatenate([gu1[2], gu1[3]], axis=1),
        'ffn1_wd': dn1.reshape(D_FF, D),
        'ffn2_wg': jnp.concatenate([gu2[0], gu2[1]], axis=1), 'ffn2_wu': jnp.concatenate([gu2[2], gu2[3]], axis=1),
        'ffn2_wd': dn2.reshape(D_FF, D),
        'w_in_p': _win_to_padded(_cols_from_shards(w_in_s)), 'w_out': w_out_s.reshape(D, D),
        'a_conv_w': _cols_from_shards(conv_s), 'b_w_up': _cols_from_shards(wup_s), 'b_a_up': _cols_from_shards(aup_s),
        'b_g_up': _cols_from_shards(gup_s),
        'a_log_rate': a_log_rate, 'a_dt_bias': a_dt_bias, 'a_out_norm': a_out_norm, 'b_shift_mu': b_shift_mu,
        'b_w0': b_w0, 'b_a0': b_a0, 'b_k_k': b_k_k, 'b_k_a': b_k_a, 'b_r_k': b_r_k, 'b_ln_gain': b_ln_gain,
        'b_ln_bias': b_ln_bias,
    }
    meta_full = _cols_from_shards(meta_s)

    h0 = jnp.concatenate([jnp.zeros((PAD, D), f32), meta_full, x[0]], axis=0)
    tgt = jnp.concatenate([jnp.zeros((SKIP, D), f32), loss_target[0]], axis=0)
    loss_local, d_h0, g = _local_step(h0, tgt, w)
    loss = lax.psum(loss_local, ("x", "y", "c"))
    grad_x = d_h0[SKIP:][None]

    big_grads = [
        _cols_to_shards(jnp.concatenate([g['ffn1_wg'], g['ffn1_wu']], axis=1)),
        g['ffn1_wd'].reshape(N_CHIPS, D_FF // N_CHIPS, D),
        _cols_to_shards(_win_from_padded(g['w_in_p'])),
        g['w_out'].reshape(N_CHIPS, D // N_CHIPS, D),
        _cols_to_shards(jnp.concatenate([g['ffn2_wg'], g['ffn2_wu']], axis=1)),
        g['ffn2_wd'].reshape(N_CHIPS, D_FF // N_CHIPS, D),
    ]
    g_packed = jnp.concatenate([a.reshape(N_CHIPS, -1, LANES) for a in big_grads], axis=1)
    assert g_packed.shape[1] == big_packed.shape[0]
    mine = _sum_slots(_scatter_chips(g_packed, "scatter_grads"), "sum_chips")
    theirs = _swap_sibling(mine, "swap_sibling")
    mine_parts = _unpack(mine, big_shapes)
    their_parts = _unpack(theirs, big_shapes)

    small_full = {
        'meta_tokens': d_h0[PAD:SKIP], 'ffn1_norm': g['ffn1_norm'], 'mix_norm': g['mix_norm'], 'a_conv_w': g['a_conv_w'],
        'a_log_rate': g['a_log_rate'], 'a_dt_bias': g['a_dt_bias'], 'a_out_norm': g['a_out_norm'],
        'b_shift_mu': g['b_shift_mu'], 'b_w0': g['b_w0'], 'b_w_up': g['b_w_up'], 'b_a0': g['b_a0'], 'b_a_up': g['b_a_up'],
        'b_g_up': g['b_g_up'], 'b_k_k': g['b_k_k'], 'b_k_a': g['b_k_a'], 'b_r_k': g['b_r_k'], 'b_ln_gain': g['b_ln_gain'],
        'b_ln_bias': g['b_ln_bias'], 'ffn2_norm': g['ffn2_norm'], 'final_norm': g['final_norm'],
    }
    s_shapes = [small_full[n].shape for n in _SMALL]
    s_sum = _sum_slots(_gather_devices(_pack([small_full[n] for n in _SMALL], f32, row_mult=256), "gather_small"),
                       "sum_small")
    s_parts = dict(zip(_SMALL, _unpack(s_sum, s_shapes)))

    grad, delta, new_m, new_v = {}, {}, {}, {}
    for n, a, b in zip(_BIG, mine_parts, their_parts):
        sh = wts[n].shape
        grad[n], delta[n], new_m[n], new_v[n] = _adamw(wts[n], [a.reshape(sh), b.reshape(sh)], mom[n], var[n],
                                                       f"adamw_{n}")
    for n in _SMALL:
        gs = s_parts[n]
        if n in _SMALL_SHARDED:
            width = wts[n].shape[-1]
            gs = lax.dynamic_slice_in_dim(gs, chip * width, width, axis=gs.ndim - 1)
        gs = gs.reshape(wts[n].shape)
        grad[n], delta[n], new_m[n], new_v[n] = _adamw(wts[n], [gs], mom[n], var[n], f"adamw_{n}")

    return (loss, grad_x, *[grad[n] for n in _WEIGHTS], *[delta[n] for n in _WEIGHTS],
            *[new_m[n] for n in _WEIGHTS], *[new_v[n] for n in _WEIGHTS])
```

```python
import functools

import jax
import jax.numpy as jnp
from jax import lax
from jax.experimental import pallas as pl
from jax.experimental.pallas import tpu as pltpu

f32 = jnp.float32
bf16 = jnp.bfloat16
HI = lax.Precision.HIGHEST
MESH = pl.DeviceIdType.MESH
ANY = pl.BlockSpec(memory_space=pl.ANY)

D = 1024
N_META = 16
CHUNK = 64
PAD = CHUNK - N_META
SKIP = PAD + N_META
EPS = 1e-6
D_FF = 2816
A_HEADS = 8
A_DK = 128
B_HEADS = 16
B_N = 64
B_GN_EPS = B_N * 1e-5
W_LORA, AA_LORA, G_LORA = 64, 64, 160
IN_TOTAL = 9520
ZP = 9600
LANES = 128
N_CHIPS = 4
N_DEV = 8

ADAM_LR, ADAM_B1, ADAM_B2, ADAM_EPS, ADAM_WD, ADAM_STEP = 0.001, 0.9, 0.999, 1e-08, 0.01, 10

MXU_DTYPE = bf16


def _tile(n, cap, mult):
    if n <= cap:
        return n
    best = None
    for t in range(mult, cap + 1, mult):
        if n % t == 0:
            best = t
    assert best is not None, (n, cap, mult)
    return best


def _sigmoid(x):
    return jax.nn.sigmoid(x)


def _silu(x):
    return x * jax.nn.sigmoid(x)


def _softplus(x):
    return jnp.maximum(x, 0.0) + jnp.log(1.0 + jnp.exp(-jnp.abs(x)))


def _head_matrix(c, nh):
    hd = c // nh
    r = lax.broadcasted_iota(jnp.int32, (c, nh), 0)
    h = lax.broadcasted_iota(jnp.int32, (c, nh), 1)
    return (r >= h * hd) & (r < (h + 1) * hd)


def _dot_exact_rhs(x, e, cb):
    dn = (((1,), (cb,)), ((), ()))
    if SCAN_PASSES == 0:
        return lax.dot_general(x, e.astype(f32), dn, precision=HI, preferred_element_type=f32)
    eb = e.astype(bf16)
    hi = x.astype(bf16)
    lo = (x - hi.astype(f32)).astype(bf16)
    return (lax.dot_general(hi, eb, dn, preferred_element_type=f32)
            + lax.dot_general(lo, eb, dn, preferred_element_type=f32))


def _head_sum_impl(x, nh):
    e = _head_matrix(x.shape[-1], nh)
    return _dot_exact_rhs(_dot_exact_rhs(x, e, 0), e, 1)


@functools.partial(jax.custom_vjp, nondiff_argnums=(1,))
def _head_sum(x, nh):
    return _head_sum_impl(x, nh)


def _head_sum_fwd(x, nh):
    return _head_sum_impl(x, nh), None


def _head_sum_bwd(nh, _, g):
    return (_head_sum_impl(g, nh),)


_head_sum.defvjp(_head_sum_fwd, _head_sum_bwd)


@functools.partial(jax.custom_vjp, nondiff_argnums=(1,))
def _shift_rows(x, s):
    n = x.shape[0]
    row = lax.broadcasted_iota(jnp.int32, x.shape, 0)
    if s > 0:
        return jnp.where(row >= s, pltpu.roll(x, s, 0), 0.0)
    return jnp.where(row < n + s, pltpu.roll(x, n + s, 0), 0.0)


def _shift_rows_fwd(x, s):
    return _shift_rows(x, s), None


def _shift_rows_bwd(s, _, g):
    return (_shift_rows(g, -s),)


_shift_rows.defvjp(_shift_rows_fwd, _shift_rows_bwd)


def _matmul(a, b, *, ta=False, tb=False, res=None, scale=1.0, name):
    assert not (ta and tb)
    (ar, ac), (br, bc) = a.shape, b.shape
    m, k = (ac, ar) if ta else (ar, ac)
    n, kb = (br, bc) if tb else (bc, br)
    assert k == kb, (a.shape, b.shape, ta, tb)
    tm = _tile(m, 1408, LANES) if ta else _tile(m, 832, 8)
    tn = _tile(n, 1408, LANES)
    tk = _tile(k, 1040, 8) if ta else _tile(k, 1408, LANES)
    nk = k // tk
    dn = (((0 if ta else 1,), (1 if tb else 0,)), ((), ()))

    def body(*refs):
        if res is not None:
            a_ref, b_ref, r_ref, o_ref, acc = refs
        else:
            a_ref, b_ref, o_ref, acc = refs
        kk = pl.program_id(2)

        @pl.when(kk == 0)
        def _():
            acc[...] = jnp.zeros_like(acc)

        acc[...] += lax.dot_general(a_ref[...].astype(MXU_DTYPE), b_ref[...].astype(MXU_DTYPE), dn,
                                    preferred_element_type=f32,
                                    precision=None if MXU_DTYPE == bf16 else HI)

        @pl.when(kk == nk - 1)
        def _():
            out = acc[...]
            if scale != 1.0:
                out = out * scale
            if res is not None:
                out = r_ref[...] + out
            o_ref[...] = out

    if ta:
        a_spec = pl.BlockSpec((tk, tm), lambda i, j, kk: (kk, i))
    else:
        a_spec = pl.BlockSpec((tm, tk), lambda i, j, kk: (i, kk))
    if tb:
        b_spec = pl.BlockSpec((tn, tk), lambda i, j, kk: (j, kk))
    else:
        b_spec = pl.BlockSpec((tk, tn), lambda i, j, kk: (kk, j))
    in_specs = [a_spec, b_spec]
    args = [a, b]
    if res is not None:
        in_specs.append(pl.BlockSpec((tm, tn), lambda i, j, kk: (i, j)))
        args.append(res)
    return pl.pallas_call(
        body, name=name, grid=(m // tm, n // tn, nk), in_specs=in_specs,
        out_specs=pl.BlockSpec((tm, tn), lambda i, j, kk: (i, j)),
        out_shape=jax.ShapeDtypeStruct((m, n), f32),
        scratch_shapes=[pltpu.VMEM((tm, tn), f32)],
        compiler_params=pltpu.CompilerParams(dimension_semantics=("parallel", "parallel", "arbitrary")),
    )(*args)


def _tw_fwd(fn, ins, in_specs, out_shapes, out_specs, grid, name, with_pid=False):
    n_in = len(ins)

    def body(*refs):
        vals = [r[...] for r in refs[:n_in]]
        outs = fn(pl.program_id(0), *vals) if with_pid else fn(*vals)
        for r, o in zip(refs[n_in:], outs):
            r[...] = o

    return pl.pallas_call(body, name=name, grid=grid, in_specs=in_specs, out_specs=out_specs,
                          out_shape=out_shapes)(*ins)


def _tw_bwd(fn, ins, in_specs, cts, ct_specs, kinds, grid, name, with_pid=False):
    n_in, n_ct = len(ins), len(cts)
    diff = [i for i, kd in enumerate(kinds) if kd is not None]

    def body(*refs):
        vals = [r[...] for r in refs[:n_in]]
        ctv = tuple(r[...] for r in refs[n_in:n_in + n_ct])
        g_refs = refs[n_in + n_ct:]
        pid = pl.program_id(0)

        def f(*dv):
            full = list(vals)
            for i, v in zip(diff, dv):
                full[i] = v
            out = fn(pid, *full) if with_pid else fn(*full)
            return tuple(out)

        _, vjp = jax.vjp(f, *[vals[i] for i in diff])
        gs = vjp(ctv)
        first = pid == 0
        for i2 in range(1, len(grid)):
            first = first & (pl.program_id(i2) == 0)
        for i, g, g_ref in zip(diff, gs, g_refs):
            if kinds[i] != 'acc':
                g_ref[...] = g
            else:
                @pl.when(first)
                def _(g=g, g_ref=g_ref):
                    g_ref[...] = g

                @pl.when(jnp.logical_not(first))
                def _(g=g, g_ref=g_ref):
                    g_ref[...] += g

    zero_map = {1: lambda *a: (0,), 2: lambda *a: (0, 0), 3: lambda *a: (0, 0, 0)}
    out_specs, out_shapes = [], []
    for i in diff:
        if kinds[i] == 'tile':
            out_shapes.append(jax.ShapeDtypeStruct(ins[i].shape, f32))
            out_specs.append(in_specs[i])
        elif kinds[i] == 'acc':
            out_shapes.append(jax.ShapeDtypeStruct(ins[i].shape, f32))
            out_specs.append(pl.BlockSpec(ins[i].shape, zero_map[ins[i].ndim]))
        else:
            out_shapes.append(jax.ShapeDtypeStruct(kinds[i][1], f32))
            out_specs.append(kinds[i][2])
    return pl.pallas_call(body, name=name, grid=grid, in_specs=list(in_specs) + list(ct_specs),
                          out_specs=out_specs, out_shape=out_shapes)(*ins, *cts)


def _row_spec(tm, c, col_block=0):
    return pl.BlockSpec((tm, c), lambda i, cb=col_block: (i, cb))


def _full_spec(shape):
    nd = len(shape)
    return pl.BlockSpec(shape, lambda *a, nd=nd: (0,) * nd)


def _f_rms(x, g):
    return (x * lax.rsqrt(jnp.mean(x * x, axis=-1, keepdims=True) + EPS) * g,)


def _f_swiglu(gate, up):
    return (_silu(gate) * up,)


def _f_loss(pid, h, g, tgt, *, tm):
    y = h * lax.rsqrt(jnp.mean(h * h, axis=-1, keepdims=True) + EPS) * g
    row = pid * tm + lax.broadcasted_iota(jnp.int32, (tm, 1), 0)
    err = jnp.where(row >= SKIP, y - tgt, 0.0)
    per_row = jnp.mean(err * err, axis=-1, keepdims=True)
    return (0.5 * jnp.sum(per_row, axis=0, keepdims=True),)


def _f_conv(x, w, *, norm, scale):
    y = x * w[3:4, :]
    for s in (1, 2, 3):
        y = y + _shift_rows(x, s) * w[3 - s:4 - s, :]
    y = _silu(y)
    if norm:
        y = y * lax.rsqrt(jnp.sum(y * y, axis=-1, keepdims=True) + 1e-6) * scale
    return (y,)


def _f_dgates(pid, abeta, aalpha, log_rate, dt_bias, *, tm):
    row = pid * tm + lax.broadcasted_iota(jnp.int32, (tm, 1), 0)
    live = row >= PAD
    beta = jnp.where(live, _sigmoid(abeta), 0.0)
    g = jnp.where(live, -jnp.exp(log_rate) * _softplus(aalpha + dt_bias), 0.0)
    return beta, g


def _f_tshift(z, mu):
    return (z + (_shift_rows(z, 1) - z) * mu,)


def _f_rwkv_pre(k, wd, ad, gd, w0, w_up, a0, a_up, g_up, k_k, k_a):
    w_log = -_softplus(-(w0 + _smm(jnp.tanh(wd), w_up))) - 0.5
    lw = -jnp.exp(w_log)
    a_lr = _sigmoid(a0 + _smm(ad, a_up))
    gate = _smm(_sigmoid(gd), g_up)
    kkp = k * k_k
    kk = kkp * lax.rsqrt(_head_sum(kkp * kkp, B_HEADS) + 1e-6)
    kmod = k * (1.0 + (a_lr - 1.0) * k_a)
    return lw, kmod, -kk, kk * a_lr, gate


def _f_mix_post(o, az, y, r, kmod, v, gate, ga, gb, out_gain, ln_g, ln_b, r_k):
    ms = _head_sum(o * o, A_HEADS) * (1.0 / A_DK)
    oa = o * lax.rsqrt(ms + EPS) * out_gain * _silu(az)
    mean = _head_sum(y, B_HEADS) * (1.0 / B_N)
    yc = y - mean
    var = _head_sum(yc * yc, B_HEADS) * (1.0 / B_N)
    yn = yc * lax.rsqrt(var + B_GN_EPS) * ln_g + ln_b
    bonus = _head_sum(r * kmod * r_k, B_HEADS) * v
    ob = (yn + bonus) * gate
    return (_sigmoid(ga) * oa + _sigmoid(gb) * ob,)


SCAN_PASSES = 3


def _split2(a):
    hi = a.astype(bf16)
    return hi, (a - hi.astype(f32)).astype(bf16)


def _dot_passes(a, b, ca, cb):
    dn = (((ca,), (cb,)), ((), ()))
    if SCAN_PASSES == 0:
        return lax.dot_general(a, b, dn, precision=HI, preferred_element_type=f32)
    if SCAN_PASSES == 1:
        return lax.dot_general(a.astype(bf16), b.astype(bf16), dn, preferred_element_type=f32)
    ah, al = _split2(a)
    bh, bl = _split2(b)
    return (lax.dot_general(ah, bh, dn, preferred_element_type=f32)
            + (lax.dot_general(ah, bl, dn, preferred_element_type=f32)
               + lax.dot_general(al, bh, dn, preferred_element_type=f32)))


@functools.partial(jax.custom_vjp, nondiff_argnums=(2, 3))
def _sdot(a, b, ca, cb):
    return _dot_passes(a, b, ca, cb)


def _sdot_fwd(a, b, ca, cb):
    return _dot_passes(a, b, ca, cb), (a, b)


def _sdot_bwd(ca, cb, res, g):
    a, b = res
    if (ca, cb) == (1, 0):
        return _dot_passes(g, b, 1, 1), _dot_passes(a, g, 0, 0)
    if (ca, cb) == (1, 1):
        return _dot_passes(g, b, 1, 0), _dot_passes(g, a, 0, 0)
    assert (ca, cb) == (0, 0)
    return _dot_passes(b, g, 1, 1), _dot_passes(a, g, 1, 0)


_sdot.defvjp(_sdot_fwd, _sdot_bwd)


def _smm(a, b):
    return _sdot(a, b, 1, 0)


def _smm_nt(a, b):
    return _sdot(a, b, 1, 1)


def _smm_tn(a, b):
    return _sdot(a, b, 0, 0)


def _tri_dot(x, ca):
    n = x.shape[0]
    incl = _tri_masks(n)[0]
    dn = (((ca,), (0,)), ((), ()))
    if SCAN_PASSES == 0:
        return lax.dot_general(incl.astype(f32), x, dn, precision=HI, preferred_element_type=f32)
    tri = incl.astype(bf16)
    hi, r1 = x.astype(bf16), None
    r1 = x - hi.astype(f32)
    mid = r1.astype(bf16)
    lo = (r1 - mid.astype(f32)).astype(bf16)
    return (lax.dot_general(tri, hi, dn, preferred_element_type=f32)
            + (lax.dot_general(tri, mid, dn, preferred_element_type=f32)
               + lax.dot_general(tri, lo, dn, preferred_element_type=f32)))


@jax.custom_vjp
def _cumsum_rows(x):
    return _tri_dot(x, 1)


def _cumsum_rows_fwd(x):
    return _tri_dot(x, 1), None


def _cumsum_rows_bwd(_, g):
    return (_tri_dot(g, 0),)


_cumsum_rows.defvjp(_cumsum_rows_fwd, _cumsum_rows_bwd)


def _tri_masks(n):
    i = lax.broadcasted_iota(jnp.int32, (n, n), 0)
    j = lax.broadcasted_iota(jnp.int32, (n, n), 1)
    return i >= j, i > j, i == j, i <= j


def _unit_lower_inv(low):
    n = low.shape[0]
    assert n == CHUNK
    _, _, eye, _ = _tri_masks(n)
    acc = eye.astype(f32) + low
    p = low
    for _ in range(5):
        p = _smm(p, p)
        acc = acc + _smm(acc, p)
    return acc


def _delta_chunk(s, q, k, v, beta, g):
    incl, strict, eye, upper = _tri_masks(CHUNK)
    g_row = jnp.sum(jnp.where(eye, g, 0.0), axis=0, keepdims=True)
    gc = jnp.sum(jnp.where(incl, g_row, 0.0), axis=1, keepdims=True)
    gc_row = jnp.sum(jnp.where(upper, g, 0.0), axis=0, keepdims=True)
    decay = jnp.where(incl, jnp.exp(jnp.where(incl, gc - gc_row, 0.0)), 0.0)
    kb = k * beta
    vb = v * beta
    m = jnp.where(strict, _smm_nt(kb, k) * decay, 0.0)
    tinv = _unit_lower_inv(-m)
    u = _smm(tinv, vb)
    wk = _smm(tinv, kb * jnp.exp(gc))
    attn = _smm_nt(q, k) * decay
    qg = q * jnp.exp(gc)
    g_last = jnp.sum(g, axis=0, keepdims=True)
    k_tail = k * jnp.exp(g_last - gc)
    v_new = u - _smm(wk, s)
    o = _smm(qg, s) + _smm(attn, v_new)
    s_new = s * jnp.exp(g_last) + _smm_tn(k_tail, v_new)
    return o, s_new


def _rwkv_chunk(st, r, k, v, a, b, lw):
    c = CHUNK
    incl, strict, _, _ = _tri_masks(c)
    lane = lax.broadcasted_iota(jnp.int32, (c, 2 * B_N), 1)
    first = lane < B_N
    bi = lax.broadcasted_iota(jnp.int32, (2 * B_N, 2 * B_N), 0) < B_N
    bj = lax.broadcasted_iota(jnp.int32, (2 * B_N, 2 * B_N), 1) < B_N
    blockdiag = bi == bj
    cum = _cumsum_rows(lw)
    e_pos = jnp.exp(cum)
    e_neg = jnp.exp(-cum)
    rt = r * e_pos
    at = a * jnp.exp(cum - lw)
    kt = k * e_neg
    bt = b * e_neg
    a_s0 = _smm_nt(at, st)
    r_s0 = _smm_nt(rt, st)
    u = jnp.zeros((c, 2 * B_N), f32)
    for sel in (first, jnp.logical_not(first)):
        at_h = jnp.where(sel, at, 0.0)
        ab = jnp.where(strict, _smm_nt(at_h, bt), 0.0)
        ak = jnp.where(strict, _smm_nt(at_h, kt), 0.0)
        t_h = _unit_lower_inv(ab)
        u_h = _smm(t_h, jnp.where(sel, a_s0, 0.0) + _smm(ak, jnp.where(sel, v, 0.0)))
        u = u + u_h
    y = r_s0
    for sel in (first, jnp.logical_not(first)):
        rt_h = jnp.where(sel, rt, 0.0)
        rb = jnp.where(incl, _smm_nt(rt_h, bt), 0.0)
        rk = jnp.where(incl, _smm_nt(rt_h, kt), 0.0)
        y = y + _smm(rb, jnp.where(sel, u, 0.0)) + _smm(rk, jnp.where(sel, v, 0.0))
    cl = jnp.sum(lw, axis=0, keepdims=True)
    dec = jnp.exp(cl - cum)
    st_new = st * jnp.exp(cl) + jnp.where(blockdiag, _smm_tn(u, b * dec) + _smm_tn(v, k * dec), 0.0)
    return y, st_new


GROUPS_PER_STEP = 8


def _scan_specs(ins, col_offs, n_chunks, reverse):
    gw = GROUPS_PER_STEP * LANES
    cidx = (lambda c: n_chunks - 1 - c) if reverse else (lambda c: c)
    specs = []
    for a, off in zip(ins, col_offs):
        if a.ndim == 2:
            assert off % gw == 0
            specs.append(pl.BlockSpec((CHUNK, gw), lambda h, c, o=off // gw: (cidx(c), h + o)))
        else:
            specs.append(pl.BlockSpec((GROUPS_PER_STEP, CHUNK, 1), lambda h, c: (h, cidx(c), 0)))
    return specs, cidx


def _group_vals(refs, g):
    return [r[:, g * LANES:(g + 1) * LANES] if len(r.shape) == 2 else r[g] for r in refs]


def _scan_fwd(chunk_fn, ins, col_offs, n_groups, n_chunks, state_shape, name):
    n_in = len(ins)
    gps = GROUPS_PER_STEP
    t = ins[0].shape[0]

    def body(*refs):
        in_refs = refs[:n_in]
        o_ref, s0_ref, st = refs[n_in:]

        @pl.when(pl.program_id(1) == 0)
        def _():
            st[...] = jnp.zeros_like(st)

        states = st[...]
        vals = [jnp.stack(col) for col in zip(*[_group_vals(in_refs, g) for g in range(gps)])]
        o, s_new = jax.vmap(chunk_fn)(states, *vals)
        s0_ref[...] = states
        st[...] = s_new
        for g in range(gps):
            o_ref[:, g * LANES:(g + 1) * LANES] = o[g]

    specs, _ = _scan_specs(ins, col_offs, n_chunks, False)
    return pl.pallas_call(
        body, name=name, grid=(n_groups // gps, n_chunks), in_specs=specs,
        out_specs=[pl.BlockSpec((CHUNK, gps * LANES), lambda h, c: (c, h)),
                   pl.BlockSpec((gps, None) + state_shape, lambda h, c: (h, c, 0, 0))],
        out_shape=[jax.ShapeDtypeStruct((t, n_groups * LANES), f32),
                   jax.ShapeDtypeStruct((n_groups, n_chunks) + state_shape, f32)],
        scratch_shapes=[pltpu.VMEM((gps,) + state_shape, f32)],
        compiler_params=pltpu.CompilerParams(dimension_semantics=("parallel", "arbitrary")),
    )(*ins)


def _scan_bwd(chunk_fn, s0s, ins, col_offs, d_out, n_groups, n_chunks, state_shape, name):
    n_in = len(ins)
    gps = GROUPS_PER_STEP
    t = d_out.shape[0]

    def body(*refs):
        s0_ref = refs[0]
        in_refs = refs[1:1 + n_in]
        do_ref = refs[1 + n_in]
        g_refs = refs[2 + n_in:2 + 2 * n_in]
        dst = refs[2 + 2 * n_in]

        @pl.when(pl.program_id(1) == 0)
        def _():
            dst[...] = jnp.zeros_like(dst)

        vals = [jnp.stack(col) for col in zip(*[_group_vals(in_refs, g) for g in range(gps)])]
        d_o = jnp.stack([do_ref[:, g * LANES:(g + 1) * LANES] for g in range(gps)])
        _, vjp = jax.vjp(jax.vmap(chunk_fn), s0_ref[...], *vals)
        gs = vjp((d_o, dst[...]))
        dst[...] = gs[0]
        for g_ref, gv in zip(g_refs, gs[1:]):
            if len(g_ref.shape) == 2:
                for g in range(gps):
                    g_ref[:, g * LANES:(g + 1) * LANES] = gv[g]
            else:
                g_ref[...] = gv

    specs, cidx = _scan_specs(ins, col_offs, n_chunks, True)
    out_lane = pl.BlockSpec((CHUNK, gps * LANES), lambda h, c: (cidx(c), h))
    g_specs = [out_lane if a.ndim == 2 else sp for a, sp in zip(ins, specs)]
    g_shapes = [(t, n_groups * LANES) if a.ndim == 2 else a.shape for a in ins]
    s0_spec = pl.BlockSpec((gps, None) + state_shape, lambda h, c: (h, cidx(c), 0, 0))
    return pl.pallas_call(
        body, name=name, grid=(n_groups // gps, n_chunks), in_specs=[s0_spec] + specs + [out_lane],
        out_specs=g_specs, out_shape=[jax.ShapeDtypeStruct(sh, f32) for sh in g_shapes],
        scratch_shapes=[pltpu.VMEM((gps,) + state_shape, f32)],
        compiler_params=pltpu.CompilerParams(dimension_semantics=("parallel", "arbitrary")),
    )(s0s, *ins, d_out)


def _rms_fwd(x, g, name):
    t = x.shape[0]
    tm = _tile(t, 416, 8)
    return _tw_fwd(_f_rms, [x, g], [_row_spec(tm, D), _full_spec(g.shape)],
                   [jax.ShapeDtypeStruct(x.shape, f32)], [_row_spec(tm, D)], (t // tm,), name)[0]


def _rms_bwd(x, g, dy, name):
    t = x.shape[0]
    tm = _tile(t, 416, 8)
    return _tw_bwd(_f_rms, [x, g], [_row_spec(tm, D), _full_spec(g.shape)], [dy], [_row_spec(tm, D)],
                   ['tile', 'acc'], (t // tm,), name)


def _ffn_fwd(h, gain, wg, wu, wd, tag):
    xn = _rms_fwd(h, gain, f"{tag}_rms")
    gate = _matmul(xn, wg, name=f"{tag}_gate")
    up = _matmul(xn, wu, name=f"{tag}_up")
    t = h.shape[0]
    tm = _tile(t, 208, 8)
    act = _tw_fwd(_f_swiglu, [gate, up], [_row_spec(tm, D_FF)] * 2, [jax.ShapeDtypeStruct((t, D_FF), f32)],
                  [_row_spec(tm, D_FF)], (t // tm,), f"{tag}_act")[0]
    out = _matmul(act, wd, res=h, scale=0.5, name=f"{tag}_down")
    return out, (xn, gate, up, act)


def _ffn_bwd(h, gain, wg, wu, wd, saved, dout, tag):
    xn, gate, up, act = saved
    t = h.shape[0]
    d_wd = _matmul(act, dout, ta=True, scale=0.5, name=f"{tag}_dwd")
    d_act = _matmul(dout, wd, tb=True, scale=0.5, name=f"{tag}_dact")
    tm = _tile(t, 208, 8)
    d_gate, d_up = _tw_bwd(_f_swiglu, [gate, up], [_row_spec(tm, D_FF)] * 2, [d_act], [_row_spec(tm, D_FF)],
                           ['tile', 'tile'], (t // tm,), f"{tag}_dactf")
    d_wg = _matmul(xn, d_gate, ta=True, name=f"{tag}_dwg")
    d_wu = _matmul(xn, d_up, ta=True, name=f"{tag}_dwu")
    d_xn = _matmul(d_gate, wg, tb=True, name=f"{tag}_dxn_g")
    d_xn = _matmul(d_up, wu, tb=True, res=d_xn, name=f"{tag}_dxn_u")
    d_hn, d_gain = _rms_bwd(h, gain, d_xn, f"{tag}_drms")
    return d_hn, d_gain, d_wg, d_wu, d_wd


def _col_spec(t, first_block):
    return pl.BlockSpec((t, LANES), lambda j, fb=first_block: (0, j + fb))


def _local_step(h0, tgt, w):
    t = h0.shape[0]
    assert t % CHUNK == 0
    nc = t // CHUNK
    grads = {}

    h1, ffn1_saved = _ffn_fwd(h0, w['ffn1_norm'], w['ffn1_wg'], w['ffn1_wu'], w['ffn1_wd'], "ffn1")
    u = _rms_fwd(h1, w['mix_norm'], "mix_rms")
    z = _matmul(u, w['w_in_p'], name="in_proj")
    zs = z[:, 9216:9216 + 304]
    abeta, aalpha = zs[:, 288:296], zs[:, 296:304]

    conv_w = w['a_conv_w']
    conv_fns = [functools.partial(_f_conv, norm=True, scale=A_DK ** -0.5),
                functools.partial(_f_conv, norm=True, scale=1.0),
                functools.partial(_f_conv, norm=False, scale=1.0)]
    qkv = []
    for idx, fn in enumerate(conv_fns):
        qkv.append(_tw_fwd(fn, [z, conv_w], [_col_spec(t, 8 * idx), pl.BlockSpec((4, LANES), lambda j, o=8 * idx: (0, j + o))],
                           [jax.ShapeDtypeStruct((t, D), f32)], [_col_spec(t, 0)], (A_HEADS,), f"a_conv{idx}")[0])
    aq, ak, av = qkv
    tmg = _tile(t, 1040, 8)
    dg_fn = functools.partial(_f_dgates, tm=tmg)
    dg_specs = [_row_spec(tmg, A_HEADS)] * 2 + [_full_spec((1, A_HEADS))] * 2
    beta, gdec = _tw_fwd(dg_fn, [abeta, aalpha, w['a_log_rate'], w['a_dt_bias']], dg_specs,
                         [jax.ShapeDtypeStruct((t, A_HEADS), f32)] * 2, [_row_spec(tmg, A_HEADS)] * 2, (t // tmg,),
                         "a_gates", with_pid=True)
    beta_h = beta.T[:, :, None]
    gdec_h = gdec.T[:, :, None]
    a_ins = [aq, ak, av, beta_h, gdec_h]
    a_offs = [0] * 5
    o_scan, a_s0 = _scan_fwd(_delta_chunk, a_ins, a_offs, A_HEADS, nc, (A_DK, A_DK), "a_scan")

    mu = w['b_shift_mu']
    mu_rkv, mu_s = mu[:, :3072], mu[:, 3072:]
    zf_rkv = _tw_fwd(_f_tshift, [z, mu_rkv], [_col_spec(t, 32), pl.BlockSpec((1, LANES), lambda j: (0, j))],
                     [jax.ShapeDtypeStruct((t, 3072), f32)], [_col_spec(t, 0)], (24,), "b_shift")[0]
    zs_b = zs[:, :288]
    zf_s = _tw_fwd(_f_tshift, [zs_b, mu_s], [_full_spec((t, 288)), _full_spec((1, 288))],
                   [jax.ShapeDtypeStruct((t, 288), f32)], [_full_spec((t, 288))], (1,), "b_shift_s")[0]
    wdf, adf, gdf = zf_s[:, 0:64], zf_s[:, 64:128], zf_s[:, 128:288]
    tmr = _tile(t, 160, 8)
    pre_params = [w['b_w0'], w['b_w_up'], w['b_a0'], w['b_a_up'], w['b_g_up'], w['b_k_k'], w['b_k_a']]
    pre_ins = [zf_rkv, wdf, adf, gdf] + pre_params
    pre_specs = ([_row_spec(tmr, D, 1), _row_spec(tmr, 64), _row_spec(tmr, 64), _row_spec(tmr, 160)]
                 + [_full_spec(p.shape) for p in pre_params])
    lw, kmod, a_s, b_s, bgate = _tw_fwd(_f_rwkv_pre, pre_ins, pre_specs, [jax.ShapeDtypeStruct((t, D), f32)] * 5,
                                        [_row_spec(tmr, D)] * 5, (t // tmr,), "b_pre")
    b_ins = [zf_rkv, kmod, zf_rkv, a_s, b_s, lw]
    b_offs = [0, 0, 2 * D, 0, 0, 0]
    y_scan, b_s0 = _scan_fwd(_rwkv_chunk, b_ins, b_offs, B_HEADS // 2, nc, (2 * B_N, 2 * B_N), "b_scan")

    out_gain_t = jnp.tile(w['a_out_norm'], (1, A_HEADS))
    r_k = w['b_r_k'].reshape(1, D)
    post_params = [out_gain_t, w['b_ln_gain'], w['b_ln_bias'], r_k]
    post_ins = [o_scan, z, y_scan, zf_rkv, kmod, zf_rkv, bgate, z, z] + post_params
    post_specs = ([_row_spec(tmr, D), _row_spec(tmr, D, 3), _row_spec(tmr, D), _row_spec(tmr, D, 0), _row_spec(tmr, D),
                   _row_spec(tmr, D, 2), _row_spec(tmr, D), _row_spec(tmr, D, 7), _row_spec(tmr, D, 8)]
                  + [_full_spec((1, D))] * 4)
    merged = _tw_fwd(_f_mix_post, post_ins, post_specs, [jax.ShapeDtypeStruct((t, D), f32)], [_row_spec(tmr, D)],
                     (t // tmr,), "mix_post")[0]
    h2 = _matmul(merged, w['w_out'], res=h1, name="out_proj")
    h3, ffn2_saved = _ffn_fwd(h2, w['ffn2_norm'], w['ffn2_wg'], w['ffn2_wu'], w['ffn2_wd'], "ffn2")

    tml = _tile(t, 416, 8)
    fnorm = w['final_norm']
    loss_fn = functools.partial(_f_loss, tm=tml)
    loss_specs = [_row_spec(tml, D), _full_spec((1, D)), _row_spec(tml, D)]
    loss_parts = _tw_fwd(loss_fn, [h3, fnorm, tgt], loss_specs, [jax.ShapeDtypeStruct((t // tml, 1, 1), f32)],
                         [pl.BlockSpec((None, 1, 1), lambda i: (i, 0, 0))], (t // tml,), "loss", with_pid=True)[0]
    loss = jnp.sum(loss_parts)
    ones = jnp.ones((t // tml, 1, 1), f32)
    d_h3, grads['final_norm'] = _tw_bwd(loss_fn, [h3, fnorm, tgt], loss_specs, [ones],
                                        [pl.BlockSpec((None, 1, 1), lambda i: (i, 0, 0))], ['tile', 'acc', None],
                                        (t // tml,), "loss_bwd", with_pid=True)

    d_hn, grads['ffn2_norm'], grads['ffn2_wg'], grads['ffn2_wu'], grads['ffn2_wd'] = _ffn_bwd(
        h2, w['ffn2_norm'], w['ffn2_wg'], w['ffn2_wu'], w['ffn2_wd'], ffn2_saved, d_h3, "ffn2")
    d_h2 = _add(d_h3, d_hn, "add_h2")
    grads['w_out'] = _matmul(merged, d_h2, ta=True, name="d_w_out")
    d_merged = _matmul(d_h2, w['w_out'], tb=True, name="d_merged")

    win = ('tile', (t, D), _row_spec(tmr, D))
    post_kinds = ['tile', win, 'tile', win, 'tile', win, 'tile', win, win] + ['acc'] * 4
    (d_o, d_az, d_y, d_r1, d_kmod1, d_v1, d_bgate, d_ga, d_gb,
     d_out_gain_t, grads['b_ln_gain'], grads['b_ln_bias'], d_r_k) = _tw_bwd(
        _f_mix_post, post_ins, post_specs, [d_merged], [_row_spec(tmr, D)], post_kinds, (t // tmr,), "mix_post_bwd")
    grads['a_out_norm'] = jnp.sum(d_out_gain_t.reshape(A_HEADS, A_DK), axis=0, keepdims=True)
    grads['b_r_k'] = d_r_k.reshape(1, B_HEADS, B_N)

    d_r2, d_kmod2, d_v2, d_as, d_bs, d_lw = _scan_bwd(_rwkv_chunk, b_s0, b_ins, b_offs, d_y, B_HEADS // 2, nc,
                                                      (2 * B_N, 2 * B_N), "b_scan_bwd")
    d_kmod = _add(d_kmod1, d_kmod2, "add_kmod")
    pre_kinds = [win] + ['tile'] * 3 + ['acc'] * 7
    pre_ct_specs = [_row_spec(tmr, D)] * 5
    (d_zf_k, d_wdf, d_adf, d_gdf, grads['b_w0'], grads['b_w_up'], grads['b_a0'], grads['b_a_up'], grads['b_g_up'],
     grads['b_k_k'], grads['b_k_a']) = _tw_bwd(
        _f_rwkv_pre, pre_ins, pre_specs, [d_lw, d_kmod, d_as, d_bs, d_bgate], pre_ct_specs, pre_kinds, (t // tmr,),
        "b_pre_bwd")
    d_zf_rkv = _assemble3(d_r1, d_r2, d_zf_k, d_v1, d_v2, "b_dzf")
    d_zb_rkv, d_mu_rkv = _tw_bwd(_f_tshift, [z, mu_rkv], [_col_spec(t, 32), pl.BlockSpec((1, LANES), lambda j: (0, j))],
                                 [d_zf_rkv], [_col_spec(t, 0)], [('tile', (t, 3072), _col_spec(t, 0)), 'tile'], (24,),
                                 "b_shift_bwd")
    d_zf_s = jnp.concatenate([d_wdf, d_adf, d_gdf], axis=1)
    d_zs_b, d_mu_s = _tw_bwd(_f_tshift, [zs_b, mu_s], [_full_spec((t, 288)), _full_spec((1, 288))], [d_zf_s],
                             [_full_spec((t, 288))], ['tile', 'tile'], (1,), "b_shift_s_bwd")
    grads['b_shift_mu'] = jnp.concatenate([d_mu_rkv, d_mu_s], axis=1)

    d_aq, d_ak, d_av, d_beta_h, d_g_h = _scan_bwd(_delta_chunk, a_s0, a_ins, a_offs, d_o, A_HEADS, nc, (A_DK, A_DK),
                                                  "a_scan_bwd")
    d_beta = d_beta_h[:, :, 0].T
    d_gdec = d_g_h[:, :, 0].T
    d_abeta, d_aalpha, grads['a_log_rate'], grads['a_dt_bias'] = _tw_bwd(
        dg_fn, [abeta, aalpha, w['a_log_rate'], w['a_dt_bias']], dg_specs, [d_beta, d_gdec],
        [_row_spec(tmg, A_HEADS)] * 2, ['tile', 'tile', 'acc', 'acc'], (t // tmg,), "a_gates_bwd", with_pid=True)
    d_zqkv, d_conv = [], []
    for idx, (fn, ct) in enumerate(zip(conv_fns, (d_aq, d_ak, d_av))):
        dz_i, dw_i = _conv_bwd(fn, z, conv_w, ct, idx, t)
        d_zqkv.append(dz_i)
        d_conv.append(dw_i)
    grads['a_conv_w'] = jnp.concatenate(d_conv, axis=1)

    d_z = jnp.concatenate(
        d_zqkv + [d_az, d_zb_rkv, d_ga, d_gb, d_zs_b, d_abeta, d_aalpha, jnp.zeros((t, ZP - 9216 - 304), f32)], axis=1)
    grads['w_in_p'] = _matmul(u, d_z, ta=True, name="d_w_in")
    d_u = _matmul(d_z, w['w_in_p'], tb=True, name="d_u")
    d_h1n, grads['mix_norm'] = _rms_bwd(h1, w['mix_norm'], d_u, "mix_drms")
    d_h1 = _add(d_h2, d_h1n, "add_h1")
    d_h0n, grads['ffn1_norm'], grads['ffn1_wg'], grads['ffn1_wu'], grads['ffn1_wd'] = _ffn_bwd(
        h0, w['ffn1_norm'], w['ffn1_wg'], w['ffn1_wu'], w['ffn1_wd'], ffn1_saved, d_h1, "ffn1")
    d_h0 = _add(d_h1, d_h0n, "add_h0")
    return loss, d_h0, grads


_WIN_SEGMENTS = ((0, 4096), (4112, 7184), (7472, 9520), (7184, 7472), (4096, 4112))


def _win_to_padded(w_in):
    parts = [w_in[:, a:b] for a, b in _WIN_SEGMENTS]
    parts.append(jnp.zeros((w_in.shape[0], ZP - IN_TOTAL), w_in.dtype))
    return jnp.concatenate(parts, axis=1)


def _win_from_padded(w_p):
    widths = [b - a for a, b in _WIN_SEGMENTS]
    offs = [sum(widths[:i]) for i in range(len(widths))]
    seg = {a: w_p[:, o:o + wd] for (a, _), o, wd in zip(_WIN_SEGMENTS, offs, widths)}
    return jnp.concatenate([seg[a] for a in sorted(seg)], axis=1)


def _add(a, b, name):
    t, c = a.shape
    tm = _tile(t, 416, 8)
    return _tw_fwd(lambda x, y: (x + y,), [a, b], [_row_spec(tm, c)] * 2, [jax.ShapeDtypeStruct(a.shape, f32)],
                   [_row_spec(tm, c)], (t // tm,), name)[0]


def _assemble3(d_r1, d_r2, d_k, d_v1, d_v2, name):
    t = d_r1.shape[0]
    tm = _tile(t, 208, 8)

    def body(r1, r2, kk, v1, v2, o_ref):
        o_ref[:, 0:D] = r1[...] + r2[...]
        o_ref[:, D:2 * D] = kk[...]
        o_ref[:, 2 * D:3 * D] = v1[...] + v2[...]

    return pl.pallas_call(body, name=name, grid=(t // tm,), in_specs=[_row_spec(tm, D)] * 5,
                          out_specs=_row_spec(tm, 3 * D), out_shape=jax.ShapeDtypeStruct((t, 3 * D), f32),
                          )(d_r1, d_r2, d_k, d_v1, d_v2)


def _conv_bwd(fn, z, conv_w, ct, idx, t):
    def body(z_ref, w_ref, ct_ref, dz_ref, dw_ref):
        _, vjp = jax.vjp(lambda a, b: fn(a, b), z_ref[...], w_ref[...])
        dz, dw = vjp((ct_ref[...],))
        dz_ref[...] = dz
        dw_ref[...] = dw

    return pl.pallas_call(
        body, name=f"a_conv{idx}_bwd", grid=(A_HEADS,),
        in_specs=[_col_spec(t, 8 * idx), pl.BlockSpec((4, LANES), lambda j, o=8 * idx: (0, j + o)), _col_spec(t, 0)],
        out_specs=[_col_spec(t, 0), pl.BlockSpec((4, LANES), lambda j: (0, j))],
        out_shape=[jax.ShapeDtypeStruct((t, D), f32), jax.ShapeDtypeStruct((4, D), f32)],
    )(z, conv_w, ct)


def _position():
    return lax.axis_index("x"), lax.axis_index("y"), lax.axis_index("c")


def _flip(v, f):
    return 1 - v if f else v


_CHIP_FLIPS = ((1, 0), (0, 1), (1, 1))
_DEV_FLIPS = tuple((fx, fy, fc) for fx in (0, 1) for fy in (0, 1) for fc in (0, 1) if (fx, fy, fc) != (0, 0, 0))


def _gather_chips(arrs, name):
    n = len(arrs)
    halves = [a.shape[0] // 2 for a in arrs]
    assert all(a.shape[0] % 32 == 0 for a in arrs)

    def body(*refs):
        ins, outs = refs[:n], refs[n:2 * n]
        send, recv, fsend, frecv, loc = refs[2 * n:]
        x, y, c = _position()
        me = 2 * x + y
        started = []
        for a in range(n):
            lc = pltpu.make_async_copy(ins[a], outs[a].at[me], loc.at[a])
            lc.start()
            started.append(lc)
        sends, plan = [], []
        for a in range(n):
            mine = pl.ds(pl.multiple_of(c * halves[a], 16), halves[a])
            theirs = pl.ds(pl.multiple_of((1 - c) * halves[a], 16), halves[a])
            for j, (fx, fy) in enumerate(_CHIP_FLIPS):
                px, py = _flip(x, fx), _flip(y, fy)
                p = 2 * px + py
                cp = pltpu.make_async_remote_copy(src_ref=ins[a].at[mine], dst_ref=outs[a].at[me, mine],
                                                  send_sem=send.at[a, j], recv_sem=recv.at[a, j],
                                                  device_id=(px, py, c), device_id_type=MESH)
                cp.start()
                sends.append(cp)
                landed = pltpu.make_async_remote_copy(src_ref=ins[a].at[mine], dst_ref=outs[a].at[p, mine],
                                                      send_sem=send.at[a, j], recv_sem=recv.at[a, j],
                                                      device_id=(px, py, c), device_id_type=MESH)
                onward = pltpu.make_async_remote_copy(src_ref=outs[a].at[p, mine], dst_ref=outs[a].at[p, mine],
                                                      send_sem=fsend.at[a, j], recv_sem=frecv.at[a, j],
                                                      device_id=(x, y, 1 - c), device_id_type=MESH)
                from_sibling = pltpu.make_async_remote_copy(src_ref=outs[a].at[p, theirs], dst_ref=outs[a].at[p, theirs],
                                                            send_sem=fsend.at[a, j], recv_sem=frecv.at[a, j],
                                                            device_id=(x, y, 1 - c), device_id_type=MESH)
                plan.append((landed, onward, from_sibling))
        for landed, onward, _ in plan:
            landed.wait_recv()
            onward.start()
        for _, _, from_sibling in plan:
            from_sibling.wait_recv()
        for cp in sends:
            cp.wait_send()
        for _, onward, _ in plan:
            onward.wait_send()
        for lc in started:
            lc.wait()

    sems = [pltpu.SemaphoreType.DMA((n, 3))] * 4 + [pltpu.SemaphoreType.DMA((n,))]
    return pl.pallas_call(
        body, name=name, in_specs=[ANY] * n, out_specs=[ANY] * n,
        out_shape=[jax.ShapeDtypeStruct((N_CHIPS,) + a.shape, a.dtype) for a in arrs], scratch_shapes=sems,
    )(*arrs)


def _swap_halves(g, name):
    n, r, w = g.shape
    hr = r // 2
    assert r % 32 == 0

    def body(g_ref, own_ref, got_ref, send, recv, loc):
        x, y, c = _position()
        mine = pl.ds(pl.multiple_of(c * hr, 16), hr)
        theirs = pl.ds(pl.multiple_of((1 - c) * hr, 16), hr)
        lc = pltpu.make_async_copy(g_ref.at[:, mine], own_ref, loc)
        lc.start()
        cp = pltpu.make_async_remote_copy(src_ref=g_ref.at[:, theirs], dst_ref=got_ref, send_sem=send, recv_sem=recv,
                                          device_id=(x, y, 1 - c), device_id_type=MESH)
        cp.start()
        cp.wait()
        lc.wait()

    half = jax.ShapeDtypeStruct((n, hr, w), g.dtype)
    return pl.pallas_call(
        body, name=name, in_specs=[ANY], out_specs=[ANY, ANY], out_shape=[half, half],
        scratch_shapes=[pltpu.SemaphoreType.DMA(())] * 3,
    )(g)


def _add_to(a, b, dtype, name):
    n, r, w = a.shape
    tr = _tile(r, 1024, 16)
    spec = pl.BlockSpec((n, tr, w), lambda i: (0, i, 0))

    def body(a_ref, b_ref, o_ref):
        o_ref[...] = (a_ref[...] + b_ref[...]).astype(dtype)

    return pl.pallas_call(body, name=name, grid=(r // tr,), in_specs=[spec, spec], out_specs=spec,
                          out_shape=jax.ShapeDtypeStruct(a.shape, dtype))(a, b)


def _join_halves(q, name):
    hr, w = q.shape

    def body(q_ref, out_ref, send, recv, loc):
        x, y, c = _position()
        mine = pl.ds(pl.multiple_of(c * hr, 8), hr)
        theirs = pl.ds(pl.multiple_of((1 - c) * hr, 8), hr)
        lc = pltpu.make_async_copy(q_ref, out_ref.at[mine], loc)
        lc.start()
        cp = pltpu.make_async_remote_copy(src_ref=q_ref, dst_ref=out_ref.at[mine], send_sem=send, recv_sem=recv,
                                          device_id=(x, y, 1 - c), device_id_type=MESH)
        cp.start()
        cp.wait_send()
        pltpu.make_async_remote_copy(src_ref=q_ref, dst_ref=out_ref.at[theirs], send_sem=send, recv_sem=recv,
                                     device_id=(x, y, 1 - c), device_id_type=MESH).wait_recv()
        lc.wait()

    return pl.pallas_call(
        body, name=name, in_specs=[ANY], out_specs=ANY, out_shape=jax.ShapeDtypeStruct((2 * hr, w), q.dtype),
        scratch_shapes=[pltpu.SemaphoreType.DMA(())] * 3,
    )(q)


def _scatter_chips(g, name):
    def body(g_ref, out_ref, send, recv, loc):
        x, y, c = _position()
        me = 2 * x + y
        lc = pltpu.make_async_copy(g_ref.at[me], out_ref.at[me], loc)
        lc.start()
        sends, recvs = [], []
        for j, (fx, fy) in enumerate(_CHIP_FLIPS):
            px, py = _flip(x, fx), _flip(y, fy)
            p = 2 * px + py
            cp = pltpu.make_async_remote_copy(src_ref=g_ref.at[p], dst_ref=out_ref.at[me], send_sem=send.at[j],
                                              recv_sem=recv.at[j], device_id=(px, py, c), device_id_type=MESH)
            cp.start()
            sends.append(cp)
            recvs.append(pltpu.make_async_remote_copy(src_ref=g_ref.at[me], dst_ref=out_ref.at[p], send_sem=send.at[j],
                                                      recv_sem=recv.at[j], device_id=(px, py, c), device_id_type=MESH))
        for cp in recvs:
            cp.wait_recv()
        for cp in sends:
            cp.wait_send()
        lc.wait()

    return pl.pallas_call(
        body, name=name, in_specs=[ANY], out_specs=ANY, out_shape=jax.ShapeDtypeStruct(g.shape, g.dtype),
        scratch_shapes=[pltpu.SemaphoreType.DMA((3,)), pltpu.SemaphoreType.DMA((3,)), pltpu.SemaphoreType.DMA(())],
    )(g)


def _gather_devices(s, name):
    def body(s_ref, out_ref, send, recv, loc):
        x, y, c = _position()
        me = 4 * x + 2 * y + c
        lc = pltpu.make_async_copy(s_ref, out_ref.at[me], loc)
        lc.start()
        sends, recvs = [], []
        for j, (fx, fy, fc) in enumerate(_DEV_FLIPS):
            px, py, pc = _flip(x, fx), _flip(y, fy), _flip(c, fc)
            cp = pltpu.make_async_remote_copy(src_ref=s_ref, dst_ref=out_ref.at[me], send_sem=send.at[j],
                                              recv_sem=recv.at[j], device_id=(px, py, pc), device_id_type=MESH)
            cp.start()
            sends.append(cp)
            recvs.append(pltpu.make_async_remote_copy(
                src_ref=s_ref, dst_ref=out_ref.at[4 * px + 2 * py + pc], send_sem=send.at[j], recv_sem=recv.at[j],
                device_id=(px, py, pc), device_id_type=MESH))
        for cp in recvs:
            cp.wait_recv()
        for cp in sends:
            cp.wait_send()
        lc.wait()

    return pl.pallas_call(
        body, name=name, in_specs=[ANY], out_specs=ANY, out_shape=jax.ShapeDtypeStruct((N_DEV,) + s.shape, s.dtype),
        scratch_shapes=[pltpu.SemaphoreType.DMA((7,)), pltpu.SemaphoreType.DMA((7,)), pltpu.SemaphoreType.DMA(())],
    )(s)


def _sum_slots(a, name):
    s, r, c = a.shape
    tr = _tile(r, 2048, 16)

    def body(a_ref, o_ref):
        acc = a_ref[0].astype(f32)
        for i in range(1, s):
            acc = acc + a_ref[i].astype(f32)
        o_ref[...] = acc

    return pl.pallas_call(body, name=name, grid=(r // tr,), in_specs=[pl.BlockSpec((s, tr, c), lambda i: (0, i, 0))],
                          out_specs=pl.BlockSpec((tr, c), lambda i: (i, 0)),
                          out_shape=jax.ShapeDtypeStruct((r, c), f32))(a)


def _adamw(w, g_parts, m, v, name):
    shape = w.shape
    size = w.size
    view = (size // LANES, LANES) if size % LANES == 0 else (1, size)
    rows = view[0]
    tr = _tile(rows, 2048, 8) if rows > 2048 else rows
    n_g = len(g_parts)

    def body(*refs):
        w_ref = refs[0]
        g_refs = refs[1:1 + n_g]
        m_ref, v_ref, g_out, d_out, m_out, v_out = refs[1 + n_g:]
        g = g_refs[0][...]
        for gr in g_refs[1:]:
            g = g + gr[...]
        m_new = ADAM_B1 * m_ref[...] + (1.0 - ADAM_B1) * g
        v_new = ADAM_B2 * v_ref[...] + (1.0 - ADAM_B2) * (g * g)
        m_hat = m_new / (1.0 - ADAM_B1 ** ADAM_STEP)
        v_hat = v_new / (1.0 - ADAM_B2 ** ADAM_STEP)
        g_out[...] = g
        d_out[...] = -ADAM_LR * (m_hat / (jnp.sqrt(v_hat) + ADAM_EPS) + ADAM_WD * w_ref[...])
        m_out[...] = m_new
        v_out[...] = v_new

    spec = pl.BlockSpec((tr, view[1]), lambda i: (i, 0))
    args = [w.reshape(view)] + [g.reshape(view) for g in g_parts] + [m.reshape(view), v.reshape(view)]
    outs = pl.pallas_call(body, name=name, grid=(rows // tr,), in_specs=[spec] * len(args), out_specs=[spec] * 4,
                          out_shape=[jax.ShapeDtypeStruct(view, f32)] * 4)(*args)
    return [o.reshape(shape) for o in outs]


_BIG = ('ffn1_w_gu', 'ffn1_w_down', 'w_in', 'w_out', 'ffn2_w_gu', 'ffn2_w_down')
_SMALL_SHARDED = ('meta_tokens', 'a_conv_w', 'b_w_up', 'b_a_up', 'b_g_up')
_WEIGHTS = ('meta_tokens', 'ffn1_norm', 'ffn1_w_gu', 'ffn1_w_down', 'mix_norm', 'w_in', 'a_conv_w', 'a_log_rate',
            'a_dt_bias', 'a_out_norm', 'b_shift_mu', 'b_w0', 'b_w_up', 'b_a0', 'b_a_up', 'b_g_up', 'b_k_k', 'b_k_a',
            'b_r_k', 'b_ln_gain', 'b_ln_bias', 'w_out', 'ffn2_norm', 'ffn2_w_gu', 'ffn2_w_down', 'final_norm')
_SMALL = tuple(n for n in _WEIGHTS if n not in _BIG)


def _rows_of(shape):
    n = 1
    for d in shape:
        n *= d
    return n, -(-n // LANES)


def _pack(arrs, dtype, row_mult=32):
    parts = []
    for a in arrs:
        n, rows = _rows_of(a.shape)
        flat = a.reshape(-1).astype(dtype)
        if n % LANES:
            flat = jnp.pad(flat, (0, rows * LANES - n))
        parts.append(flat.reshape(rows, LANES))
    total = sum(p.shape[0] for p in parts)
    extra = -total % row_mult
    if extra:
        parts.append(jnp.zeros((extra, LANES), dtype))
    return jnp.concatenate(parts, axis=0)


def _unpack(packed, shapes, lead=()):
    out, off = [], 0
    for sh in shapes:
        n, rows = _rows_of(sh)
        seg = packed[..., off:off + rows, :]
        if n % LANES:
            seg = seg.reshape(lead + (-1,))[..., :n]
        out.append(seg.reshape(lead + tuple(sh)))
        off += rows
    return out


def _cols_from_shards(s):
    return jnp.concatenate([s[i] for i in range(N_CHIPS)], axis=-1)


def _cols_to_shards(a):
    r, c = a.shape
    return a.reshape(r, N_CHIPS, c // N_CHIPS).transpose(1, 0, 2)


def kernel(x, meta_tokens, ffn1_norm, ffn1_w_gu, ffn1_w_down, mix_norm, w_in, a_conv_w, a_log_rate, a_dt_bias, a_out_norm, b_shift_mu, b_w0, b_w_up, b_a0, b_a_up, b_g_up, b_k_k, b_k_a, b_r_k, b_ln_gain, b_ln_bias, w_out, ffn2_norm, ffn2_w_gu, ffn2_w_down, final_norm, loss_target, m_meta_tokens, m_ffn1_norm, m_ffn1_w_gu, m_ffn1_w_down, m_mix_norm, m_w_in, m_a_conv_w, m_a_log_rate, m_a_dt_bias, m_a_out_norm, m_b_shift_mu, m_b_w0, m_b_w_up, m_b_a0, m_b_a_up, m_b_g_up, m_b_k_k, m_b_k_a, m_b_r_k, m_b_ln_gain, m_b_ln_bias, m_w_out, m_ffn2_norm, m_ffn2_w_gu, m_ffn2_w_down, m_final_norm, v_meta_tokens, v_ffn1_norm, v_ffn1_w_gu, v_ffn1_w_down, v_mix_norm, v_w_in, v_a_conv_w, v_a_log_rate, v_a_dt_bias, v_a_out_norm, v_b_shift_mu, v_b_w0, v_b_w_up, v_b_a0, v_b_a_up, v_b_g_up, v_b_k_k, v_b_k_a, v_b_r_k, v_b_ln_gain, v_b_ln_bias, v_w_out, v_ffn2_norm, v_ffn2_w_gu, v_ffn2_w_down, v_final_norm):
    args = locals()
    wts = {n: args[n] for n in _WEIGHTS}
    mom = {n: args["m_" + n] for n in _WEIGHTS}
    var = {n: args["v_" + n] for n in _WEIGHTS}
    chip = 2 * lax.axis_index("x") + lax.axis_index("y")

    big_shapes = [wts[n].shape[1:] for n in _BIG]
    small_shapes = [wts[n].shape[-2:] for n in _SMALL_SHARDED]
    big_packed = _pack([wts[n] for n in _BIG], bf16)
    small_packed = _pack([wts[n] for n in _SMALL_SHARDED], f32)
    big_all, small_all = _gather_chips([big_packed, small_packed], "gather_weights")
    gu1, dn1, w_in_s, w_out_s, gu2, dn2 = _unpack(big_all, big_shapes, (N_CHIPS,))
    meta_s, conv_s, wup_s, aup_s, gup_s = _unpack(small_all, small_shapes, (N_CHIPS,))
    w = {
        'ffn1_norm': ffn1_norm, 'mix_norm': mix_norm, 'ffn2_norm': ffn2_norm, 'final_norm': final_norm[None, :],
        'ffn1_wg': jnp.concatenate([gu1[0], gu1[1]], axis=1), 'ffn1_wu': jnp.concatenate([gu1[2], gu1[3]], axis=1),
        'ffn1_wd': dn1.reshape(D_FF, D),
        'ffn2_wg': jnp.concatenate([gu2[0], gu2[1]], axis=1), 'ffn2_wu': jnp.concatenate([gu2[2], gu2[3]], axis=1),
        'ffn2_wd': dn2.reshape(D_FF, D),
        'w_in_p': _win_to_padded(_cols_from_shards(w_in_s)), 'w_out': w_out_s.reshape(D, D),
        'a_conv_w': _cols_from_shards(conv_s), 'b_w_up': _cols_from_shards(wup_s), 'b_a_up': _cols_from_shards(aup_s),
        'b_g_up': _cols_from_shards(gup_s),
        'a_log_rate': a_log_rate, 'a_dt_bias': a_dt_bias, 'a_out_norm': a_out_norm, 'b_shift_mu': b_shift_mu,
        'b_w0': b_w0, 'b_a0': b_a0, 'b_k_k': b_k_k, 'b_k_a': b_k_a, 'b_r_k': b_r_k, 'b_ln_gain': b_ln_gain,
        'b_ln_bias': b_ln_bias,
    }
    meta_full = _cols_from_shards(meta_s)

    h0 = jnp.concatenate([jnp.zeros((PAD, D), f32), meta_full, x[0]], axis=0)
    tgt = jnp.concatenate([jnp.zeros((SKIP, D), f32), loss_target[0]], axis=0)
    loss_local, d_h0, g = _local_step(h0, tgt, w)
    loss = lax.psum(loss_local, ("x", "y", "c"))
    grad_x = d_h0[SKIP:][None]

    big_grads = [
        _cols_to_shards(jnp.concatenate([g['ffn1_wg'], g['ffn1_wu']], axis=1)),
        g['ffn1_wd'].reshape(N_CHIPS, D_FF // N_CHIPS, D),
        _cols_to_shards(_win_from_padded(g['w_in_p'])),
        g['w_out'].reshape(N_CHIPS, D // N_CHIPS, D),
        _cols_to_shards(jnp.concatenate([g['ffn2_wg'], g['ffn2_wu']], axis=1)),
        g['ffn2_wd'].reshape(N_CHIPS, D_FF // N_CHIPS, D),
    ]
    g_packed = jnp.concatenate([a.reshape(N_CHIPS, -1, LANES) for a in big_grads], axis=1)
    assert g_packed.shape[1] == big_packed.shape[0]
    own_half, sib_half = _swap_halves(g_packed, "swap_halves")
    chip_half = _add_to(own_half, sib_half, bf16, "add_sibling")
    summed_half = _sum_slots(_scatter_chips(chip_half, "scatter_grads"), "sum_chips")
    big_parts = _unpack(_join_halves(summed_half, "join_halves"), big_shapes)

    small_full = {
        'meta_tokens': d_h0[PAD:SKIP], 'ffn1_norm': g['ffn1_norm'], 'mix_norm': g['mix_norm'], 'a_conv_w': g['a_conv_w'],
        'a_log_rate': g['a_log_rate'], 'a_dt_bias': g['a_dt_bias'], 'a_out_norm': g['a_out_norm'],
        'b_shift_mu': g['b_shift_mu'], 'b_w0': g['b_w0'], 'b_w_up': g['b_w_up'], 'b_a0': g['b_a0'], 'b_a_up': g['b_a_up'],
        'b_g_up': g['b_g_up'], 'b_k_k': g['b_k_k'], 'b_k_a': g['b_k_a'], 'b_r_k': g['b_r_k'], 'b_ln_gain': g['b_ln_gain'],
        'b_ln_bias': g['b_ln_bias'], 'ffn2_norm': g['ffn2_norm'], 'final_norm': g['final_norm'],
    }
    s_shapes = [small_full[n].shape for n in _SMALL]
    s_sum = _sum_slots(_gather_devices(_pack([small_full[n] for n in _SMALL], f32, row_mult=256), "gather_small"),
                       "sum_small")
    s_parts = dict(zip(_SMALL, _unpack(s_sum, s_shapes)))

    grad, delta, new_m, new_v = {}, {}, {}, {}
    for n, a in zip(_BIG, big_parts):
        grad[n], delta[n], new_m[n], new_v[n] = _adamw(wts[n], [a.reshape(wts[n].shape)], mom[n], var[n], f"adamw_{n}")
    for n in _SMALL:
        gs = s_parts[n]
        if n in _SMALL_SHARDED:
            width = wts[n].shape[-1]
            gs = lax.dynamic_slice_in_dim(gs, chip * width, width, axis=gs.ndim - 1)
        gs = gs.reshape(wts[n].shape)
        grad[n], delta[n], new_m[n], new_v[n] = _adamw(wts[n], [gs], mom[n], var[n], f"adamw_{n}")

    return (loss, grad_x, *[grad[n] for n in _WEIGHTS], *[delta[n] for n in _WEIGHTS],
            *[new_m[n] for n in _WEIGHTS], *[new_v[n] for n in _WEIGHTS])
```

```python
import functools

import jax
import jax.numpy as jnp
from jax import lax
from jax.experimental import pallas as pl
from jax.experimental.pallas import tpu as pltpu

f32 = jnp.float32
bf16 = jnp.bfloat16
HI = lax.Precision.HIGHEST
MESH = pl.DeviceIdType.MESH
ANY = pl.BlockSpec(memory_space=pl.ANY)

D = 1024
N_META = 16
CHUNK = 64
PAD = CHUNK - N_META
SKIP = PAD + N_META
EPS = 1e-6
D_FF = 2816
A_HEADS = 8
A_DK = 128
B_HEADS = 16
B_N = 64
B_GN_EPS = B_N * 1e-5
W_LORA, AA_LORA, G_LORA = 64, 64, 160
IN_TOTAL = 9520
ZP = 9600
LANES = 128
N_CHIPS = 4
N_DEV = 8

ADAM_LR, ADAM_B1, ADAM_B2, ADAM_EPS, ADAM_WD, ADAM_STEP = 0.001, 0.9, 0.999, 1e-08, 0.01, 10

MXU_DTYPE = bf16


def _tile(n, cap, mult):
    if n <= cap:
        return n
    best = None
    for t in range(mult, cap + 1, mult):
        if n % t == 0:
            best = t
    assert best is not None, (n, cap, mult)
    return best


def _sigmoid(x):
    return jax.nn.sigmoid(x)


def _silu(x):
    return x * jax.nn.sigmoid(x)


def _softplus(x):
    return jnp.maximum(x, 0.0) + jnp.log(1.0 + jnp.exp(-jnp.abs(x)))


def _head_matrix(c, nh):
    hd = c // nh
    r = lax.broadcasted_iota(jnp.int32, (c, nh), 0)
    h = lax.broadcasted_iota(jnp.int32, (c, nh), 1)
    return (r >= h * hd) & (r < (h + 1) * hd)


def _dot_exact_rhs(x, e, cb):
    dn = (((1,), (cb,)), ((), ()))
    if SCAN_PASSES == 0:
        return lax.dot_general(x, e.astype(f32), dn, precision=HI, preferred_element_type=f32)
    eb = e.astype(bf16)
    hi = x.astype(bf16)
    lo = (x - hi.astype(f32)).astype(bf16)
    return (lax.dot_general(hi, eb, dn, preferred_element_type=f32)
            + lax.dot_general(lo, eb, dn, preferred_element_type=f32))


def _head_sum_impl(x, nh):
    e = _head_matrix(x.shape[-1], nh)
    return _dot_exact_rhs(_dot_exact_rhs(x, e, 0), e, 1)


@functools.partial(jax.custom_vjp, nondiff_argnums=(1,))
def _head_sum(x, nh):
    return _head_sum_impl(x, nh)


def _head_sum_fwd(x, nh):
    return _head_sum_impl(x, nh), None


def _head_sum_bwd(nh, _, g):
    return (_head_sum_impl(g, nh),)


_head_sum.defvjp(_head_sum_fwd, _head_sum_bwd)


@functools.partial(jax.custom_vjp, nondiff_argnums=(1,))
def _shift_rows(x, s):
    n = x.shape[0]
    row = lax.broadcasted_iota(jnp.int32, x.shape, 0)
    if s > 0:
        return jnp.where(row >= s, pltpu.roll(x, s, 0), 0.0)
    return jnp.where(row < n + s, pltpu.roll(x, n + s, 0), 0.0)


def _shift_rows_fwd(x, s):
    return _shift_rows(x, s), None


def _shift_rows_bwd(s, _, g):
    return (_shift_rows(g, -s),)


_shift_rows.defvjp(_shift_rows_fwd, _shift_rows_bwd)


def _matmul(a, b, *, ta=False, tb=False, res=None, scale=1.0, name):
    assert not (ta and tb)
    (ar, ac), (br, bc) = a.shape, b.shape
    m, k = (ac, ar) if ta else (ar, ac)
    n, kb = (br, bc) if tb else (bc, br)
    assert k == kb, (a.shape, b.shape, ta, tb)
    tm = _tile(m, 1408, LANES) if ta else _tile(m, 832, 8)
    tn = _tile(n, 1408, LANES)
    tk = _tile(k, 1040, 8) if ta else _tile(k, 1408, LANES)
    nk = k // tk
    dn = (((0 if ta else 1,), (1 if tb else 0,)), ((), ()))

    def body(*refs):
        if res is not None:
            a_ref, b_ref, r_ref, o_ref, acc = refs
        else:
            a_ref, b_ref, o_ref, acc = refs
        kk = pl.program_id(2)

        @pl.when(kk == 0)
        def _():
            acc[...] = jnp.zeros_like(acc)

        acc[...] += lax.dot_general(a_ref[...].astype(MXU_DTYPE), b_ref[...].astype(MXU_DTYPE), dn,
                                    preferred_element_type=f32,
                                    precision=None if MXU_DTYPE == bf16 else HI)

        @pl.when(kk == nk - 1)
        def _():
            out = acc[...]
            if scale != 1.0:
                out = out * scale
            if res is not None:
                out = r_ref[...] + out
            o_ref[...] = out

    if ta:
        a_spec = pl.BlockSpec((tk, tm), lambda i, j, kk: (kk, i))
    else:
        a_spec = pl.BlockSpec((tm, tk), lambda i, j, kk: (i, kk))
    if tb:
        b_spec = pl.BlockSpec((tn, tk), lambda i, j, kk: (j, kk))
    else:
        b_spec = pl.BlockSpec((tk, tn), lambda i, j, kk: (kk, j))
    in_specs = [a_spec, b_spec]
    args = [a, b]
    if res is not None:
        in_specs.append(pl.BlockSpec((tm, tn), lambda i, j, kk: (i, j)))
        args.append(res)
    return pl.pallas_call(
        body, name=name, grid=(m // tm, n // tn, nk), in_specs=in_specs,
        out_specs=pl.BlockSpec((tm, tn), lambda i, j, kk: (i, j)),
        out_shape=jax.ShapeDtypeStruct((m, n), f32),
        scratch_shapes=[pltpu.VMEM((tm, tn), f32)],
        compiler_params=pltpu.CompilerParams(dimension_semantics=("parallel", "parallel", "arbitrary")),
    )(*args)


def _tw_fwd(fn, ins, in_specs, out_shapes, out_specs, grid, name, with_pid=False):
    n_in = len(ins)

    def body(*refs):
        vals = [r[...] for r in refs[:n_in]]
        outs = fn(pl.program_id(0), *vals) if with_pid else fn(*vals)
        for r, o in zip(refs[n_in:], outs):
            r[...] = o

    return pl.pallas_call(body, name=name, grid=grid, in_specs=in_specs, out_specs=out_specs,
                          out_shape=out_shapes)(*ins)


def _tw_bwd(fn, ins, in_specs, cts, ct_specs, kinds, grid, name, with_pid=False):
    n_in, n_ct = len(ins), len(cts)
    diff = [i for i, kd in enumerate(kinds) if kd is not None]

    def body(*refs):
        vals = [r[...] for r in refs[:n_in]]
        ctv = tuple(r[...] for r in refs[n_in:n_in + n_ct])
        g_refs = refs[n_in + n_ct:]
        pid = pl.program_id(0)

        def f(*dv):
            full = list(vals)
            for i, v in zip(diff, dv):
                full[i] = v
            out = fn(pid, *full) if with_pid else fn(*full)
            return tuple(out)

        _, vjp = jax.vjp(f, *[vals[i] for i in diff])
        gs = vjp(ctv)
        first = pid == 0
        for i2 in range(1, len(grid)):
            first = first & (pl.program_id(i2) == 0)
        for i, g, g_ref in zip(diff, gs, g_refs):
            if kinds[i] != 'acc':
                g_ref[...] = g
            else:
                @pl.when(first)
                def _(g=g, g_ref=g_ref):
                    g_ref[...] = g

                @pl.when(jnp.logical_not(first))
                def _(g=g, g_ref=g_ref):
                    g_ref[...] += g

    zero_map = {1: lambda *a: (0,), 2: lambda *a: (0, 0), 3: lambda *a: (0, 0, 0)}
    out_specs, out_shapes = [], []
    for i in diff:
        if kinds[i] == 'tile':
            out_shapes.append(jax.ShapeDtypeStruct(ins[i].shape, f32))
            out_specs.append(in_specs[i])
        elif kinds[i] == 'acc':
            out_shapes.append(jax.ShapeDtypeStruct(ins[i].shape, f32))
            out_specs.append(pl.BlockSpec(ins[i].shape, zero_map[ins[i].ndim]))
        else:
            out_shapes.append(jax.ShapeDtypeStruct(kinds[i][1], f32))
            out_specs.append(kinds[i][2])
    return pl.pallas_call(body, name=name, grid=grid, in_specs=list(in_specs) + list(ct_specs),
                          out_specs=out_specs, out_shape=out_shapes)(*ins, *cts)


def _row_spec(tm, c, col_block=0):
    return pl.BlockSpec((tm, c), lambda i, cb=col_block: (i, cb))


def _full_spec(shape):
    nd = len(shape)
    return pl.BlockSpec(shape, lambda *a, nd=nd: (0,) * nd)


def _f_rms(x, g):
    return (x * lax.rsqrt(jnp.mean(x * x, axis=-1, keepdims=True) + EPS) * g,)


def _f_swiglu(gate, up):
    return (_silu(gate) * up,)


def _f_loss(pid, h, g, tgt, *, tm):
    y = h * lax.rsqrt(jnp.mean(h * h, axis=-1, keepdims=True) + EPS) * g
    row = pid * tm + lax.broadcasted_iota(jnp.int32, (tm, 1), 0)
    err = jnp.where(row >= SKIP, y - tgt, 0.0)
    per_row = jnp.mean(err * err, axis=-1, keepdims=True)
    return (0.5 * jnp.sum(per_row, axis=0, keepdims=True),)


def _f_conv(x, w, *, norm, scale):
    y = x * w[3:4, :]
    for s in (1, 2, 3):
        y = y + _shift_rows(x, s) * w[3 - s:4 - s, :]
    y = _silu(y)
    if norm:
        y = y * lax.rsqrt(jnp.sum(y * y, axis=-1, keepdims=True) + 1e-6) * scale
    return (y,)


def _f_dgates(pid, abeta, aalpha, log_rate, dt_bias, *, tm):
    row = pid * tm + lax.broadcasted_iota(jnp.int32, (tm, 1), 0)
    live = row >= PAD
    beta = jnp.where(live, _sigmoid(abeta), 0.0)
    g = jnp.where(live, -jnp.exp(log_rate) * _softplus(aalpha + dt_bias), 0.0)
    return beta, g


def _f_tshift(z, mu):
    return (z + (_shift_rows(z, 1) - z) * mu,)


def _f_rwkv_pre(k, wd, ad, gd, w0, w_up, a0, a_up, g_up, k_k, k_a):
    w_log = -_softplus(-(w0 + _smm(jnp.tanh(wd), w_up))) - 0.5
    lw = -jnp.exp(w_log)
    a_lr = _sigmoid(a0 + _smm(ad, a_up))
    gate = _smm(_sigmoid(gd), g_up)
    kkp = k * k_k
    kk = kkp * lax.rsqrt(_head_sum(kkp * kkp, B_HEADS) + 1e-6)
    kmod = k * (1.0 + (a_lr - 1.0) * k_a)
    return lw, kmod, -kk, kk * a_lr, gate


def _f_mix_post(o, az, y, r, kmod, v, gate, ga, gb, out_gain, ln_g, ln_b, r_k):
    ms = _head_sum(o * o, A_HEADS) * (1.0 / A_DK)
    oa = o * lax.rsqrt(ms + EPS) * out_gain * _silu(az)
    mean = _head_sum(y, B_HEADS) * (1.0 / B_N)
    yc = y - mean
    var = _head_sum(yc * yc, B_HEADS) * (1.0 / B_N)
    yn = yc * lax.rsqrt(var + B_GN_EPS) * ln_g + ln_b
    bonus = _head_sum(r * kmod * r_k, B_HEADS) * v
    ob = (yn + bonus) * gate
    return (_sigmoid(ga) * oa + _sigmoid(gb) * ob,)


SCAN_PASSES = 3


def _split2(a):
    hi = a.astype(bf16)
    return hi, (a - hi.astype(f32)).astype(bf16)


def _dot_passes(a, b, ca, cb):
    dn = (((ca,), (cb,)), ((), ()))
    if SCAN_PASSES == 0:
        return lax.dot_general(a, b, dn, precision=HI, preferred_element_type=f32)
    if SCAN_PASSES == 1:
        return lax.dot_general(a.astype(bf16), b.astype(bf16), dn, preferred_element_type=f32)
    ah, al = _split2(a)
    bh, bl = _split2(b)
    return (lax.dot_general(ah, bh, dn, preferred_element_type=f32)
            + (lax.dot_general(ah, bl, dn, preferred_element_type=f32)
               + lax.dot_general(al, bh, dn, preferred_element_type=f32)))


@functools.partial(jax.custom_vjp, nondiff_argnums=(2, 3))
def _sdot(a, b, ca, cb):
    return _dot_passes(a, b, ca, cb)


def _sdot_fwd(a, b, ca, cb):
    return _dot_passes(a, b, ca, cb), (a, b)


def _sdot_bwd(ca, cb, res, g):
    a, b = res
    if (ca, cb) == (1, 0):
        return _dot_passes(g, b, 1, 1), _dot_passes(a, g, 0, 0)
    if (ca, cb) == (1, 1):
        return _dot_passes(g, b, 1, 0), _dot_passes(g, a, 0, 0)
    assert (ca, cb) == (0, 0)
    return _dot_passes(b, g, 1, 1), _dot_passes(a, g, 1, 0)


_sdot.defvjp(_sdot_fwd, _sdot_bwd)


def _smm(a, b):
    return _sdot(a, b, 1, 0)


def _smm_nt(a, b):
    return _sdot(a, b, 1, 1)


def _smm_tn(a, b):
    return _sdot(a, b, 0, 0)


def _tri_dot(x, ca):
    n = x.shape[0]
    incl = _tri_masks(n)[0]
    dn = (((ca,), (0,)), ((), ()))
    if SCAN_PASSES == 0:
        return lax.dot_general(incl.astype(f32), x, dn, precision=HI, preferred_element_type=f32)
    tri = incl.astype(bf16)
    hi, r1 = x.astype(bf16), None
    r1 = x - hi.astype(f32)
    mid = r1.astype(bf16)
    lo = (r1 - mid.astype(f32)).astype(bf16)
    return (lax.dot_general(tri, hi, dn, preferred_element_type=f32)
            + (lax.dot_general(tri, mid, dn, preferred_element_type=f32)
               + lax.dot_general(tri, lo, dn, preferred_element_type=f32)))


@jax.custom_vjp
def _cumsum_rows(x):
    return _tri_dot(x, 1)


def _cumsum_rows_fwd(x):
    return _tri_dot(x, 1), None


def _cumsum_rows_bwd(_, g):
    return (_tri_dot(g, 0),)


_cumsum_rows.defvjp(_cumsum_rows_fwd, _cumsum_rows_bwd)


def _tri_masks(n):
    i = lax.broadcasted_iota(jnp.int32, (n, n), 0)
    j = lax.broadcasted_iota(jnp.int32, (n, n), 1)
    return i >= j, i > j, i == j, i <= j


def _unit_lower_inv(low):
    n = low.shape[0]
    assert n == CHUNK
    _, _, eye, _ = _tri_masks(n)
    acc = eye.astype(f32) + low
    p = low
    for _ in range(5):
        p = _smm(p, p)
        acc = acc + _smm(acc, p)
    return acc


def _delta_chunk(s, q, k, v, beta, g):
    incl, strict, eye, upper = _tri_masks(CHUNK)
    g_row = jnp.sum(jnp.where(eye, g, 0.0), axis=0, keepdims=True)
    gc = jnp.sum(jnp.where(incl, g_row, 0.0), axis=1, keepdims=True)
    gc_row = jnp.sum(jnp.where(upper, g, 0.0), axis=0, keepdims=True)
    decay = jnp.where(incl, jnp.exp(jnp.where(incl, gc - gc_row, 0.0)), 0.0)
    kb = k * beta
    vb = v * beta
    m = jnp.where(strict, _smm_nt(kb, k) * decay, 0.0)
    tinv = _unit_lower_inv(-m)
    u = _smm(tinv, vb)
    wk = _smm(tinv, kb * jnp.exp(gc))
    attn = _smm_nt(q, k) * decay
    qg = q * jnp.exp(gc)
    g_last = jnp.sum(g, axis=0, keepdims=True)
    k_tail = k * jnp.exp(g_last - gc)
    v_new = u - _smm(wk, s)
    o = _smm(qg, s) + _smm(attn, v_new)
    s_new = s * jnp.exp(g_last) + _smm_tn(k_tail, v_new)
    return o, s_new


def _rwkv_chunk(st, r, k, v, a, b, lw):
    c = CHUNK
    incl, strict, _, _ = _tri_masks(c)
    lane = lax.broadcasted_iota(jnp.int32, (c, 2 * B_N), 1)
    first = lane < B_N
    bi = lax.broadcasted_iota(jnp.int32, (2 * B_N, 2 * B_N), 0) < B_N
    bj = lax.broadcasted_iota(jnp.int32, (2 * B_N, 2 * B_N), 1) < B_N
    blockdiag = bi == bj
    cum = _cumsum_rows(lw)
    e_pos = jnp.exp(cum)
    e_neg = jnp.exp(-cum)
    rt = r * e_pos
    at = a * jnp.exp(cum - lw)
    kt = k * e_neg
    bt = b * e_neg
    a_s0 = _smm_nt(at, st)
    r_s0 = _smm_nt(rt, st)
    u = jnp.zeros((c, 2 * B_N), f32)
    for sel in (first, jnp.logical_not(first)):
        at_h = jnp.where(sel, at, 0.0)
        ab = jnp.where(strict, _smm_nt(at_h, bt), 0.0)
        ak = jnp.where(strict, _smm_nt(at_h, kt), 0.0)
        t_h = _unit_lower_inv(ab)
        u_h = _smm(t_h, jnp.where(sel, a_s0, 0.0) + _smm(ak, jnp.where(sel, v, 0.0)))
        u = u + u_h
    y = r_s0
    for sel in (first, jnp.logical_not(first)):
        rt_h = jnp.where(sel, rt, 0.0)
        rb = jnp.where(incl, _smm_nt(rt_h, bt), 0.0)
        rk = jnp.where(incl, _smm_nt(rt_h, kt), 0.0)
        y = y + _smm(rb, jnp.where(sel, u, 0.0)) + _smm(rk, jnp.where(sel, v, 0.0))
    cl = jnp.sum(lw, axis=0, keepdims=True)
    dec = jnp.exp(cl - cum)
    st_new = st * jnp.exp(cl) + jnp.where(blockdiag, _smm_tn(u, b * dec) + _smm_tn(v, k * dec), 0.0)
    return y, st_new


GROUPS_PER_STEP = 8


def _scan_specs(ins, col_offs, n_chunks, reverse):
    gw = GROUPS_PER_STEP * LANES
    cidx = (lambda c: n_chunks - 1 - c) if reverse else (lambda c: c)
    specs = []
    for a, off in zip(ins, col_offs):
        if a.ndim == 2:
            assert off % gw == 0
            specs.append(pl.BlockSpec((CHUNK, gw), lambda h, c, o=off // gw: (cidx(c), h + o)))
        else:
            specs.append(pl.BlockSpec((GROUPS_PER_STEP, CHUNK, 1), lambda h, c: (h, cidx(c), 0)))
    return specs, cidx


def _group_vals(refs, g):
    return [r[:, g * LANES:(g + 1) * LANES] if len(r.shape) == 2 else r[g] for r in refs]


def _scan_fwd(chunk_fn, ins, col_offs, n_groups, n_chunks, state_shape, name):
    n_in = len(ins)
    gps = GROUPS_PER_STEP
    t = ins[0].shape[0]

    def body(*refs):
        in_refs = refs[:n_in]
        o_ref, s0_ref, st = refs[n_in:]

        @pl.when(pl.program_id(1) == 0)
        def _():
            st[...] = jnp.zeros_like(st)

        states = st[...]
        vals = [jnp.stack(col) for col in zip(*[_group_vals(in_refs, g) for g in range(gps)])]
        o, s_new = jax.vmap(chunk_fn)(states, *vals)
        s0_ref[...] = states
        st[...] = s_new
        for g in range(gps):
            o_ref[:, g * LANES:(g + 1) * LANES] = o[g]

    specs, _ = _scan_specs(ins, col_offs, n_chunks, False)
    return pl.pallas_call(
        body, name=name, grid=(n_groups // gps, n_chunks), in_specs=specs,
        out_specs=[pl.BlockSpec((CHUNK, gps * LANES), lambda h, c: (c, h)),
                   pl.BlockSpec((gps, None) + state_shape, lambda h, c: (h, c, 0, 0))],
        out_shape=[jax.ShapeDtypeStruct((t, n_groups * LANES), f32),
                   jax.ShapeDtypeStruct((n_groups, n_chunks) + state_shape, f32)],
        scratch_shapes=[pltpu.VMEM((gps,) + state_shape, f32)],
        compiler_params=pltpu.CompilerParams(dimension_semantics=("parallel", "arbitrary")),
    )(*ins)


def _scan_bwd(chunk_fn, s0s, ins, col_offs, d_out, n_groups, n_chunks, state_shape, name):
    n_in = len(ins)
    gps = GROUPS_PER_STEP
    t = d_out.shape[0]

    def body(*refs):
        s0_ref = refs[0]
        in_refs = refs[1:1 + n_in]
        do_ref = refs[1 + n_in]
        g_refs = refs[2 + n_in:2 + 2 * n_in]
        dst = refs[2 + 2 * n_in]

        @pl.when(pl.program_id(1) == 0)
        def _():
            dst[...] = jnp.zeros_like(dst)

        vals = [jnp.stack(col) for col in zip(*[_group_vals(in_refs, g) for g in range(gps)])]
        d_o = jnp.stack([do_ref[:, g * LANES:(g + 1) * LANES] for g in range(gps)])
        _, vjp = jax.vjp(jax.vmap(chunk_fn), s0_ref[...], *vals)
        gs = vjp((d_o, dst[...]))
        dst[...] = gs[0]
        for g_ref, gv in zip(g_refs, gs[1:]):
            if len(g_ref.shape) == 2:
                for g in range(gps):
                    g_ref[:, g * LANES:(g + 1) * LANES] = gv[g]
            else:
                g_ref[...] = gv

    specs, cidx = _scan_specs(ins, col_offs, n_chunks, True)
    out_lane = pl.BlockSpec((CHUNK, gps * LANES), lambda h, c: (cidx(c), h))
    g_specs = [out_lane if a.ndim == 2 else sp for a, sp in zip(ins, specs)]
    g_shapes = [(t, n_groups * LANES) if a.ndim == 2 else a.shape for a in ins]
    s0_spec = pl.BlockSpec((gps, None) + state_shape, lambda h, c: (h, cidx(c), 0, 0))
    return pl.pallas_call(
        body, name=name, grid=(n_groups // gps, n_chunks), in_specs=[s0_spec] + specs + [out_lane],
        out_specs=g_specs, out_shape=[jax.ShapeDtypeStruct(sh, f32) for sh in g_shapes],
        scratch_shapes=[pltpu.VMEM((gps,) + state_shape, f32)],
        compiler_params=pltpu.CompilerParams(dimension_semantics=("parallel", "arbitrary")),
    )(s0s, *ins, d_out)


def _rms_fwd(x, g, name):
    t = x.shape[0]
    tm = _tile(t, 416, 8)
    return _tw_fwd(_f_rms, [x, g], [_row_spec(tm, D), _full_spec(g.shape)],
                   [jax.ShapeDtypeStruct(x.shape, f32)], [_row_spec(tm, D)], (t // tm,), name)[0]


def _rms_bwd(x, g, dy, name):
    t = x.shape[0]
    tm = _tile(t, 416, 8)
    return _tw_bwd(_f_rms, [x, g], [_row_spec(tm, D), _full_spec(g.shape)], [dy], [_row_spec(tm, D)],
                   ['tile', 'acc'], (t // tm,), name)


def _ffn_fwd(h, gain, wg, wu, wd, tag):
    xn = _rms_fwd(h, gain, f"{tag}_rms")
    gate = _matmul(xn, wg, name=f"{tag}_gate")
    up = _matmul(xn, wu, name=f"{tag}_up")
    t = h.shape[0]
    tm = _tile(t, 208, 8)
    act = _tw_fwd(_f_swiglu, [gate, up], [_row_spec(tm, D_FF)] * 2, [jax.ShapeDtypeStruct((t, D_FF), f32)],
                  [_row_spec(tm, D_FF)], (t // tm,), f"{tag}_act")[0]
    out = _matmul(act, wd, res=h, scale=0.5, name=f"{tag}_down")
    return out, (xn, gate, up, act)


def _ffn_bwd(h, gain, wg, wu, wd, saved, dout, tag):
    xn, gate, up, act = saved
    t = h.shape[0]
    d_wd = _matmul(act, dout, ta=True, scale=0.5, name=f"{tag}_dwd")
    d_act = _matmul(dout, wd, tb=True, scale=0.5, name=f"{tag}_dact")
    tm = _tile(t, 208, 8)
    d_gate, d_up = _tw_bwd(_f_swiglu, [gate, up], [_row_spec(tm, D_FF)] * 2, [d_act], [_row_spec(tm, D_FF)],
                           ['tile', 'tile'], (t // tm,), f"{tag}_dactf")
    d_wg = _matmul(xn, d_gate, ta=True, name=f"{tag}_dwg")
    d_wu = _matmul(xn, d_up, ta=True, name=f"{tag}_dwu")
    d_xn = _matmul(d_gate, wg, tb=True, name=f"{tag}_dxn_g")
    d_xn = _matmul(d_up, wu, tb=True, res=d_xn, name=f"{tag}_dxn_u")
    d_hn, d_gain = _rms_bwd(h, gain, d_xn, f"{tag}_drms")
    return d_hn, d_gain, d_wg, d_wu, d_wd


def _col_spec(t, first_block):
    return pl.BlockSpec((t, LANES), lambda j, fb=first_block: (0, j + fb))


def _local_step(h0, tgt, w):
    t = h0.shape[0]
    assert t % CHUNK == 0
    nc = t // CHUNK
    grads = {}

    h1, ffn1_saved = _ffn_fwd(h0, w['ffn1_norm'], w['ffn1_wg'], w['ffn1_wu'], w['ffn1_wd'], "ffn1")
    u = _rms_fwd(h1, w['mix_norm'], "mix_rms")
    z = _matmul(u, w['w_in_p'], name="in_proj")
    zs = z[:, 9216:9216 + 304]
    abeta, aalpha = zs[:, 288:296], zs[:, 296:304]

    conv_w = w['a_conv_w']
    conv_fns = [functools.partial(_f_conv, norm=True, scale=A_DK ** -0.5),
                functools.partial(_f_conv, norm=True, scale=1.0),
                functools.partial(_f_conv, norm=False, scale=1.0)]
    qkv = []
    for idx, fn in enumerate(conv_fns):
        qkv.append(_tw_fwd(fn, [z, conv_w], [_col_spec(t, 8 * idx), pl.BlockSpec((4, LANES), lambda j, o=8 * idx: (0, j + o))],
                           [jax.ShapeDtypeStruct((t, D), f32)], [_col_spec(t, 0)], (A_HEADS,), f"a_conv{idx}")[0])
    aq, ak, av = qkv
    tmg = _tile(t, 1040, 8)
    dg_fn = functools.partial(_f_dgates, tm=tmg)
    dg_specs = [_row_spec(tmg, A_HEADS)] * 2 + [_full_spec((1, A_HEADS))] * 2
    beta, gdec = _tw_fwd(dg_fn, [abeta, aalpha, w['a_log_rate'], w['a_dt_bias']], dg_specs,
                         [jax.ShapeDtypeStruct((t, A_HEADS), f32)] * 2, [_row_spec(tmg, A_HEADS)] * 2, (t // tmg,),
                         "a_gates", with_pid=True)
    beta_h = beta.T[:, :, None]
    gdec_h = gdec.T[:, :, None]
    a_ins = [aq, ak, av, beta_h, gdec_h]
    a_offs = [0] * 5
    o_scan, a_s0 = _scan_fwd(_delta_chunk, a_ins, a_offs, A_HEADS, nc, (A_DK, A_DK), "a_scan")

    mu = w['b_shift_mu']
    mu_rkv, mu_s = mu[:, :3072], mu[:, 3072:]
    zf_rkv = _tw_fwd(_f_tshift, [z, mu_rkv], [_col_spec(t, 32), pl.BlockSpec((1, LANES), lambda j: (0, j))],
                     [jax.ShapeDtypeStruct((t, 3072), f32)], [_col_spec(t, 0)], (24,), "b_shift")[0]
    zs_b = zs[:, :288]
    zf_s = _tw_fwd(_f_tshift, [zs_b, mu_s], [_full_spec((t, 288)), _full_spec((1, 288))],
                   [jax.ShapeDtypeStruct((t, 288), f32)], [_full_spec((t, 288))], (1,), "b_shift_s")[0]
    wdf, adf, gdf = zf_s[:, 0:64], zf_s[:, 64:128], zf_s[:, 128:288]
    tmr = _tile(t, 160, 8)
    pre_params = [w['b_w0'], w['b_w_up'], w['b_a0'], w['b_a_up'], w['b_g_up'], w['b_k_k'], w['b_k_a']]
    pre_ins = [zf_rkv, wdf, adf, gdf] + pre_params
    pre_specs = ([_row_spec(tmr, D, 1), _row_spec(tmr, 64), _row_spec(tmr, 64), _row_spec(tmr, 160)]
                 + [_full_spec(p.shape) for p in pre_params])
    lw, kmod, a_s, b_s, bgate = _tw_fwd(_f_rwkv_pre, pre_ins, pre_specs, [jax.ShapeDtypeStruct((t, D), f32)] * 5,
                                        [_row_spec(tmr, D)] * 5, (t // tmr,), "b_pre")
    b_ins = [zf_rkv, kmod, zf_rkv, a_s, b_s, lw]
    b_offs = [0, 0, 2 * D, 0, 0, 0]
    y_scan, b_s0 = _scan_fwd(_rwkv_chunk, b_ins, b_offs, B_HEADS // 2, nc, (2 * B_N, 2 * B_N), "b_scan")

    out_gain_t = jnp.tile(w['a_out_norm'], (1, A_HEADS))
    r_k = w['b_r_k'].reshape(1, D)
    post_params = [out_gain_t, w['b_ln_gain'], w['b_ln_bias'], r_k]
    post_ins = [o_scan, z, y_scan, zf_rkv, kmod, zf_rkv, bgate, z, z] + post_params
    post_specs = ([_row_spec(tmr, D), _row_spec(tmr, D, 3), _row_spec(tmr, D), _row_spec(tmr, D, 0), _row_spec(tmr, D),
                   _row_spec(tmr, D, 2), _row_spec(tmr, D), _row_spec(tmr, D, 7), _row_spec(tmr, D, 8)]
                  + [_full_spec((1, D))] * 4)
    merged = _tw_fwd(_f_mix_post, post_ins, post_specs, [jax.ShapeDtypeStruct((t, D), f32)], [_row_spec(tmr, D)],
                     (t // tmr,), "mix_post")[0]
    h2 = _matmul(merged, w['w_out'], res=h1, name="out_proj")
    h3, ffn2_saved = _ffn_fwd(h2, w['ffn2_norm'], w['ffn2_wg'], w['ffn2_wu'], w['ffn2_wd'], "ffn2")

    tml = _tile(t, 416, 8)
    fnorm = w['final_norm']
    loss_fn = functools.partial(_f_loss, tm=tml)
    loss_specs = [_row_spec(tml, D), _full_spec((1, D)), _row_spec(tml, D)]
    loss_parts = _tw_fwd(loss_fn, [h3, fnorm, tgt], loss_specs, [jax.ShapeDtypeStruct((t // tml, 1, 1), f32)],
                         [pl.BlockSpec((None, 1, 1), lambda i: (i, 0, 0))], (t // tml,), "loss", with_pid=True)[0]
    loss = jnp.sum(loss_parts)
    ones = jnp.ones((t // tml, 1, 1), f32)
    d_h3, grads['final_norm'] = _tw_bwd(loss_fn, [h3, fnorm, tgt], loss_specs, [ones],
                                        [pl.BlockSpec((None, 1, 1), lambda i: (i, 0, 0))], ['tile', 'acc', None],
                                        (t // tml,), "loss_bwd", with_pid=True)

    d_hn, grads['ffn2_norm'], grads['ffn2_wg'], grads['ffn2_wu'], grads['ffn2_wd'] = _ffn_bwd(
        h2, w['ffn2_norm'], w['ffn2_wg'], w['ffn2_wu'], w['ffn2_wd'], ffn2_saved, d_h3, "ffn2")
    d_h2 = _add(d_h3, d_hn, "add_h2")
    grads['w_out'] = _matmul(merged, d_h2, ta=True, name="d_w_out")
    d_merged = _matmul(d_h2, w['w_out'], tb=True, name="d_merged")

    win = ('tile', (t, D), _row_spec(tmr, D))
    post_kinds = ['tile', win, 'tile', win, 'tile', win, 'tile', win, win] + ['acc'] * 4
    (d_o, d_az, d_y, d_r1, d_kmod1, d_v1, d_bgate, d_ga, d_gb,
     d_out_gain_t, grads['b_ln_gain'], grads['b_ln_bias'], d_r_k) = _tw_bwd(
        _f_mix_post, post_ins, post_specs, [d_merged], [_row_spec(tmr, D)], post_kinds, (t // tmr,), "mix_post_bwd")
    grads['a_out_norm'] = jnp.sum(d_out_gain_t.reshape(A_HEADS, A_DK), axis=0, keepdims=True)
    grads['b_r_k'] = d_r_k.reshape(1, B_HEADS, B_N)

    d_r2, d_kmod2, d_v2, d_as, d_bs, d_lw = _scan_bwd(_rwkv_chunk, b_s0, b_ins, b_offs, d_y, B_HEADS // 2, nc,
                                                      (2 * B_N, 2 * B_N), "b_scan_bwd")
    d_kmod = _add(d_kmod1, d_kmod2, "add_kmod")
    pre_kinds = [win] + ['tile'] * 3 + ['acc'] * 7
    pre_ct_specs = [_row_spec(tmr, D)] * 5
    (d_zf_k, d_wdf, d_adf, d_gdf, grads['b_w0'], grads['b_w_up'], grads['b_a0'], grads['b_a_up'], grads['b_g_up'],
     grads['b_k_k'], grads['b_k_a']) = _tw_bwd(
        _f_rwkv_pre, pre_ins, pre_specs, [d_lw, d_kmod, d_as, d_bs, d_bgate], pre_ct_specs, pre_kinds, (t // tmr,),
        "b_pre_bwd")
    d_zf_rkv = _assemble3(d_r1, d_r2, d_zf_k, d_v1, d_v2, "b_dzf")
    d_zb_rkv, d_mu_rkv = _tw_bwd(_f_tshift, [z, mu_rkv], [_col_spec(t, 32), pl.BlockSpec((1, LANES), lambda j: (0, j))],
                                 [d_zf_rkv], [_col_spec(t, 0)], [('tile', (t, 3072), _col_spec(t, 0)), 'tile'], (24,),
                                 "b_shift_bwd")
    d_zf_s = jnp.concatenate([d_wdf, d_adf, d_gdf], axis=1)
    d_zs_b, d_mu_s = _tw_bwd(_f_tshift, [zs_b, mu_s], [_full_spec((t, 288)), _full_spec((1, 288))], [d_zf_s],
                             [_full_spec((t, 288))], ['tile', 'tile'], (1,), "b_shift_s_bwd")
    grads['b_shift_mu'] = jnp.concatenate([d_mu_rkv, d_mu_s], axis=1)

    d_aq, d_ak, d_av, d_beta_h, d_g_h = _scan_bwd(_delta_chunk, a_s0, a_ins, a_offs, d_o, A_HEADS, nc, (A_DK, A_DK),
                                                  "a_scan_bwd")
    d_beta = d_beta_h[:, :, 0].T
    d_gdec = d_g_h[:, :, 0].T
    d_abeta, d_aalpha, grads['a_log_rate'], grads['a_dt_bias'] = _tw_bwd(
        dg_fn, [abeta, aalpha, w['a_log_rate'], w['a_dt_bias']], dg_specs, [d_beta, d_gdec],
        [_row_spec(tmg, A_HEADS)] * 2, ['tile', 'tile', 'acc', 'acc'], (t // tmg,), "a_gates_bwd", with_pid=True)
    d_zqkv, d_conv = [], []
    for idx, (fn, ct) in enumerate(zip(conv_fns, (d_aq, d_ak, d_av))):
        dz_i, dw_i = _conv_bwd(fn, z, conv_w, ct, idx, t)
        d_zqkv.append(dz_i)
        d_conv.append(dw_i)
    grads['a_conv_w'] = jnp.concatenate(d_conv, axis=1)

    d_z = jnp.concatenate(
        d_zqkv + [d_az, d_zb_rkv, d_ga, d_gb, d_zs_b, d_abeta, d_aalpha, jnp.zeros((t, ZP - 9216 - 304), f32)], axis=1)
    grads['w_in_p'] = _matmul(u, d_z, ta=True, name="d_w_in")
    d_u = _matmul(d_z, w['w_in_p'], tb=True, name="d_u")
    d_h1n, grads['mix_norm'] = _rms_bwd(h1, w['mix_norm'], d_u, "mix_drms")
    d_h1 = _add(d_h2, d_h1n, "add_h1")
    d_h0n, grads['ffn1_norm'], grads['ffn1_wg'], grads['ffn1_wu'], grads['ffn1_wd'] = _ffn_bwd(
        h0, w['ffn1_norm'], w['ffn1_wg'], w['ffn1_wu'], w['ffn1_wd'], ffn1_saved, d_h1, "ffn1")
    d_h0 = _add(d_h1, d_h0n, "add_h0")
    return loss, d_h0, grads


_WIN_SEGMENTS = ((0, 4096), (4112, 7184), (7472, 9520), (7184, 7472), (4096, 4112))


def _win_to_padded(w_in):
    parts = [w_in[:, a:b] for a, b in _WIN_SEGMENTS]
    parts.append(jnp.zeros((w_in.shape[0], ZP - IN_TOTAL), w_in.dtype))
    return jnp.concatenate(parts, axis=1)


def _win_from_padded(w_p):
    widths = [b - a for a, b in _WIN_SEGMENTS]
    offs = [sum(widths[:i]) for i in range(len(widths))]
    seg = {a: w_p[:, o:o + wd] for (a, _), o, wd in zip(_WIN_SEGMENTS, offs, widths)}
    return jnp.concatenate([seg[a] for a in sorted(seg)], axis=1)


def _add(a, b, name):
    t, c = a.shape
    tm = _tile(t, 416, 8)
    return _tw_fwd(lambda x, y: (x + y,), [a, b], [_row_spec(tm, c)] * 2, [jax.ShapeDtypeStruct(a.shape, f32)],
                   [_row_spec(tm, c)], (t // tm,), name)[0]


def _assemble3(d_r1, d_r2, d_k, d_v1, d_v2, name):
    t = d_r1.shape[0]
    tm = _tile(t, 208, 8)

    def body(r1, r2, kk, v1, v2, o_ref):
        o_ref[:, 0:D] = r1[...] + r2[...]
        o_ref[:, D:2 * D] = kk[...]
        o_ref[:, 2 * D:3 * D] = v1[...] + v2[...]

    return pl.pallas_call(body, name=name, grid=(t // tm,), in_specs=[_row_spec(tm, D)] * 5,
                          out_specs=_row_spec(tm, 3 * D), out_shape=jax.ShapeDtypeStruct((t, 3 * D), f32),
                          )(d_r1, d_r2, d_k, d_v1, d_v2)


def _conv_bwd(fn, z, conv_w, ct, idx, t):
    def body(z_ref, w_ref, ct_ref, dz_ref, dw_ref):
        _, vjp = jax.vjp(lambda a, b: fn(a, b), z_ref[...], w_ref[...])
        dz, dw = vjp((ct_ref[...],))
        dz_ref[...] = dz
        dw_ref[...] = dw

    return pl.pallas_call(
        body, name=f"a_conv{idx}_bwd", grid=(A_HEADS,),
        in_specs=[_col_spec(t, 8 * idx), pl.BlockSpec((4, LANES), lambda j, o=8 * idx: (0, j + o)), _col_spec(t, 0)],
        out_specs=[_col_spec(t, 0), pl.BlockSpec((4, LANES), lambda j: (0, j))],
        out_shape=[jax.ShapeDtypeStruct((t, D), f32), jax.ShapeDtypeStruct((4, D), f32)],
    )(z, conv_w, ct)


def _position():
    return lax.axis_index("x"), lax.axis_index("y"), lax.axis_index("c")


def _flip(v, f):
    return 1 - v if f else v


_CHIP_FLIPS = ((1, 0), (0, 1), (1, 1))
_DEV_FLIPS = tuple((fx, fy, fc) for fx in (0, 1) for fy in (0, 1) for fc in (0, 1) if (fx, fy, fc) != (0, 0, 0))


def _gather_chips(arrs, name):
    n = len(arrs)
    assert all(a.shape[0] % 32 == 0 for a in arrs)
    arrs = [a.reshape(2, a.shape[0] // 2, a.shape[1]) for a in arrs]

    def body(*refs):
        ins, outs = refs[:n], refs[n:2 * n]
        send, recv, fsend, frecv, loc = refs[2 * n:]
        x, y, c = _position()
        me = 2 * x + y
        started = []
        for a in range(n):
            lc = pltpu.make_async_copy(ins[a], outs[a].at[me], loc.at[a])
            lc.start()
            started.append(lc)
        sends, plan = [], []
        for a in range(n):
            for j, (fx, fy) in enumerate(_CHIP_FLIPS):
                px, py = _flip(x, fx), _flip(y, fy)
                p = 2 * px + py
                cp = pltpu.make_async_remote_copy(src_ref=ins[a].at[c], dst_ref=outs[a].at[me, c],
                                                  send_sem=send.at[a, j], recv_sem=recv.at[a, j],
                                                  device_id=(px, py, c), device_id_type=MESH)
                cp.start()
                sends.append(cp)
                landed = pltpu.make_async_remote_copy(src_ref=ins[a].at[c], dst_ref=outs[a].at[p, c],
                                                      send_sem=send.at[a, j], recv_sem=recv.at[a, j],
                                                      device_id=(px, py, c), device_id_type=MESH)
                onward = pltpu.make_async_remote_copy(src_ref=outs[a].at[p, c], dst_ref=outs[a].at[p, c],
                                                      send_sem=fsend.at[a, j], recv_sem=frecv.at[a, j],
                                                      device_id=(x, y, 1 - c), device_id_type=MESH)
                from_sibling = pltpu.make_async_remote_copy(src_ref=outs[a].at[p, 1 - c], dst_ref=outs[a].at[p, 1 - c],
                                                            send_sem=fsend.at[a, j], recv_sem=frecv.at[a, j],
                                                            device_id=(x, y, 1 - c), device_id_type=MESH)
                plan.append((landed, onward, from_sibling))
        for landed, onward, _ in plan:
            landed.wait_recv()
            onward.start()
        for _, _, from_sibling in plan:
            from_sibling.wait_recv()
        for cp in sends:
            cp.wait_send()
        for _, onward, _ in plan:
            onward.wait_send()
        for lc in started:
            lc.wait()

    sems = [pltpu.SemaphoreType.DMA((n, 3))] * 4 + [pltpu.SemaphoreType.DMA((n,))]
    outs = pl.pallas_call(
        body, name=name, in_specs=[ANY] * n, out_specs=[ANY] * n,
        out_shape=[jax.ShapeDtypeStruct((N_CHIPS,) + a.shape, a.dtype) for a in arrs], scratch_shapes=sems,
    )(*arrs)
    return [o.reshape(N_CHIPS, o.shape[1] * o.shape[2], o.shape[3]) for o in outs]


def _swap_halves(g, name):
    _, n, hr, w = g.shape

    def body(g_ref, own_ref, got_ref, send, recv, loc):
        x, y, c = _position()
        lc = pltpu.make_async_copy(g_ref.at[c], own_ref, loc)
        lc.start()
        cp = pltpu.make_async_remote_copy(src_ref=g_ref.at[1 - c], dst_ref=got_ref, send_sem=send, recv_sem=recv,
                                          device_id=(x, y, 1 - c), device_id_type=MESH)
        cp.start()
        cp.wait()
        lc.wait()

    half = jax.ShapeDtypeStruct((n, hr, w), g.dtype)
    return pl.pallas_call(
        body, name=name, in_specs=[ANY], out_specs=[ANY, ANY], out_shape=[half, half],
        scratch_shapes=[pltpu.SemaphoreType.DMA(())] * 3,
    )(g)


def _add_to(a, b, dtype, name):
    n, r, w = a.shape
    tr = _tile(r, 1024, 16)
    spec = pl.BlockSpec((n, tr, w), lambda i: (0, i, 0))

    def body(a_ref, b_ref, o_ref):
        o_ref[...] = (a_ref[...] + b_ref[...]).astype(dtype)

    return pl.pallas_call(body, name=name, grid=(r // tr,), in_specs=[spec, spec], out_specs=spec,
                          out_shape=jax.ShapeDtypeStruct(a.shape, dtype))(a, b)


def _join_halves(q, name):
    hr, w = q.shape

    def body(q_ref, out_ref, send, recv, loc):
        x, y, c = _position()
        lc = pltpu.make_async_copy(q_ref, out_ref.at[c], loc)
        lc.start()
        cp = pltpu.make_async_remote_copy(src_ref=q_ref, dst_ref=out_ref.at[c], send_sem=send, recv_sem=recv,
                                          device_id=(x, y, 1 - c), device_id_type=MESH)
        cp.start()
        cp.wait_send()
        pltpu.make_async_remote_copy(src_ref=q_ref, dst_ref=out_ref.at[1 - c], send_sem=send, recv_sem=recv,
                                     device_id=(x, y, 1 - c), device_id_type=MESH).wait_recv()
        lc.wait()

    return pl.pallas_call(
        body, name=name, in_specs=[ANY], out_specs=ANY, out_shape=jax.ShapeDtypeStruct((2, hr, w), q.dtype),
        scratch_shapes=[pltpu.SemaphoreType.DMA(())] * 3,
    )(q).reshape(2 * hr, w)


def _scatter_chips(g, name):
    def body(g_ref, out_ref, send, recv, loc):
        x, y, c = _position()
        me = 2 * x + y
        lc = pltpu.make_async_copy(g_ref.at[me], out_ref.at[me], loc)
        lc.start()
        sends, recvs = [], []
        for j, (fx, fy) in enumerate(_CHIP_FLIPS):
            px, py = _flip(x, fx), _flip(y, fy)
            p = 2 * px + py
            cp = pltpu.make_async_remote_copy(src_ref=g_ref.at[p], dst_ref=out_ref.at[me], send_sem=send.at[j],
                                              recv_sem=recv.at[j], device_id=(px, py, c), device_id_type=MESH)
            cp.start()
            sends.append(cp)
            recvs.append(pltpu.make_async_remote_copy(src_ref=g_ref.at[me], dst_ref=out_ref.at[p], send_sem=send.at[j],
                                                      recv_sem=recv.at[j], device_id=(px, py, c), device_id_type=MESH))
        for cp in recvs:
            cp.wait_recv()
        for cp in sends:
            cp.wait_send()
        lc.wait()

    return pl.pallas_call(
        body, name=name, in_specs=[ANY], out_specs=ANY, out_shape=jax.ShapeDtypeStruct(g.shape, g.dtype),
        scratch_shapes=[pltpu.SemaphoreType.DMA((3,)), pltpu.SemaphoreType.DMA((3,)), pltpu.SemaphoreType.DMA(())],
    )(g)


def _gather_devices(s, name):
    def body(s_ref, out_ref, send, recv, loc):
        x, y, c = _position()
        me = 4 * x + 2 * y + c
        lc = pltpu.make_async_copy(s_ref, out_ref.at[me], loc)
        lc.start()
        sends, recvs = [], []
        for j, (fx, fy, fc) in enumerate(_DEV_FLIPS):
            px, py, pc = _flip(x, fx), _flip(y, fy), _flip(c, fc)
            cp = pltpu.make_async_remote_copy(src_ref=s_ref, dst_ref=out_ref.at[me], send_sem=send.at[j],
                                              recv_sem=recv.at[j], device_id=(px, py, pc), device_id_type=MESH)
            cp.start()
            sends.append(cp)
            recvs.append(pltpu.make_async_remote_copy(
                src_ref=s_ref, dst_ref=out_ref.at[4 * px + 2 * py + pc], send_sem=send.at[j], recv_sem=recv.at[j],
                device_id=(px, py, pc), device_id_type=MESH))
        for cp in recvs:
            cp.wait_recv()
        for cp in sends:
            cp.wait_send()
        lc.wait()

    return pl.pallas_call(
        body, name=name, in_specs=[ANY], out_specs=ANY, out_shape=jax.ShapeDtypeStruct((N_DEV,) + s.shape, s.dtype),
        scratch_shapes=[pltpu.SemaphoreType.DMA((7,)), pltpu.SemaphoreType.DMA((7,)), pltpu.SemaphoreType.DMA(())],
    )(s)


def _sum_slots(a, name):
    s, r, c = a.shape
    tr = _tile(r, 2048, 16)

    def body(a_ref, o_ref):
        acc = a_ref[0].astype(f32)
        for i in range(1, s):
            acc = acc + a_ref[i].astype(f32)
        o_ref[...] = acc

    return pl.pallas_call(body, name=name, grid=(r // tr,), in_specs=[pl.BlockSpec((s, tr, c), lambda i: (0, i, 0))],
                          out_specs=pl.BlockSpec((tr, c), lambda i: (i, 0)),
                          out_shape=jax.ShapeDtypeStruct((r, c), f32))(a)


def _adamw(w, g_parts, m, v, name):
    shape = w.shape
    size = w.size
    view = (size // LANES, LANES) if size % LANES == 0 else (1, size)
    rows = view[0]
    tr = _tile(rows, 2048, 8) if rows > 2048 else rows
    n_g = len(g_parts)

    def body(*refs):
        w_ref = refs[0]
        g_refs = refs[1:1 + n_g]
        m_ref, v_ref, g_out, d_out, m_out, v_out = refs[1 + n_g:]
        g = g_refs[0][...]
        for gr in g_refs[1:]:
            g = g + gr[...]
        m_new = ADAM_B1 * m_ref[...] + (1.0 - ADAM_B1) * g
        v_new = ADAM_B2 * v_ref[...] + (1.0 - ADAM_B2) * (g * g)
        m_hat = m_new / (1.0 - ADAM_B1 ** ADAM_STEP)
        v_hat = v_new / (1.0 - ADAM_B2 ** ADAM_STEP)
        g_out[...] = g
        d_out[...] = -ADAM_LR * (m_hat / (jnp.sqrt(v_hat) + ADAM_EPS) + ADAM_WD * w_ref[...])
        m_out[...] = m_new
        v_out[...] = v_new

    spec = pl.BlockSpec((tr, view[1]), lambda i: (i, 0))
    args = [w.reshape(view)] + [g.reshape(view) for g in g_parts] + [m.reshape(view), v.reshape(view)]
    outs = pl.pallas_call(body, name=name, grid=(rows // tr,), in_specs=[spec] * len(args), out_specs=[spec] * 4,
                          out_shape=[jax.ShapeDtypeStruct(view, f32)] * 4)(*args)
    return [o.reshape(shape) for o in outs]


_BIG = ('ffn1_w_gu', 'ffn1_w_down', 'w_in', 'w_out', 'ffn2_w_gu', 'ffn2_w_down')
_SMALL_SHARDED = ('meta_tokens', 'a_conv_w', 'b_w_up', 'b_a_up', 'b_g_up')
_WEIGHTS = ('meta_tokens', 'ffn1_norm', 'ffn1_w_gu', 'ffn1_w_down', 'mix_norm', 'w_in', 'a_conv_w', 'a_log_rate',
            'a_dt_bias', 'a_out_norm', 'b_shift_mu', 'b_w0', 'b_w_up', 'b_a0', 'b_a_up', 'b_g_up', 'b_k_k', 'b_k_a',
            'b_r_k', 'b_ln_gain', 'b_ln_bias', 'w_out', 'ffn2_norm', 'ffn2_w_gu', 'ffn2_w_down', 'final_norm')
_SMALL = tuple(n for n in _WEIGHTS if n not in _BIG)


def _rows_of(shape):
    n = 1
    for d in shape:
        n *= d
    return n, -(-n // LANES)


def _pack(arrs, dtype, row_mult=32):
    parts, total = [], 0
    for a in arrs:
        n, rows = _rows_of(a.shape)
        flat = a.reshape(-1).astype(dtype)
        if n % LANES:
            flat = jnp.pad(flat, (0, rows * LANES - n))
        parts.append(flat)
        total += rows
    extra = -total % row_mult
    if extra:
        parts.append(jnp.zeros((extra * LANES,), dtype))
    return jnp.concatenate(parts).reshape(total + extra, LANES)


def _unpack(packed, shapes, lead=()):
    out, off = [], 0
    for sh in shapes:
        n, rows = _rows_of(sh)
        seg = packed[..., off:off + rows, :]
        if n % LANES:
            seg = seg.reshape(lead + (-1,))[..., :n]
        out.append(seg.reshape(lead + tuple(sh)))
        off += rows
    return out


def _cols_from_shards(s):
    return jnp.concatenate([s[i] for i in range(N_CHIPS)], axis=-1)


def _cols_to_shards(a):
    r, c = a.shape
    return a.reshape(r, N_CHIPS, c // N_CHIPS).transpose(1, 0, 2)


def kernel(x, meta_tokens, ffn1_norm, ffn1_w_gu, ffn1_w_down, mix_norm, w_in, a_conv_w, a_log_rate, a_dt_bias, a_out_norm, b_shift_mu, b_w0, b_w_up, b_a0, b_a_up, b_g_up, b_k_k, b_k_a, b_r_k, b_ln_gain, b_ln_bias, w_out, ffn2_norm, ffn2_w_gu, ffn2_w_down, final_norm, loss_target, m_meta_tokens, m_ffn1_norm, m_ffn1_w_gu, m_ffn1_w_down, m_mix_norm, m_w_in, m_a_conv_w, m_a_log_rate, m_a_dt_bias, m_a_out_norm, m_b_shift_mu, m_b_w0, m_b_w_up, m_b_a0, m_b_a_up, m_b_g_up, m_b_k_k, m_b_k_a, m_b_r_k, m_b_ln_gain, m_b_ln_bias, m_w_out, m_ffn2_norm, m_ffn2_w_gu, m_ffn2_w_down, m_final_norm, v_meta_tokens, v_ffn1_norm, v_ffn1_w_gu, v_ffn1_w_down, v_mix_norm, v_w_in, v_a_conv_w, v_a_log_rate, v_a_dt_bias, v_a_out_norm, v_b_shift_mu, v_b_w0, v_b_w_up, v_b_a0, v_b_a_up, v_b_g_up, v_b_k_k, v_b_k_a, v_b_r_k, v_b_ln_gain, v_b_ln_bias, v_w_out, v_ffn2_norm, v_ffn2_w_gu, v_ffn2_w_down, v_final_norm):
    args = locals()
    wts = {n: args[n] for n in _WEIGHTS}
    mom = {n: args["m_" + n] for n in _WEIGHTS}
    var = {n: args["v_" + n] for n in _WEIGHTS}
    chip = 2 * lax.axis_index("x") + lax.axis_index("y")

    big_shapes = [wts[n].shape[1:] for n in _BIG]
    small_shapes = [wts[n].shape[-2:] for n in _SMALL_SHARDED]
    big_packed = _pack([wts[n] for n in _BIG], bf16)
    small_packed = _pack([wts[n] for n in _SMALL_SHARDED], f32)
    big_all, small_all = _gather_chips([big_packed, small_packed], "gather_weights")
    gu1, dn1, w_in_s, w_out_s, gu2, dn2 = _unpack(big_all, big_shapes, (N_CHIPS,))
    meta_s, conv_s, wup_s, aup_s, gup_s = _unpack(small_all, small_shapes, (N_CHIPS,))
    w = {
        'ffn1_norm': ffn1_norm, 'mix_norm': mix_norm, 'ffn2_norm': ffn2_norm, 'final_norm': final_norm[None, :],
        'ffn1_wg': jnp.concatenate([gu1[0], gu1[1]], axis=1), 'ffn1_wu': jnp.concatenate([gu1[2], gu1[3]], axis=1),
        'ffn1_wd': dn1.reshape(D_FF, D),
        'ffn2_wg': jnp.concatenate([gu2[0], gu2[1]], axis=1), 'ffn2_wu': jnp.concatenate([gu2[2], gu2[3]], axis=1),
        'ffn2_wd': dn2.reshape(D_FF, D),
        'w_in_p': _win_to_padded(_cols_from_shards(w_in_s)), 'w_out': w_out_s.reshape(D, D),
        'a_conv_w': _cols_from_shards(conv_s), 'b_w_up': _cols_from_shards(wup_s), 'b_a_up': _cols_from_shards(aup_s),
        'b_g_up': _cols_from_shards(gup_s),
        'a_log_rate': a_log_rate, 'a_dt_bias': a_dt_bias, 'a_out_norm': a_out_norm, 'b_shift_mu': b_shift_mu,
        'b_w0': b_w0, 'b_a0': b_a0, 'b_k_k': b_k_k, 'b_k_a': b_k_a, 'b_r_k': b_r_k, 'b_ln_gain': b_ln_gain,
        'b_ln_bias': b_ln_bias,
    }
    meta_full = _cols_from_shards(meta_s)

    h0 = jnp.concatenate([jnp.zeros((PAD, D), f32), meta_full, x[0]], axis=0)
    tgt = jnp.concatenate([jnp.zeros((SKIP, D), f32), loss_target[0]], axis=0)
    loss_local, d_h0, g = _local_step(h0, tgt, w)
    loss = lax.psum(loss_local, ("x", "y", "c"))
    grad_x = d_h0[SKIP:][None]

    big_grads = [
        _cols_to_shards(jnp.concatenate([g['ffn1_wg'], g['ffn1_wu']], axis=1)),
        g['ffn1_wd'].reshape(N_CHIPS, D_FF // N_CHIPS, D),
        _cols_to_shards(_win_from_padded(g['w_in_p'])),
        g['w_out'].reshape(N_CHIPS, D // N_CHIPS, D),
        _cols_to_shards(jnp.concatenate([g['ffn2_wg'], g['ffn2_wu']], axis=1)),
        g['ffn2_wd'].reshape(N_CHIPS, D_FF // N_CHIPS, D),
    ]
    g_packed = jnp.concatenate([a.reshape(N_CHIPS, -1, LANES) for a in big_grads], axis=1)
    assert g_packed.shape[1] == big_packed.shape[0]
    rows = g_packed.shape[1]
    g_halves = g_packed.reshape(N_CHIPS, 2, rows // 2, LANES).transpose(1, 0, 2, 3)
    own_half, sib_half = _swap_halves(g_halves, "swap_halves")
    chip_half = _add_to(own_half, sib_half, bf16, "add_sibling")
    summed_half = _sum_slots(_scatter_chips(chip_half, "scatter_grads"), "sum_chips")
    big_parts = _unpack(_join_halves(summed_half, "join_halves"), big_shapes)

    small_full = {
        'meta_tokens': d_h0[PAD:SKIP], 'ffn1_norm': g['ffn1_norm'], 'mix_norm': g['mix_norm'], 'a_conv_w': g['a_conv_w'],
        'a_log_rate': g['a_log_rate'], 'a_dt_bias': g['a_dt_bias'], 'a_out_norm': g['a_out_norm'],
        'b_shift_mu': g['b_shift_mu'], 'b_w0': g['b_w0'], 'b_w_up': g['b_w_up'], 'b_a0': g['b_a0'], 'b_a_up': g['b_a_up'],
        'b_g_up': g['b_g_up'], 'b_k_k': g['b_k_k'], 'b_k_a': g['b_k_a'], 'b_r_k': g['b_r_k'], 'b_ln_gain': g['b_ln_gain'],
        'b_ln_bias': g['b_ln_bias'], 'ffn2_norm': g['ffn2_norm'], 'final_norm': g['final_norm'],
    }
    s_shapes = [small_full[n].shape for n in _SMALL]
    s_sum = _sum_slots(_gather_devices(_pack([small_full[n] for n in _SMALL], f32, row_mult=256), "gather_small"),
                       "sum_small")
    s_parts = dict(zip(_SMALL, _unpack(s_sum, s_shapes)))

    grad, delta, new_m, new_v = {}, {}, {}, {}
    for n, a in zip(_BIG, big_parts):
        grad[n], delta[n], new_m[n], new_v[n] = _adamw(wts[n], [a.reshape(wts[n].shape)], mom[n], var[n], f"adamw_{n}")
    for n in _SMALL:
        gs = s_parts[n]
        if n in _SMALL_SHARDED:
            width = wts[n].shape[-1]
            gs = lax.dynamic_slice_in_dim(gs, chip * width, width, axis=gs.ndim - 1)
        gs = gs.reshape(wts[n].shape)
        grad[n], delta[n], new_m[n], new_v[n] = _adamw(wts[n], [gs], mom[n], var[n], f"adamw_{n}")

    return (loss, grad_x, *[grad[n] for n in _WEIGHTS], *[delta[n] for n in _WEIGHTS],
            *[new_m[n] for n in _WEIGHTS], *[new_v[n] for n in _WEIGHTS])
```

```python
import functools

import jax
import jax.numpy as jnp
from jax import lax
from jax.experimental import pallas as pl
from jax.experimental.pallas import tpu as pltpu

f32 = jnp.float32
bf16 = jnp.bfloat16
HI = lax.Precision.HIGHEST
MESH = pl.DeviceIdType.MESH
ANY = pl.BlockSpec(memory_space=pl.ANY)

D = 1024
N_META = 16
CHUNK = 64
PAD = CHUNK - N_META
SKIP = PAD + N_META
EPS = 1e-6
D_FF = 2816
A_HEADS = 8
A_DK = 128
B_HEADS = 16
B_N = 64
B_GN_EPS = B_N * 1e-5
W_LORA, AA_LORA, G_LORA = 64, 64, 160
IN_TOTAL = 9520
ZP = 9600
LANES = 128
N_CHIPS = 4
N_DEV = 8

ADAM_LR, ADAM_B1, ADAM_B2, ADAM_EPS, ADAM_WD, ADAM_STEP = 0.001, 0.9, 0.999, 1e-08, 0.01, 10

MXU_DTYPE = bf16


def _tile(n, cap, mult):
    if n <= cap:
        return n
    best = None
    for t in range(mult, cap + 1, mult):
        if n % t == 0:
            best = t
    assert best is not None, (n, cap, mult)
    return best


def _sigmoid(x):
    return jax.nn.sigmoid(x)


def _silu(x):
    return x * jax.nn.sigmoid(x)


def _softplus(x):
    return jnp.maximum(x, 0.0) + jnp.log(1.0 + jnp.exp(-jnp.abs(x)))


def _head_matrix(c, nh):
    hd = c // nh
    r = lax.broadcasted_iota(jnp.int32, (c, nh), 0)
    h = lax.broadcasted_iota(jnp.int32, (c, nh), 1)
    return (r >= h * hd) & (r < (h + 1) * hd)


def _dot_exact_rhs(x, e, cb):
    dn = (((1,), (cb,)), ((), ()))
    if SCAN_PASSES == 0:
        return lax.dot_general(x, e.astype(f32), dn, precision=HI, preferred_element_type=f32)
    eb = e.astype(bf16)
    hi = x.astype(bf16)
    lo = (x - hi.astype(f32)).astype(bf16)
    return (lax.dot_general(hi, eb, dn, preferred_element_type=f32)
            + lax.dot_general(lo, eb, dn, preferred_element_type=f32))


def _head_sum_impl(x, nh):
    e = _head_matrix(x.shape[-1], nh)
    return _dot_exact_rhs(_dot_exact_rhs(x, e, 0), e, 1)


@functools.partial(jax.custom_vjp, nondiff_argnums=(1,))
def _head_sum(x, nh):
    return _head_sum_impl(x, nh)


def _head_sum_fwd(x, nh):
    return _head_sum_impl(x, nh), None


def _head_sum_bwd(nh, _, g):
    return (_head_sum_impl(g, nh),)


_head_sum.defvjp(_head_sum_fwd, _head_sum_bwd)


@functools.partial(jax.custom_vjp, nondiff_argnums=(1,))
def _shift_rows(x, s):
    n = x.shape[0]
    row = lax.broadcasted_iota(jnp.int32, x.shape, 0)
    if s > 0:
        return jnp.where(row >= s, pltpu.roll(x, s, 0), 0.0)
    return jnp.where(row < n + s, pltpu.roll(x, n + s, 0), 0.0)


def _shift_rows_fwd(x, s):
    return _shift_rows(x, s), None


def _shift_rows_bwd(s, _, g):
    return (_shift_rows(g, -s),)


_shift_rows.defvjp(_shift_rows_fwd, _shift_rows_bwd)


def _matmul(a, b, *, ta=False, tb=False, res=None, scale=1.0, name):
    assert not (ta and tb)
    (ar, ac), (br, bc) = a.shape, b.shape
    m, k = (ac, ar) if ta else (ar, ac)
    n, kb = (br, bc) if tb else (bc, br)
    assert k == kb, (a.shape, b.shape, ta, tb)
    tm = _tile(m, 1408, LANES) if ta else _tile(m, 832, 8)
    tn = _tile(n, 1408, LANES)
    tk = _tile(k, 1040, 8) if ta else _tile(k, 1408, LANES)
    nk = k // tk
    dn = (((0 if ta else 1,), (1 if tb else 0,)), ((), ()))

    def body(*refs):
        if res is not None:
            a_ref, b_ref, r_ref, o_ref, acc = refs
        else:
            a_ref, b_ref, o_ref, acc = refs
        kk = pl.program_id(2)

        @pl.when(kk == 0)
        def _():
            acc[...] = jnp.zeros_like(acc)

        acc[...] += lax.dot_general(a_ref[...].astype(MXU_DTYPE), b_ref[...].astype(MXU_DTYPE), dn,
                                    preferred_element_type=f32,
                                    precision=None if MXU_DTYPE == bf16 else HI)

        @pl.when(kk == nk - 1)
        def _():
            out = acc[...]
            if scale != 1.0:
                out = out * scale
            if res is not None:
                out = r_ref[...] + out
            o_ref[...] = out

    if ta:
        a_spec = pl.BlockSpec((tk, tm), lambda i, j, kk: (kk, i))
    else:
        a_spec = pl.BlockSpec((tm, tk), lambda i, j, kk: (i, kk))
    if tb:
        b_spec = pl.BlockSpec((tn, tk), lambda i, j, kk: (j, kk))
    else:
        b_spec = pl.BlockSpec((tk, tn), lambda i, j, kk: (kk, j))
    in_specs = [a_spec, b_spec]
    args = [a, b]
    if res is not None:
        in_specs.append(pl.BlockSpec((tm, tn), lambda i, j, kk: (i, j)))
        args.append(res)
    return pl.pallas_call(
        body, name=name, grid=(m // tm, n // tn, nk), in_specs=in_specs,
        out_specs=pl.BlockSpec((tm, tn), lambda i, j, kk: (i, j)),
        out_shape=jax.ShapeDtypeStruct((m, n), f32),
        scratch_shapes=[pltpu.VMEM((tm, tn), f32)],
        compiler_params=pltpu.CompilerParams(dimension_semantics=("parallel", "parallel", "arbitrary")),
    )(*args)


def _tw_fwd(fn, ins, in_specs, out_shapes, out_specs, grid, name, with_pid=False):
    n_in = len(ins)

    def body(*refs):
        vals = [r[...] for r in refs[:n_in]]
        outs = fn(pl.program_id(0), *vals) if with_pid else fn(*vals)
        for r, o in zip(refs[n_in:], outs):
            r[...] = o

    return pl.pallas_call(body, name=name, grid=grid, in_specs=in_specs, out_specs=out_specs,
                          out_shape=out_shapes)(*ins)


def _tw_bwd(fn, ins, in_specs, cts, ct_specs, kinds, grid, name, with_pid=False):
    n_in, n_ct = len(ins), len(cts)
    diff = [i for i, kd in enumerate(kinds) if kd is not None]

    def body(*refs):
        vals = [r[...] for r in refs[:n_in]]
        ctv = tuple(r[...] for r in refs[n_in:n_in + n_ct])
        g_refs = refs[n_in + n_ct:]
        pid = pl.program_id(0)

        def f(*dv):
            full = list(vals)
            for i, v in zip(diff, dv):
                full[i] = v
            out = fn(pid, *full) if with_pid else fn(*full)
            return tuple(out)

        _, vjp = jax.vjp(f, *[vals[i] for i in diff])
        gs = vjp(ctv)
        first = pid == 0
        for i2 in range(1, len(grid)):
            first = first & (pl.program_id(i2) == 0)
        for i, g, g_ref in zip(diff, gs, g_refs):
            if kinds[i] != 'acc':
                g_ref[...] = g
            else:
                @pl.when(first)
                def _(g=g, g_ref=g_ref):
                    g_ref[...] = g

                @pl.when(jnp.logical_not(first))
                def _(g=g, g_ref=g_ref):
                    g_ref[...] += g

    zero_map = {1: lambda *a: (0,), 2: lambda *a: (0, 0), 3: lambda *a: (0, 0, 0)}
    out_specs, out_shapes = [], []
    for i in diff:
        if kinds[i] == 'tile':
            out_shapes.append(jax.ShapeDtypeStruct(ins[i].shape, f32))
            out_specs.append(in_specs[i])
        elif kinds[i] == 'acc':
            out_shapes.append(jax.ShapeDtypeStruct(ins[i].shape, f32))
            out_specs.append(pl.BlockSpec(ins[i].shape, zero_map[ins[i].ndim]))
        else:
            out_shapes.append(jax.ShapeDtypeStruct(kinds[i][1], f32))
            out_specs.append(kinds[i][2])
    return pl.pallas_call(body, name=name, grid=grid, in_specs=list(in_specs) + list(ct_specs),
                          out_specs=out_specs, out_shape=out_shapes)(*ins, *cts)


def _row_spec(tm, c, col_block=0):
    return pl.BlockSpec((tm, c), lambda i, cb=col_block: (i, cb))


def _full_spec(shape):
    nd = len(shape)
    return pl.BlockSpec(shape, lambda *a, nd=nd: (0,) * nd)


def _f_rms(x, g):
    return (x * lax.rsqrt(jnp.mean(x * x, axis=-1, keepdims=True) + EPS) * g,)


def _f_swiglu(gate, up):
    return (_silu(gate) * up,)


def _f_loss(pid, h, g, tgt, *, tm):
    y = h * lax.rsqrt(jnp.mean(h * h, axis=-1, keepdims=True) + EPS) * g
    row = pid * tm + lax.broadcasted_iota(jnp.int32, (tm, 1), 0)
    err = jnp.where(row >= SKIP, y - tgt, 0.0)
    per_row = jnp.mean(err * err, axis=-1, keepdims=True)
    return (0.5 * jnp.sum(per_row, axis=0, keepdims=True),)


def _f_conv(x, w, *, norm, scale):
    y = x * w[3:4, :]
    for s in (1, 2, 3):
        y = y + _shift_rows(x, s) * w[3 - s:4 - s, :]
    y = _silu(y)
    if norm:
        y = y * lax.rsqrt(jnp.sum(y * y, axis=-1, keepdims=True) + 1e-6) * scale
    return (y,)


def _f_dgates(pid, abeta, aalpha, log_rate, dt_bias, *, tm):
    row = pid * tm + lax.broadcasted_iota(jnp.int32, (tm, 1), 0)
    live = row >= PAD
    beta = jnp.where(live, _sigmoid(abeta), 0.0)
    g = jnp.where(live, -jnp.exp(log_rate) * _softplus(aalpha + dt_bias), 0.0)
    return beta, g


def _f_tshift(z, mu):
    return (z + (_shift_rows(z, 1) - z) * mu,)


def _f_rwkv_pre(k, wd, ad, gd, w0, w_up, a0, a_up, g_up, k_k, k_a):
    w_log = -_softplus(-(w0 + _smm(jnp.tanh(wd), w_up))) - 0.5
    lw = -jnp.exp(w_log)
    a_lr = _sigmoid(a0 + _smm(ad, a_up))
    gate = _smm(_sigmoid(gd), g_up)
    kkp = k * k_k
    kk = kkp * lax.rsqrt(_head_sum(kkp * kkp, B_HEADS) + 1e-6)
    kmod = k * (1.0 + (a_lr - 1.0) * k_a)
    return lw, kmod, -kk, kk * a_lr, gate


def _f_mix_post(o, az, y, r, kmod, v, gate, ga, gb, out_gain, ln_g, ln_b, r_k):
    ms = _head_sum(o * o, A_HEADS) * (1.0 / A_DK)
    oa = o * lax.rsqrt(ms + EPS) * out_gain * _silu(az)
    mean = _head_sum(y, B_HEADS) * (1.0 / B_N)
    yc = y - mean
    var = _head_sum(yc * yc, B_HEADS) * (1.0 / B_N)
    yn = yc * lax.rsqrt(var + B_GN_EPS) * ln_g + ln_b
    bonus = _head_sum(r * kmod * r_k, B_HEADS) * v
    ob = (yn + bonus) * gate
    return (_sigmoid(ga) * oa + _sigmoid(gb) * ob,)


SCAN_PASSES = 3


def _split2(a):
    hi = a.astype(bf16)
    return hi, (a - hi.astype(f32)).astype(bf16)


def _dot_passes(a, b, ca, cb):
    dn = (((ca,), (cb,)), ((), ()))
    if SCAN_PASSES == 0:
        return lax.dot_general(a, b, dn, precision=HI, preferred_element_type=f32)
    if SCAN_PASSES == 1:
        return lax.dot_general(a.astype(bf16), b.astype(bf16), dn, preferred_element_type=f32)
    ah, al = _split2(a)
    bh, bl = _split2(b)
    return (lax.dot_general(ah, bh, dn, preferred_element_type=f32)
            + (lax.dot_general(ah, bl, dn, preferred_element_type=f32)
               + lax.dot_general(al, bh, dn, preferred_element_type=f32)))


@functools.partial(jax.custom_vjp, nondiff_argnums=(2, 3))
def _sdot(a, b, ca, cb):
    return _dot_passes(a, b, ca, cb)


def _sdot_fwd(a, b, ca, cb):
    return _dot_passes(a, b, ca, cb), (a, b)


def _sdot_bwd(ca, cb, res, g):
    a, b = res
    if (ca, cb) == (1, 0):
        return _dot_passes(g, b, 1, 1), _dot_passes(a, g, 0, 0)
    if (ca, cb) == (1, 1):
        return _dot_passes(g, b, 1, 0), _dot_passes(g, a, 0, 0)
    assert (ca, cb) == (0, 0)
    return _dot_passes(b, g, 1, 1), _dot_passes(a, g, 1, 0)


_sdot.defvjp(_sdot_fwd, _sdot_bwd)


def _smm(a, b):
    return _sdot(a, b, 1, 0)


def _smm_nt(a, b):
    return _sdot(a, b, 1, 1)


def _smm_tn(a, b):
    return _sdot(a, b, 0, 0)


def _tri_dot(x, ca):
    n = x.shape[0]
    incl = _tri_masks(n)[0]
    dn = (((ca,), (0,)), ((), ()))
    if SCAN_PASSES == 0:
        return lax.dot_general(incl.astype(f32), x, dn, precision=HI, preferred_element_type=f32)
    tri = incl.astype(bf16)
    hi, r1 = x.astype(bf16), None
    r1 = x - hi.astype(f32)
    mid = r1.astype(bf16)
    lo = (r1 - mid.astype(f32)).astype(bf16)
    return (lax.dot_general(tri, hi, dn, preferred_element_type=f32)
            + (lax.dot_general(tri, mid, dn, preferred_element_type=f32)
               + lax.dot_general(tri, lo, dn, preferred_element_type=f32)))


@jax.custom_vjp
def _cumsum_rows(x):
    return _tri_dot(x, 1)


def _cumsum_rows_fwd(x):
    return _tri_dot(x, 1), None


def _cumsum_rows_bwd(_, g):
    return (_tri_dot(g, 0),)


_cumsum_rows.defvjp(_cumsum_rows_fwd, _cumsum_rows_bwd)


def _tri_masks(n):
    i = lax.broadcasted_iota(jnp.int32, (n, n), 0)
    j = lax.broadcasted_iota(jnp.int32, (n, n), 1)
    return i >= j, i > j, i == j, i <= j


def _unit_lower_inv(low):
    n = low.shape[0]
    assert n == CHUNK
    _, _, eye, _ = _tri_masks(n)
    acc = eye.astype(f32) + low
    p = low
    for _ in range(5):
        p = _smm(p, p)
        acc = acc + _smm(acc, p)
    return acc


def _delta_chunk(s, q, k, v, beta, g):
    incl, strict, eye, upper = _tri_masks(CHUNK)
    g_row = jnp.sum(jnp.where(eye, g, 0.0), axis=0, keepdims=True)
    gc = jnp.sum(jnp.where(incl, g_row, 0.0), axis=1, keepdims=True)
    gc_row = jnp.sum(jnp.where(upper, g, 0.0), axis=0, keepdims=True)
    decay = jnp.where(incl, jnp.exp(jnp.where(incl, gc - gc_row, 0.0)), 0.0)
    kb = k * beta
    vb = v * beta
    m = jnp.where(strict, _smm_nt(kb, k) * decay, 0.0)
    tinv = _unit_lower_inv(-m)
    u = _smm(tinv, vb)
    wk = _smm(tinv, kb * jnp.exp(gc))
    attn = _smm_nt(q, k) * decay
    qg = q * jnp.exp(gc)
    g_last = jnp.sum(g, axis=0, keepdims=True)
    k_tail = k * jnp.exp(g_last - gc)
    v_new = u - _smm(wk, s)
    o = _smm(qg, s) + _smm(attn, v_new)
    s_new = s * jnp.exp(g_last) + _smm_tn(k_tail, v_new)
    return o, s_new


def _rwkv_chunk(st, r, k, v, a, b, lw):
    c = CHUNK
    incl, strict, _, _ = _tri_masks(c)
    lane = lax.broadcasted_iota(jnp.int32, (c, 2 * B_N), 1)
    first = lane < B_N
    bi = lax.broadcasted_iota(jnp.int32, (2 * B_N, 2 * B_N), 0) < B_N
    bj = lax.broadcasted_iota(jnp.int32, (2 * B_N, 2 * B_N), 1) < B_N
    blockdiag = bi == bj
    cum = _cumsum_rows(lw)
    e_pos = jnp.exp(cum)
    e_neg = jnp.exp(-cum)
    rt = r * e_pos
    at = a * jnp.exp(cum - lw)
    kt = k * e_neg
    bt = b * e_neg
    a_s0 = _smm_nt(at, st)
    r_s0 = _smm_nt(rt, st)
    u = jnp.zeros((c, 2 * B_N), f32)
    for sel in (first, jnp.logical_not(first)):
        at_h = jnp.where(sel, at, 0.0)
        ab = jnp.where(strict, _smm_nt(at_h, bt), 0.0)
        ak = jnp.where(strict, _smm_nt(at_h, kt), 0.0)
        t_h = _unit_lower_inv(ab)
        u_h = _smm(t_h, jnp.where(sel, a_s0, 0.0) + _smm(ak, jnp.where(sel, v, 0.0)))
        u = u + u_h
    y = r_s0
    for sel in (first, jnp.logical_not(first)):
        rt_h = jnp.where(sel, rt, 0.0)
        rb = jnp.where(incl, _smm_nt(rt_h, bt), 0.0)
        rk = jnp.where(incl, _smm_nt(rt_h, kt), 0.0)
        y = y + _smm(rb, jnp.where(sel, u, 0.0)) + _smm(rk, jnp.where(sel, v, 0.0))
    cl = jnp.sum(lw, axis=0, keepdims=True)
    dec = jnp.exp(cl - cum)
    st_new = st * jnp.exp(cl) + jnp.where(blockdiag, _smm_tn(u, b * dec) + _smm_tn(v, k * dec), 0.0)
    return y, st_new


GROUPS_PER_STEP = 8


def _scan_specs(ins, col_offs, n_chunks, reverse):
    gw = GROUPS_PER_STEP * LANES
    cidx = (lambda c: n_chunks - 1 - c) if reverse else (lambda c: c)
    specs = []
    for a, off in zip(ins, col_offs):
        if a.ndim == 2:
            assert off % gw == 0
            specs.append(pl.BlockSpec((CHUNK, gw), lambda h, c, o=off // gw: (cidx(c), h + o)))
        else:
            specs.append(pl.BlockSpec((GROUPS_PER_STEP, CHUNK, 1), lambda h, c: (h, cidx(c), 0)))
    return specs, cidx


def _group_vals(refs, g):
    return [r[:, g * LANES:(g + 1) * LANES] if len(r.shape) == 2 else r[g] for r in refs]


def _scan_fwd(chunk_fn, ins, col_offs, n_groups, n_chunks, state_shape, name):
    n_in = len(ins)
    gps = GROUPS_PER_STEP
    t = ins[0].shape[0]

    def body(*refs):
        in_refs = refs[:n_in]
        o_ref, s0_ref, st = refs[n_in:]

        @pl.when(pl.program_id(1) == 0)
        def _():
            st[...] = jnp.zeros_like(st)

        states = st[...]
        vals = [jnp.stack(col) for col in zip(*[_group_vals(in_refs, g) for g in range(gps)])]
        o, s_new = jax.vmap(chunk_fn)(states, *vals)
        s0_ref[...] = states
        st[...] = s_new
        for g in range(gps):
            o_ref[:, g * LANES:(g + 1) * LANES] = o[g]

    specs, _ = _scan_specs(ins, col_offs, n_chunks, False)
    return pl.pallas_call(
        body, name=name, grid=(n_groups // gps, n_chunks), in_specs=specs,
        out_specs=[pl.BlockSpec((CHUNK, gps * LANES), lambda h, c: (c, h)),
                   pl.BlockSpec((gps, None) + state_shape, lambda h, c: (h, c, 0, 0))],
        out_shape=[jax.ShapeDtypeStruct((t, n_groups * LANES), f32),
                   jax.ShapeDtypeStruct((n_groups, n_chunks) + state_shape, f32)],
        scratch_shapes=[pltpu.VMEM((gps,) + state_shape, f32)],
        compiler_params=pltpu.CompilerParams(dimension_semantics=("parallel", "arbitrary")),
    )(*ins)


def _scan_bwd(chunk_fn, s0s, ins, col_offs, d_out, n_groups, n_chunks, state_shape, name):
    n_in = len(ins)
    gps = GROUPS_PER_STEP
    t = d_out.shape[0]

    def body(*refs):
        s0_ref = refs[0]
        in_refs = refs[1:1 + n_in]
        do_ref = refs[1 + n_in]
        g_refs = refs[2 + n_in:2 + 2 * n_in]
        dst = refs[2 + 2 * n_in]

        @pl.when(pl.program_id(1) == 0)
        def _():
            dst[...] = jnp.zeros_like(dst)

        vals = [jnp.stack(col) for col in zip(*[_group_vals(in_refs, g) for g in range(gps)])]
        d_o = jnp.stack([do_ref[:, g * LANES:(g + 1) * LANES] for g in range(gps)])
        _, vjp = jax.vjp(jax.vmap(chunk_fn), s0_ref[...], *vals)
        gs = vjp((d_o, dst[...]))
        dst[...] = gs[0]
        for g_ref, gv in zip(g_refs, gs[1:]):
            if len(g_ref.shape) == 2:
                for g in range(gps):
                    g_ref[:, g * LANES:(g + 1) * LANES] = gv[g]
            else:
                g_ref[...] = gv

    specs, cidx = _scan_specs(ins, col_offs, n_chunks, True)
    out_lane = pl.BlockSpec((CHUNK, gps * LANES), lambda h, c: (cidx(c), h))
    g_specs = [out_lane if a.ndim == 2 else sp for a, sp in zip(ins, specs)]
    g_shapes = [(t, n_groups * LANES) if a.ndim == 2 else a.shape for a in ins]
    s0_spec = pl.BlockSpec((gps, None) + state_shape, lambda h, c: (h, cidx(c), 0, 0))
    return pl.pallas_call(
        body, name=name, grid=(n_groups // gps, n_chunks), in_specs=[s0_spec] + specs + [out_lane],
        out_specs=g_specs, out_shape=[jax.ShapeDtypeStruct(sh, f32) for sh in g_shapes],
        scratch_shapes=[pltpu.VMEM((gps,) + state_shape, f32)],
        compiler_params=pltpu.CompilerParams(dimension_semantics=("parallel", "arbitrary")),
    )(s0s, *ins, d_out)


def _rms_fwd(x, g, name):
    t = x.shape[0]
    tm = _tile(t, 416, 8)
    return _tw_fwd(_f_rms, [x, g], [_row_spec(tm, D), _full_spec(g.shape)],
                   [jax.ShapeDtypeStruct(x.shape, f32)], [_row_spec(tm, D)], (t // tm,), name)[0]


def _rms_bwd(x, g, dy, name):
    t = x.shape[0]
    tm = _tile(t, 416, 8)
    return _tw_bwd(_f_rms, [x, g], [_row_spec(tm, D), _full_spec(g.shape)], [dy], [_row_spec(tm, D)],
                   ['tile', 'acc'], (t // tm,), name)


def _ffn_fwd(h, gain, wg, wu, wd, tag):
    xn = _rms_fwd(h, gain, f"{tag}_rms")
    gate = _matmul(xn, wg, name=f"{tag}_gate")
    up = _matmul(xn, wu, name=f"{tag}_up")
    t = h.shape[0]
    tm = _tile(t, 208, 8)
    act = _tw_fwd(_f_swiglu, [gate, up], [_row_spec(tm, D_FF)] * 2, [jax.ShapeDtypeStruct((t, D_FF), f32)],
                  [_row_spec(tm, D_FF)], (t // tm,), f"{tag}_act")[0]
    out = _matmul(act, wd, res=h, scale=0.5, name=f"{tag}_down")
    return out, (xn, gate, up, act)


def _ffn_bwd(h, gain, wg, wu, wd, saved, dout, tag):
    xn, gate, up, act = saved
    t = h.shape[0]
    d_wd = _matmul(act, dout, ta=True, scale=0.5, name=f"{tag}_dwd")
    d_act = _matmul(dout, wd, tb=True, scale=0.5, name=f"{tag}_dact")
    tm = _tile(t, 208, 8)
    d_gate, d_up = _tw_bwd(_f_swiglu, [gate, up], [_row_spec(tm, D_FF)] * 2, [d_act], [_row_spec(tm, D_FF)],
                           ['tile', 'tile'], (t // tm,), f"{tag}_dactf")
    d_wg = _matmul(xn, d_gate, ta=True, name=f"{tag}_dwg")
    d_wu = _matmul(xn, d_up, ta=True, name=f"{tag}_dwu")
    d_xn = _matmul(d_gate, wg, tb=True, name=f"{tag}_dxn_g")
    d_xn = _matmul(d_up, wu, tb=True, res=d_xn, name=f"{tag}_dxn_u")
    d_hn, d_gain = _rms_bwd(h, gain, d_xn, f"{tag}_drms")
    return d_hn, d_gain, d_wg, d_wu, d_wd


def _col_spec(t, first_block):
    return pl.BlockSpec((t, LANES), lambda j, fb=first_block: (0, j + fb))


def _local_step(h0, tgt, w):
    t = h0.shape[0]
    assert t % CHUNK == 0
    nc = t // CHUNK
    grads = {}

    h1, ffn1_saved = _ffn_fwd(h0, w['ffn1_norm'], w['ffn1_wg'], w['ffn1_wu'], w['ffn1_wd'], "ffn1")
    u = _rms_fwd(h1, w['mix_norm'], "mix_rms")
    z = _matmul(u, w['w_in_p'], name="in_proj")
    zs = z[:, 9216:9216 + 304]
    abeta, aalpha = zs[:, 288:296], zs[:, 296:304]

    conv_w = w['a_conv_w']
    conv_fns = [functools.partial(_f_conv, norm=True, scale=A_DK ** -0.5),
                functools.partial(_f_conv, norm=True, scale=1.0),
                functools.partial(_f_conv, norm=False, scale=1.0)]
    qkv = []
    for idx, fn in enumerate(conv_fns):
        qkv.append(_tw_fwd(fn, [z, conv_w], [_col_spec(t, 8 * idx), pl.BlockSpec((4, LANES), lambda j, o=8 * idx: (0, j + o))],
                           [jax.ShapeDtypeStruct((t, D), f32)], [_col_spec(t, 0)], (A_HEADS,), f"a_conv{idx}")[0])
    aq, ak, av = qkv
    tmg = _tile(t, 1040, 8)
    dg_fn = functools.partial(_f_dgates, tm=tmg)
    dg_specs = [_row_spec(tmg, A_HEADS)] * 2 + [_full_spec((1, A_HEADS))] * 2
    beta, gdec = _tw_fwd(dg_fn, [abeta, aalpha, w['a_log_rate'], w['a_dt_bias']], dg_specs,
                         [jax.ShapeDtypeStruct((t, A_HEADS), f32)] * 2, [_row_spec(tmg, A_HEADS)] * 2, (t // tmg,),
                         "a_gates", with_pid=True)
    beta_h = beta.T[:, :, None]
    gdec_h = gdec.T[:, :, None]
    a_ins = [aq, ak, av, beta_h, gdec_h]
    a_offs = [0] * 5
    o_scan, a_s0 = _scan_fwd(_delta_chunk, a_ins, a_offs, A_HEADS, nc, (A_DK, A_DK), "a_scan")

    mu = w['b_shift_mu']
    mu_rkv, mu_s = mu[:, :3072], mu[:, 3072:]
    zf_rkv = _tw_fwd(_f_tshift, [z, mu_rkv], [_col_spec(t, 32), pl.BlockSpec((1, LANES), lambda j: (0, j))],
                     [jax.ShapeDtypeStruct((t, 3072), f32)], [_col_spec(t, 0)], (24,), "b_shift")[0]
    zs_b = zs[:, :288]
    zf_s = _tw_fwd(_f_tshift, [zs_b, mu_s], [_full_spec((t, 288)), _full_spec((1, 288))],
                   [jax.ShapeDtypeStruct((t, 288), f32)], [_full_spec((t, 288))], (1,), "b_shift_s")[0]
    wdf, adf, gdf = zf_s[:, 0:64], zf_s[:, 64:128], zf_s[:, 128:288]
    tmr = _tile(t, 160, 8)
    pre_params = [w['b_w0'], w['b_w_up'], w['b_a0'], w['b_a_up'], w['b_g_up'], w['b_k_k'], w['b_k_a']]
    pre_ins = [zf_rkv, wdf, adf, gdf] + pre_params
    pre_specs = ([_row_spec(tmr, D, 1), _row_spec(tmr, 64), _row_spec(tmr, 64), _row_spec(tmr, 160)]
                 + [_full_spec(p.shape) for p in pre_params])
    lw, kmod, a_s, b_s, bgate = _tw_fwd(_f_rwkv_pre, pre_ins, pre_specs, [jax.ShapeDtypeStruct((t, D), f32)] * 5,
                                        [_row_spec(tmr, D)] * 5, (t // tmr,), "b_pre")
    b_ins = [zf_rkv, kmod, zf_rkv, a_s, b_s, lw]
    b_offs = [0, 0, 2 * D, 0, 0, 0]
    y_scan, b_s0 = _scan_fwd(_rwkv_chunk, b_ins, b_offs, B_HEADS // 2, nc, (2 * B_N, 2 * B_N), "b_scan")

    out_gain_t = jnp.tile(w['a_out_norm'], (1, A_HEADS))
    r_k = w['b_r_k'].reshape(1, D)
    post_params = [out_gain_t, w['b_ln_gain'], w['b_ln_bias'], r_k]
    post_ins = [o_scan, z, y_scan, zf_rkv, kmod, zf_rkv, bgate, z, z] + post_params
    post_specs = ([_row_spec(tmr, D), _row_spec(tmr, D, 3), _row_spec(tmr, D), _row_spec(tmr, D, 0), _row_spec(tmr, D),
                   _row_spec(tmr, D, 2), _row_spec(tmr, D), _row_spec(tmr, D, 7), _row_spec(tmr, D, 8)]
                  + [_full_spec((1, D))] * 4)
    merged = _tw_fwd(_f_mix_post, post_ins, post_specs, [jax.ShapeDtypeStruct((t, D), f32)], [_row_spec(tmr, D)],
                     (t // tmr,), "mix_post")[0]
    h2 = _matmul(merged, w['w_out'], res=h1, name="out_proj")
    h3, ffn2_saved = _ffn_fwd(h2, w['ffn2_norm'], w['ffn2_wg'], w['ffn2_wu'], w['ffn2_wd'], "ffn2")

    tml = _tile(t, 416, 8)
    fnorm = w['final_norm']
    loss_fn = functools.partial(_f_loss, tm=tml)
    loss_specs = [_row_spec(tml, D), _full_spec((1, D)), _row_spec(tml, D)]
    loss_parts = _tw_fwd(loss_fn, [h3, fnorm, tgt], loss_specs, [jax.ShapeDtypeStruct((t // tml, 1, 1), f32)],
                         [pl.BlockSpec((None, 1, 1), lambda i: (i, 0, 0))], (t // tml,), "loss", with_pid=True)[0]
    loss = jnp.sum(loss_parts)
    ones = jnp.ones((t // tml, 1, 1), f32)
    d_h3, grads['final_norm'] = _tw_bwd(loss_fn, [h3, fnorm, tgt], loss_specs, [ones],
                                        [pl.BlockSpec((None, 1, 1), lambda i: (i, 0, 0))], ['tile', 'acc', None],
                                        (t // tml,), "loss_bwd", with_pid=True)

    d_hn, grads['ffn2_norm'], grads['ffn2_wg'], grads['ffn2_wu'], grads['ffn2_wd'] = _ffn_bwd(
        h2, w['ffn2_norm'], w['ffn2_wg'], w['ffn2_wu'], w['ffn2_wd'], ffn2_saved, d_h3, "ffn2")
    d_h2 = _add(d_h3, d_hn, "add_h2")
    grads['w_out'] = _matmul(merged, d_h2, ta=True, name="d_w_out")
    d_merged = _matmul(d_h2, w['w_out'], tb=True, name="d_merged")

    win = ('tile', (t, D), _row_spec(tmr, D))
    post_kinds = ['tile', win, 'tile', win, 'tile', win, 'tile', win, win] + ['acc'] * 4
    (d_o, d_az, d_y, d_r1, d_kmod1, d_v1, d_bgate, d_ga, d_gb,
     d_out_gain_t, grads['b_ln_gain'], grads['b_ln_bias'], d_r_k) = _tw_bwd(
        _f_mix_post, post_ins, post_specs, [d_merged], [_row_spec(tmr, D)], post_kinds, (t // tmr,), "mix_post_bwd")
    grads['a_out_norm'] = jnp.sum(d_out_gain_t.reshape(A_HEADS, A_DK), axis=0, keepdims=True)
    grads['b_r_k'] = d_r_k.reshape(1, B_HEADS, B_N)

    d_r2, d_kmod2, d_v2, d_as, d_bs, d_lw = _scan_bwd(_rwkv_chunk, b_s0, b_ins, b_offs, d_y, B_HEADS // 2, nc,
                                                      (2 * B_N, 2 * B_N), "b_scan_bwd")
    d_kmod = _add(d_kmod1, d_kmod2, "add_kmod")
    pre_kinds = [win] + ['tile'] * 3 + ['acc'] * 7
    pre_ct_specs = [_row_spec(tmr, D)] * 5
    (d_zf_k, d_wdf, d_adf, d_gdf, grads['b_w0'], grads['b_w_up'], grads['b_a0'], grads['b_a_up'], grads['b_g_up'],
     grads['b_k_k'], grads['b_k_a']) = _tw_bwd(
        _f_rwkv_pre, pre_ins, pre_specs, [d_lw, d_kmod, d_as, d_bs, d_bgate], pre_ct_specs, pre_kinds, (t // tmr,),
        "b_pre_bwd")
    d_zf_rkv = _assemble3(d_r1, d_r2, d_zf_k, d_v1, d_v2, "b_dzf")
    d_zb_rkv, d_mu_rkv = _tw_bwd(_f_tshift, [z, mu_rkv], [_col_spec(t, 32), pl.BlockSpec((1, LANES), lambda j: (0, j))],
                                 [d_zf_rkv], [_col_spec(t, 0)], [('tile', (t, 3072), _col_spec(t, 0)), 'tile'], (24,),
                                 "b_shift_bwd")
    d_zf_s = jnp.concatenate([d_wdf, d_adf, d_gdf], axis=1)
    d_zs_b, d_mu_s = _tw_bwd(_f_tshift, [zs_b, mu_s], [_full_spec((t, 288)), _full_spec((1, 288))], [d_zf_s],
                             [_full_spec((t, 288))], ['tile', 'tile'], (1,), "b_shift_s_bwd")
    grads['b_shift_mu'] = jnp.concatenate([d_mu_rkv, d_mu_s], axis=1)

    d_aq, d_ak, d_av, d_beta_h, d_g_h = _scan_bwd(_delta_chunk, a_s0, a_ins, a_offs, d_o, A_HEADS, nc, (A_DK, A_DK),
                                                  "a_scan_bwd")
    d_beta = d_beta_h[:, :, 0].T
    d_gdec = d_g_h[:, :, 0].T
    d_abeta, d_aalpha, grads['a_log_rate'], grads['a_dt_bias'] = _tw_bwd(
        dg_fn, [abeta, aalpha, w['a_log_rate'], w['a_dt_bias']], dg_specs, [d_beta, d_gdec],
        [_row_spec(tmg, A_HEADS)] * 2, ['tile', 'tile', 'acc', 'acc'], (t // tmg,), "a_gates_bwd", with_pid=True)
    d_zqkv, d_conv = [], []
    for idx, (fn, ct) in enumerate(zip(conv_fns, (d_aq, d_ak, d_av))):
        dz_i, dw_i = _conv_bwd(fn, z, conv_w, ct, idx, t)
        d_zqkv.append(dz_i)
        d_conv.append(dw_i)
    grads['a_conv_w'] = jnp.concatenate(d_conv, axis=1)

    d_z = jnp.concatenate(
        d_zqkv + [d_az, d_zb_rkv, d_ga, d_gb, d_zs_b, d_abeta, d_aalpha, jnp.zeros((t, ZP - 9216 - 304), f32)], axis=1)
    grads['w_in_p'] = _matmul(u, d_z, ta=True, name="d_w_in")
    d_u = _matmul(d_z, w['w_in_p'], tb=True, name="d_u")
    d_h1n, grads['mix_norm'] = _rms_bwd(h1, w['mix_norm'], d_u, "mix_drms")
    d_h1 = _add(d_h2, d_h1n, "add_h1")
    d_h0n, grads['ffn1_norm'], grads['ffn1_wg'], grads['ffn1_wu'], grads['ffn1_wd'] = _ffn_bwd(
        h0, w['ffn1_norm'], w['ffn1_wg'], w['ffn1_wu'], w['ffn1_wd'], ffn1_saved, d_h1, "ffn1")
    d_h0 = _add(d_h1, d_h0n, "add_h0")
    return loss, d_h0, grads


_WIN_SEGMENTS = ((0, 4096), (4112, 7184), (7472, 9520), (7184, 7472), (4096, 4112))


def _win_to_padded(w_in):
    parts = [w_in[:, a:b] for a, b in _WIN_SEGMENTS]
    parts.append(jnp.zeros((w_in.shape[0], ZP - IN_TOTAL), w_in.dtype))
    return jnp.concatenate(parts, axis=1)


def _win_from_padded(w_p):
    widths = [b - a for a, b in _WIN_SEGMENTS]
    offs = [sum(widths[:i]) for i in range(len(widths))]
    seg = {a: w_p[:, o:o + wd] for (a, _), o, wd in zip(_WIN_SEGMENTS, offs, widths)}
    return jnp.concatenate([seg[a] for a in sorted(seg)], axis=1)


def _add(a, b, name):
    t, c = a.shape
    tm = _tile(t, 416, 8)
    return _tw_fwd(lambda x, y: (x + y,), [a, b], [_row_spec(tm, c)] * 2, [jax.ShapeDtypeStruct(a.shape, f32)],
                   [_row_spec(tm, c)], (t // tm,), name)[0]


def _assemble3(d_r1, d_r2, d_k, d_v1, d_v2, name):
    t = d_r1.shape[0]
    tm = _tile(t, 208, 8)

    def body(r1, r2, kk, v1, v2, o_ref):
        o_ref[:, 0:D] = r1[...] + r2[...]
        o_ref[:, D:2 * D] = kk[...]
        o_ref[:, 2 * D:3 * D] = v1[...] + v2[...]

    return pl.pallas_call(body, name=name, grid=(t // tm,), in_specs=[_row_spec(tm, D)] * 5,
                          out_specs=_row_spec(tm, 3 * D), out_shape=jax.ShapeDtypeStruct((t, 3 * D), f32),
                          )(d_r1, d_r2, d_k, d_v1, d_v2)


def _conv_bwd(fn, z, conv_w, ct, idx, t):
    def body(z_ref, w_ref, ct_ref, dz_ref, dw_ref):
        _, vjp = jax.vjp(lambda a, b: fn(a, b), z_ref[...], w_ref[...])
        dz, dw = vjp((ct_ref[...],))
        dz_ref[...] = dz
        dw_ref[...] = dw

    return pl.pallas_call(
        body, name=f"a_conv{idx}_bwd", grid=(A_HEADS,),
        in_specs=[_col_spec(t, 8 * idx), pl.BlockSpec((4, LANES), lambda j, o=8 * idx: (0, j + o)), _col_spec(t, 0)],
        out_specs=[_col_spec(t, 0), pl.BlockSpec((4, LANES), lambda j: (0, j))],
        out_shape=[jax.ShapeDtypeStruct((t, D), f32), jax.ShapeDtypeStruct((4, D), f32)],
    )(z, conv_w, ct)


def _position():
    return lax.axis_index("x"), lax.axis_index("y"), lax.axis_index("c")


def _flip(v, f):
    return 1 - v if f else v


_CHIP_FLIPS = ((1, 0), (0, 1), (1, 1))
_DEV_FLIPS = tuple((fx, fy, fc) for fx in (0, 1) for fy in (0, 1) for fc in (0, 1) if (fx, fy, fc) != (0, 0, 0))


def _gather_chips(arrs, name):
    n = len(arrs)
    assert all(a.shape[0] % 32 == 0 for a in arrs)
    arrs = [a.reshape(2, a.shape[0] // 2, a.shape[1]) for a in arrs]

    def body(*refs):
        ins, outs = refs[:n], refs[n:2 * n]
        send, recv, fsend, frecv, own = refs[2 * n:]
        x, y, c = _position()
        me = 2 * x + y
        sends, plan, owns = [], [], []
        for a in range(n):
            cp = pltpu.make_async_remote_copy(src_ref=ins[a], dst_ref=outs[a].at[me], send_sem=own.at[a, 0],
                                              recv_sem=own.at[a, 1], device_id=(x, y, 1 - c), device_id_type=MESH)
            cp.start()
            owns.append(cp)
            for j, (fx, fy) in enumerate(_CHIP_FLIPS):
                px, py = _flip(x, fx), _flip(y, fy)
                p = 2 * px + py
                cp = pltpu.make_async_remote_copy(src_ref=ins[a].at[c], dst_ref=outs[a].at[me, c],
                                                  send_sem=send.at[a, j], recv_sem=recv.at[a, j],
                                                  device_id=(px, py, c), device_id_type=MESH)
                cp.start()
                sends.append(cp)
                landed = pltpu.make_async_remote_copy(src_ref=ins[a].at[c], dst_ref=outs[a].at[p, c],
                                                      send_sem=send.at[a, j], recv_sem=recv.at[a, j],
                                                      device_id=(px, py, c), device_id_type=MESH)
                onward = pltpu.make_async_remote_copy(src_ref=outs[a].at[p, c], dst_ref=outs[a].at[p, c],
                                                      send_sem=fsend.at[a, j], recv_sem=frecv.at[a, j],
                                                      device_id=(x, y, 1 - c), device_id_type=MESH)
                from_sibling = pltpu.make_async_remote_copy(src_ref=outs[a].at[p, 1 - c], dst_ref=outs[a].at[p, 1 - c],
                                                            send_sem=fsend.at[a, j], recv_sem=frecv.at[a, j],
                                                            device_id=(x, y, 1 - c), device_id_type=MESH)
                plan.append((landed, onward, from_sibling))
        for landed, onward, _ in plan:
            landed.wait_recv()
            onward.start()
        for _, _, from_sibling in plan:
            from_sibling.wait_recv()
        for cp in sends:
            cp.wait_send()
        for _, onward, _ in plan:
            onward.wait_send()
        for cp in owns:
            cp.wait()

    sems = [pltpu.SemaphoreType.DMA((n, 3))] * 4 + [pltpu.SemaphoreType.DMA((n, 2))]
    outs = pl.pallas_call(
        body, name=name, in_specs=[ANY] * n, out_specs=[ANY] * n,
        out_shape=[jax.ShapeDtypeStruct((N_CHIPS,) + a.shape, a.dtype) for a in arrs], scratch_shapes=sems,
    )(*arrs)
    return [o.reshape(N_CHIPS, o.shape[1] * o.shape[2], o.shape[3]) for o in outs]


def _swap_sibling(src_of, shape, dtype, name):
    def body(a_ref, got_ref, send, recv):
        x, y, c = _position()
        cp = pltpu.make_async_remote_copy(src_ref=src_of(a_ref, c), dst_ref=got_ref, send_sem=send, recv_sem=recv,
                                          device_id=(x, y, 1 - c), device_id_type=MESH)
        cp.start()
        cp.wait()

    def call(a):
        return pl.pallas_call(body, name=name, in_specs=[ANY], out_specs=ANY,
                              out_shape=jax.ShapeDtypeStruct(shape, dtype),
                              scratch_shapes=[pltpu.SemaphoreType.DMA(())] * 2)(a)
    return call


def _add_halves(g, got, dtype, name):
    _, n, hr, w = g.shape
    tr = _tile(hr, 784, 16)

    def body(g_ref, got_ref, o_ref):
        c = lax.axis_index("c")
        own = jnp.where(c == 0, g_ref[0], g_ref[1])
        o_ref[...] = (own + got_ref[...]).astype(dtype)

    return pl.pallas_call(
        body, name=name, grid=(hr // tr,),
        in_specs=[pl.BlockSpec((2, n, tr, w), lambda i: (0, 0, i, 0)), pl.BlockSpec((n, tr, w), lambda i: (0, i, 0))],
        out_specs=pl.BlockSpec((n, tr, w), lambda i: (0, i, 0)),
        out_shape=jax.ShapeDtypeStruct((n, hr, w), dtype))(g, got)


def _scatter_chips(g, name):
    def body(g_ref, out_ref, send, recv):
        x, y, c = _position()
        sends = []
        for j, (fx, fy) in enumerate(_CHIP_FLIPS):
            px, py = _flip(x, fx), _flip(y, fy)
            cp = pltpu.make_async_remote_copy(src_ref=g_ref.at[2 * px + py], dst_ref=out_ref.at[j], send_sem=send.at[j],
                                              recv_sem=recv.at[j], device_id=(px, py, c), device_id_type=MESH)
            cp.start()
            sends.append(cp)
        for cp in sends:
            cp.wait_recv()
        for cp in sends:
            cp.wait_send()

    return pl.pallas_call(
        body, name=name, in_specs=[ANY], out_specs=ANY, out_shape=jax.ShapeDtypeStruct((3,) + g.shape[1:], g.dtype),
        scratch_shapes=[pltpu.SemaphoreType.DMA((3,)), pltpu.SemaphoreType.DMA((3,))],
    )(g)


def _sum_own_and_slots(own, got, name):
    n, r, w = own.shape
    tr = _tile(r, 784, 16)

    def body(own_ref, got_ref, o_ref):
        me = 2 * lax.axis_index("x") + lax.axis_index("y")
        acc = own_ref[0]
        for i in range(1, n):
            acc = jnp.where(me == i, own_ref[i], acc)
        acc = acc.astype(f32)
        for j in range(3):
            acc = acc + got_ref[j].astype(f32)
        o_ref[...] = acc

    return pl.pallas_call(
        body, name=name, grid=(r // tr,),
        in_specs=[pl.BlockSpec((n, tr, w), lambda i: (0, i, 0)), pl.BlockSpec((3, tr, w), lambda i: (0, i, 0))],
        out_specs=pl.BlockSpec((tr, w), lambda i: (i, 0)), out_shape=jax.ShapeDtypeStruct((r, w), f32))(own, got)


def _gather_devices(s, name):
    def body(s_ref, out_ref, send, recv, loc):
        x, y, c = _position()
        me = 4 * x + 2 * y + c
        lc = pltpu.make_async_copy(s_ref, out_ref.at[me], loc)
        lc.start()
        sends, recvs = [], []
        for j, (fx, fy, fc) in enumerate(_DEV_FLIPS):
            px, py, pc = _flip(x, fx), _flip(y, fy), _flip(c, fc)
            cp = pltpu.make_async_remote_copy(src_ref=s_ref, dst_ref=out_ref.at[me], send_sem=send.at[j],
                                              recv_sem=recv.at[j], device_id=(px, py, pc), device_id_type=MESH)
            cp.start()
            sends.append(cp)
            recvs.append(pltpu.make_async_remote_copy(
                src_ref=s_ref, dst_ref=out_ref.at[4 * px + 2 * py + pc], send_sem=send.at[j], recv_sem=recv.at[j],
                device_id=(px, py, pc), device_id_type=MESH))
        for cp in recvs:
            cp.wait_recv()
        for cp in sends:
            cp.wait_send()
        lc.wait()

    return pl.pallas_call(
        body, name=name, in_specs=[ANY], out_specs=ANY, out_shape=jax.ShapeDtypeStruct((N_DEV,) + s.shape, s.dtype),
        scratch_shapes=[pltpu.SemaphoreType.DMA((7,)), pltpu.SemaphoreType.DMA((7,)), pltpu.SemaphoreType.DMA(())],
    )(s)


def _sum_slots(a, name):
    s, r, c = a.shape
    tr = _tile(r, 2048, 16)

    def body(a_ref, o_ref):
        acc = a_ref[0].astype(f32)
        for i in range(1, s):
            acc = acc + a_ref[i].astype(f32)
        o_ref[...] = acc

    return pl.pallas_call(body, name=name, grid=(r // tr,), in_specs=[pl.BlockSpec((s, tr, c), lambda i: (0, i, 0))],
                          out_specs=pl.BlockSpec((tr, c), lambda i: (i, 0)),
                          out_shape=jax.ShapeDtypeStruct((r, c), f32))(a)


def _adamw(w, g_parts, m, v, name):
    shape = w.shape
    size = w.size
    view = (size // LANES, LANES) if size % LANES == 0 else (1, size)
    rows = view[0]
    tr = _tile(rows, 2048, 8) if rows > 2048 else rows
    n_g = len(g_parts)

    def body(*refs):
        w_ref = refs[0]
        g_refs = refs[1:1 + n_g]
        m_ref, v_ref, g_out, d_out, m_out, v_out = refs[1 + n_g:]
        g = g_refs[0][...]
        for gr in g_refs[1:]:
            g = g + gr[...]
        m_new = ADAM_B1 * m_ref[...] + (1.0 - ADAM_B1) * g
        v_new = ADAM_B2 * v_ref[...] + (1.0 - ADAM_B2) * (g * g)
        m_hat = m_new / (1.0 - ADAM_B1 ** ADAM_STEP)
        v_hat = v_new / (1.0 - ADAM_B2 ** ADAM_STEP)
        g_out[...] = g
        d_out[...] = -ADAM_LR * (m_hat / (jnp.sqrt(v_hat) + ADAM_EPS) + ADAM_WD * w_ref[...])
        m_out[...] = m_new
        v_out[...] = v_new

    spec = pl.BlockSpec((tr, view[1]), lambda i: (i, 0))
    args = [w.reshape(view)] + [g.reshape(view) for g in g_parts] + [m.reshape(view), v.reshape(view)]
    outs = pl.pallas_call(body, name=name, grid=(rows // tr,), in_specs=[spec] * len(args), out_specs=[spec] * 4,
                          out_shape=[jax.ShapeDtypeStruct(view, f32)] * 4)(*args)
    return [o.reshape(shape) for o in outs]


_BIG = ('ffn1_w_gu', 'ffn1_w_down', 'w_in', 'w_out', 'ffn2_w_gu', 'ffn2_w_down')
_SMALL_SHARDED = ('meta_tokens', 'a_conv_w', 'b_w_up', 'b_a_up', 'b_g_up')
_WEIGHTS = ('meta_tokens', 'ffn1_norm', 'ffn1_w_gu', 'ffn1_w_down', 'mix_norm', 'w_in', 'a_conv_w', 'a_log_rate',
            'a_dt_bias', 'a_out_norm', 'b_shift_mu', 'b_w0', 'b_w_up', 'b_a0', 'b_a_up', 'b_g_up', 'b_k_k', 'b_k_a',
            'b_r_k', 'b_ln_gain', 'b_ln_bias', 'w_out', 'ffn2_norm', 'ffn2_w_gu', 'ffn2_w_down', 'final_norm')
_SMALL = tuple(n for n in _WEIGHTS if n not in _BIG)


def _rows_of(shape):
    n = 1
    for d in shape:
        n *= d
    return n, -(-n // LANES)


def _pack(arrs, dtype, row_mult=32):
    parts, total = [], 0
    for a in arrs:
        n, rows = _rows_of(a.shape)
        flat = a.reshape(-1).astype(dtype)
        if n % LANES:
            flat = jnp.pad(flat, (0, rows * LANES - n))
        parts.append(flat)
        total += rows
    extra = -total % row_mult
    if extra:
        parts.append(jnp.zeros((extra * LANES,), dtype))
    return jnp.concatenate(parts).reshape(total + extra, LANES)


def _unpack(packed, shapes, lead=()):
    out, off = [], 0
    for sh in shapes:
        n, rows = _rows_of(sh)
        seg = packed[..., off:off + rows, :]
        if n % LANES:
            seg = seg.reshape(lead + (-1,))[..., :n]
        out.append(seg.reshape(lead + tuple(sh)))
        off += rows
    return out


def _cols_from_shards(s):
    return jnp.concatenate([s[i] for i in range(N_CHIPS)], axis=-1)


def _cols_to_shards(a):
    r, c = a.shape
    return a.reshape(r, N_CHIPS, c // N_CHIPS).transpose(1, 0, 2)


def kernel(x, meta_tokens, ffn1_norm, ffn1_w_gu, ffn1_w_down, mix_norm, w_in, a_conv_w, a_log_rate, a_dt_bias, a_out_norm, b_shift_mu, b_w0, b_w_up, b_a0, b_a_up, b_g_up, b_k_k, b_k_a, b_r_k, b_ln_gain, b_ln_bias, w_out, ffn2_norm, ffn2_w_gu, ffn2_w_down, final_norm, loss_target, m_meta_tokens, m_ffn1_norm, m_ffn1_w_gu, m_ffn1_w_down, m_mix_norm, m_w_in, m_a_conv_w, m_a_log_rate, m_a_dt_bias, m_a_out_norm, m_b_shift_mu, m_b_w0, m_b_w_up, m_b_a0, m_b_a_up, m_b_g_up, m_b_k_k, m_b_k_a, m_b_r_k, m_b_ln_gain, m_b_ln_bias, m_w_out, m_ffn2_norm, m_ffn2_w_gu, m_ffn2_w_down, m_final_norm, v_meta_tokens, v_ffn1_norm, v_ffn1_w_gu, v_ffn1_w_down, v_mix_norm, v_w_in, v_a_conv_w, v_a_log_rate, v_a_dt_bias, v_a_out_norm, v_b_shift_mu, v_b_w0, v_b_w_up, v_b_a0, v_b_a_up, v_b_g_up, v_b_k_k, v_b_k_a, v_b_r_k, v_b_ln_gain, v_b_ln_bias, v_w_out, v_ffn2_norm, v_ffn2_w_gu, v_ffn2_w_down, v_final_norm):
    args = locals()
    wts = {n: args[n] for n in _WEIGHTS}
    mom = {n: args["m_" + n] for n in _WEIGHTS}
    var = {n: args["v_" + n] for n in _WEIGHTS}
    chip = 2 * lax.axis_index("x") + lax.axis_index("y")

    big_shapes = [wts[n].shape[1:] for n in _BIG]
    small_shapes = [wts[n].shape[-2:] for n in _SMALL_SHARDED]
    big_packed = _pack([wts[n] for n in _BIG], bf16)
    small_packed = _pack([wts[n] for n in _SMALL_SHARDED], f32)
    big_all, small_all = _gather_chips([big_packed, small_packed], "gather_weights")
    gu1, dn1, w_in_s, w_out_s, gu2, dn2 = _unpack(big_all, big_shapes, (N_CHIPS,))
    meta_s, conv_s, wup_s, aup_s, gup_s = _unpack(small_all, small_shapes, (N_CHIPS,))
    w = {
        'ffn1_norm': ffn1_norm, 'mix_norm': mix_norm, 'ffn2_norm': ffn2_norm, 'final_norm': final_norm[None, :],
        'ffn1_wg': jnp.concatenate([gu1[0], gu1[1]], axis=1), 'ffn1_wu': jnp.concatenate([gu1[2], gu1[3]], axis=1),
        'ffn1_wd': dn1.reshape(D_FF, D),
        'ffn2_wg': jnp.concatenate([gu2[0], gu2[1]], axis=1), 'ffn2_wu': jnp.concatenate([gu2[2], gu2[3]], axis=1),
        'ffn2_wd': dn2.reshape(D_FF, D),
        'w_in_p': _win_to_padded(_cols_from_shards(w_in_s)), 'w_out': w_out_s.reshape(D, D),
        'a_conv_w': _cols_from_shards(conv_s), 'b_w_up': _cols_from_shards(wup_s), 'b_a_up': _cols_from_shards(aup_s),
        'b_g_up': _cols_from_shards(gup_s),
        'a_log_rate': a_log_rate, 'a_dt_bias': a_dt_bias, 'a_out_norm': a_out_norm, 'b_shift_mu': b_shift_mu,
        'b_w0': b_w0, 'b_a0': b_a0, 'b_k_k': b_k_k, 'b_k_a': b_k_a, 'b_r_k': b_r_k, 'b_ln_gain': b_ln_gain,
        'b_ln_bias': b_ln_bias,
    }
    meta_full = _cols_from_shards(meta_s)

    h0 = jnp.concatenate([jnp.zeros((PAD, D), f32), meta_full, x[0]], axis=0)
    tgt = jnp.concatenate([jnp.zeros((SKIP, D), f32), loss_target[0]], axis=0)
    loss_local, d_h0, g = _local_step(h0, tgt, w)
    loss = lax.psum(loss_local, ("x", "y", "c"))
    grad_x = d_h0[SKIP:][None]

    big_grads = [
        _cols_to_shards(jnp.concatenate([g['ffn1_wg'], g['ffn1_wu']], axis=1)),
        g['ffn1_wd'].reshape(N_CHIPS, D_FF // N_CHIPS, D),
        _cols_to_shards(_win_from_padded(g['w_in_p'])),
        g['w_out'].reshape(N_CHIPS, D // N_CHIPS, D),
        _cols_to_shards(jnp.concatenate([g['ffn2_wg'], g['ffn2_wu']], axis=1)),
        g['ffn2_wd'].reshape(N_CHIPS, D_FF // N_CHIPS, D),
    ]
    g_packed = jnp.concatenate([a.reshape(N_CHIPS, -1, LANES) for a in big_grads], axis=1)
    assert g_packed.shape[1] == big_packed.shape[0]
    rows = g_packed.shape[1]
    g_halves = g_packed.reshape(N_CHIPS, 2, rows // 2, LANES).transpose(1, 0, 2, 3)
    half_shape = (N_CHIPS, rows // 2, LANES)
    sib_half = _swap_sibling(lambda ref, c: ref.at[1 - c], half_shape, f32, "swap_halves")(g_halves)
    chip_half = _add_halves(g_halves, sib_half, bf16, "add_sibling")
    mine = _sum_own_and_slots(chip_half, _scatter_chips(chip_half, "scatter_grads"), "sum_chips")
    theirs = _swap_sibling(lambda ref, c: ref, mine.shape, f32, "swap_sums")(mine)
    core = lax.axis_index("c")
    summed = jnp.concatenate([jnp.where(core == 0, mine, theirs), jnp.where(core == 0, theirs, mine)], axis=0)
    big_parts = _unpack(summed, big_shapes)

    small_full = {
        'meta_tokens': d_h0[PAD:SKIP], 'ffn1_norm': g['ffn1_norm'], 'mix_norm': g['mix_norm'], 'a_conv_w': g['a_conv_w'],
        'a_log_rate': g['a_log_rate'], 'a_dt_bias': g['a_dt_bias'], 'a_out_norm': g['a_out_norm'],
        'b_shift_mu': g['b_shift_mu'], 'b_w0': g['b_w0'], 'b_w_up': g['b_w_up'], 'b_a0': g['b_a0'], 'b_a_up': g['b_a_up'],
        'b_g_up': g['b_g_up'], 'b_k_k': g['b_k_k'], 'b_k_a': g['b_k_a'], 'b_r_k': g['b_r_k'], 'b_ln_gain': g['b_ln_gain'],
        'b_ln_bias': g['b_ln_bias'], 'ffn2_norm': g['ffn2_norm'], 'final_norm': g['final_norm'],
    }
    s_shapes = [small_full[n].shape for n in _SMALL]
    s_sum = _sum_slots(_gather_devices(_pack([small_full[n] for n in _SMALL], f32, row_mult=256), "gather_small"),
                       "sum_small")
    s_parts = dict(zip(_SMALL, _unpack(s_sum, s_shapes)))

    grad, delta, new_m, new_v = {}, {}, {}, {}
    for n, a in zip(_BIG, big_parts):
        grad[n], delta[n], new_m[n], new_v[n] = _adamw(wts[n], [a.reshape(wts[n].shape)], mom[n], var[n], f"adamw_{n}")
    for n in _SMALL:
        gs = s_parts[n]
        if n in _SMALL_SHARDED:
            width = wts[n].shape[-1]
            gs = lax.dynamic_slice_in_dim(gs, chip * width, width, axis=gs.ndim - 1)
        gs = gs.reshape(wts[n].shape)
        grad[n], delta[n], new_m[n], new_v[n] = _adamw(wts[n], [gs], mom[n], var[n], f"adamw_{n}")

    return (loss, grad_x, *[grad[n] for n in _WEIGHTS], *[delta[n] for n in _WEIGHTS],
            *[new_m[n] for n in _WEIGHTS], *[new_v[n] for n in _WEIGHTS])
```

```python
import functools

import jax
import jax.numpy as jnp
from jax import lax
from jax.experimental import pallas as pl
from jax.experimental.pallas import tpu as pltpu

f32 = jnp.float32
bf16 = jnp.bfloat16
HI = lax.Precision.HIGHEST
MESH = pl.DeviceIdType.MESH
ANY = pl.BlockSpec(memory_space=pl.ANY)

D = 1024
N_META = 16
CHUNK = 64
PAD = CHUNK - N_META
SKIP = PAD + N_META
EPS = 1e-6
D_FF = 2816
A_HEADS = 8
A_DK = 128
B_HEADS = 16
B_N = 64
B_GN_EPS = B_N * 1e-5
W_LORA, AA_LORA, G_LORA = 64, 64, 160
IN_TOTAL = 9520
ZP = 9600
LANES = 128
N_CHIPS = 4
N_DEV = 8

ADAM_LR, ADAM_B1, ADAM_B2, ADAM_EPS, ADAM_WD, ADAM_STEP = 0.001, 0.9, 0.999, 1e-08, 0.01, 10

MXU_DTYPE = bf16


def _tile(n, cap, mult):
    if n <= cap:
        return n
    best = None
    for t in range(mult, cap + 1, mult):
        if n % t == 0:
            best = t
    assert best is not None, (n, cap, mult)
    return best


def _sigmoid(x):
    return jax.nn.sigmoid(x)


def _silu(x):
    return x * jax.nn.sigmoid(x)


def _softplus(x):
    return jnp.maximum(x, 0.0) + jnp.log(1.0 + jnp.exp(-jnp.abs(x)))


def _head_matrix(c, nh):
    hd = c // nh
    r = lax.broadcasted_iota(jnp.int32, (c, nh), 0)
    h = lax.broadcasted_iota(jnp.int32, (c, nh), 1)
    return (r >= h * hd) & (r < (h + 1) * hd)


def _dot_exact_rhs(x, e, cb):
    dn = (((1,), (cb,)), ((), ()))
    if SCAN_PASSES == 0:
        return lax.dot_general(x, e.astype(f32), dn, precision=HI, preferred_element_type=f32)
    eb = e.astype(bf16)
    hi = x.astype(bf16)
    lo = (x - hi.astype(f32)).astype(bf16)
    return (lax.dot_general(hi, eb, dn, preferred_element_type=f32)
            + lax.dot_general(lo, eb, dn, preferred_element_type=f32))


def _head_sum_impl(x, nh):
    e = _head_matrix(x.shape[-1], nh)
    return _dot_exact_rhs(_dot_exact_rhs(x, e, 0), e, 1)


@functools.partial(jax.custom_vjp, nondiff_argnums=(1,))
def _head_sum(x, nh):
    return _head_sum_impl(x, nh)


def _head_sum_fwd(x, nh):
    return _head_sum_impl(x, nh), None


def _head_sum_bwd(nh, _, g):
    return (_head_sum_impl(g, nh),)


_head_sum.defvjp(_head_sum_fwd, _head_sum_bwd)


@functools.partial(jax.custom_vjp, nondiff_argnums=(1,))
def _shift_rows(x, s):
    n = x.shape[0]
    row = lax.broadcasted_iota(jnp.int32, x.shape, 0)
    if s > 0:
        return jnp.where(row >= s, pltpu.roll(x, s, 0), 0.0)
    return jnp.where(row < n + s, pltpu.roll(x, n + s, 0), 0.0)


def _shift_rows_fwd(x, s):
    return _shift_rows(x, s), None


def _shift_rows_bwd(s, _, g):
    return (_shift_rows(g, -s),)


_shift_rows.defvjp(_shift_rows_fwd, _shift_rows_bwd)


def _matmul(a, b, *, ta=False, tb=False, res=None, scale=1.0, name):
    assert not (ta and tb)
    (ar, ac), (br, bc) = a.shape, b.shape
    m, k = (ac, ar) if ta else (ar, ac)
    n, kb = (br, bc) if tb else (bc, br)
    assert k == kb, (a.shape, b.shape, ta, tb)
    tm = _tile(m, 1408, LANES) if ta else _tile(m, 832, 8)
    tn = _tile(n, 1408, LANES)
    tk = _tile(k, 1040, 8) if ta else _tile(k, 1408, LANES)
    nk = k // tk
    dn = (((0 if ta else 1,), (1 if tb else 0,)), ((), ()))

    def body(*refs):
        if res is not None:
            a_ref, b_ref, r_ref, o_ref, acc = refs
        else:
            a_ref, b_ref, o_ref, acc = refs
        kk = pl.program_id(2)

        @pl.when(kk == 0)
        def _():
            acc[...] = jnp.zeros_like(acc)

        acc[...] += lax.dot_general(a_ref[...].astype(MXU_DTYPE), b_ref[...].astype(MXU_DTYPE), dn,
                                    preferred_element_type=f32,
                                    precision=None if MXU_DTYPE == bf16 else HI)

        @pl.when(kk == nk - 1)
        def _():
            out = acc[...]
            if scale != 1.0:
                out = out * scale
            if res is not None:
                out = r_ref[...] + out
            o_ref[...] = out

    if ta:
        a_spec = pl.BlockSpec((tk, tm), lambda i, j, kk: (kk, i))
    else:
        a_spec = pl.BlockSpec((tm, tk), lambda i, j, kk: (i, kk))
    if tb:
        b_spec = pl.BlockSpec((tn, tk), lambda i, j, kk: (j, kk))
    else:
        b_spec = pl.BlockSpec((tk, tn), lambda i, j, kk: (kk, j))
    in_specs = [a_spec, b_spec]
    args = [a, b]
    if res is not None:
        in_specs.append(pl.BlockSpec((tm, tn), lambda i, j, kk: (i, j)))
        args.append(res)
    return pl.pallas_call(
        body, name=name, grid=(m // tm, n // tn, nk), in_specs=in_specs,
        out_specs=pl.BlockSpec((tm, tn), lambda i, j, kk: (i, j)),
        out_shape=jax.ShapeDtypeStruct((m, n), f32),
        scratch_shapes=[pltpu.VMEM((tm, tn), f32)],
        compiler_params=pltpu.CompilerParams(dimension_semantics=("parallel", "parallel", "arbitrary")),
    )(*args)


def _tw_fwd(fn, ins, in_specs, out_shapes, out_specs, grid, name, with_pid=False):
    n_in = len(ins)

    def body(*refs):
        vals = [r[...] for r in refs[:n_in]]
        outs = fn(pl.program_id(0), *vals) if with_pid else fn(*vals)
        for r, o in zip(refs[n_in:], outs):
            r[...] = o

    return pl.pallas_call(body, name=name, grid=grid, in_specs=in_specs, out_specs=out_specs,
                          out_shape=out_shapes)(*ins)


def _tw_bwd(fn, ins, in_specs, cts, ct_specs, kinds, grid, name, with_pid=False):
    n_in, n_ct = len(ins), len(cts)
    diff = [i for i, kd in enumerate(kinds) if kd is not None]

    def body(*refs):
        vals = [r[...] for r in refs[:n_in]]
        ctv = tuple(r[...] for r in refs[n_in:n_in + n_ct])
        g_refs = refs[n_in + n_ct:]
        pid = pl.program_id(0)

        def f(*dv):
            full = list(vals)
            for i, v in zip(diff, dv):
                full[i] = v
            out = fn(pid, *full) if with_pid else fn(*full)
            return tuple(out)

        _, vjp = jax.vjp(f, *[vals[i] for i in diff])
        gs = vjp(ctv)
        first = pid == 0
        for i2 in range(1, len(grid)):
            first = first & (pl.program_id(i2) == 0)
        for i, g, g_ref in zip(diff, gs, g_refs):
            if kinds[i] != 'acc':
                g_ref[...] = g
            else:
                @pl.when(first)
                def _(g=g, g_ref=g_ref):
                    g_ref[...] = g

                @pl.when(jnp.logical_not(first))
                def _(g=g, g_ref=g_ref):
                    g_ref[...] += g

    zero_map = {1: lambda *a: (0,), 2: lambda *a: (0, 0), 3: lambda *a: (0, 0, 0)}
    out_specs, out_shapes = [], []
    for i in diff:
        if kinds[i] == 'tile':
            out_shapes.append(jax.ShapeDtypeStruct(ins[i].shape, f32))
            out_specs.append(in_specs[i])
        elif kinds[i] == 'acc':
            out_shapes.append(jax.ShapeDtypeStruct(ins[i].shape, f32))
            out_specs.append(pl.BlockSpec(ins[i].shape, zero_map[ins[i].ndim]))
        else:
            out_shapes.append(jax.ShapeDtypeStruct(kinds[i][1], f32))
            out_specs.append(kinds[i][2])
    return pl.pallas_call(body, name=name, grid=grid, in_specs=list(in_specs) + list(ct_specs),
                          out_specs=out_specs, out_shape=out_shapes)(*ins, *cts)


def _row_spec(tm, c, col_block=0):
    return pl.BlockSpec((tm, c), lambda i, cb=col_block: (i, cb))


def _full_spec(shape):
    nd = len(shape)
    return pl.BlockSpec(shape, lambda *a, nd=nd: (0,) * nd)


def _f_rms(x, g):
    return (x * lax.rsqrt(jnp.mean(x * x, axis=-1, keepdims=True) + EPS) * g,)


def _f_swiglu(gate, up):
    return (_silu(gate) * up,)


def _f_loss(pid, h, g, tgt, *, tm):
    y = h * lax.rsqrt(jnp.mean(h * h, axis=-1, keepdims=True) + EPS) * g
    row = pid * tm + lax.broadcasted_iota(jnp.int32, (tm, 1), 0)
    err = jnp.where(row >= SKIP, y - tgt, 0.0)
    per_row = jnp.mean(err * err, axis=-1, keepdims=True)
    return (0.5 * jnp.sum(per_row, axis=0, keepdims=True),)


def _f_conv(x, w, *, norm, scale):
    y = x * w[3:4, :]
    for s in (1, 2, 3):
        y = y + _shift_rows(x, s) * w[3 - s:4 - s, :]
    y = _silu(y)
    if norm:
        y = y * lax.rsqrt(jnp.sum(y * y, axis=-1, keepdims=True) + 1e-6) * scale
    return (y,)


def _f_dgates(pid, abeta, aalpha, log_rate, dt_bias, *, tm):
    row = pid * tm + lax.broadcasted_iota(jnp.int32, (tm, 1), 0)
    live = row >= PAD
    beta = jnp.where(live, _sigmoid(abeta), 0.0)
    g = jnp.where(live, -jnp.exp(log_rate) * _softplus(aalpha + dt_bias), 0.0)
    return beta, g


def _f_tshift(z, mu):
    return (z + (_shift_rows(z, 1) - z) * mu,)


def _f_rwkv_pre(k, wd, ad, gd, w0, w_up, a0, a_up, g_up, k_k, k_a):
    w_log = -_softplus(-(w0 + _smm(jnp.tanh(wd), w_up))) - 0.5
    lw = -jnp.exp(w_log)
    a_lr = _sigmoid(a0 + _smm(ad, a_up))
    gate = _smm(_sigmoid(gd), g_up)
    kkp = k * k_k
    kk = kkp * lax.rsqrt(_head_sum(kkp * kkp, B_HEADS) + 1e-6)
    kmod = k * (1.0 + (a_lr - 1.0) * k_a)
    return lw, kmod, -kk, kk * a_lr, gate


def _f_mix_post(o, az, y, r, kmod, v, gate, ga, gb, out_gain, ln_g, ln_b, r_k):
    ms = _head_sum(o * o, A_HEADS) * (1.0 / A_DK)
    oa = o * lax.rsqrt(ms + EPS) * out_gain * _silu(az)
    mean = _head_sum(y, B_HEADS) * (1.0 / B_N)
    yc = y - mean
    var = _head_sum(yc * yc, B_HEADS) * (1.0 / B_N)
    yn = yc * lax.rsqrt(var + B_GN_EPS) * ln_g + ln_b
    bonus = _head_sum(r * kmod * r_k, B_HEADS) * v
    ob = (yn + bonus) * gate
    return (_sigmoid(ga) * oa + _sigmoid(gb) * ob,)


SCAN_PASSES = 3


def _split2(a):
    hi = a.astype(bf16)
    return hi, (a - hi.astype(f32)).astype(bf16)


def _dot_passes(a, b, ca, cb, passes):
    dn = (((ca,), (cb,)), ((), ()))
    if SCAN_PASSES == 0:
        return lax.dot_general(a, b, dn, precision=HI, preferred_element_type=f32)
    if passes == 1:
        return lax.dot_general(a.astype(bf16), b.astype(bf16), dn, preferred_element_type=f32)
    ah, al = _split2(a)
    bh, bl = _split2(b)
    return (lax.dot_general(ah, bh, dn, preferred_element_type=f32)
            + (lax.dot_general(ah, bl, dn, preferred_element_type=f32)
               + lax.dot_general(al, bh, dn, preferred_element_type=f32)))


@functools.partial(jax.custom_vjp, nondiff_argnums=(2, 3, 4))
def _sdot(a, b, ca, cb, passes):
    return _dot_passes(a, b, ca, cb, passes)


def _sdot_fwd(a, b, ca, cb, passes):
    return _dot_passes(a, b, ca, cb, passes), (a, b)


def _sdot_bwd(ca, cb, passes, res, g):
    a, b = res
    if (ca, cb) == (1, 0):
        return _dot_passes(g, b, 1, 1, passes), _dot_passes(a, g, 0, 0, passes)
    if (ca, cb) == (1, 1):
        return _dot_passes(g, b, 1, 0, passes), _dot_passes(g, a, 0, 0, passes)
    assert (ca, cb) == (0, 0)
    return _dot_passes(b, g, 1, 1, passes), _dot_passes(a, g, 1, 0, passes)


_sdot.defvjp(_sdot_fwd, _sdot_bwd)


def _smm(a, b, passes=3):
    return _sdot(a, b, 1, 0, passes)


def _smm_nt(a, b, passes=3):
    return _sdot(a, b, 1, 1, passes)


def _smm_tn(a, b, passes=3):
    return _sdot(a, b, 0, 0, passes)


def _tri_dot(x, ca):
    n = x.shape[0]
    incl = _tri_masks(n)[0]
    dn = (((ca,), (0,)), ((), ()))
    if SCAN_PASSES == 0:
        return lax.dot_general(incl.astype(f32), x, dn, precision=HI, preferred_element_type=f32)
    tri = incl.astype(bf16)
    hi, r1 = x.astype(bf16), None
    r1 = x - hi.astype(f32)
    mid = r1.astype(bf16)
    lo = (r1 - mid.astype(f32)).astype(bf16)
    return (lax.dot_general(tri, hi, dn, preferred_element_type=f32)
            + (lax.dot_general(tri, mid, dn, preferred_element_type=f32)
               + lax.dot_general(tri, lo, dn, preferred_element_type=f32)))


@jax.custom_vjp
def _cumsum_rows(x):
    return _tri_dot(x, 1)


def _cumsum_rows_fwd(x):
    return _tri_dot(x, 1), None


def _cumsum_rows_bwd(_, g):
    return (_tri_dot(g, 0),)


_cumsum_rows.defvjp(_cumsum_rows_fwd, _cumsum_rows_bwd)


def _tri_masks(n):
    i = lax.broadcasted_iota(jnp.int32, (n, n), 0)
    j = lax.broadcasted_iota(jnp.int32, (n, n), 1)
    return i >= j, i > j, i == j, i <= j


def _unit_lower_inv_impl(low):
    n = low.shape[0]
    assert n == CHUNK
    _, _, eye, _ = _tri_masks(n)
    acc = eye.astype(f32) + low
    p = low
    for _ in range(5):
        p = _dot_passes(p, p, 1, 0, 3)
        acc = acc + _dot_passes(acc, p, 1, 0, 3)
    return acc


@jax.custom_vjp
def _unit_lower_inv(low):
    return _unit_lower_inv_impl(low)


def _unit_lower_inv_fwd(low):
    t = _unit_lower_inv_impl(low)
    return t, t


def _unit_lower_inv_bwd(t, g):
    return (_dot_passes(_dot_passes(t, g, 0, 0, 3), t, 1, 1, 3),)


_unit_lower_inv.defvjp(_unit_lower_inv_fwd, _unit_lower_inv_bwd)

DELTA_PASSES = 1


def _delta_chunk(s, q, k, v, beta, g):
    p = DELTA_PASSES
    incl, strict, eye, upper = _tri_masks(CHUNK)
    g_row = jnp.sum(jnp.where(eye, g, 0.0), axis=0, keepdims=True)
    gc = jnp.sum(jnp.where(incl, g_row, 0.0), axis=1, keepdims=True)
    gc_row = jnp.sum(jnp.where(upper, g, 0.0), axis=0, keepdims=True)
    decay = jnp.where(incl, jnp.exp(jnp.where(incl, gc - gc_row, 0.0)), 0.0)
    kb = k * beta
    vb = v * beta
    m = jnp.where(strict, _smm_nt(kb, k, p) * decay, 0.0)
    tinv = _unit_lower_inv(-m)
    u = _smm(tinv, vb, p)
    wk = _smm(tinv, kb * jnp.exp(gc), p)
    attn = _smm_nt(q, k, p) * decay
    qg = q * jnp.exp(gc)
    g_last = jnp.sum(g, axis=0, keepdims=True)
    k_tail = k * jnp.exp(g_last - gc)
    v_new = u - _smm(wk, s, p)
    o = _smm(qg, s, p) + _smm(attn, v_new, p)
    s_new = s * jnp.exp(g_last) + _smm_tn(k_tail, v_new, p)
    return o, s_new


def _rwkv_chunk(st, r, k, v, a, b, lw):
    c = CHUNK
    incl, strict, _, _ = _tri_masks(c)
    lane = lax.broadcasted_iota(jnp.int32, (c, 2 * B_N), 1)
    first = lane < B_N
    bi = lax.broadcasted_iota(jnp.int32, (2 * B_N, 2 * B_N), 0) < B_N
    bj = lax.broadcasted_iota(jnp.int32, (2 * B_N, 2 * B_N), 1) < B_N
    blockdiag = bi == bj
    cum = _cumsum_rows(lw)
    e_pos = jnp.exp(cum)
    e_neg = jnp.exp(-cum)
    rt = r * e_pos
    at = a * jnp.exp(cum - lw)
    kt = k * e_neg
    bt = b * e_neg
    a_s0 = _smm_nt(at, st)
    r_s0 = _smm_nt(rt, st)
    u = jnp.zeros((c, 2 * B_N), f32)
    for sel in (first, jnp.logical_not(first)):
        at_h = jnp.where(sel, at, 0.0)
        ab = jnp.where(strict, _smm_nt(at_h, bt), 0.0)
        ak = jnp.where(strict, _smm_nt(at_h, kt), 0.0)
        t_h = _unit_lower_inv(ab)
        u_h = _smm(t_h, jnp.where(sel, a_s0, 0.0) + _smm(ak, jnp.where(sel, v, 0.0)))
        u = u + u_h
    y = r_s0
    for sel in (first, jnp.logical_not(first)):
        rt_h = jnp.where(sel, rt, 0.0)
        rb = jnp.where(incl, _smm_nt(rt_h, bt), 0.0)
        rk = jnp.where(incl, _smm_nt(rt_h, kt), 0.0)
        y = y + _smm(rb, jnp.where(sel, u, 0.0)) + _smm(rk, jnp.where(sel, v, 0.0))
    cl = jnp.sum(lw, axis=0, keepdims=True)
    dec = jnp.exp(cl - cum)
    st_new = st * jnp.exp(cl) + jnp.where(blockdiag, _smm_tn(u, b * dec) + _smm_tn(v, k * dec), 0.0)
    return y, st_new


GROUPS_PER_STEP = 8


def _scan_specs(ins, col_offs, n_chunks, reverse):
    gw = GROUPS_PER_STEP * LANES
    cidx = (lambda c: n_chunks - 1 - c) if reverse else (lambda c: c)
    specs = []
    for a, off in zip(ins, col_offs):
        if a.ndim == 2:
            assert off % gw == 0
            specs.append(pl.BlockSpec((CHUNK, gw), lambda h, c, o=off // gw: (cidx(c), h + o)))
        else:
            specs.append(pl.BlockSpec((GROUPS_PER_STEP, CHUNK, 1), lambda h, c: (h, cidx(c), 0)))
    return specs, cidx


def _group_vals(refs, g):
    return [r[:, g * LANES:(g + 1) * LANES] if len(r.shape) == 2 else r[g] for r in refs]


def _scan_fwd(chunk_fn, ins, col_offs, n_groups, n_chunks, state_shape, name):
    n_in = len(ins)
    gps = GROUPS_PER_STEP
    t = ins[0].shape[0]

    def body(*refs):
        in_refs = refs[:n_in]
        o_ref, s0_ref, st = refs[n_in:]

        @pl.when(pl.program_id(1) == 0)
        def _():
            st[...] = jnp.zeros_like(st)

        states = st[...]
        vals = [jnp.stack(col) for col in zip(*[_group_vals(in_refs, g) for g in range(gps)])]
        o, s_new = jax.vmap(chunk_fn)(states, *vals)
        s0_ref[...] = states
        st[...] = s_new
        for g in range(gps):
            o_ref[:, g * LANES:(g + 1) * LANES] = o[g]

    specs, _ = _scan_specs(ins, col_offs, n_chunks, False)
    return pl.pallas_call(
        body, name=name, grid=(n_groups // gps, n_chunks), in_specs=specs,
        out_specs=[pl.BlockSpec((CHUNK, gps * LANES), lambda h, c: (c, h)),
                   pl.BlockSpec((gps, None) + state_shape, lambda h, c: (h, c, 0, 0))],
        out_shape=[jax.ShapeDtypeStruct((t, n_groups * LANES), f32),
                   jax.ShapeDtypeStruct((n_groups, n_chunks) + state_shape, f32)],
        scratch_shapes=[pltpu.VMEM((gps,) + state_shape, f32)],
        compiler_params=pltpu.CompilerParams(dimension_semantics=("parallel", "arbitrary")),
    )(*ins)


def _scan_bwd(chunk_fn, s0s, ins, col_offs, d_out, n_groups, n_chunks, state_shape, name):
    n_in = len(ins)
    gps = GROUPS_PER_STEP
    t = d_out.shape[0]

    def body(*refs):
        s0_ref = refs[0]
        in_refs = refs[1:1 + n_in]
        do_ref = refs[1 + n_in]
        g_refs = refs[2 + n_in:2 + 2 * n_in]
        dst = refs[2 + 2 * n_in]

        @pl.when(pl.program_id(1) == 0)
        def _():
            dst[...] = jnp.zeros_like(dst)

        vals = [jnp.stack(col) for col in zip(*[_group_vals(in_refs, g) for g in range(gps)])]
        d_o = jnp.stack([do_ref[:, g * LANES:(g + 1) * LANES] for g in range(gps)])
        _, vjp = jax.vjp(jax.vmap(chunk_fn), s0_ref[...], *vals)
        gs = vjp((d_o, dst[...]))
        dst[...] = gs[0]
        for g_ref, gv in zip(g_refs, gs[1:]):
            if len(g_ref.shape) == 2:
                for g in range(gps):
                    g_ref[:, g * LANES:(g + 1) * LANES] = gv[g]
            else:
                g_ref[...] = gv

    specs, cidx = _scan_specs(ins, col_offs, n_chunks, True)
    out_lane = pl.BlockSpec((CHUNK, gps * LANES), lambda h, c: (cidx(c), h))
    g_specs = [out_lane if a.ndim == 2 else sp for a, sp in zip(ins, specs)]
    g_shapes = [(t, n_groups * LANES) if a.ndim == 2 else a.shape for a in ins]
    s0_spec = pl.BlockSpec((gps, None) + state_shape, lambda h, c: (h, cidx(c), 0, 0))
    return pl.pallas_call(
        body, name=name, grid=(n_groups // gps, n_chunks), in_specs=[s0_spec] + specs + [out_lane],
        out_specs=g_specs, out_shape=[jax.ShapeDtypeStruct(sh, f32) for sh in g_shapes],
        scratch_shapes=[pltpu.VMEM((gps,) + state_shape, f32)],
        compiler_params=pltpu.CompilerParams(dimension_semantics=("parallel", "arbitrary")),
    )(s0s, *ins, d_out)


def _rms_fwd(x, g, name):
    t = x.shape[0]
    tm = _tile(t, 416, 8)
    return _tw_fwd(_f_rms, [x, g], [_row_spec(tm, D), _full_spec(g.shape)],
                   [jax.ShapeDtypeStruct(x.shape, f32)], [_row_spec(tm, D)], (t // tm,), name)[0]


def _rms_bwd(x, g, dy, name):
    t = x.shape[0]
    tm = _tile(t, 416, 8)
    return _tw_bwd(_f_rms, [x, g], [_row_spec(tm, D), _full_spec(g.shape)], [dy], [_row_spec(tm, D)],
                   ['tile', 'acc'], (t // tm,), name)


def _ffn_fwd(h, gain, wg, wu, wd, tag):
    xn = _rms_fwd(h, gain, f"{tag}_rms")
    gate = _matmul(xn, wg, name=f"{tag}_gate")
    up = _matmul(xn, wu, name=f"{tag}_up")
    t = h.shape[0]
    tm = _tile(t, 208, 8)
    act = _tw_fwd(_f_swiglu, [gate, up], [_row_spec(tm, D_FF)] * 2, [jax.ShapeDtypeStruct((t, D_FF), f32)],
                  [_row_spec(tm, D_FF)], (t // tm,), f"{tag}_act")[0]
    out = _matmul(act, wd, res=h, scale=0.5, name=f"{tag}_down")
    return out, (xn, gate, up, act)


def _ffn_bwd(h, gain, wg, wu, wd, saved, dout, tag):
    xn, gate, up, act = saved
    t = h.shape[0]
    d_wd = _matmul(act, dout, ta=True, scale=0.5, name=f"{tag}_dwd")
    d_act = _matmul(dout, wd, tb=True, scale=0.5, name=f"{tag}_dact")
    tm = _tile(t, 208, 8)
    d_gate, d_up = _tw_bwd(_f_swiglu, [gate, up], [_row_spec(tm, D_FF)] * 2, [d_act], [_row_spec(tm, D_FF)],
                           ['tile', 'tile'], (t // tm,), f"{tag}_dactf")
    d_wg = _matmul(xn, d_gate, ta=True, name=f"{tag}_dwg")
    d_wu = _matmul(xn, d_up, ta=True, name=f"{tag}_dwu")
    d_xn = _matmul(d_gate, wg, tb=True, name=f"{tag}_dxn_g")
    d_xn = _matmul(d_up, wu, tb=True, res=d_xn, name=f"{tag}_dxn_u")
    d_hn, d_gain = _rms_bwd(h, gain, d_xn, f"{tag}_drms")
    return d_hn, d_gain, d_wg, d_wu, d_wd


def _col_spec(t, first_block):
    return pl.BlockSpec((t, LANES), lambda j, fb=first_block: (0, j + fb))


def _local_step(h0, tgt, w):
    t = h0.shape[0]
    assert t % CHUNK == 0
    nc = t // CHUNK
    grads = {}

    h1, ffn1_saved = _ffn_fwd(h0, w['ffn1_norm'], w['ffn1_wg'], w['ffn1_wu'], w['ffn1_wd'], "ffn1")
    u = _rms_fwd(h1, w['mix_norm'], "mix_rms")
    z = _matmul(u, w['w_in_p'], name="in_proj")
    zs = z[:, 9216:9216 + 304]
    abeta, aalpha = zs[:, 288:296], zs[:, 296:304]

    conv_w = w['a_conv_w']
    conv_fns = [functools.partial(_f_conv, norm=True, scale=A_DK ** -0.5),
                functools.partial(_f_conv, norm=True, scale=1.0),
                functools.partial(_f_conv, norm=False, scale=1.0)]
    qkv = []
    for idx, fn in enumerate(conv_fns):
        qkv.append(_tw_fwd(fn, [z, conv_w], [_col_spec(t, 8 * idx), pl.BlockSpec((4, LANES), lambda j, o=8 * idx: (0, j + o))],
                           [jax.ShapeDtypeStruct((t, D), f32)], [_col_spec(t, 0)], (A_HEADS,), f"a_conv{idx}")[0])
    aq, ak, av = qkv
    tmg = _tile(t, 1040, 8)
    dg_fn = functools.partial(_f_dgates, tm=tmg)
    dg_specs = [_row_spec(tmg, A_HEADS)] * 2 + [_full_spec((1, A_HEADS))] * 2
    beta, gdec = _tw_fwd(dg_fn, [abeta, aalpha, w['a_log_rate'], w['a_dt_bias']], dg_specs,
                         [jax.ShapeDtypeStruct((t, A_HEADS), f32)] * 2, [_row_spec(tmg, A_HEADS)] * 2, (t // tmg,),
                         "a_gates", with_pid=True)
    beta_h = beta.T[:, :, None]
    gdec_h = gdec.T[:, :, None]
    a_ins = [aq, ak, av, beta_h, gdec_h]
    a_offs = [0] * 5
    o_scan, a_s0 = _scan_fwd(_delta_chunk, a_ins, a_offs, A_HEADS, nc, (A_DK, A_DK), "a_scan")

    mu = w['b_shift_mu']
    mu_rkv, mu_s = mu[:, :3072], mu[:, 3072:]
    zf_rkv = _tw_fwd(_f_tshift, [z, mu_rkv], [_col_spec(t, 32), pl.BlockSpec((1, LANES), lambda j: (0, j))],
                     [jax.ShapeDtypeStruct((t, 3072), f32)], [_col_spec(t, 0)], (24,), "b_shift")[0]
    zs_b = zs[:, :288]
    zf_s = _tw_fwd(_f_tshift, [zs_b, mu_s], [_full_spec((t, 288)), _full_spec((1, 288))],
                   [jax.ShapeDtypeStruct((t, 288), f32)], [_full_spec((t, 288))], (1,), "b_shift_s")[0]
    wdf, adf, gdf = zf_s[:, 0:64], zf_s[:, 64:128], zf_s[:, 128:288]
    tmr = _tile(t, 160, 8)
    pre_params = [w['b_w0'], w['b_w_up'], w['b_a0'], w['b_a_up'], w['b_g_up'], w['b_k_k'], w['b_k_a']]
    pre_ins = [zf_rkv, wdf, adf, gdf] + pre_params
    pre_specs = ([_row_spec(tmr, D, 1), _row_spec(tmr, 64), _row_spec(tmr, 64), _row_spec(tmr, 160)]
                 + [_full_spec(p.shape) for p in pre_params])
    lw, kmod, a_s, b_s, bgate = _tw_fwd(_f_rwkv_pre, pre_ins, pre_specs, [jax.ShapeDtypeStruct((t, D), f32)] * 5,
                                        [_row_spec(tmr, D)] * 5, (t // tmr,), "b_pre")
    b_ins = [zf_rkv, kmod, zf_rkv, a_s, b_s, lw]
    b_offs = [0, 0, 2 * D, 0, 0, 0]
    y_scan, b_s0 = _scan_fwd(_rwkv_chunk, b_ins, b_offs, B_HEADS // 2, nc, (2 * B_N, 2 * B_N), "b_scan")

    out_gain_t = jnp.tile(w['a_out_norm'], (1, A_HEADS))
    r_k = w['b_r_k'].reshape(1, D)
    post_params = [out_gain_t, w['b_ln_gain'], w['b_ln_bias'], r_k]
    post_ins = [o_scan, z, y_scan, zf_rkv, kmod, zf_rkv, bgate, z, z] + post_params
    post_specs = ([_row_spec(tmr, D), _row_spec(tmr, D, 3), _row_spec(tmr, D), _row_spec(tmr, D, 0), _row_spec(tmr, D),
                   _row_spec(tmr, D, 2), _row_spec(tmr, D), _row_spec(tmr, D, 7), _row_spec(tmr, D, 8)]
                  + [_full_spec((1, D))] * 4)
    merged = _tw_fwd(_f_mix_post, post_ins, post_specs, [jax.ShapeDtypeStruct((t, D), f32)], [_row_spec(tmr, D)],
                     (t // tmr,), "mix_post")[0]
    h2 = _matmul(merged, w['w_out'], res=h1, name="out_proj")
    h3, ffn2_saved = _ffn_fwd(h2, w['ffn2_norm'], w['ffn2_wg'], w['ffn2_wu'], w['ffn2_wd'], "ffn2")

    tml = _tile(t, 416, 8)
    fnorm = w['final_norm']
    loss_fn = functools.partial(_f_loss, tm=tml)
    loss_specs = [_row_spec(tml, D), _full_spec((1, D)), _row_spec(tml, D)]
    loss_parts = _tw_fwd(loss_fn, [h3, fnorm, tgt], loss_specs, [jax.ShapeDtypeStruct((t // tml, 1, 1), f32)],
                         [pl.BlockSpec((None, 1, 1), lambda i: (i, 0, 0))], (t // tml,), "loss", with_pid=True)[0]
    loss = jnp.sum(loss_parts)
    ones = jnp.ones((t // tml, 1, 1), f32)
    d_h3, grads['final_norm'] = _tw_bwd(loss_fn, [h3, fnorm, tgt], loss_specs, [ones],
                                        [pl.BlockSpec((None, 1, 1), lambda i: (i, 0, 0))], ['tile', 'acc', None],
                                        (t // tml,), "loss_bwd", with_pid=True)

    d_hn, grads['ffn2_norm'], grads['ffn2_wg'], grads['ffn2_wu'], grads['ffn2_wd'] = _ffn_bwd(
        h2, w['ffn2_norm'], w['ffn2_wg'], w['ffn2_wu'], w['ffn2_wd'], ffn2_saved, d_h3, "ffn2")
    d_h2 = _add(d_h3, d_hn, "add_h2")
    grads['w_out'] = _matmul(merged, d_h2, ta=True, name="d_w_out")
    d_merged = _matmul(d_h2, w['w_out'], tb=True, name="d_merged")

    win = ('tile', (t, D), _row_spec(tmr, D))
    post_kinds = ['tile', win, 'tile', win, 'tile', win, 'tile', win, win] + ['acc'] * 4
    (d_o, d_az, d_y, d_r1, d_kmod1, d_v1, d_bgate, d_ga, d_gb,
     d_out_gain_t, grads['b_ln_gain'], grads['b_ln_bias'], d_r_k) = _tw_bwd(
        _f_mix_post, post_ins, post_specs, [d_merged], [_row_spec(tmr, D)], post_kinds, (t // tmr,), "mix_post_bwd")
    grads['a_out_norm'] = jnp.sum(d_out_gain_t.reshape(A_HEADS, A_DK), axis=0, keepdims=True)
    grads['b_r_k'] = d_r_k.reshape(1, B_HEADS, B_N)

    d_r2, d_kmod2, d_v2, d_as, d_bs, d_lw = _scan_bwd(_rwkv_chunk, b_s0, b_ins, b_offs, d_y, B_HEADS // 2, nc,
                                                      (2 * B_N, 2 * B_N), "b_scan_bwd")
    d_kmod = _add(d_kmod1, d_kmod2, "add_kmod")
    pre_kinds = [win] + ['tile'] * 3 + ['acc'] * 7
    pre_ct_specs = [_row_spec(tmr, D)] * 5
    (d_zf_k, d_wdf, d_adf, d_gdf, grads['b_w0'], grads['b_w_up'], grads['b_a0'], grads['b_a_up'], grads['b_g_up'],
     grads['b_k_k'], grads['b_k_a']) = _tw_bwd(
        _f_rwkv_pre, pre_ins, pre_specs, [d_lw, d_kmod, d_as, d_bs, d_bgate], pre_ct_specs, pre_kinds, (t // tmr,),
        "b_pre_bwd")
    d_zf_rkv = _assemble3(d_r1, d_r2, d_zf_k, d_v1, d_v2, "b_dzf")
    d_zb_rkv, d_mu_rkv = _tw_bwd(_f_tshift, [z, mu_rkv], [_col_spec(t, 32), pl.BlockSpec((1, LANES), lambda j: (0, j))],
                                 [d_zf_rkv], [_col_spec(t, 0)], [('tile', (t, 3072), _col_spec(t, 0)), 'tile'], (24,),
                                 "b_shift_bwd")
    d_zf_s = jnp.concatenate([d_wdf, d_adf, d_gdf], axis=1)
    d_zs_b, d_mu_s = _tw_bwd(_f_tshift, [zs_b, mu_s], [_full_spec((t, 288)), _full_spec((1, 288))], [d_zf_s],
                             [_full_spec((t, 288))], ['tile', 'tile'], (1,), "b_shift_s_bwd")
    grads['b_shift_mu'] = jnp.concatenate([d_mu_rkv, d_mu_s], axis=1)

    d_aq, d_ak, d_av, d_beta_h, d_g_h = _scan_bwd(_delta_chunk, a_s0, a_ins, a_offs, d_o, A_HEADS, nc, (A_DK, A_DK),
                                                  "a_scan_bwd")
    d_beta = d_beta_h[:, :, 0].T
    d_gdec = d_g_h[:, :, 0].T
    d_abeta, d_aalpha, grads['a_log_rate'], grads['a_dt_bias'] = _tw_bwd(
        dg_fn, [abeta, aalpha, w['a_log_rate'], w['a_dt_bias']], dg_specs, [d_beta, d_gdec],
        [_row_spec(tmg, A_HEADS)] * 2, ['tile', 'tile', 'acc', 'acc'], (t // tmg,), "a_gates_bwd", with_pid=True)
    d_zqkv, d_conv = [], []
    for idx, (fn, ct) in enumerate(zip(conv_fns, (d_aq, d_ak, d_av))):
        dz_i, dw_i = _conv_bwd(fn, z, conv_w, ct, idx, t)
        d_zqkv.append(dz_i)
        d_conv.append(dw_i)
    grads['a_conv_w'] = jnp.concatenate(d_conv, axis=1)

    d_z = jnp.concatenate(
        d_zqkv + [d_az, d_zb_rkv, d_ga, d_gb, d_zs_b, d_abeta, d_aalpha, jnp.zeros((t, ZP - 9216 - 304), f32)], axis=1)
    grads['w_in_p'] = _matmul(u, d_z, ta=True, name="d_w_in")
    d_u = _matmul(d_z, w['w_in_p'], tb=True, name="d_u")
    d_h1n, grads['mix_norm'] = _rms_bwd(h1, w['mix_norm'], d_u, "mix_drms")
    d_h1 = _add(d_h2, d_h1n, "add_h1")
    d_h0n, grads['ffn1_norm'], grads['ffn1_wg'], grads['ffn1_wu'], grads['ffn1_wd'] = _ffn_bwd(
        h0, w['ffn1_norm'], w['ffn1_wg'], w['ffn1_wu'], w['ffn1_wd'], ffn1_saved, d_h1, "ffn1")
    d_h0 = _add(d_h1, d_h0n, "add_h0")
    return loss, d_h0, grads


_WIN_SEGMENTS = ((0, 4096), (4112, 7184), (7472, 9520), (7184, 7472), (4096, 4112))


def _win_to_padded(w_in):
    parts = [w_in[:, a:b] for a, b in _WIN_SEGMENTS]
    parts.append(jnp.zeros((w_in.shape[0], ZP - IN_TOTAL), w_in.dtype))
    return jnp.concatenate(parts, axis=1)


def _win_from_padded(w_p):
    widths = [b - a for a, b in _WIN_SEGMENTS]
    offs = [sum(widths[:i]) for i in range(len(widths))]
    seg = {a: w_p[:, o:o + wd] for (a, _), o, wd in zip(_WIN_SEGMENTS, offs, widths)}
    return jnp.concatenate([seg[a] for a in sorted(seg)], axis=1)


def _add(a, b, name):
    t, c = a.shape
    tm = _tile(t, 416, 8)
    return _tw_fwd(lambda x, y: (x + y,), [a, b], [_row_spec(tm, c)] * 2, [jax.ShapeDtypeStruct(a.shape, f32)],
                   [_row_spec(tm, c)], (t // tm,), name)[0]


def _assemble3(d_r1, d_r2, d_k, d_v1, d_v2, name):
    t = d_r1.shape[0]
    tm = _tile(t, 208, 8)

    def body(r1, r2, kk, v1, v2, o_ref):
        o_ref[:, 0:D] = r1[...] + r2[...]
        o_ref[:, D:2 * D] = kk[...]
        o_ref[:, 2 * D:3 * D] = v1[...] + v2[...]

    return pl.pallas_call(body, name=name, grid=(t // tm,), in_specs=[_row_spec(tm, D)] * 5,
                          out_specs=_row_spec(tm, 3 * D), out_shape=jax.ShapeDtypeStruct((t, 3 * D), f32),
                          )(d_r1, d_r2, d_k, d_v1, d_v2)


def _conv_bwd(fn, z, conv_w, ct, idx, t):
    def body(z_ref, w_ref, ct_ref, dz_ref, dw_ref):
        _, vjp = jax.vjp(lambda a, b: fn(a, b), z_ref[...], w_ref[...])
        dz, dw = vjp((ct_ref[...],))
        dz_ref[...] = dz
        dw_ref[...] = dw

    return pl.pallas_call(
        body, name=f"a_conv{idx}_bwd", grid=(A_HEADS,),
        in_specs=[_col_spec(t, 8 * idx), pl.BlockSpec((4, LANES), lambda j, o=8 * idx: (0, j + o)), _col_spec(t, 0)],
        out_specs=[_col_spec(t, 0), pl.BlockSpec((4, LANES), lambda j: (0, j))],
        out_shape=[jax.ShapeDtypeStruct((t, D), f32), jax.ShapeDtypeStruct((4, D), f32)],
    )(z, conv_w, ct)


def _position():
    return lax.axis_index("x"), lax.axis_index("y"), lax.axis_index("c")


def _flip(v, f):
    return 1 - v if f else v


_CHIP_FLIPS = ((1, 0), (0, 1), (1, 1))
_DEV_FLIPS = tuple((fx, fy, fc) for fx in (0, 1) for fy in (0, 1) for fc in (0, 1) if (fx, fy, fc) != (0, 0, 0))


def _gather_chips(arrs, name):
    n = len(arrs)
    assert all(a.shape[0] % 32 == 0 for a in arrs)
    arrs = [a.reshape(2, a.shape[0] // 2, a.shape[1]) for a in arrs]

    def body(*refs):
        ins, outs = refs[:n], refs[n:2 * n]
        send, recv, fsend, frecv, own = refs[2 * n:]
        x, y, c = _position()
        me = 2 * x + y
        sends, plan, owns = [], [], []
        for a in range(n):
            cp = pltpu.make_async_remote_copy(src_ref=ins[a], dst_ref=outs[a].at[me], send_sem=own.at[a, 0],
                                              recv_sem=own.at[a, 1], device_id=(x, y, 1 - c), device_id_type=MESH)
            cp.start()
            owns.append(cp)
            for j, (fx, fy) in enumerate(_CHIP_FLIPS):
                px, py = _flip(x, fx), _flip(y, fy)
                p = 2 * px + py
                cp = pltpu.make_async_remote_copy(src_ref=ins[a].at[c], dst_ref=outs[a].at[me, c],
                                                  send_sem=send.at[a, j], recv_sem=recv.at[a, j],
                                                  device_id=(px, py, c), device_id_type=MESH)
                cp.start()
                sends.append(cp)
                landed = pltpu.make_async_remote_copy(src_ref=ins[a].at[c], dst_ref=outs[a].at[p, c],
                                                      send_sem=send.at[a, j], recv_sem=recv.at[a, j],
                                                      device_id=(px, py, c), device_id_type=MESH)
                onward = pltpu.make_async_remote_copy(src_ref=outs[a].at[p, c], dst_ref=outs[a].at[p, c],
                                                      send_sem=fsend.at[a, j], recv_sem=frecv.at[a, j],
                                                      device_id=(x, y, 1 - c), device_id_type=MESH)
                from_sibling = pltpu.make_async_remote_copy(src_ref=outs[a].at[p, 1 - c], dst_ref=outs[a].at[p, 1 - c],
                                                            send_sem=fsend.at[a, j], recv_sem=frecv.at[a, j],
                                                            device_id=(x, y, 1 - c), device_id_type=MESH)
                plan.append((landed, onward, from_sibling))
        for landed, onward, _ in plan:
            landed.wait_recv()
            onward.start()
        for _, _, from_sibling in plan:
            from_sibling.wait_recv()
        for cp in sends:
            cp.wait_send()
        for _, onward, _ in plan:
            onward.wait_send()
        for cp in owns:
            cp.wait()

    sems = [pltpu.SemaphoreType.DMA((n, 3))] * 4 + [pltpu.SemaphoreType.DMA((n, 2))]
    outs = pl.pallas_call(
        body, name=name, in_specs=[ANY] * n, out_specs=[ANY] * n,
        out_shape=[jax.ShapeDtypeStruct((N_CHIPS,) + a.shape, a.dtype) for a in arrs], scratch_shapes=sems,
    )(*arrs)
    return [o.reshape(N_CHIPS, o.shape[1] * o.shape[2], o.shape[3]) for o in outs]


def _swap_sibling(src_of, shape, dtype, name):
    def body(a_ref, got_ref, send, recv):
        x, y, c = _position()
        cp = pltpu.make_async_remote_copy(src_ref=src_of(a_ref, c), dst_ref=got_ref, send_sem=send, recv_sem=recv,
                                          device_id=(x, y, 1 - c), device_id_type=MESH)
        cp.start()
        cp.wait()

    def call(a):
        return pl.pallas_call(body, name=name, in_specs=[ANY], out_specs=ANY,
                              out_shape=jax.ShapeDtypeStruct(shape, dtype),
                              scratch_shapes=[pltpu.SemaphoreType.DMA(())] * 2)(a)
    return call


def _add_halves(g, got, dtype, name):
    _, n, hr, w = g.shape
    tr = _tile(hr, 784, 16)

    def body(g_ref, got_ref, o_ref):
        c = lax.axis_index("c")
        own = jnp.where(c == 0, g_ref[0], g_ref[1])
        o_ref[...] = (own + got_ref[...]).astype(dtype)

    return pl.pallas_call(
        body, name=name, grid=(hr // tr,),
        in_specs=[pl.BlockSpec((2, n, tr, w), lambda i: (0, 0, i, 0)), pl.BlockSpec((n, tr, w), lambda i: (0, i, 0))],
        out_specs=pl.BlockSpec((n, tr, w), lambda i: (0, i, 0)),
        out_shape=jax.ShapeDtypeStruct((n, hr, w), dtype))(g, got)


def _scatter_chips(g, name):
    def body(g_ref, out_ref, send, recv):
        x, y, c = _position()
        sends = []
        for j, (fx, fy) in enumerate(_CHIP_FLIPS):
            px, py = _flip(x, fx), _flip(y, fy)
            cp = pltpu.make_async_remote_copy(src_ref=g_ref.at[2 * px + py], dst_ref=out_ref.at[j], send_sem=send.at[j],
                                              recv_sem=recv.at[j], device_id=(px, py, c), device_id_type=MESH)
            cp.start()
            sends.append(cp)
        for cp in sends:
            cp.wait_recv()
        for cp in sends:
            cp.wait_send()

    return pl.pallas_call(
        body, name=name, in_specs=[ANY], out_specs=ANY, out_shape=jax.ShapeDtypeStruct((3,) + g.shape[1:], g.dtype),
        scratch_shapes=[pltpu.SemaphoreType.DMA((3,)), pltpu.SemaphoreType.DMA((3,))],
    )(g)


def _sum_own_and_slots(own, got, name):
    n, r, w = own.shape
    tr = _tile(r, 784, 16)

    def body(own_ref, got_ref, o_ref):
        me = 2 * lax.axis_index("x") + lax.axis_index("y")
        acc = own_ref[0]
        for i in range(1, n):
            acc = jnp.where(me == i, own_ref[i], acc)
        acc = acc.astype(f32)
        for j in range(3):
            acc = acc + got_ref[j].astype(f32)
        o_ref[...] = acc

    return pl.pallas_call(
        body, name=name, grid=(r // tr,),
        in_specs=[pl.BlockSpec((n, tr, w), lambda i: (0, i, 0)), pl.BlockSpec((3, tr, w), lambda i: (0, i, 0))],
        out_specs=pl.BlockSpec((tr, w), lambda i: (i, 0)), out_shape=jax.ShapeDtypeStruct((r, w), f32))(own, got)


def _gather_devices(s, name):
    def body(s_ref, out_ref, send, recv, loc):
        x, y, c = _position()
        me = 4 * x + 2 * y + c
        lc = pltpu.make_async_copy(s_ref, out_ref.at[me], loc)
        lc.start()
        sends, recvs = [], []
        for j, (fx, fy, fc) in enumerate(_DEV_FLIPS):
            px, py, pc = _flip(x, fx), _flip(y, fy), _flip(c, fc)
            cp = pltpu.make_async_remote_copy(src_ref=s_ref, dst_ref=out_ref.at[me], send_sem=send.at[j],
                                              recv_sem=recv.at[j], device_id=(px, py, pc), device_id_type=MESH)
            cp.start()
            sends.append(cp)
            recvs.append(pltpu.make_async_remote_copy(
                src_ref=s_ref, dst_ref=out_ref.at[4 * px + 2 * py + pc], send_sem=send.at[j], recv_sem=recv.at[j],
                device_id=(px, py, pc), device_id_type=MESH))
        for cp in recvs:
            cp.wait_recv()
        for cp in sends:
            cp.wait_send()
        lc.wait()

    return pl.pallas_call(
        body, name=name, in_specs=[ANY], out_specs=ANY, out_shape=jax.ShapeDtypeStruct((N_DEV,) + s.shape, s.dtype),
        scratch_shapes=[pltpu.SemaphoreType.DMA((7,)), pltpu.SemaphoreType.DMA((7,)), pltpu.SemaphoreType.DMA(())],
    )(s)


def _sum_slots(a, name):
    s, r, c = a.shape
    tr = _tile(r, 2048, 16)

    def body(a_ref, o_ref):
        acc = a_ref[0].astype(f32)
        for i in range(1, s):
            acc = acc + a_ref[i].astype(f32)
        o_ref[...] = acc

    return pl.pallas_call(body, name=name, grid=(r // tr,), in_specs=[pl.BlockSpec((s, tr, c), lambda i: (0, i, 0))],
                          out_specs=pl.BlockSpec((tr, c), lambda i: (i, 0)),
                          out_shape=jax.ShapeDtypeStruct((r, c), f32))(a)


def _adamw(w, g_parts, m, v, name):
    shape = w.shape
    size = w.size
    view = (size // LANES, LANES) if size % LANES == 0 else (1, size)
    rows = view[0]
    tr = _tile(rows, 2048, 8) if rows > 2048 else rows
    n_g = len(g_parts)

    def body(*refs):
        w_ref = refs[0]
        g_refs = refs[1:1 + n_g]
        m_ref, v_ref, g_out, d_out, m_out, v_out = refs[1 + n_g:]
        g = g_refs[0][...]
        for gr in g_refs[1:]:
            g = g + gr[...]
        m_new = ADAM_B1 * m_ref[...] + (1.0 - ADAM_B1) * g
        v_new = ADAM_B2 * v_ref[...] + (1.0 - ADAM_B2) * (g * g)
        m_hat = m_new / (1.0 - ADAM_B1 ** ADAM_STEP)
        v_hat = v_new / (1.0 - ADAM_B2 ** ADAM_STEP)
        g_out[...] = g
        d_out[...] = -ADAM_LR * (m_hat / (jnp.sqrt(v_hat) + ADAM_EPS) + ADAM_WD * w_ref[...])
        m_out[...] = m_new
        v_out[...] = v_new

    spec = pl.BlockSpec((tr, view[1]), lambda i: (i, 0))
    args = [w.reshape(view)] + [g.reshape(view) for g in g_parts] + [m.reshape(view), v.reshape(view)]
    outs = pl.pallas_call(body, name=name, grid=(rows // tr,), in_specs=[spec] * len(args), out_specs=[spec] * 4,
                          out_shape=[jax.ShapeDtypeStruct(view, f32)] * 4)(*args)
    return [o.reshape(shape) for o in outs]


_BIG = ('ffn1_w_gu', 'ffn1_w_down', 'w_in', 'w_out', 'ffn2_w_gu', 'ffn2_w_down')
_SMALL_SHARDED = ('meta_tokens', 'a_conv_w', 'b_w_up', 'b_a_up', 'b_g_up')
_WEIGHTS = ('meta_tokens', 'ffn1_norm', 'ffn1_w_gu', 'ffn1_w_down', 'mix_norm', 'w_in', 'a_conv_w', 'a_log_rate',
            'a_dt_bias', 'a_out_norm', 'b_shift_mu', 'b_w0', 'b_w_up', 'b_a0', 'b_a_up', 'b_g_up', 'b_k_k', 'b_k_a',
            'b_r_k', 'b_ln_gain', 'b_ln_bias', 'w_out', 'ffn2_norm', 'ffn2_w_gu', 'ffn2_w_down', 'final_norm')
_SMALL = tuple(n for n in _WEIGHTS if n not in _BIG)


def _rows_of(shape):
    n = 1
    for d in shape:
        n *= d
    return n, -(-n // LANES)


def _pack(arrs, dtype, row_mult=32):
    parts, total = [], 0
    for a in arrs:
        n, rows = _rows_of(a.shape)
        flat = a.reshape(-1).astype(dtype)
        if n % LANES:
            flat = jnp.pad(flat, (0, rows * LANES - n))
        parts.append(flat)
        total += rows
    extra = -total % row_mult
    if extra:
        parts.append(jnp.zeros((extra * LANES,), dtype))
    return jnp.concatenate(parts).reshape(total + extra, LANES)


def _unpack(packed, shapes, lead=()):
    out, off = [], 0
    for sh in shapes:
        n, rows = _rows_of(sh)
        seg = packed[..., off:off + rows, :]
        if n % LANES:
            seg = seg.reshape(lead + (-1,))[..., :n]
        out.append(seg.reshape(lead + tuple(sh)))
        off += rows
    return out


def _cols_from_shards(s):
    return jnp.concatenate([s[i] for i in range(N_CHIPS)], axis=-1)


def _cols_to_shards(a):
    r, c = a.shape
    return a.reshape(r, N_CHIPS, c // N_CHIPS).transpose(1, 0, 2)


def kernel(x, meta_tokens, ffn1_norm, ffn1_w_gu, ffn1_w_down, mix_norm, w_in, a_conv_w, a_log_rate, a_dt_bias, a_out_norm, b_shift_mu, b_w0, b_w_up, b_a0, b_a_up, b_g_up, b_k_k, b_k_a, b_r_k, b_ln_gain, b_ln_bias, w_out, ffn2_norm, ffn2_w_gu, ffn2_w_down, final_norm, loss_target, m_meta_tokens, m_ffn1_norm, m_ffn1_w_gu, m_ffn1_w_down, m_mix_norm, m_w_in, m_a_conv_w, m_a_log_rate, m_a_dt_bias, m_a_out_norm, m_b_shift_mu, m_b_w0, m_b_w_up, m_b_a0, m_b_a_up, m_b_g_up, m_b_k_k, m_b_k_a, m_b_r_k, m_b_ln_gain, m_b_ln_bias, m_w_out, m_ffn2_norm, m_ffn2_w_gu, m_ffn2_w_down, m_final_norm, v_meta_tokens, v_ffn1_norm, v_ffn1_w_gu, v_ffn1_w_down, v_mix_norm, v_w_in, v_a_conv_w, v_a_log_rate, v_a_dt_bias, v_a_out_norm, v_b_shift_mu, v_b_w0, v_b_w_up, v_b_a0, v_b_a_up, v_b_g_up, v_b_k_k, v_b_k_a, v_b_r_k, v_b_ln_gain, v_b_ln_bias, v_w_out, v_ffn2_norm, v_ffn2_w_gu, v_ffn2_w_down, v_final_norm):
    args = locals()
    wts = {n: args[n] for n in _WEIGHTS}
    mom = {n: args["m_" + n] for n in _WEIGHTS}
    var = {n: args["v_" + n] for n in _WEIGHTS}
    chip = 2 * lax.axis_index("x") + lax.axis_index("y")

    big_shapes = [wts[n].shape[1:] for n in _BIG]
    small_shapes = [wts[n].shape[-2:] for n in _SMALL_SHARDED]
    big_packed = _pack([wts[n] for n in _BIG], bf16)
    small_packed = _pack([wts[n] for n in _SMALL_SHARDED], f32)
    big_all, small_all = _gather_chips([big_packed, small_packed], "gather_weights")
    gu1, dn1, w_in_s, w_out_s, gu2, dn2 = _unpack(big_all, big_shapes, (N_CHIPS,))
    meta_s, conv_s, wup_s, aup_s, gup_s = _unpack(small_all, small_shapes, (N_CHIPS,))
    w = {
        'ffn1_norm': ffn1_norm, 'mix_norm': mix_norm, 'ffn2_norm': ffn2_norm, 'final_norm': final_norm[None, :],
        'ffn1_wg': jnp.concatenate([gu1[0], gu1[1]], axis=1), 'ffn1_wu': jnp.concatenate([gu1[2], gu1[3]], axis=1),
        'ffn1_wd': dn1.reshape(D_FF, D),
        'ffn2_wg': jnp.concatenate([gu2[0], gu2[1]], axis=1), 'ffn2_wu': jnp.concatenate([gu2[2], gu2[3]], axis=1),
        'ffn2_wd': dn2.reshape(D_FF, D),
        'w_in_p': _win_to_padded(_cols_from_shards(w_in_s)), 'w_out': w_out_s.reshape(D, D),
        'a_conv_w': _cols_from_shards(conv_s), 'b_w_up': _cols_from_shards(wup_s), 'b_a_up': _cols_from_shards(aup_s),
        'b_g_up': _cols_from_shards(gup_s),
        'a_log_rate': a_log_rate, 'a_dt_bias': a_dt_bias, 'a_out_norm': a_out_norm, 'b_shift_mu': b_shift_mu,
        'b_w0': b_w0, 'b_a0': b_a0, 'b_k_k': b_k_k, 'b_k_a': b_k_a, 'b_r_k': b_r_k, 'b_ln_gain': b_ln_gain,
        'b_ln_bias': b_ln_bias,
    }
    meta_full = _cols_from_shards(meta_s)

    h0 = jnp.concatenate([jnp.zeros((PAD, D), f32), meta_full, x[0]], axis=0)
    tgt = jnp.concatenate([jnp.zeros((SKIP, D), f32), loss_target[0]], axis=0)
    loss_local, d_h0, g = _local_step(h0, tgt, w)
    loss = lax.psum(loss_local, ("x", "y", "c"))
    grad_x = d_h0[SKIP:][None]

    big_grads = [
        _cols_to_shards(jnp.concatenate([g['ffn1_wg'], g['ffn1_wu']], axis=1)),
        g['ffn1_wd'].reshape(N_CHIPS, D_FF // N_CHIPS, D),
        _cols_to_shards(_win_from_padded(g['w_in_p'])),
        g['w_out'].reshape(N_CHIPS, D // N_CHIPS, D),
        _cols_to_shards(jnp.concatenate([g['ffn2_wg'], g['ffn2_wu']], axis=1)),
        g['ffn2_wd'].reshape(N_CHIPS, D_FF // N_CHIPS, D),
    ]
    g_packed = jnp.concatenate([a.reshape(N_CHIPS, -1, LANES) for a in big_grads], axis=1)
    assert g_packed.shape[1] == big_packed.shape[0]
    rows = g_packed.shape[1]
    g_halves = g_packed.reshape(N_CHIPS, 2, rows // 2, LANES).transpose(1, 0, 2, 3)
    half_shape = (N_CHIPS, rows // 2, LANES)
    sib_half = _swap_sibling(lambda ref, c: ref.at[1 - c], half_shape, f32, "swap_halves")(g_halves)
    chip_half = _add_halves(g_halves, sib_half, bf16, "add_sibling")
    mine = _sum_own_and_slots(chip_half, _scatter_chips(chip_half, "scatter_grads"), "sum_chips")
    theirs = _swap_sibling(lambda ref, c: ref, mine.shape, f32, "swap_sums")(mine)
    core = lax.axis_index("c")
    summed = jnp.concatenate([jnp.where(core == 0, mine, theirs), jnp.where(core == 0, theirs, mine)], axis=0)
    big_parts = _unpack(summed, big_shapes)

    small_full = {
        'meta_tokens': d_h0[PAD:SKIP], 'ffn1_norm': g['ffn1_norm'], 'mix_norm': g['mix_norm'], 'a_conv_w': g['a_conv_w'],
        'a_log_rate': g['a_log_rate'], 'a_dt_bias': g['a_dt_bias'], 'a_out_norm': g['a_out_norm'],
        'b_shift_mu': g['b_shift_mu'], 'b_w0': g['b_w0'], 'b_w_up': g['b_w_up'], 'b_a0': g['b_a0'], 'b_a_up': g['b_a_up'],
        'b_g_up': g['b_g_up'], 'b_k_k': g['b_k_k'], 'b_k_a': g['b_k_a'], 'b_r_k': g['b_r_k'], 'b_ln_gain': g['b_ln_gain'],
        'b_ln_bias': g['b_ln_bias'], 'ffn2_norm': g['ffn2_norm'], 'final_norm': g['final_norm'],
    }
    s_shapes = [small_full[n].shape for n in _SMALL]
    s_sum = _sum_slots(_gather_devices(_pack([small_full[n] for n in _SMALL], f32, row_mult=256), "gather_small"),
                       "sum_small")
    s_parts = dict(zip(_SMALL, _unpack(s_sum, s_shapes)))

    grad, delta, new_m, new_v = {}, {}, {}, {}
    for n, a in zip(_BIG, big_parts):
        grad[n], delta[n], new_m[n], new_v[n] = _adamw(wts[n], [a.reshape(wts[n].shape)], mom[n], var[n], f"adamw_{n}")
    for n in _SMALL:
        gs = s_parts[n]
        if n in _SMALL_SHARDED:
            width = wts[n].shape[-1]
            gs = lax.dynamic_slice_in_dim(gs, chip * width, width, axis=gs.ndim - 1)
        gs = gs.reshape(wts[n].shape)
        grad[n], delta[n], new_m[n], new_v[n] = _adamw(wts[n], [gs], mom[n], var[n], f"adamw_{n}")

    return (loss, grad_x, *[grad[n] for n in _WEIGHTS], *[delta[n] for n in _WEIGHTS],
            *[new_m[n] for n in _WEIGHTS], *[new_v[n] for n in _WEIGHTS])
```

```python
import functools

import jax
import jax.numpy as jnp
from jax import lax
from jax.experimental import pallas as pl
from jax.experimental.pallas import tpu as pltpu

f32 = jnp.float32
bf16 = jnp.bfloat16
HI = lax.Precision.HIGHEST
MESH = pl.DeviceIdType.MESH
ANY = pl.BlockSpec(memory_space=pl.ANY)

D = 1024
N_META = 16
CHUNK = 64
PAD = CHUNK - N_META
SKIP = PAD + N_META
EPS = 1e-6
D_FF = 2816
A_HEADS = 8
A_DK = 128
B_HEADS = 16
B_N = 64
B_GN_EPS = B_N * 1e-5
W_LORA, AA_LORA, G_LORA = 64, 64, 160
IN_TOTAL = 9520
ZP = 9600
LANES = 128
N_CHIPS = 4
N_DEV = 8

ADAM_LR, ADAM_B1, ADAM_B2, ADAM_EPS, ADAM_WD, ADAM_STEP = 0.001, 0.9, 0.999, 1e-08, 0.01, 10

MXU_DTYPE = bf16


def _tile(n, cap, mult):
    if n <= cap:
        return n
    best = None
    for t in range(mult, cap + 1, mult):
        if n % t == 0:
            best = t
    assert best is not None, (n, cap, mult)
    return best


def _sigmoid(x):
    return jax.nn.sigmoid(x)


def _silu(x):
    return x * jax.nn.sigmoid(x)


def _softplus(x):
    return jnp.maximum(x, 0.0) + jnp.log(1.0 + jnp.exp(-jnp.abs(x)))


def _head_matrix(c, nh):
    hd = c // nh
    r = lax.broadcasted_iota(jnp.int32, (c, nh), 0)
    h = lax.broadcasted_iota(jnp.int32, (c, nh), 1)
    return (r >= h * hd) & (r < (h + 1) * hd)


def _dot_exact_rhs(x, e, cb):
    dn = (((1,), (cb,)), ((), ()))
    if SCAN_PASSES == 0:
        return lax.dot_general(x, e.astype(f32), dn, precision=HI, preferred_element_type=f32)
    eb = e.astype(bf16)
    hi = x.astype(bf16)
    lo = (x - hi.astype(f32)).astype(bf16)
    return (lax.dot_general(hi, eb, dn, preferred_element_type=f32)
            + lax.dot_general(lo, eb, dn, preferred_element_type=f32))


def _head_sum_impl(x, nh):
    e = _head_matrix(x.shape[-1], nh)
    return _dot_exact_rhs(_dot_exact_rhs(x, e, 0), e, 1)


@functools.partial(jax.custom_vjp, nondiff_argnums=(1,))
def _head_sum(x, nh):
    return _head_sum_impl(x, nh)


def _head_sum_fwd(x, nh):
    return _head_sum_impl(x, nh), None


def _head_sum_bwd(nh, _, g):
    return (_head_sum_impl(g, nh),)


_head_sum.defvjp(_head_sum_fwd, _head_sum_bwd)


@functools.partial(jax.custom_vjp, nondiff_argnums=(1,))
def _shift_rows(x, s):
    n = x.shape[0]
    row = lax.broadcasted_iota(jnp.int32, x.shape, 0)
    if s > 0:
        return jnp.where(row >= s, pltpu.roll(x, s, 0), 0.0)
    return jnp.where(row < n + s, pltpu.roll(x, n + s, 0), 0.0)


def _shift_rows_fwd(x, s):
    return _shift_rows(x, s), None


def _shift_rows_bwd(s, _, g):
    return (_shift_rows(g, -s),)


_shift_rows.defvjp(_shift_rows_fwd, _shift_rows_bwd)


def _matmul(a, b, *, ta=False, tb=False, res=None, scale=1.0, name):
    assert not (ta and tb)
    (ar, ac), (br, bc) = a.shape, b.shape
    m, k = (ac, ar) if ta else (ar, ac)
    n, kb = (br, bc) if tb else (bc, br)
    assert k == kb, (a.shape, b.shape, ta, tb)
    tm = _tile(m, 1408, LANES) if ta else _tile(m, 832, 8)
    tn = _tile(n, 1408, LANES)
    tk = _tile(k, 1040, 8) if ta else _tile(k, 1408, LANES)
    nk = k // tk
    dn = (((0 if ta else 1,), (1 if tb else 0,)), ((), ()))

    def body(*refs):
        if res is not None:
            a_ref, b_ref, r_ref, o_ref, acc = refs
        else:
            a_ref, b_ref, o_ref, acc = refs
        kk = pl.program_id(2)

        @pl.when(kk == 0)
        def _():
            acc[...] = jnp.zeros_like(acc)

        acc[...] += lax.dot_general(a_ref[...].astype(MXU_DTYPE), b_ref[...].astype(MXU_DTYPE), dn,
                                    preferred_element_type=f32,
                                    precision=None if MXU_DTYPE == bf16 else HI)

        @pl.when(kk == nk - 1)
        def _():
            out = acc[...]
            if scale != 1.0:
                out = out * scale
            if res is not None:
                out = r_ref[...] + out
            o_ref[...] = out

    if ta:
        a_spec = pl.BlockSpec((tk, tm), lambda i, j, kk: (kk, i))
    else:
        a_spec = pl.BlockSpec((tm, tk), lambda i, j, kk: (i, kk))
    if tb:
        b_spec = pl.BlockSpec((tn, tk), lambda i, j, kk: (j, kk))
    else:
        b_spec = pl.BlockSpec((tk, tn), lambda i, j, kk: (kk, j))
    in_specs = [a_spec, b_spec]
    args = [a, b]
    if res is not None:
        in_specs.append(pl.BlockSpec((tm, tn), lambda i, j, kk: (i, j)))
        args.append(res)
    return pl.pallas_call(
        body, name=name, grid=(m // tm, n // tn, nk), in_specs=in_specs,
        out_specs=pl.BlockSpec((tm, tn), lambda i, j, kk: (i, j)),
        out_shape=jax.ShapeDtypeStruct((m, n), f32),
        scratch_shapes=[pltpu.VMEM((tm, tn), f32)],
        compiler_params=pltpu.CompilerParams(dimension_semantics=("parallel", "parallel", "arbitrary")),
    )(*args)


def _tw_fwd(fn, ins, in_specs, out_shapes, out_specs, grid, name, with_pid=False):
    n_in = len(ins)

    def body(*refs):
        vals = [r[...] for r in refs[:n_in]]
        outs = fn(pl.program_id(0), *vals) if with_pid else fn(*vals)
        for r, o in zip(refs[n_in:], outs):
            r[...] = o.astype(r.dtype)

    return pl.pallas_call(body, name=name, grid=grid, in_specs=in_specs, out_specs=out_specs,
                          out_shape=out_shapes)(*ins)


def _tw_bwd(fn, ins, in_specs, cts, ct_specs, kinds, grid, name, with_pid=False, tile_dtype=f32, ct_extra=(),
            residual=None):
    n_in, n_ct = len(ins), len(cts)
    diff = [i for i, kd in enumerate(kinds) if kd is not None]
    n_ex = len(ct_extra)

    def body(*refs):
        vals = [r[...] for r in refs[:n_in]]
        ctv = [r[...].astype(f32) for r in refs[n_in:n_in + n_ct]]
        for (ci, _), r in zip(ct_extra, refs[n_in + n_ct:n_in + n_ct + n_ex]):
            ctv[ci] = ctv[ci] + r[...]
        ctv = tuple(ctv)
        n_fixed = n_in + n_ct + n_ex
        res_ref = refs[n_fixed] if residual is not None else None
        g_refs = refs[n_fixed + (residual is not None):]
        pid = pl.program_id(0)

        def f(*dv):
            full = list(vals)
            for i, v in zip(diff, dv):
                full[i] = v
            out = fn(pid, *full) if with_pid else fn(*full)
            return tuple(out)

        _, vjp = jax.vjp(f, *[vals[i] for i in diff])
        gs = vjp(ctv)
        first = pid == 0
        for i2 in range(1, len(grid)):
            first = first & (pl.program_id(i2) == 0)
        for i, g, g_ref in zip(diff, gs, g_refs):
            if kinds[i] != 'acc':
                if i == 0 and res_ref is not None:
                    g = res_ref[...] + g
                g_ref[...] = g.astype(g_ref.dtype)
            else:
                @pl.when(first)
                def _(g=g, g_ref=g_ref):
                    g_ref[...] = g

                @pl.when(jnp.logical_not(first))
                def _(g=g, g_ref=g_ref):
                    g_ref[...] += g

    zero_map = {1: lambda *a: (0,), 2: lambda *a: (0, 0), 3: lambda *a: (0, 0, 0)}
    out_specs, out_shapes = [], []
    for i in diff:
        if kinds[i] == 'tile':
            out_shapes.append(jax.ShapeDtypeStruct(ins[i].shape, tile_dtype))
            out_specs.append(in_specs[i])
        elif kinds[i] == 'acc':
            out_shapes.append(jax.ShapeDtypeStruct(ins[i].shape, f32))
            out_specs.append(pl.BlockSpec(ins[i].shape, zero_map[ins[i].ndim]))
        else:
            out_shapes.append(jax.ShapeDtypeStruct(kinds[i][1], tile_dtype))
            out_specs.append(kinds[i][2])
    extra_specs = [ct_specs[ci] for ci, _ in ct_extra]
    extra = [a for _, a in ct_extra]
    if residual is not None:
        assert kinds[0] == 'tile'
        extra_specs.append(in_specs[0])
        extra.append(residual)
    return pl.pallas_call(body, name=name, grid=grid, in_specs=list(in_specs) + list(ct_specs) + extra_specs,
                          out_specs=out_specs, out_shape=out_shapes)(*ins, *cts, *extra)


def _row_spec(tm, c, col_block=0):
    return pl.BlockSpec((tm, c), lambda i, cb=col_block: (i, cb))


def _full_spec(shape):
    nd = len(shape)
    return pl.BlockSpec(shape, lambda *a, nd=nd: (0,) * nd)


def _f_rms(x, g):
    return (x * lax.rsqrt(jnp.mean(x * x, axis=-1, keepdims=True) + EPS) * g,)


def _f_swiglu(gate, up):
    return (_silu(gate) * up,)


def _f_loss(pid, h, g, tgt, *, tm):
    y = h * lax.rsqrt(jnp.mean(h * h, axis=-1, keepdims=True) + EPS) * g
    row = pid * tm + lax.broadcasted_iota(jnp.int32, (tm, 1), 0)
    err = jnp.where(row >= SKIP, y - tgt, 0.0)
    per_row = jnp.mean(err * err, axis=-1, keepdims=True)
    return (0.5 * jnp.sum(per_row, axis=0, keepdims=True),)


def _f_conv(x, w, *, norm, scale):
    y = x * w[3:4, :]
    for s in (1, 2, 3):
        y = y + _shift_rows(x, s) * w[3 - s:4 - s, :]
    y = _silu(y)
    if norm:
        y = y * lax.rsqrt(jnp.sum(y * y, axis=-1, keepdims=True) + 1e-6) * scale
    return (y,)


def _f_dgates(pid, abeta, aalpha, log_rate, dt_bias, *, tm):
    row = pid * tm + lax.broadcasted_iota(jnp.int32, (tm, 1), 0)
    live = row >= PAD
    beta = jnp.where(live, _sigmoid(abeta), 0.0)
    g = jnp.where(live, -jnp.exp(log_rate) * _softplus(aalpha + dt_bias), 0.0)
    return beta, g


def _f_tshift(z, mu):
    return (z + (_shift_rows(z, 1) - z) * mu,)


def _f_rwkv_pre(k, wd, ad, gd, w0, w_up, a0, a_up, g_up, k_k, k_a):
    w_log = -_softplus(-(w0 + _smm(jnp.tanh(wd), w_up))) - 0.5
    lw = -jnp.exp(w_log)
    a_lr = _sigmoid(a0 + _smm(ad, a_up))
    gate = _smm(_sigmoid(gd), g_up)
    kkp = k * k_k
    kk = kkp * lax.rsqrt(_head_sum(kkp * kkp, B_HEADS) + 1e-6)
    kmod = k * (1.0 + (a_lr - 1.0) * k_a)
    return lw, kmod, -kk, kk * a_lr, gate


def _f_mix_post(o, az, y, r, kmod, v, gate, ga, gb, out_gain, ln_g, ln_b, r_k):
    ms = _head_sum(o * o, A_HEADS) * (1.0 / A_DK)
    oa = o * lax.rsqrt(ms + EPS) * out_gain * _silu(az)
    mean = _head_sum(y, B_HEADS) * (1.0 / B_N)
    yc = y - mean
    var = _head_sum(yc * yc, B_HEADS) * (1.0 / B_N)
    yn = yc * lax.rsqrt(var + B_GN_EPS) * ln_g + ln_b
    bonus = _head_sum(r * kmod * r_k, B_HEADS) * v
    ob = (yn + bonus) * gate
    return (_sigmoid(ga) * oa + _sigmoid(gb) * ob,)


SCAN_PASSES = 3


def _split2(a):
    hi = a.astype(bf16)
    return hi, (a - hi.astype(f32)).astype(bf16)


def _dot_passes(a, b, ca, cb, passes):
    dn = (((ca,), (cb,)), ((), ()))
    if SCAN_PASSES == 0:
        return lax.dot_general(a, b, dn, precision=HI, preferred_element_type=f32)
    if passes == 1:
        return lax.dot_general(a.astype(bf16), b.astype(bf16), dn, preferred_element_type=f32)
    ah, al = _split2(a)
    bh, bl = _split2(b)
    return (lax.dot_general(ah, bh, dn, preferred_element_type=f32)
            + (lax.dot_general(ah, bl, dn, preferred_element_type=f32)
               + lax.dot_general(al, bh, dn, preferred_element_type=f32)))


@functools.partial(jax.custom_vjp, nondiff_argnums=(2, 3, 4))
def _sdot(a, b, ca, cb, passes):
    return _dot_passes(a, b, ca, cb, passes)


def _sdot_fwd(a, b, ca, cb, passes):
    return _dot_passes(a, b, ca, cb, passes), (a, b)


def _sdot_bwd(ca, cb, passes, res, g):
    a, b = res
    if (ca, cb) == (1, 0):
        return _dot_passes(g, b, 1, 1, passes), _dot_passes(a, g, 0, 0, passes)
    if (ca, cb) == (1, 1):
        return _dot_passes(g, b, 1, 0, passes), _dot_passes(g, a, 0, 0, passes)
    assert (ca, cb) == (0, 0)
    return _dot_passes(b, g, 1, 1, passes), _dot_passes(a, g, 1, 0, passes)


_sdot.defvjp(_sdot_fwd, _sdot_bwd)


def _smm(a, b, passes=3):
    return _sdot(a, b, 1, 0, passes)


def _smm_nt(a, b, passes=3):
    return _sdot(a, b, 1, 1, passes)


def _smm_tn(a, b, passes=3):
    return _sdot(a, b, 0, 0, passes)


def _tri_dot(x, ca):
    n = x.shape[0]
    incl = _tri_masks(n)[0]
    dn = (((ca,), (0,)), ((), ()))
    if SCAN_PASSES == 0:
        return lax.dot_general(incl.astype(f32), x, dn, precision=HI, preferred_element_type=f32)
    tri = incl.astype(bf16)
    hi, r1 = x.astype(bf16), None
    r1 = x - hi.astype(f32)
    mid = r1.astype(bf16)
    lo = (r1 - mid.astype(f32)).astype(bf16)
    return (lax.dot_general(tri, hi, dn, preferred_element_type=f32)
            + (lax.dot_general(tri, mid, dn, preferred_element_type=f32)
               + lax.dot_general(tri, lo, dn, preferred_element_type=f32)))


@jax.custom_vjp
def _cumsum_rows(x):
    return _tri_dot(x, 1)


def _cumsum_rows_fwd(x):
    return _tri_dot(x, 1), None


def _cumsum_rows_bwd(_, g):
    return (_tri_dot(g, 0),)


_cumsum_rows.defvjp(_cumsum_rows_fwd, _cumsum_rows_bwd)


def _tri_masks(n):
    i = lax.broadcasted_iota(jnp.int32, (n, n), 0)
    j = lax.broadcasted_iota(jnp.int32, (n, n), 1)
    return i >= j, i > j, i == j, i <= j


def _unit_lower_inv_impl(low):
    n = low.shape[0]
    assert n == CHUNK
    _, _, eye, _ = _tri_masks(n)
    acc = eye.astype(f32) + low
    p = low
    for _ in range(5):
        p = _dot_passes(p, p, 1, 0, 3)
        acc = acc + _dot_passes(acc, p, 1, 0, 3)
    return acc


@jax.custom_vjp
def _unit_lower_inv(low):
    return _unit_lower_inv_impl(low)


def _unit_lower_inv_fwd(low):
    t = _unit_lower_inv_impl(low)
    return t, t


def _unit_lower_inv_bwd(t, g):
    return (_dot_passes(_dot_passes(t, g, 0, 0, 3), t, 1, 1, 3),)


_unit_lower_inv.defvjp(_unit_lower_inv_fwd, _unit_lower_inv_bwd)

DELTA_PASSES = 1


def _delta_chunk(s, q, k, v, beta, g):
    p = DELTA_PASSES
    incl, strict, eye, upper = _tri_masks(CHUNK)
    g_row = jnp.sum(jnp.where(eye, g, 0.0), axis=0, keepdims=True)
    gc = jnp.sum(jnp.where(incl, g_row, 0.0), axis=1, keepdims=True)
    gc_row = jnp.sum(jnp.where(upper, g, 0.0), axis=0, keepdims=True)
    decay = jnp.where(incl, jnp.exp(jnp.where(incl, gc - gc_row, 0.0)), 0.0)
    kb = k * beta
    vb = v * beta
    m = jnp.where(strict, _smm_nt(kb, k, p) * decay, 0.0)
    tinv = _unit_lower_inv(-m)
    u = _smm(tinv, vb, p)
    wk = _smm(tinv, kb * jnp.exp(gc), p)
    attn = _smm_nt(q, k, p) * decay
    qg = q * jnp.exp(gc)
    g_last = jnp.sum(g, axis=0, keepdims=True)
    k_tail = k * jnp.exp(g_last - gc)
    v_new = u - _smm(wk, s, p)
    o = _smm(qg, s, p) + _smm(attn, v_new, p)
    s_new = s * jnp.exp(g_last) + _smm_tn(k_tail, v_new, p)
    return o, s_new


def _rwkv_chunk(st, r, k, v, a, b, lw):
    c = CHUNK
    incl, strict, _, _ = _tri_masks(c)
    lane = lax.broadcasted_iota(jnp.int32, (c, 2 * B_N), 1)
    first = lane < B_N
    bi = lax.broadcasted_iota(jnp.int32, (2 * B_N, 2 * B_N), 0) < B_N
    bj = lax.broadcasted_iota(jnp.int32, (2 * B_N, 2 * B_N), 1) < B_N
    blockdiag = bi == bj
    cum = _cumsum_rows(lw)
    e_pos = jnp.exp(cum)
    e_neg = jnp.exp(-cum)
    rt = r * e_pos
    at = a * jnp.exp(cum - lw)
    kt = k * e_neg
    bt = b * e_neg
    a_s0 = _smm_nt(at, st)
    r_s0 = _smm_nt(rt, st)
    u = jnp.zeros((c, 2 * B_N), f32)
    for sel in (first, jnp.logical_not(first)):
        at_h = jnp.where(sel, at, 0.0)
        ab = jnp.where(strict, _smm_nt(at_h, bt), 0.0)
        ak = jnp.where(strict, _smm_nt(at_h, kt), 0.0)
        t_h = _unit_lower_inv(ab)
        u_h = _smm(t_h, jnp.where(sel, a_s0, 0.0) + _smm(ak, jnp.where(sel, v, 0.0)))
        u = u + u_h
    y = r_s0
    for sel in (first, jnp.logical_not(first)):
        rt_h = jnp.where(sel, rt, 0.0)
        rb = jnp.where(incl, _smm_nt(rt_h, bt), 0.0)
        rk = jnp.where(incl, _smm_nt(rt_h, kt), 0.0)
        y = y + _smm(rb, jnp.where(sel, u, 0.0)) + _smm(rk, jnp.where(sel, v, 0.0))
    cl = jnp.sum(lw, axis=0, keepdims=True)
    dec = jnp.exp(cl - cum)
    st_new = st * jnp.exp(cl) + jnp.where(blockdiag, _smm_tn(u, b * dec) + _smm_tn(v, k * dec), 0.0)
    return y, st_new


GROUPS_PER_STEP = 8


def _scan_specs(ins, col_offs, n_chunks, reverse):
    gw = GROUPS_PER_STEP * LANES
    cidx = (lambda c: n_chunks - 1 - c) if reverse else (lambda c: c)
    specs = []
    for a, off in zip(ins, col_offs):
        if a.ndim == 2:
            assert off % gw == 0
            specs.append(pl.BlockSpec((CHUNK, gw), lambda h, c, o=off // gw: (cidx(c), h + o)))
        else:
            specs.append(pl.BlockSpec((GROUPS_PER_STEP, CHUNK, 1), lambda h, c: (h, cidx(c), 0)))
    return specs, cidx


def _group_vals(refs, g):
    return [r[:, g * LANES:(g + 1) * LANES] if len(r.shape) == 2 else r[g] for r in refs]


def _scan_fwd(chunk_fn, ins, col_offs, n_groups, n_chunks, state_shape, name):
    n_in = len(ins)
    gps = GROUPS_PER_STEP
    t = ins[0].shape[0]

    def body(*refs):
        in_refs = refs[:n_in]
        o_ref, s0_ref, st = refs[n_in:]

        @pl.when(pl.program_id(1) == 0)
        def _():
            st[...] = jnp.zeros_like(st)

        states = st[...]
        vals = [jnp.stack(col) for col in zip(*[_group_vals(in_refs, g) for g in range(gps)])]
        o, s_new = jax.vmap(chunk_fn)(states, *vals)
        s0_ref[...] = states
        st[...] = s_new
        for g in range(gps):
            o_ref[:, g * LANES:(g + 1) * LANES] = o[g]

    specs, _ = _scan_specs(ins, col_offs, n_chunks, False)
    return pl.pallas_call(
        body, name=name, grid=(n_groups // gps, n_chunks), in_specs=specs,
        out_specs=[pl.BlockSpec((CHUNK, gps * LANES), lambda h, c: (c, h)),
                   pl.BlockSpec((gps, None) + state_shape, lambda h, c: (h, c, 0, 0))],
        out_shape=[jax.ShapeDtypeStruct((t, n_groups * LANES), f32),
                   jax.ShapeDtypeStruct((n_groups, n_chunks) + state_shape, f32)],
        scratch_shapes=[pltpu.VMEM((gps,) + state_shape, f32)],
        compiler_params=pltpu.CompilerParams(dimension_semantics=("parallel", "arbitrary")),
    )(*ins)


def _scan_bwd(chunk_fn, s0s, ins, col_offs, d_out, n_groups, n_chunks, state_shape, name):
    n_in = len(ins)
    gps = GROUPS_PER_STEP
    t = d_out.shape[0]

    def body(*refs):
        s0_ref = refs[0]
        in_refs = refs[1:1 + n_in]
        do_ref = refs[1 + n_in]
        g_refs = refs[2 + n_in:2 + 2 * n_in]
        dst = refs[2 + 2 * n_in]

        @pl.when(pl.program_id(1) == 0)
        def _():
            dst[...] = jnp.zeros_like(dst)

        vals = [jnp.stack(col) for col in zip(*[_group_vals(in_refs, g) for g in range(gps)])]
        d_o = jnp.stack([do_ref[:, g * LANES:(g + 1) * LANES] for g in range(gps)])
        _, vjp = jax.vjp(jax.vmap(chunk_fn), s0_ref[...], *vals)
        gs = vjp((d_o, dst[...]))
        dst[...] = gs[0]
        for g_ref, gv in zip(g_refs, gs[1:]):
            if len(g_ref.shape) == 2:
                for g in range(gps):
                    g_ref[:, g * LANES:(g + 1) * LANES] = gv[g]
            else:
                g_ref[...] = gv

    specs, cidx = _scan_specs(ins, col_offs, n_chunks, True)
    out_lane = pl.BlockSpec((CHUNK, gps * LANES), lambda h, c: (cidx(c), h))
    g_specs = [out_lane if a.ndim == 2 else sp for a, sp in zip(ins, specs)]
    g_shapes = [(t, n_groups * LANES) if a.ndim == 2 else a.shape for a in ins]
    s0_spec = pl.BlockSpec((gps, None) + state_shape, lambda h, c: (h, cidx(c), 0, 0))
    return pl.pallas_call(
        body, name=name, grid=(n_groups // gps, n_chunks), in_specs=[s0_spec] + specs + [out_lane],
        out_specs=g_specs, out_shape=[jax.ShapeDtypeStruct(sh, f32) for sh in g_shapes],
        scratch_shapes=[pltpu.VMEM((gps,) + state_shape, f32)],
        compiler_params=pltpu.CompilerParams(dimension_semantics=("parallel", "arbitrary")),
    )(s0s, *ins, d_out)


def _rms_fwd(x, g, name):
    t = x.shape[0]
    tm = _tile(t, 416, 16)
    return _tw_fwd(_f_rms, [x, g], [_row_spec(tm, D), _full_spec(g.shape)],
                   [jax.ShapeDtypeStruct(x.shape, MXU_DTYPE)], [_row_spec(tm, D)], (t // tm,), name)[0]


def _rms_bwd(x, g, dy, residual, name):
    t = x.shape[0]
    tm = _tile(t, 416, 8)
    return _tw_bwd(_f_rms, [x, g], [_row_spec(tm, D), _full_spec(g.shape)], [dy], [_row_spec(tm, D)],
                   ['tile', 'acc'], (t // tm,), name, residual=residual)


def _ffn_fwd(h, gain, wg, wu, wd, tag):
    xn = _rms_fwd(h, gain, f"{tag}_rms")
    gate = _matmul(xn, wg, name=f"{tag}_gate")
    up = _matmul(xn, wu, name=f"{tag}_up")
    t = h.shape[0]
    tm = _tile(t, 208, 16)
    act = _tw_fwd(_f_swiglu, [gate, up], [_row_spec(tm, D_FF)] * 2, [jax.ShapeDtypeStruct((t, D_FF), MXU_DTYPE)],
                  [_row_spec(tm, D_FF)], (t // tm,), f"{tag}_act")[0]
    out = _matmul(act, wd, res=h, scale=0.5, name=f"{tag}_down")
    return out, (xn, gate, up, act)


def _ffn_bwd(h, gain, wg, wu, wd, saved, dout, tag):
    xn, gate, up, act = saved
    t = h.shape[0]
    d_wd = _matmul(act, dout, ta=True, scale=0.5, name=f"{tag}_dwd")
    d_act = _matmul(dout, wd, tb=True, scale=0.5, name=f"{tag}_dact")
    tm = _tile(t, 208, 16)
    d_gate, d_up = _tw_bwd(_f_swiglu, [gate, up], [_row_spec(tm, D_FF)] * 2, [d_act], [_row_spec(tm, D_FF)],
                           ['tile', 'tile'], (t // tm,), f"{tag}_dactf", tile_dtype=MXU_DTYPE)
    d_wg = _matmul(xn, d_gate, ta=True, name=f"{tag}_dwg")
    d_wu = _matmul(xn, d_up, ta=True, name=f"{tag}_dwu")
    d_xn = _matmul(d_gate, wg, tb=True, name=f"{tag}_dxn_g")
    d_xn = _matmul(d_up, wu, tb=True, res=d_xn, name=f"{tag}_dxn_u")
    d_h, d_gain = _rms_bwd(h, gain, d_xn, dout, f"{tag}_drms")
    return d_h, d_gain, d_wg, d_wu, d_wd


def _col_spec(t, first_block):
    return pl.BlockSpec((t, LANES), lambda j, fb=first_block: (0, j + fb))


def _local_step(h0, tgt, w):
    t = h0.shape[0]
    assert t % CHUNK == 0
    nc = t // CHUNK
    grads = {}

    h1, ffn1_saved = _ffn_fwd(h0, w['ffn1_norm'], w['ffn1_wg'], w['ffn1_wu'], w['ffn1_wd'], "ffn1")
    u = _rms_fwd(h1, w['mix_norm'], "mix_rms")
    z = _matmul(u, w['w_in_p'], name="in_proj")
    zs = z[:, 9216:9216 + 304]
    abeta, aalpha = zs[:, 288:296], zs[:, 296:304]

    conv_w = w['a_conv_w']
    conv_fns = [functools.partial(_f_conv, norm=True, scale=A_DK ** -0.5),
                functools.partial(_f_conv, norm=True, scale=1.0),
                functools.partial(_f_conv, norm=False, scale=1.0)]
    qkv = []
    for idx, fn in enumerate(conv_fns):
        qkv.append(_tw_fwd(fn, [z, conv_w], [_col_spec(t, 8 * idx), pl.BlockSpec((4, LANES), lambda j, o=8 * idx: (0, j + o))],
                           [jax.ShapeDtypeStruct((t, D), f32)], [_col_spec(t, 0)], (A_HEADS,), f"a_conv{idx}")[0])
    aq, ak, av = qkv
    tmg = _tile(t, 1040, 8)
    dg_fn = functools.partial(_f_dgates, tm=tmg)
    dg_specs = [_row_spec(tmg, A_HEADS)] * 2 + [_full_spec((1, A_HEADS))] * 2
    beta, gdec = _tw_fwd(dg_fn, [abeta, aalpha, w['a_log_rate'], w['a_dt_bias']], dg_specs,
                         [jax.ShapeDtypeStruct((t, A_HEADS), f32)] * 2, [_row_spec(tmg, A_HEADS)] * 2, (t // tmg,),
                         "a_gates", with_pid=True)
    beta_h = beta.T[:, :, None]
    gdec_h = gdec.T[:, :, None]
    a_ins = [aq, ak, av, beta_h, gdec_h]
    a_offs = [0] * 5
    o_scan, a_s0 = _scan_fwd(_delta_chunk, a_ins, a_offs, A_HEADS, nc, (A_DK, A_DK), "a_scan")

    mu = w['b_shift_mu']
    mu_rkv, mu_s = mu[:, :3072], mu[:, 3072:]
    zf_rkv = _tw_fwd(_f_tshift, [z, mu_rkv], [_col_spec(t, 32), pl.BlockSpec((1, LANES), lambda j: (0, j))],
                     [jax.ShapeDtypeStruct((t, 3072), f32)], [_col_spec(t, 0)], (24,), "b_shift")[0]
    zs_b = zs[:, :288]
    zf_s = _tw_fwd(_f_tshift, [zs_b, mu_s], [_full_spec((t, 288)), _full_spec((1, 288))],
                   [jax.ShapeDtypeStruct((t, 288), f32)], [_full_spec((t, 288))], (1,), "b_shift_s")[0]
    wdf, adf, gdf = zf_s[:, 0:64], zf_s[:, 64:128], zf_s[:, 128:288]
    tmr = _tile(t, 160, 16)
    pre_params = [w['b_w0'], w['b_w_up'], w['b_a0'], w['b_a_up'], w['b_g_up'], w['b_k_k'], w['b_k_a']]
    pre_ins = [zf_rkv, wdf, adf, gdf] + pre_params
    pre_specs = ([_row_spec(tmr, D, 1), _row_spec(tmr, 64), _row_spec(tmr, 64), _row_spec(tmr, 160)]
                 + [_full_spec(p.shape) for p in pre_params])
    lw, kmod, a_s, b_s, bgate = _tw_fwd(_f_rwkv_pre, pre_ins, pre_specs, [jax.ShapeDtypeStruct((t, D), f32)] * 5,
                                        [_row_spec(tmr, D)] * 5, (t // tmr,), "b_pre")
    b_ins = [zf_rkv, kmod, zf_rkv, a_s, b_s, lw]
    b_offs = [0, 0, 2 * D, 0, 0, 0]
    y_scan, b_s0 = _scan_fwd(_rwkv_chunk, b_ins, b_offs, B_HEADS // 2, nc, (2 * B_N, 2 * B_N), "b_scan")

    out_gain_t = jnp.tile(w['a_out_norm'], (1, A_HEADS))
    r_k = w['b_r_k'].reshape(1, D)
    post_params = [out_gain_t, w['b_ln_gain'], w['b_ln_bias'], r_k]
    post_ins = [o_scan, z, y_scan, zf_rkv, kmod, zf_rkv, bgate, z, z] + post_params
    post_specs = ([_row_spec(tmr, D), _row_spec(tmr, D, 3), _row_spec(tmr, D), _row_spec(tmr, D, 0), _row_spec(tmr, D),
                   _row_spec(tmr, D, 2), _row_spec(tmr, D), _row_spec(tmr, D, 7), _row_spec(tmr, D, 8)]
                  + [_full_spec((1, D))] * 4)
    merged = _tw_fwd(_f_mix_post, post_ins, post_specs, [jax.ShapeDtypeStruct((t, D), MXU_DTYPE)],
                     [_row_spec(tmr, D)], (t // tmr,), "mix_post")[0]
    h2 = _matmul(merged, w['w_out'], res=h1, name="out_proj")
    h3, ffn2_saved = _ffn_fwd(h2, w['ffn2_norm'], w['ffn2_wg'], w['ffn2_wu'], w['ffn2_wd'], "ffn2")

    tml = _tile(t, 416, 8)
    fnorm = w['final_norm']
    loss_fn = functools.partial(_f_loss, tm=tml)
    loss_specs = [_row_spec(tml, D), _full_spec((1, D)), _row_spec(tml, D)]
    loss_parts = _tw_fwd(loss_fn, [h3, fnorm, tgt], loss_specs, [jax.ShapeDtypeStruct((t // tml, 1, 1), f32)],
                         [pl.BlockSpec((None, 1, 1), lambda i: (i, 0, 0))], (t // tml,), "loss", with_pid=True)[0]
    loss = jnp.sum(loss_parts)
    ones = jnp.ones((t // tml, 1, 1), f32)
    d_h3, grads['final_norm'] = _tw_bwd(loss_fn, [h3, fnorm, tgt], loss_specs, [ones],
                                        [pl.BlockSpec((None, 1, 1), lambda i: (i, 0, 0))], ['tile', 'acc', None],
                                        (t // tml,), "loss_bwd", with_pid=True)

    d_h2, grads['ffn2_norm'], grads['ffn2_wg'], grads['ffn2_wu'], grads['ffn2_wd'] = _ffn_bwd(
        h2, w['ffn2_norm'], w['ffn2_wg'], w['ffn2_wu'], w['ffn2_wd'], ffn2_saved, d_h3, "ffn2")
    grads['w_out'] = _matmul(merged, d_h2, ta=True, name="d_w_out")
    d_merged = _matmul(d_h2, w['w_out'], tb=True, name="d_merged")

    win = ('tile', (t, D), _row_spec(tmr, D))
    post_kinds = ['tile', win, 'tile', win, 'tile', win, 'tile', win, win] + ['acc'] * 4
    (d_o, d_az, d_y, d_r1, d_kmod1, d_v1, d_bgate, d_ga, d_gb,
     d_out_gain_t, grads['b_ln_gain'], grads['b_ln_bias'], d_r_k) = _tw_bwd(
        _f_mix_post, post_ins, post_specs, [d_merged], [_row_spec(tmr, D)], post_kinds, (t // tmr,), "mix_post_bwd")
    grads['a_out_norm'] = jnp.sum(d_out_gain_t.reshape(A_HEADS, A_DK), axis=0, keepdims=True)
    grads['b_r_k'] = d_r_k.reshape(1, B_HEADS, B_N)

    d_r2, d_kmod2, d_v2, d_as, d_bs, d_lw = _scan_bwd(_rwkv_chunk, b_s0, b_ins, b_offs, d_y, B_HEADS // 2, nc,
                                                      (2 * B_N, 2 * B_N), "b_scan_bwd")
    pre_kinds = [win] + ['tile'] * 3 + ['acc'] * 7
    pre_ct_specs = [_row_spec(tmr, D)] * 5
    (d_zf_k, d_wdf, d_adf, d_gdf, grads['b_w0'], grads['b_w_up'], grads['b_a0'], grads['b_a_up'], grads['b_g_up'],
     grads['b_k_k'], grads['b_k_a']) = _tw_bwd(
        _f_rwkv_pre, pre_ins, pre_specs, [d_lw, d_kmod1, d_as, d_bs, d_bgate], pre_ct_specs, pre_kinds, (t // tmr,),
        "b_pre_bwd", ct_extra=[(1, d_kmod2)])
    d_zf_rkv = _assemble3(d_r1, d_r2, d_zf_k, d_v1, d_v2, "b_dzf")
    d_zb_rkv, d_mu_rkv = _tw_bwd(_f_tshift, [z, mu_rkv], [_col_spec(t, 32), pl.BlockSpec((1, LANES), lambda j: (0, j))],
                                 [d_zf_rkv], [_col_spec(t, 0)], [('tile', (t, 3072), _col_spec(t, 0)), 'tile'], (24,),
                                 "b_shift_bwd")
    d_zf_s = jnp.concatenate([d_wdf, d_adf, d_gdf], axis=1)
    d_zs_b, d_mu_s = _tw_bwd(_f_tshift, [zs_b, mu_s], [_full_spec((t, 288)), _full_spec((1, 288))], [d_zf_s],
                             [_full_spec((t, 288))], ['tile', 'tile'], (1,), "b_shift_s_bwd")
    grads['b_shift_mu'] = jnp.concatenate([d_mu_rkv, d_mu_s], axis=1)

    d_aq, d_ak, d_av, d_beta_h, d_g_h = _scan_bwd(_delta_chunk, a_s0, a_ins, a_offs, d_o, A_HEADS, nc, (A_DK, A_DK),
                                                  "a_scan_bwd")
    d_beta = d_beta_h[:, :, 0].T
    d_gdec = d_g_h[:, :, 0].T
    d_abeta, d_aalpha, grads['a_log_rate'], grads['a_dt_bias'] = _tw_bwd(
        dg_fn, [abeta, aalpha, w['a_log_rate'], w['a_dt_bias']], dg_specs, [d_beta, d_gdec],
        [_row_spec(tmg, A_HEADS)] * 2, ['tile', 'tile', 'acc', 'acc'], (t // tmg,), "a_gates_bwd", with_pid=True)
    d_zqkv, d_conv = [], []
    for idx, (fn, ct) in enumerate(zip(conv_fns, (d_aq, d_ak, d_av))):
        dz_i, dw_i = _conv_bwd(fn, z, conv_w, ct, idx, t)
        d_zqkv.append(dz_i)
        d_conv.append(dw_i)
    grads['a_conv_w'] = jnp.concatenate(d_conv, axis=1)

    d_z_parts = d_zqkv + [d_az, d_zb_rkv, d_ga, d_gb, d_zs_b, d_abeta, d_aalpha, jnp.zeros((t, ZP - 9216 - 304), f32)]
    d_z = jnp.concatenate([p.astype(MXU_DTYPE) for p in d_z_parts], axis=1)
    grads['w_in_p'] = _matmul(u, d_z, ta=True, name="d_w_in")
    d_u = _matmul(d_z, w['w_in_p'], tb=True, name="d_u")
    d_h1, grads['mix_norm'] = _rms_bwd(h1, w['mix_norm'], d_u, d_h2, "mix_drms")
    d_h0, grads['ffn1_norm'], grads['ffn1_wg'], grads['ffn1_wu'], grads['ffn1_wd'] = _ffn_bwd(
        h0, w['ffn1_norm'], w['ffn1_wg'], w['ffn1_wu'], w['ffn1_wd'], ffn1_saved, d_h1, "ffn1")
    return loss, d_h0, grads


_WIN_SEGMENTS = ((0, 4096), (4112, 7184), (7472, 9520), (7184, 7472), (4096, 4112))


def _win_to_padded(w_in):
    parts = [w_in[:, a:b] for a, b in _WIN_SEGMENTS]
    parts.append(jnp.zeros((w_in.shape[0], ZP - IN_TOTAL), w_in.dtype))
    return jnp.concatenate(parts, axis=1)


def _win_from_padded(w_p):
    widths = [b - a for a, b in _WIN_SEGMENTS]
    offs = [sum(widths[:i]) for i in range(len(widths))]
    seg = {a: w_p[:, o:o + wd] for (a, _), o, wd in zip(_WIN_SEGMENTS, offs, widths)}
    return jnp.concatenate([seg[a] for a in sorted(seg)], axis=1)


def _assemble3(d_r1, d_r2, d_k, d_v1, d_v2, name):
    t = d_r1.shape[0]
    tm = _tile(t, 208, 8)

    def body(r1, r2, kk, v1, v2, o_ref):
        o_ref[:, 0:D] = r1[...] + r2[...]
        o_ref[:, D:2 * D] = kk[...]
        o_ref[:, 2 * D:3 * D] = v1[...] + v2[...]

    return pl.pallas_call(body, name=name, grid=(t // tm,), in_specs=[_row_spec(tm, D)] * 5,
                          out_specs=_row_spec(tm, 3 * D), out_shape=jax.ShapeDtypeStruct((t, 3 * D), f32),
                          )(d_r1, d_r2, d_k, d_v1, d_v2)


def _conv_bwd(fn, z, conv_w, ct, idx, t):
    def body(z_ref, w_ref, ct_ref, dz_ref, dw_ref):
        _, vjp = jax.vjp(lambda a, b: fn(a, b), z_ref[...], w_ref[...])
        dz, dw = vjp((ct_ref[...],))
        dz_ref[...] = dz
        dw_ref[...] = dw

    return pl.pallas_call(
        body, name=f"a_conv{idx}_bwd", grid=(A_HEADS,),
        in_specs=[_col_spec(t, 8 * idx), pl.BlockSpec((4, LANES), lambda j, o=8 * idx: (0, j + o)), _col_spec(t, 0)],
        out_specs=[_col_spec(t, 0), pl.BlockSpec((4, LANES), lambda j: (0, j))],
        out_shape=[jax.ShapeDtypeStruct((t, D), f32), jax.ShapeDtypeStruct((4, D), f32)],
    )(z, conv_w, ct)


def _position():
    return lax.axis_index("x"), lax.axis_index("y"), lax.axis_index("c")


def _flip(v, f):
    return 1 - v if f else v


_CHIP_FLIPS = ((1, 0), (0, 1), (1, 1))
_DEV_FLIPS = tuple((fx, fy, fc) for fx in (0, 1) for fy in (0, 1) for fc in (0, 1) if (fx, fy, fc) != (0, 0, 0))


def _gather_chips(arrs, name):
    n = len(arrs)
    assert all(a.shape[0] % 32 == 0 for a in arrs)
    arrs = [a.reshape(2, a.shape[0] // 2, a.shape[1]) for a in arrs]

    def body(*refs):
        ins, outs = refs[:n], refs[n:2 * n]
        send, recv, fsend, frecv, own = refs[2 * n:]
        x, y, c = _position()
        me = 2 * x + y
        sends, plan, owns = [], [], []
        for a in range(n):
            cp = pltpu.make_async_remote_copy(src_ref=ins[a], dst_ref=outs[a].at[me], send_sem=own.at[a, 0],
                                              recv_sem=own.at[a, 1], device_id=(x, y, 1 - c), device_id_type=MESH)
            cp.start()
            owns.append(cp)
            for j, (fx, fy) in enumerate(_CHIP_FLIPS):
                px, py = _flip(x, fx), _flip(y, fy)
                p = 2 * px + py
                cp = pltpu.make_async_remote_copy(src_ref=ins[a].at[c], dst_ref=outs[a].at[me, c],
                                                  send_sem=send.at[a, j], recv_sem=recv.at[a, j],
                                                  device_id=(px, py, c), device_id_type=MESH)
                cp.start()
                sends.append(cp)
                landed = pltpu.make_async_remote_copy(src_ref=ins[a].at[c], dst_ref=outs[a].at[p, c],
                                                      send_sem=send.at[a, j], recv_sem=recv.at[a, j],
                                                      device_id=(px, py, c), device_id_type=MESH)
                onward = pltpu.make_async_remote_copy(src_ref=outs[a].at[p, c], dst_ref=outs[a].at[p, c],
                                                      send_sem=fsend.at[a, j], recv_sem=frecv.at[a, j],
                                                      device_id=(x, y, 1 - c), device_id_type=MESH)
                from_sibling = pltpu.make_async_remote_copy(src_ref=outs[a].at[p, 1 - c], dst_ref=outs[a].at[p, 1 - c],
                                                            send_sem=fsend.at[a, j], recv_sem=frecv.at[a, j],
                                                            device_id=(x, y, 1 - c), device_id_type=MESH)
                plan.append((landed, onward, from_sibling))
        for landed, onward, _ in plan:
            landed.wait_recv()
            onward.start()
        for _, _, from_sibling in plan:
            from_sibling.wait_recv()
        for cp in sends:
            cp.wait_send()
        for _, onward, _ in plan:
            onward.wait_send()
        for cp in owns:
            cp.wait()

    sems = [pltpu.SemaphoreType.DMA((n, 3))] * 4 + [pltpu.SemaphoreType.DMA((n, 2))]
    outs = pl.pallas_call(
        body, name=name, in_specs=[ANY] * n, out_specs=[ANY] * n,
        out_shape=[jax.ShapeDtypeStruct((N_CHIPS,) + a.shape, a.dtype) for a in arrs], scratch_shapes=sems,
    )(*arrs)
    return [o.reshape(N_CHIPS, o.shape[1] * o.shape[2], o.shape[3]) for o in outs]


def _swap_sibling(src_of, shape, dtype, name):
    def body(a_ref, got_ref, send, recv):
        x, y, c = _position()
        cp = pltpu.make_async_remote_copy(src_ref=src_of(a_ref, c), dst_ref=got_ref, send_sem=send, recv_sem=recv,
                                          device_id=(x, y, 1 - c), device_id_type=MESH)
        cp.start()
        cp.wait()

    def call(a):
        return pl.pallas_call(body, name=name, in_specs=[ANY], out_specs=ANY,
                              out_shape=jax.ShapeDtypeStruct(shape, dtype),
                              scratch_shapes=[pltpu.SemaphoreType.DMA(())] * 2)(a)
    return call


def _add_halves(g, got, dtype, name):
    _, n, hr, w = g.shape
    tr = _tile(hr, 784, 16)

    def body(g_ref, got_ref, o_ref):
        c = lax.axis_index("c")
        own = jnp.where(c == 0, g_ref[0], g_ref[1])
        o_ref[...] = (own + got_ref[...]).astype(dtype)

    return pl.pallas_call(
        body, name=name, grid=(hr // tr,),
        in_specs=[pl.BlockSpec((2, n, tr, w), lambda i: (0, 0, i, 0)), pl.BlockSpec((n, tr, w), lambda i: (0, i, 0))],
        out_specs=pl.BlockSpec((n, tr, w), lambda i: (0, i, 0)),
        out_shape=jax.ShapeDtypeStruct((n, hr, w), dtype))(g, got)


def _scatter_chips(g, name):
    def body(g_ref, out_ref, send, recv):
        x, y, c = _position()
        sends = []
        for j, (fx, fy) in enumerate(_CHIP_FLIPS):
            px, py = _flip(x, fx), _flip(y, fy)
            cp = pltpu.make_async_remote_copy(src_ref=g_ref.at[2 * px + py], dst_ref=out_ref.at[j], send_sem=send.at[j],
                                              recv_sem=recv.at[j], device_id=(px, py, c), device_id_type=MESH)
            cp.start()
            sends.append(cp)
        for cp in sends:
            cp.wait_recv()
        for cp in sends:
            cp.wait_send()

    return pl.pallas_call(
        body, name=name, in_specs=[ANY], out_specs=ANY, out_shape=jax.ShapeDtypeStruct((3,) + g.shape[1:], g.dtype),
        scratch_shapes=[pltpu.SemaphoreType.DMA((3,)), pltpu.SemaphoreType.DMA((3,))],
    )(g)


def _sum_own_and_slots(own, got, name):
    n, r, w = own.shape
    tr = _tile(r, 784, 16)

    def body(own_ref, got_ref, o_ref):
        me = 2 * lax.axis_index("x") + lax.axis_index("y")
        acc = own_ref[0]
        for i in range(1, n):
            acc = jnp.where(me == i, own_ref[i], acc)
        acc = acc.astype(f32)
        for j in range(3):
            acc = acc + got_ref[j].astype(f32)
        o_ref[...] = acc

    return pl.pallas_call(
        body, name=name, grid=(r // tr,),
        in_specs=[pl.BlockSpec((n, tr, w), lambda i: (0, i, 0)), pl.BlockSpec((3, tr, w), lambda i: (0, i, 0))],
        out_specs=pl.BlockSpec((tr, w), lambda i: (i, 0)), out_shape=jax.ShapeDtypeStruct((r, w), f32))(own, got)


def _gather_devices(s, name):
    def body(s_ref, out_ref, send, recv, loc):
        x, y, c = _position()
        me = 4 * x + 2 * y + c
        lc = pltpu.make_async_copy(s_ref, out_ref.at[me], loc)
        lc.start()
        sends, recvs = [], []
        for j, (fx, fy, fc) in enumerate(_DEV_FLIPS):
            px, py, pc = _flip(x, fx), _flip(y, fy), _flip(c, fc)
            cp = pltpu.make_async_remote_copy(src_ref=s_ref, dst_ref=out_ref.at[me], send_sem=send.at[j],
                                              recv_sem=recv.at[j], device_id=(px, py, pc), device_id_type=MESH)
            cp.start()
            sends.append(cp)
            recvs.append(pltpu.make_async_remote_copy(
                src_ref=s_ref, dst_ref=out_ref.at[4 * px + 2 * py + pc], send_sem=send.at[j], recv_sem=recv.at[j],
                device_id=(px, py, pc), device_id_type=MESH))
        for cp in recvs:
            cp.wait_recv()
        for cp in sends:
            cp.wait_send()
        lc.wait()

    return pl.pallas_call(
        body, name=name, in_specs=[ANY], out_specs=ANY, out_shape=jax.ShapeDtypeStruct((N_DEV,) + s.shape, s.dtype),
        scratch_shapes=[pltpu.SemaphoreType.DMA((7,)), pltpu.SemaphoreType.DMA((7,)), pltpu.SemaphoreType.DMA(())],
    )(s)


def _sum_slots(a, name):
    s, r, c = a.shape
    tr = _tile(r, 2048, 16)

    def body(a_ref, o_ref):
        acc = a_ref[0].astype(f32)
        for i in range(1, s):
            acc = acc + a_ref[i].astype(f32)
        o_ref[...] = acc

    return pl.pallas_call(body, name=name, grid=(r // tr,), in_specs=[pl.BlockSpec((s, tr, c), lambda i: (0, i, 0))],
                          out_specs=pl.BlockSpec((tr, c), lambda i: (i, 0)),
                          out_shape=jax.ShapeDtypeStruct((r, c), f32))(a)


def _adamw(w, g_parts, m, v, name):
    shape = w.shape
    size = w.size
    view = (size // LANES, LANES) if size % LANES == 0 else (1, size)
    rows = view[0]
    tr = _tile(rows, 2048, 8) if rows > 2048 else rows
    n_g = len(g_parts)

    def body(*refs):
        w_ref = refs[0]
        g_refs = refs[1:1 + n_g]
        m_ref, v_ref, g_out, d_out, m_out, v_out = refs[1 + n_g:]
        g = g_refs[0][...]
        for gr in g_refs[1:]:
            g = g + gr[...]
        m_new = ADAM_B1 * m_ref[...] + (1.0 - ADAM_B1) * g
        v_new = ADAM_B2 * v_ref[...] + (1.0 - ADAM_B2) * (g * g)
        m_hat = m_new / (1.0 - ADAM_B1 ** ADAM_STEP)
        v_hat = v_new / (1.0 - ADAM_B2 ** ADAM_STEP)
        g_out[...] = g
        d_out[...] = -ADAM_LR * (m_hat / (jnp.sqrt(v_hat) + ADAM_EPS) + ADAM_WD * w_ref[...])
        m_out[...] = m_new
        v_out[...] = v_new

    spec = pl.BlockSpec((tr, view[1]), lambda i: (i, 0))
    args = [w.reshape(view)] + [g.reshape(view) for g in g_parts] + [m.reshape(view), v.reshape(view)]
    outs = pl.pallas_call(body, name=name, grid=(rows // tr,), in_specs=[spec] * len(args), out_specs=[spec] * 4,
                          out_shape=[jax.ShapeDtypeStruct(view, f32)] * 4)(*args)
    return [o.reshape(shape) for o in outs]


_BIG = ('ffn1_w_gu', 'ffn1_w_down', 'w_in', 'w_out', 'ffn2_w_gu', 'ffn2_w_down')
_SMALL_SHARDED = ('meta_tokens', 'a_conv_w', 'b_w_up', 'b_a_up', 'b_g_up')
_WEIGHTS = ('meta_tokens', 'ffn1_norm', 'ffn1_w_gu', 'ffn1_w_down', 'mix_norm', 'w_in', 'a_conv_w', 'a_log_rate',
            'a_dt_bias', 'a_out_norm', 'b_shift_mu', 'b_w0', 'b_w_up', 'b_a0', 'b_a_up', 'b_g_up', 'b_k_k', 'b_k_a',
            'b_r_k', 'b_ln_gain', 'b_ln_bias', 'w_out', 'ffn2_norm', 'ffn2_w_gu', 'ffn2_w_down', 'final_norm')
_SMALL = tuple(n for n in _WEIGHTS if n not in _BIG)


def _rows_of(shape):
    n = 1
    for d in shape:
        n *= d
    return n, -(-n // LANES)


def _pack(arrs, dtype, row_mult=32):
    parts, total = [], 0
    for a in arrs:
        n, rows = _rows_of(a.shape)
        flat = a.reshape(-1).astype(dtype)
        if n % LANES:
            flat = jnp.pad(flat, (0, rows * LANES - n))
        parts.append(flat)
        total += rows
    extra = -total % row_mult
    if extra:
        parts.append(jnp.zeros((extra * LANES,), dtype))
    return jnp.concatenate(parts).reshape(total + extra, LANES)


def _unpack(packed, shapes, lead=()):
    out, off = [], 0
    for sh in shapes:
        n, rows = _rows_of(sh)
        seg = packed[..., off:off + rows, :]
        if n % LANES:
            seg = seg.reshape(lead + (-1,))[..., :n]
        out.append(seg.reshape(lead + tuple(sh)))
        off += rows
    return out


def _cols_from_shards(s):
    return jnp.concatenate([s[i] for i in range(N_CHIPS)], axis=-1)


def _cols_to_shards(a):
    r, c = a.shape
    return a.reshape(r, N_CHIPS, c // N_CHIPS).transpose(1, 0, 2)


def kernel(x, meta_tokens, ffn1_norm, ffn1_w_gu, ffn1_w_down, mix_norm, w_in, a_conv_w, a_log_rate, a_dt_bias, a_out_norm, b_shift_mu, b_w0, b_w_up, b_a0, b_a_up, b_g_up, b_k_k, b_k_a, b_r_k, b_ln_gain, b_ln_bias, w_out, ffn2_norm, ffn2_w_gu, ffn2_w_down, final_norm, loss_target, m_meta_tokens, m_ffn1_norm, m_ffn1_w_gu, m_ffn1_w_down, m_mix_norm, m_w_in, m_a_conv_w, m_a_log_rate, m_a_dt_bias, m_a_out_norm, m_b_shift_mu, m_b_w0, m_b_w_up, m_b_a0, m_b_a_up, m_b_g_up, m_b_k_k, m_b_k_a, m_b_r_k, m_b_ln_gain, m_b_ln_bias, m_w_out, m_ffn2_norm, m_ffn2_w_gu, m_ffn2_w_down, m_final_norm, v_meta_tokens, v_ffn1_norm, v_ffn1_w_gu, v_ffn1_w_down, v_mix_norm, v_w_in, v_a_conv_w, v_a_log_rate, v_a_dt_bias, v_a_out_norm, v_b_shift_mu, v_b_w0, v_b_w_up, v_b_a0, v_b_a_up, v_b_g_up, v_b_k_k, v_b_k_a, v_b_r_k, v_b_ln_gain, v_b_ln_bias, v_w_out, v_ffn2_norm, v_ffn2_w_gu, v_ffn2_w_down, v_final_norm):
    args = locals()
    wts = {n: args[n] for n in _WEIGHTS}
    mom = {n: args["m_" + n] for n in _WEIGHTS}
    var = {n: args["v_" + n] for n in _WEIGHTS}
    chip = 2 * lax.axis_index("x") + lax.axis_index("y")

    big_shapes = [wts[n].shape[1:] for n in _BIG]
    small_shapes = [wts[n].shape[-2:] for n in _SMALL_SHARDED]
    big_packed = _pack([wts[n] for n in _BIG], bf16)
    small_packed = _pack([wts[n] for n in _SMALL_SHARDED], f32)
    big_all, small_all = _gather_chips([big_packed, small_packed], "gather_weights")
    gu1, dn1, w_in_s, w_out_s, gu2, dn2 = _unpack(big_all, big_shapes, (N_CHIPS,))
    meta_s, conv_s, wup_s, aup_s, gup_s = _unpack(small_all, small_shapes, (N_CHIPS,))
    w = {
        'ffn1_norm': ffn1_norm, 'mix_norm': mix_norm, 'ffn2_norm': ffn2_norm, 'final_norm': final_norm[None, :],
        'ffn1_wg': jnp.concatenate([gu1[0], gu1[1]], axis=1), 'ffn1_wu': jnp.concatenate([gu1[2], gu1[3]], axis=1),
        'ffn1_wd': dn1.reshape(D_FF, D),
        'ffn2_wg': jnp.concatenate([gu2[0], gu2[1]], axis=1), 'ffn2_wu': jnp.concatenate([gu2[2], gu2[3]], axis=1),
        'ffn2_wd': dn2.reshape(D_FF, D),
        'w_in_p': _win_to_padded(_cols_from_shards(w_in_s)), 'w_out': w_out_s.reshape(D, D),
        'a_conv_w': _cols_from_shards(conv_s), 'b_w_up': _cols_from_shards(wup_s), 'b_a_up': _cols_from_shards(aup_s),
        'b_g_up': _cols_from_shards(gup_s),
        'a_log_rate': a_log_rate, 'a_dt_bias': a_dt_bias, 'a_out_norm': a_out_norm, 'b_shift_mu': b_shift_mu,
        'b_w0': b_w0, 'b_a0': b_a0, 'b_k_k': b_k_k, 'b_k_a': b_k_a, 'b_r_k': b_r_k, 'b_ln_gain': b_ln_gain,
        'b_ln_bias': b_ln_bias,
    }
    meta_full = _cols_from_shards(meta_s)

    h0 = jnp.concatenate([jnp.zeros((PAD, D), f32), meta_full, x[0]], axis=0)
    tgt = jnp.concatenate([jnp.zeros((SKIP, D), f32), loss_target[0]], axis=0)
    loss_local, d_h0, g = _local_step(h0, tgt, w)
    loss = lax.psum(loss_local, ("x", "y", "c"))
    grad_x = d_h0[SKIP:][None]

    big_grads = [
        _cols_to_shards(jnp.concatenate([g['ffn1_wg'], g['ffn1_wu']], axis=1)),
        g['ffn1_wd'].reshape(N_CHIPS, D_FF // N_CHIPS, D),
        _cols_to_shards(_win_from_padded(g['w_in_p'])),
        g['w_out'].reshape(N_CHIPS, D // N_CHIPS, D),
        _cols_to_shards(jnp.concatenate([g['ffn2_wg'], g['ffn2_wu']], axis=1)),
        g['ffn2_wd'].reshape(N_CHIPS, D_FF // N_CHIPS, D),
    ]
    g_packed = jnp.concatenate([a.reshape(N_CHIPS, -1, LANES) for a in big_grads], axis=1)
    assert g_packed.shape[1] == big_packed.shape[0]
    rows = g_packed.shape[1]
    g_halves = g_packed.reshape(N_CHIPS, 2, rows // 2, LANES).transpose(1, 0, 2, 3)
    half_shape = (N_CHIPS, rows // 2, LANES)
    sib_half = _swap_sibling(lambda ref, c: ref.at[1 - c], half_shape, f32, "swap_halves")(g_halves)
    chip_half = _add_halves(g_halves, sib_half, bf16, "add_sibling")
    mine = _sum_own_and_slots(chip_half, _scatter_chips(chip_half, "scatter_grads"), "sum_chips")
    theirs = _swap_sibling(lambda ref, c: ref, mine.shape, f32, "swap_sums")(mine)
    core = lax.axis_index("c")
    summed = jnp.concatenate([jnp.where(core == 0, mine, theirs), jnp.where(core == 0, theirs, mine)], axis=0)
    big_parts = _unpack(summed, big_shapes)

    small_full = {
        'meta_tokens': d_h0[PAD:SKIP], 'ffn1_norm': g['ffn1_norm'], 'mix_norm': g['mix_norm'], 'a_conv_w': g['a_conv_w'],
        'a_log_rate': g['a_log_rate'], 'a_dt_bias': g['a_dt_bias'], 'a_out_norm': g['a_out_norm'],
        'b_shift_mu': g['b_shift_mu'], 'b_w0': g['b_w0'], 'b_w_up': g['b_w_up'], 'b_a0': g['b_a0'], 'b_a_up': g['b_a_up'],
        'b_g_up': g['b_g_up'], 'b_k_k': g['b_k_k'], 'b_k_a': g['b_k_a'], 'b_r_k': g['b_r_k'], 'b_ln_gain': g['b_ln_gain'],
        'b_ln_bias': g['b_ln_bias'], 'ffn2_norm': g['ffn2_norm'], 'final_norm': g['final_norm'],
    }
    s_shapes = [small_full[n].shape for n in _SMALL]
    s_sum = _sum_slots(_gather_devices(_pack([small_full[n] for n in _SMALL], f32, row_mult=256), "gather_small"),
                       "sum_small")
    s_parts = dict(zip(_SMALL, _unpack(s_sum, s_shapes)))

    grad, delta, new_m, new_v = {}, {}, {}, {}
    for n, a in zip(_BIG, big_parts):
        grad[n], delta[n], new_m[n], new_v[n] = _adamw(wts[n], [a.reshape(wts[n].shape)], mom[n], var[n], f"adamw_{n}")
    for n in _SMALL:
        gs = s_parts[n]
        if n in _SMALL_SHARDED:
            width = wts[n].shape[-1]
            gs = lax.dynamic_slice_in_dim(gs, chip * width, width, axis=gs.ndim - 1)
        gs = gs.reshape(wts[n].shape)
        grad[n], delta[n], new_m[n], new_v[n] = _adamw(wts[n], [gs], mom[n], var[n], f"adamw_{n}")

    return (loss, grad_x, *[grad[n] for n in _WEIGHTS], *[delta[n] for n in _WEIGHTS],
            *[new_m[n] for n in _WEIGHTS], *[new_v[n] for n in _WEIGHTS])
```

```python
import functools

import jax
import jax.numpy as jnp
from jax import lax
from jax.experimental import pallas as pl
from jax.experimental.pallas import tpu as pltpu

f32 = jnp.float32
bf16 = jnp.bfloat16
HI = lax.Precision.HIGHEST
MESH = pl.DeviceIdType.MESH
ANY = pl.BlockSpec(memory_space=pl.ANY)

D = 1024
N_META = 16
CHUNK = 64
PAD = CHUNK - N_META
SKIP = PAD + N_META
EPS = 1e-6
D_FF = 2816
A_HEADS = 8
A_DK = 128
B_HEADS = 16
B_N = 64
B_GN_EPS = B_N * 1e-5
W_LORA, AA_LORA, G_LORA = 64, 64, 160
IN_TOTAL = 9520
ZP = 9600
LANES = 128
N_CHIPS = 4
N_DEV = 8

ADAM_LR, ADAM_B1, ADAM_B2, ADAM_EPS, ADAM_WD, ADAM_STEP = 0.001, 0.9, 0.999, 1e-08, 0.01, 10

MXU_DTYPE = bf16


def _tile(n, cap, mult):
    if n <= cap:
        return n
    best = None
    for t in range(mult, cap + 1, mult):
        if n % t == 0:
            best = t
    assert best is not None, (n, cap, mult)
    return best


def _sigmoid(x):
    return jax.nn.sigmoid(x)


def _silu(x):
    return x * jax.nn.sigmoid(x)


def _softplus(x):
    return jnp.maximum(x, 0.0) + jnp.log(1.0 + jnp.exp(-jnp.abs(x)))


def _head_matrix(c, nh):
    hd = c // nh
    r = lax.broadcasted_iota(jnp.int32, (c, nh), 0)
    h = lax.broadcasted_iota(jnp.int32, (c, nh), 1)
    return (r >= h * hd) & (r < (h + 1) * hd)


def _dot_exact_rhs(x, e, cb):
    dn = (((1,), (cb,)), ((), ()))
    if SCAN_PASSES == 0:
        return lax.dot_general(x, e.astype(f32), dn, precision=HI, preferred_element_type=f32)
    eb = e.astype(bf16)
    hi = x.astype(bf16)
    lo = (x - hi.astype(f32)).astype(bf16)
    return (lax.dot_general(hi, eb, dn, preferred_element_type=f32)
            + lax.dot_general(lo, eb, dn, preferred_element_type=f32))


def _head_sum_impl(x, nh):
    e = _head_matrix(x.shape[-1], nh)
    return _dot_exact_rhs(_dot_exact_rhs(x, e, 0), e, 1)


@functools.partial(jax.custom_vjp, nondiff_argnums=(1,))
def _head_sum(x, nh):
    return _head_sum_impl(x, nh)


def _head_sum_fwd(x, nh):
    return _head_sum_impl(x, nh), None


def _head_sum_bwd(nh, _, g):
    return (_head_sum_impl(g, nh),)


_head_sum.defvjp(_head_sum_fwd, _head_sum_bwd)


@functools.partial(jax.custom_vjp, nondiff_argnums=(1,))
def _shift_rows(x, s):
    n = x.shape[0]
    row = lax.broadcasted_iota(jnp.int32, x.shape, 0)
    if s > 0:
        return jnp.where(row >= s, pltpu.roll(x, s, 0), 0.0)
    return jnp.where(row < n + s, pltpu.roll(x, n + s, 0), 0.0)


def _shift_rows_fwd(x, s):
    return _shift_rows(x, s), None


def _shift_rows_bwd(s, _, g):
    return (_shift_rows(g, -s),)


_shift_rows.defvjp(_shift_rows_fwd, _shift_rows_bwd)


def _matmul(a, b, *, ta=False, tb=False, res=None, scale=1.0, name):
    assert not (ta and tb)
    (ar, ac), (br, bc) = a.shape, b.shape
    m, k = (ac, ar) if ta else (ar, ac)
    n, kb = (br, bc) if tb else (bc, br)
    assert k == kb, (a.shape, b.shape, ta, tb)
    tm = _tile(m, 1408, LANES) if ta else _tile(m, 832, 8)
    tn = _tile(n, 1408, LANES)
    tk = _tile(k, 1040, 8) if ta else _tile(k, 1408, LANES)
    nk = k // tk
    dn = (((0 if ta else 1,), (1 if tb else 0,)), ((), ()))

    def body(*refs):
        if res is not None:
            a_ref, b_ref, r_ref, o_ref, acc = refs
        else:
            a_ref, b_ref, o_ref, acc = refs
        kk = pl.program_id(2)

        @pl.when(kk == 0)
        def _():
            acc[...] = jnp.zeros_like(acc)

        acc[...] += lax.dot_general(a_ref[...].astype(MXU_DTYPE), b_ref[...].astype(MXU_DTYPE), dn,
                                    preferred_element_type=f32,
                                    precision=None if MXU_DTYPE == bf16 else HI)

        @pl.when(kk == nk - 1)
        def _():
            out = acc[...]
            if scale != 1.0:
                out = out * scale
            if res is not None:
                out = r_ref[...] + out
            o_ref[...] = out

    if ta:
        a_spec = pl.BlockSpec((tk, tm), lambda i, j, kk: (kk, i))
    else:
        a_spec = pl.BlockSpec((tm, tk), lambda i, j, kk: (i, kk))
    if tb:
        b_spec = pl.BlockSpec((tn, tk), lambda i, j, kk: (j, kk))
    else:
        b_spec = pl.BlockSpec((tk, tn), lambda i, j, kk: (kk, j))
    in_specs = [a_spec, b_spec]
    args = [a, b]
    if res is not None:
        in_specs.append(pl.BlockSpec((tm, tn), lambda i, j, kk: (i, j)))
        args.append(res)
    return pl.pallas_call(
        body, name=name, grid=(m // tm, n // tn, nk), in_specs=in_specs,
        out_specs=pl.BlockSpec((tm, tn), lambda i, j, kk: (i, j)),
        out_shape=jax.ShapeDtypeStruct((m, n), f32),
        scratch_shapes=[pltpu.VMEM((tm, tn), f32)],
        compiler_params=pltpu.CompilerParams(dimension_semantics=("parallel", "parallel", "arbitrary")),
    )(*args)


def _tw_fwd(fn, ins, in_specs, out_shapes, out_specs, grid, name, with_pid=False):
    n_in = len(ins)

    def body(*refs):
        vals = [r[...] for r in refs[:n_in]]
        outs = fn(pl.program_id(0), *vals) if with_pid else fn(*vals)
        for r, o in zip(refs[n_in:], outs):
            r[...] = o.astype(r.dtype)

    return pl.pallas_call(body, name=name, grid=grid, in_specs=in_specs, out_specs=out_specs,
                          out_shape=out_shapes)(*ins)


def _tw_bwd(fn, ins, in_specs, cts, ct_specs, kinds, grid, name, with_pid=False, tile_dtype=f32, ct_extra=(),
            residual=None):
    n_in, n_ct = len(ins), len(cts)
    diff = [i for i, kd in enumerate(kinds) if kd is not None]
    n_ex = len(ct_extra)

    def body(*refs):
        vals = [r[...] for r in refs[:n_in]]
        ctv = [r[...].astype(f32) for r in refs[n_in:n_in + n_ct]]
        for (ci, _), r in zip(ct_extra, refs[n_in + n_ct:n_in + n_ct + n_ex]):
            ctv[ci] = ctv[ci] + r[...]
        ctv = tuple(ctv)
        n_fixed = n_in + n_ct + n_ex
        res_ref = refs[n_fixed] if residual is not None else None
        g_refs = refs[n_fixed + (residual is not None):]
        pid = pl.program_id(0)

        def f(*dv):
            full = list(vals)
            for i, v in zip(diff, dv):
                full[i] = v
            out = fn(pid, *full) if with_pid else fn(*full)
            return tuple(out)

        _, vjp = jax.vjp(f, *[vals[i] for i in diff])
        gs = vjp(ctv)
        first = pid == 0
        for i2 in range(1, len(grid)):
            first = first & (pl.program_id(i2) == 0)
        for i, g, g_ref in zip(diff, gs, g_refs):
            if kinds[i] != 'acc':
                if i == 0 and res_ref is not None:
                    g = res_ref[...] + g
                g_ref[...] = g.astype(g_ref.dtype)
            else:
                @pl.when(first)
                def _(g=g, g_ref=g_ref):
                    g_ref[...] = g

                @pl.when(jnp.logical_not(first))
                def _(g=g, g_ref=g_ref):
                    g_ref[...] += g

    zero_map = {1: lambda *a: (0,), 2: lambda *a: (0, 0), 3: lambda *a: (0, 0, 0)}
    out_specs, out_shapes = [], []
    for i in diff:
        if kinds[i] == 'tile':
            out_shapes.append(jax.ShapeDtypeStruct(ins[i].shape, tile_dtype))
            out_specs.append(in_specs[i])
        elif kinds[i] == 'acc':
            out_shapes.append(jax.ShapeDtypeStruct(ins[i].shape, f32))
            out_specs.append(pl.BlockSpec(ins[i].shape, zero_map[ins[i].ndim]))
        else:
            out_shapes.append(jax.ShapeDtypeStruct(kinds[i][1], tile_dtype))
            out_specs.append(kinds[i][2])
    extra_specs = [ct_specs[ci] for ci, _ in ct_extra]
    extra = [a for _, a in ct_extra]
    if residual is not None:
        assert kinds[0] == 'tile'
        extra_specs.append(in_specs[0])
        extra.append(residual)
    return pl.pallas_call(body, name=name, grid=grid, in_specs=list(in_specs) + list(ct_specs) + extra_specs,
                          out_specs=out_specs, out_shape=out_shapes)(*ins, *cts, *extra)


def _row_spec(tm, c, col_block=0):
    return pl.BlockSpec((tm, c), lambda i, cb=col_block: (i, cb))


def _full_spec(shape):
    nd = len(shape)
    return pl.BlockSpec(shape, lambda *a, nd=nd: (0,) * nd)


def _f_rms(x, g):
    return (x * lax.rsqrt(jnp.mean(x * x, axis=-1, keepdims=True) + EPS) * g,)


def _f_swiglu(gate, up):
    return (_silu(gate) * up,)


def _f_loss(pid, h, g, tgt, *, tm):
    y = h * lax.rsqrt(jnp.mean(h * h, axis=-1, keepdims=True) + EPS) * g
    row = pid * tm + lax.broadcasted_iota(jnp.int32, (tm, 1), 0)
    err = jnp.where(row >= SKIP, y - tgt, 0.0)
    per_row = jnp.mean(err * err, axis=-1, keepdims=True)
    return (0.5 * jnp.sum(per_row, axis=0, keepdims=True),)


def _f_conv(x, w, *, norm, scale):
    y = x * w[3:4, :]
    for s in (1, 2, 3):
        y = y + _shift_rows(x, s) * w[3 - s:4 - s, :]
    y = _silu(y)
    if norm:
        y = y * lax.rsqrt(jnp.sum(y * y, axis=-1, keepdims=True) + 1e-6) * scale
    return (y,)


def _f_dgates(pid, abeta, aalpha, log_rate, dt_bias, *, tm):
    row = pid * tm + lax.broadcasted_iota(jnp.int32, (tm, 1), 0)
    live = row >= PAD
    beta = jnp.where(live, _sigmoid(abeta), 0.0)
    g = jnp.where(live, -jnp.exp(log_rate) * _softplus(aalpha + dt_bias), 0.0)
    return beta, g


def _f_tshift(z, mu):
    return (z + (_shift_rows(z, 1) - z) * mu,)


def _f_rwkv_pre(k, wd, ad, gd, w0, w_up, a0, a_up, g_up, k_k, k_a):
    w_log = -_softplus(-(w0 + _smm(jnp.tanh(wd), w_up))) - 0.5
    lw = -jnp.exp(w_log)
    a_lr = _sigmoid(a0 + _smm(ad, a_up))
    gate = _smm(_sigmoid(gd), g_up)
    kkp = k * k_k
    kk = kkp * lax.rsqrt(_head_sum(kkp * kkp, B_HEADS) + 1e-6)
    kmod = k * (1.0 + (a_lr - 1.0) * k_a)
    return lw, kmod, -kk, kk * a_lr, gate


def _f_mix_post(o, az, y, r, kmod, v, gate, ga, gb, out_gain, ln_g, ln_b, r_k):
    ms = _head_sum(o * o, A_HEADS) * (1.0 / A_DK)
    oa = o * lax.rsqrt(ms + EPS) * out_gain * _silu(az)
    mean = _head_sum(y, B_HEADS) * (1.0 / B_N)
    yc = y - mean
    var = _head_sum(yc * yc, B_HEADS) * (1.0 / B_N)
    yn = yc * lax.rsqrt(var + B_GN_EPS) * ln_g + ln_b
    bonus = _head_sum(r * kmod * r_k, B_HEADS) * v
    ob = (yn + bonus) * gate
    return (_sigmoid(ga) * oa + _sigmoid(gb) * ob,)


SCAN_PASSES = 3


def _split2(a):
    hi = a.astype(bf16)
    return hi, (a - hi.astype(f32)).astype(bf16)


def _dot_passes(a, b, ca, cb, passes):
    dn = (((ca,), (cb,)), ((), ()))
    if SCAN_PASSES == 0:
        return lax.dot_general(a, b, dn, precision=HI, preferred_element_type=f32)
    if passes == 1:
        return lax.dot_general(a.astype(bf16), b.astype(bf16), dn, preferred_element_type=f32)
    ah, al = _split2(a)
    bh, bl = _split2(b)
    return (lax.dot_general(ah, bh, dn, preferred_element_type=f32)
            + (lax.dot_general(ah, bl, dn, preferred_element_type=f32)
               + lax.dot_general(al, bh, dn, preferred_element_type=f32)))


@functools.partial(jax.custom_vjp, nondiff_argnums=(2, 3, 4))
def _sdot(a, b, ca, cb, passes):
    return _dot_passes(a, b, ca, cb, passes)


def _sdot_fwd(a, b, ca, cb, passes):
    return _dot_passes(a, b, ca, cb, passes), (a, b)


def _sdot_bwd(ca, cb, passes, res, g):
    a, b = res
    if (ca, cb) == (1, 0):
        return _dot_passes(g, b, 1, 1, passes), _dot_passes(a, g, 0, 0, passes)
    if (ca, cb) == (1, 1):
        return _dot_passes(g, b, 1, 0, passes), _dot_passes(g, a, 0, 0, passes)
    assert (ca, cb) == (0, 0)
    return _dot_passes(b, g, 1, 1, passes), _dot_passes(a, g, 1, 0, passes)


_sdot.defvjp(_sdot_fwd, _sdot_bwd)


def _smm(a, b, passes=3):
    return _sdot(a, b, 1, 0, passes)


def _smm_nt(a, b, passes=3):
    return _sdot(a, b, 1, 1, passes)


def _smm_tn(a, b, passes=3):
    return _sdot(a, b, 0, 0, passes)


def _tri_dot(x, ca):
    n = x.shape[0]
    incl = _tri_masks(n)[0]
    dn = (((ca,), (0,)), ((), ()))
    if SCAN_PASSES == 0:
        return lax.dot_general(incl.astype(f32), x, dn, precision=HI, preferred_element_type=f32)
    tri = incl.astype(bf16)
    hi, r1 = x.astype(bf16), None
    r1 = x - hi.astype(f32)
    mid = r1.astype(bf16)
    lo = (r1 - mid.astype(f32)).astype(bf16)
    return (lax.dot_general(tri, hi, dn, preferred_element_type=f32)
            + (lax.dot_general(tri, mid, dn, preferred_element_type=f32)
               + lax.dot_general(tri, lo, dn, preferred_element_type=f32)))


@jax.custom_vjp
def _cumsum_rows(x):
    return _tri_dot(x, 1)


def _cumsum_rows_fwd(x):
    return _tri_dot(x, 1), None


def _cumsum_rows_bwd(_, g):
    return (_tri_dot(g, 0),)


_cumsum_rows.defvjp(_cumsum_rows_fwd, _cumsum_rows_bwd)


def _tri_masks(n):
    i = lax.broadcasted_iota(jnp.int32, (n, n), 0)
    j = lax.broadcasted_iota(jnp.int32, (n, n), 1)
    return i >= j, i > j, i == j, i <= j


def _unit_lower_inv_impl(low, passes):
    n = low.shape[0]
    assert n == CHUNK
    _, _, eye, _ = _tri_masks(n)
    acc = eye.astype(f32) + low
    p = low
    for _ in range(5):
        p = _dot_passes(p, p, 1, 0, passes)
        acc = acc + _dot_passes(acc, p, 1, 0, passes)
    return acc


@functools.partial(jax.custom_vjp, nondiff_argnums=(1,))
def _unit_lower_inv(low, passes=3):
    return _unit_lower_inv_impl(low, passes)


def _unit_lower_inv_fwd(low, passes):
    t = _unit_lower_inv_impl(low, passes)
    return t, t


def _unit_lower_inv_bwd(passes, t, g):
    return (_dot_passes(_dot_passes(t, g, 0, 0, passes), t, 1, 1, passes),)


_unit_lower_inv.defvjp(_unit_lower_inv_fwd, _unit_lower_inv_bwd)

DELTA_PASSES = 1
DELTA_INV_PASSES = 1


def _delta_chunk(s, q, k, v, beta, g):
    p = DELTA_PASSES
    incl, strict, eye, upper = _tri_masks(CHUNK)
    g_row = jnp.sum(jnp.where(eye, g, 0.0), axis=0, keepdims=True)
    gc = jnp.sum(jnp.where(incl, g_row, 0.0), axis=1, keepdims=True)
    gc_row = jnp.sum(jnp.where(upper, g, 0.0), axis=0, keepdims=True)
    decay = jnp.where(incl, jnp.exp(jnp.where(incl, gc - gc_row, 0.0)), 0.0)
    kb = k * beta
    vb = v * beta
    m = jnp.where(strict, _smm_nt(kb, k, p) * decay, 0.0)
    tinv = _unit_lower_inv(-m, DELTA_INV_PASSES)
    u = _smm(tinv, vb, p)
    wk = _smm(tinv, kb * jnp.exp(gc), p)
    attn = _smm_nt(q, k, p) * decay
    qg = q * jnp.exp(gc)
    g_last = jnp.sum(g, axis=0, keepdims=True)
    k_tail = k * jnp.exp(g_last - gc)
    v_new = u - _smm(wk, s, p)
    o = _smm(qg, s, p) + _smm(attn, v_new, p)
    s_new = s * jnp.exp(g_last) + _smm_tn(k_tail, v_new, p)
    return o, s_new


RWKV_PASSES = 1
RWKV_INV_PASSES = 1


def _rwkv_chunk(st, r, k, v, a, b, lw):
    c = CHUNK
    p, pi = RWKV_PASSES, RWKV_INV_PASSES
    _, strict, _, _ = _tri_masks(c)
    lane = lax.broadcasted_iota(jnp.int32, (c, 2 * B_N), 1)
    row = lax.broadcasted_iota(jnp.int32, (c, 2 * B_N), 0)
    first = lane < B_N
    incl2 = row >= jnp.where(first, lane, lane - B_N)
    bi = lax.broadcasted_iota(jnp.int32, (2 * B_N, 2 * B_N), 0) < B_N
    bj = lax.broadcasted_iota(jnp.int32, (2 * B_N, 2 * B_N), 1) < B_N
    blockdiag = bi == bj
    cum = _cumsum_rows(lw)
    e_pos = jnp.exp(cum)
    e_neg = jnp.exp(-cum)
    rt = r * e_pos
    at = a * jnp.exp(cum - lw)
    kt = k * e_neg
    bt = b * e_neg
    bk = jnp.concatenate([bt, kt], axis=0)
    a_s0 = _smm_nt(at, st, p)
    r_s0 = _smm_nt(rt, st, p)
    heads = (first, jnp.logical_not(first))
    u = jnp.zeros((c, 2 * B_N), f32)
    for sel in heads:
        at_h = jnp.where(sel, at, 0.0)
        ab = jnp.where(strict, _smm_nt(at_h, bt, pi), 0.0)
        ak = jnp.where(strict, _smm_nt(at_h, kt, p), 0.0)
        t_h = _unit_lower_inv(ab, pi)
        u = u + _smm(t_h, jnp.where(sel, a_s0, 0.0) + _smm(ak, jnp.where(sel, v, 0.0), p), p)
    y = r_s0
    for sel in heads:
        rbk = jnp.where(incl2, _smm_nt(jnp.where(sel, rt, 0.0), bk, p), 0.0)
        uv = jnp.concatenate([jnp.where(sel, u, 0.0), jnp.where(sel, v, 0.0)], axis=0)
        y = y + _smm(rbk, uv, p)
    cl = jnp.sum(lw, axis=0, keepdims=True)
    dec = jnp.exp(cl - cum)
    uv_all = jnp.concatenate([u, v], axis=0)
    bk_dec = jnp.concatenate([b * dec, k * dec], axis=0)
    st_new = st * jnp.exp(cl) + jnp.where(blockdiag, _smm_tn(uv_all, bk_dec, p), 0.0)
    return y, st_new


GROUPS_PER_STEP = 8


def _scan_specs(ins, col_offs, n_chunks, reverse):
    gw = GROUPS_PER_STEP * LANES
    cidx = (lambda c: n_chunks - 1 - c) if reverse else (lambda c: c)
    specs = []
    for a, off in zip(ins, col_offs):
        if a.ndim == 2:
            assert off % gw == 0
            specs.append(pl.BlockSpec((CHUNK, gw), lambda h, c, o=off // gw: (cidx(c), h + o)))
        else:
            specs.append(pl.BlockSpec((GROUPS_PER_STEP, CHUNK, 1), lambda h, c: (h, cidx(c), 0)))
    return specs, cidx


def _group_vals(refs, g):
    return [r[:, g * LANES:(g + 1) * LANES] if len(r.shape) == 2 else r[g] for r in refs]


def _scan_fwd(chunk_fn, ins, col_offs, n_groups, n_chunks, state_shape, name):
    n_in = len(ins)
    gps = GROUPS_PER_STEP
    t = ins[0].shape[0]

    def body(*refs):
        in_refs = refs[:n_in]
        o_ref, s0_ref, st = refs[n_in:]

        @pl.when(pl.program_id(1) == 0)
        def _():
            st[...] = jnp.zeros_like(st)

        states = st[...]
        vals = [jnp.stack(col) for col in zip(*[_group_vals(in_refs, g) for g in range(gps)])]
        o, s_new = jax.vmap(chunk_fn)(states, *vals)
        s0_ref[...] = states
        st[...] = s_new
        for g in range(gps):
            o_ref[:, g * LANES:(g + 1) * LANES] = o[g]

    specs, _ = _scan_specs(ins, col_offs, n_chunks, False)
    return pl.pallas_call(
        body, name=name, grid=(n_groups // gps, n_chunks), in_specs=specs,
        out_specs=[pl.BlockSpec((CHUNK, gps * LANES), lambda h, c: (c, h)),
                   pl.BlockSpec((gps, None) + state_shape, lambda h, c: (h, c, 0, 0))],
        out_shape=[jax.ShapeDtypeStruct((t, n_groups * LANES), f32),
                   jax.ShapeDtypeStruct((n_groups, n_chunks) + state_shape, f32)],
        scratch_shapes=[pltpu.VMEM((gps,) + state_shape, f32)],
        compiler_params=pltpu.CompilerParams(dimension_semantics=("parallel", "arbitrary")),
    )(*ins)


def _scan_bwd(chunk_fn, s0s, ins, col_offs, d_out, n_groups, n_chunks, state_shape, name):
    n_in = len(ins)
    gps = GROUPS_PER_STEP
    t = d_out.shape[0]

    def body(*refs):
        s0_ref = refs[0]
        in_refs = refs[1:1 + n_in]
        do_ref = refs[1 + n_in]
        g_refs = refs[2 + n_in:2 + 2 * n_in]
        dst = refs[2 + 2 * n_in]

        @pl.when(pl.program_id(1) == 0)
        def _():
            dst[...] = jnp.zeros_like(dst)

        vals = [jnp.stack(col) for col in zip(*[_group_vals(in_refs, g) for g in range(gps)])]
        d_o = jnp.stack([do_ref[:, g * LANES:(g + 1) * LANES] for g in range(gps)])
        _, vjp = jax.vjp(jax.vmap(chunk_fn), s0_ref[...], *vals)
        gs = vjp((d_o, dst[...]))
        dst[...] = gs[0]
        for g_ref, gv in zip(g_refs, gs[1:]):
            if len(g_ref.shape) == 2:
                for g in range(gps):
                    g_ref[:, g * LANES:(g + 1) * LANES] = gv[g]
            else:
                g_ref[...] = gv

    specs, cidx = _scan_specs(ins, col_offs, n_chunks, True)
    out_lane = pl.BlockSpec((CHUNK, gps * LANES), lambda h, c: (cidx(c), h))
    g_specs = [out_lane if a.ndim == 2 else sp for a, sp in zip(ins, specs)]
    g_shapes = [(t, n_groups * LANES) if a.ndim == 2 else a.shape for a in ins]
    s0_spec = pl.BlockSpec((gps, None) + state_shape, lambda h, c: (h, cidx(c), 0, 0))
    return pl.pallas_call(
        body, name=name, grid=(n_groups // gps, n_chunks), in_specs=[s0_spec] + specs + [out_lane],
        out_specs=g_specs, out_shape=[jax.ShapeDtypeStruct(sh, f32) for sh in g_shapes],
        scratch_shapes=[pltpu.VMEM((gps,) + state_shape, f32)],
        compiler_params=pltpu.CompilerParams(dimension_semantics=("parallel", "arbitrary")),
    )(s0s, *ins, d_out)


def _rms_fwd(x, g, name):
    t = x.shape[0]
    tm = _tile(t, 416, 16)
    return _tw_fwd(_f_rms, [x, g], [_row_spec(tm, D), _full_spec(g.shape)],
                   [jax.ShapeDtypeStruct(x.shape, MXU_DTYPE)], [_row_spec(tm, D)], (t // tm,), name)[0]


def _rms_bwd(x, g, dy, residual, name):
    t = x.shape[0]
    tm = _tile(t, 416, 8)
    return _tw_bwd(_f_rms, [x, g], [_row_spec(tm, D), _full_spec(g.shape)], [dy], [_row_spec(tm, D)],
                   ['tile', 'acc'], (t // tm,), name, residual=residual)


def _ffn_fwd(h, gain, wg, wu, wd, tag):
    xn = _rms_fwd(h, gain, f"{tag}_rms")
    gate = _matmul(xn, wg, name=f"{tag}_gate")
    up = _matmul(xn, wu, name=f"{tag}_up")
    t = h.shape[0]
    tm = _tile(t, 208, 16)
    act = _tw_fwd(_f_swiglu, [gate, up], [_row_spec(tm, D_FF)] * 2, [jax.ShapeDtypeStruct((t, D_FF), MXU_DTYPE)],
                  [_row_spec(tm, D_FF)], (t // tm,), f"{tag}_act")[0]
    out = _matmul(act, wd, res=h, scale=0.5, name=f"{tag}_down")
    return out, (xn, gate, up, act)


def _ffn_bwd(h, gain, wg, wu, wd, saved, dout, tag):
    xn, gate, up, act = saved
    t = h.shape[0]
    d_wd = _matmul(act, dout, ta=True, scale=0.5, name=f"{tag}_dwd")
    d_act = _matmul(dout, wd, tb=True, scale=0.5, name=f"{tag}_dact")
    tm = _tile(t, 208, 16)
    d_gate, d_up = _tw_bwd(_f_swiglu, [gate, up], [_row_spec(tm, D_FF)] * 2, [d_act], [_row_spec(tm, D_FF)],
                           ['tile', 'tile'], (t // tm,), f"{tag}_dactf", tile_dtype=MXU_DTYPE)
    d_wg = _matmul(xn, d_gate, ta=True, name=f"{tag}_dwg")
    d_wu = _matmul(xn, d_up, ta=True, name=f"{tag}_dwu")
    d_xn = _matmul(d_gate, wg, tb=True, name=f"{tag}_dxn_g")
    d_xn = _matmul(d_up, wu, tb=True, res=d_xn, name=f"{tag}_dxn_u")
    d_h, d_gain = _rms_bwd(h, gain, d_xn, dout, f"{tag}_drms")
    return d_h, d_gain, d_wg, d_wu, d_wd


def _col_spec(t, first_block):
    return pl.BlockSpec((t, LANES), lambda j, fb=first_block: (0, j + fb))


def _local_step(h0, tgt, w):
    t = h0.shape[0]
    assert t % CHUNK == 0
    nc = t // CHUNK
    grads = {}

    h1, ffn1_saved = _ffn_fwd(h0, w['ffn1_norm'], w['ffn1_wg'], w['ffn1_wu'], w['ffn1_wd'], "ffn1")
    u = _rms_fwd(h1, w['mix_norm'], "mix_rms")
    z = _matmul(u, w['w_in_p'], name="in_proj")
    zs = z[:, 9216:9216 + 304]
    abeta, aalpha = zs[:, 288:296], zs[:, 296:304]

    conv_w = w['a_conv_w']
    conv_fns = [functools.partial(_f_conv, norm=True, scale=A_DK ** -0.5),
                functools.partial(_f_conv, norm=True, scale=1.0),
                functools.partial(_f_conv, norm=False, scale=1.0)]
    qkv = []
    for idx, fn in enumerate(conv_fns):
        qkv.append(_tw_fwd(fn, [z, conv_w], [_col_spec(t, 8 * idx), pl.BlockSpec((4, LANES), lambda j, o=8 * idx: (0, j + o))],
                           [jax.ShapeDtypeStruct((t, D), f32)], [_col_spec(t, 0)], (A_HEADS,), f"a_conv{idx}")[0])
    aq, ak, av = qkv
    tmg = _tile(t, 1040, 8)
    dg_fn = functools.partial(_f_dgates, tm=tmg)
    dg_specs = [_row_spec(tmg, A_HEADS)] * 2 + [_full_spec((1, A_HEADS))] * 2
    beta, gdec = _tw_fwd(dg_fn, [abeta, aalpha, w['a_log_rate'], w['a_dt_bias']], dg_specs,
                         [jax.ShapeDtypeStruct((t, A_HEADS), f32)] * 2, [_row_spec(tmg, A_HEADS)] * 2, (t // tmg,),
                         "a_gates", with_pid=True)
    beta_h = beta.T[:, :, None]
    gdec_h = gdec.T[:, :, None]
    a_ins = [aq, ak, av, beta_h, gdec_h]
    a_offs = [0] * 5
    o_scan, a_s0 = _scan_fwd(_delta_chunk, a_ins, a_offs, A_HEADS, nc, (A_DK, A_DK), "a_scan")

    mu = w['b_shift_mu']
    mu_rkv, mu_s = mu[:, :3072], mu[:, 3072:]
    zf_rkv = _tw_fwd(_f_tshift, [z, mu_rkv], [_col_spec(t, 32), pl.BlockSpec((1, LANES), lambda j: (0, j))],
                     [jax.ShapeDtypeStruct((t, 3072), f32)], [_col_spec(t, 0)], (24,), "b_shift")[0]
    zs_b = zs[:, :288]
    zf_s = _tw_fwd(_f_tshift, [zs_b, mu_s], [_full_spec((t, 288)), _full_spec((1, 288))],
                   [jax.ShapeDtypeStruct((t, 288), f32)], [_full_spec((t, 288))], (1,), "b_shift_s")[0]
    wdf, adf, gdf = zf_s[:, 0:64], zf_s[:, 64:128], zf_s[:, 128:288]
    tmr = _tile(t, 160, 16)
    pre_params = [w['b_w0'], w['b_w_up'], w['b_a0'], w['b_a_up'], w['b_g_up'], w['b_k_k'], w['b_k_a']]
    pre_ins = [zf_rkv, wdf, adf, gdf] + pre_params
    pre_specs = ([_row_spec(tmr, D, 1), _row_spec(tmr, 64), _row_spec(tmr, 64), _row_spec(tmr, 160)]
                 + [_full_spec(p.shape) for p in pre_params])
    lw, kmod, a_s, b_s, bgate = _tw_fwd(_f_rwkv_pre, pre_ins, pre_specs, [jax.ShapeDtypeStruct((t, D), f32)] * 5,
                                        [_row_spec(tmr, D)] * 5, (t // tmr,), "b_pre")
    b_ins = [zf_rkv, kmod, zf_rkv, a_s, b_s, lw]
    b_offs = [0, 0, 2 * D, 0, 0, 0]
    y_scan, b_s0 = _scan_fwd(_rwkv_chunk, b_ins, b_offs, B_HEADS // 2, nc, (2 * B_N, 2 * B_N), "b_scan")

    out_gain_t = jnp.tile(w['a_out_norm'], (1, A_HEADS))
    r_k = w['b_r_k'].reshape(1, D)
    post_params = [out_gain_t, w['b_ln_gain'], w['b_ln_bias'], r_k]
    post_ins = [o_scan, z, y_scan, zf_rkv, kmod, zf_rkv, bgate, z, z] + post_params
    post_specs = ([_row_spec(tmr, D), _row_spec(tmr, D, 3), _row_spec(tmr, D), _row_spec(tmr, D, 0), _row_spec(tmr, D),
                   _row_spec(tmr, D, 2), _row_spec(tmr, D), _row_spec(tmr, D, 7), _row_spec(tmr, D, 8)]
                  + [_full_spec((1, D))] * 4)
    merged = _tw_fwd(_f_mix_post, post_ins, post_specs, [jax.ShapeDtypeStruct((t, D), MXU_DTYPE)],
                     [_row_spec(tmr, D)], (t // tmr,), "mix_post")[0]
    h2 = _matmul(merged, w['w_out'], res=h1, name="out_proj")
    h3, ffn2_saved = _ffn_fwd(h2, w['ffn2_norm'], w['ffn2_wg'], w['ffn2_wu'], w['ffn2_wd'], "ffn2")

    tml = _tile(t, 416, 8)
    fnorm = w['final_norm']
    loss_fn = functools.partial(_f_loss, tm=tml)
    loss_specs = [_row_spec(tml, D), _full_spec((1, D)), _row_spec(tml, D)]
    loss_parts = _tw_fwd(loss_fn, [h3, fnorm, tgt], loss_specs, [jax.ShapeDtypeStruct((t // tml, 1, 1), f32)],
                         [pl.BlockSpec((None, 1, 1), lambda i: (i, 0, 0))], (t // tml,), "loss", with_pid=True)[0]
    loss = jnp.sum(loss_parts)
    ones = jnp.ones((t // tml, 1, 1), f32)
    d_h3, grads['final_norm'] = _tw_bwd(loss_fn, [h3, fnorm, tgt], loss_specs, [ones],
                                        [pl.BlockSpec((None, 1, 1), lambda i: (i, 0, 0))], ['tile', 'acc', None],
                                        (t // tml,), "loss_bwd", with_pid=True)

    d_h2, grads['ffn2_norm'], grads['ffn2_wg'], grads['ffn2_wu'], grads['ffn2_wd'] = _ffn_bwd(
        h2, w['ffn2_norm'], w['ffn2_wg'], w['ffn2_wu'], w['ffn2_wd'], ffn2_saved, d_h3, "ffn2")
    grads['w_out'] = _matmul(merged, d_h2, ta=True, name="d_w_out")
    d_merged = _matmul(d_h2, w['w_out'], tb=True, name="d_merged")

    win = ('tile', (t, D), _row_spec(tmr, D))
    post_kinds = ['tile', win, 'tile', win, 'tile', win, 'tile', win, win] + ['acc'] * 4
    (d_o, d_az, d_y, d_r1, d_kmod1, d_v1, d_bgate, d_ga, d_gb,
     d_out_gain_t, grads['b_ln_gain'], grads['b_ln_bias'], d_r_k) = _tw_bwd(
        _f_mix_post, post_ins, post_specs, [d_merged], [_row_spec(tmr, D)], post_kinds, (t // tmr,), "mix_post_bwd")
    grads['a_out_norm'] = jnp.sum(d_out_gain_t.reshape(A_HEADS, A_DK), axis=0, keepdims=True)
    grads['b_r_k'] = d_r_k.reshape(1, B_HEADS, B_N)

    d_r2, d_kmod2, d_v2, d_as, d_bs, d_lw = _scan_bwd(_rwkv_chunk, b_s0, b_ins, b_offs, d_y, B_HEADS // 2, nc,
                                                      (2 * B_N, 2 * B_N), "b_scan_bwd")
    pre_kinds = [win] + ['tile'] * 3 + ['acc'] * 7
    pre_ct_specs = [_row_spec(tmr, D)] * 5
    (d_zf_k, d_wdf, d_adf, d_gdf, grads['b_w0'], grads['b_w_up'], grads['b_a0'], grads['b_a_up'], grads['b_g_up'],
     grads['b_k_k'], grads['b_k_a']) = _tw_bwd(
        _f_rwkv_pre, pre_ins, pre_specs, [d_lw, d_kmod1, d_as, d_bs, d_bgate], pre_ct_specs, pre_kinds, (t // tmr,),
        "b_pre_bwd", ct_extra=[(1, d_kmod2)])
    d_zf_rkv = _assemble3(d_r1, d_r2, d_zf_k, d_v1, d_v2, "b_dzf")
    d_zb_rkv, d_mu_rkv = _tw_bwd(_f_tshift, [z, mu_rkv], [_col_spec(t, 32), pl.BlockSpec((1, LANES), lambda j: (0, j))],
                                 [d_zf_rkv], [_col_spec(t, 0)], [('tile', (t, 3072), _col_spec(t, 0)), 'tile'], (24,),
                                 "b_shift_bwd")
    d_zf_s = jnp.concatenate([d_wdf, d_adf, d_gdf], axis=1)
    d_zs_b, d_mu_s = _tw_bwd(_f_tshift, [zs_b, mu_s], [_full_spec((t, 288)), _full_spec((1, 288))], [d_zf_s],
                             [_full_spec((t, 288))], ['tile', 'tile'], (1,), "b_shift_s_bwd")
    grads['b_shift_mu'] = jnp.concatenate([d_mu_rkv, d_mu_s], axis=1)

    d_aq, d_ak, d_av, d_beta_h, d_g_h = _scan_bwd(_delta_chunk, a_s0, a_ins, a_offs, d_o, A_HEADS, nc, (A_DK, A_DK),
                                                  "a_scan_bwd")
    d_beta = d_beta_h[:, :, 0].T
    d_gdec = d_g_h[:, :, 0].T
    d_abeta, d_aalpha, grads['a_log_rate'], grads['a_dt_bias'] = _tw_bwd(
        dg_fn, [abeta, aalpha, w['a_log_rate'], w['a_dt_bias']], dg_specs, [d_beta, d_gdec],
        [_row_spec(tmg, A_HEADS)] * 2, ['tile', 'tile', 'acc', 'acc'], (t // tmg,), "a_gates_bwd", with_pid=True)
    d_zqkv, d_conv = [], []
    for idx, (fn, ct) in enumerate(zip(conv_fns, (d_aq, d_ak, d_av))):
        dz_i, dw_i = _conv_bwd(fn, z, conv_w, ct, idx, t)
        d_zqkv.append(dz_i)
        d_conv.append(dw_i)
    grads['a_conv_w'] = jnp.concatenate(d_conv, axis=1)

    d_z_parts = d_zqkv + [d_az, d_zb_rkv, d_ga, d_gb, d_zs_b, d_abeta, d_aalpha, jnp.zeros((t, ZP - 9216 - 304), f32)]
    d_z = jnp.concatenate([p.astype(MXU_DTYPE) for p in d_z_parts], axis=1)
    grads['w_in_p'] = _matmul(u, d_z, ta=True, name="d_w_in")
    d_u = _matmul(d_z, w['w_in_p'], tb=True, name="d_u")
    d_h1, grads['mix_norm'] = _rms_bwd(h1, w['mix_norm'], d_u, d_h2, "mix_drms")
    d_h0, grads['ffn1_norm'], grads['ffn1_wg'], grads['ffn1_wu'], grads['ffn1_wd'] = _ffn_bwd(
        h0, w['ffn1_norm'], w['ffn1_wg'], w['ffn1_wu'], w['ffn1_wd'], ffn1_saved, d_h1, "ffn1")
    return loss, d_h0, grads


_WIN_SEGMENTS = ((0, 4096), (4112, 7184), (7472, 9520), (7184, 7472), (4096, 4112))


def _win_to_padded(w_in):
    parts = [w_in[:, a:b] for a, b in _WIN_SEGMENTS]
    parts.append(jnp.zeros((w_in.shape[0], ZP - IN_TOTAL), w_in.dtype))
    return jnp.concatenate(parts, axis=1)


def _win_from_padded(w_p):
    widths = [b - a for a, b in _WIN_SEGMENTS]
    offs = [sum(widths[:i]) for i in range(len(widths))]
    seg = {a: w_p[:, o:o + wd] for (a, _), o, wd in zip(_WIN_SEGMENTS, offs, widths)}
    return jnp.concatenate([seg[a] for a in sorted(seg)], axis=1)


def _assemble3(d_r1, d_r2, d_k, d_v1, d_v2, name):
    t = d_r1.shape[0]
    tm = _tile(t, 208, 8)

    def body(r1, r2, kk, v1, v2, o_ref):
        o_ref[:, 0:D] = r1[...] + r2[...]
        o_ref[:, D:2 * D] = kk[...]
        o_ref[:, 2 * D:3 * D] = v1[...] + v2[...]

    return pl.pallas_call(body, name=name, grid=(t // tm,), in_specs=[_row_spec(tm, D)] * 5,
                          out_specs=_row_spec(tm, 3 * D), out_shape=jax.ShapeDtypeStruct((t, 3 * D), f32),
                          )(d_r1, d_r2, d_k, d_v1, d_v2)


def _conv_bwd(fn, z, conv_w, ct, idx, t):
    def body(z_ref, w_ref, ct_ref, dz_ref, dw_ref):
        _, vjp = jax.vjp(lambda a, b: fn(a, b), z_ref[...], w_ref[...])
        dz, dw = vjp((ct_ref[...],))
        dz_ref[...] = dz
        dw_ref[...] = dw

    return pl.pallas_call(
        body, name=f"a_conv{idx}_bwd", grid=(A_HEADS,),
        in_specs=[_col_spec(t, 8 * idx), pl.BlockSpec((4, LANES), lambda j, o=8 * idx: (0, j + o)), _col_spec(t, 0)],
        out_specs=[_col_spec(t, 0), pl.BlockSpec((4, LANES), lambda j: (0, j))],
        out_shape=[jax.ShapeDtypeStruct((t, D), f32), jax.ShapeDtypeStruct((4, D), f32)],
    )(z, conv_w, ct)


def _position():
    return lax.axis_index("x"), lax.axis_index("y"), lax.axis_index("c")


def _flip(v, f):
    return 1 - v if f else v


_CHIP_FLIPS = ((1, 0), (0, 1), (1, 1))
_DEV_FLIPS = tuple((fx, fy, fc) for fx in (0, 1) for fy in (0, 1) for fc in (0, 1) if (fx, fy, fc) != (0, 0, 0))


def _gather_chips(arrs, name):
    n = len(arrs)
    assert all(a.shape[0] % 32 == 0 for a in arrs)
    arrs = [a.reshape(2, a.shape[0] // 2, a.shape[1]) for a in arrs]

    def body(*refs):
        ins, outs = refs[:n], refs[n:2 * n]
        send, recv, fsend, frecv, own = refs[2 * n:]
        x, y, c = _position()
        me = 2 * x + y
        sends, plan, owns = [], [], []
        for a in range(n):
            cp = pltpu.make_async_remote_copy(src_ref=ins[a], dst_ref=outs[a].at[me], send_sem=own.at[a, 0],
                                              recv_sem=own.at[a, 1], device_id=(x, y, 1 - c), device_id_type=MESH)
            cp.start()
            owns.append(cp)
            for j, (fx, fy) in enumerate(_CHIP_FLIPS):
                px, py = _flip(x, fx), _flip(y, fy)
                p = 2 * px + py
                cp = pltpu.make_async_remote_copy(src_ref=ins[a].at[c], dst_ref=outs[a].at[me, c],
                                                  send_sem=send.at[a, j], recv_sem=recv.at[a, j],
                                                  device_id=(px, py, c), device_id_type=MESH)
                cp.start()
                sends.append(cp)
                landed = pltpu.make_async_remote_copy(src_ref=ins[a].at[c], dst_ref=outs[a].at[p, c],
                                                      send_sem=send.at[a, j], recv_sem=recv.at[a, j],
                                                      device_id=(px, py, c), device_id_type=MESH)
                onward = pltpu.make_async_remote_copy(src_ref=outs[a].at[p, c], dst_ref=outs[a].at[p, c],
                                                      send_sem=fsend.at[a, j], recv_sem=frecv.at[a, j],
                                                      device_id=(x, y, 1 - c), device_id_type=MESH)
                from_sibling = pltpu.make_async_remote_copy(src_ref=outs[a].at[p, 1 - c], dst_ref=outs[a].at[p, 1 - c],
                                                            send_sem=fsend.at[a, j], recv_sem=frecv.at[a, j],
                                                            device_id=(x, y, 1 - c), device_id_type=MESH)
                plan.append((landed, onward, from_sibling))
        for landed, onward, _ in plan:
            landed.wait_recv()
            onward.start()
        for _, _, from_sibling in plan:
            from_sibling.wait_recv()
        for cp in sends:
            cp.wait_send()
        for _, onward, _ in plan:
            onward.wait_send()
        for cp in owns:
            cp.wait()

    sems = [pltpu.SemaphoreType.DMA((n, 3))] * 4 + [pltpu.SemaphoreType.DMA((n, 2))]
    outs = pl.pallas_call(
        body, name=name, in_specs=[ANY] * n, out_specs=[ANY] * n,
        out_shape=[jax.ShapeDtypeStruct((N_CHIPS,) + a.shape, a.dtype) for a in arrs], scratch_shapes=sems,
    )(*arrs)
    return [o.reshape(N_CHIPS, o.shape[1] * o.shape[2], o.shape[3]) for o in outs]


def _swap_sibling(src_of, shape, dtype, name):
    def body(a_ref, got_ref, send, recv):
        x, y, c = _position()
        cp = pltpu.make_async_remote_copy(src_ref=src_of(a_ref, c), dst_ref=got_ref, send_sem=send, recv_sem=recv,
                                          device_id=(x, y, 1 - c), device_id_type=MESH)
        cp.start()
        cp.wait()

    def call(a):
        return pl.pallas_call(body, name=name, in_specs=[ANY], out_specs=ANY,
                              out_shape=jax.ShapeDtypeStruct(shape, dtype),
                              scratch_shapes=[pltpu.SemaphoreType.DMA(())] * 2)(a)
    return call


def _add_halves(g, got, dtype, name):
    _, n, hr, w = g.shape
    tr = _tile(hr, 784, 16)

    def body(g_ref, got_ref, o_ref):
        c = lax.axis_index("c")
        own = jnp.where(c == 0, g_ref[0], g_ref[1])
        o_ref[...] = (own + got_ref[...]).astype(dtype)

    return pl.pallas_call(
        body, name=name, grid=(hr // tr,),
        in_specs=[pl.BlockSpec((2, n, tr, w), lambda i: (0, 0, i, 0)), pl.BlockSpec((n, tr, w), lambda i: (0, i, 0))],
        out_specs=pl.BlockSpec((n, tr, w), lambda i: (0, i, 0)),
        out_shape=jax.ShapeDtypeStruct((n, hr, w), dtype))(g, got)


def _scatter_chips(g, name):
    def body(g_ref, out_ref, send, recv):
        x, y, c = _position()
        sends = []
        for j, (fx, fy) in enumerate(_CHIP_FLIPS):
            px, py = _flip(x, fx), _flip(y, fy)
            cp = pltpu.make_async_remote_copy(src_ref=g_ref.at[2 * px + py], dst_ref=out_ref.at[j], send_sem=send.at[j],
                                              recv_sem=recv.at[j], device_id=(px, py, c), device_id_type=MESH)
            cp.start()
            sends.append(cp)
        for cp in sends:
            cp.wait_recv()
        for cp in sends:
            cp.wait_send()

    return pl.pallas_call(
        body, name=name, in_specs=[ANY], out_specs=ANY, out_shape=jax.ShapeDtypeStruct((3,) + g.shape[1:], g.dtype),
        scratch_shapes=[pltpu.SemaphoreType.DMA((3,)), pltpu.SemaphoreType.DMA((3,))],
    )(g)


def _sum_own_and_slots(own, got, name):
    n, r, w = own.shape
    tr = _tile(r, 784, 16)

    def body(own_ref, got_ref, o_ref):
        me = 2 * lax.axis_index("x") + lax.axis_index("y")
        acc = own_ref[0]
        for i in range(1, n):
            acc = jnp.where(me == i, own_ref[i], acc)
        acc = acc.astype(f32)
        for j in range(3):
            acc = acc + got_ref[j].astype(f32)
        o_ref[...] = acc

    return pl.pallas_call(
        body, name=name, grid=(r // tr,),
        in_specs=[pl.BlockSpec((n, tr, w), lambda i: (0, i, 0)), pl.BlockSpec((3, tr, w), lambda i: (0, i, 0))],
        out_specs=pl.BlockSpec((tr, w), lambda i: (i, 0)), out_shape=jax.ShapeDtypeStruct((r, w), f32))(own, got)


def _gather_devices(s, name):
    def body(s_ref, out_ref, send, recv, loc):
        x, y, c = _position()
        me = 4 * x + 2 * y + c
        lc = pltpu.make_async_copy(s_ref, out_ref.at[me], loc)
        lc.start()
        sends, recvs = [], []
        for j, (fx, fy, fc) in enumerate(_DEV_FLIPS):
            px, py, pc = _flip(x, fx), _flip(y, fy), _flip(c, fc)
            cp = pltpu.make_async_remote_copy(src_ref=s_ref, dst_ref=out_ref.at[me], send_sem=send.at[j],
                                              recv_sem=recv.at[j], device_id=(px, py, pc), device_id_type=MESH)
            cp.start()
            sends.append(cp)
            recvs.append(pltpu.make_async_remote_copy(
                src_ref=s_ref, dst_ref=out_ref.at[4 * px + 2 * py + pc], send_sem=send.at[j], recv_sem=recv.at[j],
                device_id=(px, py, pc), device_id_type=MESH))
        for cp in recvs:
            cp.wait_recv()
        for cp in sends:
            cp.wait_send()
        lc.wait()

    return pl.pallas_call(
        body, name=name, in_specs=[ANY], out_specs=ANY, out_shape=jax.ShapeDtypeStruct((N_DEV,) + s.shape, s.dtype),
        scratch_shapes=[pltpu.SemaphoreType.DMA((7,)), pltpu.SemaphoreType.DMA((7,)), pltpu.SemaphoreType.DMA(())],
    )(s)


def _sum_slots(a, name):
    s, r, c = a.shape
    tr = _tile(r, 2048, 16)

    def body(a_ref, o_ref):
        acc = a_ref[0].astype(f32)
        for i in range(1, s):
            acc = acc + a_ref[i].astype(f32)
        o_ref[...] = acc

    return pl.pallas_call(body, name=name, grid=(r // tr,), in_specs=[pl.BlockSpec((s, tr, c), lambda i: (0, i, 0))],
                          out_specs=pl.BlockSpec((tr, c), lambda i: (i, 0)),
                          out_shape=jax.ShapeDtypeStruct((r, c), f32))(a)


def _adamw(w, g_parts, m, v, name):
    shape = w.shape
    size = w.size
    view = (size // LANES, LANES) if size % LANES == 0 else (1, size)
    rows = view[0]
    tr = _tile(rows, 2048, 8) if rows > 2048 else rows
    n_g = len(g_parts)

    def body(*refs):
        w_ref = refs[0]
        g_refs = refs[1:1 + n_g]
        m_ref, v_ref, g_out, d_out, m_out, v_out = refs[1 + n_g:]
        g = g_refs[0][...]
        for gr in g_refs[1:]:
            g = g + gr[...]
        m_new = ADAM_B1 * m_ref[...] + (1.0 - ADAM_B1) * g
        v_new = ADAM_B2 * v_ref[...] + (1.0 - ADAM_B2) * (g * g)
        m_hat = m_new / (1.0 - ADAM_B1 ** ADAM_STEP)
        v_hat = v_new / (1.0 - ADAM_B2 ** ADAM_STEP)
        g_out[...] = g
        d_out[...] = -ADAM_LR * (m_hat / (jnp.sqrt(v_hat) + ADAM_EPS) + ADAM_WD * w_ref[...])
        m_out[...] = m_new
        v_out[...] = v_new

    spec = pl.BlockSpec((tr, view[1]), lambda i: (i, 0))
    args = [w.reshape(view)] + [g.reshape(view) for g in g_parts] + [m.reshape(view), v.reshape(view)]
    outs = pl.pallas_call(body, name=name, grid=(rows // tr,), in_specs=[spec] * len(args), out_specs=[spec] * 4,
                          out_shape=[jax.ShapeDtypeStruct(view, f32)] * 4)(*args)
    return [o.reshape(shape) for o in outs]


_BIG = ('ffn1_w_gu', 'ffn1_w_down', 'w_in', 'w_out', 'ffn2_w_gu', 'ffn2_w_down')
_SMALL_SHARDED = ('meta_tokens', 'a_conv_w', 'b_w_up', 'b_a_up', 'b_g_up')
_WEIGHTS = ('meta_tokens', 'ffn1_norm', 'ffn1_w_gu', 'ffn1_w_down', 'mix_norm', 'w_in', 'a_conv_w', 'a_log_rate',
            'a_dt_bias', 'a_out_norm', 'b_shift_mu', 'b_w0', 'b_w_up', 'b_a0', 'b_a_up', 'b_g_up', 'b_k_k', 'b_k_a',
            'b_r_k', 'b_ln_gain', 'b_ln_bias', 'w_out', 'ffn2_norm', 'ffn2_w_gu', 'ffn2_w_down', 'final_norm')
_SMALL = tuple(n for n in _WEIGHTS if n not in _BIG)


def _rows_of(shape):
    n = 1
    for d in shape:
        n *= d
    return n, -(-n // LANES)


def _pack(arrs, dtype, row_mult=32):
    parts, total = [], 0
    for a in arrs:
        n, rows = _rows_of(a.shape)
        flat = a.reshape(-1).astype(dtype)
        if n % LANES:
            flat = jnp.pad(flat, (0, rows * LANES - n))
        parts.append(flat)
        total += rows
    extra = -total % row_mult
    if extra:
        parts.append(jnp.zeros((extra * LANES,), dtype))
    return jnp.concatenate(parts).reshape(total + extra, LANES)


def _unpack(packed, shapes, lead=()):
    out, off = [], 0
    for sh in shapes:
        n, rows = _rows_of(sh)
        seg = packed[..., off:off + rows, :]
        if n % LANES:
            seg = seg.reshape(lead + (-1,))[..., :n]
        out.append(seg.reshape(lead + tuple(sh)))
        off += rows
    return out


def _cols_from_shards(s):
    return jnp.concatenate([s[i] for i in range(N_CHIPS)], axis=-1)


def _cols_to_shards(a):
    r, c = a.shape
    return a.reshape(r, N_CHIPS, c // N_CHIPS).transpose(1, 0, 2)


def kernel(x, meta_tokens, ffn1_norm, ffn1_w_gu, ffn1_w_down, mix_norm, w_in, a_conv_w, a_log_rate, a_dt_bias, a_out_norm, b_shift_mu, b_w0, b_w_up, b_a0, b_a_up, b_g_up, b_k_k, b_k_a, b_r_k, b_ln_gain, b_ln_bias, w_out, ffn2_norm, ffn2_w_gu, ffn2_w_down, final_norm, loss_target, m_meta_tokens, m_ffn1_norm, m_ffn1_w_gu, m_ffn1_w_down, m_mix_norm, m_w_in, m_a_conv_w, m_a_log_rate, m_a_dt_bias, m_a_out_norm, m_b_shift_mu, m_b_w0, m_b_w_up, m_b_a0, m_b_a_up, m_b_g_up, m_b_k_k, m_b_k_a, m_b_r_k, m_b_ln_gain, m_b_ln_bias, m_w_out, m_ffn2_norm, m_ffn2_w_gu, m_ffn2_w_down, m_final_norm, v_meta_tokens, v_ffn1_norm, v_ffn1_w_gu, v_ffn1_w_down, v_mix_norm, v_w_in, v_a_conv_w, v_a_log_rate, v_a_dt_bias, v_a_out_norm, v_b_shift_mu, v_b_w0, v_b_w_up, v_b_a0, v_b_a_up, v_b_g_up, v_b_k_k, v_b_k_a, v_b_r_k, v_b_ln_gain, v_b_ln_bias, v_w_out, v_ffn2_norm, v_ffn2_w_gu, v_ffn2_w_down, v_final_norm):
    args = locals()
    wts = {n: args[n] for n in _WEIGHTS}
    mom = {n: args["m_" + n] for n in _WEIGHTS}
    var = {n: args["v_" + n] for n in _WEIGHTS}
    chip = 2 * lax.axis_index("x") + lax.axis_index("y")

    big_shapes = [wts[n].shape[1:] for n in _BIG]
    small_shapes = [wts[n].shape[-2:] for n in _SMALL_SHARDED]
    big_packed = _pack([wts[n] for n in _BIG], bf16)
    small_packed = _pack([wts[n] for n in _SMALL_SHARDED], f32)
    big_all, small_all = _gather_chips([big_packed, small_packed], "gather_weights")
    gu1, dn1, w_in_s, w_out_s, gu2, dn2 = _unpack(big_all, big_shapes, (N_CHIPS,))
    meta_s, conv_s, wup_s, aup_s, gup_s = _unpack(small_all, small_shapes, (N_CHIPS,))
    w = {
        'ffn1_norm': ffn1_norm, 'mix_norm': mix_norm, 'ffn2_norm': ffn2_norm, 'final_norm': final_norm[None, :],
        'ffn1_wg': jnp.concatenate([gu1[0], gu1[1]], axis=1), 'ffn1_wu': jnp.concatenate([gu1[2], gu1[3]], axis=1),
        'ffn1_wd': dn1.reshape(D_FF, D),
        'ffn2_wg': jnp.concatenate([gu2[0], gu2[1]], axis=1), 'ffn2_wu': jnp.concatenate([gu2[2], gu2[3]], axis=1),
        'ffn2_wd': dn2.reshape(D_FF, D),
        'w_in_p': _win_to_padded(_cols_from_shards(w_in_s)), 'w_out': w_out_s.reshape(D, D),
        'a_conv_w': _cols_from_shards(conv_s), 'b_w_up': _cols_from_shards(wup_s), 'b_a_up': _cols_from_shards(aup_s),
        'b_g_up': _cols_from_shards(gup_s),
        'a_log_rate': a_log_rate, 'a_dt_bias': a_dt_bias, 'a_out_norm': a_out_norm, 'b_shift_mu': b_shift_mu,
        'b_w0': b_w0, 'b_a0': b_a0, 'b_k_k': b_k_k, 'b_k_a': b_k_a, 'b_r_k': b_r_k, 'b_ln_gain': b_ln_gain,
        'b_ln_bias': b_ln_bias,
    }
    meta_full = _cols_from_shards(meta_s)

    h0 = jnp.concatenate([jnp.zeros((PAD, D), f32), meta_full, x[0]], axis=0)
    tgt = jnp.concatenate([jnp.zeros((SKIP, D), f32), loss_target[0]], axis=0)
    loss_local, d_h0, g = _local_step(h0, tgt, w)
    loss = lax.psum(loss_local, ("x", "y", "c"))
    grad_x = d_h0[SKIP:][None]

    big_grads = [
        _cols_to_shards(jnp.concatenate([g['ffn1_wg'], g['ffn1_wu']], axis=1)),
        g['ffn1_wd'].reshape(N_CHIPS, D_FF // N_CHIPS, D),
        _cols_to_shards(_win_from_padded(g['w_in_p'])),
        g['w_out'].reshape(N_CHIPS, D // N_CHIPS, D),
        _cols_to_shards(jnp.concatenate([g['ffn2_wg'], g['ffn2_wu']], axis=1)),
        g['ffn2_wd'].reshape(N_CHIPS, D_FF // N_CHIPS, D),
    ]
    g_packed = jnp.concatenate([a.reshape(N_CHIPS, -1, LANES) for a in big_grads], axis=1)
    assert g_packed.shape[1] == big_packed.shape[0]
    rows = g_packed.shape[1]
    g_halves = g_packed.reshape(N_CHIPS, 2, rows // 2, LANES).transpose(1, 0, 2, 3)
    half_shape = (N_CHIPS, rows // 2, LANES)
    sib_half = _swap_sibling(lambda ref, c: ref.at[1 - c], half_shape, f32, "swap_halves")(g_halves)
    chip_half = _add_halves(g_halves, sib_half, bf16, "add_sibling")
    mine = _sum_own_and_slots(chip_half, _scatter_chips(chip_half, "scatter_grads"), "sum_chips")
    theirs = _swap_sibling(lambda ref, c: ref, mine.shape, f32, "swap_sums")(mine)
    core = lax.axis_index("c")
    summed = jnp.concatenate([jnp.where(core == 0, mine, theirs), jnp.where(core == 0, theirs, mine)], axis=0)
    big_parts = _unpack(summed, big_shapes)

    small_full = {
        'meta_tokens': d_h0[PAD:SKIP], 'ffn1_norm': g['ffn1_norm'], 'mix_norm': g['mix_norm'], 'a_conv_w': g['a_conv_w'],
        'a_log_rate': g['a_log_rate'], 'a_dt_bias': g['a_dt_bias'], 'a_out_norm': g['a_out_norm'],
        'b_shift_mu': g['b_shift_mu'], 'b_w0': g['b_w0'], 'b_w_up': g['b_w_up'], 'b_a0': g['b_a0'], 'b_a_up': g['b_a_up'],
        'b_g_up': g['b_g_up'], 'b_k_k': g['b_k_k'], 'b_k_a': g['b_k_a'], 'b_r_k': g['b_r_k'], 'b_ln_gain': g['b_ln_gain'],
        'b_ln_bias': g['b_ln_bias'], 'ffn2_norm': g['ffn2_norm'], 'final_norm': g['final_norm'],
    }
    s_shapes = [small_full[n].shape for n in _SMALL]
    s_sum = _sum_slots(_gather_devices(_pack([small_full[n] for n in _SMALL], f32, row_mult=256), "gather_small"),
                       "sum_small")
    s_parts = dict(zip(_SMALL, _unpack(s_sum, s_shapes)))

    grad, delta, new_m, new_v = {}, {}, {}, {}
    for n, a in zip(_BIG, big_parts):
        grad[n], delta[n], new_m[n], new_v[n] = _adamw(wts[n], [a.reshape(wts[n].shape)], mom[n], var[n], f"adamw_{n}")
    for n in _SMALL:
        gs = s_parts[n]
        if n in _SMALL_SHARDED:
            width = wts[n].shape[-1]
            gs = lax.dynamic_slice_in_dim(gs, chip * width, width, axis=gs.ndim - 1)
        gs = gs.reshape(wts[n].shape)
        grad[n], delta[n], new_m[n], new_v[n] = _adamw(wts[n], [gs], mom[n], var[n], f"adamw_{n}")

    return (loss, grad_x, *[grad[n] for n in _WEIGHTS], *[delta[n] for n in _WEIGHTS],
            *[new_m[n] for n in _WEIGHTS], *[new_v[n] for n in _WEIGHTS])
```

```python
import functools

import jax
import jax.numpy as jnp
from jax import lax
from jax.experimental import pallas as pl
from jax.experimental.pallas import tpu as pltpu

f32 = jnp.float32
bf16 = jnp.bfloat16
HI = lax.Precision.HIGHEST
MESH = pl.DeviceIdType.MESH
ANY = pl.BlockSpec(memory_space=pl.ANY)

D = 1024
N_META = 16
CHUNK = 64
PAD = CHUNK - N_META
SKIP = PAD + N_META
EPS = 1e-6
D_FF = 2816
A_HEADS = 8
A_DK = 128
B_HEADS = 16
B_N = 64
B_GN_EPS = B_N * 1e-5
W_LORA, AA_LORA, G_LORA = 64, 64, 160
IN_TOTAL = 9520
ZP = 9600
LANES = 128
N_CHIPS = 4
N_DEV = 8

ADAM_LR, ADAM_B1, ADAM_B2, ADAM_EPS, ADAM_WD, ADAM_STEP = 0.001, 0.9, 0.999, 1e-08, 0.01, 10

MXU_DTYPE = bf16


def _tile(n, cap, mult):
    if n <= cap:
        return n
    best = None
    for t in range(mult, cap + 1, mult):
        if n % t == 0:
            best = t
    assert best is not None, (n, cap, mult)
    return best


def _sigmoid(x):
    return jax.nn.sigmoid(x)


def _silu(x):
    return x * jax.nn.sigmoid(x)


def _softplus(x):
    return jnp.maximum(x, 0.0) + jnp.log(1.0 + jnp.exp(-jnp.abs(x)))


def _head_matrix(c, nh):
    hd = c // nh
    r = lax.broadcasted_iota(jnp.int32, (c, nh), 0)
    h = lax.broadcasted_iota(jnp.int32, (c, nh), 1)
    return (r >= h * hd) & (r < (h + 1) * hd)


def _dot_exact_rhs(x, e, cb):
    dn = (((1,), (cb,)), ((), ()))
    if SCAN_PASSES == 0:
        return lax.dot_general(x, e.astype(f32), dn, precision=HI, preferred_element_type=f32)
    eb = e.astype(bf16)
    hi = x.astype(bf16)
    lo = (x - hi.astype(f32)).astype(bf16)
    return (lax.dot_general(hi, eb, dn, preferred_element_type=f32)
            + lax.dot_general(lo, eb, dn, preferred_element_type=f32))


def _head_sum_impl(x, nh):
    e = _head_matrix(x.shape[-1], nh)
    return _dot_exact_rhs(_dot_exact_rhs(x, e, 0), e, 1)


@functools.partial(jax.custom_vjp, nondiff_argnums=(1,))
def _head_sum(x, nh):
    return _head_sum_impl(x, nh)


def _head_sum_fwd(x, nh):
    return _head_sum_impl(x, nh), None


def _head_sum_bwd(nh, _, g):
    return (_head_sum_impl(g, nh),)


_head_sum.defvjp(_head_sum_fwd, _head_sum_bwd)


@functools.partial(jax.custom_vjp, nondiff_argnums=(1,))
def _shift_rows(x, s):
    n = x.shape[0]
    row = lax.broadcasted_iota(jnp.int32, x.shape, 0)
    if s > 0:
        return jnp.where(row >= s, pltpu.roll(x, s, 0), 0.0)
    return jnp.where(row < n + s, pltpu.roll(x, n + s, 0), 0.0)


def _shift_rows_fwd(x, s):
    return _shift_rows(x, s), None


def _shift_rows_bwd(s, _, g):
    return (_shift_rows(g, -s),)


_shift_rows.defvjp(_shift_rows_fwd, _shift_rows_bwd)


def _matmul(a, b, *, ta=False, tb=False, res=None, scale=1.0, name, b_cols_split=None, out_cols_split=False):
    assert not (ta and tb)
    (ar, ac) = a.shape
    b0 = 0
    if b_cols_split:
        b0, bs = b_cols_split
        _, br, bc_part = b.shape
        bc = bs * bc_part
    else:
        br, bc = b.shape
    m, k = (ac, ar) if ta else (ar, ac)
    n, kb = (br, bc) if tb else (bc, br)
    assert k == kb, (a.shape, b.shape, ta, tb)
    tm = _tile(m, 1408, LANES) if ta else _tile(m, 832, 8)
    tn = _tile(n, 1408, LANES)
    tk = _tile(k, 1040, 8) if ta else _tile(k, 1408, LANES)
    nk = k // tk
    dn = (((0 if ta else 1,), (1 if tb else 0,)), ((), ()))
    if b_cols_split:
        assert (tk if tb else tn) == bc_part, (b.shape, tn, tk)

    def body(*refs):
        if res is not None:
            a_ref, b_ref, r_ref, o_ref, acc = refs
        else:
            a_ref, b_ref, o_ref, acc = refs
        kk = pl.program_id(2)

        @pl.when(kk == 0)
        def _():
            acc[...] = jnp.zeros_like(acc)

        acc[...] += lax.dot_general(a_ref[...].astype(MXU_DTYPE), b_ref[...].astype(MXU_DTYPE), dn,
                                    preferred_element_type=f32,
                                    precision=None if MXU_DTYPE == bf16 else HI)

        @pl.when(kk == nk - 1)
        def _():
            out = acc[...]
            if scale != 1.0:
                out = out * scale
            if res is not None:
                out = r_ref[...] + out
            o_ref[...] = out

    if ta:
        a_spec = pl.BlockSpec((tk, tm), lambda i, j, kk: (kk, i))
    else:
        a_spec = pl.BlockSpec((tm, tk), lambda i, j, kk: (i, kk))
    if tb and b_cols_split:
        b_spec = pl.BlockSpec((None, tn, tk), lambda i, j, kk: (kk + b0, j, 0))
    elif tb:
        b_spec = pl.BlockSpec((tn, tk), lambda i, j, kk: (j, kk))
    elif b_cols_split:
        b_spec = pl.BlockSpec((None, tk, tn), lambda i, j, kk: (j + b0, kk, 0))
    else:
        b_spec = pl.BlockSpec((tk, tn), lambda i, j, kk: (kk, j))
    in_specs = [a_spec, b_spec]
    args = [a, b]
    if res is not None:
        in_specs.append(pl.BlockSpec((tm, tn), lambda i, j, kk: (i, j)))
        args.append(res)
    if out_cols_split:
        out_spec = pl.BlockSpec((None, tm, tn), lambda i, j, kk: (j, i, 0))
        out_shape = jax.ShapeDtypeStruct((n // tn, m, tn), f32)
    else:
        out_spec = pl.BlockSpec((tm, tn), lambda i, j, kk: (i, j))
        out_shape = jax.ShapeDtypeStruct((m, n), f32)
    return pl.pallas_call(
        body, name=name, grid=(m // tm, n // tn, nk), in_specs=in_specs, out_specs=out_spec, out_shape=out_shape,
        scratch_shapes=[pltpu.VMEM((tm, tn), f32)],
        compiler_params=pltpu.CompilerParams(dimension_semantics=("parallel", "parallel", "arbitrary")),
    )(*args)


def _tw_fwd(fn, ins, in_specs, out_shapes, out_specs, grid, name, with_pid=False):
    n_in = len(ins)

    def body(*refs):
        vals = [r[...] for r in refs[:n_in]]
        outs = fn(pl.program_id(0), *vals) if with_pid else fn(*vals)
        for r, o in zip(refs[n_in:], outs):
            r[...] = o.astype(r.dtype)

    return pl.pallas_call(body, name=name, grid=grid, in_specs=in_specs, out_specs=out_specs,
                          out_shape=out_shapes)(*ins)


def _tw_bwd(fn, ins, in_specs, cts, ct_specs, kinds, grid, name, with_pid=False, tile_dtype=f32, ct_extra=(),
            residual=None):
    n_in, n_ct = len(ins), len(cts)
    diff = [i for i, kd in enumerate(kinds) if kd is not None]
    n_ex = len(ct_extra)

    def body(*refs):
        vals = [r[...] for r in refs[:n_in]]
        ctv = [r[...].astype(f32) for r in refs[n_in:n_in + n_ct]]
        for (ci, _), r in zip(ct_extra, refs[n_in + n_ct:n_in + n_ct + n_ex]):
            ctv[ci] = ctv[ci] + r[...]
        ctv = tuple(ctv)
        n_fixed = n_in + n_ct + n_ex
        res_ref = refs[n_fixed] if residual is not None else None
        g_refs = refs[n_fixed + (residual is not None):]
        pid = pl.program_id(0)

        def f(*dv):
            full = list(vals)
            for i, v in zip(diff, dv):
                full[i] = v
            out = fn(pid, *full) if with_pid else fn(*full)
            return tuple(out)

        _, vjp = jax.vjp(f, *[vals[i] for i in diff])
        gs = vjp(ctv)
        first = pid == 0
        for i2 in range(1, len(grid)):
            first = first & (pl.program_id(i2) == 0)
        for i, g, g_ref in zip(diff, gs, g_refs):
            if kinds[i] != 'acc':
                if i == 0 and res_ref is not None:
                    g = res_ref[...] + g
                g_ref[...] = g.astype(g_ref.dtype)
            else:
                @pl.when(first)
                def _(g=g, g_ref=g_ref):
                    g_ref[...] = g

                @pl.when(jnp.logical_not(first))
                def _(g=g, g_ref=g_ref):
                    g_ref[...] += g

    zero_map = {1: lambda *a: (0,), 2: lambda *a: (0, 0), 3: lambda *a: (0, 0, 0)}
    out_specs, out_shapes = [], []
    for i in diff:
        if kinds[i] == 'tile':
            out_shapes.append(jax.ShapeDtypeStruct(ins[i].shape, tile_dtype))
            out_specs.append(in_specs[i])
        elif kinds[i] == 'acc':
            out_shapes.append(jax.ShapeDtypeStruct(ins[i].shape, f32))
            out_specs.append(pl.BlockSpec(ins[i].shape, zero_map[ins[i].ndim]))
        else:
            out_shapes.append(jax.ShapeDtypeStruct(kinds[i][1], tile_dtype))
            out_specs.append(kinds[i][2])
    extra_specs = [ct_specs[ci] for ci, _ in ct_extra]
    extra = [a for _, a in ct_extra]
    if residual is not None:
        assert kinds[0] == 'tile'
        extra_specs.append(in_specs[0])
        extra.append(residual)
    return pl.pallas_call(body, name=name, grid=grid, in_specs=list(in_specs) + list(ct_specs) + extra_specs,
                          out_specs=out_specs, out_shape=out_shapes)(*ins, *cts, *extra)


def _row_spec(tm, c, col_block=0):
    return pl.BlockSpec((tm, c), lambda i, cb=col_block: (i, cb))


def _full_spec(shape):
    nd = len(shape)
    return pl.BlockSpec(shape, lambda *a, nd=nd: (0,) * nd)


def _f_rms(x, g):
    return (x * lax.rsqrt(jnp.mean(x * x, axis=-1, keepdims=True) + EPS) * g,)


def _f_swiglu(gate, up):
    return (_silu(gate) * up,)


def _f_loss(pid, h, g, tgt, *, tm):
    y = h * lax.rsqrt(jnp.mean(h * h, axis=-1, keepdims=True) + EPS) * g
    row = pid * tm + lax.broadcasted_iota(jnp.int32, (tm, 1), 0)
    err = jnp.where(row >= SKIP, y - tgt, 0.0)
    per_row = jnp.mean(err * err, axis=-1, keepdims=True)
    return (0.5 * jnp.sum(per_row, axis=0, keepdims=True),)


def _f_conv(x, w, *, norm, scale):
    y = x * w[3:4, :]
    for s in (1, 2, 3):
        y = y + _shift_rows(x, s) * w[3 - s:4 - s, :]
    y = _silu(y)
    if norm:
        y = y * lax.rsqrt(jnp.sum(y * y, axis=-1, keepdims=True) + 1e-6) * scale
    return (y,)


def _f_dgates(pid, abeta, aalpha, log_rate, dt_bias, *, tm):
    row = pid * tm + lax.broadcasted_iota(jnp.int32, (tm, 1), 0)
    live = row >= PAD
    beta = jnp.where(live, _sigmoid(abeta), 0.0)
    g = jnp.where(live, -jnp.exp(log_rate) * _softplus(aalpha + dt_bias), 0.0)
    return beta, g


def _f_tshift(z, mu):
    return (z + (_shift_rows(z, 1) - z) * mu,)


def _f_rwkv_pre(k, wd, ad, gd, w0, w_up, a0, a_up, g_up, k_k, k_a):
    w_log = -_softplus(-(w0 + _smm(jnp.tanh(wd), w_up))) - 0.5
    lw = -jnp.exp(w_log)
    a_lr = _sigmoid(a0 + _smm(ad, a_up))
    gate = _smm(_sigmoid(gd), g_up)
    kkp = k * k_k
    kk = kkp * lax.rsqrt(_head_sum(kkp * kkp, B_HEADS) + 1e-6)
    kmod = k * (1.0 + (a_lr - 1.0) * k_a)
    return lw, kmod, -kk, kk * a_lr, gate


def _f_mix_post(o, az, y, r, kmod, v, gate, ga, gb, out_gain, ln_g, ln_b, r_k):
    ms = _head_sum(o * o, A_HEADS) * (1.0 / A_DK)
    oa = o * lax.rsqrt(ms + EPS) * out_gain * _silu(az)
    mean = _head_sum(y, B_HEADS) * (1.0 / B_N)
    yc = y - mean
    var = _head_sum(yc * yc, B_HEADS) * (1.0 / B_N)
    yn = yc * lax.rsqrt(var + B_GN_EPS) * ln_g + ln_b
    bonus = _head_sum(r * kmod * r_k, B_HEADS) * v
    ob = (yn + bonus) * gate
    return (_sigmoid(ga) * oa + _sigmoid(gb) * ob,)


SCAN_PASSES = 3


def _split2(a):
    hi = a.astype(bf16)
    return hi, (a - hi.astype(f32)).astype(bf16)


def _dot_passes(a, b, ca, cb, passes):
    dn = (((ca,), (cb,)), ((), ()))
    if SCAN_PASSES == 0:
        return lax.dot_general(a, b, dn, precision=HI, preferred_element_type=f32)
    if passes == 1:
        return lax.dot_general(a.astype(bf16), b.astype(bf16), dn, preferred_element_type=f32)
    ah, al = _split2(a)
    bh, bl = _split2(b)
    return (lax.dot_general(ah, bh, dn, preferred_element_type=f32)
            + (lax.dot_general(ah, bl, dn, preferred_element_type=f32)
               + lax.dot_general(al, bh, dn, preferred_element_type=f32)))


@functools.partial(jax.custom_vjp, nondiff_argnums=(2, 3, 4))
def _sdot(a, b, ca, cb, passes):
    return _dot_passes(a, b, ca, cb, passes)


def _sdot_fwd(a, b, ca, cb, passes):
    return _dot_passes(a, b, ca, cb, passes), (a, b)


def _sdot_bwd(ca, cb, passes, res, g):
    a, b = res
    if (ca, cb) == (1, 0):
        return _dot_passes(g, b, 1, 1, passes), _dot_passes(a, g, 0, 0, passes)
    if (ca, cb) == (1, 1):
        return _dot_passes(g, b, 1, 0, passes), _dot_passes(g, a, 0, 0, passes)
    assert (ca, cb) == (0, 0)
    return _dot_passes(b, g, 1, 1, passes), _dot_passes(a, g, 1, 0, passes)


_sdot.defvjp(_sdot_fwd, _sdot_bwd)


def _smm(a, b, passes=3):
    return _sdot(a, b, 1, 0, passes)


def _smm_nt(a, b, passes=3):
    return _sdot(a, b, 1, 1, passes)


def _smm_tn(a, b, passes=3):
    return _sdot(a, b, 0, 0, passes)


def _tri_dot(x, ca):
    n = x.shape[0]
    incl = _tri_masks(n)[0]
    dn = (((ca,), (0,)), ((), ()))
    if SCAN_PASSES == 0:
        return lax.dot_general(incl.astype(f32), x, dn, precision=HI, preferred_element_type=f32)
    tri = incl.astype(bf16)
    hi, r1 = x.astype(bf16), None
    r1 = x - hi.astype(f32)
    mid = r1.astype(bf16)
    lo = (r1 - mid.astype(f32)).astype(bf16)
    return (lax.dot_general(tri, hi, dn, preferred_element_type=f32)
            + (lax.dot_general(tri, mid, dn, preferred_element_type=f32)
               + lax.dot_general(tri, lo, dn, preferred_element_type=f32)))


@jax.custom_vjp
def _cumsum_rows(x):
    return _tri_dot(x, 1)


def _cumsum_rows_fwd(x):
    return _tri_dot(x, 1), None


def _cumsum_rows_bwd(_, g):
    return (_tri_dot(g, 0),)


_cumsum_rows.defvjp(_cumsum_rows_fwd, _cumsum_rows_bwd)


def _tri_masks(n):
    i = lax.broadcasted_iota(jnp.int32, (n, n), 0)
    j = lax.broadcasted_iota(jnp.int32, (n, n), 1)
    return i >= j, i > j, i == j, i <= j


def _unit_lower_inv_impl(low, passes):
    n = low.shape[0]
    assert n == CHUNK
    _, _, eye, _ = _tri_masks(n)
    acc = eye.astype(f32) + low
    p = low
    for _ in range(5):
        p = _dot_passes(p, p, 1, 0, passes)
        acc = acc + _dot_passes(acc, p, 1, 0, passes)
    return acc


@functools.partial(jax.custom_vjp, nondiff_argnums=(1,))
def _unit_lower_inv(low, passes=3):
    return _unit_lower_inv_impl(low, passes)


def _unit_lower_inv_fwd(low, passes):
    t = _unit_lower_inv_impl(low, passes)
    return t, t


def _unit_lower_inv_bwd(passes, t, g):
    return (_dot_passes(_dot_passes(t, g, 0, 0, passes), t, 1, 1, passes),)


_unit_lower_inv.defvjp(_unit_lower_inv_fwd, _unit_lower_inv_bwd)

DELTA_PASSES = 1
DELTA_INV_PASSES = 1


def _delta_chunk(s, q, k, v, beta, g):
    p = DELTA_PASSES
    incl, strict, eye, upper = _tri_masks(CHUNK)
    g_row = jnp.sum(jnp.where(eye, g, 0.0), axis=0, keepdims=True)
    gc = jnp.sum(jnp.where(incl, g_row, 0.0), axis=1, keepdims=True)
    gc_row = jnp.sum(jnp.where(upper, g, 0.0), axis=0, keepdims=True)
    decay = jnp.where(incl, jnp.exp(jnp.where(incl, gc - gc_row, 0.0)), 0.0)
    kb = k * beta
    vb = v * beta
    m = jnp.where(strict, _smm_nt(kb, k, p) * decay, 0.0)
    tinv = _unit_lower_inv(-m, DELTA_INV_PASSES)
    u = _smm(tinv, vb, p)
    wk = _smm(tinv, kb * jnp.exp(gc), p)
    attn = _smm_nt(q, k, p) * decay
    qg = q * jnp.exp(gc)
    g_last = jnp.sum(g, axis=0, keepdims=True)
    k_tail = k * jnp.exp(g_last - gc)
    v_new = u - _smm(wk, s, p)
    o = _smm(qg, s, p) + _smm(attn, v_new, p)
    s_new = s * jnp.exp(g_last) + _smm_tn(k_tail, v_new, p)
    return o, s_new


RWKV_PASSES = 1
RWKV_INV_PASSES = 1


def _rwkv_chunk(st, r, k, v, a, b, lw):
    c = CHUNK
    p, pi = RWKV_PASSES, RWKV_INV_PASSES
    _, strict, _, _ = _tri_masks(c)
    lane = lax.broadcasted_iota(jnp.int32, (c, 2 * B_N), 1)
    row = lax.broadcasted_iota(jnp.int32, (c, 2 * B_N), 0)
    first = lane < B_N
    incl2 = row >= jnp.where(first, lane, lane - B_N)
    bi = lax.broadcasted_iota(jnp.int32, (2 * B_N, 2 * B_N), 0) < B_N
    bj = lax.broadcasted_iota(jnp.int32, (2 * B_N, 2 * B_N), 1) < B_N
    blockdiag = bi == bj
    cum = _cumsum_rows(lw)
    e_pos = jnp.exp(cum)
    e_neg = jnp.exp(-cum)
    rt = r * e_pos
    at = a * jnp.exp(cum - lw)
    kt = k * e_neg
    bt = b * e_neg
    bk = jnp.concatenate([bt, kt], axis=0)
    a_s0 = _smm_nt(at, st, p)
    r_s0 = _smm_nt(rt, st, p)
    heads = (first, jnp.logical_not(first))
    u = jnp.zeros((c, 2 * B_N), f32)
    for sel in heads:
        at_h = jnp.where(sel, at, 0.0)
        ab = jnp.where(strict, _smm_nt(at_h, bt, pi), 0.0)
        ak = jnp.where(strict, _smm_nt(at_h, kt, p), 0.0)
        t_h = _unit_lower_inv(ab, pi)
        u = u + _smm(t_h, jnp.where(sel, a_s0, 0.0) + _smm(ak, jnp.where(sel, v, 0.0), p), p)
    y = r_s0
    for sel in heads:
        rbk = jnp.where(incl2, _smm_nt(jnp.where(sel, rt, 0.0), bk, p), 0.0)
        uv = jnp.concatenate([jnp.where(sel, u, 0.0), jnp.where(sel, v, 0.0)], axis=0)
        y = y + _smm(rbk, uv, p)
    cl = jnp.sum(lw, axis=0, keepdims=True)
    dec = jnp.exp(cl - cum)
    uv_all = jnp.concatenate([u, v], axis=0)
    bk_dec = jnp.concatenate([b * dec, k * dec], axis=0)
    st_new = st * jnp.exp(cl) + jnp.where(blockdiag, _smm_tn(uv_all, bk_dec, p), 0.0)
    return y, st_new


GROUPS_PER_STEP = 8


def _scan_specs(ins, col_offs, n_chunks, reverse):
    gw = GROUPS_PER_STEP * LANES
    cidx = (lambda c: n_chunks - 1 - c) if reverse else (lambda c: c)
    specs = []
    for a, off in zip(ins, col_offs):
        if a.ndim == 2:
            assert off % gw == 0
            specs.append(pl.BlockSpec((CHUNK, gw), lambda h, c, o=off // gw: (cidx(c), h + o)))
        else:
            specs.append(pl.BlockSpec((GROUPS_PER_STEP, CHUNK, 1), lambda h, c: (h, cidx(c), 0)))
    return specs, cidx


def _group_vals(refs, g):
    return [r[:, g * LANES:(g + 1) * LANES] if len(r.shape) == 2 else r[g] for r in refs]


def _scan_fwd(chunk_fn, ins, col_offs, n_groups, n_chunks, state_shape, name):
    n_in = len(ins)
    gps = GROUPS_PER_STEP
    t = ins[0].shape[0]

    def body(*refs):
        in_refs = refs[:n_in]
        o_ref, s0_ref, st = refs[n_in:]

        @pl.when(pl.program_id(1) == 0)
        def _():
            st[...] = jnp.zeros_like(st)

        states = st[...]
        vals = [jnp.stack(col) for col in zip(*[_group_vals(in_refs, g) for g in range(gps)])]
        o, s_new = jax.vmap(chunk_fn)(states, *vals)
        s0_ref[...] = states
        st[...] = s_new
        for g in range(gps):
            o_ref[:, g * LANES:(g + 1) * LANES] = o[g]

    specs, _ = _scan_specs(ins, col_offs, n_chunks, False)
    return pl.pallas_call(
        body, name=name, grid=(n_groups // gps, n_chunks), in_specs=specs,
        out_specs=[pl.BlockSpec((CHUNK, gps * LANES), lambda h, c: (c, h)),
                   pl.BlockSpec((gps, None) + state_shape, lambda h, c: (h, c, 0, 0))],
        out_shape=[jax.ShapeDtypeStruct((t, n_groups * LANES), f32),
                   jax.ShapeDtypeStruct((n_groups, n_chunks) + state_shape, f32)],
        scratch_shapes=[pltpu.VMEM((gps,) + state_shape, f32)],
        compiler_params=pltpu.CompilerParams(dimension_semantics=("parallel", "arbitrary")),
    )(*ins)


def _scan_bwd(chunk_fn, s0s, ins, col_offs, d_out, n_groups, n_chunks, state_shape, name):
    n_in = len(ins)
    gps = GROUPS_PER_STEP
    t = d_out.shape[0]

    def body(*refs):
        s0_ref = refs[0]
        in_refs = refs[1:1 + n_in]
        do_ref = refs[1 + n_in]
        g_refs = refs[2 + n_in:2 + 2 * n_in]
        dst = refs[2 + 2 * n_in]

        @pl.when(pl.program_id(1) == 0)
        def _():
            dst[...] = jnp.zeros_like(dst)

        vals = [jnp.stack(col) for col in zip(*[_group_vals(in_refs, g) for g in range(gps)])]
        d_o = jnp.stack([do_ref[:, g * LANES:(g + 1) * LANES] for g in range(gps)])
        _, vjp = jax.vjp(jax.vmap(chunk_fn), s0_ref[...], *vals)
        gs = vjp((d_o, dst[...]))
        dst[...] = gs[0]
        for g_ref, gv in zip(g_refs, gs[1:]):
            if len(g_ref.shape) == 2:
                for g in range(gps):
                    g_ref[:, g * LANES:(g + 1) * LANES] = gv[g]
            else:
                g_ref[...] = gv

    specs, cidx = _scan_specs(ins, col_offs, n_chunks, True)
    out_lane = pl.BlockSpec((CHUNK, gps * LANES), lambda h, c: (cidx(c), h))
    g_specs = [out_lane if a.ndim == 2 else sp for a, sp in zip(ins, specs)]
    g_shapes = [(t, n_groups * LANES) if a.ndim == 2 else a.shape for a in ins]
    s0_spec = pl.BlockSpec((gps, None) + state_shape, lambda h, c: (h, cidx(c), 0, 0))
    return pl.pallas_call(
        body, name=name, grid=(n_groups // gps, n_chunks), in_specs=[s0_spec] + specs + [out_lane],
        out_specs=g_specs, out_shape=[jax.ShapeDtypeStruct(sh, f32) for sh in g_shapes],
        scratch_shapes=[pltpu.VMEM((gps,) + state_shape, f32)],
        compiler_params=pltpu.CompilerParams(dimension_semantics=("parallel", "arbitrary")),
    )(s0s, *ins, d_out)


def _rms_fwd(x, g, name):
    t = x.shape[0]
    tm = _tile(t, 416, 16)
    return _tw_fwd(_f_rms, [x, g], [_row_spec(tm, D), _full_spec(g.shape)],
                   [jax.ShapeDtypeStruct(x.shape, MXU_DTYPE)], [_row_spec(tm, D)], (t // tm,), name)[0]


def _rms_bwd(x, g, dy, residual, name):
    t = x.shape[0]
    tm = _tile(t, 416, 8)
    return _tw_bwd(_f_rms, [x, g], [_row_spec(tm, D), _full_spec(g.shape)], [dy], [_row_spec(tm, D)],
                   ['tile', 'acc'], (t // tm,), name, residual=residual)


def _ffn_fwd(h, gain, wgu, wd, tag):
    xn = _rms_fwd(h, gain, f"{tag}_rms")
    gate = _matmul(xn, wgu, b_cols_split=(0, 2), name=f"{tag}_gate")
    up = _matmul(xn, wgu, b_cols_split=(2, 2), name=f"{tag}_up")
    t = h.shape[0]
    tm = _tile(t, 208, 16)
    act = _tw_fwd(_f_swiglu, [gate, up], [_row_spec(tm, D_FF)] * 2, [jax.ShapeDtypeStruct((t, D_FF), MXU_DTYPE)],
                  [_row_spec(tm, D_FF)], (t // tm,), f"{tag}_act")[0]
    out = _matmul(act, wd, res=h, scale=0.5, name=f"{tag}_down")
    return out, (xn, gate, up, act)


def _ffn_bwd(h, gain, wgu, wd, saved, dout, tag):
    xn, gate, up, act = saved
    t = h.shape[0]
    d_wd = _matmul(act, dout, ta=True, scale=0.5, name=f"{tag}_dwd")
    d_act = _matmul(dout, wd, tb=True, scale=0.5, name=f"{tag}_dact")
    tm = _tile(t, 208, 16)
    d_gate, d_up = _tw_bwd(_f_swiglu, [gate, up], [_row_spec(tm, D_FF)] * 2, [d_act], [_row_spec(tm, D_FF)],
                           ['tile', 'tile'], (t // tm,), f"{tag}_dactf", tile_dtype=MXU_DTYPE)
    d_wg = _matmul(xn, d_gate, ta=True, out_cols_split=True, name=f"{tag}_dwg")
    d_wu = _matmul(xn, d_up, ta=True, out_cols_split=True, name=f"{tag}_dwu")
    d_xn = _matmul(d_gate, wgu, tb=True, b_cols_split=(0, 2), name=f"{tag}_dxn_g")
    d_xn = _matmul(d_up, wgu, tb=True, b_cols_split=(2, 2), res=d_xn, name=f"{tag}_dxn_u")
    d_h, d_gain = _rms_bwd(h, gain, d_xn, dout, f"{tag}_drms")
    return d_h, d_gain, d_wg, d_wu, d_wd


def _col_spec(t, first_block):
    return pl.BlockSpec((t, LANES), lambda j, fb=first_block: (0, j + fb))


def _local_step(h0, tgt, w):
    t = h0.shape[0]
    assert t % CHUNK == 0
    nc = t // CHUNK
    grads = {}

    h1, ffn1_saved = _ffn_fwd(h0, w['ffn1_norm'], w['ffn1_wgu'], w['ffn1_wd'], "ffn1")
    u = _rms_fwd(h1, w['mix_norm'], "mix_rms")
    z = _matmul(u, w['w_in_p'], name="in_proj")
    zs = z[:, 9216:9216 + 304]
    abeta, aalpha = zs[:, 288:296], zs[:, 296:304]

    conv_w = w['a_conv_w']
    conv_fns = [functools.partial(_f_conv, norm=True, scale=A_DK ** -0.5),
                functools.partial(_f_conv, norm=True, scale=1.0),
                functools.partial(_f_conv, norm=False, scale=1.0)]
    qkv = []
    for idx, fn in enumerate(conv_fns):
        qkv.append(_tw_fwd(fn, [z, conv_w], [_col_spec(t, 8 * idx), pl.BlockSpec((4, LANES), lambda j, o=8 * idx: (0, j + o))],
                           [jax.ShapeDtypeStruct((t, D), f32)], [_col_spec(t, 0)], (A_HEADS,), f"a_conv{idx}")[0])
    aq, ak, av = qkv
    tmg = _tile(t, 1040, 8)
    dg_fn = functools.partial(_f_dgates, tm=tmg)
    dg_specs = [_row_spec(tmg, A_HEADS)] * 2 + [_full_spec((1, A_HEADS))] * 2
    beta, gdec = _tw_fwd(dg_fn, [abeta, aalpha, w['a_log_rate'], w['a_dt_bias']], dg_specs,
                         [jax.ShapeDtypeStruct((t, A_HEADS), f32)] * 2, [_row_spec(tmg, A_HEADS)] * 2, (t // tmg,),
                         "a_gates", with_pid=True)
    beta_h = beta.T[:, :, None]
    gdec_h = gdec.T[:, :, None]
    a_ins = [aq, ak, av, beta_h, gdec_h]
    a_offs = [0] * 5
    o_scan, a_s0 = _scan_fwd(_delta_chunk, a_ins, a_offs, A_HEADS, nc, (A_DK, A_DK), "a_scan")

    mu = w['b_shift_mu']
    mu_rkv, mu_s = mu[:, :3072], mu[:, 3072:]
    zf_rkv = _tw_fwd(_f_tshift, [z, mu_rkv], [_col_spec(t, 32), pl.BlockSpec((1, LANES), lambda j: (0, j))],
                     [jax.ShapeDtypeStruct((t, 3072), f32)], [_col_spec(t, 0)], (24,), "b_shift")[0]
    zs_b = zs[:, :288]
    zf_s = _tw_fwd(_f_tshift, [zs_b, mu_s], [_full_spec((t, 288)), _full_spec((1, 288))],
                   [jax.ShapeDtypeStruct((t, 288), f32)], [_full_spec((t, 288))], (1,), "b_shift_s")[0]
    wdf, adf, gdf = zf_s[:, 0:64], zf_s[:, 64:128], zf_s[:, 128:288]
    tmr = _tile(t, 160, 16)
    pre_params = [w['b_w0'], w['b_w_up'], w['b_a0'], w['b_a_up'], w['b_g_up'], w['b_k_k'], w['b_k_a']]
    pre_ins = [zf_rkv, wdf, adf, gdf] + pre_params
    pre_specs = ([_row_spec(tmr, D, 1), _row_spec(tmr, 64), _row_spec(tmr, 64), _row_spec(tmr, 160)]
                 + [_full_spec(p.shape) for p in pre_params])
    lw, kmod, a_s, b_s, bgate = _tw_fwd(_f_rwkv_pre, pre_ins, pre_specs, [jax.ShapeDtypeStruct((t, D), f32)] * 5,
                                        [_row_spec(tmr, D)] * 5, (t // tmr,), "b_pre")
    b_ins = [zf_rkv, kmod, zf_rkv, a_s, b_s, lw]
    b_offs = [0, 0, 2 * D, 0, 0, 0]
    y_scan, b_s0 = _scan_fwd(_rwkv_chunk, b_ins, b_offs, B_HEADS // 2, nc, (2 * B_N, 2 * B_N), "b_scan")

    out_gain_t = jnp.tile(w['a_out_norm'], (1, A_HEADS))
    r_k = w['b_r_k'].reshape(1, D)
    post_params = [out_gain_t, w['b_ln_gain'], w['b_ln_bias'], r_k]
    post_ins = [o_scan, z, y_scan, zf_rkv, kmod, zf_rkv, bgate, z, z] + post_params
    post_specs = ([_row_spec(tmr, D), _row_spec(tmr, D, 3), _row_spec(tmr, D), _row_spec(tmr, D, 0), _row_spec(tmr, D),
                   _row_spec(tmr, D, 2), _row_spec(tmr, D), _row_spec(tmr, D, 7), _row_spec(tmr, D, 8)]
                  + [_full_spec((1, D))] * 4)
    merged = _tw_fwd(_f_mix_post, post_ins, post_specs, [jax.ShapeDtypeStruct((t, D), MXU_DTYPE)],
                     [_row_spec(tmr, D)], (t // tmr,), "mix_post")[0]
    h2 = _matmul(merged, w['w_out'], res=h1, name="out_proj")
    h3, ffn2_saved = _ffn_fwd(h2, w['ffn2_norm'], w['ffn2_wgu'], w['ffn2_wd'], "ffn2")

    tml = _tile(t, 416, 8)
    fnorm = w['final_norm']
    loss_fn = functools.partial(_f_loss, tm=tml)
    loss_specs = [_row_spec(tml, D), _full_spec((1, D)), _row_spec(tml, D)]
    loss_parts = _tw_fwd(loss_fn, [h3, fnorm, tgt], loss_specs, [jax.ShapeDtypeStruct((t // tml, 1, 1), f32)],
                         [pl.BlockSpec((None, 1, 1), lambda i: (i, 0, 0))], (t // tml,), "loss", with_pid=True)[0]
    loss = jnp.sum(loss_parts)
    ones = jnp.ones((t // tml, 1, 1), f32)
    d_h3, grads['final_norm'] = _tw_bwd(loss_fn, [h3, fnorm, tgt], loss_specs, [ones],
                                        [pl.BlockSpec((None, 1, 1), lambda i: (i, 0, 0))], ['tile', 'acc', None],
                                        (t // tml,), "loss_bwd", with_pid=True)

    d_h2, grads['ffn2_norm'], grads['ffn2_wg'], grads['ffn2_wu'], grads['ffn2_wd'] = _ffn_bwd(
        h2, w['ffn2_norm'], w['ffn2_wgu'], w['ffn2_wd'], ffn2_saved, d_h3, "ffn2")
    grads['w_out'] = _matmul(merged, d_h2, ta=True, name="d_w_out")
    d_merged = _matmul(d_h2, w['w_out'], tb=True, name="d_merged")

    win = ('tile', (t, D), _row_spec(tmr, D))
    post_kinds = ['tile', win, 'tile', win, 'tile', win, 'tile', win, win] + ['acc'] * 4
    (d_o, d_az, d_y, d_r1, d_kmod1, d_v1, d_bgate, d_ga, d_gb,
     d_out_gain_t, grads['b_ln_gain'], grads['b_ln_bias'], d_r_k) = _tw_bwd(
        _f_mix_post, post_ins, post_specs, [d_merged], [_row_spec(tmr, D)], post_kinds, (t // tmr,), "mix_post_bwd")
    grads['a_out_norm'] = jnp.sum(d_out_gain_t.reshape(A_HEADS, A_DK), axis=0, keepdims=True)
    grads['b_r_k'] = d_r_k.reshape(1, B_HEADS, B_N)

    d_r2, d_kmod2, d_v2, d_as, d_bs, d_lw = _scan_bwd(_rwkv_chunk, b_s0, b_ins, b_offs, d_y, B_HEADS // 2, nc,
                                                      (2 * B_N, 2 * B_N), "b_scan_bwd")
    pre_kinds = [win] + ['tile'] * 3 + ['acc'] * 7
    pre_ct_specs = [_row_spec(tmr, D)] * 5
    (d_zf_k, d_wdf, d_adf, d_gdf, grads['b_w0'], grads['b_w_up'], grads['b_a0'], grads['b_a_up'], grads['b_g_up'],
     grads['b_k_k'], grads['b_k_a']) = _tw_bwd(
        _f_rwkv_pre, pre_ins, pre_specs, [d_lw, d_kmod1, d_as, d_bs, d_bgate], pre_ct_specs, pre_kinds, (t // tmr,),
        "b_pre_bwd", ct_extra=[(1, d_kmod2)])
    d_zf_rkv = _assemble3(d_r1, d_r2, d_zf_k, d_v1, d_v2, "b_dzf")
    d_zb_rkv, d_mu_rkv = _tw_bwd(_f_tshift, [z, mu_rkv], [_col_spec(t, 32), pl.BlockSpec((1, LANES), lambda j: (0, j))],
                                 [d_zf_rkv], [_col_spec(t, 0)], [('tile', (t, 3072), _col_spec(t, 0)), 'tile'], (24,),
                                 "b_shift_bwd")
    d_zf_s = jnp.concatenate([d_wdf, d_adf, d_gdf], axis=1)
    d_zs_b, d_mu_s = _tw_bwd(_f_tshift, [zs_b, mu_s], [_full_spec((t, 288)), _full_spec((1, 288))], [d_zf_s],
                             [_full_spec((t, 288))], ['tile', 'tile'], (1,), "b_shift_s_bwd")
    grads['b_shift_mu'] = jnp.concatenate([d_mu_rkv, d_mu_s], axis=1)

    d_aq, d_ak, d_av, d_beta_h, d_g_h = _scan_bwd(_delta_chunk, a_s0, a_ins, a_offs, d_o, A_HEADS, nc, (A_DK, A_DK),
                                                  "a_scan_bwd")
    d_beta = d_beta_h[:, :, 0].T
    d_gdec = d_g_h[:, :, 0].T
    d_abeta, d_aalpha, grads['a_log_rate'], grads['a_dt_bias'] = _tw_bwd(
        dg_fn, [abeta, aalpha, w['a_log_rate'], w['a_dt_bias']], dg_specs, [d_beta, d_gdec],
        [_row_spec(tmg, A_HEADS)] * 2, ['tile', 'tile', 'acc', 'acc'], (t // tmg,), "a_gates_bwd", with_pid=True)
    d_zqkv, d_conv = [], []
    for idx, (fn, ct) in enumerate(zip(conv_fns, (d_aq, d_ak, d_av))):
        dz_i, dw_i = _conv_bwd(fn, z, conv_w, ct, idx, t)
        d_zqkv.append(dz_i)
        d_conv.append(dw_i)
    grads['a_conv_w'] = jnp.concatenate(d_conv, axis=1)

    d_z_parts = d_zqkv + [d_az, d_zb_rkv, d_ga, d_gb, d_zs_b, d_abeta, d_aalpha, jnp.zeros((t, ZP - 9216 - 304), f32)]
    d_z = jnp.concatenate([p.astype(MXU_DTYPE) for p in d_z_parts], axis=1)
    grads['w_in_p'] = _matmul(u, d_z, ta=True, name="d_w_in")
    d_u = _matmul(d_z, w['w_in_p'], tb=True, name="d_u")
    d_h1, grads['mix_norm'] = _rms_bwd(h1, w['mix_norm'], d_u, d_h2, "mix_drms")
    d_h0, grads['ffn1_norm'], grads['ffn1_wg'], grads['ffn1_wu'], grads['ffn1_wd'] = _ffn_bwd(
        h0, w['ffn1_norm'], w['ffn1_wgu'], w['ffn1_wd'], ffn1_saved, d_h1, "ffn1")
    return loss, d_h0, grads


_WIN_SEGMENTS = ((0, 4096), (4112, 7184), (7472, 9520), (7184, 7472), (4096, 4112))


def _win_to_padded(w_in):
    parts = [w_in[:, a:b] for a, b in _WIN_SEGMENTS]
    parts.append(jnp.zeros((w_in.shape[0], ZP - IN_TOTAL), w_in.dtype))
    return jnp.concatenate(parts, axis=1)


def _win_from_padded(w_p):
    widths = [b - a for a, b in _WIN_SEGMENTS]
    offs = [sum(widths[:i]) for i in range(len(widths))]
    seg = {a: w_p[:, o:o + wd] for (a, _), o, wd in zip(_WIN_SEGMENTS, offs, widths)}
    return jnp.concatenate([seg[a] for a in sorted(seg)], axis=1)


def _assemble3(d_r1, d_r2, d_k, d_v1, d_v2, name):
    t = d_r1.shape[0]
    tm = _tile(t, 208, 8)

    def body(r1, r2, kk, v1, v2, o_ref):
        o_ref[:, 0:D] = r1[...] + r2[...]
        o_ref[:, D:2 * D] = kk[...]
        o_ref[:, 2 * D:3 * D] = v1[...] + v2[...]

    return pl.pallas_call(body, name=name, grid=(t // tm,), in_specs=[_row_spec(tm, D)] * 5,
                          out_specs=_row_spec(tm, 3 * D), out_shape=jax.ShapeDtypeStruct((t, 3 * D), f32),
                          )(d_r1, d_r2, d_k, d_v1, d_v2)


def _conv_bwd(fn, z, conv_w, ct, idx, t):
    def body(z_ref, w_ref, ct_ref, dz_ref, dw_ref):
        _, vjp = jax.vjp(lambda a, b: fn(a, b), z_ref[...], w_ref[...])
        dz, dw = vjp((ct_ref[...],))
        dz_ref[...] = dz
        dw_ref[...] = dw

    return pl.pallas_call(
        body, name=f"a_conv{idx}_bwd", grid=(A_HEADS,),
        in_specs=[_col_spec(t, 8 * idx), pl.BlockSpec((4, LANES), lambda j, o=8 * idx: (0, j + o)), _col_spec(t, 0)],
        out_specs=[_col_spec(t, 0), pl.BlockSpec((4, LANES), lambda j: (0, j))],
        out_shape=[jax.ShapeDtypeStruct((t, D), f32), jax.ShapeDtypeStruct((4, D), f32)],
    )(z, conv_w, ct)


def _position():
    return lax.axis_index("x"), lax.axis_index("y"), lax.axis_index("c")


def _flip(v, f):
    return 1 - v if f else v


_CHIP_FLIPS = ((1, 0), (0, 1), (1, 1))
_DEV_FLIPS = tuple((fx, fy, fc) for fx in (0, 1) for fy in (0, 1) for fc in (0, 1) if (fx, fy, fc) != (0, 0, 0))


def _gather_chips(arrs, name):
    n = len(arrs)
    assert all(a.shape[0] % 32 == 0 for a in arrs)
    arrs = [a.reshape(2, a.shape[0] // 2, a.shape[1]) for a in arrs]

    def body(*refs):
        ins, outs = refs[:n], refs[n:2 * n]
        send, recv, fsend, frecv, own = refs[2 * n:]
        x, y, c = _position()
        me = 2 * x + y
        sends, plan, owns = [], [], []
        for a in range(n):
            cp = pltpu.make_async_remote_copy(src_ref=ins[a], dst_ref=outs[a].at[me], send_sem=own.at[a, 0],
                                              recv_sem=own.at[a, 1], device_id=(x, y, 1 - c), device_id_type=MESH)
            cp.start()
            owns.append(cp)
            for j, (fx, fy) in enumerate(_CHIP_FLIPS):
                px, py = _flip(x, fx), _flip(y, fy)
                p = 2 * px + py
                cp = pltpu.make_async_remote_copy(src_ref=ins[a].at[c], dst_ref=outs[a].at[me, c],
                                                  send_sem=send.at[a, j], recv_sem=recv.at[a, j],
                                                  device_id=(px, py, c), device_id_type=MESH)
                cp.start()
                sends.append(cp)
                landed = pltpu.make_async_remote_copy(src_ref=ins[a].at[c], dst_ref=outs[a].at[p, c],
                                                      send_sem=send.at[a, j], recv_sem=recv.at[a, j],
                                                      device_id=(px, py, c), device_id_type=MESH)
                onward = pltpu.make_async_remote_copy(src_ref=outs[a].at[p, c], dst_ref=outs[a].at[p, c],
                                                      send_sem=fsend.at[a, j], recv_sem=frecv.at[a, j],
                                                      device_id=(x, y, 1 - c), device_id_type=MESH)
                from_sibling = pltpu.make_async_remote_copy(src_ref=outs[a].at[p, 1 - c], dst_ref=outs[a].at[p, 1 - c],
                                                            send_sem=fsend.at[a, j], recv_sem=frecv.at[a, j],
                                                            device_id=(x, y, 1 - c), device_id_type=MESH)
                plan.append((landed, onward, from_sibling))
        for landed, onward, _ in plan:
            landed.wait_recv()
            onward.start()
        for _, _, from_sibling in plan:
            from_sibling.wait_recv()
        for cp in sends:
            cp.wait_send()
        for _, onward, _ in plan:
            onward.wait_send()
        for cp in owns:
            cp.wait()

    sems = [pltpu.SemaphoreType.DMA((n, 3))] * 4 + [pltpu.SemaphoreType.DMA((n, 2))]
    outs = pl.pallas_call(
        body, name=name, in_specs=[ANY] * n, out_specs=[ANY] * n,
        out_shape=[jax.ShapeDtypeStruct((N_CHIPS,) + a.shape, a.dtype) for a in arrs], scratch_shapes=sems,
    )(*arrs)
    return [o.reshape(N_CHIPS, o.shape[1] * o.shape[2], o.shape[3]) for o in outs]


def _swap_sibling(src_of, shape, dtype, name):
    def body(a_ref, got_ref, send, recv):
        x, y, c = _position()
        cp = pltpu.make_async_remote_copy(src_ref=src_of(a_ref, c), dst_ref=got_ref, send_sem=send, recv_sem=recv,
                                          device_id=(x, y, 1 - c), device_id_type=MESH)
        cp.start()
        cp.wait()

    def call(a):
        return pl.pallas_call(body, name=name, in_specs=[ANY], out_specs=ANY,
                              out_shape=jax.ShapeDtypeStruct(shape, dtype),
                              scratch_shapes=[pltpu.SemaphoreType.DMA(())] * 2)(a)
    return call


def _add_halves(g, got, dtype, name):
    _, n, hr, w = g.shape
    tr = _tile(hr, 784, 16)

    def body(g_ref, got_ref, o_ref):
        c = lax.axis_index("c")
        own = jnp.where(c == 0, g_ref[0], g_ref[1])
        o_ref[...] = (own + got_ref[...]).astype(dtype)

    return pl.pallas_call(
        body, name=name, grid=(hr // tr,),
        in_specs=[pl.BlockSpec((2, n, tr, w), lambda i: (0, 0, i, 0)), pl.BlockSpec((n, tr, w), lambda i: (0, i, 0))],
        out_specs=pl.BlockSpec((n, tr, w), lambda i: (0, i, 0)),
        out_shape=jax.ShapeDtypeStruct((n, hr, w), dtype))(g, got)


def _scatter_chips(g, name):
    def body(g_ref, out_ref, send, recv):
        x, y, c = _position()
        sends = []
        for j, (fx, fy) in enumerate(_CHIP_FLIPS):
            px, py = _flip(x, fx), _flip(y, fy)
            cp = pltpu.make_async_remote_copy(src_ref=g_ref.at[2 * px + py], dst_ref=out_ref.at[j], send_sem=send.at[j],
                                              recv_sem=recv.at[j], device_id=(px, py, c), device_id_type=MESH)
            cp.start()
            sends.append(cp)
        for cp in sends:
            cp.wait_recv()
        for cp in sends:
            cp.wait_send()

    return pl.pallas_call(
        body, name=name, in_specs=[ANY], out_specs=ANY, out_shape=jax.ShapeDtypeStruct((3,) + g.shape[1:], g.dtype),
        scratch_shapes=[pltpu.SemaphoreType.DMA((3,)), pltpu.SemaphoreType.DMA((3,))],
    )(g)


def _sum_own_and_slots(own, got, name):
    n, r, w = own.shape
    tr = _tile(r, 784, 16)

    def body(own_ref, got_ref, o_ref):
        me = 2 * lax.axis_index("x") + lax.axis_index("y")
        acc = own_ref[0]
        for i in range(1, n):
            acc = jnp.where(me == i, own_ref[i], acc)
        acc = acc.astype(f32)
        for j in range(3):
            acc = acc + got_ref[j].astype(f32)
        o_ref[...] = acc

    return pl.pallas_call(
        body, name=name, grid=(r // tr,),
        in_specs=[pl.BlockSpec((n, tr, w), lambda i: (0, i, 0)), pl.BlockSpec((3, tr, w), lambda i: (0, i, 0))],
        out_specs=pl.BlockSpec((tr, w), lambda i: (i, 0)), out_shape=jax.ShapeDtypeStruct((r, w), f32))(own, got)


def _gather_devices(s, name):
    def body(s_ref, out_ref, send, recv, loc):
        x, y, c = _position()
        me = 4 * x + 2 * y + c
        lc = pltpu.make_async_copy(s_ref, out_ref.at[me], loc)
        lc.start()
        sends, recvs = [], []
        for j, (fx, fy, fc) in enumerate(_DEV_FLIPS):
            px, py, pc = _flip(x, fx), _flip(y, fy), _flip(c, fc)
            cp = pltpu.make_async_remote_copy(src_ref=s_ref, dst_ref=out_ref.at[me], send_sem=send.at[j],
                                              recv_sem=recv.at[j], device_id=(px, py, pc), device_id_type=MESH)
            cp.start()
            sends.append(cp)
            recvs.append(pltpu.make_async_remote_copy(
                src_ref=s_ref, dst_ref=out_ref.at[4 * px + 2 * py + pc], send_sem=send.at[j], recv_sem=recv.at[j],
                device_id=(px, py, pc), device_id_type=MESH))
        for cp in recvs:
            cp.wait_recv()
        for cp in sends:
            cp.wait_send()
        lc.wait()

    return pl.pallas_call(
        body, name=name, in_specs=[ANY], out_specs=ANY, out_shape=jax.ShapeDtypeStruct((N_DEV,) + s.shape, s.dtype),
        scratch_shapes=[pltpu.SemaphoreType.DMA((7,)), pltpu.SemaphoreType.DMA((7,)), pltpu.SemaphoreType.DMA(())],
    )(s)


def _sum_slots(a, name):
    s, r, c = a.shape
    tr = _tile(r, 2048, 16)

    def body(a_ref, o_ref):
        acc = a_ref[0].astype(f32)
        for i in range(1, s):
            acc = acc + a_ref[i].astype(f32)
        o_ref[...] = acc

    return pl.pallas_call(body, name=name, grid=(r // tr,), in_specs=[pl.BlockSpec((s, tr, c), lambda i: (0, i, 0))],
                          out_specs=pl.BlockSpec((tr, c), lambda i: (i, 0)),
                          out_shape=jax.ShapeDtypeStruct((r, c), f32))(a)


def _adamw(w, g_parts, m, v, name):
    shape = w.shape
    size = w.size
    view = (size // LANES, LANES) if size % LANES == 0 else (1, size)
    rows = view[0]
    tr = _tile(rows, 2048, 8) if rows > 2048 else rows
    n_g = len(g_parts)

    def body(*refs):
        w_ref = refs[0]
        g_refs = refs[1:1 + n_g]
        m_ref, v_ref, g_out, d_out, m_out, v_out = refs[1 + n_g:]
        g = g_refs[0][...]
        for gr in g_refs[1:]:
            g = g + gr[...]
        m_new = ADAM_B1 * m_ref[...] + (1.0 - ADAM_B1) * g
        v_new = ADAM_B2 * v_ref[...] + (1.0 - ADAM_B2) * (g * g)
        m_hat = m_new / (1.0 - ADAM_B1 ** ADAM_STEP)
        v_hat = v_new / (1.0 - ADAM_B2 ** ADAM_STEP)
        g_out[...] = g
        d_out[...] = -ADAM_LR * (m_hat / (jnp.sqrt(v_hat) + ADAM_EPS) + ADAM_WD * w_ref[...])
        m_out[...] = m_new
        v_out[...] = v_new

    spec = pl.BlockSpec((tr, view[1]), lambda i: (i, 0))
    args = [w.reshape(view)] + [g.reshape(view) for g in g_parts] + [m.reshape(view), v.reshape(view)]
    outs = pl.pallas_call(body, name=name, grid=(rows // tr,), in_specs=[spec] * len(args), out_specs=[spec] * 4,
                          out_shape=[jax.ShapeDtypeStruct(view, f32)] * 4)(*args)
    return [o.reshape(shape) for o in outs]


_BIG = ('ffn1_w_gu', 'ffn1_w_down', 'w_in', 'w_out', 'ffn2_w_gu', 'ffn2_w_down')
_SMALL_SHARDED = ('meta_tokens', 'a_conv_w', 'b_w_up', 'b_a_up', 'b_g_up')
_WEIGHTS = ('meta_tokens', 'ffn1_norm', 'ffn1_w_gu', 'ffn1_w_down', 'mix_norm', 'w_in', 'a_conv_w', 'a_log_rate',
            'a_dt_bias', 'a_out_norm', 'b_shift_mu', 'b_w0', 'b_w_up', 'b_a0', 'b_a_up', 'b_g_up', 'b_k_k', 'b_k_a',
            'b_r_k', 'b_ln_gain', 'b_ln_bias', 'w_out', 'ffn2_norm', 'ffn2_w_gu', 'ffn2_w_down', 'final_norm')
_SMALL = tuple(n for n in _WEIGHTS if n not in _BIG)


def _rows_of(shape):
    n = 1
    for d in shape:
        n *= d
    return n, -(-n // LANES)


def _pack(arrs, dtype, row_mult=32):
    parts, total = [], 0
    for a in arrs:
        n, rows = _rows_of(a.shape)
        flat = a.reshape(-1).astype(dtype)
        if n % LANES:
            flat = jnp.pad(flat, (0, rows * LANES - n))
        parts.append(flat)
        total += rows
    extra = -total % row_mult
    if extra:
        parts.append(jnp.zeros((extra * LANES,), dtype))
    return jnp.concatenate(parts).reshape(total + extra, LANES)


def _unpack(packed, shapes, lead=()):
    out, off = [], 0
    for sh in shapes:
        n, rows = _rows_of(sh)
        seg = packed[..., off:off + rows, :]
        if n % LANES:
            seg = seg.reshape(lead + (-1,))[..., :n]
        out.append(seg.reshape(lead + tuple(sh)))
        off += rows
    return out


def _cols_from_shards(s):
    return jnp.concatenate([s[i] for i in range(N_CHIPS)], axis=-1)


def _cols_to_shards(a):
    r, c = a.shape
    return a.reshape(r, N_CHIPS, c // N_CHIPS).transpose(1, 0, 2)


def kernel(x, meta_tokens, ffn1_norm, ffn1_w_gu, ffn1_w_down, mix_norm, w_in, a_conv_w, a_log_rate, a_dt_bias, a_out_norm, b_shift_mu, b_w0, b_w_up, b_a0, b_a_up, b_g_up, b_k_k, b_k_a, b_r_k, b_ln_gain, b_ln_bias, w_out, ffn2_norm, ffn2_w_gu, ffn2_w_down, final_norm, loss_target, m_meta_tokens, m_ffn1_norm, m_ffn1_w_gu, m_ffn1_w_down, m_mix_norm, m_w_in, m_a_conv_w, m_a_log_rate, m_a_dt_bias, m_a_out_norm, m_b_shift_mu, m_b_w0, m_b_w_up, m_b_a0, m_b_a_up, m_b_g_up, m_b_k_k, m_b_k_a, m_b_r_k, m_b_ln_gain, m_b_ln_bias, m_w_out, m_ffn2_norm, m_ffn2_w_gu, m_ffn2_w_down, m_final_norm, v_meta_tokens, v_ffn1_norm, v_ffn1_w_gu, v_ffn1_w_down, v_mix_norm, v_w_in, v_a_conv_w, v_a_log_rate, v_a_dt_bias, v_a_out_norm, v_b_shift_mu, v_b_w0, v_b_w_up, v_b_a0, v_b_a_up, v_b_g_up, v_b_k_k, v_b_k_a, v_b_r_k, v_b_ln_gain, v_b_ln_bias, v_w_out, v_ffn2_norm, v_ffn2_w_gu, v_ffn2_w_down, v_final_norm):
    args = locals()
    wts = {n: args[n] for n in _WEIGHTS}
    mom = {n: args["m_" + n] for n in _WEIGHTS}
    var = {n: args["v_" + n] for n in _WEIGHTS}
    chip = 2 * lax.axis_index("x") + lax.axis_index("y")

    big_shapes = [wts[n].shape[1:] for n in _BIG]
    small_shapes = [wts[n].shape[-2:] for n in _SMALL_SHARDED]
    big_flat = [wts[n].astype(bf16).reshape(-1, LANES) for n in _BIG]
    small_packed = _pack([wts[n] for n in _SMALL_SHARDED], f32)
    gathered = _gather_chips(big_flat + [small_packed], "gather_weights")
    gu1, dn1, w_in_s, w_out_s, gu2, dn2 = [a.reshape((N_CHIPS,) + tuple(sh)) for a, sh in zip(gathered, big_shapes)]
    meta_s, conv_s, wup_s, aup_s, gup_s = _unpack(gathered[-1], small_shapes, (N_CHIPS,))
    w = {
        'ffn1_norm': ffn1_norm, 'mix_norm': mix_norm, 'ffn2_norm': ffn2_norm, 'final_norm': final_norm[None, :],
        'ffn1_wgu': gu1, 'ffn1_wd': dn1.reshape(D_FF, D), 'ffn2_wgu': gu2, 'ffn2_wd': dn2.reshape(D_FF, D),
        'w_in_p': _win_to_padded(_cols_from_shards(w_in_s)), 'w_out': w_out_s.reshape(D, D),
        'a_conv_w': _cols_from_shards(conv_s), 'b_w_up': _cols_from_shards(wup_s), 'b_a_up': _cols_from_shards(aup_s),
        'b_g_up': _cols_from_shards(gup_s),
        'a_log_rate': a_log_rate, 'a_dt_bias': a_dt_bias, 'a_out_norm': a_out_norm, 'b_shift_mu': b_shift_mu,
        'b_w0': b_w0, 'b_a0': b_a0, 'b_k_k': b_k_k, 'b_k_a': b_k_a, 'b_r_k': b_r_k, 'b_ln_gain': b_ln_gain,
        'b_ln_bias': b_ln_bias,
    }
    meta_full = _cols_from_shards(meta_s)

    h0 = jnp.concatenate([jnp.zeros((PAD, D), f32), meta_full, x[0]], axis=0)
    tgt = jnp.concatenate([jnp.zeros((SKIP, D), f32), loss_target[0]], axis=0)
    loss_local, d_h0, g = _local_step(h0, tgt, w)
    loss = lax.psum(loss_local, ("x", "y", "c"))
    grad_x = d_h0[SKIP:][None]

    big_grads = [
        jnp.concatenate([g['ffn1_wg'], g['ffn1_wu']], axis=0),
        g['ffn1_wd'].reshape(N_CHIPS, D_FF // N_CHIPS, D),
        _cols_to_shards(_win_from_padded(g['w_in_p'])),
        g['w_out'].reshape(N_CHIPS, D // N_CHIPS, D),
        jnp.concatenate([g['ffn2_wg'], g['ffn2_wu']], axis=0),
        g['ffn2_wd'].reshape(N_CHIPS, D_FF // N_CHIPS, D),
    ]
    half_rows = [_rows_of(sh)[1] // 2 for sh in big_shapes]
    g_halves = jnp.concatenate([a.reshape(N_CHIPS, 2, hr, LANES).transpose(1, 0, 2, 3)
                                for a, hr in zip(big_grads, half_rows)], axis=2)
    half_shape = g_halves.shape[1:]
    sib_half = _swap_sibling(lambda ref, c: ref.at[1 - c], half_shape, f32, "swap_halves")(g_halves)
    chip_half = _add_halves(g_halves, sib_half, bf16, "add_sibling")
    mine = _sum_own_and_slots(chip_half, _scatter_chips(chip_half, "scatter_grads"), "sum_chips")
    theirs = _swap_sibling(lambda ref, c: ref, mine.shape, f32, "swap_sums")(mine)
    core = lax.axis_index("c")
    low, high = jnp.where(core == 0, mine, theirs), jnp.where(core == 0, theirs, mine)
    big_parts, off = [], 0
    for hr in half_rows:
        big_parts.append(jnp.concatenate([low[off:off + hr], high[off:off + hr]], axis=0))
        off += hr

    small_full = {
        'meta_tokens': d_h0[PAD:SKIP], 'ffn1_norm': g['ffn1_norm'], 'mix_norm': g['mix_norm'], 'a_conv_w': g['a_conv_w'],
        'a_log_rate': g['a_log_rate'], 'a_dt_bias': g['a_dt_bias'], 'a_out_norm': g['a_out_norm'],
        'b_shift_mu': g['b_shift_mu'], 'b_w0': g['b_w0'], 'b_w_up': g['b_w_up'], 'b_a0': g['b_a0'], 'b_a_up': g['b_a_up'],
        'b_g_up': g['b_g_up'], 'b_k_k': g['b_k_k'], 'b_k_a': g['b_k_a'], 'b_r_k': g['b_r_k'], 'b_ln_gain': g['b_ln_gain'],
        'b_ln_bias': g['b_ln_bias'], 'ffn2_norm': g['ffn2_norm'], 'final_norm': g['final_norm'],
    }
    s_shapes = [small_full[n].shape for n in _SMALL]
    s_sum = _sum_slots(_gather_devices(_pack([small_full[n] for n in _SMALL], f32, row_mult=256), "gather_small"),
                       "sum_small")
    s_parts = dict(zip(_SMALL, _unpack(s_sum, s_shapes)))

    grad, delta, new_m, new_v = {}, {}, {}, {}
    for n, a in zip(_BIG, big_parts):
        grad[n], delta[n], new_m[n], new_v[n] = _adamw(wts[n], [a.reshape(wts[n].shape)], mom[n], var[n], f"adamw_{n}")
    for n in _SMALL:
        gs = s_parts[n]
        if n in _SMALL_SHARDED:
            width = wts[n].shape[-1]
            gs = lax.dynamic_slice_in_dim(gs, chip * width, width, axis=gs.ndim - 1)
        gs = gs.reshape(wts[n].shape)
        grad[n], delta[n], new_m[n], new_v[n] = _adamw(wts[n], [gs], mom[n], var[n], f"adamw_{n}")

    return (loss, grad_x, *[grad[n] for n in _WEIGHTS], *[delta[n] for n in _WEIGHTS],
            *[new_m[n] for n in _WEIGHTS], *[new_v[n] for n in _WEIGHTS])
```

```python
import functools

import jax
import jax.numpy as jnp
from jax import lax
from jax.experimental import pallas as pl
from jax.experimental.pallas import tpu as pltpu

f32 = jnp.float32
bf16 = jnp.bfloat16
HI = lax.Precision.HIGHEST
MESH = pl.DeviceIdType.MESH
ANY = pl.BlockSpec(memory_space=pl.ANY)

D = 1024
N_META = 16
CHUNK = 64
PAD = CHUNK - N_META
SKIP = PAD + N_META
EPS = 1e-6
D_FF = 2816
A_HEADS = 8
A_DK = 128
B_HEADS = 16
B_N = 64
B_GN_EPS = B_N * 1e-5
W_LORA, AA_LORA, G_LORA = 64, 64, 160
IN_TOTAL = 9520
ZP = 9600
LANES = 128
N_CHIPS = 4
N_DEV = 8

ADAM_LR, ADAM_B1, ADAM_B2, ADAM_EPS, ADAM_WD, ADAM_STEP = 0.001, 0.9, 0.999, 1e-08, 0.01, 10

MXU_DTYPE = bf16


def _tile(n, cap, mult):
    if n <= cap:
        return n
    best = None
    for t in range(mult, cap + 1, mult):
        if n % t == 0:
            best = t
    assert best is not None, (n, cap, mult)
    return best


def _sigmoid(x):
    return jax.nn.sigmoid(x)


def _silu(x):
    return x * jax.nn.sigmoid(x)


def _softplus(x):
    return jnp.maximum(x, 0.0) + jnp.log(1.0 + jnp.exp(-jnp.abs(x)))


def _head_matrix(c, nh):
    hd = c // nh
    r = lax.broadcasted_iota(jnp.int32, (c, nh), 0)
    h = lax.broadcasted_iota(jnp.int32, (c, nh), 1)
    return (r >= h * hd) & (r < (h + 1) * hd)


def _dot_exact_rhs(x, e, cb):
    dn = (((1,), (cb,)), ((), ()))
    if SCAN_PASSES == 0:
        return lax.dot_general(x, e.astype(f32), dn, precision=HI, preferred_element_type=f32)
    eb = e.astype(bf16)
    hi = x.astype(bf16)
    lo = (x - hi.astype(f32)).astype(bf16)
    return (lax.dot_general(hi, eb, dn, preferred_element_type=f32)
            + lax.dot_general(lo, eb, dn, preferred_element_type=f32))


def _head_sum_impl(x, nh):
    e = _head_matrix(x.shape[-1], nh)
    return _dot_exact_rhs(_dot_exact_rhs(x, e, 0), e, 1)


@functools.partial(jax.custom_vjp, nondiff_argnums=(1,))
def _head_sum(x, nh):
    return _head_sum_impl(x, nh)


def _head_sum_fwd(x, nh):
    return _head_sum_impl(x, nh), None


def _head_sum_bwd(nh, _, g):
    return (_head_sum_impl(g, nh),)


_head_sum.defvjp(_head_sum_fwd, _head_sum_bwd)


@functools.partial(jax.custom_vjp, nondiff_argnums=(1,))
def _shift_rows(x, s):
    n = x.shape[0]
    row = lax.broadcasted_iota(jnp.int32, x.shape, 0)
    if s > 0:
        return jnp.where(row >= s, pltpu.roll(x, s, 0), 0.0)
    return jnp.where(row < n + s, pltpu.roll(x, n + s, 0), 0.0)


def _shift_rows_fwd(x, s):
    return _shift_rows(x, s), None


def _shift_rows_bwd(s, _, g):
    return (_shift_rows(g, -s),)


_shift_rows.defvjp(_shift_rows_fwd, _shift_rows_bwd)


def _matmul(a, b, *, ta=False, tb=False, res=None, scale=1.0, name, b_cols_split=None, out_cols_split=False):
    assert not (ta and tb)
    (ar, ac) = a.shape
    b0 = 0
    if b_cols_split:
        b0, bs = b_cols_split
        _, br, bc_part = b.shape
        bc = bs * bc_part
    else:
        br, bc = b.shape
    m, k = (ac, ar) if ta else (ar, ac)
    n, kb = (br, bc) if tb else (bc, br)
    assert k == kb, (a.shape, b.shape, ta, tb)
    tm = _tile(m, 1408, LANES) if ta else _tile(m, 832, 8)
    tn = _tile(n, 1408, LANES)
    tk = _tile(k, 1040, 8) if ta else _tile(k, 1408, LANES)
    nk = k // tk
    dn = (((0 if ta else 1,), (1 if tb else 0,)), ((), ()))
    if b_cols_split:
        assert (tk if tb else tn) == bc_part, (b.shape, tn, tk)

    def body(*refs):
        if res is not None:
            a_ref, b_ref, r_ref, o_ref, acc = refs
        else:
            a_ref, b_ref, o_ref, acc = refs
        kk = pl.program_id(2)

        @pl.when(kk == 0)
        def _():
            acc[...] = jnp.zeros_like(acc)

        acc[...] += lax.dot_general(a_ref[...].astype(MXU_DTYPE), b_ref[...].astype(MXU_DTYPE), dn,
                                    preferred_element_type=f32,
                                    precision=None if MXU_DTYPE == bf16 else HI)

        @pl.when(kk == nk - 1)
        def _():
            out = acc[...]
            if scale != 1.0:
                out = out * scale
            if res is not None:
                out = r_ref[...] + out
            o_ref[...] = out

    if ta:
        a_spec = pl.BlockSpec((tk, tm), lambda i, j, kk: (kk, i))
    else:
        a_spec = pl.BlockSpec((tm, tk), lambda i, j, kk: (i, kk))
    if tb and b_cols_split:
        b_spec = pl.BlockSpec((None, tn, tk), lambda i, j, kk: (kk + b0, j, 0))
    elif tb:
        b_spec = pl.BlockSpec((tn, tk), lambda i, j, kk: (j, kk))
    elif b_cols_split:
        b_spec = pl.BlockSpec((None, tk, tn), lambda i, j, kk: (j + b0, kk, 0))
    else:
        b_spec = pl.BlockSpec((tk, tn), lambda i, j, kk: (kk, j))
    in_specs = [a_spec, b_spec]
    args = [a, b]
    if res is not None:
        in_specs.append(pl.BlockSpec((tm, tn), lambda i, j, kk: (i, j)))
        args.append(res)
    if out_cols_split:
        out_spec = pl.BlockSpec((None, tm, tn), lambda i, j, kk: (j, i, 0))
        out_shape = jax.ShapeDtypeStruct((n // tn, m, tn), f32)
    else:
        out_spec = pl.BlockSpec((tm, tn), lambda i, j, kk: (i, j))
        out_shape = jax.ShapeDtypeStruct((m, n), f32)
    return pl.pallas_call(
        body, name=name, grid=(m // tm, n // tn, nk), in_specs=in_specs, out_specs=out_spec, out_shape=out_shape,
        scratch_shapes=[pltpu.VMEM((tm, tn), f32)],
        compiler_params=pltpu.CompilerParams(dimension_semantics=("parallel", "parallel", "arbitrary")),
    )(*args)


def _tw_fwd(fn, ins, in_specs, out_shapes, out_specs, grid, name, with_pid=False):
    n_in = len(ins)

    def body(*refs):
        vals = [r[...] for r in refs[:n_in]]
        outs = fn(pl.program_id(0), *vals) if with_pid else fn(*vals)
        for r, o in zip(refs[n_in:], outs):
            r[...] = o.astype(r.dtype)

    return pl.pallas_call(body, name=name, grid=grid, in_specs=in_specs, out_specs=out_specs,
                          out_shape=out_shapes)(*ins)


def _tw_bwd(fn, ins, in_specs, cts, ct_specs, kinds, grid, name, with_pid=False, tile_dtype=f32, ct_extra=(),
            residual=None):
    n_in, n_ct = len(ins), len(cts)
    diff = [i for i, kd in enumerate(kinds) if kd is not None]
    n_ex = len(ct_extra)

    def body(*refs):
        vals = [r[...] for r in refs[:n_in]]
        ctv = [r[...].astype(f32) for r in refs[n_in:n_in + n_ct]]
        for (ci, _), r in zip(ct_extra, refs[n_in + n_ct:n_in + n_ct + n_ex]):
            ctv[ci] = ctv[ci] + r[...]
        ctv = tuple(ctv)
        n_fixed = n_in + n_ct + n_ex
        res_ref = refs[n_fixed] if residual is not None else None
        g_refs = refs[n_fixed + (residual is not None):]
        pid = pl.program_id(0)

        def f(*dv):
            full = list(vals)
            for i, v in zip(diff, dv):
                full[i] = v
            out = fn(pid, *full) if with_pid else fn(*full)
            return tuple(out)

        _, vjp = jax.vjp(f, *[vals[i] for i in diff])
        gs = vjp(ctv)
        first = pid == 0
        for i2 in range(1, len(grid)):
            first = first & (pl.program_id(i2) == 0)
        for i, g, g_ref in zip(diff, gs, g_refs):
            if kinds[i] != 'acc':
                if i == 0 and res_ref is not None:
                    g = res_ref[...] + g
                g_ref[...] = g.astype(g_ref.dtype)
            else:
                @pl.when(first)
                def _(g=g, g_ref=g_ref):
                    g_ref[...] = g

                @pl.when(jnp.logical_not(first))
                def _(g=g, g_ref=g_ref):
                    g_ref[...] += g

    zero_map = {1: lambda *a: (0,), 2: lambda *a: (0, 0), 3: lambda *a: (0, 0, 0)}
    out_specs, out_shapes = [], []
    for i in diff:
        if kinds[i] == 'tile':
            out_shapes.append(jax.ShapeDtypeStruct(ins[i].shape, tile_dtype))
            out_specs.append(in_specs[i])
        elif kinds[i] == 'acc':
            out_shapes.append(jax.ShapeDtypeStruct(ins[i].shape, f32))
            out_specs.append(pl.BlockSpec(ins[i].shape, zero_map[ins[i].ndim]))
        else:
            out_shapes.append(jax.ShapeDtypeStruct(kinds[i][1], tile_dtype))
            out_specs.append(kinds[i][2])
    extra_specs = [ct_specs[ci] for ci, _ in ct_extra]
    extra = [a for _, a in ct_extra]
    if residual is not None:
        assert kinds[0] == 'tile'
        extra_specs.append(in_specs[0])
        extra.append(residual)
    return pl.pallas_call(body, name=name, grid=grid, in_specs=list(in_specs) + list(ct_specs) + extra_specs,
                          out_specs=out_specs, out_shape=out_shapes)(*ins, *cts, *extra)


def _row_spec(tm, c, col_block=0):
    return pl.BlockSpec((tm, c), lambda i, cb=col_block: (i, cb))


def _full_spec(shape):
    nd = len(shape)
    return pl.BlockSpec(shape, lambda *a, nd=nd: (0,) * nd)


def _f_rms(x, g):
    return (x * lax.rsqrt(jnp.mean(x * x, axis=-1, keepdims=True) + EPS) * g,)


def _f_swiglu(gate, up):
    return (_silu(gate) * up,)


def _f_loss(pid, h, g, tgt, *, tm):
    y = h * lax.rsqrt(jnp.mean(h * h, axis=-1, keepdims=True) + EPS) * g
    row = pid * tm + lax.broadcasted_iota(jnp.int32, (tm, 1), 0)
    err = jnp.where(row >= SKIP, y - tgt, 0.0)
    per_row = jnp.mean(err * err, axis=-1, keepdims=True)
    return (0.5 * jnp.sum(per_row, axis=0, keepdims=True),)


def _f_conv(x, w, *, norm, scale):
    y = x * w[3:4, :]
    for s in (1, 2, 3):
        y = y + _shift_rows(x, s) * w[3 - s:4 - s, :]
    y = _silu(y)
    if norm:
        y = y * lax.rsqrt(jnp.sum(y * y, axis=-1, keepdims=True) + 1e-6) * scale
    return (y,)


def _f_dgates(pid, abeta, aalpha, log_rate, dt_bias, *, tm):
    row = pid * tm + lax.broadcasted_iota(jnp.int32, (tm, 1), 0)
    live = row >= PAD
    beta = jnp.where(live, _sigmoid(abeta), 0.0)
    g = jnp.where(live, -jnp.exp(log_rate) * _softplus(aalpha + dt_bias), 0.0)
    return beta, g


def _f_tshift(z, mu):
    return (z + (_shift_rows(z, 1) - z) * mu,)


def _f_rwkv_pre(k, wd, ad, gd, w0, w_up, a0, a_up, g_up, k_k, k_a):
    w_log = -_softplus(-(w0 + _smm(jnp.tanh(wd), w_up))) - 0.5
    lw = -jnp.exp(w_log)
    a_lr = _sigmoid(a0 + _smm(ad, a_up))
    gate = _smm(_sigmoid(gd), g_up)
    kkp = k * k_k
    kk = kkp * lax.rsqrt(_head_sum(kkp * kkp, B_HEADS) + 1e-6)
    kmod = k * (1.0 + (a_lr - 1.0) * k_a)
    return lw, kmod, -kk, kk * a_lr, gate


def _f_mix_post(o, az, y, r, kmod, v, gate, ga, gb, out_gain, ln_g, ln_b, r_k):
    ms = _head_sum(o * o, A_HEADS) * (1.0 / A_DK)
    oa = o * lax.rsqrt(ms + EPS) * out_gain * _silu(az)
    mean = _head_sum(y, B_HEADS) * (1.0 / B_N)
    yc = y - mean
    var = _head_sum(yc * yc, B_HEADS) * (1.0 / B_N)
    yn = yc * lax.rsqrt(var + B_GN_EPS) * ln_g + ln_b
    bonus = _head_sum(r * kmod * r_k, B_HEADS) * v
    ob = (yn + bonus) * gate
    return (_sigmoid(ga) * oa + _sigmoid(gb) * ob,)


SCAN_PASSES = 3


def _split2(a):
    hi = a.astype(bf16)
    return hi, (a - hi.astype(f32)).astype(bf16)


def _dot_passes(a, b, ca, cb, passes):
    dn = (((ca,), (cb,)), ((), ()))
    if SCAN_PASSES == 0:
        return lax.dot_general(a, b, dn, precision=HI, preferred_element_type=f32)
    if passes == 1:
        return lax.dot_general(a.astype(bf16), b.astype(bf16), dn, preferred_element_type=f32)
    ah, al = _split2(a)
    bh, bl = _split2(b)
    return (lax.dot_general(ah, bh, dn, preferred_element_type=f32)
            + (lax.dot_general(ah, bl, dn, preferred_element_type=f32)
               + lax.dot_general(al, bh, dn, preferred_element_type=f32)))


@functools.partial(jax.custom_vjp, nondiff_argnums=(2, 3, 4))
def _sdot(a, b, ca, cb, passes):
    return _dot_passes(a, b, ca, cb, passes)


def _sdot_fwd(a, b, ca, cb, passes):
    return _dot_passes(a, b, ca, cb, passes), (a, b)


def _sdot_bwd(ca, cb, passes, res, g):
    a, b = res
    if (ca, cb) == (1, 0):
        return _dot_passes(g, b, 1, 1, passes), _dot_passes(a, g, 0, 0, passes)
    if (ca, cb) == (1, 1):
        return _dot_passes(g, b, 1, 0, passes), _dot_passes(g, a, 0, 0, passes)
    assert (ca, cb) == (0, 0)
    return _dot_passes(b, g, 1, 1, passes), _dot_passes(a, g, 1, 0, passes)


_sdot.defvjp(_sdot_fwd, _sdot_bwd)


def _smm(a, b, passes=3):
    return _sdot(a, b, 1, 0, passes)


def _smm_nt(a, b, passes=3):
    return _sdot(a, b, 1, 1, passes)


def _smm_tn(a, b, passes=3):
    return _sdot(a, b, 0, 0, passes)


def _tri_dot(x, ca):
    n = x.shape[0]
    incl = _tri_masks(n)[0]
    dn = (((ca,), (0,)), ((), ()))
    if SCAN_PASSES == 0:
        return lax.dot_general(incl.astype(f32), x, dn, precision=HI, preferred_element_type=f32)
    tri = incl.astype(bf16)
    hi, r1 = x.astype(bf16), None
    r1 = x - hi.astype(f32)
    mid = r1.astype(bf16)
    lo = (r1 - mid.astype(f32)).astype(bf16)
    return (lax.dot_general(tri, hi, dn, preferred_element_type=f32)
            + (lax.dot_general(tri, mid, dn, preferred_element_type=f32)
               + lax.dot_general(tri, lo, dn, preferred_element_type=f32)))


@jax.custom_vjp
def _cumsum_rows(x):
    return _tri_dot(x, 1)


def _cumsum_rows_fwd(x):
    return _tri_dot(x, 1), None


def _cumsum_rows_bwd(_, g):
    return (_tri_dot(g, 0),)


_cumsum_rows.defvjp(_cumsum_rows_fwd, _cumsum_rows_bwd)


def _tri_masks(n):
    i = lax.broadcasted_iota(jnp.int32, (n, n), 0)
    j = lax.broadcasted_iota(jnp.int32, (n, n), 1)
    return i >= j, i > j, i == j, i <= j


def _unit_lower_inv_impl(low, passes):
    n = low.shape[0]
    assert n == CHUNK
    _, _, eye, _ = _tri_masks(n)
    acc = eye.astype(f32) + low
    p = low
    for _ in range(5):
        p = _dot_passes(p, p, 1, 0, passes)
        acc = acc + _dot_passes(acc, p, 1, 0, passes)
    return acc


@functools.partial(jax.custom_vjp, nondiff_argnums=(1,))
def _unit_lower_inv(low, passes=3):
    return _unit_lower_inv_impl(low, passes)


def _unit_lower_inv_fwd(low, passes):
    t = _unit_lower_inv_impl(low, passes)
    return t, t


def _unit_lower_inv_bwd(passes, t, g):
    return (_dot_passes(_dot_passes(t, g, 0, 0, passes), t, 1, 1, passes),)


_unit_lower_inv.defvjp(_unit_lower_inv_fwd, _unit_lower_inv_bwd)

DELTA_PASSES = 1
DELTA_INV_PASSES = 1


def _delta_chunk(s, q, k, v, beta_row, g_row):
    p = DELTA_PASSES
    incl, strict, eye, upper = _tri_masks(CHUNK)
    beta = jnp.sum(jnp.where(eye, beta_row, 0.0), axis=1, keepdims=True)
    g = jnp.sum(jnp.where(eye, g_row, 0.0), axis=1, keepdims=True)
    gc = jnp.sum(jnp.where(incl, g_row, 0.0), axis=1, keepdims=True)
    gc_row = jnp.sum(jnp.where(upper, g, 0.0), axis=0, keepdims=True)
    decay = jnp.where(incl, jnp.exp(jnp.where(incl, gc - gc_row, 0.0)), 0.0)
    kb = k * beta
    vb = v * beta
    m = jnp.where(strict, _smm_nt(kb, k, p) * decay, 0.0)
    tinv = _unit_lower_inv(-m, DELTA_INV_PASSES)
    u = _smm(tinv, vb, p)
    wk = _smm(tinv, kb * jnp.exp(gc), p)
    attn = _smm_nt(q, k, p) * decay
    qg = q * jnp.exp(gc)
    g_last = jnp.sum(g, axis=0, keepdims=True)
    k_tail = k * jnp.exp(g_last - gc)
    v_new = u - _smm(wk, s, p)
    o = _smm(qg, s, p) + _smm(attn, v_new, p)
    s_new = s * jnp.exp(g_last) + _smm_tn(k_tail, v_new, p)
    return o, s_new


RWKV_PASSES = 1
RWKV_INV_PASSES = 1


def _rwkv_chunk(st, r, k, v, a, b, lw):
    c = CHUNK
    p, pi = RWKV_PASSES, RWKV_INV_PASSES
    _, strict, _, _ = _tri_masks(c)
    lane = lax.broadcasted_iota(jnp.int32, (c, 2 * B_N), 1)
    row = lax.broadcasted_iota(jnp.int32, (c, 2 * B_N), 0)
    first = lane < B_N
    incl2 = row >= jnp.where(first, lane, lane - B_N)
    bi = lax.broadcasted_iota(jnp.int32, (2 * B_N, 2 * B_N), 0) < B_N
    bj = lax.broadcasted_iota(jnp.int32, (2 * B_N, 2 * B_N), 1) < B_N
    blockdiag = bi == bj
    cum = _cumsum_rows(lw)
    e_pos = jnp.exp(cum)
    e_neg = jnp.exp(-cum)
    rt = r * e_pos
    at = a * jnp.exp(cum - lw)
    kt = k * e_neg
    bt = b * e_neg
    bk = jnp.concatenate([bt, kt], axis=0)
    a_s0 = _smm_nt(at, st, p)
    r_s0 = _smm_nt(rt, st, p)
    heads = (first, jnp.logical_not(first))
    u = jnp.zeros((c, 2 * B_N), f32)
    for sel in heads:
        at_h = jnp.where(sel, at, 0.0)
        ab = jnp.where(strict, _smm_nt(at_h, bt, pi), 0.0)
        ak = jnp.where(strict, _smm_nt(at_h, kt, p), 0.0)
        t_h = _unit_lower_inv(ab, pi)
        u = u + _smm(t_h, jnp.where(sel, a_s0, 0.0) + _smm(ak, jnp.where(sel, v, 0.0), p), p)
    y = r_s0
    for sel in heads:
        rbk = jnp.where(incl2, _smm_nt(jnp.where(sel, rt, 0.0), bk, p), 0.0)
        uv = jnp.concatenate([jnp.where(sel, u, 0.0), jnp.where(sel, v, 0.0)], axis=0)
        y = y + _smm(rbk, uv, p)
    cl = jnp.sum(lw, axis=0, keepdims=True)
    dec = jnp.exp(cl - cum)
    uv_all = jnp.concatenate([u, v], axis=0)
    bk_dec = jnp.concatenate([b * dec, k * dec], axis=0)
    st_new = st * jnp.exp(cl) + jnp.where(blockdiag, _smm_tn(uv_all, bk_dec, p), 0.0)
    return y, st_new


GROUPS_PER_STEP = 8


def _scan_specs(ins, col_offs, n_chunks, reverse):
    gw = GROUPS_PER_STEP * LANES
    cidx = (lambda c: n_chunks - 1 - c) if reverse else (lambda c: c)
    specs = []
    for a, off in zip(ins, col_offs):
        if a.ndim == 2:
            assert off % gw == 0
            specs.append(pl.BlockSpec((CHUNK, gw), lambda h, c, o=off // gw: (cidx(c), h + o)))
        else:
            specs.append(pl.BlockSpec((GROUPS_PER_STEP, None, 1, CHUNK), lambda h, c: (h, cidx(c), 0, 0)))
    return specs, cidx


def _group_vals(refs, g):
    return [r[:, g * LANES:(g + 1) * LANES] if len(r.shape) == 2 else r[g] for r in refs]


def _scan_fwd(chunk_fn, ins, col_offs, n_groups, n_chunks, state_shape, name):
    n_in = len(ins)
    gps = GROUPS_PER_STEP
    t = ins[0].shape[0]

    def body(*refs):
        in_refs = refs[:n_in]
        o_ref, s0_ref, st = refs[n_in:]

        @pl.when(pl.program_id(1) == 0)
        def _():
            st[...] = jnp.zeros_like(st)

        states = st[...]
        vals = [jnp.stack(col) for col in zip(*[_group_vals(in_refs, g) for g in range(gps)])]
        o, s_new = jax.vmap(chunk_fn)(states, *vals)
        s0_ref[...] = states
        st[...] = s_new
        for g in range(gps):
            o_ref[:, g * LANES:(g + 1) * LANES] = o[g]

    specs, _ = _scan_specs(ins, col_offs, n_chunks, False)
    return pl.pallas_call(
        body, name=name, grid=(n_groups // gps, n_chunks), in_specs=specs,
        out_specs=[pl.BlockSpec((CHUNK, gps * LANES), lambda h, c: (c, h)),
                   pl.BlockSpec((gps, None) + state_shape, lambda h, c: (h, c, 0, 0))],
        out_shape=[jax.ShapeDtypeStruct((t, n_groups * LANES), f32),
                   jax.ShapeDtypeStruct((n_groups, n_chunks) + state_shape, f32)],
        scratch_shapes=[pltpu.VMEM((gps,) + state_shape, f32)],
        compiler_params=pltpu.CompilerParams(dimension_semantics=("parallel", "arbitrary")),
    )(*ins)


def _scan_bwd(chunk_fn, s0s, ins, col_offs, d_out, n_groups, n_chunks, state_shape, name):
    n_in = len(ins)
    gps = GROUPS_PER_STEP
    t = d_out.shape[0]

    def body(*refs):
        s0_ref = refs[0]
        in_refs = refs[1:1 + n_in]
        do_ref = refs[1 + n_in]
        g_refs = refs[2 + n_in:2 + 2 * n_in]
        dst = refs[2 + 2 * n_in]

        @pl.when(pl.program_id(1) == 0)
        def _():
            dst[...] = jnp.zeros_like(dst)

        vals = [jnp.stack(col) for col in zip(*[_group_vals(in_refs, g) for g in range(gps)])]
        d_o = jnp.stack([do_ref[:, g * LANES:(g + 1) * LANES] for g in range(gps)])
        _, vjp = jax.vjp(jax.vmap(chunk_fn), s0_ref[...], *vals)
        gs = vjp((d_o, dst[...]))
        dst[...] = gs[0]
        for g_ref, gv in zip(g_refs, gs[1:]):
            if len(g_ref.shape) == 2:
                for g in range(gps):
                    g_ref[:, g * LANES:(g + 1) * LANES] = gv[g]
            else:
                g_ref[...] = gv

    specs, cidx = _scan_specs(ins, col_offs, n_chunks, True)
    out_lane = pl.BlockSpec((CHUNK, gps * LANES), lambda h, c: (cidx(c), h))
    g_specs = [out_lane if a.ndim == 2 else sp for a, sp in zip(ins, specs)]
    g_shapes = [(t, n_groups * LANES) if a.ndim == 2 else a.shape for a in ins]
    s0_spec = pl.BlockSpec((gps, None) + state_shape, lambda h, c: (h, cidx(c), 0, 0))
    return pl.pallas_call(
        body, name=name, grid=(n_groups // gps, n_chunks), in_specs=[s0_spec] + specs + [out_lane],
        out_specs=g_specs, out_shape=[jax.ShapeDtypeStruct(sh, f32) for sh in g_shapes],
        scratch_shapes=[pltpu.VMEM((gps,) + state_shape, f32)],
        compiler_params=pltpu.CompilerParams(dimension_semantics=("parallel", "arbitrary")),
    )(s0s, *ins, d_out)


def _rms_fwd(x, g, name):
    t = x.shape[0]
    tm = _tile(t, 416, 16)
    return _tw_fwd(_f_rms, [x, g], [_row_spec(tm, D), _full_spec(g.shape)],
                   [jax.ShapeDtypeStruct(x.shape, MXU_DTYPE)], [_row_spec(tm, D)], (t // tm,), name)[0]


def _rms_bwd(x, g, dy, residual, name):
    t = x.shape[0]
    tm = _tile(t, 416, 8)
    return _tw_bwd(_f_rms, [x, g], [_row_spec(tm, D), _full_spec(g.shape)], [dy], [_row_spec(tm, D)],
                   ['tile', 'acc'], (t // tm,), name, residual=residual)


def _ffn_fwd(h, gain, wgu, wd, tag):
    xn = _rms_fwd(h, gain, f"{tag}_rms")
    gate = _matmul(xn, wgu, b_cols_split=(0, 2), name=f"{tag}_gate")
    up = _matmul(xn, wgu, b_cols_split=(2, 2), name=f"{tag}_up")
    t = h.shape[0]
    tm = _tile(t, 208, 16)
    act = _tw_fwd(_f_swiglu, [gate, up], [_row_spec(tm, D_FF)] * 2, [jax.ShapeDtypeStruct((t, D_FF), MXU_DTYPE)],
                  [_row_spec(tm, D_FF)], (t // tm,), f"{tag}_act")[0]
    out = _matmul(act, wd, res=h, scale=0.5, name=f"{tag}_down")
    return out, (xn, gate, up, act)


def _ffn_bwd(h, gain, wgu, wd, saved, dout, tag):
    xn, gate, up, act = saved
    t = h.shape[0]
    d_wd = _matmul(act, dout, ta=True, scale=0.5, name=f"{tag}_dwd")
    d_act = _matmul(dout, wd, tb=True, scale=0.5, name=f"{tag}_dact")
    tm = _tile(t, 208, 16)
    d_gate, d_up = _tw_bwd(_f_swiglu, [gate, up], [_row_spec(tm, D_FF)] * 2, [d_act], [_row_spec(tm, D_FF)],
                           ['tile', 'tile'], (t // tm,), f"{tag}_dactf", tile_dtype=MXU_DTYPE)
    d_wg = _matmul(xn, d_gate, ta=True, out_cols_split=True, name=f"{tag}_dwg")
    d_wu = _matmul(xn, d_up, ta=True, out_cols_split=True, name=f"{tag}_dwu")
    d_xn = _matmul(d_gate, wgu, tb=True, b_cols_split=(0, 2), name=f"{tag}_dxn_g")
    d_xn = _matmul(d_up, wgu, tb=True, b_cols_split=(2, 2), res=d_xn, name=f"{tag}_dxn_u")
    d_h, d_gain = _rms_bwd(h, gain, d_xn, dout, f"{tag}_drms")
    return d_h, d_gain, d_wg, d_wu, d_wd


def _col_spec(t, first_block):
    return pl.BlockSpec((t, LANES), lambda j, fb=first_block: (0, j + fb))


def _local_step(h0, tgt, w):
    t = h0.shape[0]
    assert t % CHUNK == 0
    nc = t // CHUNK
    grads = {}

    h1, ffn1_saved = _ffn_fwd(h0, w['ffn1_norm'], w['ffn1_wgu'], w['ffn1_wd'], "ffn1")
    u = _rms_fwd(h1, w['mix_norm'], "mix_rms")
    z = _matmul(u, w['w_in_p'], name="in_proj")
    zs = z[:, 9216:9216 + 304]
    abeta, aalpha = zs[:, 288:296], zs[:, 296:304]

    conv_w = w['a_conv_w']
    conv_fns = [functools.partial(_f_conv, norm=True, scale=A_DK ** -0.5),
                functools.partial(_f_conv, norm=True, scale=1.0),
                functools.partial(_f_conv, norm=False, scale=1.0)]
    qkv = []
    for idx, fn in enumerate(conv_fns):
        qkv.append(_tw_fwd(fn, [z, conv_w], [_col_spec(t, 8 * idx), pl.BlockSpec((4, LANES), lambda j, o=8 * idx: (0, j + o))],
                           [jax.ShapeDtypeStruct((t, D), f32)], [_col_spec(t, 0)], (A_HEADS,), f"a_conv{idx}")[0])
    aq, ak, av = qkv
    tmg = _tile(t, 1040, 8)
    dg_fn = functools.partial(_f_dgates, tm=tmg)
    dg_specs = [_row_spec(tmg, A_HEADS)] * 2 + [_full_spec((1, A_HEADS))] * 2
    beta, gdec = _tw_fwd(dg_fn, [abeta, aalpha, w['a_log_rate'], w['a_dt_bias']], dg_specs,
                         [jax.ShapeDtypeStruct((t, A_HEADS), f32)] * 2, [_row_spec(tmg, A_HEADS)] * 2, (t // tmg,),
                         "a_gates", with_pid=True)
    beta_h = beta.T.reshape(A_HEADS, nc, 1, CHUNK)
    gdec_h = gdec.T.reshape(A_HEADS, nc, 1, CHUNK)
    a_ins = [aq, ak, av, beta_h, gdec_h]
    a_offs = [0] * 5
    o_scan, a_s0 = _scan_fwd(_delta_chunk, a_ins, a_offs, A_HEADS, nc, (A_DK, A_DK), "a_scan")

    mu = w['b_shift_mu']
    mu_rkv, mu_s = mu[:, :3072], mu[:, 3072:]
    zf_rkv = _tw_fwd(_f_tshift, [z, mu_rkv], [_col_spec(t, 32), pl.BlockSpec((1, LANES), lambda j: (0, j))],
                     [jax.ShapeDtypeStruct((t, 3072), f32)], [_col_spec(t, 0)], (24,), "b_shift")[0]
    zs_b = zs[:, :288]
    zf_s = _tw_fwd(_f_tshift, [zs_b, mu_s], [_full_spec((t, 288)), _full_spec((1, 288))],
                   [jax.ShapeDtypeStruct((t, 288), f32)], [_full_spec((t, 288))], (1,), "b_shift_s")[0]
    wdf, adf, gdf = zf_s[:, 0:64], zf_s[:, 64:128], zf_s[:, 128:288]
    tmr = _tile(t, 160, 16)
    pre_params = [w['b_w0'], w['b_w_up'], w['b_a0'], w['b_a_up'], w['b_g_up'], w['b_k_k'], w['b_k_a']]
    pre_ins = [zf_rkv, wdf, adf, gdf] + pre_params
    pre_specs = ([_row_spec(tmr, D, 1), _row_spec(tmr, 64), _row_spec(tmr, 64), _row_spec(tmr, 160)]
                 + [_full_spec(p.shape) for p in pre_params])
    lw, kmod, a_s, b_s, bgate = _tw_fwd(_f_rwkv_pre, pre_ins, pre_specs, [jax.ShapeDtypeStruct((t, D), f32)] * 5,
                                        [_row_spec(tmr, D)] * 5, (t // tmr,), "b_pre")
    b_ins = [zf_rkv, kmod, zf_rkv, a_s, b_s, lw]
    b_offs = [0, 0, 2 * D, 0, 0, 0]
    y_scan, b_s0 = _scan_fwd(_rwkv_chunk, b_ins, b_offs, B_HEADS // 2, nc, (2 * B_N, 2 * B_N), "b_scan")

    out_gain_t = jnp.tile(w['a_out_norm'], (1, A_HEADS))
    r_k = w['b_r_k'].reshape(1, D)
    post_params = [out_gain_t, w['b_ln_gain'], w['b_ln_bias'], r_k]
    post_ins = [o_scan, z, y_scan, zf_rkv, kmod, zf_rkv, bgate, z, z] + post_params
    post_specs = ([_row_spec(tmr, D), _row_spec(tmr, D, 3), _row_spec(tmr, D), _row_spec(tmr, D, 0), _row_spec(tmr, D),
                   _row_spec(tmr, D, 2), _row_spec(tmr, D), _row_spec(tmr, D, 7), _row_spec(tmr, D, 8)]
                  + [_full_spec((1, D))] * 4)
    merged = _tw_fwd(_f_mix_post, post_ins, post_specs, [jax.ShapeDtypeStruct((t, D), MXU_DTYPE)],
                     [_row_spec(tmr, D)], (t // tmr,), "mix_post")[0]
    h2 = _matmul(merged, w['w_out'], res=h1, name="out_proj")
    h3, ffn2_saved = _ffn_fwd(h2, w['ffn2_norm'], w['ffn2_wgu'], w['ffn2_wd'], "ffn2")

    tml = _tile(t, 416, 8)
    fnorm = w['final_norm']
    loss_fn = functools.partial(_f_loss, tm=tml)
    loss_specs = [_row_spec(tml, D), _full_spec((1, D)), _row_spec(tml, D)]
    loss_parts = _tw_fwd(loss_fn, [h3, fnorm, tgt], loss_specs, [jax.ShapeDtypeStruct((t // tml, 1, 1), f32)],
                         [pl.BlockSpec((None, 1, 1), lambda i: (i, 0, 0))], (t // tml,), "loss", with_pid=True)[0]
    loss = jnp.sum(loss_parts)
    ones = jnp.ones((t // tml, 1, 1), f32)
    d_h3, grads['final_norm'] = _tw_bwd(loss_fn, [h3, fnorm, tgt], loss_specs, [ones],
                                        [pl.BlockSpec((None, 1, 1), lambda i: (i, 0, 0))], ['tile', 'acc', None],
                                        (t // tml,), "loss_bwd", with_pid=True)

    d_h2, grads['ffn2_norm'], grads['ffn2_wg'], grads['ffn2_wu'], grads['ffn2_wd'] = _ffn_bwd(
        h2, w['ffn2_norm'], w['ffn2_wgu'], w['ffn2_wd'], ffn2_saved, d_h3, "ffn2")
    grads['w_out'] = _matmul(merged, d_h2, ta=True, name="d_w_out")
    d_merged = _matmul(d_h2, w['w_out'], tb=True, name="d_merged")

    win = ('tile', (t, D), _row_spec(tmr, D))
    post_kinds = ['tile', win, 'tile', win, 'tile', win, 'tile', win, win] + ['acc'] * 4
    (d_o, d_az, d_y, d_r1, d_kmod1, d_v1, d_bgate, d_ga, d_gb,
     d_out_gain_t, grads['b_ln_gain'], grads['b_ln_bias'], d_r_k) = _tw_bwd(
        _f_mix_post, post_ins, post_specs, [d_merged], [_row_spec(tmr, D)], post_kinds, (t // tmr,), "mix_post_bwd")
    grads['a_out_norm'] = jnp.sum(d_out_gain_t.reshape(A_HEADS, A_DK), axis=0, keepdims=True)
    grads['b_r_k'] = d_r_k.reshape(1, B_HEADS, B_N)

    d_r2, d_kmod2, d_v2, d_as, d_bs, d_lw = _scan_bwd(_rwkv_chunk, b_s0, b_ins, b_offs, d_y, B_HEADS // 2, nc,
                                                      (2 * B_N, 2 * B_N), "b_scan_bwd")
    pre_kinds = [win] + ['tile'] * 3 + ['acc'] * 7
    pre_ct_specs = [_row_spec(tmr, D)] * 5
    (d_zf_k, d_wdf, d_adf, d_gdf, grads['b_w0'], grads['b_w_up'], grads['b_a0'], grads['b_a_up'], grads['b_g_up'],
     grads['b_k_k'], grads['b_k_a']) = _tw_bwd(
        _f_rwkv_pre, pre_ins, pre_specs, [d_lw, d_kmod1, d_as, d_bs, d_bgate], pre_ct_specs, pre_kinds, (t // tmr,),
        "b_pre_bwd", ct_extra=[(1, d_kmod2)])
    d_zf_rkv = _assemble3(d_r1, d_r2, d_zf_k, d_v1, d_v2, "b_dzf")
    d_zb_rkv, d_mu_rkv = _tw_bwd(_f_tshift, [z, mu_rkv], [_col_spec(t, 32), pl.BlockSpec((1, LANES), lambda j: (0, j))],
                                 [d_zf_rkv], [_col_spec(t, 0)], [('tile', (t, 3072), _col_spec(t, 0)), 'tile'], (24,),
                                 "b_shift_bwd")
    d_zf_s = jnp.concatenate([d_wdf, d_adf, d_gdf], axis=1)
    d_zs_b, d_mu_s = _tw_bwd(_f_tshift, [zs_b, mu_s], [_full_spec((t, 288)), _full_spec((1, 288))], [d_zf_s],
                             [_full_spec((t, 288))], ['tile', 'tile'], (1,), "b_shift_s_bwd")
    grads['b_shift_mu'] = jnp.concatenate([d_mu_rkv, d_mu_s], axis=1)

    d_aq, d_ak, d_av, d_beta_h, d_g_h = _scan_bwd(_delta_chunk, a_s0, a_ins, a_offs, d_o, A_HEADS, nc, (A_DK, A_DK),
                                                  "a_scan_bwd")
    d_beta = d_beta_h.reshape(A_HEADS, t).T
    d_gdec = d_g_h.reshape(A_HEADS, t).T
    d_abeta, d_aalpha, grads['a_log_rate'], grads['a_dt_bias'] = _tw_bwd(
        dg_fn, [abeta, aalpha, w['a_log_rate'], w['a_dt_bias']], dg_specs, [d_beta, d_gdec],
        [_row_spec(tmg, A_HEADS)] * 2, ['tile', 'tile', 'acc', 'acc'], (t // tmg,), "a_gates_bwd", with_pid=True)
    d_zqkv, d_conv = [], []
    for idx, (fn, ct) in enumerate(zip(conv_fns, (d_aq, d_ak, d_av))):
        dz_i, dw_i = _conv_bwd(fn, z, conv_w, ct, idx, t)
        d_zqkv.append(dz_i)
        d_conv.append(dw_i)
    grads['a_conv_w'] = jnp.concatenate(d_conv, axis=1)

    d_z_parts = d_zqkv + [d_az, d_zb_rkv, d_ga, d_gb, d_zs_b, d_abeta, d_aalpha, jnp.zeros((t, ZP - 9216 - 304), f32)]
    d_z = jnp.concatenate([p.astype(MXU_DTYPE) for p in d_z_parts], axis=1)
    grads['w_in_p'] = _matmul(u, d_z, ta=True, name="d_w_in")
    d_u = _matmul(d_z, w['w_in_p'], tb=True, name="d_u")
    d_h1, grads['mix_norm'] = _rms_bwd(h1, w['mix_norm'], d_u, d_h2, "mix_drms")
    d_h0, grads['ffn1_norm'], grads['ffn1_wg'], grads['ffn1_wu'], grads['ffn1_wd'] = _ffn_bwd(
        h0, w['ffn1_norm'], w['ffn1_wgu'], w['ffn1_wd'], ffn1_saved, d_h1, "ffn1")
    return loss, d_h0, grads


_WIN_SEGMENTS = ((0, 4096), (4112, 7184), (7472, 9520), (7184, 7472), (4096, 4112))


def _win_to_padded(w_in):
    parts = [w_in[:, a:b] for a, b in _WIN_SEGMENTS]
    parts.append(jnp.zeros((w_in.shape[0], ZP - IN_TOTAL), w_in.dtype))
    return jnp.concatenate(parts, axis=1)


def _win_from_padded(w_p):
    widths = [b - a for a, b in _WIN_SEGMENTS]
    offs = [sum(widths[:i]) for i in range(len(widths))]
    seg = {a: w_p[:, o:o + wd] for (a, _), o, wd in zip(_WIN_SEGMENTS, offs, widths)}
    return jnp.concatenate([seg[a] for a in sorted(seg)], axis=1)


def _assemble3(d_r1, d_r2, d_k, d_v1, d_v2, name):
    t = d_r1.shape[0]
    tm = _tile(t, 208, 8)

    def body(r1, r2, kk, v1, v2, o_ref):
        o_ref[:, 0:D] = r1[...] + r2[...]
        o_ref[:, D:2 * D] = kk[...]
        o_ref[:, 2 * D:3 * D] = v1[...] + v2[...]

    return pl.pallas_call(body, name=name, grid=(t // tm,), in_specs=[_row_spec(tm, D)] * 5,
                          out_specs=_row_spec(tm, 3 * D), out_shape=jax.ShapeDtypeStruct((t, 3 * D), f32),
                          )(d_r1, d_r2, d_k, d_v1, d_v2)


def _conv_bwd(fn, z, conv_w, ct, idx, t):
    def body(z_ref, w_ref, ct_ref, dz_ref, dw_ref):
        _, vjp = jax.vjp(lambda a, b: fn(a, b), z_ref[...], w_ref[...])
        dz, dw = vjp((ct_ref[...],))
        dz_ref[...] = dz
        dw_ref[...] = dw

    return pl.pallas_call(
        body, name=f"a_conv{idx}_bwd", grid=(A_HEADS,),
        in_specs=[_col_spec(t, 8 * idx), pl.BlockSpec((4, LANES), lambda j, o=8 * idx: (0, j + o)), _col_spec(t, 0)],
        out_specs=[_col_spec(t, 0), pl.BlockSpec((4, LANES), lambda j: (0, j))],
        out_shape=[jax.ShapeDtypeStruct((t, D), f32), jax.ShapeDtypeStruct((4, D), f32)],
    )(z, conv_w, ct)


def _position():
    return lax.axis_index("x"), lax.axis_index("y"), lax.axis_index("c")


def _flip(v, f):
    return 1 - v if f else v


_CHIP_FLIPS = ((1, 0), (0, 1), (1, 1))
_DEV_FLIPS = tuple((fx, fy, fc) for fx in (0, 1) for fy in (0, 1) for fc in (0, 1) if (fx, fy, fc) != (0, 0, 0))


def _gather_chips(arrs, name):
    n = len(arrs)
    assert all(a.shape[0] % 32 == 0 for a in arrs)
    arrs = [a.reshape(2, a.shape[0] // 2, a.shape[1]) for a in arrs]

    def body(*refs):
        ins, outs = refs[:n], refs[n:2 * n]
        send, recv, fsend, frecv, own = refs[2 * n:]
        x, y, c = _position()
        me = 2 * x + y
        sends, plan, owns = [], [], []
        for a in range(n):
            cp = pltpu.make_async_remote_copy(src_ref=ins[a], dst_ref=outs[a].at[me], send_sem=own.at[a, 0],
                                              recv_sem=own.at[a, 1], device_id=(x, y, 1 - c), device_id_type=MESH)
            cp.start()
            owns.append(cp)
            for j, (fx, fy) in enumerate(_CHIP_FLIPS):
                px, py = _flip(x, fx), _flip(y, fy)
                p = 2 * px + py
                cp = pltpu.make_async_remote_copy(src_ref=ins[a].at[c], dst_ref=outs[a].at[me, c],
                                                  send_sem=send.at[a, j], recv_sem=recv.at[a, j],
                                                  device_id=(px, py, c), device_id_type=MESH)
                cp.start()
                sends.append(cp)
                landed = pltpu.make_async_remote_copy(src_ref=ins[a].at[c], dst_ref=outs[a].at[p, c],
                                                      send_sem=send.at[a, j], recv_sem=recv.at[a, j],
                                                      device_id=(px, py, c), device_id_type=MESH)
                onward = pltpu.make_async_remote_copy(src_ref=outs[a].at[p, c], dst_ref=outs[a].at[p, c],
                                                      send_sem=fsend.at[a, j], recv_sem=frecv.at[a, j],
                                                      device_id=(x, y, 1 - c), device_id_type=MESH)
                from_sibling = pltpu.make_async_remote_copy(src_ref=outs[a].at[p, 1 - c], dst_ref=outs[a].at[p, 1 - c],
                                                            send_sem=fsend.at[a, j], recv_sem=frecv.at[a, j],
                                                            device_id=(x, y, 1 - c), device_id_type=MESH)
                plan.append((landed, onward, from_sibling))
        for landed, onward, _ in plan:
            landed.wait_recv()
            onward.start()
        for _, _, from_sibling in plan:
            from_sibling.wait_recv()
        for cp in sends:
            cp.wait_send()
        for _, onward, _ in plan:
            onward.wait_send()
        for cp in owns:
            cp.wait()

    sems = [pltpu.SemaphoreType.DMA((n, 3))] * 4 + [pltpu.SemaphoreType.DMA((n, 2))]
    outs = pl.pallas_call(
        body, name=name, in_specs=[ANY] * n, out_specs=[ANY] * n,
        out_shape=[jax.ShapeDtypeStruct((N_CHIPS,) + a.shape, a.dtype) for a in arrs], scratch_shapes=sems,
    )(*arrs)
    return [o.reshape(N_CHIPS, o.shape[1] * o.shape[2], o.shape[3]) for o in outs]


def _swap_sibling(arrs, src_of, shapes, name):
    n = len(arrs)

    def body(*refs):
        a_refs, got_refs = refs[:n], refs[n:2 * n]
        send, recv = refs[2 * n:]
        x, y, c = _position()
        copies = []
        for i in range(n):
            cp = pltpu.make_async_remote_copy(src_ref=src_of(a_refs[i], c), dst_ref=got_refs[i], send_sem=send.at[i],
                                              recv_sem=recv.at[i], device_id=(x, y, 1 - c), device_id_type=MESH)
            cp.start()
            copies.append(cp)
        for cp in copies:
            cp.wait()

    return pl.pallas_call(body, name=name, in_specs=[ANY] * n, out_specs=[ANY] * n,
                          out_shape=[jax.ShapeDtypeStruct(sh, a.dtype) for sh, a in zip(shapes, arrs)],
                          scratch_shapes=[pltpu.SemaphoreType.DMA((n,))] * 2)(*arrs)


def _row_tile(rows, width):
    return _tile(rows, max(16, (784 * LANES // width) // 16 * 16), 16)


def _add_halves(g, got, dtype, name):
    _, n, hr, w = g.shape
    tr = _row_tile(hr, w)

    def body(g_ref, got_ref, o_ref):
        c = lax.axis_index("c")
        own = jnp.where(c == 0, g_ref[0], g_ref[1])
        o_ref[...] = (own + got_ref[...]).astype(dtype)

    return pl.pallas_call(
        body, name=name, grid=(hr // tr,),
        in_specs=[pl.BlockSpec((2, n, tr, w), lambda i: (0, 0, i, 0)), pl.BlockSpec((n, tr, w), lambda i: (0, i, 0))],
        out_specs=pl.BlockSpec((n, tr, w), lambda i: (0, i, 0)),
        out_shape=jax.ShapeDtypeStruct((n, hr, w), dtype))(g, got)


def _scatter_chips(gs, name):
    n = len(gs)

    def body(*refs):
        g_refs, out_refs = refs[:n], refs[n:2 * n]
        send, recv = refs[2 * n:]
        x, y, c = _position()
        sends = []
        for i in range(n):
            for j, (fx, fy) in enumerate(_CHIP_FLIPS):
                px, py = _flip(x, fx), _flip(y, fy)
                cp = pltpu.make_async_remote_copy(src_ref=g_refs[i].at[2 * px + py], dst_ref=out_refs[i].at[j],
                                                  send_sem=send.at[i, j], recv_sem=recv.at[i, j],
                                                  device_id=(px, py, c), device_id_type=MESH)
                cp.start()
                sends.append(cp)
        for cp in sends:
            cp.wait_recv()
        for cp in sends:
            cp.wait_send()

    return pl.pallas_call(
        body, name=name, in_specs=[ANY] * n, out_specs=[ANY] * n,
        out_shape=[jax.ShapeDtypeStruct((3,) + g.shape[1:], g.dtype) for g in gs],
        scratch_shapes=[pltpu.SemaphoreType.DMA((n, 3)), pltpu.SemaphoreType.DMA((n, 3))],
    )(*gs)


def _sum_own_and_slots(own, got, name):
    n, r, w = own.shape
    tr = _row_tile(r, w)

    def body(own_ref, got_ref, o_ref):
        me = 2 * lax.axis_index("x") + lax.axis_index("y")
        acc = own_ref[0]
        for i in range(1, n):
            acc = jnp.where(me == i, own_ref[i], acc)
        acc = acc.astype(f32)
        for j in range(3):
            acc = acc + got_ref[j].astype(f32)
        o_ref[...] = acc

    return pl.pallas_call(
        body, name=name, grid=(r // tr,),
        in_specs=[pl.BlockSpec((n, tr, w), lambda i: (0, i, 0)), pl.BlockSpec((3, tr, w), lambda i: (0, i, 0))],
        out_specs=pl.BlockSpec((tr, w), lambda i: (i, 0)), out_shape=jax.ShapeDtypeStruct((r, w), f32))(own, got)


def _gather_devices(s, name):
    def body(s_ref, out_ref, send, recv, loc):
        x, y, c = _position()
        me = 4 * x + 2 * y + c
        lc = pltpu.make_async_copy(s_ref, out_ref.at[me], loc)
        lc.start()
        sends, recvs = [], []
        for j, (fx, fy, fc) in enumerate(_DEV_FLIPS):
            px, py, pc = _flip(x, fx), _flip(y, fy), _flip(c, fc)
            cp = pltpu.make_async_remote_copy(src_ref=s_ref, dst_ref=out_ref.at[me], send_sem=send.at[j],
                                              recv_sem=recv.at[j], device_id=(px, py, pc), device_id_type=MESH)
            cp.start()
            sends.append(cp)
            recvs.append(pltpu.make_async_remote_copy(
                src_ref=s_ref, dst_ref=out_ref.at[4 * px + 2 * py + pc], send_sem=send.at[j], recv_sem=recv.at[j],
                device_id=(px, py, pc), device_id_type=MESH))
        for cp in recvs:
            cp.wait_recv()
        for cp in sends:
            cp.wait_send()
        lc.wait()

    return pl.pallas_call(
        body, name=name, in_specs=[ANY], out_specs=ANY, out_shape=jax.ShapeDtypeStruct((N_DEV,) + s.shape, s.dtype),
        scratch_shapes=[pltpu.SemaphoreType.DMA((7,)), pltpu.SemaphoreType.DMA((7,)), pltpu.SemaphoreType.DMA(())],
    )(s)


def _sum_slots(a, name):
    s, r, c = a.shape
    tr = _tile(r, 2048, 16)

    def body(a_ref, o_ref):
        acc = a_ref[0].astype(f32)
        for i in range(1, s):
            acc = acc + a_ref[i].astype(f32)
        o_ref[...] = acc

    return pl.pallas_call(body, name=name, grid=(r // tr,), in_specs=[pl.BlockSpec((s, tr, c), lambda i: (0, i, 0))],
                          out_specs=pl.BlockSpec((tr, c), lambda i: (i, 0)),
                          out_shape=jax.ShapeDtypeStruct((r, c), f32))(a)


def _adamw(w, g_parts, m, v, name):
    shape = w.shape
    size = w.size
    if shape[-1] % LANES == 0 or size < 8 * LANES:
        view = (size // LANES, LANES) if size % LANES == 0 else (1, size)
        cap = 2048
    else:
        view = (size // shape[-1], shape[-1])
        cap = 128
    rows = view[0]
    tr = _tile(rows, cap, 8) if rows > cap else rows
    n_g = len(g_parts)

    def body(*refs):
        w_ref = refs[0]
        g_refs = refs[1:1 + n_g]
        m_ref, v_ref, g_out, d_out, m_out, v_out = refs[1 + n_g:]
        g = g_refs[0][...]
        for gr in g_refs[1:]:
            g = g + gr[...]
        m_new = ADAM_B1 * m_ref[...] + (1.0 - ADAM_B1) * g
        v_new = ADAM_B2 * v_ref[...] + (1.0 - ADAM_B2) * (g * g)
        m_hat = m_new / (1.0 - ADAM_B1 ** ADAM_STEP)
        v_hat = v_new / (1.0 - ADAM_B2 ** ADAM_STEP)
        g_out[...] = g
        d_out[...] = -ADAM_LR * (m_hat / (jnp.sqrt(v_hat) + ADAM_EPS) + ADAM_WD * w_ref[...])
        m_out[...] = m_new
        v_out[...] = v_new

    spec = pl.BlockSpec((tr, view[1]), lambda i: (i, 0))
    args = [w.reshape(view)] + [g.reshape(view) for g in g_parts] + [m.reshape(view), v.reshape(view)]
    outs = pl.pallas_call(body, name=name, grid=(rows // tr,), in_specs=[spec] * len(args), out_specs=[spec] * 4,
                          out_shape=[jax.ShapeDtypeStruct(view, f32)] * 4)(*args)
    return [o.reshape(shape) for o in outs]


_BIG = ('ffn1_w_gu', 'ffn1_w_down', 'w_in', 'w_out', 'ffn2_w_gu', 'ffn2_w_down')
_SMALL_SHARDED = ('meta_tokens', 'a_conv_w', 'b_w_up', 'b_a_up', 'b_g_up')
_WEIGHTS = ('meta_tokens', 'ffn1_norm', 'ffn1_w_gu', 'ffn1_w_down', 'mix_norm', 'w_in', 'a_conv_w', 'a_log_rate',
            'a_dt_bias', 'a_out_norm', 'b_shift_mu', 'b_w0', 'b_w_up', 'b_a0', 'b_a_up', 'b_g_up', 'b_k_k', 'b_k_a',
            'b_r_k', 'b_ln_gain', 'b_ln_bias', 'w_out', 'ffn2_norm', 'ffn2_w_gu', 'ffn2_w_down', 'final_norm')
_SMALL = tuple(n for n in _WEIGHTS if n not in _BIG)


def _rows_of(shape):
    n = 1
    for d in shape:
        n *= d
    return n, -(-n // LANES)


def _pack(arrs, dtype, row_mult=32):
    parts, total = [], 0
    for a in arrs:
        n, rows = _rows_of(a.shape)
        flat = a.reshape(-1).astype(dtype)
        if n % LANES:
            flat = jnp.pad(flat, (0, rows * LANES - n))
        parts.append(flat)
        total += rows
    extra = -total % row_mult
    if extra:
        parts.append(jnp.zeros((extra * LANES,), dtype))
    return jnp.concatenate(parts).reshape(total + extra, LANES)


def _unpack(packed, shapes, lead=()):
    out, off = [], 0
    for sh in shapes:
        n, rows = _rows_of(sh)
        seg = packed[..., off:off + rows, :]
        if n % LANES:
            seg = seg.reshape(lead + (-1,))[..., :n]
        out.append(seg.reshape(lead + tuple(sh)))
        off += rows
    return out


def _cols_from_shards(s):
    return jnp.concatenate([s[i] for i in range(N_CHIPS)], axis=-1)


def _cols_to_shards(a):
    r, c = a.shape
    return a.reshape(r, N_CHIPS, c // N_CHIPS).transpose(1, 0, 2)


def kernel(x, meta_tokens, ffn1_norm, ffn1_w_gu, ffn1_w_down, mix_norm, w_in, a_conv_w, a_log_rate, a_dt_bias, a_out_norm, b_shift_mu, b_w0, b_w_up, b_a0, b_a_up, b_g_up, b_k_k, b_k_a, b_r_k, b_ln_gain, b_ln_bias, w_out, ffn2_norm, ffn2_w_gu, ffn2_w_down, final_norm, loss_target, m_meta_tokens, m_ffn1_norm, m_ffn1_w_gu, m_ffn1_w_down, m_mix_norm, m_w_in, m_a_conv_w, m_a_log_rate, m_a_dt_bias, m_a_out_norm, m_b_shift_mu, m_b_w0, m_b_w_up, m_b_a0, m_b_a_up, m_b_g_up, m_b_k_k, m_b_k_a, m_b_r_k, m_b_ln_gain, m_b_ln_bias, m_w_out, m_ffn2_norm, m_ffn2_w_gu, m_ffn2_w_down, m_final_norm, v_meta_tokens, v_ffn1_norm, v_ffn1_w_gu, v_ffn1_w_down, v_mix_norm, v_w_in, v_a_conv_w, v_a_log_rate, v_a_dt_bias, v_a_out_norm, v_b_shift_mu, v_b_w0, v_b_w_up, v_b_a0, v_b_a_up, v_b_g_up, v_b_k_k, v_b_k_a, v_b_r_k, v_b_ln_gain, v_b_ln_bias, v_w_out, v_ffn2_norm, v_ffn2_w_gu, v_ffn2_w_down, v_final_norm):
    args = locals()
    wts = {n: args[n] for n in _WEIGHTS}
    mom = {n: args["m_" + n] for n in _WEIGHTS}
    var = {n: args["v_" + n] for n in _WEIGHTS}
    chip = 2 * lax.axis_index("x") + lax.axis_index("y")

    big_shapes = [wts[n].shape[1:] for n in _BIG]
    small_shapes = [wts[n].shape[-2:] for n in _SMALL_SHARDED]
    big_flat = [wts[n].astype(bf16).reshape((-1, LANES) if wts[n].shape[-1] % LANES == 0 else wts[n].shape[1:])
                for n in _BIG]
    small_packed = _pack([wts[n] for n in _SMALL_SHARDED], f32)
    gathered = _gather_chips(big_flat + [small_packed], "gather_weights")
    gu1, dn1, w_in_s, w_out_s, gu2, dn2 = [a.reshape((N_CHIPS,) + tuple(sh)) for a, sh in zip(gathered, big_shapes)]
    meta_s, conv_s, wup_s, aup_s, gup_s = _unpack(gathered[-1], small_shapes, (N_CHIPS,))
    w = {
        'ffn1_norm': ffn1_norm, 'mix_norm': mix_norm, 'ffn2_norm': ffn2_norm, 'final_norm': final_norm[None, :],
        'ffn1_wgu': gu1, 'ffn1_wd': dn1.reshape(D_FF, D), 'ffn2_wgu': gu2, 'ffn2_wd': dn2.reshape(D_FF, D),
        'w_in_p': _win_to_padded(_cols_from_shards(w_in_s)), 'w_out': w_out_s.reshape(D, D),
        'a_conv_w': _cols_from_shards(conv_s), 'b_w_up': _cols_from_shards(wup_s), 'b_a_up': _cols_from_shards(aup_s),
        'b_g_up': _cols_from_shards(gup_s),
        'a_log_rate': a_log_rate, 'a_dt_bias': a_dt_bias, 'a_out_norm': a_out_norm, 'b_shift_mu': b_shift_mu,
        'b_w0': b_w0, 'b_a0': b_a0, 'b_k_k': b_k_k, 'b_k_a': b_k_a, 'b_r_k': b_r_k, 'b_ln_gain': b_ln_gain,
        'b_ln_bias': b_ln_bias,
    }
    meta_full = _cols_from_shards(meta_s)

    h0 = jnp.concatenate([jnp.zeros((PAD, D), f32), meta_full, x[0]], axis=0)
    tgt = jnp.concatenate([jnp.zeros((SKIP, D), f32), loss_target[0]], axis=0)
    loss_local, d_h0, g = _local_step(h0, tgt, w)
    loss = lax.psum(loss_local, ("x", "y", "c"))
    grad_x = d_h0[SKIP:][None]

    big_grads = [
        jnp.concatenate([g['ffn1_wg'], g['ffn1_wu']], axis=0),
        g['ffn1_wd'].reshape(N_CHIPS, D_FF // N_CHIPS, D),
        _cols_to_shards(_win_from_padded(g['w_in_p'])),
        g['w_out'].reshape(N_CHIPS, D // N_CHIPS, D),
        jnp.concatenate([g['ffn2_wg'], g['ffn2_wu']], axis=0),
        g['ffn2_wd'].reshape(N_CHIPS, D_FF // N_CHIPS, D),
    ]
    packed = [i for i, sh in enumerate(big_shapes) if sh[-1] % LANES == 0]
    native = [i for i in range(len(big_shapes)) if i not in packed]
    assert len(native) == 1
    half_rows = [_rows_of(big_shapes[i])[1] // 2 for i in packed]
    g_halves = [jnp.concatenate([big_grads[i].reshape(N_CHIPS, 2, hr, LANES).transpose(1, 0, 2, 3)
                                 for i, hr in zip(packed, half_rows)], axis=2)]
    nat = big_grads[native[0]]
    g_halves.append(nat.reshape(N_CHIPS, 2, nat.shape[1] // 2, nat.shape[2]).transpose(1, 0, 2, 3))
    sib_halves = _swap_sibling(g_halves, lambda ref, c: ref.at[1 - c], [a.shape[1:] for a in g_halves], "swap_halves")
    chip_halves = [_add_halves(a, b, bf16, f"add_sibling{i}") for i, (a, b) in enumerate(zip(g_halves, sib_halves))]
    got = _scatter_chips(chip_halves, "scatter_grads")
    mine = [_sum_own_and_slots(a, b, f"sum_chips{i}") for i, (a, b) in enumerate(zip(chip_halves, got))]
    theirs = _swap_sibling(mine, lambda ref, c: ref, [a.shape for a in mine], "swap_sums")
    core = lax.axis_index("c")
    low = [jnp.where(core == 0, a, b) for a, b in zip(mine, theirs)]
    high = [jnp.where(core == 0, b, a) for a, b in zip(mine, theirs)]
    big_parts, off = [None] * len(big_shapes), 0
    for i, hr in zip(packed, half_rows):
        big_parts[i] = jnp.concatenate([low[0][off:off + hr], high[0][off:off + hr]], axis=0)
        off += hr
    big_parts[native[0]] = jnp.concatenate([low[1], high[1]], axis=0)

    small_full = {
        'meta_tokens': d_h0[PAD:SKIP], 'ffn1_norm': g['ffn1_norm'], 'mix_norm': g['mix_norm'], 'a_conv_w': g['a_conv_w'],
        'a_log_rate': g['a_log_rate'], 'a_dt_bias': g['a_dt_bias'], 'a_out_norm': g['a_out_norm'],
        'b_shift_mu': g['b_shift_mu'], 'b_w0': g['b_w0'], 'b_w_up': g['b_w_up'], 'b_a0': g['b_a0'], 'b_a_up': g['b_a_up'],
        'b_g_up': g['b_g_up'], 'b_k_k': g['b_k_k'], 'b_k_a': g['b_k_a'], 'b_r_k': g['b_r_k'], 'b_ln_gain': g['b_ln_gain'],
        'b_ln_bias': g['b_ln_bias'], 'ffn2_norm': g['ffn2_norm'], 'final_norm': g['final_norm'],
    }
    s_shapes = [small_full[n].shape for n in _SMALL]
    s_sum = _sum_slots(_gather_devices(_pack([small_full[n] for n in _SMALL], f32, row_mult=256), "gather_small"),
                       "sum_small")
    s_parts = dict(zip(_SMALL, _unpack(s_sum, s_shapes)))

    grad, delta, new_m, new_v = {}, {}, {}, {}
    for n, a in zip(_BIG, big_parts):
        grad[n], delta[n], new_m[n], new_v[n] = _adamw(wts[n], [a.reshape(wts[n].shape)], mom[n], var[n], f"adamw_{n}")
    for n in _SMALL:
        gs = s_parts[n]
        if n in _SMALL_SHARDED:
            width = wts[n].shape[-1]
            gs = lax.dynamic_slice_in_dim(gs, chip * width, width, axis=gs.ndim - 1)
        gs = gs.reshape(wts[n].shape)
        grad[n], delta[n], new_m[n], new_v[n] = _adamw(wts[n], [gs], mom[n], var[n], f"adamw_{n}")

    return (loss, grad_x, *[grad[n] for n in _WEIGHTS], *[delta[n] for n in _WEIGHTS],
            *[new_m[n] for n in _WEIGHTS], *[new_v[n] for n in _WEIGHTS])
```

```python
import functools

import jax
import jax.numpy as jnp
from jax import lax
from jax.experimental import pallas as pl
from jax.experimental.pallas import tpu as pltpu

f32 = jnp.float32
bf16 = jnp.bfloat16
HI = lax.Precision.HIGHEST
MESH = pl.DeviceIdType.MESH
ANY = pl.BlockSpec(memory_space=pl.ANY)

D = 1024
N_META = 16
CHUNK = 64
PAD = CHUNK - N_META
SKIP = PAD + N_META
EPS = 1e-6
D_FF = 2816
A_HEADS = 8
A_DK = 128
B_HEADS = 16
B_N = 64
B_GN_EPS = B_N * 1e-5
W_LORA, AA_LORA, G_LORA = 64, 64, 160
IN_TOTAL = 9520
ZP = 9600
LANES = 128
N_CHIPS = 4
N_DEV = 8

ADAM_LR, ADAM_B1, ADAM_B2, ADAM_EPS, ADAM_WD, ADAM_STEP = 0.001, 0.9, 0.999, 1e-08, 0.01, 10

MXU_DTYPE = bf16


def _tile(n, cap, mult):
    if n <= cap:
        return n
    best = None
    for t in range(mult, cap + 1, mult):
        if n % t == 0:
            best = t
    assert best is not None, (n, cap, mult)
    return best


def _sigmoid(x):
    return jax.nn.sigmoid(x)


def _silu(x):
    return x * jax.nn.sigmoid(x)


def _softplus(x):
    return jnp.maximum(x, 0.0) + jnp.log(1.0 + jnp.exp(-jnp.abs(x)))


def _head_matrix(c, nh):
    hd = c // nh
    r = lax.broadcasted_iota(jnp.int32, (c, nh), 0)
    h = lax.broadcasted_iota(jnp.int32, (c, nh), 1)
    return (r >= h * hd) & (r < (h + 1) * hd)


def _dot_exact_rhs(x, e, cb):
    dn = (((1,), (cb,)), ((), ()))
    if SCAN_PASSES == 0:
        return lax.dot_general(x, e.astype(f32), dn, precision=HI, preferred_element_type=f32)
    eb = e.astype(bf16)
    hi = x.astype(bf16)
    lo = (x - hi.astype(f32)).astype(bf16)
    return (lax.dot_general(hi, eb, dn, preferred_element_type=f32)
            + lax.dot_general(lo, eb, dn, preferred_element_type=f32))


def _head_sum_impl(x, nh):
    e = _head_matrix(x.shape[-1], nh)
    return _dot_exact_rhs(_dot_exact_rhs(x, e, 0), e, 1)


@functools.partial(jax.custom_vjp, nondiff_argnums=(1,))
def _head_sum(x, nh):
    return _head_sum_impl(x, nh)


def _head_sum_fwd(x, nh):
    return _head_sum_impl(x, nh), None


def _head_sum_bwd(nh, _, g):
    return (_head_sum_impl(g, nh),)


_head_sum.defvjp(_head_sum_fwd, _head_sum_bwd)


@functools.partial(jax.custom_vjp, nondiff_argnums=(1,))
def _shift_rows(x, s):
    n = x.shape[0]
    row = lax.broadcasted_iota(jnp.int32, x.shape, 0)
    if s > 0:
        return jnp.where(row >= s, pltpu.roll(x, s, 0), 0.0)
    return jnp.where(row < n + s, pltpu.roll(x, n + s, 0), 0.0)


def _shift_rows_fwd(x, s):
    return _shift_rows(x, s), None


def _shift_rows_bwd(s, _, g):
    return (_shift_rows(g, -s),)


_shift_rows.defvjp(_shift_rows_fwd, _shift_rows_bwd)


def _matmul(a, b, *, ta=False, tb=False, res=None, scale=1.0, name, b_cols_split=None, out_cols_split=False):
    assert not (ta and tb)
    (ar, ac) = a.shape
    b0 = 0
    if b_cols_split:
        b0, bs = b_cols_split
        _, br, bc_part = b.shape
        bc = bs * bc_part
    else:
        br, bc = b.shape
    m, k = (ac, ar) if ta else (ar, ac)
    n, kb = (br, bc) if tb else (bc, br)
    assert k == kb, (a.shape, b.shape, ta, tb)
    tm = _tile(m, 1408, LANES) if ta else _tile(m, 832, 8)
    tn = _tile(n, 1408, LANES)
    tk = _tile(k, 1040, 8) if ta else _tile(k, 1408, LANES)
    nk = k // tk
    dn = (((0 if ta else 1,), (1 if tb else 0,)), ((), ()))
    if b_cols_split:
        assert (tk if tb else tn) == bc_part, (b.shape, tn, tk)

    def body(*refs):
        if res is not None:
            a_ref, b_ref, r_ref, o_ref, acc = refs
        else:
            a_ref, b_ref, o_ref, acc = refs
        kk = pl.program_id(2)

        @pl.when(kk == 0)
        def _():
            acc[...] = jnp.zeros_like(acc)

        acc[...] += lax.dot_general(a_ref[...].astype(MXU_DTYPE), b_ref[...].astype(MXU_DTYPE), dn,
                                    preferred_element_type=f32,
                                    precision=None if MXU_DTYPE == bf16 else HI)

        @pl.when(kk == nk - 1)
        def _():
            out = acc[...]
            if scale != 1.0:
                out = out * scale
            if res is not None:
                out = r_ref[...] + out
            o_ref[...] = out

    if ta:
        a_spec = pl.BlockSpec((tk, tm), lambda i, j, kk: (kk, i))
    else:
        a_spec = pl.BlockSpec((tm, tk), lambda i, j, kk: (i, kk))
    if tb and b_cols_split:
        b_spec = pl.BlockSpec((None, tn, tk), lambda i, j, kk: (kk + b0, j, 0))
    elif tb:
        b_spec = pl.BlockSpec((tn, tk), lambda i, j, kk: (j, kk))
    elif b_cols_split:
        b_spec = pl.BlockSpec((None, tk, tn), lambda i, j, kk: (j + b0, kk, 0))
    else:
        b_spec = pl.BlockSpec((tk, tn), lambda i, j, kk: (kk, j))
    in_specs = [a_spec, b_spec]
    args = [a, b]
    if res is not None:
        in_specs.append(pl.BlockSpec((tm, tn), lambda i, j, kk: (i, j)))
        args.append(res)
    if out_cols_split:
        out_spec = pl.BlockSpec((None, tm, tn), lambda i, j, kk: (j, i, 0))
        out_shape = jax.ShapeDtypeStruct((n // tn, m, tn), f32)
    else:
        out_spec = pl.BlockSpec((tm, tn), lambda i, j, kk: (i, j))
        out_shape = jax.ShapeDtypeStruct((m, n), f32)
    return pl.pallas_call(
        body, name=name, grid=(m // tm, n // tn, nk), in_specs=in_specs, out_specs=out_spec, out_shape=out_shape,
        scratch_shapes=[pltpu.VMEM((tm, tn), f32)],
        compiler_params=pltpu.CompilerParams(dimension_semantics=("parallel", "parallel", "arbitrary")),
    )(*args)


def _tw_fwd(fn, ins, in_specs, out_shapes, out_specs, grid, name, with_pid=False):
    n_in = len(ins)

    def body(*refs):
        vals = [r[...] for r in refs[:n_in]]
        outs = fn(pl.program_id(0), *vals) if with_pid else fn(*vals)
        for r, o in zip(refs[n_in:], outs):
            r[...] = o.astype(r.dtype)

    return pl.pallas_call(body, name=name, grid=grid, in_specs=in_specs, out_specs=out_specs,
                          out_shape=out_shapes)(*ins)


def _tw_bwd(fn, ins, in_specs, cts, ct_specs, kinds, grid, name, with_pid=False, tile_dtype=f32, ct_extra=(),
            residual=None):
    n_in, n_ct = len(ins), len(cts)
    diff = [i for i, kd in enumerate(kinds) if kd is not None]
    n_ex = len(ct_extra)

    def body(*refs):
        vals = [r[...] for r in refs[:n_in]]
        ctv = [r[...].astype(f32) for r in refs[n_in:n_in + n_ct]]
        for (ci, _), r in zip(ct_extra, refs[n_in + n_ct:n_in + n_ct + n_ex]):
            ctv[ci] = ctv[ci] + r[...]
        ctv = tuple(ctv)
        n_fixed = n_in + n_ct + n_ex
        res_ref = refs[n_fixed] if residual is not None else None
        g_refs = refs[n_fixed + (residual is not None):]
        pid = pl.program_id(0)

        def f(*dv):
            full = list(vals)
            for i, v in zip(diff, dv):
                full[i] = v
            out = fn(pid, *full) if with_pid else fn(*full)
            return tuple(out)

        _, vjp = jax.vjp(f, *[vals[i] for i in diff])
        gs = vjp(ctv)
        first = pid == 0
        for i2 in range(1, len(grid)):
            first = first & (pl.program_id(i2) == 0)
        for i, g, g_ref in zip(diff, gs, g_refs):
            if kinds[i] != 'acc':
                if i == 0 and res_ref is not None:
                    g = res_ref[...] + g
                g_ref[...] = g.astype(g_ref.dtype)
            else:
                @pl.when(first)
                def _(g=g, g_ref=g_ref):
                    g_ref[...] = g

                @pl.when(jnp.logical_not(first))
                def _(g=g, g_ref=g_ref):
                    g_ref[...] += g

    zero_map = {1: lambda *a: (0,), 2: lambda *a: (0, 0), 3: lambda *a: (0, 0, 0)}
    out_specs, out_shapes = [], []
    for i in diff:
        if kinds[i] == 'tile':
            out_shapes.append(jax.ShapeDtypeStruct(ins[i].shape, tile_dtype))
            out_specs.append(in_specs[i])
        elif kinds[i] == 'acc':
            out_shapes.append(jax.ShapeDtypeStruct(ins[i].shape, f32))
            out_specs.append(pl.BlockSpec(ins[i].shape, zero_map[ins[i].ndim]))
        else:
            out_shapes.append(jax.ShapeDtypeStruct(kinds[i][1], tile_dtype))
            out_specs.append(kinds[i][2])
    extra_specs = [ct_specs[ci] for ci, _ in ct_extra]
    extra = [a for _, a in ct_extra]
    if residual is not None:
        assert kinds[0] == 'tile'
        extra_specs.append(in_specs[0])
        extra.append(residual)
    return pl.pallas_call(body, name=name, grid=grid, in_specs=list(in_specs) + list(ct_specs) + extra_specs,
                          out_specs=out_specs, out_shape=out_shapes)(*ins, *cts, *extra)


def _row_spec(tm, c, col_block=0):
    return pl.BlockSpec((tm, c), lambda i, cb=col_block: (i, cb))


def _full_spec(shape):
    nd = len(shape)
    return pl.BlockSpec(shape, lambda *a, nd=nd: (0,) * nd)


def _f_rms(x, g):
    return (x * lax.rsqrt(jnp.mean(x * x, axis=-1, keepdims=True) + EPS) * g,)


def _f_swiglu(gate, up):
    return (_silu(gate) * up,)


def _f_loss(pid, h, g, tgt, *, tm):
    y = h * lax.rsqrt(jnp.mean(h * h, axis=-1, keepdims=True) + EPS) * g
    row = pid * tm + lax.broadcasted_iota(jnp.int32, (tm, 1), 0)
    err = jnp.where(row >= SKIP, y - tgt, 0.0)
    per_row = jnp.mean(err * err, axis=-1, keepdims=True)
    return (0.5 * jnp.sum(per_row, axis=0, keepdims=True),)


def _f_conv(x, w, *, norm, scale):
    y = x * w[3:4, :]
    for s in (1, 2, 3):
        y = y + _shift_rows(x, s) * w[3 - s:4 - s, :]
    y = _silu(y)
    if norm:
        y = y * lax.rsqrt(jnp.sum(y * y, axis=-1, keepdims=True) + 1e-6) * scale
    return (y,)


def _f_dgates(pid, abeta, aalpha, log_rate, dt_bias, *, tm):
    row = pid * tm + lax.broadcasted_iota(jnp.int32, (tm, 1), 0)
    live = row >= PAD
    beta = jnp.where(live, _sigmoid(abeta), 0.0)
    g = jnp.where(live, -jnp.exp(log_rate) * _softplus(aalpha + dt_bias), 0.0)
    return beta, g


def _f_tshift(z, mu):
    return (z + (_shift_rows(z, 1) - z) * mu,)


def _f_rwkv_pre(k, wd, ad, gd, w0, w_up, a0, a_up, g_up, k_k, k_a):
    w_log = -_softplus(-(w0 + _smm(jnp.tanh(wd), w_up))) - 0.5
    lw = -jnp.exp(w_log)
    a_lr = _sigmoid(a0 + _smm(ad, a_up))
    gate = _smm(_sigmoid(gd), g_up)
    kkp = k * k_k
    kk = kkp * lax.rsqrt(_head_sum(kkp * kkp, B_HEADS) + 1e-6)
    kmod = k * (1.0 + (a_lr - 1.0) * k_a)
    return lw, kmod, -kk, kk * a_lr, gate


def _f_mix_post(o, az, y, r, kmod, v, gate, ga, gb, out_gain, ln_g, ln_b, r_k):
    ms = _head_sum(o * o, A_HEADS) * (1.0 / A_DK)
    oa = o * lax.rsqrt(ms + EPS) * out_gain * _silu(az)
    mean = _head_sum(y, B_HEADS) * (1.0 / B_N)
    yc = y - mean
    var = _head_sum(yc * yc, B_HEADS) * (1.0 / B_N)
    yn = yc * lax.rsqrt(var + B_GN_EPS) * ln_g + ln_b
    bonus = _head_sum(r * kmod * r_k, B_HEADS) * v
    ob = (yn + bonus) * gate
    return (_sigmoid(ga) * oa + _sigmoid(gb) * ob,)


SCAN_PASSES = 3


def _split2(a):
    hi = a.astype(bf16)
    return hi, (a - hi.astype(f32)).astype(bf16)


def _dot_passes(a, b, ca, cb, passes):
    dn = (((ca,), (cb,)), ((), ()))
    if SCAN_PASSES == 0:
        return lax.dot_general(a, b, dn, precision=HI, preferred_element_type=f32)
    if passes == 1:
        return lax.dot_general(a.astype(bf16), b.astype(bf16), dn, preferred_element_type=f32)
    ah, al = _split2(a)
    bh, bl = _split2(b)
    return (lax.dot_general(ah, bh, dn, preferred_element_type=f32)
            + (lax.dot_general(ah, bl, dn, preferred_element_type=f32)
               + lax.dot_general(al, bh, dn, preferred_element_type=f32)))


@functools.partial(jax.custom_vjp, nondiff_argnums=(2, 3, 4))
def _sdot(a, b, ca, cb, passes):
    return _dot_passes(a, b, ca, cb, passes)


def _sdot_fwd(a, b, ca, cb, passes):
    return _dot_passes(a, b, ca, cb, passes), (a, b)


def _sdot_bwd(ca, cb, passes, res, g):
    a, b = res
    if (ca, cb) == (1, 0):
        return _dot_passes(g, b, 1, 1, passes), _dot_passes(a, g, 0, 0, passes)
    if (ca, cb) == (1, 1):
        return _dot_passes(g, b, 1, 0, passes), _dot_passes(g, a, 0, 0, passes)
    assert (ca, cb) == (0, 0)
    return _dot_passes(b, g, 1, 1, passes), _dot_passes(a, g, 1, 0, passes)


_sdot.defvjp(_sdot_fwd, _sdot_bwd)


def _smm(a, b, passes=3):
    return _sdot(a, b, 1, 0, passes)


def _smm_nt(a, b, passes=3):
    return _sdot(a, b, 1, 1, passes)


def _smm_tn(a, b, passes=3):
    return _sdot(a, b, 0, 0, passes)


def _tri_dot(x, ca):
    n = x.shape[0]
    incl = _tri_masks(n)[0]
    dn = (((ca,), (0,)), ((), ()))
    if SCAN_PASSES == 0:
        return lax.dot_general(incl.astype(f32), x, dn, precision=HI, preferred_element_type=f32)
    tri = incl.astype(bf16)
    hi, r1 = x.astype(bf16), None
    r1 = x - hi.astype(f32)
    mid = r1.astype(bf16)
    lo = (r1 - mid.astype(f32)).astype(bf16)
    return (lax.dot_general(tri, hi, dn, preferred_element_type=f32)
            + (lax.dot_general(tri, mid, dn, preferred_element_type=f32)
               + lax.dot_general(tri, lo, dn, preferred_element_type=f32)))


@jax.custom_vjp
def _cumsum_rows(x):
    return _tri_dot(x, 1)


def _cumsum_rows_fwd(x):
    return _tri_dot(x, 1), None


def _cumsum_rows_bwd(_, g):
    return (_tri_dot(g, 0),)


_cumsum_rows.defvjp(_cumsum_rows_fwd, _cumsum_rows_bwd)


def _tri_masks(n):
    i = lax.broadcasted_iota(jnp.int32, (n, n), 0)
    j = lax.broadcasted_iota(jnp.int32, (n, n), 1)
    return i >= j, i > j, i == j, i <= j


def _unit_lower_inv_impl(low, passes):
    n = low.shape[0]
    assert n == CHUNK
    _, _, eye, _ = _tri_masks(n)
    acc = eye.astype(f32) + low
    p = low
    for _ in range(5):
        p = _dot_passes(p, p, 1, 0, passes)
        acc = acc + _dot_passes(acc, p, 1, 0, passes)
    return acc


@functools.partial(jax.custom_vjp, nondiff_argnums=(1,))
def _unit_lower_inv(low, passes=3):
    return _unit_lower_inv_impl(low, passes)


def _unit_lower_inv_fwd(low, passes):
    t = _unit_lower_inv_impl(low, passes)
    return t, t


def _unit_lower_inv_bwd(passes, t, g):
    return (_dot_passes(_dot_passes(t, g, 0, 0, passes), t, 1, 1, passes),)


_unit_lower_inv.defvjp(_unit_lower_inv_fwd, _unit_lower_inv_bwd)

DELTA_PASSES = 1
DELTA_INV_PASSES = 1


def _delta_chunk(s, q, k, v, beta_row, g_row):
    p = DELTA_PASSES
    incl, strict, eye, upper = _tri_masks(CHUNK)
    beta = jnp.sum(jnp.where(eye, beta_row, 0.0), axis=1, keepdims=True)
    g = jnp.sum(jnp.where(eye, g_row, 0.0), axis=1, keepdims=True)
    gc = jnp.sum(jnp.where(incl, g_row, 0.0), axis=1, keepdims=True)
    gc_row = jnp.sum(jnp.where(upper, g, 0.0), axis=0, keepdims=True)
    decay = jnp.where(incl, jnp.exp(jnp.where(incl, gc - gc_row, 0.0)), 0.0)
    kb = k * beta
    vb = v * beta
    m = jnp.where(strict, _smm_nt(kb, k, p) * decay, 0.0)
    tinv = _unit_lower_inv(-m, DELTA_INV_PASSES)
    u = _smm(tinv, vb, p)
    wk = _smm(tinv, kb * jnp.exp(gc), p)
    attn = _smm_nt(q, k, p) * decay
    qg = q * jnp.exp(gc)
    g_last = jnp.sum(g, axis=0, keepdims=True)
    k_tail = k * jnp.exp(g_last - gc)
    v_new = u - _smm(wk, s, p)
    o = _smm(qg, s, p) + _smm(attn, v_new, p)
    s_new = s * jnp.exp(g_last) + _smm_tn(k_tail, v_new, p)
    return o, s_new


RWKV_PASSES = 1
RWKV_INV_PASSES = 1


def _rwkv_chunk(st, r, k, v, a, b, lw):
    c = CHUNK
    p, pi = RWKV_PASSES, RWKV_INV_PASSES
    _, strict, _, _ = _tri_masks(c)
    lane = lax.broadcasted_iota(jnp.int32, (c, 2 * B_N), 1)
    row = lax.broadcasted_iota(jnp.int32, (c, 2 * B_N), 0)
    first = lane < B_N
    incl2 = row >= jnp.where(first, lane, lane - B_N)
    bi = lax.broadcasted_iota(jnp.int32, (2 * B_N, 2 * B_N), 0) < B_N
    bj = lax.broadcasted_iota(jnp.int32, (2 * B_N, 2 * B_N), 1) < B_N
    blockdiag = bi == bj
    cum = _cumsum_rows(lw)
    e_pos = jnp.exp(cum)
    e_neg = jnp.exp(-cum)
    rt = r * e_pos
    at = a * jnp.exp(cum - lw)
    kt = k * e_neg
    bt = b * e_neg
    bk = jnp.concatenate([bt, kt], axis=0)
    a_s0 = _smm_nt(at, st, p)
    r_s0 = _smm_nt(rt, st, p)
    heads = (first, jnp.logical_not(first))
    u = jnp.zeros((c, 2 * B_N), f32)
    for sel in heads:
        at_h = jnp.where(sel, at, 0.0)
        ab = jnp.where(strict, _smm_nt(at_h, bt, pi), 0.0)
        ak = jnp.where(strict, _smm_nt(at_h, kt, p), 0.0)
        t_h = _unit_lower_inv(ab, pi)
        u = u + _smm(t_h, jnp.where(sel, a_s0, 0.0) + _smm(ak, jnp.where(sel, v, 0.0), p), p)
    y = r_s0
    for sel in heads:
        rbk = jnp.where(incl2, _smm_nt(jnp.where(sel, rt, 0.0), bk, p), 0.0)
        uv = jnp.concatenate([jnp.where(sel, u, 0.0), jnp.where(sel, v, 0.0)], axis=0)
        y = y + _smm(rbk, uv, p)
    cl = jnp.sum(lw, axis=0, keepdims=True)
    dec = jnp.exp(cl - cum)
    uv_all = jnp.concatenate([u, v], axis=0)
    bk_dec = jnp.concatenate([b * dec, k * dec], axis=0)
    st_new = st * jnp.exp(cl) + jnp.where(blockdiag, _smm_tn(uv_all, bk_dec, p), 0.0)
    return y, st_new


GROUPS_PER_STEP = 8


def _scan_specs(ins, col_offs, n_chunks, reverse):
    gw = GROUPS_PER_STEP * LANES
    cidx = (lambda c: n_chunks - 1 - c) if reverse else (lambda c: c)
    specs = []
    for a, off in zip(ins, col_offs):
        if a.ndim == 2:
            assert off % gw == 0
            specs.append(pl.BlockSpec((CHUNK, gw), lambda h, c, o=off // gw: (cidx(c), h + o)))
        else:
            specs.append(pl.BlockSpec((GROUPS_PER_STEP, None, 1, CHUNK), lambda h, c: (h, cidx(c), 0, 0)))
    return specs, cidx


def _group_vals(refs, g):
    return [r[:, g * LANES:(g + 1) * LANES] if len(r.shape) == 2 else r[g] for r in refs]


def _scan_fwd(chunk_fn, ins, col_offs, n_groups, n_chunks, state_shape, name):
    n_in = len(ins)
    gps = GROUPS_PER_STEP
    t = ins[0].shape[0]

    def body(*refs):
        in_refs = refs[:n_in]
        o_ref, s0_ref, st = refs[n_in:]

        @pl.when(pl.program_id(1) == 0)
        def _():
            st[...] = jnp.zeros_like(st)

        states = st[...]
        vals = [jnp.stack(col) for col in zip(*[_group_vals(in_refs, g) for g in range(gps)])]
        o, s_new = jax.vmap(chunk_fn)(states, *vals)
        s0_ref[...] = states
        st[...] = s_new
        for g in range(gps):
            o_ref[:, g * LANES:(g + 1) * LANES] = o[g]

    specs, _ = _scan_specs(ins, col_offs, n_chunks, False)
    return pl.pallas_call(
        body, name=name, grid=(n_groups // gps, n_chunks), in_specs=specs,
        out_specs=[pl.BlockSpec((CHUNK, gps * LANES), lambda h, c: (c, h)),
                   pl.BlockSpec((gps, None) + state_shape, lambda h, c: (h, c, 0, 0))],
        out_shape=[jax.ShapeDtypeStruct((t, n_groups * LANES), f32),
                   jax.ShapeDtypeStruct((n_groups, n_chunks) + state_shape, f32)],
        scratch_shapes=[pltpu.VMEM((gps,) + state_shape, f32)],
        compiler_params=pltpu.CompilerParams(dimension_semantics=("parallel", "arbitrary")),
    )(*ins)


def _scan_bwd(chunk_fn, s0s, ins, col_offs, d_out, n_groups, n_chunks, state_shape, name):
    n_in = len(ins)
    gps = GROUPS_PER_STEP
    t = d_out.shape[0]

    def body(*refs):
        s0_ref = refs[0]
        in_refs = refs[1:1 + n_in]
        do_ref = refs[1 + n_in]
        g_refs = refs[2 + n_in:2 + 2 * n_in]
        dst = refs[2 + 2 * n_in]

        @pl.when(pl.program_id(1) == 0)
        def _():
            dst[...] = jnp.zeros_like(dst)

        vals = [jnp.stack(col) for col in zip(*[_group_vals(in_refs, g) for g in range(gps)])]
        d_o = jnp.stack([do_ref[:, g * LANES:(g + 1) * LANES] for g in range(gps)])
        _, vjp = jax.vjp(jax.vmap(chunk_fn), s0_ref[...], *vals)
        gs = vjp((d_o, dst[...]))
        dst[...] = gs[0]
        for g_ref, gv in zip(g_refs, gs[1:]):
            if len(g_ref.shape) == 2:
                for g in range(gps):
                    g_ref[:, g * LANES:(g + 1) * LANES] = gv[g]
            else:
                g_ref[...] = gv

    specs, cidx = _scan_specs(ins, col_offs, n_chunks, True)
    out_lane = pl.BlockSpec((CHUNK, gps * LANES), lambda h, c: (cidx(c), h))
    g_specs = [out_lane if a.ndim == 2 else sp for a, sp in zip(ins, specs)]
    g_shapes = [(t, n_groups * LANES) if a.ndim == 2 else a.shape for a in ins]
    s0_spec = pl.BlockSpec((gps, None) + state_shape, lambda h, c: (h, cidx(c), 0, 0))
    return pl.pallas_call(
        body, name=name, grid=(n_groups // gps, n_chunks), in_specs=[s0_spec] + specs + [out_lane],
        out_specs=g_specs, out_shape=[jax.ShapeDtypeStruct(sh, f32) for sh in g_shapes],
        scratch_shapes=[pltpu.VMEM((gps,) + state_shape, f32)],
        compiler_params=pltpu.CompilerParams(dimension_semantics=("parallel", "arbitrary")),
    )(s0s, *ins, d_out)


def _rms_fwd(x, g, name):
    t = x.shape[0]
    tm = _tile(t, 416, 16)
    return _tw_fwd(_f_rms, [x, g], [_row_spec(tm, D), _full_spec(g.shape)],
                   [jax.ShapeDtypeStruct(x.shape, MXU_DTYPE)], [_row_spec(tm, D)], (t // tm,), name)[0]


def _rms_bwd(x, g, dy, residual, name):
    t = x.shape[0]
    tm = _tile(t, 416, 8)
    return _tw_bwd(_f_rms, [x, g], [_row_spec(tm, D), _full_spec(g.shape)], [dy], [_row_spec(tm, D)],
                   ['tile', 'acc'], (t // tm,), name, residual=residual)


def _ffn_fwd(h, gain, wgu, wd, tag):
    xn = _rms_fwd(h, gain, f"{tag}_rms")
    gate = _matmul(xn, wgu, b_cols_split=(0, 2), name=f"{tag}_gate")
    up = _matmul(xn, wgu, b_cols_split=(2, 2), name=f"{tag}_up")
    t = h.shape[0]
    tm = _tile(t, 208, 16)
    act = _tw_fwd(_f_swiglu, [gate, up], [_row_spec(tm, D_FF)] * 2, [jax.ShapeDtypeStruct((t, D_FF), MXU_DTYPE)],
                  [_row_spec(tm, D_FF)], (t // tm,), f"{tag}_act")[0]
    out = _matmul(act, wd, res=h, scale=0.5, name=f"{tag}_down")
    return out, (xn, gate, up, act)


def _ffn_bwd(h, gain, wgu, wd, saved, dout, tag):
    xn, gate, up, act = saved
    t = h.shape[0]
    d_wd = _matmul(act, dout, ta=True, scale=0.5, name=f"{tag}_dwd")
    d_act = _matmul(dout, wd, tb=True, scale=0.5, name=f"{tag}_dact")
    tm = _tile(t, 208, 16)
    d_gate, d_up = _tw_bwd(_f_swiglu, [gate, up], [_row_spec(tm, D_FF)] * 2, [d_act], [_row_spec(tm, D_FF)],
                           ['tile', 'tile'], (t // tm,), f"{tag}_dactf", tile_dtype=MXU_DTYPE)
    d_wg = _matmul(xn, d_gate, ta=True, out_cols_split=True, name=f"{tag}_dwg")
    d_wu = _matmul(xn, d_up, ta=True, out_cols_split=True, name=f"{tag}_dwu")
    d_xn = _matmul(d_gate, wgu, tb=True, b_cols_split=(0, 2), name=f"{tag}_dxn_g")
    d_xn = _matmul(d_up, wgu, tb=True, b_cols_split=(2, 2), res=d_xn, name=f"{tag}_dxn_u")
    d_h, d_gain = _rms_bwd(h, gain, d_xn, dout, f"{tag}_drms")
    return d_h, d_gain, d_wg, d_wu, d_wd


def _col_spec(t, first_block):
    return pl.BlockSpec((t, LANES), lambda j, fb=first_block: (0, j + fb))


def _local_step(h0, tgt, w):
    t = h0.shape[0]
    assert t % CHUNK == 0
    nc = t // CHUNK
    grads = {}

    h1, ffn1_saved = _ffn_fwd(h0, w['ffn1_norm'], w['ffn1_wgu'], w['ffn1_wd'], "ffn1")
    u = _rms_fwd(h1, w['mix_norm'], "mix_rms")
    z = _matmul(u, w['w_in_p'], name="in_proj")
    zs = z[:, 9216:9216 + 304]
    abeta, aalpha = zs[:, 288:296], zs[:, 296:304]

    conv_w = w['a_conv_w']
    conv_fns = [functools.partial(_f_conv, norm=True, scale=A_DK ** -0.5),
                functools.partial(_f_conv, norm=True, scale=1.0),
                functools.partial(_f_conv, norm=False, scale=1.0)]
    qkv = []
    for idx, fn in enumerate(conv_fns):
        qkv.append(_tw_fwd(fn, [z, conv_w], [_col_spec(t, 8 * idx), pl.BlockSpec((4, LANES), lambda j, o=8 * idx: (0, j + o))],
                           [jax.ShapeDtypeStruct((t, D), f32)], [_col_spec(t, 0)], (A_HEADS,), f"a_conv{idx}")[0])
    aq, ak, av = qkv
    tmg = _tile(t, 1040, 8)
    dg_fn = functools.partial(_f_dgates, tm=tmg)
    dg_specs = [_row_spec(tmg, A_HEADS)] * 2 + [_full_spec((1, A_HEADS))] * 2
    beta, gdec = _tw_fwd(dg_fn, [abeta, aalpha, w['a_log_rate'], w['a_dt_bias']], dg_specs,
                         [jax.ShapeDtypeStruct((t, A_HEADS), f32)] * 2, [_row_spec(tmg, A_HEADS)] * 2, (t // tmg,),
                         "a_gates", with_pid=True)
    beta_h = beta.T.reshape(A_HEADS, nc, 1, CHUNK)
    gdec_h = gdec.T.reshape(A_HEADS, nc, 1, CHUNK)
    a_ins = [aq, ak, av, beta_h, gdec_h]
    a_offs = [0] * 5
    o_scan, a_s0 = _scan_fwd(_delta_chunk, a_ins, a_offs, A_HEADS, nc, (A_DK, A_DK), "a_scan")

    mu = w['b_shift_mu']
    mu_rkv, mu_s = mu[:, :3072], mu[:, 3072:]
    zf_rkv = _tw_fwd(_f_tshift, [z, mu_rkv], [_col_spec(t, 32), pl.BlockSpec((1, LANES), lambda j: (0, j))],
                     [jax.ShapeDtypeStruct((t, 3072), f32)], [_col_spec(t, 0)], (24,), "b_shift")[0]
    zs_b = zs[:, :288]
    zf_s = _tw_fwd(_f_tshift, [zs_b, mu_s], [_full_spec((t, 288)), _full_spec((1, 288))],
                   [jax.ShapeDtypeStruct((t, 288), f32)], [_full_spec((t, 288))], (1,), "b_shift_s")[0]
    wdf, adf, gdf = zf_s[:, 0:64], zf_s[:, 64:128], zf_s[:, 128:288]
    tmr = _tile(t, 160, 16)
    pre_params = [w['b_w0'], w['b_w_up'], w['b_a0'], w['b_a_up'], w['b_g_up'], w['b_k_k'], w['b_k_a']]
    pre_ins = [zf_rkv, wdf, adf, gdf] + pre_params
    pre_specs = ([_row_spec(tmr, D, 1), _row_spec(tmr, 64), _row_spec(tmr, 64), _row_spec(tmr, 160)]
                 + [_full_spec(p.shape) for p in pre_params])
    lw, kmod, a_s, b_s, bgate = _tw_fwd(_f_rwkv_pre, pre_ins, pre_specs, [jax.ShapeDtypeStruct((t, D), f32)] * 5,
                                        [_row_spec(tmr, D)] * 5, (t // tmr,), "b_pre")
    b_ins = [zf_rkv, kmod, zf_rkv, a_s, b_s, lw]
    b_offs = [0, 0, 2 * D, 0, 0, 0]
    y_scan, b_s0 = _scan_fwd(_rwkv_chunk, b_ins, b_offs, B_HEADS // 2, nc, (2 * B_N, 2 * B_N), "b_scan")

    out_gain_t = jnp.tile(w['a_out_norm'], (1, A_HEADS))
    r_k = w['b_r_k'].reshape(1, D)
    post_params = [out_gain_t, w['b_ln_gain'], w['b_ln_bias'], r_k]
    post_ins = [o_scan, z, y_scan, zf_rkv, kmod, zf_rkv, bgate, z, z] + post_params
    post_specs = ([_row_spec(tmr, D), _row_spec(tmr, D, 3), _row_spec(tmr, D), _row_spec(tmr, D, 0), _row_spec(tmr, D),
                   _row_spec(tmr, D, 2), _row_spec(tmr, D), _row_spec(tmr, D, 7), _row_spec(tmr, D, 8)]
                  + [_full_spec((1, D))] * 4)
    merged = _tw_fwd(_f_mix_post, post_ins, post_specs, [jax.ShapeDtypeStruct((t, D), MXU_DTYPE)],
                     [_row_spec(tmr, D)], (t // tmr,), "mix_post")[0]
    h2 = _matmul(merged, w['w_out'], res=h1, name="out_proj")
    h3, ffn2_saved = _ffn_fwd(h2, w['ffn2_norm'], w['ffn2_wgu'], w['ffn2_wd'], "ffn2")

    tml = _tile(t, 416, 8)
    fnorm = w['final_norm']
    loss_fn = functools.partial(_f_loss, tm=tml)
    loss_specs = [_row_spec(tml, D), _full_spec((1, D)), _row_spec(tml, D)]
    loss_parts = _tw_fwd(loss_fn, [h3, fnorm, tgt], loss_specs, [jax.ShapeDtypeStruct((t // tml, 1, 1), f32)],
                         [pl.BlockSpec((None, 1, 1), lambda i: (i, 0, 0))], (t // tml,), "loss", with_pid=True)[0]
    loss = jnp.sum(loss_parts)
    ones = jnp.ones((t // tml, 1, 1), f32)
    d_h3, grads['final_norm'] = _tw_bwd(loss_fn, [h3, fnorm, tgt], loss_specs, [ones],
                                        [pl.BlockSpec((None, 1, 1), lambda i: (i, 0, 0))], ['tile', 'acc', None],
                                        (t // tml,), "loss_bwd", with_pid=True)

    d_h2, grads['ffn2_norm'], grads['ffn2_wg'], grads['ffn2_wu'], grads['ffn2_wd'] = _ffn_bwd(
        h2, w['ffn2_norm'], w['ffn2_wgu'], w['ffn2_wd'], ffn2_saved, d_h3, "ffn2")
    grads['w_out'] = _matmul(merged, d_h2, ta=True, name="d_w_out")
    d_merged = _matmul(d_h2, w['w_out'], tb=True, name="d_merged")

    win = ('tile', (t, D), _row_spec(tmr, D))
    post_kinds = ['tile', win, 'tile', win, 'tile', win, 'tile', win, win] + ['acc'] * 4
    (d_o, d_az, d_y, d_r1, d_kmod1, d_v1, d_bgate, d_ga, d_gb,
     d_out_gain_t, grads['b_ln_gain'], grads['b_ln_bias'], d_r_k) = _tw_bwd(
        _f_mix_post, post_ins, post_specs, [d_merged], [_row_spec(tmr, D)], post_kinds, (t // tmr,), "mix_post_bwd")
    grads['a_out_norm'] = jnp.sum(d_out_gain_t.reshape(A_HEADS, A_DK), axis=0, keepdims=True)
    grads['b_r_k'] = d_r_k.reshape(1, B_HEADS, B_N)

    d_r2, d_kmod2, d_v2, d_as, d_bs, d_lw = _scan_bwd(_rwkv_chunk, b_s0, b_ins, b_offs, d_y, B_HEADS // 2, nc,
                                                      (2 * B_N, 2 * B_N), "b_scan_bwd")
    pre_kinds = [win] + ['tile'] * 3 + ['acc'] * 7
    pre_ct_specs = [_row_spec(tmr, D)] * 5
    (d_zf_k, d_wdf, d_adf, d_gdf, grads['b_w0'], grads['b_w_up'], grads['b_a0'], grads['b_a_up'], grads['b_g_up'],
     grads['b_k_k'], grads['b_k_a']) = _tw_bwd(
        _f_rwkv_pre, pre_ins, pre_specs, [d_lw, d_kmod1, d_as, d_bs, d_bgate], pre_ct_specs, pre_kinds, (t // tmr,),
        "b_pre_bwd", ct_extra=[(1, d_kmod2)])
    d_zf_rkv = _assemble3(d_r1, d_r2, d_zf_k, d_v1, d_v2, "b_dzf")
    d_zb_rkv, d_mu_rkv = _tw_bwd(_f_tshift, [z, mu_rkv], [_col_spec(t, 32), pl.BlockSpec((1, LANES), lambda j: (0, j))],
                                 [d_zf_rkv], [_col_spec(t, 0)], [('tile', (t, 3072), _col_spec(t, 0)), 'tile'], (24,),
                                 "b_shift_bwd")
    d_zf_s = jnp.concatenate([d_wdf, d_adf, d_gdf], axis=1)
    d_zs_b, d_mu_s = _tw_bwd(_f_tshift, [zs_b, mu_s], [_full_spec((t, 288)), _full_spec((1, 288))], [d_zf_s],
                             [_full_spec((t, 288))], ['tile', 'tile'], (1,), "b_shift_s_bwd")
    grads['b_shift_mu'] = jnp.concatenate([d_mu_rkv, d_mu_s], axis=1)

    d_aq, d_ak, d_av, d_beta_h, d_g_h = _scan_bwd(_delta_chunk, a_s0, a_ins, a_offs, d_o, A_HEADS, nc, (A_DK, A_DK),
                                                  "a_scan_bwd")
    d_beta = d_beta_h.reshape(A_HEADS, t).T
    d_gdec = d_g_h.reshape(A_HEADS, t).T
    d_abeta, d_aalpha, grads['a_log_rate'], grads['a_dt_bias'] = _tw_bwd(
        dg_fn, [abeta, aalpha, w['a_log_rate'], w['a_dt_bias']], dg_specs, [d_beta, d_gdec],
        [_row_spec(tmg, A_HEADS)] * 2, ['tile', 'tile', 'acc', 'acc'], (t // tmg,), "a_gates_bwd", with_pid=True)
    d_zqkv, d_conv = [], []
    for idx, (fn, ct) in enumerate(zip(conv_fns, (d_aq, d_ak, d_av))):
        dz_i, dw_i = _conv_bwd(fn, z, conv_w, ct, idx, t)
        d_zqkv.append(dz_i)
        d_conv.append(dw_i)
    grads['a_conv_w'] = jnp.concatenate(d_conv, axis=1)

    d_z_parts = d_zqkv + [d_az, d_zb_rkv, d_ga, d_gb, d_zs_b, d_abeta, d_aalpha, jnp.zeros((t, ZP - 9216 - 304), f32)]
    d_z = jnp.concatenate([p.astype(MXU_DTYPE) for p in d_z_parts], axis=1)
    grads['w_in_p'] = _matmul(u, d_z, ta=True, name="d_w_in")
    d_u = _matmul(d_z, w['w_in_p'], tb=True, name="d_u")
    d_h1, grads['mix_norm'] = _rms_bwd(h1, w['mix_norm'], d_u, d_h2, "mix_drms")
    d_h0, grads['ffn1_norm'], grads['ffn1_wg'], grads['ffn1_wu'], grads['ffn1_wd'] = _ffn_bwd(
        h0, w['ffn1_norm'], w['ffn1_wgu'], w['ffn1_wd'], ffn1_saved, d_h1, "ffn1")
    return loss, d_h0, grads


_WIN_SEGMENTS = ((0, 4096), (4112, 7184), (7472, 9520), (7184, 7472), (4096, 4112))


def _win_to_padded(w_in):
    parts = [w_in[:, a:b] for a, b in _WIN_SEGMENTS]
    parts.append(jnp.zeros((w_in.shape[0], ZP - IN_TOTAL), w_in.dtype))
    return jnp.concatenate(parts, axis=1)


def _win_from_padded(w_p):
    widths = [b - a for a, b in _WIN_SEGMENTS]
    offs = [sum(widths[:i]) for i in range(len(widths))]
    seg = {a: w_p[:, o:o + wd] for (a, _), o, wd in zip(_WIN_SEGMENTS, offs, widths)}
    return jnp.concatenate([seg[a] for a in sorted(seg)], axis=1)


def _assemble3(d_r1, d_r2, d_k, d_v1, d_v2, name):
    t = d_r1.shape[0]
    tm = _tile(t, 208, 8)

    def body(r1, r2, kk, v1, v2, o_ref):
        o_ref[:, 0:D] = r1[...] + r2[...]
        o_ref[:, D:2 * D] = kk[...]
        o_ref[:, 2 * D:3 * D] = v1[...] + v2[...]

    return pl.pallas_call(body, name=name, grid=(t // tm,), in_specs=[_row_spec(tm, D)] * 5,
                          out_specs=_row_spec(tm, 3 * D), out_shape=jax.ShapeDtypeStruct((t, 3 * D), f32),
                          )(d_r1, d_r2, d_k, d_v1, d_v2)


def _conv_bwd(fn, z, conv_w, ct, idx, t):
    def body(z_ref, w_ref, ct_ref, dz_ref, dw_ref):
        _, vjp = jax.vjp(lambda a, b: fn(a, b), z_ref[...], w_ref[...])
        dz, dw = vjp((ct_ref[...],))
        dz_ref[...] = dz
        dw_ref[...] = dw

    return pl.pallas_call(
        body, name=f"a_conv{idx}_bwd", grid=(A_HEADS,),
        in_specs=[_col_spec(t, 8 * idx), pl.BlockSpec((4, LANES), lambda j, o=8 * idx: (0, j + o)), _col_spec(t, 0)],
        out_specs=[_col_spec(t, 0), pl.BlockSpec((4, LANES), lambda j: (0, j))],
        out_shape=[jax.ShapeDtypeStruct((t, D), f32), jax.ShapeDtypeStruct((4, D), f32)],
    )(z, conv_w, ct)


def _position():
    return lax.axis_index("x"), lax.axis_index("y"), lax.axis_index("c")


def _flip(v, f):
    return 1 - v if f else v


_CHIP_FLIPS = ((1, 0), (0, 1), (1, 1))
_DEV_FLIPS = tuple((fx, fy, fc) for fx in (0, 1) for fy in (0, 1) for fc in (0, 1) if (fx, fy, fc) != (0, 0, 0))


def _gather_chips(arrs, name):
    n = len(arrs)
    assert all(a.shape[0] % 32 == 0 for a in arrs)
    arrs = [a.reshape(2, a.shape[0] // 2, a.shape[1]) for a in arrs]

    def body(*refs):
        ins, outs = refs[:n], refs[n:2 * n]
        send, recv, fsend, frecv, own = refs[2 * n:]
        x, y, c = _position()
        me = 2 * x + y
        sends, plan, owns = [], [], []
        for a in range(n):
            cp = pltpu.make_async_remote_copy(src_ref=ins[a], dst_ref=outs[a].at[me], send_sem=own.at[a, 0],
                                              recv_sem=own.at[a, 1], device_id=(x, y, 1 - c), device_id_type=MESH)
            cp.start()
            owns.append(cp)
            for j, (fx, fy) in enumerate(_CHIP_FLIPS):
                px, py = _flip(x, fx), _flip(y, fy)
                p = 2 * px + py
                cp = pltpu.make_async_remote_copy(src_ref=ins[a].at[c], dst_ref=outs[a].at[me, c],
                                                  send_sem=send.at[a, j], recv_sem=recv.at[a, j],
                                                  device_id=(px, py, c), device_id_type=MESH)
                cp.start()
                sends.append(cp)
                landed = pltpu.make_async_remote_copy(src_ref=ins[a].at[c], dst_ref=outs[a].at[p, c],
                                                      send_sem=send.at[a, j], recv_sem=recv.at[a, j],
                                                      device_id=(px, py, c), device_id_type=MESH)
                onward = pltpu.make_async_remote_copy(src_ref=outs[a].at[p, c], dst_ref=outs[a].at[p, c],
                                                      send_sem=fsend.at[a, j], recv_sem=frecv.at[a, j],
                                                      device_id=(x, y, 1 - c), device_id_type=MESH)
                from_sibling = pltpu.make_async_remote_copy(src_ref=outs[a].at[p, 1 - c], dst_ref=outs[a].at[p, 1 - c],
                                                            send_sem=fsend.at[a, j], recv_sem=frecv.at[a, j],
                                                            device_id=(x, y, 1 - c), device_id_type=MESH)
                plan.append((landed, onward, from_sibling))
        for landed, onward, _ in plan:
            landed.wait_recv()
            onward.start()
        for _, _, from_sibling in plan:
            from_sibling.wait_recv()
        for cp in sends:
            cp.wait_send()
        for _, onward, _ in plan:
            onward.wait_send()
        for cp in owns:
            cp.wait()

    sems = [pltpu.SemaphoreType.DMA((n, 3))] * 4 + [pltpu.SemaphoreType.DMA((n, 2))]
    outs = pl.pallas_call(
        body, name=name, in_specs=[ANY] * n, out_specs=[ANY] * n,
        out_shape=[jax.ShapeDtypeStruct((N_CHIPS,) + a.shape, a.dtype) for a in arrs], scratch_shapes=sems,
    )(*arrs)
    return [o.reshape(N_CHIPS, o.shape[1] * o.shape[2], o.shape[3]) for o in outs]


def _swap_sibling(arrs, src_of, shapes, name):
    n = len(arrs)

    def body(*refs):
        a_refs, got_refs = refs[:n], refs[n:2 * n]
        send, recv = refs[2 * n:]
        x, y, c = _position()
        copies = []
        for i in range(n):
            cp = pltpu.make_async_remote_copy(src_ref=src_of(a_refs[i], c), dst_ref=got_refs[i], send_sem=send.at[i],
                                              recv_sem=recv.at[i], device_id=(x, y, 1 - c), device_id_type=MESH)
            cp.start()
            copies.append(cp)
        for cp in copies:
            cp.wait()

    return pl.pallas_call(body, name=name, in_specs=[ANY] * n, out_specs=[ANY] * n,
                          out_shape=[jax.ShapeDtypeStruct(sh, a.dtype) for sh, a in zip(shapes, arrs)],
                          scratch_shapes=[pltpu.SemaphoreType.DMA((n,))] * 2)(*arrs)


def _row_tile(rows, width):
    return _tile(rows, max(16, (784 * LANES // width) // 16 * 16), 16)


def _add_halves(g, got, dtype, name):
    n, _, hr, w = g.shape
    tr = _row_tile(hr, w)

    def body(g_ref, got_ref, o_ref):
        c = lax.axis_index("c")
        own = jnp.where(c == 0, g_ref[:, 0], g_ref[:, 1])
        o_ref[...] = (own + got_ref[...]).astype(dtype)

    return pl.pallas_call(
        body, name=name, grid=(hr // tr,),
        in_specs=[pl.BlockSpec((n, 2, tr, w), lambda i: (0, 0, i, 0)), pl.BlockSpec((n, tr, w), lambda i: (0, i, 0))],
        out_specs=pl.BlockSpec((n, tr, w), lambda i: (0, i, 0)),
        out_shape=jax.ShapeDtypeStruct((n, hr, w), dtype))(g, got)


def _scatter_chips(gs, name):
    n = len(gs)

    def body(*refs):
        g_refs, out_refs = refs[:n], refs[n:2 * n]
        send, recv = refs[2 * n:]
        x, y, c = _position()
        sends = []
        for i in range(n):
            for j, (fx, fy) in enumerate(_CHIP_FLIPS):
                px, py = _flip(x, fx), _flip(y, fy)
                cp = pltpu.make_async_remote_copy(src_ref=g_refs[i].at[2 * px + py], dst_ref=out_refs[i].at[j],
                                                  send_sem=send.at[i, j], recv_sem=recv.at[i, j],
                                                  device_id=(px, py, c), device_id_type=MESH)
                cp.start()
                sends.append(cp)
        for cp in sends:
            cp.wait_recv()
        for cp in sends:
            cp.wait_send()

    return pl.pallas_call(
        body, name=name, in_specs=[ANY] * n, out_specs=[ANY] * n,
        out_shape=[jax.ShapeDtypeStruct((3,) + g.shape[1:], g.dtype) for g in gs],
        scratch_shapes=[pltpu.SemaphoreType.DMA((n, 3)), pltpu.SemaphoreType.DMA((n, 3))],
    )(*gs)


def _sum_own_and_slots(own, got, name):
    n, r, w = own.shape
    tr = _row_tile(r, w)

    def body(own_ref, got_ref, o_ref):
        me = 2 * lax.axis_index("x") + lax.axis_index("y")
        acc = own_ref[0]
        for i in range(1, n):
            acc = jnp.where(me == i, own_ref[i], acc)
        acc = acc.astype(f32)
        for j in range(3):
            acc = acc + got_ref[j].astype(f32)
        o_ref[...] = acc

    return pl.pallas_call(
        body, name=name, grid=(r // tr,),
        in_specs=[pl.BlockSpec((n, tr, w), lambda i: (0, i, 0)), pl.BlockSpec((3, tr, w), lambda i: (0, i, 0))],
        out_specs=pl.BlockSpec((tr, w), lambda i: (i, 0)), out_shape=jax.ShapeDtypeStruct((r, w), f32))(own, got)


def _gather_devices(s, name):
    def body(s_ref, out_ref, send, recv, loc):
        x, y, c = _position()
        me = 4 * x + 2 * y + c
        lc = pltpu.make_async_copy(s_ref, out_ref.at[me], loc)
        lc.start()
        sends, recvs = [], []
        for j, (fx, fy, fc) in enumerate(_DEV_FLIPS):
            px, py, pc = _flip(x, fx), _flip(y, fy), _flip(c, fc)
            cp = pltpu.make_async_remote_copy(src_ref=s_ref, dst_ref=out_ref.at[me], send_sem=send.at[j],
                                              recv_sem=recv.at[j], device_id=(px, py, pc), device_id_type=MESH)
            cp.start()
            sends.append(cp)
            recvs.append(pltpu.make_async_remote_copy(
                src_ref=s_ref, dst_ref=out_ref.at[4 * px + 2 * py + pc], send_sem=send.at[j], recv_sem=recv.at[j],
                device_id=(px, py, pc), device_id_type=MESH))
        for cp in recvs:
            cp.wait_recv()
        for cp in sends:
            cp.wait_send()
        lc.wait()

    return pl.pallas_call(
        body, name=name, in_specs=[ANY], out_specs=ANY, out_shape=jax.ShapeDtypeStruct((N_DEV,) + s.shape, s.dtype),
        scratch_shapes=[pltpu.SemaphoreType.DMA((7,)), pltpu.SemaphoreType.DMA((7,)), pltpu.SemaphoreType.DMA(())],
    )(s)


def _sum_slots(a, name):
    s, r, c = a.shape
    tr = _tile(r, 2048, 16)

    def body(a_ref, o_ref):
        acc = a_ref[0].astype(f32)
        for i in range(1, s):
            acc = acc + a_ref[i].astype(f32)
        o_ref[...] = acc

    return pl.pallas_call(body, name=name, grid=(r // tr,), in_specs=[pl.BlockSpec((s, tr, c), lambda i: (0, i, 0))],
                          out_specs=pl.BlockSpec((tr, c), lambda i: (i, 0)),
                          out_shape=jax.ShapeDtypeStruct((r, c), f32))(a)


def _adamw(w, g_parts, m, v, name):
    shape = w.shape
    view = shape if len(shape) >= 2 else (1,) + shape
    assert all(d == 1 for d in view[:-2]), shape
    rows, cols = view[-2:]
    cap = max(8, (256 * 1024 // cols) // 8 * 8)
    tr = rows if rows <= cap else _tile(rows, cap, 8)
    lead = len(view) - 2
    n_g = len(g_parts)

    def body(*refs):
        w_ref = refs[0]
        g_refs = refs[1:1 + n_g]
        m_ref, v_ref, g_out, d_out, m_out, v_out = refs[1 + n_g:]
        g = g_refs[0][...]
        for gr in g_refs[1:]:
            g = g + gr[...]
        m_new = ADAM_B1 * m_ref[...] + (1.0 - ADAM_B1) * g
        v_new = ADAM_B2 * v_ref[...] + (1.0 - ADAM_B2) * (g * g)
        m_hat = m_new / (1.0 - ADAM_B1 ** ADAM_STEP)
        v_hat = v_new / (1.0 - ADAM_B2 ** ADAM_STEP)
        g_out[...] = g
        d_out[...] = -ADAM_LR * (m_hat / (jnp.sqrt(v_hat) + ADAM_EPS) + ADAM_WD * w_ref[...])
        m_out[...] = m_new
        v_out[...] = v_new

    spec = pl.BlockSpec((None,) * lead + (tr, cols), lambda i: (0,) * lead + (i, 0))
    args = [w.reshape(view)] + [g.reshape(view) for g in g_parts] + [m.reshape(view), v.reshape(view)]
    outs = pl.pallas_call(body, name=name, grid=(rows // tr,), in_specs=[spec] * len(args), out_specs=[spec] * 4,
                          out_shape=[jax.ShapeDtypeStruct(view, f32)] * 4)(*args)
    return [o.reshape(shape) for o in outs]


_BIG = ('ffn1_w_gu', 'ffn1_w_down', 'w_in', 'w_out', 'ffn2_w_gu', 'ffn2_w_down')
_SMALL_SHARDED = ('meta_tokens', 'a_conv_w', 'b_w_up', 'b_a_up', 'b_g_up')
_WEIGHTS = ('meta_tokens', 'ffn1_norm', 'ffn1_w_gu', 'ffn1_w_down', 'mix_norm', 'w_in', 'a_conv_w', 'a_log_rate',
            'a_dt_bias', 'a_out_norm', 'b_shift_mu', 'b_w0', 'b_w_up', 'b_a0', 'b_a_up', 'b_g_up', 'b_k_k', 'b_k_a',
            'b_r_k', 'b_ln_gain', 'b_ln_bias', 'w_out', 'ffn2_norm', 'ffn2_w_gu', 'ffn2_w_down', 'final_norm')
_SMALL = tuple(n for n in _WEIGHTS if n not in _BIG)


def _rows_of(shape):
    n = 1
    for d in shape:
        n *= d
    return n, -(-n // LANES)


def _pack(arrs, dtype, row_mult=32):
    parts, total = [], 0
    for a in arrs:
        n, rows = _rows_of(a.shape)
        flat = a.reshape(-1).astype(dtype)
        if n % LANES:
            flat = jnp.pad(flat, (0, rows * LANES - n))
        parts.append(flat)
        total += rows
    extra = -total % row_mult
    if extra:
        parts.append(jnp.zeros((extra * LANES,), dtype))
    return jnp.concatenate(parts).reshape(total + extra, LANES)


def _unpack(packed, shapes, lead=()):
    out, off = [], 0
    for sh in shapes:
        n, rows = _rows_of(sh)
        seg = packed[..., off:off + rows, :]
        if n % LANES:
            seg = seg.reshape(lead + (-1,))[..., :n]
        out.append(seg.reshape(lead + tuple(sh)))
        off += rows
    return out


def _cols_from_shards(s):
    return jnp.concatenate([s[i] for i in range(N_CHIPS)], axis=-1)


def _cols_to_shards(a):
    r, c = a.shape
    return a.reshape(r, N_CHIPS, c // N_CHIPS).transpose(1, 0, 2)


def kernel(x, meta_tokens, ffn1_norm, ffn1_w_gu, ffn1_w_down, mix_norm, w_in, a_conv_w, a_log_rate, a_dt_bias, a_out_norm, b_shift_mu, b_w0, b_w_up, b_a0, b_a_up, b_g_up, b_k_k, b_k_a, b_r_k, b_ln_gain, b_ln_bias, w_out, ffn2_norm, ffn2_w_gu, ffn2_w_down, final_norm, loss_target, m_meta_tokens, m_ffn1_norm, m_ffn1_w_gu, m_ffn1_w_down, m_mix_norm, m_w_in, m_a_conv_w, m_a_log_rate, m_a_dt_bias, m_a_out_norm, m_b_shift_mu, m_b_w0, m_b_w_up, m_b_a0, m_b_a_up, m_b_g_up, m_b_k_k, m_b_k_a, m_b_r_k, m_b_ln_gain, m_b_ln_bias, m_w_out, m_ffn2_norm, m_ffn2_w_gu, m_ffn2_w_down, m_final_norm, v_meta_tokens, v_ffn1_norm, v_ffn1_w_gu, v_ffn1_w_down, v_mix_norm, v_w_in, v_a_conv_w, v_a_log_rate, v_a_dt_bias, v_a_out_norm, v_b_shift_mu, v_b_w0, v_b_w_up, v_b_a0, v_b_a_up, v_b_g_up, v_b_k_k, v_b_k_a, v_b_r_k, v_b_ln_gain, v_b_ln_bias, v_w_out, v_ffn2_norm, v_ffn2_w_gu, v_ffn2_w_down, v_final_norm):
    args = locals()
    wts = {n: args[n] for n in _WEIGHTS}
    mom = {n: args["m_" + n] for n in _WEIGHTS}
    var = {n: args["v_" + n] for n in _WEIGHTS}
    chip = 2 * lax.axis_index("x") + lax.axis_index("y")

    big_shapes = [wts[n].shape[1:] for n in _BIG]
    small_shapes = [wts[n].shape[-2:] for n in _SMALL_SHARDED]
    big_flat = [wts[n].astype(bf16).reshape(wts[n].shape[1:]) for n in _BIG]
    small_packed = _pack([wts[n] for n in _SMALL_SHARDED], f32)
    gathered = _gather_chips(big_flat + [small_packed], "gather_weights")
    gu1, dn1, w_in_s, w_out_s, gu2, dn2 = [a.reshape((N_CHIPS,) + tuple(sh)) for a, sh in zip(gathered, big_shapes)]
    meta_s, conv_s, wup_s, aup_s, gup_s = _unpack(gathered[-1], small_shapes, (N_CHIPS,))
    w = {
        'ffn1_norm': ffn1_norm, 'mix_norm': mix_norm, 'ffn2_norm': ffn2_norm, 'final_norm': final_norm[None, :],
        'ffn1_wgu': gu1, 'ffn1_wd': dn1.reshape(D_FF, D), 'ffn2_wgu': gu2, 'ffn2_wd': dn2.reshape(D_FF, D),
        'w_in_p': _win_to_padded(_cols_from_shards(w_in_s)), 'w_out': w_out_s.reshape(D, D),
        'a_conv_w': _cols_from_shards(conv_s), 'b_w_up': _cols_from_shards(wup_s), 'b_a_up': _cols_from_shards(aup_s),
        'b_g_up': _cols_from_shards(gup_s),
        'a_log_rate': a_log_rate, 'a_dt_bias': a_dt_bias, 'a_out_norm': a_out_norm, 'b_shift_mu': b_shift_mu,
        'b_w0': b_w0, 'b_a0': b_a0, 'b_k_k': b_k_k, 'b_k_a': b_k_a, 'b_r_k': b_r_k, 'b_ln_gain': b_ln_gain,
        'b_ln_bias': b_ln_bias,
    }
    meta_full = _cols_from_shards(meta_s)

    h0 = jnp.concatenate([jnp.zeros((PAD, D), f32), meta_full, x[0]], axis=0)
    tgt = jnp.concatenate([jnp.zeros((SKIP, D), f32), loss_target[0]], axis=0)
    loss_local, d_h0, g = _local_step(h0, tgt, w)
    loss = lax.psum(loss_local, ("x", "y", "c"))
    grad_x = d_h0[SKIP:][None]

    big_grads = [
        jnp.concatenate([g['ffn1_wg'], g['ffn1_wu']], axis=0),
        g['ffn1_wd'].reshape(N_CHIPS, D_FF // N_CHIPS, D),
        _cols_to_shards(_win_from_padded(g['w_in_p'])),
        g['w_out'].reshape(N_CHIPS, D // N_CHIPS, D),
        jnp.concatenate([g['ffn2_wg'], g['ffn2_wu']], axis=0),
        g['ffn2_wd'].reshape(N_CHIPS, D_FF // N_CHIPS, D),
    ]
    g_halves = [a.reshape(N_CHIPS, 2, a.shape[1] // 2, a.shape[2]) for a in big_grads]
    sib_halves = _swap_sibling(g_halves, lambda ref, c: ref.at[:, 1 - c], [a.shape[:1] + a.shape[2:] for a in g_halves],
                               "swap_halves")
    chip_halves = [_add_halves(a, b, bf16, f"add_sibling{i}") for i, (a, b) in enumerate(zip(g_halves, sib_halves))]
    got = _scatter_chips(chip_halves, "scatter_grads")
    mine = [_sum_own_and_slots(a, b, f"sum_chips{i}") for i, (a, b) in enumerate(zip(chip_halves, got))]
    theirs = _swap_sibling(mine, lambda ref, c: ref, [a.shape for a in mine], "swap_sums")
    core = lax.axis_index("c")
    big_parts = [jnp.concatenate([jnp.where(core == 0, a, b), jnp.where(core == 0, b, a)], axis=0)
                 for a, b in zip(mine, theirs)]

    small_full = {
        'meta_tokens': d_h0[PAD:SKIP], 'ffn1_norm': g['ffn1_norm'], 'mix_norm': g['mix_norm'], 'a_conv_w': g['a_conv_w'],
        'a_log_rate': g['a_log_rate'], 'a_dt_bias': g['a_dt_bias'], 'a_out_norm': g['a_out_norm'],
        'b_shift_mu': g['b_shift_mu'], 'b_w0': g['b_w0'], 'b_w_up': g['b_w_up'], 'b_a0': g['b_a0'], 'b_a_up': g['b_a_up'],
        'b_g_up': g['b_g_up'], 'b_k_k': g['b_k_k'], 'b_k_a': g['b_k_a'], 'b_r_k': g['b_r_k'], 'b_ln_gain': g['b_ln_gain'],
        'b_ln_bias': g['b_ln_bias'], 'ffn2_norm': g['ffn2_norm'], 'final_norm': g['final_norm'],
    }
    s_shapes = [small_full[n].shape for n in _SMALL]
    s_sum = _sum_slots(_gather_devices(_pack([small_full[n] for n in _SMALL], f32, row_mult=256), "gather_small"),
                       "sum_small")
    s_parts = dict(zip(_SMALL, _unpack(s_sum, s_shapes)))

    grad, delta, new_m, new_v = {}, {}, {}, {}
    for n, a in zip(_BIG, big_parts):
        grad[n], delta[n], new_m[n], new_v[n] = _adamw(wts[n], [a.reshape(wts[n].shape)], mom[n], var[n], f"adamw_{n}")
    for n in _SMALL:
        gs = s_parts[n]
        if n in _SMALL_SHARDED:
            width = wts[n].shape[-1]
            gs = lax.dynamic_slice_in_dim(gs, chip * width, width, axis=gs.ndim - 1)
        gs = gs.reshape(wts[n].shape)
        grad[n], delta[n], new_m[n], new_v[n] = _adamw(wts[n], [gs], mom[n], var[n], f"adamw_{n}")

    return (loss, grad_x, *[grad[n] for n in _WEIGHTS], *[delta[n] for n in _WEIGHTS],
            *[new_m[n] for n in _WEIGHTS], *[new_v[n] for n in _WEIGHTS])
```

```python
import functools

import jax
import jax.numpy as jnp
from jax import lax
from jax.experimental import pallas as pl
from jax.experimental.pallas import tpu as pltpu

f32 = jnp.float32
bf16 = jnp.bfloat16
HI = lax.Precision.HIGHEST
MESH = pl.DeviceIdType.MESH
ANY = pl.BlockSpec(memory_space=pl.ANY)

D = 1024
N_META = 16
CHUNK = 64
PAD = CHUNK - N_META
SKIP = PAD + N_META
EPS = 1e-6
D_FF = 2816
A_HEADS = 8
A_DK = 128
B_HEADS = 16
B_N = 64
B_GN_EPS = B_N * 1e-5
W_LORA, AA_LORA, G_LORA = 64, 64, 160
IN_TOTAL = 9520
ZP = 9600
LANES = 128
N_CHIPS = 4
N_DEV = 8

ADAM_LR, ADAM_B1, ADAM_B2, ADAM_EPS, ADAM_WD, ADAM_STEP = 0.001, 0.9, 0.999, 1e-08, 0.01, 10

MXU_DTYPE = bf16


def _tile(n, cap, mult):
    if n <= cap:
        return n
    best = None
    for t in range(mult, cap + 1, mult):
        if n % t == 0:
            best = t
    assert best is not None, (n, cap, mult)
    return best


def _sigmoid(x):
    return jax.nn.sigmoid(x)


def _silu(x):
    return x * jax.nn.sigmoid(x)


def _softplus(x):
    return jnp.maximum(x, 0.0) + jnp.log(1.0 + jnp.exp(-jnp.abs(x)))


def _head_matrix(c, nh):
    hd = c // nh
    r = lax.broadcasted_iota(jnp.int32, (c, nh), 0)
    h = lax.broadcasted_iota(jnp.int32, (c, nh), 1)
    return (r >= h * hd) & (r < (h + 1) * hd)


def _dot_exact_rhs(x, e, cb):
    dn = (((1,), (cb,)), ((), ()))
    if SCAN_PASSES == 0:
        return lax.dot_general(x, e.astype(f32), dn, precision=HI, preferred_element_type=f32)
    eb = e.astype(bf16)
    hi = x.astype(bf16)
    lo = (x - hi.astype(f32)).astype(bf16)
    return (lax.dot_general(hi, eb, dn, preferred_element_type=f32)
            + lax.dot_general(lo, eb, dn, preferred_element_type=f32))


def _head_sum_impl(x, nh):
    e = _head_matrix(x.shape[-1], nh)
    return _dot_exact_rhs(_dot_exact_rhs(x, e, 0), e, 1)


@functools.partial(jax.custom_vjp, nondiff_argnums=(1,))
def _head_sum(x, nh):
    return _head_sum_impl(x, nh)


def _head_sum_fwd(x, nh):
    return _head_sum_impl(x, nh), None


def _head_sum_bwd(nh, _, g):
    return (_head_sum_impl(g, nh),)


_head_sum.defvjp(_head_sum_fwd, _head_sum_bwd)


@functools.partial(jax.custom_vjp, nondiff_argnums=(1,))
def _shift_rows(x, s):
    n = x.shape[0]
    row = lax.broadcasted_iota(jnp.int32, x.shape, 0)
    if s > 0:
        return jnp.where(row >= s, pltpu.roll(x, s, 0), 0.0)
    return jnp.where(row < n + s, pltpu.roll(x, n + s, 0), 0.0)


def _shift_rows_fwd(x, s):
    return _shift_rows(x, s), None


def _shift_rows_bwd(s, _, g):
    return (_shift_rows(g, -s),)


_shift_rows.defvjp(_shift_rows_fwd, _shift_rows_bwd)


def _matmul(a, b, *, ta=False, tb=False, res=None, scale=1.0, name, b_cols_split=None, out_cols_split=False):
    assert not (ta and tb)
    (ar, ac) = a.shape
    b0 = 0
    if b_cols_split:
        b0, bs = b_cols_split
        _, br, bc_part = b.shape
        bc = bs * bc_part
    else:
        br, bc = b.shape
    m, k = (ac, ar) if ta else (ar, ac)
    n, kb = (br, bc) if tb else (bc, br)
    assert k == kb, (a.shape, b.shape, ta, tb)
    tm = _tile(m, 1408, LANES) if ta else _tile(m, 832, 8)
    tn = _tile(n, 1408, LANES)
    tk = _tile(k, 1040, 8) if ta else _tile(k, 1408, LANES)
    nk = k // tk
    dn = (((0 if ta else 1,), (1 if tb else 0,)), ((), ()))
    if b_cols_split:
        assert (tk if tb else tn) == bc_part, (b.shape, tn, tk)

    def body(*refs):
        if res is not None:
            a_ref, b_ref, r_ref, o_ref, acc = refs
        else:
            a_ref, b_ref, o_ref, acc = refs
        kk = pl.program_id(2)

        @pl.when(kk == 0)
        def _():
            acc[...] = jnp.zeros_like(acc)

        acc[...] += lax.dot_general(a_ref[...].astype(MXU_DTYPE), b_ref[...].astype(MXU_DTYPE), dn,
                                    preferred_element_type=f32,
                                    precision=None if MXU_DTYPE == bf16 else HI)

        @pl.when(kk == nk - 1)
        def _():
            out = acc[...]
            if scale != 1.0:
                out = out * scale
            if res is not None:
                out = r_ref[...] + out
            o_ref[...] = out

    if ta:
        a_spec = pl.BlockSpec((tk, tm), lambda i, j, kk: (kk, i))
    else:
        a_spec = pl.BlockSpec((tm, tk), lambda i, j, kk: (i, kk))
    if tb and b_cols_split:
        b_spec = pl.BlockSpec((None, tn, tk), lambda i, j, kk: (kk + b0, j, 0))
    elif tb:
        b_spec = pl.BlockSpec((tn, tk), lambda i, j, kk: (j, kk))
    elif b_cols_split:
        b_spec = pl.BlockSpec((None, tk, tn), lambda i, j, kk: (j + b0, kk, 0))
    else:
        b_spec = pl.BlockSpec((tk, tn), lambda i, j, kk: (kk, j))
    in_specs = [a_spec, b_spec]
    args = [a, b]
    if res is not None:
        in_specs.append(pl.BlockSpec((tm, tn), lambda i, j, kk: (i, j)))
        args.append(res)
    if out_cols_split:
        out_spec = pl.BlockSpec((None, tm, tn), lambda i, j, kk: (j, i, 0))
        out_shape = jax.ShapeDtypeStruct((n // tn, m, tn), f32)
    else:
        out_spec = pl.BlockSpec((tm, tn), lambda i, j, kk: (i, j))
        out_shape = jax.ShapeDtypeStruct((m, n), f32)
    return pl.pallas_call(
        body, name=name, grid=(m // tm, n // tn, nk), in_specs=in_specs, out_specs=out_spec, out_shape=out_shape,
        scratch_shapes=[pltpu.VMEM((tm, tn), f32)],
        compiler_params=pltpu.CompilerParams(dimension_semantics=("parallel", "parallel", "arbitrary")),
    )(*args)


def _tw_fwd(fn, ins, in_specs, out_shapes, out_specs, grid, name, with_pid=False):
    n_in = len(ins)

    def body(*refs):
        vals = [r[...] for r in refs[:n_in]]
        outs = fn(pl.program_id(0), *vals) if with_pid else fn(*vals)
        for r, o in zip(refs[n_in:], outs):
            r[...] = o.astype(r.dtype)

    return pl.pallas_call(body, name=name, grid=grid, in_specs=in_specs, out_specs=out_specs,
                          out_shape=out_shapes)(*ins)


def _tw_bwd(fn, ins, in_specs, cts, ct_specs, kinds, grid, name, with_pid=False, tile_dtype=f32, ct_extra=(),
            residual=None):
    n_in, n_ct = len(ins), len(cts)
    diff = [i for i, kd in enumerate(kinds) if kd is not None]
    n_ex = len(ct_extra)

    def body(*refs):
        vals = [r[...] for r in refs[:n_in]]
        ctv = [r[...].astype(f32) for r in refs[n_in:n_in + n_ct]]
        for (ci, _), r in zip(ct_extra, refs[n_in + n_ct:n_in + n_ct + n_ex]):
            ctv[ci] = ctv[ci] + r[...]
        ctv = tuple(ctv)
        n_fixed = n_in + n_ct + n_ex
        res_ref = refs[n_fixed] if residual is not None else None
        g_refs = refs[n_fixed + (residual is not None):]
        pid = pl.program_id(0)

        def f(*dv):
            full = list(vals)
            for i, v in zip(diff, dv):
                full[i] = v
            out = fn(pid, *full) if with_pid else fn(*full)
            return tuple(out)

        _, vjp = jax.vjp(f, *[vals[i] for i in diff])
        gs = vjp(ctv)
        first = pid == 0
        for i2 in range(1, len(grid)):
            first = first & (pl.program_id(i2) == 0)
        for i, g, g_ref in zip(diff, gs, g_refs):
            if kinds[i] != 'acc':
                if i == 0 and res_ref is not None:
                    g = res_ref[...] + g
                g_ref[...] = g.astype(g_ref.dtype)
            else:
                @pl.when(first)
                def _(g=g, g_ref=g_ref):
                    g_ref[...] = g

                @pl.when(jnp.logical_not(first))
                def _(g=g, g_ref=g_ref):
                    g_ref[...] += g

    zero_map = {1: lambda *a: (0,), 2: lambda *a: (0, 0), 3: lambda *a: (0, 0, 0)}
    out_specs, out_shapes = [], []
    for i in diff:
        if kinds[i] == 'tile':
            out_shapes.append(jax.ShapeDtypeStruct(ins[i].shape, tile_dtype))
            out_specs.append(in_specs[i])
        elif kinds[i] == 'acc':
            out_shapes.append(jax.ShapeDtypeStruct(ins[i].shape, f32))
            out_specs.append(pl.BlockSpec(ins[i].shape, zero_map[ins[i].ndim]))
        else:
            out_shapes.append(jax.ShapeDtypeStruct(kinds[i][1], tile_dtype))
            out_specs.append(kinds[i][2])
    extra_specs = [ct_specs[ci] for ci, _ in ct_extra]
    extra = [a for _, a in ct_extra]
    if residual is not None:
        assert kinds[0] == 'tile'
        extra_specs.append(in_specs[0])
        extra.append(residual)
    return pl.pallas_call(body, name=name, grid=grid, in_specs=list(in_specs) + list(ct_specs) + extra_specs,
                          out_specs=out_specs, out_shape=out_shapes)(*ins, *cts, *extra)


def _row_spec(tm, c, col_block=0):
    return pl.BlockSpec((tm, c), lambda i, cb=col_block: (i, cb))


def _full_spec(shape):
    nd = len(shape)
    return pl.BlockSpec(shape, lambda *a, nd=nd: (0,) * nd)


def _f_rms(x, g):
    return (x * lax.rsqrt(jnp.mean(x * x, axis=-1, keepdims=True) + EPS) * g,)


def _f_swiglu(gate, up):
    return (_silu(gate) * up,)


def _f_loss(pid, h, g, tgt, *, tm):
    y = h * lax.rsqrt(jnp.mean(h * h, axis=-1, keepdims=True) + EPS) * g
    row = pid * tm + lax.broadcasted_iota(jnp.int32, (tm, 1), 0)
    err = jnp.where(row >= SKIP, y - tgt, 0.0)
    per_row = jnp.mean(err * err, axis=-1, keepdims=True)
    return (0.5 * jnp.sum(per_row, axis=0, keepdims=True),)


def _f_conv(x, w, *, norm, scale):
    y = x * w[3:4, :]
    for s in (1, 2, 3):
        y = y + _shift_rows(x, s) * w[3 - s:4 - s, :]
    y = _silu(y)
    if norm:
        y = y * lax.rsqrt(jnp.sum(y * y, axis=-1, keepdims=True) + 1e-6) * scale
    return (y,)


def _f_dgates(pid, abeta, aalpha, log_rate, dt_bias, *, tm):
    row = pid * tm + lax.broadcasted_iota(jnp.int32, (tm, 1), 0)
    live = row >= PAD
    beta = jnp.where(live, _sigmoid(abeta), 0.0)
    g = jnp.where(live, -jnp.exp(log_rate) * _softplus(aalpha + dt_bias), 0.0)
    return beta, g


def _f_tshift(z, mu):
    return (z + (_shift_rows(z, 1) - z) * mu,)


def _f_rwkv_pre(k, wd, ad, gd, w0, w_up, a0, a_up, g_up, k_k, k_a):
    w_log = -_softplus(-(w0 + _smm(jnp.tanh(wd), w_up))) - 0.5
    lw = -jnp.exp(w_log)
    a_lr = _sigmoid(a0 + _smm(ad, a_up))
    gate = _smm(_sigmoid(gd), g_up)
    kkp = k * k_k
    kk = kkp * lax.rsqrt(_head_sum(kkp * kkp, B_HEADS) + 1e-6)
    kmod = k * (1.0 + (a_lr - 1.0) * k_a)
    return lw, kmod, -kk, kk * a_lr, gate


def _f_mix_post(o, az, y, r, kmod, v, gate, ga, gb, out_gain, ln_g, ln_b, r_k):
    ms = _head_sum(o * o, A_HEADS) * (1.0 / A_DK)
    oa = o * lax.rsqrt(ms + EPS) * out_gain * _silu(az)
    mean = _head_sum(y, B_HEADS) * (1.0 / B_N)
    yc = y - mean
    var = _head_sum(yc * yc, B_HEADS) * (1.0 / B_N)
    yn = yc * lax.rsqrt(var + B_GN_EPS) * ln_g + ln_b
    bonus = _head_sum(r * kmod * r_k, B_HEADS) * v
    ob = (yn + bonus) * gate
    return (_sigmoid(ga) * oa + _sigmoid(gb) * ob,)


SCAN_PASSES = 3


def _split2(a):
    hi = a.astype(bf16)
    return hi, (a - hi.astype(f32)).astype(bf16)


def _dot_passes(a, b, ca, cb, passes):
    dn = (((ca,), (cb,)), ((), ()))
    if SCAN_PASSES == 0:
        return lax.dot_general(a, b, dn, precision=HI, preferred_element_type=f32)
    if passes == 1:
        return lax.dot_general(a.astype(bf16), b.astype(bf16), dn, preferred_element_type=f32)
    ah, al = _split2(a)
    bh, bl = _split2(b)
    return (lax.dot_general(ah, bh, dn, preferred_element_type=f32)
            + (lax.dot_general(ah, bl, dn, preferred_element_type=f32)
               + lax.dot_general(al, bh, dn, preferred_element_type=f32)))


@functools.partial(jax.custom_vjp, nondiff_argnums=(2, 3, 4))
def _sdot(a, b, ca, cb, passes):
    return _dot_passes(a, b, ca, cb, passes)


def _sdot_fwd(a, b, ca, cb, passes):
    return _dot_passes(a, b, ca, cb, passes), (a, b)


def _sdot_bwd(ca, cb, passes, res, g):
    a, b = res
    if (ca, cb) == (1, 0):
        return _dot_passes(g, b, 1, 1, passes), _dot_passes(a, g, 0, 0, passes)
    if (ca, cb) == (1, 1):
        return _dot_passes(g, b, 1, 0, passes), _dot_passes(g, a, 0, 0, passes)
    assert (ca, cb) == (0, 0)
    return _dot_passes(b, g, 1, 1, passes), _dot_passes(a, g, 1, 0, passes)


_sdot.defvjp(_sdot_fwd, _sdot_bwd)


def _smm(a, b, passes=3):
    return _sdot(a, b, 1, 0, passes)


def _smm_nt(a, b, passes=3):
    return _sdot(a, b, 1, 1, passes)


def _smm_tn(a, b, passes=3):
    return _sdot(a, b, 0, 0, passes)


def _tri_dot(x, ca):
    n = x.shape[0]
    incl = _tri_masks(n)[0]
    dn = (((ca,), (0,)), ((), ()))
    if SCAN_PASSES == 0:
        return lax.dot_general(incl.astype(f32), x, dn, precision=HI, preferred_element_type=f32)
    tri = incl.astype(bf16)
    hi, r1 = x.astype(bf16), None
    r1 = x - hi.astype(f32)
    mid = r1.astype(bf16)
    lo = (r1 - mid.astype(f32)).astype(bf16)
    return (lax.dot_general(tri, hi, dn, preferred_element_type=f32)
            + (lax.dot_general(tri, mid, dn, preferred_element_type=f32)
               + lax.dot_general(tri, lo, dn, preferred_element_type=f32)))


@jax.custom_vjp
def _cumsum_rows(x):
    return _tri_dot(x, 1)


def _cumsum_rows_fwd(x):
    return _tri_dot(x, 1), None


def _cumsum_rows_bwd(_, g):
    return (_tri_dot(g, 0),)


_cumsum_rows.defvjp(_cumsum_rows_fwd, _cumsum_rows_bwd)


def _tri_masks(n):
    i = lax.broadcasted_iota(jnp.int32, (n, n), 0)
    j = lax.broadcasted_iota(jnp.int32, (n, n), 1)
    return i >= j, i > j, i == j, i <= j


def _unit_lower_inv_impl(low, passes):
    n = low.shape[0]
    assert n == CHUNK
    _, _, eye, _ = _tri_masks(n)
    acc = eye.astype(f32) + low
    p = low
    for _ in range(5):
        p = _dot_passes(p, p, 1, 0, passes)
        acc = acc + _dot_passes(acc, p, 1, 0, passes)
    return acc


@functools.partial(jax.custom_vjp, nondiff_argnums=(1,))
def _unit_lower_inv(low, passes=3):
    return _unit_lower_inv_impl(low, passes)


def _unit_lower_inv_fwd(low, passes):
    t = _unit_lower_inv_impl(low, passes)
    return t, t


def _unit_lower_inv_bwd(passes, t, g):
    return (_dot_passes(_dot_passes(t, g, 0, 0, passes), t, 1, 1, passes),)


_unit_lower_inv.defvjp(_unit_lower_inv_fwd, _unit_lower_inv_bwd)

DELTA_PASSES = 1
DELTA_INV_PASSES = 1


def _delta_chunk(s, q, k, v, beta_row, g_row):
    p = DELTA_PASSES
    incl, strict, eye, upper = _tri_masks(CHUNK)
    beta = jnp.sum(jnp.where(eye, beta_row, 0.0), axis=1, keepdims=True)
    g = jnp.sum(jnp.where(eye, g_row, 0.0), axis=1, keepdims=True)
    gc = jnp.sum(jnp.where(incl, g_row, 0.0), axis=1, keepdims=True)
    gc_row = jnp.sum(jnp.where(upper, g, 0.0), axis=0, keepdims=True)
    decay = jnp.where(incl, jnp.exp(jnp.where(incl, gc - gc_row, 0.0)), 0.0)
    kb = k * beta
    vb = v * beta
    m = jnp.where(strict, _smm_nt(kb, k, p) * decay, 0.0)
    tinv = _unit_lower_inv(-m, DELTA_INV_PASSES)
    u = _smm(tinv, vb, p)
    wk = _smm(tinv, kb * jnp.exp(gc), p)
    attn = _smm_nt(q, k, p) * decay
    qg = q * jnp.exp(gc)
    g_last = jnp.sum(g, axis=0, keepdims=True)
    k_tail = k * jnp.exp(g_last - gc)
    v_new = u - _smm(wk, s, p)
    o = _smm(qg, s, p) + _smm(attn, v_new, p)
    s_new = s * jnp.exp(g_last) + _smm_tn(k_tail, v_new, p)
    return o, s_new


RWKV_PASSES = 1
RWKV_INV_PASSES = 1


def _rwkv_chunk(st, r, k, v, a, b, lw):
    c = CHUNK
    p, pi = RWKV_PASSES, RWKV_INV_PASSES
    _, strict, _, _ = _tri_masks(c)
    lane = lax.broadcasted_iota(jnp.int32, (c, 2 * B_N), 1)
    row = lax.broadcasted_iota(jnp.int32, (c, 2 * B_N), 0)
    first = lane < B_N
    incl2 = row >= jnp.where(first, lane, lane - B_N)
    bi = lax.broadcasted_iota(jnp.int32, (2 * B_N, 2 * B_N), 0) < B_N
    bj = lax.broadcasted_iota(jnp.int32, (2 * B_N, 2 * B_N), 1) < B_N
    blockdiag = bi == bj
    cum = _cumsum_rows(lw)
    e_pos = jnp.exp(cum)
    e_neg = jnp.exp(-cum)
    rt = r * e_pos
    at = a * jnp.exp(cum - lw)
    kt = k * e_neg
    bt = b * e_neg
    bk = jnp.concatenate([bt, kt], axis=0)
    a_s0 = _smm_nt(at, st, p)
    r_s0 = _smm_nt(rt, st, p)
    heads = (first, jnp.logical_not(first))
    u = jnp.zeros((c, 2 * B_N), f32)
    for sel in heads:
        at_h = jnp.where(sel, at, 0.0)
        ab = jnp.where(strict, _smm_nt(at_h, bt, pi), 0.0)
        ak = jnp.where(strict, _smm_nt(at_h, kt, p), 0.0)
        t_h = _unit_lower_inv(ab, pi)
        u = u + _smm(t_h, jnp.where(sel, a_s0, 0.0) + _smm(ak, jnp.where(sel, v, 0.0), p), p)
    y = r_s0
    for sel in heads:
        rbk = jnp.where(incl2, _smm_nt(jnp.where(sel, rt, 0.0), bk, p), 0.0)
        uv = jnp.concatenate([jnp.where(sel, u, 0.0), jnp.where(sel, v, 0.0)], axis=0)
        y = y + _smm(rbk, uv, p)
    cl = jnp.sum(lw, axis=0, keepdims=True)
    dec = jnp.exp(cl - cum)
    uv_all = jnp.concatenate([u, v], axis=0)
    bk_dec = jnp.concatenate([b * dec, k * dec], axis=0)
    st_new = st * jnp.exp(cl) + jnp.where(blockdiag, _smm_tn(uv_all, bk_dec, p), 0.0)
    return y, st_new


GROUPS_PER_STEP = 8


def _scan_specs(ins, col_offs, n_chunks, reverse):
    gw = GROUPS_PER_STEP * LANES
    cidx = (lambda c: n_chunks - 1 - c) if reverse else (lambda c: c)
    specs = []
    for a, off in zip(ins, col_offs):
        if a.ndim == 2:
            assert off % gw == 0
            specs.append(pl.BlockSpec((CHUNK, gw), lambda h, c, o=off // gw: (cidx(c), h + o)))
        else:
            specs.append(pl.BlockSpec((GROUPS_PER_STEP, None, 1, CHUNK), lambda h, c: (h, cidx(c), 0, 0)))
    return specs, cidx


def _group_vals(refs, g):
    return [r[:, g * LANES:(g + 1) * LANES] if len(r.shape) == 2 else r[g] for r in refs]


def _scan_fwd(chunk_fn, ins, col_offs, n_groups, n_chunks, state_shape, name):
    n_in = len(ins)
    gps = GROUPS_PER_STEP
    t = ins[0].shape[0]

    def body(*refs):
        in_refs = refs[:n_in]
        o_ref, s0_ref, st = refs[n_in:]

        @pl.when(pl.program_id(1) == 0)
        def _():
            st[...] = jnp.zeros_like(st)

        states = st[...]
        vals = [jnp.stack(col) for col in zip(*[_group_vals(in_refs, g) for g in range(gps)])]
        o, s_new = jax.vmap(chunk_fn)(states, *vals)
        s0_ref[...] = states
        st[...] = s_new
        for g in range(gps):
            o_ref[:, g * LANES:(g + 1) * LANES] = o[g]

    specs, _ = _scan_specs(ins, col_offs, n_chunks, False)
    return pl.pallas_call(
        body, name=name, grid=(n_groups // gps, n_chunks), in_specs=specs,
        out_specs=[pl.BlockSpec((CHUNK, gps * LANES), lambda h, c: (c, h)),
                   pl.BlockSpec((gps, None) + state_shape, lambda h, c: (h, c, 0, 0))],
        out_shape=[jax.ShapeDtypeStruct((t, n_groups * LANES), f32),
                   jax.ShapeDtypeStruct((n_groups, n_chunks) + state_shape, f32)],
        scratch_shapes=[pltpu.VMEM((gps,) + state_shape, f32)],
        compiler_params=pltpu.CompilerParams(dimension_semantics=("parallel", "arbitrary")),
    )(*ins)


def _scan_bwd(chunk_fn, s0s, ins, col_offs, d_out, n_groups, n_chunks, state_shape, name):
    n_in = len(ins)
    gps = GROUPS_PER_STEP
    t = d_out.shape[0]

    def body(*refs):
        s0_ref = refs[0]
        in_refs = refs[1:1 + n_in]
        do_ref = refs[1 + n_in]
        g_refs = refs[2 + n_in:2 + 2 * n_in]
        dst = refs[2 + 2 * n_in]

        @pl.when(pl.program_id(1) == 0)
        def _():
            dst[...] = jnp.zeros_like(dst)

        vals = [jnp.stack(col) for col in zip(*[_group_vals(in_refs, g) for g in range(gps)])]
        d_o = jnp.stack([do_ref[:, g * LANES:(g + 1) * LANES] for g in range(gps)])
        _, vjp = jax.vjp(jax.vmap(chunk_fn), s0_ref[...], *vals)
        gs = vjp((d_o, dst[...]))
        dst[...] = gs[0]
        for g_ref, gv in zip(g_refs, gs[1:]):
            if len(g_ref.shape) == 2:
                for g in range(gps):
                    g_ref[:, g * LANES:(g + 1) * LANES] = gv[g]
            else:
                g_ref[...] = gv

    specs, cidx = _scan_specs(ins, col_offs, n_chunks, True)
    out_lane = pl.BlockSpec((CHUNK, gps * LANES), lambda h, c: (cidx(c), h))
    g_specs = [out_lane if a.ndim == 2 else sp for a, sp in zip(ins, specs)]
    g_shapes = [(t, n_groups * LANES) if a.ndim == 2 else a.shape for a in ins]
    s0_spec = pl.BlockSpec((gps, None) + state_shape, lambda h, c: (h, cidx(c), 0, 0))
    return pl.pallas_call(
        body, name=name, grid=(n_groups // gps, n_chunks), in_specs=[s0_spec] + specs + [out_lane],
        out_specs=g_specs, out_shape=[jax.ShapeDtypeStruct(sh, f32) for sh in g_shapes],
        scratch_shapes=[pltpu.VMEM((gps,) + state_shape, f32)],
        compiler_params=pltpu.CompilerParams(dimension_semantics=("parallel", "arbitrary")),
    )(s0s, *ins, d_out)


def _rms_fwd(x, g, name):
    t = x.shape[0]
    tm = _tile(t, 416, 16)
    return _tw_fwd(_f_rms, [x, g], [_row_spec(tm, D), _full_spec(g.shape)],
                   [jax.ShapeDtypeStruct(x.shape, MXU_DTYPE)], [_row_spec(tm, D)], (t // tm,), name)[0]


def _rms_bwd(x, g, dy, residual, name):
    t = x.shape[0]
    tm = _tile(t, 416, 8)
    return _tw_bwd(_f_rms, [x, g], [_row_spec(tm, D), _full_spec(g.shape)], [dy], [_row_spec(tm, D)],
                   ['tile', 'acc'], (t // tm,), name, residual=residual)


def _ffn_fwd(h, gain, wgu, wd, tag):
    xn = _rms_fwd(h, gain, f"{tag}_rms")
    gate = _matmul(xn, wgu, b_cols_split=(0, 2), name=f"{tag}_gate")
    up = _matmul(xn, wgu, b_cols_split=(2, 2), name=f"{tag}_up")
    t = h.shape[0]
    tm = _tile(t, 208, 16)
    act = _tw_fwd(_f_swiglu, [gate, up], [_row_spec(tm, D_FF)] * 2, [jax.ShapeDtypeStruct((t, D_FF), MXU_DTYPE)],
                  [_row_spec(tm, D_FF)], (t // tm,), f"{tag}_act")[0]
    out = _matmul(act, wd, res=h, scale=0.5, name=f"{tag}_down")
    return out, (xn, gate, up, act)


def _ffn_bwd(h, gain, wgu, wd, saved, dout, tag):
    xn, gate, up, act = saved
    t = h.shape[0]
    d_wd = _matmul(act, dout, ta=True, scale=0.5, name=f"{tag}_dwd")
    d_act = _matmul(dout, wd, tb=True, scale=0.5, name=f"{tag}_dact")
    tm = _tile(t, 208, 16)
    d_gate, d_up = _tw_bwd(_f_swiglu, [gate, up], [_row_spec(tm, D_FF)] * 2, [d_act], [_row_spec(tm, D_FF)],
                           ['tile', 'tile'], (t // tm,), f"{tag}_dactf", tile_dtype=MXU_DTYPE)
    d_wg = _matmul(xn, d_gate, ta=True, out_cols_split=True, name=f"{tag}_dwg")
    d_wu = _matmul(xn, d_up, ta=True, out_cols_split=True, name=f"{tag}_dwu")
    d_xn = _matmul(d_gate, wgu, tb=True, b_cols_split=(0, 2), name=f"{tag}_dxn_g")
    d_xn = _matmul(d_up, wgu, tb=True, b_cols_split=(2, 2), res=d_xn, name=f"{tag}_dxn_u")
    d_h, d_gain = _rms_bwd(h, gain, d_xn, dout, f"{tag}_drms")
    return d_h, d_gain, d_wg, d_wu, d_wd


def _col_spec(t, first_block):
    return pl.BlockSpec((t, LANES), lambda j, fb=first_block: (0, j + fb))


def _local_step(h0, tgt, w):
    t = h0.shape[0]
    assert t % CHUNK == 0
    nc = t // CHUNK
    grads = {}

    h1, ffn1_saved = _ffn_fwd(h0, w['ffn1_norm'], w['ffn1_wgu'], w['ffn1_wd'], "ffn1")
    u = _rms_fwd(h1, w['mix_norm'], "mix_rms")
    z = _matmul(u, w['w_in_p'], name="in_proj")
    zs = z[:, 9216:9216 + 304]
    abeta, aalpha = zs[:, 288:296], zs[:, 296:304]

    conv_w = w['a_conv_w']
    conv_fns = [functools.partial(_f_conv, norm=True, scale=A_DK ** -0.5),
                functools.partial(_f_conv, norm=True, scale=1.0),
                functools.partial(_f_conv, norm=False, scale=1.0)]
    qkv = []
    for idx, fn in enumerate(conv_fns):
        qkv.append(_tw_fwd(fn, [z, conv_w], [_col_spec(t, 8 * idx), pl.BlockSpec((4, LANES), lambda j, o=8 * idx: (0, j + o))],
                           [jax.ShapeDtypeStruct((t, D), f32)], [_col_spec(t, 0)], (A_HEADS,), f"a_conv{idx}")[0])
    aq, ak, av = qkv
    tmg = _tile(t, 1040, 8)
    dg_fn = functools.partial(_f_dgates, tm=tmg)
    dg_specs = [_row_spec(tmg, A_HEADS)] * 2 + [_full_spec((1, A_HEADS))] * 2
    beta, gdec = _tw_fwd(dg_fn, [abeta, aalpha, w['a_log_rate'], w['a_dt_bias']], dg_specs,
                         [jax.ShapeDtypeStruct((t, A_HEADS), f32)] * 2, [_row_spec(tmg, A_HEADS)] * 2, (t // tmg,),
                         "a_gates", with_pid=True)
    beta_h = beta.T.reshape(A_HEADS, nc, 1, CHUNK)
    gdec_h = gdec.T.reshape(A_HEADS, nc, 1, CHUNK)
    a_ins = [aq, ak, av, beta_h, gdec_h]
    a_offs = [0] * 5
    o_scan, a_s0 = _scan_fwd(_delta_chunk, a_ins, a_offs, A_HEADS, nc, (A_DK, A_DK), "a_scan")

    mu = w['b_shift_mu']
    mu_rkv, mu_s = mu[:, :3072], mu[:, 3072:]
    zf_rkv = _tw_fwd(_f_tshift, [z, mu_rkv], [_col_spec(t, 32), pl.BlockSpec((1, LANES), lambda j: (0, j))],
                     [jax.ShapeDtypeStruct((t, 3072), f32)], [_col_spec(t, 0)], (24,), "b_shift")[0]
    zs_b = zs[:, :288]
    zf_s = _tw_fwd(_f_tshift, [zs_b, mu_s], [_full_spec((t, 288)), _full_spec((1, 288))],
                   [jax.ShapeDtypeStruct((t, 288), f32)], [_full_spec((t, 288))], (1,), "b_shift_s")[0]
    wdf, adf, gdf = zf_s[:, 0:64], zf_s[:, 64:128], zf_s[:, 128:288]
    tmr = _tile(t, 160, 16)
    pre_params = [w['b_w0'], w['b_w_up'], w['b_a0'], w['b_a_up'], w['b_g_up'], w['b_k_k'], w['b_k_a']]
    pre_ins = [zf_rkv, wdf, adf, gdf] + pre_params
    pre_specs = ([_row_spec(tmr, D, 1), _row_spec(tmr, 64), _row_spec(tmr, 64), _row_spec(tmr, 160)]
                 + [_full_spec(p.shape) for p in pre_params])
    lw, kmod, a_s, b_s, bgate = _tw_fwd(_f_rwkv_pre, pre_ins, pre_specs, [jax.ShapeDtypeStruct((t, D), f32)] * 5,
                                        [_row_spec(tmr, D)] * 5, (t // tmr,), "b_pre")
    b_ins = [zf_rkv, kmod, zf_rkv, a_s, b_s, lw]
    b_offs = [0, 0, 2 * D, 0, 0, 0]
    y_scan, b_s0 = _scan_fwd(_rwkv_chunk, b_ins, b_offs, B_HEADS // 2, nc, (2 * B_N, 2 * B_N), "b_scan")

    out_gain_t = jnp.tile(w['a_out_norm'], (1, A_HEADS))
    r_k = w['b_r_k'].reshape(1, D)
    post_params = [out_gain_t, w['b_ln_gain'], w['b_ln_bias'], r_k]
    post_ins = [o_scan, z, y_scan, zf_rkv, kmod, zf_rkv, bgate, z, z] + post_params
    post_specs = ([_row_spec(tmr, D), _row_spec(tmr, D, 3), _row_spec(tmr, D), _row_spec(tmr, D, 0), _row_spec(tmr, D),
                   _row_spec(tmr, D, 2), _row_spec(tmr, D), _row_spec(tmr, D, 7), _row_spec(tmr, D, 8)]
                  + [_full_spec((1, D))] * 4)
    merged = _tw_fwd(_f_mix_post, post_ins, post_specs, [jax.ShapeDtypeStruct((t, D), MXU_DTYPE)],
                     [_row_spec(tmr, D)], (t // tmr,), "mix_post")[0]
    h2 = _matmul(merged, w['w_out'], res=h1, name="out_proj")
    h3, ffn2_saved = _ffn_fwd(h2, w['ffn2_norm'], w['ffn2_wgu'], w['ffn2_wd'], "ffn2")

    tml = _tile(t, 416, 8)
    fnorm = w['final_norm']
    loss_fn = functools.partial(_f_loss, tm=tml)
    loss_specs = [_row_spec(tml, D), _full_spec((1, D)), _row_spec(tml, D)]
    loss_parts = _tw_fwd(loss_fn, [h3, fnorm, tgt], loss_specs, [jax.ShapeDtypeStruct((t // tml, 1, 1), f32)],
                         [pl.BlockSpec((None, 1, 1), lambda i: (i, 0, 0))], (t // tml,), "loss", with_pid=True)[0]
    loss = jnp.sum(loss_parts)
    ones = jnp.ones((t // tml, 1, 1), f32)
    d_h3, grads['final_norm'] = _tw_bwd(loss_fn, [h3, fnorm, tgt], loss_specs, [ones],
                                        [pl.BlockSpec((None, 1, 1), lambda i: (i, 0, 0))], ['tile', 'acc', None],
                                        (t // tml,), "loss_bwd", with_pid=True)

    d_h2, grads['ffn2_norm'], grads['ffn2_wg'], grads['ffn2_wu'], grads['ffn2_wd'] = _ffn_bwd(
        h2, w['ffn2_norm'], w['ffn2_wgu'], w['ffn2_wd'], ffn2_saved, d_h3, "ffn2")
    grads['w_out'] = _matmul(merged, d_h2, ta=True, name="d_w_out")
    d_merged = _matmul(d_h2, w['w_out'], tb=True, name="d_merged")

    win = ('tile', (t, D), _row_spec(tmr, D))
    post_kinds = ['tile', win, 'tile', win, 'tile', win, 'tile', win, win] + ['acc'] * 4
    (d_o, d_az, d_y, d_r1, d_kmod1, d_v1, d_bgate, d_ga, d_gb,
     d_out_gain_t, grads['b_ln_gain'], grads['b_ln_bias'], d_r_k) = _tw_bwd(
        _f_mix_post, post_ins, post_specs, [d_merged], [_row_spec(tmr, D)], post_kinds, (t // tmr,), "mix_post_bwd")
    grads['a_out_norm'] = jnp.sum(d_out_gain_t.reshape(A_HEADS, A_DK), axis=0, keepdims=True)
    grads['b_r_k'] = d_r_k.reshape(1, B_HEADS, B_N)

    d_r2, d_kmod2, d_v2, d_as, d_bs, d_lw = _scan_bwd(_rwkv_chunk, b_s0, b_ins, b_offs, d_y, B_HEADS // 2, nc,
                                                      (2 * B_N, 2 * B_N), "b_scan_bwd")
    pre_kinds = [win] + ['tile'] * 3 + ['acc'] * 7
    pre_ct_specs = [_row_spec(tmr, D)] * 5
    (d_zf_k, d_wdf, d_adf, d_gdf, grads['b_w0'], grads['b_w_up'], grads['b_a0'], grads['b_a_up'], grads['b_g_up'],
     grads['b_k_k'], grads['b_k_a']) = _tw_bwd(
        _f_rwkv_pre, pre_ins, pre_specs, [d_lw, d_kmod1, d_as, d_bs, d_bgate], pre_ct_specs, pre_kinds, (t // tmr,),
        "b_pre_bwd", ct_extra=[(1, d_kmod2)])
    d_zb_rkv, d_mu_rkv = _shift_bwd3(z, mu_rkv, d_r1, d_r2, d_zf_k, d_v1, d_v2, t)
    d_zf_s = jnp.concatenate([d_wdf, d_adf, d_gdf], axis=1)
    d_zs_b, d_mu_s = _tw_bwd(_f_tshift, [zs_b, mu_s], [_full_spec((t, 288)), _full_spec((1, 288))], [d_zf_s],
                             [_full_spec((t, 288))], ['tile', 'tile'], (1,), "b_shift_s_bwd")
    grads['b_shift_mu'] = jnp.concatenate([d_mu_rkv, d_mu_s], axis=1)

    d_aq, d_ak, d_av, d_beta_h, d_g_h = _scan_bwd(_delta_chunk, a_s0, a_ins, a_offs, d_o, A_HEADS, nc, (A_DK, A_DK),
                                                  "a_scan_bwd")
    d_beta = d_beta_h.reshape(A_HEADS, t).T
    d_gdec = d_g_h.reshape(A_HEADS, t).T
    d_abeta, d_aalpha, grads['a_log_rate'], grads['a_dt_bias'] = _tw_bwd(
        dg_fn, [abeta, aalpha, w['a_log_rate'], w['a_dt_bias']], dg_specs, [d_beta, d_gdec],
        [_row_spec(tmg, A_HEADS)] * 2, ['tile', 'tile', 'acc', 'acc'], (t // tmg,), "a_gates_bwd", with_pid=True)
    d_zqkv, d_conv = [], []
    for idx, (fn, ct) in enumerate(zip(conv_fns, (d_aq, d_ak, d_av))):
        dz_i, dw_i = _conv_bwd(fn, z, conv_w, ct, idx, t)
        d_zqkv.append(dz_i)
        d_conv.append(dw_i)
    grads['a_conv_w'] = jnp.concatenate(d_conv, axis=1)

    d_z_parts = d_zqkv + [d_az, d_zb_rkv, d_ga, d_gb, d_zs_b, d_abeta, d_aalpha, jnp.zeros((t, ZP - 9216 - 304), f32)]
    d_z = jnp.concatenate([p.astype(MXU_DTYPE) for p in d_z_parts], axis=1)
    grads['w_in_p'] = _matmul(u, d_z, ta=True, name="d_w_in")
    d_u = _matmul(d_z, w['w_in_p'], tb=True, name="d_u")
    d_h1, grads['mix_norm'] = _rms_bwd(h1, w['mix_norm'], d_u, d_h2, "mix_drms")
    d_h0, grads['ffn1_norm'], grads['ffn1_wg'], grads['ffn1_wu'], grads['ffn1_wd'] = _ffn_bwd(
        h0, w['ffn1_norm'], w['ffn1_wgu'], w['ffn1_wd'], ffn1_saved, d_h1, "ffn1")
    return loss, d_h0, grads


_WIN_SEGMENTS = ((0, 4096), (4112, 7184), (7472, 9520), (7184, 7472), (4096, 4112))


def _win_to_padded(w_in):
    parts = [w_in[:, a:b] for a, b in _WIN_SEGMENTS]
    parts.append(jnp.zeros((w_in.shape[0], ZP - IN_TOTAL), w_in.dtype))
    return jnp.concatenate(parts, axis=1)


def _win_from_padded(w_p):
    widths = [b - a for a, b in _WIN_SEGMENTS]
    offs = [sum(widths[:i]) for i in range(len(widths))]
    seg = {a: w_p[:, o:o + wd] for (a, _), o, wd in zip(_WIN_SEGMENTS, offs, widths)}
    return jnp.concatenate([seg[a] for a in sorted(seg)], axis=1)


def _shift_bwd3(z, mu, d_r1, d_r2, d_k, d_v1, d_v2, t):
    nb = D // LANES

    def body(z_ref, mu_ref, r1, r2, kk, v1, v2, dz_ref, dmu_ref):
        j = pl.program_id(0)
        ct = jnp.where(j < nb, r1[...] + r2[...], jnp.where(j < 2 * nb, kk[...], v1[...] + v2[...]))
        _, vjp = jax.vjp(lambda a, b: _f_tshift(a, b), z_ref[...], mu_ref[...])
        dz, dmu = vjp((ct,))
        dz_ref[...] = dz
        dmu_ref[...] = dmu

    def window(first):
        return pl.BlockSpec((t, LANES), lambda j, f=first: (0, jnp.clip(j - f * nb, 0, nb - 1)))

    return pl.pallas_call(
        body, name="b_shift_bwd", grid=(3 * nb,),
        in_specs=[_col_spec(t, 32), pl.BlockSpec((1, LANES), lambda j: (0, j)), window(0), window(0), window(1),
                  window(2), window(2)],
        out_specs=[_col_spec(t, 0), pl.BlockSpec((1, LANES), lambda j: (0, j))],
        out_shape=[jax.ShapeDtypeStruct((t, 3 * D), f32), jax.ShapeDtypeStruct((1, 3 * D), f32)],
    )(z, mu, d_r1, d_r2, d_k, d_v1, d_v2)


def _conv_bwd(fn, z, conv_w, ct, idx, t):
    def body(z_ref, w_ref, ct_ref, dz_ref, dw_ref):
        _, vjp = jax.vjp(lambda a, b: fn(a, b), z_ref[...], w_ref[...])
        dz, dw = vjp((ct_ref[...],))
        dz_ref[...] = dz
        dw_ref[...] = dw

    return pl.pallas_call(
        body, name=f"a_conv{idx}_bwd", grid=(A_HEADS,),
        in_specs=[_col_spec(t, 8 * idx), pl.BlockSpec((4, LANES), lambda j, o=8 * idx: (0, j + o)), _col_spec(t, 0)],
        out_specs=[_col_spec(t, 0), pl.BlockSpec((4, LANES), lambda j: (0, j))],
        out_shape=[jax.ShapeDtypeStruct((t, D), f32), jax.ShapeDtypeStruct((4, D), f32)],
    )(z, conv_w, ct)


def _position():
    return lax.axis_index("x"), lax.axis_index("y"), lax.axis_index("c")


def _flip(v, f):
    return 1 - v if f else v


_CHIP_FLIPS = ((1, 0), (0, 1), (1, 1))


def _gather_chips(arrs, name):
    n = len(arrs)
    assert all(a.shape[0] % 32 == 0 for a in arrs)
    arrs = [a.reshape(2, a.shape[0] // 2, a.shape[1]) for a in arrs]

    def body(*refs):
        ins, outs = refs[:n], refs[n:2 * n]
        send, recv, fsend, frecv, own = refs[2 * n:]
        x, y, c = _position()
        me = 2 * x + y
        sends, plan, owns = [], [], []
        for a in range(n):
            cp = pltpu.make_async_remote_copy(src_ref=ins[a], dst_ref=outs[a].at[me], send_sem=own.at[a, 0],
                                              recv_sem=own.at[a, 1], device_id=(x, y, 1 - c), device_id_type=MESH)
            cp.start()
            owns.append(cp)
            for j, (fx, fy) in enumerate(_CHIP_FLIPS):
                px, py = _flip(x, fx), _flip(y, fy)
                p = 2 * px + py
                cp = pltpu.make_async_remote_copy(src_ref=ins[a].at[c], dst_ref=outs[a].at[me, c],
                                                  send_sem=send.at[a, j], recv_sem=recv.at[a, j],
                                                  device_id=(px, py, c), device_id_type=MESH)
                cp.start()
                sends.append(cp)
                landed = pltpu.make_async_remote_copy(src_ref=ins[a].at[c], dst_ref=outs[a].at[p, c],
                                                      send_sem=send.at[a, j], recv_sem=recv.at[a, j],
                                                      device_id=(px, py, c), device_id_type=MESH)
                onward = pltpu.make_async_remote_copy(src_ref=outs[a].at[p, c], dst_ref=outs[a].at[p, c],
                                                      send_sem=fsend.at[a, j], recv_sem=frecv.at[a, j],
                                                      device_id=(x, y, 1 - c), device_id_type=MESH)
                from_sibling = pltpu.make_async_remote_copy(src_ref=outs[a].at[p, 1 - c], dst_ref=outs[a].at[p, 1 - c],
                                                            send_sem=fsend.at[a, j], recv_sem=frecv.at[a, j],
                                                            device_id=(x, y, 1 - c), device_id_type=MESH)
                plan.append((landed, onward, from_sibling))
        for landed, onward, _ in plan:
            landed.wait_recv()
            onward.start()
        for _, _, from_sibling in plan:
            from_sibling.wait_recv()
        for cp in sends:
            cp.wait_send()
        for _, onward, _ in plan:
            onward.wait_send()
        for cp in owns:
            cp.wait()

    sems = [pltpu.SemaphoreType.DMA((n, 3))] * 4 + [pltpu.SemaphoreType.DMA((n, 2))]
    outs = pl.pallas_call(
        body, name=name, in_specs=[ANY] * n, out_specs=[ANY] * n,
        out_shape=[jax.ShapeDtypeStruct((N_CHIPS,) + a.shape, a.dtype) for a in arrs], scratch_shapes=sems,
    )(*arrs)
    return [o.reshape(N_CHIPS, o.shape[1] * o.shape[2], o.shape[3]) for o in outs]


def _swap_sibling(arrs, src_of, shapes, name):
    n = len(arrs)

    def body(*refs):
        a_refs, got_refs = refs[:n], refs[n:2 * n]
        send, recv = refs[2 * n:]
        x, y, c = _position()
        copies = []
        for i in range(n):
            cp = pltpu.make_async_remote_copy(src_ref=src_of(a_refs[i], c), dst_ref=got_refs[i], send_sem=send.at[i],
                                              recv_sem=recv.at[i], device_id=(x, y, 1 - c), device_id_type=MESH)
            cp.start()
            copies.append(cp)
        for cp in copies:
            cp.wait()

    return pl.pallas_call(body, name=name, in_specs=[ANY] * n, out_specs=[ANY] * n,
                          out_shape=[jax.ShapeDtypeStruct(sh, a.dtype) for sh, a in zip(shapes, arrs)],
                          scratch_shapes=[pltpu.SemaphoreType.DMA((n,))] * 2)(*arrs)


def _row_tile(rows, width):
    return _tile(rows, max(16, (784 * LANES // width) // 16 * 16), 16)


def _add_halves(g, got, dtype, name):
    n, _, hr, w = g.shape
    tr = _row_tile(hr, w)

    def body(g_ref, got_ref, o_ref):
        c = lax.axis_index("c")
        own = jnp.where(c == 0, g_ref[:, 0], g_ref[:, 1])
        o_ref[...] = (own + got_ref[...]).astype(dtype)

    return pl.pallas_call(
        body, name=name, grid=(hr // tr,),
        in_specs=[pl.BlockSpec((n, 2, tr, w), lambda i: (0, 0, i, 0)), pl.BlockSpec((n, tr, w), lambda i: (0, i, 0))],
        out_specs=pl.BlockSpec((n, tr, w), lambda i: (0, i, 0)),
        out_shape=jax.ShapeDtypeStruct((n, hr, w), dtype))(g, got)


def _scatter_chips(gs, name):
    n = len(gs)

    def body(*refs):
        g_refs, out_refs = refs[:n], refs[n:2 * n]
        send, recv = refs[2 * n:]
        x, y, c = _position()
        sends = []
        for i in range(n):
            for j, (fx, fy) in enumerate(_CHIP_FLIPS):
                px, py = _flip(x, fx), _flip(y, fy)
                cp = pltpu.make_async_remote_copy(src_ref=g_refs[i].at[2 * px + py], dst_ref=out_refs[i].at[j],
                                                  send_sem=send.at[i, j], recv_sem=recv.at[i, j],
                                                  device_id=(px, py, c), device_id_type=MESH)
                cp.start()
                sends.append(cp)
        for cp in sends:
            cp.wait_recv()
        for cp in sends:
            cp.wait_send()

    return pl.pallas_call(
        body, name=name, in_specs=[ANY] * n, out_specs=[ANY] * n,
        out_shape=[jax.ShapeDtypeStruct((3,) + g.shape[1:], g.dtype) for g in gs],
        scratch_shapes=[pltpu.SemaphoreType.DMA((n, 3)), pltpu.SemaphoreType.DMA((n, 3))],
    )(*gs)


def _sum_own_and_slots(own, got, name):
    n, r, w = own.shape
    tr = _row_tile(r, w)

    def body(own_ref, got_ref, o_ref):
        me = 2 * lax.axis_index("x") + lax.axis_index("y")
        acc = own_ref[0]
        for i in range(1, n):
            acc = jnp.where(me == i, own_ref[i], acc)
        acc = acc.astype(f32)
        for j in range(3):
            acc = acc + got_ref[j].astype(f32)
        o_ref[...] = acc

    return pl.pallas_call(
        body, name=name, grid=(r // tr,),
        in_specs=[pl.BlockSpec((n, tr, w), lambda i: (0, i, 0)), pl.BlockSpec((3, tr, w), lambda i: (0, i, 0))],
        out_specs=pl.BlockSpec((tr, w), lambda i: (i, 0)), out_shape=jax.ShapeDtypeStruct((r, w), f32))(own, got)


def _share_chips(a, name):
    def body(a_ref, out_ref, send, recv):
        x, y, c = _position()
        sends = []
        for j, (fx, fy) in enumerate(_CHIP_FLIPS):
            cp = pltpu.make_async_remote_copy(src_ref=a_ref, dst_ref=out_ref.at[j], send_sem=send.at[j],
                                              recv_sem=recv.at[j], device_id=(_flip(x, fx), _flip(y, fy), c),
                                              device_id_type=MESH)
            cp.start()
            sends.append(cp)
        for cp in sends:
            cp.wait_recv()
        for cp in sends:
            cp.wait_send()

    return pl.pallas_call(
        body, name=name, in_specs=[ANY], out_specs=ANY, out_shape=jax.ShapeDtypeStruct((3,) + a.shape, a.dtype),
        scratch_shapes=[pltpu.SemaphoreType.DMA((3,)), pltpu.SemaphoreType.DMA((3,))],
    )(a)


def _sum_in_chip_order(pair, got, name):
    r, w = pair.shape
    tr = _tile(r, 1408, 8)

    def body(p_ref, g_ref, o_ref):
        x, y = lax.axis_index("x"), lax.axis_index("y")
        me = 2 * x + y
        across = [2 * _flip(x, fx) + _flip(y, fy) for fx, fy in _CHIP_FLIPS]
        acc = None
        for i in range(N_CHIPS):
            term = p_ref[...]
            for j in range(3):
                term = jnp.where(across[j] == i, g_ref[j], term)
            acc = term if acc is None else acc + term
        o_ref[...] = acc

    return pl.pallas_call(
        body, name=name, grid=(r // tr,),
        in_specs=[pl.BlockSpec((tr, w), lambda i: (i, 0)), pl.BlockSpec((3, tr, w), lambda i: (0, i, 0))],
        out_specs=pl.BlockSpec((tr, w), lambda i: (i, 0)), out_shape=jax.ShapeDtypeStruct((r, w), f32))(pair, got)


def _add2(a, b, name):
    r, w = a.shape
    tr = _tile(r, 1408, 8)
    spec = pl.BlockSpec((tr, w), lambda i: (i, 0))

    def body(a_ref, b_ref, o_ref):
        o_ref[...] = a_ref[...] + b_ref[...]

    return pl.pallas_call(body, name=name, grid=(r // tr,), in_specs=[spec, spec], out_specs=spec,
                          out_shape=jax.ShapeDtypeStruct(a.shape, f32))(a, b)


def _adamw(w, g_parts, m, v, name):
    shape = w.shape
    view = shape if len(shape) >= 2 else (1,) + shape
    assert all(d == 1 for d in view[:-2]), shape
    rows, cols = view[-2:]
    cap = max(8, (256 * 1024 // cols) // 8 * 8)
    tr = rows if rows <= cap else _tile(rows, cap, 8)
    lead = len(view) - 2
    n_g = len(g_parts)

    def body(*refs):
        w_ref = refs[0]
        g_refs = refs[1:1 + n_g]
        m_ref, v_ref, g_out, d_out, m_out, v_out = refs[1 + n_g:]
        g = g_refs[0][...]
        for gr in g_refs[1:]:
            g = g + gr[...]
        m_new = ADAM_B1 * m_ref[...] + (1.0 - ADAM_B1) * g
        v_new = ADAM_B2 * v_ref[...] + (1.0 - ADAM_B2) * (g * g)
        m_hat = m_new / (1.0 - ADAM_B1 ** ADAM_STEP)
        v_hat = v_new / (1.0 - ADAM_B2 ** ADAM_STEP)
        g_out[...] = g
        d_out[...] = -ADAM_LR * (m_hat / (jnp.sqrt(v_hat) + ADAM_EPS) + ADAM_WD * w_ref[...])
        m_out[...] = m_new
        v_out[...] = v_new

    spec = pl.BlockSpec((None,) * lead + (tr, cols), lambda i: (0,) * lead + (i, 0))
    args = [w.reshape(view)] + [g.reshape(view) for g in g_parts] + [m.reshape(view), v.reshape(view)]
    outs = pl.pallas_call(body, name=name, grid=(rows // tr,), in_specs=[spec] * len(args), out_specs=[spec] * 4,
                          out_shape=[jax.ShapeDtypeStruct(view, f32)] * 4)(*args)
    return [o.reshape(shape) for o in outs]


_BIG = ('ffn1_w_gu', 'ffn1_w_down', 'w_in', 'w_out', 'ffn2_w_gu', 'ffn2_w_down')
_SMALL_SHARDED = ('meta_tokens', 'a_conv_w', 'b_w_up', 'b_a_up', 'b_g_up')
_WEIGHTS = ('meta_tokens', 'ffn1_norm', 'ffn1_w_gu', 'ffn1_w_down', 'mix_norm', 'w_in', 'a_conv_w', 'a_log_rate',
            'a_dt_bias', 'a_out_norm', 'b_shift_mu', 'b_w0', 'b_w_up', 'b_a0', 'b_a_up', 'b_g_up', 'b_k_k', 'b_k_a',
            'b_r_k', 'b_ln_gain', 'b_ln_bias', 'w_out', 'ffn2_norm', 'ffn2_w_gu', 'ffn2_w_down', 'final_norm')
_SMALL = tuple(n for n in _WEIGHTS if n not in _BIG)


def _rows_of(shape):
    n = 1
    for d in shape:
        n *= d
    return n, -(-n // LANES)


def _pack(arrs, dtype, row_mult=32):
    parts, total = [], 0
    for a in arrs:
        n, rows = _rows_of(a.shape)
        flat = a.reshape(-1).astype(dtype)
        if n % LANES:
            flat = jnp.pad(flat, (0, rows * LANES - n))
        parts.append(flat)
        total += rows
    extra = -total % row_mult
    if extra:
        parts.append(jnp.zeros((extra * LANES,), dtype))
    return jnp.concatenate(parts).reshape(total + extra, LANES)


def _unpack(packed, shapes, lead=()):
    out, off = [], 0
    for sh in shapes:
        n, rows = _rows_of(sh)
        seg = packed[..., off:off + rows, :]
        if n % LANES:
            seg = seg.reshape(lead + (-1,))[..., :n]
        out.append(seg.reshape(lead + tuple(sh)))
        off += rows
    return out


def _cols_from_shards(s):
    return jnp.concatenate([s[i] for i in range(N_CHIPS)], axis=-1)


def _cols_to_shards(a):
    r, c = a.shape
    return a.reshape(r, N_CHIPS, c // N_CHIPS).transpose(1, 0, 2)


def kernel(x, meta_tokens, ffn1_norm, ffn1_w_gu, ffn1_w_down, mix_norm, w_in, a_conv_w, a_log_rate, a_dt_bias, a_out_norm, b_shift_mu, b_w0, b_w_up, b_a0, b_a_up, b_g_up, b_k_k, b_k_a, b_r_k, b_ln_gain, b_ln_bias, w_out, ffn2_norm, ffn2_w_gu, ffn2_w_down, final_norm, loss_target, m_meta_tokens, m_ffn1_norm, m_ffn1_w_gu, m_ffn1_w_down, m_mix_norm, m_w_in, m_a_conv_w, m_a_log_rate, m_a_dt_bias, m_a_out_norm, m_b_shift_mu, m_b_w0, m_b_w_up, m_b_a0, m_b_a_up, m_b_g_up, m_b_k_k, m_b_k_a, m_b_r_k, m_b_ln_gain, m_b_ln_bias, m_w_out, m_ffn2_norm, m_ffn2_w_gu, m_ffn2_w_down, m_final_norm, v_meta_tokens, v_ffn1_norm, v_ffn1_w_gu, v_ffn1_w_down, v_mix_norm, v_w_in, v_a_conv_w, v_a_log_rate, v_a_dt_bias, v_a_out_norm, v_b_shift_mu, v_b_w0, v_b_w_up, v_b_a0, v_b_a_up, v_b_g_up, v_b_k_k, v_b_k_a, v_b_r_k, v_b_ln_gain, v_b_ln_bias, v_w_out, v_ffn2_norm, v_ffn2_w_gu, v_ffn2_w_down, v_final_norm):
    args = locals()
    wts = {n: args[n] for n in _WEIGHTS}
    mom = {n: args["m_" + n] for n in _WEIGHTS}
    var = {n: args["v_" + n] for n in _WEIGHTS}
    chip = 2 * lax.axis_index("x") + lax.axis_index("y")

    big_shapes = [wts[n].shape[1:] for n in _BIG]
    small_shapes = [wts[n].shape[-2:] for n in _SMALL_SHARDED]
    big_flat = [wts[n].astype(bf16).reshape(wts[n].shape[1:]) for n in _BIG]
    small_packed = _pack([wts[n] for n in _SMALL_SHARDED], f32)
    gathered = _gather_chips(big_flat + [small_packed], "gather_weights")
    gu1, dn1, w_in_s, w_out_s, gu2, dn2 = [a.reshape((N_CHIPS,) + tuple(sh)) for a, sh in zip(gathered, big_shapes)]
    meta_s, conv_s, wup_s, aup_s, gup_s = _unpack(gathered[-1], small_shapes, (N_CHIPS,))
    w = {
        'ffn1_norm': ffn1_norm, 'mix_norm': mix_norm, 'ffn2_norm': ffn2_norm, 'final_norm': final_norm[None, :],
        'ffn1_wgu': gu1, 'ffn1_wd': dn1.reshape(D_FF, D), 'ffn2_wgu': gu2, 'ffn2_wd': dn2.reshape(D_FF, D),
        'w_in_p': _win_to_padded(_cols_from_shards(w_in_s)), 'w_out': w_out_s.reshape(D, D),
        'a_conv_w': _cols_from_shards(conv_s), 'b_w_up': _cols_from_shards(wup_s), 'b_a_up': _cols_from_shards(aup_s),
        'b_g_up': _cols_from_shards(gup_s),
        'a_log_rate': a_log_rate, 'a_dt_bias': a_dt_bias, 'a_out_norm': a_out_norm, 'b_shift_mu': b_shift_mu,
        'b_w0': b_w0, 'b_a0': b_a0, 'b_k_k': b_k_k, 'b_k_a': b_k_a, 'b_r_k': b_r_k, 'b_ln_gain': b_ln_gain,
        'b_ln_bias': b_ln_bias,
    }
    meta_full = _cols_from_shards(meta_s)

    h0 = jnp.concatenate([jnp.zeros((PAD, D), f32), meta_full, x[0]], axis=0)
    tgt = jnp.concatenate([jnp.zeros((SKIP, D), f32), loss_target[0]], axis=0)
    loss_local, d_h0, g = _local_step(h0, tgt, w)
    loss = lax.psum(loss_local, ("x", "y", "c"))
    grad_x = d_h0[SKIP:][None]

    big_grads = [
        jnp.concatenate([g['ffn1_wg'], g['ffn1_wu']], axis=0),
        g['ffn1_wd'].reshape(N_CHIPS, D_FF // N_CHIPS, D),
        _cols_to_shards(_win_from_padded(g['w_in_p'])),
        g['w_out'].reshape(N_CHIPS, D // N_CHIPS, D),
        jnp.concatenate([g['ffn2_wg'], g['ffn2_wu']], axis=0),
        g['ffn2_wd'].reshape(N_CHIPS, D_FF // N_CHIPS, D),
    ]
    g_halves = [a.reshape(N_CHIPS, 2, a.shape[1] // 2, a.shape[2]) for a in big_grads]
    sib_halves = _swap_sibling(g_halves, lambda ref, c: ref.at[:, 1 - c], [a.shape[:1] + a.shape[2:] for a in g_halves],
                               "swap_halves")
    chip_halves = [_add_halves(a, b, bf16, f"add_sibling{i}") for i, (a, b) in enumerate(zip(g_halves, sib_halves))]
    got = _scatter_chips(chip_halves, "scatter_grads")
    mine = [_sum_own_and_slots(a, b, f"sum_chips{i}") for i, (a, b) in enumerate(zip(chip_halves, got))]
    theirs = _swap_sibling(mine, lambda ref, c: ref, [a.shape for a in mine], "swap_sums")
    core = lax.axis_index("c")
    big_parts = [jnp.concatenate([jnp.where(core == 0, a, b), jnp.where(core == 0, b, a)], axis=0)
                 for a, b in zip(mine, theirs)]

    small_full = {
        'meta_tokens': d_h0[PAD:SKIP], 'ffn1_norm': g['ffn1_norm'], 'mix_norm': g['mix_norm'], 'a_conv_w': g['a_conv_w'],
        'a_log_rate': g['a_log_rate'], 'a_dt_bias': g['a_dt_bias'], 'a_out_norm': g['a_out_norm'],
        'b_shift_mu': g['b_shift_mu'], 'b_w0': g['b_w0'], 'b_w_up': g['b_w_up'], 'b_a0': g['b_a0'], 'b_a_up': g['b_a_up'],
        'b_g_up': g['b_g_up'], 'b_k_k': g['b_k_k'], 'b_k_a': g['b_k_a'], 'b_r_k': g['b_r_k'], 'b_ln_gain': g['b_ln_gain'],
        'b_ln_bias': g['b_ln_bias'], 'ffn2_norm': g['ffn2_norm'], 'final_norm': g['final_norm'],
    }
    s_shapes = [small_full[n].shape for n in _SMALL]
    s_packed = _pack([small_full[n] for n in _SMALL], f32, row_mult=256)
    (s_sib,) = _swap_sibling([s_packed], lambda ref, c: ref, [s_packed.shape], "swap_small")
    s_pair = _add2(s_packed, s_sib, "add_small")
    s_sum = _sum_in_chip_order(s_pair, _share_chips(s_pair, "share_small"), "sum_small")
    s_parts = dict(zip(_SMALL, _unpack(s_sum, s_shapes)))

    grad, delta, new_m, new_v = {}, {}, {}, {}
    for n, a in zip(_BIG, big_parts):
        grad[n], delta[n], new_m[n], new_v[n] = _adamw(wts[n], [a.reshape(wts[n].shape)], mom[n], var[n], f"adamw_{n}")
    for n in _SMALL:
        gs = s_parts[n]
        if n in _SMALL_SHARDED:
            width = wts[n].shape[-1]
            gs = lax.dynamic_slice_in_dim(gs, chip * width, width, axis=gs.ndim - 1)
        gs = gs.reshape(wts[n].shape)
        grad[n], delta[n], new_m[n], new_v[n] = _adamw(wts[n], [gs], mom[n], var[n], f"adamw_{n}")

    return (loss, grad_x, *[grad[n] for n in _WEIGHTS], *[delta[n] for n in _WEIGHTS],
            *[new_m[n] for n in _WEIGHTS], *[new_v[n] for n in _WEIGHTS])
```

```python
import functools

import jax
import jax.numpy as jnp
from jax import lax
from jax.experimental import pallas as pl
from jax.experimental.pallas import tpu as pltpu

f32 = jnp.float32
bf16 = jnp.bfloat16
HI = lax.Precision.HIGHEST
MESH = pl.DeviceIdType.MESH
ANY = pl.BlockSpec(memory_space=pl.ANY)

D = 1024
N_META = 16
CHUNK = 64
PAD = CHUNK - N_META
SKIP = PAD + N_META
EPS = 1e-6
D_FF = 2816
A_HEADS = 8
A_DK = 128
B_HEADS = 16
B_N = 64
B_GN_EPS = B_N * 1e-5
W_LORA, AA_LORA, G_LORA = 64, 64, 160
IN_TOTAL = 9520
ZP = 9600
LANES = 128
N_CHIPS = 4
N_DEV = 8

ADAM_LR, ADAM_B1, ADAM_B2, ADAM_EPS, ADAM_WD, ADAM_STEP = 0.001, 0.9, 0.999, 1e-08, 0.01, 10

MXU_DTYPE = bf16


def _tile(n, cap, mult):
    if n <= cap:
        return n
    best = None
    for t in range(mult, cap + 1, mult):
        if n % t == 0:
            best = t
    assert best is not None, (n, cap, mult)
    return best


def _sigmoid(x):
    return jax.nn.sigmoid(x)


def _silu(x):
    return x * jax.nn.sigmoid(x)


def _softplus(x):
    return jnp.maximum(x, 0.0) + jnp.log(1.0 + jnp.exp(-jnp.abs(x)))


def _head_matrix(c, nh):
    hd = c // nh
    r = lax.broadcasted_iota(jnp.int32, (c, nh), 0)
    h = lax.broadcasted_iota(jnp.int32, (c, nh), 1)
    return (r >= h * hd) & (r < (h + 1) * hd)


def _dot_exact_rhs(x, e, cb):
    dn = (((1,), (cb,)), ((), ()))
    if SCAN_PASSES == 0:
        return lax.dot_general(x, e.astype(f32), dn, precision=HI, preferred_element_type=f32)
    eb = e.astype(bf16)
    hi = x.astype(bf16)
    lo = (x - hi.astype(f32)).astype(bf16)
    return (lax.dot_general(hi, eb, dn, preferred_element_type=f32)
            + lax.dot_general(lo, eb, dn, preferred_element_type=f32))


def _head_sum_impl(x, nh):
    e = _head_matrix(x.shape[-1], nh)
    return _dot_exact_rhs(_dot_exact_rhs(x, e, 0), e, 1)


@functools.partial(jax.custom_vjp, nondiff_argnums=(1,))
def _head_sum(x, nh):
    return _head_sum_impl(x, nh)


def _head_sum_fwd(x, nh):
    return _head_sum_impl(x, nh), None


def _head_sum_bwd(nh, _, g):
    return (_head_sum_impl(g, nh),)


_head_sum.defvjp(_head_sum_fwd, _head_sum_bwd)


@functools.partial(jax.custom_vjp, nondiff_argnums=(1,))
def _shift_rows(x, s):
    n = x.shape[0]
    row = lax.broadcasted_iota(jnp.int32, x.shape, 0)
    if s > 0:
        return jnp.where(row >= s, pltpu.roll(x, s, 0), 0.0)
    return jnp.where(row < n + s, pltpu.roll(x, n + s, 0), 0.0)


def _shift_rows_fwd(x, s):
    return _shift_rows(x, s), None


def _shift_rows_bwd(s, _, g):
    return (_shift_rows(g, -s),)


_shift_rows.defvjp(_shift_rows_fwd, _shift_rows_bwd)


def _matmul(a, b, *, ta=False, tb=False, res=None, scale=1.0, name, b_cols_split=None, out_cols_split=False):
    assert not (ta and tb)
    (ar, ac) = a.shape
    b0 = 0
    if b_cols_split:
        b0, bs = b_cols_split
        _, br, bc_part = b.shape
        bc = bs * bc_part
    else:
        br, bc = b.shape
    m, k = (ac, ar) if ta else (ar, ac)
    n, kb = (br, bc) if tb else (bc, br)
    assert k == kb, (a.shape, b.shape, ta, tb)
    tm = _tile(m, 1408, LANES) if ta else _tile(m, 832, 8)
    tn = _tile(n, 1408, LANES)
    tk = _tile(k, 1040, 8) if ta else _tile(k, 1408, LANES)
    nk = k // tk
    dn = (((0 if ta else 1,), (1 if tb else 0,)), ((), ()))
    if b_cols_split:
        assert (tk if tb else tn) == bc_part, (b.shape, tn, tk)

    def body(*refs):
        if res is not None:
            a_ref, b_ref, r_ref, o_ref, acc = refs
        else:
            a_ref, b_ref, o_ref, acc = refs
        kk = pl.program_id(2)

        @pl.when(kk == 0)
        def _():
            acc[...] = jnp.zeros_like(acc)

        acc[...] += lax.dot_general(a_ref[...].astype(MXU_DTYPE), b_ref[...].astype(MXU_DTYPE), dn,
                                    preferred_element_type=f32,
                                    precision=None if MXU_DTYPE == bf16 else HI)

        @pl.when(kk == nk - 1)
        def _():
            out = acc[...]
            if scale != 1.0:
                out = out * scale
            if res is not None:
                out = r_ref[...] + out
            o_ref[...] = out

    if ta:
        a_spec = pl.BlockSpec((tk, tm), lambda i, j, kk: (kk, i))
    else:
        a_spec = pl.BlockSpec((tm, tk), lambda i, j, kk: (i, kk))
    if tb and b_cols_split:
        b_spec = pl.BlockSpec((None, tn, tk), lambda i, j, kk: (kk + b0, j, 0))
    elif tb:
        b_spec = pl.BlockSpec((tn, tk), lambda i, j, kk: (j, kk))
    elif b_cols_split:
        b_spec = pl.BlockSpec((None, tk, tn), lambda i, j, kk: (j + b0, kk, 0))
    else:
        b_spec = pl.BlockSpec((tk, tn), lambda i, j, kk: (kk, j))
    in_specs = [a_spec, b_spec]
    args = [a, b]
    if res is not None:
        in_specs.append(pl.BlockSpec((tm, tn), lambda i, j, kk: (i, j)))
        args.append(res)
    if out_cols_split:
        out_spec = pl.BlockSpec((None, tm, tn), lambda i, j, kk: (j, i, 0))
        out_shape = jax.ShapeDtypeStruct((n // tn, m, tn), f32)
    else:
        out_spec = pl.BlockSpec((tm, tn), lambda i, j, kk: (i, j))
        out_shape = jax.ShapeDtypeStruct((m, n), f32)
    return pl.pallas_call(
        body, name=name, grid=(m // tm, n // tn, nk), in_specs=in_specs, out_specs=out_spec, out_shape=out_shape,
        scratch_shapes=[pltpu.VMEM((tm, tn), f32)],
        compiler_params=pltpu.CompilerParams(dimension_semantics=("parallel", "parallel", "arbitrary")),
    )(*args)


def _tw_fwd(fn, ins, in_specs, out_shapes, out_specs, grid, name, with_pid=False):
    n_in = len(ins)

    def body(*refs):
        vals = [r[...] for r in refs[:n_in]]
        outs = fn(pl.program_id(0), *vals) if with_pid else fn(*vals)
        for r, o in zip(refs[n_in:], outs):
            r[...] = o.astype(r.dtype)

    return pl.pallas_call(body, name=name, grid=grid, in_specs=in_specs, out_specs=out_specs,
                          out_shape=out_shapes)(*ins)


def _tw_bwd(fn, ins, in_specs, cts, ct_specs, kinds, grid, name, with_pid=False, tile_dtype=f32, ct_extra=(),
            residual=None):
    n_in, n_ct = len(ins), len(cts)
    diff = [i for i, kd in enumerate(kinds) if kd is not None]
    n_ex = len(ct_extra)

    def body(*refs):
        vals = [r[...] for r in refs[:n_in]]
        ctv = [r[...].astype(f32) for r in refs[n_in:n_in + n_ct]]
        for (ci, _), r in zip(ct_extra, refs[n_in + n_ct:n_in + n_ct + n_ex]):
            ctv[ci] = ctv[ci] + r[...]
        ctv = tuple(ctv)
        n_fixed = n_in + n_ct + n_ex
        res_ref = refs[n_fixed] if residual is not None else None
        g_refs = refs[n_fixed + (residual is not None):]
        pid = pl.program_id(0)

        def f(*dv):
            full = list(vals)
            for i, v in zip(diff, dv):
                full[i] = v
            out = fn(pid, *full) if with_pid else fn(*full)
            return tuple(out)

        _, vjp = jax.vjp(f, *[vals[i] for i in diff])
        gs = vjp(ctv)
        first = pid == 0
        for i2 in range(1, len(grid)):
            first = first & (pl.program_id(i2) == 0)
        for i, g, g_ref in zip(diff, gs, g_refs):
            if kinds[i] != 'acc':
                if i == 0 and res_ref is not None:
                    g = res_ref[...] + g
                g_ref[...] = g.astype(g_ref.dtype)
            else:
                @pl.when(first)
                def _(g=g, g_ref=g_ref):
                    g_ref[...] = g

                @pl.when(jnp.logical_not(first))
                def _(g=g, g_ref=g_ref):
                    g_ref[...] += g

    zero_map = {1: lambda *a: (0,), 2: lambda *a: (0, 0), 3: lambda *a: (0, 0, 0)}
    out_specs, out_shapes = [], []
    for i in diff:
        if kinds[i] == 'tile':
            out_shapes.append(jax.ShapeDtypeStruct(ins[i].shape, tile_dtype))
            out_specs.append(in_specs[i])
        elif kinds[i] == 'acc':
            out_shapes.append(jax.ShapeDtypeStruct(ins[i].shape, f32))
            out_specs.append(pl.BlockSpec(ins[i].shape, zero_map[ins[i].ndim]))
        else:
            out_shapes.append(jax.ShapeDtypeStruct(kinds[i][1], tile_dtype))
            out_specs.append(kinds[i][2])
    extra_specs = [ct_specs[ci] for ci, _ in ct_extra]
    extra = [a for _, a in ct_extra]
    if residual is not None:
        assert kinds[0] == 'tile'
        extra_specs.append(in_specs[0])
        extra.append(residual)
    return pl.pallas_call(body, name=name, grid=grid, in_specs=list(in_specs) + list(ct_specs) + extra_specs,
                          out_specs=out_specs, out_shape=out_shapes)(*ins, *cts, *extra)


def _row_spec(tm, c, col_block=0):
    return pl.BlockSpec((tm, c), lambda i, cb=col_block: (i, cb))


def _full_spec(shape):
    nd = len(shape)
    return pl.BlockSpec(shape, lambda *a, nd=nd: (0,) * nd)


def _f_rms(x, g):
    return (x * lax.rsqrt(jnp.mean(x * x, axis=-1, keepdims=True) + EPS) * g,)


def _f_swiglu(gate, up):
    return (_silu(gate) * up,)


def _f_loss(pid, h, g, tgt, *, tm):
    y = h * lax.rsqrt(jnp.mean(h * h, axis=-1, keepdims=True) + EPS) * g
    row = pid * tm + lax.broadcasted_iota(jnp.int32, (tm, 1), 0)
    err = jnp.where(row >= SKIP, y - tgt, 0.0)
    per_row = jnp.mean(err * err, axis=-1, keepdims=True)
    return (0.5 * jnp.sum(per_row, axis=0, keepdims=True),)


def _f_conv(x, w, *, norm, scale):
    y = x * w[3:4, :]
    for s in (1, 2, 3):
        y = y + _shift_rows(x, s) * w[3 - s:4 - s, :]
    y = _silu(y)
    if norm:
        y = y * lax.rsqrt(jnp.sum(y * y, axis=-1, keepdims=True) + 1e-6) * scale
    return (y,)


def _f_dgates(pid, abeta, aalpha, log_rate, dt_bias, *, tm):
    row = pid * tm + lax.broadcasted_iota(jnp.int32, (tm, 1), 0)
    live = row >= PAD
    beta = jnp.where(live, _sigmoid(abeta), 0.0)
    g = jnp.where(live, -jnp.exp(log_rate) * _softplus(aalpha + dt_bias), 0.0)
    return beta, g


def _f_tshift(z, mu):
    return (z + (_shift_rows(z, 1) - z) * mu,)


def _f_rwkv_pre(k, wd, ad, gd, w0, w_up, a0, a_up, g_up, k_k, k_a):
    w_log = -_softplus(-(w0 + _smm(jnp.tanh(wd), w_up, 1))) - 0.5
    lw = -jnp.exp(w_log)
    a_lr = _sigmoid(a0 + _smm(ad, a_up, 1))
    gate = _smm(_sigmoid(gd), g_up, 1)
    kkp = k * k_k
    kk = kkp * lax.rsqrt(_head_sum(kkp * kkp, B_HEADS) + 1e-6)
    kmod = k * (1.0 + (a_lr - 1.0) * k_a)
    return lw, kmod, -kk, kk * a_lr, gate


def _f_mix_post(o, az, y, r, kmod, v, gate, ga, gb, out_gain, ln_g, ln_b, r_k):
    ms = _head_sum(o * o, A_HEADS) * (1.0 / A_DK)
    oa = o * lax.rsqrt(ms + EPS) * out_gain * _silu(az)
    mean = _head_sum(y, B_HEADS) * (1.0 / B_N)
    yc = y - mean
    var = _head_sum(yc * yc, B_HEADS) * (1.0 / B_N)
    yn = yc * lax.rsqrt(var + B_GN_EPS) * ln_g + ln_b
    bonus = _head_sum(r * kmod * r_k, B_HEADS) * v
    ob = (yn + bonus) * gate
    return (_sigmoid(ga) * oa + _sigmoid(gb) * ob,)


SCAN_PASSES = 3


def _split2(a):
    hi = a.astype(bf16)
    return hi, (a - hi.astype(f32)).astype(bf16)


def _dot_passes(a, b, ca, cb, passes):
    dn = (((ca,), (cb,)), ((), ()))
    if SCAN_PASSES == 0:
        return lax.dot_general(a, b, dn, precision=HI, preferred_element_type=f32)
    if passes == 1:
        return lax.dot_general(a.astype(bf16), b.astype(bf16), dn, preferred_element_type=f32)
    ah, al = _split2(a)
    bh, bl = _split2(b)
    return (lax.dot_general(ah, bh, dn, preferred_element_type=f32)
            + (lax.dot_general(ah, bl, dn, preferred_element_type=f32)
               + lax.dot_general(al, bh, dn, preferred_element_type=f32)))


@functools.partial(jax.custom_vjp, nondiff_argnums=(2, 3, 4))
def _sdot(a, b, ca, cb, passes):
    return _dot_passes(a, b, ca, cb, passes)


def _sdot_fwd(a, b, ca, cb, passes):
    return _dot_passes(a, b, ca, cb, passes), (a, b)


def _sdot_bwd(ca, cb, passes, res, g):
    a, b = res
    if (ca, cb) == (1, 0):
        return _dot_passes(g, b, 1, 1, passes), _dot_passes(a, g, 0, 0, passes)
    if (ca, cb) == (1, 1):
        return _dot_passes(g, b, 1, 0, passes), _dot_passes(g, a, 0, 0, passes)
    assert (ca, cb) == (0, 0)
    return _dot_passes(b, g, 1, 1, passes), _dot_passes(a, g, 1, 0, passes)


_sdot.defvjp(_sdot_fwd, _sdot_bwd)


def _smm(a, b, passes=3):
    return _sdot(a, b, 1, 0, passes)


def _smm_nt(a, b, passes=3):
    return _sdot(a, b, 1, 1, passes)


def _smm_tn(a, b, passes=3):
    return _sdot(a, b, 0, 0, passes)


def _tri_dot(x, ca):
    n = x.shape[0]
    incl = _tri_masks(n)[0]
    dn = (((ca,), (0,)), ((), ()))
    if SCAN_PASSES == 0:
        return lax.dot_general(incl.astype(f32), x, dn, precision=HI, preferred_element_type=f32)
    tri = incl.astype(bf16)
    hi, r1 = x.astype(bf16), None
    r1 = x - hi.astype(f32)
    mid = r1.astype(bf16)
    lo = (r1 - mid.astype(f32)).astype(bf16)
    return (lax.dot_general(tri, hi, dn, preferred_element_type=f32)
            + (lax.dot_general(tri, mid, dn, preferred_element_type=f32)
               + lax.dot_general(tri, lo, dn, preferred_element_type=f32)))


@jax.custom_vjp
def _cumsum_rows(x):
    return _tri_dot(x, 1)


def _cumsum_rows_fwd(x):
    return _tri_dot(x, 1), None


def _cumsum_rows_bwd(_, g):
    return (_tri_dot(g, 0),)


_cumsum_rows.defvjp(_cumsum_rows_fwd, _cumsum_rows_bwd)


def _tri_masks(n):
    i = lax.broadcasted_iota(jnp.int32, (n, n), 0)
    j = lax.broadcasted_iota(jnp.int32, (n, n), 1)
    return i >= j, i > j, i == j, i <= j


def _unit_lower_inv_impl(low, passes):
    n = low.shape[0]
    assert n == CHUNK
    _, _, eye, _ = _tri_masks(n)
    acc = eye.astype(f32) + low
    p = low
    for _ in range(5):
        p = _dot_passes(p, p, 1, 0, passes)
        acc = acc + _dot_passes(acc, p, 1, 0, passes)
    return acc


@functools.partial(jax.custom_vjp, nondiff_argnums=(1,))
def _unit_lower_inv(low, passes=3):
    return _unit_lower_inv_impl(low, passes)


def _unit_lower_inv_fwd(low, passes):
    t = _unit_lower_inv_impl(low, passes)
    return t, t


def _unit_lower_inv_bwd(passes, t, g):
    return (_dot_passes(_dot_passes(t, g, 0, 0, passes), t, 1, 1, passes),)


_unit_lower_inv.defvjp(_unit_lower_inv_fwd, _unit_lower_inv_bwd)

DELTA_PASSES = 1
DELTA_INV_PASSES = 1


def _delta_chunk(s, q, k, v, beta_row, g_row):
    p = DELTA_PASSES
    incl, strict, eye, upper = _tri_masks(CHUNK)
    beta = jnp.sum(jnp.where(eye, beta_row, 0.0), axis=1, keepdims=True)
    g = jnp.sum(jnp.where(eye, g_row, 0.0), axis=1, keepdims=True)
    gc = jnp.sum(jnp.where(incl, g_row, 0.0), axis=1, keepdims=True)
    gc_row = jnp.sum(jnp.where(upper, g, 0.0), axis=0, keepdims=True)
    decay = jnp.where(incl, jnp.exp(jnp.where(incl, gc - gc_row, 0.0)), 0.0)
    kb = k * beta
    vb = v * beta
    m = jnp.where(strict, _smm_nt(kb, k, p) * decay, 0.0)
    tinv = _unit_lower_inv(-m, DELTA_INV_PASSES)
    u = _smm(tinv, vb, p)
    wk = _smm(tinv, kb * jnp.exp(gc), p)
    attn = _smm_nt(q, k, p) * decay
    qg = q * jnp.exp(gc)
    g_last = jnp.sum(g, axis=0, keepdims=True)
    k_tail = k * jnp.exp(g_last - gc)
    v_new = u - _smm(wk, s, p)
    o = _smm(qg, s, p) + _smm(attn, v_new, p)
    s_new = s * jnp.exp(g_last) + _smm_tn(k_tail, v_new, p)
    return o, s_new


RWKV_PASSES = 1
RWKV_INV_PASSES = 1


def _rwkv_chunk(st, r, k, v, a, b, lw):
    c = CHUNK
    p, pi = RWKV_PASSES, RWKV_INV_PASSES
    _, strict, _, _ = _tri_masks(c)
    lane = lax.broadcasted_iota(jnp.int32, (c, 2 * B_N), 1)
    row = lax.broadcasted_iota(jnp.int32, (c, 2 * B_N), 0)
    first = lane < B_N
    incl2 = row >= jnp.where(first, lane, lane - B_N)
    bi = lax.broadcasted_iota(jnp.int32, (2 * B_N, 2 * B_N), 0) < B_N
    bj = lax.broadcasted_iota(jnp.int32, (2 * B_N, 2 * B_N), 1) < B_N
    blockdiag = bi == bj
    cum = _cumsum_rows(lw)
    e_pos = jnp.exp(cum)
    e_neg = jnp.exp(-cum)
    rt = r * e_pos
    at = a * jnp.exp(cum - lw)
    kt = k * e_neg
    bt = b * e_neg
    bk = jnp.concatenate([bt, kt], axis=0)
    a_s0 = _smm_nt(at, st, p)
    r_s0 = _smm_nt(rt, st, p)
    heads = (first, jnp.logical_not(first))
    u = jnp.zeros((c, 2 * B_N), f32)
    for sel in heads:
        at_h = jnp.where(sel, at, 0.0)
        ab = jnp.where(strict, _smm_nt(at_h, bt, pi), 0.0)
        ak = jnp.where(strict, _smm_nt(at_h, kt, p), 0.0)
        t_h = _unit_lower_inv(ab, pi)
        u = u + _smm(t_h, jnp.where(sel, a_s0, 0.0) + _smm(ak, jnp.where(sel, v, 0.0), p), p)
    y = r_s0
    for sel in heads:
        rbk = jnp.where(incl2, _smm_nt(jnp.where(sel, rt, 0.0), bk, p), 0.0)
        uv = jnp.concatenate([jnp.where(sel, u, 0.0), jnp.where(sel, v, 0.0)], axis=0)
        y = y + _smm(rbk, uv, p)
    cl = jnp.sum(lw, axis=0, keepdims=True)
    dec = jnp.exp(cl - cum)
    uv_all = jnp.concatenate([u, v], axis=0)
    bk_dec = jnp.concatenate([b * dec, k * dec], axis=0)
    st_new = st * jnp.exp(cl) + jnp.where(blockdiag, _smm_tn(uv_all, bk_dec, p), 0.0)
    return y, st_new


GROUPS_PER_STEP = 8


def _scan_specs(ins, col_offs, n_chunks, reverse):
    gw = GROUPS_PER_STEP * LANES
    cidx = (lambda c: n_chunks - 1 - c) if reverse else (lambda c: c)
    specs = []
    for a, off in zip(ins, col_offs):
        if a.ndim == 2:
            assert off % gw == 0
            specs.append(pl.BlockSpec((CHUNK, gw), lambda h, c, o=off // gw: (cidx(c), h + o)))
        else:
            specs.append(pl.BlockSpec((GROUPS_PER_STEP, None, 1, CHUNK), lambda h, c: (h, cidx(c), 0, 0)))
    return specs, cidx


def _group_vals(refs, g):
    return [r[:, g * LANES:(g + 1) * LANES] if len(r.shape) == 2 else r[g] for r in refs]


def _scan_fwd(chunk_fn, ins, col_offs, n_groups, n_chunks, state_shape, name):
    n_in = len(ins)
    gps = GROUPS_PER_STEP
    t = ins[0].shape[0]

    def body(*refs):
        in_refs = refs[:n_in]
        o_ref, s0_ref, st = refs[n_in:]

        @pl.when(pl.program_id(1) == 0)
        def _():
            st[...] = jnp.zeros_like(st)

        states = st[...]
        vals = [jnp.stack(col) for col in zip(*[_group_vals(in_refs, g) for g in range(gps)])]
        o, s_new = jax.vmap(chunk_fn)(states, *vals)
        s0_ref[...] = states
        st[...] = s_new
        for g in range(gps):
            o_ref[:, g * LANES:(g + 1) * LANES] = o[g]

    specs, _ = _scan_specs(ins, col_offs, n_chunks, False)
    return pl.pallas_call(
        body, name=name, grid=(n_groups // gps, n_chunks), in_specs=specs,
        out_specs=[pl.BlockSpec((CHUNK, gps * LANES), lambda h, c: (c, h)),
                   pl.BlockSpec((gps, None) + state_shape, lambda h, c: (h, c, 0, 0))],
        out_shape=[jax.ShapeDtypeStruct((t, n_groups * LANES), f32),
                   jax.ShapeDtypeStruct((n_groups, n_chunks) + state_shape, f32)],
        scratch_shapes=[pltpu.VMEM((gps,) + state_shape, f32)],
        compiler_params=pltpu.CompilerParams(dimension_semantics=("parallel", "arbitrary")),
    )(*ins)


def _scan_bwd(chunk_fn, s0s, ins, col_offs, d_out, n_groups, n_chunks, state_shape, name):
    n_in = len(ins)
    gps = GROUPS_PER_STEP
    t = d_out.shape[0]

    def body(*refs):
        s0_ref = refs[0]
        in_refs = refs[1:1 + n_in]
        do_ref = refs[1 + n_in]
        g_refs = refs[2 + n_in:2 + 2 * n_in]
        dst = refs[2 + 2 * n_in]

        @pl.when(pl.program_id(1) == 0)
        def _():
            dst[...] = jnp.zeros_like(dst)

        vals = [jnp.stack(col) for col in zip(*[_group_vals(in_refs, g) for g in range(gps)])]
        d_o = jnp.stack([do_ref[:, g * LANES:(g + 1) * LANES] for g in range(gps)])
        _, vjp = jax.vjp(jax.vmap(chunk_fn), s0_ref[...], *vals)
        gs = vjp((d_o, dst[...]))
        dst[...] = gs[0]
        for g_ref, gv in zip(g_refs, gs[1:]):
            if len(g_ref.shape) == 2:
                for g in range(gps):
                    g_ref[:, g * LANES:(g + 1) * LANES] = gv[g]
            else:
                g_ref[...] = gv

    specs, cidx = _scan_specs(ins, col_offs, n_chunks, True)
    out_lane = pl.BlockSpec((CHUNK, gps * LANES), lambda h, c: (cidx(c), h))
    g_specs = [out_lane if a.ndim == 2 else sp for a, sp in zip(ins, specs)]
    g_shapes = [(t, n_groups * LANES) if a.ndim == 2 else a.shape for a in ins]
    s0_spec = pl.BlockSpec((gps, None) + state_shape, lambda h, c: (h, cidx(c), 0, 0))
    return pl.pallas_call(
        body, name=name, grid=(n_groups // gps, n_chunks), in_specs=[s0_spec] + specs + [out_lane],
        out_specs=g_specs, out_shape=[jax.ShapeDtypeStruct(sh, f32) for sh in g_shapes],
        scratch_shapes=[pltpu.VMEM((gps,) + state_shape, f32)],
        compiler_params=pltpu.CompilerParams(dimension_semantics=("parallel", "arbitrary")),
    )(s0s, *ins, d_out)


def _rms_fwd(x, g, name):
    t = x.shape[0]
    tm = _tile(t, 416, 16)
    return _tw_fwd(_f_rms, [x, g], [_row_spec(tm, D), _full_spec(g.shape)],
                   [jax.ShapeDtypeStruct(x.shape, MXU_DTYPE)], [_row_spec(tm, D)], (t // tm,), name)[0]


def _rms_bwd(x, g, dy, residual, name):
    t = x.shape[0]
    tm = _tile(t, 416, 8)
    return _tw_bwd(_f_rms, [x, g], [_row_spec(tm, D), _full_spec(g.shape)], [dy], [_row_spec(tm, D)],
                   ['tile', 'acc'], (t // tm,), name, residual=residual)


def _ffn_fwd(h, gain, wgu, wd, tag):
    xn = _rms_fwd(h, gain, f"{tag}_rms")
    gate, up, act = _gate_up_act(xn, wgu, f"{tag}_gate_up")
    out = _matmul(act, wd, res=h, scale=0.5, name=f"{tag}_down")
    return out, (xn, gate, up, act)


def _mxu_dot(a, b, dn):
    return lax.dot_general(a.astype(MXU_DTYPE), b.astype(MXU_DTYPE), dn, preferred_element_type=f32,
                           precision=None if MXU_DTYPE == bf16 else HI)


def _gate_up_act(xn, wgu, name):
    t = xn.shape[0]
    wdt = wgu.shape[2]
    tm = _tile(t, 416, 16)
    dn = (((1,), (0,)), ((), ()))

    def body(x_ref, wg_ref, wu_ref, g_ref, u_ref, a_ref):
        x = x_ref[...]
        g = _mxu_dot(x, wg_ref[...], dn)
        u = _mxu_dot(x, wu_ref[...], dn)
        g_ref[...] = g
        u_ref[...] = u
        a_ref[...] = _f_swiglu(g, u)[0].astype(a_ref.dtype)

    out_spec = pl.BlockSpec((tm, wdt), lambda i, j: (i, j))
    return pl.pallas_call(
        body, name=name, grid=(t // tm, 2),
        in_specs=[pl.BlockSpec((tm, D), lambda i, j: (i, 0)), pl.BlockSpec((None, D, wdt), lambda i, j: (j, 0, 0)),
                  pl.BlockSpec((None, D, wdt), lambda i, j: (j + 2, 0, 0))],
        out_specs=[out_spec] * 3,
        out_shape=[jax.ShapeDtypeStruct((t, 2 * wdt), f32)] * 2 + [jax.ShapeDtypeStruct((t, 2 * wdt), MXU_DTYPE)],
        compiler_params=pltpu.CompilerParams(dimension_semantics=("parallel", "parallel")),
    )(xn, wgu, wgu)


def _d_gate_up(dout, wd, gate, up, name):
    t = dout.shape[0]
    wdt = D_FF // 2
    tm = _tile(t, 416, 16)
    dn = (((1,), (1,)), ((), ()))

    def body(do_ref, wd_ref, g_ref, u_ref, dg_ref, du_ref):
        d_act = 0.5 * _mxu_dot(do_ref[...], wd_ref[...], dn)
        _, vjp = jax.vjp(_f_swiglu, g_ref[...], u_ref[...])
        dg, du = vjp((d_act,))
        dg_ref[...] = dg.astype(dg_ref.dtype)
        du_ref[...] = du.astype(du_ref.dtype)

    spec = pl.BlockSpec((tm, wdt), lambda i, j: (i, j))
    return pl.pallas_call(
        body, name=name, grid=(t // tm, 2),
        in_specs=[pl.BlockSpec((tm, D), lambda i, j: (i, 0)), pl.BlockSpec((wdt, D), lambda i, j: (j, 0)), spec, spec],
        out_specs=[spec] * 2, out_shape=[jax.ShapeDtypeStruct((t, D_FF), MXU_DTYPE)] * 2,
        compiler_params=pltpu.CompilerParams(dimension_semantics=("parallel", "parallel")),
    )(dout, wd, gate, up)


def _ffn_bwd(h, gain, wgu, wd, saved, dout, tag):
    xn, gate, up, act = saved
    t = h.shape[0]
    d_wd = _matmul(act, dout, ta=True, scale=0.5, name=f"{tag}_dwd")
    d_gate, d_up = _d_gate_up(dout, wd, gate, up, f"{tag}_dact")
    d_wg = _matmul(xn, d_gate, ta=True, out_cols_split=True, name=f"{tag}_dwg")
    d_wu = _matmul(xn, d_up, ta=True, out_cols_split=True, name=f"{tag}_dwu")
    d_xn = _matmul(d_gate, wgu, tb=True, b_cols_split=(0, 2), name=f"{tag}_dxn_g")
    d_xn = _matmul(d_up, wgu, tb=True, b_cols_split=(2, 2), res=d_xn, name=f"{tag}_dxn_u")
    d_h, d_gain = _rms_bwd(h, gain, d_xn, dout, f"{tag}_drms")
    return d_h, d_gain, d_wg, d_wu, d_wd


def _col_spec(t, first_block):
    return pl.BlockSpec((t, LANES), lambda j, fb=first_block: (0, j + fb))


def _local_step(h0, tgt, w):
    t = h0.shape[0]
    assert t % CHUNK == 0
    nc = t // CHUNK
    grads = {}

    h1, ffn1_saved = _ffn_fwd(h0, w['ffn1_norm'], w['ffn1_wgu'], w['ffn1_wd'], "ffn1")
    u = _rms_fwd(h1, w['mix_norm'], "mix_rms")
    z = _matmul(u, w['w_in_p'], name="in_proj")
    zs = z[:, 9216:9216 + 304]
    abeta, aalpha = zs[:, 288:296], zs[:, 296:304]

    conv_w = w['a_conv_w']
    conv_fns = [functools.partial(_f_conv, norm=True, scale=A_DK ** -0.5),
                functools.partial(_f_conv, norm=True, scale=1.0),
                functools.partial(_f_conv, norm=False, scale=1.0)]
    qkv = []
    for idx, fn in enumerate(conv_fns):
        qkv.append(_tw_fwd(fn, [z, conv_w], [_col_spec(t, 8 * idx), pl.BlockSpec((4, LANES), lambda j, o=8 * idx: (0, j + o))],
                           [jax.ShapeDtypeStruct((t, D), f32)], [_col_spec(t, 0)], (A_HEADS,), f"a_conv{idx}")[0])
    aq, ak, av = qkv
    tmg = _tile(t, 1040, 8)
    dg_fn = functools.partial(_f_dgates, tm=tmg)
    dg_specs = [_row_spec(tmg, A_HEADS)] * 2 + [_full_spec((1, A_HEADS))] * 2
    beta, gdec = _tw_fwd(dg_fn, [abeta, aalpha, w['a_log_rate'], w['a_dt_bias']], dg_specs,
                         [jax.ShapeDtypeStruct((t, A_HEADS), f32)] * 2, [_row_spec(tmg, A_HEADS)] * 2, (t // tmg,),
                         "a_gates", with_pid=True)
    beta_h = beta.T.reshape(A_HEADS, nc, 1, CHUNK)
    gdec_h = gdec.T.reshape(A_HEADS, nc, 1, CHUNK)
    a_ins = [aq, ak, av, beta_h, gdec_h]
    a_offs = [0] * 5
    o_scan, a_s0 = _scan_fwd(_delta_chunk, a_ins, a_offs, A_HEADS, nc, (A_DK, A_DK), "a_scan")

    mu = w['b_shift_mu']
    mu_rkv, mu_s = mu[:, :3072], mu[:, 3072:]
    zf_rkv = _tw_fwd(_f_tshift, [z, mu_rkv], [_col_spec(t, 32), pl.BlockSpec((1, LANES), lambda j: (0, j))],
                     [jax.ShapeDtypeStruct((t, 3072), f32)], [_col_spec(t, 0)], (24,), "b_shift")[0]
    zs_b = zs[:, :288]
    zf_s = _tw_fwd(_f_tshift, [zs_b, mu_s], [_full_spec((t, 288)), _full_spec((1, 288))],
                   [jax.ShapeDtypeStruct((t, 288), f32)], [_full_spec((t, 288))], (1,), "b_shift_s")[0]
    wdf, adf, gdf = zf_s[:, 0:64], zf_s[:, 64:128], zf_s[:, 128:288]
    tmr = _tile(t, 160, 16)
    pre_params = [w['b_w0'], w['b_w_up'], w['b_a0'], w['b_a_up'], w['b_g_up'], w['b_k_k'], w['b_k_a']]
    pre_ins = [zf_rkv, wdf, adf, gdf] + pre_params
    pre_specs = ([_row_spec(tmr, D, 1), _row_spec(tmr, 64), _row_spec(tmr, 64), _row_spec(tmr, 160)]
                 + [_full_spec(p.shape) for p in pre_params])
    lw, kmod, a_s, b_s, bgate = _tw_fwd(_f_rwkv_pre, pre_ins, pre_specs, [jax.ShapeDtypeStruct((t, D), f32)] * 5,
                                        [_row_spec(tmr, D)] * 5, (t // tmr,), "b_pre")
    b_ins = [zf_rkv, kmod, zf_rkv, a_s, b_s, lw]
    b_offs = [0, 0, 2 * D, 0, 0, 0]
    y_scan, b_s0 = _scan_fwd(_rwkv_chunk, b_ins, b_offs, B_HEADS // 2, nc, (2 * B_N, 2 * B_N), "b_scan")

    out_gain_t = jnp.tile(w['a_out_norm'], (1, A_HEADS))
    r_k = w['b_r_k'].reshape(1, D)
    post_params = [out_gain_t, w['b_ln_gain'], w['b_ln_bias'], r_k]
    post_ins = [o_scan, z, y_scan, zf_rkv, kmod, zf_rkv, bgate, z, z] + post_params
    post_specs = ([_row_spec(tmr, D), _row_spec(tmr, D, 3), _row_spec(tmr, D), _row_spec(tmr, D, 0), _row_spec(tmr, D),
                   _row_spec(tmr, D, 2), _row_spec(tmr, D), _row_spec(tmr, D, 7), _row_spec(tmr, D, 8)]
                  + [_full_spec((1, D))] * 4)
    merged = _tw_fwd(_f_mix_post, post_ins, post_specs, [jax.ShapeDtypeStruct((t, D), MXU_DTYPE)],
                     [_row_spec(tmr, D)], (t // tmr,), "mix_post")[0]
    h2 = _matmul(merged, w['w_out'], res=h1, name="out_proj")
    h3, ffn2_saved = _ffn_fwd(h2, w['ffn2_norm'], w['ffn2_wgu'], w['ffn2_wd'], "ffn2")

    tml = _tile(t, 416, 8)
    fnorm = w['final_norm']
    loss_fn = functools.partial(_f_loss, tm=tml)
    loss_specs = [_row_spec(tml, D), _full_spec((1, D)), _row_spec(tml, D)]
    loss_parts, d_h3, grads['final_norm'] = _loss_and_grad(loss_fn, h3, fnorm, tgt, loss_specs, tml)
    loss = jnp.sum(loss_parts)

    d_h2, grads['ffn2_norm'], grads['ffn2_wg'], grads['ffn2_wu'], grads['ffn2_wd'] = _ffn_bwd(
        h2, w['ffn2_norm'], w['ffn2_wgu'], w['ffn2_wd'], ffn2_saved, d_h3, "ffn2")
    grads['w_out'] = _matmul(merged, d_h2, ta=True, name="d_w_out")
    d_merged = _matmul(d_h2, w['w_out'], tb=True, name="d_merged")

    win = ('tile', (t, D), _row_spec(tmr, D))
    post_kinds = ['tile', win, 'tile', win, 'tile', win, 'tile', win, win] + ['acc'] * 4
    (d_o, d_az, d_y, d_r1, d_kmod1, d_v1, d_bgate, d_ga, d_gb,
     d_out_gain_t, grads['b_ln_gain'], grads['b_ln_bias'], d_r_k) = _tw_bwd(
        _f_mix_post, post_ins, post_specs, [d_merged], [_row_spec(tmr, D)], post_kinds, (t // tmr,), "mix_post_bwd")
    grads['a_out_norm'] = jnp.sum(d_out_gain_t.reshape(A_HEADS, A_DK), axis=0, keepdims=True)
    grads['b_r_k'] = d_r_k.reshape(1, B_HEADS, B_N)

    d_r2, d_kmod2, d_v2, d_as, d_bs, d_lw = _scan_bwd(_rwkv_chunk, b_s0, b_ins, b_offs, d_y, B_HEADS // 2, nc,
                                                      (2 * B_N, 2 * B_N), "b_scan_bwd")
    pre_kinds = [win] + ['tile'] * 3 + ['acc'] * 7
    pre_ct_specs = [_row_spec(tmr, D)] * 5
    (d_zf_k, d_wdf, d_adf, d_gdf, grads['b_w0'], grads['b_w_up'], grads['b_a0'], grads['b_a_up'], grads['b_g_up'],
     grads['b_k_k'], grads['b_k_a']) = _tw_bwd(
        _f_rwkv_pre, pre_ins, pre_specs, [d_lw, d_kmod1, d_as, d_bs, d_bgate], pre_ct_specs, pre_kinds, (t // tmr,),
        "b_pre_bwd", ct_extra=[(1, d_kmod2)])
    d_zb_rkv, d_mu_rkv = _shift_bwd3(z, mu_rkv, d_r1, d_r2, d_zf_k, d_v1, d_v2, t)
    d_zf_s = jnp.concatenate([d_wdf, d_adf, d_gdf], axis=1)
    d_zs_b, d_mu_s = _tw_bwd(_f_tshift, [zs_b, mu_s], [_full_spec((t, 288)), _full_spec((1, 288))], [d_zf_s],
                             [_full_spec((t, 288))], ['tile', 'tile'], (1,), "b_shift_s_bwd")
    grads['b_shift_mu'] = jnp.concatenate([d_mu_rkv, d_mu_s], axis=1)

    d_aq, d_ak, d_av, d_beta_h, d_g_h = _scan_bwd(_delta_chunk, a_s0, a_ins, a_offs, d_o, A_HEADS, nc, (A_DK, A_DK),
                                                  "a_scan_bwd")
    d_beta = d_beta_h.reshape(A_HEADS, t).T
    d_gdec = d_g_h.reshape(A_HEADS, t).T
    d_abeta, d_aalpha, grads['a_log_rate'], grads['a_dt_bias'] = _tw_bwd(
        dg_fn, [abeta, aalpha, w['a_log_rate'], w['a_dt_bias']], dg_specs, [d_beta, d_gdec],
        [_row_spec(tmg, A_HEADS)] * 2, ['tile', 'tile', 'acc', 'acc'], (t // tmg,), "a_gates_bwd", with_pid=True)
    d_zqkv, d_conv = [], []
    for idx, (fn, ct) in enumerate(zip(conv_fns, (d_aq, d_ak, d_av))):
        dz_i, dw_i = _conv_bwd(fn, z, conv_w, ct, idx, t)
        d_zqkv.append(dz_i)
        d_conv.append(dw_i)
    grads['a_conv_w'] = jnp.concatenate(d_conv, axis=1)

    d_z_parts = d_zqkv + [d_az, d_zb_rkv, d_ga, d_gb, d_zs_b, d_abeta, d_aalpha, jnp.zeros((t, ZP - 9216 - 304), f32)]
    d_z = jnp.concatenate([p.astype(MXU_DTYPE) for p in d_z_parts], axis=1)
    grads['w_in_p'] = _matmul(u, d_z, ta=True, name="d_w_in")
    d_u = _matmul(d_z, w['w_in_p'], tb=True, name="d_u")
    d_h1, grads['mix_norm'] = _rms_bwd(h1, w['mix_norm'], d_u, d_h2, "mix_drms")
    d_h0, grads['ffn1_norm'], grads['ffn1_wg'], grads['ffn1_wu'], grads['ffn1_wd'] = _ffn_bwd(
        h0, w['ffn1_norm'], w['ffn1_wgu'], w['ffn1_wd'], ffn1_saved, d_h1, "ffn1")
    return loss, d_h0, grads


_WIN_SEGMENTS = ((0, 4096), (4112, 7184), (7472, 9520), (7184, 7472), (4096, 4112))


def _win_to_padded(w_in):
    parts = [w_in[:, a:b] for a, b in _WIN_SEGMENTS]
    parts.append(jnp.zeros((w_in.shape[0], ZP - IN_TOTAL), w_in.dtype))
    return jnp.concatenate(parts, axis=1)


def _win_from_padded(w_p):
    widths = [b - a for a, b in _WIN_SEGMENTS]
    offs = [sum(widths[:i]) for i in range(len(widths))]
    seg = {a: w_p[:, o:o + wd] for (a, _), o, wd in zip(_WIN_SEGMENTS, offs, widths)}
    return jnp.concatenate([seg[a] for a in sorted(seg)], axis=1)


def _loss_and_grad(loss_fn, h, gain, tgt, specs, tm):
    t = h.shape[0]
    n = t // tm

    def body(h_ref, g_ref, t_ref, l_ref, dh_ref, dg_ref):
        pid = pl.program_id(0)
        tg = t_ref[...]
        (part,), vjp = jax.vjp(lambda a, b: loss_fn(pid, a, b, tg), h_ref[...], g_ref[...])
        dh, dg = vjp((jnp.ones_like(part),))
        l_ref[...] = part
        dh_ref[...] = dh

        @pl.when(pid == 0)
        def _():
            dg_ref[...] = dg

        @pl.when(pid != 0)
        def _():
            dg_ref[...] += dg

    return pl.pallas_call(
        body, name="loss", grid=(n,), in_specs=specs,
        out_specs=[pl.BlockSpec((None, 1, 1), lambda i: (i, 0, 0)), specs[0], _full_spec(gain.shape)],
        out_shape=[jax.ShapeDtypeStruct((n, 1, 1), f32), jax.ShapeDtypeStruct(h.shape, f32),
                   jax.ShapeDtypeStruct(gain.shape, f32)],
    )(h, gain, tgt)


def _shift_bwd3(z, mu, d_r1, d_r2, d_k, d_v1, d_v2, t):
    nb = D // LANES

    def body(z_ref, mu_ref, r1, r2, kk, v1, v2, dz_ref, dmu_ref):
        j = pl.program_id(0)
        ct = jnp.where(j < nb, r1[...] + r2[...], jnp.where(j < 2 * nb, kk[...], v1[...] + v2[...]))
        _, vjp = jax.vjp(lambda a, b: _f_tshift(a, b), z_ref[...], mu_ref[...])
        dz, dmu = vjp((ct,))
        dz_ref[...] = dz
        dmu_ref[...] = dmu

    def window(first):
        return pl.BlockSpec((t, LANES), lambda j, f=first: (0, jnp.clip(j - f * nb, 0, nb - 1)))

    return pl.pallas_call(
        body, name="b_shift_bwd", grid=(3 * nb,),
        in_specs=[_col_spec(t, 32), pl.BlockSpec((1, LANES), lambda j: (0, j)), window(0), window(0), window(1),
                  window(2), window(2)],
        out_specs=[_col_spec(t, 0), pl.BlockSpec((1, LANES), lambda j: (0, j))],
        out_shape=[jax.ShapeDtypeStruct((t, 3 * D), f32), jax.ShapeDtypeStruct((1, 3 * D), f32)],
    )(z, mu, d_r1, d_r2, d_k, d_v1, d_v2)


def _conv_bwd(fn, z, conv_w, ct, idx, t):
    def body(z_ref, w_ref, ct_ref, dz_ref, dw_ref):
        _, vjp = jax.vjp(lambda a, b: fn(a, b), z_ref[...], w_ref[...])
        dz, dw = vjp((ct_ref[...],))
        dz_ref[...] = dz
        dw_ref[...] = dw

    return pl.pallas_call(
        body, name=f"a_conv{idx}_bwd", grid=(A_HEADS,),
        in_specs=[_col_spec(t, 8 * idx), pl.BlockSpec((4, LANES), lambda j, o=8 * idx: (0, j + o)), _col_spec(t, 0)],
        out_specs=[_col_spec(t, 0), pl.BlockSpec((4, LANES), lambda j: (0, j))],
        out_shape=[jax.ShapeDtypeStruct((t, D), f32), jax.ShapeDtypeStruct((4, D), f32)],
    )(z, conv_w, ct)


def _position():
    return lax.axis_index("x"), lax.axis_index("y"), lax.axis_index("c")


def _flip(v, f):
    return 1 - v if f else v


_CHIP_FLIPS = ((1, 0), (0, 1), (1, 1))


def _gather_chips(arrs, name):
    n = len(arrs)
    assert all(a.shape[0] % 32 == 0 for a in arrs)
    arrs = [a.reshape(2, a.shape[0] // 2, a.shape[1]) for a in arrs]

    def body(*refs):
        ins, outs = refs[:n], refs[n:2 * n]
        send, recv, fsend, frecv, own = refs[2 * n:]
        x, y, c = _position()
        me = 2 * x + y
        sends, plan, owns = [], [], []
        for a in range(n):
            cp = pltpu.make_async_remote_copy(src_ref=ins[a], dst_ref=outs[a].at[me], send_sem=own.at[a, 0],
                                              recv_sem=own.at[a, 1], device_id=(x, y, 1 - c), device_id_type=MESH)
            cp.start()
            owns.append(cp)
            for j, (fx, fy) in enumerate(_CHIP_FLIPS):
                px, py = _flip(x, fx), _flip(y, fy)
                p = 2 * px + py
                cp = pltpu.make_async_remote_copy(src_ref=ins[a].at[c], dst_ref=outs[a].at[me, c],
                                                  send_sem=send.at[a, j], recv_sem=recv.at[a, j],
                                                  device_id=(px, py, c), device_id_type=MESH)
                cp.start()
                sends.append(cp)
                landed = pltpu.make_async_remote_copy(src_ref=ins[a].at[c], dst_ref=outs[a].at[p, c],
                                                      send_sem=send.at[a, j], recv_sem=recv.at[a, j],
                                                      device_id=(px, py, c), device_id_type=MESH)
                onward = pltpu.make_async_remote_copy(src_ref=outs[a].at[p, c], dst_ref=outs[a].at[p, c],
                                                      send_sem=fsend.at[a, j], recv_sem=frecv.at[a, j],
                                                      device_id=(x, y, 1 - c), device_id_type=MESH)
                from_sibling = pltpu.make_async_remote_copy(src_ref=outs[a].at[p, 1 - c], dst_ref=outs[a].at[p, 1 - c],
                                                            send_sem=fsend.at[a, j], recv_sem=frecv.at[a, j],
                                                            device_id=(x, y, 1 - c), device_id_type=MESH)
                plan.append((landed, onward, from_sibling))
        for landed, onward, _ in plan:
            landed.wait_recv()
            onward.start()
        for _, _, from_sibling in plan:
            from_sibling.wait_recv()
        for cp in sends:
            cp.wait_send()
        for _, onward, _ in plan:
            onward.wait_send()
        for cp in owns:
            cp.wait()

    sems = [pltpu.SemaphoreType.DMA((n, 3))] * 4 + [pltpu.SemaphoreType.DMA((n, 2))]
    outs = pl.pallas_call(
        body, name=name, in_specs=[ANY] * n, out_specs=[ANY] * n,
        out_shape=[jax.ShapeDtypeStruct((N_CHIPS,) + a.shape, a.dtype) for a in arrs], scratch_shapes=sems,
    )(*arrs)
    return [o.reshape(N_CHIPS, o.shape[1] * o.shape[2], o.shape[3]) for o in outs]


def _swap_sibling(arrs, src_of, shapes, name):
    n = len(arrs)

    def body(*refs):
        a_refs, got_refs = refs[:n], refs[n:2 * n]
        send, recv = refs[2 * n:]
        x, y, c = _position()
        copies = []
        for i in range(n):
            cp = pltpu.make_async_remote_copy(src_ref=src_of(a_refs[i], c), dst_ref=got_refs[i], send_sem=send.at[i],
                                              recv_sem=recv.at[i], device_id=(x, y, 1 - c), device_id_type=MESH)
            cp.start()
            copies.append(cp)
        for cp in copies:
            cp.wait()

    return pl.pallas_call(body, name=name, in_specs=[ANY] * n, out_specs=[ANY] * n,
                          out_shape=[jax.ShapeDtypeStruct(sh, a.dtype) for sh, a in zip(shapes, arrs)],
                          scratch_shapes=[pltpu.SemaphoreType.DMA((n,))] * 2)(*arrs)


def _row_tile(rows, width):
    return _tile(rows, max(16, (784 * LANES // width) // 16 * 16), 16)


def _add_halves(g, got, dtype, name):
    n, _, hr, w = g.shape
    tr = _row_tile(hr, w)

    def body(g_ref, got_ref, o_ref):
        c = lax.axis_index("c")
        own = jnp.where(c == 0, g_ref[:, 0], g_ref[:, 1])
        o_ref[...] = (own + got_ref[...]).astype(dtype)

    return pl.pallas_call(
        body, name=name, grid=(hr // tr,),
        in_specs=[pl.BlockSpec((n, 2, tr, w), lambda i: (0, 0, i, 0)), pl.BlockSpec((n, tr, w), lambda i: (0, i, 0))],
        out_specs=pl.BlockSpec((n, tr, w), lambda i: (0, i, 0)),
        out_shape=jax.ShapeDtypeStruct((n, hr, w), dtype))(g, got)


def _scatter_chips(gs, name):
    n = len(gs)

    def body(*refs):
        g_refs, out_refs = refs[:n], refs[n:2 * n]
        send, recv = refs[2 * n:]
        x, y, c = _position()
        sends = []
        for i in range(n):
            for j, (fx, fy) in enumerate(_CHIP_FLIPS):
                px, py = _flip(x, fx), _flip(y, fy)
                cp = pltpu.make_async_remote_copy(src_ref=g_refs[i].at[2 * px + py], dst_ref=out_refs[i].at[j],
                                                  send_sem=send.at[i, j], recv_sem=recv.at[i, j],
                                                  device_id=(px, py, c), device_id_type=MESH)
                cp.start()
                sends.append(cp)
        for cp in sends:
            cp.wait_recv()
        for cp in sends:
            cp.wait_send()

    return pl.pallas_call(
        body, name=name, in_specs=[ANY] * n, out_specs=[ANY] * n,
        out_shape=[jax.ShapeDtypeStruct((3,) + g.shape[1:], g.dtype) for g in gs],
        scratch_shapes=[pltpu.SemaphoreType.DMA((n, 3)), pltpu.SemaphoreType.DMA((n, 3))],
    )(*gs)


def _sum_own_and_slots(own, got, name):
    n, r, w = own.shape
    tr = _row_tile(r, w)

    def body(own_ref, got_ref, o_ref):
        me = 2 * lax.axis_index("x") + lax.axis_index("y")
        acc = own_ref[0]
        for i in range(1, n):
            acc = jnp.where(me == i, own_ref[i], acc)
        acc = acc.astype(f32)
        for j in range(3):
            acc = acc + got_ref[j].astype(f32)
        o_ref[...] = acc

    return pl.pallas_call(
        body, name=name, grid=(r // tr,),
        in_specs=[pl.BlockSpec((n, tr, w), lambda i: (0, i, 0)), pl.BlockSpec((3, tr, w), lambda i: (0, i, 0))],
        out_specs=pl.BlockSpec((tr, w), lambda i: (i, 0)), out_shape=jax.ShapeDtypeStruct((r, w), f32))(own, got)


def _share_chips(a, name):
    def body(a_ref, out_ref, send, recv):
        x, y, c = _position()
        sends = []
        for j, (fx, fy) in enumerate(_CHIP_FLIPS):
            cp = pltpu.make_async_remote_copy(src_ref=a_ref, dst_ref=out_ref.at[j], send_sem=send.at[j],
                                              recv_sem=recv.at[j], device_id=(_flip(x, fx), _flip(y, fy), c),
                                              device_id_type=MESH)
            cp.start()
            sends.append(cp)
        for cp in sends:
            cp.wait_recv()
        for cp in sends:
            cp.wait_send()

    return pl.pallas_call(
        body, name=name, in_specs=[ANY], out_specs=ANY, out_shape=jax.ShapeDtypeStruct((3,) + a.shape, a.dtype),
        scratch_shapes=[pltpu.SemaphoreType.DMA((3,)), pltpu.SemaphoreType.DMA((3,))],
    )(a)


def _sum_in_chip_order(pair, got, name):
    r, w = pair.shape
    tr = _tile(r, 1408, 8)

    def body(p_ref, g_ref, o_ref):
        x, y = lax.axis_index("x"), lax.axis_index("y")
        me = 2 * x + y
        across = [2 * _flip(x, fx) + _flip(y, fy) for fx, fy in _CHIP_FLIPS]
        acc = None
        for i in range(N_CHIPS):
            term = p_ref[...]
            for j in range(3):
                term = jnp.where(across[j] == i, g_ref[j], term)
            acc = term if acc is None else acc + term
        o_ref[...] = acc

    return pl.pallas_call(
        body, name=name, grid=(r // tr,),
        in_specs=[pl.BlockSpec((tr, w), lambda i: (i, 0)), pl.BlockSpec((3, tr, w), lambda i: (0, i, 0))],
        out_specs=pl.BlockSpec((tr, w), lambda i: (i, 0)), out_shape=jax.ShapeDtypeStruct((r, w), f32))(pair, got)


def _add2(a, b, name):
    r, w = a.shape
    tr = _tile(r, 1408, 8)
    spec = pl.BlockSpec((tr, w), lambda i: (i, 0))

    def body(a_ref, b_ref, o_ref):
        o_ref[...] = a_ref[...] + b_ref[...]

    return pl.pallas_call(body, name=name, grid=(r // tr,), in_specs=[spec, spec], out_specs=spec,
                          out_shape=jax.ShapeDtypeStruct(a.shape, f32))(a, b)


def _adamw(w, g_parts, m, v, name):
    shape = w.shape
    view = shape if len(shape) >= 2 else (1,) + shape
    assert all(d == 1 for d in view[:-2]), shape
    rows, cols = view[-2:]
    cap = max(8, (256 * 1024 // cols) // 8 * 8)
    tr = rows if rows <= cap else _tile(rows, cap, 8)
    lead = len(view) - 2
    n_g = len(g_parts)

    def body(*refs):
        w_ref = refs[0]
        g_refs = refs[1:1 + n_g]
        m_ref, v_ref, g_out, d_out, m_out, v_out = refs[1 + n_g:]
        g = g_refs[0][...]
        for gr in g_refs[1:]:
            g = g + gr[...]
        m_new = ADAM_B1 * m_ref[...] + (1.0 - ADAM_B1) * g
        v_new = ADAM_B2 * v_ref[...] + (1.0 - ADAM_B2) * (g * g)
        m_hat = m_new / (1.0 - ADAM_B1 ** ADAM_STEP)
        v_hat = v_new / (1.0 - ADAM_B2 ** ADAM_STEP)
        g_out[...] = g
        d_out[...] = -ADAM_LR * (m_hat / (jnp.sqrt(v_hat) + ADAM_EPS) + ADAM_WD * w_ref[...])
        m_out[...] = m_new
        v_out[...] = v_new

    spec = pl.BlockSpec((None,) * lead + (tr, cols), lambda i: (0,) * lead + (i, 0))
    args = [w.reshape(view)] + [g.reshape(view) for g in g_parts] + [m.reshape(view), v.reshape(view)]
    outs = pl.pallas_call(body, name=name, grid=(rows // tr,), in_specs=[spec] * len(args), out_specs=[spec] * 4,
                          out_shape=[jax.ShapeDtypeStruct(view, f32)] * 4)(*args)
    return [o.reshape(shape) for o in outs]


_BIG = ('ffn1_w_gu', 'ffn1_w_down', 'w_in', 'w_out', 'ffn2_w_gu', 'ffn2_w_down')
_SMALL_SHARDED = ('meta_tokens', 'a_conv_w', 'b_w_up', 'b_a_up', 'b_g_up')
_WEIGHTS = ('meta_tokens', 'ffn1_norm', 'ffn1_w_gu', 'ffn1_w_down', 'mix_norm', 'w_in', 'a_conv_w', 'a_log_rate',
            'a_dt_bias', 'a_out_norm', 'b_shift_mu', 'b_w0', 'b_w_up', 'b_a0', 'b_a_up', 'b_g_up', 'b_k_k', 'b_k_a',
            'b_r_k', 'b_ln_gain', 'b_ln_bias', 'w_out', 'ffn2_norm', 'ffn2_w_gu', 'ffn2_w_down', 'final_norm')
_SMALL = tuple(n for n in _WEIGHTS if n not in _BIG)


def _rows_of(shape):
    n = 1
    for d in shape:
        n *= d
    return n, -(-n // LANES)


def _pack(arrs, dtype, row_mult=32):
    parts, total = [], 0
    for a in arrs:
        n, rows = _rows_of(a.shape)
        flat = a.reshape(-1).astype(dtype)
        if n % LANES:
            flat = jnp.pad(flat, (0, rows * LANES - n))
        parts.append(flat)
        total += rows
    extra = -total % row_mult
    if extra:
        parts.append(jnp.zeros((extra * LANES,), dtype))
    return jnp.concatenate(parts).reshape(total + extra, LANES)


def _unpack(packed, shapes, lead=()):
    out, off = [], 0
    for sh in shapes:
        n, rows = _rows_of(sh)
        seg = packed[..., off:off + rows, :]
        if n % LANES:
            seg = seg.reshape(lead + (-1,))[..., :n]
        out.append(seg.reshape(lead + tuple(sh)))
        off += rows
    return out


def _cols_from_shards(s):
    return jnp.concatenate([s[i] for i in range(N_CHIPS)], axis=-1)


def _cols_to_shards(a):
    r, c = a.shape
    return a.reshape(r, N_CHIPS, c // N_CHIPS).transpose(1, 0, 2)


def kernel(x, meta_tokens, ffn1_norm, ffn1_w_gu, ffn1_w_down, mix_norm, w_in, a_conv_w, a_log_rate, a_dt_bias, a_out_norm, b_shift_mu, b_w0, b_w_up, b_a0, b_a_up, b_g_up, b_k_k, b_k_a, b_r_k, b_ln_gain, b_ln_bias, w_out, ffn2_norm, ffn2_w_gu, ffn2_w_down, final_norm, loss_target, m_meta_tokens, m_ffn1_norm, m_ffn1_w_gu, m_ffn1_w_down, m_mix_norm, m_w_in, m_a_conv_w, m_a_log_rate, m_a_dt_bias, m_a_out_norm, m_b_shift_mu, m_b_w0, m_b_w_up, m_b_a0, m_b_a_up, m_b_g_up, m_b_k_k, m_b_k_a, m_b_r_k, m_b_ln_gain, m_b_ln_bias, m_w_out, m_ffn2_norm, m_ffn2_w_gu, m_ffn2_w_down, m_final_norm, v_meta_tokens, v_ffn1_norm, v_ffn1_w_gu, v_ffn1_w_down, v_mix_norm, v_w_in, v_a_conv_w, v_a_log_rate, v_a_dt_bias, v_a_out_norm, v_b_shift_mu, v_b_w0, v_b_w_up, v_b_a0, v_b_a_up, v_b_g_up, v_b_k_k, v_b_k_a, v_b_r_k, v_b_ln_gain, v_b_ln_bias, v_w_out, v_ffn2_norm, v_ffn2_w_gu, v_ffn2_w_down, v_final_norm):
    args = locals()
    wts = {n: args[n] for n in _WEIGHTS}
    mom = {n: args["m_" + n] for n in _WEIGHTS}
    var = {n: args["v_" + n] for n in _WEIGHTS}
    chip = 2 * lax.axis_index("x") + lax.axis_index("y")

    big_shapes = [wts[n].shape[1:] for n in _BIG]
    small_shapes = [wts[n].shape[-2:] for n in _SMALL_SHARDED]
    big_flat = [wts[n].astype(bf16).reshape(wts[n].shape[1:]) for n in _BIG]
    small_packed = _pack([wts[n] for n in _SMALL_SHARDED], f32)
    gathered = _gather_chips(big_flat + [small_packed], "gather_weights")
    gu1, dn1, w_in_s, w_out_s, gu2, dn2 = [a.reshape((N_CHIPS,) + tuple(sh)) for a, sh in zip(gathered, big_shapes)]
    meta_s, conv_s, wup_s, aup_s, gup_s = _unpack(gathered[-1], small_shapes, (N_CHIPS,))
    w = {
        'ffn1_norm': ffn1_norm, 'mix_norm': mix_norm, 'ffn2_norm': ffn2_norm, 'final_norm': final_norm[None, :],
        'ffn1_wgu': gu1, 'ffn1_wd': dn1.reshape(D_FF, D), 'ffn2_wgu': gu2, 'ffn2_wd': dn2.reshape(D_FF, D),
        'w_in_p': _win_to_padded(_cols_from_shards(w_in_s)), 'w_out': w_out_s.reshape(D, D),
        'a_conv_w': _cols_from_shards(conv_s), 'b_w_up': _cols_from_shards(wup_s), 'b_a_up': _cols_from_shards(aup_s),
        'b_g_up': _cols_from_shards(gup_s),
        'a_log_rate': a_log_rate, 'a_dt_bias': a_dt_bias, 'a_out_norm': a_out_norm, 'b_shift_mu': b_shift_mu,
        'b_w0': b_w0, 'b_a0': b_a0, 'b_k_k': b_k_k, 'b_k_a': b_k_a, 'b_r_k': b_r_k, 'b_ln_gain': b_ln_gain,
        'b_ln_bias': b_ln_bias,
    }
    meta_full = _cols_from_shards(meta_s)

    h0 = jnp.concatenate([jnp.zeros((PAD, D), f32), meta_full, x[0]], axis=0)
    tgt = jnp.concatenate([jnp.zeros((SKIP, D), f32), loss_target[0]], axis=0)
    loss_local, d_h0, g = _local_step(h0, tgt, w)
    loss = lax.psum(loss_local, ("x", "y", "c"))
    grad_x = d_h0[SKIP:][None]

    big_grads = [
        jnp.concatenate([g['ffn1_wg'], g['ffn1_wu']], axis=0),
        g['ffn1_wd'].reshape(N_CHIPS, D_FF // N_CHIPS, D),
        _cols_to_shards(_win_from_padded(g['w_in_p'])),
        g['w_out'].reshape(N_CHIPS, D // N_CHIPS, D),
        jnp.concatenate([g['ffn2_wg'], g['ffn2_wu']], axis=0),
        g['ffn2_wd'].reshape(N_CHIPS, D_FF // N_CHIPS, D),
    ]
    g_halves = [a.reshape(N_CHIPS, 2, a.shape[1] // 2, a.shape[2]) for a in big_grads]
    sib_halves = _swap_sibling(g_halves, lambda ref, c: ref.at[:, 1 - c], [a.shape[:1] + a.shape[2:] for a in g_halves],
                               "swap_halves")
    chip_halves = [_add_halves(a, b, bf16, f"add_sibling{i}") for i, (a, b) in enumerate(zip(g_halves, sib_halves))]
    got = _scatter_chips(chip_halves, "scatter_grads")
    mine = [_sum_own_and_slots(a, b, f"sum_chips{i}") for i, (a, b) in enumerate(zip(chip_halves, got))]
    theirs = _swap_sibling(mine, lambda ref, c: ref, [a.shape for a in mine], "swap_sums")
    core = lax.axis_index("c")
    big_parts = [jnp.concatenate([jnp.where(core == 0, a, b), jnp.where(core == 0, b, a)], axis=0)
                 for a, b in zip(mine, theirs)]

    small_full = {
        'meta_tokens': d_h0[PAD:SKIP], 'ffn1_norm': g['ffn1_norm'], 'mix_norm': g['mix_norm'], 'a_conv_w': g['a_conv_w'],
        'a_log_rate': g['a_log_rate'], 'a_dt_bias': g['a_dt_bias'], 'a_out_norm': g['a_out_norm'],
        'b_shift_mu': g['b_shift_mu'], 'b_w0': g['b_w0'], 'b_w_up': g['b_w_up'], 'b_a0': g['b_a0'], 'b_a_up': g['b_a_up'],
        'b_g_up': g['b_g_up'], 'b_k_k': g['b_k_k'], 'b_k_a': g['b_k_a'], 'b_r_k': g['b_r_k'], 'b_ln_gain': g['b_ln_gain'],
        'b_ln_bias': g['b_ln_bias'], 'ffn2_norm': g['ffn2_norm'], 'final_norm': g['final_norm'],
    }
    s_shapes = [small_full[n].shape for n in _SMALL]
    s_packed = _pack([small_full[n] for n in _SMALL], f32, row_mult=256)
    (s_sib,) = _swap_sibling([s_packed], lambda ref, c: ref, [s_packed.shape], "swap_small")
    s_pair = _add2(s_packed, s_sib, "add_small")
    s_sum = _sum_in_chip_order(s_pair, _share_chips(s_pair, "share_small"), "sum_small")
    s_parts = dict(zip(_SMALL, _unpack(s_sum, s_shapes)))

    grad, delta, new_m, new_v = {}, {}, {}, {}
    for n, a in zip(_BIG, big_parts):
        grad[n], delta[n], new_m[n], new_v[n] = _adamw(wts[n], [a.reshape(wts[n].shape)], mom[n], var[n], f"adamw_{n}")
    for n in _SMALL:
        gs = s_parts[n]
        if n in _SMALL_SHARDED:
            width = wts[n].shape[-1]
            gs = lax.dynamic_slice_in_dim(gs, chip * width, width, axis=gs.ndim - 1)
        gs = gs.reshape(wts[n].shape)
        grad[n], delta[n], new_m[n], new_v[n] = _adamw(wts[n], [gs], mom[n], var[n], f"adamw_{n}")

    return (loss, grad_x, *[grad[n] for n in _WEIGHTS], *[delta[n] for n in _WEIGHTS],
            *[new_m[n] for n in _WEIGHTS], *[new_v[n] for n in _WEIGHTS])
```

```python
import functools

import jax
import jax.numpy as jnp
from jax import lax
from jax.experimental import pallas as pl
from jax.experimental.pallas import tpu as pltpu

f32 = jnp.float32
bf16 = jnp.bfloat16
HI = lax.Precision.HIGHEST
MESH = pl.DeviceIdType.MESH
ANY = pl.BlockSpec(memory_space=pl.ANY)

D = 1024
N_META = 16
CHUNK = 64
PAD = CHUNK - N_META
SKIP = PAD + N_META
EPS = 1e-6
D_FF = 2816
A_HEADS = 8
A_DK = 128
B_HEADS = 16
B_N = 64
B_GN_EPS = B_N * 1e-5
W_LORA, AA_LORA, G_LORA = 64, 64, 160
IN_TOTAL = 9520
ZP = 9600
LANES = 128
N_CHIPS = 4
N_DEV = 8

ADAM_LR, ADAM_B1, ADAM_B2, ADAM_EPS, ADAM_WD, ADAM_STEP = 0.001, 0.9, 0.999, 1e-08, 0.01, 10

MXU_DTYPE = bf16


def _tile(n, cap, mult):
    if n <= cap:
        return n
    best = None
    for t in range(mult, cap + 1, mult):
        if n % t == 0:
            best = t
    assert best is not None, (n, cap, mult)
    return best


def _sigmoid(x):
    return jax.nn.sigmoid(x)


def _silu(x):
    return x * jax.nn.sigmoid(x)


def _softplus(x):
    return jnp.maximum(x, 0.0) + jnp.log(1.0 + jnp.exp(-jnp.abs(x)))


def _head_matrix(c, nh):
    hd = c // nh
    r = lax.broadcasted_iota(jnp.int32, (c, nh), 0)
    h = lax.broadcasted_iota(jnp.int32, (c, nh), 1)
    return (r >= h * hd) & (r < (h + 1) * hd)


def _dot_exact_rhs(x, e, cb):
    dn = (((1,), (cb,)), ((), ()))
    if SCAN_PASSES == 0:
        return lax.dot_general(x, e.astype(f32), dn, precision=HI, preferred_element_type=f32)
    eb = e.astype(bf16)
    hi = x.astype(bf16)
    lo = (x - hi.astype(f32)).astype(bf16)
    return (lax.dot_general(hi, eb, dn, preferred_element_type=f32)
            + lax.dot_general(lo, eb, dn, preferred_element_type=f32))


def _head_sum_impl(x, nh):
    e = _head_matrix(x.shape[-1], nh)
    return _dot_exact_rhs(_dot_exact_rhs(x, e, 0), e, 1)


@functools.partial(jax.custom_vjp, nondiff_argnums=(1,))
def _head_sum(x, nh):
    return _head_sum_impl(x, nh)


def _head_sum_fwd(x, nh):
    return _head_sum_impl(x, nh), None


def _head_sum_bwd(nh, _, g):
    return (_head_sum_impl(g, nh),)


_head_sum.defvjp(_head_sum_fwd, _head_sum_bwd)


@functools.partial(jax.custom_vjp, nondiff_argnums=(1,))
def _shift_rows(x, s):
    n = x.shape[0]
    row = lax.broadcasted_iota(jnp.int32, x.shape, 0)
    if s > 0:
        return jnp.where(row >= s, pltpu.roll(x, s, 0), 0.0)
    return jnp.where(row < n + s, pltpu.roll(x, n + s, 0), 0.0)


def _shift_rows_fwd(x, s):
    return _shift_rows(x, s), None


def _shift_rows_bwd(s, _, g):
    return (_shift_rows(g, -s),)


_shift_rows.defvjp(_shift_rows_fwd, _shift_rows_bwd)


def _matmul(a, b, *, ta=False, tb=False, res=None, scale=1.0, name, b_cols_split=None, out_cols_split=False):
    assert not (ta and tb)
    (ar, ac) = a.shape
    b0 = 0
    if b_cols_split:
        b0, bs = b_cols_split
        _, br, bc_part = b.shape
        bc = bs * bc_part
    else:
        br, bc = b.shape
    m, k = (ac, ar) if ta else (ar, ac)
    n, kb = (br, bc) if tb else (bc, br)
    assert k == kb, (a.shape, b.shape, ta, tb)
    tm = _tile(m, 1408, LANES) if ta else _tile(m, 832, 8)
    tn = _tile(n, 1408, LANES)
    tk = _tile(k, 1040, 8) if ta else _tile(k, 1408, LANES)
    nk = k // tk
    dn = (((0 if ta else 1,), (1 if tb else 0,)), ((), ()))
    if b_cols_split:
        assert (tk if tb else tn) == bc_part, (b.shape, tn, tk)

    def body(*refs):
        if res is not None:
            a_ref, b_ref, r_ref, o_ref, acc = refs
        else:
            a_ref, b_ref, o_ref, acc = refs
        kk = pl.program_id(2)

        @pl.when(kk == 0)
        def _():
            acc[...] = jnp.zeros_like(acc)

        acc[...] += lax.dot_general(a_ref[...].astype(MXU_DTYPE), b_ref[...].astype(MXU_DTYPE), dn,
                                    preferred_element_type=f32,
                                    precision=None if MXU_DTYPE == bf16 else HI)

        @pl.when(kk == nk - 1)
        def _():
            out = acc[...]
            if scale != 1.0:
                out = out * scale
            if res is not None:
                out = r_ref[...] + out
            o_ref[...] = out

    if ta:
        a_spec = pl.BlockSpec((tk, tm), lambda i, j, kk: (kk, i))
    else:
        a_spec = pl.BlockSpec((tm, tk), lambda i, j, kk: (i, kk))
    if tb and b_cols_split:
        b_spec = pl.BlockSpec((None, tn, tk), lambda i, j, kk: (kk + b0, j, 0))
    elif tb:
        b_spec = pl.BlockSpec((tn, tk), lambda i, j, kk: (j, kk))
    elif b_cols_split:
        b_spec = pl.BlockSpec((None, tk, tn), lambda i, j, kk: (j + b0, kk, 0))
    else:
        b_spec = pl.BlockSpec((tk, tn), lambda i, j, kk: (kk, j))
    in_specs = [a_spec, b_spec]
    args = [a, b]
    if res is not None:
        in_specs.append(pl.BlockSpec((tm, tn), lambda i, j, kk: (i, j)))
        args.append(res)
    if out_cols_split:
        out_spec = pl.BlockSpec((None, tm, tn), lambda i, j, kk: (j, i, 0))
        out_shape = jax.ShapeDtypeStruct((n // tn, m, tn), f32)
    else:
        out_spec = pl.BlockSpec((tm, tn), lambda i, j, kk: (i, j))
        out_shape = jax.ShapeDtypeStruct((m, n), f32)
    return pl.pallas_call(
        body, name=name, grid=(m // tm, n // tn, nk), in_specs=in_specs, out_specs=out_spec, out_shape=out_shape,
        scratch_shapes=[pltpu.VMEM((tm, tn), f32)],
        compiler_params=pltpu.CompilerParams(dimension_semantics=("parallel", "parallel", "arbitrary")),
    )(*args)


def _tw_fwd(fn, ins, in_specs, out_shapes, out_specs, grid, name, with_pid=False):
    n_in = len(ins)

    def body(*refs):
        vals = [r[...] for r in refs[:n_in]]
        outs = fn(pl.program_id(0), *vals) if with_pid else fn(*vals)
        for r, o in zip(refs[n_in:], outs):
            r[...] = o.astype(r.dtype)

    return pl.pallas_call(body, name=name, grid=grid, in_specs=in_specs, out_specs=out_specs,
                          out_shape=out_shapes)(*ins)


def _tw_bwd(fn, ins, in_specs, cts, ct_specs, kinds, grid, name, with_pid=False, tile_dtype=f32, ct_extra=(),
            residual=None):
    n_in, n_ct = len(ins), len(cts)
    diff = [i for i, kd in enumerate(kinds) if kd is not None]
    n_ex = len(ct_extra)

    def body(*refs):
        vals = [r[...] for r in refs[:n_in]]
        ctv = [r[...].astype(f32) for r in refs[n_in:n_in + n_ct]]
        for (ci, _), r in zip(ct_extra, refs[n_in + n_ct:n_in + n_ct + n_ex]):
            ctv[ci] = ctv[ci] + r[...]
        ctv = tuple(ctv)
        n_fixed = n_in + n_ct + n_ex
        res_ref = refs[n_fixed] if residual is not None else None
        g_refs = refs[n_fixed + (residual is not None):]
        pid = pl.program_id(0)

        def f(*dv):
            full = list(vals)
            for i, v in zip(diff, dv):
                full[i] = v
            out = fn(pid, *full) if with_pid else fn(*full)
            return tuple(out)

        _, vjp = jax.vjp(f, *[vals[i] for i in diff])
        gs = vjp(ctv)
        first = pid == 0
        for i2 in range(1, len(grid)):
            first = first & (pl.program_id(i2) == 0)
        for i, g, g_ref in zip(diff, gs, g_refs):
            if kinds[i] != 'acc':
                if i == 0 and res_ref is not None:
                    g = res_ref[...] + g
                g_ref[...] = g.astype(g_ref.dtype)
            else:
                @pl.when(first)
                def _(g=g, g_ref=g_ref):
                    g_ref[...] = g

                @pl.when(jnp.logical_not(first))
                def _(g=g, g_ref=g_ref):
                    g_ref[...] += g

    zero_map = {1: lambda *a: (0,), 2: lambda *a: (0, 0), 3: lambda *a: (0, 0, 0)}
    out_specs, out_shapes = [], []
    for i in diff:
        if kinds[i] == 'tile':
            out_shapes.append(jax.ShapeDtypeStruct(ins[i].shape, tile_dtype))
            out_specs.append(in_specs[i])
        elif kinds[i] == 'acc':
            out_shapes.append(jax.ShapeDtypeStruct(ins[i].shape, f32))
            out_specs.append(pl.BlockSpec(ins[i].shape, zero_map[ins[i].ndim]))
        else:
            out_shapes.append(jax.ShapeDtypeStruct(kinds[i][1], kinds[i][3] if len(kinds[i]) > 3 else tile_dtype))
            out_specs.append(kinds[i][2])
    extra_specs = [ct_specs[ci] for ci, _ in ct_extra]
    extra = [a for _, a in ct_extra]
    if residual is not None:
        assert kinds[0] == 'tile'
        extra_specs.append(in_specs[0])
        extra.append(residual)
    return pl.pallas_call(body, name=name, grid=grid, in_specs=list(in_specs) + list(ct_specs) + extra_specs,
                          out_specs=out_specs, out_shape=out_shapes)(*ins, *cts, *extra)


def _row_spec(tm, c, col_block=0):
    return pl.BlockSpec((tm, c), lambda i, cb=col_block: (i, cb))


def _full_spec(shape):
    nd = len(shape)
    return pl.BlockSpec(shape, lambda *a, nd=nd: (0,) * nd)


def _f_rms(x, g):
    return (x * lax.rsqrt(jnp.mean(x * x, axis=-1, keepdims=True) + EPS) * g,)


def _f_swiglu(gate, up):
    return (_silu(gate) * up,)


def _f_loss(pid, h, g, tgt, *, tm):
    y = h * lax.rsqrt(jnp.mean(h * h, axis=-1, keepdims=True) + EPS) * g
    row = pid * tm + lax.broadcasted_iota(jnp.int32, (tm, 1), 0)
    err = jnp.where(row >= SKIP, y - tgt, 0.0)
    per_row = jnp.mean(err * err, axis=-1, keepdims=True)
    return (0.5 * jnp.sum(per_row, axis=0, keepdims=True),)


def _f_conv(x, w, *, norm, scale):
    y = x * w[3:4, :]
    for s in (1, 2, 3):
        y = y + _shift_rows(x, s) * w[3 - s:4 - s, :]
    y = _silu(y)
    if norm:
        y = y * lax.rsqrt(jnp.sum(y * y, axis=-1, keepdims=True) + 1e-6) * scale
    return (y,)


def _f_dgates(pid, abeta, aalpha, log_rate, dt_bias, *, tm):
    row = pid * tm + lax.broadcasted_iota(jnp.int32, (tm, 1), 0)
    live = row >= PAD
    beta = jnp.where(live, _sigmoid(abeta), 0.0)
    g = jnp.where(live, -jnp.exp(log_rate) * _softplus(aalpha + dt_bias), 0.0)
    return beta, g


def _f_tshift(z, mu):
    return (z + (_shift_rows(z, 1) - z) * mu,)


def _f_rwkv_pre(k, wd, ad, gd, w0, w_up, a0, a_up, g_up, k_k, k_a):
    w_log = -_softplus(-(w0 + _smm(jnp.tanh(wd), w_up, 1))) - 0.5
    lw = -jnp.exp(w_log)
    a_lr = _sigmoid(a0 + _smm(ad, a_up, 1))
    gate = _smm(_sigmoid(gd), g_up, 1)
    kkp = k * k_k
    kk = kkp * lax.rsqrt(_head_sum(kkp * kkp, B_HEADS) + 1e-6)
    kmod = k * (1.0 + (a_lr - 1.0) * k_a)
    return lw, kmod, -kk, kk * a_lr, gate


def _f_mix_post(o, az, y, r, kmod, v, gate, ga, gb, out_gain, ln_g, ln_b, r_k):
    ms = _head_sum(o * o, A_HEADS) * (1.0 / A_DK)
    oa = o * lax.rsqrt(ms + EPS) * out_gain * _silu(az)
    mean = _head_sum(y, B_HEADS) * (1.0 / B_N)
    yc = y - mean
    var = _head_sum(yc * yc, B_HEADS) * (1.0 / B_N)
    yn = yc * lax.rsqrt(var + B_GN_EPS) * ln_g + ln_b
    bonus = _head_sum(r * kmod * r_k, B_HEADS) * v
    ob = (yn + bonus) * gate
    return (_sigmoid(ga) * oa + _sigmoid(gb) * ob,)


SCAN_PASSES = 3


def _split2(a):
    hi = a.astype(bf16)
    return hi, (a - hi.astype(f32)).astype(bf16)


def _dot_passes(a, b, ca, cb, passes):
    dn = (((ca,), (cb,)), ((), ()))
    if SCAN_PASSES == 0:
        return lax.dot_general(a, b, dn, precision=HI, preferred_element_type=f32)
    if passes == 1:
        return lax.dot_general(a.astype(bf16), b.astype(bf16), dn, preferred_element_type=f32)
    ah, al = _split2(a)
    bh, bl = _split2(b)
    return (lax.dot_general(ah, bh, dn, preferred_element_type=f32)
            + (lax.dot_general(ah, bl, dn, preferred_element_type=f32)
               + lax.dot_general(al, bh, dn, preferred_element_type=f32)))


@functools.partial(jax.custom_vjp, nondiff_argnums=(2, 3, 4))
def _sdot(a, b, ca, cb, passes):
    return _dot_passes(a, b, ca, cb, passes)


def _sdot_fwd(a, b, ca, cb, passes):
    return _dot_passes(a, b, ca, cb, passes), (a, b)


def _sdot_bwd(ca, cb, passes, res, g):
    a, b = res
    if (ca, cb) == (1, 0):
        return _dot_passes(g, b, 1, 1, passes), _dot_passes(a, g, 0, 0, passes)
    if (ca, cb) == (1, 1):
        return _dot_passes(g, b, 1, 0, passes), _dot_passes(g, a, 0, 0, passes)
    assert (ca, cb) == (0, 0)
    return _dot_passes(b, g, 1, 1, passes), _dot_passes(a, g, 1, 0, passes)


_sdot.defvjp(_sdot_fwd, _sdot_bwd)


def _smm(a, b, passes=3):
    return _sdot(a, b, 1, 0, passes)


def _smm_nt(a, b, passes=3):
    return _sdot(a, b, 1, 1, passes)


def _smm_tn(a, b, passes=3):
    return _sdot(a, b, 0, 0, passes)


def _tri_dot(x, ca):
    n = x.shape[0]
    incl = _tri_masks(n)[0]
    dn = (((ca,), (0,)), ((), ()))
    if SCAN_PASSES == 0:
        return lax.dot_general(incl.astype(f32), x, dn, precision=HI, preferred_element_type=f32)
    tri = incl.astype(bf16)
    hi, r1 = x.astype(bf16), None
    r1 = x - hi.astype(f32)
    mid = r1.astype(bf16)
    lo = (r1 - mid.astype(f32)).astype(bf16)
    return (lax.dot_general(tri, hi, dn, preferred_element_type=f32)
            + (lax.dot_general(tri, mid, dn, preferred_element_type=f32)
               + lax.dot_general(tri, lo, dn, preferred_element_type=f32)))


@jax.custom_vjp
def _cumsum_rows(x):
    return _tri_dot(x, 1)


def _cumsum_rows_fwd(x):
    return _tri_dot(x, 1), None


def _cumsum_rows_bwd(_, g):
    return (_tri_dot(g, 0),)


_cumsum_rows.defvjp(_cumsum_rows_fwd, _cumsum_rows_bwd)


def _tri_masks(n):
    i = lax.broadcasted_iota(jnp.int32, (n, n), 0)
    j = lax.broadcasted_iota(jnp.int32, (n, n), 1)
    return i >= j, i > j, i == j, i <= j


def _unit_lower_inv_impl(low, passes):
    n = low.shape[0]
    assert n == CHUNK
    _, _, eye, _ = _tri_masks(n)
    acc = eye.astype(f32) + low
    p = low
    for _ in range(5):
        p = _dot_passes(p, p, 1, 0, passes)
        acc = acc + _dot_passes(acc, p, 1, 0, passes)
    return acc


@functools.partial(jax.custom_vjp, nondiff_argnums=(1,))
def _unit_lower_inv(low, passes=3):
    return _unit_lower_inv_impl(low, passes)


def _unit_lower_inv_fwd(low, passes):
    t = _unit_lower_inv_impl(low, passes)
    return t, t


def _unit_lower_inv_bwd(passes, t, g):
    return (_dot_passes(_dot_passes(t, g, 0, 0, passes), t, 1, 1, passes),)


_unit_lower_inv.defvjp(_unit_lower_inv_fwd, _unit_lower_inv_bwd)

DELTA_PASSES = 1
DELTA_INV_PASSES = 1


def _delta_chunk(s, q, k, v, beta_row, g_row):
    p = DELTA_PASSES
    incl, strict, eye, upper = _tri_masks(CHUNK)
    beta = jnp.sum(jnp.where(eye, beta_row, 0.0), axis=1, keepdims=True)
    g = jnp.sum(jnp.where(eye, g_row, 0.0), axis=1, keepdims=True)
    gc = jnp.sum(jnp.where(incl, g_row, 0.0), axis=1, keepdims=True)
    gc_row = jnp.sum(jnp.where(upper, g, 0.0), axis=0, keepdims=True)
    decay = jnp.where(incl, jnp.exp(jnp.where(incl, gc - gc_row, 0.0)), 0.0)
    kb = k * beta
    vb = v * beta
    m = jnp.where(strict, _smm_nt(kb, k, p) * decay, 0.0)
    tinv = _unit_lower_inv(-m, DELTA_INV_PASSES)
    u = _smm(tinv, vb, p)
    wk = _smm(tinv, kb * jnp.exp(gc), p)
    attn = _smm_nt(q, k, p) * decay
    qg = q * jnp.exp(gc)
    g_last = jnp.sum(g, axis=0, keepdims=True)
    k_tail = k * jnp.exp(g_last - gc)
    v_new = u - _smm(wk, s, p)
    o = _smm(qg, s, p) + _smm(attn, v_new, p)
    s_new = s * jnp.exp(g_last) + _smm_tn(k_tail, v_new, p)
    return o, s_new


RWKV_PASSES = 1
RWKV_INV_PASSES = 1


def _rwkv_chunk(st, r, k, v, a, b, lw):
    c = CHUNK
    p, pi = RWKV_PASSES, RWKV_INV_PASSES
    _, strict, _, _ = _tri_masks(c)
    lane = lax.broadcasted_iota(jnp.int32, (c, 2 * B_N), 1)
    row = lax.broadcasted_iota(jnp.int32, (c, 2 * B_N), 0)
    first = lane < B_N
    incl2 = row >= jnp.where(first, lane, lane - B_N)
    bi = lax.broadcasted_iota(jnp.int32, (2 * B_N, 2 * B_N), 0) < B_N
    bj = lax.broadcasted_iota(jnp.int32, (2 * B_N, 2 * B_N), 1) < B_N
    blockdiag = bi == bj
    cum = _cumsum_rows(lw)
    e_pos = jnp.exp(cum)
    e_neg = jnp.exp(-cum)
    rt = r * e_pos
    at = a * jnp.exp(cum - lw)
    kt = k * e_neg
    bt = b * e_neg
    bk = jnp.concatenate([bt, kt], axis=0)
    a_s0 = _smm_nt(at, st, p)
    r_s0 = _smm_nt(rt, st, p)
    heads = (first, jnp.logical_not(first))
    u = jnp.zeros((c, 2 * B_N), f32)
    for sel in heads:
        at_h = jnp.where(sel, at, 0.0)
        ab = jnp.where(strict, _smm_nt(at_h, bt, pi), 0.0)
        ak = jnp.where(strict, _smm_nt(at_h, kt, p), 0.0)
        t_h = _unit_lower_inv(ab, pi)
        u = u + _smm(t_h, jnp.where(sel, a_s0, 0.0) + _smm(ak, jnp.where(sel, v, 0.0), p), p)
    y = r_s0
    for sel in heads:
        rbk = jnp.where(incl2, _smm_nt(jnp.where(sel, rt, 0.0), bk, p), 0.0)
        uv = jnp.concatenate([jnp.where(sel, u, 0.0), jnp.where(sel, v, 0.0)], axis=0)
        y = y + _smm(rbk, uv, p)
    cl = jnp.sum(lw, axis=0, keepdims=True)
    dec = jnp.exp(cl - cum)
    uv_all = jnp.concatenate([u, v], axis=0)
    bk_dec = jnp.concatenate([b * dec, k * dec], axis=0)
    st_new = st * jnp.exp(cl) + jnp.where(blockdiag, _smm_tn(uv_all, bk_dec, p), 0.0)
    return y, st_new


GROUPS_PER_STEP = 8


def _scan_specs(ins, col_offs, n_chunks, reverse):
    gw = GROUPS_PER_STEP * LANES
    cidx = (lambda c: n_chunks - 1 - c) if reverse else (lambda c: c)
    specs = []
    for a, off in zip(ins, col_offs):
        if a.ndim == 2:
            assert off % gw == 0
            specs.append(pl.BlockSpec((CHUNK, gw), lambda h, c, o=off // gw: (cidx(c), h + o)))
        else:
            specs.append(pl.BlockSpec((GROUPS_PER_STEP, None, 1, CHUNK), lambda h, c: (h, cidx(c), 0, 0)))
    return specs, cidx


def _group_vals(refs, g):
    return [r[:, g * LANES:(g + 1) * LANES] if len(r.shape) == 2 else r[g] for r in refs]


def _scan_fwd(chunk_fn, ins, col_offs, n_groups, n_chunks, state_shape, name):
    n_in = len(ins)
    gps = GROUPS_PER_STEP
    t = ins[0].shape[0]

    def body(*refs):
        in_refs = refs[:n_in]
        o_ref, s0_ref, st = refs[n_in:]

        @pl.when(pl.program_id(1) == 0)
        def _():
            st[...] = jnp.zeros_like(st)

        states = st[...]
        vals = [jnp.stack(col) for col in zip(*[_group_vals(in_refs, g) for g in range(gps)])]
        o, s_new = jax.vmap(chunk_fn)(states, *vals)
        s0_ref[...] = states
        st[...] = s_new
        for g in range(gps):
            o_ref[:, g * LANES:(g + 1) * LANES] = o[g]

    specs, _ = _scan_specs(ins, col_offs, n_chunks, False)
    return pl.pallas_call(
        body, name=name, grid=(n_groups // gps, n_chunks), in_specs=specs,
        out_specs=[pl.BlockSpec((CHUNK, gps * LANES), lambda h, c: (c, h)),
                   pl.BlockSpec((gps, None) + state_shape, lambda h, c: (h, c, 0, 0))],
        out_shape=[jax.ShapeDtypeStruct((t, n_groups * LANES), f32),
                   jax.ShapeDtypeStruct((n_groups, n_chunks) + state_shape, f32)],
        scratch_shapes=[pltpu.VMEM((gps,) + state_shape, f32)],
        compiler_params=pltpu.CompilerParams(dimension_semantics=("parallel", "arbitrary")),
    )(*ins)


def _scan_bwd(chunk_fn, s0s, ins, col_offs, d_out, n_groups, n_chunks, state_shape, name):
    n_in = len(ins)
    gps = GROUPS_PER_STEP
    t = d_out.shape[0]

    def body(*refs):
        s0_ref = refs[0]
        in_refs = refs[1:1 + n_in]
        do_ref = refs[1 + n_in]
        g_refs = refs[2 + n_in:2 + 2 * n_in]
        dst = refs[2 + 2 * n_in]

        @pl.when(pl.program_id(1) == 0)
        def _():
            dst[...] = jnp.zeros_like(dst)

        vals = [jnp.stack(col) for col in zip(*[_group_vals(in_refs, g) for g in range(gps)])]
        d_o = jnp.stack([do_ref[:, g * LANES:(g + 1) * LANES] for g in range(gps)])
        _, vjp = jax.vjp(jax.vmap(chunk_fn), s0_ref[...], *vals)
        gs = vjp((d_o, dst[...]))
        dst[...] = gs[0]
        for g_ref, gv in zip(g_refs, gs[1:]):
            if len(g_ref.shape) == 2:
                for g in range(gps):
                    g_ref[:, g * LANES:(g + 1) * LANES] = gv[g]
            else:
                g_ref[...] = gv

    specs, cidx = _scan_specs(ins, col_offs, n_chunks, True)
    out_lane = pl.BlockSpec((CHUNK, gps * LANES), lambda h, c: (cidx(c), h))
    g_specs = [out_lane if a.ndim == 2 else sp for a, sp in zip(ins, specs)]
    g_shapes = [(t, n_groups * LANES) if a.ndim == 2 else a.shape for a in ins]
    s0_spec = pl.BlockSpec((gps, None) + state_shape, lambda h, c: (h, cidx(c), 0, 0))
    return pl.pallas_call(
        body, name=name, grid=(n_groups // gps, n_chunks), in_specs=[s0_spec] + specs + [out_lane],
        out_specs=g_specs, out_shape=[jax.ShapeDtypeStruct(sh, f32) for sh in g_shapes],
        scratch_shapes=[pltpu.VMEM((gps,) + state_shape, f32)],
        compiler_params=pltpu.CompilerParams(dimension_semantics=("parallel", "arbitrary")),
    )(s0s, *ins, d_out)


def _rms_fwd(x, g, name):
    t = x.shape[0]
    tm = _tile(t, 416, 16)
    return _tw_fwd(_f_rms, [x, g], [_row_spec(tm, D), _full_spec(g.shape)],
                   [jax.ShapeDtypeStruct(x.shape, MXU_DTYPE)], [_row_spec(tm, D)], (t // tm,), name)[0]


def _rms_bwd(x, g, dy, residual, name):
    t = x.shape[0]
    tm = _tile(t, 416, 8)
    return _tw_bwd(_f_rms, [x, g], [_row_spec(tm, D), _full_spec(g.shape)], [dy], [_row_spec(tm, D)],
                   ['tile', 'acc'], (t // tm,), name, residual=residual)


def _ffn_fwd(h, gain, wgu, wd, tag):
    xn = _rms_fwd(h, gain, f"{tag}_rms")
    gate, up, act = _gate_up_act(xn, wgu, f"{tag}_gate_up")
    out = _matmul(act, wd, res=h, scale=0.5, name=f"{tag}_down")
    return out, (xn, gate, up, act)


def _mxu_dot(a, b, dn):
    return lax.dot_general(a.astype(MXU_DTYPE), b.astype(MXU_DTYPE), dn, preferred_element_type=f32,
                           precision=None if MXU_DTYPE == bf16 else HI)


def _gate_up_act(xn, wgu, name):
    t = xn.shape[0]
    wdt = wgu.shape[2]
    tm = _tile(t, 416, 16)
    dn = (((1,), (0,)), ((), ()))

    def body(x_ref, wg_ref, wu_ref, g_ref, u_ref, a_ref):
        x = x_ref[...]
        g = _mxu_dot(x, wg_ref[...], dn)
        u = _mxu_dot(x, wu_ref[...], dn)
        g_ref[...] = g
        u_ref[...] = u
        a_ref[...] = _f_swiglu(g, u)[0].astype(a_ref.dtype)

    out_spec = pl.BlockSpec((tm, wdt), lambda i, j: (i, j))
    return pl.pallas_call(
        body, name=name, grid=(t // tm, 2),
        in_specs=[pl.BlockSpec((tm, D), lambda i, j: (i, 0)), pl.BlockSpec((None, D, wdt), lambda i, j: (j, 0, 0)),
                  pl.BlockSpec((None, D, wdt), lambda i, j: (j + 2, 0, 0))],
        out_specs=[out_spec] * 3,
        out_shape=[jax.ShapeDtypeStruct((t, 2 * wdt), f32)] * 2 + [jax.ShapeDtypeStruct((t, 2 * wdt), MXU_DTYPE)],
        compiler_params=pltpu.CompilerParams(dimension_semantics=("parallel", "parallel")),
    )(xn, wgu, wgu)


def _d_gate_up(dout, wd, gate, up, name):
    t = dout.shape[0]
    wdt = D_FF // 2
    tm = _tile(t, 416, 16)
    dn = (((1,), (1,)), ((), ()))

    def body(do_ref, wd_ref, g_ref, u_ref, dg_ref, du_ref):
        d_act = 0.5 * _mxu_dot(do_ref[...], wd_ref[...], dn)
        _, vjp = jax.vjp(_f_swiglu, g_ref[...], u_ref[...])
        dg, du = vjp((d_act,))
        dg_ref[...] = dg.astype(dg_ref.dtype)
        du_ref[...] = du.astype(du_ref.dtype)

    spec = pl.BlockSpec((tm, wdt), lambda i, j: (i, j))
    return pl.pallas_call(
        body, name=name, grid=(t // tm, 2),
        in_specs=[pl.BlockSpec((tm, D), lambda i, j: (i, 0)), pl.BlockSpec((wdt, D), lambda i, j: (j, 0)), spec, spec],
        out_specs=[spec] * 2, out_shape=[jax.ShapeDtypeStruct((t, D_FF), MXU_DTYPE)] * 2,
        compiler_params=pltpu.CompilerParams(dimension_semantics=("parallel", "parallel")),
    )(dout, wd, gate, up)


def _ffn_bwd(h, gain, wgu, wd, saved, dout, tag):
    xn, gate, up, act = saved
    t = h.shape[0]
    d_wd = _matmul(act, dout, ta=True, scale=0.5, name=f"{tag}_dwd")
    d_gate, d_up = _d_gate_up(dout, wd, gate, up, f"{tag}_dact")
    d_wg = _matmul(xn, d_gate, ta=True, out_cols_split=True, name=f"{tag}_dwg")
    d_wu = _matmul(xn, d_up, ta=True, out_cols_split=True, name=f"{tag}_dwu")
    d_xn = _matmul(d_gate, wgu, tb=True, b_cols_split=(0, 2), name=f"{tag}_dxn_g")
    d_xn = _matmul(d_up, wgu, tb=True, b_cols_split=(2, 2), res=d_xn, name=f"{tag}_dxn_u")
    d_h, d_gain = _rms_bwd(h, gain, d_xn, dout, f"{tag}_drms")
    return d_h, d_gain, d_wg, d_wu, d_wd


def _col_spec(t, first_block):
    return pl.BlockSpec((t, LANES), lambda j, fb=first_block: (0, j + fb))


def _local_step(h0, tgt, w):
    t = h0.shape[0]
    assert t % CHUNK == 0
    nc = t // CHUNK
    grads = {}

    h1, ffn1_saved = _ffn_fwd(h0, w['ffn1_norm'], w['ffn1_wgu'], w['ffn1_wd'], "ffn1")
    u = _rms_fwd(h1, w['mix_norm'], "mix_rms")
    z = _matmul(u, w['w_in_p'], name="in_proj")
    zs = z[:, 9216:9216 + 304]
    abeta, aalpha = zs[:, 288:296], zs[:, 296:304]

    conv_w = w['a_conv_w']
    conv_fns = [functools.partial(_f_conv, norm=True, scale=A_DK ** -0.5),
                functools.partial(_f_conv, norm=True, scale=1.0),
                functools.partial(_f_conv, norm=False, scale=1.0)]
    qkv = []
    for idx, fn in enumerate(conv_fns):
        qkv.append(_tw_fwd(fn, [z, conv_w], [_col_spec(t, 8 * idx), pl.BlockSpec((4, LANES), lambda j, o=8 * idx: (0, j + o))],
                           [jax.ShapeDtypeStruct((t, D), f32)], [_col_spec(t, 0)], (A_HEADS,), f"a_conv{idx}")[0])
    aq, ak, av = qkv
    tmg = _tile(t, 1040, 8)
    dg_fn = functools.partial(_f_dgates, tm=tmg)
    dg_specs = [_row_spec(tmg, A_HEADS)] * 2 + [_full_spec((1, A_HEADS))] * 2
    beta, gdec = _tw_fwd(dg_fn, [abeta, aalpha, w['a_log_rate'], w['a_dt_bias']], dg_specs,
                         [jax.ShapeDtypeStruct((t, A_HEADS), f32)] * 2, [_row_spec(tmg, A_HEADS)] * 2, (t // tmg,),
                         "a_gates", with_pid=True)
    beta_h = beta.T.reshape(A_HEADS, nc, 1, CHUNK)
    gdec_h = gdec.T.reshape(A_HEADS, nc, 1, CHUNK)
    a_ins = [aq, ak, av, beta_h, gdec_h]
    a_offs = [0] * 5
    o_scan, a_s0 = _scan_fwd(_delta_chunk, a_ins, a_offs, A_HEADS, nc, (A_DK, A_DK), "a_scan")

    mu = w['b_shift_mu']
    mu_rkv, mu_s = mu[:, :3072], mu[:, 3072:]
    zf_rkv = _tw_fwd(_f_tshift, [z, mu_rkv], [_col_spec(t, 32), pl.BlockSpec((1, LANES), lambda j: (0, j))],
                     [jax.ShapeDtypeStruct((t, 3072), f32)], [_col_spec(t, 0)], (24,), "b_shift")[0]
    zs_b = zs[:, :288]
    zf_s = _tw_fwd(_f_tshift, [zs_b, mu_s], [_full_spec((t, 288)), _full_spec((1, 288))],
                   [jax.ShapeDtypeStruct((t, 288), f32)], [_full_spec((t, 288))], (1,), "b_shift_s")[0]
    wdf, adf, gdf = zf_s[:, 0:64], zf_s[:, 64:128], zf_s[:, 128:288]
    tmr = _tile(t, 160, 16)
    pre_params = [w['b_w0'], w['b_w_up'], w['b_a0'], w['b_a_up'], w['b_g_up'], w['b_k_k'], w['b_k_a']]
    pre_ins = [zf_rkv, wdf, adf, gdf] + pre_params
    pre_specs = ([_row_spec(tmr, D, 1), _row_spec(tmr, 64), _row_spec(tmr, 64), _row_spec(tmr, 160)]
                 + [_full_spec(p.shape) for p in pre_params])
    lw, kmod, a_s, b_s, bgate = _tw_fwd(_f_rwkv_pre, pre_ins, pre_specs, [jax.ShapeDtypeStruct((t, D), f32)] * 5,
                                        [_row_spec(tmr, D)] * 5, (t // tmr,), "b_pre")
    b_ins = [zf_rkv, kmod, zf_rkv, a_s, b_s, lw]
    b_offs = [0, 0, 2 * D, 0, 0, 0]
    y_scan, b_s0 = _scan_fwd(_rwkv_chunk, b_ins, b_offs, B_HEADS // 2, nc, (2 * B_N, 2 * B_N), "b_scan")

    out_gain_t = jnp.tile(w['a_out_norm'], (1, A_HEADS))
    r_k = w['b_r_k'].reshape(1, D)
    post_params = [out_gain_t, w['b_ln_gain'], w['b_ln_bias'], r_k]
    post_ins = [o_scan, z, y_scan, zf_rkv, kmod, zf_rkv, bgate, z, z] + post_params
    post_specs = ([_row_spec(tmr, D), _row_spec(tmr, D, 3), _row_spec(tmr, D), _row_spec(tmr, D, 0), _row_spec(tmr, D),
                   _row_spec(tmr, D, 2), _row_spec(tmr, D), _row_spec(tmr, D, 7), _row_spec(tmr, D, 8)]
                  + [_full_spec((1, D))] * 4)
    merged = _tw_fwd(_f_mix_post, post_ins, post_specs, [jax.ShapeDtypeStruct((t, D), MXU_DTYPE)],
                     [_row_spec(tmr, D)], (t // tmr,), "mix_post")[0]
    h2 = _matmul(merged, w['w_out'], res=h1, name="out_proj")
    h3, ffn2_saved = _ffn_fwd(h2, w['ffn2_norm'], w['ffn2_wgu'], w['ffn2_wd'], "ffn2")

    tml = _tile(t, 416, 8)
    fnorm = w['final_norm']
    loss_fn = functools.partial(_f_loss, tm=tml)
    loss_specs = [_row_spec(tml, D), _full_spec((1, D)), _row_spec(tml, D)]
    loss_parts, d_h3, grads['final_norm'] = _loss_and_grad(loss_fn, h3, fnorm, tgt, loss_specs, tml)
    loss = jnp.sum(loss_parts)

    d_h2, grads['ffn2_norm'], grads['ffn2_wg'], grads['ffn2_wu'], grads['ffn2_wd'] = _ffn_bwd(
        h2, w['ffn2_norm'], w['ffn2_wgu'], w['ffn2_wd'], ffn2_saved, d_h3, "ffn2")
    grads['w_out'] = _matmul(merged, d_h2, ta=True, name="d_w_out")
    d_merged = _matmul(d_h2, w['w_out'], tb=True, name="d_merged")

    win = ('tile', (t, D), _row_spec(tmr, D))
    zwin = win + (MXU_DTYPE,)
    post_kinds = ['tile', zwin, 'tile', win, 'tile', win, 'tile', zwin, zwin] + ['acc'] * 4
    (d_o, d_az, d_y, d_r1, d_kmod1, d_v1, d_bgate, d_ga, d_gb,
     d_out_gain_t, grads['b_ln_gain'], grads['b_ln_bias'], d_r_k) = _tw_bwd(
        _f_mix_post, post_ins, post_specs, [d_merged], [_row_spec(tmr, D)], post_kinds, (t // tmr,), "mix_post_bwd")
    grads['a_out_norm'] = jnp.sum(d_out_gain_t.reshape(A_HEADS, A_DK), axis=0, keepdims=True)
    grads['b_r_k'] = d_r_k.reshape(1, B_HEADS, B_N)

    d_r2, d_kmod2, d_v2, d_as, d_bs, d_lw = _scan_bwd(_rwkv_chunk, b_s0, b_ins, b_offs, d_y, B_HEADS // 2, nc,
                                                      (2 * B_N, 2 * B_N), "b_scan_bwd")
    pre_kinds = [win] + ['tile'] * 3 + ['acc'] * 7
    pre_ct_specs = [_row_spec(tmr, D)] * 5
    (d_zf_k, d_wdf, d_adf, d_gdf, grads['b_w0'], grads['b_w_up'], grads['b_a0'], grads['b_a_up'], grads['b_g_up'],
     grads['b_k_k'], grads['b_k_a']) = _tw_bwd(
        _f_rwkv_pre, pre_ins, pre_specs, [d_lw, d_kmod1, d_as, d_bs, d_bgate], pre_ct_specs, pre_kinds, (t // tmr,),
        "b_pre_bwd", ct_extra=[(1, d_kmod2)])
    d_zb_rkv, d_mu_rkv = _shift_bwd3(z, mu_rkv, d_r1, d_r2, d_zf_k, d_v1, d_v2, t)
    d_zf_s = jnp.concatenate([d_wdf, d_adf, d_gdf], axis=1)
    d_zs_b, d_mu_s = _tw_bwd(_f_tshift, [zs_b, mu_s], [_full_spec((t, 288)), _full_spec((1, 288))], [d_zf_s],
                             [_full_spec((t, 288))], ['tile', 'tile'], (1,), "b_shift_s_bwd")
    grads['b_shift_mu'] = jnp.concatenate([d_mu_rkv, d_mu_s], axis=1)

    d_aq, d_ak, d_av, d_beta_h, d_g_h = _scan_bwd(_delta_chunk, a_s0, a_ins, a_offs, d_o, A_HEADS, nc, (A_DK, A_DK),
                                                  "a_scan_bwd")
    d_beta = d_beta_h.reshape(A_HEADS, t).T
    d_gdec = d_g_h.reshape(A_HEADS, t).T
    d_abeta, d_aalpha, grads['a_log_rate'], grads['a_dt_bias'] = _tw_bwd(
        dg_fn, [abeta, aalpha, w['a_log_rate'], w['a_dt_bias']], dg_specs, [d_beta, d_gdec],
        [_row_spec(tmg, A_HEADS)] * 2, ['tile', 'tile', 'acc', 'acc'], (t // tmg,), "a_gates_bwd", with_pid=True)
    d_zqkv, d_conv = [], []
    for idx, (fn, ct) in enumerate(zip(conv_fns, (d_aq, d_ak, d_av))):
        dz_i, dw_i = _conv_bwd(fn, z, conv_w, ct, idx, t)
        d_zqkv.append(dz_i)
        d_conv.append(dw_i)
    grads['a_conv_w'] = jnp.concatenate(d_conv, axis=1)

    d_z_parts = d_zqkv + [d_az, d_zb_rkv, d_ga, d_gb, d_zs_b, d_abeta, d_aalpha, jnp.zeros((t, ZP - 9216 - 304), f32)]
    d_z = jnp.concatenate([p.astype(MXU_DTYPE) for p in d_z_parts], axis=1)
    grads['w_in_p'] = _matmul(u, d_z, ta=True, name="d_w_in")
    d_u = _matmul(d_z, w['w_in_p'], tb=True, name="d_u")
    d_h1, grads['mix_norm'] = _rms_bwd(h1, w['mix_norm'], d_u, d_h2, "mix_drms")
    d_h0, grads['ffn1_norm'], grads['ffn1_wg'], grads['ffn1_wu'], grads['ffn1_wd'] = _ffn_bwd(
        h0, w['ffn1_norm'], w['ffn1_wgu'], w['ffn1_wd'], ffn1_saved, d_h1, "ffn1")
    return loss, d_h0, grads


_WIN_SEGMENTS = ((0, 4096), (4112, 7184), (7472, 9520), (7184, 7472), (4096, 4112))


def _win_to_padded(w_in):
    parts = [w_in[:, a:b] for a, b in _WIN_SEGMENTS]
    parts.append(jnp.zeros((w_in.shape[0], ZP - IN_TOTAL), w_in.dtype))
    return jnp.concatenate(parts, axis=1)


def _win_from_padded(w_p):
    widths = [b - a for a, b in _WIN_SEGMENTS]
    offs = [sum(widths[:i]) for i in range(len(widths))]
    seg = {a: w_p[:, o:o + wd] for (a, _), o, wd in zip(_WIN_SEGMENTS, offs, widths)}
    return jnp.concatenate([seg[a] for a in sorted(seg)], axis=1)


def _loss_and_grad(loss_fn, h, gain, tgt, specs, tm):
    t = h.shape[0]
    n = t // tm

    def body(h_ref, g_ref, t_ref, l_ref, dh_ref, dg_ref):
        pid = pl.program_id(0)
        tg = t_ref[...]
        (part,), vjp = jax.vjp(lambda a, b: loss_fn(pid, a, b, tg), h_ref[...], g_ref[...])
        dh, dg = vjp((jnp.ones_like(part),))
        l_ref[...] = part
        dh_ref[...] = dh

        @pl.when(pid == 0)
        def _():
            dg_ref[...] = dg

        @pl.when(pid != 0)
        def _():
            dg_ref[...] += dg

    return pl.pallas_call(
        body, name="loss", grid=(n,), in_specs=specs,
        out_specs=[pl.BlockSpec((None, 1, 1), lambda i: (i, 0, 0)), specs[0], _full_spec(gain.shape)],
        out_shape=[jax.ShapeDtypeStruct((n, 1, 1), f32), jax.ShapeDtypeStruct(h.shape, f32),
                   jax.ShapeDtypeStruct(gain.shape, f32)],
    )(h, gain, tgt)


def _shift_bwd3(z, mu, d_r1, d_r2, d_k, d_v1, d_v2, t):
    nb = D // LANES

    def body(z_ref, mu_ref, r1, r2, kk, v1, v2, dz_ref, dmu_ref):
        j = pl.program_id(0)
        ct = jnp.where(j < nb, r1[...] + r2[...], jnp.where(j < 2 * nb, kk[...], v1[...] + v2[...]))
        _, vjp = jax.vjp(lambda a, b: _f_tshift(a, b), z_ref[...], mu_ref[...])
        dz, dmu = vjp((ct,))
        dz_ref[...] = dz.astype(dz_ref.dtype)
        dmu_ref[...] = dmu

    def window(first):
        return pl.BlockSpec((t, LANES), lambda j, f=first: (0, jnp.clip(j - f * nb, 0, nb - 1)))

    return pl.pallas_call(
        body, name="b_shift_bwd", grid=(3 * nb,),
        in_specs=[_col_spec(t, 32), pl.BlockSpec((1, LANES), lambda j: (0, j)), window(0), window(0), window(1),
                  window(2), window(2)],
        out_specs=[_col_spec(t, 0), pl.BlockSpec((1, LANES), lambda j: (0, j))],
        out_shape=[jax.ShapeDtypeStruct((t, 3 * D), MXU_DTYPE), jax.ShapeDtypeStruct((1, 3 * D), f32)],
    )(z, mu, d_r1, d_r2, d_k, d_v1, d_v2)


def _conv_bwd(fn, z, conv_w, ct, idx, t):
    def body(z_ref, w_ref, ct_ref, dz_ref, dw_ref):
        _, vjp = jax.vjp(lambda a, b: fn(a, b), z_ref[...], w_ref[...])
        dz, dw = vjp((ct_ref[...],))
        dz_ref[...] = dz.astype(dz_ref.dtype)
        dw_ref[...] = dw

    return pl.pallas_call(
        body, name=f"a_conv{idx}_bwd", grid=(A_HEADS,),
        in_specs=[_col_spec(t, 8 * idx), pl.BlockSpec((4, LANES), lambda j, o=8 * idx: (0, j + o)), _col_spec(t, 0)],
        out_specs=[_col_spec(t, 0), pl.BlockSpec((4, LANES), lambda j: (0, j))],
        out_shape=[jax.ShapeDtypeStruct((t, D), MXU_DTYPE), jax.ShapeDtypeStruct((4, D), f32)],
    )(z, conv_w, ct)


def _position():
    return lax.axis_index("x"), lax.axis_index("y"), lax.axis_index("c")


def _flip(v, f):
    return 1 - v if f else v


_CHIP_FLIPS = ((1, 0), (0, 1), (1, 1))


def _gather_chips(arrs, name):
    n = len(arrs)
    assert all(a.shape[0] % 32 == 0 for a in arrs)
    arrs = [a.reshape(2, a.shape[0] // 2, a.shape[1]) for a in arrs]

    def body(*refs):
        ins, outs = refs[:n], refs[n:2 * n]
        send, recv, fsend, frecv, own = refs[2 * n:]
        x, y, c = _position()
        me = 2 * x + y
        sends, plan, owns = [], [], []
        for a in range(n):
            cp = pltpu.make_async_remote_copy(src_ref=ins[a], dst_ref=outs[a].at[me], send_sem=own.at[a, 0],
                                              recv_sem=own.at[a, 1], device_id=(x, y, 1 - c), device_id_type=MESH)
            cp.start()
            owns.append(cp)
            for j, (fx, fy) in enumerate(_CHIP_FLIPS):
                px, py = _flip(x, fx), _flip(y, fy)
                p = 2 * px + py
                cp = pltpu.make_async_remote_copy(src_ref=ins[a].at[c], dst_ref=outs[a].at[me, c],
                                                  send_sem=send.at[a, j], recv_sem=recv.at[a, j],
                                                  device_id=(px, py, c), device_id_type=MESH)
                cp.start()
                sends.append(cp)
                landed = pltpu.make_async_remote_copy(src_ref=ins[a].at[c], dst_ref=outs[a].at[p, c],
                                                      send_sem=send.at[a, j], recv_sem=recv.at[a, j],
                                                      device_id=(px, py, c), device_id_type=MESH)
                onward = pltpu.make_async_remote_copy(src_ref=outs[a].at[p, c], dst_ref=outs[a].at[p, c],
                                                      send_sem=fsend.at[a, j], recv_sem=frecv.at[a, j],
                                                      device_id=(x, y, 1 - c), device_id_type=MESH)
                from_sibling = pltpu.make_async_remote_copy(src_ref=outs[a].at[p, 1 - c], dst_ref=outs[a].at[p, 1 - c],
                                                            send_sem=fsend.at[a, j], recv_sem=frecv.at[a, j],
                                                            device_id=(x, y, 1 - c), device_id_type=MESH)
                plan.append((landed, onward, from_sibling))
        for landed, onward, _ in plan:
            landed.wait_recv()
            onward.start()
        for _, _, from_sibling in plan:
            from_sibling.wait_recv()
        for cp in sends:
            cp.wait_send()
        for _, onward, _ in plan:
            onward.wait_send()
        for cp in owns:
            cp.wait()

    sems = [pltpu.SemaphoreType.DMA((n, 3))] * 4 + [pltpu.SemaphoreType.DMA((n, 2))]
    outs = pl.pallas_call(
        body, name=name, in_specs=[ANY] * n, out_specs=[ANY] * n,
        out_shape=[jax.ShapeDtypeStruct((N_CHIPS,) + a.shape, a.dtype) for a in arrs], scratch_shapes=sems,
    )(*arrs)
    return [o.reshape(N_CHIPS, o.shape[1] * o.shape[2], o.shape[3]) for o in outs]


def _swap_sibling(arrs, src_of, shapes, name):
    n = len(arrs)

    def body(*refs):
        a_refs, got_refs = refs[:n], refs[n:2 * n]
        send, recv = refs[2 * n:]
        x, y, c = _position()
        copies = []
        for i in range(n):
            cp = pltpu.make_async_remote_copy(src_ref=src_of(a_refs[i], c), dst_ref=got_refs[i], send_sem=send.at[i],
                                              recv_sem=recv.at[i], device_id=(x, y, 1 - c), device_id_type=MESH)
            cp.start()
            copies.append(cp)
        for cp in copies:
            cp.wait()

    return pl.pallas_call(body, name=name, in_specs=[ANY] * n, out_specs=[ANY] * n,
                          out_shape=[jax.ShapeDtypeStruct(sh, a.dtype) for sh, a in zip(shapes, arrs)],
                          scratch_shapes=[pltpu.SemaphoreType.DMA((n,))] * 2)(*arrs)


def _row_tile(rows, width):
    return _tile(rows, max(16, (784 * LANES // width) // 16 * 16), 16)


def _add_halves(g, got, dtype, name):
    n, _, hr, w = g.shape
    tr = _row_tile(hr, w)

    def body(g_ref, got_ref, o_ref):
        c = lax.axis_index("c")
        own = jnp.where(c == 0, g_ref[:, 0], g_ref[:, 1])
        o_ref[...] = (own + got_ref[...]).astype(dtype)

    return pl.pallas_call(
        body, name=name, grid=(hr // tr,),
        in_specs=[pl.BlockSpec((n, 2, tr, w), lambda i: (0, 0, i, 0)), pl.BlockSpec((n, tr, w), lambda i: (0, i, 0))],
        out_specs=pl.BlockSpec((n, tr, w), lambda i: (0, i, 0)),
        out_shape=jax.ShapeDtypeStruct((n, hr, w), dtype))(g, got)


def _scatter_chips(gs, name):
    n = len(gs)

    def body(*refs):
        g_refs, out_refs = refs[:n], refs[n:2 * n]
        send, recv = refs[2 * n:]
        x, y, c = _position()
        sends = []
        for i in range(n):
            for j, (fx, fy) in enumerate(_CHIP_FLIPS):
                px, py = _flip(x, fx), _flip(y, fy)
                cp = pltpu.make_async_remote_copy(src_ref=g_refs[i].at[2 * px + py], dst_ref=out_refs[i].at[j],
                                                  send_sem=send.at[i, j], recv_sem=recv.at[i, j],
                                                  device_id=(px, py, c), device_id_type=MESH)
                cp.start()
                sends.append(cp)
        for cp in sends:
            cp.wait_recv()
        for cp in sends:
            cp.wait_send()

    return pl.pallas_call(
        body, name=name, in_specs=[ANY] * n, out_specs=[ANY] * n,
        out_shape=[jax.ShapeDtypeStruct((3,) + g.shape[1:], g.dtype) for g in gs],
        scratch_shapes=[pltpu.SemaphoreType.DMA((n, 3)), pltpu.SemaphoreType.DMA((n, 3))],
    )(*gs)


def _sum_own_and_slots(own, got, name):
    n, r, w = own.shape
    tr = _row_tile(r, w)

    def body(own_ref, got_ref, o_ref):
        me = 2 * lax.axis_index("x") + lax.axis_index("y")
        acc = own_ref[0]
        for i in range(1, n):
            acc = jnp.where(me == i, own_ref[i], acc)
        acc = acc.astype(f32)
        for j in range(3):
            acc = acc + got_ref[j].astype(f32)
        o_ref[...] = acc

    return pl.pallas_call(
        body, name=name, grid=(r // tr,),
        in_specs=[pl.BlockSpec((n, tr, w), lambda i: (0, i, 0)), pl.BlockSpec((3, tr, w), lambda i: (0, i, 0))],
        out_specs=pl.BlockSpec((tr, w), lambda i: (i, 0)), out_shape=jax.ShapeDtypeStruct((r, w), f32))(own, got)


def _share_chips(a, name):
    def body(a_ref, out_ref, send, recv):
        x, y, c = _position()
        sends = []
        for j, (fx, fy) in enumerate(_CHIP_FLIPS):
            cp = pltpu.make_async_remote_copy(src_ref=a_ref, dst_ref=out_ref.at[j], send_sem=send.at[j],
                                              recv_sem=recv.at[j], device_id=(_flip(x, fx), _flip(y, fy), c),
                                              device_id_type=MESH)
            cp.start()
            sends.append(cp)
        for cp in sends:
            cp.wait_recv()
        for cp in sends:
            cp.wait_send()

    return pl.pallas_call(
        body, name=name, in_specs=[ANY], out_specs=ANY, out_shape=jax.ShapeDtypeStruct((3,) + a.shape, a.dtype),
        scratch_shapes=[pltpu.SemaphoreType.DMA((3,)), pltpu.SemaphoreType.DMA((3,))],
    )(a)


def _sum_in_chip_order(pair, got, name):
    r, w = pair.shape
    tr = _tile(r, 1408, 8)

    def body(p_ref, g_ref, o_ref):
        x, y = lax.axis_index("x"), lax.axis_index("y")
        me = 2 * x + y
        across = [2 * _flip(x, fx) + _flip(y, fy) for fx, fy in _CHIP_FLIPS]
        acc = None
        for i in range(N_CHIPS):
            term = p_ref[...]
            for j in range(3):
                term = jnp.where(across[j] == i, g_ref[j], term)
            acc = term if acc is None else acc + term
        o_ref[...] = acc

    return pl.pallas_call(
        body, name=name, grid=(r // tr,),
        in_specs=[pl.BlockSpec((tr, w), lambda i: (i, 0)), pl.BlockSpec((3, tr, w), lambda i: (0, i, 0))],
        out_specs=pl.BlockSpec((tr, w), lambda i: (i, 0)), out_shape=jax.ShapeDtypeStruct((r, w), f32))(pair, got)


def _add2(a, b, name):
    r, w = a.shape
    tr = _tile(r, 1408, 8)
    spec = pl.BlockSpec((tr, w), lambda i: (i, 0))

    def body(a_ref, b_ref, o_ref):
        o_ref[...] = a_ref[...] + b_ref[...]

    return pl.pallas_call(body, name=name, grid=(r // tr,), in_specs=[spec, spec], out_specs=spec,
                          out_shape=jax.ShapeDtypeStruct(a.shape, f32))(a, b)


def _adamw(w, g_parts, m, v, name):
    shape = w.shape
    view = shape if len(shape) >= 2 else (1,) + shape
    assert all(d == 1 for d in view[:-2]), shape
    rows, cols = view[-2:]
    cap = max(8, (256 * 1024 // cols) // 8 * 8)
    tr = rows if rows <= cap else _tile(rows, cap, 8)
    lead = len(view) - 2
    n_g = len(g_parts)

    def body(*refs):
        w_ref = refs[0]
        g_refs = refs[1:1 + n_g]
        m_ref, v_ref, g_out, d_out, m_out, v_out = refs[1 + n_g:]
        g = g_refs[0][...]
        for gr in g_refs[1:]:
            g = g + gr[...]
        m_new = ADAM_B1 * m_ref[...] + (1.0 - ADAM_B1) * g
        v_new = ADAM_B2 * v_ref[...] + (1.0 - ADAM_B2) * (g * g)
        m_hat = m_new / (1.0 - ADAM_B1 ** ADAM_STEP)
        v_hat = v_new / (1.0 - ADAM_B2 ** ADAM_STEP)
        g_out[...] = g
        d_out[...] = -ADAM_LR * (m_hat / (jnp.sqrt(v_hat) + ADAM_EPS) + ADAM_WD * w_ref[...])
        m_out[...] = m_new
        v_out[...] = v_new

    spec = pl.BlockSpec((None,) * lead + (tr, cols), lambda i: (0,) * lead + (i, 0))
    args = [w.reshape(view)] + [g.reshape(view) for g in g_parts] + [m.reshape(view), v.reshape(view)]
    outs = pl.pallas_call(body, name=name, grid=(rows // tr,), in_specs=[spec] * len(args), out_specs=[spec] * 4,
                          out_shape=[jax.ShapeDtypeStruct(view, f32)] * 4)(*args)
    return [o.reshape(shape) for o in outs]


_BIG = ('ffn1_w_gu', 'ffn1_w_down', 'w_in', 'w_out', 'ffn2_w_gu', 'ffn2_w_down')
_SMALL_SHARDED = ('meta_tokens', 'a_conv_w', 'b_w_up', 'b_a_up', 'b_g_up')
_WEIGHTS = ('meta_tokens', 'ffn1_norm', 'ffn1_w_gu', 'ffn1_w_down', 'mix_norm', 'w_in', 'a_conv_w', 'a_log_rate',
            'a_dt_bias', 'a_out_norm', 'b_shift_mu', 'b_w0', 'b_w_up', 'b_a0', 'b_a_up', 'b_g_up', 'b_k_k', 'b_k_a',
            'b_r_k', 'b_ln_gain', 'b_ln_bias', 'w_out', 'ffn2_norm', 'ffn2_w_gu', 'ffn2_w_down', 'final_norm')
_SMALL = tuple(n for n in _WEIGHTS if n not in _BIG)


def _rows_of(shape):
    n = 1
    for d in shape:
        n *= d
    return n, -(-n // LANES)


def _pack(arrs, dtype, row_mult=32):
    parts, total = [], 0
    for a in arrs:
        n, rows = _rows_of(a.shape)
        flat = a.reshape(-1).astype(dtype)
        if n % LANES:
            flat = jnp.pad(flat, (0, rows * LANES - n))
        parts.append(flat)
        total += rows
    extra = -total % row_mult
    if extra:
        parts.append(jnp.zeros((extra * LANES,), dtype))
    return jnp.concatenate(parts).reshape(total + extra, LANES)


def _unpack(packed, shapes, lead=()):
    out, off = [], 0
    for sh in shapes:
        n, rows = _rows_of(sh)
        seg = packed[..., off:off + rows, :]
        if n % LANES:
            seg = seg.reshape(lead + (-1,))[..., :n]
        out.append(seg.reshape(lead + tuple(sh)))
        off += rows
    return out


def _cols_from_shards(s):
    return jnp.concatenate([s[i] for i in range(N_CHIPS)], axis=-1)


def _cols_to_shards(a):
    r, c = a.shape
    return a.reshape(r, N_CHIPS, c // N_CHIPS).transpose(1, 0, 2)


def kernel(x, meta_tokens, ffn1_norm, ffn1_w_gu, ffn1_w_down, mix_norm, w_in, a_conv_w, a_log_rate, a_dt_bias, a_out_norm, b_shift_mu, b_w0, b_w_up, b_a0, b_a_up, b_g_up, b_k_k, b_k_a, b_r_k, b_ln_gain, b_ln_bias, w_out, ffn2_norm, ffn2_w_gu, ffn2_w_down, final_norm, loss_target, m_meta_tokens, m_ffn1_norm, m_ffn1_w_gu, m_ffn1_w_down, m_mix_norm, m_w_in, m_a_conv_w, m_a_log_rate, m_a_dt_bias, m_a_out_norm, m_b_shift_mu, m_b_w0, m_b_w_up, m_b_a0, m_b_a_up, m_b_g_up, m_b_k_k, m_b_k_a, m_b_r_k, m_b_ln_gain, m_b_ln_bias, m_w_out, m_ffn2_norm, m_ffn2_w_gu, m_ffn2_w_down, m_final_norm, v_meta_tokens, v_ffn1_norm, v_ffn1_w_gu, v_ffn1_w_down, v_mix_norm, v_w_in, v_a_conv_w, v_a_log_rate, v_a_dt_bias, v_a_out_norm, v_b_shift_mu, v_b_w0, v_b_w_up, v_b_a0, v_b_a_up, v_b_g_up, v_b_k_k, v_b_k_a, v_b_r_k, v_b_ln_gain, v_b_ln_bias, v_w_out, v_ffn2_norm, v_ffn2_w_gu, v_ffn2_w_down, v_final_norm):
    args = locals()
    wts = {n: args[n] for n in _WEIGHTS}
    mom = {n: args["m_" + n] for n in _WEIGHTS}
    var = {n: args["v_" + n] for n in _WEIGHTS}
    chip = 2 * lax.axis_index("x") + lax.axis_index("y")

    big_shapes = [wts[n].shape[1:] for n in _BIG]
    small_shapes = [wts[n].shape[-2:] for n in _SMALL_SHARDED]
    big_flat = [wts[n].astype(bf16).reshape(wts[n].shape[1:]) for n in _BIG]
    small_packed = _pack([wts[n] for n in _SMALL_SHARDED], f32)
    gathered = _gather_chips(big_flat + [small_packed], "gather_weights")
    gu1, dn1, w_in_s, w_out_s, gu2, dn2 = [a.reshape((N_CHIPS,) + tuple(sh)) for a, sh in zip(gathered, big_shapes)]
    meta_s, conv_s, wup_s, aup_s, gup_s = _unpack(gathered[-1], small_shapes, (N_CHIPS,))
    w = {
        'ffn1_norm': ffn1_norm, 'mix_norm': mix_norm, 'ffn2_norm': ffn2_norm, 'final_norm': final_norm[None, :],
        'ffn1_wgu': gu1, 'ffn1_wd': dn1.reshape(D_FF, D), 'ffn2_wgu': gu2, 'ffn2_wd': dn2.reshape(D_FF, D),
        'w_in_p': _win_to_padded(_cols_from_shards(w_in_s)), 'w_out': w_out_s.reshape(D, D),
        'a_conv_w': _cols_from_shards(conv_s), 'b_w_up': _cols_from_shards(wup_s), 'b_a_up': _cols_from_shards(aup_s),
        'b_g_up': _cols_from_shards(gup_s),
        'a_log_rate': a_log_rate, 'a_dt_bias': a_dt_bias, 'a_out_norm': a_out_norm, 'b_shift_mu': b_shift_mu,
        'b_w0': b_w0, 'b_a0': b_a0, 'b_k_k': b_k_k, 'b_k_a': b_k_a, 'b_r_k': b_r_k, 'b_ln_gain': b_ln_gain,
        'b_ln_bias': b_ln_bias,
    }
    meta_full = _cols_from_shards(meta_s)

    h0 = jnp.concatenate([jnp.zeros((PAD, D), f32), meta_full, x[0]], axis=0)
    tgt = jnp.concatenate([jnp.zeros((SKIP, D), f32), loss_target[0]], axis=0)
    loss_local, d_h0, g = _local_step(h0, tgt, w)
    loss = lax.psum(loss_local, ("x", "y", "c"))
    grad_x = d_h0[SKIP:][None]

    big_grads = [
        jnp.concatenate([g['ffn1_wg'], g['ffn1_wu']], axis=0),
        g['ffn1_wd'].reshape(N_CHIPS, D_FF // N_CHIPS, D),
        _cols_to_shards(_win_from_padded(g['w_in_p'])),
        g['w_out'].reshape(N_CHIPS, D // N_CHIPS, D),
        jnp.concatenate([g['ffn2_wg'], g['ffn2_wu']], axis=0),
        g['ffn2_wd'].reshape(N_CHIPS, D_FF // N_CHIPS, D),
    ]
    g_halves = [a.reshape(N_CHIPS, 2, a.shape[1] // 2, a.shape[2]) for a in big_grads]
    sib_halves = _swap_sibling(g_halves, lambda ref, c: ref.at[:, 1 - c], [a.shape[:1] + a.shape[2:] for a in g_halves],
                               "swap_halves")
    chip_halves = [_add_halves(a, b, bf16, f"add_sibling{i}") for i, (a, b) in enumerate(zip(g_halves, sib_halves))]
    got = _scatter_chips(chip_halves, "scatter_grads")
    mine = [_sum_own_and_slots(a, b, f"sum_chips{i}") for i, (a, b) in enumerate(zip(chip_halves, got))]
    theirs = _swap_sibling(mine, lambda ref, c: ref, [a.shape for a in mine], "swap_sums")
    core = lax.axis_index("c")
    big_parts = [jnp.concatenate([jnp.where(core == 0, a, b), jnp.where(core == 0, b, a)], axis=0)
                 for a, b in zip(mine, theirs)]

    small_full = {
        'meta_tokens': d_h0[PAD:SKIP], 'ffn1_norm': g['ffn1_norm'], 'mix_norm': g['mix_norm'], 'a_conv_w': g['a_conv_w'],
        'a_log_rate': g['a_log_rate'], 'a_dt_bias': g['a_dt_bias'], 'a_out_norm': g['a_out_norm'],
        'b_shift_mu': g['b_shift_mu'], 'b_w0': g['b_w0'], 'b_w_up': g['b_w_up'], 'b_a0': g['b_a0'], 'b_a_up': g['b_a_up'],
        'b_g_up': g['b_g_up'], 'b_k_k': g['b_k_k'], 'b_k_a': g['b_k_a'], 'b_r_k': g['b_r_k'], 'b_ln_gain': g['b_ln_gain'],
        'b_ln_bias': g['b_ln_bias'], 'ffn2_norm': g['ffn2_norm'], 'final_norm': g['final_norm'],
    }
    s_shapes = [small_full[n].shape for n in _SMALL]
    s_packed = _pack([small_full[n] for n in _SMALL], f32, row_mult=256)
    (s_sib,) = _swap_sibling([s_packed], lambda ref, c: ref, [s_packed.shape], "swap_small")
    s_pair = _add2(s_packed, s_sib, "add_small")
    s_sum = _sum_in_chip_order(s_pair, _share_chips(s_pair, "share_small"), "sum_small")
    s_parts = dict(zip(_SMALL, _unpack(s_sum, s_shapes)))

    grad, delta, new_m, new_v = {}, {}, {}, {}
    for n, a in zip(_BIG, big_parts):
        grad[n], delta[n], new_m[n], new_v[n] = _adamw(wts[n], [a.reshape(wts[n].shape)], mom[n], var[n], f"adamw_{n}")
    for n in _SMALL:
        gs = s_parts[n]
        if n in _SMALL_SHARDED:
            width = wts[n].shape[-1]
            gs = lax.dynamic_slice_in_dim(gs, chip * width, width, axis=gs.ndim - 1)
        gs = gs.reshape(wts[n].shape)
        grad[n], delta[n], new_m[n], new_v[n] = _adamw(wts[n], [gs], mom[n], var[n], f"adamw_{n}")

    return (loss, grad_x, *[grad[n] for n in _WEIGHTS], *[delta[n] for n in _WEIGHTS],
            *[new_m[n] for n in _WEIGHTS], *[new_v[n] for n in _WEIGHTS])
```

```python
import functools

import jax
import jax.numpy as jnp
from jax import lax
from jax.experimental import pallas as pl
from jax.experimental.pallas import tpu as pltpu

f32 = jnp.float32
bf16 = jnp.bfloat16
HI = lax.Precision.HIGHEST
MESH = pl.DeviceIdType.MESH
ANY = pl.BlockSpec(memory_space=pl.ANY)

D = 1024
N_META = 16
CHUNK = 64
PAD = CHUNK - N_META
SKIP = PAD + N_META
EPS = 1e-6
D_FF = 2816
A_HEADS = 8
A_DK = 128
B_HEADS = 16
B_N = 64
B_GN_EPS = B_N * 1e-5
W_LORA, AA_LORA, G_LORA = 64, 64, 160
IN_TOTAL = 9520
ZP = 9600
LANES = 128
N_CHIPS = 4
N_DEV = 8

ADAM_LR, ADAM_B1, ADAM_B2, ADAM_EPS, ADAM_WD, ADAM_STEP = 0.001, 0.9, 0.999, 1e-08, 0.01, 10

MXU_DTYPE = bf16


def _tile(n, cap, mult):
    if n <= cap:
        return n
    best = None
    for t in range(mult, cap + 1, mult):
        if n % t == 0:
            best = t
    assert best is not None, (n, cap, mult)
    return best


def _sigmoid(x):
    return jax.nn.sigmoid(x)


def _silu(x):
    return x * jax.nn.sigmoid(x)


def _softplus(x):
    return jnp.maximum(x, 0.0) + jnp.log(1.0 + jnp.exp(-jnp.abs(x)))


def _head_matrix(c, nh):
    hd = c // nh
    r = lax.broadcasted_iota(jnp.int32, (c, nh), 0)
    h = lax.broadcasted_iota(jnp.int32, (c, nh), 1)
    return (r >= h * hd) & (r < (h + 1) * hd)


def _dot_exact_rhs(x, e, cb):
    dn = (((1,), (cb,)), ((), ()))
    if SCAN_PASSES == 0:
        return lax.dot_general(x, e.astype(f32), dn, precision=HI, preferred_element_type=f32)
    eb = e.astype(bf16)
    hi = x.astype(bf16)
    lo = (x - hi.astype(f32)).astype(bf16)
    return (lax.dot_general(hi, eb, dn, preferred_element_type=f32)
            + lax.dot_general(lo, eb, dn, preferred_element_type=f32))


def _head_sum_impl(x, nh):
    e = _head_matrix(x.shape[-1], nh)
    return _dot_exact_rhs(_dot_exact_rhs(x, e, 0), e, 1)


@functools.partial(jax.custom_vjp, nondiff_argnums=(1,))
def _head_sum(x, nh):
    return _head_sum_impl(x, nh)


def _head_sum_fwd(x, nh):
    return _head_sum_impl(x, nh), None


def _head_sum_bwd(nh, _, g):
    return (_head_sum_impl(g, nh),)


_head_sum.defvjp(_head_sum_fwd, _head_sum_bwd)


@functools.partial(jax.custom_vjp, nondiff_argnums=(1,))
def _shift_rows(x, s):
    n = x.shape[0]
    row = lax.broadcasted_iota(jnp.int32, x.shape, 0)
    if s > 0:
        return jnp.where(row >= s, pltpu.roll(x, s, 0), 0.0)
    return jnp.where(row < n + s, pltpu.roll(x, n + s, 0), 0.0)


def _shift_rows_fwd(x, s):
    return _shift_rows(x, s), None


def _shift_rows_bwd(s, _, g):
    return (_shift_rows(g, -s),)


_shift_rows.defvjp(_shift_rows_fwd, _shift_rows_bwd)


def _matmul(a, b, *, ta=False, tb=False, res=None, scale=1.0, name, b_cols_split=None, out_cols_split=None,
            out_into=None):
    assert not (ta and tb)
    (ar, ac) = a.shape
    b0 = 0
    if b_cols_split:
        b0, bs = b_cols_split
        _, br, bc_part = b.shape
        bc = bs * bc_part
    else:
        br, bc = b.shape
    m, k = (ac, ar) if ta else (ar, ac)
    n, kb = (br, bc) if tb else (bc, br)
    assert k == kb, (a.shape, b.shape, ta, tb)
    tm = _tile(m, 1408, LANES) if ta else _tile(m, 832, 8)
    tn = _tile(n, 1408, LANES)
    tk = _tile(k, 1040, 8) if ta else _tile(k, 1408, LANES)
    nk = k // tk
    dn = (((0 if ta else 1,), (1 if tb else 0,)), ((), ()))
    if b_cols_split:
        assert (tk if tb else tn) == bc_part, (b.shape, tn, tk)

    def body(*refs):
        a_ref, b_ref = refs[:2]
        r_ref = refs[2] if res is not None else None
        o_ref, acc = refs[-2:]
        kk = pl.program_id(2)

        @pl.when(kk == 0)
        def _():
            acc[...] = jnp.zeros_like(acc)

        acc[...] += lax.dot_general(a_ref[...].astype(MXU_DTYPE), b_ref[...].astype(MXU_DTYPE), dn,
                                    preferred_element_type=f32,
                                    precision=None if MXU_DTYPE == bf16 else HI)

        @pl.when(kk == nk - 1)
        def _():
            out = acc[...]
            if scale != 1.0:
                out = out * scale
            if res is not None:
                out = r_ref[...] + out
            o_ref[...] = out

    if ta:
        a_spec = pl.BlockSpec((tk, tm), lambda i, j, kk: (kk, i))
    else:
        a_spec = pl.BlockSpec((tm, tk), lambda i, j, kk: (i, kk))
    if tb and b_cols_split:
        b_spec = pl.BlockSpec((None, tn, tk), lambda i, j, kk: (kk + b0, j, 0))
    elif tb:
        b_spec = pl.BlockSpec((tn, tk), lambda i, j, kk: (j, kk))
    elif b_cols_split:
        b_spec = pl.BlockSpec((None, tk, tn), lambda i, j, kk: (j + b0, kk, 0))
    else:
        b_spec = pl.BlockSpec((tk, tn), lambda i, j, kk: (kk, j))
    in_specs = [a_spec, b_spec]
    args = [a, b]
    if res is not None:
        in_specs.append(pl.BlockSpec((tm, tn), lambda i, j, kk: (i, j)))
        args.append(res)
    aliases = {}
    if out_cols_split:
        o0, total = out_cols_split
        out_spec = pl.BlockSpec((None, tm, tn), lambda i, j, kk: (j + o0, i, 0))
        out_shape = jax.ShapeDtypeStruct((total, m, tn), f32)
        if out_into is not None:
            assert out_into.shape == out_shape.shape
            in_specs.append(ANY)
            args.append(out_into)
            aliases = {len(args) - 1: 0}
    else:
        out_spec = pl.BlockSpec((tm, tn), lambda i, j, kk: (i, j))
        out_shape = jax.ShapeDtypeStruct((m, n), f32)
    return pl.pallas_call(
        body, name=name, grid=(m // tm, n // tn, nk), in_specs=in_specs, out_specs=out_spec, out_shape=out_shape,
        scratch_shapes=[pltpu.VMEM((tm, tn), f32)], input_output_aliases=aliases,
        compiler_params=pltpu.CompilerParams(dimension_semantics=("parallel", "parallel", "arbitrary")),
    )(*args)


def _tw_fwd(fn, ins, in_specs, out_shapes, out_specs, grid, name, with_pid=False):
    n_in = len(ins)

    def body(*refs):
        vals = [r[...] for r in refs[:n_in]]
        outs = fn(pl.program_id(0), *vals) if with_pid else fn(*vals)
        for r, o in zip(refs[n_in:], outs):
            r[...] = o.astype(r.dtype)

    return pl.pallas_call(body, name=name, grid=grid, in_specs=in_specs, out_specs=out_specs,
                          out_shape=out_shapes)(*ins)


def _tw_bwd(fn, ins, in_specs, cts, ct_specs, kinds, grid, name, with_pid=False, tile_dtype=f32, ct_extra=(),
            residual=None):
    n_in, n_ct = len(ins), len(cts)
    diff = [i for i, kd in enumerate(kinds) if kd is not None]
    n_ex = len(ct_extra)

    def body(*refs):
        vals = [r[...] for r in refs[:n_in]]
        ctv = [r[...].astype(f32) for r in refs[n_in:n_in + n_ct]]
        for (ci, _), r in zip(ct_extra, refs[n_in + n_ct:n_in + n_ct + n_ex]):
            ctv[ci] = ctv[ci] + r[...]
        ctv = tuple(ctv)
        n_fixed = n_in + n_ct + n_ex
        res_ref = refs[n_fixed] if residual is not None else None
        g_refs = refs[n_fixed + (residual is not None):]
        pid = pl.program_id(0)

        def f(*dv):
            full = list(vals)
            for i, v in zip(diff, dv):
                full[i] = v
            out = fn(pid, *full) if with_pid else fn(*full)
            return tuple(out)

        _, vjp = jax.vjp(f, *[vals[i] for i in diff])
        gs = vjp(ctv)
        first = pid == 0
        for i2 in range(1, len(grid)):
            first = first & (pl.program_id(i2) == 0)
        for i, g, g_ref in zip(diff, gs, g_refs):
            if kinds[i] != 'acc':
                if i == 0 and res_ref is not None:
                    g = res_ref[...] + g
                g_ref[...] = g.astype(g_ref.dtype)
            else:
                @pl.when(first)
                def _(g=g, g_ref=g_ref):
                    g_ref[...] = g

                @pl.when(jnp.logical_not(first))
                def _(g=g, g_ref=g_ref):
                    g_ref[...] += g

    zero_map = {1: lambda *a: (0,), 2: lambda *a: (0, 0), 3: lambda *a: (0, 0, 0)}
    out_specs, out_shapes = [], []
    for i in diff:
        if kinds[i] == 'tile':
            out_shapes.append(jax.ShapeDtypeStruct(ins[i].shape, tile_dtype))
            out_specs.append(in_specs[i])
        elif kinds[i] == 'acc':
            out_shapes.append(jax.ShapeDtypeStruct(ins[i].shape, f32))
            out_specs.append(pl.BlockSpec(ins[i].shape, zero_map[ins[i].ndim]))
        else:
            out_shapes.append(jax.ShapeDtypeStruct(kinds[i][1], kinds[i][3] if len(kinds[i]) > 3 else tile_dtype))
            out_specs.append(kinds[i][2])
    extra_specs = [ct_specs[ci] for ci, _ in ct_extra]
    extra = [a for _, a in ct_extra]
    if residual is not None:
        assert kinds[0] == 'tile'
        extra_specs.append(in_specs[0])
        extra.append(residual)
    return pl.pallas_call(body, name=name, grid=grid, in_specs=list(in_specs) + list(ct_specs) + extra_specs,
                          out_specs=out_specs, out_shape=out_shapes)(*ins, *cts, *extra)


def _row_spec(tm, c, col_block=0):
    return pl.BlockSpec((tm, c), lambda i, cb=col_block: (i, cb))


def _full_spec(shape):
    nd = len(shape)
    return pl.BlockSpec(shape, lambda *a, nd=nd: (0,) * nd)


def _f_rms(x, g):
    return (x * lax.rsqrt(jnp.mean(x * x, axis=-1, keepdims=True) + EPS) * g,)


def _f_swiglu(gate, up):
    return (_silu(gate) * up,)


def _f_loss(pid, h, g, tgt, *, tm):
    y = h * lax.rsqrt(jnp.mean(h * h, axis=-1, keepdims=True) + EPS) * g
    row = pid * tm + lax.broadcasted_iota(jnp.int32, (tm, 1), 0)
    err = jnp.where(row >= SKIP, y - tgt, 0.0)
    per_row = jnp.mean(err * err, axis=-1, keepdims=True)
    return (0.5 * jnp.sum(per_row, axis=0, keepdims=True),)


def _f_conv(x, w, *, norm, scale):
    y = x * w[3:4, :]
    for s in (1, 2, 3):
        y = y + _shift_rows(x, s) * w[3 - s:4 - s, :]
    y = _silu(y)
    if norm:
        y = y * lax.rsqrt(jnp.sum(y * y, axis=-1, keepdims=True) + 1e-6) * scale
    return (y,)


def _f_dgates(pid, abeta, aalpha, log_rate, dt_bias, *, tm):
    row = pid * tm + lax.broadcasted_iota(jnp.int32, (tm, 1), 0)
    live = row >= PAD
    beta = jnp.where(live, _sigmoid(abeta), 0.0)
    g = jnp.where(live, -jnp.exp(log_rate) * _softplus(aalpha + dt_bias), 0.0)
    return beta, g


def _f_tshift(z, mu):
    return (z + (_shift_rows(z, 1) - z) * mu,)


def _f_rwkv_pre(k, wd, ad, gd, w0, w_up, a0, a_up, g_up, k_k, k_a):
    w_log = -_softplus(-(w0 + _smm(jnp.tanh(wd), w_up, 1))) - 0.5
    lw = -jnp.exp(w_log)
    a_lr = _sigmoid(a0 + _smm(ad, a_up, 1))
    gate = _smm(_sigmoid(gd), g_up, 1)
    kkp = k * k_k
    kk = kkp * lax.rsqrt(_head_sum(kkp * kkp, B_HEADS) + 1e-6)
    kmod = k * (1.0 + (a_lr - 1.0) * k_a)
    return lw, kmod, -kk, kk * a_lr, gate


def _f_mix_post(o, az, y, r, kmod, v, gate, ga, gb, out_gain, ln_g, ln_b, r_k):
    ms = _head_sum(o * o, A_HEADS) * (1.0 / A_DK)
    oa = o * lax.rsqrt(ms + EPS) * out_gain * _silu(az)
    mean = _head_sum(y, B_HEADS) * (1.0 / B_N)
    yc = y - mean
    var = _head_sum(yc * yc, B_HEADS) * (1.0 / B_N)
    yn = yc * lax.rsqrt(var + B_GN_EPS) * ln_g + ln_b
    bonus = _head_sum(r * kmod * r_k, B_HEADS) * v
    ob = (yn + bonus) * gate
    return (_sigmoid(ga) * oa + _sigmoid(gb) * ob,)


SCAN_PASSES = 3


def _split2(a):
    hi = a.astype(bf16)
    return hi, (a - hi.astype(f32)).astype(bf16)


def _dot_passes(a, b, ca, cb, passes):
    dn = (((ca,), (cb,)), ((), ()))
    if SCAN_PASSES == 0:
        return lax.dot_general(a, b, dn, precision=HI, preferred_element_type=f32)
    if passes == 1:
        return lax.dot_general(a.astype(bf16), b.astype(bf16), dn, preferred_element_type=f32)
    ah, al = _split2(a)
    bh, bl = _split2(b)
    return (lax.dot_general(ah, bh, dn, preferred_element_type=f32)
            + (lax.dot_general(ah, bl, dn, preferred_element_type=f32)
               + lax.dot_general(al, bh, dn, preferred_element_type=f32)))


@functools.partial(jax.custom_vjp, nondiff_argnums=(2, 3, 4))
def _sdot(a, b, ca, cb, passes):
    return _dot_passes(a, b, ca, cb, passes)


def _sdot_fwd(a, b, ca, cb, passes):
    return _dot_passes(a, b, ca, cb, passes), (a, b)


def _sdot_bwd(ca, cb, passes, res, g):
    a, b = res
    if (ca, cb) == (1, 0):
        return _dot_passes(g, b, 1, 1, passes), _dot_passes(a, g, 0, 0, passes)
    if (ca, cb) == (1, 1):
        return _dot_passes(g, b, 1, 0, passes), _dot_passes(g, a, 0, 0, passes)
    assert (ca, cb) == (0, 0)
    return _dot_passes(b, g, 1, 1, passes), _dot_passes(a, g, 1, 0, passes)


_sdot.defvjp(_sdot_fwd, _sdot_bwd)


def _smm(a, b, passes=3):
    return _sdot(a, b, 1, 0, passes)


def _smm_nt(a, b, passes=3):
    return _sdot(a, b, 1, 1, passes)


def _smm_tn(a, b, passes=3):
    return _sdot(a, b, 0, 0, passes)


def _tri_dot(x, ca):
    n = x.shape[0]
    incl = _tri_masks(n)[0]
    dn = (((ca,), (0,)), ((), ()))
    if SCAN_PASSES == 0:
        return lax.dot_general(incl.astype(f32), x, dn, precision=HI, preferred_element_type=f32)
    tri = incl.astype(bf16)
    hi, r1 = x.astype(bf16), None
    r1 = x - hi.astype(f32)
    mid = r1.astype(bf16)
    lo = (r1 - mid.astype(f32)).astype(bf16)
    return (lax.dot_general(tri, hi, dn, preferred_element_type=f32)
            + (lax.dot_general(tri, mid, dn, preferred_element_type=f32)
               + lax.dot_general(tri, lo, dn, preferred_element_type=f32)))


@jax.custom_vjp
def _cumsum_rows(x):
    return _tri_dot(x, 1)


def _cumsum_rows_fwd(x):
    return _tri_dot(x, 1), None


def _cumsum_rows_bwd(_, g):
    return (_tri_dot(g, 0),)


_cumsum_rows.defvjp(_cumsum_rows_fwd, _cumsum_rows_bwd)


def _tri_masks(n):
    i = lax.broadcasted_iota(jnp.int32, (n, n), 0)
    j = lax.broadcasted_iota(jnp.int32, (n, n), 1)
    return i >= j, i > j, i == j, i <= j


def _unit_lower_inv_impl(low, passes):
    n = low.shape[0]
    assert n == CHUNK
    _, _, eye, _ = _tri_masks(n)
    acc = eye.astype(f32) + low
    p = low
    for _ in range(5):
        p = _dot_passes(p, p, 1, 0, passes)
        acc = acc + _dot_passes(acc, p, 1, 0, passes)
    return acc


@functools.partial(jax.custom_vjp, nondiff_argnums=(1,))
def _unit_lower_inv(low, passes=3):
    return _unit_lower_inv_impl(low, passes)


def _unit_lower_inv_fwd(low, passes):
    t = _unit_lower_inv_impl(low, passes)
    return t, t


def _unit_lower_inv_bwd(passes, t, g):
    return (_dot_passes(_dot_passes(t, g, 0, 0, passes), t, 1, 1, passes),)


_unit_lower_inv.defvjp(_unit_lower_inv_fwd, _unit_lower_inv_bwd)

DELTA_PASSES = 1
DELTA_INV_PASSES = 1


def _delta_chunk(s, q, k, v, beta_row, g_row):
    p = DELTA_PASSES
    incl, strict, eye, upper = _tri_masks(CHUNK)
    beta = jnp.sum(jnp.where(eye, beta_row, 0.0), axis=1, keepdims=True)
    g = jnp.sum(jnp.where(eye, g_row, 0.0), axis=1, keepdims=True)
    gc = jnp.sum(jnp.where(incl, g_row, 0.0), axis=1, keepdims=True)
    gc_row = jnp.sum(jnp.where(upper, g, 0.0), axis=0, keepdims=True)
    decay = jnp.where(incl, jnp.exp(jnp.where(incl, gc - gc_row, 0.0)), 0.0)
    kb = k * beta
    vb = v * beta
    m = jnp.where(strict, _smm_nt(kb, k, p) * decay, 0.0)
    tinv = _unit_lower_inv(-m, DELTA_INV_PASSES)
    u = _smm(tinv, vb, p)
    wk = _smm(tinv, kb * jnp.exp(gc), p)
    attn = _smm_nt(q, k, p) * decay
    qg = q * jnp.exp(gc)
    g_last = jnp.sum(g, axis=0, keepdims=True)
    k_tail = k * jnp.exp(g_last - gc)
    v_new = u - _smm(wk, s, p)
    o = _smm(qg, s, p) + _smm(attn, v_new, p)
    s_new = s * jnp.exp(g_last) + _smm_tn(k_tail, v_new, p)
    return o, s_new


RWKV_PASSES = 1
RWKV_INV_PASSES = 1


def _rwkv_chunk(st, r, k, v, a, b, lw):
    c = CHUNK
    p, pi = RWKV_PASSES, RWKV_INV_PASSES
    _, strict, _, _ = _tri_masks(c)
    lane = lax.broadcasted_iota(jnp.int32, (c, 2 * B_N), 1)
    row = lax.broadcasted_iota(jnp.int32, (c, 2 * B_N), 0)
    first = lane < B_N
    incl2 = row >= jnp.where(first, lane, lane - B_N)
    bi = lax.broadcasted_iota(jnp.int32, (2 * B_N, 2 * B_N), 0) < B_N
    bj = lax.broadcasted_iota(jnp.int32, (2 * B_N, 2 * B_N), 1) < B_N
    blockdiag = bi == bj
    cum = _cumsum_rows(lw)
    e_pos = jnp.exp(cum)
    e_neg = jnp.exp(-cum)
    rt = r * e_pos
    at = a * jnp.exp(cum - lw)
    kt = k * e_neg
    bt = b * e_neg
    bk = jnp.concatenate([bt, kt], axis=0)
    a_s0 = _smm_nt(at, st, p)
    r_s0 = _smm_nt(rt, st, p)
    heads = (first, jnp.logical_not(first))
    u = jnp.zeros((c, 2 * B_N), f32)
    for sel in heads:
        at_h = jnp.where(sel, at, 0.0)
        ab = jnp.where(strict, _smm_nt(at_h, bt, pi), 0.0)
        ak = jnp.where(strict, _smm_nt(at_h, kt, p), 0.0)
        t_h = _unit_lower_inv(ab, pi)
        u = u + _smm(t_h, jnp.where(sel, a_s0, 0.0) + _smm(ak, jnp.where(sel, v, 0.0), p), p)
    y = r_s0
    for sel in heads:
        rbk = jnp.where(incl2, _smm_nt(jnp.where(sel, rt, 0.0), bk, p), 0.0)
        uv = jnp.concatenate([jnp.where(sel, u, 0.0), jnp.where(sel, v, 0.0)], axis=0)
        y = y + _smm(rbk, uv, p)
    cl = jnp.sum(lw, axis=0, keepdims=True)
    dec = jnp.exp(cl - cum)
    uv_all = jnp.concatenate([u, v], axis=0)
    bk_dec = jnp.concatenate([b * dec, k * dec], axis=0)
    st_new = st * jnp.exp(cl) + jnp.where(blockdiag, _smm_tn(uv_all, bk_dec, p), 0.0)
    return y, st_new


GROUPS_PER_STEP = 8


def _scan_specs(ins, col_offs, n_chunks, reverse):
    gw = GROUPS_PER_STEP * LANES
    cidx = (lambda c: n_chunks - 1 - c) if reverse else (lambda c: c)
    specs = []
    for a, off in zip(ins, col_offs):
        if a.ndim == 2:
            assert off % gw == 0
            specs.append(pl.BlockSpec((CHUNK, gw), lambda h, c, o=off // gw: (cidx(c), h + o)))
        else:
            specs.append(pl.BlockSpec((GROUPS_PER_STEP, None, 1, CHUNK), lambda h, c: (h, cidx(c), 0, 0)))
    return specs, cidx


def _group_vals(refs, g):
    return [r[:, g * LANES:(g + 1) * LANES] if len(r.shape) == 2 else r[g] for r in refs]


def _scan_fwd(chunk_fn, ins, col_offs, n_groups, n_chunks, state_shape, name):
    n_in = len(ins)
    gps = GROUPS_PER_STEP
    t = ins[0].shape[0]

    def body(*refs):
        in_refs = refs[:n_in]
        o_ref, s0_ref, st = refs[n_in:]

        @pl.when(pl.program_id(1) == 0)
        def _():
            st[...] = jnp.zeros_like(st)

        states = st[...]
        vals = [jnp.stack(col) for col in zip(*[_group_vals(in_refs, g) for g in range(gps)])]
        o, s_new = jax.vmap(chunk_fn)(states, *vals)
        s0_ref[...] = states
        st[...] = s_new
        for g in range(gps):
            o_ref[:, g * LANES:(g + 1) * LANES] = o[g]

    specs, _ = _scan_specs(ins, col_offs, n_chunks, False)
    return pl.pallas_call(
        body, name=name, grid=(n_groups // gps, n_chunks), in_specs=specs,
        out_specs=[pl.BlockSpec((CHUNK, gps * LANES), lambda h, c: (c, h)),
                   pl.BlockSpec((gps, None) + state_shape, lambda h, c: (h, c, 0, 0))],
        out_shape=[jax.ShapeDtypeStruct((t, n_groups * LANES), f32),
                   jax.ShapeDtypeStruct((n_groups, n_chunks) + state_shape, f32)],
        scratch_shapes=[pltpu.VMEM((gps,) + state_shape, f32)],
        compiler_params=pltpu.CompilerParams(dimension_semantics=("parallel", "arbitrary")),
    )(*ins)


def _scan_bwd(chunk_fn, s0s, ins, col_offs, d_out, n_groups, n_chunks, state_shape, name):
    n_in = len(ins)
    gps = GROUPS_PER_STEP
    t = d_out.shape[0]

    def body(*refs):
        s0_ref = refs[0]
        in_refs = refs[1:1 + n_in]
        do_ref = refs[1 + n_in]
        g_refs = refs[2 + n_in:2 + 2 * n_in]
        dst = refs[2 + 2 * n_in]

        @pl.when(pl.program_id(1) == 0)
        def _():
            dst[...] = jnp.zeros_like(dst)

        vals = [jnp.stack(col) for col in zip(*[_group_vals(in_refs, g) for g in range(gps)])]
        d_o = jnp.stack([do_ref[:, g * LANES:(g + 1) * LANES] for g in range(gps)])
        _, vjp = jax.vjp(jax.vmap(chunk_fn), s0_ref[...], *vals)
        gs = vjp((d_o, dst[...]))
        dst[...] = gs[0]
        for g_ref, gv in zip(g_refs, gs[1:]):
            if len(g_ref.shape) == 2:
                for g in range(gps):
                    g_ref[:, g * LANES:(g + 1) * LANES] = gv[g]
            else:
                g_ref[...] = gv

    specs, cidx = _scan_specs(ins, col_offs, n_chunks, True)
    out_lane = pl.BlockSpec((CHUNK, gps * LANES), lambda h, c: (cidx(c), h))
    g_specs = [out_lane if a.ndim == 2 else sp for a, sp in zip(ins, specs)]
    g_shapes = [(t, n_groups * LANES) if a.ndim == 2 else a.shape for a in ins]
    s0_spec = pl.BlockSpec((gps, None) + state_shape, lambda h, c: (h, cidx(c), 0, 0))
    return pl.pallas_call(
        body, name=name, grid=(n_groups // gps, n_chunks), in_specs=[s0_spec] + specs + [out_lane],
        out_specs=g_specs, out_shape=[jax.ShapeDtypeStruct(sh, f32) for sh in g_shapes],
        scratch_shapes=[pltpu.VMEM((gps,) + state_shape, f32)],
        compiler_params=pltpu.CompilerParams(dimension_semantics=("parallel", "arbitrary")),
    )(s0s, *ins, d_out)


def _rms_fwd(x, g, name):
    t = x.shape[0]
    tm = _tile(t, 416, 16)
    return _tw_fwd(_f_rms, [x, g], [_row_spec(tm, D), _full_spec(g.shape)],
                   [jax.ShapeDtypeStruct(x.shape, MXU_DTYPE)], [_row_spec(tm, D)], (t // tm,), name)[0]


def _rms_bwd(x, g, dy, residual, name):
    t = x.shape[0]
    tm = _tile(t, 416, 8)
    return _tw_bwd(_f_rms, [x, g], [_row_spec(tm, D), _full_spec(g.shape)], [dy], [_row_spec(tm, D)],
                   ['tile', 'acc'], (t // tm,), name, residual=residual)


def _ffn_fwd(h, gain, wgu, wd, tag):
    xn = _rms_fwd(h, gain, f"{tag}_rms")
    gate, up, act = _gate_up_act(xn, wgu, f"{tag}_gate_up")
    out = _matmul(act, wd, res=h, scale=0.5, name=f"{tag}_down")
    return out, (xn, gate, up, act)


def _mxu_dot(a, b, dn):
    return lax.dot_general(a.astype(MXU_DTYPE), b.astype(MXU_DTYPE), dn, preferred_element_type=f32,
                           precision=None if MXU_DTYPE == bf16 else HI)


def _gate_up_act(xn, wgu, name):
    t = xn.shape[0]
    wdt = wgu.shape[2]
    tm = _tile(t, 416, 16)
    dn = (((1,), (0,)), ((), ()))

    def body(x_ref, wg_ref, wu_ref, g_ref, u_ref, a_ref):
        x = x_ref[...]
        g = _mxu_dot(x, wg_ref[...], dn)
        u = _mxu_dot(x, wu_ref[...], dn)
        g_ref[...] = g
        u_ref[...] = u
        a_ref[...] = _f_swiglu(g, u)[0].astype(a_ref.dtype)

    out_spec = pl.BlockSpec((tm, wdt), lambda i, j: (i, j))
    return pl.pallas_call(
        body, name=name, grid=(t // tm, 2),
        in_specs=[pl.BlockSpec((tm, D), lambda i, j: (i, 0)), pl.BlockSpec((None, D, wdt), lambda i, j: (j, 0, 0)),
                  pl.BlockSpec((None, D, wdt), lambda i, j: (j + 2, 0, 0))],
        out_specs=[out_spec] * 3,
        out_shape=[jax.ShapeDtypeStruct((t, 2 * wdt), f32)] * 2 + [jax.ShapeDtypeStruct((t, 2 * wdt), MXU_DTYPE)],
        compiler_params=pltpu.CompilerParams(dimension_semantics=("parallel", "parallel")),
    )(xn, wgu, wgu)


def _d_gate_up(dout, wd, gate, up, name):
    t = dout.shape[0]
    wdt = D_FF // 2
    tm = _tile(t, 416, 16)
    dn = (((1,), (1,)), ((), ()))

    def body(do_ref, wd_ref, g_ref, u_ref, dg_ref, du_ref):
        d_act = 0.5 * _mxu_dot(do_ref[...], wd_ref[...], dn)
        _, vjp = jax.vjp(_f_swiglu, g_ref[...], u_ref[...])
        dg, du = vjp((d_act,))
        dg_ref[...] = dg.astype(dg_ref.dtype)
        du_ref[...] = du.astype(du_ref.dtype)

    spec = pl.BlockSpec((tm, wdt), lambda i, j: (i, j))
    return pl.pallas_call(
        body, name=name, grid=(t // tm, 2),
        in_specs=[pl.BlockSpec((tm, D), lambda i, j: (i, 0)), pl.BlockSpec((wdt, D), lambda i, j: (j, 0)), spec, spec],
        out_specs=[spec] * 2, out_shape=[jax.ShapeDtypeStruct((t, D_FF), MXU_DTYPE)] * 2,
        compiler_params=pltpu.CompilerParams(dimension_semantics=("parallel", "parallel")),
    )(dout, wd, gate, up)


def _ffn_bwd(h, gain, wgu, wd, saved, dout, tag):
    xn, gate, up, act = saved
    t = h.shape[0]
    d_wd = _matmul(act, dout, ta=True, scale=0.5, name=f"{tag}_dwd")
    d_gate, d_up = _d_gate_up(dout, wd, gate, up, f"{tag}_dact")
    d_wgu = _matmul(xn, d_gate, ta=True, out_cols_split=(0, N_CHIPS), name=f"{tag}_dwg")
    d_wgu = _matmul(xn, d_up, ta=True, out_cols_split=(2, N_CHIPS), out_into=d_wgu, name=f"{tag}_dwu")
    d_xn = _matmul(d_gate, wgu, tb=True, b_cols_split=(0, 2), name=f"{tag}_dxn_g")
    d_xn = _matmul(d_up, wgu, tb=True, b_cols_split=(2, 2), res=d_xn, name=f"{tag}_dxn_u")
    d_h, d_gain = _rms_bwd(h, gain, d_xn, dout, f"{tag}_drms")
    return d_h, d_gain, d_wgu, d_wd


def _col_spec(t, first_block):
    return pl.BlockSpec((t, LANES), lambda j, fb=first_block: (0, j + fb))


def _local_step(h0, tgt, w):
    t = h0.shape[0]
    assert t % CHUNK == 0
    nc = t // CHUNK
    grads = {}

    h1, ffn1_saved = _ffn_fwd(h0, w['ffn1_norm'], w['ffn1_wgu'], w['ffn1_wd'], "ffn1")
    u = _rms_fwd(h1, w['mix_norm'], "mix_rms")
    z = _matmul(u, w['w_in_p'], name="in_proj")
    zs = z[:, 9216:9216 + 304]
    abeta, aalpha = zs[:, 288:296], zs[:, 296:304]

    conv_w = w['a_conv_w']
    conv_fns = [functools.partial(_f_conv, norm=True, scale=A_DK ** -0.5),
                functools.partial(_f_conv, norm=True, scale=1.0),
                functools.partial(_f_conv, norm=False, scale=1.0)]
    qkv = []
    for idx, fn in enumerate(conv_fns):
        qkv.append(_tw_fwd(fn, [z, conv_w], [_col_spec(t, 8 * idx), pl.BlockSpec((4, LANES), lambda j, o=8 * idx: (0, j + o))],
                           [jax.ShapeDtypeStruct((t, D), f32)], [_col_spec(t, 0)], (A_HEADS,), f"a_conv{idx}")[0])
    aq, ak, av = qkv
    tmg = _tile(t, 1040, 8)
    dg_fn = functools.partial(_f_dgates, tm=tmg)
    dg_specs = [_row_spec(tmg, A_HEADS)] * 2 + [_full_spec((1, A_HEADS))] * 2
    beta, gdec = _tw_fwd(dg_fn, [abeta, aalpha, w['a_log_rate'], w['a_dt_bias']], dg_specs,
                         [jax.ShapeDtypeStruct((t, A_HEADS), f32)] * 2, [_row_spec(tmg, A_HEADS)] * 2, (t // tmg,),
                         "a_gates", with_pid=True)
    beta_h = beta.T.reshape(A_HEADS, nc, 1, CHUNK)
    gdec_h = gdec.T.reshape(A_HEADS, nc, 1, CHUNK)
    a_ins = [aq, ak, av, beta_h, gdec_h]
    a_offs = [0] * 5
    o_scan, a_s0 = _scan_fwd(_delta_chunk, a_ins, a_offs, A_HEADS, nc, (A_DK, A_DK), "a_scan")

    mu = w['b_shift_mu']
    mu_rkv, mu_s = mu[:, :3072], mu[:, 3072:]
    zf_rkv = _tw_fwd(_f_tshift, [z, mu_rkv], [_col_spec(t, 32), pl.BlockSpec((1, LANES), lambda j: (0, j))],
                     [jax.ShapeDtypeStruct((t, 3072), f32)], [_col_spec(t, 0)], (24,), "b_shift")[0]
    zs_b = zs[:, :288]
    zf_s = _tw_fwd(_f_tshift, [zs_b, mu_s], [_full_spec((t, 288)), _full_spec((1, 288))],
                   [jax.ShapeDtypeStruct((t, 288), f32)], [_full_spec((t, 288))], (1,), "b_shift_s")[0]
    wdf, adf, gdf = zf_s[:, 0:64], zf_s[:, 64:128], zf_s[:, 128:288]
    tmr = _tile(t, 160, 16)
    pre_params = [w['b_w0'], w['b_w_up'], w['b_a0'], w['b_a_up'], w['b_g_up'], w['b_k_k'], w['b_k_a']]
    pre_ins = [zf_rkv, wdf, adf, gdf] + pre_params
    pre_specs = ([_row_spec(tmr, D, 1), _row_spec(tmr, 64), _row_spec(tmr, 64), _row_spec(tmr, 160)]
                 + [_full_spec(p.shape) for p in pre_params])
    lw, kmod, a_s, b_s, bgate = _tw_fwd(_f_rwkv_pre, pre_ins, pre_specs, [jax.ShapeDtypeStruct((t, D), f32)] * 5,
                                        [_row_spec(tmr, D)] * 5, (t // tmr,), "b_pre")
    b_ins = [zf_rkv, kmod, zf_rkv, a_s, b_s, lw]
    b_offs = [0, 0, 2 * D, 0, 0, 0]
    y_scan, b_s0 = _scan_fwd(_rwkv_chunk, b_ins, b_offs, B_HEADS // 2, nc, (2 * B_N, 2 * B_N), "b_scan")

    out_gain_t = jnp.tile(w['a_out_norm'], (1, A_HEADS))
    r_k = w['b_r_k'].reshape(1, D)
    post_params = [out_gain_t, w['b_ln_gain'], w['b_ln_bias'], r_k]
    post_ins = [o_scan, z, y_scan, zf_rkv, kmod, zf_rkv, bgate, z, z] + post_params
    post_specs = ([_row_spec(tmr, D), _row_spec(tmr, D, 3), _row_spec(tmr, D), _row_spec(tmr, D, 0), _row_spec(tmr, D),
                   _row_spec(tmr, D, 2), _row_spec(tmr, D), _row_spec(tmr, D, 7), _row_spec(tmr, D, 8)]
                  + [_full_spec((1, D))] * 4)
    merged = _tw_fwd(_f_mix_post, post_ins, post_specs, [jax.ShapeDtypeStruct((t, D), MXU_DTYPE)],
                     [_row_spec(tmr, D)], (t // tmr,), "mix_post")[0]
    h2 = _matmul(merged, w['w_out'], res=h1, name="out_proj")
    h3, ffn2_saved = _ffn_fwd(h2, w['ffn2_norm'], w['ffn2_wgu'], w['ffn2_wd'], "ffn2")

    tml = _tile(t, 416, 8)
    fnorm = w['final_norm']
    loss_fn = functools.partial(_f_loss, tm=tml)
    loss_specs = [_row_spec(tml, D), _full_spec((1, D)), _row_spec(tml, D)]
    loss_parts, d_h3, grads['final_norm'] = _loss_and_grad(loss_fn, h3, fnorm, tgt, loss_specs, tml)
    loss = jnp.sum(loss_parts)

    d_h2, grads['ffn2_norm'], grads['ffn2_wgu'], grads['ffn2_wd'] = _ffn_bwd(
        h2, w['ffn2_norm'], w['ffn2_wgu'], w['ffn2_wd'], ffn2_saved, d_h3, "ffn2")
    grads['w_out'] = _matmul(merged, d_h2, ta=True, name="d_w_out")
    d_merged = _matmul(d_h2, w['w_out'], tb=True, name="d_merged")

    win = ('tile', (t, D), _row_spec(tmr, D))
    zwin = win + (MXU_DTYPE,)
    post_kinds = ['tile', zwin, 'tile', win, 'tile', win, 'tile', zwin, zwin] + ['acc'] * 4
    (d_o, d_az, d_y, d_r1, d_kmod1, d_v1, d_bgate, d_ga, d_gb,
     d_out_gain_t, grads['b_ln_gain'], grads['b_ln_bias'], d_r_k) = _tw_bwd(
        _f_mix_post, post_ins, post_specs, [d_merged], [_row_spec(tmr, D)], post_kinds, (t // tmr,), "mix_post_bwd")
    grads['a_out_norm'] = jnp.sum(d_out_gain_t.reshape(A_HEADS, A_DK), axis=0, keepdims=True)
    grads['b_r_k'] = d_r_k.reshape(1, B_HEADS, B_N)

    d_r2, d_kmod2, d_v2, d_as, d_bs, d_lw = _scan_bwd(_rwkv_chunk, b_s0, b_ins, b_offs, d_y, B_HEADS // 2, nc,
                                                      (2 * B_N, 2 * B_N), "b_scan_bwd")
    pre_kinds = [win] + ['tile'] * 3 + ['acc'] * 7
    pre_ct_specs = [_row_spec(tmr, D)] * 5
    (d_zf_k, d_wdf, d_adf, d_gdf, grads['b_w0'], grads['b_w_up'], grads['b_a0'], grads['b_a_up'], grads['b_g_up'],
     grads['b_k_k'], grads['b_k_a']) = _tw_bwd(
        _f_rwkv_pre, pre_ins, pre_specs, [d_lw, d_kmod1, d_as, d_bs, d_bgate], pre_ct_specs, pre_kinds, (t // tmr,),
        "b_pre_bwd", ct_extra=[(1, d_kmod2)])
    d_zb_rkv, d_mu_rkv = _shift_bwd3(z, mu_rkv, d_r1, d_r2, d_zf_k, d_v1, d_v2, t)
    d_zf_s = jnp.concatenate([d_wdf, d_adf, d_gdf], axis=1)
    d_zs_b, d_mu_s = _tw_bwd(_f_tshift, [zs_b, mu_s], [_full_spec((t, 288)), _full_spec((1, 288))], [d_zf_s],
                             [_full_spec((t, 288))], ['tile', 'tile'], (1,), "b_shift_s_bwd")
    grads['b_shift_mu'] = jnp.concatenate([d_mu_rkv, d_mu_s], axis=1)

    d_aq, d_ak, d_av, d_beta_h, d_g_h = _scan_bwd(_delta_chunk, a_s0, a_ins, a_offs, d_o, A_HEADS, nc, (A_DK, A_DK),
                                                  "a_scan_bwd")
    d_beta = d_beta_h.reshape(A_HEADS, t).T
    d_gdec = d_g_h.reshape(A_HEADS, t).T
    d_abeta, d_aalpha, grads['a_log_rate'], grads['a_dt_bias'] = _tw_bwd(
        dg_fn, [abeta, aalpha, w['a_log_rate'], w['a_dt_bias']], dg_specs, [d_beta, d_gdec],
        [_row_spec(tmg, A_HEADS)] * 2, ['tile', 'tile', 'acc', 'acc'], (t // tmg,), "a_gates_bwd", with_pid=True)
    d_zqkv, d_conv = [], []
    for idx, (fn, ct) in enumerate(zip(conv_fns, (d_aq, d_ak, d_av))):
        dz_i, dw_i = _conv_bwd(fn, z, conv_w, ct, idx, t)
        d_zqkv.append(dz_i)
        d_conv.append(dw_i)
    grads['a_conv_w'] = jnp.concatenate(d_conv, axis=1)

    d_z_parts = d_zqkv + [d_az, d_zb_rkv, d_ga, d_gb, d_zs_b, d_abeta, d_aalpha, jnp.zeros((t, ZP - 9216 - 304), f32)]
    d_z = jnp.concatenate([p.astype(MXU_DTYPE) for p in d_z_parts], axis=1)
    grads['w_in_p'] = _matmul(u, d_z, ta=True, name="d_w_in")
    d_u = _matmul(d_z, w['w_in_p'], tb=True, name="d_u")
    d_h1, grads['mix_norm'] = _rms_bwd(h1, w['mix_norm'], d_u, d_h2, "mix_drms")
    d_h0, grads['ffn1_norm'], grads['ffn1_wgu'], grads['ffn1_wd'] = _ffn_bwd(
        h0, w['ffn1_norm'], w['ffn1_wgu'], w['ffn1_wd'], ffn1_saved, d_h1, "ffn1")
    return loss, d_h0, grads


_WIN_SEGMENTS = ((0, 4096), (4112, 7184), (7472, 9520), (7184, 7472), (4096, 4112))


_WIN_SHARD = IN_TOTAL // N_CHIPS


def _win_pieces():
    pieces, pad_at = [], 0
    for a, b in _WIN_SEGMENTS:
        c = a
        while c < b:
            stop = min(b, (c // _WIN_SHARD + 1) * _WIN_SHARD)
            pieces.append((c, pad_at + c - a, stop - c))
            c = stop
        pad_at += b - a
    return pieces


def _win_shards_to_padded(shards):
    parts = [shards[c // _WIN_SHARD][:, c % _WIN_SHARD:c % _WIN_SHARD + n] for c, _, n in _win_pieces()]
    parts.append(jnp.zeros((shards.shape[1], ZP - IN_TOTAL), shards.dtype))
    return jnp.concatenate(parts, axis=1)


def _win_padded_to_shards(w_p):
    by_shard = [[] for _ in range(N_CHIPS)]
    for c, p, n in sorted(_win_pieces()):
        by_shard[c // _WIN_SHARD].append(w_p[:, p:p + n])
    return jnp.stack([jnp.concatenate(parts, axis=1) for parts in by_shard])


def _loss_and_grad(loss_fn, h, gain, tgt, specs, tm):
    t = h.shape[0]
    n = t // tm

    def body(h_ref, g_ref, t_ref, l_ref, dh_ref, dg_ref):
        pid = pl.program_id(0)
        tg = t_ref[...]
        (part,), vjp = jax.vjp(lambda a, b: loss_fn(pid, a, b, tg), h_ref[...], g_ref[...])
        dh, dg = vjp((jnp.ones_like(part),))
        l_ref[...] = part
        dh_ref[...] = dh

        @pl.when(pid == 0)
        def _():
            dg_ref[...] = dg

        @pl.when(pid != 0)
        def _():
            dg_ref[...] += dg

    return pl.pallas_call(
        body, name="loss", grid=(n,), in_specs=specs,
        out_specs=[pl.BlockSpec((None, 1, 1), lambda i: (i, 0, 0)), specs[0], _full_spec(gain.shape)],
        out_shape=[jax.ShapeDtypeStruct((n, 1, 1), f32), jax.ShapeDtypeStruct(h.shape, f32),
                   jax.ShapeDtypeStruct(gain.shape, f32)],
    )(h, gain, tgt)


def _shift_bwd3(z, mu, d_r1, d_r2, d_k, d_v1, d_v2, t):
    nb = D // LANES

    def body(z_ref, mu_ref, r1, r2, kk, v1, v2, dz_ref, dmu_ref):
        j = pl.program_id(0)
        ct = jnp.where(j < nb, r1[...] + r2[...], jnp.where(j < 2 * nb, kk[...], v1[...] + v2[...]))
        _, vjp = jax.vjp(lambda a, b: _f_tshift(a, b), z_ref[...], mu_ref[...])
        dz, dmu = vjp((ct,))
        dz_ref[...] = dz.astype(dz_ref.dtype)
        dmu_ref[...] = dmu

    def window(first):
        return pl.BlockSpec((t, LANES), lambda j, f=first: (0, jnp.clip(j - f * nb, 0, nb - 1)))

    return pl.pallas_call(
        body, name="b_shift_bwd", grid=(3 * nb,),
        in_specs=[_col_spec(t, 32), pl.BlockSpec((1, LANES), lambda j: (0, j)), window(0), window(0), window(1),
                  window(2), window(2)],
        out_specs=[_col_spec(t, 0), pl.BlockSpec((1, LANES), lambda j: (0, j))],
        out_shape=[jax.ShapeDtypeStruct((t, 3 * D), MXU_DTYPE), jax.ShapeDtypeStruct((1, 3 * D), f32)],
    )(z, mu, d_r1, d_r2, d_k, d_v1, d_v2)


def _conv_bwd(fn, z, conv_w, ct, idx, t):
    def body(z_ref, w_ref, ct_ref, dz_ref, dw_ref):
        _, vjp = jax.vjp(lambda a, b: fn(a, b), z_ref[...], w_ref[...])
        dz, dw = vjp((ct_ref[...],))
        dz_ref[...] = dz.astype(dz_ref.dtype)
        dw_ref[...] = dw

    return pl.pallas_call(
        body, name=f"a_conv{idx}_bwd", grid=(A_HEADS,),
        in_specs=[_col_spec(t, 8 * idx), pl.BlockSpec((4, LANES), lambda j, o=8 * idx: (0, j + o)), _col_spec(t, 0)],
        out_specs=[_col_spec(t, 0), pl.BlockSpec((4, LANES), lambda j: (0, j))],
        out_shape=[jax.ShapeDtypeStruct((t, D), MXU_DTYPE), jax.ShapeDtypeStruct((4, D), f32)],
    )(z, conv_w, ct)


def _position():
    return lax.axis_index("x"), lax.axis_index("y"), lax.axis_index("c")


def _flip(v, f):
    return 1 - v if f else v


_CHIP_FLIPS = ((1, 0), (0, 1), (1, 1))


def _gather_chips(arrs, name):
    n = len(arrs)
    assert all(a.shape[0] % 32 == 0 for a in arrs)
    arrs = [a.reshape(2, a.shape[0] // 2, a.shape[1]) for a in arrs]

    def body(*refs):
        ins, outs = refs[:n], refs[n:2 * n]
        send, recv, fsend, frecv, own = refs[2 * n:]
        x, y, c = _position()
        me = 2 * x + y
        sends, plan, owns = [], [], []
        for a in range(n):
            cp = pltpu.make_async_remote_copy(src_ref=ins[a], dst_ref=outs[a].at[me], send_sem=own.at[a, 0],
                                              recv_sem=own.at[a, 1], device_id=(x, y, 1 - c), device_id_type=MESH)
            cp.start()
            owns.append(cp)
            for j, (fx, fy) in enumerate(_CHIP_FLIPS):
                px, py = _flip(x, fx), _flip(y, fy)
                p = 2 * px + py
                cp = pltpu.make_async_remote_copy(src_ref=ins[a].at[c], dst_ref=outs[a].at[me, c],
                                                  send_sem=send.at[a, j], recv_sem=recv.at[a, j],
                                                  device_id=(px, py, c), device_id_type=MESH)
                cp.start()
                sends.append(cp)
                landed = pltpu.make_async_remote_copy(src_ref=ins[a].at[c], dst_ref=outs[a].at[p, c],
                                                      send_sem=send.at[a, j], recv_sem=recv.at[a, j],
                                                      device_id=(px, py, c), device_id_type=MESH)
                onward = pltpu.make_async_remote_copy(src_ref=outs[a].at[p, c], dst_ref=outs[a].at[p, c],
                                                      send_sem=fsend.at[a, j], recv_sem=frecv.at[a, j],
                                                      device_id=(x, y, 1 - c), device_id_type=MESH)
                from_sibling = pltpu.make_async_remote_copy(src_ref=outs[a].at[p, 1 - c], dst_ref=outs[a].at[p, 1 - c],
                                                            send_sem=fsend.at[a, j], recv_sem=frecv.at[a, j],
                                                            device_id=(x, y, 1 - c), device_id_type=MESH)
                plan.append((landed, onward, from_sibling))
        for landed, onward, _ in plan:
            landed.wait_recv()
            onward.start()
        for _, _, from_sibling in plan:
            from_sibling.wait_recv()
        for cp in sends:
            cp.wait_send()
        for _, onward, _ in plan:
            onward.wait_send()
        for cp in owns:
            cp.wait()

    sems = [pltpu.SemaphoreType.DMA((n, 3))] * 4 + [pltpu.SemaphoreType.DMA((n, 2))]
    outs = pl.pallas_call(
        body, name=name, in_specs=[ANY] * n, out_specs=[ANY] * n,
        out_shape=[jax.ShapeDtypeStruct((N_CHIPS,) + a.shape, a.dtype) for a in arrs], scratch_shapes=sems,
    )(*arrs)
    return [o.reshape(N_CHIPS, o.shape[1] * o.shape[2], o.shape[3]) for o in outs]


def _swap_sibling(arrs, src_of, shapes, name):
    n = len(arrs)

    def body(*refs):
        a_refs, got_refs = refs[:n], refs[n:2 * n]
        send, recv = refs[2 * n:]
        x, y, c = _position()
        copies = []
        for i in range(n):
            cp = pltpu.make_async_remote_copy(src_ref=src_of(a_refs[i], c), dst_ref=got_refs[i], send_sem=send.at[i],
                                              recv_sem=recv.at[i], device_id=(x, y, 1 - c), device_id_type=MESH)
            cp.start()
            copies.append(cp)
        for cp in copies:
            cp.wait()

    return pl.pallas_call(body, name=name, in_specs=[ANY] * n, out_specs=[ANY] * n,
                          out_shape=[jax.ShapeDtypeStruct(sh, a.dtype) for sh, a in zip(shapes, arrs)],
                          scratch_shapes=[pltpu.SemaphoreType.DMA((n,))] * 2)(*arrs)


def _row_tile(rows, width):
    return _tile(rows, max(16, (784 * LANES // width) // 16 * 16), 16)


def _add_halves(g, got, dtype, name):
    n, _, hr, w = g.shape
    tr = _row_tile(hr, w)

    def body(g_ref, got_ref, o_ref):
        c = lax.axis_index("c")
        own = jnp.where(c == 0, g_ref[:, 0], g_ref[:, 1])
        o_ref[...] = (own + got_ref[...]).astype(dtype)

    return pl.pallas_call(
        body, name=name, grid=(hr // tr,),
        in_specs=[pl.BlockSpec((n, 2, tr, w), lambda i: (0, 0, i, 0)), pl.BlockSpec((n, tr, w), lambda i: (0, i, 0))],
        out_specs=pl.BlockSpec((n, tr, w), lambda i: (0, i, 0)),
        out_shape=jax.ShapeDtypeStruct((n, hr, w), dtype))(g, got)


def _scatter_chips(gs, name):
    n = len(gs)

    def body(*refs):
        g_refs, out_refs = refs[:n], refs[n:2 * n]
        send, recv = refs[2 * n:]
        x, y, c = _position()
        sends = []
        for i in range(n):
            for j, (fx, fy) in enumerate(_CHIP_FLIPS):
                px, py = _flip(x, fx), _flip(y, fy)
                cp = pltpu.make_async_remote_copy(src_ref=g_refs[i].at[2 * px + py], dst_ref=out_refs[i].at[j],
                                                  send_sem=send.at[i, j], recv_sem=recv.at[i, j],
                                                  device_id=(px, py, c), device_id_type=MESH)
                cp.start()
                sends.append(cp)
        for cp in sends:
            cp.wait_recv()
        for cp in sends:
            cp.wait_send()

    return pl.pallas_call(
        body, name=name, in_specs=[ANY] * n, out_specs=[ANY] * n,
        out_shape=[jax.ShapeDtypeStruct((3,) + g.shape[1:], g.dtype) for g in gs],
        scratch_shapes=[pltpu.SemaphoreType.DMA((n, 3)), pltpu.SemaphoreType.DMA((n, 3))],
    )(*gs)


def _sum_own_and_slots(own, got, name):
    n, r, w = own.shape
    tr = _row_tile(r, w)

    def body(own_ref, got_ref, o_ref):
        me = 2 * lax.axis_index("x") + lax.axis_index("y")
        acc = own_ref[0]
        for i in range(1, n):
            acc = jnp.where(me == i, own_ref[i], acc)
        acc = acc.astype(f32)
        for j in range(3):
            acc = acc + got_ref[j].astype(f32)
        o_ref[...] = acc

    return pl.pallas_call(
        body, name=name, grid=(r // tr,),
        in_specs=[pl.BlockSpec((n, tr, w), lambda i: (0, i, 0)), pl.BlockSpec((3, tr, w), lambda i: (0, i, 0))],
        out_specs=pl.BlockSpec((tr, w), lambda i: (i, 0)), out_shape=jax.ShapeDtypeStruct((r, w), f32))(own, got)


def _share_chips(a, name):
    def body(a_ref, out_ref, send, recv):
        x, y, c = _position()
        sends = []
        for j, (fx, fy) in enumerate(_CHIP_FLIPS):
            cp = pltpu.make_async_remote_copy(src_ref=a_ref, dst_ref=out_ref.at[j], send_sem=send.at[j],
                                              recv_sem=recv.at[j], device_id=(_flip(x, fx), _flip(y, fy), c),
                                              device_id_type=MESH)
            cp.start()
            sends.append(cp)
        for cp in sends:
            cp.wait_recv()
        for cp in sends:
            cp.wait_send()

    return pl.pallas_call(
        body, name=name, in_specs=[ANY], out_specs=ANY, out_shape=jax.ShapeDtypeStruct((3,) + a.shape, a.dtype),
        scratch_shapes=[pltpu.SemaphoreType.DMA((3,)), pltpu.SemaphoreType.DMA((3,))],
    )(a)


def _sum_in_chip_order(pair, got, name):
    r, w = pair.shape
    tr = _tile(r, 1408, 8)

    def body(p_ref, g_ref, o_ref):
        x, y = lax.axis_index("x"), lax.axis_index("y")
        me = 2 * x + y
        across = [2 * _flip(x, fx) + _flip(y, fy) for fx, fy in _CHIP_FLIPS]
        acc = None
        for i in range(N_CHIPS):
            term = p_ref[...]
            for j in range(3):
                term = jnp.where(across[j] == i, g_ref[j], term)
            acc = term if acc is None else acc + term
        o_ref[...] = acc

    return pl.pallas_call(
        body, name=name, grid=(r // tr,),
        in_specs=[pl.BlockSpec((tr, w), lambda i: (i, 0)), pl.BlockSpec((3, tr, w), lambda i: (0, i, 0))],
        out_specs=pl.BlockSpec((tr, w), lambda i: (i, 0)), out_shape=jax.ShapeDtypeStruct((r, w), f32))(pair, got)


def _add2(a, b, name):
    r, w = a.shape
    tr = _tile(r, 1408, 8)
    spec = pl.BlockSpec((tr, w), lambda i: (i, 0))

    def body(a_ref, b_ref, o_ref):
        o_ref[...] = a_ref[...] + b_ref[...]

    return pl.pallas_call(body, name=name, grid=(r // tr,), in_specs=[spec, spec], out_specs=spec,
                          out_shape=jax.ShapeDtypeStruct(a.shape, f32))(a, b)


def _adamw(w, g_parts, m, v, name):
    shape = w.shape
    view = shape if len(shape) >= 2 else (1,) + shape
    assert all(d == 1 for d in view[:-2]), shape
    rows, cols = view[-2:]
    cap = max(8, (256 * 1024 // cols) // 8 * 8)
    tr = rows if rows <= cap else _tile(rows, cap, 8)
    lead = len(view) - 2
    n_g = len(g_parts)

    def body(*refs):
        w_ref = refs[0]
        g_refs = refs[1:1 + n_g]
        m_ref, v_ref, g_out, d_out, m_out, v_out = refs[1 + n_g:]
        g = g_refs[0][...]
        for gr in g_refs[1:]:
            g = g + gr[...]
        m_new = ADAM_B1 * m_ref[...] + (1.0 - ADAM_B1) * g
        v_new = ADAM_B2 * v_ref[...] + (1.0 - ADAM_B2) * (g * g)
        m_hat = m_new / (1.0 - ADAM_B1 ** ADAM_STEP)
        v_hat = v_new / (1.0 - ADAM_B2 ** ADAM_STEP)
        g_out[...] = g
        d_out[...] = -ADAM_LR * (m_hat / (jnp.sqrt(v_hat) + ADAM_EPS) + ADAM_WD * w_ref[...])
        m_out[...] = m_new
        v_out[...] = v_new

    spec = pl.BlockSpec((None,) * lead + (tr, cols), lambda i: (0,) * lead + (i, 0))
    args = [w.reshape(view)] + [g.reshape(view) for g in g_parts] + [m.reshape(view), v.reshape(view)]
    outs = pl.pallas_call(body, name=name, grid=(rows // tr,), in_specs=[spec] * len(args), out_specs=[spec] * 4,
                          out_shape=[jax.ShapeDtypeStruct(view, f32)] * 4)(*args)
    return [o.reshape(shape) for o in outs]


_BIG = ('ffn1_w_gu', 'ffn1_w_down', 'w_in', 'w_out', 'ffn2_w_gu', 'ffn2_w_down')
_SMALL_SHARDED = ('meta_tokens', 'a_conv_w', 'b_w_up', 'b_a_up', 'b_g_up')
_WEIGHTS = ('meta_tokens', 'ffn1_norm', 'ffn1_w_gu', 'ffn1_w_down', 'mix_norm', 'w_in', 'a_conv_w', 'a_log_rate',
            'a_dt_bias', 'a_out_norm', 'b_shift_mu', 'b_w0', 'b_w_up', 'b_a0', 'b_a_up', 'b_g_up', 'b_k_k', 'b_k_a',
            'b_r_k', 'b_ln_gain', 'b_ln_bias', 'w_out', 'ffn2_norm', 'ffn2_w_gu', 'ffn2_w_down', 'final_norm')
_SMALL = tuple(n for n in _WEIGHTS if n not in _BIG)


def _rows_of(shape):
    n = 1
    for d in shape:
        n *= d
    return n, -(-n // LANES)


def _pack(arrs, dtype, row_mult=32):
    parts, total = [], 0
    for a in arrs:
        n, rows = _rows_of(a.shape)
        flat = a.reshape(-1).astype(dtype)
        if n % LANES:
            flat = jnp.pad(flat, (0, rows * LANES - n))
        parts.append(flat)
        total += rows
    extra = -total % row_mult
    if extra:
        parts.append(jnp.zeros((extra * LANES,), dtype))
    return jnp.concatenate(parts).reshape(total + extra, LANES)


def _unpack(packed, shapes, lead=()):
    out, off = [], 0
    for sh in shapes:
        n, rows = _rows_of(sh)
        seg = packed[..., off:off + rows, :]
        if n % LANES:
            seg = seg.reshape(lead + (-1,))[..., :n]
        out.append(seg.reshape(lead + tuple(sh)))
        off += rows
    return out


def _cols_from_shards(s):
    return jnp.concatenate([s[i] for i in range(N_CHIPS)], axis=-1)


def kernel(x, meta_tokens, ffn1_norm, ffn1_w_gu, ffn1_w_down, mix_norm, w_in, a_conv_w, a_log_rate, a_dt_bias, a_out_norm, b_shift_mu, b_w0, b_w_up, b_a0, b_a_up, b_g_up, b_k_k, b_k_a, b_r_k, b_ln_gain, b_ln_bias, w_out, ffn2_norm, ffn2_w_gu, ffn2_w_down, final_norm, loss_target, m_meta_tokens, m_ffn1_norm, m_ffn1_w_gu, m_ffn1_w_down, m_mix_norm, m_w_in, m_a_conv_w, m_a_log_rate, m_a_dt_bias, m_a_out_norm, m_b_shift_mu, m_b_w0, m_b_w_up, m_b_a0, m_b_a_up, m_b_g_up, m_b_k_k, m_b_k_a, m_b_r_k, m_b_ln_gain, m_b_ln_bias, m_w_out, m_ffn2_norm, m_ffn2_w_gu, m_ffn2_w_down, m_final_norm, v_meta_tokens, v_ffn1_norm, v_ffn1_w_gu, v_ffn1_w_down, v_mix_norm, v_w_in, v_a_conv_w, v_a_log_rate, v_a_dt_bias, v_a_out_norm, v_b_shift_mu, v_b_w0, v_b_w_up, v_b_a0, v_b_a_up, v_b_g_up, v_b_k_k, v_b_k_a, v_b_r_k, v_b_ln_gain, v_b_ln_bias, v_w_out, v_ffn2_norm, v_ffn2_w_gu, v_ffn2_w_down, v_final_norm):
    args = locals()
    wts = {n: args[n] for n in _WEIGHTS}
    mom = {n: args["m_" + n] for n in _WEIGHTS}
    var = {n: args["v_" + n] for n in _WEIGHTS}
    chip = 2 * lax.axis_index("x") + lax.axis_index("y")

    big_shapes = [wts[n].shape[1:] for n in _BIG]
    small_shapes = [wts[n].shape[-2:] for n in _SMALL_SHARDED]
    big_flat = [wts[n].astype(bf16).reshape(wts[n].shape[1:]) for n in _BIG]
    small_packed = _pack([wts[n] for n in _SMALL_SHARDED], f32)
    gathered = _gather_chips(big_flat + [small_packed], "gather_weights")
    gu1, dn1, w_in_s, w_out_s, gu2, dn2 = [a.reshape((N_CHIPS,) + tuple(sh)) for a, sh in zip(gathered, big_shapes)]
    meta_s, conv_s, wup_s, aup_s, gup_s = _unpack(gathered[-1], small_shapes, (N_CHIPS,))
    w = {
        'ffn1_norm': ffn1_norm, 'mix_norm': mix_norm, 'ffn2_norm': ffn2_norm, 'final_norm': final_norm[None, :],
        'ffn1_wgu': gu1, 'ffn1_wd': dn1.reshape(D_FF, D), 'ffn2_wgu': gu2, 'ffn2_wd': dn2.reshape(D_FF, D),
        'w_in_p': _win_shards_to_padded(w_in_s), 'w_out': w_out_s.reshape(D, D),
        'a_conv_w': _cols_from_shards(conv_s), 'b_w_up': _cols_from_shards(wup_s), 'b_a_up': _cols_from_shards(aup_s),
        'b_g_up': _cols_from_shards(gup_s),
        'a_log_rate': a_log_rate, 'a_dt_bias': a_dt_bias, 'a_out_norm': a_out_norm, 'b_shift_mu': b_shift_mu,
        'b_w0': b_w0, 'b_a0': b_a0, 'b_k_k': b_k_k, 'b_k_a': b_k_a, 'b_r_k': b_r_k, 'b_ln_gain': b_ln_gain,
        'b_ln_bias': b_ln_bias,
    }
    meta_full = _cols_from_shards(meta_s)

    h0 = jnp.concatenate([jnp.zeros((PAD, D), f32), meta_full, x[0]], axis=0)
    tgt = jnp.concatenate([jnp.zeros((SKIP, D), f32), loss_target[0]], axis=0)
    loss_local, d_h0, g = _local_step(h0, tgt, w)
    loss = lax.psum(loss_local, ("x", "y", "c"))
    grad_x = d_h0[SKIP:][None]

    big_grads = [
        g['ffn1_wgu'],
        g['ffn1_wd'].reshape(N_CHIPS, D_FF // N_CHIPS, D),
        _win_padded_to_shards(g['w_in_p']),
        g['w_out'].reshape(N_CHIPS, D // N_CHIPS, D),
        g['ffn2_wgu'],
        g['ffn2_wd'].reshape(N_CHIPS, D_FF // N_CHIPS, D),
    ]
    g_halves = [a.reshape(N_CHIPS, 2, a.shape[1] // 2, a.shape[2]) for a in big_grads]
    sib_halves = _swap_sibling(g_halves, lambda ref, c: ref.at[:, 1 - c], [a.shape[:1] + a.shape[2:] for a in g_halves],
                               "swap_halves")
    chip_halves = [_add_halves(a, b, bf16, f"add_sibling{i}") for i, (a, b) in enumerate(zip(g_halves, sib_halves))]
    got = _scatter_chips(chip_halves, "scatter_grads")
    mine = [_sum_own_and_slots(a, b, f"sum_chips{i}") for i, (a, b) in enumerate(zip(chip_halves, got))]
    theirs = _swap_sibling(mine, lambda ref, c: ref, [a.shape for a in mine], "swap_sums")
    core = lax.axis_index("c")
    big_parts = [jnp.concatenate([jnp.where(core == 0, a, b), jnp.where(core == 0, b, a)], axis=0)
                 for a, b in zip(mine, theirs)]

    small_full = {
        'meta_tokens': d_h0[PAD:SKIP], 'ffn1_norm': g['ffn1_norm'], 'mix_norm': g['mix_norm'], 'a_conv_w': g['a_conv_w'],
        'a_log_rate': g['a_log_rate'], 'a_dt_bias': g['a_dt_bias'], 'a_out_norm': g['a_out_norm'],
        'b_shift_mu': g['b_shift_mu'], 'b_w0': g['b_w0'], 'b_w_up': g['b_w_up'], 'b_a0': g['b_a0'], 'b_a_up': g['b_a_up'],
        'b_g_up': g['b_g_up'], 'b_k_k': g['b_k_k'], 'b_k_a': g['b_k_a'], 'b_r_k': g['b_r_k'], 'b_ln_gain': g['b_ln_gain'],
        'b_ln_bias': g['b_ln_bias'], 'ffn2_norm': g['ffn2_norm'], 'final_norm': g['final_norm'],
    }
    s_shapes = [small_full[n].shape for n in _SMALL]
    s_packed = _pack([small_full[n] for n in _SMALL], f32, row_mult=256)
    (s_sib,) = _swap_sibling([s_packed], lambda ref, c: ref, [s_packed.shape], "swap_small")
    s_pair = _add2(s_packed, s_sib, "add_small")
    s_sum = _sum_in_chip_order(s_pair, _share_chips(s_pair, "share_small"), "sum_small")
    s_parts = dict(zip(_SMALL, _unpack(s_sum, s_shapes)))

    grad, delta, new_m, new_v = {}, {}, {}, {}
    for n, a in zip(_BIG, big_parts):
        grad[n], delta[n], new_m[n], new_v[n] = _adamw(wts[n], [a.reshape(wts[n].shape)], mom[n], var[n], f"adamw_{n}")
    for n in _SMALL:
        gs = s_parts[n]
        if n in _SMALL_SHARDED:
            width = wts[n].shape[-1]
            gs = lax.dynamic_slice_in_dim(gs, chip * width, width, axis=gs.ndim - 1)
        gs = gs.reshape(wts[n].shape)
        grad[n], delta[n], new_m[n], new_v[n] = _adamw(wts[n], [gs], mom[n], var[n], f"adamw_{n}")

    return (loss, grad_x, *[grad[n] for n in _WEIGHTS], *[delta[n] for n in _WEIGHTS],
            *[new_m[n] for n in _WEIGHTS], *[new_v[n] for n in _WEIGHTS])
```

```python
import functools

import jax
import jax.numpy as jnp
from jax import lax
from jax.experimental import pallas as pl
from jax.experimental.pallas import tpu as pltpu

f32 = jnp.float32
bf16 = jnp.bfloat16
HI = lax.Precision.HIGHEST
MESH = pl.DeviceIdType.MESH
ANY = pl.BlockSpec(memory_space=pl.ANY)

D = 1024
N_META = 16
CHUNK = 64
PAD = CHUNK - N_META
SKIP = PAD + N_META
EPS = 1e-6
D_FF = 2816
A_HEADS = 8
A_DK = 128
B_HEADS = 16
B_N = 64
B_GN_EPS = B_N * 1e-5
W_LORA, AA_LORA, G_LORA = 64, 64, 160
IN_TOTAL = 9520
ZP = 9600
LANES = 128
N_CHIPS = 4
N_DEV = 8

ADAM_LR, ADAM_B1, ADAM_B2, ADAM_EPS, ADAM_WD, ADAM_STEP = 0.001, 0.9, 0.999, 1e-08, 0.01, 10

MXU_DTYPE = bf16


def _tile(n, cap, mult):
    if n <= cap:
        return n
    best = None
    for t in range(mult, cap + 1, mult):
        if n % t == 0:
            best = t
    assert best is not None, (n, cap, mult)
    return best


def _sigmoid(x):
    return jax.nn.sigmoid(x)


def _silu(x):
    return x * jax.nn.sigmoid(x)


def _softplus(x):
    return jnp.maximum(x, 0.0) + jnp.log(1.0 + jnp.exp(-jnp.abs(x)))


def _head_matrix(c, nh):
    hd = c // nh
    r = lax.broadcasted_iota(jnp.int32, (c, nh), 0)
    h = lax.broadcasted_iota(jnp.int32, (c, nh), 1)
    return (r >= h * hd) & (r < (h + 1) * hd)


def _dot_exact_rhs(x, e, cb):
    dn = (((1,), (cb,)), ((), ()))
    if SCAN_PASSES == 0:
        return lax.dot_general(x, e.astype(f32), dn, precision=HI, preferred_element_type=f32)
    eb = e.astype(bf16)
    hi = x.astype(bf16)
    lo = (x - hi.astype(f32)).astype(bf16)
    return (lax.dot_general(hi, eb, dn, preferred_element_type=f32)
            + lax.dot_general(lo, eb, dn, preferred_element_type=f32))


def _head_sum_impl(x, nh):
    e = _head_matrix(x.shape[-1], nh)
    return _dot_exact_rhs(_dot_exact_rhs(x, e, 0), e, 1)


@functools.partial(jax.custom_vjp, nondiff_argnums=(1,))
def _head_sum(x, nh):
    return _head_sum_impl(x, nh)


def _head_sum_fwd(x, nh):
    return _head_sum_impl(x, nh), None


def _head_sum_bwd(nh, _, g):
    return (_head_sum_impl(g, nh),)


_head_sum.defvjp(_head_sum_fwd, _head_sum_bwd)


@functools.partial(jax.custom_vjp, nondiff_argnums=(1,))
def _shift_rows(x, s):
    n = x.shape[0]
    row = lax.broadcasted_iota(jnp.int32, x.shape, 0)
    if s > 0:
        return jnp.where(row >= s, pltpu.roll(x, s, 0), 0.0)
    return jnp.where(row < n + s, pltpu.roll(x, n + s, 0), 0.0)


def _shift_rows_fwd(x, s):
    return _shift_rows(x, s), None


def _shift_rows_bwd(s, _, g):
    return (_shift_rows(g, -s),)


_shift_rows.defvjp(_shift_rows_fwd, _shift_rows_bwd)


def _matmul(a, b, *, ta=False, tb=False, res=None, scale=1.0, name, b_cols_split=None, out_cols_split=None,
            out_into=None):
    assert not (ta and tb)
    (ar, ac) = a.shape
    b0 = 0
    if b_cols_split:
        b0, bs = b_cols_split
        _, br, bc_part = b.shape
        bc = bs * bc_part
    else:
        br, bc = b.shape
    m, k = (ac, ar) if ta else (ar, ac)
    n, kb = (br, bc) if tb else (bc, br)
    assert k == kb, (a.shape, b.shape, ta, tb)
    tm = _tile(m, 1408, LANES) if ta else _tile(m, 832, 8)
    tn = _tile(n, 1408, LANES)
    tk = _tile(k, 1040, 8) if ta else _tile(k, 1408, LANES)
    nk = k // tk
    dn = (((0 if ta else 1,), (1 if tb else 0,)), ((), ()))
    if b_cols_split:
        assert (tk if tb else tn) == bc_part, (b.shape, tn, tk)

    def body(*refs):
        a_ref, b_ref = refs[:2]
        r_ref = refs[2] if res is not None else None
        o_ref, acc = refs[-2:]
        kk = pl.program_id(2)

        @pl.when(kk == 0)
        def _():
            acc[...] = jnp.zeros_like(acc)

        acc[...] += lax.dot_general(a_ref[...].astype(MXU_DTYPE), b_ref[...].astype(MXU_DTYPE), dn,
                                    preferred_element_type=f32,
                                    precision=None if MXU_DTYPE == bf16 else HI)

        @pl.when(kk == nk - 1)
        def _():
            out = acc[...]
            if scale != 1.0:
                out = out * scale
            if res is not None:
                out = r_ref[...] + out
            o_ref[...] = out

    if ta:
        a_spec = pl.BlockSpec((tk, tm), lambda i, j, kk: (kk, i))
    else:
        a_spec = pl.BlockSpec((tm, tk), lambda i, j, kk: (i, kk))
    if tb and b_cols_split:
        b_spec = pl.BlockSpec((None, tn, tk), lambda i, j, kk: (kk + b0, j, 0))
    elif tb:
        b_spec = pl.BlockSpec((tn, tk), lambda i, j, kk: (j, kk))
    elif b_cols_split:
        b_spec = pl.BlockSpec((None, tk, tn), lambda i, j, kk: (j + b0, kk, 0))
    else:
        b_spec = pl.BlockSpec((tk, tn), lambda i, j, kk: (kk, j))
    in_specs = [a_spec, b_spec]
    args = [a, b]
    if res is not None:
        in_specs.append(pl.BlockSpec((tm, tn), lambda i, j, kk: (i, j)))
        args.append(res)
    aliases = {}
    if out_cols_split:
        o0, total = out_cols_split
        out_spec = pl.BlockSpec((None, tm, tn), lambda i, j, kk: (j + o0, i, 0))
        out_shape = jax.ShapeDtypeStruct((total, m, tn), f32)
        if out_into is not None:
            assert out_into.shape == out_shape.shape
            in_specs.append(ANY)
            args.append(out_into)
            aliases = {len(args) - 1: 0}
    else:
        out_spec = pl.BlockSpec((tm, tn), lambda i, j, kk: (i, j))
        out_shape = jax.ShapeDtypeStruct((m, n), f32)
    return pl.pallas_call(
        body, name=name, grid=(m // tm, n // tn, nk), in_specs=in_specs, out_specs=out_spec, out_shape=out_shape,
        scratch_shapes=[pltpu.VMEM((tm, tn), f32)], input_output_aliases=aliases,
        compiler_params=pltpu.CompilerParams(dimension_semantics=("parallel", "parallel", "arbitrary")),
    )(*args)


def _tw_fwd(fn, ins, in_specs, out_shapes, out_specs, grid, name, with_pid=False):
    n_in = len(ins)

    def body(*refs):
        vals = [r[...] for r in refs[:n_in]]
        outs = fn(pl.program_id(0), *vals) if with_pid else fn(*vals)
        for r, o in zip(refs[n_in:], outs):
            r[...] = o.astype(r.dtype)

    return pl.pallas_call(body, name=name, grid=grid, in_specs=in_specs, out_specs=out_specs,
                          out_shape=out_shapes)(*ins)


def _tw_bwd(fn, ins, in_specs, cts, ct_specs, kinds, grid, name, with_pid=False, tile_dtype=f32, ct_extra=(),
            residual=None):
    n_in, n_ct = len(ins), len(cts)
    diff = [i for i, kd in enumerate(kinds) if kd is not None]
    n_ex = len(ct_extra)

    def body(*refs):
        vals = [r[...] for r in refs[:n_in]]
        ctv = [r[...].astype(f32) for r in refs[n_in:n_in + n_ct]]
        for (ci, _), r in zip(ct_extra, refs[n_in + n_ct:n_in + n_ct + n_ex]):
            ctv[ci] = ctv[ci] + r[...]
        ctv = tuple(ctv)
        n_fixed = n_in + n_ct + n_ex
        res_ref = refs[n_fixed] if residual is not None else None
        g_refs = refs[n_fixed + (residual is not None):]
        pid = pl.program_id(0)

        def f(*dv):
            full = list(vals)
            for i, v in zip(diff, dv):
                full[i] = v
            out = fn(pid, *full) if with_pid else fn(*full)
            return tuple(out)

        _, vjp = jax.vjp(f, *[vals[i] for i in diff])
        gs = vjp(ctv)
        first = pid == 0
        for i2 in range(1, len(grid)):
            first = first & (pl.program_id(i2) == 0)
        for i, g, g_ref in zip(diff, gs, g_refs):
            if kinds[i] != 'acc':
                if i == 0 and res_ref is not None:
                    g = res_ref[...] + g
                g_ref[...] = g.astype(g_ref.dtype)
            else:
                @pl.when(first)
                def _(g=g, g_ref=g_ref):
                    g_ref[...] = g

                @pl.when(jnp.logical_not(first))
                def _(g=g, g_ref=g_ref):
                    g_ref[...] += g

    zero_map = {1: lambda *a: (0,), 2: lambda *a: (0, 0), 3: lambda *a: (0, 0, 0)}
    out_specs, out_shapes = [], []
    for i in diff:
        if kinds[i] == 'tile':
            out_shapes.append(jax.ShapeDtypeStruct(ins[i].shape, tile_dtype))
            out_specs.append(in_specs[i])
        elif kinds[i] == 'acc':
            out_shapes.append(jax.ShapeDtypeStruct(ins[i].shape, f32))
            out_specs.append(pl.BlockSpec(ins[i].shape, zero_map[ins[i].ndim]))
        else:
            out_shapes.append(jax.ShapeDtypeStruct(kinds[i][1], kinds[i][3] if len(kinds[i]) > 3 else tile_dtype))
            out_specs.append(kinds[i][2])
    extra_specs = [ct_specs[ci] for ci, _ in ct_extra]
    extra = [a for _, a in ct_extra]
    if residual is not None:
        assert kinds[0] == 'tile'
        extra_specs.append(in_specs[0])
        extra.append(residual)
    return pl.pallas_call(body, name=name, grid=grid, in_specs=list(in_specs) + list(ct_specs) + extra_specs,
                          out_specs=out_specs, out_shape=out_shapes)(*ins, *cts, *extra)


def _row_spec(tm, c, col_block=0):
    return pl.BlockSpec((tm, c), lambda i, cb=col_block: (i, cb))


def _full_spec(shape):
    nd = len(shape)
    return pl.BlockSpec(shape, lambda *a, nd=nd: (0,) * nd)


def _f_rms(x, g):
    return (x * lax.rsqrt(jnp.mean(x * x, axis=-1, keepdims=True) + EPS) * g,)


def _f_swiglu(gate, up):
    return (_silu(gate) * up,)


def _f_loss(pid, h, g, tgt, *, tm):
    y = h * lax.rsqrt(jnp.mean(h * h, axis=-1, keepdims=True) + EPS) * g
    row = pid * tm + lax.broadcasted_iota(jnp.int32, (tm, 1), 0)
    err = jnp.where(row >= SKIP, y - tgt, 0.0)
    per_row = jnp.mean(err * err, axis=-1, keepdims=True)
    return (0.5 * jnp.sum(per_row, axis=0, keepdims=True),)


def _f_conv(x, w, *, norm, scale):
    y = x * w[3:4, :]
    for s in (1, 2, 3):
        y = y + _shift_rows(x, s) * w[3 - s:4 - s, :]
    y = _silu(y)
    if norm:
        y = y * lax.rsqrt(jnp.sum(y * y, axis=-1, keepdims=True) + 1e-6) * scale
    return (y,)


def _f_dgates(pid, abeta, aalpha, log_rate, dt_bias, *, tm):
    row = pid * tm + lax.broadcasted_iota(jnp.int32, (tm, 1), 0)
    live = row >= PAD
    beta = jnp.where(live, _sigmoid(abeta), 0.0)
    g = jnp.where(live, -jnp.exp(log_rate) * _softplus(aalpha + dt_bias), 0.0)
    return beta, g


def _f_tshift(z, mu):
    return (z + (_shift_rows(z, 1) - z) * mu,)


def _f_rwkv_pre(k, wd, ad, gd, w0, w_up, a0, a_up, g_up, k_k, k_a):
    w_log = -_softplus(-(w0 + _smm(jnp.tanh(wd), w_up, 1))) - 0.5
    lw = -jnp.exp(w_log)
    a_lr = _sigmoid(a0 + _smm(ad, a_up, 1))
    gate = _smm(_sigmoid(gd), g_up, 1)
    kkp = k * k_k
    kk = kkp * lax.rsqrt(_head_sum(kkp * kkp, B_HEADS) + 1e-6)
    kmod = k * (1.0 + (a_lr - 1.0) * k_a)
    return lw, kmod, -kk, kk * a_lr, gate


def _f_mix_post(o, az, y, r, kmod, v, gate, ga, gb, out_gain, ln_g, ln_b, r_k):
    ms = _head_sum(o * o, A_HEADS) * (1.0 / A_DK)
    oa = o * lax.rsqrt(ms + EPS) * out_gain * _silu(az)
    mean = _head_sum(y, B_HEADS) * (1.0 / B_N)
    yc = y - mean
    var = _head_sum(yc * yc, B_HEADS) * (1.0 / B_N)
    yn = yc * lax.rsqrt(var + B_GN_EPS) * ln_g + ln_b
    bonus = _head_sum(r * kmod * r_k, B_HEADS) * v
    ob = (yn + bonus) * gate
    return (_sigmoid(ga) * oa + _sigmoid(gb) * ob,)


SCAN_PASSES = 3


def _split2(a):
    hi = a.astype(bf16)
    return hi, (a - hi.astype(f32)).astype(bf16)


def _dot_passes(a, b, ca, cb, passes):
    dn = (((ca,), (cb,)), ((), ()))
    if SCAN_PASSES == 0:
        return lax.dot_general(a, b, dn, precision=HI, preferred_element_type=f32)
    if passes == 1:
        return lax.dot_general(a.astype(bf16), b.astype(bf16), dn, preferred_element_type=f32)
    ah, al = _split2(a)
    bh, bl = _split2(b)
    return (lax.dot_general(ah, bh, dn, preferred_element_type=f32)
            + (lax.dot_general(ah, bl, dn, preferred_element_type=f32)
               + lax.dot_general(al, bh, dn, preferred_element_type=f32)))


@functools.partial(jax.custom_vjp, nondiff_argnums=(2, 3, 4))
def _sdot(a, b, ca, cb, passes):
    return _dot_passes(a, b, ca, cb, passes)


def _sdot_fwd(a, b, ca, cb, passes):
    return _dot_passes(a, b, ca, cb, passes), (a, b)


def _sdot_bwd(ca, cb, passes, res, g):
    a, b = res
    if (ca, cb) == (1, 0):
        return _dot_passes(g, b, 1, 1, passes), _dot_passes(a, g, 0, 0, passes)
    if (ca, cb) == (1, 1):
        return _dot_passes(g, b, 1, 0, passes), _dot_passes(g, a, 0, 0, passes)
    assert (ca, cb) == (0, 0)
    return _dot_passes(b, g, 1, 1, passes), _dot_passes(a, g, 1, 0, passes)


_sdot.defvjp(_sdot_fwd, _sdot_bwd)


def _smm(a, b, passes=3):
    return _sdot(a, b, 1, 0, passes)


def _smm_nt(a, b, passes=3):
    return _sdot(a, b, 1, 1, passes)


def _smm_tn(a, b, passes=3):
    return _sdot(a, b, 0, 0, passes)


def _tri_dot(x, ca):
    n = x.shape[0]
    incl = _tri_masks(n)[0]
    dn = (((ca,), (0,)), ((), ()))
    if SCAN_PASSES == 0:
        return lax.dot_general(incl.astype(f32), x, dn, precision=HI, preferred_element_type=f32)
    tri = incl.astype(bf16)
    hi, r1 = x.astype(bf16), None
    r1 = x - hi.astype(f32)
    mid = r1.astype(bf16)
    lo = (r1 - mid.astype(f32)).astype(bf16)
    return (lax.dot_general(tri, hi, dn, preferred_element_type=f32)
            + (lax.dot_general(tri, mid, dn, preferred_element_type=f32)
               + lax.dot_general(tri, lo, dn, preferred_element_type=f32)))


@jax.custom_vjp
def _cumsum_rows(x):
    return _tri_dot(x, 1)


def _cumsum_rows_fwd(x):
    return _tri_dot(x, 1), None


def _cumsum_rows_bwd(_, g):
    return (_tri_dot(g, 0),)


_cumsum_rows.defvjp(_cumsum_rows_fwd, _cumsum_rows_bwd)


def _tri_masks(n):
    i = lax.broadcasted_iota(jnp.int32, (n, n), 0)
    j = lax.broadcasted_iota(jnp.int32, (n, n), 1)
    return i >= j, i > j, i == j, i <= j


def _unit_lower_inv_impl(low, passes):
    n = low.shape[0]
    assert n == CHUNK
    _, _, eye, _ = _tri_masks(n)
    acc = eye.astype(f32) + low
    p = low
    for _ in range(5):
        p = _dot_passes(p, p, 1, 0, passes)
        acc = acc + _dot_passes(acc, p, 1, 0, passes)
    return acc


@functools.partial(jax.custom_vjp, nondiff_argnums=(1,))
def _unit_lower_inv(low, passes=3):
    return _unit_lower_inv_impl(low, passes)


def _unit_lower_inv_fwd(low, passes):
    t = _unit_lower_inv_impl(low, passes)
    return t, t


def _unit_lower_inv_bwd(passes, t, g):
    return (_dot_passes(_dot_passes(t, g, 0, 0, passes), t, 1, 1, passes),)


_unit_lower_inv.defvjp(_unit_lower_inv_fwd, _unit_lower_inv_bwd)

DELTA_PASSES = 1
DELTA_INV_PASSES = 1


def _delta_chunk(s, q, k, v, beta_row, g_row):
    p = DELTA_PASSES
    incl, strict, eye, upper = _tri_masks(CHUNK)
    beta = jnp.sum(jnp.where(eye, beta_row, 0.0), axis=1, keepdims=True)
    g = jnp.sum(jnp.where(eye, g_row, 0.0), axis=1, keepdims=True)
    gc = jnp.sum(jnp.where(incl, g_row, 0.0), axis=1, keepdims=True)
    gc_row = jnp.sum(jnp.where(upper, g, 0.0), axis=0, keepdims=True)
    decay = jnp.where(incl, jnp.exp(jnp.where(incl, gc - gc_row, 0.0)), 0.0)
    kb = k * beta
    vb = v * beta
    m = jnp.where(strict, _smm_nt(kb, k, p) * decay, 0.0)
    tinv = _unit_lower_inv(-m, DELTA_INV_PASSES)
    u = _smm(tinv, vb, p)
    wk = _smm(tinv, kb * jnp.exp(gc), p)
    attn = _smm_nt(q, k, p) * decay
    qg = q * jnp.exp(gc)
    g_last = jnp.sum(g, axis=0, keepdims=True)
    k_tail = k * jnp.exp(g_last - gc)
    v_new = u - _smm(wk, s, p)
    o = _smm(qg, s, p) + _smm(attn, v_new, p)
    s_new = s * jnp.exp(g_last) + _smm_tn(k_tail, v_new, p)
    return o, s_new


RWKV_PASSES = 1
RWKV_INV_PASSES = 1


def _rwkv_chunk(st, r, k, v, a, b, lw):
    c = CHUNK
    p, pi = RWKV_PASSES, RWKV_INV_PASSES
    _, strict, _, _ = _tri_masks(c)
    lane = lax.broadcasted_iota(jnp.int32, (c, 2 * B_N), 1)
    row = lax.broadcasted_iota(jnp.int32, (c, 2 * B_N), 0)
    first = lane < B_N
    incl2 = row >= jnp.where(first, lane, lane - B_N)
    bi = lax.broadcasted_iota(jnp.int32, (2 * B_N, 2 * B_N), 0) < B_N
    bj = lax.broadcasted_iota(jnp.int32, (2 * B_N, 2 * B_N), 1) < B_N
    blockdiag = bi == bj
    cum = _cumsum_rows(lw)
    e_pos = jnp.exp(cum)
    e_neg = jnp.exp(-cum)
    rt = r * e_pos
    at = a * jnp.exp(cum - lw)
    kt = k * e_neg
    bt = b * e_neg
    bk = jnp.concatenate([bt, kt], axis=0)
    a_s0 = _smm_nt(at, st, p)
    r_s0 = _smm_nt(rt, st, p)
    heads = (first, jnp.logical_not(first))
    u = jnp.zeros((c, 2 * B_N), f32)
    for sel in heads:
        at_h = jnp.where(sel, at, 0.0)
        ab = jnp.where(strict, _smm_nt(at_h, bt, pi), 0.0)
        ak = jnp.where(strict, _smm_nt(at_h, kt, p), 0.0)
        t_h = _unit_lower_inv(ab, pi)
        u = u + _smm(t_h, jnp.where(sel, a_s0, 0.0) + _smm(ak, jnp.where(sel, v, 0.0), p), p)
    y = r_s0
    for sel in heads:
        rbk = jnp.where(incl2, _smm_nt(jnp.where(sel, rt, 0.0), bk, p), 0.0)
        uv = jnp.concatenate([jnp.where(sel, u, 0.0), jnp.where(sel, v, 0.0)], axis=0)
        y = y + _smm(rbk, uv, p)
    cl = jnp.sum(lw, axis=0, keepdims=True)
    dec = jnp.exp(cl - cum)
    uv_all = jnp.concatenate([u, v], axis=0)
    bk_dec = jnp.concatenate([b * dec, k * dec], axis=0)
    st_new = st * jnp.exp(cl) + jnp.where(blockdiag, _smm_tn(uv_all, bk_dec, p), 0.0)
    return y, st_new


GROUPS_PER_STEP = 8


def _scan_specs(ins, col_offs, n_chunks, reverse):
    gw = GROUPS_PER_STEP * LANES
    cidx = (lambda c: n_chunks - 1 - c) if reverse else (lambda c: c)
    specs = []
    for a, off in zip(ins, col_offs):
        if a.ndim == 2:
            assert off % gw == 0
            specs.append(pl.BlockSpec((CHUNK, gw), lambda h, c, o=off // gw: (cidx(c), h + o)))
        else:
            specs.append(pl.BlockSpec((GROUPS_PER_STEP, None, 1, CHUNK), lambda h, c: (h, cidx(c), 0, 0)))
    return specs, cidx


def _group_vals(refs, g):
    return [r[:, g * LANES:(g + 1) * LANES] if len(r.shape) == 2 else r[g] for r in refs]


def _scan_fwd(chunk_fn, ins, col_offs, n_groups, n_chunks, state_shape, name):
    n_in = len(ins)
    gps = GROUPS_PER_STEP
    t = ins[0].shape[0]

    def body(*refs):
        in_refs = refs[:n_in]
        o_ref, s0_ref, st = refs[n_in:]

        @pl.when(pl.program_id(1) == 0)
        def _():
            st[...] = jnp.zeros_like(st)

        states = st[...]
        vals = [jnp.stack(col) for col in zip(*[_group_vals(in_refs, g) for g in range(gps)])]
        o, s_new = jax.vmap(chunk_fn)(states, *vals)
        s0_ref[...] = states
        st[...] = s_new
        for g in range(gps):
            o_ref[:, g * LANES:(g + 1) * LANES] = o[g]

    specs, _ = _scan_specs(ins, col_offs, n_chunks, False)
    return pl.pallas_call(
        body, name=name, grid=(n_groups // gps, n_chunks), in_specs=specs,
        out_specs=[pl.BlockSpec((CHUNK, gps * LANES), lambda h, c: (c, h)),
                   pl.BlockSpec((gps, None) + state_shape, lambda h, c: (h, c, 0, 0))],
        out_shape=[jax.ShapeDtypeStruct((t, n_groups * LANES), f32),
                   jax.ShapeDtypeStruct((n_groups, n_chunks) + state_shape, f32)],
        scratch_shapes=[pltpu.VMEM((gps,) + state_shape, f32)],
        compiler_params=pltpu.CompilerParams(dimension_semantics=("parallel", "arbitrary")),
    )(*ins)


def _scan_bwd(chunk_fn, s0s, ins, col_offs, d_out, n_groups, n_chunks, state_shape, name):
    n_in = len(ins)
    gps = GROUPS_PER_STEP
    t = d_out.shape[0]

    def body(*refs):
        s0_ref = refs[0]
        in_refs = refs[1:1 + n_in]
        do_ref = refs[1 + n_in]
        g_refs = refs[2 + n_in:2 + 2 * n_in]
        dst = refs[2 + 2 * n_in]

        @pl.when(pl.program_id(1) == 0)
        def _():
            dst[...] = jnp.zeros_like(dst)

        vals = [jnp.stack(col) for col in zip(*[_group_vals(in_refs, g) for g in range(gps)])]
        d_o = jnp.stack([do_ref[:, g * LANES:(g + 1) * LANES] for g in range(gps)])
        _, vjp = jax.vjp(jax.vmap(chunk_fn), s0_ref[...], *vals)
        gs = vjp((d_o, dst[...]))
        dst[...] = gs[0]
        for g_ref, gv in zip(g_refs, gs[1:]):
            if len(g_ref.shape) == 2:
                for g in range(gps):
                    g_ref[:, g * LANES:(g + 1) * LANES] = gv[g]
            else:
                g_ref[...] = gv

    specs, cidx = _scan_specs(ins, col_offs, n_chunks, True)
    out_lane = pl.BlockSpec((CHUNK, gps * LANES), lambda h, c: (cidx(c), h))
    g_specs = [out_lane if a.ndim == 2 else sp for a, sp in zip(ins, specs)]
    g_shapes = [(t, n_groups * LANES) if a.ndim == 2 else a.shape for a in ins]
    s0_spec = pl.BlockSpec((gps, None) + state_shape, lambda h, c: (h, cidx(c), 0, 0))
    return pl.pallas_call(
        body, name=name, grid=(n_groups // gps, n_chunks), in_specs=[s0_spec] + specs + [out_lane],
        out_specs=g_specs, out_shape=[jax.ShapeDtypeStruct(sh, f32) for sh in g_shapes],
        scratch_shapes=[pltpu.VMEM((gps,) + state_shape, f32)],
        compiler_params=pltpu.CompilerParams(dimension_semantics=("parallel", "arbitrary")),
    )(s0s, *ins, d_out)


def _rms_fwd(x, g, name):
    t = x.shape[0]
    tm = _tile(t, 416, 16)
    return _tw_fwd(_f_rms, [x, g], [_row_spec(tm, D), _full_spec(g.shape)],
                   [jax.ShapeDtypeStruct(x.shape, MXU_DTYPE)], [_row_spec(tm, D)], (t // tm,), name)[0]


def _rms_bwd(x, g, dy, residual, name):
    t = x.shape[0]
    tm = _tile(t, 416, 8)
    return _tw_bwd(_f_rms, [x, g], [_row_spec(tm, D), _full_spec(g.shape)], [dy], [_row_spec(tm, D)],
                   ['tile', 'acc'], (t // tm,), name, residual=residual)


def _ffn_fwd(h, gain, wgu, wd, tag):
    xn = _rms_fwd(h, gain, f"{tag}_rms")
    gate, up, act = _gate_up_act(xn, wgu, f"{tag}_gate_up")
    out = _matmul(act, wd, res=h, scale=0.5, name=f"{tag}_down")
    return out, (xn, gate, up, act)


def _mxu_dot(a, b, dn):
    return lax.dot_general(a.astype(MXU_DTYPE), b.astype(MXU_DTYPE), dn, preferred_element_type=f32,
                           precision=None if MXU_DTYPE == bf16 else HI)


def _gate_up_act(xn, wgu, name):
    t = xn.shape[0]
    wdt = wgu.shape[2]
    tm = _tile(t, 416, 16)
    dn = (((1,), (0,)), ((), ()))

    def body(x_ref, wg_ref, wu_ref, g_ref, u_ref, a_ref):
        x = x_ref[...]
        g = _mxu_dot(x, wg_ref[...], dn)
        u = _mxu_dot(x, wu_ref[...], dn)
        g_ref[...] = g
        u_ref[...] = u
        a_ref[...] = _f_swiglu(g, u)[0].astype(a_ref.dtype)

    out_spec = pl.BlockSpec((tm, wdt), lambda i, j: (i, j))
    return pl.pallas_call(
        body, name=name, grid=(t // tm, 2),
        in_specs=[pl.BlockSpec((tm, D), lambda i, j: (i, 0)), pl.BlockSpec((None, D, wdt), lambda i, j: (j, 0, 0)),
                  pl.BlockSpec((None, D, wdt), lambda i, j: (j + 2, 0, 0))],
        out_specs=[out_spec] * 3,
        out_shape=[jax.ShapeDtypeStruct((t, 2 * wdt), f32)] * 2 + [jax.ShapeDtypeStruct((t, 2 * wdt), MXU_DTYPE)],
        compiler_params=pltpu.CompilerParams(dimension_semantics=("parallel", "parallel")),
    )(xn, wgu, wgu)


def _d_gate_up(dout, wd, gate, up, name):
    t = dout.shape[0]
    wdt = D_FF // 2
    tm = _tile(t, 416, 16)
    dn = (((1,), (1,)), ((), ()))

    def body(do_ref, wd_ref, g_ref, u_ref, dg_ref, du_ref):
        d_act = 0.5 * _mxu_dot(do_ref[...], wd_ref[...], dn)
        _, vjp = jax.vjp(_f_swiglu, g_ref[...], u_ref[...])
        dg, du = vjp((d_act,))
        dg_ref[...] = dg.astype(dg_ref.dtype)
        du_ref[...] = du.astype(du_ref.dtype)

    spec = pl.BlockSpec((tm, wdt), lambda i, j: (i, j))
    return pl.pallas_call(
        body, name=name, grid=(t // tm, 2),
        in_specs=[pl.BlockSpec((tm, D), lambda i, j: (i, 0)), pl.BlockSpec((wdt, D), lambda i, j: (j, 0)), spec, spec],
        out_specs=[spec] * 2, out_shape=[jax.ShapeDtypeStruct((t, D_FF), MXU_DTYPE)] * 2,
        compiler_params=pltpu.CompilerParams(dimension_semantics=("parallel", "parallel")),
    )(dout, wd, gate, up)


def _ffn_bwd(h, gain, wgu, wd, saved, dout, tag):
    xn, gate, up, act = saved
    t = h.shape[0]
    d_wd = _matmul(act, dout, ta=True, scale=0.5, name=f"{tag}_dwd")
    d_gate, d_up = _d_gate_up(dout, wd, gate, up, f"{tag}_dact")
    d_wgu = _matmul(xn, d_gate, ta=True, out_cols_split=(0, N_CHIPS), name=f"{tag}_dwg")
    d_wgu = _matmul(xn, d_up, ta=True, out_cols_split=(2, N_CHIPS), out_into=d_wgu, name=f"{tag}_dwu")
    d_xn = _matmul(d_gate, wgu, tb=True, b_cols_split=(0, 2), name=f"{tag}_dxn_g")
    d_xn = _matmul(d_up, wgu, tb=True, b_cols_split=(2, 2), res=d_xn, name=f"{tag}_dxn_u")
    d_h, d_gain = _rms_bwd(h, gain, d_xn, dout, f"{tag}_drms")
    return d_h, d_gain, d_wgu, d_wd


def _col_spec(t, first_block):
    return pl.BlockSpec((t, LANES), lambda j, fb=first_block: (0, j + fb))


def _local_step(h0, tgt, w):
    t = h0.shape[0]
    assert t % CHUNK == 0
    nc = t // CHUNK
    grads = {}

    h1, ffn1_saved = _ffn_fwd(h0, w['ffn1_norm'], w['ffn1_wgu'], w['ffn1_wd'], "ffn1")
    u = _rms_fwd(h1, w['mix_norm'], "mix_rms")
    z = _matmul(u, w['w_in_p'], name="in_proj")
    zs = z[:, 9216:9216 + 304]
    abeta, aalpha = zs[:, 288:296], zs[:, 296:304]

    conv_w = w['a_conv_w']
    conv_fns = [functools.partial(_f_conv, norm=True, scale=A_DK ** -0.5),
                functools.partial(_f_conv, norm=True, scale=1.0),
                functools.partial(_f_conv, norm=False, scale=1.0)]
    qkv = []
    for idx, fn in enumerate(conv_fns):
        qkv.append(_tw_fwd(fn, [z, conv_w], [_col_spec(t, 8 * idx), pl.BlockSpec((4, LANES), lambda j, o=8 * idx: (0, j + o))],
                           [jax.ShapeDtypeStruct((t, D), f32)], [_col_spec(t, 0)], (A_HEADS,), f"a_conv{idx}")[0])
    aq, ak, av = qkv
    tmg = _tile(t, 1040, 8)
    dg_fn = functools.partial(_f_dgates, tm=tmg)
    dg_specs = [_row_spec(tmg, A_HEADS)] * 2 + [_full_spec((1, A_HEADS))] * 2
    beta, gdec = _tw_fwd(dg_fn, [abeta, aalpha, w['a_log_rate'], w['a_dt_bias']], dg_specs,
                         [jax.ShapeDtypeStruct((t, A_HEADS), f32)] * 2, [_row_spec(tmg, A_HEADS)] * 2, (t // tmg,),
                         "a_gates", with_pid=True)
    beta_h = beta.T.reshape(A_HEADS, nc, 1, CHUNK)
    gdec_h = gdec.T.reshape(A_HEADS, nc, 1, CHUNK)
    a_ins = [aq, ak, av, beta_h, gdec_h]
    a_offs = [0] * 5
    o_scan, a_s0 = _scan_fwd(_delta_chunk, a_ins, a_offs, A_HEADS, nc, (A_DK, A_DK), "a_scan")

    mu = w['b_shift_mu']
    mu_rkv, mu_s = mu[:, :3072], mu[:, 3072:]
    zf_rkv = _tw_fwd(_f_tshift, [z, mu_rkv], [_col_spec(t, 32), pl.BlockSpec((1, LANES), lambda j: (0, j))],
                     [jax.ShapeDtypeStruct((t, 3072), f32)], [_col_spec(t, 0)], (24,), "b_shift")[0]
    zs_b = zs[:, :288]
    zf_s = _tw_fwd(_f_tshift, [zs_b, mu_s], [_full_spec((t, 288)), _full_spec((1, 288))],
                   [jax.ShapeDtypeStruct((t, 288), f32)], [_full_spec((t, 288))], (1,), "b_shift_s")[0]
    wdf, adf, gdf = zf_s[:, 0:64], zf_s[:, 64:128], zf_s[:, 128:288]
    tmr = _tile(t, 160, 16)
    pre_params = [w['b_w0'], w['b_w_up'], w['b_a0'], w['b_a_up'], w['b_g_up'], w['b_k_k'], w['b_k_a']]
    pre_ins = [zf_rkv, wdf, adf, gdf] + pre_params
    pre_specs = ([_row_spec(tmr, D, 1), _row_spec(tmr, 64), _row_spec(tmr, 64), _row_spec(tmr, 160)]
                 + [_full_spec(p.shape) for p in pre_params])
    lw, kmod, a_s, b_s, bgate = _tw_fwd(_f_rwkv_pre, pre_ins, pre_specs, [jax.ShapeDtypeStruct((t, D), f32)] * 5,
                                        [_row_spec(tmr, D)] * 5, (t // tmr,), "b_pre")
    b_ins = [zf_rkv, kmod, zf_rkv, a_s, b_s, lw]
    b_offs = [0, 0, 2 * D, 0, 0, 0]
    y_scan, b_s0 = _scan_fwd(_rwkv_chunk, b_ins, b_offs, B_HEADS // 2, nc, (2 * B_N, 2 * B_N), "b_scan")

    out_gain_t = jnp.tile(w['a_out_norm'], (1, A_HEADS))
    r_k = w['b_r_k'].reshape(1, D)
    post_params = [out_gain_t, w['b_ln_gain'], w['b_ln_bias'], r_k]
    post_ins = [o_scan, z, y_scan, zf_rkv, kmod, zf_rkv, bgate, z, z] + post_params
    post_specs = ([_row_spec(tmr, D), _row_spec(tmr, D, 3), _row_spec(tmr, D), _row_spec(tmr, D, 0), _row_spec(tmr, D),
                   _row_spec(tmr, D, 2), _row_spec(tmr, D), _row_spec(tmr, D, 7), _row_spec(tmr, D, 8)]
                  + [_full_spec((1, D))] * 4)
    merged = _tw_fwd(_f_mix_post, post_ins, post_specs, [jax.ShapeDtypeStruct((t, D), MXU_DTYPE)],
                     [_row_spec(tmr, D)], (t // tmr,), "mix_post")[0]
    h2 = _matmul(merged, w['w_out'], res=h1, name="out_proj")
    h3, ffn2_saved = _ffn_fwd(h2, w['ffn2_norm'], w['ffn2_wgu'], w['ffn2_wd'], "ffn2")

    tml = _tile(t, 416, 8)
    fnorm = w['final_norm']
    loss_fn = functools.partial(_f_loss, tm=tml)
    loss_specs = [_row_spec(tml, D), _full_spec((1, D)), _row_spec(tml, D)]
    loss_parts, d_h3, grads['final_norm'] = _loss_and_grad(loss_fn, h3, fnorm, tgt, loss_specs, tml)
    loss = jnp.sum(loss_parts)

    d_h2, grads['ffn2_norm'], grads['ffn2_wgu'], grads['ffn2_wd'] = _ffn_bwd(
        h2, w['ffn2_norm'], w['ffn2_wgu'], w['ffn2_wd'], ffn2_saved, d_h3, "ffn2")
    grads['w_out'] = _matmul(merged, d_h2, ta=True, name="d_w_out")
    d_merged = _matmul(d_h2, w['w_out'], tb=True, name="d_merged")

    win = ('tile', (t, D), _row_spec(tmr, D))
    zwin = win + (MXU_DTYPE,)
    post_kinds = ['tile', zwin, 'tile', win, 'tile', win, 'tile', zwin, zwin] + ['acc'] * 4
    (d_o, d_az, d_y, d_r1, d_kmod1, d_v1, d_bgate, d_ga, d_gb,
     d_out_gain_t, grads['b_ln_gain'], grads['b_ln_bias'], d_r_k) = _tw_bwd(
        _f_mix_post, post_ins, post_specs, [d_merged], [_row_spec(tmr, D)], post_kinds, (t // tmr,), "mix_post_bwd")
    grads['a_out_norm'] = jnp.sum(d_out_gain_t.reshape(A_HEADS, A_DK), axis=0, keepdims=True)
    grads['b_r_k'] = d_r_k.reshape(1, B_HEADS, B_N)

    d_r2, d_kmod2, d_v2, d_as, d_bs, d_lw = _scan_bwd(_rwkv_chunk, b_s0, b_ins, b_offs, d_y, B_HEADS // 2, nc,
                                                      (2 * B_N, 2 * B_N), "b_scan_bwd")
    pre_kinds = [win] + ['tile'] * 3 + ['acc'] * 7
    pre_ct_specs = [_row_spec(tmr, D)] * 5
    (d_zf_k, d_wdf, d_adf, d_gdf, grads['b_w0'], grads['b_w_up'], grads['b_a0'], grads['b_a_up'], grads['b_g_up'],
     grads['b_k_k'], grads['b_k_a']) = _tw_bwd(
        _f_rwkv_pre, pre_ins, pre_specs, [d_lw, d_kmod1, d_as, d_bs, d_bgate], pre_ct_specs, pre_kinds, (t // tmr,),
        "b_pre_bwd", ct_extra=[(1, d_kmod2)])
    d_zb_rkv, d_mu_rkv = _shift_bwd3(z, mu_rkv, d_r1, d_r2, d_zf_k, d_v1, d_v2, t)
    d_zf_s = jnp.concatenate([d_wdf, d_adf, d_gdf], axis=1)
    d_zs_b, d_mu_s = _tw_bwd(_f_tshift, [zs_b, mu_s], [_full_spec((t, 288)), _full_spec((1, 288))], [d_zf_s],
                             [_full_spec((t, 288))], ['tile', 'tile'], (1,), "b_shift_s_bwd")
    grads['b_shift_mu'] = jnp.concatenate([d_mu_rkv, d_mu_s], axis=1)

    d_aq, d_ak, d_av, d_beta_h, d_g_h = _scan_bwd(_delta_chunk, a_s0, a_ins, a_offs, d_o, A_HEADS, nc, (A_DK, A_DK),
                                                  "a_scan_bwd")
    d_beta = d_beta_h.reshape(A_HEADS, t).T
    d_gdec = d_g_h.reshape(A_HEADS, t).T
    d_abeta, d_aalpha, grads['a_log_rate'], grads['a_dt_bias'] = _tw_bwd(
        dg_fn, [abeta, aalpha, w['a_log_rate'], w['a_dt_bias']], dg_specs, [d_beta, d_gdec],
        [_row_spec(tmg, A_HEADS)] * 2, ['tile', 'tile', 'acc', 'acc'], (t // tmg,), "a_gates_bwd", with_pid=True)
    d_zqkv, d_conv = [], []
    for idx, (fn, ct) in enumerate(zip(conv_fns, (d_aq, d_ak, d_av))):
        dz_i, dw_i = _conv_bwd(fn, z, conv_w, ct, idx, t)
        d_zqkv.append(dz_i)
        d_conv.append(dw_i)
    grads['a_conv_w'] = jnp.concatenate(d_conv, axis=1)

    d_small = jnp.concatenate([d_zs_b, d_abeta, d_aalpha, jnp.zeros((t, ZP - 9216 - 304), f32)], axis=1)
    d_small = lax.optimization_barrier(d_small.astype(MXU_DTYPE))
    d_z_parts = d_zqkv + [d_az, d_zb_rkv, d_ga, d_gb, d_small]
    d_z = jnp.concatenate([p.astype(MXU_DTYPE) for p in d_z_parts], axis=1)
    grads['w_in_p'] = _matmul(u, d_z, ta=True, name="d_w_in")
    d_u = _matmul(d_z, w['w_in_p'], tb=True, name="d_u")
    d_h1, grads['mix_norm'] = _rms_bwd(h1, w['mix_norm'], d_u, d_h2, "mix_drms")
    d_h0, grads['ffn1_norm'], grads['ffn1_wgu'], grads['ffn1_wd'] = _ffn_bwd(
        h0, w['ffn1_norm'], w['ffn1_wgu'], w['ffn1_wd'], ffn1_saved, d_h1, "ffn1")
    return loss, d_h0, grads


_WIN_SEGMENTS = ((0, 4096), (4112, 7184), (7472, 9520), (7184, 7472), (4096, 4112))


_WIN_SHARD = IN_TOTAL // N_CHIPS


def _win_pieces():
    pieces, pad_at = [], 0
    for a, b in _WIN_SEGMENTS:
        c = a
        while c < b:
            stop = min(b, (c // _WIN_SHARD + 1) * _WIN_SHARD)
            pieces.append((c, pad_at + c - a, stop - c))
            c = stop
        pad_at += b - a
    return pieces


def _win_shards_to_padded(shards):
    parts = [shards[c // _WIN_SHARD][:, c % _WIN_SHARD:c % _WIN_SHARD + n] for c, _, n in _win_pieces()]
    parts.append(jnp.zeros((shards.shape[1], ZP - IN_TOTAL), shards.dtype))
    return jnp.concatenate(parts, axis=1)


def _win_padded_to_shards(w_p):
    by_shard = [[] for _ in range(N_CHIPS)]
    for c, p, n in sorted(_win_pieces()):
        by_shard[c // _WIN_SHARD].append(w_p[:, p:p + n])
    return jnp.stack([jnp.concatenate(parts, axis=1) for parts in by_shard])


def _loss_and_grad(loss_fn, h, gain, tgt, specs, tm):
    t = h.shape[0]
    n = t // tm

    def body(h_ref, g_ref, t_ref, l_ref, dh_ref, dg_ref):
        pid = pl.program_id(0)
        tg = t_ref[...]
        (part,), vjp = jax.vjp(lambda a, b: loss_fn(pid, a, b, tg), h_ref[...], g_ref[...])
        dh, dg = vjp((jnp.ones_like(part),))
        l_ref[...] = part
        dh_ref[...] = dh

        @pl.when(pid == 0)
        def _():
            dg_ref[...] = dg

        @pl.when(pid != 0)
        def _():
            dg_ref[...] += dg

    return pl.pallas_call(
        body, name="loss", grid=(n,), in_specs=specs,
        out_specs=[pl.BlockSpec((None, 1, 1), lambda i: (i, 0, 0)), specs[0], _full_spec(gain.shape)],
        out_shape=[jax.ShapeDtypeStruct((n, 1, 1), f32), jax.ShapeDtypeStruct(h.shape, f32),
                   jax.ShapeDtypeStruct(gain.shape, f32)],
    )(h, gain, tgt)


def _shift_bwd3(z, mu, d_r1, d_r2, d_k, d_v1, d_v2, t):
    nb = D // LANES

    def body(z_ref, mu_ref, r1, r2, kk, v1, v2, dz_ref, dmu_ref):
        j = pl.program_id(0)
        ct = jnp.where(j < nb, r1[...] + r2[...], jnp.where(j < 2 * nb, kk[...], v1[...] + v2[...]))
        _, vjp = jax.vjp(lambda a, b: _f_tshift(a, b), z_ref[...], mu_ref[...])
        dz, dmu = vjp((ct,))
        dz_ref[...] = dz.astype(dz_ref.dtype)
        dmu_ref[...] = dmu

    def window(first):
        return pl.BlockSpec((t, LANES), lambda j, f=first: (0, jnp.clip(j - f * nb, 0, nb - 1)))

    return pl.pallas_call(
        body, name="b_shift_bwd", grid=(3 * nb,),
        in_specs=[_col_spec(t, 32), pl.BlockSpec((1, LANES), lambda j: (0, j)), window(0), window(0), window(1),
                  window(2), window(2)],
        out_specs=[_col_spec(t, 0), pl.BlockSpec((1, LANES), lambda j: (0, j))],
        out_shape=[jax.ShapeDtypeStruct((t, 3 * D), MXU_DTYPE), jax.ShapeDtypeStruct((1, 3 * D), f32)],
    )(z, mu, d_r1, d_r2, d_k, d_v1, d_v2)


def _conv_bwd(fn, z, conv_w, ct, idx, t):
    def body(z_ref, w_ref, ct_ref, dz_ref, dw_ref):
        _, vjp = jax.vjp(lambda a, b: fn(a, b), z_ref[...], w_ref[...])
        dz, dw = vjp((ct_ref[...],))
        dz_ref[...] = dz.astype(dz_ref.dtype)
        dw_ref[...] = dw

    return pl.pallas_call(
        body, name=f"a_conv{idx}_bwd", grid=(A_HEADS,),
        in_specs=[_col_spec(t, 8 * idx), pl.BlockSpec((4, LANES), lambda j, o=8 * idx: (0, j + o)), _col_spec(t, 0)],
        out_specs=[_col_spec(t, 0), pl.BlockSpec((4, LANES), lambda j: (0, j))],
        out_shape=[jax.ShapeDtypeStruct((t, D), MXU_DTYPE), jax.ShapeDtypeStruct((4, D), f32)],
    )(z, conv_w, ct)


def _position():
    return lax.axis_index("x"), lax.axis_index("y"), lax.axis_index("c")


def _flip(v, f):
    return 1 - v if f else v


_CHIP_FLIPS = ((1, 0), (0, 1), (1, 1))


def _gather_chips(arrs, name):
    n = len(arrs)
    assert all(a.shape[0] % 32 == 0 for a in arrs)
    arrs = [a.reshape(2, a.shape[0] // 2, a.shape[1]) for a in arrs]

    def body(*refs):
        ins, outs = refs[:n], refs[n:2 * n]
        send, recv, fsend, frecv, own = refs[2 * n:]
        x, y, c = _position()
        me = 2 * x + y
        sends, plan, owns = [], [], []
        for a in range(n):
            cp = pltpu.make_async_remote_copy(src_ref=ins[a], dst_ref=outs[a].at[me], send_sem=own.at[a, 0],
                                              recv_sem=own.at[a, 1], device_id=(x, y, 1 - c), device_id_type=MESH)
            cp.start()
            owns.append(cp)
            for j, (fx, fy) in enumerate(_CHIP_FLIPS):
                px, py = _flip(x, fx), _flip(y, fy)
                p = 2 * px + py
                cp = pltpu.make_async_remote_copy(src_ref=ins[a].at[c], dst_ref=outs[a].at[me, c],
                                                  send_sem=send.at[a, j], recv_sem=recv.at[a, j],
                                                  device_id=(px, py, c), device_id_type=MESH)
                cp.start()
                sends.append(cp)
                landed = pltpu.make_async_remote_copy(src_ref=ins[a].at[c], dst_ref=outs[a].at[p, c],
                                                      send_sem=send.at[a, j], recv_sem=recv.at[a, j],
                                                      device_id=(px, py, c), device_id_type=MESH)
                onward = pltpu.make_async_remote_copy(src_ref=outs[a].at[p, c], dst_ref=outs[a].at[p, c],
                                                      send_sem=fsend.at[a, j], recv_sem=frecv.at[a, j],
                                                      device_id=(x, y, 1 - c), device_id_type=MESH)
                from_sibling = pltpu.make_async_remote_copy(src_ref=outs[a].at[p, 1 - c], dst_ref=outs[a].at[p, 1 - c],
                                                            send_sem=fsend.at[a, j], recv_sem=frecv.at[a, j],
                                                            device_id=(x, y, 1 - c), device_id_type=MESH)
                plan.append((landed, onward, from_sibling))
        for landed, onward, _ in plan:
            landed.wait_recv()
            onward.start()
        for _, _, from_sibling in plan:
            from_sibling.wait_recv()
        for cp in sends:
            cp.wait_send()
        for _, onward, _ in plan:
            onward.wait_send()
        for cp in owns:
            cp.wait()

    sems = [pltpu.SemaphoreType.DMA((n, 3))] * 4 + [pltpu.SemaphoreType.DMA((n, 2))]
    outs = pl.pallas_call(
        body, name=name, in_specs=[ANY] * n, out_specs=[ANY] * n,
        out_shape=[jax.ShapeDtypeStruct((N_CHIPS,) + a.shape, a.dtype) for a in arrs], scratch_shapes=sems,
    )(*arrs)
    return [o.reshape(N_CHIPS, o.shape[1] * o.shape[2], o.shape[3]) for o in outs]


def _swap_sibling(arrs, src_of, shapes, name):
    n = len(arrs)

    def body(*refs):
        a_refs, got_refs = refs[:n], refs[n:2 * n]
        send, recv = refs[2 * n:]
        x, y, c = _position()
        copies = []
        for i in range(n):
            cp = pltpu.make_async_remote_copy(src_ref=src_of(a_refs[i], c), dst_ref=got_refs[i], send_sem=send.at[i],
                                              recv_sem=recv.at[i], device_id=(x, y, 1 - c), device_id_type=MESH)
            cp.start()
            copies.append(cp)
        for cp in copies:
            cp.wait()

    return pl.pallas_call(body, name=name, in_specs=[ANY] * n, out_specs=[ANY] * n,
                          out_shape=[jax.ShapeDtypeStruct(sh, a.dtype) for sh, a in zip(shapes, arrs)],
                          scratch_shapes=[pltpu.SemaphoreType.DMA((n,))] * 2)(*arrs)


def _row_tile(rows, width):
    return _tile(rows, max(16, (784 * LANES // width) // 16 * 16), 16)


def _add_halves(g, got, dtype, name):
    n, _, hr, w = g.shape
    tr = _row_tile(hr, w)

    def body(g_ref, got_ref, o_ref):
        c = lax.axis_index("c")
        own = jnp.where(c == 0, g_ref[:, 0], g_ref[:, 1])
        o_ref[...] = (own + got_ref[...]).astype(dtype)

    return pl.pallas_call(
        body, name=name, grid=(hr // tr,),
        in_specs=[pl.BlockSpec((n, 2, tr, w), lambda i: (0, 0, i, 0)), pl.BlockSpec((n, tr, w), lambda i: (0, i, 0))],
        out_specs=pl.BlockSpec((n, tr, w), lambda i: (0, i, 0)),
        out_shape=jax.ShapeDtypeStruct((n, hr, w), dtype))(g, got)


def _scatter_chips(gs, name):
    n = len(gs)

    def body(*refs):
        g_refs, out_refs = refs[:n], refs[n:2 * n]
        send, recv = refs[2 * n:]
        x, y, c = _position()
        sends = []
        for i in range(n):
            for j, (fx, fy) in enumerate(_CHIP_FLIPS):
                px, py = _flip(x, fx), _flip(y, fy)
                cp = pltpu.make_async_remote_copy(src_ref=g_refs[i].at[2 * px + py], dst_ref=out_refs[i].at[j],
                                                  send_sem=send.at[i, j], recv_sem=recv.at[i, j],
                                                  device_id=(px, py, c), device_id_type=MESH)
                cp.start()
                sends.append(cp)
        for cp in sends:
            cp.wait_recv()
        for cp in sends:
            cp.wait_send()

    return pl.pallas_call(
        body, name=name, in_specs=[ANY] * n, out_specs=[ANY] * n,
        out_shape=[jax.ShapeDtypeStruct((3,) + g.shape[1:], g.dtype) for g in gs],
        scratch_shapes=[pltpu.SemaphoreType.DMA((n, 3)), pltpu.SemaphoreType.DMA((n, 3))],
    )(*gs)


def _sum_own_and_slots(own, got, name):
    n, r, w = own.shape
    tr = _row_tile(r, w)

    def body(own_ref, got_ref, o_ref):
        me = 2 * lax.axis_index("x") + lax.axis_index("y")
        acc = own_ref[0]
        for i in range(1, n):
            acc = jnp.where(me == i, own_ref[i], acc)
        acc = acc.astype(f32)
        for j in range(3):
            acc = acc + got_ref[j].astype(f32)
        o_ref[...] = acc

    return pl.pallas_call(
        body, name=name, grid=(r // tr,),
        in_specs=[pl.BlockSpec((n, tr, w), lambda i: (0, i, 0)), pl.BlockSpec((3, tr, w), lambda i: (0, i, 0))],
        out_specs=pl.BlockSpec((tr, w), lambda i: (i, 0)), out_shape=jax.ShapeDtypeStruct((r, w), f32))(own, got)


def _share_chips(a, name):
    def body(a_ref, out_ref, send, recv):
        x, y, c = _position()
        sends = []
        for j, (fx, fy) in enumerate(_CHIP_FLIPS):
            cp = pltpu.make_async_remote_copy(src_ref=a_ref, dst_ref=out_ref.at[j], send_sem=send.at[j],
                                              recv_sem=recv.at[j], device_id=(_flip(x, fx), _flip(y, fy), c),
                                              device_id_type=MESH)
            cp.start()
            sends.append(cp)
        for cp in sends:
            cp.wait_recv()
        for cp in sends:
            cp.wait_send()

    return pl.pallas_call(
        body, name=name, in_specs=[ANY], out_specs=ANY, out_shape=jax.ShapeDtypeStruct((3,) + a.shape, a.dtype),
        scratch_shapes=[pltpu.SemaphoreType.DMA((3,)), pltpu.SemaphoreType.DMA((3,))],
    )(a)


def _sum_in_chip_order(pair, got, name):
    r, w = pair.shape
    tr = _tile(r, 1408, 8)

    def body(p_ref, g_ref, o_ref):
        x, y = lax.axis_index("x"), lax.axis_index("y")
        me = 2 * x + y
        across = [2 * _flip(x, fx) + _flip(y, fy) for fx, fy in _CHIP_FLIPS]
        acc = None
        for i in range(N_CHIPS):
            term = p_ref[...]
            for j in range(3):
                term = jnp.where(across[j] == i, g_ref[j], term)
            acc = term if acc is None else acc + term
        o_ref[...] = acc

    return pl.pallas_call(
        body, name=name, grid=(r // tr,),
        in_specs=[pl.BlockSpec((tr, w), lambda i: (i, 0)), pl.BlockSpec((3, tr, w), lambda i: (0, i, 0))],
        out_specs=pl.BlockSpec((tr, w), lambda i: (i, 0)), out_shape=jax.ShapeDtypeStruct((r, w), f32))(pair, got)


def _add2(a, b, name):
    r, w = a.shape
    tr = _tile(r, 1408, 8)
    spec = pl.BlockSpec((tr, w), lambda i: (i, 0))

    def body(a_ref, b_ref, o_ref):
        o_ref[...] = a_ref[...] + b_ref[...]

    return pl.pallas_call(body, name=name, grid=(r // tr,), in_specs=[spec, spec], out_specs=spec,
                          out_shape=jax.ShapeDtypeStruct(a.shape, f32))(a, b)


def _adamw(w, g_parts, m, v, name):
    shape = w.shape
    view = shape if len(shape) >= 2 else (1,) + shape
    assert all(d == 1 for d in view[:-2]), shape
    rows, cols = view[-2:]
    cap = max(8, (256 * 1024 // cols) // 8 * 8)
    tr = rows if rows <= cap else _tile(rows, cap, 8)
    lead = len(view) - 2
    n_g = len(g_parts)

    def body(*refs):
        w_ref = refs[0]
        g_refs = refs[1:1 + n_g]
        m_ref, v_ref, g_out, d_out, m_out, v_out = refs[1 + n_g:]
        g = g_refs[0][...]
        for gr in g_refs[1:]:
            g = g + gr[...]
        m_new = ADAM_B1 * m_ref[...] + (1.0 - ADAM_B1) * g
        v_new = ADAM_B2 * v_ref[...] + (1.0 - ADAM_B2) * (g * g)
        m_hat = m_new / (1.0 - ADAM_B1 ** ADAM_STEP)
        v_hat = v_new / (1.0 - ADAM_B2 ** ADAM_STEP)
        g_out[...] = g
        d_out[...] = -ADAM_LR * (m_hat / (jnp.sqrt(v_hat) + ADAM_EPS) + ADAM_WD * w_ref[...])
        m_out[...] = m_new
        v_out[...] = v_new

    spec = pl.BlockSpec((None,) * lead + (tr, cols), lambda i: (0,) * lead + (i, 0))
    args = [w.reshape(view)] + [g.reshape(view) for g in g_parts] + [m.reshape(view), v.reshape(view)]
    outs = pl.pallas_call(body, name=name, grid=(rows // tr,), in_specs=[spec] * len(args), out_specs=[spec] * 4,
                          out_shape=[jax.ShapeDtypeStruct(view, f32)] * 4)(*args)
    return [o.reshape(shape) for o in outs]


_BIG = ('ffn1_w_gu', 'ffn1_w_down', 'w_in', 'w_out', 'ffn2_w_gu', 'ffn2_w_down')
_SMALL_SHARDED = ('meta_tokens', 'a_conv_w', 'b_w_up', 'b_a_up', 'b_g_up')
_WEIGHTS = ('meta_tokens', 'ffn1_norm', 'ffn1_w_gu', 'ffn1_w_down', 'mix_norm', 'w_in', 'a_conv_w', 'a_log_rate',
            'a_dt_bias', 'a_out_norm', 'b_shift_mu', 'b_w0', 'b_w_up', 'b_a0', 'b_a_up', 'b_g_up', 'b_k_k', 'b_k_a',
            'b_r_k', 'b_ln_gain', 'b_ln_bias', 'w_out', 'ffn2_norm', 'ffn2_w_gu', 'ffn2_w_down', 'final_norm')
_SMALL = tuple(n for n in _WEIGHTS if n not in _BIG)


def _rows_of(shape):
    n = 1
    for d in shape:
        n *= d
    return n, -(-n // LANES)


def _pack(arrs, dtype, row_mult=32):
    parts, total = [], 0
    for a in arrs:
        n, rows = _rows_of(a.shape)
        flat = a.reshape(-1).astype(dtype)
        if n % LANES:
            flat = jnp.pad(flat, (0, rows * LANES - n))
        parts.append(flat)
        total += rows
    extra = -total % row_mult
    if extra:
        parts.append(jnp.zeros((extra * LANES,), dtype))
    return jnp.concatenate(parts).reshape(total + extra, LANES)


def _unpack(packed, shapes, lead=()):
    out, off = [], 0
    for sh in shapes:
        n, rows = _rows_of(sh)
        seg = packed[..., off:off + rows, :]
        if n % LANES:
            seg = seg.reshape(lead + (-1,))[..., :n]
        out.append(seg.reshape(lead + tuple(sh)))
        off += rows
    return out


def _cols_from_shards(s):
    return jnp.concatenate([s[i] for i in range(N_CHIPS)], axis=-1)


def kernel(x, meta_tokens, ffn1_norm, ffn1_w_gu, ffn1_w_down, mix_norm, w_in, a_conv_w, a_log_rate, a_dt_bias, a_out_norm, b_shift_mu, b_w0, b_w_up, b_a0, b_a_up, b_g_up, b_k_k, b_k_a, b_r_k, b_ln_gain, b_ln_bias, w_out, ffn2_norm, ffn2_w_gu, ffn2_w_down, final_norm, loss_target, m_meta_tokens, m_ffn1_norm, m_ffn1_w_gu, m_ffn1_w_down, m_mix_norm, m_w_in, m_a_conv_w, m_a_log_rate, m_a_dt_bias, m_a_out_norm, m_b_shift_mu, m_b_w0, m_b_w_up, m_b_a0, m_b_a_up, m_b_g_up, m_b_k_k, m_b_k_a, m_b_r_k, m_b_ln_gain, m_b_ln_bias, m_w_out, m_ffn2_norm, m_ffn2_w_gu, m_ffn2_w_down, m_final_norm, v_meta_tokens, v_ffn1_norm, v_ffn1_w_gu, v_ffn1_w_down, v_mix_norm, v_w_in, v_a_conv_w, v_a_log_rate, v_a_dt_bias, v_a_out_norm, v_b_shift_mu, v_b_w0, v_b_w_up, v_b_a0, v_b_a_up, v_b_g_up, v_b_k_k, v_b_k_a, v_b_r_k, v_b_ln_gain, v_b_ln_bias, v_w_out, v_ffn2_norm, v_ffn2_w_gu, v_ffn2_w_down, v_final_norm):
    args = locals()
    wts = {n: args[n] for n in _WEIGHTS}
    mom = {n: args["m_" + n] for n in _WEIGHTS}
    var = {n: args["v_" + n] for n in _WEIGHTS}
    chip = 2 * lax.axis_index("x") + lax.axis_index("y")

    big_shapes = [wts[n].shape[1:] for n in _BIG]
    small_shapes = [wts[n].shape[-2:] for n in _SMALL_SHARDED]
    big_flat = [wts[n].astype(bf16).reshape(wts[n].shape[1:]) for n in _BIG]
    small_packed = _pack([wts[n] for n in _SMALL_SHARDED], f32)
    gathered = _gather_chips(big_flat + [small_packed], "gather_weights")
    gu1, dn1, w_in_s, w_out_s, gu2, dn2 = [a.reshape((N_CHIPS,) + tuple(sh)) for a, sh in zip(gathered, big_shapes)]
    meta_s, conv_s, wup_s, aup_s, gup_s = _unpack(gathered[-1], small_shapes, (N_CHIPS,))
    w = {
        'ffn1_norm': ffn1_norm, 'mix_norm': mix_norm, 'ffn2_norm': ffn2_norm, 'final_norm': final_norm[None, :],
        'ffn1_wgu': gu1, 'ffn1_wd': dn1.reshape(D_FF, D), 'ffn2_wgu': gu2, 'ffn2_wd': dn2.reshape(D_FF, D),
        'w_in_p': _win_shards_to_padded(w_in_s), 'w_out': w_out_s.reshape(D, D),
        'a_conv_w': _cols_from_shards(conv_s), 'b_w_up': _cols_from_shards(wup_s), 'b_a_up': _cols_from_shards(aup_s),
        'b_g_up': _cols_from_shards(gup_s),
        'a_log_rate': a_log_rate, 'a_dt_bias': a_dt_bias, 'a_out_norm': a_out_norm, 'b_shift_mu': b_shift_mu,
        'b_w0': b_w0, 'b_a0': b_a0, 'b_k_k': b_k_k, 'b_k_a': b_k_a, 'b_r_k': b_r_k, 'b_ln_gain': b_ln_gain,
        'b_ln_bias': b_ln_bias,
    }
    meta_full = _cols_from_shards(meta_s)

    h0 = jnp.concatenate([jnp.zeros((PAD, D), f32), meta_full, x[0]], axis=0)
    tgt = jnp.concatenate([jnp.zeros((SKIP, D), f32), loss_target[0]], axis=0)
    loss_local, d_h0, g = _local_step(h0, tgt, w)
    loss = lax.psum(loss_local, ("x", "y", "c"))
    grad_x = d_h0[SKIP:][None]

    big_grads = [
        g['ffn1_wgu'],
        g['ffn1_wd'].reshape(N_CHIPS, D_FF // N_CHIPS, D),
        _win_padded_to_shards(g['w_in_p']),
        g['w_out'].reshape(N_CHIPS, D // N_CHIPS, D),
        g['ffn2_wgu'],
        g['ffn2_wd'].reshape(N_CHIPS, D_FF // N_CHIPS, D),
    ]
    g_halves = [a.reshape(N_CHIPS, 2, a.shape[1] // 2, a.shape[2]) for a in big_grads]
    sib_halves = _swap_sibling(g_halves, lambda ref, c: ref.at[:, 1 - c], [a.shape[:1] + a.shape[2:] for a in g_halves],
                               "swap_halves")
    chip_halves = [_add_halves(a, b, bf16, f"add_sibling{i}") for i, (a, b) in enumerate(zip(g_halves, sib_halves))]
    got = _scatter_chips(chip_halves, "scatter_grads")
    mine = [_sum_own_and_slots(a, b, f"sum_chips{i}") for i, (a, b) in enumerate(zip(chip_halves, got))]
    theirs = _swap_sibling(mine, lambda ref, c: ref, [a.shape for a in mine], "swap_sums")
    core = lax.axis_index("c")
    big_parts = [jnp.concatenate([jnp.where(core == 0, a, b), jnp.where(core == 0, b, a)], axis=0)
                 for a, b in zip(mine, theirs)]

    small_full = {
        'meta_tokens': d_h0[PAD:SKIP], 'ffn1_norm': g['ffn1_norm'], 'mix_norm': g['mix_norm'], 'a_conv_w': g['a_conv_w'],
        'a_log_rate': g['a_log_rate'], 'a_dt_bias': g['a_dt_bias'], 'a_out_norm': g['a_out_norm'],
        'b_shift_mu': g['b_shift_mu'], 'b_w0': g['b_w0'], 'b_w_up': g['b_w_up'], 'b_a0': g['b_a0'], 'b_a_up': g['b_a_up'],
        'b_g_up': g['b_g_up'], 'b_k_k': g['b_k_k'], 'b_k_a': g['b_k_a'], 'b_r_k': g['b_r_k'], 'b_ln_gain': g['b_ln_gain'],
        'b_ln_bias': g['b_ln_bias'], 'ffn2_norm': g['ffn2_norm'], 'final_norm': g['final_norm'],
    }
    s_shapes = [small_full[n].shape for n in _SMALL]
    s_packed = _pack([small_full[n] for n in _SMALL], f32, row_mult=256)
    (s_sib,) = _swap_sibling([s_packed], lambda ref, c: ref, [s_packed.shape], "swap_small")
    s_pair = _add2(s_packed, s_sib, "add_small")
    s_sum = _sum_in_chip_order(s_pair, _share_chips(s_pair, "share_small"), "sum_small")
    s_parts = dict(zip(_SMALL, _unpack(s_sum, s_shapes)))

    grad, delta, new_m, new_v = {}, {}, {}, {}
    for n, a in zip(_BIG, big_parts):
        grad[n], delta[n], new_m[n], new_v[n] = _adamw(wts[n], [a.reshape(wts[n].shape)], mom[n], var[n], f"adamw_{n}")
    for n in _SMALL:
        gs = s_parts[n]
        if n in _SMALL_SHARDED:
            width = wts[n].shape[-1]
            gs = lax.dynamic_slice_in_dim(gs, chip * width, width, axis=gs.ndim - 1)
        gs = gs.reshape(wts[n].shape)
        grad[n], delta[n], new_m[n], new_v[n] = _adamw(wts[n], [gs], mom[n], var[n], f"adamw_{n}")

    return (loss, grad_x, *[grad[n] for n in _WEIGHTS], *[delta[n] for n in _WEIGHTS],
            *[new_m[n] for n in _WEIGHTS], *[new_v[n] for n in _WEIGHTS])
```

```python
import functools

import jax
import jax.numpy as jnp
from jax import lax
from jax.experimental import pallas as pl
from jax.experimental.pallas import tpu as pltpu

f32 = jnp.float32
bf16 = jnp.bfloat16
HI = lax.Precision.HIGHEST
MESH = pl.DeviceIdType.MESH
ANY = pl.BlockSpec(memory_space=pl.ANY)

D = 1024
N_META = 16
CHUNK = 64
PAD = CHUNK - N_META
SKIP = PAD + N_META
EPS = 1e-6
D_FF = 2816
A_HEADS = 8
A_DK = 128
B_HEADS = 16
B_N = 64
B_GN_EPS = B_N * 1e-5
IN_TOTAL = 9520
ZP = 9600
LANES = 128
N_CHIPS = 4

ADAM_LR, ADAM_B1, ADAM_B2, ADAM_EPS, ADAM_WD, ADAM_STEP = 0.001, 0.9, 0.999, 1e-08, 0.01, 10

MXU_DTYPE = bf16


def _tile(n, cap, mult):
    if n <= cap:
        return n
    best = None
    for t in range(mult, cap + 1, mult):
        if n % t == 0:
            best = t
    assert best is not None, (n, cap, mult)
    return best


def _sigmoid(x):
    return jax.nn.sigmoid(x)


def _silu(x):
    return x * jax.nn.sigmoid(x)


def _softplus(x):
    return jnp.maximum(x, 0.0) + jnp.log(1.0 + jnp.exp(-jnp.abs(x)))


def _head_matrix(c, nh):
    hd = c // nh
    r = lax.broadcasted_iota(jnp.int32, (c, nh), 0)
    h = lax.broadcasted_iota(jnp.int32, (c, nh), 1)
    return (r >= h * hd) & (r < (h + 1) * hd)


def _dot_exact_rhs(x, e, cb):
    dn = (((1,), (cb,)), ((), ()))
    if SCAN_PASSES == 0:
        return lax.dot_general(x, e.astype(f32), dn, precision=HI, preferred_element_type=f32)
    eb = e.astype(bf16)
    hi = x.astype(bf16)
    lo = (x - hi.astype(f32)).astype(bf16)
    return (lax.dot_general(hi, eb, dn, preferred_element_type=f32)
            + lax.dot_general(lo, eb, dn, preferred_element_type=f32))


def _head_sum_impl(x, nh):
    e = _head_matrix(x.shape[-1], nh)
    return _dot_exact_rhs(_dot_exact_rhs(x, e, 0), e, 1)


@functools.partial(jax.custom_vjp, nondiff_argnums=(1,))
def _head_sum(x, nh):
    return _head_sum_impl(x, nh)


def _head_sum_fwd(x, nh):
    return _head_sum_impl(x, nh), None


def _head_sum_bwd(nh, _, g):
    return (_head_sum_impl(g, nh),)


_head_sum.defvjp(_head_sum_fwd, _head_sum_bwd)


@functools.partial(jax.custom_vjp, nondiff_argnums=(1,))
def _shift_rows(x, s):
    n = x.shape[0]
    row = lax.broadcasted_iota(jnp.int32, x.shape, 0)
    if s > 0:
        return jnp.where(row >= s, pltpu.roll(x, s, 0), 0.0)
    return jnp.where(row < n + s, pltpu.roll(x, n + s, 0), 0.0)


def _shift_rows_fwd(x, s):
    return _shift_rows(x, s), None


def _shift_rows_bwd(s, _, g):
    return (_shift_rows(g, -s),)


_shift_rows.defvjp(_shift_rows_fwd, _shift_rows_bwd)


def _matmul(a, b, *, ta=False, tb=False, res=None, scale=1.0, name, b_cols_split=None, out_cols_split=None,
            out_into=None):
    assert not (ta and tb)
    (ar, ac) = a.shape
    b0 = 0
    if b_cols_split:
        b0, bs = b_cols_split
        _, br, bc_part = b.shape
        bc = bs * bc_part
    else:
        br, bc = b.shape
    m, k = (ac, ar) if ta else (ar, ac)
    n, kb = (br, bc) if tb else (bc, br)
    assert k == kb, (a.shape, b.shape, ta, tb)
    tm = _tile(m, 1408, LANES) if ta else _tile(m, 832, 8)
    tn = _tile(n, 1920, LANES)
    tk = _tile(k, 1040, 8) if ta else _tile(k, 1920, LANES)
    nk = k // tk
    dn = (((0 if ta else 1,), (1 if tb else 0,)), ((), ()))
    if b_cols_split:
        assert (tk if tb else tn) == bc_part, (b.shape, tn, tk)

    def body(*refs):
        a_ref, b_ref = refs[:2]
        r_ref = refs[2] if res is not None else None
        o_ref, acc = refs[-2:]
        kk = pl.program_id(2)

        @pl.when(kk == 0)
        def _():
            acc[...] = jnp.zeros_like(acc)

        acc[...] += lax.dot_general(a_ref[...].astype(MXU_DTYPE), b_ref[...].astype(MXU_DTYPE), dn,
                                    preferred_element_type=f32,
                                    precision=None if MXU_DTYPE == bf16 else HI)

        @pl.when(kk == nk - 1)
        def _():
            out = acc[...]
            if scale != 1.0:
                out = out * scale
            if res is not None:
                out = r_ref[...] + out
            o_ref[...] = out

    if ta:
        a_spec = pl.BlockSpec((tk, tm), lambda i, j, kk: (kk, i))
    else:
        a_spec = pl.BlockSpec((tm, tk), lambda i, j, kk: (i, kk))
    if tb and b_cols_split:
        b_spec = pl.BlockSpec((None, tn, tk), lambda i, j, kk: (kk + b0, j, 0))
    elif tb:
        b_spec = pl.BlockSpec((tn, tk), lambda i, j, kk: (j, kk))
    elif b_cols_split:
        b_spec = pl.BlockSpec((None, tk, tn), lambda i, j, kk: (j + b0, kk, 0))
    else:
        b_spec = pl.BlockSpec((tk, tn), lambda i, j, kk: (kk, j))
    in_specs = [a_spec, b_spec]
    args = [a, b]
    if res is not None:
        in_specs.append(pl.BlockSpec((tm, tn), lambda i, j, kk: (i, j)))
        args.append(res)
    aliases = {}
    if out_cols_split:
        o0, total = out_cols_split
        out_spec = pl.BlockSpec((None, tm, tn), lambda i, j, kk: (j + o0, i, 0))
        out_shape = jax.ShapeDtypeStruct((total, m, tn), f32)
        if out_into is not None:
            assert out_into.shape == out_shape.shape
            in_specs.append(ANY)
            args.append(out_into)
            aliases = {len(args) - 1: 0}
    else:
        out_spec = pl.BlockSpec((tm, tn), lambda i, j, kk: (i, j))
        out_shape = jax.ShapeDtypeStruct((m, n), f32)
    return pl.pallas_call(
        body, name=name, grid=(m // tm, n // tn, nk), in_specs=in_specs, out_specs=out_spec, out_shape=out_shape,
        scratch_shapes=[pltpu.VMEM((tm, tn), f32)], input_output_aliases=aliases,
        compiler_params=pltpu.CompilerParams(dimension_semantics=("parallel", "parallel", "arbitrary")),
    )(*args)


def _tw_fwd(fn, ins, in_specs, out_shapes, out_specs, grid, name, with_pid=False):
    n_in = len(ins)

    def body(*refs):
        vals = [r[...] for r in refs[:n_in]]
        outs = fn(pl.program_id(0), *vals) if with_pid else fn(*vals)
        for r, o in zip(refs[n_in:], outs):
            r[...] = o.astype(r.dtype)

    return pl.pallas_call(body, name=name, grid=grid, in_specs=in_specs, out_specs=out_specs,
                          out_shape=out_shapes)(*ins)


def _tw_bwd(fn, ins, in_specs, cts, ct_specs, kinds, grid, name, with_pid=False, tile_dtype=f32, ct_extra=(),
            residual=None):
    n_in, n_ct = len(ins), len(cts)
    diff = [i for i, kd in enumerate(kinds) if kd is not None]
    n_ex = len(ct_extra)

    def body(*refs):
        vals = [r[...] for r in refs[:n_in]]
        ctv = [r[...].astype(f32) for r in refs[n_in:n_in + n_ct]]
        for (ci, _), r in zip(ct_extra, refs[n_in + n_ct:n_in + n_ct + n_ex]):
            ctv[ci] = ctv[ci] + r[...]
        ctv = tuple(ctv)
        n_fixed = n_in + n_ct + n_ex
        res_ref = refs[n_fixed] if residual is not None else None
        g_refs = refs[n_fixed + (residual is not None):]
        pid = pl.program_id(0)

        def f(*dv):
            full = list(vals)
            for i, v in zip(diff, dv):
                full[i] = v
            out = fn(pid, *full) if with_pid else fn(*full)
            return tuple(out)

        _, vjp = jax.vjp(f, *[vals[i] for i in diff])
        gs = vjp(ctv)
        first = pid == 0
        for i2 in range(1, len(grid)):
            first = first & (pl.program_id(i2) == 0)
        for i, g, g_ref in zip(diff, gs, g_refs):
            if kinds[i] != 'acc':
                if i == 0 and res_ref is not None:
                    g = res_ref[...] + g
                g_ref[...] = g.astype(g_ref.dtype)
            else:
                @pl.when(first)
                def _(g=g, g_ref=g_ref):
                    g_ref[...] = g

                @pl.when(jnp.logical_not(first))
                def _(g=g, g_ref=g_ref):
                    g_ref[...] += g

    zero_map = {1: lambda *a: (0,), 2: lambda *a: (0, 0), 3: lambda *a: (0, 0, 0)}
    out_specs, out_shapes = [], []
    for i in diff:
        if kinds[i] == 'tile':
            out_shapes.append(jax.ShapeDtypeStruct(ins[i].shape, tile_dtype))
            out_specs.append(in_specs[i])
        elif kinds[i] == 'acc':
            out_shapes.append(jax.ShapeDtypeStruct(ins[i].shape, f32))
            out_specs.append(pl.BlockSpec(ins[i].shape, zero_map[ins[i].ndim]))
        else:
            out_shapes.append(jax.ShapeDtypeStruct(kinds[i][1], kinds[i][3] if len(kinds[i]) > 3 else tile_dtype))
            out_specs.append(kinds[i][2])
    extra_specs = [ct_specs[ci] for ci, _ in ct_extra]
    extra = [a for _, a in ct_extra]
    if residual is not None:
        assert kinds[0] == 'tile'
        extra_specs.append(in_specs[0])
        extra.append(residual)
    return pl.pallas_call(body, name=name, grid=grid, in_specs=list(in_specs) + list(ct_specs) + extra_specs,
                          out_specs=out_specs, out_shape=out_shapes)(*ins, *cts, *extra)


def _row_spec(tm, c, col_block=0):
    return pl.BlockSpec((tm, c), lambda i, cb=col_block: (i, cb))


def _full_spec(shape):
    nd = len(shape)
    return pl.BlockSpec(shape, lambda *a, nd=nd: (0,) * nd)


def _f_rms(x, g):
    return (x * lax.rsqrt(jnp.mean(x * x, axis=-1, keepdims=True) + EPS) * g,)


def _f_swiglu(gate, up):
    return (_silu(gate) * up,)


def _f_loss(pid, h, g, tgt, *, tm):
    y = h * lax.rsqrt(jnp.mean(h * h, axis=-1, keepdims=True) + EPS) * g
    row = pid * tm + lax.broadcasted_iota(jnp.int32, (tm, 1), 0)
    err = jnp.where(row >= SKIP, y - tgt, 0.0)
    per_row = jnp.mean(err * err, axis=-1, keepdims=True)
    return (0.5 * jnp.sum(per_row, axis=0, keepdims=True),)


def _f_conv(x, w, *, norm, scale):
    y = x * w[3:4, :]
    for s in (1, 2, 3):
        y = y + _shift_rows(x, s) * w[3 - s:4 - s, :]
    y = _silu(y)
    if norm:
        y = y * lax.rsqrt(jnp.sum(y * y, axis=-1, keepdims=True) + 1e-6) * scale
    return (y,)


def _f_dgates(pid, abeta, aalpha, log_rate, dt_bias, *, tm):
    row = pid * tm + lax.broadcasted_iota(jnp.int32, (tm, 1), 0)
    live = row >= PAD
    beta = jnp.where(live, _sigmoid(abeta), 0.0)
    g = jnp.where(live, -jnp.exp(log_rate) * _softplus(aalpha + dt_bias), 0.0)
    return beta, g


def _f_tshift(z, mu):
    return (z + (_shift_rows(z, 1) - z) * mu,)


def _f_rwkv_pre(k, wd, ad, gd, w0, w_up, a0, a_up, g_up, k_k, k_a):
    w_log = -_softplus(-(w0 + _smm(jnp.tanh(wd), w_up, 1))) - 0.5
    lw = -jnp.exp(w_log)
    a_lr = _sigmoid(a0 + _smm(ad, a_up, 1))
    gate = _smm(_sigmoid(gd), g_up, 1)
    kkp = k * k_k
    kk = kkp * lax.rsqrt(_head_sum(kkp * kkp, B_HEADS) + 1e-6)
    kmod = k * (1.0 + (a_lr - 1.0) * k_a)
    return lw, kmod, -kk, kk * a_lr, gate


def _f_mix_post(o, az, y, r, kmod, v, gate, ga, gb, out_gain, ln_g, ln_b, r_k):
    ms = _head_sum(o * o, A_HEADS) * (1.0 / A_DK)
    oa = o * lax.rsqrt(ms + EPS) * out_gain * _silu(az)
    mean = _head_sum(y, B_HEADS) * (1.0 / B_N)
    yc = y - mean
    var = _head_sum(yc * yc, B_HEADS) * (1.0 / B_N)
    yn = yc * lax.rsqrt(var + B_GN_EPS) * ln_g + ln_b
    bonus = _head_sum(r * kmod * r_k, B_HEADS) * v
    ob = (yn + bonus) * gate
    return (_sigmoid(ga) * oa + _sigmoid(gb) * ob,)


SCAN_PASSES = 3


def _split2(a):
    hi = a.astype(bf16)
    return hi, (a - hi.astype(f32)).astype(bf16)


def _dot_passes(a, b, ca, cb, passes):
    dn = (((ca,), (cb,)), ((), ()))
    if SCAN_PASSES == 0:
        return lax.dot_general(a, b, dn, precision=HI, preferred_element_type=f32)
    if passes == 1:
        return lax.dot_general(a.astype(bf16), b.astype(bf16), dn, preferred_element_type=f32)
    ah, al = _split2(a)
    bh, bl = _split2(b)
    return (lax.dot_general(ah, bh, dn, preferred_element_type=f32)
            + (lax.dot_general(ah, bl, dn, preferred_element_type=f32)
               + lax.dot_general(al, bh, dn, preferred_element_type=f32)))


@functools.partial(jax.custom_vjp, nondiff_argnums=(2, 3, 4))
def _sdot(a, b, ca, cb, passes):
    return _dot_passes(a, b, ca, cb, passes)


def _sdot_fwd(a, b, ca, cb, passes):
    return _dot_passes(a, b, ca, cb, passes), (a, b)


def _sdot_bwd(ca, cb, passes, res, g):
    a, b = res
    if (ca, cb) == (1, 0):
        return _dot_passes(g, b, 1, 1, passes), _dot_passes(a, g, 0, 0, passes)
    if (ca, cb) == (1, 1):
        return _dot_passes(g, b, 1, 0, passes), _dot_passes(g, a, 0, 0, passes)
    assert (ca, cb) == (0, 0)
    return _dot_passes(b, g, 1, 1, passes), _dot_passes(a, g, 1, 0, passes)


_sdot.defvjp(_sdot_fwd, _sdot_bwd)


def _smm(a, b, passes=3):
    return _sdot(a, b, 1, 0, passes)


def _smm_nt(a, b, passes=3):
    return _sdot(a, b, 1, 1, passes)


def _smm_tn(a, b, passes=3):
    return _sdot(a, b, 0, 0, passes)


def _tri_dot(x, ca):
    n = x.shape[0]
    incl = _tri_masks(n)[0]
    dn = (((ca,), (0,)), ((), ()))
    if SCAN_PASSES == 0:
        return lax.dot_general(incl.astype(f32), x, dn, precision=HI, preferred_element_type=f32)
    tri = incl.astype(bf16)
    hi, r1 = x.astype(bf16), None
    r1 = x - hi.astype(f32)
    mid = r1.astype(bf16)
    lo = (r1 - mid.astype(f32)).astype(bf16)
    return (lax.dot_general(tri, hi, dn, preferred_element_type=f32)
            + (lax.dot_general(tri, mid, dn, preferred_element_type=f32)
               + lax.dot_general(tri, lo, dn, preferred_element_type=f32)))


@jax.custom_vjp
def _cumsum_rows(x):
    return _tri_dot(x, 1)


def _cumsum_rows_fwd(x):
    return _tri_dot(x, 1), None


def _cumsum_rows_bwd(_, g):
    return (_tri_dot(g, 0),)


_cumsum_rows.defvjp(_cumsum_rows_fwd, _cumsum_rows_bwd)


def _tri_masks(n):
    i = lax.broadcasted_iota(jnp.int32, (n, n), 0)
    j = lax.broadcasted_iota(jnp.int32, (n, n), 1)
    return i >= j, i > j, i == j, i <= j


def _unit_lower_inv_impl(low, passes):
    n = low.shape[0]
    assert n == CHUNK
    _, _, eye, _ = _tri_masks(n)
    acc = eye.astype(f32) + low
    p = low
    for _ in range(5):
        p = _dot_passes(p, p, 1, 0, passes)
        acc = acc + _dot_passes(acc, p, 1, 0, passes)
    return acc


@functools.partial(jax.custom_vjp, nondiff_argnums=(1,))
def _unit_lower_inv(low, passes=3):
    return _unit_lower_inv_impl(low, passes)


def _unit_lower_inv_fwd(low, passes):
    t = _unit_lower_inv_impl(low, passes)
    return t, t


def _unit_lower_inv_bwd(passes, t, g):
    return (_dot_passes(_dot_passes(t, g, 0, 0, passes), t, 1, 1, passes),)


_unit_lower_inv.defvjp(_unit_lower_inv_fwd, _unit_lower_inv_bwd)

DELTA_PASSES = 1
DELTA_INV_PASSES = 1


def _delta_chunk(s, q, k, v, beta_row, g_row):
    p = DELTA_PASSES
    incl, strict, eye, upper = _tri_masks(CHUNK)
    beta = jnp.sum(jnp.where(eye, beta_row, 0.0), axis=1, keepdims=True)
    g = jnp.sum(jnp.where(eye, g_row, 0.0), axis=1, keepdims=True)
    gc = jnp.sum(jnp.where(incl, g_row, 0.0), axis=1, keepdims=True)
    gc_row = jnp.sum(jnp.where(upper, g, 0.0), axis=0, keepdims=True)
    decay = jnp.where(incl, jnp.exp(jnp.where(incl, gc - gc_row, 0.0)), 0.0)
    kb = k * beta
    vb = v * beta
    m = jnp.where(strict, _smm_nt(kb, k, p) * decay, 0.0)
    tinv = _unit_lower_inv(-m, DELTA_INV_PASSES)
    u = _smm(tinv, vb, p)
    wk = _smm(tinv, kb * jnp.exp(gc), p)
    attn = _smm_nt(q, k, p) * decay
    qg = q * jnp.exp(gc)
    g_last = jnp.sum(g, axis=0, keepdims=True)
    k_tail = k * jnp.exp(g_last - gc)
    v_new = u - _smm(wk, s, p)
    o = _smm(qg, s, p) + _smm(attn, v_new, p)
    s_new = s * jnp.exp(g_last) + _smm_tn(k_tail, v_new, p)
    return o, s_new


RWKV_PASSES = 1
RWKV_INV_PASSES = 1


def _rwkv_chunk(st, r, k, v, a, b, lw):
    c = CHUNK
    p, pi = RWKV_PASSES, RWKV_INV_PASSES
    _, strict, _, _ = _tri_masks(c)
    lane = lax.broadcasted_iota(jnp.int32, (c, 2 * B_N), 1)
    row = lax.broadcasted_iota(jnp.int32, (c, 2 * B_N), 0)
    first = lane < B_N
    incl2 = row >= jnp.where(first, lane, lane - B_N)
    bi = lax.broadcasted_iota(jnp.int32, (2 * B_N, 2 * B_N), 0) < B_N
    bj = lax.broadcasted_iota(jnp.int32, (2 * B_N, 2 * B_N), 1) < B_N
    blockdiag = bi == bj
    cum = _cumsum_rows(lw)
    e_pos = jnp.exp(cum)
    e_neg = jnp.exp(-cum)
    rt = r * e_pos
    at = a * jnp.exp(cum - lw)
    kt = k * e_neg
    bt = b * e_neg
    bk = jnp.concatenate([bt, kt], axis=0)
    a_s0 = _smm_nt(at, st, p)
    r_s0 = _smm_nt(rt, st, p)
    heads = (first, jnp.logical_not(first))
    u = jnp.zeros((c, 2 * B_N), f32)
    for sel in heads:
        at_h = jnp.where(sel, at, 0.0)
        ab = jnp.where(strict, _smm_nt(at_h, bt, pi), 0.0)
        ak = jnp.where(strict, _smm_nt(at_h, kt, p), 0.0)
        t_h = _unit_lower_inv(ab, pi)
        u = u + _smm(t_h, jnp.where(sel, a_s0, 0.0) + _smm(ak, jnp.where(sel, v, 0.0), p), p)
    y = r_s0
    for sel in heads:
        rbk = jnp.where(incl2, _smm_nt(jnp.where(sel, rt, 0.0), bk, p), 0.0)
        uv = jnp.concatenate([jnp.where(sel, u, 0.0), jnp.where(sel, v, 0.0)], axis=0)
        y = y + _smm(rbk, uv, p)
    cl = jnp.sum(lw, axis=0, keepdims=True)
    dec = jnp.exp(cl - cum)
    uv_all = jnp.concatenate([u, v], axis=0)
    bk_dec = jnp.concatenate([b * dec, k * dec], axis=0)
    st_new = st * jnp.exp(cl) + jnp.where(blockdiag, _smm_tn(uv_all, bk_dec, p), 0.0)
    return y, st_new


GROUPS_PER_STEP = 8


def _scan_specs(ins, col_offs, n_chunks, reverse):
    gw = GROUPS_PER_STEP * LANES
    cidx = (lambda c: n_chunks - 1 - c) if reverse else (lambda c: c)
    specs = []
    for a, off in zip(ins, col_offs):
        if a.ndim == 2:
            assert off % gw == 0
            specs.append(pl.BlockSpec((CHUNK, gw), lambda h, c, o=off // gw: (cidx(c), h + o)))
        else:
            specs.append(pl.BlockSpec((GROUPS_PER_STEP, None, 1, CHUNK), lambda h, c: (h, cidx(c), 0, 0)))
    return specs, cidx


def _group_vals(refs, g):
    return [r[:, g * LANES:(g + 1) * LANES] if len(r.shape) == 2 else r[g] for r in refs]


def _scan_fwd(chunk_fn, ins, col_offs, n_groups, n_chunks, state_shape, name):
    n_in = len(ins)
    gps = GROUPS_PER_STEP
    t = ins[0].shape[0]

    def body(*refs):
        in_refs = refs[:n_in]
        o_ref, s0_ref, st = refs[n_in:]

        @pl.when(pl.program_id(1) == 0)
        def _():
            st[...] = jnp.zeros_like(st)

        states = st[...]
        vals = [jnp.stack(col) for col in zip(*[_group_vals(in_refs, g) for g in range(gps)])]
        o, s_new = jax.vmap(chunk_fn)(states, *vals)
        s0_ref[...] = states
        st[...] = s_new
        for g in range(gps):
            o_ref[:, g * LANES:(g + 1) * LANES] = o[g]

    specs, _ = _scan_specs(ins, col_offs, n_chunks, False)
    return pl.pallas_call(
        body, name=name, grid=(n_groups // gps, n_chunks), in_specs=specs,
        out_specs=[pl.BlockSpec((CHUNK, gps * LANES), lambda h, c: (c, h)),
                   pl.BlockSpec((gps, None) + state_shape, lambda h, c: (h, c, 0, 0))],
        out_shape=[jax.ShapeDtypeStruct((t, n_groups * LANES), f32),
                   jax.ShapeDtypeStruct((n_groups, n_chunks) + state_shape, f32)],
        scratch_shapes=[pltpu.VMEM((gps,) + state_shape, f32)],
        compiler_params=pltpu.CompilerParams(dimension_semantics=("parallel", "arbitrary")),
    )(*ins)


def _scan_bwd(chunk_fn, s0s, ins, col_offs, d_out, n_groups, n_chunks, state_shape, name):
    n_in = len(ins)
    gps = GROUPS_PER_STEP
    t = d_out.shape[0]

    def body(*refs):
        s0_ref = refs[0]
        in_refs = refs[1:1 + n_in]
        do_ref = refs[1 + n_in]
        g_refs = refs[2 + n_in:2 + 2 * n_in]
        dst = refs[2 + 2 * n_in]

        @pl.when(pl.program_id(1) == 0)
        def _():
            dst[...] = jnp.zeros_like(dst)

        vals = [jnp.stack(col) for col in zip(*[_group_vals(in_refs, g) for g in range(gps)])]
        d_o = jnp.stack([do_ref[:, g * LANES:(g + 1) * LANES] for g in range(gps)])
        _, vjp = jax.vjp(jax.vmap(chunk_fn), s0_ref[...], *vals)
        gs = vjp((d_o, dst[...]))
        dst[...] = gs[0]
        for g_ref, gv in zip(g_refs, gs[1:]):
            if len(g_ref.shape) == 2:
                for g in range(gps):
                    g_ref[:, g * LANES:(g + 1) * LANES] = gv[g]
            else:
                g_ref[...] = gv

    specs, cidx = _scan_specs(ins, col_offs, n_chunks, True)
    out_lane = pl.BlockSpec((CHUNK, gps * LANES), lambda h, c: (cidx(c), h))
    g_specs = [out_lane if a.ndim == 2 else sp for a, sp in zip(ins, specs)]
    g_shapes = [(t, n_groups * LANES) if a.ndim == 2 else a.shape for a in ins]
    s0_spec = pl.BlockSpec((gps, None) + state_shape, lambda h, c: (h, cidx(c), 0, 0))
    return pl.pallas_call(
        body, name=name, grid=(n_groups // gps, n_chunks), in_specs=[s0_spec] + specs + [out_lane],
        out_specs=g_specs, out_shape=[jax.ShapeDtypeStruct(sh, f32) for sh in g_shapes],
        scratch_shapes=[pltpu.VMEM((gps,) + state_shape, f32)],
        compiler_params=pltpu.CompilerParams(dimension_semantics=("parallel", "arbitrary")),
    )(s0s, *ins, d_out)


def _rms_fwd(x, g, name):
    t = x.shape[0]
    tm = _tile(t, 416, 16)
    return _tw_fwd(_f_rms, [x, g], [_row_spec(tm, D), _full_spec(g.shape)],
                   [jax.ShapeDtypeStruct(x.shape, MXU_DTYPE)], [_row_spec(tm, D)], (t // tm,), name)[0]


def _rms_bwd(x, g, dy, residual, name):
    t = x.shape[0]
    tm = _tile(t, 416, 8)
    return _tw_bwd(_f_rms, [x, g], [_row_spec(tm, D), _full_spec(g.shape)], [dy], [_row_spec(tm, D)],
                   ['tile', 'acc'], (t // tm,), name, residual=residual)


def _ffn_fwd(h, gain, wgu, wd, tag):
    xn = _rms_fwd(h, gain, f"{tag}_rms")
    gate, up, act = _gate_up_act(xn, wgu, f"{tag}_gate_up")
    out = _matmul(act, wd, res=h, scale=0.5, name=f"{tag}_down")
    return out, (xn, gate, up, act)


def _mxu_dot(a, b, dn):
    return lax.dot_general(a.astype(MXU_DTYPE), b.astype(MXU_DTYPE), dn, preferred_element_type=f32,
                           precision=None if MXU_DTYPE == bf16 else HI)


def _gate_up_act(xn, wgu, name):
    t = xn.shape[0]
    wdt = wgu.shape[2]
    tm = _tile(t, 416, 16)
    dn = (((1,), (0,)), ((), ()))

    def body(x_ref, wg_ref, wu_ref, g_ref, u_ref, a_ref):
        x = x_ref[...]
        g = _mxu_dot(x, wg_ref[...], dn)
        u = _mxu_dot(x, wu_ref[...], dn)
        g_ref[...] = g
        u_ref[...] = u
        a_ref[...] = _f_swiglu(g, u)[0].astype(a_ref.dtype)

    out_spec = pl.BlockSpec((tm, wdt), lambda i, j: (i, j))
    return pl.pallas_call(
        body, name=name, grid=(t // tm, 2),
        in_specs=[pl.BlockSpec((tm, D), lambda i, j: (i, 0)), pl.BlockSpec((None, D, wdt), lambda i, j: (j, 0, 0)),
                  pl.BlockSpec((None, D, wdt), lambda i, j: (j + 2, 0, 0))],
        out_specs=[out_spec] * 3,
        out_shape=[jax.ShapeDtypeStruct((t, 2 * wdt), f32)] * 2 + [jax.ShapeDtypeStruct((t, 2 * wdt), MXU_DTYPE)],
        compiler_params=pltpu.CompilerParams(dimension_semantics=("parallel", "parallel")),
    )(xn, wgu, wgu)


def _d_gate_up(dout, wd, gate, up, name):
    t = dout.shape[0]
    wdt = D_FF // 2
    tm = _tile(t, 416, 16)
    dn = (((1,), (1,)), ((), ()))

    def body(do_ref, wd_ref, g_ref, u_ref, dg_ref, du_ref):
        d_act = 0.5 * _mxu_dot(do_ref[...], wd_ref[...], dn)
        _, vjp = jax.vjp(_f_swiglu, g_ref[...], u_ref[...])
        dg, du = vjp((d_act,))
        dg_ref[...] = dg.astype(dg_ref.dtype)
        du_ref[...] = du.astype(du_ref.dtype)

    spec = pl.BlockSpec((tm, wdt), lambda i, j: (i, j))
    return pl.pallas_call(
        body, name=name, grid=(t // tm, 2),
        in_specs=[pl.BlockSpec((tm, D), lambda i, j: (i, 0)), pl.BlockSpec((wdt, D), lambda i, j: (j, 0)), spec, spec],
        out_specs=[spec] * 2, out_shape=[jax.ShapeDtypeStruct((t, D_FF), MXU_DTYPE)] * 2,
        compiler_params=pltpu.CompilerParams(dimension_semantics=("parallel", "parallel")),
    )(dout, wd, gate, up)


def _ffn_bwd(h, gain, wgu, wd, saved, dout, tag):
    xn, gate, up, act = saved
    t = h.shape[0]
    d_wd = _matmul(act, dout, ta=True, scale=0.5, name=f"{tag}_dwd")
    d_gate, d_up = _d_gate_up(dout, wd, gate, up, f"{tag}_dact")
    d_wgu = _matmul(xn, d_gate, ta=True, out_cols_split=(0, N_CHIPS), name=f"{tag}_dwg")
    d_wgu = _matmul(xn, d_up, ta=True, out_cols_split=(2, N_CHIPS), out_into=d_wgu, name=f"{tag}_dwu")
    d_xn = _matmul(d_gate, wgu, tb=True, b_cols_split=(0, 2), name=f"{tag}_dxn_g")
    d_xn = _matmul(d_up, wgu, tb=True, b_cols_split=(2, 2), res=d_xn, name=f"{tag}_dxn_u")
    d_h, d_gain = _rms_bwd(h, gain, d_xn, dout, f"{tag}_drms")
    return d_h, d_gain, d_wgu, d_wd


def _col_spec(t, first_block):
    return pl.BlockSpec((t, LANES), lambda j, fb=first_block: (0, j + fb))


def _local_step(h0, tgt, w):
    t = h0.shape[0]
    assert t % CHUNK == 0
    nc = t // CHUNK
    grads = {}

    h1, ffn1_saved = _ffn_fwd(h0, w['ffn1_norm'], w['ffn1_wgu'], w['ffn1_wd'], "ffn1")
    u = _rms_fwd(h1, w['mix_norm'], "mix_rms")
    z = _matmul(u, w['w_in_p'], name="in_proj")
    zs = z[:, 9216:9216 + 304]
    abeta, aalpha = zs[:, 288:296], zs[:, 296:304]

    conv_w = w['a_conv_w']
    conv_fns = [functools.partial(_f_conv, norm=True, scale=A_DK ** -0.5),
                functools.partial(_f_conv, norm=True, scale=1.0),
                functools.partial(_f_conv, norm=False, scale=1.0)]
    qkv = []
    for idx, fn in enumerate(conv_fns):
        qkv.append(_tw_fwd(fn, [z, conv_w], [_col_spec(t, 8 * idx), pl.BlockSpec((4, LANES), lambda j, o=8 * idx: (0, j + o))],
                           [jax.ShapeDtypeStruct((t, D), f32)], [_col_spec(t, 0)], (A_HEADS,), f"a_conv{idx}")[0])
    aq, ak, av = qkv
    tmg = _tile(t, 1040, 8)
    dg_fn = functools.partial(_f_dgates, tm=tmg)
    dg_specs = [_row_spec(tmg, A_HEADS)] * 2 + [_full_spec((1, A_HEADS))] * 2
    beta, gdec = _tw_fwd(dg_fn, [abeta, aalpha, w['a_log_rate'], w['a_dt_bias']], dg_specs,
                         [jax.ShapeDtypeStruct((t, A_HEADS), f32)] * 2, [_row_spec(tmg, A_HEADS)] * 2, (t // tmg,),
                         "a_gates", with_pid=True)
    beta_h = beta.T.reshape(A_HEADS, nc, 1, CHUNK)
    gdec_h = gdec.T.reshape(A_HEADS, nc, 1, CHUNK)
    a_ins = [aq, ak, av, beta_h, gdec_h]
    a_offs = [0] * 5
    o_scan, a_s0 = _scan_fwd(_delta_chunk, a_ins, a_offs, A_HEADS, nc, (A_DK, A_DK), "a_scan")

    mu = w['b_shift_mu']
    mu_rkv, mu_s = mu[:, :3072], mu[:, 3072:]
    zf_rkv = _tw_fwd(_f_tshift, [z, mu_rkv], [_col_spec(t, 32), pl.BlockSpec((1, LANES), lambda j: (0, j))],
                     [jax.ShapeDtypeStruct((t, 3072), f32)], [_col_spec(t, 0)], (24,), "b_shift")[0]
    zs_b = zs[:, :288]
    zf_s = _tw_fwd(_f_tshift, [zs_b, mu_s], [_full_spec((t, 288)), _full_spec((1, 288))],
                   [jax.ShapeDtypeStruct((t, 288), f32)], [_full_spec((t, 288))], (1,), "b_shift_s")[0]
    wdf, adf, gdf = zf_s[:, 0:64], zf_s[:, 64:128], zf_s[:, 128:288]
    tmr = _tile(t, 160, 16)
    pre_params = [w['b_w0'], w['b_w_up'], w['b_a0'], w['b_a_up'], w['b_g_up'], w['b_k_k'], w['b_k_a']]
    pre_ins = [zf_rkv, wdf, adf, gdf] + pre_params
    pre_specs = ([_row_spec(tmr, D, 1), _row_spec(tmr, 64), _row_spec(tmr, 64), _row_spec(tmr, 160)]
                 + [_full_spec(p.shape) for p in pre_params])
    lw, kmod, a_s, b_s, bgate = _tw_fwd(_f_rwkv_pre, pre_ins, pre_specs, [jax.ShapeDtypeStruct((t, D), f32)] * 5,
                                        [_row_spec(tmr, D)] * 5, (t // tmr,), "b_pre")
    b_ins = [zf_rkv, kmod, zf_rkv, a_s, b_s, lw]
    b_offs = [0, 0, 2 * D, 0, 0, 0]
    y_scan, b_s0 = _scan_fwd(_rwkv_chunk, b_ins, b_offs, B_HEADS // 2, nc, (2 * B_N, 2 * B_N), "b_scan")

    out_gain_t = jnp.tile(w['a_out_norm'], (1, A_HEADS))
    r_k = w['b_r_k'].reshape(1, D)
    post_params = [out_gain_t, w['b_ln_gain'], w['b_ln_bias'], r_k]
    post_ins = [o_scan, z, y_scan, zf_rkv, kmod, zf_rkv, bgate, z, z] + post_params
    post_specs = ([_row_spec(tmr, D), _row_spec(tmr, D, 3), _row_spec(tmr, D), _row_spec(tmr, D, 0), _row_spec(tmr, D),
                   _row_spec(tmr, D, 2), _row_spec(tmr, D), _row_spec(tmr, D, 7), _row_spec(tmr, D, 8)]
                  + [_full_spec((1, D))] * 4)
    merged = _tw_fwd(_f_mix_post, post_ins, post_specs, [jax.ShapeDtypeStruct((t, D), MXU_DTYPE)],
                     [_row_spec(tmr, D)], (t // tmr,), "mix_post")[0]
    h2 = _matmul(merged, w['w_out'], res=h1, name="out_proj")
    h3, ffn2_saved = _ffn_fwd(h2, w['ffn2_norm'], w['ffn2_wgu'], w['ffn2_wd'], "ffn2")

    tml = _tile(t, 416, 8)
    fnorm = w['final_norm']
    loss_fn = functools.partial(_f_loss, tm=tml)
    loss_specs = [_row_spec(tml, D), _full_spec((1, D)), _row_spec(tml, D)]
    loss_parts, d_h3, grads['final_norm'] = _loss_and_grad(loss_fn, h3, fnorm, tgt, loss_specs, tml)
    loss = jnp.sum(loss_parts)

    d_h2, grads['ffn2_norm'], grads['ffn2_wgu'], grads['ffn2_wd'] = _ffn_bwd(
        h2, w['ffn2_norm'], w['ffn2_wgu'], w['ffn2_wd'], ffn2_saved, d_h3, "ffn2")
    grads['w_out'] = _matmul(merged, d_h2, ta=True, name="d_w_out")
    d_merged = _matmul(d_h2, w['w_out'], tb=True, name="d_merged")

    win = ('tile', (t, D), _row_spec(tmr, D))
    zwin = win + (MXU_DTYPE,)
    post_kinds = ['tile', zwin, 'tile', win, 'tile', win, 'tile', zwin, zwin] + ['acc'] * 4
    (d_o, d_az, d_y, d_r1, d_kmod1, d_v1, d_bgate, d_ga, d_gb,
     d_out_gain_t, grads['b_ln_gain'], grads['b_ln_bias'], d_r_k) = _tw_bwd(
        _f_mix_post, post_ins, post_specs, [d_merged], [_row_spec(tmr, D)], post_kinds, (t // tmr,), "mix_post_bwd")
    grads['a_out_norm'] = jnp.sum(d_out_gain_t.reshape(A_HEADS, A_DK), axis=0, keepdims=True)
    grads['b_r_k'] = d_r_k.reshape(1, B_HEADS, B_N)

    d_r2, d_kmod2, d_v2, d_as, d_bs, d_lw = _scan_bwd(_rwkv_chunk, b_s0, b_ins, b_offs, d_y, B_HEADS // 2, nc,
                                                      (2 * B_N, 2 * B_N), "b_scan_bwd")
    pre_kinds = [win] + ['tile'] * 3 + ['acc'] * 7
    pre_ct_specs = [_row_spec(tmr, D)] * 5
    (d_zf_k, d_wdf, d_adf, d_gdf, grads['b_w0'], grads['b_w_up'], grads['b_a0'], grads['b_a_up'], grads['b_g_up'],
     grads['b_k_k'], grads['b_k_a']) = _tw_bwd(
        _f_rwkv_pre, pre_ins, pre_specs, [d_lw, d_kmod1, d_as, d_bs, d_bgate], pre_ct_specs, pre_kinds, (t // tmr,),
        "b_pre_bwd", ct_extra=[(1, d_kmod2)])
    d_zb_rkv, d_mu_rkv = _shift_bwd3(z, mu_rkv, d_r1, d_r2, d_zf_k, d_v1, d_v2, t)
    d_zf_s = jnp.concatenate([d_wdf, d_adf, d_gdf], axis=1)
    d_zs_b, d_mu_s = _tw_bwd(_f_tshift, [zs_b, mu_s], [_full_spec((t, 288)), _full_spec((1, 288))], [d_zf_s],
                             [_full_spec((t, 288))], ['tile', 'tile'], (1,), "b_shift_s_bwd")
    grads['b_shift_mu'] = jnp.concatenate([d_mu_rkv, d_mu_s], axis=1)

    d_aq, d_ak, d_av, d_beta_h, d_g_h = _scan_bwd(_delta_chunk, a_s0, a_ins, a_offs, d_o, A_HEADS, nc, (A_DK, A_DK),
                                                  "a_scan_bwd")
    d_beta = d_beta_h.reshape(A_HEADS, t).T
    d_gdec = d_g_h.reshape(A_HEADS, t).T
    d_abeta, d_aalpha, grads['a_log_rate'], grads['a_dt_bias'] = _tw_bwd(
        dg_fn, [abeta, aalpha, w['a_log_rate'], w['a_dt_bias']], dg_specs, [d_beta, d_gdec],
        [_row_spec(tmg, A_HEADS)] * 2, ['tile', 'tile', 'acc', 'acc'], (t // tmg,), "a_gates_bwd", with_pid=True)
    d_zqkv, d_conv = [], []
    for idx, (fn, ct) in enumerate(zip(conv_fns, (d_aq, d_ak, d_av))):
        dz_i, dw_i = _conv_bwd(fn, z, conv_w, ct, idx, t)
        d_zqkv.append(dz_i)
        d_conv.append(dw_i)
    grads['a_conv_w'] = jnp.concatenate(d_conv, axis=1)

    d_small = jnp.concatenate([d_zs_b, d_abeta, d_aalpha, jnp.zeros((t, ZP - 9216 - 304), f32)], axis=1)
    d_small = lax.optimization_barrier(d_small.astype(MXU_DTYPE))
    d_z_parts = d_zqkv + [d_az, d_zb_rkv, d_ga, d_gb, d_small]
    d_z = jnp.concatenate([p.astype(MXU_DTYPE) for p in d_z_parts], axis=1)
    grads['w_in_p'] = _matmul(u, d_z, ta=True, name="d_w_in")
    d_u = _matmul(d_z, w['w_in_p'], tb=True, name="d_u")
    d_h1, grads['mix_norm'] = _rms_bwd(h1, w['mix_norm'], d_u, d_h2, "mix_drms")
    d_h0, grads['ffn1_norm'], grads['ffn1_wgu'], grads['ffn1_wd'] = _ffn_bwd(
        h0, w['ffn1_norm'], w['ffn1_wgu'], w['ffn1_wd'], ffn1_saved, d_h1, "ffn1")
    return loss, d_h0, grads


_WIN_SEGMENTS = ((0, 4096), (4112, 7184), (7472, 9520), (7184, 7472), (4096, 4112))


_WIN_SHARD = IN_TOTAL // N_CHIPS


def _win_pieces():
    pieces, pad_at = [], 0
    for a, b in _WIN_SEGMENTS:
        c = a
        while c < b:
            stop = min(b, (c // _WIN_SHARD + 1) * _WIN_SHARD)
            pieces.append((c, pad_at + c - a, stop - c))
            c = stop
        pad_at += b - a
    return pieces


def _win_shards_to_padded(shards):
    parts = [shards[c // _WIN_SHARD][:, c % _WIN_SHARD:c % _WIN_SHARD + n] for c, _, n in _win_pieces()]
    parts.append(jnp.zeros((shards.shape[1], ZP - IN_TOTAL), shards.dtype))
    return jnp.concatenate(parts, axis=1)


def _win_padded_to_shards(w_p):
    by_shard = [[] for _ in range(N_CHIPS)]
    for c, p, n in sorted(_win_pieces()):
        by_shard[c // _WIN_SHARD].append(w_p[:, p:p + n])
    return jnp.stack([jnp.concatenate(parts, axis=1) for parts in by_shard])


def _loss_and_grad(loss_fn, h, gain, tgt, specs, tm):
    t = h.shape[0]
    n = t // tm

    def body(h_ref, g_ref, t_ref, l_ref, dh_ref, dg_ref):
        pid = pl.program_id(0)
        tg = t_ref[...]
        (part,), vjp = jax.vjp(lambda a, b: loss_fn(pid, a, b, tg), h_ref[...], g_ref[...])
        dh, dg = vjp((jnp.ones_like(part),))
        l_ref[...] = part
        dh_ref[...] = dh

        @pl.when(pid == 0)
        def _():
            dg_ref[...] = dg

        @pl.when(pid != 0)
        def _():
            dg_ref[...] += dg

    return pl.pallas_call(
        body, name="loss", grid=(n,), in_specs=specs,
        out_specs=[pl.BlockSpec((None, 1, 1), lambda i: (i, 0, 0)), specs[0], _full_spec(gain.shape)],
        out_shape=[jax.ShapeDtypeStruct((n, 1, 1), f32), jax.ShapeDtypeStruct(h.shape, f32),
                   jax.ShapeDtypeStruct(gain.shape, f32)],
    )(h, gain, tgt)


def _shift_bwd3(z, mu, d_r1, d_r2, d_k, d_v1, d_v2, t):
    nb = D // LANES

    def body(z_ref, mu_ref, r1, r2, kk, v1, v2, dz_ref, dmu_ref):
        j = pl.program_id(0)
        ct = jnp.where(j < nb, r1[...] + r2[...], jnp.where(j < 2 * nb, kk[...], v1[...] + v2[...]))
        _, vjp = jax.vjp(lambda a, b: _f_tshift(a, b), z_ref[...], mu_ref[...])
        dz, dmu = vjp((ct,))
        dz_ref[...] = dz.astype(dz_ref.dtype)
        dmu_ref[...] = dmu

    def window(first):
        return pl.BlockSpec((t, LANES), lambda j, f=first: (0, jnp.clip(j - f * nb, 0, nb - 1)))

    return pl.pallas_call(
        body, name="b_shift_bwd", grid=(3 * nb,),
        in_specs=[_col_spec(t, 32), pl.BlockSpec((1, LANES), lambda j: (0, j)), window(0), window(0), window(1),
                  window(2), window(2)],
        out_specs=[_col_spec(t, 0), pl.BlockSpec((1, LANES), lambda j: (0, j))],
        out_shape=[jax.ShapeDtypeStruct((t, 3 * D), MXU_DTYPE), jax.ShapeDtypeStruct((1, 3 * D), f32)],
    )(z, mu, d_r1, d_r2, d_k, d_v1, d_v2)


def _conv_bwd(fn, z, conv_w, ct, idx, t):
    def body(z_ref, w_ref, ct_ref, dz_ref, dw_ref):
        _, vjp = jax.vjp(lambda a, b: fn(a, b), z_ref[...], w_ref[...])
        dz, dw = vjp((ct_ref[...],))
        dz_ref[...] = dz.astype(dz_ref.dtype)
        dw_ref[...] = dw

    return pl.pallas_call(
        body, name=f"a_conv{idx}_bwd", grid=(A_HEADS,),
        in_specs=[_col_spec(t, 8 * idx), pl.BlockSpec((4, LANES), lambda j, o=8 * idx: (0, j + o)), _col_spec(t, 0)],
        out_specs=[_col_spec(t, 0), pl.BlockSpec((4, LANES), lambda j: (0, j))],
        out_shape=[jax.ShapeDtypeStruct((t, D), MXU_DTYPE), jax.ShapeDtypeStruct((4, D), f32)],
    )(z, conv_w, ct)


def _position():
    return lax.axis_index("x"), lax.axis_index("y"), lax.axis_index("c")


def _flip(v, f):
    return 1 - v if f else v


_CHIP_FLIPS = ((1, 0), (0, 1), (1, 1))


def _gather_chips(arrs, name):
    n = len(arrs)
    assert all(a.shape[0] % 32 == 0 for a in arrs)
    arrs = [a.reshape(2, a.shape[0] // 2, a.shape[1]) for a in arrs]

    def body(*refs):
        ins, outs = refs[:n], refs[n:2 * n]
        send, recv, fsend, frecv, own = refs[2 * n:]
        x, y, c = _position()
        me = 2 * x + y
        sends, plan, owns = [], [], []
        for a in range(n):
            cp = pltpu.make_async_remote_copy(src_ref=ins[a], dst_ref=outs[a].at[me], send_sem=own.at[a, 0],
                                              recv_sem=own.at[a, 1], device_id=(x, y, 1 - c), device_id_type=MESH)
            cp.start()
            owns.append(cp)
            for j, (fx, fy) in enumerate(_CHIP_FLIPS):
                px, py = _flip(x, fx), _flip(y, fy)
                p = 2 * px + py
                cp = pltpu.make_async_remote_copy(src_ref=ins[a].at[c], dst_ref=outs[a].at[me, c],
                                                  send_sem=send.at[a, j], recv_sem=recv.at[a, j],
                                                  device_id=(px, py, c), device_id_type=MESH)
                cp.start()
                sends.append(cp)
                landed = pltpu.make_async_remote_copy(src_ref=ins[a].at[c], dst_ref=outs[a].at[p, c],
                                                      send_sem=send.at[a, j], recv_sem=recv.at[a, j],
                                                      device_id=(px, py, c), device_id_type=MESH)
                onward = pltpu.make_async_remote_copy(src_ref=outs[a].at[p, c], dst_ref=outs[a].at[p, c],
                                                      send_sem=fsend.at[a, j], recv_sem=frecv.at[a, j],
                                                      device_id=(x, y, 1 - c), device_id_type=MESH)
                from_sibling = pltpu.make_async_remote_copy(src_ref=outs[a].at[p, 1 - c], dst_ref=outs[a].at[p, 1 - c],
                                                            send_sem=fsend.at[a, j], recv_sem=frecv.at[a, j],
                                                            device_id=(x, y, 1 - c), device_id_type=MESH)
                plan.append((landed, onward, from_sibling))
        for landed, onward, _ in plan:
            landed.wait_recv()
            onward.start()
        for _, _, from_sibling in plan:
            from_sibling.wait_recv()
        for cp in sends:
            cp.wait_send()
        for _, onward, _ in plan:
            onward.wait_send()
        for cp in owns:
            cp.wait()

    sems = [pltpu.SemaphoreType.DMA((n, 3))] * 4 + [pltpu.SemaphoreType.DMA((n, 2))]
    outs = pl.pallas_call(
        body, name=name, in_specs=[ANY] * n, out_specs=[ANY] * n,
        out_shape=[jax.ShapeDtypeStruct((N_CHIPS,) + a.shape, a.dtype) for a in arrs], scratch_shapes=sems,
    )(*arrs)
    return [o.reshape(N_CHIPS, o.shape[1] * o.shape[2], o.shape[3]) for o in outs]


def _swap_sibling(arrs, src_of, shapes, name):
    n = len(arrs)

    def body(*refs):
        a_refs, got_refs = refs[:n], refs[n:2 * n]
        send, recv = refs[2 * n:]
        x, y, c = _position()
        copies = []
        for i in range(n):
            cp = pltpu.make_async_remote_copy(src_ref=src_of(a_refs[i], c), dst_ref=got_refs[i], send_sem=send.at[i],
                                              recv_sem=recv.at[i], device_id=(x, y, 1 - c), device_id_type=MESH)
            cp.start()
            copies.append(cp)
        for cp in copies:
            cp.wait()

    return pl.pallas_call(body, name=name, in_specs=[ANY] * n, out_specs=[ANY] * n,
                          out_shape=[jax.ShapeDtypeStruct(sh, a.dtype) for sh, a in zip(shapes, arrs)],
                          scratch_shapes=[pltpu.SemaphoreType.DMA((n,))] * 2)(*arrs)


def _row_tile(rows, width):
    return _tile(rows, max(16, (784 * LANES // width) // 16 * 16), 16)


def _add_halves(g, got, dtype, name):
    n, _, hr, w = g.shape
    tr = _row_tile(hr, w)

    def body(g_ref, got_ref, o_ref):
        c = lax.axis_index("c")
        own = jnp.where(c == 0, g_ref[:, 0], g_ref[:, 1])
        o_ref[...] = (own + got_ref[...]).astype(dtype)

    return pl.pallas_call(
        body, name=name, grid=(hr // tr,),
        in_specs=[pl.BlockSpec((n, 2, tr, w), lambda i: (0, 0, i, 0)), pl.BlockSpec((n, tr, w), lambda i: (0, i, 0))],
        out_specs=pl.BlockSpec((n, tr, w), lambda i: (0, i, 0)),
        out_shape=jax.ShapeDtypeStruct((n, hr, w), dtype))(g, got)


def _scatter_chips(gs, name):
    n = len(gs)

    def body(*refs):
        g_refs, out_refs = refs[:n], refs[n:2 * n]
        send, recv = refs[2 * n:]
        x, y, c = _position()
        sends = []
        for i in range(n):
            for j, (fx, fy) in enumerate(_CHIP_FLIPS):
                px, py = _flip(x, fx), _flip(y, fy)
                cp = pltpu.make_async_remote_copy(src_ref=g_refs[i].at[2 * px + py], dst_ref=out_refs[i].at[j],
                                                  send_sem=send.at[i, j], recv_sem=recv.at[i, j],
                                                  device_id=(px, py, c), device_id_type=MESH)
                cp.start()
                sends.append(cp)
        for cp in sends:
            cp.wait_recv()
        for cp in sends:
            cp.wait_send()

    return pl.pallas_call(
        body, name=name, in_specs=[ANY] * n, out_specs=[ANY] * n,
        out_shape=[jax.ShapeDtypeStruct((3,) + g.shape[1:], g.dtype) for g in gs],
        scratch_shapes=[pltpu.SemaphoreType.DMA((n, 3)), pltpu.SemaphoreType.DMA((n, 3))],
    )(*gs)


def _sum_own_and_slots(own, got, name):
    n, r, w = own.shape
    tr = _row_tile(r, w)

    def body(own_ref, got_ref, o_ref):
        me = 2 * lax.axis_index("x") + lax.axis_index("y")
        acc = own_ref[0]
        for i in range(1, n):
            acc = jnp.where(me == i, own_ref[i], acc)
        acc = acc.astype(f32)
        for j in range(3):
            acc = acc + got_ref[j].astype(f32)
        o_ref[...] = acc

    return pl.pallas_call(
        body, name=name, grid=(r // tr,),
        in_specs=[pl.BlockSpec((n, tr, w), lambda i: (0, i, 0)), pl.BlockSpec((3, tr, w), lambda i: (0, i, 0))],
        out_specs=pl.BlockSpec((tr, w), lambda i: (i, 0)), out_shape=jax.ShapeDtypeStruct((r, w), f32))(own, got)


def _share_chips(a, name):
    def body(a_ref, out_ref, send, recv):
        x, y, c = _position()
        sends = []
        for j, (fx, fy) in enumerate(_CHIP_FLIPS):
            cp = pltpu.make_async_remote_copy(src_ref=a_ref, dst_ref=out_ref.at[j], send_sem=send.at[j],
                                              recv_sem=recv.at[j], device_id=(_flip(x, fx), _flip(y, fy), c),
                                              device_id_type=MESH)
            cp.start()
            sends.append(cp)
        for cp in sends:
            cp.wait_recv()
        for cp in sends:
            cp.wait_send()

    return pl.pallas_call(
        body, name=name, in_specs=[ANY], out_specs=ANY, out_shape=jax.ShapeDtypeStruct((3,) + a.shape, a.dtype),
        scratch_shapes=[pltpu.SemaphoreType.DMA((3,)), pltpu.SemaphoreType.DMA((3,))],
    )(a)


def _sum_in_chip_order(pair, got, name):
    r, w = pair.shape
    tr = _tile(r, 1408, 8)

    def body(p_ref, g_ref, o_ref):
        x, y = lax.axis_index("x"), lax.axis_index("y")
        me = 2 * x + y
        across = [2 * _flip(x, fx) + _flip(y, fy) for fx, fy in _CHIP_FLIPS]
        acc = None
        for i in range(N_CHIPS):
            term = p_ref[...]
            for j in range(3):
                term = jnp.where(across[j] == i, g_ref[j], term)
            acc = term if acc is None else acc + term
        o_ref[...] = acc

    return pl.pallas_call(
        body, name=name, grid=(r // tr,),
        in_specs=[pl.BlockSpec((tr, w), lambda i: (i, 0)), pl.BlockSpec((3, tr, w), lambda i: (0, i, 0))],
        out_specs=pl.BlockSpec((tr, w), lambda i: (i, 0)), out_shape=jax.ShapeDtypeStruct((r, w), f32))(pair, got)


def _add2(a, b, name):
    r, w = a.shape
    tr = _tile(r, 1408, 8)
    spec = pl.BlockSpec((tr, w), lambda i: (i, 0))

    def body(a_ref, b_ref, o_ref):
        o_ref[...] = a_ref[...] + b_ref[...]

    return pl.pallas_call(body, name=name, grid=(r // tr,), in_specs=[spec, spec], out_specs=spec,
                          out_shape=jax.ShapeDtypeStruct(a.shape, f32))(a, b)


def _adamw(w, g_parts, m, v, name):
    shape = w.shape
    view = shape if len(shape) >= 2 else (1,) + shape
    assert all(d == 1 for d in view[:-2]), shape
    rows, cols = view[-2:]
    cap = max(8, (256 * 1024 // cols) // 8 * 8)
    tr = rows if rows <= cap else _tile(rows, cap, 8)
    lead = len(view) - 2
    n_g = len(g_parts)

    def body(*refs):
        w_ref = refs[0]
        g_refs = refs[1:1 + n_g]
        m_ref, v_ref, g_out, d_out, m_out, v_out = refs[1 + n_g:]
        g = g_refs[0][...]
        for gr in g_refs[1:]:
            g = g + gr[...]
        m_new = ADAM_B1 * m_ref[...] + (1.0 - ADAM_B1) * g
        v_new = ADAM_B2 * v_ref[...] + (1.0 - ADAM_B2) * (g * g)
        m_hat = m_new / (1.0 - ADAM_B1 ** ADAM_STEP)
        v_hat = v_new / (1.0 - ADAM_B2 ** ADAM_STEP)
        g_out[...] = g
        d_out[...] = -ADAM_LR * (m_hat / (jnp.sqrt(v_hat) + ADAM_EPS) + ADAM_WD * w_ref[...])
        m_out[...] = m_new
        v_out[...] = v_new

    spec = pl.BlockSpec((None,) * lead + (tr, cols), lambda i: (0,) * lead + (i, 0))
    args = [w.reshape(view)] + [g.reshape(view) for g in g_parts] + [m.reshape(view), v.reshape(view)]
    outs = pl.pallas_call(body, name=name, grid=(rows // tr,), in_specs=[spec] * len(args), out_specs=[spec] * 4,
                          out_shape=[jax.ShapeDtypeStruct(view, f32)] * 4)(*args)
    return [o.reshape(shape) for o in outs]


_BIG = ('ffn1_w_gu', 'ffn1_w_down', 'w_in', 'w_out', 'ffn2_w_gu', 'ffn2_w_down')
_SMALL_SHARDED = ('meta_tokens', 'a_conv_w', 'b_w_up', 'b_a_up', 'b_g_up')
_WEIGHTS = ('meta_tokens', 'ffn1_norm', 'ffn1_w_gu', 'ffn1_w_down', 'mix_norm', 'w_in', 'a_conv_w', 'a_log_rate',
            'a_dt_bias', 'a_out_norm', 'b_shift_mu', 'b_w0', 'b_w_up', 'b_a0', 'b_a_up', 'b_g_up', 'b_k_k', 'b_k_a',
            'b_r_k', 'b_ln_gain', 'b_ln_bias', 'w_out', 'ffn2_norm', 'ffn2_w_gu', 'ffn2_w_down', 'final_norm')
_SMALL = tuple(n for n in _WEIGHTS if n not in _BIG)


def _rows_of(shape):
    n = 1
    for d in shape:
        n *= d
    return n, -(-n // LANES)


def _pack(arrs, dtype, row_mult=32):
    parts, total = [], 0
    for a in arrs:
        n, rows = _rows_of(a.shape)
        flat = a.reshape(-1).astype(dtype)
        if n % LANES:
            flat = jnp.pad(flat, (0, rows * LANES - n))
        parts.append(flat)
        total += rows
    extra = -total % row_mult
    if extra:
        parts.append(jnp.zeros((extra * LANES,), dtype))
    return jnp.concatenate(parts).reshape(total + extra, LANES)


def _unpack(packed, shapes, lead=()):
    out, off = [], 0
    for sh in shapes:
        n, rows = _rows_of(sh)
        seg = packed[..., off:off + rows, :]
        if n % LANES:
            seg = seg.reshape(lead + (-1,))[..., :n]
        out.append(seg.reshape(lead + tuple(sh)))
        off += rows
    return out


def _cols_from_shards(s):
    return jnp.concatenate([s[i] for i in range(N_CHIPS)], axis=-1)


def kernel(x, meta_tokens, ffn1_norm, ffn1_w_gu, ffn1_w_down, mix_norm, w_in, a_conv_w, a_log_rate, a_dt_bias, a_out_norm, b_shift_mu, b_w0, b_w_up, b_a0, b_a_up, b_g_up, b_k_k, b_k_a, b_r_k, b_ln_gain, b_ln_bias, w_out, ffn2_norm, ffn2_w_gu, ffn2_w_down, final_norm, loss_target, m_meta_tokens, m_ffn1_norm, m_ffn1_w_gu, m_ffn1_w_down, m_mix_norm, m_w_in, m_a_conv_w, m_a_log_rate, m_a_dt_bias, m_a_out_norm, m_b_shift_mu, m_b_w0, m_b_w_up, m_b_a0, m_b_a_up, m_b_g_up, m_b_k_k, m_b_k_a, m_b_r_k, m_b_ln_gain, m_b_ln_bias, m_w_out, m_ffn2_norm, m_ffn2_w_gu, m_ffn2_w_down, m_final_norm, v_meta_tokens, v_ffn1_norm, v_ffn1_w_gu, v_ffn1_w_down, v_mix_norm, v_w_in, v_a_conv_w, v_a_log_rate, v_a_dt_bias, v_a_out_norm, v_b_shift_mu, v_b_w0, v_b_w_up, v_b_a0, v_b_a_up, v_b_g_up, v_b_k_k, v_b_k_a, v_b_r_k, v_b_ln_gain, v_b_ln_bias, v_w_out, v_ffn2_norm, v_ffn2_w_gu, v_ffn2_w_down, v_final_norm):
    args = locals()
    wts = {n: args[n] for n in _WEIGHTS}
    mom = {n: args["m_" + n] for n in _WEIGHTS}
    var = {n: args["v_" + n] for n in _WEIGHTS}
    chip = 2 * lax.axis_index("x") + lax.axis_index("y")

    big_shapes = [wts[n].shape[1:] for n in _BIG]
    small_shapes = [wts[n].shape[-2:] for n in _SMALL_SHARDED]
    big_flat = [wts[n].astype(bf16).reshape(wts[n].shape[1:]) for n in _BIG]
    small_packed = _pack([wts[n] for n in _SMALL_SHARDED], f32)
    gathered = _gather_chips(big_flat + [small_packed], "gather_weights")
    gu1, dn1, w_in_s, w_out_s, gu2, dn2 = [a.reshape((N_CHIPS,) + tuple(sh)) for a, sh in zip(gathered, big_shapes)]
    meta_s, conv_s, wup_s, aup_s, gup_s = _unpack(gathered[-1], small_shapes, (N_CHIPS,))
    w = {
        'ffn1_norm': ffn1_norm, 'mix_norm': mix_norm, 'ffn2_norm': ffn2_norm, 'final_norm': final_norm[None, :],
        'ffn1_wgu': gu1, 'ffn1_wd': dn1.reshape(D_FF, D), 'ffn2_wgu': gu2, 'ffn2_wd': dn2.reshape(D_FF, D),
        'w_in_p': _win_shards_to_padded(w_in_s), 'w_out': w_out_s.reshape(D, D),
        'a_conv_w': _cols_from_shards(conv_s), 'b_w_up': _cols_from_shards(wup_s), 'b_a_up': _cols_from_shards(aup_s),
        'b_g_up': _cols_from_shards(gup_s),
        'a_log_rate': a_log_rate, 'a_dt_bias': a_dt_bias, 'a_out_norm': a_out_norm, 'b_shift_mu': b_shift_mu,
        'b_w0': b_w0, 'b_a0': b_a0, 'b_k_k': b_k_k, 'b_k_a': b_k_a, 'b_r_k': b_r_k, 'b_ln_gain': b_ln_gain,
        'b_ln_bias': b_ln_bias,
    }
    meta_full = _cols_from_shards(meta_s)

    h0 = jnp.concatenate([jnp.zeros((PAD, D), f32), meta_full, x[0]], axis=0)
    tgt = jnp.concatenate([jnp.zeros((SKIP, D), f32), loss_target[0]], axis=0)
    loss_local, d_h0, g = _local_step(h0, tgt, w)
    loss = lax.psum(loss_local, ("x", "y", "c"))
    grad_x = d_h0[SKIP:][None]

    big_grads = [
        g['ffn1_wgu'],
        g['ffn1_wd'].reshape(N_CHIPS, D_FF // N_CHIPS, D),
        _win_padded_to_shards(g['w_in_p']),
        g['w_out'].reshape(N_CHIPS, D // N_CHIPS, D),
        g['ffn2_wgu'],
        g['ffn2_wd'].reshape(N_CHIPS, D_FF // N_CHIPS, D),
    ]
    g_halves = [a.reshape(N_CHIPS, 2, a.shape[1] // 2, a.shape[2]) for a in big_grads]
    sib_halves = _swap_sibling(g_halves, lambda ref, c: ref.at[:, 1 - c], [a.shape[:1] + a.shape[2:] for a in g_halves],
                               "swap_halves")
    chip_halves = [_add_halves(a, b, bf16, f"add_sibling{i}") for i, (a, b) in enumerate(zip(g_halves, sib_halves))]
    got = _scatter_chips(chip_halves, "scatter_grads")
    mine = [_sum_own_and_slots(a, b, f"sum_chips{i}") for i, (a, b) in enumerate(zip(chip_halves, got))]
    theirs = _swap_sibling(mine, lambda ref, c: ref, [a.shape for a in mine], "swap_sums")
    core = lax.axis_index("c")
    big_parts = [jnp.concatenate([jnp.where(core == 0, a, b), jnp.where(core == 0, b, a)], axis=0)
                 for a, b in zip(mine, theirs)]

    small_full = {
        'meta_tokens': d_h0[PAD:SKIP], 'ffn1_norm': g['ffn1_norm'], 'mix_norm': g['mix_norm'], 'a_conv_w': g['a_conv_w'],
        'a_log_rate': g['a_log_rate'], 'a_dt_bias': g['a_dt_bias'], 'a_out_norm': g['a_out_norm'],
        'b_shift_mu': g['b_shift_mu'], 'b_w0': g['b_w0'], 'b_w_up': g['b_w_up'], 'b_a0': g['b_a0'], 'b_a_up': g['b_a_up'],
        'b_g_up': g['b_g_up'], 'b_k_k': g['b_k_k'], 'b_k_a': g['b_k_a'], 'b_r_k': g['b_r_k'], 'b_ln_gain': g['b_ln_gain'],
        'b_ln_bias': g['b_ln_bias'], 'ffn2_norm': g['ffn2_norm'], 'final_norm': g['final_norm'],
    }
    s_shapes = [small_full[n].shape for n in _SMALL]
    s_packed = _pack([small_full[n] for n in _SMALL], f32, row_mult=256)
    (s_sib,) = _swap_sibling([s_packed], lambda ref, c: ref, [s_packed.shape], "swap_small")
    s_pair = _add2(s_packed, s_sib, "add_small")
    s_sum = _sum_in_chip_order(s_pair, _share_chips(s_pair, "share_small"), "sum_small")
    s_parts = dict(zip(_SMALL, _unpack(s_sum, s_shapes)))

    grad, delta, new_m, new_v = {}, {}, {}, {}
    for n, a in zip(_BIG, big_parts):
        grad[n], delta[n], new_m[n], new_v[n] = _adamw(wts[n], [a.reshape(wts[n].shape)], mom[n], var[n], f"adamw_{n}")
    for n in _SMALL:
        gs = s_parts[n]
        if n in _SMALL_SHARDED:
            width = wts[n].shape[-1]
            gs = lax.dynamic_slice_in_dim(gs, chip * width, width, axis=gs.ndim - 1)
        gs = gs.reshape(wts[n].shape)
        grad[n], delta[n], new_m[n], new_v[n] = _adamw(wts[n], [gs], mom[n], var[n], f"adamw_{n}")

    return (loss, grad_x, *[grad[n] for n in _WEIGHTS], *[delta[n] for n in _WEIGHTS],
            *[new_m[n] for n in _WEIGHTS], *[new_v[n] for n in _WEIGHTS])
```

```python
import functools

import jax
import jax.numpy as jnp
from jax import lax
from jax.experimental import pallas as pl
from jax.experimental.pallas import tpu as pltpu

f32 = jnp.float32
bf16 = jnp.bfloat16
HI = lax.Precision.HIGHEST
MESH = pl.DeviceIdType.MESH
ANY = pl.BlockSpec(memory_space=pl.ANY)

D = 1024
N_META = 16
CHUNK = 64
PAD = CHUNK - N_META
SKIP = PAD + N_META
EPS = 1e-6
D_FF = 2816
A_HEADS = 8
A_DK = 128
B_HEADS = 16
B_N = 64
B_GN_EPS = B_N * 1e-5
IN_TOTAL = 9520
ZP = 9600
LANES = 128
N_CHIPS = 4

ADAM_LR, ADAM_B1, ADAM_B2, ADAM_EPS, ADAM_WD, ADAM_STEP = 0.001, 0.9, 0.999, 1e-08, 0.01, 10

MXU_DTYPE = bf16


def _tile(n, cap, mult):
    if n <= cap:
        return n
    best = None
    for t in range(mult, cap + 1, mult):
        if n % t == 0:
            best = t
    assert best is not None, (n, cap, mult)
    return best


def _sigmoid(x):
    return jax.nn.sigmoid(x)


def _silu(x):
    return x * jax.nn.sigmoid(x)


def _softplus(x):
    return jnp.maximum(x, 0.0) + jnp.log(1.0 + jnp.exp(-jnp.abs(x)))


def _head_matrix(c, nh):
    hd = c // nh
    r = lax.broadcasted_iota(jnp.int32, (c, nh), 0)
    h = lax.broadcasted_iota(jnp.int32, (c, nh), 1)
    return (r >= h * hd) & (r < (h + 1) * hd)


def _dot_exact_rhs(x, e, cb):
    dn = (((1,), (cb,)), ((), ()))
    if SCAN_PASSES == 0:
        return lax.dot_general(x, e.astype(f32), dn, precision=HI, preferred_element_type=f32)
    eb = e.astype(bf16)
    hi = x.astype(bf16)
    lo = (x - hi.astype(f32)).astype(bf16)
    return (lax.dot_general(hi, eb, dn, preferred_element_type=f32)
            + lax.dot_general(lo, eb, dn, preferred_element_type=f32))


def _head_sum_impl(x, nh):
    e = _head_matrix(x.shape[-1], nh)
    return _dot_exact_rhs(_dot_exact_rhs(x, e, 0), e, 1)


@functools.partial(jax.custom_vjp, nondiff_argnums=(1,))
def _head_sum(x, nh):
    return _head_sum_impl(x, nh)


def _head_sum_fwd(x, nh):
    return _head_sum_impl(x, nh), None


def _head_sum_bwd(nh, _, g):
    return (_head_sum_impl(g, nh),)


_head_sum.defvjp(_head_sum_fwd, _head_sum_bwd)


@functools.partial(jax.custom_vjp, nondiff_argnums=(1,))
def _shift_rows(x, s):
    n = x.shape[0]
    row = lax.broadcasted_iota(jnp.int32, x.shape, 0)
    if s > 0:
        return jnp.where(row >= s, pltpu.roll(x, s, 0), 0.0)
    return jnp.where(row < n + s, pltpu.roll(x, n + s, 0), 0.0)


def _shift_rows_fwd(x, s):
    return _shift_rows(x, s), None


def _shift_rows_bwd(s, _, g):
    return (_shift_rows(g, -s),)


_shift_rows.defvjp(_shift_rows_fwd, _shift_rows_bwd)


def _matmul(a, b, *, ta=False, tb=False, res=None, scale=1.0, name, b_cols_split=None, out_cols_split=None,
            out_into=None):
    assert not (ta and tb)
    (ar, ac) = a.shape
    b0 = 0
    if b_cols_split:
        b0, bs = b_cols_split
        _, br, bc_part = b.shape
        bc = bs * bc_part
    else:
        br, bc = b.shape
    m, k = (ac, ar) if ta else (ar, ac)
    n, kb = (br, bc) if tb else (bc, br)
    assert k == kb, (a.shape, b.shape, ta, tb)
    tm = _tile(m, 1408, LANES) if ta else _tile(m, 832, 8)
    tn = _tile(n, 1920, LANES)
    tk = _tile(k, 1040, 8) if ta else _tile(k, 1920, LANES)
    nk = k // tk
    dn = (((0 if ta else 1,), (1 if tb else 0,)), ((), ()))
    if b_cols_split:
        assert (tk if tb else tn) == bc_part, (b.shape, tn, tk)

    def body(*refs):
        a_ref, b_ref = refs[:2]
        r_ref = refs[2] if res is not None else None
        o_ref, acc = refs[-2:]
        kk = pl.program_id(2)

        @pl.when(kk == 0)
        def _():
            acc[...] = jnp.zeros_like(acc)

        acc[...] += lax.dot_general(a_ref[...].astype(MXU_DTYPE), b_ref[...].astype(MXU_DTYPE), dn,
                                    preferred_element_type=f32,
                                    precision=None if MXU_DTYPE == bf16 else HI)

        @pl.when(kk == nk - 1)
        def _():
            out = acc[...]
            if scale != 1.0:
                out = out * scale
            if res is not None:
                out = r_ref[...] + out
            o_ref[...] = out

    if ta:
        a_spec = pl.BlockSpec((tk, tm), lambda i, j, kk: (kk, i))
    else:
        a_spec = pl.BlockSpec((tm, tk), lambda i, j, kk: (i, kk))
    if tb and b_cols_split:
        b_spec = pl.BlockSpec((None, tn, tk), lambda i, j, kk: (kk + b0, j, 0))
    elif tb:
        b_spec = pl.BlockSpec((tn, tk), lambda i, j, kk: (j, kk))
    elif b_cols_split:
        b_spec = pl.BlockSpec((None, tk, tn), lambda i, j, kk: (j + b0, kk, 0))
    else:
        b_spec = pl.BlockSpec((tk, tn), lambda i, j, kk: (kk, j))
    in_specs = [a_spec, b_spec]
    args = [a, b]
    if res is not None:
        in_specs.append(pl.BlockSpec((tm, tn), lambda i, j, kk: (i, j)))
        args.append(res)
    aliases = {}
    if out_cols_split:
        o0, total = out_cols_split
        out_spec = pl.BlockSpec((None, tm, tn), lambda i, j, kk: (j + o0, i, 0))
        out_shape = jax.ShapeDtypeStruct((total, m, tn), f32)
        if out_into is not None:
            assert out_into.shape == out_shape.shape
            in_specs.append(ANY)
            args.append(out_into)
            aliases = {len(args) - 1: 0}
    else:
        out_spec = pl.BlockSpec((tm, tn), lambda i, j, kk: (i, j))
        out_shape = jax.ShapeDtypeStruct((m, n), f32)
    return pl.pallas_call(
        body, name=name, grid=(m // tm, n // tn, nk), in_specs=in_specs, out_specs=out_spec, out_shape=out_shape,
        scratch_shapes=[pltpu.VMEM((tm, tn), f32)], input_output_aliases=aliases,
        compiler_params=pltpu.CompilerParams(dimension_semantics=("parallel", "parallel", "arbitrary")),
    )(*args)


def _tw_fwd(fn, ins, in_specs, out_shapes, out_specs, grid, name, with_pid=False):
    n_in = len(ins)

    def body(*refs):
        vals = [r[...] for r in refs[:n_in]]
        outs = fn(pl.program_id(0), *vals) if with_pid else fn(*vals)
        for r, o in zip(refs[n_in:], outs):
            r[...] = o.astype(r.dtype)

    return pl.pallas_call(body, name=name, grid=grid, in_specs=in_specs, out_specs=out_specs,
                          out_shape=out_shapes)(*ins)


def _tw_bwd(fn, ins, in_specs, cts, ct_specs, kinds, grid, name, with_pid=False, tile_dtype=f32, ct_extra=(),
            residual=None):
    n_in, n_ct = len(ins), len(cts)
    diff = [i for i, kd in enumerate(kinds) if kd is not None]
    n_ex = len(ct_extra)

    def body(*refs):
        vals = [r[...] for r in refs[:n_in]]
        ctv = [r[...].astype(f32) for r in refs[n_in:n_in + n_ct]]
        for (ci, _), r in zip(ct_extra, refs[n_in + n_ct:n_in + n_ct + n_ex]):
            ctv[ci] = ctv[ci] + r[...]
        ctv = tuple(ctv)
        n_fixed = n_in + n_ct + n_ex
        res_ref = refs[n_fixed] if residual is not None else None
        g_refs = refs[n_fixed + (residual is not None):]
        pid = pl.program_id(0)

        def f(*dv):
            full = list(vals)
            for i, v in zip(diff, dv):
                full[i] = v
            out = fn(pid, *full) if with_pid else fn(*full)
            return tuple(out)

        _, vjp = jax.vjp(f, *[vals[i] for i in diff])
        gs = vjp(ctv)
        first = pid == 0
        for i2 in range(1, len(grid)):
            first = first & (pl.program_id(i2) == 0)
        for i, g, g_ref in zip(diff, gs, g_refs):
            if kinds[i] != 'acc':
                if i == 0 and res_ref is not None:
                    g = res_ref[...] + g
                g_ref[...] = g.astype(g_ref.dtype)
            else:
                @pl.when(first)
                def _(g=g, g_ref=g_ref):
                    g_ref[...] = g

                @pl.when(jnp.logical_not(first))
                def _(g=g, g_ref=g_ref):
                    g_ref[...] += g

    zero_map = {1: lambda *a: (0,), 2: lambda *a: (0, 0), 3: lambda *a: (0, 0, 0)}
    out_specs, out_shapes = [], []
    for i in diff:
        if kinds[i] == 'tile':
            out_shapes.append(jax.ShapeDtypeStruct(ins[i].shape, tile_dtype))
            out_specs.append(in_specs[i])
        elif kinds[i] == 'acc':
            out_shapes.append(jax.ShapeDtypeStruct(ins[i].shape, f32))
            out_specs.append(pl.BlockSpec(ins[i].shape, zero_map[ins[i].ndim]))
        else:
            out_shapes.append(jax.ShapeDtypeStruct(kinds[i][1], kinds[i][3] if len(kinds[i]) > 3 else tile_dtype))
            out_specs.append(kinds[i][2])
    extra_specs = [ct_specs[ci] for ci, _ in ct_extra]
    extra = [a for _, a in ct_extra]
    if residual is not None:
        assert kinds[0] == 'tile'
        extra_specs.append(in_specs[0])
        extra.append(residual)
    return pl.pallas_call(body, name=name, grid=grid, in_specs=list(in_specs) + list(ct_specs) + extra_specs,
                          out_specs=out_specs, out_shape=out_shapes)(*ins, *cts, *extra)


def _row_spec(tm, c, col_block=0):
    return pl.BlockSpec((tm, c), lambda i, cb=col_block: (i, cb))


def _full_spec(shape):
    nd = len(shape)
    return pl.BlockSpec(shape, lambda *a, nd=nd: (0,) * nd)


def _f_rms(x, g):
    return (x * lax.rsqrt(jnp.mean(x * x, axis=-1, keepdims=True) + EPS) * g,)


def _f_swiglu(gate, up):
    return (_silu(gate) * up,)


def _f_loss(pid, h, g, tgt, *, tm):
    y = h * lax.rsqrt(jnp.mean(h * h, axis=-1, keepdims=True) + EPS) * g
    row = pid * tm + lax.broadcasted_iota(jnp.int32, (tm, 1), 0)
    err = jnp.where(row >= SKIP, y - tgt, 0.0)
    per_row = jnp.mean(err * err, axis=-1, keepdims=True)
    return (0.5 * jnp.sum(per_row, axis=0, keepdims=True),)


def _f_conv(x, w, *, norm, scale):
    y = x * w[3:4, :]
    for s in (1, 2, 3):
        y = y + _shift_rows(x, s) * w[3 - s:4 - s, :]
    y = _silu(y)
    if norm:
        y = y * lax.rsqrt(jnp.sum(y * y, axis=-1, keepdims=True) + 1e-6) * scale
    return (y,)


def _f_dgates(pid, abeta, aalpha, log_rate, dt_bias, *, tm):
    row = pid * tm + lax.broadcasted_iota(jnp.int32, (tm, 1), 0)
    live = row >= PAD
    beta = jnp.where(live, _sigmoid(abeta), 0.0)
    g = jnp.where(live, -jnp.exp(log_rate) * _softplus(aalpha + dt_bias), 0.0)
    return beta, g


def _f_tshift(z, mu):
    return (z + (_shift_rows(z, 1) - z) * mu,)


def _f_rwkv_pre(k, wd, ad, gd, w0, w_up, a0, a_up, g_up, k_k, k_a):
    w_log = -_softplus(-(w0 + _smm(jnp.tanh(wd), w_up, 1))) - 0.5
    lw = -jnp.exp(w_log)
    a_lr = _sigmoid(a0 + _smm(ad, a_up, 1))
    gate = _smm(_sigmoid(gd), g_up, 1)
    kkp = k * k_k
    kk = kkp * lax.rsqrt(_head_sum(kkp * kkp, B_HEADS) + 1e-6)
    kmod = k * (1.0 + (a_lr - 1.0) * k_a)
    return lw, kmod, -kk, kk * a_lr, gate


def _f_mix_post(o, az, y, r, kmod, v, gate, ga, gb, out_gain, ln_g, ln_b, r_k):
    ms = _head_sum(o * o, A_HEADS) * (1.0 / A_DK)
    oa = o * lax.rsqrt(ms + EPS) * out_gain * _silu(az)
    mean = _head_sum(y, B_HEADS) * (1.0 / B_N)
    yc = y - mean
    var = _head_sum(yc * yc, B_HEADS) * (1.0 / B_N)
    yn = yc * lax.rsqrt(var + B_GN_EPS) * ln_g + ln_b
    bonus = _head_sum(r * kmod * r_k, B_HEADS) * v
    ob = (yn + bonus) * gate
    return (_sigmoid(ga) * oa + _sigmoid(gb) * ob,)


SCAN_PASSES = 3


def _split2(a):
    hi = a.astype(bf16)
    return hi, (a - hi.astype(f32)).astype(bf16)


def _dot_passes(a, b, ca, cb, passes):
    dn = (((ca,), (cb,)), ((), ()))
    if SCAN_PASSES == 0:
        return lax.dot_general(a, b, dn, precision=HI, preferred_element_type=f32)
    if passes == 1:
        return lax.dot_general(a.astype(bf16), b.astype(bf16), dn, preferred_element_type=f32)
    ah, al = _split2(a)
    bh, bl = _split2(b)
    return (lax.dot_general(ah, bh, dn, preferred_element_type=f32)
            + (lax.dot_general(ah, bl, dn, preferred_element_type=f32)
               + lax.dot_general(al, bh, dn, preferred_element_type=f32)))


@functools.partial(jax.custom_vjp, nondiff_argnums=(2, 3, 4))
def _sdot(a, b, ca, cb, passes):
    return _dot_passes(a, b, ca, cb, passes)


def _sdot_fwd(a, b, ca, cb, passes):
    return _dot_passes(a, b, ca, cb, passes), (a, b)


def _sdot_bwd(ca, cb, passes, res, g):
    a, b = res
    if (ca, cb) == (1, 0):
        return _dot_passes(g, b, 1, 1, passes), _dot_passes(a, g, 0, 0, passes)
    if (ca, cb) == (1, 1):
        return _dot_passes(g, b, 1, 0, passes), _dot_passes(g, a, 0, 0, passes)
    assert (ca, cb) == (0, 0)
    return _dot_passes(b, g, 1, 1, passes), _dot_passes(a, g, 1, 0, passes)


_sdot.defvjp(_sdot_fwd, _sdot_bwd)


def _smm(a, b, passes=3):
    return _sdot(a, b, 1, 0, passes)


def _smm_nt(a, b, passes=3):
    return _sdot(a, b, 1, 1, passes)


def _smm_tn(a, b, passes=3):
    return _sdot(a, b, 0, 0, passes)


def _tri_dot(x, ca):
    n = x.shape[0]
    incl = _tri_masks(n)[0]
    dn = (((ca,), (0,)), ((), ()))
    if SCAN_PASSES == 0:
        return lax.dot_general(incl.astype(f32), x, dn, precision=HI, preferred_element_type=f32)
    tri = incl.astype(bf16)
    hi, r1 = x.astype(bf16), None
    r1 = x - hi.astype(f32)
    mid = r1.astype(bf16)
    lo = (r1 - mid.astype(f32)).astype(bf16)
    return (lax.dot_general(tri, hi, dn, preferred_element_type=f32)
            + (lax.dot_general(tri, mid, dn, preferred_element_type=f32)
               + lax.dot_general(tri, lo, dn, preferred_element_type=f32)))


@jax.custom_vjp
def _cumsum_rows(x):
    return _tri_dot(x, 1)


def _cumsum_rows_fwd(x):
    return _tri_dot(x, 1), None


def _cumsum_rows_bwd(_, g):
    return (_tri_dot(g, 0),)


_cumsum_rows.defvjp(_cumsum_rows_fwd, _cumsum_rows_bwd)


def _tri_masks(n):
    i = lax.broadcasted_iota(jnp.int32, (n, n), 0)
    j = lax.broadcasted_iota(jnp.int32, (n, n), 1)
    return i >= j, i > j, i == j, i <= j


def _unit_lower_inv_impl(low, passes):
    n = low.shape[0]
    assert n == CHUNK
    _, _, eye, _ = _tri_masks(n)
    acc = eye.astype(f32) + low
    p = low
    for _ in range(5):
        p = _dot_passes(p, p, 1, 0, passes)
        acc = acc + _dot_passes(acc, p, 1, 0, passes)
    return acc


@functools.partial(jax.custom_vjp, nondiff_argnums=(1,))
def _unit_lower_inv(low, passes=3):
    return _unit_lower_inv_impl(low, passes)


def _unit_lower_inv_fwd(low, passes):
    t = _unit_lower_inv_impl(low, passes)
    return t, t


def _unit_lower_inv_bwd(passes, t, g):
    return (_dot_passes(_dot_passes(t, g, 0, 0, passes), t, 1, 1, passes),)


_unit_lower_inv.defvjp(_unit_lower_inv_fwd, _unit_lower_inv_bwd)

DELTA_PASSES = 1
DELTA_INV_PASSES = 1


def _delta_chunk(s, q, k, v, beta_row, g_row):
    p = DELTA_PASSES
    incl, strict, eye, upper = _tri_masks(CHUNK)
    beta = jnp.sum(jnp.where(eye, beta_row, 0.0), axis=1, keepdims=True)
    g = jnp.sum(jnp.where(eye, g_row, 0.0), axis=1, keepdims=True)
    gc = jnp.sum(jnp.where(incl, g_row, 0.0), axis=1, keepdims=True)
    gc_row = jnp.sum(jnp.where(upper, g, 0.0), axis=0, keepdims=True)
    decay = jnp.where(incl, jnp.exp(jnp.where(incl, gc - gc_row, 0.0)), 0.0)
    kb = k * beta
    vb = v * beta
    m = jnp.where(strict, _smm_nt(kb, k, p) * decay, 0.0)
    tinv = _unit_lower_inv(-m, DELTA_INV_PASSES)
    u = _smm(tinv, vb, p)
    wk = _smm(tinv, kb * jnp.exp(gc), p)
    attn = _smm_nt(q, k, p) * decay
    qg = q * jnp.exp(gc)
    g_last = jnp.sum(g, axis=0, keepdims=True)
    k_tail = k * jnp.exp(g_last - gc)
    v_new = u - _smm(wk, s, p)
    o = _smm(qg, s, p) + _smm(attn, v_new, p)
    s_new = s * jnp.exp(g_last) + _smm_tn(k_tail, v_new, p)
    return o, s_new


RWKV_PASSES = 1
RWKV_INV_PASSES = 1


def _rwkv_chunk(st, r, k, v, a, b, lw):
    c = CHUNK
    p, pi = RWKV_PASSES, RWKV_INV_PASSES
    _, strict, _, _ = _tri_masks(c)
    lane = lax.broadcasted_iota(jnp.int32, (c, 2 * B_N), 1)
    row = lax.broadcasted_iota(jnp.int32, (c, 2 * B_N), 0)
    first = lane < B_N
    incl2 = row >= jnp.where(first, lane, lane - B_N)
    bi = lax.broadcasted_iota(jnp.int32, (2 * B_N, 2 * B_N), 0) < B_N
    bj = lax.broadcasted_iota(jnp.int32, (2 * B_N, 2 * B_N), 1) < B_N
    blockdiag = bi == bj
    cum = _cumsum_rows(lw)
    e_pos = jnp.exp(cum)
    e_neg = jnp.exp(-cum)
    rt = r * e_pos
    at = a * jnp.exp(cum - lw)
    kt = k * e_neg
    bt = b * e_neg
    bk = jnp.concatenate([bt, kt], axis=0)
    a_s0 = _smm_nt(at, st, p)
    r_s0 = _smm_nt(rt, st, p)
    heads = (first, jnp.logical_not(first))
    u = jnp.zeros((c, 2 * B_N), f32)
    for sel in heads:
        at_h = jnp.where(sel, at, 0.0)
        ab = jnp.where(strict, _smm_nt(at_h, bt, pi), 0.0)
        ak = jnp.where(strict, _smm_nt(at_h, kt, p), 0.0)
        t_h = _unit_lower_inv(ab, pi)
        u = u + _smm(t_h, jnp.where(sel, a_s0, 0.0) + _smm(ak, jnp.where(sel, v, 0.0), p), p)
    y = r_s0
    for sel in heads:
        rbk = jnp.where(incl2, _smm_nt(jnp.where(sel, rt, 0.0), bk, p), 0.0)
        uv = jnp.concatenate([jnp.where(sel, u, 0.0), jnp.where(sel, v, 0.0)], axis=0)
        y = y + _smm(rbk, uv, p)
    cl = jnp.sum(lw, axis=0, keepdims=True)
    dec = jnp.exp(cl - cum)
    uv_all = jnp.concatenate([u, v], axis=0)
    bk_dec = jnp.concatenate([b * dec, k * dec], axis=0)
    st_new = st * jnp.exp(cl) + jnp.where(blockdiag, _smm_tn(uv_all, bk_dec, p), 0.0)
    return y, st_new


GROUPS_PER_STEP = 8


def _scan_specs(ins, col_offs, n_chunks, reverse):
    gw = GROUPS_PER_STEP * LANES
    cidx = (lambda c: n_chunks - 1 - c) if reverse else (lambda c: c)
    specs = []
    for a, off in zip(ins, col_offs):
        if a.ndim == 2:
            assert off % gw == 0
            specs.append(pl.BlockSpec((CHUNK, gw), lambda h, c, o=off // gw: (cidx(c), h + o)))
        else:
            specs.append(pl.BlockSpec((GROUPS_PER_STEP, None, 1, CHUNK), lambda h, c: (h, cidx(c), 0, 0)))
    return specs, cidx


def _group_vals(refs, g):
    return [r[:, g * LANES:(g + 1) * LANES] if len(r.shape) == 2 else r[g] for r in refs]


def _scan_fwd(chunk_fn, ins, col_offs, n_groups, n_chunks, state_shape, name):
    n_in = len(ins)
    gps = GROUPS_PER_STEP
    t = ins[0].shape[0]

    def body(*refs):
        in_refs = refs[:n_in]
        o_ref, s0_ref, st = refs[n_in:]

        @pl.when(pl.program_id(1) == 0)
        def _():
            st[...] = jnp.zeros_like(st)

        states = st[...]
        vals = [jnp.stack(col) for col in zip(*[_group_vals(in_refs, g) for g in range(gps)])]
        o, s_new = jax.vmap(chunk_fn)(states, *vals)
        s0_ref[...] = states
        st[...] = s_new
        for g in range(gps):
            o_ref[:, g * LANES:(g + 1) * LANES] = o[g]

    specs, _ = _scan_specs(ins, col_offs, n_chunks, False)
    return pl.pallas_call(
        body, name=name, grid=(n_groups // gps, n_chunks), in_specs=specs,
        out_specs=[pl.BlockSpec((CHUNK, gps * LANES), lambda h, c: (c, h)),
                   pl.BlockSpec((gps, None) + state_shape, lambda h, c: (h, c, 0, 0))],
        out_shape=[jax.ShapeDtypeStruct((t, n_groups * LANES), f32),
                   jax.ShapeDtypeStruct((n_groups, n_chunks) + state_shape, f32)],
        scratch_shapes=[pltpu.VMEM((gps,) + state_shape, f32)],
        compiler_params=pltpu.CompilerParams(dimension_semantics=("parallel", "arbitrary")),
    )(*ins)


def _scan_bwd(chunk_fn, s0s, ins, col_offs, d_out, n_groups, n_chunks, state_shape, name):
    n_in = len(ins)
    gps = GROUPS_PER_STEP
    t = d_out.shape[0]

    def body(*refs):
        s0_ref = refs[0]
        in_refs = refs[1:1 + n_in]
        do_ref = refs[1 + n_in]
        g_refs = refs[2 + n_in:2 + 2 * n_in]
        dst = refs[2 + 2 * n_in]

        @pl.when(pl.program_id(1) == 0)
        def _():
            dst[...] = jnp.zeros_like(dst)

        vals = [jnp.stack(col) for col in zip(*[_group_vals(in_refs, g) for g in range(gps)])]
        d_o = jnp.stack([do_ref[:, g * LANES:(g + 1) * LANES] for g in range(gps)])
        _, vjp = jax.vjp(jax.vmap(chunk_fn), s0_ref[...], *vals)
        gs = vjp((d_o, dst[...]))
        dst[...] = gs[0]
        for g_ref, gv in zip(g_refs, gs[1:]):
            if len(g_ref.shape) == 2:
                for g in range(gps):
                    g_ref[:, g * LANES:(g + 1) * LANES] = gv[g]
            else:
                g_ref[...] = gv

    specs, cidx = _scan_specs(ins, col_offs, n_chunks, True)
    out_lane = pl.BlockSpec((CHUNK, gps * LANES), lambda h, c: (cidx(c), h))
    g_specs = [out_lane if a.ndim == 2 else sp for a, sp in zip(ins, specs)]
    g_shapes = [(t, n_groups * LANES) if a.ndim == 2 else a.shape for a in ins]
    s0_spec = pl.BlockSpec((gps, None) + state_shape, lambda h, c: (h, cidx(c), 0, 0))
    return pl.pallas_call(
        body, name=name, grid=(n_groups // gps, n_chunks), in_specs=[s0_spec] + specs + [out_lane],
        out_specs=g_specs, out_shape=[jax.ShapeDtypeStruct(sh, f32) for sh in g_shapes],
        scratch_shapes=[pltpu.VMEM((gps,) + state_shape, f32)],
        compiler_params=pltpu.CompilerParams(dimension_semantics=("parallel", "arbitrary")),
    )(s0s, *ins, d_out)


def _rms_fwd(x, g, name):
    t = x.shape[0]
    tm = _tile(t, 416, 16)
    return _tw_fwd(_f_rms, [x, g], [_row_spec(tm, D), _full_spec(g.shape)],
                   [jax.ShapeDtypeStruct(x.shape, MXU_DTYPE)], [_row_spec(tm, D)], (t // tm,), name)[0]


def _rms_bwd(x, g, dy, residual, name):
    t = x.shape[0]
    tm = _tile(t, 416, 8)
    return _tw_bwd(_f_rms, [x, g], [_row_spec(tm, D), _full_spec(g.shape)], [dy], [_row_spec(tm, D)],
                   ['tile', 'acc'], (t // tm,), name, residual=residual)


def _ffn_fwd(h, gain, wgu, wd, tag):
    xn = _rms_fwd(h, gain, f"{tag}_rms")
    gate, up, act = _gate_up_act(xn, wgu, f"{tag}_gate_up")
    out = _matmul(act, wd, res=h, scale=0.5, name=f"{tag}_down")
    return out, (xn, gate, up, act)


def _mxu_dot(a, b, dn):
    return lax.dot_general(a.astype(MXU_DTYPE), b.astype(MXU_DTYPE), dn, preferred_element_type=f32,
                           precision=None if MXU_DTYPE == bf16 else HI)


def _gate_up_act(xn, wgu, name):
    t = xn.shape[0]
    wdt = wgu.shape[2]
    tm = _tile(t, 416, 16)
    dn = (((1,), (0,)), ((), ()))

    def body(x_ref, wg_ref, wu_ref, g_ref, u_ref, a_ref):
        x = x_ref[...]
        g = _mxu_dot(x, wg_ref[...], dn)
        u = _mxu_dot(x, wu_ref[...], dn)
        g_ref[...] = g
        u_ref[...] = u
        a_ref[...] = _f_swiglu(g, u)[0].astype(a_ref.dtype)

    out_spec = pl.BlockSpec((tm, wdt), lambda j, i: (i, j))
    return pl.pallas_call(
        body, name=name, grid=(2, t // tm),
        in_specs=[pl.BlockSpec((tm, D), lambda j, i: (i, 0)), pl.BlockSpec((None, D, wdt), lambda j, i: (j, 0, 0)),
                  pl.BlockSpec((None, D, wdt), lambda j, i: (j + 2, 0, 0))],
        out_specs=[out_spec] * 3,
        out_shape=[jax.ShapeDtypeStruct((t, 2 * wdt), f32)] * 2 + [jax.ShapeDtypeStruct((t, 2 * wdt), MXU_DTYPE)],
        compiler_params=pltpu.CompilerParams(dimension_semantics=("parallel", "parallel")),
    )(xn, wgu, wgu)


def _d_gate_up(dout, wd, gate, up, name):
    t = dout.shape[0]
    wdt = D_FF // 2
    tm = _tile(t, 416, 16)
    dn = (((1,), (1,)), ((), ()))

    def body(do_ref, wd_ref, g_ref, u_ref, dg_ref, du_ref):
        d_act = 0.5 * _mxu_dot(do_ref[...], wd_ref[...], dn)
        _, vjp = jax.vjp(_f_swiglu, g_ref[...], u_ref[...])
        dg, du = vjp((d_act,))
        dg_ref[...] = dg.astype(dg_ref.dtype)
        du_ref[...] = du.astype(du_ref.dtype)

    spec = pl.BlockSpec((tm, wdt), lambda j, i: (i, j))
    return pl.pallas_call(
        body, name=name, grid=(2, t // tm),
        in_specs=[pl.BlockSpec((tm, D), lambda j, i: (i, 0)), pl.BlockSpec((wdt, D), lambda j, i: (j, 0)), spec, spec],
        out_specs=[spec] * 2, out_shape=[jax.ShapeDtypeStruct((t, D_FF), MXU_DTYPE)] * 2,
        compiler_params=pltpu.CompilerParams(dimension_semantics=("parallel", "parallel")),
    )(dout, wd, gate, up)


def _ffn_bwd(h, gain, wgu, wd, saved, dout, tag):
    xn, gate, up, act = saved
    t = h.shape[0]
    d_wd = _matmul(act, dout, ta=True, scale=0.5, name=f"{tag}_dwd")
    d_gate, d_up = _d_gate_up(dout, wd, gate, up, f"{tag}_dact")
    d_wgu = _matmul(xn, d_gate, ta=True, out_cols_split=(0, N_CHIPS), name=f"{tag}_dwg")
    d_wgu = _matmul(xn, d_up, ta=True, out_cols_split=(2, N_CHIPS), out_into=d_wgu, name=f"{tag}_dwu")
    d_xn = _matmul(d_gate, wgu, tb=True, b_cols_split=(0, 2), name=f"{tag}_dxn_g")
    d_xn = _matmul(d_up, wgu, tb=True, b_cols_split=(2, 2), res=d_xn, name=f"{tag}_dxn_u")
    d_h, d_gain = _rms_bwd(h, gain, d_xn, dout, f"{tag}_drms")
    return d_h, d_gain, d_wgu, d_wd


def _col_spec(t, first_block):
    return pl.BlockSpec((t, LANES), lambda j, fb=first_block: (0, j + fb))


def _local_step(h0, tgt, w):
    t = h0.shape[0]
    assert t % CHUNK == 0
    nc = t // CHUNK
    grads = {}

    h1, ffn1_saved = _ffn_fwd(h0, w['ffn1_norm'], w['ffn1_wgu'], w['ffn1_wd'], "ffn1")
    u = _rms_fwd(h1, w['mix_norm'], "mix_rms")
    z = _matmul(u, w['w_in_p'], name="in_proj")
    zs = z[:, 9216:9216 + 304]
    abeta, aalpha = zs[:, 288:296], zs[:, 296:304]

    conv_w = w['a_conv_w']
    conv_fns = [functools.partial(_f_conv, norm=True, scale=A_DK ** -0.5),
                functools.partial(_f_conv, norm=True, scale=1.0),
                functools.partial(_f_conv, norm=False, scale=1.0)]
    qkv = []
    for idx, fn in enumerate(conv_fns):
        qkv.append(_tw_fwd(fn, [z, conv_w], [_col_spec(t, 8 * idx), pl.BlockSpec((4, LANES), lambda j, o=8 * idx: (0, j + o))],
                           [jax.ShapeDtypeStruct((t, D), f32)], [_col_spec(t, 0)], (A_HEADS,), f"a_conv{idx}")[0])
    aq, ak, av = qkv
    tmg = _tile(t, 1040, 8)
    dg_fn = functools.partial(_f_dgates, tm=tmg)
    dg_specs = [_row_spec(tmg, A_HEADS)] * 2 + [_full_spec((1, A_HEADS))] * 2
    beta, gdec = _tw_fwd(dg_fn, [abeta, aalpha, w['a_log_rate'], w['a_dt_bias']], dg_specs,
                         [jax.ShapeDtypeStruct((t, A_HEADS), f32)] * 2, [_row_spec(tmg, A_HEADS)] * 2, (t // tmg,),
                         "a_gates", with_pid=True)
    beta_h = beta.T.reshape(A_HEADS, nc, 1, CHUNK)
    gdec_h = gdec.T.reshape(A_HEADS, nc, 1, CHUNK)
    a_ins = [aq, ak, av, beta_h, gdec_h]
    a_offs = [0] * 5
    o_scan, a_s0 = _scan_fwd(_delta_chunk, a_ins, a_offs, A_HEADS, nc, (A_DK, A_DK), "a_scan")

    mu = w['b_shift_mu']
    mu_rkv, mu_s = mu[:, :3072], mu[:, 3072:]
    zf_rkv = _tw_fwd(_f_tshift, [z, mu_rkv], [_col_spec(t, 32), pl.BlockSpec((1, LANES), lambda j: (0, j))],
                     [jax.ShapeDtypeStruct((t, 3072), f32)], [_col_spec(t, 0)], (24,), "b_shift")[0]
    zs_b = zs[:, :288]
    zf_s = _tw_fwd(_f_tshift, [zs_b, mu_s], [_full_spec((t, 288)), _full_spec((1, 288))],
                   [jax.ShapeDtypeStruct((t, 288), f32)], [_full_spec((t, 288))], (1,), "b_shift_s")[0]
    wdf, adf, gdf = zf_s[:, 0:64], zf_s[:, 64:128], zf_s[:, 128:288]
    tmr = _tile(t, 160, 16)
    pre_params = [w['b_w0'], w['b_w_up'], w['b_a0'], w['b_a_up'], w['b_g_up'], w['b_k_k'], w['b_k_a']]
    pre_ins = [zf_rkv, wdf, adf, gdf] + pre_params
    pre_specs = ([_row_spec(tmr, D, 1), _row_spec(tmr, 64), _row_spec(tmr, 64), _row_spec(tmr, 160)]
                 + [_full_spec(p.shape) for p in pre_params])
    lw, kmod, a_s, b_s, bgate = _tw_fwd(_f_rwkv_pre, pre_ins, pre_specs, [jax.ShapeDtypeStruct((t, D), f32)] * 5,
                                        [_row_spec(tmr, D)] * 5, (t // tmr,), "b_pre")
    b_ins = [zf_rkv, kmod, zf_rkv, a_s, b_s, lw]
    b_offs = [0, 0, 2 * D, 0, 0, 0]
    y_scan, b_s0 = _scan_fwd(_rwkv_chunk, b_ins, b_offs, B_HEADS // 2, nc, (2 * B_N, 2 * B_N), "b_scan")

    out_gain_t = jnp.tile(w['a_out_norm'], (1, A_HEADS))
    r_k = w['b_r_k'].reshape(1, D)
    post_params = [out_gain_t, w['b_ln_gain'], w['b_ln_bias'], r_k]
    post_ins = [o_scan, z, y_scan, zf_rkv, kmod, zf_rkv, bgate, z, z] + post_params
    post_specs = ([_row_spec(tmr, D), _row_spec(tmr, D, 3), _row_spec(tmr, D), _row_spec(tmr, D, 0), _row_spec(tmr, D),
                   _row_spec(tmr, D, 2), _row_spec(tmr, D), _row_spec(tmr, D, 7), _row_spec(tmr, D, 8)]
                  + [_full_spec((1, D))] * 4)
    merged = _tw_fwd(_f_mix_post, post_ins, post_specs, [jax.ShapeDtypeStruct((t, D), MXU_DTYPE)],
                     [_row_spec(tmr, D)], (t // tmr,), "mix_post")[0]
    h2 = _matmul(merged, w['w_out'], res=h1, name="out_proj")
    h3, ffn2_saved = _ffn_fwd(h2, w['ffn2_norm'], w['ffn2_wgu'], w['ffn2_wd'], "ffn2")

    tml = _tile(t, 416, 8)
    fnorm = w['final_norm']
    loss_fn = functools.partial(_f_loss, tm=tml)
    loss_specs = [_row_spec(tml, D), _full_spec((1, D)), _row_spec(tml, D)]
    loss_parts, d_h3, grads['final_norm'] = _loss_and_grad(loss_fn, h3, fnorm, tgt, loss_specs, tml)
    loss = jnp.sum(loss_parts)

    d_h2, grads['ffn2_norm'], grads['ffn2_wgu'], grads['ffn2_wd'] = _ffn_bwd(
        h2, w['ffn2_norm'], w['ffn2_wgu'], w['ffn2_wd'], ffn2_saved, d_h3, "ffn2")
    grads['w_out'] = _matmul(merged, d_h2, ta=True, name="d_w_out")
    d_merged = _matmul(d_h2, w['w_out'], tb=True, name="d_merged")

    win = ('tile', (t, D), _row_spec(tmr, D))
    zwin = win + (MXU_DTYPE,)
    post_kinds = ['tile', zwin, 'tile', win, 'tile', win, 'tile', zwin, zwin] + ['acc'] * 4
    (d_o, d_az, d_y, d_r1, d_kmod1, d_v1, d_bgate, d_ga, d_gb,
     d_out_gain_t, grads['b_ln_gain'], grads['b_ln_bias'], d_r_k) = _tw_bwd(
        _f_mix_post, post_ins, post_specs, [d_merged], [_row_spec(tmr, D)], post_kinds, (t // tmr,), "mix_post_bwd")
    grads['a_out_norm'] = jnp.sum(d_out_gain_t.reshape(A_HEADS, A_DK), axis=0, keepdims=True)
    grads['b_r_k'] = d_r_k.reshape(1, B_HEADS, B_N)

    d_r2, d_kmod2, d_v2, d_as, d_bs, d_lw = _scan_bwd(_rwkv_chunk, b_s0, b_ins, b_offs, d_y, B_HEADS // 2, nc,
                                                      (2 * B_N, 2 * B_N), "b_scan_bwd")
    pre_kinds = [win] + ['tile'] * 3 + ['acc'] * 7
    pre_ct_specs = [_row_spec(tmr, D)] * 5
    (d_zf_k, d_wdf, d_adf, d_gdf, grads['b_w0'], grads['b_w_up'], grads['b_a0'], grads['b_a_up'], grads['b_g_up'],
     grads['b_k_k'], grads['b_k_a']) = _tw_bwd(
        _f_rwkv_pre, pre_ins, pre_specs, [d_lw, d_kmod1, d_as, d_bs, d_bgate], pre_ct_specs, pre_kinds, (t // tmr,),
        "b_pre_bwd", ct_extra=[(1, d_kmod2)])
    d_zb_rkv, d_mu_rkv = _shift_bwd3(z, mu_rkv, d_r1, d_r2, d_zf_k, d_v1, d_v2, t)
    d_zf_s = jnp.concatenate([d_wdf, d_adf, d_gdf], axis=1)
    d_zs_b, d_mu_s = _tw_bwd(_f_tshift, [zs_b, mu_s], [_full_spec((t, 288)), _full_spec((1, 288))], [d_zf_s],
                             [_full_spec((t, 288))], ['tile', 'tile'], (1,), "b_shift_s_bwd")
    grads['b_shift_mu'] = jnp.concatenate([d_mu_rkv, d_mu_s], axis=1)

    d_aq, d_ak, d_av, d_beta_h, d_g_h = _scan_bwd(_delta_chunk, a_s0, a_ins, a_offs, d_o, A_HEADS, nc, (A_DK, A_DK),
                                                  "a_scan_bwd")
    d_beta = d_beta_h.reshape(A_HEADS, t).T
    d_gdec = d_g_h.reshape(A_HEADS, t).T
    d_abeta, d_aalpha, grads['a_log_rate'], grads['a_dt_bias'] = _tw_bwd(
        dg_fn, [abeta, aalpha, w['a_log_rate'], w['a_dt_bias']], dg_specs, [d_beta, d_gdec],
        [_row_spec(tmg, A_HEADS)] * 2, ['tile', 'tile', 'acc', 'acc'], (t // tmg,), "a_gates_bwd", with_pid=True)
    d_zqkv, d_conv = [], []
    for idx, (fn, ct) in enumerate(zip(conv_fns, (d_aq, d_ak, d_av))):
        dz_i, dw_i = _conv_bwd(fn, z, conv_w, ct, idx, t)
        d_zqkv.append(dz_i)
        d_conv.append(dw_i)
    grads['a_conv_w'] = jnp.concatenate(d_conv, axis=1)

    d_small = jnp.concatenate([d_zs_b, d_abeta, d_aalpha, jnp.zeros((t, ZP - 9216 - 304), f32)], axis=1)
    d_small = lax.optimization_barrier(d_small.astype(MXU_DTYPE))
    d_z_parts = d_zqkv + [d_az, d_zb_rkv, d_ga, d_gb, d_small]
    d_z = jnp.concatenate([p.astype(MXU_DTYPE) for p in d_z_parts], axis=1)
    grads['w_in_p'] = _matmul(u, d_z, ta=True, name="d_w_in")
    d_u = _matmul(d_z, w['w_in_p'], tb=True, name="d_u")
    d_h1, grads['mix_norm'] = _rms_bwd(h1, w['mix_norm'], d_u, d_h2, "mix_drms")
    d_h0, grads['ffn1_norm'], grads['ffn1_wgu'], grads['ffn1_wd'] = _ffn_bwd(
        h0, w['ffn1_norm'], w['ffn1_wgu'], w['ffn1_wd'], ffn1_saved, d_h1, "ffn1")
    return loss, d_h0, grads


_WIN_SEGMENTS = ((0, 4096), (4112, 7184), (7472, 9520), (7184, 7472), (4096, 4112))


_WIN_SHARD = IN_TOTAL // N_CHIPS


def _win_pieces():
    pieces, pad_at = [], 0
    for a, b in _WIN_SEGMENTS:
        c = a
        while c < b:
            stop = min(b, (c // _WIN_SHARD + 1) * _WIN_SHARD)
            pieces.append((c, pad_at + c - a, stop - c))
            c = stop
        pad_at += b - a
    return pieces


def _win_shards_to_padded(shards):
    parts = [shards[c // _WIN_SHARD][:, c % _WIN_SHARD:c % _WIN_SHARD + n] for c, _, n in _win_pieces()]
    parts.append(jnp.zeros((shards.shape[1], ZP - IN_TOTAL), shards.dtype))
    return jnp.concatenate(parts, axis=1)


def _win_padded_to_shards(w_p):
    by_shard = [[] for _ in range(N_CHIPS)]
    for c, p, n in sorted(_win_pieces()):
        by_shard[c // _WIN_SHARD].append(w_p[:, p:p + n])
    return jnp.stack([jnp.concatenate(parts, axis=1) for parts in by_shard])


def _loss_and_grad(loss_fn, h, gain, tgt, specs, tm):
    t = h.shape[0]
    n = t // tm

    def body(h_ref, g_ref, t_ref, l_ref, dh_ref, dg_ref):
        pid = pl.program_id(0)
        tg = t_ref[...]
        (part,), vjp = jax.vjp(lambda a, b: loss_fn(pid, a, b, tg), h_ref[...], g_ref[...])
        dh, dg = vjp((jnp.ones_like(part),))
        l_ref[...] = part
        dh_ref[...] = dh

        @pl.when(pid == 0)
        def _():
            dg_ref[...] = dg

        @pl.when(pid != 0)
        def _():
            dg_ref[...] += dg

    return pl.pallas_call(
        body, name="loss", grid=(n,), in_specs=specs,
        out_specs=[pl.BlockSpec((None, 1, 1), lambda i: (i, 0, 0)), specs[0], _full_spec(gain.shape)],
        out_shape=[jax.ShapeDtypeStruct((n, 1, 1), f32), jax.ShapeDtypeStruct(h.shape, f32),
                   jax.ShapeDtypeStruct(gain.shape, f32)],
    )(h, gain, tgt)


def _shift_bwd3(z, mu, d_r1, d_r2, d_k, d_v1, d_v2, t):
    nb = D // LANES

    def body(z_ref, mu_ref, r1, r2, kk, v1, v2, dz_ref, dmu_ref):
        j = pl.program_id(0)
        ct = jnp.where(j < nb, r1[...] + r2[...], jnp.where(j < 2 * nb, kk[...], v1[...] + v2[...]))
        _, vjp = jax.vjp(lambda a, b: _f_tshift(a, b), z_ref[...], mu_ref[...])
        dz, dmu = vjp((ct,))
        dz_ref[...] = dz.astype(dz_ref.dtype)
        dmu_ref[...] = dmu

    def window(first):
        return pl.BlockSpec((t, LANES), lambda j, f=first: (0, jnp.clip(j - f * nb, 0, nb - 1)))

    return pl.pallas_call(
        body, name="b_shift_bwd", grid=(3 * nb,),
        in_specs=[_col_spec(t, 32), pl.BlockSpec((1, LANES), lambda j: (0, j)), window(0), window(0), window(1),
                  window(2), window(2)],
        out_specs=[_col_spec(t, 0), pl.BlockSpec((1, LANES), lambda j: (0, j))],
        out_shape=[jax.ShapeDtypeStruct((t, 3 * D), MXU_DTYPE), jax.ShapeDtypeStruct((1, 3 * D), f32)],
    )(z, mu, d_r1, d_r2, d_k, d_v1, d_v2)


def _conv_bwd(fn, z, conv_w, ct, idx, t):
    def body(z_ref, w_ref, ct_ref, dz_ref, dw_ref):
        _, vjp = jax.vjp(lambda a, b: fn(a, b), z_ref[...], w_ref[...])
        dz, dw = vjp((ct_ref[...],))
        dz_ref[...] = dz.astype(dz_ref.dtype)
        dw_ref[...] = dw

    return pl.pallas_call(
        body, name=f"a_conv{idx}_bwd", grid=(A_HEADS,),
        in_specs=[_col_spec(t, 8 * idx), pl.BlockSpec((4, LANES), lambda j, o=8 * idx: (0, j + o)), _col_spec(t, 0)],
        out_specs=[_col_spec(t, 0), pl.BlockSpec((4, LANES), lambda j: (0, j))],
        out_shape=[jax.ShapeDtypeStruct((t, D), MXU_DTYPE), jax.ShapeDtypeStruct((4, D), f32)],
    )(z, conv_w, ct)


def _position():
    return lax.axis_index("x"), lax.axis_index("y"), lax.axis_index("c")


def _flip(v, f):
    return 1 - v if f else v


_CHIP_FLIPS = ((1, 0), (0, 1), (1, 1))


def _gather_chips(arrs, name):
    n = len(arrs)
    assert all(a.shape[0] % 32 == 0 for a in arrs)
    arrs = [a.reshape(2, a.shape[0] // 2, a.shape[1]) for a in arrs]

    def body(*refs):
        ins, outs = refs[:n], refs[n:2 * n]
        send, recv, fsend, frecv, own = refs[2 * n:]
        x, y, c = _position()
        me = 2 * x + y
        sends, plan, owns = [], [], []
        for a in range(n):
            cp = pltpu.make_async_remote_copy(src_ref=ins[a], dst_ref=outs[a].at[me], send_sem=own.at[a, 0],
                                              recv_sem=own.at[a, 1], device_id=(x, y, 1 - c), device_id_type=MESH)
            cp.start()
            owns.append(cp)
            for j, (fx, fy) in enumerate(_CHIP_FLIPS):
                px, py = _flip(x, fx), _flip(y, fy)
                p = 2 * px + py
                cp = pltpu.make_async_remote_copy(src_ref=ins[a].at[c], dst_ref=outs[a].at[me, c],
                                                  send_sem=send.at[a, j], recv_sem=recv.at[a, j],
                                                  device_id=(px, py, c), device_id_type=MESH)
                cp.start()
                sends.append(cp)
                landed = pltpu.make_async_remote_copy(src_ref=ins[a].at[c], dst_ref=outs[a].at[p, c],
                                                      send_sem=send.at[a, j], recv_sem=recv.at[a, j],
                                                      device_id=(px, py, c), device_id_type=MESH)
                onward = pltpu.make_async_remote_copy(src_ref=outs[a].at[p, c], dst_ref=outs[a].at[p, c],
                                                      send_sem=fsend.at[a, j], recv_sem=frecv.at[a, j],
                                                      device_id=(x, y, 1 - c), device_id_type=MESH)
                from_sibling = pltpu.make_async_remote_copy(src_ref=outs[a].at[p, 1 - c], dst_ref=outs[a].at[p, 1 - c],
                                                            send_sem=fsend.at[a, j], recv_sem=frecv.at[a, j],
                                                            device_id=(x, y, 1 - c), device_id_type=MESH)
                plan.append((landed, onward, from_sibling))
        for landed, onward, _ in plan:
            landed.wait_recv()
            onward.start()
        for _, _, from_sibling in plan:
            from_sibling.wait_recv()
        for cp in sends:
            cp.wait_send()
        for _, onward, _ in plan:
            onward.wait_send()
        for cp in owns:
            cp.wait()

    sems = [pltpu.SemaphoreType.DMA((n, 3))] * 4 + [pltpu.SemaphoreType.DMA((n, 2))]
    outs = pl.pallas_call(
        body, name=name, in_specs=[ANY] * n, out_specs=[ANY] * n,
        out_shape=[jax.ShapeDtypeStruct((N_CHIPS,) + a.shape, a.dtype) for a in arrs], scratch_shapes=sems,
    )(*arrs)
    return [o.reshape(N_CHIPS, o.shape[1] * o.shape[2], o.shape[3]) for o in outs]


def _swap_sibling(arrs, src_of, shapes, name):
    n = len(arrs)

    def body(*refs):
        a_refs, got_refs = refs[:n], refs[n:2 * n]
        send, recv = refs[2 * n:]
        x, y, c = _position()
        copies = []
        for i in range(n):
            cp = pltpu.make_async_remote_copy(src_ref=src_of(a_refs[i], c), dst_ref=got_refs[i], send_sem=send.at[i],
                                              recv_sem=recv.at[i], device_id=(x, y, 1 - c), device_id_type=MESH)
            cp.start()
            copies.append(cp)
        for cp in copies:
            cp.wait()

    return pl.pallas_call(body, name=name, in_specs=[ANY] * n, out_specs=[ANY] * n,
                          out_shape=[jax.ShapeDtypeStruct(sh, a.dtype) for sh, a in zip(shapes, arrs)],
                          scratch_shapes=[pltpu.SemaphoreType.DMA((n,))] * 2)(*arrs)


def _row_tile(rows, width):
    return _tile(rows, max(16, (784 * LANES // width) // 16 * 16), 16)


def _add_halves(g, got, dtype, name):
    n, _, hr, w = g.shape
    tr = _row_tile(hr, w)

    def body(g_ref, got_ref, o_ref):
        c = lax.axis_index("c")
        own = jnp.where(c == 0, g_ref[:, 0], g_ref[:, 1])
        o_ref[...] = (own + got_ref[...]).astype(dtype)

    return pl.pallas_call(
        body, name=name, grid=(hr // tr,),
        in_specs=[pl.BlockSpec((n, 2, tr, w), lambda i: (0, 0, i, 0)), pl.BlockSpec((n, tr, w), lambda i: (0, i, 0))],
        out_specs=pl.BlockSpec((n, tr, w), lambda i: (0, i, 0)),
        out_shape=jax.ShapeDtypeStruct((n, hr, w), dtype))(g, got)


def _scatter_chips(gs, name):
    n = len(gs)

    def body(*refs):
        g_refs, out_refs = refs[:n], refs[n:2 * n]
        send, recv = refs[2 * n:]
        x, y, c = _position()
        sends = []
        for i in range(n):
            for j, (fx, fy) in enumerate(_CHIP_FLIPS):
                px, py = _flip(x, fx), _flip(y, fy)
                cp = pltpu.make_async_remote_copy(src_ref=g_refs[i].at[2 * px + py], dst_ref=out_refs[i].at[j],
                                                  send_sem=send.at[i, j], recv_sem=recv.at[i, j],
                                                  device_id=(px, py, c), device_id_type=MESH)
                cp.start()
                sends.append(cp)
        for cp in sends:
            cp.wait_recv()
        for cp in sends:
            cp.wait_send()

    return pl.pallas_call(
        body, name=name, in_specs=[ANY] * n, out_specs=[ANY] * n,
        out_shape=[jax.ShapeDtypeStruct((3,) + g.shape[1:], g.dtype) for g in gs],
        scratch_shapes=[pltpu.SemaphoreType.DMA((n, 3)), pltpu.SemaphoreType.DMA((n, 3))],
    )(*gs)


def _sum_own_and_slots(own, got, name):
    n, r, w = own.shape
    tr = _row_tile(r, w)

    def body(own_ref, got_ref, o_ref):
        me = 2 * lax.axis_index("x") + lax.axis_index("y")
        acc = own_ref[0]
        for i in range(1, n):
            acc = jnp.where(me == i, own_ref[i], acc)
        acc = acc.astype(f32)
        for j in range(3):
            acc = acc + got_ref[j].astype(f32)
        o_ref[...] = acc

    return pl.pallas_call(
        body, name=name, grid=(r // tr,),
        in_specs=[pl.BlockSpec((n, tr, w), lambda i: (0, i, 0)), pl.BlockSpec((3, tr, w), lambda i: (0, i, 0))],
        out_specs=pl.BlockSpec((tr, w), lambda i: (i, 0)), out_shape=jax.ShapeDtypeStruct((r, w), f32))(own, got)


def _share_chips(a, name):
    def body(a_ref, out_ref, send, recv):
        x, y, c = _position()
        sends = []
        for j, (fx, fy) in enumerate(_CHIP_FLIPS):
            cp = pltpu.make_async_remote_copy(src_ref=a_ref, dst_ref=out_ref.at[j], send_sem=send.at[j],
                                              recv_sem=recv.at[j], device_id=(_flip(x, fx), _flip(y, fy), c),
                                              device_id_type=MESH)
            cp.start()
            sends.append(cp)
        for cp in sends:
            cp.wait_recv()
        for cp in sends:
            cp.wait_send()

    return pl.pallas_call(
        body, name=name, in_specs=[ANY], out_specs=ANY, out_shape=jax.ShapeDtypeStruct((3,) + a.shape, a.dtype),
        scratch_shapes=[pltpu.SemaphoreType.DMA((3,)), pltpu.SemaphoreType.DMA((3,))],
    )(a)


def _sum_in_chip_order(pair, got, name):
    r, w = pair.shape
    tr = _tile(r, 1408, 8)

    def body(p_ref, g_ref, o_ref):
        x, y = lax.axis_index("x"), lax.axis_index("y")
        me = 2 * x + y
        across = [2 * _flip(x, fx) + _flip(y, fy) for fx, fy in _CHIP_FLIPS]
        acc = None
        for i in range(N_CHIPS):
            term = p_ref[...]
            for j in range(3):
                term = jnp.where(across[j] == i, g_ref[j], term)
            acc = term if acc is None else acc + term
        o_ref[...] = acc

    return pl.pallas_call(
        body, name=name, grid=(r // tr,),
        in_specs=[pl.BlockSpec((tr, w), lambda i: (i, 0)), pl.BlockSpec((3, tr, w), lambda i: (0, i, 0))],
        out_specs=pl.BlockSpec((tr, w), lambda i: (i, 0)), out_shape=jax.ShapeDtypeStruct((r, w), f32))(pair, got)


def _add2(a, b, name):
    r, w = a.shape
    tr = _tile(r, 1408, 8)
    spec = pl.BlockSpec((tr, w), lambda i: (i, 0))

    def body(a_ref, b_ref, o_ref):
        o_ref[...] = a_ref[...] + b_ref[...]

    return pl.pallas_call(body, name=name, grid=(r // tr,), in_specs=[spec, spec], out_specs=spec,
                          out_shape=jax.ShapeDtypeStruct(a.shape, f32))(a, b)


def _adamw(w, g_parts, m, v, name):
    shape = w.shape
    view = shape if len(shape) >= 2 else (1,) + shape
    assert all(d == 1 for d in view[:-2]), shape
    rows, cols = view[-2:]
    cap = max(8, (256 * 1024 // cols) // 8 * 8)
    tr = rows if rows <= cap else _tile(rows, cap, 8)
    lead = len(view) - 2
    n_g = len(g_parts)

    def body(*refs):
        w_ref = refs[0]
        g_refs = refs[1:1 + n_g]
        m_ref, v_ref, g_out, d_out, m_out, v_out = refs[1 + n_g:]
        g = g_refs[0][...]
        for gr in g_refs[1:]:
            g = g + gr[...]
        m_new = ADAM_B1 * m_ref[...] + (1.0 - ADAM_B1) * g
        v_new = ADAM_B2 * v_ref[...] + (1.0 - ADAM_B2) * (g * g)
        m_hat = m_new / (1.0 - ADAM_B1 ** ADAM_STEP)
        v_hat = v_new / (1.0 - ADAM_B2 ** ADAM_STEP)
        g_out[...] = g
        d_out[...] = -ADAM_LR * (m_hat / (jnp.sqrt(v_hat) + ADAM_EPS) + ADAM_WD * w_ref[...])
        m_out[...] = m_new
        v_out[...] = v_new

    spec = pl.BlockSpec((None,) * lead + (tr, cols), lambda i: (0,) * lead + (i, 0))
    args = [w.reshape(view)] + [g.reshape(view) for g in g_parts] + [m.reshape(view), v.reshape(view)]
    outs = pl.pallas_call(body, name=name, grid=(rows // tr,), in_specs=[spec] * len(args), out_specs=[spec] * 4,
                          out_shape=[jax.ShapeDtypeStruct(view, f32)] * 4)(*args)
    return [o.reshape(shape) for o in outs]


_BIG = ('ffn1_w_gu', 'ffn1_w_down', 'w_in', 'w_out', 'ffn2_w_gu', 'ffn2_w_down')
_SMALL_SHARDED = ('meta_tokens', 'a_conv_w', 'b_w_up', 'b_a_up', 'b_g_up')
_WEIGHTS = ('meta_tokens', 'ffn1_norm', 'ffn1_w_gu', 'ffn1_w_down', 'mix_norm', 'w_in', 'a_conv_w', 'a_log_rate',
            'a_dt_bias', 'a_out_norm', 'b_shift_mu', 'b_w0', 'b_w_up', 'b_a0', 'b_a_up', 'b_g_up', 'b_k_k', 'b_k_a',
            'b_r_k', 'b_ln_gain', 'b_ln_bias', 'w_out', 'ffn2_norm', 'ffn2_w_gu', 'ffn2_w_down', 'final_norm')
_SMALL = tuple(n for n in _WEIGHTS if n not in _BIG)


def _rows_of(shape):
    n = 1
    for d in shape:
        n *= d
    return n, -(-n // LANES)


def _pack(arrs, dtype, row_mult=32):
    parts, total = [], 0
    for a in arrs:
        n, rows = _rows_of(a.shape)
        flat = a.reshape(-1).astype(dtype)
        if n % LANES:
            flat = jnp.pad(flat, (0, rows * LANES - n))
        parts.append(flat)
        total += rows
    extra = -total % row_mult
    if extra:
        parts.append(jnp.zeros((extra * LANES,), dtype))
    return jnp.concatenate(parts).reshape(total + extra, LANES)


def _unpack(packed, shapes, lead=()):
    out, off = [], 0
    for sh in shapes:
        n, rows = _rows_of(sh)
        seg = packed[..., off:off + rows, :]
        if n % LANES:
            seg = seg.reshape(lead + (-1,))[..., :n]
        out.append(seg.reshape(lead + tuple(sh)))
        off += rows
    return out


def _cols_from_shards(s):
    return jnp.concatenate([s[i] for i in range(N_CHIPS)], axis=-1)


def kernel(x, meta_tokens, ffn1_norm, ffn1_w_gu, ffn1_w_down, mix_norm, w_in, a_conv_w, a_log_rate, a_dt_bias, a_out_norm, b_shift_mu, b_w0, b_w_up, b_a0, b_a_up, b_g_up, b_k_k, b_k_a, b_r_k, b_ln_gain, b_ln_bias, w_out, ffn2_norm, ffn2_w_gu, ffn2_w_down, final_norm, loss_target, m_meta_tokens, m_ffn1_norm, m_ffn1_w_gu, m_ffn1_w_down, m_mix_norm, m_w_in, m_a_conv_w, m_a_log_rate, m_a_dt_bias, m_a_out_norm, m_b_shift_mu, m_b_w0, m_b_w_up, m_b_a0, m_b_a_up, m_b_g_up, m_b_k_k, m_b_k_a, m_b_r_k, m_b_ln_gain, m_b_ln_bias, m_w_out, m_ffn2_norm, m_ffn2_w_gu, m_ffn2_w_down, m_final_norm, v_meta_tokens, v_ffn1_norm, v_ffn1_w_gu, v_ffn1_w_down, v_mix_norm, v_w_in, v_a_conv_w, v_a_log_rate, v_a_dt_bias, v_a_out_norm, v_b_shift_mu, v_b_w0, v_b_w_up, v_b_a0, v_b_a_up, v_b_g_up, v_b_k_k, v_b_k_a, v_b_r_k, v_b_ln_gain, v_b_ln_bias, v_w_out, v_ffn2_norm, v_ffn2_w_gu, v_ffn2_w_down, v_final_norm):
    args = locals()
    wts = {n: args[n] for n in _WEIGHTS}
    mom = {n: args["m_" + n] for n in _WEIGHTS}
    var = {n: args["v_" + n] for n in _WEIGHTS}
    chip = 2 * lax.axis_index("x") + lax.axis_index("y")

    big_shapes = [wts[n].shape[1:] for n in _BIG]
    small_shapes = [wts[n].shape[-2:] for n in _SMALL_SHARDED]
    big_flat = [wts[n].astype(bf16).reshape(wts[n].shape[1:]) for n in _BIG]
    small_packed = _pack([wts[n] for n in _SMALL_SHARDED], f32)
    gathered = _gather_chips(big_flat + [small_packed], "gather_weights")
    gu1, dn1, w_in_s, w_out_s, gu2, dn2 = [a.reshape((N_CHIPS,) + tuple(sh)) for a, sh in zip(gathered, big_shapes)]
    meta_s, conv_s, wup_s, aup_s, gup_s = _unpack(gathered[-1], small_shapes, (N_CHIPS,))
    w = {
        'ffn1_norm': ffn1_norm, 'mix_norm': mix_norm, 'ffn2_norm': ffn2_norm, 'final_norm': final_norm[None, :],
        'ffn1_wgu': gu1, 'ffn1_wd': dn1.reshape(D_FF, D), 'ffn2_wgu': gu2, 'ffn2_wd': dn2.reshape(D_FF, D),
        'w_in_p': _win_shards_to_padded(w_in_s), 'w_out': w_out_s.reshape(D, D),
        'a_conv_w': _cols_from_shards(conv_s), 'b_w_up': _cols_from_shards(wup_s), 'b_a_up': _cols_from_shards(aup_s),
        'b_g_up': _cols_from_shards(gup_s),
        'a_log_rate': a_log_rate, 'a_dt_bias': a_dt_bias, 'a_out_norm': a_out_norm, 'b_shift_mu': b_shift_mu,
        'b_w0': b_w0, 'b_a0': b_a0, 'b_k_k': b_k_k, 'b_k_a': b_k_a, 'b_r_k': b_r_k, 'b_ln_gain': b_ln_gain,
        'b_ln_bias': b_ln_bias,
    }
    meta_full = _cols_from_shards(meta_s)

    h0 = jnp.concatenate([jnp.zeros((PAD, D), f32), meta_full, x[0]], axis=0)
    tgt = jnp.concatenate([jnp.zeros((SKIP, D), f32), loss_target[0]], axis=0)
    loss_local, d_h0, g = _local_step(h0, tgt, w)
    loss = lax.psum(loss_local, ("x", "y", "c"))
    grad_x = d_h0[SKIP:][None]

    big_grads = [
        g['ffn1_wgu'],
        g['ffn1_wd'].reshape(N_CHIPS, D_FF // N_CHIPS, D),
        _win_padded_to_shards(g['w_in_p']),
        g['w_out'].reshape(N_CHIPS, D // N_CHIPS, D),
        g['ffn2_wgu'],
        g['ffn2_wd'].reshape(N_CHIPS, D_FF // N_CHIPS, D),
    ]
    g_halves = [a.reshape(N_CHIPS, 2, a.shape[1] // 2, a.shape[2]) for a in big_grads]
    sib_halves = _swap_sibling(g_halves, lambda ref, c: ref.at[:, 1 - c], [a.shape[:1] + a.shape[2:] for a in g_halves],
                               "swap_halves")
    chip_halves = [_add_halves(a, b, bf16, f"add_sibling{i}") for i, (a, b) in enumerate(zip(g_halves, sib_halves))]
    got = _scatter_chips(chip_halves, "scatter_grads")
    mine = [_sum_own_and_slots(a, b, f"sum_chips{i}") for i, (a, b) in enumerate(zip(chip_halves, got))]
    theirs = _swap_sibling(mine, lambda ref, c: ref, [a.shape for a in mine], "swap_sums")
    core = lax.axis_index("c")
    big_parts = [jnp.concatenate([jnp.where(core == 0, a, b), jnp.where(core == 0, b, a)], axis=0)
                 for a, b in zip(mine, theirs)]

    small_full = {
        'meta_tokens': d_h0[PAD:SKIP], 'ffn1_norm': g['ffn1_norm'], 'mix_norm': g['mix_norm'], 'a_conv_w': g['a_conv_w'],
        'a_log_rate': g['a_log_rate'], 'a_dt_bias': g['a_dt_bias'], 'a_out_norm': g['a_out_norm'],
        'b_shift_mu': g['b_shift_mu'], 'b_w0': g['b_w0'], 'b_w_up': g['b_w_up'], 'b_a0': g['b_a0'], 'b_a_up': g['b_a_up'],
        'b_g_up': g['b_g_up'], 'b_k_k': g['b_k_k'], 'b_k_a': g['b_k_a'], 'b_r_k': g['b_r_k'], 'b_ln_gain': g['b_ln_gain'],
        'b_ln_bias': g['b_ln_bias'], 'ffn2_norm': g['ffn2_norm'], 'final_norm': g['final_norm'],
    }
    s_shapes = [small_full[n].shape for n in _SMALL]
    s_packed = _pack([small_full[n] for n in _SMALL], f32, row_mult=256)
    (s_sib,) = _swap_sibling([s_packed], lambda ref, c: ref, [s_packed.shape], "swap_small")
    s_pair = _add2(s_packed, s_sib, "add_small")
    s_sum = _sum_in_chip_order(s_pair, _share_chips(s_pair, "share_small"), "sum_small")
    s_parts = dict(zip(_SMALL, _unpack(s_sum, s_shapes)))

    grad, delta, new_m, new_v = {}, {}, {}, {}
    for n, a in zip(_BIG, big_parts):
        grad[n], delta[n], new_m[n], new_v[n] = _adamw(wts[n], [a.reshape(wts[n].shape)], mom[n], var[n], f"adamw_{n}")
    for n in _SMALL:
        gs = s_parts[n]
        if n in _SMALL_SHARDED:
            width = wts[n].shape[-1]
            gs = lax.dynamic_slice_in_dim(gs, chip * width, width, axis=gs.ndim - 1)
        gs = gs.reshape(wts[n].shape)
        grad[n], delta[n], new_m[n], new_v[n] = _adamw(wts[n], [gs], mom[n], var[n], f"adamw_{n}")

    return (loss, grad_x, *[grad[n] for n in _WEIGHTS], *[delta[n] for n in _WEIGHTS],
            *[new_m[n] for n in _WEIGHTS], *[new_v[n] for n in _WEIGHTS])
```

```python
import functools

import jax
import jax.numpy as jnp
from jax import lax
from jax.experimental import pallas as pl
from jax.experimental.pallas import tpu as pltpu

f32 = jnp.float32
bf16 = jnp.bfloat16
MESH = pl.DeviceIdType.MESH
ANY = pl.BlockSpec(memory_space=pl.ANY)

D = 1024
N_META = 16
CHUNK = 64
PAD = CHUNK - N_META
SKIP = PAD + N_META
EPS = 1e-6
D_FF = 2816
A_HEADS = 8
A_DK = 128
B_HEADS = 16
B_N = 64
B_GN_EPS = B_N * 1e-5
IN_TOTAL = 9520
ZP = 9600
LANES = 128
N_CHIPS = 4

ADAM_LR, ADAM_B1, ADAM_B2, ADAM_EPS, ADAM_WD, ADAM_STEP = 0.001, 0.9, 0.999, 1e-08, 0.01, 10

MXU_DTYPE = bf16


def _tile(n, cap, mult):
    if n <= cap:
        return n
    best = None
    for t in range(mult, cap + 1, mult):
        if n % t == 0:
            best = t
    assert best is not None, (n, cap, mult)
    return best


def _sigmoid(x):
    return jax.nn.sigmoid(x)


def _silu(x):
    return x * jax.nn.sigmoid(x)


def _softplus(x):
    return jnp.maximum(x, 0.0) + jnp.log(1.0 + jnp.exp(-jnp.abs(x)))


def _head_matrix(c, nh):
    hd = c // nh
    r = lax.broadcasted_iota(jnp.int32, (c, nh), 0)
    h = lax.broadcasted_iota(jnp.int32, (c, nh), 1)
    return (r >= h * hd) & (r < (h + 1) * hd)


def _dot_exact_rhs(x, e, cb):
    dn = (((1,), (cb,)), ((), ()))
    eb = e.astype(bf16)
    hi = x.astype(bf16)
    lo = (x - hi.astype(f32)).astype(bf16)
    return (lax.dot_general(hi, eb, dn, preferred_element_type=f32)
            + lax.dot_general(lo, eb, dn, preferred_element_type=f32))


def _head_sum_impl(x, nh):
    e = _head_matrix(x.shape[-1], nh)
    return _dot_exact_rhs(_dot_exact_rhs(x, e, 0), e, 1)


@functools.partial(jax.custom_vjp, nondiff_argnums=(1,))
def _head_sum(x, nh):
    return _head_sum_impl(x, nh)


def _head_sum_fwd(x, nh):
    return _head_sum_impl(x, nh), None


def _head_sum_bwd(nh, _, g):
    return (_head_sum_impl(g, nh),)


_head_sum.defvjp(_head_sum_fwd, _head_sum_bwd)


@functools.partial(jax.custom_vjp, nondiff_argnums=(1,))
def _shift_rows(x, s):
    n = x.shape[0]
    row = lax.broadcasted_iota(jnp.int32, x.shape, 0)
    if s > 0:
        return jnp.where(row >= s, pltpu.roll(x, s, 0), 0.0)
    return jnp.where(row < n + s, pltpu.roll(x, n + s, 0), 0.0)


def _shift_rows_fwd(x, s):
    return _shift_rows(x, s), None


def _shift_rows_bwd(s, _, g):
    return (_shift_rows(g, -s),)


_shift_rows.defvjp(_shift_rows_fwd, _shift_rows_bwd)


def _matmul(a, b, *, ta=False, tb=False, res=None, scale=1.0, name, b_cols_split=None, out_cols_split=None,
            out_into=None):
    assert not (ta and tb)
    (ar, ac) = a.shape
    b0 = 0
    if b_cols_split:
        b0, bs = b_cols_split
        _, br, bc_part = b.shape
        bc = bs * bc_part
    else:
        br, bc = b.shape
    m, k = (ac, ar) if ta else (ar, ac)
    n, kb = (br, bc) if tb else (bc, br)
    assert k == kb, (a.shape, b.shape, ta, tb)
    tm = _tile(m, 1408, LANES) if ta else _tile(m, 832, 8)
    tn = _tile(n, 1920, LANES)
    tk = _tile(k, 1040, 8) if ta else _tile(k, 1920, LANES)
    nk = k // tk
    dn = (((0 if ta else 1,), (1 if tb else 0,)), ((), ()))
    if b_cols_split:
        assert (tk if tb else tn) == bc_part, (b.shape, tn, tk)

    def body(*refs):
        a_ref, b_ref = refs[:2]
        r_ref = refs[2] if res is not None else None
        o_ref, acc = refs[-2:]
        kk = pl.program_id(2)

        @pl.when(kk == 0)
        def _():
            acc[...] = jnp.zeros_like(acc)

        acc[...] += lax.dot_general(a_ref[...].astype(MXU_DTYPE), b_ref[...].astype(MXU_DTYPE), dn,
                                    preferred_element_type=f32)

        @pl.when(kk == nk - 1)
        def _():
            out = acc[...]
            if scale != 1.0:
                out = out * scale
            if res is not None:
                out = r_ref[...] + out
            o_ref[...] = out

    if ta:
        a_spec = pl.BlockSpec((tk, tm), lambda i, j, kk: (kk, i))
    else:
        a_spec = pl.BlockSpec((tm, tk), lambda i, j, kk: (i, kk))
    if tb and b_cols_split:
        b_spec = pl.BlockSpec((None, tn, tk), lambda i, j, kk: (kk + b0, j, 0))
    elif tb:
        b_spec = pl.BlockSpec((tn, tk), lambda i, j, kk: (j, kk))
    elif b_cols_split:
        b_spec = pl.BlockSpec((None, tk, tn), lambda i, j, kk: (j + b0, kk, 0))
    else:
        b_spec = pl.BlockSpec((tk, tn), lambda i, j, kk: (kk, j))
    in_specs = [a_spec, b_spec]
    args = [a, b]
    if res is not None:
        in_specs.append(pl.BlockSpec((tm, tn), lambda i, j, kk: (i, j)))
        args.append(res)
    aliases = {}
    if out_cols_split:
        o0, total = out_cols_split
        out_spec = pl.BlockSpec((None, tm, tn), lambda i, j, kk: (j + o0, i, 0))
        out_shape = jax.ShapeDtypeStruct((total, m, tn), f32)
        if out_into is not None:
            assert out_into.shape == out_shape.shape
            in_specs.append(ANY)
            args.append(out_into)
            aliases = {len(args) - 1: 0}
    else:
        out_spec = pl.BlockSpec((tm, tn), lambda i, j, kk: (i, j))
        out_shape = jax.ShapeDtypeStruct((m, n), f32)
    return pl.pallas_call(
        body, name=name, grid=(m // tm, n // tn, nk), in_specs=in_specs, out_specs=out_spec, out_shape=out_shape,
        scratch_shapes=[pltpu.VMEM((tm, tn), f32)], input_output_aliases=aliases,
        compiler_params=pltpu.CompilerParams(dimension_semantics=("parallel", "parallel", "arbitrary")),
    )(*args)


def _tw_fwd(fn, ins, in_specs, out_shapes, out_specs, grid, name, with_pid=False):
    n_in = len(ins)

    def body(*refs):
        vals = [r[...] for r in refs[:n_in]]
        outs = fn(pl.program_id(0), *vals) if with_pid else fn(*vals)
        for r, o in zip(refs[n_in:], outs):
            r[...] = o.astype(r.dtype)

    return pl.pallas_call(body, name=name, grid=grid, in_specs=in_specs, out_specs=out_specs,
                          out_shape=out_shapes)(*ins)


def _tw_bwd(fn, ins, in_specs, cts, ct_specs, kinds, grid, name, with_pid=False, tile_dtype=f32, ct_extra=(),
            residual=None):
    n_in, n_ct = len(ins), len(cts)
    diff = [i for i, kd in enumerate(kinds) if kd is not None]
    n_ex = len(ct_extra)

    def body(*refs):
        vals = [r[...] for r in refs[:n_in]]
        ctv = [r[...].astype(f32) for r in refs[n_in:n_in + n_ct]]
        for (ci, _), r in zip(ct_extra, refs[n_in + n_ct:n_in + n_ct + n_ex]):
            ctv[ci] = ctv[ci] + r[...]
        ctv = tuple(ctv)
        n_fixed = n_in + n_ct + n_ex
        res_ref = refs[n_fixed] if residual is not None else None
        g_refs = refs[n_fixed + (residual is not None):]
        pid = pl.program_id(0)

        def f(*dv):
            full = list(vals)
            for i, v in zip(diff, dv):
                full[i] = v
            out = fn(pid, *full) if with_pid else fn(*full)
            return tuple(out)

        _, vjp = jax.vjp(f, *[vals[i] for i in diff])
        gs = vjp(ctv)
        first = pid == 0
        for i2 in range(1, len(grid)):
            first = first & (pl.program_id(i2) == 0)
        for i, g, g_ref in zip(diff, gs, g_refs):
            if kinds[i] != 'acc':
                if i == 0 and res_ref is not None:
                    g = res_ref[...] + g
                g_ref[...] = g.astype(g_ref.dtype)
            else:
                @pl.when(first)
                def _(g=g, g_ref=g_ref):
                    g_ref[...] = g

                @pl.when(jnp.logical_not(first))
                def _(g=g, g_ref=g_ref):
                    g_ref[...] += g

    zero_map = {1: lambda *a: (0,), 2: lambda *a: (0, 0), 3: lambda *a: (0, 0, 0)}
    out_specs, out_shapes = [], []
    for i in diff:
        if kinds[i] == 'tile':
            out_shapes.append(jax.ShapeDtypeStruct(ins[i].shape, tile_dtype))
            out_specs.append(in_specs[i])
        elif kinds[i] == 'acc':
            out_shapes.append(jax.ShapeDtypeStruct(ins[i].shape, f32))
            out_specs.append(pl.BlockSpec(ins[i].shape, zero_map[ins[i].ndim]))
        else:
            out_shapes.append(jax.ShapeDtypeStruct(kinds[i][1], kinds[i][3] if len(kinds[i]) > 3 else tile_dtype))
            out_specs.append(kinds[i][2])
    extra_specs = [ct_specs[ci] for ci, _ in ct_extra]
    extra = [a for _, a in ct_extra]
    if residual is not None:
        assert kinds[0] == 'tile'
        extra_specs.append(in_specs[0])
        extra.append(residual)
    return pl.pallas_call(body, name=name, grid=grid, in_specs=list(in_specs) + list(ct_specs) + extra_specs,
                          out_specs=out_specs, out_shape=out_shapes)(*ins, *cts, *extra)


def _row_spec(tm, c, col_block=0):
    return pl.BlockSpec((tm, c), lambda i, cb=col_block: (i, cb))


def _full_spec(shape):
    nd = len(shape)
    return pl.BlockSpec(shape, lambda *a, nd=nd: (0,) * nd)


def _f_rms(x, g):
    return (x * lax.rsqrt(jnp.mean(x * x, axis=-1, keepdims=True) + EPS) * g,)


def _f_swiglu(gate, up):
    return (_silu(gate) * up,)


def _f_loss(pid, h, g, tgt, *, tm):
    y = h * lax.rsqrt(jnp.mean(h * h, axis=-1, keepdims=True) + EPS) * g
    row = pid * tm + lax.broadcasted_iota(jnp.int32, (tm, 1), 0)
    err = jnp.where(row >= SKIP, y - tgt, 0.0)
    per_row = jnp.mean(err * err, axis=-1, keepdims=True)
    return (0.5 * jnp.sum(per_row, axis=0, keepdims=True),)


def _f_conv(x, w, *, norm, scale):
    y = x * w[3:4, :]
    for s in (1, 2, 3):
        y = y + _shift_rows(x, s) * w[3 - s:4 - s, :]
    y = _silu(y)
    if norm:
        y = y * lax.rsqrt(jnp.sum(y * y, axis=-1, keepdims=True) + 1e-6) * scale
    return (y,)


def _f_dgates(pid, abeta, aalpha, log_rate, dt_bias, *, tm):
    row = pid * tm + lax.broadcasted_iota(jnp.int32, (tm, 1), 0)
    live = row >= PAD
    beta = jnp.where(live, _sigmoid(abeta), 0.0)
    g = jnp.where(live, -jnp.exp(log_rate) * _softplus(aalpha + dt_bias), 0.0)
    return beta, g


def _f_tshift(z, mu):
    return (z + (_shift_rows(z, 1) - z) * mu,)


def _f_rwkv_pre(k, wd, ad, gd, w0, w_up, a0, a_up, g_up, k_k, k_a):
    w_log = -_softplus(-(w0 + _smm(jnp.tanh(wd), w_up, 1))) - 0.5
    lw = -jnp.exp(w_log)
    a_lr = _sigmoid(a0 + _smm(ad, a_up, 1))
    gate = _smm(_sigmoid(gd), g_up, 1)
    kkp = k * k_k
    kk = kkp * lax.rsqrt(_head_sum(kkp * kkp, B_HEADS) + 1e-6)
    kmod = k * (1.0 + (a_lr - 1.0) * k_a)
    return lw, kmod, -kk, kk * a_lr, gate


def _f_mix_post(o, az, y, r, kmod, v, gate, ga, gb, out_gain, ln_g, ln_b, r_k):
    ms = _head_sum(o * o, A_HEADS) * (1.0 / A_DK)
    oa = o * lax.rsqrt(ms + EPS) * out_gain * _silu(az)
    mean = _head_sum(y, B_HEADS) * (1.0 / B_N)
    yc = y - mean
    var = _head_sum(yc * yc, B_HEADS) * (1.0 / B_N)
    yn = yc * lax.rsqrt(var + B_GN_EPS) * ln_g + ln_b
    bonus = _head_sum(r * kmod * r_k, B_HEADS) * v
    ob = (yn + bonus) * gate
    return (_sigmoid(ga) * oa + _sigmoid(gb) * ob,)


def _split2(a):
    hi = a.astype(bf16)
    return hi, (a - hi.astype(f32)).astype(bf16)


def _dot_passes(a, b, ca, cb, passes):
    dn = (((ca,), (cb,)), ((), ()))
    if passes == 1:
        return lax.dot_general(a.astype(bf16), b.astype(bf16), dn, preferred_element_type=f32)
    ah, al = _split2(a)
    bh, bl = _split2(b)
    return (lax.dot_general(ah, bh, dn, preferred_element_type=f32)
            + (lax.dot_general(ah, bl, dn, preferred_element_type=f32)
               + lax.dot_general(al, bh, dn, preferred_element_type=f32)))


@functools.partial(jax.custom_vjp, nondiff_argnums=(2, 3, 4))
def _sdot(a, b, ca, cb, passes):
    return _dot_passes(a, b, ca, cb, passes)


def _sdot_fwd(a, b, ca, cb, passes):
    return _dot_passes(a, b, ca, cb, passes), (a, b)


def _sdot_bwd(ca, cb, passes, res, g):
    a, b = res
    if (ca, cb) == (1, 0):
        return _dot_passes(g, b, 1, 1, passes), _dot_passes(a, g, 0, 0, passes)
    if (ca, cb) == (1, 1):
        return _dot_passes(g, b, 1, 0, passes), _dot_passes(g, a, 0, 0, passes)
    assert (ca, cb) == (0, 0)
    return _dot_passes(b, g, 1, 1, passes), _dot_passes(a, g, 1, 0, passes)


_sdot.defvjp(_sdot_fwd, _sdot_bwd)


def _smm(a, b, passes=3):
    return _sdot(a, b, 1, 0, passes)


def _smm_nt(a, b, passes=3):
    return _sdot(a, b, 1, 1, passes)


def _smm_tn(a, b, passes=3):
    return _sdot(a, b, 0, 0, passes)


def _tri_dot(x, ca):
    n = x.shape[0]
    incl = _tri_masks(n)[0]
    dn = (((ca,), (0,)), ((), ()))
    tri = incl.astype(bf16)
    hi, r1 = x.astype(bf16), None
    r1 = x - hi.astype(f32)
    mid = r1.astype(bf16)
    lo = (r1 - mid.astype(f32)).astype(bf16)
    return (lax.dot_general(tri, hi, dn, preferred_element_type=f32)
            + (lax.dot_general(tri, mid, dn, preferred_element_type=f32)
               + lax.dot_general(tri, lo, dn, preferred_element_type=f32)))


@jax.custom_vjp
def _cumsum_rows(x):
    return _tri_dot(x, 1)


def _cumsum_rows_fwd(x):
    return _tri_dot(x, 1), None


def _cumsum_rows_bwd(_, g):
    return (_tri_dot(g, 0),)


_cumsum_rows.defvjp(_cumsum_rows_fwd, _cumsum_rows_bwd)


def _tri_masks(n):
    i = lax.broadcasted_iota(jnp.int32, (n, n), 0)
    j = lax.broadcasted_iota(jnp.int32, (n, n), 1)
    return i >= j, i > j, i == j, i <= j


def _unit_lower_inv_impl(low, passes):
    n = low.shape[0]
    assert n == CHUNK
    _, _, eye, _ = _tri_masks(n)
    acc = eye.astype(f32) + low
    p = low
    for _ in range(5):
        p = _dot_passes(p, p, 1, 0, passes)
        acc = acc + _dot_passes(acc, p, 1, 0, passes)
    return acc


@functools.partial(jax.custom_vjp, nondiff_argnums=(1,))
def _unit_lower_inv(low, passes=3):
    return _unit_lower_inv_impl(low, passes)


def _unit_lower_inv_fwd(low, passes):
    t = _unit_lower_inv_impl(low, passes)
    return t, t


def _unit_lower_inv_bwd(passes, t, g):
    return (_dot_passes(_dot_passes(t, g, 0, 0, passes), t, 1, 1, passes),)


_unit_lower_inv.defvjp(_unit_lower_inv_fwd, _unit_lower_inv_bwd)


@functools.partial(jax.custom_vjp, nondiff_argnums=(2,))
def _unit_lower_inv_saved(low, t_saved, passes):
    return t_saved


def _unit_lower_inv_saved_fwd(low, t_saved, passes):
    return t_saved, t_saved


def _unit_lower_inv_saved_bwd(passes, t, g):
    return _unit_lower_inv_bwd(passes, t, g) + (jnp.zeros_like(t),)


_unit_lower_inv_saved.defvjp(_unit_lower_inv_saved_fwd, _unit_lower_inv_saved_bwd)


def _inverse(low, passes, saved):
    return _unit_lower_inv(low, passes) if saved is None else _unit_lower_inv_saved(low, saved, passes)

DELTA_PASSES = 1
DELTA_INV_PASSES = 1


def _delta_chunk(s, q, k, v, beta_row, g_row, inv_saved=None):
    p = DELTA_PASSES
    incl, strict, eye, upper = _tri_masks(CHUNK)
    beta = jnp.sum(jnp.where(eye, beta_row, 0.0), axis=1, keepdims=True)
    g = jnp.sum(jnp.where(eye, g_row, 0.0), axis=1, keepdims=True)
    gc = jnp.sum(jnp.where(incl, g_row, 0.0), axis=1, keepdims=True)
    gc_row = jnp.sum(jnp.where(upper, g, 0.0), axis=0, keepdims=True)
    decay = jnp.where(incl, jnp.exp(jnp.where(incl, gc - gc_row, 0.0)), 0.0)
    kb = k * beta
    vb = v * beta
    m = jnp.where(strict, _smm_nt(kb, k, p) * decay, 0.0)
    tinv = _inverse(-m, DELTA_INV_PASSES, inv_saved)
    u = _smm(tinv, vb, p)
    wk = _smm(tinv, kb * jnp.exp(gc), p)
    attn = _smm_nt(q, k, p) * decay
    qg = q * jnp.exp(gc)
    g_last = jnp.sum(g, axis=0, keepdims=True)
    k_tail = k * jnp.exp(g_last - gc)
    v_new = u - _smm(wk, s, p)
    o = _smm(qg, s, p) + _smm(attn, v_new, p)
    s_new = s * jnp.exp(g_last) + _smm_tn(k_tail, v_new, p)
    return o, s_new, tinv


RWKV_PASSES = 1
RWKV_INV_PASSES = 1


def _rwkv_chunk(st, r, k, v, a, b, lw, inv_saved=None):
    c = CHUNK
    p, pi = RWKV_PASSES, RWKV_INV_PASSES
    _, strict, _, _ = _tri_masks(c)
    lane = lax.broadcasted_iota(jnp.int32, (c, 2 * B_N), 1)
    row = lax.broadcasted_iota(jnp.int32, (c, 2 * B_N), 0)
    first = lane < B_N
    incl2 = row >= jnp.where(first, lane, lane - B_N)
    bi = lax.broadcasted_iota(jnp.int32, (2 * B_N, 2 * B_N), 0) < B_N
    bj = lax.broadcasted_iota(jnp.int32, (2 * B_N, 2 * B_N), 1) < B_N
    blockdiag = bi == bj
    cum = _cumsum_rows(lw)
    e_pos = jnp.exp(cum)
    e_neg = jnp.exp(-cum)
    rt = r * e_pos
    at = a * jnp.exp(cum - lw)
    kt = k * e_neg
    bt = b * e_neg
    bk = jnp.concatenate([bt, kt], axis=0)
    a_s0 = _smm_nt(at, st, p)
    r_s0 = _smm_nt(rt, st, p)
    heads = (first, jnp.logical_not(first))
    u = jnp.zeros((c, 2 * B_N), f32)
    invs = []
    for hi, sel in enumerate(heads):
        at_h = jnp.where(sel, at, 0.0)
        ab = jnp.where(strict, _smm_nt(at_h, bt, pi), 0.0)
        ak = jnp.where(strict, _smm_nt(at_h, kt, p), 0.0)
        t_h = _inverse(ab, pi, None if inv_saved is None else inv_saved[hi])
        invs.append(t_h)
        u = u + _smm(t_h, jnp.where(sel, a_s0, 0.0) + _smm(ak, jnp.where(sel, v, 0.0), p), p)
    y = r_s0
    for sel in heads:
        rbk = jnp.where(incl2, _smm_nt(jnp.where(sel, rt, 0.0), bk, p), 0.0)
        uv = jnp.concatenate([jnp.where(sel, u, 0.0), jnp.where(sel, v, 0.0)], axis=0)
        y = y + _smm(rbk, uv, p)
    cl = jnp.sum(lw, axis=0, keepdims=True)
    dec = jnp.exp(cl - cum)
    uv_all = jnp.concatenate([u, v], axis=0)
    bk_dec = jnp.concatenate([b * dec, k * dec], axis=0)
    st_new = st * jnp.exp(cl) + jnp.where(blockdiag, _smm_tn(uv_all, bk_dec, p), 0.0)
    return y, st_new, jnp.stack(invs)


GROUPS_PER_STEP = 8


def _scan_specs(ins, col_offs, n_chunks, reverse):
    gw = GROUPS_PER_STEP * LANES
    cidx = (lambda c: n_chunks - 1 - c) if reverse else (lambda c: c)
    specs = []
    for a, off in zip(ins, col_offs):
        if a.ndim == 2:
            assert off % gw == 0
            specs.append(pl.BlockSpec((CHUNK, gw), lambda h, c, o=off // gw: (cidx(c), h + o)))
        else:
            specs.append(pl.BlockSpec((GROUPS_PER_STEP, None, 1, CHUNK), lambda h, c: (h, cidx(c), 0, 0)))
    return specs, cidx


def _group_vals(refs, g):
    return [r[:, g * LANES:(g + 1) * LANES] if len(r.shape) == 2 else r[g] for r in refs]


def _scan_fwd(chunk_fn, ins, col_offs, n_groups, n_chunks, state_shape, inv_shape, name):
    n_in = len(ins)
    gps = GROUPS_PER_STEP
    t = ins[0].shape[0]

    def body(*refs):
        in_refs = refs[:n_in]
        o_ref, s0_ref, inv_ref, st = refs[n_in:]

        @pl.when(pl.program_id(1) == 0)
        def _():
            st[...] = jnp.zeros_like(st)

        states = st[...]
        vals = [jnp.stack(col) for col in zip(*[_group_vals(in_refs, g) for g in range(gps)])]
        o, s_new, inv = jax.vmap(chunk_fn)(states, *vals)
        s0_ref[...] = states
        inv_ref[...] = inv
        st[...] = s_new
        for g in range(gps):
            o_ref[:, g * LANES:(g + 1) * LANES] = o[g]

    specs, _ = _scan_specs(ins, col_offs, n_chunks, False)
    zeros_i = (0,) * len(inv_shape)
    return pl.pallas_call(
        body, name=name, grid=(n_groups // gps, n_chunks), in_specs=specs,
        out_specs=[pl.BlockSpec((CHUNK, gps * LANES), lambda h, c: (c, h)),
                   pl.BlockSpec((gps, None) + state_shape, lambda h, c: (h, c, 0, 0)),
                   pl.BlockSpec((gps, None) + inv_shape, lambda h, c: (h, c) + zeros_i)],
        out_shape=[jax.ShapeDtypeStruct((t, n_groups * LANES), f32),
                   jax.ShapeDtypeStruct((n_groups, n_chunks) + state_shape, f32),
                   jax.ShapeDtypeStruct((n_groups, n_chunks) + inv_shape, f32)],
        scratch_shapes=[pltpu.VMEM((gps,) + state_shape, f32)],
        compiler_params=pltpu.CompilerParams(dimension_semantics=("parallel", "arbitrary")),
    )(*ins)


def _scan_bwd(chunk_fn, s0s, invs, ins, col_offs, d_out, n_groups, n_chunks, state_shape, name):
    n_in = len(ins)
    gps = GROUPS_PER_STEP
    t = d_out.shape[0]
    inv_shape = invs.shape[2:]

    def body(*refs):
        s0_ref, inv_ref = refs[:2]
        in_refs = refs[2:2 + n_in]
        do_ref = refs[2 + n_in]
        g_refs = refs[3 + n_in:3 + 2 * n_in]
        dst = refs[3 + 2 * n_in]

        @pl.when(pl.program_id(1) == 0)
        def _():
            dst[...] = jnp.zeros_like(dst)

        vals = [jnp.stack(col) for col in zip(*[_group_vals(in_refs, g) for g in range(gps)])]
        d_o = jnp.stack([do_ref[:, g * LANES:(g + 1) * LANES] for g in range(gps)])
        inv = inv_ref[...]

        def with_saved(s, *a):
            return jax.vmap(lambda ss, ii, *aa: chunk_fn(ss, *aa, inv_saved=ii)[:2])(s, inv, *a)

        _, vjp = jax.vjp(with_saved, s0_ref[...], *vals)
        gs = vjp((d_o, dst[...]))
        dst[...] = gs[0]
        for g_ref, gv in zip(g_refs, gs[1:]):
            if len(g_ref.shape) == 2:
                for g in range(gps):
                    g_ref[:, g * LANES:(g + 1) * LANES] = gv[g]
            else:
                g_ref[...] = gv

    specs, cidx = _scan_specs(ins, col_offs, n_chunks, True)
    out_lane = pl.BlockSpec((CHUNK, gps * LANES), lambda h, c: (cidx(c), h))
    g_specs = [out_lane if a.ndim == 2 else sp for a, sp in zip(ins, specs)]
    g_shapes = [(t, n_groups * LANES) if a.ndim == 2 else a.shape for a in ins]
    s0_spec = pl.BlockSpec((gps, None) + state_shape, lambda h, c: (h, cidx(c), 0, 0))
    zeros_i = (0,) * len(inv_shape)
    inv_spec = pl.BlockSpec((gps, None) + inv_shape, lambda h, c: (h, cidx(c)) + zeros_i)
    return pl.pallas_call(
        body, name=name, grid=(n_groups // gps, n_chunks), in_specs=[s0_spec, inv_spec] + specs + [out_lane],
        out_specs=g_specs, out_shape=[jax.ShapeDtypeStruct(sh, f32) for sh in g_shapes],
        scratch_shapes=[pltpu.VMEM((gps,) + state_shape, f32)],
        compiler_params=pltpu.CompilerParams(dimension_semantics=("parallel", "arbitrary")),
    )(s0s, invs, *ins, d_out)


def _rms_fwd(x, g, name):
    t = x.shape[0]
    tm = _tile(t, 416, 16)
    return _tw_fwd(_f_rms, [x, g], [_row_spec(tm, D), _full_spec(g.shape)],
                   [jax.ShapeDtypeStruct(x.shape, MXU_DTYPE)], [_row_spec(tm, D)], (t // tm,), name)[0]


def _rms_bwd(x, g, dy, residual, name):
    t = x.shape[0]
    tm = _tile(t, 416, 8)
    return _tw_bwd(_f_rms, [x, g], [_row_spec(tm, D), _full_spec(g.shape)], [dy], [_row_spec(tm, D)],
                   ['tile', 'acc'], (t // tm,), name, residual=residual)


def _ffn_fwd(h, gain, wgu, wd, tag):
    xn = _rms_fwd(h, gain, f"{tag}_rms")
    gate, up, act = _gate_up_act(xn, wgu, f"{tag}_gate_up")
    out = _matmul(act, wd, res=h, scale=0.5, name=f"{tag}_down")
    return out, (xn, gate, up, act)


def _mxu_dot(a, b, dn):
    return lax.dot_general(a.astype(MXU_DTYPE), b.astype(MXU_DTYPE), dn, preferred_element_type=f32)


def _gate_up_act(xn, wgu, name):
    t = xn.shape[0]
    wdt = wgu.shape[2]
    tm = _tile(t, 416, 16)
    dn = (((1,), (0,)), ((), ()))

    def body(x_ref, wg_ref, wu_ref, g_ref, u_ref, a_ref):
        x = x_ref[...]
        g = _mxu_dot(x, wg_ref[...], dn)
        u = _mxu_dot(x, wu_ref[...], dn)
        g_ref[...] = g
        u_ref[...] = u
        a_ref[...] = _f_swiglu(g, u)[0].astype(a_ref.dtype)

    out_spec = pl.BlockSpec((tm, wdt), lambda j, i: (i, j))
    return pl.pallas_call(
        body, name=name, grid=(2, t // tm),
        in_specs=[pl.BlockSpec((tm, D), lambda j, i: (i, 0)), pl.BlockSpec((None, D, wdt), lambda j, i: (j, 0, 0)),
                  pl.BlockSpec((None, D, wdt), lambda j, i: (j + 2, 0, 0))],
        out_specs=[out_spec] * 3,
        out_shape=[jax.ShapeDtypeStruct((t, 2 * wdt), f32)] * 2 + [jax.ShapeDtypeStruct((t, 2 * wdt), MXU_DTYPE)],
        compiler_params=pltpu.CompilerParams(dimension_semantics=("parallel", "parallel")),
    )(xn, wgu, wgu)


def _d_gate_up(dout, wd, gate, up, name):
    t = dout.shape[0]
    wdt = D_FF // 2
    tm = _tile(t, 416, 16)
    dn = (((1,), (1,)), ((), ()))

    def body(do_ref, wd_ref, g_ref, u_ref, dg_ref, du_ref):
        d_act = 0.5 * _mxu_dot(do_ref[...], wd_ref[...], dn)
        _, vjp = jax.vjp(_f_swiglu, g_ref[...], u_ref[...])
        dg, du = vjp((d_act,))
        dg_ref[...] = dg.astype(dg_ref.dtype)
        du_ref[...] = du.astype(du_ref.dtype)

    spec = pl.BlockSpec((tm, wdt), lambda j, i: (i, j))
    return pl.pallas_call(
        body, name=name, grid=(2, t // tm),
        in_specs=[pl.BlockSpec((tm, D), lambda j, i: (i, 0)), pl.BlockSpec((wdt, D), lambda j, i: (j, 0)), spec, spec],
        out_specs=[spec] * 2, out_shape=[jax.ShapeDtypeStruct((t, D_FF), MXU_DTYPE)] * 2,
        compiler_params=pltpu.CompilerParams(dimension_semantics=("parallel", "parallel")),
    )(dout, wd, gate, up)


def _ffn_bwd(h, gain, wgu, wd, saved, dout, tag):
    xn, gate, up, act = saved
    t = h.shape[0]
    d_wd = _matmul(act, dout, ta=True, scale=0.5, name=f"{tag}_dwd")
    d_gate, d_up = _d_gate_up(dout, wd, gate, up, f"{tag}_dact")
    d_wgu = _matmul(xn, d_gate, ta=True, out_cols_split=(0, N_CHIPS), name=f"{tag}_dwg")
    d_wgu = _matmul(xn, d_up, ta=True, out_cols_split=(2, N_CHIPS), out_into=d_wgu, name=f"{tag}_dwu")
    d_xn = _matmul(d_gate, wgu, tb=True, b_cols_split=(0, 2), name=f"{tag}_dxn_g")
    d_xn = _matmul(d_up, wgu, tb=True, b_cols_split=(2, 2), res=d_xn, name=f"{tag}_dxn_u")
    d_h, d_gain = _rms_bwd(h, gain, d_xn, dout, f"{tag}_drms")
    return d_h, d_gain, d_wgu, d_wd


def _col_spec(t, first_block):
    return pl.BlockSpec((t, LANES), lambda j, fb=first_block: (0, j + fb))


def _local_step(h0, tgt, w):
    t = h0.shape[0]
    assert t % CHUNK == 0
    nc = t // CHUNK
    grads = {}

    h1, ffn1_saved = _ffn_fwd(h0, w['ffn1_norm'], w['ffn1_wgu'], w['ffn1_wd'], "ffn1")
    u = _rms_fwd(h1, w['mix_norm'], "mix_rms")
    z = _matmul(u, w['w_in_p'], name="in_proj")
    zs = z[:, 9216:9216 + 304]
    abeta, aalpha = zs[:, 288:296], zs[:, 296:304]

    conv_w = w['a_conv_w']
    conv_fns = [functools.partial(_f_conv, norm=True, scale=A_DK ** -0.5),
                functools.partial(_f_conv, norm=True, scale=1.0),
                functools.partial(_f_conv, norm=False, scale=1.0)]
    qkv = []
    for idx, fn in enumerate(conv_fns):
        qkv.append(_tw_fwd(fn, [z, conv_w], [_col_spec(t, 8 * idx), pl.BlockSpec((4, LANES), lambda j, o=8 * idx: (0, j + o))],
                           [jax.ShapeDtypeStruct((t, D), f32)], [_col_spec(t, 0)], (A_HEADS,), f"a_conv{idx}")[0])
    aq, ak, av = qkv
    tmg = _tile(t, 1040, 8)
    dg_fn = functools.partial(_f_dgates, tm=tmg)
    dg_specs = [_row_spec(tmg, A_HEADS)] * 2 + [_full_spec((1, A_HEADS))] * 2
    beta, gdec = _tw_fwd(dg_fn, [abeta, aalpha, w['a_log_rate'], w['a_dt_bias']], dg_specs,
                         [jax.ShapeDtypeStruct((t, A_HEADS), f32)] * 2, [_row_spec(tmg, A_HEADS)] * 2, (t // tmg,),
                         "a_gates", with_pid=True)
    beta_h = beta.T.reshape(A_HEADS, nc, 1, CHUNK)
    gdec_h = gdec.T.reshape(A_HEADS, nc, 1, CHUNK)
    a_ins = [aq, ak, av, beta_h, gdec_h]
    a_offs = [0] * 5
    o_scan, a_s0, a_inv = _scan_fwd(_delta_chunk, a_ins, a_offs, A_HEADS, nc, (A_DK, A_DK), (CHUNK, CHUNK), "a_scan")

    mu = w['b_shift_mu']
    mu_rkv, mu_s = mu[:, :3072], mu[:, 3072:]
    zf_rkv = _tw_fwd(_f_tshift, [z, mu_rkv], [_col_spec(t, 32), pl.BlockSpec((1, LANES), lambda j: (0, j))],
                     [jax.ShapeDtypeStruct((t, 3072), f32)], [_col_spec(t, 0)], (24,), "b_shift")[0]
    zs_b = zs[:, :288]
    zf_s = _tw_fwd(_f_tshift, [zs_b, mu_s], [_full_spec((t, 288)), _full_spec((1, 288))],
                   [jax.ShapeDtypeStruct((t, 288), f32)], [_full_spec((t, 288))], (1,), "b_shift_s")[0]
    wdf, adf, gdf = zf_s[:, 0:64], zf_s[:, 64:128], zf_s[:, 128:288]
    tmr = _tile(t, 160, 16)
    pre_params = [w['b_w0'], w['b_w_up'], w['b_a0'], w['b_a_up'], w['b_g_up'], w['b_k_k'], w['b_k_a']]
    pre_ins = [zf_rkv, wdf, adf, gdf] + pre_params
    pre_specs = ([_row_spec(tmr, D, 1), _row_spec(tmr, 64), _row_spec(tmr, 64), _row_spec(tmr, 160)]
                 + [_full_spec(p.shape) for p in pre_params])
    lw, kmod, a_s, b_s, bgate = _tw_fwd(_f_rwkv_pre, pre_ins, pre_specs, [jax.ShapeDtypeStruct((t, D), f32)] * 5,
                                        [_row_spec(tmr, D)] * 5, (t // tmr,), "b_pre")
    b_ins = [zf_rkv, kmod, zf_rkv, a_s, b_s, lw]
    b_offs = [0, 0, 2 * D, 0, 0, 0]
    y_scan, b_s0, b_inv = _scan_fwd(_rwkv_chunk, b_ins, b_offs, B_HEADS // 2, nc, (2 * B_N, 2 * B_N),
                                    (2, CHUNK, CHUNK), "b_scan")

    out_gain_t = jnp.tile(w['a_out_norm'], (1, A_HEADS))
    r_k = w['b_r_k'].reshape(1, D)
    post_params = [out_gain_t, w['b_ln_gain'], w['b_ln_bias'], r_k]
    post_ins = [o_scan, z, y_scan, zf_rkv, kmod, zf_rkv, bgate, z, z] + post_params
    post_specs = ([_row_spec(tmr, D), _row_spec(tmr, D, 3), _row_spec(tmr, D), _row_spec(tmr, D, 0), _row_spec(tmr, D),
                   _row_spec(tmr, D, 2), _row_spec(tmr, D), _row_spec(tmr, D, 7), _row_spec(tmr, D, 8)]
                  + [_full_spec((1, D))] * 4)
    merged = _tw_fwd(_f_mix_post, post_ins, post_specs, [jax.ShapeDtypeStruct((t, D), MXU_DTYPE)],
                     [_row_spec(tmr, D)], (t // tmr,), "mix_post")[0]
    h2 = _matmul(merged, w['w_out'], res=h1, name="out_proj")
    h3, ffn2_saved = _ffn_fwd(h2, w['ffn2_norm'], w['ffn2_wgu'], w['ffn2_wd'], "ffn2")

    tml = _tile(t, 416, 8)
    fnorm = w['final_norm']
    loss_fn = functools.partial(_f_loss, tm=tml)
    loss_specs = [_row_spec(tml, D), _full_spec((1, D)), _row_spec(tml, D)]
    loss_parts, d_h3, grads['final_norm'] = _loss_and_grad(loss_fn, h3, fnorm, tgt, loss_specs, tml)
    loss = jnp.sum(loss_parts)

    d_h2, grads['ffn2_norm'], grads['ffn2_wgu'], grads['ffn2_wd'] = _ffn_bwd(
        h2, w['ffn2_norm'], w['ffn2_wgu'], w['ffn2_wd'], ffn2_saved, d_h3, "ffn2")
    grads['w_out'] = _matmul(merged, d_h2, ta=True, name="d_w_out")
    d_merged = _matmul(d_h2, w['w_out'], tb=True, name="d_merged")

    win = ('tile', (t, D), _row_spec(tmr, D))
    zwin = win + (MXU_DTYPE,)
    post_kinds = ['tile', zwin, 'tile', win, 'tile', win, 'tile', zwin, zwin] + ['acc'] * 4
    (d_o, d_az, d_y, d_r1, d_kmod1, d_v1, d_bgate, d_ga, d_gb,
     d_out_gain_t, grads['b_ln_gain'], grads['b_ln_bias'], d_r_k) = _tw_bwd(
        _f_mix_post, post_ins, post_specs, [d_merged], [_row_spec(tmr, D)], post_kinds, (t // tmr,), "mix_post_bwd")
    grads['a_out_norm'] = jnp.sum(d_out_gain_t.reshape(A_HEADS, A_DK), axis=0, keepdims=True)
    grads['b_r_k'] = d_r_k.reshape(1, B_HEADS, B_N)

    d_r2, d_kmod2, d_v2, d_as, d_bs, d_lw = _scan_bwd(_rwkv_chunk, b_s0, b_inv, b_ins, b_offs, d_y, B_HEADS // 2, nc,
                                                      (2 * B_N, 2 * B_N), "b_scan_bwd")
    pre_kinds = [win] + ['tile'] * 3 + ['acc'] * 7
    pre_ct_specs = [_row_spec(tmr, D)] * 5
    (d_zf_k, d_wdf, d_adf, d_gdf, grads['b_w0'], grads['b_w_up'], grads['b_a0'], grads['b_a_up'], grads['b_g_up'],
     grads['b_k_k'], grads['b_k_a']) = _tw_bwd(
        _f_rwkv_pre, pre_ins, pre_specs, [d_lw, d_kmod1, d_as, d_bs, d_bgate], pre_ct_specs, pre_kinds, (t // tmr,),
        "b_pre_bwd", ct_extra=[(1, d_kmod2)])
    d_zb_rkv, d_mu_rkv = _shift_bwd3(z, mu_rkv, d_r1, d_r2, d_zf_k, d_v1, d_v2, t)
    d_zf_s = jnp.concatenate([d_wdf, d_adf, d_gdf], axis=1)
    d_zs_b, d_mu_s = _tw_bwd(_f_tshift, [zs_b, mu_s], [_full_spec((t, 288)), _full_spec((1, 288))], [d_zf_s],
                             [_full_spec((t, 288))], ['tile', 'tile'], (1,), "b_shift_s_bwd")
    grads['b_shift_mu'] = jnp.concatenate([d_mu_rkv, d_mu_s], axis=1)

    d_aq, d_ak, d_av, d_beta_h, d_g_h = _scan_bwd(_delta_chunk, a_s0, a_inv, a_ins, a_offs, d_o, A_HEADS, nc,
                                                  (A_DK, A_DK), "a_scan_bwd")
    d_beta = d_beta_h.reshape(A_HEADS, t).T
    d_gdec = d_g_h.reshape(A_HEADS, t).T
    d_abeta, d_aalpha, grads['a_log_rate'], grads['a_dt_bias'] = _tw_bwd(
        dg_fn, [abeta, aalpha, w['a_log_rate'], w['a_dt_bias']], dg_specs, [d_beta, d_gdec],
        [_row_spec(tmg, A_HEADS)] * 2, ['tile', 'tile', 'acc', 'acc'], (t // tmg,), "a_gates_bwd", with_pid=True)
    d_zqkv, d_conv = [], []
    for idx, (fn, ct) in enumerate(zip(conv_fns, (d_aq, d_ak, d_av))):
        dz_i, dw_i = _conv_bwd(fn, z, conv_w, ct, idx, t)
        d_zqkv.append(dz_i)
        d_conv.append(dw_i)
    grads['a_conv_w'] = jnp.concatenate(d_conv, axis=1)

    d_small = jnp.concatenate([d_zs_b, d_abeta, d_aalpha, jnp.zeros((t, ZP - 9216 - 304), f32)], axis=1)
    d_small = lax.optimization_barrier(d_small.astype(MXU_DTYPE))
    d_z_parts = d_zqkv + [d_az, d_zb_rkv, d_ga, d_gb, d_small]
    d_z = jnp.concatenate([p.astype(MXU_DTYPE) for p in d_z_parts], axis=1)
    grads['w_in_p'] = _matmul(u, d_z, ta=True, name="d_w_in")
    d_u = _matmul(d_z, w['w_in_p'], tb=True, name="d_u")
    d_h1, grads['mix_norm'] = _rms_bwd(h1, w['mix_norm'], d_u, d_h2, "mix_drms")
    d_h0, grads['ffn1_norm'], grads['ffn1_wgu'], grads['ffn1_wd'] = _ffn_bwd(
        h0, w['ffn1_norm'], w['ffn1_wgu'], w['ffn1_wd'], ffn1_saved, d_h1, "ffn1")
    return loss, d_h0, grads


_WIN_SEGMENTS = ((0, 4096), (4112, 7184), (7472, 9520), (7184, 7472), (4096, 4112))


_WIN_SHARD = IN_TOTAL // N_CHIPS


def _win_pieces():
    pieces, pad_at = [], 0
    for a, b in _WIN_SEGMENTS:
        c = a
        while c < b:
            stop = min(b, (c // _WIN_SHARD + 1) * _WIN_SHARD)
            pieces.append((c, pad_at + c - a, stop - c))
            c = stop
        pad_at += b - a
    return pieces


def _win_shards_to_padded(shards):
    parts = [shards[c // _WIN_SHARD][:, c % _WIN_SHARD:c % _WIN_SHARD + n] for c, _, n in _win_pieces()]
    parts.append(jnp.zeros((shards.shape[1], ZP - IN_TOTAL), shards.dtype))
    return jnp.concatenate(parts, axis=1)


def _win_padded_to_shards(w_p):
    by_shard = [[] for _ in range(N_CHIPS)]
    for c, p, n in sorted(_win_pieces()):
        by_shard[c // _WIN_SHARD].append(w_p[:, p:p + n])
    return jnp.stack([jnp.concatenate(parts, axis=1) for parts in by_shard])


def _loss_and_grad(loss_fn, h, gain, tgt, specs, tm):
    t = h.shape[0]
    n = t // tm

    def body(h_ref, g_ref, t_ref, l_ref, dh_ref, dg_ref):
        pid = pl.program_id(0)
        tg = t_ref[...]
        (part,), vjp = jax.vjp(lambda a, b: loss_fn(pid, a, b, tg), h_ref[...], g_ref[...])
        dh, dg = vjp((jnp.ones_like(part),))
        l_ref[...] = part
        dh_ref[...] = dh

        @pl.when(pid == 0)
        def _():
            dg_ref[...] = dg

        @pl.when(pid != 0)
        def _():
            dg_ref[...] += dg

    return pl.pallas_call(
        body, name="loss", grid=(n,), in_specs=specs,
        out_specs=[pl.BlockSpec((None, 1, 1), lambda i: (i, 0, 0)), specs[0], _full_spec(gain.shape)],
        out_shape=[jax.ShapeDtypeStruct((n, 1, 1), f32), jax.ShapeDtypeStruct(h.shape, f32),
                   jax.ShapeDtypeStruct(gain.shape, f32)],
    )(h, gain, tgt)


def _shift_bwd3(z, mu, d_r1, d_r2, d_k, d_v1, d_v2, t):
    nb = D // LANES

    def body(z_ref, mu_ref, r1, r2, kk, v1, v2, dz_ref, dmu_ref):
        j = pl.program_id(0)
        ct = jnp.where(j < nb, r1[...] + r2[...], jnp.where(j < 2 * nb, kk[...], v1[...] + v2[...]))
        _, vjp = jax.vjp(lambda a, b: _f_tshift(a, b), z_ref[...], mu_ref[...])
        dz, dmu = vjp((ct,))
        dz_ref[...] = dz.astype(dz_ref.dtype)
        dmu_ref[...] = dmu

    def window(first):
        return pl.BlockSpec((t, LANES), lambda j, f=first: (0, jnp.clip(j - f * nb, 0, nb - 1)))

    return pl.pallas_call(
        body, name="b_shift_bwd", grid=(3 * nb,),
        in_specs=[_col_spec(t, 32), pl.BlockSpec((1, LANES), lambda j: (0, j)), window(0), window(0), window(1),
                  window(2), window(2)],
        out_specs=[_col_spec(t, 0), pl.BlockSpec((1, LANES), lambda j: (0, j))],
        out_shape=[jax.ShapeDtypeStruct((t, 3 * D), MXU_DTYPE), jax.ShapeDtypeStruct((1, 3 * D), f32)],
    )(z, mu, d_r1, d_r2, d_k, d_v1, d_v2)


def _conv_bwd(fn, z, conv_w, ct, idx, t):
    def body(z_ref, w_ref, ct_ref, dz_ref, dw_ref):
        _, vjp = jax.vjp(lambda a, b: fn(a, b), z_ref[...], w_ref[...])
        dz, dw = vjp((ct_ref[...],))
        dz_ref[...] = dz.astype(dz_ref.dtype)
        dw_ref[...] = dw

    return pl.pallas_call(
        body, name=f"a_conv{idx}_bwd", grid=(A_HEADS,),
        in_specs=[_col_spec(t, 8 * idx), pl.BlockSpec((4, LANES), lambda j, o=8 * idx: (0, j + o)), _col_spec(t, 0)],
        out_specs=[_col_spec(t, 0), pl.BlockSpec((4, LANES), lambda j: (0, j))],
        out_shape=[jax.ShapeDtypeStruct((t, D), MXU_DTYPE), jax.ShapeDtypeStruct((4, D), f32)],
    )(z, conv_w, ct)


def _position():
    return lax.axis_index("x"), lax.axis_index("y"), lax.axis_index("c")


def _flip(v, f):
    return 1 - v if f else v


_CHIP_FLIPS = ((1, 0), (0, 1), (1, 1))


def _gather_chips(arrs, name):
    n = len(arrs)
    assert all(a.shape[0] % 32 == 0 for a in arrs)
    arrs = [a.reshape(2, a.shape[0] // 2, a.shape[1]) for a in arrs]

    def body(*refs):
        ins, outs = refs[:n], refs[n:2 * n]
        send, recv, fsend, frecv, own = refs[2 * n:]
        x, y, c = _position()
        me = 2 * x + y
        sends, plan, owns = [], [], []
        for a in range(n):
            cp = pltpu.make_async_remote_copy(src_ref=ins[a], dst_ref=outs[a].at[me], send_sem=own.at[a, 0],
                                              recv_sem=own.at[a, 1], device_id=(x, y, 1 - c), device_id_type=MESH)
            cp.start()
            owns.append(cp)
            for j, (fx, fy) in enumerate(_CHIP_FLIPS):
                px, py = _flip(x, fx), _flip(y, fy)
                p = 2 * px + py
                cp = pltpu.make_async_remote_copy(src_ref=ins[a].at[c], dst_ref=outs[a].at[me, c],
                                                  send_sem=send.at[a, j], recv_sem=recv.at[a, j],
                                                  device_id=(px, py, c), device_id_type=MESH)
                cp.start()
                sends.append(cp)
                landed = pltpu.make_async_remote_copy(src_ref=ins[a].at[c], dst_ref=outs[a].at[p, c],
                                                      send_sem=send.at[a, j], recv_sem=recv.at[a, j],
                                                      device_id=(px, py, c), device_id_type=MESH)
                onward = pltpu.make_async_remote_copy(src_ref=outs[a].at[p, c], dst_ref=outs[a].at[p, c],
                                                      send_sem=fsend.at[a, j], recv_sem=frecv.at[a, j],
                                                      device_id=(x, y, 1 - c), device_id_type=MESH)
                from_sibling = pltpu.make_async_remote_copy(src_ref=outs[a].at[p, 1 - c], dst_ref=outs[a].at[p, 1 - c],
                                                            send_sem=fsend.at[a, j], recv_sem=frecv.at[a, j],
                                                            device_id=(x, y, 1 - c), device_id_type=MESH)
                plan.append((landed, onward, from_sibling))
        for landed, onward, _ in plan:
            landed.wait_recv()
            onward.start()
        for _, _, from_sibling in plan:
            from_sibling.wait_recv()
        for cp in sends:
            cp.wait_send()
        for _, onward, _ in plan:
            onward.wait_send()
        for cp in owns:
            cp.wait()

    sems = [pltpu.SemaphoreType.DMA((n, 3))] * 4 + [pltpu.SemaphoreType.DMA((n, 2))]
    outs = pl.pallas_call(
        body, name=name, in_specs=[ANY] * n, out_specs=[ANY] * n,
        out_shape=[jax.ShapeDtypeStruct((N_CHIPS,) + a.shape, a.dtype) for a in arrs], scratch_shapes=sems,
    )(*arrs)
    return [o.reshape(N_CHIPS, o.shape[1] * o.shape[2], o.shape[3]) for o in outs]


def _swap_sibling(arrs, src_of, shapes, name):
    n = len(arrs)

    def body(*refs):
        a_refs, got_refs = refs[:n], refs[n:2 * n]
        send, recv = refs[2 * n:]
        x, y, c = _position()
        copies = []
        for i in range(n):
            cp = pltpu.make_async_remote_copy(src_ref=src_of(a_refs[i], c), dst_ref=got_refs[i], send_sem=send.at[i],
                                              recv_sem=recv.at[i], device_id=(x, y, 1 - c), device_id_type=MESH)
            cp.start()
            copies.append(cp)
        for cp in copies:
            cp.wait()

    return pl.pallas_call(body, name=name, in_specs=[ANY] * n, out_specs=[ANY] * n,
                          out_shape=[jax.ShapeDtypeStruct(sh, a.dtype) for sh, a in zip(shapes, arrs)],
                          scratch_shapes=[pltpu.SemaphoreType.DMA((n,))] * 2)(*arrs)


def _row_tile(rows, width):
    return _tile(rows, max(16, (784 * LANES // width) // 16 * 16), 16)


def _add_halves(g, got, dtype, name):
    n, _, hr, w = g.shape
    tr = _row_tile(hr, w)

    def body(g_ref, got_ref, o_ref):
        c = lax.axis_index("c")
        own = jnp.where(c == 0, g_ref[:, 0], g_ref[:, 1])
        o_ref[...] = (own + got_ref[...]).astype(dtype)

    return pl.pallas_call(
        body, name=name, grid=(hr // tr,),
        in_specs=[pl.BlockSpec((n, 2, tr, w), lambda i: (0, 0, i, 0)), pl.BlockSpec((n, tr, w), lambda i: (0, i, 0))],
        out_specs=pl.BlockSpec((n, tr, w), lambda i: (0, i, 0)),
        out_shape=jax.ShapeDtypeStruct((n, hr, w), dtype))(g, got)


def _scatter_chips(gs, name):
    n = len(gs)

    def body(*refs):
        g_refs, out_refs = refs[:n], refs[n:2 * n]
        send, recv = refs[2 * n:]
        x, y, c = _position()
        sends = []
        for i in range(n):
            for j, (fx, fy) in enumerate(_CHIP_FLIPS):
                px, py = _flip(x, fx), _flip(y, fy)
                cp = pltpu.make_async_remote_copy(src_ref=g_refs[i].at[2 * px + py], dst_ref=out_refs[i].at[j],
                                                  send_sem=send.at[i, j], recv_sem=recv.at[i, j],
                                                  device_id=(px, py, c), device_id_type=MESH)
                cp.start()
                sends.append(cp)
        for cp in sends:
            cp.wait_recv()
        for cp in sends:
            cp.wait_send()

    return pl.pallas_call(
        body, name=name, in_specs=[ANY] * n, out_specs=[ANY] * n,
        out_shape=[jax.ShapeDtypeStruct((3,) + g.shape[1:], g.dtype) for g in gs],
        scratch_shapes=[pltpu.SemaphoreType.DMA((n, 3)), pltpu.SemaphoreType.DMA((n, 3))],
    )(*gs)


def _sum_own_and_slots(own, got, name):
    n, r, w = own.shape
    tr = _row_tile(r, w)

    def body(own_ref, got_ref, o_ref):
        me = 2 * lax.axis_index("x") + lax.axis_index("y")
        acc = own_ref[0]
        for i in range(1, n):
            acc = jnp.where(me == i, own_ref[i], acc)
        acc = acc.astype(f32)
        for j in range(3):
            acc = acc + got_ref[j].astype(f32)
        o_ref[...] = acc

    return pl.pallas_call(
        body, name=name, grid=(r // tr,),
        in_specs=[pl.BlockSpec((n, tr, w), lambda i: (0, i, 0)), pl.BlockSpec((3, tr, w), lambda i: (0, i, 0))],
        out_specs=pl.BlockSpec((tr, w), lambda i: (i, 0)), out_shape=jax.ShapeDtypeStruct((r, w), f32))(own, got)


def _share_chips(a, name):
    def body(a_ref, out_ref, send, recv):
        x, y, c = _position()
        sends = []
        for j, (fx, fy) in enumerate(_CHIP_FLIPS):
            cp = pltpu.make_async_remote_copy(src_ref=a_ref, dst_ref=out_ref.at[j], send_sem=send.at[j],
                                              recv_sem=recv.at[j], device_id=(_flip(x, fx), _flip(y, fy), c),
                                              device_id_type=MESH)
            cp.start()
            sends.append(cp)
        for cp in sends:
            cp.wait_recv()
        for cp in sends:
            cp.wait_send()

    return pl.pallas_call(
        body, name=name, in_specs=[ANY], out_specs=ANY, out_shape=jax.ShapeDtypeStruct((3,) + a.shape, a.dtype),
        scratch_shapes=[pltpu.SemaphoreType.DMA((3,)), pltpu.SemaphoreType.DMA((3,))],
    )(a)


def _sum_in_chip_order(pair, got, name):
    r, w = pair.shape
    tr = _tile(r, 1408, 8)

    def body(p_ref, g_ref, o_ref):
        x, y = lax.axis_index("x"), lax.axis_index("y")
        me = 2 * x + y
        across = [2 * _flip(x, fx) + _flip(y, fy) for fx, fy in _CHIP_FLIPS]
        acc = None
        for i in range(N_CHIPS):
            term = p_ref[...]
            for j in range(3):
                term = jnp.where(across[j] == i, g_ref[j], term)
            acc = term if acc is None else acc + term
        o_ref[...] = acc

    return pl.pallas_call(
        body, name=name, grid=(r // tr,),
        in_specs=[pl.BlockSpec((tr, w), lambda i: (i, 0)), pl.BlockSpec((3, tr, w), lambda i: (0, i, 0))],
        out_specs=pl.BlockSpec((tr, w), lambda i: (i, 0)), out_shape=jax.ShapeDtypeStruct((r, w), f32))(pair, got)


def _add2(a, b, name):
    r, w = a.shape
    tr = _tile(r, 1408, 8)
    spec = pl.BlockSpec((tr, w), lambda i: (i, 0))

    def body(a_ref, b_ref, o_ref):
        o_ref[...] = a_ref[...] + b_ref[...]

    return pl.pallas_call(body, name=name, grid=(r // tr,), in_specs=[spec, spec], out_specs=spec,
                          out_shape=jax.ShapeDtypeStruct(a.shape, f32))(a, b)


def _adamw(w, g_parts, m, v, name):
    shape = w.shape
    view = shape if len(shape) >= 2 else (1,) + shape
    assert all(d == 1 for d in view[:-2]), shape
    rows, cols = view[-2:]
    cap = max(8, (256 * 1024 // cols) // 8 * 8)
    tr = rows if rows <= cap else _tile(rows, cap, 8)
    lead = len(view) - 2
    n_g = len(g_parts)

    def body(*refs):
        w_ref = refs[0]
        g_refs = refs[1:1 + n_g]
        m_ref, v_ref, g_out, d_out, m_out, v_out = refs[1 + n_g:]
        g = g_refs[0][...]
        for gr in g_refs[1:]:
            g = g + gr[...]
        m_new = ADAM_B1 * m_ref[...] + (1.0 - ADAM_B1) * g
        v_new = ADAM_B2 * v_ref[...] + (1.0 - ADAM_B2) * (g * g)
        m_hat = m_new / (1.0 - ADAM_B1 ** ADAM_STEP)
        v_hat = v_new / (1.0 - ADAM_B2 ** ADAM_STEP)
        g_out[...] = g
        d_out[...] = -ADAM_LR * (m_hat / (jnp.sqrt(v_hat) + ADAM_EPS) + ADAM_WD * w_ref[...])
        m_out[...] = m_new
        v_out[...] = v_new

    spec = pl.BlockSpec((None,) * lead + (tr, cols), lambda i: (0,) * lead + (i, 0))
    args = [w.reshape(view)] + [g.reshape(view) for g in g_parts] + [m.reshape(view), v.reshape(view)]
    outs = pl.pallas_call(body, name=name, grid=(rows // tr,), in_specs=[spec] * len(args), out_specs=[spec] * 4,
                          out_shape=[jax.ShapeDtypeStruct(view, f32)] * 4)(*args)
    return [o.reshape(shape) for o in outs]


_BIG = ('ffn1_w_gu', 'ffn1_w_down', 'w_in', 'w_out', 'ffn2_w_gu', 'ffn2_w_down')
_SMALL_SHARDED = ('meta_tokens', 'a_conv_w', 'b_w_up', 'b_a_up', 'b_g_up')
_WEIGHTS = ('meta_tokens', 'ffn1_norm', 'ffn1_w_gu', 'ffn1_w_down', 'mix_norm', 'w_in', 'a_conv_w', 'a_log_rate',
            'a_dt_bias', 'a_out_norm', 'b_shift_mu', 'b_w0', 'b_w_up', 'b_a0', 'b_a_up', 'b_g_up', 'b_k_k', 'b_k_a',
            'b_r_k', 'b_ln_gain', 'b_ln_bias', 'w_out', 'ffn2_norm', 'ffn2_w_gu', 'ffn2_w_down', 'final_norm')
_SMALL = tuple(n for n in _WEIGHTS if n not in _BIG)


def _rows_of(shape):
    n = 1
    for d in shape:
        n *= d
    return n, -(-n // LANES)


def _pack(arrs, dtype, row_mult=32):
    parts, total = [], 0
    for a in arrs:
        n, rows = _rows_of(a.shape)
        flat = a.reshape(-1).astype(dtype)
        if n % LANES:
            flat = jnp.pad(flat, (0, rows * LANES - n))
        parts.append(flat)
        total += rows
    extra = -total % row_mult
    if extra:
        parts.append(jnp.zeros((extra * LANES,), dtype))
    return jnp.concatenate(parts).reshape(total + extra, LANES)


def _unpack(packed, shapes, lead=()):
    out, off = [], 0
    for sh in shapes:
        n, rows = _rows_of(sh)
        seg = packed[..., off:off + rows, :]
        if n % LANES:
            seg = seg.reshape(lead + (-1,))[..., :n]
        out.append(seg.reshape(lead + tuple(sh)))
        off += rows
    return out


def _cols_from_shards(s):
    return jnp.concatenate([s[i] for i in range(N_CHIPS)], axis=-1)


def kernel(x, meta_tokens, ffn1_norm, ffn1_w_gu, ffn1_w_down, mix_norm, w_in, a_conv_w, a_log_rate, a_dt_bias, a_out_norm, b_shift_mu, b_w0, b_w_up, b_a0, b_a_up, b_g_up, b_k_k, b_k_a, b_r_k, b_ln_gain, b_ln_bias, w_out, ffn2_norm, ffn2_w_gu, ffn2_w_down, final_norm, loss_target, m_meta_tokens, m_ffn1_norm, m_ffn1_w_gu, m_ffn1_w_down, m_mix_norm, m_w_in, m_a_conv_w, m_a_log_rate, m_a_dt_bias, m_a_out_norm, m_b_shift_mu, m_b_w0, m_b_w_up, m_b_a0, m_b_a_up, m_b_g_up, m_b_k_k, m_b_k_a, m_b_r_k, m_b_ln_gain, m_b_ln_bias, m_w_out, m_ffn2_norm, m_ffn2_w_gu, m_ffn2_w_down, m_final_norm, v_meta_tokens, v_ffn1_norm, v_ffn1_w_gu, v_ffn1_w_down, v_mix_norm, v_w_in, v_a_conv_w, v_a_log_rate, v_a_dt_bias, v_a_out_norm, v_b_shift_mu, v_b_w0, v_b_w_up, v_b_a0, v_b_a_up, v_b_g_up, v_b_k_k, v_b_k_a, v_b_r_k, v_b_ln_gain, v_b_ln_bias, v_w_out, v_ffn2_norm, v_ffn2_w_gu, v_ffn2_w_down, v_final_norm):
    args = locals()
    wts = {n: args[n] for n in _WEIGHTS}
    mom = {n: args["m_" + n] for n in _WEIGHTS}
    var = {n: args["v_" + n] for n in _WEIGHTS}
    chip = 2 * lax.axis_index("x") + lax.axis_index("y")

    big_shapes = [wts[n].shape[1:] for n in _BIG]
    small_shapes = [wts[n].shape[-2:] for n in _SMALL_SHARDED]
    big_flat = [wts[n].astype(bf16).reshape(wts[n].shape[1:]) for n in _BIG]
    small_packed = _pack([wts[n] for n in _SMALL_SHARDED], f32)
    gathered = _gather_chips(big_flat + [small_packed], "gather_weights")
    gu1, dn1, w_in_s, w_out_s, gu2, dn2 = [a.reshape((N_CHIPS,) + tuple(sh)) for a, sh in zip(gathered, big_shapes)]
    meta_s, conv_s, wup_s, aup_s, gup_s = _unpack(gathered[-1], small_shapes, (N_CHIPS,))
    w = {
        'ffn1_norm': ffn1_norm, 'mix_norm': mix_norm, 'ffn2_norm': ffn2_norm, 'final_norm': final_norm[None, :],
        'ffn1_wgu': gu1, 'ffn1_wd': dn1.reshape(D_FF, D), 'ffn2_wgu': gu2, 'ffn2_wd': dn2.reshape(D_FF, D),
        'w_in_p': _win_shards_to_padded(w_in_s), 'w_out': w_out_s.reshape(D, D),
        'a_conv_w': _cols_from_shards(conv_s), 'b_w_up': _cols_from_shards(wup_s), 'b_a_up': _cols_from_shards(aup_s),
        'b_g_up': _cols_from_shards(gup_s),
        'a_log_rate': a_log_rate, 'a_dt_bias': a_dt_bias, 'a_out_norm': a_out_norm, 'b_shift_mu': b_shift_mu,
        'b_w0': b_w0, 'b_a0': b_a0, 'b_k_k': b_k_k, 'b_k_a': b_k_a, 'b_r_k': b_r_k, 'b_ln_gain': b_ln_gain,
        'b_ln_bias': b_ln_bias,
    }
    meta_full = _cols_from_shards(meta_s)

    h0 = jnp.concatenate([jnp.zeros((PAD, D), f32), meta_full, x[0]], axis=0)
    tgt = jnp.concatenate([jnp.zeros((SKIP, D), f32), loss_target[0]], axis=0)
    loss_local, d_h0, g = _local_step(h0, tgt, w)
    loss = lax.psum(loss_local, ("x", "y", "c"))
    grad_x = d_h0[SKIP:][None]

    big_grads = [
        g['ffn1_wgu'],
        g['ffn1_wd'].reshape(N_CHIPS, D_FF // N_CHIPS, D),
        _win_padded_to_shards(g['w_in_p']),
        g['w_out'].reshape(N_CHIPS, D // N_CHIPS, D),
        g['ffn2_wgu'],
        g['ffn2_wd'].reshape(N_CHIPS, D_FF // N_CHIPS, D),
    ]
    g_halves = [a.reshape(N_CHIPS, 2, a.shape[1] // 2, a.shape[2]) for a in big_grads]
    sib_halves = _swap_sibling(g_halves, lambda ref, c: ref.at[:, 1 - c], [a.shape[:1] + a.shape[2:] for a in g_halves],
                               "swap_halves")
    chip_halves = [_add_halves(a, b, bf16, f"add_sibling{i}") for i, (a, b) in enumerate(zip(g_halves, sib_halves))]
    got = _scatter_chips(chip_halves, "scatter_grads")
    mine = [_sum_own_and_slots(a, b, f"sum_chips{i}") for i, (a, b) in enumerate(zip(chip_halves, got))]
    theirs = _swap_sibling(mine, lambda ref, c: ref, [a.shape for a in mine], "swap_sums")
    core = lax.axis_index("c")
    big_parts = [jnp.concatenate([jnp.where(core == 0, a, b), jnp.where(core == 0, b, a)], axis=0)
                 for a, b in zip(mine, theirs)]

    small_full = {
        'meta_tokens': d_h0[PAD:SKIP], 'ffn1_norm': g['ffn1_norm'], 'mix_norm': g['mix_norm'], 'a_conv_w': g['a_conv_w'],
        'a_log_rate': g['a_log_rate'], 'a_dt_bias': g['a_dt_bias'], 'a_out_norm': g['a_out_norm'],
        'b_shift_mu': g['b_shift_mu'], 'b_w0': g['b_w0'], 'b_w_up': g['b_w_up'], 'b_a0': g['b_a0'], 'b_a_up': g['b_a_up'],
        'b_g_up': g['b_g_up'], 'b_k_k': g['b_k_k'], 'b_k_a': g['b_k_a'], 'b_r_k': g['b_r_k'], 'b_ln_gain': g['b_ln_gain'],
        'b_ln_bias': g['b_ln_bias'], 'ffn2_norm': g['ffn2_norm'], 'final_norm': g['final_norm'],
    }
    s_shapes = [small_full[n].shape for n in _SMALL]
    s_packed = _pack([small_full[n] for n in _SMALL], f32, row_mult=256)
    (s_sib,) = _swap_sibling([s_packed], lambda ref, c: ref, [s_packed.shape], "swap_small")
    s_pair = _add2(s_packed, s_sib, "add_small")
    s_sum = _sum_in_chip_order(s_pair, _share_chips(s_pair, "share_small"), "sum_small")
    s_parts = dict(zip(_SMALL, _unpack(s_sum, s_shapes)))

    grad, delta, new_m, new_v = {}, {}, {}, {}
    for n, a in zip(_BIG, big_parts):
        grad[n], delta[n], new_m[n], new_v[n] = _adamw(wts[n], [a.reshape(wts[n].shape)], mom[n], var[n], f"adamw_{n}")
    for n in _SMALL:
        gs = s_parts[n]
        if n in _SMALL_SHARDED:
            width = wts[n].shape[-1]
            gs = lax.dynamic_slice_in_dim(gs, chip * width, width, axis=gs.ndim - 1)
        gs = gs.reshape(wts[n].shape)
        grad[n], delta[n], new_m[n], new_v[n] = _adamw(wts[n], [gs], mom[n], var[n], f"adamw_{n}")

    return (loss, grad_x, *[grad[n] for n in _WEIGHTS], *[delta[n] for n in _WEIGHTS],
            *[new_m[n] for n in _WEIGHTS], *[new_v[n] for n in _WEIGHTS])
```

```python
import functools

import jax
import jax.numpy as jnp
from jax import lax
from jax.experimental import pallas as pl
from jax.experimental.pallas import tpu as pltpu

f32 = jnp.float32
bf16 = jnp.bfloat16
MESH = pl.DeviceIdType.MESH
ANY = pl.BlockSpec(memory_space=pl.ANY)

D = 1024
N_META = 16
CHUNK = 64
PAD = CHUNK - N_META
SKIP = PAD + N_META
EPS = 1e-6
D_FF = 2816
A_HEADS = 8
A_DK = 128
B_HEADS = 16
B_N = 64
B_GN_EPS = B_N * 1e-5
IN_TOTAL = 9520
ZP = 9600
LANES = 128
N_CHIPS = 4

ADAM_LR, ADAM_B1, ADAM_B2, ADAM_EPS, ADAM_WD, ADAM_STEP = 0.001, 0.9, 0.999, 1e-08, 0.01, 10

MXU_DTYPE = bf16


def _tile(n, cap, mult):
    if n <= cap:
        return n
    best = None
    for t in range(mult, cap + 1, mult):
        if n % t == 0:
            best = t
    assert best is not None, (n, cap, mult)
    return best


def _sigmoid(x):
    return jax.nn.sigmoid(x)


def _silu(x):
    return x * jax.nn.sigmoid(x)


def _softplus(x):
    return jnp.maximum(x, 0.0) + jnp.log(1.0 + jnp.exp(-jnp.abs(x)))


def _head_matrix(c, nh):
    hd = c // nh
    r = lax.broadcasted_iota(jnp.int32, (c, nh), 0)
    h = lax.broadcasted_iota(jnp.int32, (c, nh), 1)
    return (r >= h * hd) & (r < (h + 1) * hd)


def _dot_exact_rhs(x, e, cb):
    dn = (((1,), (cb,)), ((), ()))
    eb = e.astype(bf16)
    hi = x.astype(bf16)
    lo = (x - hi.astype(f32)).astype(bf16)
    return (lax.dot_general(hi, eb, dn, preferred_element_type=f32)
            + lax.dot_general(lo, eb, dn, preferred_element_type=f32))


def _head_sum_impl(x, nh):
    e = _head_matrix(x.shape[-1], nh)
    return _dot_exact_rhs(_dot_exact_rhs(x, e, 0), e, 1)


@functools.partial(jax.custom_vjp, nondiff_argnums=(1,))
def _head_sum(x, nh):
    return _head_sum_impl(x, nh)


def _head_sum_fwd(x, nh):
    return _head_sum_impl(x, nh), None


def _head_sum_bwd(nh, _, g):
    return (_head_sum_impl(g, nh),)


_head_sum.defvjp(_head_sum_fwd, _head_sum_bwd)


@functools.partial(jax.custom_vjp, nondiff_argnums=(1,))
def _shift_rows(x, s):
    n = x.shape[0]
    row = lax.broadcasted_iota(jnp.int32, x.shape, 0)
    if s > 0:
        return jnp.where(row >= s, pltpu.roll(x, s, 0), 0.0)
    return jnp.where(row < n + s, pltpu.roll(x, n + s, 0), 0.0)


def _shift_rows_fwd(x, s):
    return _shift_rows(x, s), None


def _shift_rows_bwd(s, _, g):
    return (_shift_rows(g, -s),)


_shift_rows.defvjp(_shift_rows_fwd, _shift_rows_bwd)


def _matmul(a, b, *, ta=False, tb=False, res=None, scale=1.0, name, b_cols_split=None, out_cols_split=None,
            out_into=None):
    assert not (ta and tb)
    (ar, ac) = a.shape
    b0 = 0
    if b_cols_split:
        b0, bs = b_cols_split
        _, br, bc_part = b.shape
        bc = bs * bc_part
    else:
        br, bc = b.shape
    m, k = (ac, ar) if ta else (ar, ac)
    n, kb = (br, bc) if tb else (bc, br)
    assert k == kb, (a.shape, b.shape, ta, tb)
    tm = _tile(m, 1408, LANES) if ta else _tile(m, 1040, 16)
    tn = _tile(n, 1920, LANES)
    tk = _tile(k, 1040, 8) if ta else _tile(k, 1920, LANES)
    nk = k // tk
    dn = (((0 if ta else 1,), (1 if tb else 0,)), ((), ()))
    if b_cols_split:
        assert (tk if tb else tn) == bc_part, (b.shape, tn, tk)

    def body(*refs):
        a_ref, b_ref = refs[:2]
        r_ref = refs[2] if res is not None else None
        o_ref, acc = refs[-2:]
        kk = pl.program_id(2)

        @pl.when(kk == 0)
        def _():
            acc[...] = jnp.zeros_like(acc)

        acc[...] += lax.dot_general(a_ref[...].astype(MXU_DTYPE), b_ref[...].astype(MXU_DTYPE), dn,
                                    preferred_element_type=f32)

        @pl.when(kk == nk - 1)
        def _():
            out = acc[...]
            if scale != 1.0:
                out = out * scale
            if res is not None:
                out = r_ref[...] + out
            o_ref[...] = out

    if ta:
        a_spec = pl.BlockSpec((tk, tm), lambda i, j, kk: (kk, i))
    else:
        a_spec = pl.BlockSpec((tm, tk), lambda i, j, kk: (i, kk))
    if tb and b_cols_split:
        b_spec = pl.BlockSpec((None, tn, tk), lambda i, j, kk: (kk + b0, j, 0))
    elif tb:
        b_spec = pl.BlockSpec((tn, tk), lambda i, j, kk: (j, kk))
    elif b_cols_split:
        b_spec = pl.BlockSpec((None, tk, tn), lambda i, j, kk: (j + b0, kk, 0))
    else:
        b_spec = pl.BlockSpec((tk, tn), lambda i, j, kk: (kk, j))
    in_specs = [a_spec, b_spec]
    args = [a, b]
    if res is not None:
        in_specs.append(pl.BlockSpec((tm, tn), lambda i, j, kk: (i, j)))
        args.append(res)
    aliases = {}
    if out_cols_split:
        o0, total = out_cols_split
        out_spec = pl.BlockSpec((None, tm, tn), lambda i, j, kk: (j + o0, i, 0))
        out_shape = jax.ShapeDtypeStruct((total, m, tn), f32)
        if out_into is not None:
            assert out_into.shape == out_shape.shape
            in_specs.append(ANY)
            args.append(out_into)
            aliases = {len(args) - 1: 0}
    else:
        out_spec = pl.BlockSpec((tm, tn), lambda i, j, kk: (i, j))
        out_shape = jax.ShapeDtypeStruct((m, n), f32)
    return pl.pallas_call(
        body, name=name, grid=(m // tm, n // tn, nk), in_specs=in_specs, out_specs=out_spec, out_shape=out_shape,
        scratch_shapes=[pltpu.VMEM((tm, tn), f32)], input_output_aliases=aliases,
        compiler_params=pltpu.CompilerParams(dimension_semantics=("parallel", "parallel", "arbitrary")),
    )(*args)


def _tw_fwd(fn, ins, in_specs, out_shapes, out_specs, grid, name, with_pid=False):
    n_in = len(ins)

    def body(*refs):
        vals = [r[...] for r in refs[:n_in]]
        outs = fn(pl.program_id(0), *vals) if with_pid else fn(*vals)
        for r, o in zip(refs[n_in:], outs):
            r[...] = o.astype(r.dtype)

    return pl.pallas_call(body, name=name, grid=grid, in_specs=in_specs, out_specs=out_specs,
                          out_shape=out_shapes)(*ins)


def _tw_bwd(fn, ins, in_specs, cts, ct_specs, kinds, grid, name, with_pid=False, tile_dtype=f32, ct_extra=(),
            residual=None):
    n_in, n_ct = len(ins), len(cts)
    diff = [i for i, kd in enumerate(kinds) if kd is not None]
    n_ex = len(ct_extra)

    def body(*refs):
        vals = [r[...] for r in refs[:n_in]]
        ctv = [r[...].astype(f32) for r in refs[n_in:n_in + n_ct]]
        for (ci, _), r in zip(ct_extra, refs[n_in + n_ct:n_in + n_ct + n_ex]):
            ctv[ci] = ctv[ci] + r[...]
        ctv = tuple(ctv)
        n_fixed = n_in + n_ct + n_ex
        res_ref = refs[n_fixed] if residual is not None else None
        g_refs = refs[n_fixed + (residual is not None):]
        pid = pl.program_id(0)

        def f(*dv):
            full = list(vals)
            for i, v in zip(diff, dv):
                full[i] = v
            out = fn(pid, *full) if with_pid else fn(*full)
            return tuple(out)

        _, vjp = jax.vjp(f, *[vals[i] for i in diff])
        gs = vjp(ctv)
        first = pid == 0
        for i2 in range(1, len(grid)):
            first = first & (pl.program_id(i2) == 0)
        for i, g, g_ref in zip(diff, gs, g_refs):
            if kinds[i] != 'acc':
                if i == 0 and res_ref is not None:
                    g = res_ref[...] + g
                g_ref[...] = g.astype(g_ref.dtype)
            else:
                @pl.when(first)
                def _(g=g, g_ref=g_ref):
                    g_ref[...] = g

                @pl.when(jnp.logical_not(first))
                def _(g=g, g_ref=g_ref):
                    g_ref[...] += g

    zero_map = {1: lambda *a: (0,), 2: lambda *a: (0, 0), 3: lambda *a: (0, 0, 0)}
    out_specs, out_shapes = [], []
    for i in diff:
        if kinds[i] == 'tile':
            out_shapes.append(jax.ShapeDtypeStruct(ins[i].shape, tile_dtype))
            out_specs.append(in_specs[i])
        elif kinds[i] == 'acc':
            out_shapes.append(jax.ShapeDtypeStruct(ins[i].shape, f32))
            out_specs.append(pl.BlockSpec(ins[i].shape, zero_map[ins[i].ndim]))
        else:
            out_shapes.append(jax.ShapeDtypeStruct(kinds[i][1], kinds[i][3] if len(kinds[i]) > 3 else tile_dtype))
            out_specs.append(kinds[i][2])
    extra_specs = [ct_specs[ci] for ci, _ in ct_extra]
    extra = [a for _, a in ct_extra]
    if residual is not None:
        assert kinds[0] == 'tile'
        extra_specs.append(in_specs[0])
        extra.append(residual)
    return pl.pallas_call(body, name=name, grid=grid, in_specs=list(in_specs) + list(ct_specs) + extra_specs,
                          out_specs=out_specs, out_shape=out_shapes)(*ins, *cts, *extra)


def _row_spec(tm, c, col_block=0):
    return pl.BlockSpec((tm, c), lambda i, cb=col_block: (i, cb))


def _full_spec(shape):
    nd = len(shape)
    return pl.BlockSpec(shape, lambda *a, nd=nd: (0,) * nd)


def _f_rms(x, g):
    return (x * lax.rsqrt(jnp.mean(x * x, axis=-1, keepdims=True) + EPS) * g,)


def _f_swiglu(gate, up):
    return (_silu(gate) * up,)


def _f_loss(pid, h, g, tgt, *, tm):
    y = h * lax.rsqrt(jnp.mean(h * h, axis=-1, keepdims=True) + EPS) * g
    row = pid * tm + lax.broadcasted_iota(jnp.int32, (tm, 1), 0)
    err = jnp.where(row >= SKIP, y - tgt, 0.0)
    per_row = jnp.mean(err * err, axis=-1, keepdims=True)
    return (0.5 * jnp.sum(per_row, axis=0, keepdims=True),)


def _f_conv(x, w, *, norm, scale):
    y = x * w[3:4, :]
    for s in (1, 2, 3):
        y = y + _shift_rows(x, s) * w[3 - s:4 - s, :]
    y = _silu(y)
    if norm:
        y = y * lax.rsqrt(jnp.sum(y * y, axis=-1, keepdims=True) + 1e-6) * scale
    return (y,)


def _f_dgates(pid, abeta, aalpha, log_rate, dt_bias, *, tm):
    row = pid * tm + lax.broadcasted_iota(jnp.int32, (tm, 1), 0)
    live = row >= PAD
    beta = jnp.where(live, _sigmoid(abeta), 0.0)
    g = jnp.where(live, -jnp.exp(log_rate) * _softplus(aalpha + dt_bias), 0.0)
    return beta, g


def _f_tshift(z, mu):
    return (z + (_shift_rows(z, 1) - z) * mu,)


def _f_rwkv_pre(k, wd, ad, gd, w0, w_up, a0, a_up, g_up, k_k, k_a):
    w_log = -_softplus(-(w0 + _smm(jnp.tanh(wd), w_up, 1))) - 0.5
    lw = -jnp.exp(w_log)
    a_lr = _sigmoid(a0 + _smm(ad, a_up, 1))
    gate = _smm(_sigmoid(gd), g_up, 1)
    kkp = k * k_k
    kk = kkp * lax.rsqrt(_head_sum(kkp * kkp, B_HEADS) + 1e-6)
    kmod = k * (1.0 + (a_lr - 1.0) * k_a)
    return lw, kmod, -kk, kk * a_lr, gate


def _f_mix_post(o, az, y, r, kmod, v, gate, ga, gb, out_gain, ln_g, ln_b, r_k):
    ms = _head_sum(o * o, A_HEADS) * (1.0 / A_DK)
    oa = o * lax.rsqrt(ms + EPS) * out_gain * _silu(az)
    mean = _head_sum(y, B_HEADS) * (1.0 / B_N)
    yc = y - mean
    var = _head_sum(yc * yc, B_HEADS) * (1.0 / B_N)
    yn = yc * lax.rsqrt(var + B_GN_EPS) * ln_g + ln_b
    bonus = _head_sum(r * kmod * r_k, B_HEADS) * v
    ob = (yn + bonus) * gate
    return (_sigmoid(ga) * oa + _sigmoid(gb) * ob,)


def _split2(a):
    hi = a.astype(bf16)
    return hi, (a - hi.astype(f32)).astype(bf16)


def _dot_passes(a, b, ca, cb, passes):
    dn = (((ca,), (cb,)), ((), ()))
    if passes == 1:
        return lax.dot_general(a.astype(bf16), b.astype(bf16), dn, preferred_element_type=f32)
    ah, al = _split2(a)
    bh, bl = _split2(b)
    return (lax.dot_general(ah, bh, dn, preferred_element_type=f32)
            + (lax.dot_general(ah, bl, dn, preferred_element_type=f32)
               + lax.dot_general(al, bh, dn, preferred_element_type=f32)))


@functools.partial(jax.custom_vjp, nondiff_argnums=(2, 3, 4))
def _sdot(a, b, ca, cb, passes):
    return _dot_passes(a, b, ca, cb, passes)


def _sdot_fwd(a, b, ca, cb, passes):
    return _dot_passes(a, b, ca, cb, passes), (a, b)


def _sdot_bwd(ca, cb, passes, res, g):
    a, b = res
    if (ca, cb) == (1, 0):
        return _dot_passes(g, b, 1, 1, passes), _dot_passes(a, g, 0, 0, passes)
    if (ca, cb) == (1, 1):
        return _dot_passes(g, b, 1, 0, passes), _dot_passes(g, a, 0, 0, passes)
    assert (ca, cb) == (0, 0)
    return _dot_passes(b, g, 1, 1, passes), _dot_passes(a, g, 1, 0, passes)


_sdot.defvjp(_sdot_fwd, _sdot_bwd)


def _smm(a, b, passes=3):
    return _sdot(a, b, 1, 0, passes)


def _smm_nt(a, b, passes=3):
    return _sdot(a, b, 1, 1, passes)


def _smm_tn(a, b, passes=3):
    return _sdot(a, b, 0, 0, passes)


def _tri_dot(x, ca):
    n = x.shape[0]
    incl = _tri_masks(n)[0]
    dn = (((ca,), (0,)), ((), ()))
    tri = incl.astype(bf16)
    hi, r1 = x.astype(bf16), None
    r1 = x - hi.astype(f32)
    mid = r1.astype(bf16)
    lo = (r1 - mid.astype(f32)).astype(bf16)
    return (lax.dot_general(tri, hi, dn, preferred_element_type=f32)
            + (lax.dot_general(tri, mid, dn, preferred_element_type=f32)
               + lax.dot_general(tri, lo, dn, preferred_element_type=f32)))


@jax.custom_vjp
def _cumsum_rows(x):
    return _tri_dot(x, 1)


def _cumsum_rows_fwd(x):
    return _tri_dot(x, 1), None


def _cumsum_rows_bwd(_, g):
    return (_tri_dot(g, 0),)


_cumsum_rows.defvjp(_cumsum_rows_fwd, _cumsum_rows_bwd)


def _tri_masks(n):
    i = lax.broadcasted_iota(jnp.int32, (n, n), 0)
    j = lax.broadcasted_iota(jnp.int32, (n, n), 1)
    return i >= j, i > j, i == j, i <= j


def _unit_lower_inv_impl(low, passes):
    n = low.shape[0]
    assert n == CHUNK
    _, _, eye, _ = _tri_masks(n)
    acc = eye.astype(f32) + low
    p = low
    for _ in range(5):
        p = _dot_passes(p, p, 1, 0, passes)
        acc = acc + _dot_passes(acc, p, 1, 0, passes)
    return acc


@functools.partial(jax.custom_vjp, nondiff_argnums=(1,))
def _unit_lower_inv(low, passes=3):
    return _unit_lower_inv_impl(low, passes)


def _unit_lower_inv_fwd(low, passes):
    t = _unit_lower_inv_impl(low, passes)
    return t, t


def _unit_lower_inv_bwd(passes, t, g):
    return (_dot_passes(_dot_passes(t, g, 0, 0, passes), t, 1, 1, passes),)


_unit_lower_inv.defvjp(_unit_lower_inv_fwd, _unit_lower_inv_bwd)


@functools.partial(jax.custom_vjp, nondiff_argnums=(2,))
def _unit_lower_inv_saved(low, t_saved, passes):
    return t_saved


def _unit_lower_inv_saved_fwd(low, t_saved, passes):
    return t_saved, t_saved


def _unit_lower_inv_saved_bwd(passes, t, g):
    return _unit_lower_inv_bwd(passes, t, g) + (jnp.zeros_like(t),)


_unit_lower_inv_saved.defvjp(_unit_lower_inv_saved_fwd, _unit_lower_inv_saved_bwd)


def _inverse(low, passes, saved):
    return _unit_lower_inv(low, passes) if saved is None else _unit_lower_inv_saved(low, saved, passes)

DELTA_PASSES = 1
DELTA_INV_PASSES = 1


def _delta_chunk(s, q, k, v, beta_row, g_row, inv_saved=None):
    p = DELTA_PASSES
    incl, strict, eye, upper = _tri_masks(CHUNK)
    beta = jnp.sum(jnp.where(eye, beta_row, 0.0), axis=1, keepdims=True)
    g = jnp.sum(jnp.where(eye, g_row, 0.0), axis=1, keepdims=True)
    gc = jnp.sum(jnp.where(incl, g_row, 0.0), axis=1, keepdims=True)
    gc_row = jnp.sum(jnp.where(upper, g, 0.0), axis=0, keepdims=True)
    decay = jnp.where(incl, jnp.exp(jnp.where(incl, gc - gc_row, 0.0)), 0.0)
    kb = k * beta
    vb = v * beta
    m = jnp.where(strict, _smm_nt(kb, k, p) * decay, 0.0)
    tinv = _inverse(-m, DELTA_INV_PASSES, inv_saved)
    u = _smm(tinv, vb, p)
    wk = _smm(tinv, kb * jnp.exp(gc), p)
    attn = _smm_nt(q, k, p) * decay
    qg = q * jnp.exp(gc)
    g_last = jnp.sum(g, axis=0, keepdims=True)
    k_tail = k * jnp.exp(g_last - gc)
    v_new = u - _smm(wk, s, p)
    o = _smm(qg, s, p) + _smm(attn, v_new, p)
    s_new = s * jnp.exp(g_last) + _smm_tn(k_tail, v_new, p)
    return o, s_new, tinv


RWKV_PASSES = 1
RWKV_INV_PASSES = 1


def _rwkv_chunk(st, r, k, v, a, b, lw, inv_saved=None):
    c = CHUNK
    p, pi = RWKV_PASSES, RWKV_INV_PASSES
    _, strict, _, _ = _tri_masks(c)
    lane = lax.broadcasted_iota(jnp.int32, (c, 2 * B_N), 1)
    row = lax.broadcasted_iota(jnp.int32, (c, 2 * B_N), 0)
    first = lane < B_N
    incl2 = row >= jnp.where(first, lane, lane - B_N)
    bi = lax.broadcasted_iota(jnp.int32, (2 * B_N, 2 * B_N), 0) < B_N
    bj = lax.broadcasted_iota(jnp.int32, (2 * B_N, 2 * B_N), 1) < B_N
    blockdiag = bi == bj
    cum = _cumsum_rows(lw)
    e_pos = jnp.exp(cum)
    e_neg = jnp.exp(-cum)
    rt = r * e_pos
    at = a * jnp.exp(cum - lw)
    kt = k * e_neg
    bt = b * e_neg
    bk = jnp.concatenate([bt, kt], axis=0)
    a_s0 = _smm_nt(at, st, p)
    r_s0 = _smm_nt(rt, st, p)
    heads = (first, jnp.logical_not(first))
    u = jnp.zeros((c, 2 * B_N), f32)
    invs = []
    for hi, sel in enumerate(heads):
        at_h = jnp.where(sel, at, 0.0)
        ab = jnp.where(strict, _smm_nt(at_h, bt, pi), 0.0)
        ak = jnp.where(strict, _smm_nt(at_h, kt, p), 0.0)
        t_h = _inverse(ab, pi, None if inv_saved is None else inv_saved[hi])
        invs.append(t_h)
        u = u + _smm(t_h, jnp.where(sel, a_s0, 0.0) + _smm(ak, jnp.where(sel, v, 0.0), p), p)
    y = r_s0
    for sel in heads:
        rbk = jnp.where(incl2, _smm_nt(jnp.where(sel, rt, 0.0), bk, p), 0.0)
        uv = jnp.concatenate([jnp.where(sel, u, 0.0), jnp.where(sel, v, 0.0)], axis=0)
        y = y + _smm(rbk, uv, p)
    cl = jnp.sum(lw, axis=0, keepdims=True)
    dec = jnp.exp(cl - cum)
    uv_all = jnp.concatenate([u, v], axis=0)
    bk_dec = jnp.concatenate([b * dec, k * dec], axis=0)
    st_new = st * jnp.exp(cl) + jnp.where(blockdiag, _smm_tn(uv_all, bk_dec, p), 0.0)
    return y, st_new, jnp.stack(invs)


GROUPS_PER_STEP = 8


def _scan_specs(ins, col_offs, n_chunks, reverse):
    gw = GROUPS_PER_STEP * LANES
    cidx = (lambda c: n_chunks - 1 - c) if reverse else (lambda c: c)
    specs = []
    for a, off in zip(ins, col_offs):
        if a.ndim == 2:
            assert off % gw == 0
            specs.append(pl.BlockSpec((CHUNK, gw), lambda h, c, o=off // gw: (cidx(c), h + o)))
        else:
            specs.append(pl.BlockSpec((GROUPS_PER_STEP, None, 1, CHUNK), lambda h, c: (h, cidx(c), 0, 0)))
    return specs, cidx


def _group_vals(refs, g):
    return [r[:, g * LANES:(g + 1) * LANES] if len(r.shape) == 2 else r[g] for r in refs]


def _scan_fwd(chunk_fn, ins, col_offs, n_groups, n_chunks, state_shape, inv_shape, name):
    n_in = len(ins)
    gps = GROUPS_PER_STEP
    t = ins[0].shape[0]

    def body(*refs):
        in_refs = refs[:n_in]
        o_ref, s0_ref, inv_ref, st = refs[n_in:]

        @pl.when(pl.program_id(1) == 0)
        def _():
            st[...] = jnp.zeros_like(st)

        states = st[...]
        vals = [jnp.stack(col) for col in zip(*[_group_vals(in_refs, g) for g in range(gps)])]
        o, s_new, inv = jax.vmap(chunk_fn)(states, *vals)
        s0_ref[...] = states
        inv_ref[...] = inv
        st[...] = s_new
        for g in range(gps):
            o_ref[:, g * LANES:(g + 1) * LANES] = o[g]

    specs, _ = _scan_specs(ins, col_offs, n_chunks, False)
    zeros_i = (0,) * len(inv_shape)
    return pl.pallas_call(
        body, name=name, grid=(n_groups // gps, n_chunks), in_specs=specs,
        out_specs=[pl.BlockSpec((CHUNK, gps * LANES), lambda h, c: (c, h)),
                   pl.BlockSpec((gps, None) + state_shape, lambda h, c: (h, c, 0, 0)),
                   pl.BlockSpec((gps, None) + inv_shape, lambda h, c: (h, c) + zeros_i)],
        out_shape=[jax.ShapeDtypeStruct((t, n_groups * LANES), f32),
                   jax.ShapeDtypeStruct((n_groups, n_chunks) + state_shape, f32),
                   jax.ShapeDtypeStruct((n_groups, n_chunks) + inv_shape, f32)],
        scratch_shapes=[pltpu.VMEM((gps,) + state_shape, f32)],
        compiler_params=pltpu.CompilerParams(dimension_semantics=("parallel", "arbitrary")),
    )(*ins)


def _scan_bwd(chunk_fn, s0s, invs, ins, col_offs, d_out, n_groups, n_chunks, state_shape, name):
    n_in = len(ins)
    gps = GROUPS_PER_STEP
    t = d_out.shape[0]
    inv_shape = invs.shape[2:]

    def body(*refs):
        s0_ref, inv_ref = refs[:2]
        in_refs = refs[2:2 + n_in]
        do_ref = refs[2 + n_in]
        g_refs = refs[3 + n_in:3 + 2 * n_in]
        dst = refs[3 + 2 * n_in]

        @pl.when(pl.program_id(1) == 0)
        def _():
            dst[...] = jnp.zeros_like(dst)

        vals = [jnp.stack(col) for col in zip(*[_group_vals(in_refs, g) for g in range(gps)])]
        d_o = jnp.stack([do_ref[:, g * LANES:(g + 1) * LANES] for g in range(gps)])
        inv = inv_ref[...]

        def with_saved(s, *a):
            return jax.vmap(lambda ss, ii, *aa: chunk_fn(ss, *aa, inv_saved=ii)[:2])(s, inv, *a)

        _, vjp = jax.vjp(with_saved, s0_ref[...], *vals)
        gs = vjp((d_o, dst[...]))
        dst[...] = gs[0]
        for g_ref, gv in zip(g_refs, gs[1:]):
            if len(g_ref.shape) == 2:
                for g in range(gps):
                    g_ref[:, g * LANES:(g + 1) * LANES] = gv[g]
            else:
                g_ref[...] = gv

    specs, cidx = _scan_specs(ins, col_offs, n_chunks, True)
    out_lane = pl.BlockSpec((CHUNK, gps * LANES), lambda h, c: (cidx(c), h))
    g_specs = [out_lane if a.ndim == 2 else sp for a, sp in zip(ins, specs)]
    g_shapes = [(t, n_groups * LANES) if a.ndim == 2 else a.shape for a in ins]
    s0_spec = pl.BlockSpec((gps, None) + state_shape, lambda h, c: (h, cidx(c), 0, 0))
    zeros_i = (0,) * len(inv_shape)
    inv_spec = pl.BlockSpec((gps, None) + inv_shape, lambda h, c: (h, cidx(c)) + zeros_i)
    return pl.pallas_call(
        body, name=name, grid=(n_groups // gps, n_chunks), in_specs=[s0_spec, inv_spec] + specs + [out_lane],
        out_specs=g_specs, out_shape=[jax.ShapeDtypeStruct(sh, f32) for sh in g_shapes],
        scratch_shapes=[pltpu.VMEM((gps,) + state_shape, f32)],
        compiler_params=pltpu.CompilerParams(dimension_semantics=("parallel", "arbitrary")),
    )(s0s, invs, *ins, d_out)


def _rms_fwd(x, g, name):
    t = x.shape[0]
    tm = _tile(t, 416, 16)
    return _tw_fwd(_f_rms, [x, g], [_row_spec(tm, D), _full_spec(g.shape)],
                   [jax.ShapeDtypeStruct(x.shape, MXU_DTYPE)], [_row_spec(tm, D)], (t // tm,), name)[0]


def _rms_bwd(x, g, dy, residual, name):
    t = x.shape[0]
    tm = _tile(t, 416, 8)
    return _tw_bwd(_f_rms, [x, g], [_row_spec(tm, D), _full_spec(g.shape)], [dy], [_row_spec(tm, D)],
                   ['tile', 'acc'], (t // tm,), name, residual=residual)


def _ffn_fwd(h, gain, wgu, wd, tag):
    xn = _rms_fwd(h, gain, f"{tag}_rms")
    gate, up, act = _gate_up_act(xn, wgu, f"{tag}_gate_up")
    out = _matmul(act, wd, res=h, scale=0.5, name=f"{tag}_down")
    return out, (xn, gate, up, act)


def _mxu_dot(a, b, dn):
    return lax.dot_general(a.astype(MXU_DTYPE), b.astype(MXU_DTYPE), dn, preferred_element_type=f32)


def _gate_up_act(xn, wgu, name):
    t = xn.shape[0]
    wdt = wgu.shape[2]
    tm = _tile(t, 416, 16)
    dn = (((1,), (0,)), ((), ()))

    def body(x_ref, wg_ref, wu_ref, g_ref, u_ref, a_ref):
        x = x_ref[...]
        g = _mxu_dot(x, wg_ref[...], dn)
        u = _mxu_dot(x, wu_ref[...], dn)
        g_ref[...] = g
        u_ref[...] = u
        a_ref[...] = _f_swiglu(g, u)[0].astype(a_ref.dtype)

    out_spec = pl.BlockSpec((tm, wdt), lambda j, i: (i, j))
    return pl.pallas_call(
        body, name=name, grid=(2, t // tm),
        in_specs=[pl.BlockSpec((tm, D), lambda j, i: (i, 0)), pl.BlockSpec((None, D, wdt), lambda j, i: (j, 0, 0)),
                  pl.BlockSpec((None, D, wdt), lambda j, i: (j + 2, 0, 0))],
        out_specs=[out_spec] * 3,
        out_shape=[jax.ShapeDtypeStruct((t, 2 * wdt), f32)] * 2 + [jax.ShapeDtypeStruct((t, 2 * wdt), MXU_DTYPE)],
        compiler_params=pltpu.CompilerParams(dimension_semantics=("parallel", "parallel")),
    )(xn, wgu, wgu)


def _d_gate_up(dout, wd, gate, up, name):
    t = dout.shape[0]
    wdt = D_FF // 2
    tm = _tile(t, 416, 16)
    dn = (((1,), (1,)), ((), ()))

    def body(do_ref, wd_ref, g_ref, u_ref, dg_ref, du_ref):
        d_act = 0.5 * _mxu_dot(do_ref[...], wd_ref[...], dn)
        _, vjp = jax.vjp(_f_swiglu, g_ref[...], u_ref[...])
        dg, du = vjp((d_act,))
        dg_ref[...] = dg.astype(dg_ref.dtype)
        du_ref[...] = du.astype(du_ref.dtype)

    spec = pl.BlockSpec((tm, wdt), lambda j, i: (i, j))
    return pl.pallas_call(
        body, name=name, grid=(2, t // tm),
        in_specs=[pl.BlockSpec((tm, D), lambda j, i: (i, 0)), pl.BlockSpec((wdt, D), lambda j, i: (j, 0)), spec, spec],
        out_specs=[spec] * 2, out_shape=[jax.ShapeDtypeStruct((t, D_FF), MXU_DTYPE)] * 2,
        compiler_params=pltpu.CompilerParams(dimension_semantics=("parallel", "parallel")),
    )(dout, wd, gate, up)


def _ffn_bwd(h, gain, wgu, wd, saved, dout, tag):
    xn, gate, up, act = saved
    t = h.shape[0]
    d_wd = _matmul(act, dout, ta=True, scale=0.5, name=f"{tag}_dwd")
    d_gate, d_up = _d_gate_up(dout, wd, gate, up, f"{tag}_dact")
    d_wgu = _matmul(xn, d_gate, ta=True, out_cols_split=(0, N_CHIPS), name=f"{tag}_dwg")
    d_wgu = _matmul(xn, d_up, ta=True, out_cols_split=(2, N_CHIPS), out_into=d_wgu, name=f"{tag}_dwu")
    d_xn = _matmul(d_gate, wgu, tb=True, b_cols_split=(0, 2), name=f"{tag}_dxn_g")
    d_xn = _matmul(d_up, wgu, tb=True, b_cols_split=(2, 2), res=d_xn, name=f"{tag}_dxn_u")
    d_h, d_gain = _rms_bwd(h, gain, d_xn, dout, f"{tag}_drms")
    return d_h, d_gain, d_wgu, d_wd


def _col_spec(t, first_block):
    return pl.BlockSpec((t, LANES), lambda j, fb=first_block: (0, j + fb))


def _local_step(h0, tgt, w):
    t = h0.shape[0]
    assert t % CHUNK == 0
    nc = t // CHUNK
    grads = {}

    h1, ffn1_saved = _ffn_fwd(h0, w['ffn1_norm'], w['ffn1_wgu'], w['ffn1_wd'], "ffn1")
    u = _rms_fwd(h1, w['mix_norm'], "mix_rms")
    z = _matmul(u, w['w_in_p'], name="in_proj")
    zs = z[:, 9216:9216 + 304]
    abeta, aalpha = zs[:, 288:296], zs[:, 296:304]

    conv_w = w['a_conv_w']
    conv_fns = [functools.partial(_f_conv, norm=True, scale=A_DK ** -0.5),
                functools.partial(_f_conv, norm=True, scale=1.0),
                functools.partial(_f_conv, norm=False, scale=1.0)]
    qkv = []
    for idx, fn in enumerate(conv_fns):
        qkv.append(_tw_fwd(fn, [z, conv_w], [_col_spec(t, 8 * idx), pl.BlockSpec((4, LANES), lambda j, o=8 * idx: (0, j + o))],
                           [jax.ShapeDtypeStruct((t, D), f32)], [_col_spec(t, 0)], (A_HEADS,), f"a_conv{idx}")[0])
    aq, ak, av = qkv
    tmg = _tile(t, 1040, 8)
    dg_fn = functools.partial(_f_dgates, tm=tmg)
    dg_specs = [_row_spec(tmg, A_HEADS)] * 2 + [_full_spec((1, A_HEADS))] * 2
    beta, gdec = _tw_fwd(dg_fn, [abeta, aalpha, w['a_log_rate'], w['a_dt_bias']], dg_specs,
                         [jax.ShapeDtypeStruct((t, A_HEADS), f32)] * 2, [_row_spec(tmg, A_HEADS)] * 2, (t // tmg,),
                         "a_gates", with_pid=True)
    beta_h = beta.T.reshape(A_HEADS, nc, 1, CHUNK)
    gdec_h = gdec.T.reshape(A_HEADS, nc, 1, CHUNK)
    a_ins = [aq, ak, av, beta_h, gdec_h]
    a_offs = [0] * 5
    o_scan, a_s0, a_inv = _scan_fwd(_delta_chunk, a_ins, a_offs, A_HEADS, nc, (A_DK, A_DK), (CHUNK, CHUNK), "a_scan")

    mu = w['b_shift_mu']
    mu_rkv, mu_s = mu[:, :3072], mu[:, 3072:]
    zf_rkv = _tw_fwd(_f_tshift, [z, mu_rkv], [_col_spec(t, 32), pl.BlockSpec((1, LANES), lambda j: (0, j))],
                     [jax.ShapeDtypeStruct((t, 3072), f32)], [_col_spec(t, 0)], (24,), "b_shift")[0]
    zs_b = zs[:, :288]
    zf_s = _tw_fwd(_f_tshift, [zs_b, mu_s], [_full_spec((t, 288)), _full_spec((1, 288))],
                   [jax.ShapeDtypeStruct((t, 288), f32)], [_full_spec((t, 288))], (1,), "b_shift_s")[0]
    wdf, adf, gdf = zf_s[:, 0:64], zf_s[:, 64:128], zf_s[:, 128:288]
    tmr = _tile(t, 160, 16)
    pre_params = [w['b_w0'], w['b_w_up'], w['b_a0'], w['b_a_up'], w['b_g_up'], w['b_k_k'], w['b_k_a']]
    pre_ins = [zf_rkv, wdf, adf, gdf] + pre_params
    pre_specs = ([_row_spec(tmr, D, 1), _row_spec(tmr, 64), _row_spec(tmr, 64), _row_spec(tmr, 160)]
                 + [_full_spec(p.shape) for p in pre_params])
    lw, kmod, a_s, b_s, bgate = _tw_fwd(_f_rwkv_pre, pre_ins, pre_specs, [jax.ShapeDtypeStruct((t, D), f32)] * 5,
                                        [_row_spec(tmr, D)] * 5, (t // tmr,), "b_pre")
    b_ins = [zf_rkv, kmod, zf_rkv, a_s, b_s, lw]
    b_offs = [0, 0, 2 * D, 0, 0, 0]
    y_scan, b_s0, b_inv = _scan_fwd(_rwkv_chunk, b_ins, b_offs, B_HEADS // 2, nc, (2 * B_N, 2 * B_N),
                                    (2, CHUNK, CHUNK), "b_scan")

    out_gain_t = jnp.tile(w['a_out_norm'], (1, A_HEADS))
    r_k = w['b_r_k'].reshape(1, D)
    post_params = [out_gain_t, w['b_ln_gain'], w['b_ln_bias'], r_k]
    post_ins = [o_scan, z, y_scan, zf_rkv, kmod, zf_rkv, bgate, z, z] + post_params
    post_specs = ([_row_spec(tmr, D), _row_spec(tmr, D, 3), _row_spec(tmr, D), _row_spec(tmr, D, 0), _row_spec(tmr, D),
                   _row_spec(tmr, D, 2), _row_spec(tmr, D), _row_spec(tmr, D, 7), _row_spec(tmr, D, 8)]
                  + [_full_spec((1, D))] * 4)
    merged = _tw_fwd(_f_mix_post, post_ins, post_specs, [jax.ShapeDtypeStruct((t, D), MXU_DTYPE)],
                     [_row_spec(tmr, D)], (t // tmr,), "mix_post")[0]
    h2 = _matmul(merged, w['w_out'], res=h1, name="out_proj")
    h3, ffn2_saved = _ffn_fwd(h2, w['ffn2_norm'], w['ffn2_wgu'], w['ffn2_wd'], "ffn2")

    tml = _tile(t, 416, 8)
    fnorm = w['final_norm']
    loss_fn = functools.partial(_f_loss, tm=tml)
    loss_specs = [_row_spec(tml, D), _full_spec((1, D)), _row_spec(tml, D)]
    loss_parts, d_h3, grads['final_norm'] = _loss_and_grad(loss_fn, h3, fnorm, tgt, loss_specs, tml)
    loss = jnp.sum(loss_parts)

    d_h2, grads['ffn2_norm'], grads['ffn2_wgu'], grads['ffn2_wd'] = _ffn_bwd(
        h2, w['ffn2_norm'], w['ffn2_wgu'], w['ffn2_wd'], ffn2_saved, d_h3, "ffn2")
    grads['w_out'] = _matmul(merged, d_h2, ta=True, name="d_w_out")
    d_merged = _matmul(d_h2, w['w_out'], tb=True, name="d_merged")

    win = ('tile', (t, D), _row_spec(tmr, D))
    zwin = win + (MXU_DTYPE,)
    post_kinds = ['tile', zwin, 'tile', win, 'tile', win, 'tile', zwin, zwin] + ['acc'] * 4
    (d_o, d_az, d_y, d_r1, d_kmod1, d_v1, d_bgate, d_ga, d_gb,
     d_out_gain_t, grads['b_ln_gain'], grads['b_ln_bias'], d_r_k) = _tw_bwd(
        _f_mix_post, post_ins, post_specs, [d_merged], [_row_spec(tmr, D)], post_kinds, (t // tmr,), "mix_post_bwd")
    grads['a_out_norm'] = jnp.sum(d_out_gain_t.reshape(A_HEADS, A_DK), axis=0, keepdims=True)
    grads['b_r_k'] = d_r_k.reshape(1, B_HEADS, B_N)

    d_r2, d_kmod2, d_v2, d_as, d_bs, d_lw = _scan_bwd(_rwkv_chunk, b_s0, b_inv, b_ins, b_offs, d_y, B_HEADS // 2, nc,
                                                      (2 * B_N, 2 * B_N), "b_scan_bwd")
    pre_kinds = [win] + ['tile'] * 3 + ['acc'] * 7
    pre_ct_specs = [_row_spec(tmr, D)] * 5
    (d_zf_k, d_wdf, d_adf, d_gdf, grads['b_w0'], grads['b_w_up'], grads['b_a0'], grads['b_a_up'], grads['b_g_up'],
     grads['b_k_k'], grads['b_k_a']) = _tw_bwd(
        _f_rwkv_pre, pre_ins, pre_specs, [d_lw, d_kmod1, d_as, d_bs, d_bgate], pre_ct_specs, pre_kinds, (t // tmr,),
        "b_pre_bwd", ct_extra=[(1, d_kmod2)])
    d_zb_rkv, d_mu_rkv = _shift_bwd3(z, mu_rkv, d_r1, d_r2, d_zf_k, d_v1, d_v2, t)
    d_zf_s = jnp.concatenate([d_wdf, d_adf, d_gdf], axis=1)
    d_zs_b, d_mu_s = _tw_bwd(_f_tshift, [zs_b, mu_s], [_full_spec((t, 288)), _full_spec((1, 288))], [d_zf_s],
                             [_full_spec((t, 288))], ['tile', 'tile'], (1,), "b_shift_s_bwd")
    grads['b_shift_mu'] = jnp.concatenate([d_mu_rkv, d_mu_s], axis=1)

    d_aq, d_ak, d_av, d_beta_h, d_g_h = _scan_bwd(_delta_chunk, a_s0, a_inv, a_ins, a_offs, d_o, A_HEADS, nc,
                                                  (A_DK, A_DK), "a_scan_bwd")
    d_beta = d_beta_h.reshape(A_HEADS, t).T
    d_gdec = d_g_h.reshape(A_HEADS, t).T
    d_abeta, d_aalpha, grads['a_log_rate'], grads['a_dt_bias'] = _tw_bwd(
        dg_fn, [abeta, aalpha, w['a_log_rate'], w['a_dt_bias']], dg_specs, [d_beta, d_gdec],
        [_row_spec(tmg, A_HEADS)] * 2, ['tile', 'tile', 'acc', 'acc'], (t // tmg,), "a_gates_bwd", with_pid=True)
    d_zqkv, d_conv = [], []
    for idx, (fn, ct) in enumerate(zip(conv_fns, (d_aq, d_ak, d_av))):
        dz_i, dw_i = _conv_bwd(fn, z, conv_w, ct, idx, t)
        d_zqkv.append(dz_i)
        d_conv.append(dw_i)
    grads['a_conv_w'] = jnp.concatenate(d_conv, axis=1)

    d_small = jnp.concatenate([d_zs_b, d_abeta, d_aalpha, jnp.zeros((t, ZP - 9216 - 304), f32)], axis=1)
    d_small = lax.optimization_barrier(d_small.astype(MXU_DTYPE))
    d_z_parts = d_zqkv + [d_az, d_zb_rkv, d_ga, d_gb, d_small]
    d_z = jnp.concatenate([p.astype(MXU_DTYPE) for p in d_z_parts], axis=1)
    grads['w_in_p'] = _matmul(u, d_z, ta=True, name="d_w_in")
    d_u = _matmul(d_z, w['w_in_p'], tb=True, name="d_u")
    d_h1, grads['mix_norm'] = _rms_bwd(h1, w['mix_norm'], d_u, d_h2, "mix_drms")
    d_h0, grads['ffn1_norm'], grads['ffn1_wgu'], grads['ffn1_wd'] = _ffn_bwd(
        h0, w['ffn1_norm'], w['ffn1_wgu'], w['ffn1_wd'], ffn1_saved, d_h1, "ffn1")
    return loss, d_h0, grads


_WIN_SEGMENTS = ((0, 4096), (4112, 7184), (7472, 9520), (7184, 7472), (4096, 4112))


_WIN_SHARD = IN_TOTAL // N_CHIPS


def _win_pieces():
    pieces, pad_at = [], 0
    for a, b in _WIN_SEGMENTS:
        c = a
        while c < b:
            stop = min(b, (c // _WIN_SHARD + 1) * _WIN_SHARD)
            pieces.append((c, pad_at + c - a, stop - c))
            c = stop
        pad_at += b - a
    return pieces


def _win_shards_to_padded(shards):
    parts = [shards[c // _WIN_SHARD][:, c % _WIN_SHARD:c % _WIN_SHARD + n] for c, _, n in _win_pieces()]
    parts.append(jnp.zeros((shards.shape[1], ZP - IN_TOTAL), shards.dtype))
    return jnp.concatenate(parts, axis=1)


def _win_padded_to_shards(w_p):
    by_shard = [[] for _ in range(N_CHIPS)]
    for c, p, n in sorted(_win_pieces()):
        by_shard[c // _WIN_SHARD].append(w_p[:, p:p + n])
    return jnp.stack([jnp.concatenate(parts, axis=1) for parts in by_shard])


def _loss_and_grad(loss_fn, h, gain, tgt, specs, tm):
    t = h.shape[0]
    n = t // tm

    def body(h_ref, g_ref, t_ref, l_ref, dh_ref, dg_ref):
        pid = pl.program_id(0)
        tg = t_ref[...]
        (part,), vjp = jax.vjp(lambda a, b: loss_fn(pid, a, b, tg), h_ref[...], g_ref[...])
        dh, dg = vjp((jnp.ones_like(part),))
        l_ref[...] = part
        dh_ref[...] = dh

        @pl.when(pid == 0)
        def _():
            dg_ref[...] = dg

        @pl.when(pid != 0)
        def _():
            dg_ref[...] += dg

    return pl.pallas_call(
        body, name="loss", grid=(n,), in_specs=specs,
        out_specs=[pl.BlockSpec((None, 1, 1), lambda i: (i, 0, 0)), specs[0], _full_spec(gain.shape)],
        out_shape=[jax.ShapeDtypeStruct((n, 1, 1), f32), jax.ShapeDtypeStruct(h.shape, f32),
                   jax.ShapeDtypeStruct(gain.shape, f32)],
    )(h, gain, tgt)


def _shift_bwd3(z, mu, d_r1, d_r2, d_k, d_v1, d_v2, t):
    nb = D // LANES

    def body(z_ref, mu_ref, r1, r2, kk, v1, v2, dz_ref, dmu_ref):
        j = pl.program_id(0)
        ct = jnp.where(j < nb, r1[...] + r2[...], jnp.where(j < 2 * nb, kk[...], v1[...] + v2[...]))
        _, vjp = jax.vjp(lambda a, b: _f_tshift(a, b), z_ref[...], mu_ref[...])
        dz, dmu = vjp((ct,))
        dz_ref[...] = dz.astype(dz_ref.dtype)
        dmu_ref[...] = dmu

    def window(first):
        return pl.BlockSpec((t, LANES), lambda j, f=first: (0, jnp.clip(j - f * nb, 0, nb - 1)))

    return pl.pallas_call(
        body, name="b_shift_bwd", grid=(3 * nb,),
        in_specs=[_col_spec(t, 32), pl.BlockSpec((1, LANES), lambda j: (0, j)), window(0), window(0), window(1),
                  window(2), window(2)],
        out_specs=[_col_spec(t, 0), pl.BlockSpec((1, LANES), lambda j: (0, j))],
        out_shape=[jax.ShapeDtypeStruct((t, 3 * D), MXU_DTYPE), jax.ShapeDtypeStruct((1, 3 * D), f32)],
    )(z, mu, d_r1, d_r2, d_k, d_v1, d_v2)


def _conv_bwd(fn, z, conv_w, ct, idx, t):
    def body(z_ref, w_ref, ct_ref, dz_ref, dw_ref):
        _, vjp = jax.vjp(lambda a, b: fn(a, b), z_ref[...], w_ref[...])
        dz, dw = vjp((ct_ref[...],))
        dz_ref[...] = dz.astype(dz_ref.dtype)
        dw_ref[...] = dw

    return pl.pallas_call(
        body, name=f"a_conv{idx}_bwd", grid=(A_HEADS,),
        in_specs=[_col_spec(t, 8 * idx), pl.BlockSpec((4, LANES), lambda j, o=8 * idx: (0, j + o)), _col_spec(t, 0)],
        out_specs=[_col_spec(t, 0), pl.BlockSpec((4, LANES), lambda j: (0, j))],
        out_shape=[jax.ShapeDtypeStruct((t, D), MXU_DTYPE), jax.ShapeDtypeStruct((4, D), f32)],
    )(z, conv_w, ct)


def _position():
    return lax.axis_index("x"), lax.axis_index("y"), lax.axis_index("c")


def _flip(v, f):
    return 1 - v if f else v


_CHIP_FLIPS = ((1, 0), (0, 1), (1, 1))


def _gather_chips(arrs, name):
    n = len(arrs)
    assert all(a.shape[0] % 32 == 0 for a in arrs)
    arrs = [a.reshape(2, a.shape[0] // 2, a.shape[1]) for a in arrs]

    def body(*refs):
        ins, outs = refs[:n], refs[n:2 * n]
        send, recv, fsend, frecv, own = refs[2 * n:]
        x, y, c = _position()
        me = 2 * x + y
        sends, plan, owns = [], [], []
        for a in range(n):
            cp = pltpu.make_async_remote_copy(src_ref=ins[a], dst_ref=outs[a].at[me], send_sem=own.at[a, 0],
                                              recv_sem=own.at[a, 1], device_id=(x, y, 1 - c), device_id_type=MESH)
            cp.start()
            owns.append(cp)
            for j, (fx, fy) in enumerate(_CHIP_FLIPS):
                px, py = _flip(x, fx), _flip(y, fy)
                p = 2 * px + py
                cp = pltpu.make_async_remote_copy(src_ref=ins[a].at[c], dst_ref=outs[a].at[me, c],
                                                  send_sem=send.at[a, j], recv_sem=recv.at[a, j],
                                                  device_id=(px, py, c), device_id_type=MESH)
                cp.start()
                sends.append(cp)
                landed = pltpu.make_async_remote_copy(src_ref=ins[a].at[c], dst_ref=outs[a].at[p, c],
                                                      send_sem=send.at[a, j], recv_sem=recv.at[a, j],
                                                      device_id=(px, py, c), device_id_type=MESH)
                onward = pltpu.make_async_remote_copy(src_ref=outs[a].at[p, c], dst_ref=outs[a].at[p, c],
                                                      send_sem=fsend.at[a, j], recv_sem=frecv.at[a, j],
                                                      device_id=(x, y, 1 - c), device_id_type=MESH)
                from_sibling = pltpu.make_async_remote_copy(src_ref=outs[a].at[p, 1 - c], dst_ref=outs[a].at[p, 1 - c],
                                                            send_sem=fsend.at[a, j], recv_sem=frecv.at[a, j],
                                                            device_id=(x, y, 1 - c), device_id_type=MESH)
                plan.append((landed, onward, from_sibling))
        for landed, onward, _ in plan:
            landed.wait_recv()
            onward.start()
        for _, _, from_sibling in plan:
            from_sibling.wait_recv()
        for cp in sends:
            cp.wait_send()
        for _, onward, _ in plan:
            onward.wait_send()
        for cp in owns:
            cp.wait()

    sems = [pltpu.SemaphoreType.DMA((n, 3))] * 4 + [pltpu.SemaphoreType.DMA((n, 2))]
    outs = pl.pallas_call(
        body, name=name, in_specs=[ANY] * n, out_specs=[ANY] * n,
        out_shape=[jax.ShapeDtypeStruct((N_CHIPS,) + a.shape, a.dtype) for a in arrs], scratch_shapes=sems,
    )(*arrs)
    return [o.reshape(N_CHIPS, o.shape[1] * o.shape[2], o.shape[3]) for o in outs]


def _swap_sibling(arrs, src_of, shapes, name):
    n = len(arrs)

    def body(*refs):
        a_refs, got_refs = refs[:n], refs[n:2 * n]
        send, recv = refs[2 * n:]
        x, y, c = _position()
        copies = []
        for i in range(n):
            cp = pltpu.make_async_remote_copy(src_ref=src_of(a_refs[i], c), dst_ref=got_refs[i], send_sem=send.at[i],
                                              recv_sem=recv.at[i], device_id=(x, y, 1 - c), device_id_type=MESH)
            cp.start()
            copies.append(cp)
        for cp in copies:
            cp.wait()

    return pl.pallas_call(body, name=name, in_specs=[ANY] * n, out_specs=[ANY] * n,
                          out_shape=[jax.ShapeDtypeStruct(sh, a.dtype) for sh, a in zip(shapes, arrs)],
                          scratch_shapes=[pltpu.SemaphoreType.DMA((n,))] * 2)(*arrs)


def _row_tile(rows, width):
    return _tile(rows, max(16, (784 * LANES // width) // 16 * 16), 16)


def _add_halves(g, got, dtype, name):
    n, _, hr, w = g.shape
    tr = _row_tile(hr, w)

    def body(g_ref, got_ref, o_ref):
        c = lax.axis_index("c")
        own = jnp.where(c == 0, g_ref[:, 0], g_ref[:, 1])
        o_ref[...] = (own + got_ref[...]).astype(dtype)

    return pl.pallas_call(
        body, name=name, grid=(hr // tr,),
        in_specs=[pl.BlockSpec((n, 2, tr, w), lambda i: (0, 0, i, 0)), pl.BlockSpec((n, tr, w), lambda i: (0, i, 0))],
        out_specs=pl.BlockSpec((n, tr, w), lambda i: (0, i, 0)),
        out_shape=jax.ShapeDtypeStruct((n, hr, w), dtype))(g, got)


def _scatter_chips(gs, name):
    n = len(gs)

    def body(*refs):
        g_refs, out_refs = refs[:n], refs[n:2 * n]
        send, recv = refs[2 * n:]
        x, y, c = _position()
        sends = []
        for i in range(n):
            for j, (fx, fy) in enumerate(_CHIP_FLIPS):
                px, py = _flip(x, fx), _flip(y, fy)
                cp = pltpu.make_async_remote_copy(src_ref=g_refs[i].at[2 * px + py], dst_ref=out_refs[i].at[j],
                                                  send_sem=send.at[i, j], recv_sem=recv.at[i, j],
                                                  device_id=(px, py, c), device_id_type=MESH)
                cp.start()
                sends.append(cp)
        for cp in sends:
            cp.wait_recv()
        for cp in sends:
            cp.wait_send()

    return pl.pallas_call(
        body, name=name, in_specs=[ANY] * n, out_specs=[ANY] * n,
        out_shape=[jax.ShapeDtypeStruct((3,) + g.shape[1:], g.dtype) for g in gs],
        scratch_shapes=[pltpu.SemaphoreType.DMA((n, 3)), pltpu.SemaphoreType.DMA((n, 3))],
    )(*gs)


def _sum_own_and_slots(own, got, name):
    n, r, w = own.shape
    tr = _row_tile(r, w)

    def body(own_ref, got_ref, o_ref):
        me = 2 * lax.axis_index("x") + lax.axis_index("y")
        acc = own_ref[0]
        for i in range(1, n):
            acc = jnp.where(me == i, own_ref[i], acc)
        acc = acc.astype(f32)
        for j in range(3):
            acc = acc + got_ref[j].astype(f32)
        o_ref[...] = acc

    return pl.pallas_call(
        body, name=name, grid=(r // tr,),
        in_specs=[pl.BlockSpec((n, tr, w), lambda i: (0, i, 0)), pl.BlockSpec((3, tr, w), lambda i: (0, i, 0))],
        out_specs=pl.BlockSpec((tr, w), lambda i: (i, 0)), out_shape=jax.ShapeDtypeStruct((r, w), f32))(own, got)


def _share_chips(a, name):
    def body(a_ref, out_ref, send, recv):
        x, y, c = _position()
        sends = []
        for j, (fx, fy) in enumerate(_CHIP_FLIPS):
            cp = pltpu.make_async_remote_copy(src_ref=a_ref, dst_ref=out_ref.at[j], send_sem=send.at[j],
                                              recv_sem=recv.at[j], device_id=(_flip(x, fx), _flip(y, fy), c),
                                              device_id_type=MESH)
            cp.start()
            sends.append(cp)
        for cp in sends:
            cp.wait_recv()
        for cp in sends:
            cp.wait_send()

    return pl.pallas_call(
        body, name=name, in_specs=[ANY], out_specs=ANY, out_shape=jax.ShapeDtypeStruct((3,) + a.shape, a.dtype),
        scratch_shapes=[pltpu.SemaphoreType.DMA((3,)), pltpu.SemaphoreType.DMA((3,))],
    )(a)


def _sum_in_chip_order(pair, got, name):
    r, w = pair.shape
    tr = _tile(r, 1408, 8)

    def body(p_ref, g_ref, o_ref):
        x, y = lax.axis_index("x"), lax.axis_index("y")
        me = 2 * x + y
        across = [2 * _flip(x, fx) + _flip(y, fy) for fx, fy in _CHIP_FLIPS]
        acc = None
        for i in range(N_CHIPS):
            term = p_ref[...]
            for j in range(3):
                term = jnp.where(across[j] == i, g_ref[j], term)
            acc = term if acc is None else acc + term
        o_ref[...] = acc

    return pl.pallas_call(
        body, name=name, grid=(r // tr,),
        in_specs=[pl.BlockSpec((tr, w), lambda i: (i, 0)), pl.BlockSpec((3, tr, w), lambda i: (0, i, 0))],
        out_specs=pl.BlockSpec((tr, w), lambda i: (i, 0)), out_shape=jax.ShapeDtypeStruct((r, w), f32))(pair, got)


def _add2(a, b, name):
    r, w = a.shape
    tr = _tile(r, 1408, 8)
    spec = pl.BlockSpec((tr, w), lambda i: (i, 0))

    def body(a_ref, b_ref, o_ref):
        o_ref[...] = a_ref[...] + b_ref[...]

    return pl.pallas_call(body, name=name, grid=(r // tr,), in_specs=[spec, spec], out_specs=spec,
                          out_shape=jax.ShapeDtypeStruct(a.shape, f32))(a, b)


def _adamw(w, g_parts, m, v, name):
    shape = w.shape
    view = shape if len(shape) >= 2 else (1,) + shape
    assert all(d == 1 for d in view[:-2]), shape
    rows, cols = view[-2:]
    cap = max(8, (256 * 1024 // cols) // 8 * 8)
    tr = rows if rows <= cap else _tile(rows, cap, 8)
    lead = len(view) - 2
    n_g = len(g_parts)

    def body(*refs):
        w_ref = refs[0]
        g_refs = refs[1:1 + n_g]
        m_ref, v_ref, g_out, d_out, m_out, v_out = refs[1 + n_g:]
        g = g_refs[0][...]
        for gr in g_refs[1:]:
            g = g + gr[...]
        m_new = ADAM_B1 * m_ref[...] + (1.0 - ADAM_B1) * g
        v_new = ADAM_B2 * v_ref[...] + (1.0 - ADAM_B2) * (g * g)
        m_hat = m_new / (1.0 - ADAM_B1 ** ADAM_STEP)
        v_hat = v_new / (1.0 - ADAM_B2 ** ADAM_STEP)
        g_out[...] = g
        d_out[...] = -ADAM_LR * (m_hat / (jnp.sqrt(v_hat) + ADAM_EPS) + ADAM_WD * w_ref[...])
        m_out[...] = m_new
        v_out[...] = v_new

    spec = pl.BlockSpec((None,) * lead + (tr, cols), lambda i: (0,) * lead + (i, 0))
    args = [w.reshape(view)] + [g.reshape(view) for g in g_parts] + [m.reshape(view), v.reshape(view)]
    outs = pl.pallas_call(body, name=name, grid=(rows // tr,), in_specs=[spec] * len(args), out_specs=[spec] * 4,
                          out_shape=[jax.ShapeDtypeStruct(view, f32)] * 4)(*args)
    return [o.reshape(shape) for o in outs]


_BIG = ('ffn1_w_gu', 'ffn1_w_down', 'w_in', 'w_out', 'ffn2_w_gu', 'ffn2_w_down')
_SMALL_SHARDED = ('meta_tokens', 'a_conv_w', 'b_w_up', 'b_a_up', 'b_g_up')
_WEIGHTS = ('meta_tokens', 'ffn1_norm', 'ffn1_w_gu', 'ffn1_w_down', 'mix_norm', 'w_in', 'a_conv_w', 'a_log_rate',
            'a_dt_bias', 'a_out_norm', 'b_shift_mu', 'b_w0', 'b_w_up', 'b_a0', 'b_a_up', 'b_g_up', 'b_k_k', 'b_k_a',
            'b_r_k', 'b_ln_gain', 'b_ln_bias', 'w_out', 'ffn2_norm', 'ffn2_w_gu', 'ffn2_w_down', 'final_norm')
_SMALL = tuple(n for n in _WEIGHTS if n not in _BIG)


def _rows_of(shape):
    n = 1
    for d in shape:
        n *= d
    return n, -(-n // LANES)


def _pack(arrs, dtype, row_mult=32):
    parts, total = [], 0
    for a in arrs:
        n, rows = _rows_of(a.shape)
        flat = a.reshape(-1).astype(dtype)
        if n % LANES:
            flat = jnp.pad(flat, (0, rows * LANES - n))
        parts.append(flat)
        total += rows
    extra = -total % row_mult
    if extra:
        parts.append(jnp.zeros((extra * LANES,), dtype))
    return jnp.concatenate(parts).reshape(total + extra, LANES)


def _unpack(packed, shapes, lead=()):
    out, off = [], 0
    for sh in shapes:
        n, rows = _rows_of(sh)
        seg = packed[..., off:off + rows, :]
        if n % LANES:
            seg = seg.reshape(lead + (-1,))[..., :n]
        out.append(seg.reshape(lead + tuple(sh)))
        off += rows
    return out


def _cols_from_shards(s):
    return jnp.concatenate([s[i] for i in range(N_CHIPS)], axis=-1)


def kernel(x, meta_tokens, ffn1_norm, ffn1_w_gu, ffn1_w_down, mix_norm, w_in, a_conv_w, a_log_rate, a_dt_bias, a_out_norm, b_shift_mu, b_w0, b_w_up, b_a0, b_a_up, b_g_up, b_k_k, b_k_a, b_r_k, b_ln_gain, b_ln_bias, w_out, ffn2_norm, ffn2_w_gu, ffn2_w_down, final_norm, loss_target, m_meta_tokens, m_ffn1_norm, m_ffn1_w_gu, m_ffn1_w_down, m_mix_norm, m_w_in, m_a_conv_w, m_a_log_rate, m_a_dt_bias, m_a_out_norm, m_b_shift_mu, m_b_w0, m_b_w_up, m_b_a0, m_b_a_up, m_b_g_up, m_b_k_k, m_b_k_a, m_b_r_k, m_b_ln_gain, m_b_ln_bias, m_w_out, m_ffn2_norm, m_ffn2_w_gu, m_ffn2_w_down, m_final_norm, v_meta_tokens, v_ffn1_norm, v_ffn1_w_gu, v_ffn1_w_down, v_mix_norm, v_w_in, v_a_conv_w, v_a_log_rate, v_a_dt_bias, v_a_out_norm, v_b_shift_mu, v_b_w0, v_b_w_up, v_b_a0, v_b_a_up, v_b_g_up, v_b_k_k, v_b_k_a, v_b_r_k, v_b_ln_gain, v_b_ln_bias, v_w_out, v_ffn2_norm, v_ffn2_w_gu, v_ffn2_w_down, v_final_norm):
    args = locals()
    wts = {n: args[n] for n in _WEIGHTS}
    mom = {n: args["m_" + n] for n in _WEIGHTS}
    var = {n: args["v_" + n] for n in _WEIGHTS}
    chip = 2 * lax.axis_index("x") + lax.axis_index("y")

    big_shapes = [wts[n].shape[1:] for n in _BIG]
    small_shapes = [wts[n].shape[-2:] for n in _SMALL_SHARDED]
    big_flat = [wts[n].astype(bf16).reshape(wts[n].shape[1:]) for n in _BIG]
    small_packed = _pack([wts[n] for n in _SMALL_SHARDED], f32)
    gathered = _gather_chips(big_flat + [small_packed], "gather_weights")
    gu1, dn1, w_in_s, w_out_s, gu2, dn2 = [a.reshape((N_CHIPS,) + tuple(sh)) for a, sh in zip(gathered, big_shapes)]
    meta_s, conv_s, wup_s, aup_s, gup_s = _unpack(gathered[-1], small_shapes, (N_CHIPS,))
    w = {
        'ffn1_norm': ffn1_norm, 'mix_norm': mix_norm, 'ffn2_norm': ffn2_norm, 'final_norm': final_norm[None, :],
        'ffn1_wgu': gu1, 'ffn1_wd': dn1.reshape(D_FF, D), 'ffn2_wgu': gu2, 'ffn2_wd': dn2.reshape(D_FF, D),
        'w_in_p': _win_shards_to_padded(w_in_s), 'w_out': w_out_s.reshape(D, D),
        'a_conv_w': _cols_from_shards(conv_s), 'b_w_up': _cols_from_shards(wup_s), 'b_a_up': _cols_from_shards(aup_s),
        'b_g_up': _cols_from_shards(gup_s),
        'a_log_rate': a_log_rate, 'a_dt_bias': a_dt_bias, 'a_out_norm': a_out_norm, 'b_shift_mu': b_shift_mu,
        'b_w0': b_w0, 'b_a0': b_a0, 'b_k_k': b_k_k, 'b_k_a': b_k_a, 'b_r_k': b_r_k, 'b_ln_gain': b_ln_gain,
        'b_ln_bias': b_ln_bias,
    }
    meta_full = _cols_from_shards(meta_s)

    h0 = jnp.concatenate([jnp.zeros((PAD, D), f32), meta_full, x[0]], axis=0)
    tgt = jnp.concatenate([jnp.zeros((SKIP, D), f32), loss_target[0]], axis=0)
    loss_local, d_h0, g = _local_step(h0, tgt, w)
    loss = lax.psum(loss_local, ("x", "y", "c"))
    grad_x = d_h0[SKIP:][None]

    big_grads = [
        g['ffn1_wgu'],
        g['ffn1_wd'].reshape(N_CHIPS, D_FF // N_CHIPS, D),
        _win_padded_to_shards(g['w_in_p']),
        g['w_out'].reshape(N_CHIPS, D // N_CHIPS, D),
        g['ffn2_wgu'],
        g['ffn2_wd'].reshape(N_CHIPS, D_FF // N_CHIPS, D),
    ]
    g_halves = [a.reshape(N_CHIPS, 2, a.shape[1] // 2, a.shape[2]) for a in big_grads]
    sib_halves = _swap_sibling(g_halves, lambda ref, c: ref.at[:, 1 - c], [a.shape[:1] + a.shape[2:] for a in g_halves],
                               "swap_halves")
    chip_halves = [_add_halves(a, b, bf16, f"add_sibling{i}") for i, (a, b) in enumerate(zip(g_halves, sib_halves))]
    got = _scatter_chips(chip_halves, "scatter_grads")
    mine = [_sum_own_and_slots(a, b, f"sum_chips{i}") for i, (a, b) in enumerate(zip(chip_halves, got))]
    theirs = _swap_sibling(mine, lambda ref, c: ref, [a.shape for a in mine], "swap_sums")
    core = lax.axis_index("c")
    big_parts = [jnp.concatenate([jnp.where(core == 0, a, b), jnp.where(core == 0, b, a)], axis=0)
                 for a, b in zip(mine, theirs)]

    small_full = {
        'meta_tokens': d_h0[PAD:SKIP], 'ffn1_norm': g['ffn1_norm'], 'mix_norm': g['mix_norm'], 'a_conv_w': g['a_conv_w'],
        'a_log_rate': g['a_log_rate'], 'a_dt_bias': g['a_dt_bias'], 'a_out_norm': g['a_out_norm'],
        'b_shift_mu': g['b_shift_mu'], 'b_w0': g['b_w0'], 'b_w_up': g['b_w_up'], 'b_a0': g['b_a0'], 'b_a_up': g['b_a_up'],
        'b_g_up': g['b_g_up'], 'b_k_k': g['b_k_k'], 'b_k_a': g['b_k_a'], 'b_r_k': g['b_r_k'], 'b_ln_gain': g['b_ln_gain'],
        'b_ln_bias': g['b_ln_bias'], 'ffn2_norm': g['ffn2_norm'], 'final_norm': g['final_norm'],
    }
    s_shapes = [small_full[n].shape for n in _SMALL]
    s_packed = _pack([small_full[n] for n in _SMALL], f32, row_mult=256)
    (s_sib,) = _swap_sibling([s_packed], lambda ref, c: ref, [s_packed.shape], "swap_small")
    s_pair = _add2(s_packed, s_sib, "add_small")
    s_sum = _sum_in_chip_order(s_pair, _share_chips(s_pair, "share_small"), "sum_small")
    s_parts = dict(zip(_SMALL, _unpack(s_sum, s_shapes)))

    grad, delta, new_m, new_v = {}, {}, {}, {}
    for n, a in zip(_BIG, big_parts):
        grad[n], delta[n], new_m[n], new_v[n] = _adamw(wts[n], [a.reshape(wts[n].shape)], mom[n], var[n], f"adamw_{n}")
    for n in _SMALL:
        gs = s_parts[n]
        if n in _SMALL_SHARDED:
            width = wts[n].shape[-1]
            gs = lax.dynamic_slice_in_dim(gs, chip * width, width, axis=gs.ndim - 1)
        gs = gs.reshape(wts[n].shape)
        grad[n], delta[n], new_m[n], new_v[n] = _adamw(wts[n], [gs], mom[n], var[n], f"adamw_{n}")

    return (loss, grad_x, *[grad[n] for n in _WEIGHTS], *[delta[n] for n in _WEIGHTS],
            *[new_m[n] for n in _WEIGHTS], *[new_v[n] for n in _WEIGHTS])
```

```python
import functools

import jax
import jax.numpy as jnp
from jax import lax
from jax.experimental import pallas as pl
from jax.experimental.pallas import tpu as pltpu

f32 = jnp.float32
bf16 = jnp.bfloat16
MESH = pl.DeviceIdType.MESH
ANY = pl.BlockSpec(memory_space=pl.ANY)

D = 1024
N_META = 16
CHUNK = 64
PAD = CHUNK - N_META
SKIP = PAD + N_META
EPS = 1e-6
D_FF = 2816
A_HEADS = 8
A_DK = 128
B_HEADS = 16
B_N = 64
B_GN_EPS = B_N * 1e-5
IN_TOTAL = 9520
ZP = 9600
LANES = 128
N_CHIPS = 4

ADAM_LR, ADAM_B1, ADAM_B2, ADAM_EPS, ADAM_WD, ADAM_STEP = 0.001, 0.9, 0.999, 1e-08, 0.01, 10

MXU_DTYPE = bf16


def _tile(n, cap, mult):
    if n <= cap:
        return n
    best = None
    for t in range(mult, cap + 1, mult):
        if n % t == 0:
            best = t
    assert best is not None, (n, cap, mult)
    return best


def _sigmoid(x):
    return jax.nn.sigmoid(x)


def _silu(x):
    return x * jax.nn.sigmoid(x)


def _softplus(x):
    return jnp.maximum(x, 0.0) + jnp.log(1.0 + jnp.exp(-jnp.abs(x)))


def _head_matrix(c, nh):
    hd = c // nh
    r = lax.broadcasted_iota(jnp.int32, (c, nh), 0)
    h = lax.broadcasted_iota(jnp.int32, (c, nh), 1)
    return (r >= h * hd) & (r < (h + 1) * hd)


def _dot_exact_rhs(x, e, cb):
    dn = (((1,), (cb,)), ((), ()))
    eb = e.astype(bf16)
    hi = x.astype(bf16)
    lo = (x - hi.astype(f32)).astype(bf16)
    return (lax.dot_general(hi, eb, dn, preferred_element_type=f32)
            + lax.dot_general(lo, eb, dn, preferred_element_type=f32))


def _head_sum_impl(x, nh):
    e = _head_matrix(x.shape[-1], nh)
    return _dot_exact_rhs(_dot_exact_rhs(x, e, 0), e, 1)


@functools.partial(jax.custom_vjp, nondiff_argnums=(1,))
def _head_sum(x, nh):
    return _head_sum_impl(x, nh)


def _head_sum_fwd(x, nh):
    return _head_sum_impl(x, nh), None


def _head_sum_bwd(nh, _, g):
    return (_head_sum_impl(g, nh),)


_head_sum.defvjp(_head_sum_fwd, _head_sum_bwd)


@functools.partial(jax.custom_vjp, nondiff_argnums=(1,))
def _shift_rows(x, s):
    n = x.shape[0]
    row = lax.broadcasted_iota(jnp.int32, x.shape, 0)
    if s > 0:
        return jnp.where(row >= s, pltpu.roll(x, s, 0), 0.0)
    return jnp.where(row < n + s, pltpu.roll(x, n + s, 0), 0.0)


def _shift_rows_fwd(x, s):
    return _shift_rows(x, s), None


def _shift_rows_bwd(s, _, g):
    return (_shift_rows(g, -s),)


_shift_rows.defvjp(_shift_rows_fwd, _shift_rows_bwd)


def _matmul(a, b, *, ta=False, tb=False, res=None, scale=1.0, name, b_cols_split=None, out_cols_split=None,
            out_into=None):
    assert not (ta and tb)
    (ar, ac) = a.shape
    b0 = 0
    if b_cols_split:
        b0, bs = b_cols_split
        _, br, bc_part = b.shape
        bc = bs * bc_part
    else:
        br, bc = b.shape
    m, k = (ac, ar) if ta else (ar, ac)
    n, kb = (br, bc) if tb else (bc, br)
    assert k == kb, (a.shape, b.shape, ta, tb)
    tm = _tile(m, 1408, LANES) if ta else _tile(m, 1040, 16)
    tn = _tile(n, 1920, LANES)
    tk = _tile(k, 1040, 8) if ta else _tile(k, 1920, LANES)
    nk = k // tk
    dn = (((0 if ta else 1,), (1 if tb else 0,)), ((), ()))
    if b_cols_split:
        assert (tk if tb else tn) == bc_part, (b.shape, tn, tk)

    def body(*refs):
        a_ref, b_ref = refs[:2]
        r_ref = refs[2] if res is not None else None
        o_ref, acc = refs[-2:]
        kk = pl.program_id(2)

        @pl.when(kk == 0)
        def _():
            acc[...] = jnp.zeros_like(acc)

        acc[...] += lax.dot_general(a_ref[...].astype(MXU_DTYPE), b_ref[...].astype(MXU_DTYPE), dn,
                                    preferred_element_type=f32)

        @pl.when(kk == nk - 1)
        def _():
            out = acc[...]
            if scale != 1.0:
                out = out * scale
            if res is not None:
                out = r_ref[...] + out
            o_ref[...] = out

    if ta:
        a_spec = pl.BlockSpec((tk, tm), lambda i, j, kk: (kk, i))
    else:
        a_spec = pl.BlockSpec((tm, tk), lambda i, j, kk: (i, kk))
    if tb and b_cols_split:
        b_spec = pl.BlockSpec((None, tn, tk), lambda i, j, kk: (kk + b0, j, 0))
    elif tb:
        b_spec = pl.BlockSpec((tn, tk), lambda i, j, kk: (j, kk))
    elif b_cols_split:
        b_spec = pl.BlockSpec((None, tk, tn), lambda i, j, kk: (j + b0, kk, 0))
    else:
        b_spec = pl.BlockSpec((tk, tn), lambda i, j, kk: (kk, j))
    in_specs = [a_spec, b_spec]
    args = [a, b]
    if res is not None:
        in_specs.append(pl.BlockSpec((tm, tn), lambda i, j, kk: (i, j)))
        args.append(res)
    aliases = {}
    if out_cols_split:
        o0, total = out_cols_split
        out_spec = pl.BlockSpec((None, tm, tn), lambda i, j, kk: (j + o0, i, 0))
        out_shape = jax.ShapeDtypeStruct((total, m, tn), f32)
        if out_into is not None:
            assert out_into.shape == out_shape.shape
            in_specs.append(ANY)
            args.append(out_into)
            aliases = {len(args) - 1: 0}
    else:
        out_spec = pl.BlockSpec((tm, tn), lambda i, j, kk: (i, j))
        out_shape = jax.ShapeDtypeStruct((m, n), f32)
    return pl.pallas_call(
        body, name=name, grid=(m // tm, n // tn, nk), in_specs=in_specs, out_specs=out_spec, out_shape=out_shape,
        scratch_shapes=[pltpu.VMEM((tm, tn), f32)], input_output_aliases=aliases,
        compiler_params=pltpu.CompilerParams(dimension_semantics=("parallel", "parallel", "arbitrary")),
    )(*args)


def _tw_fwd(fn, ins, in_specs, out_shapes, out_specs, grid, name, with_pid=False):
    n_in = len(ins)

    def body(*refs):
        vals = [r[...] for r in refs[:n_in]]
        outs = fn(pl.program_id(0), *vals) if with_pid else fn(*vals)
        for r, o in zip(refs[n_in:], outs):
            r[...] = o.astype(r.dtype)

    return pl.pallas_call(body, name=name, grid=grid, in_specs=in_specs, out_specs=out_specs,
                          out_shape=out_shapes)(*ins)


def _tw_bwd(fn, ins, in_specs, cts, ct_specs, kinds, grid, name, with_pid=False, tile_dtype=f32, ct_extra=(),
            residual=None):
    n_in, n_ct = len(ins), len(cts)
    diff = [i for i, kd in enumerate(kinds) if kd is not None]
    n_ex = len(ct_extra)

    def body(*refs):
        vals = [r[...] for r in refs[:n_in]]
        ctv = [r[...].astype(f32) for r in refs[n_in:n_in + n_ct]]
        for (ci, _), r in zip(ct_extra, refs[n_in + n_ct:n_in + n_ct + n_ex]):
            ctv[ci] = ctv[ci] + r[...]
        ctv = tuple(ctv)
        n_fixed = n_in + n_ct + n_ex
        res_ref = refs[n_fixed] if residual is not None else None
        g_refs = refs[n_fixed + (residual is not None):]
        pid = pl.program_id(0)

        def f(*dv):
            full = list(vals)
            for i, v in zip(diff, dv):
                full[i] = v
            out = fn(pid, *full) if with_pid else fn(*full)
            return tuple(out)

        _, vjp = jax.vjp(f, *[vals[i] for i in diff])
        gs = vjp(ctv)
        first = pid == 0
        for i2 in range(1, len(grid)):
            first = first & (pl.program_id(i2) == 0)
        for i, g, g_ref in zip(diff, gs, g_refs):
            if kinds[i] != 'acc':
                if i == 0 and res_ref is not None:
                    g = res_ref[...] + g
                g_ref[...] = g.astype(g_ref.dtype)
            else:
                @pl.when(first)
                def _(g=g, g_ref=g_ref):
                    g_ref[...] = g

                @pl.when(jnp.logical_not(first))
                def _(g=g, g_ref=g_ref):
                    g_ref[...] += g

    zero_map = {1: lambda *a: (0,), 2: lambda *a: (0, 0), 3: lambda *a: (0, 0, 0)}
    out_specs, out_shapes = [], []
    for i in diff:
        if kinds[i] == 'tile':
            out_shapes.append(jax.ShapeDtypeStruct(ins[i].shape, tile_dtype))
            out_specs.append(in_specs[i])
        elif kinds[i] == 'acc':
            out_shapes.append(jax.ShapeDtypeStruct(ins[i].shape, f32))
            out_specs.append(pl.BlockSpec(ins[i].shape, zero_map[ins[i].ndim]))
        else:
            out_shapes.append(jax.ShapeDtypeStruct(kinds[i][1], kinds[i][3] if len(kinds[i]) > 3 else tile_dtype))
            out_specs.append(kinds[i][2])
    extra_specs = [ct_specs[ci] for ci, _ in ct_extra]
    extra = [a for _, a in ct_extra]
    if residual is not None:
        assert kinds[0] == 'tile'
        extra_specs.append(in_specs[0])
        extra.append(residual)
    return pl.pallas_call(body, name=name, grid=grid, in_specs=list(in_specs) + list(ct_specs) + extra_specs,
                          out_specs=out_specs, out_shape=out_shapes)(*ins, *cts, *extra)


def _row_spec(tm, c, col_block=0):
    return pl.BlockSpec((tm, c), lambda i, cb=col_block: (i, cb))


def _full_spec(shape):
    nd = len(shape)
    return pl.BlockSpec(shape, lambda *a, nd=nd: (0,) * nd)


def _f_rms(x, g):
    return (x * lax.rsqrt(jnp.mean(x * x, axis=-1, keepdims=True) + EPS) * g,)


def _f_swiglu(gate, up):
    return (_silu(gate) * up,)


def _f_loss(pid, h, g, tgt, *, tm):
    y = h * lax.rsqrt(jnp.mean(h * h, axis=-1, keepdims=True) + EPS) * g
    row = pid * tm + lax.broadcasted_iota(jnp.int32, (tm, 1), 0)
    err = jnp.where(row >= SKIP, y - tgt, 0.0)
    per_row = jnp.mean(err * err, axis=-1, keepdims=True)
    return (0.5 * jnp.sum(per_row, axis=0, keepdims=True),)


def _f_conv(x, w, *, norm, scale):
    y = x * w[3:4, :]
    for s in (1, 2, 3):
        y = y + _shift_rows(x, s) * w[3 - s:4 - s, :]
    y = _silu(y)
    if norm:
        y = y * lax.rsqrt(jnp.sum(y * y, axis=-1, keepdims=True) + 1e-6) * scale
    return (y,)


def _f_dgates(pid, abeta, aalpha, log_rate, dt_bias, *, tm):
    row = pid * tm + lax.broadcasted_iota(jnp.int32, (tm, 1), 0)
    live = row >= PAD
    beta = jnp.where(live, _sigmoid(abeta), 0.0)
    g = jnp.where(live, -jnp.exp(log_rate) * _softplus(aalpha + dt_bias), 0.0)
    return beta, g


def _f_tshift(z, mu):
    return (z + (_shift_rows(z, 1) - z) * mu,)


def _f_rwkv_pre(k, wd, ad, gd, w0, w_up, a0, a_up, g_up, k_k, k_a):
    w_log = -_softplus(-(w0 + _smm(jnp.tanh(wd), w_up, 1))) - 0.5
    lw = -jnp.exp(w_log)
    a_lr = _sigmoid(a0 + _smm(ad, a_up, 1))
    gate = _smm(_sigmoid(gd), g_up, 1)
    kkp = k * k_k
    kk = kkp * lax.rsqrt(_head_sum(kkp * kkp, B_HEADS) + 1e-6)
    kmod = k * (1.0 + (a_lr - 1.0) * k_a)
    return lw, kmod, -kk, kk * a_lr, gate


def _f_mix_post(o, az, y, r, kmod, v, gate, ga, gb, out_gain, ln_g, ln_b, r_k):
    ms = _head_sum(o * o, A_HEADS) * (1.0 / A_DK)
    oa = o * lax.rsqrt(ms + EPS) * out_gain * _silu(az)
    mean = _head_sum(y, B_HEADS) * (1.0 / B_N)
    yc = y - mean
    var = _head_sum(yc * yc, B_HEADS) * (1.0 / B_N)
    yn = yc * lax.rsqrt(var + B_GN_EPS) * ln_g + ln_b
    bonus = _head_sum(r * kmod * r_k, B_HEADS) * v
    ob = (yn + bonus) * gate
    return (_sigmoid(ga) * oa + _sigmoid(gb) * ob,)


def _split2(a):
    hi = a.astype(bf16)
    return hi, (a - hi.astype(f32)).astype(bf16)


def _dot_passes(a, b, ca, cb, passes):
    dn = (((ca,), (cb,)), ((), ()))
    if passes == 1:
        return lax.dot_general(a.astype(bf16), b.astype(bf16), dn, preferred_element_type=f32)
    ah, al = _split2(a)
    bh, bl = _split2(b)
    return (lax.dot_general(ah, bh, dn, preferred_element_type=f32)
            + (lax.dot_general(ah, bl, dn, preferred_element_type=f32)
               + lax.dot_general(al, bh, dn, preferred_element_type=f32)))


@functools.partial(jax.custom_vjp, nondiff_argnums=(2, 3, 4))
def _sdot(a, b, ca, cb, passes):
    return _dot_passes(a, b, ca, cb, passes)


def _sdot_fwd(a, b, ca, cb, passes):
    return _dot_passes(a, b, ca, cb, passes), (a, b)


def _sdot_bwd(ca, cb, passes, res, g):
    a, b = res
    if (ca, cb) == (1, 0):
        return _dot_passes(g, b, 1, 1, passes), _dot_passes(a, g, 0, 0, passes)
    if (ca, cb) == (1, 1):
        return _dot_passes(g, b, 1, 0, passes), _dot_passes(g, a, 0, 0, passes)
    assert (ca, cb) == (0, 0)
    return _dot_passes(b, g, 1, 1, passes), _dot_passes(a, g, 1, 0, passes)


_sdot.defvjp(_sdot_fwd, _sdot_bwd)


def _smm(a, b, passes=3):
    return _sdot(a, b, 1, 0, passes)


def _smm_nt(a, b, passes=3):
    return _sdot(a, b, 1, 1, passes)


def _smm_tn(a, b, passes=3):
    return _sdot(a, b, 0, 0, passes)


def _tri_dot(x, ca):
    n = x.shape[0]
    incl = _tri_masks(n)[0]
    dn = (((ca,), (0,)), ((), ()))
    tri = incl.astype(bf16)
    hi, r1 = x.astype(bf16), None
    r1 = x - hi.astype(f32)
    mid = r1.astype(bf16)
    lo = (r1 - mid.astype(f32)).astype(bf16)
    return (lax.dot_general(tri, hi, dn, preferred_element_type=f32)
            + (lax.dot_general(tri, mid, dn, preferred_element_type=f32)
               + lax.dot_general(tri, lo, dn, preferred_element_type=f32)))


@jax.custom_vjp
def _cumsum_rows(x):
    return _tri_dot(x, 1)


def _cumsum_rows_fwd(x):
    return _tri_dot(x, 1), None


def _cumsum_rows_bwd(_, g):
    return (_tri_dot(g, 0),)


_cumsum_rows.defvjp(_cumsum_rows_fwd, _cumsum_rows_bwd)


def _tri_masks(n):
    i = lax.broadcasted_iota(jnp.int32, (n, n), 0)
    j = lax.broadcasted_iota(jnp.int32, (n, n), 1)
    return i >= j, i > j, i == j, i <= j


def _unit_lower_inv_impl(low, passes):
    n = low.shape[0]
    assert n == CHUNK
    _, _, eye, _ = _tri_masks(n)
    acc = eye.astype(f32) + low
    p = low
    for _ in range(5):
        p = _dot_passes(p, p, 1, 0, passes)
        acc = acc + _dot_passes(acc, p, 1, 0, passes)
    return acc


@functools.partial(jax.custom_vjp, nondiff_argnums=(1,))
def _unit_lower_inv(low, passes=3):
    return _unit_lower_inv_impl(low, passes)


def _unit_lower_inv_fwd(low, passes):
    t = _unit_lower_inv_impl(low, passes)
    return t, t


def _unit_lower_inv_bwd(passes, t, g):
    return (_dot_passes(_dot_passes(t, g, 0, 0, passes), t, 1, 1, passes),)


_unit_lower_inv.defvjp(_unit_lower_inv_fwd, _unit_lower_inv_bwd)


@functools.partial(jax.custom_vjp, nondiff_argnums=(2,))
def _unit_lower_inv_saved(low, t_saved, passes):
    return t_saved


def _unit_lower_inv_saved_fwd(low, t_saved, passes):
    return t_saved, t_saved


def _unit_lower_inv_saved_bwd(passes, t, g):
    return _unit_lower_inv_bwd(passes, t, g) + (jnp.zeros_like(t),)


_unit_lower_inv_saved.defvjp(_unit_lower_inv_saved_fwd, _unit_lower_inv_saved_bwd)


def _inverse(low, passes, saved):
    return _unit_lower_inv(low, passes) if saved is None else _unit_lower_inv_saved(low, saved, passes)

DELTA_PASSES = 1
DELTA_INV_PASSES = 1


def _delta_chunk(s, q, k, v, beta_row, g_row, inv_saved=None):
    p = DELTA_PASSES
    incl, strict, eye, upper = _tri_masks(CHUNK)
    beta = jnp.sum(jnp.where(eye, beta_row, 0.0), axis=1, keepdims=True)
    g = jnp.sum(jnp.where(eye, g_row, 0.0), axis=1, keepdims=True)
    gc = jnp.sum(jnp.where(incl, g_row, 0.0), axis=1, keepdims=True)
    gc_row = jnp.sum(jnp.where(upper, g, 0.0), axis=0, keepdims=True)
    decay = jnp.where(incl, jnp.exp(jnp.where(incl, gc - gc_row, 0.0)), 0.0)
    kb = k * beta
    vb = v * beta
    m = jnp.where(strict, _smm_nt(kb, k, p) * decay, 0.0)
    tinv = _inverse(-m, DELTA_INV_PASSES, inv_saved)
    u = _smm(tinv, vb, p)
    wk = _smm(tinv, kb * jnp.exp(gc), p)
    attn = _smm_nt(q, k, p) * decay
    qg = q * jnp.exp(gc)
    g_last = jnp.sum(g, axis=0, keepdims=True)
    k_tail = k * jnp.exp(g_last - gc)
    v_new = u - _smm(wk, s, p)
    o = _smm(qg, s, p) + _smm(attn, v_new, p)
    s_new = s * jnp.exp(g_last) + _smm_tn(k_tail, v_new, p)
    return o, s_new, tinv


RWKV_PASSES = 1
RWKV_INV_PASSES = 1


def _rwkv_chunk(st, r, k, v, a, b, lw, inv_saved=None):
    c = CHUNK
    p, pi = RWKV_PASSES, RWKV_INV_PASSES
    _, strict, _, _ = _tri_masks(c)
    lane = lax.broadcasted_iota(jnp.int32, (c, 2 * B_N), 1)
    row = lax.broadcasted_iota(jnp.int32, (c, 2 * B_N), 0)
    first = lane < B_N
    incl2 = row >= jnp.where(first, lane, lane - B_N)
    bi = lax.broadcasted_iota(jnp.int32, (2 * B_N, 2 * B_N), 0) < B_N
    bj = lax.broadcasted_iota(jnp.int32, (2 * B_N, 2 * B_N), 1) < B_N
    blockdiag = bi == bj
    cum = _cumsum_rows(lw)
    e_pos = jnp.exp(cum)
    e_neg = jnp.exp(-cum)
    rt = r * e_pos
    at = a * jnp.exp(cum - lw)
    kt = k * e_neg
    bt = b * e_neg
    bk = jnp.concatenate([bt, kt], axis=0)
    a_s0 = _smm_nt(at, st, p)
    r_s0 = _smm_nt(rt, st, p)
    heads = (first, jnp.logical_not(first))
    u = jnp.zeros((c, 2 * B_N), f32)
    invs = []
    for hi, sel in enumerate(heads):
        at_h = jnp.where(sel, at, 0.0)
        ab = jnp.where(strict, _smm_nt(at_h, bt, pi), 0.0)
        ak = jnp.where(strict, _smm_nt(at_h, kt, p), 0.0)
        t_h = _inverse(ab, pi, None if inv_saved is None else inv_saved[hi])
        invs.append(t_h)
        u = u + _smm(t_h, jnp.where(sel, a_s0, 0.0) + _smm(ak, jnp.where(sel, v, 0.0), p), p)
    y = r_s0
    for sel in heads:
        rbk = jnp.where(incl2, _smm_nt(jnp.where(sel, rt, 0.0), bk, p), 0.0)
        uv = jnp.concatenate([jnp.where(sel, u, 0.0), jnp.where(sel, v, 0.0)], axis=0)
        y = y + _smm(rbk, uv, p)
    cl = jnp.sum(lw, axis=0, keepdims=True)
    dec = jnp.exp(cl - cum)
    uv_all = jnp.concatenate([u, v], axis=0)
    bk_dec = jnp.concatenate([b * dec, k * dec], axis=0)
    st_new = st * jnp.exp(cl) + jnp.where(blockdiag, _smm_tn(uv_all, bk_dec, p), 0.0)
    return y, st_new, jnp.stack(invs)


GROUPS_PER_STEP = 8


def _scan_specs(ins, col_offs, n_chunks, reverse):
    gw = GROUPS_PER_STEP * LANES
    cidx = (lambda c: n_chunks - 1 - c) if reverse else (lambda c: c)
    specs = []
    for a, off in zip(ins, col_offs):
        if a.ndim == 2:
            assert off % gw == 0
            specs.append(pl.BlockSpec((CHUNK, gw), lambda h, c, o=off // gw: (cidx(c), h + o)))
        else:
            specs.append(pl.BlockSpec((GROUPS_PER_STEP, None, 1, CHUNK), lambda h, c: (h, cidx(c), 0, 0)))
    return specs, cidx


def _group_vals(refs, g):
    return [r[:, g * LANES:(g + 1) * LANES] if len(r.shape) == 2 else r[g] for r in refs]


def _scan_fwd(chunk_fn, ins, col_offs, n_groups, n_chunks, state_shape, inv_shape, name):
    n_in = len(ins)
    gps = GROUPS_PER_STEP
    t = ins[0].shape[0]

    def body(*refs):
        in_refs = refs[:n_in]
        o_ref, s0_ref, inv_ref, st = refs[n_in:]

        @pl.when(pl.program_id(1) == 0)
        def _():
            st[...] = jnp.zeros_like(st)

        states = st[...]
        vals = [jnp.stack(col) for col in zip(*[_group_vals(in_refs, g) for g in range(gps)])]
        o, s_new, inv = jax.vmap(chunk_fn)(states, *vals)
        s0_ref[...] = states
        inv_ref[...] = inv
        st[...] = s_new
        for g in range(gps):
            o_ref[:, g * LANES:(g + 1) * LANES] = o[g]

    specs, _ = _scan_specs(ins, col_offs, n_chunks, False)
    zeros_i = (0,) * len(inv_shape)
    return pl.pallas_call(
        body, name=name, grid=(n_groups // gps, n_chunks), in_specs=specs,
        out_specs=[pl.BlockSpec((CHUNK, gps * LANES), lambda h, c: (c, h)),
                   pl.BlockSpec((gps, None) + state_shape, lambda h, c: (h, c, 0, 0)),
                   pl.BlockSpec((gps, None) + inv_shape, lambda h, c: (h, c) + zeros_i)],
        out_shape=[jax.ShapeDtypeStruct((t, n_groups * LANES), f32),
                   jax.ShapeDtypeStruct((n_groups, n_chunks) + state_shape, f32),
                   jax.ShapeDtypeStruct((n_groups, n_chunks) + inv_shape, f32)],
        scratch_shapes=[pltpu.VMEM((gps,) + state_shape, f32)],
        compiler_params=pltpu.CompilerParams(dimension_semantics=("parallel", "arbitrary")),
    )(*ins)


def _scan_bwd(chunk_fn, s0s, invs, ins, col_offs, d_out, n_groups, n_chunks, state_shape, name):
    n_in = len(ins)
    gps = GROUPS_PER_STEP
    t = d_out.shape[0]
    inv_shape = invs.shape[2:]

    def body(*refs):
        s0_ref, inv_ref = refs[:2]
        in_refs = refs[2:2 + n_in]
        do_ref = refs[2 + n_in]
        g_refs = refs[3 + n_in:3 + 2 * n_in]
        dst = refs[3 + 2 * n_in]

        @pl.when(pl.program_id(1) == 0)
        def _():
            dst[...] = jnp.zeros_like(dst)

        vals = [jnp.stack(col) for col in zip(*[_group_vals(in_refs, g) for g in range(gps)])]
        d_o = jnp.stack([do_ref[:, g * LANES:(g + 1) * LANES] for g in range(gps)])
        inv = inv_ref[...]

        def with_saved(s, *a):
            return jax.vmap(lambda ss, ii, *aa: chunk_fn(ss, *aa, inv_saved=ii)[:2])(s, inv, *a)

        _, vjp = jax.vjp(with_saved, s0_ref[...], *vals)
        gs = vjp((d_o, dst[...]))
        dst[...] = gs[0]
        for g_ref, gv in zip(g_refs, gs[1:]):
            if len(g_ref.shape) == 2:
                for g in range(gps):
                    g_ref[:, g * LANES:(g + 1) * LANES] = gv[g]
            else:
                g_ref[...] = gv

    specs, cidx = _scan_specs(ins, col_offs, n_chunks, True)
    out_lane = pl.BlockSpec((CHUNK, gps * LANES), lambda h, c: (cidx(c), h))
    g_specs = [out_lane if a.ndim == 2 else sp for a, sp in zip(ins, specs)]
    g_shapes = [(t, n_groups * LANES) if a.ndim == 2 else a.shape for a in ins]
    s0_spec = pl.BlockSpec((gps, None) + state_shape, lambda h, c: (h, cidx(c), 0, 0))
    zeros_i = (0,) * len(inv_shape)
    inv_spec = pl.BlockSpec((gps, None) + inv_shape, lambda h, c: (h, cidx(c)) + zeros_i)
    return pl.pallas_call(
        body, name=name, grid=(n_groups // gps, n_chunks), in_specs=[s0_spec, inv_spec] + specs + [out_lane],
        out_specs=g_specs, out_shape=[jax.ShapeDtypeStruct(sh, f32) for sh in g_shapes],
        scratch_shapes=[pltpu.VMEM((gps,) + state_shape, f32)],
        compiler_params=pltpu.CompilerParams(dimension_semantics=("parallel", "arbitrary")),
    )(s0s, invs, *ins, d_out)


def _rms_fwd(x, g, name):
    t = x.shape[0]
    tm = _tile(t, 416, 16)
    return _tw_fwd(_f_rms, [x, g], [_row_spec(tm, D), _full_spec(g.shape)],
                   [jax.ShapeDtypeStruct(x.shape, MXU_DTYPE)], [_row_spec(tm, D)], (t // tm,), name)[0]


def _rms_bwd(x, g, dy, residual, name):
    t = x.shape[0]
    tm = _tile(t, 416, 8)
    return _tw_bwd(_f_rms, [x, g], [_row_spec(tm, D), _full_spec(g.shape)], [dy], [_row_spec(tm, D)],
                   ['tile', 'acc'], (t // tm,), name, residual=residual)


def _ffn_fwd(h, gain, wgu, wd, tag):
    xn = _rms_fwd(h, gain, f"{tag}_rms")
    gate, up, act = _gate_up_act(xn, wgu, f"{tag}_gate_up")
    out = _matmul(act, wd, res=h, scale=0.5, name=f"{tag}_down")
    return out, (xn, gate, up, act)


def _mxu_dot(a, b, dn):
    return lax.dot_general(a.astype(MXU_DTYPE), b.astype(MXU_DTYPE), dn, preferred_element_type=f32)


def _gate_up_act(xn, wgu, name):
    t = xn.shape[0]
    wdt = wgu.shape[2]
    tm = _tile(t, 416, 16)
    dn = (((1,), (0,)), ((), ()))

    def body(x_ref, wg_ref, wu_ref, g_ref, u_ref, a_ref):
        x = x_ref[...]
        g = _mxu_dot(x, wg_ref[...], dn)
        u = _mxu_dot(x, wu_ref[...], dn)
        g_ref[...] = g
        u_ref[...] = u
        a_ref[...] = _f_swiglu(g, u)[0].astype(a_ref.dtype)

    out_spec = pl.BlockSpec((tm, wdt), lambda j, i: (i, j))
    return pl.pallas_call(
        body, name=name, grid=(2, t // tm),
        in_specs=[pl.BlockSpec((tm, D), lambda j, i: (i, 0)), pl.BlockSpec((None, D, wdt), lambda j, i: (j, 0, 0)),
                  pl.BlockSpec((None, D, wdt), lambda j, i: (j + 2, 0, 0))],
        out_specs=[out_spec] * 3,
        out_shape=[jax.ShapeDtypeStruct((t, 2 * wdt), f32)] * 2 + [jax.ShapeDtypeStruct((t, 2 * wdt), MXU_DTYPE)],
        compiler_params=pltpu.CompilerParams(dimension_semantics=("parallel", "parallel")),
    )(xn, wgu, wgu)


def _d_gate_up(dout, wd, gate, up, name):
    t = dout.shape[0]
    wdt = D_FF // 2
    tm = _tile(t, 416, 16)
    dn = (((1,), (1,)), ((), ()))

    def body(do_ref, wd_ref, g_ref, u_ref, dg_ref, du_ref):
        d_act = 0.5 * _mxu_dot(do_ref[...], wd_ref[...], dn)
        _, vjp = jax.vjp(_f_swiglu, g_ref[...], u_ref[...])
        dg, du = vjp((d_act,))
        dg_ref[...] = dg.astype(dg_ref.dtype)
        du_ref[...] = du.astype(du_ref.dtype)

    spec = pl.BlockSpec((tm, wdt), lambda j, i: (i, j))
    return pl.pallas_call(
        body, name=name, grid=(2, t // tm),
        in_specs=[pl.BlockSpec((tm, D), lambda j, i: (i, 0)), pl.BlockSpec((wdt, D), lambda j, i: (j, 0)), spec, spec],
        out_specs=[spec] * 2, out_shape=[jax.ShapeDtypeStruct((t, D_FF), MXU_DTYPE)] * 2,
        compiler_params=pltpu.CompilerParams(dimension_semantics=("parallel", "parallel")),
    )(dout, wd, gate, up)


def _ffn_bwd(h, gain, wgu, wd, saved, dout, tag):
    xn, gate, up, act = saved
    t = h.shape[0]
    d_wd = _matmul(act, dout, ta=True, scale=0.5, name=f"{tag}_dwd")
    d_gate, d_up = _d_gate_up(dout, wd, gate, up, f"{tag}_dact")
    d_wgu = _matmul(xn, d_gate, ta=True, out_cols_split=(0, N_CHIPS), name=f"{tag}_dwg")
    d_wgu = _matmul(xn, d_up, ta=True, out_cols_split=(2, N_CHIPS), out_into=d_wgu, name=f"{tag}_dwu")
    d_xn = _matmul(d_gate, wgu, tb=True, b_cols_split=(0, 2), name=f"{tag}_dxn_g")
    d_xn = _matmul(d_up, wgu, tb=True, b_cols_split=(2, 2), res=d_xn, name=f"{tag}_dxn_u")
    d_h, d_gain = _rms_bwd(h, gain, d_xn, dout, f"{tag}_drms")
    return d_h, d_gain, d_wgu, d_wd


def _col_spec(t, first_block):
    return pl.BlockSpec((t, LANES), lambda j, fb=first_block: (0, j + fb))


def _local_step(h0, tgt, w):
    t = h0.shape[0]
    assert t % CHUNK == 0
    nc = t // CHUNK
    grads = {}

    h1, ffn1_saved = _ffn_fwd(h0, w['ffn1_norm'], w['ffn1_wgu'], w['ffn1_wd'], "ffn1")
    u = _rms_fwd(h1, w['mix_norm'], "mix_rms")
    z = _matmul(u, w['w_in_p'], name="in_proj")
    zs = z[:, 9216:9216 + 304]
    abeta, aalpha = zs[:, 288:296], zs[:, 296:304]

    conv_w = w['a_conv_w']
    conv_fns = [functools.partial(_f_conv, norm=True, scale=A_DK ** -0.5),
                functools.partial(_f_conv, norm=True, scale=1.0),
                functools.partial(_f_conv, norm=False, scale=1.0)]
    qkv = []
    for idx, fn in enumerate(conv_fns):
        qkv.append(_tw_fwd(fn, [z, conv_w], [_col_spec(t, 8 * idx), pl.BlockSpec((4, LANES), lambda j, o=8 * idx: (0, j + o))],
                           [jax.ShapeDtypeStruct((t, D), f32)], [_col_spec(t, 0)], (A_HEADS,), f"a_conv{idx}")[0])
    aq, ak, av = qkv
    tmg = _tile(t, 1040, 8)
    dg_fn = functools.partial(_f_dgates, tm=tmg)
    dg_specs = [_row_spec(tmg, A_HEADS)] * 2 + [_full_spec((1, A_HEADS))] * 2
    beta, gdec = _tw_fwd(dg_fn, [abeta, aalpha, w['a_log_rate'], w['a_dt_bias']], dg_specs,
                         [jax.ShapeDtypeStruct((t, A_HEADS), f32)] * 2, [_row_spec(tmg, A_HEADS)] * 2, (t // tmg,),
                         "a_gates", with_pid=True)
    beta_h = beta.T.reshape(A_HEADS, nc, 1, CHUNK)
    gdec_h = gdec.T.reshape(A_HEADS, nc, 1, CHUNK)
    a_ins = [aq, ak, av, beta_h, gdec_h]
    a_offs = [0] * 5
    o_scan, a_s0, a_inv = _scan_fwd(_delta_chunk, a_ins, a_offs, A_HEADS, nc, (A_DK, A_DK), (CHUNK, CHUNK), "a_scan")

    mu = w['b_shift_mu']
    mu_rkv, mu_s = mu[:, :3072], mu[:, 3072:]
    zf_rkv = _tw_fwd(_f_tshift, [z, mu_rkv], [_col_spec(t, 32), pl.BlockSpec((1, LANES), lambda j: (0, j))],
                     [jax.ShapeDtypeStruct((t, 3072), f32)], [_col_spec(t, 0)], (24,), "b_shift")[0]
    zs_b = zs[:, :288]
    zf_s = _tw_fwd(_f_tshift, [zs_b, mu_s], [_full_spec((t, 288)), _full_spec((1, 288))],
                   [jax.ShapeDtypeStruct((t, 288), f32)], [_full_spec((t, 288))], (1,), "b_shift_s")[0]
    wdf, adf, gdf = zf_s[:, 0:64], zf_s[:, 64:128], zf_s[:, 128:288]
    tmr = _tile(t, 160, 16)
    pre_params = [w['b_w0'], w['b_w_up'], w['b_a0'], w['b_a_up'], w['b_g_up'], w['b_k_k'], w['b_k_a']]
    pre_ins = [zf_rkv, wdf, adf, gdf] + pre_params
    pre_specs = ([_row_spec(tmr, D, 1), _row_spec(tmr, 64), _row_spec(tmr, 64), _row_spec(tmr, 160)]
                 + [_full_spec(p.shape) for p in pre_params])
    lw, kmod, a_s, b_s, bgate = _tw_fwd(_f_rwkv_pre, pre_ins, pre_specs, [jax.ShapeDtypeStruct((t, D), f32)] * 5,
                                        [_row_spec(tmr, D)] * 5, (t // tmr,), "b_pre")
    b_ins = [zf_rkv, kmod, zf_rkv, a_s, b_s, lw]
    b_offs = [0, 0, 2 * D, 0, 0, 0]
    y_scan, b_s0, b_inv = _scan_fwd(_rwkv_chunk, b_ins, b_offs, B_HEADS // 2, nc, (2 * B_N, 2 * B_N),
                                    (2, CHUNK, CHUNK), "b_scan")

    out_gain_t = jnp.tile(w['a_out_norm'], (1, A_HEADS))
    r_k = w['b_r_k'].reshape(1, D)
    post_params = [out_gain_t, w['b_ln_gain'], w['b_ln_bias'], r_k]
    post_ins = [o_scan, z, y_scan, zf_rkv, kmod, zf_rkv, bgate, z, z] + post_params
    post_specs = ([_row_spec(tmr, D), _row_spec(tmr, D, 3), _row_spec(tmr, D), _row_spec(tmr, D, 0), _row_spec(tmr, D),
                   _row_spec(tmr, D, 2), _row_spec(tmr, D), _row_spec(tmr, D, 7), _row_spec(tmr, D, 8)]
                  + [_full_spec((1, D))] * 4)
    merged = _tw_fwd(_f_mix_post, post_ins, post_specs, [jax.ShapeDtypeStruct((t, D), MXU_DTYPE)],
                     [_row_spec(tmr, D)], (t // tmr,), "mix_post")[0]
    h2 = _matmul(merged, w['w_out'], res=h1, name="out_proj")
    h3, ffn2_saved = _ffn_fwd(h2, w['ffn2_norm'], w['ffn2_wgu'], w['ffn2_wd'], "ffn2")

    tml = _tile(t, 416, 8)
    fnorm = w['final_norm']
    loss_fn = functools.partial(_f_loss, tm=tml)
    loss_specs = [_row_spec(tml, D), _full_spec((1, D)), _row_spec(tml, D)]
    loss_parts, d_h3, grads['final_norm'] = _loss_and_grad(loss_fn, h3, fnorm, tgt, loss_specs, tml)
    loss = jnp.sum(loss_parts)

    d_h2, grads['ffn2_norm'], grads['ffn2_wgu'], grads['ffn2_wd'] = _ffn_bwd(
        h2, w['ffn2_norm'], w['ffn2_wgu'], w['ffn2_wd'], ffn2_saved, d_h3, "ffn2")
    grads['w_out'] = _matmul(merged, d_h2, ta=True, name="d_w_out")
    d_merged = _matmul(d_h2, w['w_out'], tb=True, name="d_merged")

    win = ('tile', (t, D), _row_spec(tmr, D))
    zwin = win + (MXU_DTYPE,)
    post_kinds = ['tile', zwin, 'tile', win, 'tile', win, 'tile', zwin, zwin] + ['acc'] * 4
    (d_o, d_az, d_y, d_r1, d_kmod1, d_v1, d_bgate, d_ga, d_gb,
     d_out_gain_t, grads['b_ln_gain'], grads['b_ln_bias'], d_r_k) = _tw_bwd(
        _f_mix_post, post_ins, post_specs, [d_merged], [_row_spec(tmr, D)], post_kinds, (t // tmr,), "mix_post_bwd")
    grads['a_out_norm'] = jnp.sum(d_out_gain_t.reshape(A_HEADS, A_DK), axis=0, keepdims=True)
    grads['b_r_k'] = d_r_k.reshape(1, B_HEADS, B_N)

    d_r2, d_kmod2, d_v2, d_as, d_bs, d_lw = _scan_bwd(_rwkv_chunk, b_s0, b_inv, b_ins, b_offs, d_y, B_HEADS // 2, nc,
                                                      (2 * B_N, 2 * B_N), "b_scan_bwd")
    pre_kinds = [win] + ['tile'] * 3 + ['acc'] * 7
    pre_ct_specs = [_row_spec(tmr, D)] * 5
    (d_zf_k, d_wdf, d_adf, d_gdf, grads['b_w0'], grads['b_w_up'], grads['b_a0'], grads['b_a_up'], grads['b_g_up'],
     grads['b_k_k'], grads['b_k_a']) = _tw_bwd(
        _f_rwkv_pre, pre_ins, pre_specs, [d_lw, d_kmod1, d_as, d_bs, d_bgate], pre_ct_specs, pre_kinds, (t // tmr,),
        "b_pre_bwd", ct_extra=[(1, d_kmod2)])
    d_zb_rkv, d_mu_rkv = _shift_bwd3(z, mu_rkv, d_r1, d_r2, d_zf_k, d_v1, d_v2, t)
    d_zf_s = jnp.concatenate([d_wdf, d_adf, d_gdf], axis=1)
    d_zs_b, d_mu_s = _tw_bwd(_f_tshift, [zs_b, mu_s], [_full_spec((t, 288)), _full_spec((1, 288))], [d_zf_s],
                             [_full_spec((t, 288))], ['tile', 'tile'], (1,), "b_shift_s_bwd")
    grads['b_shift_mu'] = jnp.concatenate([d_mu_rkv, d_mu_s], axis=1)

    d_aq, d_ak, d_av, d_beta_h, d_g_h = _scan_bwd(_delta_chunk, a_s0, a_inv, a_ins, a_offs, d_o, A_HEADS, nc,
                                                  (A_DK, A_DK), "a_scan_bwd")
    d_beta = d_beta_h.reshape(A_HEADS, t).T
    d_gdec = d_g_h.reshape(A_HEADS, t).T
    d_abeta, d_aalpha, grads['a_log_rate'], grads['a_dt_bias'] = _tw_bwd(
        dg_fn, [abeta, aalpha, w['a_log_rate'], w['a_dt_bias']], dg_specs, [d_beta, d_gdec],
        [_row_spec(tmg, A_HEADS)] * 2, ['tile', 'tile', 'acc', 'acc'], (t // tmg,), "a_gates_bwd", with_pid=True)
    d_zqkv, d_conv = [], []
    for idx, (fn, ct) in enumerate(zip(conv_fns, (d_aq, d_ak, d_av))):
        dz_i, dw_i = _conv_bwd(fn, z, conv_w, ct, idx, t)
        d_zqkv.append(dz_i)
        d_conv.append(dw_i)
    grads['a_conv_w'] = jnp.concatenate(d_conv, axis=1)

    d_small = jnp.concatenate([d_zs_b, d_abeta, d_aalpha, jnp.zeros((t, ZP - 9216 - 304), f32)], axis=1)
    d_small = lax.optimization_barrier(d_small.astype(MXU_DTYPE))
    d_z_parts = d_zqkv + [d_az, d_zb_rkv, d_ga, d_gb, d_small]
    d_z = jnp.concatenate([p.astype(MXU_DTYPE) for p in d_z_parts], axis=1)
    grads['w_in_p'] = _matmul(u, d_z, ta=True, name="d_w_in")
    d_u = _matmul(d_z, w['w_in_p'], tb=True, name="d_u")
    d_h1, grads['mix_norm'] = _rms_bwd(h1, w['mix_norm'], d_u, d_h2, "mix_drms")
    d_h0, grads['ffn1_norm'], grads['ffn1_wgu'], grads['ffn1_wd'] = _ffn_bwd(
        h0, w['ffn1_norm'], w['ffn1_wgu'], w['ffn1_wd'], ffn1_saved, d_h1, "ffn1")
    return loss, d_h0, grads


_WIN_SEGMENTS = ((0, 4096), (4112, 7184), (7472, 9520), (7184, 7472), (4096, 4112))


_WIN_SHARD = IN_TOTAL // N_CHIPS


def _win_pieces():
    pieces, pad_at = [], 0
    for a, b in _WIN_SEGMENTS:
        c = a
        while c < b:
            stop = min(b, (c // _WIN_SHARD + 1) * _WIN_SHARD)
            pieces.append((c, pad_at + c - a, stop - c))
            c = stop
        pad_at += b - a
    return pieces


def _win_shards_to_padded(shards):
    parts = [shards[c // _WIN_SHARD][:, c % _WIN_SHARD:c % _WIN_SHARD + n] for c, _, n in _win_pieces()]
    parts.append(jnp.zeros((shards.shape[1], ZP - IN_TOTAL), shards.dtype))
    return jnp.concatenate(parts, axis=1)


def _win_padded_to_shards(w_p):
    by_shard = [[] for _ in range(N_CHIPS)]
    for c, p, n in sorted(_win_pieces()):
        by_shard[c // _WIN_SHARD].append(w_p[:, p:p + n])
    return jnp.stack([jnp.concatenate(parts, axis=1) for parts in by_shard])


def _loss_and_grad(loss_fn, h, gain, tgt, specs, tm):
    t = h.shape[0]
    n = t // tm

    def body(h_ref, g_ref, t_ref, l_ref, dh_ref, dg_ref):
        pid = pl.program_id(0)
        tg = t_ref[...]
        (part,), vjp = jax.vjp(lambda a, b: loss_fn(pid, a, b, tg), h_ref[...], g_ref[...])
        dh, dg = vjp((jnp.ones_like(part),))
        l_ref[...] = part
        dh_ref[...] = dh

        @pl.when(pid == 0)
        def _():
            dg_ref[...] = dg

        @pl.when(pid != 0)
        def _():
            dg_ref[...] += dg

    return pl.pallas_call(
        body, name="loss", grid=(n,), in_specs=specs,
        out_specs=[pl.BlockSpec((None, 1, 1), lambda i: (i, 0, 0)), specs[0], _full_spec(gain.shape)],
        out_shape=[jax.ShapeDtypeStruct((n, 1, 1), f32), jax.ShapeDtypeStruct(h.shape, f32),
                   jax.ShapeDtypeStruct(gain.shape, f32)],
    )(h, gain, tgt)


def _shift_bwd3(z, mu, d_r1, d_r2, d_k, d_v1, d_v2, t):
    nb = D // LANES

    def body(z_ref, mu_ref, r1, r2, kk, v1, v2, dz_ref, dmu_ref):
        j = pl.program_id(0)
        ct = jnp.where(j < nb, r1[...] + r2[...], jnp.where(j < 2 * nb, kk[...], v1[...] + v2[...]))
        _, vjp = jax.vjp(lambda a, b: _f_tshift(a, b), z_ref[...], mu_ref[...])
        dz, dmu = vjp((ct,))
        dz_ref[...] = dz.astype(dz_ref.dtype)
        dmu_ref[...] = dmu

    def window(first):
        return pl.BlockSpec((t, LANES), lambda j, f=first: (0, jnp.clip(j - f * nb, 0, nb - 1)))

    return pl.pallas_call(
        body, name="b_shift_bwd", grid=(3 * nb,),
        in_specs=[_col_spec(t, 32), pl.BlockSpec((1, LANES), lambda j: (0, j)), window(0), window(0), window(1),
                  window(2), window(2)],
        out_specs=[_col_spec(t, 0), pl.BlockSpec((1, LANES), lambda j: (0, j))],
        out_shape=[jax.ShapeDtypeStruct((t, 3 * D), MXU_DTYPE), jax.ShapeDtypeStruct((1, 3 * D), f32)],
    )(z, mu, d_r1, d_r2, d_k, d_v1, d_v2)


def _conv_bwd(fn, z, conv_w, ct, idx, t):
    def body(z_ref, w_ref, ct_ref, dz_ref, dw_ref):
        _, vjp = jax.vjp(lambda a, b: fn(a, b), z_ref[...], w_ref[...])
        dz, dw = vjp((ct_ref[...],))
        dz_ref[...] = dz.astype(dz_ref.dtype)
        dw_ref[...] = dw

    return pl.pallas_call(
        body, name=f"a_conv{idx}_bwd", grid=(A_HEADS,),
        in_specs=[_col_spec(t, 8 * idx), pl.BlockSpec((4, LANES), lambda j, o=8 * idx: (0, j + o)), _col_spec(t, 0)],
        out_specs=[_col_spec(t, 0), pl.BlockSpec((4, LANES), lambda j: (0, j))],
        out_shape=[jax.ShapeDtypeStruct((t, D), MXU_DTYPE), jax.ShapeDtypeStruct((4, D), f32)],
    )(z, conv_w, ct)


def _position():
    return lax.axis_index("x"), lax.axis_index("y"), lax.axis_index("c")


def _flip(v, f):
    return 1 - v if f else v


_CHIP_FLIPS = ((1, 0), (0, 1), (1, 1))


def _gather_chips(arrs, name):
    n = len(arrs)
    assert all(a.shape[0] % 32 == 0 for a in arrs)
    arrs = [a.reshape(2, a.shape[0] // 2, a.shape[1]) for a in arrs]

    def body(*refs):
        ins, outs = refs[:n], refs[n:2 * n]
        send, recv, fsend, frecv, own = refs[2 * n:]
        x, y, c = _position()
        me = 2 * x + y
        sends, plan, owns = [], [], []
        for a in range(n):
            cp = pltpu.make_async_remote_copy(src_ref=ins[a], dst_ref=outs[a].at[me], send_sem=own.at[a, 0],
                                              recv_sem=own.at[a, 1], device_id=(x, y, 1 - c), device_id_type=MESH)
            cp.start()
            owns.append(cp)
            for j, (fx, fy) in enumerate(_CHIP_FLIPS):
                px, py = _flip(x, fx), _flip(y, fy)
                p = 2 * px + py
                cp = pltpu.make_async_remote_copy(src_ref=ins[a].at[c], dst_ref=outs[a].at[me, c],
                                                  send_sem=send.at[a, j], recv_sem=recv.at[a, j],
                                                  device_id=(px, py, c), device_id_type=MESH)
                cp.start()
                sends.append(cp)
                landed = pltpu.make_async_remote_copy(src_ref=ins[a].at[c], dst_ref=outs[a].at[p, c],
                                                      send_sem=send.at[a, j], recv_sem=recv.at[a, j],
                                                      device_id=(px, py, c), device_id_type=MESH)
                onward = pltpu.make_async_remote_copy(src_ref=outs[a].at[p, c], dst_ref=outs[a].at[p, c],
                                                      send_sem=fsend.at[a, j], recv_sem=frecv.at[a, j],
                                                      device_id=(x, y, 1 - c), device_id_type=MESH)
                from_sibling = pltpu.make_async_remote_copy(src_ref=outs[a].at[p, 1 - c], dst_ref=outs[a].at[p, 1 - c],
                                                            send_sem=fsend.at[a, j], recv_sem=frecv.at[a, j],
                                                            device_id=(x, y, 1 - c), device_id_type=MESH)
                plan.append((landed, onward, from_sibling))
        for landed, onward, _ in plan:
            landed.wait_recv()
            onward.start()
        for _, _, from_sibling in plan:
            from_sibling.wait_recv()
        for cp in sends:
            cp.wait_send()
        for _, onward, _ in plan:
            onward.wait_send()
        for cp in owns:
            cp.wait()

    sems = [pltpu.SemaphoreType.DMA((n, 3))] * 4 + [pltpu.SemaphoreType.DMA((n, 2))]
    outs = pl.pallas_call(
        body, name=name, in_specs=[ANY] * n, out_specs=[ANY] * n,
        out_shape=[jax.ShapeDtypeStruct((N_CHIPS,) + a.shape, a.dtype) for a in arrs], scratch_shapes=sems,
    )(*arrs)
    return [o.reshape(N_CHIPS, o.shape[1] * o.shape[2], o.shape[3]) for o in outs]


def _swap_sibling(arrs, src_of, shapes, name):
    n = len(arrs)

    def body(*refs):
        a_refs, got_refs = refs[:n], refs[n:2 * n]
        send, recv = refs[2 * n:]
        x, y, c = _position()
        copies = []
        for i in range(n):
            cp = pltpu.make_async_remote_copy(src_ref=src_of(a_refs[i], c), dst_ref=got_refs[i], send_sem=send.at[i],
                                              recv_sem=recv.at[i], device_id=(x, y, 1 - c), device_id_type=MESH)
            cp.start()
            copies.append(cp)
        for cp in copies:
            cp.wait()

    return pl.pallas_call(body, name=name, in_specs=[ANY] * n, out_specs=[ANY] * n,
                          out_shape=[jax.ShapeDtypeStruct(sh, a.dtype) for sh, a in zip(shapes, arrs)],
                          scratch_shapes=[pltpu.SemaphoreType.DMA((n,))] * 2)(*arrs)


def _row_tile(rows, width):
    return _tile(rows, max(16, (784 * LANES // width) // 16 * 16), 16)


def _add_halves(g, got, dtype, name):
    n, _, hr, w = g.shape
    tr = _row_tile(hr, w)

    def body(g_ref, got_ref, o_ref):
        c = lax.axis_index("c")
        own = jnp.where(c == 0, g_ref[:, 0], g_ref[:, 1])
        o_ref[...] = (own + got_ref[...]).astype(dtype)

    return pl.pallas_call(
        body, name=name, grid=(hr // tr,),
        in_specs=[pl.BlockSpec((n, 2, tr, w), lambda i: (0, 0, i, 0)), pl.BlockSpec((n, tr, w), lambda i: (0, i, 0))],
        out_specs=pl.BlockSpec((n, tr, w), lambda i: (0, i, 0)),
        out_shape=jax.ShapeDtypeStruct((n, hr, w), dtype))(g, got)


def _scatter_chips(gs, name):
    n = len(gs)

    def body(*refs):
        g_refs, out_refs = refs[:n], refs[n:2 * n]
        send, recv = refs[2 * n:]
        x, y, c = _position()
        sends = []
        for i in range(n):
            for j, (fx, fy) in enumerate(_CHIP_FLIPS):
                px, py = _flip(x, fx), _flip(y, fy)
                cp = pltpu.make_async_remote_copy(src_ref=g_refs[i].at[2 * px + py], dst_ref=out_refs[i].at[j],
                                                  send_sem=send.at[i, j], recv_sem=recv.at[i, j],
                                                  device_id=(px, py, c), device_id_type=MESH)
                cp.start()
                sends.append(cp)
        for cp in sends:
            cp.wait_recv()
        for cp in sends:
            cp.wait_send()

    return pl.pallas_call(
        body, name=name, in_specs=[ANY] * n, out_specs=[ANY] * n,
        out_shape=[jax.ShapeDtypeStruct((3,) + g.shape[1:], g.dtype) for g in gs],
        scratch_shapes=[pltpu.SemaphoreType.DMA((n, 3)), pltpu.SemaphoreType.DMA((n, 3))],
    )(*gs)


def _sum_own_and_slots(own, got, name):
    n, r, w = own.shape
    tr = _row_tile(r, w)

    def body(own_ref, got_ref, o_ref):
        me = 2 * lax.axis_index("x") + lax.axis_index("y")
        acc = own_ref[0]
        for i in range(1, n):
            acc = jnp.where(me == i, own_ref[i], acc)
        acc = acc.astype(f32)
        for j in range(3):
            acc = acc + got_ref[j].astype(f32)
        o_ref[...] = acc

    return pl.pallas_call(
        body, name=name, grid=(r // tr,),
        in_specs=[pl.BlockSpec((n, tr, w), lambda i: (0, i, 0)), pl.BlockSpec((3, tr, w), lambda i: (0, i, 0))],
        out_specs=pl.BlockSpec((tr, w), lambda i: (i, 0)), out_shape=jax.ShapeDtypeStruct((r, w), f32))(own, got)


def _share_chips(a, name):
    def body(a_ref, out_ref, send, recv):
        x, y, c = _position()
        sends = []
        for j, (fx, fy) in enumerate(_CHIP_FLIPS):
            cp = pltpu.make_async_remote_copy(src_ref=a_ref, dst_ref=out_ref.at[j], send_sem=send.at[j],
                                              recv_sem=recv.at[j], device_id=(_flip(x, fx), _flip(y, fy), c),
                                              device_id_type=MESH)
            cp.start()
            sends.append(cp)
        for cp in sends:
            cp.wait_recv()
        for cp in sends:
            cp.wait_send()

    return pl.pallas_call(
        body, name=name, in_specs=[ANY], out_specs=ANY, out_shape=jax.ShapeDtypeStruct((3,) + a.shape, a.dtype),
        scratch_shapes=[pltpu.SemaphoreType.DMA((3,)), pltpu.SemaphoreType.DMA((3,))],
    )(a)


def _sum_in_chip_order(pair, got, name):
    r, w = pair.shape
    tr = _tile(r, 1408, 8)

    def body(p_ref, g_ref, o_ref):
        x, y = lax.axis_index("x"), lax.axis_index("y")
        me = 2 * x + y
        across = [2 * _flip(x, fx) + _flip(y, fy) for fx, fy in _CHIP_FLIPS]
        acc = None
        for i in range(N_CHIPS):
            term = p_ref[...]
            for j in range(3):
                term = jnp.where(across[j] == i, g_ref[j], term)
            acc = term if acc is None else acc + term
        o_ref[...] = acc

    return pl.pallas_call(
        body, name=name, grid=(r // tr,),
        in_specs=[pl.BlockSpec((tr, w), lambda i: (i, 0)), pl.BlockSpec((3, tr, w), lambda i: (0, i, 0))],
        out_specs=pl.BlockSpec((tr, w), lambda i: (i, 0)), out_shape=jax.ShapeDtypeStruct((r, w), f32))(pair, got)


def _add2(a, b, name):
    r, w = a.shape
    tr = _tile(r, 1408, 8)
    spec = pl.BlockSpec((tr, w), lambda i: (i, 0))

    def body(a_ref, b_ref, o_ref):
        o_ref[...] = a_ref[...] + b_ref[...]

    return pl.pallas_call(body, name=name, grid=(r // tr,), in_specs=[spec, spec], out_specs=spec,
                          out_shape=jax.ShapeDtypeStruct(a.shape, f32))(a, b)


def _adamw(w, g_parts, m, v, name):
    shape = w.shape
    view = shape if len(shape) >= 2 else (1,) + shape
    assert all(d == 1 for d in view[:-2]), shape
    rows, cols = view[-2:]
    cap = max(8, (512 * 1024 // cols) // 8 * 8)
    tr = rows if rows <= cap else _tile(rows, cap, 8)
    lead = len(view) - 2
    n_g = len(g_parts)

    def body(*refs):
        w_ref = refs[0]
        g_refs = refs[1:1 + n_g]
        m_ref, v_ref, g_out, d_out, m_out, v_out = refs[1 + n_g:]
        g = g_refs[0][...]
        for gr in g_refs[1:]:
            g = g + gr[...]
        m_new = ADAM_B1 * m_ref[...] + (1.0 - ADAM_B1) * g
        v_new = ADAM_B2 * v_ref[...] + (1.0 - ADAM_B2) * (g * g)
        m_hat = m_new / (1.0 - ADAM_B1 ** ADAM_STEP)
        v_hat = v_new / (1.0 - ADAM_B2 ** ADAM_STEP)
        g_out[...] = g
        d_out[...] = -ADAM_LR * (m_hat / (jnp.sqrt(v_hat) + ADAM_EPS) + ADAM_WD * w_ref[...])
        m_out[...] = m_new
        v_out[...] = v_new

    spec = pl.BlockSpec((None,) * lead + (tr, cols), lambda i: (0,) * lead + (i, 0))
    args = [w.reshape(view)] + [g.reshape(view) for g in g_parts] + [m.reshape(view), v.reshape(view)]
    outs = pl.pallas_call(body, name=name, grid=(rows // tr,), in_specs=[spec] * len(args), out_specs=[spec] * 4,
                          out_shape=[jax.ShapeDtypeStruct(view, f32)] * 4)(*args)
    return [o.reshape(shape) for o in outs]


_BIG = ('ffn1_w_gu', 'ffn1_w_down', 'w_in', 'w_out', 'ffn2_w_gu', 'ffn2_w_down')
_SMALL_SHARDED = ('meta_tokens', 'a_conv_w', 'b_w_up', 'b_a_up', 'b_g_up')
_WEIGHTS = ('meta_tokens', 'ffn1_norm', 'ffn1_w_gu', 'ffn1_w_down', 'mix_norm', 'w_in', 'a_conv_w', 'a_log_rate',
            'a_dt_bias', 'a_out_norm', 'b_shift_mu', 'b_w0', 'b_w_up', 'b_a0', 'b_a_up', 'b_g_up', 'b_k_k', 'b_k_a',
            'b_r_k', 'b_ln_gain', 'b_ln_bias', 'w_out', 'ffn2_norm', 'ffn2_w_gu', 'ffn2_w_down', 'final_norm')
_SMALL = tuple(n for n in _WEIGHTS if n not in _BIG)


def _rows_of(shape):
    n = 1
    for d in shape:
        n *= d
    return n, -(-n // LANES)


def _pack(arrs, dtype, row_mult=32):
    parts, total = [], 0
    for a in arrs:
        n, rows = _rows_of(a.shape)
        flat = a.reshape(-1).astype(dtype)
        if n % LANES:
            flat = jnp.pad(flat, (0, rows * LANES - n))
        parts.append(flat)
        total += rows
    extra = -total % row_mult
    if extra:
        parts.append(jnp.zeros((extra * LANES,), dtype))
    return jnp.concatenate(parts).reshape(total + extra, LANES)


def _unpack(packed, shapes, lead=()):
    out, off = [], 0
    for sh in shapes:
        n, rows = _rows_of(sh)
        seg = packed[..., off:off + rows, :]
        if n % LANES:
            seg = seg.reshape(lead + (-1,))[..., :n]
        out.append(seg.reshape(lead + tuple(sh)))
        off += rows
    return out


def _cols_from_shards(s):
    return jnp.concatenate([s[i] for i in range(N_CHIPS)], axis=-1)


def kernel(x, meta_tokens, ffn1_norm, ffn1_w_gu, ffn1_w_down, mix_norm, w_in, a_conv_w, a_log_rate, a_dt_bias, a_out_norm, b_shift_mu, b_w0, b_w_up, b_a0, b_a_up, b_g_up, b_k_k, b_k_a, b_r_k, b_ln_gain, b_ln_bias, w_out, ffn2_norm, ffn2_w_gu, ffn2_w_down, final_norm, loss_target, m_meta_tokens, m_ffn1_norm, m_ffn1_w_gu, m_ffn1_w_down, m_mix_norm, m_w_in, m_a_conv_w, m_a_log_rate, m_a_dt_bias, m_a_out_norm, m_b_shift_mu, m_b_w0, m_b_w_up, m_b_a0, m_b_a_up, m_b_g_up, m_b_k_k, m_b_k_a, m_b_r_k, m_b_ln_gain, m_b_ln_bias, m_w_out, m_ffn2_norm, m_ffn2_w_gu, m_ffn2_w_down, m_final_norm, v_meta_tokens, v_ffn1_norm, v_ffn1_w_gu, v_ffn1_w_down, v_mix_norm, v_w_in, v_a_conv_w, v_a_log_rate, v_a_dt_bias, v_a_out_norm, v_b_shift_mu, v_b_w0, v_b_w_up, v_b_a0, v_b_a_up, v_b_g_up, v_b_k_k, v_b_k_a, v_b_r_k, v_b_ln_gain, v_b_ln_bias, v_w_out, v_ffn2_norm, v_ffn2_w_gu, v_ffn2_w_down, v_final_norm):
    args = locals()
    wts = {n: args[n] for n in _WEIGHTS}
    mom = {n: args["m_" + n] for n in _WEIGHTS}
    var = {n: args["v_" + n] for n in _WEIGHTS}
    chip = 2 * lax.axis_index("x") + lax.axis_index("y")

    big_shapes = [wts[n].shape[1:] for n in _BIG]
    small_shapes = [wts[n].shape[-2:] for n in _SMALL_SHARDED]
    big_flat = [wts[n].astype(bf16).reshape(wts[n].shape[1:]) for n in _BIG]
    small_packed = _pack([wts[n] for n in _SMALL_SHARDED], f32)
    gathered = _gather_chips(big_flat + [small_packed], "gather_weights")
    gu1, dn1, w_in_s, w_out_s, gu2, dn2 = [a.reshape((N_CHIPS,) + tuple(sh)) for a, sh in zip(gathered, big_shapes)]
    meta_s, conv_s, wup_s, aup_s, gup_s = _unpack(gathered[-1], small_shapes, (N_CHIPS,))
    w = {
        'ffn1_norm': ffn1_norm, 'mix_norm': mix_norm, 'ffn2_norm': ffn2_norm, 'final_norm': final_norm[None, :],
        'ffn1_wgu': gu1, 'ffn1_wd': dn1.reshape(D_FF, D), 'ffn2_wgu': gu2, 'ffn2_wd': dn2.reshape(D_FF, D),
        'w_in_p': _win_shards_to_padded(w_in_s), 'w_out': w_out_s.reshape(D, D),
        'a_conv_w': _cols_from_shards(conv_s), 'b_w_up': _cols_from_shards(wup_s), 'b_a_up': _cols_from_shards(aup_s),
        'b_g_up': _cols_from_shards(gup_s),
        'a_log_rate': a_log_rate, 'a_dt_bias': a_dt_bias, 'a_out_norm': a_out_norm, 'b_shift_mu': b_shift_mu,
        'b_w0': b_w0, 'b_a0': b_a0, 'b_k_k': b_k_k, 'b_k_a': b_k_a, 'b_r_k': b_r_k, 'b_ln_gain': b_ln_gain,
        'b_ln_bias': b_ln_bias,
    }
    meta_full = _cols_from_shards(meta_s)

    h0 = jnp.concatenate([jnp.zeros((PAD, D), f32), meta_full, x[0]], axis=0)
    tgt = jnp.concatenate([jnp.zeros((SKIP, D), f32), loss_target[0]], axis=0)
    loss_local, d_h0, g = _local_step(h0, tgt, w)
    loss = lax.psum(loss_local, ("x", "y", "c"))
    grad_x = d_h0[SKIP:][None]

    big_grads = [
        g['ffn1_wgu'],
        g['ffn1_wd'].reshape(N_CHIPS, D_FF // N_CHIPS, D),
        _win_padded_to_shards(g['w_in_p']),
        g['w_out'].reshape(N_CHIPS, D // N_CHIPS, D),
        g['ffn2_wgu'],
        g['ffn2_wd'].reshape(N_CHIPS, D_FF // N_CHIPS, D),
    ]
    g_halves = [a.reshape(N_CHIPS, 2, a.shape[1] // 2, a.shape[2]) for a in big_grads]
    sib_halves = _swap_sibling(g_halves, lambda ref, c: ref.at[:, 1 - c], [a.shape[:1] + a.shape[2:] for a in g_halves],
                               "swap_halves")
    chip_halves = [_add_halves(a, b, bf16, f"add_sibling{i}") for i, (a, b) in enumerate(zip(g_halves, sib_halves))]
    got = _scatter_chips(chip_halves, "scatter_grads")
    mine = [_sum_own_and_slots(a, b, f"sum_chips{i}") for i, (a, b) in enumerate(zip(chip_halves, got))]
    theirs = _swap_sibling(mine, lambda ref, c: ref, [a.shape for a in mine], "swap_sums")
    core = lax.axis_index("c")
    big_parts = [jnp.concatenate([jnp.where(core == 0, a, b), jnp.where(core == 0, b, a)], axis=0)
                 for a, b in zip(mine, theirs)]

    small_full = {
        'meta_tokens': d_h0[PAD:SKIP], 'ffn1_norm': g['ffn1_norm'], 'mix_norm': g['mix_norm'], 'a_conv_w': g['a_conv_w'],
        'a_log_rate': g['a_log_rate'], 'a_dt_bias': g['a_dt_bias'], 'a_out_norm': g['a_out_norm'],
        'b_shift_mu': g['b_shift_mu'], 'b_w0': g['b_w0'], 'b_w_up': g['b_w_up'], 'b_a0': g['b_a0'], 'b_a_up': g['b_a_up'],
        'b_g_up': g['b_g_up'], 'b_k_k': g['b_k_k'], 'b_k_a': g['b_k_a'], 'b_r_k': g['b_r_k'], 'b_ln_gain': g['b_ln_gain'],
        'b_ln_bias': g['b_ln_bias'], 'ffn2_norm': g['ffn2_norm'], 'final_norm': g['final_norm'],
    }
    s_shapes = [small_full[n].shape for n in _SMALL]
    s_packed = _pack([small_full[n] for n in _SMALL], f32, row_mult=256)
    (s_sib,) = _swap_sibling([s_packed], lambda ref, c: ref, [s_packed.shape], "swap_small")
    s_pair = _add2(s_packed, s_sib, "add_small")
    s_sum = _sum_in_chip_order(s_pair, _share_chips(s_pair, "share_small"), "sum_small")
    s_parts = dict(zip(_SMALL, _unpack(s_sum, s_shapes)))

    grad, delta, new_m, new_v = {}, {}, {}, {}
    for n, a in zip(_BIG, big_parts):
        grad[n], delta[n], new_m[n], new_v[n] = _adamw(wts[n], [a.reshape(wts[n].shape)], mom[n], var[n], f"adamw_{n}")
    for n in _SMALL:
        gs = s_parts[n]
        if n in _SMALL_SHARDED:
            width = wts[n].shape[-1]
            gs = lax.dynamic_slice_in_dim(gs, chip * width, width, axis=gs.ndim - 1)
        gs = gs.reshape(wts[n].shape)
        grad[n], delta[n], new_m[n], new_v[n] = _adamw(wts[n], [gs], mom[n], var[n], f"adamw_{n}")

    return (loss, grad_x, *[grad[n] for n in _WEIGHTS], *[delta[n] for n in _WEIGHTS],
            *[new_m[n] for n in _WEIGHTS], *[new_v[n] for n in _WEIGHTS])
```

```python
import functools

import jax
import jax.numpy as jnp
from jax import lax
from jax.experimental import pallas as pl
from jax.experimental.pallas import tpu as pltpu

f32 = jnp.float32
bf16 = jnp.bfloat16
MESH = pl.DeviceIdType.MESH
ANY = pl.BlockSpec(memory_space=pl.ANY)

D = 1024
N_META = 16
CHUNK = 64
PAD = CHUNK - N_META
SKIP = PAD + N_META
EPS = 1e-6
D_FF = 2816
A_HEADS = 8
A_DK = 128
B_HEADS = 16
B_N = 64
B_GN_EPS = B_N * 1e-5
IN_TOTAL = 9520
ZP = 9600
LANES = 128
N_CHIPS = 4

ADAM_LR, ADAM_B1, ADAM_B2, ADAM_EPS, ADAM_WD, ADAM_STEP = 0.001, 0.9, 0.999, 1e-08, 0.01, 10

MXU_DTYPE = bf16


def _tile(n, cap, mult):
    if n <= cap:
        return n
    best = None
    for t in range(mult, cap + 1, mult):
        if n % t == 0:
            best = t
    assert best is not None, (n, cap, mult)
    return best


def _sigmoid(x):
    return jax.nn.sigmoid(x)


def _silu(x):
    return x * jax.nn.sigmoid(x)


def _softplus(x):
    return jnp.maximum(x, 0.0) + jnp.log(1.0 + jnp.exp(-jnp.abs(x)))


def _head_matrix(c, nh):
    hd = c // nh
    r = lax.broadcasted_iota(jnp.int32, (c, nh), 0)
    h = lax.broadcasted_iota(jnp.int32, (c, nh), 1)
    return (r >= h * hd) & (r < (h + 1) * hd)


def _dot_exact_rhs(x, e, cb):
    dn = (((1,), (cb,)), ((), ()))
    eb = e.astype(bf16)
    hi = x.astype(bf16)
    lo = (x - hi.astype(f32)).astype(bf16)
    return (lax.dot_general(hi, eb, dn, preferred_element_type=f32)
            + lax.dot_general(lo, eb, dn, preferred_element_type=f32))


def _head_sum_impl(x, nh):
    e = _head_matrix(x.shape[-1], nh)
    return _dot_exact_rhs(_dot_exact_rhs(x, e, 0), e, 1)


@functools.partial(jax.custom_vjp, nondiff_argnums=(1,))
def _head_sum(x, nh):
    return _head_sum_impl(x, nh)


def _head_sum_fwd(x, nh):
    return _head_sum_impl(x, nh), None


def _head_sum_bwd(nh, _, g):
    return (_head_sum_impl(g, nh),)


_head_sum.defvjp(_head_sum_fwd, _head_sum_bwd)


@functools.partial(jax.custom_vjp, nondiff_argnums=(1,))
def _shift_rows(x, s):
    n = x.shape[0]
    row = lax.broadcasted_iota(jnp.int32, x.shape, 0)
    if s > 0:
        return jnp.where(row >= s, pltpu.roll(x, s, 0), 0.0)
    return jnp.where(row < n + s, pltpu.roll(x, n + s, 0), 0.0)


def _shift_rows_fwd(x, s):
    return _shift_rows(x, s), None


def _shift_rows_bwd(s, _, g):
    return (_shift_rows(g, -s),)


_shift_rows.defvjp(_shift_rows_fwd, _shift_rows_bwd)


def _matmul(a, b, *, ta=False, tb=False, res=None, scale=1.0, name, b_cols_split=None, out_cols_split=None,
            out_into=None):
    assert not (ta and tb)
    (ar, ac) = a.shape
    b0 = 0
    if b_cols_split:
        b0, bs = b_cols_split
        _, br, bc_part = b.shape
        bc = bs * bc_part
    else:
        br, bc = b.shape
    m, k = (ac, ar) if ta else (ar, ac)
    n, kb = (br, bc) if tb else (bc, br)
    assert k == kb, (a.shape, b.shape, ta, tb)
    tm = _tile(m, 1408, LANES) if ta else _tile(m, 1040, 16)
    tn = _tile(n, 1920, LANES)
    tk = _tile(k, 1040, 8) if ta else _tile(k, 1920, LANES)
    nk = k // tk
    dn = (((0 if ta else 1,), (1 if tb else 0,)), ((), ()))
    if b_cols_split:
        assert (tk if tb else tn) == bc_part, (b.shape, tn, tk)

    def body(*refs):
        a_ref, b_ref = refs[:2]
        r_ref = refs[2] if res is not None else None
        o_ref, acc = refs[-2:]
        kk = pl.program_id(2)

        @pl.when(kk == 0)
        def _():
            acc[...] = jnp.zeros_like(acc)

        acc[...] += lax.dot_general(a_ref[...].astype(MXU_DTYPE), b_ref[...].astype(MXU_DTYPE), dn,
                                    preferred_element_type=f32)

        @pl.when(kk == nk - 1)
        def _():
            out = acc[...]
            if scale != 1.0:
                out = out * scale
            if res is not None:
                out = r_ref[...] + out
            o_ref[...] = out

    if ta:
        a_spec = pl.BlockSpec((tk, tm), lambda i, j, kk: (kk, i))
    else:
        a_spec = pl.BlockSpec((tm, tk), lambda i, j, kk: (i, kk))
    if tb and b_cols_split:
        b_spec = pl.BlockSpec((None, tn, tk), lambda i, j, kk: (kk + b0, j, 0))
    elif tb:
        b_spec = pl.BlockSpec((tn, tk), lambda i, j, kk: (j, kk))
    elif b_cols_split:
        b_spec = pl.BlockSpec((None, tk, tn), lambda i, j, kk: (j + b0, kk, 0))
    else:
        b_spec = pl.BlockSpec((tk, tn), lambda i, j, kk: (kk, j))
    in_specs = [a_spec, b_spec]
    args = [a, b]
    if res is not None:
        in_specs.append(pl.BlockSpec((tm, tn), lambda i, j, kk: (i, j)))
        args.append(res)
    aliases = {}
    if out_cols_split:
        o0, total = out_cols_split
        out_spec = pl.BlockSpec((None, tm, tn), lambda i, j, kk: (j + o0, i, 0))
        out_shape = jax.ShapeDtypeStruct((total, m, tn), f32)
        if out_into is not None:
            assert out_into.shape == out_shape.shape
            in_specs.append(ANY)
            args.append(out_into)
            aliases = {len(args) - 1: 0}
    else:
        out_spec = pl.BlockSpec((tm, tn), lambda i, j, kk: (i, j))
        out_shape = jax.ShapeDtypeStruct((m, n), f32)
    return pl.pallas_call(
        body, name=name, grid=(m // tm, n // tn, nk), in_specs=in_specs, out_specs=out_spec, out_shape=out_shape,
        scratch_shapes=[pltpu.VMEM((tm, tn), f32)], input_output_aliases=aliases,
        compiler_params=pltpu.CompilerParams(dimension_semantics=("parallel", "parallel", "arbitrary")),
    )(*args)


def _tw_fwd(fn, ins, in_specs, out_shapes, out_specs, grid, name, with_pid=False):
    n_in = len(ins)

    def body(*refs):
        vals = [r[...] for r in refs[:n_in]]
        outs = fn(pl.program_id(0), *vals) if with_pid else fn(*vals)
        for r, o in zip(refs[n_in:], outs):
            r[...] = o.astype(r.dtype)

    return pl.pallas_call(body, name=name, grid=grid, in_specs=in_specs, out_specs=out_specs,
                          out_shape=out_shapes)(*ins)


def _tw_bwd(fn, ins, in_specs, cts, ct_specs, kinds, grid, name, with_pid=False, tile_dtype=f32, ct_extra=(),
            residual=None):
    n_in, n_ct = len(ins), len(cts)
    diff = [i for i, kd in enumerate(kinds) if kd is not None]
    n_ex = len(ct_extra)

    def body(*refs):
        vals = [r[...] for r in refs[:n_in]]
        ctv = [r[...].astype(f32) for r in refs[n_in:n_in + n_ct]]
        for (ci, _), r in zip(ct_extra, refs[n_in + n_ct:n_in + n_ct + n_ex]):
            ctv[ci] = ctv[ci] + r[...]
        ctv = tuple(ctv)
        n_fixed = n_in + n_ct + n_ex
        res_ref = refs[n_fixed] if residual is not None else None
        g_refs = refs[n_fixed + (residual is not None):]
        pid = pl.program_id(0)

        def f(*dv):
            full = list(vals)
            for i, v in zip(diff, dv):
                full[i] = v
            out = fn(pid, *full) if with_pid else fn(*full)
            return tuple(out)

        _, vjp = jax.vjp(f, *[vals[i] for i in diff])
        gs = vjp(ctv)
        first = pid == 0
        for i2 in range(1, len(grid)):
            first = first & (pl.program_id(i2) == 0)
        for i, g, g_ref in zip(diff, gs, g_refs):
            if kinds[i] != 'acc':
                if i == 0 and res_ref is not None:
                    g = res_ref[...] + g
                g_ref[...] = g.astype(g_ref.dtype)
            else:
                @pl.when(first)
                def _(g=g, g_ref=g_ref):
                    g_ref[...] = g

                @pl.when(jnp.logical_not(first))
                def _(g=g, g_ref=g_ref):
                    g_ref[...] += g

    zero_map = {1: lambda *a: (0,), 2: lambda *a: (0, 0), 3: lambda *a: (0, 0, 0)}
    out_specs, out_shapes = [], []
    for i in diff:
        if kinds[i] == 'tile':
            out_shapes.append(jax.ShapeDtypeStruct(ins[i].shape, tile_dtype))
            out_specs.append(in_specs[i])
        elif kinds[i] == 'acc':
            out_shapes.append(jax.ShapeDtypeStruct(ins[i].shape, f32))
            out_specs.append(pl.BlockSpec(ins[i].shape, zero_map[ins[i].ndim]))
        else:
            out_shapes.append(jax.ShapeDtypeStruct(kinds[i][1], kinds[i][3] if len(kinds[i]) > 3 else tile_dtype))
            out_specs.append(kinds[i][2])
    extra_specs = [ct_specs[ci] for ci, _ in ct_extra]
    extra = [a for _, a in ct_extra]
    if residual is not None:
        assert kinds[0] == 'tile'
        extra_specs.append(in_specs[0])
        extra.append(residual)
    return pl.pallas_call(body, name=name, grid=grid, in_specs=list(in_specs) + list(ct_specs) + extra_specs,
                          out_specs=out_specs, out_shape=out_shapes)(*ins, *cts, *extra)


def _row_spec(tm, c, col_block=0):
    return pl.BlockSpec((tm, c), lambda i, cb=col_block: (i, cb))


def _full_spec(shape):
    nd = len(shape)
    return pl.BlockSpec(shape, lambda *a, nd=nd: (0,) * nd)


def _f_rms(x, g):
    return (x * lax.rsqrt(jnp.mean(x * x, axis=-1, keepdims=True) + EPS) * g,)


def _f_swiglu(gate, up):
    return (_silu(gate) * up,)


def _f_loss(pid, h, g, tgt, *, tm):
    y = h * lax.rsqrt(jnp.mean(h * h, axis=-1, keepdims=True) + EPS) * g
    row = pid * tm + lax.broadcasted_iota(jnp.int32, (tm, 1), 0)
    err = jnp.where(row >= SKIP, y - tgt, 0.0)
    per_row = jnp.mean(err * err, axis=-1, keepdims=True)
    return (0.5 * jnp.sum(per_row, axis=0, keepdims=True),)


def _f_conv(x, w, *, norm, scale):
    y = x * w[3:4, :]
    for s in (1, 2, 3):
        y = y + _shift_rows(x, s) * w[3 - s:4 - s, :]
    y = _silu(y)
    if norm:
        y = y * lax.rsqrt(jnp.sum(y * y, axis=-1, keepdims=True) + 1e-6) * scale
    return (y,)


def _f_dgates(pid, abeta, aalpha, log_rate, dt_bias, *, tm):
    row = pid * tm + lax.broadcasted_iota(jnp.int32, (tm, 1), 0)
    live = row >= PAD
    beta = jnp.where(live, _sigmoid(abeta), 0.0)
    g = jnp.where(live, -jnp.exp(log_rate) * _softplus(aalpha + dt_bias), 0.0)
    return beta, g


def _f_tshift(z, mu):
    return (z + (_shift_rows(z, 1) - z) * mu,)


def _f_rwkv_pre(k, wd, ad, gd, w0, w_up, a0, a_up, g_up, k_k, k_a):
    w_log = -_softplus(-(w0 + _smm(jnp.tanh(wd), w_up, 1))) - 0.5
    lw = -jnp.exp(w_log)
    a_lr = _sigmoid(a0 + _smm(ad, a_up, 1))
    gate = _smm(_sigmoid(gd), g_up, 1)
    kkp = k * k_k
    kk = kkp * lax.rsqrt(_head_sum(kkp * kkp, B_HEADS) + 1e-6)
    kmod = k * (1.0 + (a_lr - 1.0) * k_a)
    return lw, kmod, -kk, kk * a_lr, gate


def _f_mix_post(o, az, y, r, kmod, v, gate, ga, gb, out_gain, ln_g, ln_b, r_k):
    ms = _head_sum(o * o, A_HEADS) * (1.0 / A_DK)
    oa = o * lax.rsqrt(ms + EPS) * out_gain * _silu(az)
    mean = _head_sum(y, B_HEADS) * (1.0 / B_N)
    yc = y - mean
    var = _head_sum(yc * yc, B_HEADS) * (1.0 / B_N)
    yn = yc * lax.rsqrt(var + B_GN_EPS) * ln_g + ln_b
    bonus = _head_sum(r * kmod * r_k, B_HEADS) * v
    ob = (yn + bonus) * gate
    return (_sigmoid(ga) * oa + _sigmoid(gb) * ob,)


def _split2(a):
    hi = a.astype(bf16)
    return hi, (a - hi.astype(f32)).astype(bf16)


def _dot_passes(a, b, ca, cb, passes):
    dn = (((ca,), (cb,)), ((), ()))
    if passes == 1:
        return lax.dot_general(a.astype(bf16), b.astype(bf16), dn, preferred_element_type=f32)
    ah, al = _split2(a)
    bh, bl = _split2(b)
    return (lax.dot_general(ah, bh, dn, preferred_element_type=f32)
            + (lax.dot_general(ah, bl, dn, preferred_element_type=f32)
               + lax.dot_general(al, bh, dn, preferred_element_type=f32)))


@functools.partial(jax.custom_vjp, nondiff_argnums=(2, 3, 4))
def _sdot(a, b, ca, cb, passes):
    return _dot_passes(a, b, ca, cb, passes)


def _sdot_fwd(a, b, ca, cb, passes):
    return _dot_passes(a, b, ca, cb, passes), (a, b)


def _sdot_bwd(ca, cb, passes, res, g):
    a, b = res
    if (ca, cb) == (1, 0):
        return _dot_passes(g, b, 1, 1, passes), _dot_passes(a, g, 0, 0, passes)
    if (ca, cb) == (1, 1):
        return _dot_passes(g, b, 1, 0, passes), _dot_passes(g, a, 0, 0, passes)
    assert (ca, cb) == (0, 0)
    return _dot_passes(b, g, 1, 1, passes), _dot_passes(a, g, 1, 0, passes)


_sdot.defvjp(_sdot_fwd, _sdot_bwd)


def _smm(a, b, passes=3):
    return _sdot(a, b, 1, 0, passes)


def _smm_nt(a, b, passes=3):
    return _sdot(a, b, 1, 1, passes)


def _smm_tn(a, b, passes=3):
    return _sdot(a, b, 0, 0, passes)


def _tri_dot(x, ca):
    n = x.shape[0]
    incl = _tri_masks(n)[0]
    dn = (((ca,), (0,)), ((), ()))
    tri = incl.astype(bf16)
    hi, r1 = x.astype(bf16), None
    r1 = x - hi.astype(f32)
    mid = r1.astype(bf16)
    lo = (r1 - mid.astype(f32)).astype(bf16)
    return (lax.dot_general(tri, hi, dn, preferred_element_type=f32)
            + (lax.dot_general(tri, mid, dn, preferred_element_type=f32)
               + lax.dot_general(tri, lo, dn, preferred_element_type=f32)))


@jax.custom_vjp
def _cumsum_rows(x):
    return _tri_dot(x, 1)


def _cumsum_rows_fwd(x):
    return _tri_dot(x, 1), None


def _cumsum_rows_bwd(_, g):
    return (_tri_dot(g, 0),)


_cumsum_rows.defvjp(_cumsum_rows_fwd, _cumsum_rows_bwd)


def _tri_masks(n):
    i = lax.broadcasted_iota(jnp.int32, (n, n), 0)
    j = lax.broadcasted_iota(jnp.int32, (n, n), 1)
    return i >= j, i > j, i == j, i <= j


def _unit_lower_inv_impl(low, passes):
    n = low.shape[0]
    assert n == CHUNK
    _, _, eye, _ = _tri_masks(n)
    acc = eye.astype(f32) + low
    p = low
    for _ in range(5):
        p = _dot_passes(p, p, 1, 0, passes)
        acc = acc + _dot_passes(acc, p, 1, 0, passes)
    return acc


@functools.partial(jax.custom_vjp, nondiff_argnums=(1,))
def _unit_lower_inv(low, passes=3):
    return _unit_lower_inv_impl(low, passes)


def _unit_lower_inv_fwd(low, passes):
    t = _unit_lower_inv_impl(low, passes)
    return t, t


def _unit_lower_inv_bwd(passes, t, g):
    return (_dot_passes(_dot_passes(t, g, 0, 0, passes), t, 1, 1, passes),)


_unit_lower_inv.defvjp(_unit_lower_inv_fwd, _unit_lower_inv_bwd)


@functools.partial(jax.custom_vjp, nondiff_argnums=(2,))
def _unit_lower_inv_saved(low, t_saved, passes):
    return t_saved


def _unit_lower_inv_saved_fwd(low, t_saved, passes):
    return t_saved, t_saved


def _unit_lower_inv_saved_bwd(passes, t, g):
    return _unit_lower_inv_bwd(passes, t, g) + (jnp.zeros_like(t),)


_unit_lower_inv_saved.defvjp(_unit_lower_inv_saved_fwd, _unit_lower_inv_saved_bwd)


def _inverse(low, passes, saved):
    return _unit_lower_inv(low, passes) if saved is None else _unit_lower_inv_saved(low, saved, passes)

DELTA_PASSES = 1
DELTA_INV_PASSES = 1


def _delta_chunk(s, q, k, v, beta_row, g_row, inv_saved=None):
    p = DELTA_PASSES
    incl, strict, eye, upper = _tri_masks(CHUNK)
    beta = jnp.sum(jnp.where(eye, beta_row, 0.0), axis=1, keepdims=True)
    g = jnp.sum(jnp.where(eye, g_row, 0.0), axis=1, keepdims=True)
    gc = jnp.sum(jnp.where(incl, g_row, 0.0), axis=1, keepdims=True)
    gc_row = jnp.sum(jnp.where(upper, g, 0.0), axis=0, keepdims=True)
    decay = jnp.where(incl, jnp.exp(jnp.where(incl, gc - gc_row, 0.0)), 0.0)
    kb = k * beta
    vb = v * beta
    m = jnp.where(strict, _smm_nt(kb, k, p) * decay, 0.0)
    tinv = _inverse(-m, DELTA_INV_PASSES, inv_saved)
    u = _smm(tinv, vb, p)
    wk = _smm(tinv, kb * jnp.exp(gc), p)
    attn = _smm_nt(q, k, p) * decay
    qg = q * jnp.exp(gc)
    g_last = jnp.sum(g, axis=0, keepdims=True)
    k_tail = k * jnp.exp(g_last - gc)
    v_new = u - _smm(wk, s, p)
    o = _smm(qg, s, p) + _smm(attn, v_new, p)
    s_new = s * jnp.exp(g_last) + _smm_tn(k_tail, v_new, p)
    return o, s_new, tinv


RWKV_PASSES = 1
RWKV_INV_PASSES = 1


def _rwkv_chunk(st, r, k, v, a, b, lw, inv_saved=None):
    c = CHUNK
    p, pi = RWKV_PASSES, RWKV_INV_PASSES
    _, strict, _, _ = _tri_masks(c)
    lane = lax.broadcasted_iota(jnp.int32, (c, 2 * B_N), 1)
    row = lax.broadcasted_iota(jnp.int32, (c, 2 * B_N), 0)
    first = lane < B_N
    incl2 = row >= jnp.where(first, lane, lane - B_N)
    bi = lax.broadcasted_iota(jnp.int32, (2 * B_N, 2 * B_N), 0) < B_N
    bj = lax.broadcasted_iota(jnp.int32, (2 * B_N, 2 * B_N), 1) < B_N
    blockdiag = bi == bj
    cum = _cumsum_rows(lw)
    e_pos = jnp.exp(cum)
    e_neg = jnp.exp(-cum)
    rt = r * e_pos
    at = a * jnp.exp(cum - lw)
    kt = k * e_neg
    bt = b * e_neg
    bk = jnp.concatenate([bt, kt], axis=0)
    a_s0 = _smm_nt(at, st, p)
    r_s0 = _smm_nt(rt, st, p)
    heads = (first, jnp.logical_not(first))
    u = jnp.zeros((c, 2 * B_N), f32)
    invs = []
    for hi, sel in enumerate(heads):
        at_h = jnp.where(sel, at, 0.0)
        ab = jnp.where(strict, _smm_nt(at_h, bt, pi), 0.0)
        ak = jnp.where(strict, _smm_nt(at_h, kt, p), 0.0)
        t_h = _inverse(ab, pi, None if inv_saved is None else inv_saved[hi])
        invs.append(t_h)
        u = u + _smm(t_h, jnp.where(sel, a_s0, 0.0) + _smm(ak, jnp.where(sel, v, 0.0), p), p)
    y = r_s0
    for sel in heads:
        rbk = jnp.where(incl2, _smm_nt(jnp.where(sel, rt, 0.0), bk, p), 0.0)
        uv = jnp.concatenate([jnp.where(sel, u, 0.0), jnp.where(sel, v, 0.0)], axis=0)
        y = y + _smm(rbk, uv, p)
    cl = jnp.sum(lw, axis=0, keepdims=True)
    dec = jnp.exp(cl - cum)
    uv_all = jnp.concatenate([u, v], axis=0)
    bk_dec = jnp.concatenate([b * dec, k * dec], axis=0)
    st_new = st * jnp.exp(cl) + jnp.where(blockdiag, _smm_tn(uv_all, bk_dec, p), 0.0)
    return y, st_new, jnp.stack(invs)


GROUPS_PER_STEP = 8


def _scan_specs(ins, col_offs, n_chunks, reverse):
    gw = GROUPS_PER_STEP * LANES
    cidx = (lambda c: n_chunks - 1 - c) if reverse else (lambda c: c)
    specs = []
    for a, off in zip(ins, col_offs):
        if a.ndim == 2:
            assert off % gw == 0
            specs.append(pl.BlockSpec((CHUNK, gw), lambda h, c, o=off // gw: (cidx(c), h + o)))
        else:
            specs.append(pl.BlockSpec((GROUPS_PER_STEP, None, 1, CHUNK), lambda h, c: (h, cidx(c), 0, 0)))
    return specs, cidx


def _group_vals(refs, g):
    return [r[:, g * LANES:(g + 1) * LANES] if len(r.shape) == 2 else r[g] for r in refs]


def _scan_fwd(chunk_fn, ins, col_offs, n_groups, n_chunks, state_shape, inv_shape, name):
    n_in = len(ins)
    gps = GROUPS_PER_STEP
    t = ins[0].shape[0]

    def body(*refs):
        in_refs = refs[:n_in]
        o_ref, s0_ref, inv_ref, st = refs[n_in:]

        @pl.when(pl.program_id(1) == 0)
        def _():
            st[...] = jnp.zeros_like(st)

        states = st[...]
        vals = [jnp.stack(col) for col in zip(*[_group_vals(in_refs, g) for g in range(gps)])]
        o, s_new, inv = jax.vmap(chunk_fn)(states, *vals)
        s0_ref[...] = states
        inv_ref[...] = inv
        st[...] = s_new
        for g in range(gps):
            o_ref[:, g * LANES:(g + 1) * LANES] = o[g]

    specs, _ = _scan_specs(ins, col_offs, n_chunks, False)
    zeros_i = (0,) * len(inv_shape)
    return pl.pallas_call(
        body, name=name, grid=(n_groups // gps, n_chunks), in_specs=specs,
        out_specs=[pl.BlockSpec((CHUNK, gps * LANES), lambda h, c: (c, h)),
                   pl.BlockSpec((gps, None) + state_shape, lambda h, c: (h, c, 0, 0)),
                   pl.BlockSpec((gps, None) + inv_shape, lambda h, c: (h, c) + zeros_i)],
        out_shape=[jax.ShapeDtypeStruct((t, n_groups * LANES), f32),
                   jax.ShapeDtypeStruct((n_groups, n_chunks) + state_shape, f32),
                   jax.ShapeDtypeStruct((n_groups, n_chunks) + inv_shape, f32)],
        scratch_shapes=[pltpu.VMEM((gps,) + state_shape, f32)],
        compiler_params=pltpu.CompilerParams(dimension_semantics=("parallel", "arbitrary")),
    )(*ins)


def _scan_bwd(chunk_fn, s0s, invs, ins, col_offs, d_out, n_groups, n_chunks, state_shape, name):
    n_in = len(ins)
    gps = GROUPS_PER_STEP
    t = d_out.shape[0]
    inv_shape = invs.shape[2:]

    def body(*refs):
        s0_ref, inv_ref = refs[:2]
        in_refs = refs[2:2 + n_in]
        do_ref = refs[2 + n_in]
        g_refs = refs[3 + n_in:3 + 2 * n_in]
        dst = refs[3 + 2 * n_in]

        @pl.when(pl.program_id(1) == 0)
        def _():
            dst[...] = jnp.zeros_like(dst)

        vals = [jnp.stack(col) for col in zip(*[_group_vals(in_refs, g) for g in range(gps)])]
        d_o = jnp.stack([do_ref[:, g * LANES:(g + 1) * LANES] for g in range(gps)])
        inv = inv_ref[...]

        def with_saved(s, *a):
            return jax.vmap(lambda ss, ii, *aa: chunk_fn(ss, *aa, inv_saved=ii)[:2])(s, inv, *a)

        _, vjp = jax.vjp(with_saved, s0_ref[...], *vals)
        gs = vjp((d_o, dst[...]))
        dst[...] = gs[0]
        for g_ref, gv in zip(g_refs, gs[1:]):
            if len(g_ref.shape) == 2:
                for g in range(gps):
                    g_ref[:, g * LANES:(g + 1) * LANES] = gv[g]
            else:
                g_ref[...] = gv

    specs, cidx = _scan_specs(ins, col_offs, n_chunks, True)
    out_lane = pl.BlockSpec((CHUNK, gps * LANES), lambda h, c: (cidx(c), h))
    g_specs = [out_lane if a.ndim == 2 else sp for a, sp in zip(ins, specs)]
    g_shapes = [(t, n_groups * LANES) if a.ndim == 2 else a.shape for a in ins]
    s0_spec = pl.BlockSpec((gps, None) + state_shape, lambda h, c: (h, cidx(c), 0, 0))
    zeros_i = (0,) * len(inv_shape)
    inv_spec = pl.BlockSpec((gps, None) + inv_shape, lambda h, c: (h, cidx(c)) + zeros_i)
    return pl.pallas_call(
        body, name=name, grid=(n_groups // gps, n_chunks), in_specs=[s0_spec, inv_spec] + specs + [out_lane],
        out_specs=g_specs, out_shape=[jax.ShapeDtypeStruct(sh, f32) for sh in g_shapes],
        scratch_shapes=[pltpu.VMEM((gps,) + state_shape, f32)],
        compiler_params=pltpu.CompilerParams(dimension_semantics=("parallel", "arbitrary")),
    )(s0s, invs, *ins, d_out)


def _rms_fwd(x, g, name):
    t = x.shape[0]
    tm = _tile(t, 416, 16)
    return _tw_fwd(_f_rms, [x, g], [_row_spec(tm, D), _full_spec(g.shape)],
                   [jax.ShapeDtypeStruct(x.shape, MXU_DTYPE)], [_row_spec(tm, D)], (t // tm,), name)[0]


def _rms_bwd(x, g, dy, residual, name):
    t = x.shape[0]
    tm = _tile(t, 416, 8)
    return _tw_bwd(_f_rms, [x, g], [_row_spec(tm, D), _full_spec(g.shape)], [dy], [_row_spec(tm, D)],
                   ['tile', 'acc'], (t // tm,), name, residual=residual)


def _ffn_fwd(h, gain, wgu, wd, tag):
    xn = _rms_fwd(h, gain, f"{tag}_rms")
    gate, up, act = _gate_up_act(xn, wgu, f"{tag}_gate_up")
    out = _matmul(act, wd, res=h, scale=0.5, name=f"{tag}_down")
    return out, (xn, gate, up, act)


def _mxu_dot(a, b, dn):
    return lax.dot_general(a.astype(MXU_DTYPE), b.astype(MXU_DTYPE), dn, preferred_element_type=f32)


def _gate_up_act(xn, wgu, name):
    t = xn.shape[0]
    wdt = wgu.shape[2]
    tm = _tile(t, 416, 16)
    dn = (((1,), (0,)), ((), ()))

    def body(x_ref, wg_ref, wu_ref, g_ref, u_ref, a_ref):
        x = x_ref[...]
        g = _mxu_dot(x, wg_ref[...], dn)
        u = _mxu_dot(x, wu_ref[...], dn)
        g_ref[...] = g
        u_ref[...] = u
        a_ref[...] = _f_swiglu(g, u)[0].astype(a_ref.dtype)

    out_spec = pl.BlockSpec((tm, wdt), lambda j, i: (i, j))
    return pl.pallas_call(
        body, name=name, grid=(2, t // tm),
        in_specs=[pl.BlockSpec((tm, D), lambda j, i: (i, 0)), pl.BlockSpec((None, D, wdt), lambda j, i: (j, 0, 0)),
                  pl.BlockSpec((None, D, wdt), lambda j, i: (j + 2, 0, 0))],
        out_specs=[out_spec] * 3,
        out_shape=[jax.ShapeDtypeStruct((t, 2 * wdt), f32)] * 2 + [jax.ShapeDtypeStruct((t, 2 * wdt), MXU_DTYPE)],
        compiler_params=pltpu.CompilerParams(dimension_semantics=("parallel", "parallel")),
    )(xn, wgu, wgu)


def _d_gate_up(dout, wd, gate, up, name):
    t = dout.shape[0]
    wdt = D_FF // 2
    tm = _tile(t, 416, 16)
    dn = (((1,), (1,)), ((), ()))

    def body(do_ref, wd_ref, g_ref, u_ref, dg_ref, du_ref):
        d_act = 0.5 * _mxu_dot(do_ref[...], wd_ref[...], dn)
        _, vjp = jax.vjp(_f_swiglu, g_ref[...], u_ref[...])
        dg, du = vjp((d_act,))
        dg_ref[...] = dg.astype(dg_ref.dtype)
        du_ref[...] = du.astype(du_ref.dtype)

    spec = pl.BlockSpec((tm, wdt), lambda j, i: (i, j))
    return pl.pallas_call(
        body, name=name, grid=(2, t // tm),
        in_specs=[pl.BlockSpec((tm, D), lambda j, i: (i, 0)), pl.BlockSpec((wdt, D), lambda j, i: (j, 0)), spec, spec],
        out_specs=[spec] * 2, out_shape=[jax.ShapeDtypeStruct((t, D_FF), MXU_DTYPE)] * 2,
        compiler_params=pltpu.CompilerParams(dimension_semantics=("parallel", "parallel")),
    )(dout, wd, gate, up)


def _ffn_bwd(h, gain, wgu, wd, saved, dout, tag):
    xn, gate, up, act = saved
    t = h.shape[0]
    d_wd = _matmul(act, dout, ta=True, scale=0.5, name=f"{tag}_dwd")
    d_gate, d_up = _d_gate_up(dout, wd, gate, up, f"{tag}_dact")
    d_wgu = _matmul(xn, d_gate, ta=True, out_cols_split=(0, N_CHIPS), name=f"{tag}_dwg")
    d_wgu = _matmul(xn, d_up, ta=True, out_cols_split=(2, N_CHIPS), out_into=d_wgu, name=f"{tag}_dwu")
    d_xn = _matmul(d_gate, wgu, tb=True, b_cols_split=(0, 2), name=f"{tag}_dxn_g")
    d_xn = _matmul(d_up, wgu, tb=True, b_cols_split=(2, 2), res=d_xn, name=f"{tag}_dxn_u")
    d_h, d_gain = _rms_bwd(h, gain, d_xn, dout, f"{tag}_drms")
    return d_h, d_gain, d_wgu, d_wd


def _col_spec(t, first_block):
    return pl.BlockSpec((t, LANES), lambda j, fb=first_block: (0, j + fb))


def _local_step(h0, tgt, w):
    t = h0.shape[0]
    assert t % CHUNK == 0
    nc = t // CHUNK
    grads = {}

    h1, ffn1_saved = _ffn_fwd(h0, w['ffn1_norm'], w['ffn1_wgu'], w['ffn1_wd'], "ffn1")
    u = _rms_fwd(h1, w['mix_norm'], "mix_rms")
    z = _matmul(u, w['w_in_p'], name="in_proj")
    zs = z[:, 9216:9216 + 304]
    abeta, aalpha = zs[:, 288:296], zs[:, 296:304]

    conv_w = w['a_conv_w']
    conv_fns = [functools.partial(_f_conv, norm=True, scale=A_DK ** -0.5),
                functools.partial(_f_conv, norm=True, scale=1.0),
                functools.partial(_f_conv, norm=False, scale=1.0)]
    qkv = []
    for idx, fn in enumerate(conv_fns):
        qkv.append(_tw_fwd(fn, [z, conv_w], [_col_spec(t, 8 * idx), pl.BlockSpec((4, LANES), lambda j, o=8 * idx: (0, j + o))],
                           [jax.ShapeDtypeStruct((t, D), f32)], [_col_spec(t, 0)], (A_HEADS,), f"a_conv{idx}")[0])
    aq, ak, av = qkv
    tmg = _tile(t, 1040, 8)
    dg_fn = functools.partial(_f_dgates, tm=tmg)
    dg_specs = [_row_spec(tmg, A_HEADS)] * 2 + [_full_spec((1, A_HEADS))] * 2
    beta, gdec = _tw_fwd(dg_fn, [abeta, aalpha, w['a_log_rate'], w['a_dt_bias']], dg_specs,
                         [jax.ShapeDtypeStruct((t, A_HEADS), f32)] * 2, [_row_spec(tmg, A_HEADS)] * 2, (t // tmg,),
                         "a_gates", with_pid=True)
    beta_h = beta.T.reshape(A_HEADS, nc, 1, CHUNK)
    gdec_h = gdec.T.reshape(A_HEADS, nc, 1, CHUNK)
    a_ins = [aq, ak, av, beta_h, gdec_h]
    a_offs = [0] * 5
    o_scan, a_s0, a_inv = _scan_fwd(_delta_chunk, a_ins, a_offs, A_HEADS, nc, (A_DK, A_DK), (CHUNK, CHUNK), "a_scan")

    mu = w['b_shift_mu']
    mu_rkv, mu_s = mu[:, :3072], mu[:, 3072:]
    zf_rkv = _tw_fwd(_f_tshift, [z, mu_rkv], [_col_spec(t, 32), pl.BlockSpec((1, LANES), lambda j: (0, j))],
                     [jax.ShapeDtypeStruct((t, 3072), f32)], [_col_spec(t, 0)], (24,), "b_shift")[0]
    zs_b = zs[:, :288]
    zf_s = _tw_fwd(_f_tshift, [zs_b, mu_s], [_full_spec((t, 288)), _full_spec((1, 288))],
                   [jax.ShapeDtypeStruct((t, 288), f32)], [_full_spec((t, 288))], (1,), "b_shift_s")[0]
    wdf, adf, gdf = zf_s[:, 0:64], zf_s[:, 64:128], zf_s[:, 128:288]
    tmr = _tile(t, 208, 16)
    pre_params = [w['b_w0'], w['b_w_up'], w['b_a0'], w['b_a_up'], w['b_g_up'], w['b_k_k'], w['b_k_a']]
    pre_ins = [zf_rkv, wdf, adf, gdf] + pre_params
    pre_specs = ([_row_spec(tmr, D, 1), _row_spec(tmr, 64), _row_spec(tmr, 64), _row_spec(tmr, 160)]
                 + [_full_spec(p.shape) for p in pre_params])
    lw, kmod, a_s, b_s, bgate = _tw_fwd(_f_rwkv_pre, pre_ins, pre_specs, [jax.ShapeDtypeStruct((t, D), f32)] * 5,
                                        [_row_spec(tmr, D)] * 5, (t // tmr,), "b_pre")
    b_ins = [zf_rkv, kmod, zf_rkv, a_s, b_s, lw]
    b_offs = [0, 0, 2 * D, 0, 0, 0]
    y_scan, b_s0, b_inv = _scan_fwd(_rwkv_chunk, b_ins, b_offs, B_HEADS // 2, nc, (2 * B_N, 2 * B_N),
                                    (2, CHUNK, CHUNK), "b_scan")

    out_gain_t = jnp.tile(w['a_out_norm'], (1, A_HEADS))
    r_k = w['b_r_k'].reshape(1, D)
    post_params = [out_gain_t, w['b_ln_gain'], w['b_ln_bias'], r_k]
    post_ins = [o_scan, z, y_scan, zf_rkv, kmod, zf_rkv, bgate, z, z] + post_params
    post_specs = ([_row_spec(tmr, D), _row_spec(tmr, D, 3), _row_spec(tmr, D), _row_spec(tmr, D, 0), _row_spec(tmr, D),
                   _row_spec(tmr, D, 2), _row_spec(tmr, D), _row_spec(tmr, D, 7), _row_spec(tmr, D, 8)]
                  + [_full_spec((1, D))] * 4)
    merged = _tw_fwd(_f_mix_post, post_ins, post_specs, [jax.ShapeDtypeStruct((t, D), MXU_DTYPE)],
                     [_row_spec(tmr, D)], (t // tmr,), "mix_post")[0]
    h2 = _matmul(merged, w['w_out'], res=h1, name="out_proj")
    h3, ffn2_saved = _ffn_fwd(h2, w['ffn2_norm'], w['ffn2_wgu'], w['ffn2_wd'], "ffn2")

    tml = _tile(t, 416, 8)
    fnorm = w['final_norm']
    loss_fn = functools.partial(_f_loss, tm=tml)
    loss_specs = [_row_spec(tml, D), _full_spec((1, D)), _row_spec(tml, D)]
    loss_parts, d_h3, grads['final_norm'] = _loss_and_grad(loss_fn, h3, fnorm, tgt, loss_specs, tml)
    loss = jnp.sum(loss_parts)

    d_h2, grads['ffn2_norm'], grads['ffn2_wgu'], grads['ffn2_wd'] = _ffn_bwd(
        h2, w['ffn2_norm'], w['ffn2_wgu'], w['ffn2_wd'], ffn2_saved, d_h3, "ffn2")
    grads['w_out'] = _matmul(merged, d_h2, ta=True, name="d_w_out")
    d_merged = _matmul(d_h2, w['w_out'], tb=True, name="d_merged")

    win = ('tile', (t, D), _row_spec(tmr, D))
    zwin = win + (MXU_DTYPE,)
    post_kinds = ['tile', zwin, 'tile', win, 'tile', win, 'tile', zwin, zwin] + ['acc'] * 4
    (d_o, d_az, d_y, d_r1, d_kmod1, d_v1, d_bgate, d_ga, d_gb,
     d_out_gain_t, grads['b_ln_gain'], grads['b_ln_bias'], d_r_k) = _tw_bwd(
        _f_mix_post, post_ins, post_specs, [d_merged], [_row_spec(tmr, D)], post_kinds, (t // tmr,), "mix_post_bwd")
    grads['a_out_norm'] = jnp.sum(d_out_gain_t.reshape(A_HEADS, A_DK), axis=0, keepdims=True)
    grads['b_r_k'] = d_r_k.reshape(1, B_HEADS, B_N)

    d_r2, d_kmod2, d_v2, d_as, d_bs, d_lw = _scan_bwd(_rwkv_chunk, b_s0, b_inv, b_ins, b_offs, d_y, B_HEADS // 2, nc,
                                                      (2 * B_N, 2 * B_N), "b_scan_bwd")
    pre_kinds = [win] + ['tile'] * 3 + ['acc'] * 7
    pre_ct_specs = [_row_spec(tmr, D)] * 5
    (d_zf_k, d_wdf, d_adf, d_gdf, grads['b_w0'], grads['b_w_up'], grads['b_a0'], grads['b_a_up'], grads['b_g_up'],
     grads['b_k_k'], grads['b_k_a']) = _tw_bwd(
        _f_rwkv_pre, pre_ins, pre_specs, [d_lw, d_kmod1, d_as, d_bs, d_bgate], pre_ct_specs, pre_kinds, (t // tmr,),
        "b_pre_bwd", ct_extra=[(1, d_kmod2)])
    d_zb_rkv, d_mu_rkv = _shift_bwd3(z, mu_rkv, d_r1, d_r2, d_zf_k, d_v1, d_v2, t)
    d_zf_s = jnp.concatenate([d_wdf, d_adf, d_gdf], axis=1)
    d_zs_b, d_mu_s = _tw_bwd(_f_tshift, [zs_b, mu_s], [_full_spec((t, 288)), _full_spec((1, 288))], [d_zf_s],
                             [_full_spec((t, 288))], ['tile', 'tile'], (1,), "b_shift_s_bwd")
    grads['b_shift_mu'] = jnp.concatenate([d_mu_rkv, d_mu_s], axis=1)

    d_aq, d_ak, d_av, d_beta_h, d_g_h = _scan_bwd(_delta_chunk, a_s0, a_inv, a_ins, a_offs, d_o, A_HEADS, nc,
                                                  (A_DK, A_DK), "a_scan_bwd")
    d_beta = d_beta_h.reshape(A_HEADS, t).T
    d_gdec = d_g_h.reshape(A_HEADS, t).T
    d_abeta, d_aalpha, grads['a_log_rate'], grads['a_dt_bias'] = _tw_bwd(
        dg_fn, [abeta, aalpha, w['a_log_rate'], w['a_dt_bias']], dg_specs, [d_beta, d_gdec],
        [_row_spec(tmg, A_HEADS)] * 2, ['tile', 'tile', 'acc', 'acc'], (t // tmg,), "a_gates_bwd", with_pid=True)
    d_zqkv, d_conv = [], []
    for idx, (fn, ct) in enumerate(zip(conv_fns, (d_aq, d_ak, d_av))):
        dz_i, dw_i = _conv_bwd(fn, z, conv_w, ct, idx, t)
        d_zqkv.append(dz_i)
        d_conv.append(dw_i)
    grads['a_conv_w'] = jnp.concatenate(d_conv, axis=1)

    d_small = jnp.concatenate([d_zs_b, d_abeta, d_aalpha, jnp.zeros((t, ZP - 9216 - 304), f32)], axis=1)
    d_small = lax.optimization_barrier(d_small.astype(MXU_DTYPE))
    d_z_parts = d_zqkv + [d_az, d_zb_rkv, d_ga, d_gb, d_small]
    d_z = jnp.concatenate([p.astype(MXU_DTYPE) for p in d_z_parts], axis=1)
    grads['w_in_p'] = _matmul(u, d_z, ta=True, name="d_w_in")
    d_u = _matmul(d_z, w['w_in_p'], tb=True, name="d_u")
    d_h1, grads['mix_norm'] = _rms_bwd(h1, w['mix_norm'], d_u, d_h2, "mix_drms")
    d_h0, grads['ffn1_norm'], grads['ffn1_wgu'], grads['ffn1_wd'] = _ffn_bwd(
        h0, w['ffn1_norm'], w['ffn1_wgu'], w['ffn1_wd'], ffn1_saved, d_h1, "ffn1")
    return loss, d_h0, grads


_WIN_SEGMENTS = ((0, 4096), (4112, 7184), (7472, 9520), (7184, 7472), (4096, 4112))


_WIN_SHARD = IN_TOTAL // N_CHIPS


def _win_pieces():
    pieces, pad_at = [], 0
    for a, b in _WIN_SEGMENTS:
        c = a
        while c < b:
            stop = min(b, (c // _WIN_SHARD + 1) * _WIN_SHARD)
            pieces.append((c, pad_at + c - a, stop - c))
            c = stop
        pad_at += b - a
    return pieces


def _win_shards_to_padded(shards):
    parts = [shards[c // _WIN_SHARD][:, c % _WIN_SHARD:c % _WIN_SHARD + n] for c, _, n in _win_pieces()]
    parts.append(jnp.zeros((shards.shape[1], ZP - IN_TOTAL), shards.dtype))
    return jnp.concatenate(parts, axis=1)


def _win_padded_to_shards(w_p):
    by_shard = [[] for _ in range(N_CHIPS)]
    for c, p, n in sorted(_win_pieces()):
        by_shard[c // _WIN_SHARD].append(w_p[:, p:p + n])
    return jnp.stack([jnp.concatenate(parts, axis=1) for parts in by_shard])


def _loss_and_grad(loss_fn, h, gain, tgt, specs, tm):
    t = h.shape[0]
    n = t // tm

    def body(h_ref, g_ref, t_ref, l_ref, dh_ref, dg_ref):
        pid = pl.program_id(0)
        tg = t_ref[...]
        (part,), vjp = jax.vjp(lambda a, b: loss_fn(pid, a, b, tg), h_ref[...], g_ref[...])
        dh, dg = vjp((jnp.ones_like(part),))
        l_ref[...] = part
        dh_ref[...] = dh

        @pl.when(pid == 0)
        def _():
            dg_ref[...] = dg

        @pl.when(pid != 0)
        def _():
            dg_ref[...] += dg

    return pl.pallas_call(
        body, name="loss", grid=(n,), in_specs=specs,
        out_specs=[pl.BlockSpec((None, 1, 1), lambda i: (i, 0, 0)), specs[0], _full_spec(gain.shape)],
        out_shape=[jax.ShapeDtypeStruct((n, 1, 1), f32), jax.ShapeDtypeStruct(h.shape, f32),
                   jax.ShapeDtypeStruct(gain.shape, f32)],
    )(h, gain, tgt)


def _shift_bwd3(z, mu, d_r1, d_r2, d_k, d_v1, d_v2, t):
    nb = D // LANES

    def body(z_ref, mu_ref, r1, r2, kk, v1, v2, dz_ref, dmu_ref):
        j = pl.program_id(0)
        ct = jnp.where(j < nb, r1[...] + r2[...], jnp.where(j < 2 * nb, kk[...], v1[...] + v2[...]))
        _, vjp = jax.vjp(lambda a, b: _f_tshift(a, b), z_ref[...], mu_ref[...])
        dz, dmu = vjp((ct,))
        dz_ref[...] = dz.astype(dz_ref.dtype)
        dmu_ref[...] = dmu

    def window(first):
        return pl.BlockSpec((t, LANES), lambda j, f=first: (0, jnp.clip(j - f * nb, 0, nb - 1)))

    return pl.pallas_call(
        body, name="b_shift_bwd", grid=(3 * nb,),
        in_specs=[_col_spec(t, 32), pl.BlockSpec((1, LANES), lambda j: (0, j)), window(0), window(0), window(1),
                  window(2), window(2)],
        out_specs=[_col_spec(t, 0), pl.BlockSpec((1, LANES), lambda j: (0, j))],
        out_shape=[jax.ShapeDtypeStruct((t, 3 * D), MXU_DTYPE), jax.ShapeDtypeStruct((1, 3 * D), f32)],
    )(z, mu, d_r1, d_r2, d_k, d_v1, d_v2)


def _conv_bwd(fn, z, conv_w, ct, idx, t):
    def body(z_ref, w_ref, ct_ref, dz_ref, dw_ref):
        _, vjp = jax.vjp(lambda a, b: fn(a, b), z_ref[...], w_ref[...])
        dz, dw = vjp((ct_ref[...],))
        dz_ref[...] = dz.astype(dz_ref.dtype)
        dw_ref[...] = dw

    return pl.pallas_call(
        body, name=f"a_conv{idx}_bwd", grid=(A_HEADS,),
        in_specs=[_col_spec(t, 8 * idx), pl.BlockSpec((4, LANES), lambda j, o=8 * idx: (0, j + o)), _col_spec(t, 0)],
        out_specs=[_col_spec(t, 0), pl.BlockSpec((4, LANES), lambda j: (0, j))],
        out_shape=[jax.ShapeDtypeStruct((t, D), MXU_DTYPE), jax.ShapeDtypeStruct((4, D), f32)],
    )(z, conv_w, ct)


def _position():
    return lax.axis_index("x"), lax.axis_index("y"), lax.axis_index("c")


def _flip(v, f):
    return 1 - v if f else v


_CHIP_FLIPS = ((1, 0), (0, 1), (1, 1))


def _gather_chips(arrs, name):
    n = len(arrs)
    assert all(a.shape[0] % 32 == 0 for a in arrs)
    arrs = [a.reshape(2, a.shape[0] // 2, a.shape[1]) for a in arrs]

    def body(*refs):
        ins, outs = refs[:n], refs[n:2 * n]
        send, recv, fsend, frecv, own = refs[2 * n:]
        x, y, c = _position()
        me = 2 * x + y
        sends, plan, owns = [], [], []
        for a in range(n):
            cp = pltpu.make_async_remote_copy(src_ref=ins[a], dst_ref=outs[a].at[me], send_sem=own.at[a, 0],
                                              recv_sem=own.at[a, 1], device_id=(x, y, 1 - c), device_id_type=MESH)
            cp.start()
            owns.append(cp)
            for j, (fx, fy) in enumerate(_CHIP_FLIPS):
                px, py = _flip(x, fx), _flip(y, fy)
                p = 2 * px + py
                cp = pltpu.make_async_remote_copy(src_ref=ins[a].at[c], dst_ref=outs[a].at[me, c],
                                                  send_sem=send.at[a, j], recv_sem=recv.at[a, j],
                                                  device_id=(px, py, c), device_id_type=MESH)
                cp.start()
                sends.append(cp)
                landed = pltpu.make_async_remote_copy(src_ref=ins[a].at[c], dst_ref=outs[a].at[p, c],
                                                      send_sem=send.at[a, j], recv_sem=recv.at[a, j],
                                                      device_id=(px, py, c), device_id_type=MESH)
                onward = pltpu.make_async_remote_copy(src_ref=outs[a].at[p, c], dst_ref=outs[a].at[p, c],
                                                      send_sem=fsend.at[a, j], recv_sem=frecv.at[a, j],
                                                      device_id=(x, y, 1 - c), device_id_type=MESH)
                from_sibling = pltpu.make_async_remote_copy(src_ref=outs[a].at[p, 1 - c], dst_ref=outs[a].at[p, 1 - c],
                                                            send_sem=fsend.at[a, j], recv_sem=frecv.at[a, j],
                                                            device_id=(x, y, 1 - c), device_id_type=MESH)
                plan.append((landed, onward, from_sibling))
        for landed, onward, _ in plan:
            landed.wait_recv()
            onward.start()
        for _, _, from_sibling in plan:
            from_sibling.wait_recv()
        for cp in sends:
            cp.wait_send()
        for _, onward, _ in plan:
            onward.wait_send()
        for cp in owns:
            cp.wait()

    sems = [pltpu.SemaphoreType.DMA((n, 3))] * 4 + [pltpu.SemaphoreType.DMA((n, 2))]
    outs = pl.pallas_call(
        body, name=name, in_specs=[ANY] * n, out_specs=[ANY] * n,
        out_shape=[jax.ShapeDtypeStruct((N_CHIPS,) + a.shape, a.dtype) for a in arrs], scratch_shapes=sems,
    )(*arrs)
    return [o.reshape(N_CHIPS, o.shape[1] * o.shape[2], o.shape[3]) for o in outs]


def _swap_sibling(arrs, src_of, shapes, name):
    n = len(arrs)

    def body(*refs):
        a_refs, got_refs = refs[:n], refs[n:2 * n]
        send, recv = refs[2 * n:]
        x, y, c = _position()
        copies = []
        for i in range(n):
            cp = pltpu.make_async_remote_copy(src_ref=src_of(a_refs[i], c), dst_ref=got_refs[i], send_sem=send.at[i],
                                              recv_sem=recv.at[i], device_id=(x, y, 1 - c), device_id_type=MESH)
            cp.start()
            copies.append(cp)
        for cp in copies:
            cp.wait()

    return pl.pallas_call(body, name=name, in_specs=[ANY] * n, out_specs=[ANY] * n,
                          out_shape=[jax.ShapeDtypeStruct(sh, a.dtype) for sh, a in zip(shapes, arrs)],
                          scratch_shapes=[pltpu.SemaphoreType.DMA((n,))] * 2)(*arrs)


def _row_tile(rows, width):
    return _tile(rows, max(16, (784 * LANES // width) // 16 * 16), 16)


def _add_halves(g, got, dtype, name):
    n, _, hr, w = g.shape
    tr = _row_tile(hr, w)

    def body(g_ref, got_ref, o_ref):
        c = lax.axis_index("c")
        own = jnp.where(c == 0, g_ref[:, 0], g_ref[:, 1])
        o_ref[...] = (own + got_ref[...]).astype(dtype)

    return pl.pallas_call(
        body, name=name, grid=(hr // tr,),
        in_specs=[pl.BlockSpec((n, 2, tr, w), lambda i: (0, 0, i, 0)), pl.BlockSpec((n, tr, w), lambda i: (0, i, 0))],
        out_specs=pl.BlockSpec((n, tr, w), lambda i: (0, i, 0)),
        out_shape=jax.ShapeDtypeStruct((n, hr, w), dtype))(g, got)


def _scatter_chips(gs, name):
    n = len(gs)

    def body(*refs):
        g_refs, out_refs = refs[:n], refs[n:2 * n]
        send, recv = refs[2 * n:]
        x, y, c = _position()
        sends = []
        for i in range(n):
            for j, (fx, fy) in enumerate(_CHIP_FLIPS):
                px, py = _flip(x, fx), _flip(y, fy)
                cp = pltpu.make_async_remote_copy(src_ref=g_refs[i].at[2 * px + py], dst_ref=out_refs[i].at[j],
                                                  send_sem=send.at[i, j], recv_sem=recv.at[i, j],
                                                  device_id=(px, py, c), device_id_type=MESH)
                cp.start()
                sends.append(cp)
        for cp in sends:
            cp.wait_recv()
        for cp in sends:
            cp.wait_send()

    return pl.pallas_call(
        body, name=name, in_specs=[ANY] * n, out_specs=[ANY] * n,
        out_shape=[jax.ShapeDtypeStruct((3,) + g.shape[1:], g.dtype) for g in gs],
        scratch_shapes=[pltpu.SemaphoreType.DMA((n, 3)), pltpu.SemaphoreType.DMA((n, 3))],
    )(*gs)


def _sum_own_and_slots(own, got, name):
    n, r, w = own.shape
    tr = _row_tile(r, w)

    def body(own_ref, got_ref, o_ref):
        me = 2 * lax.axis_index("x") + lax.axis_index("y")
        acc = own_ref[0]
        for i in range(1, n):
            acc = jnp.where(me == i, own_ref[i], acc)
        acc = acc.astype(f32)
        for j in range(3):
            acc = acc + got_ref[j].astype(f32)
        o_ref[...] = acc

    return pl.pallas_call(
        body, name=name, grid=(r // tr,),
        in_specs=[pl.BlockSpec((n, tr, w), lambda i: (0, i, 0)), pl.BlockSpec((3, tr, w), lambda i: (0, i, 0))],
        out_specs=pl.BlockSpec((tr, w), lambda i: (i, 0)), out_shape=jax.ShapeDtypeStruct((r, w), f32))(own, got)


def _share_chips(a, name):
    def body(a_ref, out_ref, send, recv):
        x, y, c = _position()
        sends = []
        for j, (fx, fy) in enumerate(_CHIP_FLIPS):
            cp = pltpu.make_async_remote_copy(src_ref=a_ref, dst_ref=out_ref.at[j], send_sem=send.at[j],
                                              recv_sem=recv.at[j], device_id=(_flip(x, fx), _flip(y, fy), c),
                                              device_id_type=MESH)
            cp.start()
            sends.append(cp)
        for cp in sends:
            cp.wait_recv()
        for cp in sends:
            cp.wait_send()

    return pl.pallas_call(
        body, name=name, in_specs=[ANY], out_specs=ANY, out_shape=jax.ShapeDtypeStruct((3,) + a.shape, a.dtype),
        scratch_shapes=[pltpu.SemaphoreType.DMA((3,)), pltpu.SemaphoreType.DMA((3,))],
    )(a)


def _sum_in_chip_order(pair, got, name):
    r, w = pair.shape
    tr = _tile(r, 1408, 8)

    def body(p_ref, g_ref, o_ref):
        x, y = lax.axis_index("x"), lax.axis_index("y")
        me = 2 * x + y
        across = [2 * _flip(x, fx) + _flip(y, fy) for fx, fy in _CHIP_FLIPS]
        acc = None
        for i in range(N_CHIPS):
            term = p_ref[...]
            for j in range(3):
                term = jnp.where(across[j] == i, g_ref[j], term)
            acc = term if acc is None else acc + term
        o_ref[...] = acc

    return pl.pallas_call(
        body, name=name, grid=(r // tr,),
        in_specs=[pl.BlockSpec((tr, w), lambda i: (i, 0)), pl.BlockSpec((3, tr, w), lambda i: (0, i, 0))],
        out_specs=pl.BlockSpec((tr, w), lambda i: (i, 0)), out_shape=jax.ShapeDtypeStruct((r, w), f32))(pair, got)


def _add2(a, b, name):
    r, w = a.shape
    tr = _tile(r, 1408, 8)
    spec = pl.BlockSpec((tr, w), lambda i: (i, 0))

    def body(a_ref, b_ref, o_ref):
        o_ref[...] = a_ref[...] + b_ref[...]

    return pl.pallas_call(body, name=name, grid=(r // tr,), in_specs=[spec, spec], out_specs=spec,
                          out_shape=jax.ShapeDtypeStruct(a.shape, f32))(a, b)


def _adamw(w, g_parts, m, v, name):
    shape = w.shape
    view = shape if len(shape) >= 2 else (1,) + shape
    assert all(d == 1 for d in view[:-2]), shape
    rows, cols = view[-2:]
    cap = max(8, (256 * 1024 // cols) // 8 * 8)
    tr = rows if rows <= cap else _tile(rows, cap, 8)
    lead = len(view) - 2
    n_g = len(g_parts)

    def body(*refs):
        w_ref = refs[0]
        g_refs = refs[1:1 + n_g]
        m_ref, v_ref, g_out, d_out, m_out, v_out = refs[1 + n_g:]
        g = g_refs[0][...]
        for gr in g_refs[1:]:
            g = g + gr[...]
        m_new = ADAM_B1 * m_ref[...] + (1.0 - ADAM_B1) * g
        v_new = ADAM_B2 * v_ref[...] + (1.0 - ADAM_B2) * (g * g)
        m_hat = m_new / (1.0 - ADAM_B1 ** ADAM_STEP)
        v_hat = v_new / (1.0 - ADAM_B2 ** ADAM_STEP)
        g_out[...] = g
        d_out[...] = -ADAM_LR * (m_hat / (jnp.sqrt(v_hat) + ADAM_EPS) + ADAM_WD * w_ref[...])
        m_out[...] = m_new
        v_out[...] = v_new

    spec = pl.BlockSpec((None,) * lead + (tr, cols), lambda i: (0,) * lead + (i, 0))
    args = [w.reshape(view)] + [g.reshape(view) for g in g_parts] + [m.reshape(view), v.reshape(view)]
    outs = pl.pallas_call(body, name=name, grid=(rows // tr,), in_specs=[spec] * len(args), out_specs=[spec] * 4,
                          out_shape=[jax.ShapeDtypeStruct(view, f32)] * 4)(*args)
    return [o.reshape(shape) for o in outs]


_BIG = ('ffn1_w_gu', 'ffn1_w_down', 'w_in', 'w_out', 'ffn2_w_gu', 'ffn2_w_down')
_SMALL_SHARDED = ('meta_tokens', 'a_conv_w', 'b_w_up', 'b_a_up', 'b_g_up')
_WEIGHTS = ('meta_tokens', 'ffn1_norm', 'ffn1_w_gu', 'ffn1_w_down', 'mix_norm', 'w_in', 'a_conv_w', 'a_log_rate',
            'a_dt_bias', 'a_out_norm', 'b_shift_mu', 'b_w0', 'b_w_up', 'b_a0', 'b_a_up', 'b_g_up', 'b_k_k', 'b_k_a',
            'b_r_k', 'b_ln_gain', 'b_ln_bias', 'w_out', 'ffn2_norm', 'ffn2_w_gu', 'ffn2_w_down', 'final_norm')
_SMALL = tuple(n for n in _WEIGHTS if n not in _BIG)


def _rows_of(shape):
    n = 1
    for d in shape:
        n *= d
    return n, -(-n // LANES)


def _pack(arrs, dtype, row_mult=32):
    parts, total = [], 0
    for a in arrs:
        n, rows = _rows_of(a.shape)
        flat = a.reshape(-1).astype(dtype)
        if n % LANES:
            flat = jnp.pad(flat, (0, rows * LANES - n))
        parts.append(flat)
        total += rows
    extra = -total % row_mult
    if extra:
        parts.append(jnp.zeros((extra * LANES,), dtype))
    return jnp.concatenate(parts).reshape(total + extra, LANES)


def _unpack(packed, shapes, lead=()):
    out, off = [], 0
    for sh in shapes:
        n, rows = _rows_of(sh)
        seg = packed[..., off:off + rows, :]
        if n % LANES:
            seg = seg.reshape(lead + (-1,))[..., :n]
        out.append(seg.reshape(lead + tuple(sh)))
        off += rows
    return out


def _cols_from_shards(s):
    return jnp.concatenate([s[i] for i in range(N_CHIPS)], axis=-1)


def kernel(x, meta_tokens, ffn1_norm, ffn1_w_gu, ffn1_w_down, mix_norm, w_in, a_conv_w, a_log_rate, a_dt_bias, a_out_norm, b_shift_mu, b_w0, b_w_up, b_a0, b_a_up, b_g_up, b_k_k, b_k_a, b_r_k, b_ln_gain, b_ln_bias, w_out, ffn2_norm, ffn2_w_gu, ffn2_w_down, final_norm, loss_target, m_meta_tokens, m_ffn1_norm, m_ffn1_w_gu, m_ffn1_w_down, m_mix_norm, m_w_in, m_a_conv_w, m_a_log_rate, m_a_dt_bias, m_a_out_norm, m_b_shift_mu, m_b_w0, m_b_w_up, m_b_a0, m_b_a_up, m_b_g_up, m_b_k_k, m_b_k_a, m_b_r_k, m_b_ln_gain, m_b_ln_bias, m_w_out, m_ffn2_norm, m_ffn2_w_gu, m_ffn2_w_down, m_final_norm, v_meta_tokens, v_ffn1_norm, v_ffn1_w_gu, v_ffn1_w_down, v_mix_norm, v_w_in, v_a_conv_w, v_a_log_rate, v_a_dt_bias, v_a_out_norm, v_b_shift_mu, v_b_w0, v_b_w_up, v_b_a0, v_b_a_up, v_b_g_up, v_b_k_k, v_b_k_a, v_b_r_k, v_b_ln_gain, v_b_ln_bias, v_w_out, v_ffn2_norm, v_ffn2_w_gu, v_ffn2_w_down, v_final_norm):
    args = locals()
    wts = {n: args[n] for n in _WEIGHTS}
    mom = {n: args["m_" + n] for n in _WEIGHTS}
    var = {n: args["v_" + n] for n in _WEIGHTS}
    chip = 2 * lax.axis_index("x") + lax.axis_index("y")

    big_shapes = [wts[n].shape[1:] for n in _BIG]
    small_shapes = [wts[n].shape[-2:] for n in _SMALL_SHARDED]
    big_flat = [wts[n].astype(bf16).reshape(wts[n].shape[1:]) for n in _BIG]
    small_packed = _pack([wts[n] for n in _SMALL_SHARDED], f32)
    gathered = _gather_chips(big_flat + [small_packed], "gather_weights")
    gu1, dn1, w_in_s, w_out_s, gu2, dn2 = [a.reshape((N_CHIPS,) + tuple(sh)) for a, sh in zip(gathered, big_shapes)]
    meta_s, conv_s, wup_s, aup_s, gup_s = _unpack(gathered[-1], small_shapes, (N_CHIPS,))
    w = {
        'ffn1_norm': ffn1_norm, 'mix_norm': mix_norm, 'ffn2_norm': ffn2_norm, 'final_norm': final_norm[None, :],
        'ffn1_wgu': gu1, 'ffn1_wd': dn1.reshape(D_FF, D), 'ffn2_wgu': gu2, 'ffn2_wd': dn2.reshape(D_FF, D),
        'w_in_p': _win_shards_to_padded(w_in_s), 'w_out': w_out_s.reshape(D, D),
        'a_conv_w': _cols_from_shards(conv_s), 'b_w_up': _cols_from_shards(wup_s), 'b_a_up': _cols_from_shards(aup_s),
        'b_g_up': _cols_from_shards(gup_s),
        'a_log_rate': a_log_rate, 'a_dt_bias': a_dt_bias, 'a_out_norm': a_out_norm, 'b_shift_mu': b_shift_mu,
        'b_w0': b_w0, 'b_a0': b_a0, 'b_k_k': b_k_k, 'b_k_a': b_k_a, 'b_r_k': b_r_k, 'b_ln_gain': b_ln_gain,
        'b_ln_bias': b_ln_bias,
    }
    meta_full = _cols_from_shards(meta_s)

    h0 = jnp.concatenate([jnp.zeros((PAD, D), f32), meta_full, x[0]], axis=0)
    tgt = jnp.concatenate([jnp.zeros((SKIP, D), f32), loss_target[0]], axis=0)
    loss_local, d_h0, g = _local_step(h0, tgt, w)
    loss = lax.psum(loss_local, ("x", "y", "c"))
    grad_x = d_h0[SKIP:][None]

    big_grads = [
        g['ffn1_wgu'],
        g['ffn1_wd'].reshape(N_CHIPS, D_FF // N_CHIPS, D),
        _win_padded_to_shards(g['w_in_p']),
        g['w_out'].reshape(N_CHIPS, D // N_CHIPS, D),
        g['ffn2_wgu'],
        g['ffn2_wd'].reshape(N_CHIPS, D_FF // N_CHIPS, D),
    ]
    g_halves = [a.reshape(N_CHIPS, 2, a.shape[1] // 2, a.shape[2]) for a in big_grads]
    sib_halves = _swap_sibling(g_halves, lambda ref, c: ref.at[:, 1 - c], [a.shape[:1] + a.shape[2:] for a in g_halves],
                               "swap_halves")
    chip_halves = [_add_halves(a, b, bf16, f"add_sibling{i}") for i, (a, b) in enumerate(zip(g_halves, sib_halves))]
    got = _scatter_chips(chip_halves, "scatter_grads")
    mine = [_sum_own_and_slots(a, b, f"sum_chips{i}") for i, (a, b) in enumerate(zip(chip_halves, got))]
    theirs = _swap_sibling(mine, lambda ref, c: ref, [a.shape for a in mine], "swap_sums")
    core = lax.axis_index("c")
    big_parts = [jnp.concatenate([jnp.where(core == 0, a, b), jnp.where(core == 0, b, a)], axis=0)
                 for a, b in zip(mine, theirs)]

    small_full = {
        'meta_tokens': d_h0[PAD:SKIP], 'ffn1_norm': g['ffn1_norm'], 'mix_norm': g['mix_norm'], 'a_conv_w': g['a_conv_w'],
        'a_log_rate': g['a_log_rate'], 'a_dt_bias': g['a_dt_bias'], 'a_out_norm': g['a_out_norm'],
        'b_shift_mu': g['b_shift_mu'], 'b_w0': g['b_w0'], 'b_w_up': g['b_w_up'], 'b_a0': g['b_a0'], 'b_a_up': g['b_a_up'],
        'b_g_up': g['b_g_up'], 'b_k_k': g['b_k_k'], 'b_k_a': g['b_k_a'], 'b_r_k': g['b_r_k'], 'b_ln_gain': g['b_ln_gain'],
        'b_ln_bias': g['b_ln_bias'], 'ffn2_norm': g['ffn2_norm'], 'final_norm': g['final_norm'],
    }
    s_shapes = [small_full[n].shape for n in _SMALL]
    s_packed = _pack([small_full[n] for n in _SMALL], f32, row_mult=256)
    (s_sib,) = _swap_sibling([s_packed], lambda ref, c: ref, [s_packed.shape], "swap_small")
    s_pair = _add2(s_packed, s_sib, "add_small")
    s_sum = _sum_in_chip_order(s_pair, _share_chips(s_pair, "share_small"), "sum_small")
    s_parts = dict(zip(_SMALL, _unpack(s_sum, s_shapes)))

    grad, delta, new_m, new_v = {}, {}, {}, {}
    for n, a in zip(_BIG, big_parts):
        grad[n], delta[n], new_m[n], new_v[n] = _adamw(wts[n], [a.reshape(wts[n].shape)], mom[n], var[n], f"adamw_{n}")
    for n in _SMALL:
        gs = s_parts[n]
        if n in _SMALL_SHARDED:
            width = wts[n].shape[-1]
            gs = lax.dynamic_slice_in_dim(gs, chip * width, width, axis=gs.ndim - 1)
        gs = gs.reshape(wts[n].shape)
        grad[n], delta[n], new_m[n], new_v[n] = _adamw(wts[n], [gs], mom[n], var[n], f"adamw_{n}")

    return (loss, grad_x, *[grad[n] for n in _WEIGHTS], *[delta[n] for n in _WEIGHTS],
            *[new_m[n] for n in _WEIGHTS], *[new_v[n] for n in _WEIGHTS])
```

```python
import functools

import jax
import jax.numpy as jnp
from jax import lax
from jax.experimental import pallas as pl
from jax.experimental.pallas import tpu as pltpu

f32 = jnp.float32
bf16 = jnp.bfloat16
MESH = pl.DeviceIdType.MESH
ANY = pl.BlockSpec(memory_space=pl.ANY)

D = 1024
N_META = 16
CHUNK = 64
PAD = CHUNK - N_META
SKIP = PAD + N_META
EPS = 1e-6
D_FF = 2816
A_HEADS = 8
A_DK = 128
B_HEADS = 16
B_N = 64
B_GN_EPS = B_N * 1e-5
IN_TOTAL = 9520
ZP = 9600
LANES = 128
N_CHIPS = 4

ADAM_LR, ADAM_B1, ADAM_B2, ADAM_EPS, ADAM_WD, ADAM_STEP = 0.001, 0.9, 0.999, 1e-08, 0.01, 10

MXU_DTYPE = bf16


def _tile(n, cap, mult):
    if n <= cap:
        return n
    best = None
    for t in range(mult, cap + 1, mult):
        if n % t == 0:
            best = t
    assert best is not None, (n, cap, mult)
    return best


def _sigmoid(x):
    return jax.nn.sigmoid(x)


def _silu(x):
    return x * jax.nn.sigmoid(x)


def _softplus(x):
    return jnp.maximum(x, 0.0) + jnp.log(1.0 + jnp.exp(-jnp.abs(x)))


def _head_matrix(c, nh):
    hd = c // nh
    r = lax.broadcasted_iota(jnp.int32, (c, nh), 0)
    h = lax.broadcasted_iota(jnp.int32, (c, nh), 1)
    return (r >= h * hd) & (r < (h + 1) * hd)


def _dot_exact_rhs(x, e, cb):
    dn = (((1,), (cb,)), ((), ()))
    eb = e.astype(bf16)
    hi = x.astype(bf16)
    lo = (x - hi.astype(f32)).astype(bf16)
    return (lax.dot_general(hi, eb, dn, preferred_element_type=f32)
            + lax.dot_general(lo, eb, dn, preferred_element_type=f32))


def _head_sum_impl(x, nh):
    e = _head_matrix(x.shape[-1], nh)
    return _dot_exact_rhs(_dot_exact_rhs(x, e, 0), e, 1)


@functools.partial(jax.custom_vjp, nondiff_argnums=(1,))
def _head_sum(x, nh):
    return _head_sum_impl(x, nh)


def _head_sum_fwd(x, nh):
    return _head_sum_impl(x, nh), None


def _head_sum_bwd(nh, _, g):
    return (_head_sum_impl(g, nh),)


_head_sum.defvjp(_head_sum_fwd, _head_sum_bwd)


@functools.partial(jax.custom_vjp, nondiff_argnums=(1,))
def _shift_rows(x, s):
    n = x.shape[0]
    row = lax.broadcasted_iota(jnp.int32, x.shape, 0)
    if s > 0:
        return jnp.where(row >= s, pltpu.roll(x, s, 0), 0.0)
    return jnp.where(row < n + s, pltpu.roll(x, n + s, 0), 0.0)


def _shift_rows_fwd(x, s):
    return _shift_rows(x, s), None


def _shift_rows_bwd(s, _, g):
    return (_shift_rows(g, -s),)


_shift_rows.defvjp(_shift_rows_fwd, _shift_rows_bwd)


def _matmul(a, b, *, ta=False, tb=False, res=None, scale=1.0, name, b_cols_split=None, out_cols_split=None,
            out_into=None):
    assert not (ta and tb)
    (ar, ac) = a.shape
    b0 = 0
    if b_cols_split:
        b0, bs = b_cols_split
        _, br, bc_part = b.shape
        bc = bs * bc_part
    else:
        br, bc = b.shape
    m, k = (ac, ar) if ta else (ar, ac)
    n, kb = (br, bc) if tb else (bc, br)
    assert k == kb, (a.shape, b.shape, ta, tb)
    tm = _tile(m, 1408, LANES) if ta else _tile(m, 1040, 16)
    tn = _tile(n, 1920, LANES)
    tk = _tile(k, 1040, 8) if ta else _tile(k, 1920, LANES)
    nk = k // tk
    dn = (((0 if ta else 1,), (1 if tb else 0,)), ((), ()))
    if b_cols_split:
        assert (tk if tb else tn) == bc_part, (b.shape, tn, tk)

    def body(*refs):
        a_ref, b_ref = refs[:2]
        r_ref = refs[2] if res is not None else None
        o_ref, acc = refs[-2:]
        kk = pl.program_id(2)

        @pl.when(kk == 0)
        def _():
            acc[...] = jnp.zeros_like(acc)

        acc[...] += lax.dot_general(a_ref[...].astype(MXU_DTYPE), b_ref[...].astype(MXU_DTYPE), dn,
                                    preferred_element_type=f32)

        @pl.when(kk == nk - 1)
        def _():
            out = acc[...]
            if scale != 1.0:
                out = out * scale
            if res is not None:
                out = r_ref[...] + out
            o_ref[...] = out

    if ta:
        a_spec = pl.BlockSpec((tk, tm), lambda i, j, kk: (kk, i))
    else:
        a_spec = pl.BlockSpec((tm, tk), lambda i, j, kk: (i, kk))
    if tb and b_cols_split:
        b_spec = pl.BlockSpec((None, tn, tk), lambda i, j, kk: (kk + b0, j, 0))
    elif tb:
        b_spec = pl.BlockSpec((tn, tk), lambda i, j, kk: (j, kk))
    elif b_cols_split:
        b_spec = pl.BlockSpec((None, tk, tn), lambda i, j, kk: (j + b0, kk, 0))
    else:
        b_spec = pl.BlockSpec((tk, tn), lambda i, j, kk: (kk, j))
    in_specs = [a_spec, b_spec]
    args = [a, b]
    if res is not None:
        in_specs.append(pl.BlockSpec((tm, tn), lambda i, j, kk: (i, j)))
        args.append(res)
    aliases = {}
    if out_cols_split:
        o0, total = out_cols_split
        out_spec = pl.BlockSpec((None, tm, tn), lambda i, j, kk: (j + o0, i, 0))
        out_shape = jax.ShapeDtypeStruct((total, m, tn), f32)
        if out_into is not None:
            assert out_into.shape == out_shape.shape
            in_specs.append(ANY)
            args.append(out_into)
            aliases = {len(args) - 1: 0}
    else:
        out_spec = pl.BlockSpec((tm, tn), lambda i, j, kk: (i, j))
        out_shape = jax.ShapeDtypeStruct((m, n), f32)
    return pl.pallas_call(
        body, name=name, grid=(m // tm, n // tn, nk), in_specs=in_specs, out_specs=out_spec, out_shape=out_shape,
        scratch_shapes=[pltpu.VMEM((tm, tn), f32)], input_output_aliases=aliases,
        compiler_params=pltpu.CompilerParams(dimension_semantics=("parallel", "parallel", "arbitrary")),
    )(*args)


def _tw_fwd(fn, ins, in_specs, out_shapes, out_specs, grid, name, with_pid=False):
    n_in = len(ins)

    def body(*refs):
        vals = [r[...] for r in refs[:n_in]]
        outs = fn(pl.program_id(0), *vals) if with_pid else fn(*vals)
        for r, o in zip(refs[n_in:], outs):
            r[...] = o.astype(r.dtype)

    return pl.pallas_call(body, name=name, grid=grid, in_specs=in_specs, out_specs=out_specs,
                          out_shape=out_shapes)(*ins)


def _tw_bwd(fn, ins, in_specs, cts, ct_specs, kinds, grid, name, with_pid=False, tile_dtype=f32, ct_extra=(),
            residual=None):
    n_in, n_ct = len(ins), len(cts)
    diff = [i for i, kd in enumerate(kinds) if kd is not None]
    n_ex = len(ct_extra)

    def body(*refs):
        vals = [r[...] for r in refs[:n_in]]
        ctv = [r[...].astype(f32) for r in refs[n_in:n_in + n_ct]]
        for (ci, _), r in zip(ct_extra, refs[n_in + n_ct:n_in + n_ct + n_ex]):
            ctv[ci] = ctv[ci] + r[...]
        ctv = tuple(ctv)
        n_fixed = n_in + n_ct + n_ex
        res_ref = refs[n_fixed] if residual is not None else None
        g_refs = refs[n_fixed + (residual is not None):]
        pid = pl.program_id(0)

        def f(*dv):
            full = list(vals)
            for i, v in zip(diff, dv):
                full[i] = v
            out = fn(pid, *full) if with_pid else fn(*full)
            return tuple(out)

        _, vjp = jax.vjp(f, *[vals[i] for i in diff])
        gs = vjp(ctv)
        first = pid == 0
        for i2 in range(1, len(grid)):
            first = first & (pl.program_id(i2) == 0)
        for i, g, g_ref in zip(diff, gs, g_refs):
            if kinds[i] != 'acc':
                if i == 0 and res_ref is not None:
                    g = res_ref[...] + g
                g_ref[...] = g.astype(g_ref.dtype)
            else:
                @pl.when(first)
                def _(g=g, g_ref=g_ref):
                    g_ref[...] = g

                @pl.when(jnp.logical_not(first))
                def _(g=g, g_ref=g_ref):
                    g_ref[...] += g

    zero_map = {1: lambda *a: (0,), 2: lambda *a: (0, 0), 3: lambda *a: (0, 0, 0)}
    out_specs, out_shapes = [], []
    for i in diff:
        if kinds[i] == 'tile':
            out_shapes.append(jax.ShapeDtypeStruct(ins[i].shape, tile_dtype))
            out_specs.append(in_specs[i])
        elif kinds[i] == 'acc':
            out_shapes.append(jax.ShapeDtypeStruct(ins[i].shape, f32))
            out_specs.append(pl.BlockSpec(ins[i].shape, zero_map[ins[i].ndim]))
        else:
            out_shapes.append(jax.ShapeDtypeStruct(kinds[i][1], kinds[i][3] if len(kinds[i]) > 3 else tile_dtype))
            out_specs.append(kinds[i][2])
    extra_specs = [ct_specs[ci] for ci, _ in ct_extra]
    extra = [a for _, a in ct_extra]
    if residual is not None:
        assert kinds[0] == 'tile'
        extra_specs.append(in_specs[0])
        extra.append(residual)
    return pl.pallas_call(body, name=name, grid=grid, in_specs=list(in_specs) + list(ct_specs) + extra_specs,
                          out_specs=out_specs, out_shape=out_shapes)(*ins, *cts, *extra)


def _row_spec(tm, c, col_block=0):
    return pl.BlockSpec((tm, c), lambda i, cb=col_block: (i, cb))


def _full_spec(shape):
    nd = len(shape)
    return pl.BlockSpec(shape, lambda *a, nd=nd: (0,) * nd)


def _f_rms(x, g):
    return (x * lax.rsqrt(jnp.mean(x * x, axis=-1, keepdims=True) + EPS) * g,)


def _f_swiglu(gate, up):
    return (_silu(gate) * up,)


def _f_loss(pid, h, g, tgt, *, tm):
    y = h * lax.rsqrt(jnp.mean(h * h, axis=-1, keepdims=True) + EPS) * g
    row = pid * tm + lax.broadcasted_iota(jnp.int32, (tm, 1), 0)
    err = jnp.where(row >= SKIP, y - tgt, 0.0)
    per_row = jnp.mean(err * err, axis=-1, keepdims=True)
    return (0.5 * jnp.sum(per_row, axis=0, keepdims=True),)


def _f_conv(x, w, *, norm, scale):
    y = x * w[3:4, :]
    for s in (1, 2, 3):
        y = y + _shift_rows(x, s) * w[3 - s:4 - s, :]
    y = _silu(y)
    if norm:
        y = y * lax.rsqrt(jnp.sum(y * y, axis=-1, keepdims=True) + 1e-6) * scale
    return (y,)


def _f_dgates(pid, abeta, aalpha, log_rate, dt_bias, *, tm):
    row = pid * tm + lax.broadcasted_iota(jnp.int32, (tm, 1), 0)
    live = row >= PAD
    beta = jnp.where(live, _sigmoid(abeta), 0.0)
    g = jnp.where(live, -jnp.exp(log_rate) * _softplus(aalpha + dt_bias), 0.0)
    return beta, g


def _f_tshift(z, mu):
    return (z + (_shift_rows(z, 1) - z) * mu,)


def _f_rwkv_pre(k, wd, ad, gd, w0, w_up, a0, a_up, g_up, k_k, k_a):
    w_log = -_softplus(-(w0 + _smm(jnp.tanh(wd), w_up, 1))) - 0.5
    lw = -jnp.exp(w_log)
    a_lr = _sigmoid(a0 + _smm(ad, a_up, 1))
    gate = _smm(_sigmoid(gd), g_up, 1)
    kkp = k * k_k
    kk = kkp * lax.rsqrt(_head_sum(kkp * kkp, B_HEADS) + 1e-6)
    kmod = k * (1.0 + (a_lr - 1.0) * k_a)
    return lw, kmod, -kk, kk * a_lr, gate


def _f_mix_post(o, az, y, r, kmod, v, gate, ga, gb, out_gain, ln_g, ln_b, r_k):
    ms = _head_sum(o * o, A_HEADS) * (1.0 / A_DK)
    oa = o * lax.rsqrt(ms + EPS) * out_gain * _silu(az)
    mean = _head_sum(y, B_HEADS) * (1.0 / B_N)
    yc = y - mean
    var = _head_sum(yc * yc, B_HEADS) * (1.0 / B_N)
    yn = yc * lax.rsqrt(var + B_GN_EPS) * ln_g + ln_b
    bonus = _head_sum(r * kmod * r_k, B_HEADS) * v
    ob = (yn + bonus) * gate
    return (_sigmoid(ga) * oa + _sigmoid(gb) * ob,)


def _split2(a):
    hi = a.astype(bf16)
    return hi, (a - hi.astype(f32)).astype(bf16)


def _dot_passes(a, b, ca, cb, passes):
    dn = (((ca,), (cb,)), ((), ()))
    if passes == 1:
        return lax.dot_general(a.astype(bf16), b.astype(bf16), dn, preferred_element_type=f32)
    ah, al = _split2(a)
    bh, bl = _split2(b)
    return (lax.dot_general(ah, bh, dn, preferred_element_type=f32)
            + (lax.dot_general(ah, bl, dn, preferred_element_type=f32)
               + lax.dot_general(al, bh, dn, preferred_element_type=f32)))


@functools.partial(jax.custom_vjp, nondiff_argnums=(2, 3, 4))
def _sdot(a, b, ca, cb, passes):
    return _dot_passes(a, b, ca, cb, passes)


def _sdot_fwd(a, b, ca, cb, passes):
    return _dot_passes(a, b, ca, cb, passes), (a, b)


def _sdot_bwd(ca, cb, passes, res, g):
    a, b = res
    if (ca, cb) == (1, 0):
        return _dot_passes(g, b, 1, 1, passes), _dot_passes(a, g, 0, 0, passes)
    if (ca, cb) == (1, 1):
        return _dot_passes(g, b, 1, 0, passes), _dot_passes(g, a, 0, 0, passes)
    assert (ca, cb) == (0, 0)
    return _dot_passes(b, g, 1, 1, passes), _dot_passes(a, g, 1, 0, passes)


_sdot.defvjp(_sdot_fwd, _sdot_bwd)


def _smm(a, b, passes=3):
    return _sdot(a, b, 1, 0, passes)


def _smm_nt(a, b, passes=3):
    return _sdot(a, b, 1, 1, passes)


def _smm_tn(a, b, passes=3):
    return _sdot(a, b, 0, 0, passes)


def _tri_dot(x, ca):
    n = x.shape[0]
    incl = _tri_masks(n)[0]
    dn = (((ca,), (0,)), ((), ()))
    tri = incl.astype(bf16)
    hi, r1 = x.astype(bf16), None
    r1 = x - hi.astype(f32)
    mid = r1.astype(bf16)
    lo = (r1 - mid.astype(f32)).astype(bf16)
    return (lax.dot_general(tri, hi, dn, preferred_element_type=f32)
            + (lax.dot_general(tri, mid, dn, preferred_element_type=f32)
               + lax.dot_general(tri, lo, dn, preferred_element_type=f32)))


@jax.custom_vjp
def _cumsum_rows(x):
    return _tri_dot(x, 1)


def _cumsum_rows_fwd(x):
    return _tri_dot(x, 1), None


def _cumsum_rows_bwd(_, g):
    return (_tri_dot(g, 0),)


_cumsum_rows.defvjp(_cumsum_rows_fwd, _cumsum_rows_bwd)


def _tri_masks(n):
    i = lax.broadcasted_iota(jnp.int32, (n, n), 0)
    j = lax.broadcasted_iota(jnp.int32, (n, n), 1)
    return i >= j, i > j, i == j, i <= j


def _unit_lower_inv_impl(low, passes):
    n = low.shape[0]
    assert n == CHUNK
    _, _, eye, _ = _tri_masks(n)
    acc = eye.astype(f32) + low
    p = low
    for _ in range(5):
        p = _dot_passes(p, p, 1, 0, passes)
        acc = acc + _dot_passes(acc, p, 1, 0, passes)
    return acc


@functools.partial(jax.custom_vjp, nondiff_argnums=(1,))
def _unit_lower_inv(low, passes=3):
    return _unit_lower_inv_impl(low, passes)


def _unit_lower_inv_fwd(low, passes):
    t = _unit_lower_inv_impl(low, passes)
    return t, t


def _unit_lower_inv_bwd(passes, t, g):
    return (_dot_passes(_dot_passes(t, g, 0, 0, passes), t, 1, 1, passes),)


_unit_lower_inv.defvjp(_unit_lower_inv_fwd, _unit_lower_inv_bwd)


@functools.partial(jax.custom_vjp, nondiff_argnums=(2,))
def _unit_lower_inv_saved(low, t_saved, passes):
    return t_saved


def _unit_lower_inv_saved_fwd(low, t_saved, passes):
    return t_saved, t_saved


def _unit_lower_inv_saved_bwd(passes, t, g):
    return _unit_lower_inv_bwd(passes, t, g) + (jnp.zeros_like(t),)


_unit_lower_inv_saved.defvjp(_unit_lower_inv_saved_fwd, _unit_lower_inv_saved_bwd)


def _inverse(low, passes, saved):
    return _unit_lower_inv(low, passes) if saved is None else _unit_lower_inv_saved(low, saved, passes)

DELTA_PASSES = 1
DELTA_INV_PASSES = 1


def _delta_chunk(s, q, k, v, beta_row, g_row, inv_saved=None):
    p = DELTA_PASSES
    incl, strict, eye, upper = _tri_masks(CHUNK)
    beta = jnp.sum(jnp.where(eye, beta_row, 0.0), axis=1, keepdims=True)
    g = jnp.sum(jnp.where(eye, g_row, 0.0), axis=1, keepdims=True)
    gc = jnp.sum(jnp.where(incl, g_row, 0.0), axis=1, keepdims=True)
    gc_row = jnp.sum(jnp.where(upper, g, 0.0), axis=0, keepdims=True)
    decay = jnp.where(incl, jnp.exp(jnp.where(incl, gc - gc_row, 0.0)), 0.0)
    kb = k * beta
    vb = v * beta
    m = jnp.where(strict, _smm_nt(kb, k, p) * decay, 0.0)
    tinv = _inverse(-m, DELTA_INV_PASSES, inv_saved)
    u = _smm(tinv, vb, p)
    wk = _smm(tinv, kb * jnp.exp(gc), p)
    attn = _smm_nt(q, k, p) * decay
    qg = q * jnp.exp(gc)
    g_last = jnp.sum(g, axis=0, keepdims=True)
    k_tail = k * jnp.exp(g_last - gc)
    v_new = u - _smm(wk, s, p)
    o = _smm(qg, s, p) + _smm(attn, v_new, p)
    s_new = s * jnp.exp(g_last) + _smm_tn(k_tail, v_new, p)
    return o, s_new, tinv


RWKV_PASSES = 1
RWKV_INV_PASSES = 1


def _rwkv_chunk(st, r, k, v, a, b, lw, inv_saved=None):
    c = CHUNK
    p, pi = RWKV_PASSES, RWKV_INV_PASSES
    _, strict, _, _ = _tri_masks(c)
    lane = lax.broadcasted_iota(jnp.int32, (c, 2 * B_N), 1)
    row = lax.broadcasted_iota(jnp.int32, (c, 2 * B_N), 0)
    first = lane < B_N
    incl2 = row >= jnp.where(first, lane, lane - B_N)
    bi = lax.broadcasted_iota(jnp.int32, (2 * B_N, 2 * B_N), 0) < B_N
    bj = lax.broadcasted_iota(jnp.int32, (2 * B_N, 2 * B_N), 1) < B_N
    blockdiag = bi == bj
    cum = _cumsum_rows(lw)
    e_pos = jnp.exp(cum)
    e_neg = jnp.exp(-cum)
    rt = r * e_pos
    at = a * jnp.exp(cum - lw)
    kt = k * e_neg
    bt = b * e_neg
    bk = jnp.concatenate([bt, kt], axis=0)
    a_s0 = _smm_nt(at, st, p)
    r_s0 = _smm_nt(rt, st, p)
    heads = (first, jnp.logical_not(first))
    u = jnp.zeros((c, 2 * B_N), f32)
    invs = []
    for hi, sel in enumerate(heads):
        at_h = jnp.where(sel, at, 0.0)
        ab = jnp.where(strict, _smm_nt(at_h, bt, pi), 0.0)
        ak = jnp.where(strict, _smm_nt(at_h, kt, p), 0.0)
        t_h = _inverse(ab, pi, None if inv_saved is None else inv_saved[hi])
        invs.append(t_h)
        u = u + _smm(t_h, jnp.where(sel, a_s0, 0.0) + _smm(ak, jnp.where(sel, v, 0.0), p), p)
    y = r_s0
    for sel in heads:
        rbk = jnp.where(incl2, _smm_nt(jnp.where(sel, rt, 0.0), bk, p), 0.0)
        uv = jnp.concatenate([jnp.where(sel, u, 0.0), jnp.where(sel, v, 0.0)], axis=0)
        y = y + _smm(rbk, uv, p)
    cl = jnp.sum(lw, axis=0, keepdims=True)
    dec = jnp.exp(cl - cum)
    uv_all = jnp.concatenate([u, v], axis=0)
    bk_dec = jnp.concatenate([b * dec, k * dec], axis=0)
    st_new = st * jnp.exp(cl) + jnp.where(blockdiag, _smm_tn(uv_all, bk_dec, p), 0.0)
    return y, st_new, jnp.stack(invs)


GROUPS_PER_STEP = 8


def _scan_specs(ins, col_offs, n_chunks, reverse):
    gw = GROUPS_PER_STEP * LANES
    cidx = (lambda c: n_chunks - 1 - c) if reverse else (lambda c: c)
    specs = []
    for a, off in zip(ins, col_offs):
        if a.ndim == 2:
            assert off % gw == 0
            specs.append(pl.BlockSpec((CHUNK, gw), lambda h, c, o=off // gw: (cidx(c), h + o)))
        else:
            specs.append(pl.BlockSpec((GROUPS_PER_STEP, None, 1, CHUNK), lambda h, c: (h, cidx(c), 0, 0)))
    return specs, cidx


def _group_vals(refs, g):
    return [r[:, g * LANES:(g + 1) * LANES] if len(r.shape) == 2 else r[g] for r in refs]


def _scan_fwd(chunk_fn, ins, col_offs, n_groups, n_chunks, state_shape, inv_shape, name):
    n_in = len(ins)
    gps = GROUPS_PER_STEP
    t = ins[0].shape[0]

    def body(*refs):
        in_refs = refs[:n_in]
        o_ref, s0_ref, inv_ref, st = refs[n_in:]

        @pl.when(pl.program_id(1) == 0)
        def _():
            st[...] = jnp.zeros_like(st)

        states = st[...]
        vals = [jnp.stack(col) for col in zip(*[_group_vals(in_refs, g) for g in range(gps)])]
        o, s_new, inv = jax.vmap(chunk_fn)(states, *vals)
        s0_ref[...] = states
        inv_ref[...] = inv
        st[...] = s_new
        for g in range(gps):
            o_ref[:, g * LANES:(g + 1) * LANES] = o[g]

    specs, _ = _scan_specs(ins, col_offs, n_chunks, False)
    zeros_i = (0,) * len(inv_shape)
    return pl.pallas_call(
        body, name=name, grid=(n_groups // gps, n_chunks), in_specs=specs,
        out_specs=[pl.BlockSpec((CHUNK, gps * LANES), lambda h, c: (c, h)),
                   pl.BlockSpec((gps, None) + state_shape, lambda h, c: (h, c, 0, 0)),
                   pl.BlockSpec((gps, None) + inv_shape, lambda h, c: (h, c) + zeros_i)],
        out_shape=[jax.ShapeDtypeStruct((t, n_groups * LANES), f32),
                   jax.ShapeDtypeStruct((n_groups, n_chunks) + state_shape, f32),
                   jax.ShapeDtypeStruct((n_groups, n_chunks) + inv_shape, f32)],
        scratch_shapes=[pltpu.VMEM((gps,) + state_shape, f32)],
        compiler_params=pltpu.CompilerParams(dimension_semantics=("parallel", "arbitrary")),
    )(*ins)


def _scan_bwd(chunk_fn, s0s, invs, ins, col_offs, d_out, n_groups, n_chunks, state_shape, name):
    n_in = len(ins)
    gps = GROUPS_PER_STEP
    t = d_out.shape[0]
    inv_shape = invs.shape[2:]

    def body(*refs):
        s0_ref, inv_ref = refs[:2]
        in_refs = refs[2:2 + n_in]
        do_ref = refs[2 + n_in]
        g_refs = refs[3 + n_in:3 + 2 * n_in]
        dst = refs[3 + 2 * n_in]

        @pl.when(pl.program_id(1) == 0)
        def _():
            dst[...] = jnp.zeros_like(dst)

        vals = [jnp.stack(col) for col in zip(*[_group_vals(in_refs, g) for g in range(gps)])]
        d_o = jnp.stack([do_ref[:, g * LANES:(g + 1) * LANES] for g in range(gps)])
        inv = inv_ref[...]

        def with_saved(s, *a):
            return jax.vmap(lambda ss, ii, *aa: chunk_fn(ss, *aa, inv_saved=ii)[:2])(s, inv, *a)

        _, vjp = jax.vjp(with_saved, s0_ref[...], *vals)
        gs = vjp((d_o, dst[...]))
        dst[...] = gs[0]
        for g_ref, gv in zip(g_refs, gs[1:]):
            if len(g_ref.shape) == 2:
                for g in range(gps):
                    g_ref[:, g * LANES:(g + 1) * LANES] = gv[g]
            else:
                g_ref[...] = gv

    specs, cidx = _scan_specs(ins, col_offs, n_chunks, True)
    out_lane = pl.BlockSpec((CHUNK, gps * LANES), lambda h, c: (cidx(c), h))
    g_specs = [out_lane if a.ndim == 2 else sp for a, sp in zip(ins, specs)]
    g_shapes = [(t, n_groups * LANES) if a.ndim == 2 else a.shape for a in ins]
    s0_spec = pl.BlockSpec((gps, None) + state_shape, lambda h, c: (h, cidx(c), 0, 0))
    zeros_i = (0,) * len(inv_shape)
    inv_spec = pl.BlockSpec((gps, None) + inv_shape, lambda h, c: (h, cidx(c)) + zeros_i)
    return pl.pallas_call(
        body, name=name, grid=(n_groups // gps, n_chunks), in_specs=[s0_spec, inv_spec] + specs + [out_lane],
        out_specs=g_specs, out_shape=[jax.ShapeDtypeStruct(sh, f32) for sh in g_shapes],
        scratch_shapes=[pltpu.VMEM((gps,) + state_shape, f32)],
        compiler_params=pltpu.CompilerParams(dimension_semantics=("parallel", "arbitrary")),
    )(s0s, invs, *ins, d_out)


def _rms_fwd(x, g, name):
    t = x.shape[0]
    tm = _tile(t, 832, 16)
    return _tw_fwd(_f_rms, [x, g], [_row_spec(tm, D), _full_spec(g.shape)],
                   [jax.ShapeDtypeStruct(x.shape, MXU_DTYPE)], [_row_spec(tm, D)], (t // tm,), name)[0]


def _rms_bwd(x, g, dy, residual, name):
    t = x.shape[0]
    tm = _tile(t, 832, 8)
    return _tw_bwd(_f_rms, [x, g], [_row_spec(tm, D), _full_spec(g.shape)], [dy], [_row_spec(tm, D)],
                   ['tile', 'acc'], (t // tm,), name, residual=residual)


def _ffn_fwd(h, gain, wgu, wd, tag):
    xn = _rms_fwd(h, gain, f"{tag}_rms")
    gate, up, act = _gate_up_act(xn, wgu, f"{tag}_gate_up")
    out = _matmul(act, wd, res=h, scale=0.5, name=f"{tag}_down")
    return out, (xn, gate, up, act)


def _mxu_dot(a, b, dn):
    return lax.dot_general(a.astype(MXU_DTYPE), b.astype(MXU_DTYPE), dn, preferred_element_type=f32)


def _gate_up_act(xn, wgu, name):
    t = xn.shape[0]
    wdt = wgu.shape[2]
    tm = _tile(t, 416, 16)
    dn = (((1,), (0,)), ((), ()))

    def body(x_ref, wg_ref, wu_ref, g_ref, u_ref, a_ref):
        x = x_ref[...]
        g = _mxu_dot(x, wg_ref[...], dn)
        u = _mxu_dot(x, wu_ref[...], dn)
        g_ref[...] = g
        u_ref[...] = u
        a_ref[...] = _f_swiglu(g, u)[0].astype(a_ref.dtype)

    out_spec = pl.BlockSpec((tm, wdt), lambda j, i: (i, j))
    return pl.pallas_call(
        body, name=name, grid=(2, t // tm),
        in_specs=[pl.BlockSpec((tm, D), lambda j, i: (i, 0)), pl.BlockSpec((None, D, wdt), lambda j, i: (j, 0, 0)),
                  pl.BlockSpec((None, D, wdt), lambda j, i: (j + 2, 0, 0))],
        out_specs=[out_spec] * 3,
        out_shape=[jax.ShapeDtypeStruct((t, 2 * wdt), f32)] * 2 + [jax.ShapeDtypeStruct((t, 2 * wdt), MXU_DTYPE)],
        compiler_params=pltpu.CompilerParams(dimension_semantics=("parallel", "parallel")),
    )(xn, wgu, wgu)


def _d_gate_up(dout, wd, gate, up, name):
    t = dout.shape[0]
    wdt = D_FF // 2
    tm = _tile(t, 416, 16)
    dn = (((1,), (1,)), ((), ()))

    def body(do_ref, wd_ref, g_ref, u_ref, dg_ref, du_ref):
        d_act = 0.5 * _mxu_dot(do_ref[...], wd_ref[...], dn)
        _, vjp = jax.vjp(_f_swiglu, g_ref[...], u_ref[...])
        dg, du = vjp((d_act,))
        dg_ref[...] = dg.astype(dg_ref.dtype)
        du_ref[...] = du.astype(du_ref.dtype)

    spec = pl.BlockSpec((tm, wdt), lambda j, i: (i, j))
    return pl.pallas_call(
        body, name=name, grid=(2, t // tm),
        in_specs=[pl.BlockSpec((tm, D), lambda j, i: (i, 0)), pl.BlockSpec((wdt, D), lambda j, i: (j, 0)), spec, spec],
        out_specs=[spec] * 2, out_shape=[jax.ShapeDtypeStruct((t, D_FF), MXU_DTYPE)] * 2,
        compiler_params=pltpu.CompilerParams(dimension_semantics=("parallel", "parallel")),
    )(dout, wd, gate, up)


def _ffn_bwd(h, gain, wgu, wd, saved, dout, tag):
    xn, gate, up, act = saved
    t = h.shape[0]
    d_wd = _matmul(act, dout, ta=True, scale=0.5, name=f"{tag}_dwd")
    d_gate, d_up = _d_gate_up(dout, wd, gate, up, f"{tag}_dact")
    d_wgu = _matmul(xn, d_gate, ta=True, out_cols_split=(0, N_CHIPS), name=f"{tag}_dwg")
    d_wgu = _matmul(xn, d_up, ta=True, out_cols_split=(2, N_CHIPS), out_into=d_wgu, name=f"{tag}_dwu")
    d_xn = _matmul(d_gate, wgu, tb=True, b_cols_split=(0, 2), name=f"{tag}_dxn_g")
    d_xn = _matmul(d_up, wgu, tb=True, b_cols_split=(2, 2), res=d_xn, name=f"{tag}_dxn_u")
    d_h, d_gain = _rms_bwd(h, gain, d_xn, dout, f"{tag}_drms")
    return d_h, d_gain, d_wgu, d_wd


def _col_spec(t, first_block):
    return pl.BlockSpec((t, LANES), lambda j, fb=first_block: (0, j + fb))


def _local_step(h0, tgt, w):
    t = h0.shape[0]
    assert t % CHUNK == 0
    nc = t // CHUNK
    grads = {}

    h1, ffn1_saved = _ffn_fwd(h0, w['ffn1_norm'], w['ffn1_wgu'], w['ffn1_wd'], "ffn1")
    u = _rms_fwd(h1, w['mix_norm'], "mix_rms")
    z = _matmul(u, w['w_in_p'], name="in_proj")
    zs = z[:, 9216:9216 + 304]
    abeta, aalpha = zs[:, 288:296], zs[:, 296:304]

    conv_w = w['a_conv_w']
    conv_fns = [functools.partial(_f_conv, norm=True, scale=A_DK ** -0.5),
                functools.partial(_f_conv, norm=True, scale=1.0),
                functools.partial(_f_conv, norm=False, scale=1.0)]
    qkv = []
    for idx, fn in enumerate(conv_fns):
        qkv.append(_tw_fwd(fn, [z, conv_w], [_col_spec(t, 8 * idx), pl.BlockSpec((4, LANES), lambda j, o=8 * idx: (0, j + o))],
                           [jax.ShapeDtypeStruct((t, D), f32)], [_col_spec(t, 0)], (A_HEADS,), f"a_conv{idx}")[0])
    aq, ak, av = qkv
    tmg = _tile(t, 1040, 8)
    dg_fn = functools.partial(_f_dgates, tm=tmg)
    dg_specs = [_row_spec(tmg, A_HEADS)] * 2 + [_full_spec((1, A_HEADS))] * 2
    beta, gdec = _tw_fwd(dg_fn, [abeta, aalpha, w['a_log_rate'], w['a_dt_bias']], dg_specs,
                         [jax.ShapeDtypeStruct((t, A_HEADS), f32)] * 2, [_row_spec(tmg, A_HEADS)] * 2, (t // tmg,),
                         "a_gates", with_pid=True)
    beta_h = beta.T.reshape(A_HEADS, nc, 1, CHUNK)
    gdec_h = gdec.T.reshape(A_HEADS, nc, 1, CHUNK)
    a_ins = [aq, ak, av, beta_h, gdec_h]
    a_offs = [0] * 5
    o_scan, a_s0, a_inv = _scan_fwd(_delta_chunk, a_ins, a_offs, A_HEADS, nc, (A_DK, A_DK), (CHUNK, CHUNK), "a_scan")

    mu = w['b_shift_mu']
    mu_rkv, mu_s = mu[:, :3072], mu[:, 3072:]
    zf_rkv = _tw_fwd(_f_tshift, [z, mu_rkv], [_col_spec(t, 32), pl.BlockSpec((1, LANES), lambda j: (0, j))],
                     [jax.ShapeDtypeStruct((t, 3072), f32)], [_col_spec(t, 0)], (24,), "b_shift")[0]
    zs_b = zs[:, :288]
    zf_s = _tw_fwd(_f_tshift, [zs_b, mu_s], [_full_spec((t, 288)), _full_spec((1, 288))],
                   [jax.ShapeDtypeStruct((t, 288), f32)], [_full_spec((t, 288))], (1,), "b_shift_s")[0]
    wdf, adf, gdf = zf_s[:, 0:64], zf_s[:, 64:128], zf_s[:, 128:288]
    tmr = _tile(t, 208, 16)
    pre_params = [w['b_w0'], w['b_w_up'], w['b_a0'], w['b_a_up'], w['b_g_up'], w['b_k_k'], w['b_k_a']]
    pre_ins = [zf_rkv, wdf, adf, gdf] + pre_params
    pre_specs = ([_row_spec(tmr, D, 1), _row_spec(tmr, 64), _row_spec(tmr, 64), _row_spec(tmr, 160)]
                 + [_full_spec(p.shape) for p in pre_params])
    lw, kmod, a_s, b_s, bgate = _tw_fwd(_f_rwkv_pre, pre_ins, pre_specs, [jax.ShapeDtypeStruct((t, D), f32)] * 5,
                                        [_row_spec(tmr, D)] * 5, (t // tmr,), "b_pre")
    b_ins = [zf_rkv, kmod, zf_rkv, a_s, b_s, lw]
    b_offs = [0, 0, 2 * D, 0, 0, 0]
    y_scan, b_s0, b_inv = _scan_fwd(_rwkv_chunk, b_ins, b_offs, B_HEADS // 2, nc, (2 * B_N, 2 * B_N),
                                    (2, CHUNK, CHUNK), "b_scan")

    out_gain_t = jnp.tile(w['a_out_norm'], (1, A_HEADS))
    r_k = w['b_r_k'].reshape(1, D)
    post_params = [out_gain_t, w['b_ln_gain'], w['b_ln_bias'], r_k]
    post_ins = [o_scan, z, y_scan, zf_rkv, kmod, zf_rkv, bgate, z, z] + post_params
    post_specs = ([_row_spec(tmr, D), _row_spec(tmr, D, 3), _row_spec(tmr, D), _row_spec(tmr, D, 0), _row_spec(tmr, D),
                   _row_spec(tmr, D, 2), _row_spec(tmr, D), _row_spec(tmr, D, 7), _row_spec(tmr, D, 8)]
                  + [_full_spec((1, D))] * 4)
    merged = _tw_fwd(_f_mix_post, post_ins, post_specs, [jax.ShapeDtypeStruct((t, D), MXU_DTYPE)],
                     [_row_spec(tmr, D)], (t // tmr,), "mix_post")[0]
    h2 = _matmul(merged, w['w_out'], res=h1, name="out_proj")
    h3, ffn2_saved = _ffn_fwd(h2, w['ffn2_norm'], w['ffn2_wgu'], w['ffn2_wd'], "ffn2")

    tml = _tile(t, 416, 8)
    fnorm = w['final_norm']
    loss_fn = functools.partial(_f_loss, tm=tml)
    loss_specs = [_row_spec(tml, D), _full_spec((1, D)), _row_spec(tml, D)]
    loss_parts, d_h3, grads['final_norm'] = _loss_and_grad(loss_fn, h3, fnorm, tgt, loss_specs, tml)
    loss = jnp.sum(loss_parts)

    d_h2, grads['ffn2_norm'], grads['ffn2_wgu'], grads['ffn2_wd'] = _ffn_bwd(
        h2, w['ffn2_norm'], w['ffn2_wgu'], w['ffn2_wd'], ffn2_saved, d_h3, "ffn2")
    grads['w_out'] = _matmul(merged, d_h2, ta=True, name="d_w_out")
    d_merged = _matmul(d_h2, w['w_out'], tb=True, name="d_merged")

    win = ('tile', (t, D), _row_spec(tmr, D))
    zwin = win + (MXU_DTYPE,)
    post_kinds = ['tile', zwin, 'tile', win, 'tile', win, 'tile', zwin, zwin] + ['acc'] * 4
    (d_o, d_az, d_y, d_r1, d_kmod1, d_v1, d_bgate, d_ga, d_gb,
     d_out_gain_t, grads['b_ln_gain'], grads['b_ln_bias'], d_r_k) = _tw_bwd(
        _f_mix_post, post_ins, post_specs, [d_merged], [_row_spec(tmr, D)], post_kinds, (t // tmr,), "mix_post_bwd")
    grads['a_out_norm'] = jnp.sum(d_out_gain_t.reshape(A_HEADS, A_DK), axis=0, keepdims=True)
    grads['b_r_k'] = d_r_k.reshape(1, B_HEADS, B_N)

    d_r2, d_kmod2, d_v2, d_as, d_bs, d_lw = _scan_bwd(_rwkv_chunk, b_s0, b_inv, b_ins, b_offs, d_y, B_HEADS // 2, nc,
                                                      (2 * B_N, 2 * B_N), "b_scan_bwd")
    pre_kinds = [win] + ['tile'] * 3 + ['acc'] * 7
    pre_ct_specs = [_row_spec(tmr, D)] * 5
    (d_zf_k, d_wdf, d_adf, d_gdf, grads['b_w0'], grads['b_w_up'], grads['b_a0'], grads['b_a_up'], grads['b_g_up'],
     grads['b_k_k'], grads['b_k_a']) = _tw_bwd(
        _f_rwkv_pre, pre_ins, pre_specs, [d_lw, d_kmod1, d_as, d_bs, d_bgate], pre_ct_specs, pre_kinds, (t // tmr,),
        "b_pre_bwd", ct_extra=[(1, d_kmod2)])
    d_zb_rkv, d_mu_rkv = _shift_bwd3(z, mu_rkv, d_r1, d_r2, d_zf_k, d_v1, d_v2, t)
    d_zf_s = jnp.concatenate([d_wdf, d_adf, d_gdf], axis=1)
    d_zs_b, d_mu_s = _tw_bwd(_f_tshift, [zs_b, mu_s], [_full_spec((t, 288)), _full_spec((1, 288))], [d_zf_s],
                             [_full_spec((t, 288))], ['tile', 'tile'], (1,), "b_shift_s_bwd")
    grads['b_shift_mu'] = jnp.concatenate([d_mu_rkv, d_mu_s], axis=1)

    d_aq, d_ak, d_av, d_beta_h, d_g_h = _scan_bwd(_delta_chunk, a_s0, a_inv, a_ins, a_offs, d_o, A_HEADS, nc,
                                                  (A_DK, A_DK), "a_scan_bwd")
    d_beta = d_beta_h.reshape(A_HEADS, t).T
    d_gdec = d_g_h.reshape(A_HEADS, t).T
    d_abeta, d_aalpha, grads['a_log_rate'], grads['a_dt_bias'] = _tw_bwd(
        dg_fn, [abeta, aalpha, w['a_log_rate'], w['a_dt_bias']], dg_specs, [d_beta, d_gdec],
        [_row_spec(tmg, A_HEADS)] * 2, ['tile', 'tile', 'acc', 'acc'], (t // tmg,), "a_gates_bwd", with_pid=True)
    d_zqkv, d_conv = [], []
    for idx, (fn, ct) in enumerate(zip(conv_fns, (d_aq, d_ak, d_av))):
        dz_i, dw_i = _conv_bwd(fn, z, conv_w, ct, idx, t)
        d_zqkv.append(dz_i)
        d_conv.append(dw_i)
    grads['a_conv_w'] = jnp.concatenate(d_conv, axis=1)

    d_small = jnp.concatenate([d_zs_b, d_abeta, d_aalpha, jnp.zeros((t, ZP - 9216 - 304), f32)], axis=1)
    d_small = lax.optimization_barrier(d_small.astype(MXU_DTYPE))
    d_z_parts = d_zqkv + [d_az, d_zb_rkv, d_ga, d_gb, d_small]
    d_z = jnp.concatenate([p.astype(MXU_DTYPE) for p in d_z_parts], axis=1)
    grads['w_in_p'] = _matmul(u, d_z, ta=True, name="d_w_in")
    d_u = _matmul(d_z, w['w_in_p'], tb=True, name="d_u")
    d_h1, grads['mix_norm'] = _rms_bwd(h1, w['mix_norm'], d_u, d_h2, "mix_drms")
    d_h0, grads['ffn1_norm'], grads['ffn1_wgu'], grads['ffn1_wd'] = _ffn_bwd(
        h0, w['ffn1_norm'], w['ffn1_wgu'], w['ffn1_wd'], ffn1_saved, d_h1, "ffn1")
    return loss, d_h0, grads


_WIN_SEGMENTS = ((0, 4096), (4112, 7184), (7472, 9520), (7184, 7472), (4096, 4112))


_WIN_SHARD = IN_TOTAL // N_CHIPS


def _win_pieces():
    pieces, pad_at = [], 0
    for a, b in _WIN_SEGMENTS:
        c = a
        while c < b:
            stop = min(b, (c // _WIN_SHARD + 1) * _WIN_SHARD)
            pieces.append((c, pad_at + c - a, stop - c))
            c = stop
        pad_at += b - a
    return pieces


def _win_shards_to_padded(shards):
    parts = [shards[c // _WIN_SHARD][:, c % _WIN_SHARD:c % _WIN_SHARD + n] for c, _, n in _win_pieces()]
    parts.append(jnp.zeros((shards.shape[1], ZP - IN_TOTAL), shards.dtype))
    return jnp.concatenate(parts, axis=1)


def _win_padded_to_shards(w_p):
    by_shard = [[] for _ in range(N_CHIPS)]
    for c, p, n in sorted(_win_pieces()):
        by_shard[c // _WIN_SHARD].append(w_p[:, p:p + n])
    return jnp.stack([jnp.concatenate(parts, axis=1) for parts in by_shard])


def _loss_and_grad(loss_fn, h, gain, tgt, specs, tm):
    t = h.shape[0]
    n = t // tm

    def body(h_ref, g_ref, t_ref, l_ref, dh_ref, dg_ref):
        pid = pl.program_id(0)
        tg = t_ref[...]
        (part,), vjp = jax.vjp(lambda a, b: loss_fn(pid, a, b, tg), h_ref[...], g_ref[...])
        dh, dg = vjp((jnp.ones_like(part),))
        l_ref[...] = part
        dh_ref[...] = dh

        @pl.when(pid == 0)
        def _():
            dg_ref[...] = dg

        @pl.when(pid != 0)
        def _():
            dg_ref[...] += dg

    return pl.pallas_call(
        body, name="loss", grid=(n,), in_specs=specs,
        out_specs=[pl.BlockSpec((None, 1, 1), lambda i: (i, 0, 0)), specs[0], _full_spec(gain.shape)],
        out_shape=[jax.ShapeDtypeStruct((n, 1, 1), f32), jax.ShapeDtypeStruct(h.shape, f32),
                   jax.ShapeDtypeStruct(gain.shape, f32)],
    )(h, gain, tgt)


def _shift_bwd3(z, mu, d_r1, d_r2, d_k, d_v1, d_v2, t):
    nb = D // LANES

    def body(z_ref, mu_ref, r1, r2, kk, v1, v2, dz_ref, dmu_ref):
        j = pl.program_id(0)
        ct = jnp.where(j < nb, r1[...] + r2[...], jnp.where(j < 2 * nb, kk[...], v1[...] + v2[...]))
        _, vjp = jax.vjp(lambda a, b: _f_tshift(a, b), z_ref[...], mu_ref[...])
        dz, dmu = vjp((ct,))
        dz_ref[...] = dz.astype(dz_ref.dtype)
        dmu_ref[...] = dmu

    def window(first):
        return pl.BlockSpec((t, LANES), lambda j, f=first: (0, jnp.clip(j - f * nb, 0, nb - 1)))

    return pl.pallas_call(
        body, name="b_shift_bwd", grid=(3 * nb,),
        in_specs=[_col_spec(t, 32), pl.BlockSpec((1, LANES), lambda j: (0, j)), window(0), window(0), window(1),
                  window(2), window(2)],
        out_specs=[_col_spec(t, 0), pl.BlockSpec((1, LANES), lambda j: (0, j))],
        out_shape=[jax.ShapeDtypeStruct((t, 3 * D), MXU_DTYPE), jax.ShapeDtypeStruct((1, 3 * D), f32)],
    )(z, mu, d_r1, d_r2, d_k, d_v1, d_v2)


def _conv_bwd(fn, z, conv_w, ct, idx, t):
    def body(z_ref, w_ref, ct_ref, dz_ref, dw_ref):
        _, vjp = jax.vjp(lambda a, b: fn(a, b), z_ref[...], w_ref[...])
        dz, dw = vjp((ct_ref[...],))
        dz_ref[...] = dz.astype(dz_ref.dtype)
        dw_ref[...] = dw

    return pl.pallas_call(
        body, name=f"a_conv{idx}_bwd", grid=(A_HEADS,),
        in_specs=[_col_spec(t, 8 * idx), pl.BlockSpec((4, LANES), lambda j, o=8 * idx: (0, j + o)), _col_spec(t, 0)],
        out_specs=[_col_spec(t, 0), pl.BlockSpec((4, LANES), lambda j: (0, j))],
        out_shape=[jax.ShapeDtypeStruct((t, D), MXU_DTYPE), jax.ShapeDtypeStruct((4, D), f32)],
    )(z, conv_w, ct)


def _position():
    return lax.axis_index("x"), lax.axis_index("y"), lax.axis_index("c")


def _flip(v, f):
    return 1 - v if f else v


_CHIP_FLIPS = ((1, 0), (0, 1), (1, 1))


def _gather_chips(arrs, name):
    n = len(arrs)
    assert all(a.shape[0] % 32 == 0 for a in arrs)
    arrs = [a.reshape(2, a.shape[0] // 2, a.shape[1]) for a in arrs]

    def body(*refs):
        ins, outs = refs[:n], refs[n:2 * n]
        send, recv, fsend, frecv, own = refs[2 * n:]
        x, y, c = _position()
        me = 2 * x + y
        sends, plan, owns = [], [], []
        for a in range(n):
            cp = pltpu.make_async_remote_copy(src_ref=ins[a], dst_ref=outs[a].at[me], send_sem=own.at[a, 0],
                                              recv_sem=own.at[a, 1], device_id=(x, y, 1 - c), device_id_type=MESH)
            cp.start()
            owns.append(cp)
            for j, (fx, fy) in enumerate(_CHIP_FLIPS):
                px, py = _flip(x, fx), _flip(y, fy)
                p = 2 * px + py
                cp = pltpu.make_async_remote_copy(src_ref=ins[a].at[c], dst_ref=outs[a].at[me, c],
                                                  send_sem=send.at[a, j], recv_sem=recv.at[a, j],
                                                  device_id=(px, py, c), device_id_type=MESH)
                cp.start()
                sends.append(cp)
                landed = pltpu.make_async_remote_copy(src_ref=ins[a].at[c], dst_ref=outs[a].at[p, c],
                                                      send_sem=send.at[a, j], recv_sem=recv.at[a, j],
                                                      device_id=(px, py, c), device_id_type=MESH)
                onward = pltpu.make_async_remote_copy(src_ref=outs[a].at[p, c], dst_ref=outs[a].at[p, c],
                                                      send_sem=fsend.at[a, j], recv_sem=frecv.at[a, j],
                                                      device_id=(x, y, 1 - c), device_id_type=MESH)
                from_sibling = pltpu.make_async_remote_copy(src_ref=outs[a].at[p, 1 - c], dst_ref=outs[a].at[p, 1 - c],
                                                            send_sem=fsend.at[a, j], recv_sem=frecv.at[a, j],
                                                            device_id=(x, y, 1 - c), device_id_type=MESH)
                plan.append((landed, onward, from_sibling))
        for landed, onward, _ in plan:
            landed.wait_recv()
            onward.start()
        for _, _, from_sibling in plan:
            from_sibling.wait_recv()
        for cp in sends:
            cp.wait_send()
        for _, onward, _ in plan:
            onward.wait_send()
        for cp in owns:
            cp.wait()

    sems = [pltpu.SemaphoreType.DMA((n, 3))] * 4 + [pltpu.SemaphoreType.DMA((n, 2))]
    outs = pl.pallas_call(
        body, name=name, in_specs=[ANY] * n, out_specs=[ANY] * n,
        out_shape=[jax.ShapeDtypeStruct((N_CHIPS,) + a.shape, a.dtype) for a in arrs], scratch_shapes=sems,
    )(*arrs)
    return [o.reshape(N_CHIPS, o.shape[1] * o.shape[2], o.shape[3]) for o in outs]


def _swap_sibling(arrs, src_of, shapes, name):
    n = len(arrs)

    def body(*refs):
        a_refs, got_refs = refs[:n], refs[n:2 * n]
        send, recv = refs[2 * n:]
        x, y, c = _position()
        copies = []
        for i in range(n):
            cp = pltpu.make_async_remote_copy(src_ref=src_of(a_refs[i], c), dst_ref=got_refs[i], send_sem=send.at[i],
                                              recv_sem=recv.at[i], device_id=(x, y, 1 - c), device_id_type=MESH)
            cp.start()
            copies.append(cp)
        for cp in copies:
            cp.wait()

    return pl.pallas_call(body, name=name, in_specs=[ANY] * n, out_specs=[ANY] * n,
                          out_shape=[jax.ShapeDtypeStruct(sh, a.dtype) for sh, a in zip(shapes, arrs)],
                          scratch_shapes=[pltpu.SemaphoreType.DMA((n,))] * 2)(*arrs)


def _row_tile(rows, width):
    return _tile(rows, max(16, (784 * LANES // width) // 16 * 16), 16)


def _add_halves(g, got, dtype, name):
    n, _, hr, w = g.shape
    tr = _row_tile(hr, w)

    def body(g_ref, got_ref, o_ref):
        c = lax.axis_index("c")
        own = jnp.where(c == 0, g_ref[:, 0], g_ref[:, 1])
        o_ref[...] = (own + got_ref[...]).astype(dtype)

    return pl.pallas_call(
        body, name=name, grid=(hr // tr,),
        in_specs=[pl.BlockSpec((n, 2, tr, w), lambda i: (0, 0, i, 0)), pl.BlockSpec((n, tr, w), lambda i: (0, i, 0))],
        out_specs=pl.BlockSpec((n, tr, w), lambda i: (0, i, 0)),
        out_shape=jax.ShapeDtypeStruct((n, hr, w), dtype))(g, got)


def _scatter_chips(gs, name):
    n = len(gs)

    def body(*refs):
        g_refs, out_refs = refs[:n], refs[n:2 * n]
        send, recv = refs[2 * n:]
        x, y, c = _position()
        sends = []
        for i in range(n):
            for j, (fx, fy) in enumerate(_CHIP_FLIPS):
                px, py = _flip(x, fx), _flip(y, fy)
                cp = pltpu.make_async_remote_copy(src_ref=g_refs[i].at[2 * px + py], dst_ref=out_refs[i].at[j],
                                                  send_sem=send.at[i, j], recv_sem=recv.at[i, j],
                                                  device_id=(px, py, c), device_id_type=MESH)
                cp.start()
                sends.append(cp)
        for cp in sends:
            cp.wait_recv()
        for cp in sends:
            cp.wait_send()

    return pl.pallas_call(
        body, name=name, in_specs=[ANY] * n, out_specs=[ANY] * n,
        out_shape=[jax.ShapeDtypeStruct((3,) + g.shape[1:], g.dtype) for g in gs],
        scratch_shapes=[pltpu.SemaphoreType.DMA((n, 3)), pltpu.SemaphoreType.DMA((n, 3))],
    )(*gs)


def _sum_own_and_slots(own, got, name):
    n, r, w = own.shape
    tr = _row_tile(r, w)

    def body(own_ref, got_ref, o_ref):
        me = 2 * lax.axis_index("x") + lax.axis_index("y")
        acc = own_ref[0]
        for i in range(1, n):
            acc = jnp.where(me == i, own_ref[i], acc)
        acc = acc.astype(f32)
        for j in range(3):
            acc = acc + got_ref[j].astype(f32)
        o_ref[...] = acc

    return pl.pallas_call(
        body, name=name, grid=(r // tr,),
        in_specs=[pl.BlockSpec((n, tr, w), lambda i: (0, i, 0)), pl.BlockSpec((3, tr, w), lambda i: (0, i, 0))],
        out_specs=pl.BlockSpec((tr, w), lambda i: (i, 0)), out_shape=jax.ShapeDtypeStruct((r, w), f32))(own, got)


def _share_chips(a, name):
    def body(a_ref, out_ref, send, recv):
        x, y, c = _position()
        sends = []
        for j, (fx, fy) in enumerate(_CHIP_FLIPS):
            cp = pltpu.make_async_remote_copy(src_ref=a_ref, dst_ref=out_ref.at[j], send_sem=send.at[j],
                                              recv_sem=recv.at[j], device_id=(_flip(x, fx), _flip(y, fy), c),
                                              device_id_type=MESH)
            cp.start()
            sends.append(cp)
        for cp in sends:
            cp.wait_recv()
        for cp in sends:
            cp.wait_send()

    return pl.pallas_call(
        body, name=name, in_specs=[ANY], out_specs=ANY, out_shape=jax.ShapeDtypeStruct((3,) + a.shape, a.dtype),
        scratch_shapes=[pltpu.SemaphoreType.DMA((3,)), pltpu.SemaphoreType.DMA((3,))],
    )(a)


def _sum_in_chip_order(pair, got, name):
    r, w = pair.shape
    tr = _tile(r, 1408, 8)

    def body(p_ref, g_ref, o_ref):
        x, y = lax.axis_index("x"), lax.axis_index("y")
        me = 2 * x + y
        across = [2 * _flip(x, fx) + _flip(y, fy) for fx, fy in _CHIP_FLIPS]
        acc = None
        for i in range(N_CHIPS):
            term = p_ref[...]
            for j in range(3):
                term = jnp.where(across[j] == i, g_ref[j], term)
            acc = term if acc is None else acc + term
        o_ref[...] = acc

    return pl.pallas_call(
        body, name=name, grid=(r // tr,),
        in_specs=[pl.BlockSpec((tr, w), lambda i: (i, 0)), pl.BlockSpec((3, tr, w), lambda i: (0, i, 0))],
        out_specs=pl.BlockSpec((tr, w), lambda i: (i, 0)), out_shape=jax.ShapeDtypeStruct((r, w), f32))(pair, got)


def _add2(a, b, name):
    r, w = a.shape
    tr = _tile(r, 1408, 8)
    spec = pl.BlockSpec((tr, w), lambda i: (i, 0))

    def body(a_ref, b_ref, o_ref):
        o_ref[...] = a_ref[...] + b_ref[...]

    return pl.pallas_call(body, name=name, grid=(r // tr,), in_specs=[spec, spec], out_specs=spec,
                          out_shape=jax.ShapeDtypeStruct(a.shape, f32))(a, b)


def _adamw(w, g_parts, m, v, name):
    shape = w.shape
    view = shape if len(shape) >= 2 else (1,) + shape
    assert all(d == 1 for d in view[:-2]), shape
    rows, cols = view[-2:]
    cap = max(8, (256 * 1024 // cols) // 8 * 8)
    tr = rows if rows <= cap else _tile(rows, cap, 8)
    lead = len(view) - 2
    n_g = len(g_parts)

    def body(*refs):
        w_ref = refs[0]
        g_refs = refs[1:1 + n_g]
        m_ref, v_ref, g_out, d_out, m_out, v_out = refs[1 + n_g:]
        g = g_refs[0][...]
        for gr in g_refs[1:]:
            g = g + gr[...]
        m_new = ADAM_B1 * m_ref[...] + (1.0 - ADAM_B1) * g
        v_new = ADAM_B2 * v_ref[...] + (1.0 - ADAM_B2) * (g * g)
        m_hat = m_new / (1.0 - ADAM_B1 ** ADAM_STEP)
        v_hat = v_new / (1.0 - ADAM_B2 ** ADAM_STEP)
        g_out[...] = g
        d_out[...] = -ADAM_LR * (m_hat / (jnp.sqrt(v_hat) + ADAM_EPS) + ADAM_WD * w_ref[...])
        m_out[...] = m_new
        v_out[...] = v_new

    spec = pl.BlockSpec((None,) * lead + (tr, cols), lambda i: (0,) * lead + (i, 0))
    args = [w.reshape(view)] + [g.reshape(view) for g in g_parts] + [m.reshape(view), v.reshape(view)]
    outs = pl.pallas_call(body, name=name, grid=(rows // tr,), in_specs=[spec] * len(args), out_specs=[spec] * 4,
                          out_shape=[jax.ShapeDtypeStruct(view, f32)] * 4)(*args)
    return [o.reshape(shape) for o in outs]


_BIG = ('ffn1_w_gu', 'ffn1_w_down', 'w_in', 'w_out', 'ffn2_w_gu', 'ffn2_w_down')
_SMALL_SHARDED = ('meta_tokens', 'a_conv_w', 'b_w_up', 'b_a_up', 'b_g_up')
_WEIGHTS = ('meta_tokens', 'ffn1_norm', 'ffn1_w_gu', 'ffn1_w_down', 'mix_norm', 'w_in', 'a_conv_w', 'a_log_rate',
            'a_dt_bias', 'a_out_norm', 'b_shift_mu', 'b_w0', 'b_w_up', 'b_a0', 'b_a_up', 'b_g_up', 'b_k_k', 'b_k_a',
            'b_r_k', 'b_ln_gain', 'b_ln_bias', 'w_out', 'ffn2_norm', 'ffn2_w_gu', 'ffn2_w_down', 'final_norm')
_SMALL = tuple(n for n in _WEIGHTS if n not in _BIG)


def _rows_of(shape):
    n = 1
    for d in shape:
        n *= d
    return n, -(-n // LANES)


def _pack(arrs, dtype, row_mult=32):
    parts, total = [], 0
    for a in arrs:
        n, rows = _rows_of(a.shape)
        flat = a.reshape(-1).astype(dtype)
        if n % LANES:
            flat = jnp.pad(flat, (0, rows * LANES - n))
        parts.append(flat)
        total += rows
    extra = -total % row_mult
    if extra:
        parts.append(jnp.zeros((extra * LANES,), dtype))
    return jnp.concatenate(parts).reshape(total + extra, LANES)


def _unpack(packed, shapes, lead=()):
    out, off = [], 0
    for sh in shapes:
        n, rows = _rows_of(sh)
        seg = packed[..., off:off + rows, :]
        if n % LANES:
            seg = seg.reshape(lead + (-1,))[..., :n]
        out.append(seg.reshape(lead + tuple(sh)))
        off += rows
    return out


def _cols_from_shards(s):
    return jnp.concatenate([s[i] for i in range(N_CHIPS)], axis=-1)


def kernel(x, meta_tokens, ffn1_norm, ffn1_w_gu, ffn1_w_down, mix_norm, w_in, a_conv_w, a_log_rate, a_dt_bias, a_out_norm, b_shift_mu, b_w0, b_w_up, b_a0, b_a_up, b_g_up, b_k_k, b_k_a, b_r_k, b_ln_gain, b_ln_bias, w_out, ffn2_norm, ffn2_w_gu, ffn2_w_down, final_norm, loss_target, m_meta_tokens, m_ffn1_norm, m_ffn1_w_gu, m_ffn1_w_down, m_mix_norm, m_w_in, m_a_conv_w, m_a_log_rate, m_a_dt_bias, m_a_out_norm, m_b_shift_mu, m_b_w0, m_b_w_up, m_b_a0, m_b_a_up, m_b_g_up, m_b_k_k, m_b_k_a, m_b_r_k, m_b_ln_gain, m_b_ln_bias, m_w_out, m_ffn2_norm, m_ffn2_w_gu, m_ffn2_w_down, m_final_norm, v_meta_tokens, v_ffn1_norm, v_ffn1_w_gu, v_ffn1_w_down, v_mix_norm, v_w_in, v_a_conv_w, v_a_log_rate, v_a_dt_bias, v_a_out_norm, v_b_shift_mu, v_b_w0, v_b_w_up, v_b_a0, v_b_a_up, v_b_g_up, v_b_k_k, v_b_k_a, v_b_r_k, v_b_ln_gain, v_b_ln_bias, v_w_out, v_ffn2_norm, v_ffn2_w_gu, v_ffn2_w_down, v_final_norm):
    args = locals()
    wts = {n: args[n] for n in _WEIGHTS}
    mom = {n: args["m_" + n] for n in _WEIGHTS}
    var = {n: args["v_" + n] for n in _WEIGHTS}
    chip = 2 * lax.axis_index("x") + lax.axis_index("y")

    big_shapes = [wts[n].shape[1:] for n in _BIG]
    small_shapes = [wts[n].shape[-2:] for n in _SMALL_SHARDED]
    big_flat = [wts[n].astype(bf16).reshape(wts[n].shape[1:]) for n in _BIG]
    small_packed = _pack([wts[n] for n in _SMALL_SHARDED], f32)
    gathered = _gather_chips(big_flat + [small_packed], "gather_weights")
    gu1, dn1, w_in_s, w_out_s, gu2, dn2 = [a.reshape((N_CHIPS,) + tuple(sh)) for a, sh in zip(gathered, big_shapes)]
    meta_s, conv_s, wup_s, aup_s, gup_s = _unpack(gathered[-1], small_shapes, (N_CHIPS,))
    w = {
        'ffn1_norm': ffn1_norm, 'mix_norm': mix_norm, 'ffn2_norm': ffn2_norm, 'final_norm': final_norm[None, :],
        'ffn1_wgu': gu1, 'ffn1_wd': dn1.reshape(D_FF, D), 'ffn2_wgu': gu2, 'ffn2_wd': dn2.reshape(D_FF, D),
        'w_in_p': _win_shards_to_padded(w_in_s), 'w_out': w_out_s.reshape(D, D),
        'a_conv_w': _cols_from_shards(conv_s), 'b_w_up': _cols_from_shards(wup_s), 'b_a_up': _cols_from_shards(aup_s),
        'b_g_up': _cols_from_shards(gup_s),
        'a_log_rate': a_log_rate, 'a_dt_bias': a_dt_bias, 'a_out_norm': a_out_norm, 'b_shift_mu': b_shift_mu,
        'b_w0': b_w0, 'b_a0': b_a0, 'b_k_k': b_k_k, 'b_k_a': b_k_a, 'b_r_k': b_r_k, 'b_ln_gain': b_ln_gain,
        'b_ln_bias': b_ln_bias,
    }
    meta_full = _cols_from_shards(meta_s)

    h0 = jnp.concatenate([jnp.zeros((PAD, D), f32), meta_full, x[0]], axis=0)
    tgt = jnp.concatenate([jnp.zeros((SKIP, D), f32), loss_target[0]], axis=0)
    loss_local, d_h0, g = _local_step(h0, tgt, w)
    loss = lax.psum(loss_local, ("x", "y", "c"))
    grad_x = d_h0[SKIP:][None]

    big_grads = [
        g['ffn1_wgu'],
        g['ffn1_wd'].reshape(N_CHIPS, D_FF // N_CHIPS, D),
        _win_padded_to_shards(g['w_in_p']),
        g['w_out'].reshape(N_CHIPS, D // N_CHIPS, D),
        g['ffn2_wgu'],
        g['ffn2_wd'].reshape(N_CHIPS, D_FF // N_CHIPS, D),
    ]
    g_halves = [a.reshape(N_CHIPS, 2, a.shape[1] // 2, a.shape[2]) for a in big_grads]
    sib_halves = _swap_sibling(g_halves, lambda ref, c: ref.at[:, 1 - c], [a.shape[:1] + a.shape[2:] for a in g_halves],
                               "swap_halves")
    chip_halves = [_add_halves(a, b, bf16, f"add_sibling{i}") for i, (a, b) in enumerate(zip(g_halves, sib_halves))]
    got = _scatter_chips(chip_halves, "scatter_grads")
    mine = [_sum_own_and_slots(a, b, f"sum_chips{i}") for i, (a, b) in enumerate(zip(chip_halves, got))]
    theirs = _swap_sibling(mine, lambda ref, c: ref, [a.shape for a in mine], "swap_sums")
    core = lax.axis_index("c")
    big_parts = [jnp.concatenate([jnp.where(core == 0, a, b), jnp.where(core == 0, b, a)], axis=0)
                 for a, b in zip(mine, theirs)]

    small_full = {
        'meta_tokens': d_h0[PAD:SKIP], 'ffn1_norm': g['ffn1_norm'], 'mix_norm': g['mix_norm'], 'a_conv_w': g['a_conv_w'],
        'a_log_rate': g['a_log_rate'], 'a_dt_bias': g['a_dt_bias'], 'a_out_norm': g['a_out_norm'],
        'b_shift_mu': g['b_shift_mu'], 'b_w0': g['b_w0'], 'b_w_up': g['b_w_up'], 'b_a0': g['b_a0'], 'b_a_up': g['b_a_up'],
        'b_g_up': g['b_g_up'], 'b_k_k': g['b_k_k'], 'b_k_a': g['b_k_a'], 'b_r_k': g['b_r_k'], 'b_ln_gain': g['b_ln_gain'],
        'b_ln_bias': g['b_ln_bias'], 'ffn2_norm': g['ffn2_norm'], 'final_norm': g['final_norm'],
    }
    s_shapes = [small_full[n].shape for n in _SMALL]
    s_packed = _pack([small_full[n] for n in _SMALL], f32, row_mult=256)
    (s_sib,) = _swap_sibling([s_packed], lambda ref, c: ref, [s_packed.shape], "swap_small")
    s_pair = _add2(s_packed, s_sib, "add_small")
    s_sum = _sum_in_chip_order(s_pair, _share_chips(s_pair, "share_small"), "sum_small")
    s_parts = dict(zip(_SMALL, _unpack(s_sum, s_shapes)))

    grad, delta, new_m, new_v = {}, {}, {}, {}
    for n, a in zip(_BIG, big_parts):
        grad[n], delta[n], new_m[n], new_v[n] = _adamw(wts[n], [a.reshape(wts[n].shape)], mom[n], var[n], f"adamw_{n}")
    for n in _SMALL:
        gs = s_parts[n]
        if n in _SMALL_SHARDED:
            width = wts[n].shape[-1]
            gs = lax.dynamic_slice_in_dim(gs, chip * width, width, axis=gs.ndim - 1)
        gs = gs.reshape(wts[n].shape)
        grad[n], delta[n], new_m[n], new_v[n] = _adamw(wts[n], [gs], mom[n], var[n], f"adamw_{n}")

    return (loss, grad_x, *[grad[n] for n in _WEIGHTS], *[delta[n] for n in _WEIGHTS],
            *[new_m[n] for n in _WEIGHTS], *[new_v[n] for n in _WEIGHTS])
```

```python
import functools

import jax
import jax.numpy as jnp
from jax import lax
from jax.experimental import pallas as pl
from jax.experimental.pallas import tpu as pltpu

f32 = jnp.float32
bf16 = jnp.bfloat16
MESH = pl.DeviceIdType.MESH
ANY = pl.BlockSpec(memory_space=pl.ANY)

D = 1024
N_META = 16
CHUNK = 64
PAD = CHUNK - N_META
SKIP = PAD + N_META
EPS = 1e-6
D_FF = 2816
A_HEADS = 8
A_DK = 128
B_HEADS = 16
B_N = 64
B_GN_EPS = B_N * 1e-5
IN_TOTAL = 9520
ZP = 9600
LANES = 128
N_CHIPS = 4

ADAM_LR, ADAM_B1, ADAM_B2, ADAM_EPS, ADAM_WD, ADAM_STEP = 0.001, 0.9, 0.999, 1e-08, 0.01, 10

MXU_DTYPE = bf16


def _tile(n, cap, mult):
    if n <= cap:
        return n
    best = None
    for t in range(mult, cap + 1, mult):
        if n % t == 0:
            best = t
    assert best is not None, (n, cap, mult)
    return best


def _sigmoid(x):
    return jax.nn.sigmoid(x)


def _silu(x):
    return x * jax.nn.sigmoid(x)


def _softplus(x):
    return jnp.maximum(x, 0.0) + jnp.log(1.0 + jnp.exp(-jnp.abs(x)))


def _head_matrix(c, nh):
    hd = c // nh
    r = lax.broadcasted_iota(jnp.int32, (c, nh), 0)
    h = lax.broadcasted_iota(jnp.int32, (c, nh), 1)
    return (r >= h * hd) & (r < (h + 1) * hd)


def _dot_exact_rhs(x, e, cb):
    dn = (((1,), (cb,)), ((), ()))
    eb = e.astype(bf16)
    hi = x.astype(bf16)
    lo = (x - hi.astype(f32)).astype(bf16)
    return (lax.dot_general(hi, eb, dn, preferred_element_type=f32)
            + lax.dot_general(lo, eb, dn, preferred_element_type=f32))


def _head_sum_impl(x, nh):
    e = _head_matrix(x.shape[-1], nh)
    return _dot_exact_rhs(_dot_exact_rhs(x, e, 0), e, 1)


@functools.partial(jax.custom_vjp, nondiff_argnums=(1,))
def _head_sum(x, nh):
    return _head_sum_impl(x, nh)


def _head_sum_fwd(x, nh):
    return _head_sum_impl(x, nh), None


def _head_sum_bwd(nh, _, g):
    return (_head_sum_impl(g, nh),)


_head_sum.defvjp(_head_sum_fwd, _head_sum_bwd)


@functools.partial(jax.custom_vjp, nondiff_argnums=(1,))
def _shift_rows(x, s):
    n = x.shape[0]
    row = lax.broadcasted_iota(jnp.int32, x.shape, 0)
    if s > 0:
        return jnp.where(row >= s, pltpu.roll(x, s, 0), 0.0)
    return jnp.where(row < n + s, pltpu.roll(x, n + s, 0), 0.0)


def _shift_rows_fwd(x, s):
    return _shift_rows(x, s), None


def _shift_rows_bwd(s, _, g):
    return (_shift_rows(g, -s),)


_shift_rows.defvjp(_shift_rows_fwd, _shift_rows_bwd)


def _matmul(a, b, *, ta=False, tb=False, res=None, scale=1.0, name, b_cols_split=None, out_cols_split=None,
            out_into=None, cols_outer=False):
    assert not (ta and tb)
    (ar, ac) = a.shape
    b0 = 0
    if b_cols_split:
        b0, bs = b_cols_split
        _, br, bc_part = b.shape
        bc = bs * bc_part
    else:
        br, bc = b.shape
    m, k = (ac, ar) if ta else (ar, ac)
    n, kb = (br, bc) if tb else (bc, br)
    assert k == kb, (a.shape, b.shape, ta, tb)
    tm = _tile(m, 1408, LANES) if ta else _tile(m, 1040, 16)
    tn = _tile(n, 1920, LANES)
    tk = _tile(k, 1040, 8) if ta else _tile(k, 1920, LANES)
    nk = k // tk
    dn = (((0 if ta else 1,), (1 if tb else 0,)), ((), ()))
    if b_cols_split:
        assert (tk if tb else tn) == bc_part, (b.shape, tn, tk)

    def body(*refs):
        a_ref, b_ref = refs[:2]
        r_ref = refs[2] if res is not None else None
        o_ref, acc = refs[-2:]
        kk = pl.program_id(2)

        @pl.when(kk == 0)
        def _():
            acc[...] = jnp.zeros_like(acc)

        acc[...] += lax.dot_general(a_ref[...].astype(MXU_DTYPE), b_ref[...].astype(MXU_DTYPE), dn,
                                    preferred_element_type=f32)

        @pl.when(kk == nk - 1)
        def _():
            out = acc[...]
            if scale != 1.0:
                out = out * scale
            if res is not None:
                out = r_ref[...] + out
            o_ref[...] = out

    if ta:
        a_spec = pl.BlockSpec((tk, tm), lambda i, j, kk: (kk, i))
    else:
        a_spec = pl.BlockSpec((tm, tk), lambda i, j, kk: (i, kk))
    if tb and b_cols_split:
        b_spec = pl.BlockSpec((None, tn, tk), lambda i, j, kk: (kk + b0, j, 0))
    elif tb:
        b_spec = pl.BlockSpec((tn, tk), lambda i, j, kk: (j, kk))
    elif b_cols_split:
        b_spec = pl.BlockSpec((None, tk, tn), lambda i, j, kk: (j + b0, kk, 0))
    else:
        b_spec = pl.BlockSpec((tk, tn), lambda i, j, kk: (kk, j))
    in_specs = [a_spec, b_spec]
    args = [a, b]
    if res is not None:
        in_specs.append(pl.BlockSpec((tm, tn), lambda i, j, kk: (i, j)))
        args.append(res)
    aliases = {}
    if out_cols_split:
        o0, total = out_cols_split
        out_spec = pl.BlockSpec((None, tm, tn), lambda i, j, kk: (j + o0, i, 0))
        out_shape = jax.ShapeDtypeStruct((total, m, tn), f32)
        if out_into is not None:
            assert out_into.shape == out_shape.shape
            in_specs.append(ANY)
            args.append(out_into)
            aliases = {len(args) - 1: 0}
    else:
        out_spec = pl.BlockSpec((tm, tn), lambda i, j, kk: (i, j))
        out_shape = jax.ShapeDtypeStruct((m, n), f32)
    grid = (m // tm, n // tn, nk)
    if cols_outer:
        def swapped(spec):
            return pl.BlockSpec(spec.block_shape, lambda g0, g1, kk, f=spec.index_map: f(g1, g0, kk))
        in_specs = [swapped(sp) for sp in in_specs]
        out_spec = swapped(out_spec)
        grid = (n // tn, m // tm, nk)
    return pl.pallas_call(
        body, name=name, grid=grid, in_specs=in_specs, out_specs=out_spec, out_shape=out_shape,
        scratch_shapes=[pltpu.VMEM((tm, tn), f32)], input_output_aliases=aliases,
        compiler_params=pltpu.CompilerParams(dimension_semantics=("parallel", "parallel", "arbitrary")),
    )(*args)


def _tw_fwd(fn, ins, in_specs, out_shapes, out_specs, grid, name, with_pid=False):
    n_in = len(ins)

    def body(*refs):
        vals = [r[...] for r in refs[:n_in]]
        outs = fn(pl.program_id(0), *vals) if with_pid else fn(*vals)
        for r, o in zip(refs[n_in:], outs):
            r[...] = o.astype(r.dtype)

    return pl.pallas_call(body, name=name, grid=grid, in_specs=in_specs, out_specs=out_specs,
                          out_shape=out_shapes)(*ins)


def _tw_bwd(fn, ins, in_specs, cts, ct_specs, kinds, grid, name, with_pid=False, tile_dtype=f32, ct_extra=(),
            residual=None):
    n_in, n_ct = len(ins), len(cts)
    diff = [i for i, kd in enumerate(kinds) if kd is not None]
    n_ex = len(ct_extra)

    def body(*refs):
        vals = [r[...] for r in refs[:n_in]]
        ctv = [r[...].astype(f32) for r in refs[n_in:n_in + n_ct]]
        for (ci, _), r in zip(ct_extra, refs[n_in + n_ct:n_in + n_ct + n_ex]):
            ctv[ci] = ctv[ci] + r[...]
        ctv = tuple(ctv)
        n_fixed = n_in + n_ct + n_ex
        res_ref = refs[n_fixed] if residual is not None else None
        g_refs = refs[n_fixed + (residual is not None):]
        pid = pl.program_id(0)

        def f(*dv):
            full = list(vals)
            for i, v in zip(diff, dv):
                full[i] = v
            out = fn(pid, *full) if with_pid else fn(*full)
            return tuple(out)

        _, vjp = jax.vjp(f, *[vals[i] for i in diff])
        gs = vjp(ctv)
        first = pid == 0
        for i2 in range(1, len(grid)):
            first = first & (pl.program_id(i2) == 0)
        for i, g, g_ref in zip(diff, gs, g_refs):
            if kinds[i] != 'acc':
                if i == 0 and res_ref is not None:
                    g = res_ref[...] + g
                g_ref[...] = g.astype(g_ref.dtype)
            else:
                @pl.when(first)
                def _(g=g, g_ref=g_ref):
                    g_ref[...] = g

                @pl.when(jnp.logical_not(first))
                def _(g=g, g_ref=g_ref):
                    g_ref[...] += g

    zero_map = {1: lambda *a: (0,), 2: lambda *a: (0, 0), 3: lambda *a: (0, 0, 0)}
    out_specs, out_shapes = [], []
    for i in diff:
        if kinds[i] == 'tile':
            out_shapes.append(jax.ShapeDtypeStruct(ins[i].shape, tile_dtype))
            out_specs.append(in_specs[i])
        elif kinds[i] == 'acc':
            out_shapes.append(jax.ShapeDtypeStruct(ins[i].shape, f32))
            out_specs.append(pl.BlockSpec(ins[i].shape, zero_map[ins[i].ndim]))
        else:
            out_shapes.append(jax.ShapeDtypeStruct(kinds[i][1], kinds[i][3] if len(kinds[i]) > 3 else tile_dtype))
            out_specs.append(kinds[i][2])
    extra_specs = [ct_specs[ci] for ci, _ in ct_extra]
    extra = [a for _, a in ct_extra]
    if residual is not None:
        assert kinds[0] == 'tile'
        extra_specs.append(in_specs[0])
        extra.append(residual)
    return pl.pallas_call(body, name=name, grid=grid, in_specs=list(in_specs) + list(ct_specs) + extra_specs,
                          out_specs=out_specs, out_shape=out_shapes)(*ins, *cts, *extra)


def _row_spec(tm, c, col_block=0):
    return pl.BlockSpec((tm, c), lambda i, cb=col_block: (i, cb))


def _full_spec(shape):
    nd = len(shape)
    return pl.BlockSpec(shape, lambda *a, nd=nd: (0,) * nd)


def _f_rms(x, g):
    return (x * lax.rsqrt(jnp.mean(x * x, axis=-1, keepdims=True) + EPS) * g,)


def _f_swiglu(gate, up):
    return (_silu(gate) * up,)


def _f_loss(pid, h, g, tgt, *, tm):
    y = h * lax.rsqrt(jnp.mean(h * h, axis=-1, keepdims=True) + EPS) * g
    row = pid * tm + lax.broadcasted_iota(jnp.int32, (tm, 1), 0)
    err = jnp.where(row >= SKIP, y - tgt, 0.0)
    per_row = jnp.mean(err * err, axis=-1, keepdims=True)
    return (0.5 * jnp.sum(per_row, axis=0, keepdims=True),)


def _f_conv(x, w, *, norm, scale):
    y = x * w[3:4, :]
    for s in (1, 2, 3):
        y = y + _shift_rows(x, s) * w[3 - s:4 - s, :]
    y = _silu(y)
    if norm:
        y = y * lax.rsqrt(jnp.sum(y * y, axis=-1, keepdims=True) + 1e-6) * scale
    return (y,)


def _f_dgates(pid, abeta, aalpha, log_rate, dt_bias, *, tm):
    row = pid * tm + lax.broadcasted_iota(jnp.int32, (tm, 1), 0)
    live = row >= PAD
    beta = jnp.where(live, _sigmoid(abeta), 0.0)
    g = jnp.where(live, -jnp.exp(log_rate) * _softplus(aalpha + dt_bias), 0.0)
    return beta, g


def _f_tshift(z, mu):
    return (z + (_shift_rows(z, 1) - z) * mu,)


def _f_rwkv_pre(k, wd, ad, gd, w0, w_up, a0, a_up, g_up, k_k, k_a):
    w_log = -_softplus(-(w0 + _smm(jnp.tanh(wd), w_up, 1))) - 0.5
    lw = -jnp.exp(w_log)
    a_lr = _sigmoid(a0 + _smm(ad, a_up, 1))
    gate = _smm(_sigmoid(gd), g_up, 1)
    kkp = k * k_k
    kk = kkp * lax.rsqrt(_head_sum(kkp * kkp, B_HEADS) + 1e-6)
    kmod = k * (1.0 + (a_lr - 1.0) * k_a)
    return lw, kmod, -kk, kk * a_lr, gate


def _f_mix_post(o, az, y, r, kmod, v, gate, ga, gb, out_gain, ln_g, ln_b, r_k):
    ms = _head_sum(o * o, A_HEADS) * (1.0 / A_DK)
    oa = o * lax.rsqrt(ms + EPS) * out_gain * _silu(az)
    mean = _head_sum(y, B_HEADS) * (1.0 / B_N)
    yc = y - mean
    var = _head_sum(yc * yc, B_HEADS) * (1.0 / B_N)
    yn = yc * lax.rsqrt(var + B_GN_EPS) * ln_g + ln_b
    bonus = _head_sum(r * kmod * r_k, B_HEADS) * v
    ob = (yn + bonus) * gate
    return (_sigmoid(ga) * oa + _sigmoid(gb) * ob,)


def _split2(a):
    hi = a.astype(bf16)
    return hi, (a - hi.astype(f32)).astype(bf16)


def _dot_passes(a, b, ca, cb, passes):
    dn = (((ca,), (cb,)), ((), ()))
    if passes == 1:
        return lax.dot_general(a.astype(bf16), b.astype(bf16), dn, preferred_element_type=f32)
    ah, al = _split2(a)
    bh, bl = _split2(b)
    return (lax.dot_general(ah, bh, dn, preferred_element_type=f32)
            + (lax.dot_general(ah, bl, dn, preferred_element_type=f32)
               + lax.dot_general(al, bh, dn, preferred_element_type=f32)))


@functools.partial(jax.custom_vjp, nondiff_argnums=(2, 3, 4))
def _sdot(a, b, ca, cb, passes):
    return _dot_passes(a, b, ca, cb, passes)


def _sdot_fwd(a, b, ca, cb, passes):
    return _dot_passes(a, b, ca, cb, passes), (a, b)


def _sdot_bwd(ca, cb, passes, res, g):
    a, b = res
    if (ca, cb) == (1, 0):
        return _dot_passes(g, b, 1, 1, passes), _dot_passes(a, g, 0, 0, passes)
    if (ca, cb) == (1, 1):
        return _dot_passes(g, b, 1, 0, passes), _dot_passes(g, a, 0, 0, passes)
    assert (ca, cb) == (0, 0)
    return _dot_passes(b, g, 1, 1, passes), _dot_passes(a, g, 1, 0, passes)


_sdot.defvjp(_sdot_fwd, _sdot_bwd)


def _smm(a, b, passes=3):
    return _sdot(a, b, 1, 0, passes)


def _smm_nt(a, b, passes=3):
    return _sdot(a, b, 1, 1, passes)


def _smm_tn(a, b, passes=3):
    return _sdot(a, b, 0, 0, passes)


def _tri_dot(x, ca):
    n = x.shape[0]
    incl = _tri_masks(n)[0]
    dn = (((ca,), (0,)), ((), ()))
    tri = incl.astype(bf16)
    hi, r1 = x.astype(bf16), None
    r1 = x - hi.astype(f32)
    mid = r1.astype(bf16)
    lo = (r1 - mid.astype(f32)).astype(bf16)
    return (lax.dot_general(tri, hi, dn, preferred_element_type=f32)
            + (lax.dot_general(tri, mid, dn, preferred_element_type=f32)
               + lax.dot_general(tri, lo, dn, preferred_element_type=f32)))


@jax.custom_vjp
def _cumsum_rows(x):
    return _tri_dot(x, 1)


def _cumsum_rows_fwd(x):
    return _tri_dot(x, 1), None


def _cumsum_rows_bwd(_, g):
    return (_tri_dot(g, 0),)


_cumsum_rows.defvjp(_cumsum_rows_fwd, _cumsum_rows_bwd)


def _tri_masks(n):
    i = lax.broadcasted_iota(jnp.int32, (n, n), 0)
    j = lax.broadcasted_iota(jnp.int32, (n, n), 1)
    return i >= j, i > j, i == j, i <= j


def _unit_lower_inv_impl(low, passes):
    n = low.shape[0]
    assert n == CHUNK
    _, _, eye, _ = _tri_masks(n)
    acc = eye.astype(f32) + low
    p = low
    for _ in range(5):
        p = _dot_passes(p, p, 1, 0, passes)
        acc = acc + _dot_passes(acc, p, 1, 0, passes)
    return acc


@functools.partial(jax.custom_vjp, nondiff_argnums=(1,))
def _unit_lower_inv(low, passes=3):
    return _unit_lower_inv_impl(low, passes)


def _unit_lower_inv_fwd(low, passes):
    t = _unit_lower_inv_impl(low, passes)
    return t, t


def _unit_lower_inv_bwd(passes, t, g):
    return (_dot_passes(_dot_passes(t, g, 0, 0, passes), t, 1, 1, passes),)


_unit_lower_inv.defvjp(_unit_lower_inv_fwd, _unit_lower_inv_bwd)


@functools.partial(jax.custom_vjp, nondiff_argnums=(2,))
def _unit_lower_inv_saved(low, t_saved, passes):
    return t_saved


def _unit_lower_inv_saved_fwd(low, t_saved, passes):
    return t_saved, t_saved


def _unit_lower_inv_saved_bwd(passes, t, g):
    return _unit_lower_inv_bwd(passes, t, g) + (jnp.zeros_like(t),)


_unit_lower_inv_saved.defvjp(_unit_lower_inv_saved_fwd, _unit_lower_inv_saved_bwd)


def _inverse(low, passes, saved):
    return _unit_lower_inv(low, passes) if saved is None else _unit_lower_inv_saved(low, saved, passes)

DELTA_PASSES = 1
DELTA_INV_PASSES = 1


def _delta_chunk(s, q, k, v, beta_row, g_row, inv_saved=None):
    p = DELTA_PASSES
    incl, strict, eye, upper = _tri_masks(CHUNK)
    beta = jnp.sum(jnp.where(eye, beta_row, 0.0), axis=1, keepdims=True)
    g = jnp.sum(jnp.where(eye, g_row, 0.0), axis=1, keepdims=True)
    gc = jnp.sum(jnp.where(incl, g_row, 0.0), axis=1, keepdims=True)
    gc_row = jnp.sum(jnp.where(upper, g, 0.0), axis=0, keepdims=True)
    decay = jnp.where(incl, jnp.exp(jnp.where(incl, gc - gc_row, 0.0)), 0.0)
    kb = k * beta
    vb = v * beta
    m = jnp.where(strict, _smm_nt(kb, k, p) * decay, 0.0)
    tinv = _inverse(-m, DELTA_INV_PASSES, inv_saved)
    u = _smm(tinv, vb, p)
    wk = _smm(tinv, kb * jnp.exp(gc), p)
    attn = _smm_nt(q, k, p) * decay
    qg = q * jnp.exp(gc)
    g_last = jnp.sum(g, axis=0, keepdims=True)
    k_tail = k * jnp.exp(g_last - gc)
    v_new = u - _smm(wk, s, p)
    o = _smm(qg, s, p) + _smm(attn, v_new, p)
    s_new = s * jnp.exp(g_last) + _smm_tn(k_tail, v_new, p)
    return o, s_new, tinv


RWKV_PASSES = 1
RWKV_INV_PASSES = 1


def _rwkv_chunk(st, r, k, v, a, b, lw, inv_saved=None):
    c = CHUNK
    p, pi = RWKV_PASSES, RWKV_INV_PASSES
    _, strict, _, _ = _tri_masks(c)
    lane = lax.broadcasted_iota(jnp.int32, (c, 2 * B_N), 1)
    row = lax.broadcasted_iota(jnp.int32, (c, 2 * B_N), 0)
    first = lane < B_N
    incl2 = row >= jnp.where(first, lane, lane - B_N)
    bi = lax.broadcasted_iota(jnp.int32, (2 * B_N, 2 * B_N), 0) < B_N
    bj = lax.broadcasted_iota(jnp.int32, (2 * B_N, 2 * B_N), 1) < B_N
    blockdiag = bi == bj
    cum = _cumsum_rows(lw)
    e_pos = jnp.exp(cum)
    e_neg = jnp.exp(-cum)
    rt = r * e_pos
    at = a * jnp.exp(cum - lw)
    kt = k * e_neg
    bt = b * e_neg
    bk = jnp.concatenate([bt, kt], axis=0)
    a_s0 = _smm_nt(at, st, p)
    r_s0 = _smm_nt(rt, st, p)
    heads = (first, jnp.logical_not(first))
    u = jnp.zeros((c, 2 * B_N), f32)
    invs = []
    for hi, sel in enumerate(heads):
        at_h = jnp.where(sel, at, 0.0)
        ab = jnp.where(strict, _smm_nt(at_h, bt, pi), 0.0)
        ak = jnp.where(strict, _smm_nt(at_h, kt, p), 0.0)
        t_h = _inverse(ab, pi, None if inv_saved is None else inv_saved[hi])
        invs.append(t_h)
        u = u + _smm(t_h, jnp.where(sel, a_s0, 0.0) + _smm(ak, jnp.where(sel, v, 0.0), p), p)
    y = r_s0
    for sel in heads:
        rbk = jnp.where(incl2, _smm_nt(jnp.where(sel, rt, 0.0), bk, p), 0.0)
        uv = jnp.concatenate([jnp.where(sel, u, 0.0), jnp.where(sel, v, 0.0)], axis=0)
        y = y + _smm(rbk, uv, p)
    cl = jnp.sum(lw, axis=0, keepdims=True)
    dec = jnp.exp(cl - cum)
    uv_all = jnp.concatenate([u, v], axis=0)
    bk_dec = jnp.concatenate([b * dec, k * dec], axis=0)
    st_new = st * jnp.exp(cl) + jnp.where(blockdiag, _smm_tn(uv_all, bk_dec, p), 0.0)
    return y, st_new, jnp.stack(invs)


GROUPS_PER_STEP = 8


def _scan_specs(ins, col_offs, n_chunks, reverse):
    gw = GROUPS_PER_STEP * LANES
    cidx = (lambda c: n_chunks - 1 - c) if reverse else (lambda c: c)
    specs = []
    for a, off in zip(ins, col_offs):
        if a.ndim == 2:
            assert off % gw == 0
            specs.append(pl.BlockSpec((CHUNK, gw), lambda h, c, o=off // gw: (cidx(c), h + o)))
        else:
            specs.append(pl.BlockSpec((GROUPS_PER_STEP, None, 1, CHUNK), lambda h, c: (h, cidx(c), 0, 0)))
    return specs, cidx


def _group_vals(refs, g):
    return [r[:, g * LANES:(g + 1) * LANES] if len(r.shape) == 2 else r[g] for r in refs]


def _scan_fwd(chunk_fn, ins, col_offs, n_groups, n_chunks, state_shape, inv_shape, name):
    n_in = len(ins)
    gps = GROUPS_PER_STEP
    t = ins[0].shape[0]

    def body(*refs):
        in_refs = refs[:n_in]
        o_ref, s0_ref, inv_ref, st = refs[n_in:]

        @pl.when(pl.program_id(1) == 0)
        def _():
            st[...] = jnp.zeros_like(st)

        states = st[...]
        vals = [jnp.stack(col) for col in zip(*[_group_vals(in_refs, g) for g in range(gps)])]
        o, s_new, inv = jax.vmap(chunk_fn)(states, *vals)
        s0_ref[...] = states
        inv_ref[...] = inv
        st[...] = s_new
        for g in range(gps):
            o_ref[:, g * LANES:(g + 1) * LANES] = o[g]

    specs, _ = _scan_specs(ins, col_offs, n_chunks, False)
    zeros_i = (0,) * len(inv_shape)
    return pl.pallas_call(
        body, name=name, grid=(n_groups // gps, n_chunks), in_specs=specs,
        out_specs=[pl.BlockSpec((CHUNK, gps * LANES), lambda h, c: (c, h)),
                   pl.BlockSpec((gps, None) + state_shape, lambda h, c: (h, c, 0, 0)),
                   pl.BlockSpec((gps, None) + inv_shape, lambda h, c: (h, c) + zeros_i)],
        out_shape=[jax.ShapeDtypeStruct((t, n_groups * LANES), f32),
                   jax.ShapeDtypeStruct((n_groups, n_chunks) + state_shape, f32),
                   jax.ShapeDtypeStruct((n_groups, n_chunks) + inv_shape, f32)],
        scratch_shapes=[pltpu.VMEM((gps,) + state_shape, f32)],
        compiler_params=pltpu.CompilerParams(dimension_semantics=("parallel", "arbitrary")),
    )(*ins)


def _scan_bwd(chunk_fn, s0s, invs, ins, col_offs, d_out, n_groups, n_chunks, state_shape, name):
    n_in = len(ins)
    gps = GROUPS_PER_STEP
    t = d_out.shape[0]
    inv_shape = invs.shape[2:]

    def body(*refs):
        s0_ref, inv_ref = refs[:2]
        in_refs = refs[2:2 + n_in]
        do_ref = refs[2 + n_in]
        g_refs = refs[3 + n_in:3 + 2 * n_in]
        dst = refs[3 + 2 * n_in]

        @pl.when(pl.program_id(1) == 0)
        def _():
            dst[...] = jnp.zeros_like(dst)

        vals = [jnp.stack(col) for col in zip(*[_group_vals(in_refs, g) for g in range(gps)])]
        d_o = jnp.stack([do_ref[:, g * LANES:(g + 1) * LANES] for g in range(gps)])
        inv = inv_ref[...]

        def with_saved(s, *a):
            return jax.vmap(lambda ss, ii, *aa: chunk_fn(ss, *aa, inv_saved=ii)[:2])(s, inv, *a)

        _, vjp = jax.vjp(with_saved, s0_ref[...], *vals)
        gs = vjp((d_o, dst[...]))
        dst[...] = gs[0]
        for g_ref, gv in zip(g_refs, gs[1:]):
            if len(g_ref.shape) == 2:
                for g in range(gps):
                    g_ref[:, g * LANES:(g + 1) * LANES] = gv[g]
            else:
                g_ref[...] = gv

    specs, cidx = _scan_specs(ins, col_offs, n_chunks, True)
    out_lane = pl.BlockSpec((CHUNK, gps * LANES), lambda h, c: (cidx(c), h))
    g_specs = [out_lane if a.ndim == 2 else sp for a, sp in zip(ins, specs)]
    g_shapes = [(t, n_groups * LANES) if a.ndim == 2 else a.shape for a in ins]
    s0_spec = pl.BlockSpec((gps, None) + state_shape, lambda h, c: (h, cidx(c), 0, 0))
    zeros_i = (0,) * len(inv_shape)
    inv_spec = pl.BlockSpec((gps, None) + inv_shape, lambda h, c: (h, cidx(c)) + zeros_i)
    return pl.pallas_call(
        body, name=name, grid=(n_groups // gps, n_chunks), in_specs=[s0_spec, inv_spec] + specs + [out_lane],
        out_specs=g_specs, out_shape=[jax.ShapeDtypeStruct(sh, f32) for sh in g_shapes],
        scratch_shapes=[pltpu.VMEM((gps,) + state_shape, f32)],
        compiler_params=pltpu.CompilerParams(dimension_semantics=("parallel", "arbitrary")),
    )(s0s, invs, *ins, d_out)


def _rms_fwd(x, g, name):
    t = x.shape[0]
    tm = _tile(t, 416, 16)
    return _tw_fwd(_f_rms, [x, g], [_row_spec(tm, D), _full_spec(g.shape)],
                   [jax.ShapeDtypeStruct(x.shape, MXU_DTYPE)], [_row_spec(tm, D)], (t // tm,), name)[0]


def _rms_bwd(x, g, dy, residual, name):
    t = x.shape[0]
    tm = _tile(t, 416, 8)
    return _tw_bwd(_f_rms, [x, g], [_row_spec(tm, D), _full_spec(g.shape)], [dy], [_row_spec(tm, D)],
                   ['tile', 'acc'], (t // tm,), name, residual=residual)


def _ffn_fwd(h, gain, wgu, wd, tag):
    xn = _rms_fwd(h, gain, f"{tag}_rms")
    gate, up, act = _gate_up_act(xn, wgu, f"{tag}_gate_up")
    out = _matmul(act, wd, res=h, scale=0.5, name=f"{tag}_down")
    return out, (xn, gate, up, act)


def _mxu_dot(a, b, dn):
    return lax.dot_general(a.astype(MXU_DTYPE), b.astype(MXU_DTYPE), dn, preferred_element_type=f32)


def _gate_up_act(xn, wgu, name):
    t = xn.shape[0]
    wdt = wgu.shape[2]
    tm = _tile(t, 416, 16)
    dn = (((1,), (0,)), ((), ()))

    def body(x_ref, wg_ref, wu_ref, g_ref, u_ref, a_ref):
        x = x_ref[...]
        g = _mxu_dot(x, wg_ref[...], dn)
        u = _mxu_dot(x, wu_ref[...], dn)
        g_ref[...] = g
        u_ref[...] = u
        a_ref[...] = _f_swiglu(g, u)[0].astype(a_ref.dtype)

    out_spec = pl.BlockSpec((tm, wdt), lambda j, i: (i, j))
    return pl.pallas_call(
        body, name=name, grid=(2, t // tm),
        in_specs=[pl.BlockSpec((tm, D), lambda j, i: (i, 0)), pl.BlockSpec((None, D, wdt), lambda j, i: (j, 0, 0)),
                  pl.BlockSpec((None, D, wdt), lambda j, i: (j + 2, 0, 0))],
        out_specs=[out_spec] * 3,
        out_shape=[jax.ShapeDtypeStruct((t, 2 * wdt), f32)] * 2 + [jax.ShapeDtypeStruct((t, 2 * wdt), MXU_DTYPE)],
        compiler_params=pltpu.CompilerParams(dimension_semantics=("parallel", "parallel")),
    )(xn, wgu, wgu)


def _d_gate_up(dout, wd, gate, up, name):
    t = dout.shape[0]
    wdt = D_FF // 2
    tm = _tile(t, 416, 16)
    dn = (((1,), (1,)), ((), ()))

    def body(do_ref, wd_ref, g_ref, u_ref, dg_ref, du_ref):
        d_act = 0.5 * _mxu_dot(do_ref[...], wd_ref[...], dn)
        _, vjp = jax.vjp(_f_swiglu, g_ref[...], u_ref[...])
        dg, du = vjp((d_act,))
        dg_ref[...] = dg.astype(dg_ref.dtype)
        du_ref[...] = du.astype(du_ref.dtype)

    spec = pl.BlockSpec((tm, wdt), lambda j, i: (i, j))
    return pl.pallas_call(
        body, name=name, grid=(2, t // tm),
        in_specs=[pl.BlockSpec((tm, D), lambda j, i: (i, 0)), pl.BlockSpec((wdt, D), lambda j, i: (j, 0)), spec, spec],
        out_specs=[spec] * 2, out_shape=[jax.ShapeDtypeStruct((t, D_FF), MXU_DTYPE)] * 2,
        compiler_params=pltpu.CompilerParams(dimension_semantics=("parallel", "parallel")),
    )(dout, wd, gate, up)


def _ffn_bwd(h, gain, wgu, wd, saved, dout, tag):
    xn, gate, up, act = saved
    t = h.shape[0]
    d_wd = _matmul(act, dout, ta=True, scale=0.5, name=f"{tag}_dwd")
    d_gate, d_up = _d_gate_up(dout, wd, gate, up, f"{tag}_dact")
    d_wgu = _matmul(xn, d_gate, ta=True, out_cols_split=(0, N_CHIPS), name=f"{tag}_dwg")
    d_wgu = _matmul(xn, d_up, ta=True, out_cols_split=(2, N_CHIPS), out_into=d_wgu, name=f"{tag}_dwu")
    d_xn = _matmul(d_gate, wgu, tb=True, b_cols_split=(0, 2), name=f"{tag}_dxn_g")
    d_xn = _matmul(d_up, wgu, tb=True, b_cols_split=(2, 2), res=d_xn, name=f"{tag}_dxn_u")
    d_h, d_gain = _rms_bwd(h, gain, d_xn, dout, f"{tag}_drms")
    return d_h, d_gain, d_wgu, d_wd


def _col_spec(t, first_block):
    return pl.BlockSpec((t, LANES), lambda j, fb=first_block: (0, j + fb))


def _local_step(h0, tgt, w):
    t = h0.shape[0]
    assert t % CHUNK == 0
    nc = t // CHUNK
    grads = {}

    h1, ffn1_saved = _ffn_fwd(h0, w['ffn1_norm'], w['ffn1_wgu'], w['ffn1_wd'], "ffn1")
    u = _rms_fwd(h1, w['mix_norm'], "mix_rms")
    z = _matmul(u, w['w_in_p'], cols_outer=True, name="in_proj")
    zs = z[:, 9216:9216 + 304]
    abeta, aalpha = zs[:, 288:296], zs[:, 296:304]

    conv_w = w['a_conv_w']
    conv_fns = [functools.partial(_f_conv, norm=True, scale=A_DK ** -0.5),
                functools.partial(_f_conv, norm=True, scale=1.0),
                functools.partial(_f_conv, norm=False, scale=1.0)]
    qkv = []
    for idx, fn in enumerate(conv_fns):
        qkv.append(_tw_fwd(fn, [z, conv_w], [_col_spec(t, 8 * idx), pl.BlockSpec((4, LANES), lambda j, o=8 * idx: (0, j + o))],
                           [jax.ShapeDtypeStruct((t, D), f32)], [_col_spec(t, 0)], (A_HEADS,), f"a_conv{idx}")[0])
    aq, ak, av = qkv
    tmg = _tile(t, 1040, 8)
    dg_fn = functools.partial(_f_dgates, tm=tmg)
    dg_specs = [_row_spec(tmg, A_HEADS)] * 2 + [_full_spec((1, A_HEADS))] * 2
    beta, gdec = _tw_fwd(dg_fn, [abeta, aalpha, w['a_log_rate'], w['a_dt_bias']], dg_specs,
                         [jax.ShapeDtypeStruct((t, A_HEADS), f32)] * 2, [_row_spec(tmg, A_HEADS)] * 2, (t // tmg,),
                         "a_gates", with_pid=True)
    beta_h = beta.T.reshape(A_HEADS, nc, 1, CHUNK)
    gdec_h = gdec.T.reshape(A_HEADS, nc, 1, CHUNK)
    a_ins = [aq, ak, av, beta_h, gdec_h]
    a_offs = [0] * 5
    o_scan, a_s0, a_inv = _scan_fwd(_delta_chunk, a_ins, a_offs, A_HEADS, nc, (A_DK, A_DK), (CHUNK, CHUNK), "a_scan")

    mu = w['b_shift_mu']
    mu_rkv, mu_s = mu[:, :3072], mu[:, 3072:]
    zf_rkv = _tw_fwd(_f_tshift, [z, mu_rkv], [_col_spec(t, 32), pl.BlockSpec((1, LANES), lambda j: (0, j))],
                     [jax.ShapeDtypeStruct((t, 3072), f32)], [_col_spec(t, 0)], (24,), "b_shift")[0]
    zs_b = zs[:, :288]
    zf_s = _tw_fwd(_f_tshift, [zs_b, mu_s], [_full_spec((t, 288)), _full_spec((1, 288))],
                   [jax.ShapeDtypeStruct((t, 288), f32)], [_full_spec((t, 288))], (1,), "b_shift_s")[0]
    wdf, adf, gdf = zf_s[:, 0:64], zf_s[:, 64:128], zf_s[:, 128:288]
    tmr = _tile(t, 208, 16)
    pre_params = [w['b_w0'], w['b_w_up'], w['b_a0'], w['b_a_up'], w['b_g_up'], w['b_k_k'], w['b_k_a']]
    pre_ins = [zf_rkv, wdf, adf, gdf] + pre_params
    pre_specs = ([_row_spec(tmr, D, 1), _row_spec(tmr, 64), _row_spec(tmr, 64), _row_spec(tmr, 160)]
                 + [_full_spec(p.shape) for p in pre_params])
    lw, kmod, a_s, b_s, bgate = _tw_fwd(_f_rwkv_pre, pre_ins, pre_specs, [jax.ShapeDtypeStruct((t, D), f32)] * 5,
                                        [_row_spec(tmr, D)] * 5, (t // tmr,), "b_pre")
    b_ins = [zf_rkv, kmod, zf_rkv, a_s, b_s, lw]
    b_offs = [0, 0, 2 * D, 0, 0, 0]
    y_scan, b_s0, b_inv = _scan_fwd(_rwkv_chunk, b_ins, b_offs, B_HEADS // 2, nc, (2 * B_N, 2 * B_N),
                                    (2, CHUNK, CHUNK), "b_scan")

    out_gain_t = jnp.tile(w['a_out_norm'], (1, A_HEADS))
    r_k = w['b_r_k'].reshape(1, D)
    post_params = [out_gain_t, w['b_ln_gain'], w['b_ln_bias'], r_k]
    post_ins = [o_scan, z, y_scan, zf_rkv, kmod, zf_rkv, bgate, z, z] + post_params
    post_specs = ([_row_spec(tmr, D), _row_spec(tmr, D, 3), _row_spec(tmr, D), _row_spec(tmr, D, 0), _row_spec(tmr, D),
                   _row_spec(tmr, D, 2), _row_spec(tmr, D), _row_spec(tmr, D, 7), _row_spec(tmr, D, 8)]
                  + [_full_spec((1, D))] * 4)
    merged = _tw_fwd(_f_mix_post, post_ins, post_specs, [jax.ShapeDtypeStruct((t, D), MXU_DTYPE)],
                     [_row_spec(tmr, D)], (t // tmr,), "mix_post")[0]
    h2 = _matmul(merged, w['w_out'], res=h1, name="out_proj")
    h3, ffn2_saved = _ffn_fwd(h2, w['ffn2_norm'], w['ffn2_wgu'], w['ffn2_wd'], "ffn2")

    tml = _tile(t, 416, 8)
    fnorm = w['final_norm']
    loss_fn = functools.partial(_f_loss, tm=tml)
    loss_specs = [_row_spec(tml, D), _full_spec((1, D)), _row_spec(tml, D)]
    loss_parts, d_h3, grads['final_norm'] = _loss_and_grad(loss_fn, h3, fnorm, tgt, loss_specs, tml)
    loss = jnp.sum(loss_parts)

    d_h2, grads['ffn2_norm'], grads['ffn2_wgu'], grads['ffn2_wd'] = _ffn_bwd(
        h2, w['ffn2_norm'], w['ffn2_wgu'], w['ffn2_wd'], ffn2_saved, d_h3, "ffn2")
    grads['w_out'] = _matmul(merged, d_h2, ta=True, name="d_w_out")
    d_merged = _matmul(d_h2, w['w_out'], tb=True, name="d_merged")

    win = ('tile', (t, D), _row_spec(tmr, D))
    zwin = win + (MXU_DTYPE,)
    post_kinds = ['tile', zwin, 'tile', win, 'tile', win, 'tile', zwin, zwin] + ['acc'] * 4
    (d_o, d_az, d_y, d_r1, d_kmod1, d_v1, d_bgate, d_ga, d_gb,
     d_out_gain_t, grads['b_ln_gain'], grads['b_ln_bias'], d_r_k) = _tw_bwd(
        _f_mix_post, post_ins, post_specs, [d_merged], [_row_spec(tmr, D)], post_kinds, (t // tmr,), "mix_post_bwd")
    grads['a_out_norm'] = jnp.sum(d_out_gain_t.reshape(A_HEADS, A_DK), axis=0, keepdims=True)
    grads['b_r_k'] = d_r_k.reshape(1, B_HEADS, B_N)

    d_r2, d_kmod2, d_v2, d_as, d_bs, d_lw = _scan_bwd(_rwkv_chunk, b_s0, b_inv, b_ins, b_offs, d_y, B_HEADS // 2, nc,
                                                      (2 * B_N, 2 * B_N), "b_scan_bwd")
    pre_kinds = [win] + ['tile'] * 3 + ['acc'] * 7
    pre_ct_specs = [_row_spec(tmr, D)] * 5
    (d_zf_k, d_wdf, d_adf, d_gdf, grads['b_w0'], grads['b_w_up'], grads['b_a0'], grads['b_a_up'], grads['b_g_up'],
     grads['b_k_k'], grads['b_k_a']) = _tw_bwd(
        _f_rwkv_pre, pre_ins, pre_specs, [d_lw, d_kmod1, d_as, d_bs, d_bgate], pre_ct_specs, pre_kinds, (t // tmr,),
        "b_pre_bwd", ct_extra=[(1, d_kmod2)])
    d_zb_rkv, d_mu_rkv = _shift_bwd3(z, mu_rkv, d_r1, d_r2, d_zf_k, d_v1, d_v2, t)
    d_zf_s = jnp.concatenate([d_wdf, d_adf, d_gdf], axis=1)
    d_zs_b, d_mu_s = _tw_bwd(_f_tshift, [zs_b, mu_s], [_full_spec((t, 288)), _full_spec((1, 288))], [d_zf_s],
                             [_full_spec((t, 288))], ['tile', 'tile'], (1,), "b_shift_s_bwd")
    grads['b_shift_mu'] = jnp.concatenate([d_mu_rkv, d_mu_s], axis=1)

    d_aq, d_ak, d_av, d_beta_h, d_g_h = _scan_bwd(_delta_chunk, a_s0, a_inv, a_ins, a_offs, d_o, A_HEADS, nc,
                                                  (A_DK, A_DK), "a_scan_bwd")
    d_beta = d_beta_h.reshape(A_HEADS, t).T
    d_gdec = d_g_h.reshape(A_HEADS, t).T
    d_abeta, d_aalpha, grads['a_log_rate'], grads['a_dt_bias'] = _tw_bwd(
        dg_fn, [abeta, aalpha, w['a_log_rate'], w['a_dt_bias']], dg_specs, [d_beta, d_gdec],
        [_row_spec(tmg, A_HEADS)] * 2, ['tile', 'tile', 'acc', 'acc'], (t // tmg,), "a_gates_bwd", with_pid=True)
    d_zqkv, d_conv = [], []
    for idx, (fn, ct) in enumerate(zip(conv_fns, (d_aq, d_ak, d_av))):
        dz_i, dw_i = _conv_bwd(fn, z, conv_w, ct, idx, t)
        d_zqkv.append(dz_i)
        d_conv.append(dw_i)
    grads['a_conv_w'] = jnp.concatenate(d_conv, axis=1)

    d_small = jnp.concatenate([d_zs_b, d_abeta, d_aalpha, jnp.zeros((t, ZP - 9216 - 304), f32)], axis=1)
    d_small = lax.optimization_barrier(d_small.astype(MXU_DTYPE))
    d_z_parts = d_zqkv + [d_az, d_zb_rkv, d_ga, d_gb, d_small]
    d_z = jnp.concatenate([p.astype(MXU_DTYPE) for p in d_z_parts], axis=1)
    grads['w_in_p'] = _matmul(u, d_z, ta=True, name="d_w_in")
    d_u = _matmul(d_z, w['w_in_p'], tb=True, name="d_u")
    d_h1, grads['mix_norm'] = _rms_bwd(h1, w['mix_norm'], d_u, d_h2, "mix_drms")
    d_h0, grads['ffn1_norm'], grads['ffn1_wgu'], grads['ffn1_wd'] = _ffn_bwd(
        h0, w['ffn1_norm'], w['ffn1_wgu'], w['ffn1_wd'], ffn1_saved, d_h1, "ffn1")
    return loss, d_h0, grads


_WIN_SEGMENTS = ((0, 4096), (4112, 7184), (7472, 9520), (7184, 7472), (4096, 4112))


_WIN_SHARD = IN_TOTAL // N_CHIPS


def _win_pieces():
    pieces, pad_at = [], 0
    for a, b in _WIN_SEGMENTS:
        c = a
        while c < b:
            stop = min(b, (c // _WIN_SHARD + 1) * _WIN_SHARD)
            pieces.append((c, pad_at + c - a, stop - c))
            c = stop
        pad_at += b - a
    return pieces


def _win_shards_to_padded(shards):
    parts = [shards[c // _WIN_SHARD][:, c % _WIN_SHARD:c % _WIN_SHARD + n] for c, _, n in _win_pieces()]
    parts.append(jnp.zeros((shards.shape[1], ZP - IN_TOTAL), shards.dtype))
    return jnp.concatenate(parts, axis=1)


def _win_padded_to_shards(w_p):
    by_shard = [[] for _ in range(N_CHIPS)]
    for c, p, n in sorted(_win_pieces()):
        by_shard[c // _WIN_SHARD].append(w_p[:, p:p + n])
    return jnp.stack([jnp.concatenate(parts, axis=1) for parts in by_shard])


def _loss_and_grad(loss_fn, h, gain, tgt, specs, tm):
    t = h.shape[0]
    n = t // tm

    def body(h_ref, g_ref, t_ref, l_ref, dh_ref, dg_ref):
        pid = pl.program_id(0)
        tg = t_ref[...]
        (part,), vjp = jax.vjp(lambda a, b: loss_fn(pid, a, b, tg), h_ref[...], g_ref[...])
        dh, dg = vjp((jnp.ones_like(part),))
        l_ref[...] = part
        dh_ref[...] = dh

        @pl.when(pid == 0)
        def _():
            dg_ref[...] = dg

        @pl.when(pid != 0)
        def _():
            dg_ref[...] += dg

    return pl.pallas_call(
        body, name="loss", grid=(n,), in_specs=specs,
        out_specs=[pl.BlockSpec((None, 1, 1), lambda i: (i, 0, 0)), specs[0], _full_spec(gain.shape)],
        out_shape=[jax.ShapeDtypeStruct((n, 1, 1), f32), jax.ShapeDtypeStruct(h.shape, f32),
                   jax.ShapeDtypeStruct(gain.shape, f32)],
    )(h, gain, tgt)


def _shift_bwd3(z, mu, d_r1, d_r2, d_k, d_v1, d_v2, t):
    nb = D // LANES

    def body(z_ref, mu_ref, r1, r2, kk, v1, v2, dz_ref, dmu_ref):
        j = pl.program_id(0)
        ct = jnp.where(j < nb, r1[...] + r2[...], jnp.where(j < 2 * nb, kk[...], v1[...] + v2[...]))
        _, vjp = jax.vjp(lambda a, b: _f_tshift(a, b), z_ref[...], mu_ref[...])
        dz, dmu = vjp((ct,))
        dz_ref[...] = dz.astype(dz_ref.dtype)
        dmu_ref[...] = dmu

    def window(first):
        return pl.BlockSpec((t, LANES), lambda j, f=first: (0, jnp.clip(j - f * nb, 0, nb - 1)))

    return pl.pallas_call(
        body, name="b_shift_bwd", grid=(3 * nb,),
        in_specs=[_col_spec(t, 32), pl.BlockSpec((1, LANES), lambda j: (0, j)), window(0), window(0), window(1),
                  window(2), window(2)],
        out_specs=[_col_spec(t, 0), pl.BlockSpec((1, LANES), lambda j: (0, j))],
        out_shape=[jax.ShapeDtypeStruct((t, 3 * D), MXU_DTYPE), jax.ShapeDtypeStruct((1, 3 * D), f32)],
    )(z, mu, d_r1, d_r2, d_k, d_v1, d_v2)


def _conv_bwd(fn, z, conv_w, ct, idx, t):
    def body(z_ref, w_ref, ct_ref, dz_ref, dw_ref):
        _, vjp = jax.vjp(lambda a, b: fn(a, b), z_ref[...], w_ref[...])
        dz, dw = vjp((ct_ref[...],))
        dz_ref[...] = dz.astype(dz_ref.dtype)
        dw_ref[...] = dw

    return pl.pallas_call(
        body, name=f"a_conv{idx}_bwd", grid=(A_HEADS,),
        in_specs=[_col_spec(t, 8 * idx), pl.BlockSpec((4, LANES), lambda j, o=8 * idx: (0, j + o)), _col_spec(t, 0)],
        out_specs=[_col_spec(t, 0), pl.BlockSpec((4, LANES), lambda j: (0, j))],
        out_shape=[jax.ShapeDtypeStruct((t, D), MXU_DTYPE), jax.ShapeDtypeStruct((4, D), f32)],
    )(z, conv_w, ct)


def _position():
    return lax.axis_index("x"), lax.axis_index("y"), lax.axis_index("c")


def _flip(v, f):
    return 1 - v if f else v


_CHIP_FLIPS = ((1, 0), (0, 1), (1, 1))


def _gather_chips(arrs, name):
    n = len(arrs)
    assert all(a.shape[0] % 32 == 0 for a in arrs)
    arrs = [a.reshape(2, a.shape[0] // 2, a.shape[1]) for a in arrs]

    def body(*refs):
        ins, outs = refs[:n], refs[n:2 * n]
        send, recv, fsend, frecv, own = refs[2 * n:]
        x, y, c = _position()
        me = 2 * x + y
        sends, plan, owns = [], [], []
        for a in range(n):
            cp = pltpu.make_async_remote_copy(src_ref=ins[a], dst_ref=outs[a].at[me], send_sem=own.at[a, 0],
                                              recv_sem=own.at[a, 1], device_id=(x, y, 1 - c), device_id_type=MESH)
            cp.start()
            owns.append(cp)
            for j, (fx, fy) in enumerate(_CHIP_FLIPS):
                px, py = _flip(x, fx), _flip(y, fy)
                p = 2 * px + py
                cp = pltpu.make_async_remote_copy(src_ref=ins[a].at[c], dst_ref=outs[a].at[me, c],
                                                  send_sem=send.at[a, j], recv_sem=recv.at[a, j],
                                                  device_id=(px, py, c), device_id_type=MESH)
                cp.start()
                sends.append(cp)
                landed = pltpu.make_async_remote_copy(src_ref=ins[a].at[c], dst_ref=outs[a].at[p, c],
                                                      send_sem=send.at[a, j], recv_sem=recv.at[a, j],
                                                      device_id=(px, py, c), device_id_type=MESH)
                onward = pltpu.make_async_remote_copy(src_ref=outs[a].at[p, c], dst_ref=outs[a].at[p, c],
                                                      send_sem=fsend.at[a, j], recv_sem=frecv.at[a, j],
                                                      device_id=(x, y, 1 - c), device_id_type=MESH)
                from_sibling = pltpu.make_async_remote_copy(src_ref=outs[a].at[p, 1 - c], dst_ref=outs[a].at[p, 1 - c],
                                                            send_sem=fsend.at[a, j], recv_sem=frecv.at[a, j],
                                                            device_id=(x, y, 1 - c), device_id_type=MESH)
                plan.append((landed, onward, from_sibling))
        for landed, onward, _ in plan:
            landed.wait_recv()
            onward.start()
        for _, _, from_sibling in plan:
            from_sibling.wait_recv()
        for cp in sends:
            cp.wait_send()
        for _, onward, _ in plan:
            onward.wait_send()
        for cp in owns:
            cp.wait()

    sems = [pltpu.SemaphoreType.DMA((n, 3))] * 4 + [pltpu.SemaphoreType.DMA((n, 2))]
    outs = pl.pallas_call(
        body, name=name, in_specs=[ANY] * n, out_specs=[ANY] * n,
        out_shape=[jax.ShapeDtypeStruct((N_CHIPS,) + a.shape, a.dtype) for a in arrs], scratch_shapes=sems,
    )(*arrs)
    return [o.reshape(N_CHIPS, o.shape[1] * o.shape[2], o.shape[3]) for o in outs]


def _swap_sibling(arrs, src_of, shapes, name):
    n = len(arrs)

    def body(*refs):
        a_refs, got_refs = refs[:n], refs[n:2 * n]
        send, recv = refs[2 * n:]
        x, y, c = _position()
        copies = []
        for i in range(n):
            cp = pltpu.make_async_remote_copy(src_ref=src_of(a_refs[i], c), dst_ref=got_refs[i], send_sem=send.at[i],
                                              recv_sem=recv.at[i], device_id=(x, y, 1 - c), device_id_type=MESH)
            cp.start()
            copies.append(cp)
        for cp in copies:
            cp.wait()

    return pl.pallas_call(body, name=name, in_specs=[ANY] * n, out_specs=[ANY] * n,
                          out_shape=[jax.ShapeDtypeStruct(sh, a.dtype) for sh, a in zip(shapes, arrs)],
                          scratch_shapes=[pltpu.SemaphoreType.DMA((n,))] * 2)(*arrs)


def _row_tile(rows, width):
    return _tile(rows, max(16, (784 * LANES // width) // 16 * 16), 16)


def _add_halves(g, got, dtype, name):
    n, _, hr, w = g.shape
    tr = _row_tile(hr, w)

    def body(g_ref, got_ref, o_ref):
        c = lax.axis_index("c")
        own = jnp.where(c == 0, g_ref[:, 0], g_ref[:, 1])
        o_ref[...] = (own + got_ref[...]).astype(dtype)

    return pl.pallas_call(
        body, name=name, grid=(hr // tr,),
        in_specs=[pl.BlockSpec((n, 2, tr, w), lambda i: (0, 0, i, 0)), pl.BlockSpec((n, tr, w), lambda i: (0, i, 0))],
        out_specs=pl.BlockSpec((n, tr, w), lambda i: (0, i, 0)),
        out_shape=jax.ShapeDtypeStruct((n, hr, w), dtype))(g, got)


def _scatter_chips(gs, name):
    n = len(gs)

    def body(*refs):
        g_refs, out_refs = refs[:n], refs[n:2 * n]
        send, recv = refs[2 * n:]
        x, y, c = _position()
        sends = []
        for i in range(n):
            for j, (fx, fy) in enumerate(_CHIP_FLIPS):
                px, py = _flip(x, fx), _flip(y, fy)
                cp = pltpu.make_async_remote_copy(src_ref=g_refs[i].at[2 * px + py], dst_ref=out_refs[i].at[j],
                                                  send_sem=send.at[i, j], recv_sem=recv.at[i, j],
                                                  device_id=(px, py, c), device_id_type=MESH)
                cp.start()
                sends.append(cp)
        for cp in sends:
            cp.wait_recv()
        for cp in sends:
            cp.wait_send()

    return pl.pallas_call(
        body, name=name, in_specs=[ANY] * n, out_specs=[ANY] * n,
        out_shape=[jax.ShapeDtypeStruct((3,) + g.shape[1:], g.dtype) for g in gs],
        scratch_shapes=[pltpu.SemaphoreType.DMA((n, 3)), pltpu.SemaphoreType.DMA((n, 3))],
    )(*gs)


def _sum_own_and_slots(own, got, name):
    n, r, w = own.shape
    tr = _row_tile(r, w)

    def body(own_ref, got_ref, o_ref):
        me = 2 * lax.axis_index("x") + lax.axis_index("y")
        acc = own_ref[0]
        for i in range(1, n):
            acc = jnp.where(me == i, own_ref[i], acc)
        acc = acc.astype(f32)
        for j in range(3):
            acc = acc + got_ref[j].astype(f32)
        o_ref[...] = acc

    return pl.pallas_call(
        body, name=name, grid=(r // tr,),
        in_specs=[pl.BlockSpec((n, tr, w), lambda i: (0, i, 0)), pl.BlockSpec((3, tr, w), lambda i: (0, i, 0))],
        out_specs=pl.BlockSpec((tr, w), lambda i: (i, 0)), out_shape=jax.ShapeDtypeStruct((r, w), f32))(own, got)


def _share_chips(a, name):
    def body(a_ref, out_ref, send, recv):
        x, y, c = _position()
        sends = []
        for j, (fx, fy) in enumerate(_CHIP_FLIPS):
            cp = pltpu.make_async_remote_copy(src_ref=a_ref, dst_ref=out_ref.at[j], send_sem=send.at[j],
                                              recv_sem=recv.at[j], device_id=(_flip(x, fx), _flip(y, fy), c),
                                              device_id_type=MESH)
            cp.start()
            sends.append(cp)
        for cp in sends:
            cp.wait_recv()
        for cp in sends:
            cp.wait_send()

    return pl.pallas_call(
        body, name=name, in_specs=[ANY], out_specs=ANY, out_shape=jax.ShapeDtypeStruct((3,) + a.shape, a.dtype),
        scratch_shapes=[pltpu.SemaphoreType.DMA((3,)), pltpu.SemaphoreType.DMA((3,))],
    )(a)


def _sum_in_chip_order(pair, got, name):
    r, w = pair.shape
    tr = _tile(r, 1408, 8)

    def body(p_ref, g_ref, o_ref):
        x, y = lax.axis_index("x"), lax.axis_index("y")
        me = 2 * x + y
        across = [2 * _flip(x, fx) + _flip(y, fy) for fx, fy in _CHIP_FLIPS]
        acc = None
        for i in range(N_CHIPS):
            term = p_ref[...]
            for j in range(3):
                term = jnp.where(across[j] == i, g_ref[j], term)
            acc = term if acc is None else acc + term
        o_ref[...] = acc

    return pl.pallas_call(
        body, name=name, grid=(r // tr,),
        in_specs=[pl.BlockSpec((tr, w), lambda i: (i, 0)), pl.BlockSpec((3, tr, w), lambda i: (0, i, 0))],
        out_specs=pl.BlockSpec((tr, w), lambda i: (i, 0)), out_shape=jax.ShapeDtypeStruct((r, w), f32))(pair, got)


def _add2(a, b, name):
    r, w = a.shape
    tr = _tile(r, 1408, 8)
    spec = pl.BlockSpec((tr, w), lambda i: (i, 0))

    def body(a_ref, b_ref, o_ref):
        o_ref[...] = a_ref[...] + b_ref[...]

    return pl.pallas_call(body, name=name, grid=(r // tr,), in_specs=[spec, spec], out_specs=spec,
                          out_shape=jax.ShapeDtypeStruct(a.shape, f32))(a, b)


def _adamw(w, g_parts, m, v, name):
    shape = w.shape
    view = shape if len(shape) >= 2 else (1,) + shape
    assert all(d == 1 for d in view[:-2]), shape
    rows, cols = view[-2:]
    cap = max(8, (256 * 1024 // cols) // 8 * 8)
    tr = rows if rows <= cap else _tile(rows, cap, 8)
    lead = len(view) - 2
    n_g = len(g_parts)

    def body(*refs):
        w_ref = refs[0]
        g_refs = refs[1:1 + n_g]
        m_ref, v_ref, g_out, d_out, m_out, v_out = refs[1 + n_g:]
        g = g_refs[0][...]
        for gr in g_refs[1:]:
            g = g + gr[...]
        m_new = ADAM_B1 * m_ref[...] + (1.0 - ADAM_B1) * g
        v_new = ADAM_B2 * v_ref[...] + (1.0 - ADAM_B2) * (g * g)
        m_hat = m_new / (1.0 - ADAM_B1 ** ADAM_STEP)
        v_hat = v_new / (1.0 - ADAM_B2 ** ADAM_STEP)
        g_out[...] = g
        d_out[...] = -ADAM_LR * (m_hat / (jnp.sqrt(v_hat) + ADAM_EPS) + ADAM_WD * w_ref[...])
        m_out[...] = m_new
        v_out[...] = v_new

    spec = pl.BlockSpec((None,) * lead + (tr, cols), lambda i: (0,) * lead + (i, 0))
    args = [w.reshape(view)] + [g.reshape(view) for g in g_parts] + [m.reshape(view), v.reshape(view)]
    outs = pl.pallas_call(body, name=name, grid=(rows // tr,), in_specs=[spec] * len(args), out_specs=[spec] * 4,
                          out_shape=[jax.ShapeDtypeStruct(view, f32)] * 4)(*args)
    return [o.reshape(shape) for o in outs]


_BIG = ('ffn1_w_gu', 'ffn1_w_down', 'w_in', 'w_out', 'ffn2_w_gu', 'ffn2_w_down')
_SMALL_SHARDED = ('meta_tokens', 'a_conv_w', 'b_w_up', 'b_a_up', 'b_g_up')
_WEIGHTS = ('meta_tokens', 'ffn1_norm', 'ffn1_w_gu', 'ffn1_w_down', 'mix_norm', 'w_in', 'a_conv_w', 'a_log_rate',
            'a_dt_bias', 'a_out_norm', 'b_shift_mu', 'b_w0', 'b_w_up', 'b_a0', 'b_a_up', 'b_g_up', 'b_k_k', 'b_k_a',
            'b_r_k', 'b_ln_gain', 'b_ln_bias', 'w_out', 'ffn2_norm', 'ffn2_w_gu', 'ffn2_w_down', 'final_norm')
_SMALL = tuple(n for n in _WEIGHTS if n not in _BIG)


def _rows_of(shape):
    n = 1
    for d in shape:
        n *= d
    return n, -(-n // LANES)


def _pack(arrs, dtype, row_mult=32):
    parts, total = [], 0
    for a in arrs:
        n, rows = _rows_of(a.shape)
        flat = a.reshape(-1).astype(dtype)
        if n % LANES:
            flat = jnp.pad(flat, (0, rows * LANES - n))
        parts.append(flat)
        total += rows
    extra = -total % row_mult
    if extra:
        parts.append(jnp.zeros((extra * LANES,), dtype))
    return jnp.concatenate(parts).reshape(total + extra, LANES)


def _unpack(packed, shapes, lead=()):
    out, off = [], 0
    for sh in shapes:
        n, rows = _rows_of(sh)
        seg = packed[..., off:off + rows, :]
        if n % LANES:
            seg = seg.reshape(lead + (-1,))[..., :n]
        out.append(seg.reshape(lead + tuple(sh)))
        off += rows
    return out


def _cols_from_shards(s):
    return jnp.concatenate([s[i] for i in range(N_CHIPS)], axis=-1)


def kernel(x, meta_tokens, ffn1_norm, ffn1_w_gu, ffn1_w_down, mix_norm, w_in, a_conv_w, a_log_rate, a_dt_bias, a_out_norm, b_shift_mu, b_w0, b_w_up, b_a0, b_a_up, b_g_up, b_k_k, b_k_a, b_r_k, b_ln_gain, b_ln_bias, w_out, ffn2_norm, ffn2_w_gu, ffn2_w_down, final_norm, loss_target, m_meta_tokens, m_ffn1_norm, m_ffn1_w_gu, m_ffn1_w_down, m_mix_norm, m_w_in, m_a_conv_w, m_a_log_rate, m_a_dt_bias, m_a_out_norm, m_b_shift_mu, m_b_w0, m_b_w_up, m_b_a0, m_b_a_up, m_b_g_up, m_b_k_k, m_b_k_a, m_b_r_k, m_b_ln_gain, m_b_ln_bias, m_w_out, m_ffn2_norm, m_ffn2_w_gu, m_ffn2_w_down, m_final_norm, v_meta_tokens, v_ffn1_norm, v_ffn1_w_gu, v_ffn1_w_down, v_mix_norm, v_w_in, v_a_conv_w, v_a_log_rate, v_a_dt_bias, v_a_out_norm, v_b_shift_mu, v_b_w0, v_b_w_up, v_b_a0, v_b_a_up, v_b_g_up, v_b_k_k, v_b_k_a, v_b_r_k, v_b_ln_gain, v_b_ln_bias, v_w_out, v_ffn2_norm, v_ffn2_w_gu, v_ffn2_w_down, v_final_norm):
    args = locals()
    wts = {n: args[n] for n in _WEIGHTS}
    mom = {n: args["m_" + n] for n in _WEIGHTS}
    var = {n: args["v_" + n] for n in _WEIGHTS}
    chip = 2 * lax.axis_index("x") + lax.axis_index("y")

    big_shapes = [wts[n].shape[1:] for n in _BIG]
    small_shapes = [wts[n].shape[-2:] for n in _SMALL_SHARDED]
    big_flat = [wts[n].astype(bf16).reshape(wts[n].shape[1:]) for n in _BIG]
    small_packed = _pack([wts[n] for n in _SMALL_SHARDED], f32)
    gathered = _gather_chips(big_flat + [small_packed], "gather_weights")
    gu1, dn1, w_in_s, w_out_s, gu2, dn2 = [a.reshape((N_CHIPS,) + tuple(sh)) for a, sh in zip(gathered, big_shapes)]
    meta_s, conv_s, wup_s, aup_s, gup_s = _unpack(gathered[-1], small_shapes, (N_CHIPS,))
    w = {
        'ffn1_norm': ffn1_norm, 'mix_norm': mix_norm, 'ffn2_norm': ffn2_norm, 'final_norm': final_norm[None, :],
        'ffn1_wgu': gu1, 'ffn1_wd': dn1.reshape(D_FF, D), 'ffn2_wgu': gu2, 'ffn2_wd': dn2.reshape(D_FF, D),
        'w_in_p': _win_shards_to_padded(w_in_s), 'w_out': w_out_s.reshape(D, D),
        'a_conv_w': _cols_from_shards(conv_s), 'b_w_up': _cols_from_shards(wup_s), 'b_a_up': _cols_from_shards(aup_s),
        'b_g_up': _cols_from_shards(gup_s),
        'a_log_rate': a_log_rate, 'a_dt_bias': a_dt_bias, 'a_out_norm': a_out_norm, 'b_shift_mu': b_shift_mu,
        'b_w0': b_w0, 'b_a0': b_a0, 'b_k_k': b_k_k, 'b_k_a': b_k_a, 'b_r_k': b_r_k, 'b_ln_gain': b_ln_gain,
        'b_ln_bias': b_ln_bias,
    }
    meta_full = _cols_from_shards(meta_s)

    h0 = jnp.concatenate([jnp.zeros((PAD, D), f32), meta_full, x[0]], axis=0)
    tgt = jnp.concatenate([jnp.zeros((SKIP, D), f32), loss_target[0]], axis=0)
    loss_local, d_h0, g = _local_step(h0, tgt, w)
    loss = lax.psum(loss_local, ("x", "y", "c"))
    grad_x = d_h0[SKIP:][None]

    big_grads = [
        g['ffn1_wgu'],
        g['ffn1_wd'].reshape(N_CHIPS, D_FF // N_CHIPS, D),
        _win_padded_to_shards(g['w_in_p']),
        g['w_out'].reshape(N_CHIPS, D // N_CHIPS, D),
        g['ffn2_wgu'],
        g['ffn2_wd'].reshape(N_CHIPS, D_FF // N_CHIPS, D),
    ]
    g_halves = [a.reshape(N_CHIPS, 2, a.shape[1] // 2, a.shape[2]) for a in big_grads]
    sib_halves = _swap_sibling(g_halves, lambda ref, c: ref.at[:, 1 - c], [a.shape[:1] + a.shape[2:] for a in g_halves],
                               "swap_halves")
    chip_halves = [_add_halves(a, b, bf16, f"add_sibling{i}") for i, (a, b) in enumerate(zip(g_halves, sib_halves))]
    got = _scatter_chips(chip_halves, "scatter_grads")
    mine = [_sum_own_and_slots(a, b, f"sum_chips{i}") for i, (a, b) in enumerate(zip(chip_halves, got))]
    theirs = _swap_sibling(mine, lambda ref, c: ref, [a.shape for a in mine], "swap_sums")
    core = lax.axis_index("c")
    big_parts = [jnp.concatenate([jnp.where(core == 0, a, b), jnp.where(core == 0, b, a)], axis=0)
                 for a, b in zip(mine, theirs)]

    small_full = {
        'meta_tokens': d_h0[PAD:SKIP], 'ffn1_norm': g['ffn1_norm'], 'mix_norm': g['mix_norm'], 'a_conv_w': g['a_conv_w'],
        'a_log_rate': g['a_log_rate'], 'a_dt_bias': g['a_dt_bias'], 'a_out_norm': g['a_out_norm'],
        'b_shift_mu': g['b_shift_mu'], 'b_w0': g['b_w0'], 'b_w_up': g['b_w_up'], 'b_a0': g['b_a0'], 'b_a_up': g['b_a_up'],
        'b_g_up': g['b_g_up'], 'b_k_k': g['b_k_k'], 'b_k_a': g['b_k_a'], 'b_r_k': g['b_r_k'], 'b_ln_gain': g['b_ln_gain'],
        'b_ln_bias': g['b_ln_bias'], 'ffn2_norm': g['ffn2_norm'], 'final_norm': g['final_norm'],
    }
    s_shapes = [small_full[n].shape for n in _SMALL]
    s_packed = _pack([small_full[n] for n in _SMALL], f32, row_mult=256)
    (s_sib,) = _swap_sibling([s_packed], lambda ref, c: ref, [s_packed.shape], "swap_small")
    s_pair = _add2(s_packed, s_sib, "add_small")
    s_sum = _sum_in_chip_order(s_pair, _share_chips(s_pair, "share_small"), "sum_small")
    s_parts = dict(zip(_SMALL, _unpack(s_sum, s_shapes)))

    grad, delta, new_m, new_v = {}, {}, {}, {}
    for n, a in zip(_BIG, big_parts):
        grad[n], delta[n], new_m[n], new_v[n] = _adamw(wts[n], [a.reshape(wts[n].shape)], mom[n], var[n], f"adamw_{n}")
    for n in _SMALL:
        gs = s_parts[n]
        if n in _SMALL_SHARDED:
            width = wts[n].shape[-1]
            gs = lax.dynamic_slice_in_dim(gs, chip * width, width, axis=gs.ndim - 1)
        gs = gs.reshape(wts[n].shape)
        grad[n], delta[n], new_m[n], new_v[n] = _adamw(wts[n], [gs], mom[n], var[n], f"adamw_{n}")

    return (loss, grad_x, *[grad[n] for n in _WEIGHTS], *[delta[n] for n in _WEIGHTS],
            *[new_m[n] for n in _WEIGHTS], *[new_v[n] for n in _WEIGHTS])
```
